```python
import jax, jax.numpy as jnp
from jax import lax
import numpy as np

D_MODEL = 1024
BATCH = 8
SEQ = 4096
DEPTH = 1

LRU_WIDTH = D_MODEL
LRU_HEADS = 4
LRU_HEAD_DIM = LRU_WIDTH // LRU_HEADS
CONV_WIDTH = 4
LRU_C = 8.0
SGU_WIDTH = D_MODEL
SGU_GROUPS = 4
SGU_GROUP_DIM = SGU_WIDTH // SGU_GROUPS
CHUNK = 128
D_FF = 4 * D_MODEL
NORM_EPS = 1e-6
LN_EPS = 1e-5
OFF_XA = 0
OFF_GA = OFF_XA + LRU_WIDTH
OFF_U = OFF_GA + LRU_WIDTH
OFF_V = OFF_U + SGU_WIDTH
OFF_MA = OFF_V + SGU_WIDTH
OFF_MB = OFF_MA + D_MODEL
D_IN = OFF_MB + D_MODEL

kernel_name = "hybrid_rglru_sgu_gated_block"


def rms_norm(x, g):
    xf = x.astype(jnp.float32)
    y = xf * lax.rsqrt(jnp.mean(xf * xf, axis=-1, keepdims=True) + NORM_EPS)
    return (y * g.astype(jnp.float32)).astype(x.dtype)


def layer_norm(x, g, b):
    xf = x.astype(jnp.float32)
    mu = jnp.mean(xf, axis=-1, keepdims=True)
    var = jnp.mean(jnp.square(xf - mu), axis=-1, keepdims=True)
    y = (xf - mu) * lax.rsqrt(var + LN_EPS)
    return (y * g.astype(jnp.float32) + b.astype(jnp.float32)).astype(x.dtype)


def causal_depthwise_conv(x, w, b):
    k_w = w.shape[0]
    s = x.shape[1]
    xp = jnp.pad(x, ((0, 0), (k_w - 1, 0), (0, 0)))
    out = b
    for k in range(k_w):
        out = out + xp[:, k_w - 1 - k:k_w - 1 - k + s] * w[k]
    return out


def rg_lru(x, w_r, b_r, w_i, b_i, lam):
    bsz, s, c = x.shape
    xh = x.reshape(bsz, s, LRU_HEADS, LRU_HEAD_DIM)
    r = jax.nn.sigmoid(jnp.einsum('bshi,hij->bshj', xh, w_r) + b_r).reshape(bsz, s, c)
    i = jax.nn.sigmoid(jnp.einsum('bshi,hij->bshj', xh, w_i) + b_i).reshape(bsz, s, c)
    log_a = -LRU_C * r.astype(jnp.float32) * jax.nn.softplus(-lam.astype(jnp.float32))
    a = jnp.exp(log_a)
    mult = jnp.sqrt(-jnp.expm1(2.0 * log_a))
    bx = x.astype(jnp.float32) * i.astype(jnp.float32) * mult

    def combine(left, right):
        a_l, b_l = left
        a_r, b_r2 = right
        return a_l * a_r, a_r * b_l + b_r2

    _, h = lax.associative_scan(combine, (a, bx), axis=1)
    return h.astype(x.dtype)


def chunked_spatial_gating(u, v, ln_g, ln_b, w_s, b_s):
    bsz, s, c = v.shape
    n_chunks = s // CHUNK
    v = layer_norm(v, ln_g, ln_b)
    vc = v.reshape(bsz, n_chunks, CHUNK, SGU_GROUPS, SGU_GROUP_DIM)
    mask = jnp.tril(jnp.ones((CHUNK, CHUNK), dtype=w_s.dtype))
    sp = jnp.einsum('gts,bnsgc->bntgc', w_s * mask, vc) + jnp.transpose(b_s)[:, :, None]
    return u * sp.reshape(bsz, s, c)


def _fwd_setup_inputs(seed: int = 0) -> dict:
    key = jax.random.key(seed)
    ks = jax.random.split(key, 24)
    f32 = jnp.float32
    nrm = lambda k, shape, scale: jax.random.normal(k, shape, f32) * scale
    x = jax.random.normal(ks[0], (BATCH, SEQ, D_MODEL), f32)
    norm_mix_g = 1.0 + nrm(ks[1], (DEPTH, D_MODEL), 0.02)
    w_in = nrm(ks[2], (DEPTH, D_MODEL, D_IN), D_MODEL ** -0.5)
    conv_w = nrm(ks[3], (DEPTH, CONV_WIDTH, LRU_WIDTH), CONV_WIDTH ** -0.5)
    conv_b = nrm(ks[4], (DEPTH, LRU_WIDTH), 0.01)
    w_rgate = nrm(ks[5], (DEPTH, LRU_HEADS, LRU_HEAD_DIM, LRU_HEAD_DIM), LRU_HEAD_DIM ** -0.5)
    b_rgate = nrm(ks[6], (DEPTH, LRU_HEADS, LRU_HEAD_DIM), 0.01)
    w_igate = nrm(ks[7], (DEPTH, LRU_HEADS, LRU_HEAD_DIM, LRU_HEAD_DIM), LRU_HEAD_DIM ** -0.5)
    b_igate = nrm(ks[8], (DEPTH, LRU_HEADS, LRU_HEAD_DIM), 0.01)
    a_c = jax.random.uniform(ks[9], (DEPTH, LRU_WIDTH), f32, 0.9, 0.999)
    sig = a_c ** (1.0 / LRU_C)
    lru_lambda = jnp.log(sig) - jnp.log1p(-sig)
    w_out_a = nrm(ks[10], (DEPTH, LRU_WIDTH, D_MODEL), LRU_WIDTH ** -0.5)
    sgu_ln_g = 1.0 + nrm(ks[11], (DEPTH, SGU_WIDTH), 0.02)
    sgu_ln_b = nrm(ks[12], (DEPTH, SGU_WIDTH), 0.01)
    sgu_w_s = nrm(ks[13], (DEPTH, SGU_GROUPS, CHUNK, CHUNK), CHUNK ** -0.5)
    sgu_b_s = 1.0 + nrm(ks[14], (DEPTH, SGU_GROUPS, CHUNK), 0.01)
    w_out_b = nrm(ks[15], (DEPTH, SGU_WIDTH, D_MODEL), SGU_WIDTH ** -0.5)
    w_out = nrm(ks[16], (DEPTH, D_MODEL, D_MODEL), D_MODEL ** -0.5)
    norm_mlp_g = 1.0 + nrm(ks[17], (DEPTH, D_MODEL), 0.02)
    w_up = nrm(ks[18], (DEPTH, D_MODEL, D_FF), D_MODEL ** -0.5)
    w_down = nrm(ks[19], (DEPTH, D_FF, D_MODEL), D_FF ** -0.5)
    norm_final_g = 1.0 + nrm(ks[20], (D_MODEL,), 0.02)
    return {"x": x, "norm_mix_g": norm_mix_g, "w_in": w_in, "conv_w": conv_w, "conv_b": conv_b,
            "w_rgate": w_rgate, "b_rgate": b_rgate, "w_igate": w_igate, "b_igate": b_igate,
            "lru_lambda": lru_lambda, "w_out_a": w_out_a, "sgu_ln_g": sgu_ln_g, "sgu_ln_b": sgu_ln_b,
            "sgu_w_s": sgu_w_s, "sgu_b_s": sgu_b_s, "w_out_b": w_out_b, "w_out": w_out,
            "norm_mlp_g": norm_mlp_g, "w_up": w_up, "w_down": w_down, "norm_final_g": norm_final_g}


def _fwd_reference(x, norm_mix_g, w_in, conv_w, conv_b, w_rgate, b_rgate, w_igate, b_igate,
              lru_lambda, w_out_a, sgu_ln_g, sgu_ln_b, sgu_w_s, sgu_b_s, w_out_b, w_out,
              norm_mlp_g, w_up, w_down, norm_final_g):
    h = x
    for l in range(DEPTH):
        n = rms_norm(h, norm_mix_g[l])
        z = n @ w_in[l]
        xa = z[..., OFF_XA:OFF_GA]
        ga = z[..., OFF_GA:OFF_U]
        ub = z[..., OFF_U:OFF_V]
        vb = z[..., OFF_V:OFF_MA]
        ma = z[..., OFF_MA:OFF_MB]
        mb = z[..., OFF_MB:D_IN]
        xa = causal_depthwise_conv(xa, conv_w[l], conv_b[l])
        ya = rg_lru(xa, w_rgate[l], b_rgate[l], w_igate[l], b_igate[l], lru_lambda[l]) * jax.nn.gelu(ga)
        yb = chunked_spatial_gating(jax.nn.gelu(ub), jax.nn.gelu(vb), sgu_ln_g[l], sgu_ln_b[l],
                                    sgu_w_s[l], sgu_b_s[l])
        merged = jax.nn.sigmoid(ma) * (ya @ w_out_a[l]) + jax.nn.sigmoid(mb) * (yb @ w_out_b[l])
        h = h + merged @ w_out[l]
        n2 = rms_norm(h, norm_mlp_g[l])
        h = h + jnp.square(jax.nn.relu(n2 @ w_up[l])) @ w_down[l]
    return rms_norm(h, norm_final_g)


import jax as _jax
import jax.numpy as _jnp

TWIN_FORMAT = 'train_step'
FWD_PARAMS = ['x', 'norm_mix_g', 'w_in', 'conv_w', 'conv_b', 'w_rgate', 'b_rgate', 'w_igate', 'b_igate', 'lru_lambda', 'w_out_a', 'sgu_ln_g', 'sgu_ln_b', 'sgu_w_s', 'sgu_b_s', 'w_out_b', 'w_out', 'norm_mlp_g', 'w_up', 'w_down', 'norm_final_g']
TWIN_WEIGHTS = ['norm_mix_g', 'w_in', 'conv_w', 'conv_b', 'w_rgate', 'b_rgate', 'w_igate', 'b_igate', 'lru_lambda', 'w_out_a', 'sgu_ln_g', 'sgu_ln_b', 'sgu_w_s', 'sgu_b_s', 'w_out_b', 'w_out', 'norm_mlp_g', 'w_up', 'w_down', 'norm_final_g']
TWIN_DIFF_INPUT = 'x'
TWIN_INPUTS = ['x', 'norm_mix_g', 'w_in', 'conv_w', 'conv_b', 'w_rgate', 'b_rgate', 'w_igate', 'b_igate', 'lru_lambda', 'w_out_a', 'sgu_ln_g', 'sgu_ln_b', 'sgu_w_s', 'sgu_b_s', 'w_out_b', 'w_out', 'norm_mlp_g', 'w_up', 'w_down', 'norm_final_g', 'loss_target', 'm_norm_mix_g', 'm_w_in', 'm_conv_w', 'm_conv_b', 'm_w_rgate', 'm_b_rgate', 'm_w_igate', 'm_b_igate', 'm_lru_lambda', 'm_w_out_a', 'm_sgu_ln_g', 'm_sgu_ln_b', 'm_sgu_w_s', 'm_sgu_b_s', 'm_w_out_b', 'm_w_out', 'm_norm_mlp_g', 'm_w_up', 'm_w_down', 'm_norm_final_g', 'v_norm_mix_g', 'v_w_in', 'v_conv_w', 'v_conv_b', 'v_w_rgate', 'v_b_rgate', 'v_w_igate', 'v_b_igate', 'v_lru_lambda', 'v_w_out_a', 'v_sgu_ln_g', 'v_sgu_ln_b', 'v_sgu_w_s', 'v_sgu_b_s', 'v_w_out_b', 'v_w_out', 'v_norm_mlp_g', 'v_w_up', 'v_w_down', 'v_norm_final_g']
TWIN_OUTPUTS = ['loss', 'grad_x', 'grad_norm_mix_g', 'grad_w_in', 'grad_conv_w', 'grad_conv_b', 'grad_w_rgate', 'grad_b_rgate', 'grad_w_igate', 'grad_b_igate', 'grad_lru_lambda', 'grad_w_out_a', 'grad_sgu_ln_g', 'grad_sgu_ln_b', 'grad_sgu_w_s', 'grad_sgu_b_s', 'grad_w_out_b', 'grad_w_out', 'grad_norm_mlp_g', 'grad_w_up', 'grad_w_down', 'grad_norm_final_g', 'delta_norm_mix_g', 'delta_w_in', 'delta_conv_w', 'delta_conv_b', 'delta_w_rgate', 'delta_b_rgate', 'delta_w_igate', 'delta_b_igate', 'delta_lru_lambda', 'delta_w_out_a', 'delta_sgu_ln_g', 'delta_sgu_ln_b', 'delta_sgu_w_s', 'delta_sgu_b_s', 'delta_w_out_b', 'delta_w_out', 'delta_norm_mlp_g', 'delta_w_up', 'delta_w_down', 'delta_norm_final_g', 'new_m_norm_mix_g', 'new_m_w_in', 'new_m_conv_w', 'new_m_conv_b', 'new_m_w_rgate', 'new_m_b_rgate', 'new_m_w_igate', 'new_m_b_igate', 'new_m_lru_lambda', 'new_m_w_out_a', 'new_m_sgu_ln_g', 'new_m_sgu_ln_b', 'new_m_sgu_w_s', 'new_m_sgu_b_s', 'new_m_w_out_b', 'new_m_w_out', 'new_m_norm_mlp_g', 'new_m_w_up', 'new_m_w_down', 'new_m_norm_final_g', 'new_v_norm_mix_g', 'new_v_w_in', 'new_v_conv_w', 'new_v_conv_b', 'new_v_w_rgate', 'new_v_b_rgate', 'new_v_w_igate', 'new_v_b_igate', 'new_v_lru_lambda', 'new_v_w_out_a', 'new_v_sgu_ln_g', 'new_v_sgu_ln_b', 'new_v_sgu_w_s', 'new_v_sgu_b_s', 'new_v_w_out_b', 'new_v_w_out', 'new_v_norm_mlp_g', 'new_v_w_up', 'new_v_w_down', 'new_v_norm_final_g']
TWIN_LEAF_KINDS = {'loss': 'loss', 'grad_x': 'grad_x', 'grad_norm_mix_g': 'grad_w', 'grad_w_in': 'grad_w', 'grad_conv_w': 'grad_w', 'grad_conv_b': 'grad_w', 'grad_w_rgate': 'grad_w', 'grad_b_rgate': 'grad_w', 'grad_w_igate': 'grad_w', 'grad_b_igate': 'grad_w', 'grad_lru_lambda': 'grad_w', 'grad_w_out_a': 'grad_w', 'grad_sgu_ln_g': 'grad_w', 'grad_sgu_ln_b': 'grad_w', 'grad_sgu_w_s': 'grad_w', 'grad_sgu_b_s': 'grad_w', 'grad_w_out_b': 'grad_w', 'grad_w_out': 'grad_w', 'grad_norm_mlp_g': 'grad_w', 'grad_w_up': 'grad_w', 'grad_w_down': 'grad_w', 'grad_norm_final_g': 'grad_w', 'delta_norm_mix_g': 'delta_w', 'delta_w_in': 'delta_w', 'delta_conv_w': 'delta_w', 'delta_conv_b': 'delta_w', 'delta_w_rgate': 'delta_w', 'delta_b_rgate': 'delta_w', 'delta_w_igate': 'delta_w', 'delta_b_igate': 'delta_w', 'delta_lru_lambda': 'delta_w', 'delta_w_out_a': 'delta_w', 'delta_sgu_ln_g': 'delta_w', 'delta_sgu_ln_b': 'delta_w', 'delta_sgu_w_s': 'delta_w', 'delta_sgu_b_s': 'delta_w', 'delta_w_out_b': 'delta_w', 'delta_w_out': 'delta_w', 'delta_norm_mlp_g': 'delta_w', 'delta_w_up': 'delta_w', 'delta_w_down': 'delta_w', 'delta_norm_final_g': 'delta_w', 'new_m_norm_mix_g': 'new_m', 'new_m_w_in': 'new_m', 'new_m_conv_w': 'new_m', 'new_m_conv_b': 'new_m', 'new_m_w_rgate': 'new_m', 'new_m_b_rgate': 'new_m', 'new_m_w_igate': 'new_m', 'new_m_b_igate': 'new_m', 'new_m_lru_lambda': 'new_m', 'new_m_w_out_a': 'new_m', 'new_m_sgu_ln_g': 'new_m', 'new_m_sgu_ln_b': 'new_m', 'new_m_sgu_w_s': 'new_m', 'new_m_sgu_b_s': 'new_m', 'new_m_w_out_b': 'new_m', 'new_m_w_out': 'new_m', 'new_m_norm_mlp_g': 'new_m', 'new_m_w_up': 'new_m', 'new_m_w_down': 'new_m', 'new_m_norm_final_g': 'new_m', 'new_v_norm_mix_g': 'new_v', 'new_v_w_in': 'new_v', 'new_v_conv_w': 'new_v', 'new_v_conv_b': 'new_v', 'new_v_w_rgate': 'new_v', 'new_v_b_rgate': 'new_v', 'new_v_w_igate': 'new_v', 'new_v_b_igate': 'new_v', 'new_v_lru_lambda': 'new_v', 'new_v_w_out_a': 'new_v', 'new_v_sgu_ln_g': 'new_v', 'new_v_sgu_ln_b': 'new_v', 'new_v_sgu_w_s': 'new_v', 'new_v_sgu_b_s': 'new_v', 'new_v_w_out_b': 'new_v', 'new_v_w_out': 'new_v', 'new_v_norm_mlp_g': 'new_v', 'new_v_w_up': 'new_v', 'new_v_w_down': 'new_v', 'new_v_norm_final_g': 'new_v'}


def _forward(args):
    return _fwd_reference(*[args[k] for k in FWD_PARAMS])


def _output_shape():
    def fwd():
        inp = _fwd_setup_inputs(0)
        return _fwd_reference(*[inp[k] for k in FWD_PARAMS])
    out = _jax.eval_shape(fwd)
    return out.shape, out.dtype

N_MICROBATCH = 1
ADAM_LR = 0.001
ADAM_B1 = 0.9
ADAM_B2 = 0.999
ADAM_EPS = 1e-08
ADAM_WD = 0.01
ADAM_STEP = 10
PER_EXAMPLE_BATCH_AXIS = {'x': 0, 'loss_target': 0}
SHARED_INPUTS = []
_WEIGHT_DTYPES = {'norm_mix_g': _jnp.float32, 'w_in': _jnp.float32, 'conv_w': _jnp.float32, 'conv_b': _jnp.float32, 'w_rgate': _jnp.float32, 'b_rgate': _jnp.float32, 'w_igate': _jnp.float32, 'b_igate': _jnp.float32, 'lru_lambda': _jnp.float32, 'w_out_a': _jnp.float32, 'sgu_ln_g': _jnp.float32, 'sgu_ln_b': _jnp.float32, 'sgu_w_s': _jnp.float32, 'sgu_b_s': _jnp.float32, 'w_out_b': _jnp.float32, 'w_out': _jnp.float32, 'norm_mlp_g': _jnp.float32, 'w_up': _jnp.float32, 'w_down': _jnp.float32, 'norm_final_g': _jnp.float32}
MOMENT_SCALE = {'norm_mix_g': 1.103029e-01, 'w_in': 4.565345e-02, 'conv_w': 4.205270e-02, 'conv_b': 4.758945e-01, 'w_rgate': 1.103055e-02, 'b_rgate': 9.194933e-03, 'w_igate': 1.962320e-02, 'b_igate': 1.458177e-02, 'lru_lambda': 1.912384e-02, 'w_out_a': 3.764454e-02, 'sgu_ln_g': 4.489299e-02, 'sgu_ln_b': 4.273609e-02, 'sgu_w_s': 6.195116e-02, 'sgu_b_s': 8.632840e-02, 'w_out_b': 8.115249e-02, 'w_out': 8.966606e-02, 'norm_mlp_g': 1.596707e-01, 'w_up': 7.670577e-02, 'w_down': 1.591219e-01, 'norm_final_g': 3.239905e+01}


def _to_microbatches(a, axis):
    t = _jnp.moveaxis(a, axis, 0)
    t = t.reshape((N_MICROBATCH, t.shape[0] // N_MICROBATCH) + t.shape[1:])
    return _jnp.moveaxis(t, 1, axis + 1)


def setup_inputs(seed: int = 0) -> dict:
    inp = _fwd_setup_inputs(seed)
    key = _jax.random.fold_in(_jax.random.key(seed), 7919)
    shape, _ = _output_shape()
    out = dict(inp)
    out["loss_target"] = _jax.random.normal(_jax.random.fold_in(key, 0), shape, _jnp.float32)
    for i, name in enumerate(TWIN_WEIGHTS):
        w = inp[name].astype(_jnp.float32)
        if MOMENT_SCALE is None:
            s = _jnp.sqrt(_jnp.mean(_jnp.square(w)) + 1e-30)
        else:
            s = MOMENT_SCALE[name]
        km, kv = _jax.random.split(_jax.random.fold_in(key, i + 1))
        out[name] = w
        out["m_" + name] = s * _jax.random.normal(km, w.shape, _jnp.float32)
        out["v_" + name] = (s * s) * _jax.random.uniform(kv, w.shape, _jnp.float32, 0.5, 1.5)
    if N_MICROBATCH > 1:
        for name, axis in PER_EXAMPLE_BATCH_AXIS.items():
            out[name] = _to_microbatches(out[name], axis)
    return {'x': out['x'], 'norm_mix_g': out['norm_mix_g'], 'w_in': out['w_in'], 'conv_w': out['conv_w'], 'conv_b': out['conv_b'], 'w_rgate': out['w_rgate'], 'b_rgate': out['b_rgate'], 'w_igate': out['w_igate'], 'b_igate': out['b_igate'], 'lru_lambda': out['lru_lambda'], 'w_out_a': out['w_out_a'], 'sgu_ln_g': out['sgu_ln_g'], 'sgu_ln_b': out['sgu_ln_b'], 'sgu_w_s': out['sgu_w_s'], 'sgu_b_s': out['sgu_b_s'], 'w_out_b': out['w_out_b'], 'w_out': out['w_out'], 'norm_mlp_g': out['norm_mlp_g'], 'w_up': out['w_up'], 'w_down': out['w_down'], 'norm_final_g': out['norm_final_g'], 'loss_target': out['loss_target'], 'm_norm_mix_g': out['m_norm_mix_g'], 'm_w_in': out['m_w_in'], 'm_conv_w': out['m_conv_w'], 'm_conv_b': out['m_conv_b'], 'm_w_rgate': out['m_w_rgate'], 'm_b_rgate': out['m_b_rgate'], 'm_w_igate': out['m_w_igate'], 'm_b_igate': out['m_b_igate'], 'm_lru_lambda': out['m_lru_lambda'], 'm_w_out_a': out['m_w_out_a'], 'm_sgu_ln_g': out['m_sgu_ln_g'], 'm_sgu_ln_b': out['m_sgu_ln_b'], 'm_sgu_w_s': out['m_sgu_w_s'], 'm_sgu_b_s': out['m_sgu_b_s'], 'm_w_out_b': out['m_w_out_b'], 'm_w_out': out['m_w_out'], 'm_norm_mlp_g': out['m_norm_mlp_g'], 'm_w_up': out['m_w_up'], 'm_w_down': out['m_w_down'], 'm_norm_final_g': out['m_norm_final_g'], 'v_norm_mix_g': out['v_norm_mix_g'], 'v_w_in': out['v_w_in'], 'v_conv_w': out['v_conv_w'], 'v_conv_b': out['v_conv_b'], 'v_w_rgate': out['v_w_rgate'], 'v_b_rgate': out['v_b_rgate'], 'v_w_igate': out['v_w_igate'], 'v_b_igate': out['v_b_igate'], 'v_lru_lambda': out['v_lru_lambda'], 'v_w_out_a': out['v_w_out_a'], 'v_sgu_ln_g': out['v_sgu_ln_g'], 'v_sgu_ln_b': out['v_sgu_ln_b'], 'v_sgu_w_s': out['v_sgu_w_s'], 'v_sgu_b_s': out['v_sgu_b_s'], 'v_w_out_b': out['v_w_out_b'], 'v_w_out': out['v_w_out'], 'v_norm_mlp_g': out['v_norm_mlp_g'], 'v_w_up': out['v_w_up'], 'v_w_down': out['v_w_down'], 'v_norm_final_g': out['v_norm_final_g']}


def _loss(weights, diff, rest, loss_target):
    with _jax.named_scope("forward"):
        args = {**rest, TWIN_DIFF_INPUT: diff, **{k: w.astype(_WEIGHT_DTYPES[k]) for k, w in weights.items()}}
        y = _forward(args)
    with _jax.named_scope("loss_head"):
        err = _jnp.square(y.astype(_jnp.float32) - loss_target)
        return 0.5 * _jnp.sum(_jnp.mean(err, axis=-1)) if err.ndim else 0.5 * err


def _adamw(w, g, m, v):
    m = ADAM_B1 * m + (1.0 - ADAM_B1) * g
    v = ADAM_B2 * v + (1.0 - ADAM_B2) * _jnp.square(g)
    m_hat = m / (1.0 - ADAM_B1 ** ADAM_STEP)
    v_hat = v / (1.0 - ADAM_B2 ** ADAM_STEP)
    delta = -ADAM_LR * (m_hat / (_jnp.sqrt(v_hat) + ADAM_EPS) + ADAM_WD * w)
    return delta, m, v


def reference(x, norm_mix_g, w_in, conv_w, conv_b, w_rgate, b_rgate, w_igate, b_igate, lru_lambda, w_out_a, sgu_ln_g, sgu_ln_b, sgu_w_s, sgu_b_s, w_out_b, w_out, norm_mlp_g, w_up, w_down, norm_final_g, loss_target, m_norm_mix_g, m_w_in, m_conv_w, m_conv_b, m_w_rgate, m_b_rgate, m_w_igate, m_b_igate, m_lru_lambda, m_w_out_a, m_sgu_ln_g, m_sgu_ln_b, m_sgu_w_s, m_sgu_b_s, m_w_out_b, m_w_out, m_norm_mlp_g, m_w_up, m_w_down, m_norm_final_g, v_norm_mix_g, v_w_in, v_conv_w, v_conv_b, v_w_rgate, v_b_rgate, v_w_igate, v_b_igate, v_lru_lambda, v_w_out_a, v_sgu_ln_g, v_sgu_ln_b, v_sgu_w_s, v_sgu_b_s, v_w_out_b, v_w_out, v_norm_mlp_g, v_w_up, v_w_down, v_norm_final_g):
    given = dict(x=x, norm_mix_g=norm_mix_g, w_in=w_in, conv_w=conv_w, conv_b=conv_b, w_rgate=w_rgate, b_rgate=b_rgate, w_igate=w_igate, b_igate=b_igate, lru_lambda=lru_lambda, w_out_a=w_out_a, sgu_ln_g=sgu_ln_g, sgu_ln_b=sgu_ln_b, sgu_w_s=sgu_w_s, sgu_b_s=sgu_b_s, w_out_b=w_out_b, w_out=w_out, norm_mlp_g=norm_mlp_g, w_up=w_up, w_down=w_down, norm_final_g=norm_final_g, loss_target=loss_target, m_norm_mix_g=m_norm_mix_g, m_w_in=m_w_in, m_conv_w=m_conv_w, m_conv_b=m_conv_b, m_w_rgate=m_w_rgate, m_b_rgate=m_b_rgate, m_w_igate=m_w_igate, m_b_igate=m_b_igate, m_lru_lambda=m_lru_lambda, m_w_out_a=m_w_out_a, m_sgu_ln_g=m_sgu_ln_g, m_sgu_ln_b=m_sgu_ln_b, m_sgu_w_s=m_sgu_w_s, m_sgu_b_s=m_sgu_b_s, m_w_out_b=m_w_out_b, m_w_out=m_w_out, m_norm_mlp_g=m_norm_mlp_g, m_w_up=m_w_up, m_w_down=m_w_down, m_norm_final_g=m_norm_final_g, v_norm_mix_g=v_norm_mix_g, v_w_in=v_w_in, v_conv_w=v_conv_w, v_conv_b=v_conv_b, v_w_rgate=v_w_rgate, v_b_rgate=v_b_rgate, v_w_igate=v_w_igate, v_b_igate=v_b_igate, v_lru_lambda=v_lru_lambda, v_w_out_a=v_w_out_a, v_sgu_ln_g=v_sgu_ln_g, v_sgu_ln_b=v_sgu_ln_b, v_sgu_w_s=v_sgu_w_s, v_sgu_b_s=v_sgu_b_s, v_w_out_b=v_w_out_b, v_w_out=v_w_out, v_norm_mlp_g=v_norm_mlp_g, v_w_up=v_w_up, v_w_down=v_w_down, v_norm_final_g=v_norm_final_g)
    weights = {n: given[n] for n in TWIN_WEIGHTS}
    shared = {n: given[n] for n in SHARED_INPUTS}
    per_example = {n: given[n] for n in ['x']}
    grad_fn = _jax.value_and_grad(_loss, argnums=(0, 1))

    def one_microbatch(ex, loss_target):
        ex = dict(ex)
        diff = ex.pop(TWIN_DIFF_INPUT)
        return grad_fn(weights, diff, {**shared, **ex}, loss_target)

    if N_MICROBATCH == 1:
        loss, (grad_w, grad_x) = one_microbatch(per_example, given["loss_target"])
    else:
        def body(carry, xs):
            loss_sum, grad_sum = carry
            l_k, (gw_k, gx_k) = one_microbatch(xs[0], xs[1])
            with _jax.named_scope("update"):
                return (loss_sum + l_k, _jax.tree.map(_jnp.add, grad_sum, gw_k)), gx_k

        init = (_jnp.zeros((), _jnp.float32), _jax.tree.map(_jnp.zeros_like, weights))
        (loss, grad_w), grad_x = _jax.lax.scan(body, init, (per_example, given["loss_target"]))
    with _jax.named_scope("update"):
        delta_w, new_m, new_v = {}, {}, {}
        for n in TWIN_WEIGHTS:
            delta_w[n], new_m[n], new_v[n] = _adamw(weights[n], grad_w[n], given["m_" + n], given["v_" + n])
    return (loss, grad_x, *[grad_w[n] for n in TWIN_WEIGHTS], *[delta_w[n] for n in TWIN_WEIGHTS],
            *[new_m[n] for n in TWIN_WEIGHTS], *[new_v[n] for n in TWIN_WEIGHTS])
```

```python
import functools

import jax
import jax.numpy as jnp
from jax import lax
from jax.experimental import pallas as pl
from jax.experimental.pallas import tpu as pltpu

F32 = jnp.float32
BF16 = jnp.bfloat16
MESH = pl.DeviceIdType.MESH

D_MODEL = 1024
D_IN = 6 * D_MODEL
D_FF = 4 * D_MODEL
N_CHIPS = 4
IN_SHARD = D_IN // N_CHIPS
HEADS = 4
HEAD_DIM = D_MODEL // HEADS
GROUPS = 4
GROUP_DIM = D_MODEL // GROUPS
CHUNK = 128
CONV_WIDTH = 4
LRU_C = 8.0
NORM_EPS = 1e-6
LN_EPS = 1e-5

ADAM_LR = 0.001
ADAM_B1 = 0.9
ADAM_B2 = 0.999
ADAM_EPS = 1e-08
ADAM_WD = 0.01
ADAM_STEP = 10

SUBLANES = 8
TOKEN_TILE = 256
DW_TOKEN_TILE = 512
VMEM_LIMIT_BYTES = 56 * 1024 * 1024

GELU_K0 = 0.7978845608028654
GELU_K1 = 0.044715


def _params(n_grid_axes=1):
    return pltpu.CompilerParams(
        dimension_semantics=("arbitrary",) * n_grid_axes, vmem_limit_bytes=VMEM_LIMIT_BYTES)


def _resident(shape):
    nd = len(shape)
    return pl.BlockSpec(shape, lambda *_: (0,) * nd, pipeline_mode=pl.Buffered(1))


def _const(shape):
    nd = len(shape)
    return pl.BlockSpec(shape, lambda *_: (0,) * nd)


def _dot(a, b):
    return jnp.dot(a, b, preferred_element_type=F32)


def _dot_nt(a, b):
    return lax.dot_general(a, b, (((1,), (1,)), ((), ())), preferred_element_type=F32)


def _dot_tn(a, b):
    return lax.dot_general(a, b, (((0,), (0,)), ((), ())), preferred_element_type=F32)


def _gelu(x):
    t = jnp.tanh(GELU_K0 * x * (1.0 + GELU_K1 * x * x))
    return 0.5 * x * (1.0 + t)


def _gelu_and_grad(x):
    x2 = x * x
    t = jnp.tanh(GELU_K0 * x * (1.0 + GELU_K1 * x2))
    g = 0.5 * x * (1.0 + t)
    dg = 0.5 * (1.0 + t) + 0.5 * x * (1.0 - t * t) * (GELU_K0 * (1.0 + 3.0 * GELU_K1 * x2))
    return g, dg


def _rms(x):
    r = lax.rsqrt(jnp.mean(x * x, axis=-1, keepdims=True) + NORM_EPS)
    return x * r, r


def _rms_bwd(dn, xhat, r):
    return r * (dn - xhat * jnp.mean(dn * xhat, axis=-1, keepdims=True))


def _col_sum(v):
    return jnp.sum(v, axis=0, keepdims=True)


def _shift_down(x, tail8, k):
    xs = pltpu.roll(x, k, 0)
    ts = pltpu.roll(tail8, k, 0)
    ridx = lax.broadcasted_iota(jnp.int32, tail8.shape, 0)
    head = jnp.where(ridx < k, ts, xs[0:SUBLANES])
    return jnp.concatenate([head, xs[SUBLANES:]], axis=0)


def _shift_up(x, head8, k):
    n = x.shape[0]
    xs = pltpu.roll(x, n - k, 0)
    hs = pltpu.roll(head8, SUBLANES - k, 0)
    ridx = lax.broadcasted_iota(jnp.int32, head8.shape, 0)
    last = jnp.where(ridx >= SUBLANES - k, hs, xs[n - SUBLANES:n])
    return jnp.concatenate([xs[:n - SUBLANES], last], axis=0)


def _scan_forward(a, b, carry):
    n = a.shape[0]
    sub = lax.broadcasted_iota(jnp.int32, a.shape, 0) & (SUBLANES - 1)
    for s in (1, 2, 4):
        a_s = pltpu.roll(a, s, 0)
        b_s = pltpu.roll(b, s, 0)
        m = sub >= s
        b = jnp.where(m, a * b_s + b, b)
        a = jnp.where(m, a * a_s, a)
    out = []
    for g in range(n // SUBLANES):
        rows = slice(g * SUBLANES, (g + 1) * SUBLANES)
        h = a[rows] * carry + b[rows]
        out.append(h)
        carry = h[SUBLANES - 1:SUBLANES]
    return jnp.concatenate(out, axis=0), carry


def _scan_backward(a, b, carry):
    n = a.shape[0]
    sub = lax.broadcasted_iota(jnp.int32, a.shape, 0) & (SUBLANES - 1)
    for s in (1, 2, 4):
        a_s = pltpu.roll(a, n - s, 0)
        b_s = pltpu.roll(b, n - s, 0)
        m = sub < SUBLANES - s
        b = jnp.where(m, a * b_s + b, b)
        a = jnp.where(m, a * a_s, a)
    out = [None] * (n // SUBLANES)
    for g in reversed(range(n // SUBLANES)):
        rows = slice(g * SUBLANES, (g + 1) * SUBLANES)
        h = a[rows] * carry + b[rows]
        out[g] = h
        carry = h[0:1]
    return jnp.concatenate(out, axis=0), carry


def _softplus_neg(lam):
    e = jnp.exp(-jnp.abs(lam))
    u = 1.0 + e
    log1p_e = jnp.where(u == 1.0, e, jnp.log(u) * (e / jnp.where(u == 1.0, 1.0, u - 1.0)))
    return jnp.maximum(-lam, 0.0) + log1p_e


def _one_minus_exp_neg(x):
    poly = x * (1.0 - 0.5 * x * (1.0 - (1.0 / 3.0) * x * (1.0 - 0.25 * x * (1.0 - 0.2 * x))))
    return jnp.where(x < 0.02, poly, 1.0 - jnp.exp(-x))


def _lru_gates(xa, tail8, cw_ref, cb_ref, wr_ref, br_ref, wi_ref, bi_ref, lam_ref):
    cw = cw_ref[...]
    xs = [xa] + [_shift_down(xa, tail8, k) for k in range(1, CONV_WIDTH)]
    xc = cb_ref[...] + cw[0:1] * xs[0]
    for k in range(1, CONV_WIDTH):
        xc = xc + cw[k:k + 1] * xs[k]
    xcb = xc.astype(BF16)
    pre_r, pre_i = [], []
    for h in range(HEADS):
        cols = slice(h * HEAD_DIM, (h + 1) * HEAD_DIM)
        pre_r.append(_dot(xcb[:, cols], wr_ref[h]))
        pre_i.append(_dot(xcb[:, cols], wi_ref[h]))
    r = jax.nn.sigmoid(jnp.concatenate(pre_r, axis=1) + br_ref[...])
    ig = jax.nn.sigmoid(jnp.concatenate(pre_i, axis=1) + bi_ref[...])
    sp = _softplus_neg(lam_ref[...])
    log_a = (-LRU_C) * sp * r
    a = jnp.exp(log_a)
    mult = jnp.sqrt(_one_minus_exp_neg(-2.0 * log_a))
    return xs, xc, xcb, r, ig, sp, a, mult


def _fwd_in(x, g1, w_in_st):
    t = x.shape[0]

    def body(x_ref, g_ref, w_ref, z_ref, n_ref):
        xhat, _ = _rms(x_ref[...])
        n = (xhat * g_ref[...]).astype(BF16)
        n_ref[...] = n
        for k in range(N_CHIPS):
            z_ref[:, k * IN_SHARD:(k + 1) * IN_SHARD] = _dot(n, w_ref[k])

    return pl.pallas_call(
        body, name="fwd_in", grid=(t // TOKEN_TILE,),
        in_specs=[pl.BlockSpec((TOKEN_TILE, D_MODEL), lambda i: (i, 0)), _const((1, D_MODEL)),
                  _resident((N_CHIPS, D_MODEL, IN_SHARD))],
        out_specs=[pl.BlockSpec((TOKEN_TILE, D_IN), lambda i: (i, 0)),
                   pl.BlockSpec((TOKEN_TILE, D_MODEL), lambda i: (i, 0))],
        out_shape=[jax.ShapeDtypeStruct((t, D_IN), F32), jax.ShapeDtypeStruct((t, D_MODEL), BF16)],
        compiler_params=_params(),
    )(x, g1, w_in_st)


def _fwd_lru(z, conv_w, conv_b, wr, br, wi, bi, lam):
    t = z.shape[0]

    def body(xa_ref, ga_ref, cw_ref, cb_ref, wr_ref, br_ref, wi_ref, bi_ref, lam_ref, h_ref, ya_ref,
             tail_ref, carry_ref):
        @pl.when(pl.program_id(0) == 0)
        def _():
            tail_ref[...] = jnp.zeros_like(tail_ref)
            carry_ref[...] = jnp.zeros_like(carry_ref)

        xa = xa_ref[...]
        _, xc, _, _, ig, _, a, mult = _lru_gates(
            xa, tail_ref[...], cw_ref, cb_ref, wr_ref, br_ref, wi_ref, bi_ref, lam_ref)
        tail_ref[...] = xa[TOKEN_TILE - SUBLANES:]
        h, carry = _scan_forward(a, xc * ig * mult, carry_ref[...])
        carry_ref[...] = carry
        h_ref[...] = h
        ya_ref[...] = (h * _gelu(ga_ref[...])).astype(BF16)

    tile = lambda j: pl.BlockSpec((TOKEN_TILE, D_MODEL), lambda i: (i, j))
    return pl.pallas_call(
        body, name="fwd_lru", grid=(t // TOKEN_TILE,),
        in_specs=[tile(0), tile(1), _const((CONV_WIDTH, D_MODEL)), _const((1, D_MODEL)),
                  _resident((HEADS, HEAD_DIM, HEAD_DIM)), _const((1, D_MODEL)),
                  _resident((HEADS, HEAD_DIM, HEAD_DIM)), _const((1, D_MODEL)), _const((1, D_MODEL))],
        out_specs=[tile(0), tile(0)],
        out_shape=[jax.ShapeDtypeStruct((t, D_MODEL), F32), jax.ShapeDtypeStruct((t, D_MODEL), BF16)],
        scratch_shapes=[pltpu.VMEM((SUBLANES, D_MODEL), F32), pltpu.VMEM((1, D_MODEL), F32)],
        compiler_params=_params(),
    )(z, z, conv_w, conv_b, wr, br, wi, bi, lam)


def _sgu_forward_parts(ub, vb, lg_ref, lb_ref):
    u, du = _gelu_and_grad(ub)
    vg, dvg = _gelu_and_grad(vb)
    mu = jnp.mean(vg, axis=-1, keepdims=True)
    d = vg - mu
    rstd = lax.rsqrt(jnp.mean(d * d, axis=-1, keepdims=True) + LN_EPS)
    vhat = d * rstd
    vn = (vhat * lg_ref[...] + lb_ref[...]).astype(BF16)
    return u, du, dvg, rstd, vhat, vn


def _causal_mask():
    rows = lax.broadcasted_iota(jnp.int32, (CHUNK, CHUNK), 0)
    cols = lax.broadcasted_iota(jnp.int32, (CHUNK, CHUNK), 1)
    return rows >= cols


def _fwd_sgu(z, ln_g, ln_b, w_s, bias_full):
    t = z.shape[0]

    def body(ub_ref, vb_ref, lg_ref, lb_ref, ws_ref, bias_ref, yb_ref):
        u, _, _, _, _, vn = _sgu_forward_parts(ub_ref[...], vb_ref[...], lg_ref, lb_ref)
        mask = _causal_mask()
        wm = [jnp.where(mask, ws_ref[g], 0.0).astype(BF16) for g in range(GROUPS)]
        for c in range(TOKEN_TILE // CHUNK):
            rows = slice(c * CHUNK, (c + 1) * CHUNK)
            for g in range(GROUPS):
                cols = slice(g * GROUP_DIM, (g + 1) * GROUP_DIM)
                sp = _dot(wm[g], vn[rows, cols]) + bias_ref[:, cols]
                yb_ref[rows, cols] = (u[rows, cols] * sp).astype(BF16)

    tile = lambda j: pl.BlockSpec((TOKEN_TILE, D_MODEL), lambda i: (i, j))
    return pl.pallas_call(
        body, name="fwd_sgu", grid=(t // TOKEN_TILE,),
        in_specs=[tile(2), tile(3), _const((1, D_MODEL)), _const((1, D_MODEL)),
                  _const((GROUPS, CHUNK, CHUNK)), _const((CHUNK, D_MODEL))],
        out_specs=tile(0),
        out_shape=jax.ShapeDtypeStruct((t, D_MODEL), BF16),
        compiler_params=_params(),
    )(z, z, ln_g, ln_b, w_s, bias_full)


def _fwd_merge(ya, yb, z, x, w_oa, w_ob, w_out, g2):
    t = x.shape[0]

    def body(ya_ref, yb_ref, m_ref, x_ref, woa_ref, wob_ref, wout_ref, g_ref, pa_ref, pb_ref, h1_ref, n2_ref):
        pa = _dot(ya_ref[...], woa_ref[...])
        pb = _dot(yb_ref[...], wob_ref[...])
        pa_ref[...] = pa
        pb_ref[...] = pb
        merged = jax.nn.sigmoid(m_ref[:, :D_MODEL]) * pa + jax.nn.sigmoid(m_ref[:, D_MODEL:]) * pb
        h1 = x_ref[...] + _dot(merged.astype(BF16), wout_ref[...])
        h1_ref[...] = h1
        xhat, _ = _rms(h1)
        n2_ref[...] = (xhat * g_ref[...]).astype(BF16)

    tile = pl.BlockSpec((TOKEN_TILE, D_MODEL), lambda i: (i, 0))
    sq = _resident((D_MODEL, D_MODEL))
    return pl.pallas_call(
        body, name="fwd_merge", grid=(t // TOKEN_TILE,),
        in_specs=[tile, tile, pl.BlockSpec((TOKEN_TILE, 2 * D_MODEL), lambda i: (i, 2)), tile, sq, sq, sq,
                  _const((1, D_MODEL))],
        out_specs=[tile, tile, tile, tile],
        out_shape=[jax.ShapeDtypeStruct((t, D_MODEL), F32)] * 3 + [jax.ShapeDtypeStruct((t, D_MODEL), BF16)],
        compiler_params=_params(),
    )(ya, yb, z, x, w_oa, w_ob, w_out, g2)


def _fwd_mlp(n2, h1, target, w_up_st, w_down, g3):
    t = n2.shape[0]

    def body(n2_ref, h1_ref, tgt_ref, wup_ref, wdown_ref, g_ref, up_ref, act_ref, dh2_ref, dh2b_ref, loss_ref,
             dg3_ref):
        @pl.when(pl.program_id(0) == 0)
        def _():
            loss_ref[...] = jnp.zeros_like(loss_ref)
            dg3_ref[...] = jnp.zeros_like(dg3_ref)

        n2 = n2_ref[...]
        h2 = h1_ref[...]
        for k in range(N_CHIPS):
            cols = slice(k * D_MODEL, (k + 1) * D_MODEL)
            up = _dot(n2, wup_ref[k])
            up_ref[:, cols] = up
            r = jnp.maximum(up, 0.0)
            act = (r * r).astype(BF16)
            act_ref[:, cols] = act
            h2 = h2 + _dot(act, wdown_ref[cols, :])
        xhat, r3 = _rms(h2)
        diff = xhat * g_ref[...] - tgt_ref[...]
        sq = jnp.sum(diff * diff, axis=1, keepdims=True)
        loss_ref[...] = loss_ref[...] + (0.5 / D_MODEL) * jnp.sum(sq, axis=0, keepdims=True)
        dy = diff * (1.0 / D_MODEL)
        dg3_ref[...] = dg3_ref[...] + _col_sum(dy * xhat)
        dh2 = _rms_bwd(dy * g_ref[...], xhat, r3)
        dh2_ref[...] = dh2
        dh2b_ref[...] = dh2.astype(BF16)

    tile = pl.BlockSpec((TOKEN_TILE, D_MODEL), lambda i: (i, 0))
    wide = pl.BlockSpec((TOKEN_TILE, D_FF), lambda i: (i, 0))
    return pl.pallas_call(
        body, name="fwd_mlp", grid=(t // TOKEN_TILE,),
        in_specs=[tile, tile, tile, _resident((N_CHIPS, D_MODEL, D_MODEL)), _resident((D_FF, D_MODEL)),
                  _const((1, D_MODEL))],
        out_specs=[wide, wide, tile, tile, _const((SUBLANES, 128)), _const((1, D_MODEL))],
        out_shape=[jax.ShapeDtypeStruct((t, D_FF), F32), jax.ShapeDtypeStruct((t, D_FF), BF16),
                   jax.ShapeDtypeStruct((t, D_MODEL), F32), jax.ShapeDtypeStruct((t, D_MODEL), BF16),
                   jax.ShapeDtypeStruct((SUBLANES, 128), F32), jax.ShapeDtypeStruct((1, D_MODEL), F32)],
        compiler_params=_params(),
    )(n2, h1, target, w_up_st, w_down, g3)


def _bwd_mlp(dh2, dh2b, up, h1, w_up_st, w_down, g2):
    t = dh2.shape[0]

    def body(dh2_ref, dh2b_ref, up_ref, h1_ref, wup_ref, wdown_ref, g_ref, dup_ref, dh1_ref, dg2_ref):
        @pl.when(pl.program_id(0) == 0)
        def _():
            dg2_ref[...] = jnp.zeros_like(dg2_ref)

        dh2b = dh2b_ref[...]
        dn2 = jnp.zeros((TOKEN_TILE, D_MODEL), F32)
        for k in range(N_CHIPS):
            cols = slice(k * D_MODEL, (k + 1) * D_MODEL)
            dact = _dot_nt(dh2b, wdown_ref[cols, :])
            dup = (dact * (2.0 * jnp.maximum(up_ref[:, cols], 0.0))).astype(BF16)
            dup_ref[:, cols] = dup
            dn2 = dn2 + _dot_nt(dup, wup_ref[k])
        xhat, r2 = _rms(h1_ref[...])
        dg2_ref[...] = dg2_ref[...] + _col_sum(dn2 * xhat)
        dh1_ref[...] = dh2_ref[...] + _rms_bwd(dn2 * g_ref[...], xhat, r2)

    tile = pl.BlockSpec((TOKEN_TILE, D_MODEL), lambda i: (i, 0))
    wide = pl.BlockSpec((TOKEN_TILE, D_FF), lambda i: (i, 0))
    return pl.pallas_call(
        body, name="bwd_mlp", grid=(t // TOKEN_TILE,),
        in_specs=[tile, tile, wide, tile, _resident((N_CHIPS, D_MODEL, D_MODEL)), _resident((D_FF, D_MODEL)),
                  _const((1, D_MODEL))],
        out_specs=[wide, tile, _const((1, D_MODEL))],
        out_shape=[jax.ShapeDtypeStruct((t, D_FF), BF16), jax.ShapeDtypeStruct((t, D_MODEL), F32),
                   jax.ShapeDtypeStruct((1, D_MODEL), F32)],
        compiler_params=_params(),
    )(dh2, dh2b, up, h1, w_up_st, w_down, g2)


def _bwd_merge(dh1, pa, pb, z, w_oa, w_ob, w_out):
    t = dh1.shape[0]

    def body(dh1_ref, pa_ref, pb_ref, m_ref, woa_ref, wob_ref, wout_ref, dz_ref, dya_ref, dyb_ref, mg_ref,
             dpa_ref, dpb_ref, dh1b_ref):
        dh1b = dh1_ref[...].astype(BF16)
        dh1b_ref[...] = dh1b
        dm = _dot_nt(dh1b, wout_ref[...])
        pa = pa_ref[...]
        pb = pb_ref[...]
        sa = jax.nn.sigmoid(m_ref[:, :D_MODEL])
        sb = jax.nn.sigmoid(m_ref[:, D_MODEL:])
        mg_ref[...] = (sa * pa + sb * pb).astype(BF16)
        dz_ref[:, :D_MODEL] = (dm * pa * sa * (1.0 - sa)).astype(BF16)
        dz_ref[:, D_MODEL:] = (dm * pb * sb * (1.0 - sb)).astype(BF16)
        dpa = (dm * sa).astype(BF16)
        dpb = (dm * sb).astype(BF16)
        dpa_ref[...] = dpa
        dpb_ref[...] = dpb
        dya_ref[...] = _dot_nt(dpa, woa_ref[...])
        dyb_ref[...] = _dot_nt(dpb, wob_ref[...])

    tile = pl.BlockSpec((TOKEN_TILE, D_MODEL), lambda i: (i, 0))
    pair = pl.BlockSpec((TOKEN_TILE, 2 * D_MODEL), lambda i: (i, 2))
    sq = _resident((D_MODEL, D_MODEL))
    act_bf = jax.ShapeDtypeStruct((t, D_MODEL), BF16)
    return pl.pallas_call(
        body, name="bwd_merge", grid=(t // TOKEN_TILE,),
        in_specs=[tile, tile, tile, pair, sq, sq, sq],
        out_specs=[pair, tile, tile, tile, tile, tile, tile],
        out_shape=[jax.ShapeDtypeStruct((t, D_IN), BF16), jax.ShapeDtypeStruct((t, D_MODEL), F32),
                   jax.ShapeDtypeStruct((t, D_MODEL), F32), act_bf, act_bf, act_bf, act_bf],
        compiler_params=_params(),
    )(dh1, pa, pb, z, w_oa, w_ob, w_out)


def _bwd_sgu(dz, dyb, z, ln_g, ln_b, w_s, bias_full):
    t = dyb.shape[0]
    n_tiles = t // TOKEN_TILE

    def body(dz_any, dyb_ref, ub_ref, vb_ref, lg_ref, lb_ref, ws_ref, bias_ref, dz_ref, dlg_ref, dlb_ref, dws_ref,
             dbs_ref, dvn_ref, dsp_acc):
        del dz_any
        i = pl.program_id(0)

        @pl.when(i == 0)
        def _():
            dlg_ref[...] = jnp.zeros_like(dlg_ref)
            dlb_ref[...] = jnp.zeros_like(dlb_ref)
            dws_ref[...] = jnp.zeros_like(dws_ref)
            dsp_acc[...] = jnp.zeros_like(dsp_acc)

        u, du, dvg, rstd, vhat, vn = _sgu_forward_parts(ub_ref[...], vb_ref[...], lg_ref, lb_ref)
        dyb = dyb_ref[...]
        mask = _causal_mask()
        wm = [jnp.where(mask, ws_ref[g], 0.0).astype(BF16) for g in range(GROUPS)]
        for c in range(TOKEN_TILE // CHUNK):
            rows = slice(c * CHUNK, (c + 1) * CHUNK)
            for g in range(GROUPS):
                cols = slice(g * GROUP_DIM, (g + 1) * GROUP_DIM)
                vn_blk = vn[rows, cols]
                sp = _dot(wm[g], vn_blk) + bias_ref[:, cols]
                dyb_blk = dyb[rows, cols]
                dz_ref[rows, cols] = (dyb_blk * sp * du[rows, cols]).astype(BF16)
                dsp = dyb_blk * u[rows, cols]
                dsp_acc[:, cols] = dsp_acc[:, cols] + dsp
                dspb = dsp.astype(BF16)
                dvn_ref[rows, cols] = _dot_tn(wm[g], dspb)
                wcols = slice(g * CHUNK, (g + 1) * CHUNK)
                dws_ref[:, wcols] = dws_ref[:, wcols] + jnp.where(mask, _dot_nt(dspb, vn_blk), 0.0)
        dvn = dvn_ref[...]
        dlg_ref[...] = dlg_ref[...] + _col_sum(dvn * vhat)
        dlb_ref[...] = dlb_ref[...] + _col_sum(dvn)
        dvhat = dvn * lg_ref[...]
        dvgel = rstd * (dvhat - jnp.mean(dvhat, axis=-1, keepdims=True)
                        - vhat * jnp.mean(dvhat * vhat, axis=-1, keepdims=True))
        dz_ref[:, D_MODEL:] = (dvgel * dvg).astype(BF16)

        @pl.when(i == n_tiles - 1)
        def _():
            lane = lax.broadcasted_iota(jnp.int32, (CHUNK, 128), 1)
            out = jnp.zeros((CHUNK, 128), F32)
            for g in range(GROUPS):
                s = jnp.sum(dsp_acc[:, g * GROUP_DIM:(g + 1) * GROUP_DIM], axis=1, keepdims=True)
                out = out + jnp.where(lane == g, s, 0.0)
            dbs_ref[...] = out

    tile = lambda j: pl.BlockSpec((TOKEN_TILE, D_MODEL), lambda i: (i, j))
    return pl.pallas_call(
        body, name="bwd_sgu", grid=(n_tiles,),
        in_specs=[pl.BlockSpec(memory_space=pl.ANY), tile(0), tile(2), tile(3), _const((1, D_MODEL)),
                  _const((1, D_MODEL)), _const((GROUPS, CHUNK, CHUNK)), _const((CHUNK, D_MODEL))],
        out_specs=[pl.BlockSpec((TOKEN_TILE, 2 * D_MODEL), lambda i: (i, 1)), _const((1, D_MODEL)),
                   _const((1, D_MODEL)), _const((CHUNK, GROUPS * CHUNK)), _const((CHUNK, 128))],
        out_shape=[jax.ShapeDtypeStruct((t, D_IN), BF16), jax.ShapeDtypeStruct((1, D_MODEL), F32),
                   jax.ShapeDtypeStruct((1, D_MODEL), F32), jax.ShapeDtypeStruct((CHUNK, GROUPS * CHUNK), F32),
                   jax.ShapeDtypeStruct((CHUNK, 128), F32)],
        scratch_shapes=[pltpu.VMEM((TOKEN_TILE, D_MODEL), F32), pltpu.VMEM((CHUNK, D_MODEL), F32)],
        input_output_aliases={0: 0},
        compiler_params=_params(),
    )(dz, dyb, z, z, ln_g, ln_b, w_s, bias_full)


def _bwd_lru(dz, dya, z, h, conv_w, conv_b, wr, br, wi, bi, lam):
    t = dya.shape[0]
    n_tiles = t // TOKEN_TILE
    per_tile = TOKEN_TILE // SUBLANES

    def body(dz_any, dya_ref, xa_ref, xa_prev_ref, ga_ref, h_ref, h_prev_ref, cw_ref, cb_ref, wr_ref, br_ref, wi_ref,
             bi_ref, lam_ref, dz_ref, dcw_ref, dcb_ref, dwr_ref, dbr_ref, dwi_ref, dbi_ref, dlam_ref, lam_carry,
             dxc_head):
        del dz_any
        i = pl.program_id(0)

        @pl.when(i == 0)
        def _():
            for ref in (dcw_ref, dcb_ref, dwr_ref, dbr_ref, dwi_ref, dbi_ref, dlam_ref, lam_carry, dxc_head):
                ref[...] = jnp.zeros_like(ref)

        first_tile = i == n_tiles - 1
        xa = xa_ref[...]
        tail = jnp.where(first_tile, 0.0, xa_prev_ref[...])
        h_tail = jnp.where(first_tile, 0.0, h_prev_ref[...])
        xs, xc, xcb, r, ig, sp, a, mult = _lru_gates(xa, tail, cw_ref, cb_ref, wr_ref, br_ref, wi_ref, bi_ref, lam_ref)
        h = h_ref[...]
        h_prev = _shift_down(h, h_tail, 1)
        dya = dya_ref[...]
        gg, dgg = _gelu_and_grad(ga_ref[...])
        dz_ref[:, D_MODEL:] = (dya * h * dgg).astype(BF16)
        ones = jnp.ones((SUBLANES, D_MODEL), F32)
        lam_t, lam_first = _scan_backward(_shift_up(a, ones, 1), dya * gg, lam_carry[...])
        lam_carry[...] = a[0:1] * lam_first
        dmult = lam_t * xc * ig
        dla = lam_t * h_prev * a - dmult * (a * a) / mult
        dr = dla * ((-LRU_C) * sp)
        dlam_ref[...] = dlam_ref[...] + _col_sum(dla * r) * (LRU_C * jax.nn.sigmoid(-lam_ref[...]))
        dpr = dr * r * (1.0 - r)
        dpi = lam_t * xc * mult * ig * (1.0 - ig)
        dbr_ref[...] = dbr_ref[...] + _col_sum(dpr)
        dbi_ref[...] = dbi_ref[...] + _col_sum(dpi)
        dprb = dpr.astype(BF16)
        dpib = dpi.astype(BF16)
        dxc_gate = []
        for hd in range(HEADS):
            cols = slice(hd * HEAD_DIM, (hd + 1) * HEAD_DIM)
            dxc_gate.append(_dot_nt(dprb[:, cols], wr_ref[hd]) + _dot_nt(dpib[:, cols], wi_ref[hd]))
            dwr_ref[hd] = dwr_ref[hd] + _dot_tn(xcb[:, cols], dprb[:, cols])
            dwi_ref[hd] = dwi_ref[hd] + _dot_tn(xcb[:, cols], dpib[:, cols])
        dxc = lam_t * ig * mult + jnp.concatenate(dxc_gate, axis=1)
        dcb_ref[...] = dcb_ref[...] + _col_sum(dxc)
        cw = cw_ref[...]
        head = dxc_head[...]
        dxa = cw[0:1] * dxc
        for k in range(CONV_WIDTH):
            dcw_ref[k:k + 1, :] = dcw_ref[k:k + 1, :] + _col_sum(dxc * xs[k])
            if k:
                dxa = dxa + cw[k:k + 1] * _shift_up(dxc, head, k)
        dxc_head[...] = dxc[0:SUBLANES]
        dz_ref[:, :D_MODEL] = dxa.astype(BF16)

    rev = lambda i: n_tiles - 1 - i
    tile = lambda j: pl.BlockSpec((TOKEN_TILE, D_MODEL), lambda i: (rev(i), j))
    prev8 = pl.BlockSpec((SUBLANES, D_MODEL), lambda i: (jnp.maximum(rev(i) * per_tile - 1, 0), 0))
    vec = _const((1, D_MODEL))
    gate_w = _resident((HEADS, HEAD_DIM, HEAD_DIM))
    vec_shape = jax.ShapeDtypeStruct((1, D_MODEL), F32)
    gate_shape = jax.ShapeDtypeStruct((HEADS, HEAD_DIM, HEAD_DIM), F32)
    return pl.pallas_call(
        body, name="bwd_lru", grid=(n_tiles,),
        in_specs=[pl.BlockSpec(memory_space=pl.ANY), tile(0), tile(0), prev8, tile(1), tile(0), prev8,
                  _const((CONV_WIDTH, D_MODEL)), vec, gate_w, vec, gate_w, vec, vec],
        out_specs=[pl.BlockSpec((TOKEN_TILE, 2 * D_MODEL), lambda i: (rev(i), 0)), _const((SUBLANES, D_MODEL)), vec,
                   _const((HEADS, HEAD_DIM, HEAD_DIM)), vec, _const((HEADS, HEAD_DIM, HEAD_DIM)), vec, vec],
        out_shape=[jax.ShapeDtypeStruct((t, D_IN), BF16), jax.ShapeDtypeStruct((SUBLANES, D_MODEL), F32), vec_shape,
                   gate_shape, vec_shape, gate_shape, vec_shape, vec_shape],
        scratch_shapes=[pltpu.VMEM((1, D_MODEL), F32), pltpu.VMEM((SUBLANES, D_MODEL), F32)],
        input_output_aliases={0: 0},
        compiler_params=_params(),
    )(dz, dya, z, z, z, h, h, conv_w, conv_b, wr, br, wi, bi, lam)


def _bwd_in(dz, x, dh1, w_in_st, g1):
    t = x.shape[0]

    def body(dz_ref, x_ref, dh1_ref, w_ref, g_ref, dx_ref, dg1_ref):
        @pl.when(pl.program_id(0) == 0)
        def _():
            dg1_ref[...] = jnp.zeros_like(dg1_ref)

        dn1 = jnp.zeros((TOKEN_TILE, D_MODEL), F32)
        for k in range(N_CHIPS):
            dn1 = dn1 + _dot_nt(dz_ref[:, k * IN_SHARD:(k + 1) * IN_SHARD], w_ref[k])
        xhat, r1 = _rms(x_ref[...])
        dg1_ref[...] = dg1_ref[...] + _col_sum(dn1 * xhat)
        dx_ref[...] = dh1_ref[...] + _rms_bwd(dn1 * g_ref[...], xhat, r1)

    tile = pl.BlockSpec((TOKEN_TILE, D_MODEL), lambda i: (i, 0))
    return pl.pallas_call(
        body, name="bwd_in", grid=(t // TOKEN_TILE,),
        in_specs=[pl.BlockSpec((TOKEN_TILE, D_IN), lambda i: (i, 0)), tile, tile,
                  _resident((N_CHIPS, D_MODEL, IN_SHARD)), _const((1, D_MODEL))],
        out_specs=[tile, _const((1, D_MODEL))],
        out_shape=[jax.ShapeDtypeStruct((t, D_MODEL), F32), jax.ShapeDtypeStruct((1, D_MODEL), F32)],
        compiler_params=_params(),
    )(dz, x, dh1, w_in_st, g1)


def _weight_grad(name, a, b, n_blocks, a_varies, b_varies, width):
    t = a.shape[0]
    n_t = t // DW_TOKEN_TILE

    def body(a_ref, b_ref, o_ref, acc_ref):
        s = pl.program_id(1)

        @pl.when(s == 0)
        def _():
            acc_ref[...] = jnp.zeros_like(acc_ref)

        acc_ref[...] = acc_ref[...] + _dot_tn(a_ref[...], b_ref[...])

        @pl.when(s == n_t - 1)
        def _():
            o_ref[...] = acc_ref[...].astype(BF16)

    return pl.pallas_call(
        body, name=name, grid=(n_blocks, n_t),
        in_specs=[pl.BlockSpec((DW_TOKEN_TILE, D_MODEL), (lambda j, s: (s, j)) if a_varies else (lambda j, s: (s, 0))),
                  pl.BlockSpec((DW_TOKEN_TILE, width), (lambda j, s: (s, j)) if b_varies else (lambda j, s: (s, 0)))],
        out_specs=pl.BlockSpec((None, D_MODEL, width), lambda j, s: (j, 0, 0)),
        out_shape=jax.ShapeDtypeStruct((n_blocks, D_MODEL, width), BF16),
        scratch_shapes=[pltpu.VMEM((D_MODEL, width), F32)],
        compiler_params=_params(2),
    )(a, b)


def _place():
    x, y, c = lax.axis_index("x"), lax.axis_index("y"), lax.axis_index("c")
    other_chips = [(1 - x, y), (x, 1 - y), (1 - x, 1 - y)]
    return x, y, c, other_chips


def _chip_index(px, py):
    return 2 * px + py


ANY = pl.BlockSpec(memory_space=pl.ANY)


def _gather_weights(shards, small):
    n = len(shards)
    halves = [s.shape[0] // 2 for s in shards]

    def body(*refs):
        ins, small_in = refs[:n], refs[n]
        outs, small_out = refs[n + 1:2 * n + 1], refs[2 * n + 1]
        send, recv, local_sem, small_send, small_recv = refs[2 * n + 2:]
        x, y, c, chips = _place()
        me, sibling = (x, y, c), (x, y, 1 - c)

        def block(w, px, py, pc):
            return outs[w].at[_chip_index(px, py), pl.ds(pc * halves[w], halves[w]), :]

        def copy(w, k, blk, to, src=None):
            return pltpu.make_async_remote_copy(
                src_ref=block(w, *blk) if src is None else src, dst_ref=block(w, *blk),
                send_sem=send.at[7 * w + k], recv_sem=recv.at[7 * w + k], device_id=to, device_id_type=MESH)

        def small_copy(j, chip_from, to):
            return pltpu.make_async_remote_copy(
                src_ref=small_in, dst_ref=small_out.at[_chip_index(*chip_from)],
                send_sem=small_send.at[j], recv_sem=small_recv.at[j], device_id=to, device_id_type=MESH)

        local, first = [], []
        for w in range(n):
            src = ins[w].at[pl.ds(c * halves[w], halves[w]), :]
            cp = pltpu.make_async_copy(src, block(w, *me), local_sem.at[w])
            cp.start()
            local.append(cp)
            mine = [copy(w, 0, me, sibling, src)]
            mine += [copy(w, 1 + j, me, (*chip, c), src) for j, chip in enumerate(chips)]
            for cp in mine:
                cp.start()
            first += mine
        cp = pltpu.make_async_copy(small_in, small_out.at[_chip_index(x, y)], local_sem.at[n])
        cp.start()
        local.append(cp)
        small_sends = [small_copy(j, (x, y), (*chip, c)) for j, chip in enumerate(chips)]
        for cp in small_sends:
            cp.start()
        passed = []
        for w in range(n):
            for j, chip in enumerate(chips):
                copy(w, 1 + j, (*chip, c), me).wait_recv()
                cp = copy(w, 4 + j, (*chip, c), sibling)
                cp.start()
                passed.append(cp)
        for w in range(n):
            copy(w, 0, sibling, me).wait_recv()
            for j, chip in enumerate(chips):
                copy(w, 4 + j, (*chip, 1 - c), me).wait_recv()
        for j, chip in enumerate(chips):
            small_copy(j, chip, me).wait_recv()
        for cp in first + passed + small_sends:
            cp.wait_send()
        for cp in local:
            cp.wait()

    out_shape = [jax.ShapeDtypeStruct((N_CHIPS,) + s.shape, s.dtype) for s in shards]
    out_shape.append(jax.ShapeDtypeStruct((N_CHIPS,) + small.shape, small.dtype))
    out = pl.pallas_call(
        body, name="gather_weights", in_specs=[ANY] * (n + 1), out_specs=[ANY] * (n + 1), out_shape=out_shape,
        scratch_shapes=[pltpu.SemaphoreType.DMA((7 * n,)), pltpu.SemaphoreType.DMA((7 * n,)),
                        pltpu.SemaphoreType.DMA((n + 1,)), pltpu.SemaphoreType.DMA((3,)),
                        pltpu.SemaphoreType.DMA((3,))],
    )(*shards, small)
    return out[:n], out[n]


def _send_sibling_halves(grads):
    n = len(grads)
    halves = [g.shape[1] // 2 for g in grads]

    def body(*refs):
        ins, outs = refs[:n], refs[n:2 * n]
        send, recv = refs[2 * n:]
        x, y, c, _ = _place()
        copies = []
        for w in range(n):
            cp = pltpu.make_async_remote_copy(
                src_ref=ins[w].at[:, pl.ds((1 - c) * halves[w], halves[w]), :], dst_ref=outs[w],
                send_sem=send.at[w], recv_sem=recv.at[w], device_id=(x, y, 1 - c), device_id_type=MESH)
            cp.start()
            copies.append(cp)
        for cp in copies:
            cp.wait_recv()
        for cp in copies:
            cp.wait_send()

    return pl.pallas_call(
        body, name="send_sibling_halves", in_specs=[ANY] * n, out_specs=[ANY] * n,
        out_shape=[jax.ShapeDtypeStruct((N_CHIPS, h, g.shape[2]), g.dtype) for g, h in zip(grads, halves)],
        scratch_shapes=[pltpu.SemaphoreType.DMA((n,)), pltpu.SemaphoreType.DMA((n,))],
    )(*grads)


def _row_block(rows):
    return min(rows, 256)


def _pair_add(name, core, mine, theirs):
    _, _, h, cols = mine.shape
    rb = _row_block(h)

    def body(core_ref, a_ref, b_ref, o_ref):
        del core_ref
        o_ref[...] = (a_ref[...].astype(F32) + b_ref[...].astype(F32)).astype(BF16)

    return pl.pallas_call(
        body, name=name,
        grid_spec=pltpu.PrefetchScalarGridSpec(
            num_scalar_prefetch=1, grid=(N_CHIPS, h // rb),
            in_specs=[pl.BlockSpec((None, None, rb, cols), lambda k, r, core_ref: (k, core_ref[0], r, 0)),
                      pl.BlockSpec((None, rb, cols), lambda k, r, core_ref: (k, r, 0))],
            out_specs=pl.BlockSpec((None, rb, cols), lambda k, r, core_ref: (k, r, 0))),
        out_shape=jax.ShapeDtypeStruct(theirs.shape, BF16),
        compiler_params=_params(2),
    )(core, mine, theirs)


def _exchange_chip_sums(sums):
    n = len(sums)

    def body(*refs):
        ins, outs = refs[:n], refs[n:2 * n]
        send, recv, local_sem = refs[2 * n:]
        x, y, c, chips = _place()
        k_me = _chip_index(x, y)
        local, sends = [], []
        for w in range(n):
            cp = pltpu.make_async_copy(ins[w].at[k_me], outs[w].at[k_me], local_sem.at[w])
            cp.start()
            local.append(cp)
            for j, chip in enumerate(chips):
                cp = pltpu.make_async_remote_copy(
                    src_ref=ins[w].at[_chip_index(*chip)], dst_ref=outs[w].at[k_me],
                    send_sem=send.at[3 * w + j], recv_sem=recv.at[3 * w + j], device_id=(*chip, c),
                    device_id_type=MESH)
                cp.start()
                sends.append(cp)
        for w in range(n):
            for j, chip in enumerate(chips):
                pltpu.make_async_remote_copy(
                    src_ref=ins[w].at[k_me], dst_ref=outs[w].at[_chip_index(*chip)],
                    send_sem=send.at[3 * w + j], recv_sem=recv.at[3 * w + j], device_id=(*chip, c),
                    device_id_type=MESH).wait_recv()
        for cp in sends:
            cp.wait_send()
        for cp in local:
            cp.wait()

    return pl.pallas_call(
        body, name="exchange_chip_sums", in_specs=[ANY] * n, out_specs=[ANY] * n,
        out_shape=[jax.ShapeDtypeStruct(s.shape, s.dtype) for s in sums],
        scratch_shapes=[pltpu.SemaphoreType.DMA((3 * n,)), pltpu.SemaphoreType.DMA((3 * n,)),
                        pltpu.SemaphoreType.DMA((n,))],
    )(*sums)


def _chip_sum(name, parts):
    _, h, cols = parts.shape
    rb = _row_block(h)

    def body(p_ref, o_ref):
        acc = p_ref[0].astype(F32)
        for k in range(1, N_CHIPS):
            acc = acc + p_ref[k].astype(F32)
        o_ref[...] = acc

    return pl.pallas_call(
        body, name=name, grid=(h // rb,),
        in_specs=[pl.BlockSpec((N_CHIPS, rb, cols), lambda r: (0, r, 0))],
        out_specs=pl.BlockSpec((rb, cols), lambda r: (r, 0)),
        out_shape=jax.ShapeDtypeStruct((h, cols), F32),
        compiler_params=_params(),
    )(parts)


def _share_with_sibling(halves):
    n = len(halves)

    def body(*refs):
        ins, outs = refs[:n], refs[n:2 * n]
        send, recv, local_sem = refs[2 * n:]
        x, y, c, _ = _place()
        local, sends = [], []
        for w in range(n):
            cp = pltpu.make_async_copy(ins[w], outs[w].at[c], local_sem.at[w])
            cp.start()
            local.append(cp)
            cp = pltpu.make_async_remote_copy(
                src_ref=ins[w], dst_ref=outs[w].at[c], send_sem=send.at[w], recv_sem=recv.at[w],
                device_id=(x, y, 1 - c), device_id_type=MESH)
            cp.start()
            sends.append(cp)
        for w in range(n):
            pltpu.make_async_remote_copy(
                src_ref=ins[w], dst_ref=outs[w].at[1 - c], send_sem=send.at[w], recv_sem=recv.at[w],
                device_id=(x, y, 1 - c), device_id_type=MESH).wait_recv()
        for cp in sends:
            cp.wait_send()
        for cp in local:
            cp.wait()

    return pl.pallas_call(
        body, name="share_with_sibling", in_specs=[ANY] * n, out_specs=[ANY] * n,
        out_shape=[jax.ShapeDtypeStruct((2,) + s.shape, s.dtype) for s in halves],
        scratch_shapes=[pltpu.SemaphoreType.DMA((n,)), pltpu.SemaphoreType.DMA((n,)),
                        pltpu.SemaphoreType.DMA((n,))],
    )(*halves)


SMALL_ROWS = 24
ROW_G1, ROW_CW, ROW_CB, ROW_BR, ROW_BI, ROW_LAM, ROW_LG, ROW_LB, ROW_G2, ROW_G3, ROW_LOSS, ROW_BS = (
    0, 1, 5, 6, 7, 8, 9, 10, 11, 12, 13, 16)
N_DEV = 8


def _all_reduce_small(dg1, dcw, dcb, dbr, dbi, dlam, dlg, dlb, dg2, dg3, loss, dbs, dws):
    def body(dg1_ref, dcw_ref, dcb_ref, dbr_ref, dbi_ref, dlam_ref, dlg_ref, dlb_ref, dg2_ref, dg3_ref, loss_ref,
             dbs_ref, dws_ref, vec_out, ws_out, vec_all, ws_all, send, recv):
        x, y, c, chips = _place()
        me, sibling = (x, y, c), (x, y, 1 - c)

        def dev(px, py, pc):
            return 4 * px + 2 * py + pc

        mine = vec_all.at[dev(*me)]
        mine[...] = jnp.zeros((SMALL_ROWS, D_MODEL), F32)
        for row, ref in ((ROW_G1, dg1_ref), (ROW_CB, dcb_ref), (ROW_BR, dbr_ref), (ROW_BI, dbi_ref),
                         (ROW_LAM, dlam_ref), (ROW_LG, dlg_ref), (ROW_LB, dlb_ref), (ROW_G2, dg2_ref),
                         (ROW_G3, dg3_ref)):
            mine[row:row + 1, :] = ref[...]
        mine[ROW_CW:ROW_CW + CONV_WIDTH, :] = dcw_ref[0:CONV_WIDTH, :]
        mine[ROW_LOSS:ROW_LOSS + 1, 0:128] = loss_ref[0:1, :]
        mine[ROW_BS:ROW_BS + GROUPS, 0:128] = jnp.transpose(dbs_ref[...])[0:GROUPS, :]
        ws_all.at[dev(*me)][...] = dws_ref[...]

        def copies(k, blk, to):
            d = dev(*blk)
            return [pltpu.make_async_remote_copy(
                src_ref=buf.at[d], dst_ref=buf.at[d], send_sem=send.at[2 * k + q], recv_sem=recv.at[2 * k + q],
                device_id=to, device_id_type=MESH) for q, buf in enumerate((vec_all, ws_all))]

        first = copies(0, me, sibling)
        for j, chip in enumerate(chips):
            first += copies(1 + j, me, (*chip, c))
        for cp in first:
            cp.start()
        passed = []
        for j, chip in enumerate(chips):
            for cp in copies(1 + j, (*chip, c), me):
                cp.wait_recv()
            fwd = copies(4 + j, (*chip, c), sibling)
            for cp in fwd:
                cp.start()
            passed += fwd
        for cp in copies(0, sibling, me):
            cp.wait_recv()
        for j, chip in enumerate(chips):
            for cp in copies(4 + j, (*chip, 1 - c), me):
                cp.wait_recv()
        for cp in first + passed:
            cp.wait_send()
        vec = vec_all[0]
        ws = ws_all[0]
        for d in range(1, N_DEV):
            vec = vec + vec_all[d]
            ws = ws + ws_all[d]
        vec_out[...] = vec
        ws_out[...] = ws

    vm = pl.BlockSpec(memory_space=pltpu.VMEM)
    return pl.pallas_call(
        body, name="all_reduce_small", in_specs=[vm] * 13, out_specs=[vm, vm],
        out_shape=[jax.ShapeDtypeStruct((SMALL_ROWS, D_MODEL), F32),
                   jax.ShapeDtypeStruct((CHUNK, GROUPS * CHUNK), F32)],
        scratch_shapes=[pltpu.VMEM((N_DEV, SMALL_ROWS, D_MODEL), F32), pltpu.VMEM((N_DEV, CHUNK, GROUPS * CHUNK), F32),
                        pltpu.SemaphoreType.DMA((14,)), pltpu.SemaphoreType.DMA((14,))],
        compiler_params=pltpu.CompilerParams(vmem_limit_bytes=VMEM_LIMIT_BYTES),
    )(dg1, dcw, dcb, dbr, dbi, dlam, dlg, dlb, dg2, dg3, loss, dbs, dws)


def _adamw_math(w, g, m, v):
    m = ADAM_B1 * m + (1.0 - ADAM_B1) * g
    v = ADAM_B2 * v + (1.0 - ADAM_B2) * (g * g)
    m_hat = m / (1.0 - ADAM_B1 ** ADAM_STEP)
    v_hat = v / (1.0 - ADAM_B2 ** ADAM_STEP)
    delta = (-ADAM_LR) * (m_hat / (jnp.sqrt(v_hat) + ADAM_EPS) + ADAM_WD * w)
    return delta, m, v


def _adamw(name, g, w, m, v):
    rows, cols = w.shape
    rb = _row_block(rows)

    def body(g_ref, w_ref, m_ref, v_ref, d_ref, nm_ref, nv_ref):
        d_ref[...], nm_ref[...], nv_ref[...] = _adamw_math(w_ref[...], g_ref[...], m_ref[...], v_ref[...])

    blk = pl.BlockSpec((rb, cols), lambda r: (r, 0))
    return pl.pallas_call(
        body, name=name, grid=(rows // rb,), in_specs=[blk] * 4, out_specs=[blk] * 3,
        out_shape=[jax.ShapeDtypeStruct(w.shape, F32)] * 3, compiler_params=_params(),
    )(g, w, m, v)


def _adamw_small(grads, ws, ms, vs):
    n = len(grads)

    def body(*refs):
        g_refs, w_refs, m_refs, v_refs = refs[:n], refs[n:2 * n], refs[2 * n:3 * n], refs[3 * n:4 * n]
        outs = refs[4 * n:]
        for p in range(n):
            d, nm, nv = _adamw_math(w_refs[p][...], g_refs[p][...], m_refs[p][...], v_refs[p][...])
            outs[p][...] = d
            outs[n + p][...] = nm
            outs[2 * n + p][...] = nv

    vm = pl.BlockSpec(memory_space=pltpu.VMEM)
    shapes = [jax.ShapeDtypeStruct(w.shape, F32) for w in ws]
    out = pl.pallas_call(
        body, name="adamw_small", in_specs=[vm] * (4 * n), out_specs=[vm] * (3 * n), out_shape=shapes * 3,
    )(*grads, *ws, *ms, *vs)
    return out[:n], out[n:2 * n], out[2 * n:]


def _unstack_heads(w_st):
    per = HEAD_DIM // N_CHIPS
    return w_st.reshape(N_CHIPS, HEADS, per, HEAD_DIM).transpose(1, 0, 2, 3).reshape(HEADS, HEAD_DIM, HEAD_DIM)


def _stack_heads(w):
    per = HEAD_DIM // N_CHIPS
    return w.reshape(HEADS, N_CHIPS, per, HEAD_DIM).transpose(1, 0, 2, 3).reshape(N_CHIPS, HEADS * per, HEAD_DIM)


def _local_step(x, target, wt):
    z, n1 = _fwd_in(x, wt["g1"], wt["w_in"])
    h, ya = _fwd_lru(z, wt["conv_w"], wt["conv_b"], wt["wr"], wt["br"], wt["wi"], wt["bi"], wt["lam"])
    yb = _fwd_sgu(z, wt["ln_g"], wt["ln_b"], wt["w_s"], wt["bias_s"])
    pa, pb, h1, n2 = _fwd_merge(ya, yb, z, x, wt["w_oa"], wt["w_ob"], wt["w_out"], wt["g2"])
    up, act, dh2, dh2b, loss, dg3 = _fwd_mlp(n2, h1, target, wt["w_up"], wt["w_down"], wt["g3"])

    dup, dh1, dg2 = _bwd_mlp(dh2, dh2b, up, h1, wt["w_up"], wt["w_down"], wt["g2"])
    d_up = _weight_grad("dw_up", n2, dup, N_CHIPS, False, True, D_MODEL)
    d_down = _weight_grad("dw_down", act, dh2b, N_CHIPS, True, False, D_MODEL)
    dz, dya, dyb, merged, dpa, dpb, dh1b = _bwd_merge(dh1, pa, pb, z, wt["w_oa"], wt["w_ob"], wt["w_out"])
    d_out = _weight_grad("dw_out", merged, dh1b, 1, False, False, D_MODEL)
    d_oa = _weight_grad("dw_out_a", ya, dpa, 1, False, False, D_MODEL)
    d_ob = _weight_grad("dw_out_b", yb, dpb, 1, False, False, D_MODEL)
    dz, dlg, dlb, dws, dbs = _bwd_sgu(dz, dyb, z, wt["ln_g"], wt["ln_b"], wt["w_s"], wt["bias_s"])
    dz, dcw, dcb, dwr, dbr, dwi, dbi, dlam = _bwd_lru(
        dz, dya, z, h, wt["conv_w"], wt["conv_b"], wt["wr"], wt["br"], wt["wi"], wt["bi"], wt["lam"])
    d_in = _weight_grad("dw_in", n1, dz, N_CHIPS, False, True, IN_SHARD)
    dx, dg1 = _bwd_in(dz, x, dh1, wt["w_in"], wt["g1"])

    quarter = lambda g: g.reshape(N_CHIPS, D_MODEL // N_CHIPS, D_MODEL)
    big = [d_in, _stack_heads(dwr).astype(BF16), _stack_heads(dwi).astype(BF16), quarter(d_oa), quarter(d_ob),
           quarter(d_out), d_up, d_down]
    small = (dg1, dcw, dcb, dbr, dbi, dlam, dlg, dlb, dg2, dg3, loss, dbs, dws)
    return dx, big, small


def kernel(x, norm_mix_g, w_in, conv_w, conv_b, w_rgate, b_rgate, w_igate, b_igate, lru_lambda, w_out_a, sgu_ln_g, sgu_ln_b, sgu_w_s, sgu_b_s, w_out_b, w_out, norm_mlp_g, w_up, w_down, norm_final_g, loss_target, m_norm_mix_g, m_w_in, m_conv_w, m_conv_b, m_w_rgate, m_b_rgate, m_w_igate, m_b_igate, m_lru_lambda, m_w_out_a, m_sgu_ln_g, m_sgu_ln_b, m_sgu_w_s, m_sgu_b_s, m_w_out_b, m_w_out, m_norm_mlp_g, m_w_up, m_w_down, m_norm_final_g, v_norm_mix_g, v_w_in, v_conv_w, v_conv_b, v_w_rgate, v_b_rgate, v_w_igate, v_b_igate, v_lru_lambda, v_w_out_a, v_sgu_ln_g, v_sgu_ln_b, v_sgu_w_s, v_sgu_b_s, v_w_out_b, v_w_out, v_norm_mlp_g, v_w_up, v_w_down, v_norm_final_g):
    chip = _chip_index(lax.axis_index("x"), lax.axis_index("y"))
    core = lax.axis_index("c")
    quarter_h = HEAD_DIM // N_CHIPS
    quarter_d = D_MODEL // N_CHIPS

    as_2d = lambda a: a.reshape(-1, a.shape[-1])
    big_w = [as_2d(w) for w in (w_in, w_rgate, w_igate, w_out_a, w_out_b, w_out, w_up, w_down)]
    big_m = [as_2d(w) for w in (m_w_in, m_w_rgate, m_w_igate, m_w_out_a, m_w_out_b, m_w_out, m_w_up, m_w_down)]
    big_v = [as_2d(w) for w in (v_w_in, v_w_rgate, v_w_igate, v_w_out_a, v_w_out_b, v_w_out, v_w_up, v_w_down)]

    packed = jnp.concatenate([conv_w[0], b_rgate[0], b_igate[0]], axis=1)
    packed = jnp.concatenate([packed, jnp.zeros_like(packed)], axis=0)
    stacked, packed_all = _gather_weights([w.astype(BF16) for w in big_w], packed)
    w_in_st, wr_st, wi_st, w_oa_st, w_ob_st, w_out_st, w_up_st, w_down_st = stacked
    pick = lambda lo, hi: packed_all[:, :HEADS, lo:hi].transpose(1, 0, 2).reshape(HEADS, -1)
    conv_w_full = pick(0, quarter_d)
    br_full = pick(quarter_d, quarter_d + quarter_h).reshape(1, D_MODEL)
    bi_full = pick(quarter_d + quarter_h, quarter_d + 2 * quarter_h).reshape(1, D_MODEL)
    bias_s = jnp.broadcast_to(jnp.transpose(sgu_b_s[0])[:, :, None], (CHUNK, GROUPS, GROUP_DIM)).reshape(CHUNK, D_MODEL)
    wt = dict(
        g1=norm_mix_g, w_in=w_in_st, conv_w=conv_w_full, conv_b=conv_b, wr=_unstack_heads(wr_st), br=br_full,
        wi=_unstack_heads(wi_st), bi=bi_full, lam=lru_lambda, w_oa=w_oa_st.reshape(D_MODEL, D_MODEL), ln_g=sgu_ln_g,
        ln_b=sgu_ln_b, w_s=sgu_w_s[0], bias_s=bias_s, w_ob=w_ob_st.reshape(D_MODEL, D_MODEL),
        w_out=w_out_st.reshape(D_MODEL, D_MODEL), g2=norm_mlp_g, w_up=w_up_st, w_down=w_down_st.reshape(D_FF, D_MODEL),
        g3=norm_final_g.reshape(1, D_MODEL))

    grad_x, big_g, small_g = _local_step(x[0], loss_target[0], wt)

    names = ("w_in", "w_rgate", "w_igate", "w_out_a", "w_out_b", "w_out", "w_up", "w_down")
    from_sibling = _send_sibling_halves(big_g)
    core_arr = core.reshape(1).astype(jnp.int32)
    pair = [_pair_add("pair_add_" + nm, core_arr, g.reshape(N_CHIPS, 2, g.shape[1] // 2, g.shape[2]), s)
            for nm, g, s in zip(names, big_g, from_sibling)]
    parts = _exchange_chip_sums(pair)
    halves = [_chip_sum("chip_sum_" + nm, p) for nm, p in zip(names, parts)]
    full = [f.reshape(w.shape) for f, w in zip(_share_with_sibling(halves), big_w)]
    big_out = [_adamw("adamw_" + nm, g, w, m, v) for nm, g, w, m, v in zip(names, full, big_w, big_m, big_v)]

    vec, ws_sum = _all_reduce_small(*small_g)
    row = lambda r: vec[r:r + 1]
    shard = lambda a, width: lax.dynamic_slice_in_dim(a, chip * width, width, axis=1)
    g_small = dict(
        norm_mix_g=row(ROW_G1), conv_w=shard(vec[ROW_CW:ROW_CW + CONV_WIDTH], quarter_d), conv_b=row(ROW_CB),
        b_rgate=shard(row(ROW_BR).reshape(HEADS, HEAD_DIM), quarter_h),
        b_igate=shard(row(ROW_BI).reshape(HEADS, HEAD_DIM), quarter_h), lru_lambda=row(ROW_LAM),
        sgu_ln_g=row(ROW_LG), sgu_ln_b=row(ROW_LB),
        sgu_w_s=ws_sum.reshape(CHUNK, GROUPS, CHUNK).transpose(1, 0, 2).reshape(GROUPS * CHUNK, CHUNK),
        sgu_b_s=vec[ROW_BS:ROW_BS + GROUPS, 0:CHUNK], norm_mlp_g=row(ROW_G2), norm_final_g=row(ROW_G3))
    loss = vec[ROW_LOSS, 0]
    small_names = list(g_small)
    given = dict(
        norm_mix_g=(norm_mix_g, m_norm_mix_g, v_norm_mix_g), conv_w=(conv_w, m_conv_w, v_conv_w),
        conv_b=(conv_b, m_conv_b, v_conv_b), b_rgate=(b_rgate, m_b_rgate, v_b_rgate),
        b_igate=(b_igate, m_b_igate, v_b_igate), lru_lambda=(lru_lambda, m_lru_lambda, v_lru_lambda),
        sgu_ln_g=(sgu_ln_g, m_sgu_ln_g, v_sgu_ln_g), sgu_ln_b=(sgu_ln_b, m_sgu_ln_b, v_sgu_ln_b),
        sgu_w_s=(sgu_w_s, m_sgu_w_s, v_sgu_w_s), sgu_b_s=(sgu_b_s, m_sgu_b_s, v_sgu_b_s),
        norm_mlp_g=(norm_mlp_g, m_norm_mlp_g, v_norm_mlp_g), norm_final_g=(norm_final_g, m_norm_final_g, v_norm_final_g))
    g2d = [g_small[nm] for nm in small_names]
    to2d = lambda a, g: a.reshape(g.shape)
    d_s, m_s, v_s = _adamw_small(
        g2d, *[[to2d(given[nm][q], g) for nm, g in zip(small_names, g2d)] for q in range(3)])

    shapes = dict(
        norm_mix_g=norm_mix_g, w_in=w_in, conv_w=conv_w, conv_b=conv_b, w_rgate=w_rgate, b_rgate=b_rgate,
        w_igate=w_igate, b_igate=b_igate, lru_lambda=lru_lambda, w_out_a=w_out_a, sgu_ln_g=sgu_ln_g,
        sgu_ln_b=sgu_ln_b, sgu_w_s=sgu_w_s, sgu_b_s=sgu_b_s, w_out_b=w_out_b, w_out=w_out, norm_mlp_g=norm_mlp_g,
        w_up=w_up, w_down=w_down, norm_final_g=norm_final_g)
    grads, deltas, new_m, new_v = {}, {}, {}, {}
    for nm, g, (d, nmom, nvar) in zip(names, full, big_out):
        grads[nm], deltas[nm], new_m[nm], new_v[nm] = g, d, nmom, nvar
    for p, nm in enumerate(small_names):
        grads[nm], deltas[nm], new_m[nm], new_v[nm] = g2d[p], d_s[p], m_s[p], v_s[p]
    order = list(shapes)
    out = [loss, grad_x[None]]
    for group in (grads, deltas, new_m, new_v):
        out += [group[nm].reshape(shapes[nm].shape) for nm in order]
    return tuple(out)
```

```python
import functools

import jax
import jax.numpy as jnp
from jax import lax
from jax.experimental import pallas as pl
from jax.experimental.pallas import tpu as pltpu

F32 = jnp.float32
BF16 = jnp.bfloat16
MESH = pl.DeviceIdType.MESH

D_MODEL = 1024
D_IN = 6 * D_MODEL
D_FF = 4 * D_MODEL
N_CHIPS = 4
IN_SHARD = D_IN // N_CHIPS
HEADS = 4
HEAD_DIM = D_MODEL // HEADS
GROUPS = 4
GROUP_DIM = D_MODEL // GROUPS
CHUNK = 128
CONV_WIDTH = 4
LRU_C = 8.0
NORM_EPS = 1e-6
LN_EPS = 1e-5

ADAM_LR = 0.001
ADAM_B1 = 0.9
ADAM_B2 = 0.999
ADAM_EPS = 1e-08
ADAM_WD = 0.01
ADAM_STEP = 10

SUBLANES = 8
TOKEN_TILE = 256
DW_TOKEN_TILE = 512
VMEM_LIMIT_BYTES = 56 * 1024 * 1024

GELU_K0 = 0.7978845608028654
GELU_K1 = 0.044715


def _params(n_grid_axes=1):
    return pltpu.CompilerParams(
        dimension_semantics=("arbitrary",) * n_grid_axes, vmem_limit_bytes=VMEM_LIMIT_BYTES)


def _resident(shape):
    nd = len(shape)
    return pl.BlockSpec(shape, lambda *_: (0,) * nd, pipeline_mode=pl.Buffered(1))


def _const(shape):
    nd = len(shape)
    return pl.BlockSpec(shape, lambda *_: (0,) * nd)


def _dot(a, b):
    return jnp.dot(a, b, preferred_element_type=F32)


def _dot_nt(a, b):
    return lax.dot_general(a, b, (((1,), (1,)), ((), ())), preferred_element_type=F32)


def _dot_tn(a, b):
    return lax.dot_general(a, b, (((0,), (0,)), ((), ())), preferred_element_type=F32)


def _gelu(x):
    t = jnp.tanh(GELU_K0 * x * (1.0 + GELU_K1 * x * x))
    return 0.5 * x * (1.0 + t)


def _gelu_and_grad(x):
    x2 = x * x
    t = jnp.tanh(GELU_K0 * x * (1.0 + GELU_K1 * x2))
    g = 0.5 * x * (1.0 + t)
    dg = 0.5 * (1.0 + t) + 0.5 * x * (1.0 - t * t) * (GELU_K0 * (1.0 + 3.0 * GELU_K1 * x2))
    return g, dg


def _rms(x):
    r = lax.rsqrt(jnp.mean(x * x, axis=-1, keepdims=True) + NORM_EPS)
    return x * r, r


def _rms_bwd(dn, xhat, r):
    return r * (dn - xhat * jnp.mean(dn * xhat, axis=-1, keepdims=True))


def _col_sum(v):
    return jnp.sum(v, axis=0, keepdims=True)


def _shift_down(x, tail8, k):
    xs = pltpu.roll(x, k, 0)
    ts = pltpu.roll(tail8, k, 0)
    ridx = lax.broadcasted_iota(jnp.int32, tail8.shape, 0)
    head = jnp.where(ridx < k, ts, xs[0:SUBLANES])
    return jnp.concatenate([head, xs[SUBLANES:]], axis=0)


def _shift_up(x, head8, k):
    n = x.shape[0]
    xs = pltpu.roll(x, n - k, 0)
    hs = pltpu.roll(head8, SUBLANES - k, 0)
    ridx = lax.broadcasted_iota(jnp.int32, head8.shape, 0)
    last = jnp.where(ridx >= SUBLANES - k, hs, xs[n - SUBLANES:n])
    return jnp.concatenate([xs[:n - SUBLANES], last], axis=0)


def _scan_forward(a, b, carry):
    n = a.shape[0]
    sub = lax.broadcasted_iota(jnp.int32, a.shape, 0) & (SUBLANES - 1)
    for s in (1, 2, 4):
        a_s = pltpu.roll(a, s, 0)
        b_s = pltpu.roll(b, s, 0)
        m = sub >= s
        b = jnp.where(m, a * b_s + b, b)
        a = jnp.where(m, a * a_s, a)
    out = []
    for g in range(n // SUBLANES):
        rows = slice(g * SUBLANES, (g + 1) * SUBLANES)
        h = a[rows] * carry + b[rows]
        out.append(h)
        carry = h[SUBLANES - 1:SUBLANES]
    return jnp.concatenate(out, axis=0), carry


def _scan_backward(a, b, carry):
    n = a.shape[0]
    sub = lax.broadcasted_iota(jnp.int32, a.shape, 0) & (SUBLANES - 1)
    for s in (1, 2, 4):
        a_s = pltpu.roll(a, n - s, 0)
        b_s = pltpu.roll(b, n - s, 0)
        m = sub < SUBLANES - s
        b = jnp.where(m, a * b_s + b, b)
        a = jnp.where(m, a * a_s, a)
    out = [None] * (n // SUBLANES)
    for g in reversed(range(n // SUBLANES)):
        rows = slice(g * SUBLANES, (g + 1) * SUBLANES)
        h = a[rows] * carry + b[rows]
        out[g] = h
        carry = h[0:1]
    return jnp.concatenate(out, axis=0), carry


def _softplus_neg(lam):
    e = jnp.exp(-jnp.abs(lam))
    u = 1.0 + e
    log1p_e = jnp.where(u == 1.0, e, jnp.log(u) * (e / jnp.where(u == 1.0, 1.0, u - 1.0)))
    return jnp.maximum(-lam, 0.0) + log1p_e


def _one_minus_exp_neg(x):
    poly = x * (1.0 - 0.5 * x * (1.0 - (1.0 / 3.0) * x * (1.0 - 0.25 * x * (1.0 - 0.2 * x))))
    return jnp.where(x < 0.02, poly, 1.0 - jnp.exp(-x))


def _lru_gates(xa, tail8, cw_ref, cb_ref, wr_ref, br_ref, wi_ref, bi_ref, lam_ref):
    cw = cw_ref[...]
    xs = [xa] + [_shift_down(xa, tail8, k) for k in range(1, CONV_WIDTH)]
    xc = cb_ref[...] + cw[0:1] * xs[0]
    for k in range(1, CONV_WIDTH):
        xc = xc + cw[k:k + 1] * xs[k]
    xcb = xc.astype(BF16)
    pre_r, pre_i = [], []
    for h in range(HEADS):
        cols = slice(h * HEAD_DIM, (h + 1) * HEAD_DIM)
        pre_r.append(_dot(xcb[:, cols], wr_ref[h]))
        pre_i.append(_dot(xcb[:, cols], wi_ref[h]))
    r = jax.nn.sigmoid(jnp.concatenate(pre_r, axis=1) + br_ref[...])
    ig = jax.nn.sigmoid(jnp.concatenate(pre_i, axis=1) + bi_ref[...])
    sp = _softplus_neg(lam_ref[...])
    log_a = (-LRU_C) * sp * r
    a = jnp.exp(log_a)
    mult = jnp.sqrt(_one_minus_exp_neg(-2.0 * log_a))
    return xs, xc, xcb, r, ig, sp, a, mult


def _fwd_in(x, g1, w_in_st):
    t = x.shape[0]

    def body(x_ref, g_ref, w_ref, z_ref, n_ref):
        xhat, _ = _rms(x_ref[...])
        n = (xhat * g_ref[...]).astype(BF16)
        n_ref[...] = n
        for k in range(N_CHIPS):
            z_ref[:, k * IN_SHARD:(k + 1) * IN_SHARD] = _dot(n, w_ref[k])

    return pl.pallas_call(
        body, name="fwd_in", grid=(t // TOKEN_TILE,),
        in_specs=[pl.BlockSpec((TOKEN_TILE, D_MODEL), lambda i: (i, 0)), _const((1, D_MODEL)),
                  _resident((N_CHIPS, D_MODEL, IN_SHARD))],
        out_specs=[pl.BlockSpec((TOKEN_TILE, D_IN), lambda i: (i, 0)),
                   pl.BlockSpec((TOKEN_TILE, D_MODEL), lambda i: (i, 0))],
        out_shape=[jax.ShapeDtypeStruct((t, D_IN), F32), jax.ShapeDtypeStruct((t, D_MODEL), BF16)],
        compiler_params=_params(),
    )(x, g1, w_in_st)


def _fwd_lru(z, conv_w, conv_b, wr, br, wi, bi, lam):
    t = z.shape[0]

    def body(xa_ref, ga_ref, cw_ref, cb_ref, wr_ref, br_ref, wi_ref, bi_ref, lam_ref, h_ref, ya_ref,
             tail_ref, carry_ref):
        @pl.when(pl.program_id(0) == 0)
        def _():
            tail_ref[...] = jnp.zeros_like(tail_ref)
            carry_ref[...] = jnp.zeros_like(carry_ref)

        xa = xa_ref[...]
        _, xc, _, _, ig, _, a, mult = _lru_gates(
            xa, tail_ref[...], cw_ref, cb_ref, wr_ref, br_ref, wi_ref, bi_ref, lam_ref)
        tail_ref[...] = xa[TOKEN_TILE - SUBLANES:]
        h, carry = _scan_forward(a, xc * ig * mult, carry_ref[...])
        carry_ref[...] = carry
        h_ref[...] = h
        ya_ref[...] = (h * _gelu(ga_ref[...])).astype(BF16)

    tile = lambda j: pl.BlockSpec((TOKEN_TILE, D_MODEL), lambda i: (i, j))
    return pl.pallas_call(
        body, name="fwd_lru", grid=(t // TOKEN_TILE,),
        in_specs=[tile(0), tile(1), _const((CONV_WIDTH, D_MODEL)), _const((1, D_MODEL)),
                  _resident((HEADS, HEAD_DIM, HEAD_DIM)), _const((1, D_MODEL)),
                  _resident((HEADS, HEAD_DIM, HEAD_DIM)), _const((1, D_MODEL)), _const((1, D_MODEL))],
        out_specs=[tile(0), tile(0)],
        out_shape=[jax.ShapeDtypeStruct((t, D_MODEL), F32), jax.ShapeDtypeStruct((t, D_MODEL), BF16)],
        scratch_shapes=[pltpu.VMEM((SUBLANES, D_MODEL), F32), pltpu.VMEM((1, D_MODEL), F32)],
        compiler_params=_params(),
    )(z, z, conv_w, conv_b, wr, br, wi, bi, lam)


def _sgu_forward_parts(ub, vb, lg_ref, lb_ref):
    u, du = _gelu_and_grad(ub)
    vg, dvg = _gelu_and_grad(vb)
    mu = jnp.mean(vg, axis=-1, keepdims=True)
    d = vg - mu
    rstd = lax.rsqrt(jnp.mean(d * d, axis=-1, keepdims=True) + LN_EPS)
    vhat = d * rstd
    vn = (vhat * lg_ref[...] + lb_ref[...]).astype(BF16)
    return u, du, dvg, rstd, vhat, vn


def _causal_mask():
    rows = lax.broadcasted_iota(jnp.int32, (CHUNK, CHUNK), 0)
    cols = lax.broadcasted_iota(jnp.int32, (CHUNK, CHUNK), 1)
    return rows >= cols


def _fwd_sgu(z, ln_g, ln_b, w_s, bias_full):
    t = z.shape[0]

    def body(ub_ref, vb_ref, lg_ref, lb_ref, ws_ref, bias_ref, yb_ref):
        u, _, _, _, _, vn = _sgu_forward_parts(ub_ref[...], vb_ref[...], lg_ref, lb_ref)
        mask = _causal_mask()
        wm = [jnp.where(mask, ws_ref[g], 0.0).astype(BF16) for g in range(GROUPS)]
        for c in range(TOKEN_TILE // CHUNK):
            rows = slice(c * CHUNK, (c + 1) * CHUNK)
            for g in range(GROUPS):
                cols = slice(g * GROUP_DIM, (g + 1) * GROUP_DIM)
                sp = _dot(wm[g], vn[rows, cols]) + bias_ref[:, cols]
                yb_ref[rows, cols] = (u[rows, cols] * sp).astype(BF16)

    tile = lambda j: pl.BlockSpec((TOKEN_TILE, D_MODEL), lambda i: (i, j))
    return pl.pallas_call(
        body, name="fwd_sgu", grid=(t // TOKEN_TILE,),
        in_specs=[tile(2), tile(3), _const((1, D_MODEL)), _const((1, D_MODEL)),
                  _const((GROUPS, CHUNK, CHUNK)), _const((CHUNK, D_MODEL))],
        out_specs=tile(0),
        out_shape=jax.ShapeDtypeStruct((t, D_MODEL), BF16),
        compiler_params=_params(),
    )(z, z, ln_g, ln_b, w_s, bias_full)


def _fwd_merge(ya, yb, z, x, w_oa, w_ob, w_out, g2):
    t = x.shape[0]

    def body(ya_ref, yb_ref, m_ref, x_ref, woa_ref, wob_ref, wout_ref, g_ref, pa_ref, pb_ref, h1_ref, n2_ref):
        pa = _dot(ya_ref[...], woa_ref[...])
        pb = _dot(yb_ref[...], wob_ref[...])
        pa_ref[...] = pa
        pb_ref[...] = pb
        merged = jax.nn.sigmoid(m_ref[:, :D_MODEL]) * pa + jax.nn.sigmoid(m_ref[:, D_MODEL:]) * pb
        h1 = x_ref[...] + _dot(merged.astype(BF16), wout_ref[...])
        h1_ref[...] = h1
        xhat, _ = _rms(h1)
        n2_ref[...] = (xhat * g_ref[...]).astype(BF16)

    tile = pl.BlockSpec((TOKEN_TILE, D_MODEL), lambda i: (i, 0))
    sq = _resident((D_MODEL, D_MODEL))
    return pl.pallas_call(
        body, name="fwd_merge", grid=(t // TOKEN_TILE,),
        in_specs=[tile, tile, pl.BlockSpec((TOKEN_TILE, 2 * D_MODEL), lambda i: (i, 2)), tile, sq, sq, sq,
                  _const((1, D_MODEL))],
        out_specs=[tile, tile, tile, tile],
        out_shape=[jax.ShapeDtypeStruct((t, D_MODEL), F32)] * 3 + [jax.ShapeDtypeStruct((t, D_MODEL), BF16)],
        compiler_params=_params(),
    )(ya, yb, z, x, w_oa, w_ob, w_out, g2)


def _fwd_mlp(n2, h1, target, w_up_st, w_down, g3):
    t = n2.shape[0]

    def body(n2_ref, h1_ref, tgt_ref, wup_ref, wdown_ref, g_ref, up_ref, act_ref, dh2_ref, dh2b_ref, loss_ref,
             dg3_ref):
        @pl.when(pl.program_id(0) == 0)
        def _():
            loss_ref[...] = jnp.zeros_like(loss_ref)
            dg3_ref[...] = jnp.zeros_like(dg3_ref)

        n2 = n2_ref[...]
        h2 = h1_ref[...]
        for k in range(N_CHIPS):
            cols = slice(k * D_MODEL, (k + 1) * D_MODEL)
            up = _dot(n2, wup_ref[k])
            up_ref[:, cols] = up
            r = jnp.maximum(up, 0.0)
            act = (r * r).astype(BF16)
            act_ref[:, cols] = act
            h2 = h2 + _dot(act, wdown_ref[cols, :])
        xhat, r3 = _rms(h2)
        diff = xhat * g_ref[...] - tgt_ref[...]
        sq = jnp.sum(diff * diff, axis=1, keepdims=True)
        loss_ref[...] = loss_ref[...] + (0.5 / D_MODEL) * jnp.sum(sq, axis=0, keepdims=True)
        dy = diff * (1.0 / D_MODEL)
        dg3_ref[...] = dg3_ref[...] + _col_sum(dy * xhat)
        dh2 = _rms_bwd(dy * g_ref[...], xhat, r3)
        dh2_ref[...] = dh2
        dh2b_ref[...] = dh2.astype(BF16)

    tile = pl.BlockSpec((TOKEN_TILE, D_MODEL), lambda i: (i, 0))
    wide = pl.BlockSpec((TOKEN_TILE, D_FF), lambda i: (i, 0))
    return pl.pallas_call(
        body, name="fwd_mlp", grid=(t // TOKEN_TILE,),
        in_specs=[tile, tile, tile, _resident((N_CHIPS, D_MODEL, D_MODEL)), _resident((D_FF, D_MODEL)),
                  _const((1, D_MODEL))],
        out_specs=[wide, wide, tile, tile, _const((SUBLANES, 128)), _const((1, D_MODEL))],
        out_shape=[jax.ShapeDtypeStruct((t, D_FF), F32), jax.ShapeDtypeStruct((t, D_FF), BF16),
                   jax.ShapeDtypeStruct((t, D_MODEL), F32), jax.ShapeDtypeStruct((t, D_MODEL), BF16),
                   jax.ShapeDtypeStruct((SUBLANES, 128), F32), jax.ShapeDtypeStruct((1, D_MODEL), F32)],
        compiler_params=_params(),
    )(n2, h1, target, w_up_st, w_down, g3)


def _bwd_mlp(dh2, dh2b, up, h1, w_up_st, w_down, g2):
    t = dh2.shape[0]

    def body(dh2_ref, dh2b_ref, up_ref, h1_ref, wup_ref, wdown_ref, g_ref, dup_ref, dh1_ref, dg2_ref):
        @pl.when(pl.program_id(0) == 0)
        def _():
            dg2_ref[...] = jnp.zeros_like(dg2_ref)

        dh2b = dh2b_ref[...]
        dn2 = jnp.zeros((TOKEN_TILE, D_MODEL), F32)
        for k in range(N_CHIPS):
            cols = slice(k * D_MODEL, (k + 1) * D_MODEL)
            dact = _dot_nt(dh2b, wdown_ref[cols, :])
            dup = (dact * (2.0 * jnp.maximum(up_ref[:, cols], 0.0))).astype(BF16)
            dup_ref[:, cols] = dup
            dn2 = dn2 + _dot_nt(dup, wup_ref[k])
        xhat, r2 = _rms(h1_ref[...])
        dg2_ref[...] = dg2_ref[...] + _col_sum(dn2 * xhat)
        dh1_ref[...] = dh2_ref[...] + _rms_bwd(dn2 * g_ref[...], xhat, r2)

    tile = pl.BlockSpec((TOKEN_TILE, D_MODEL), lambda i: (i, 0))
    wide = pl.BlockSpec((TOKEN_TILE, D_FF), lambda i: (i, 0))
    return pl.pallas_call(
        body, name="bwd_mlp", grid=(t // TOKEN_TILE,),
        in_specs=[tile, tile, wide, tile, _resident((N_CHIPS, D_MODEL, D_MODEL)), _resident((D_FF, D_MODEL)),
                  _const((1, D_MODEL))],
        out_specs=[wide, tile, _const((1, D_MODEL))],
        out_shape=[jax.ShapeDtypeStruct((t, D_FF), BF16), jax.ShapeDtypeStruct((t, D_MODEL), F32),
                   jax.ShapeDtypeStruct((1, D_MODEL), F32)],
        compiler_params=_params(),
    )(dh2, dh2b, up, h1, w_up_st, w_down, g2)


def _bwd_merge(dh1, pa, pb, z, w_oa, w_ob, w_out):
    t = dh1.shape[0]

    def body(dh1_ref, pa_ref, pb_ref, m_ref, woa_ref, wob_ref, wout_ref, dz_ref, dya_ref, dyb_ref, mg_ref,
             dpa_ref, dpb_ref, dh1b_ref):
        dh1b = dh1_ref[...].astype(BF16)
        dh1b_ref[...] = dh1b
        dm = _dot_nt(dh1b, wout_ref[...])
        pa = pa_ref[...]
        pb = pb_ref[...]
        sa = jax.nn.sigmoid(m_ref[:, :D_MODEL])
        sb = jax.nn.sigmoid(m_ref[:, D_MODEL:])
        mg_ref[...] = (sa * pa + sb * pb).astype(BF16)
        dz_ref[:, :D_MODEL] = (dm * pa * sa * (1.0 - sa)).astype(BF16)
        dz_ref[:, D_MODEL:] = (dm * pb * sb * (1.0 - sb)).astype(BF16)
        dpa = (dm * sa).astype(BF16)
        dpb = (dm * sb).astype(BF16)
        dpa_ref[...] = dpa
        dpb_ref[...] = dpb
        dya_ref[...] = _dot_nt(dpa, woa_ref[...])
        dyb_ref[...] = _dot_nt(dpb, wob_ref[...])

    tile = pl.BlockSpec((TOKEN_TILE, D_MODEL), lambda i: (i, 0))
    pair = pl.BlockSpec((TOKEN_TILE, 2 * D_MODEL), lambda i: (i, 2))
    sq = _resident((D_MODEL, D_MODEL))
    act_bf = jax.ShapeDtypeStruct((t, D_MODEL), BF16)
    return pl.pallas_call(
        body, name="bwd_merge", grid=(t // TOKEN_TILE,),
        in_specs=[tile, tile, tile, pair, sq, sq, sq],
        out_specs=[pair, tile, tile, tile, tile, tile, tile],
        out_shape=[jax.ShapeDtypeStruct((t, D_IN), BF16), jax.ShapeDtypeStruct((t, D_MODEL), F32),
                   jax.ShapeDtypeStruct((t, D_MODEL), F32), act_bf, act_bf, act_bf, act_bf],
        compiler_params=_params(),
    )(dh1, pa, pb, z, w_oa, w_ob, w_out)


def _bwd_sgu(dz, dyb, z, ln_g, ln_b, w_s, bias_full):
    t = dyb.shape[0]
    n_tiles = t // TOKEN_TILE

    def body(dz_any, dyb_ref, ub_ref, vb_ref, lg_ref, lb_ref, ws_ref, bias_ref, dz_ref, dlg_ref, dlb_ref, dws_ref,
             dbs_ref, dvn_ref, dsp_acc):
        del dz_any
        i = pl.program_id(0)

        @pl.when(i == 0)
        def _():
            dlg_ref[...] = jnp.zeros_like(dlg_ref)
            dlb_ref[...] = jnp.zeros_like(dlb_ref)
            dws_ref[...] = jnp.zeros_like(dws_ref)
            dsp_acc[...] = jnp.zeros_like(dsp_acc)

        u, du, dvg, rstd, vhat, vn = _sgu_forward_parts(ub_ref[...], vb_ref[...], lg_ref, lb_ref)
        dyb = dyb_ref[...]
        mask = _causal_mask()
        wm = [jnp.where(mask, ws_ref[g], 0.0).astype(BF16) for g in range(GROUPS)]
        for c in range(TOKEN_TILE // CHUNK):
            rows = slice(c * CHUNK, (c + 1) * CHUNK)
            for g in range(GROUPS):
                cols = slice(g * GROUP_DIM, (g + 1) * GROUP_DIM)
                vn_blk = vn[rows, cols]
                sp = _dot(wm[g], vn_blk) + bias_ref[:, cols]
                dyb_blk = dyb[rows, cols]
                dz_ref[rows, cols] = (dyb_blk * sp * du[rows, cols]).astype(BF16)
                dsp = dyb_blk * u[rows, cols]
                dsp_acc[:, cols] = dsp_acc[:, cols] + dsp
                dspb = dsp.astype(BF16)
                dvn_ref[rows, cols] = _dot_tn(wm[g], dspb)
                wcols = slice(g * CHUNK, (g + 1) * CHUNK)
                dws_ref[:, wcols] = dws_ref[:, wcols] + jnp.where(mask, _dot_nt(dspb, vn_blk), 0.0)
        dvn = dvn_ref[...]
        dlg_ref[...] = dlg_ref[...] + _col_sum(dvn * vhat)
        dlb_ref[...] = dlb_ref[...] + _col_sum(dvn)
        dvhat = dvn * lg_ref[...]
        dvgel = rstd * (dvhat - jnp.mean(dvhat, axis=-1, keepdims=True)
                        - vhat * jnp.mean(dvhat * vhat, axis=-1, keepdims=True))
        dz_ref[:, D_MODEL:] = (dvgel * dvg).astype(BF16)

        @pl.when(i == n_tiles - 1)
        def _():
            lane = lax.broadcasted_iota(jnp.int32, (CHUNK, 128), 1)
            out = jnp.zeros((CHUNK, 128), F32)
            for g in range(GROUPS):
                s = jnp.sum(dsp_acc[:, g * GROUP_DIM:(g + 1) * GROUP_DIM], axis=1, keepdims=True)
                out = out + jnp.where(lane == g, s, 0.0)
            dbs_ref[...] = out

    tile = lambda j: pl.BlockSpec((TOKEN_TILE, D_MODEL), lambda i: (i, j))
    return pl.pallas_call(
        body, name="bwd_sgu", grid=(n_tiles,),
        in_specs=[pl.BlockSpec(memory_space=pl.ANY), tile(0), tile(2), tile(3), _const((1, D_MODEL)),
                  _const((1, D_MODEL)), _const((GROUPS, CHUNK, CHUNK)), _const((CHUNK, D_MODEL))],
        out_specs=[pl.BlockSpec((TOKEN_TILE, 2 * D_MODEL), lambda i: (i, 1)), _const((1, D_MODEL)),
                   _const((1, D_MODEL)), _const((CHUNK, GROUPS * CHUNK)), _const((CHUNK, 128))],
        out_shape=[jax.ShapeDtypeStruct((t, D_IN), BF16), jax.ShapeDtypeStruct((1, D_MODEL), F32),
                   jax.ShapeDtypeStruct((1, D_MODEL), F32), jax.ShapeDtypeStruct((CHUNK, GROUPS * CHUNK), F32),
                   jax.ShapeDtypeStruct((CHUNK, 128), F32)],
        scratch_shapes=[pltpu.VMEM((TOKEN_TILE, D_MODEL), F32), pltpu.VMEM((CHUNK, D_MODEL), F32)],
        input_output_aliases={0: 0},
        compiler_params=_params(),
    )(dz, dyb, z, z, ln_g, ln_b, w_s, bias_full)


def _bwd_lru(dz, dya, z, h, conv_w, conv_b, wr, br, wi, bi, lam):
    t = dya.shape[0]
    n_tiles = t // TOKEN_TILE
    per_tile = TOKEN_TILE // SUBLANES

    def body(dz_any, dya_ref, xa_ref, xa_prev_ref, ga_ref, h_ref, h_prev_ref, cw_ref, cb_ref, wr_ref, br_ref, wi_ref,
             bi_ref, lam_ref, dz_ref, dcw_ref, dcb_ref, dwr_ref, dbr_ref, dwi_ref, dbi_ref, dlam_ref, lam_carry,
             dxc_head):
        del dz_any
        i = pl.program_id(0)

        @pl.when(i == 0)
        def _():
            for ref in (dcw_ref, dcb_ref, dwr_ref, dbr_ref, dwi_ref, dbi_ref, dlam_ref, lam_carry, dxc_head):
                ref[...] = jnp.zeros_like(ref)

        first_tile = i == n_tiles - 1
        xa = xa_ref[...]
        tail = jnp.where(first_tile, 0.0, xa_prev_ref[...])
        h_tail = jnp.where(first_tile, 0.0, h_prev_ref[...])
        xs, xc, xcb, r, ig, sp, a, mult = _lru_gates(xa, tail, cw_ref, cb_ref, wr_ref, br_ref, wi_ref, bi_ref, lam_ref)
        h = h_ref[...]
        h_prev = _shift_down(h, h_tail, 1)
        dya = dya_ref[...]
        gg, dgg = _gelu_and_grad(ga_ref[...])
        dz_ref[:, D_MODEL:] = (dya * h * dgg).astype(BF16)
        ones = jnp.ones((SUBLANES, D_MODEL), F32)
        lam_t, lam_first = _scan_backward(_shift_up(a, ones, 1), dya * gg, lam_carry[...])
        lam_carry[...] = a[0:1] * lam_first
        dmult = lam_t * xc * ig
        dla = lam_t * h_prev * a - dmult * (a * a) / mult
        dr = dla * ((-LRU_C) * sp)
        dlam_ref[...] = dlam_ref[...] + _col_sum(dla * r) * (LRU_C * jax.nn.sigmoid(-lam_ref[...]))
        dpr = dr * r * (1.0 - r)
        dpi = lam_t * xc * mult * ig * (1.0 - ig)
        dbr_ref[...] = dbr_ref[...] + _col_sum(dpr)
        dbi_ref[...] = dbi_ref[...] + _col_sum(dpi)
        dprb = dpr.astype(BF16)
        dpib = dpi.astype(BF16)
        dxc_gate = []
        for hd in range(HEADS):
            cols = slice(hd * HEAD_DIM, (hd + 1) * HEAD_DIM)
            dxc_gate.append(_dot_nt(dprb[:, cols], wr_ref[hd]) + _dot_nt(dpib[:, cols], wi_ref[hd]))
            dwr_ref[hd] = dwr_ref[hd] + _dot_tn(xcb[:, cols], dprb[:, cols])
            dwi_ref[hd] = dwi_ref[hd] + _dot_tn(xcb[:, cols], dpib[:, cols])
        dxc = lam_t * ig * mult + jnp.concatenate(dxc_gate, axis=1)
        dcb_ref[...] = dcb_ref[...] + _col_sum(dxc)
        cw = cw_ref[...]
        head = dxc_head[...]
        dxa = cw[0:1] * dxc
        for k in range(CONV_WIDTH):
            dcw_ref[k:k + 1, :] = dcw_ref[k:k + 1, :] + _col_sum(dxc * xs[k])
            if k:
                dxa = dxa + cw[k:k + 1] * _shift_up(dxc, head, k)
        dxc_head[...] = dxc[0:SUBLANES]
        dz_ref[:, :D_MODEL] = dxa.astype(BF16)

    rev = lambda i: n_tiles - 1 - i
    tile = lambda j: pl.BlockSpec((TOKEN_TILE, D_MODEL), lambda i: (rev(i), j))
    prev8 = pl.BlockSpec((SUBLANES, D_MODEL), lambda i: (jnp.maximum(rev(i) * per_tile - 1, 0), 0))
    vec = _const((1, D_MODEL))
    gate_w = _resident((HEADS, HEAD_DIM, HEAD_DIM))
    vec_shape = jax.ShapeDtypeStruct((1, D_MODEL), F32)
    gate_shape = jax.ShapeDtypeStruct((HEADS, HEAD_DIM, HEAD_DIM), F32)
    return pl.pallas_call(
        body, name="bwd_lru", grid=(n_tiles,),
        in_specs=[pl.BlockSpec(memory_space=pl.ANY), tile(0), tile(0), prev8, tile(1), tile(0), prev8,
                  _const((CONV_WIDTH, D_MODEL)), vec, gate_w, vec, gate_w, vec, vec],
        out_specs=[pl.BlockSpec((TOKEN_TILE, 2 * D_MODEL), lambda i: (rev(i), 0)), _const((SUBLANES, D_MODEL)), vec,
                   _const((HEADS, HEAD_DIM, HEAD_DIM)), vec, _const((HEADS, HEAD_DIM, HEAD_DIM)), vec, vec],
        out_shape=[jax.ShapeDtypeStruct((t, D_IN), BF16), jax.ShapeDtypeStruct((SUBLANES, D_MODEL), F32), vec_shape,
                   gate_shape, vec_shape, gate_shape, vec_shape, vec_shape],
        scratch_shapes=[pltpu.VMEM((1, D_MODEL), F32), pltpu.VMEM((SUBLANES, D_MODEL), F32)],
        input_output_aliases={0: 0},
        compiler_params=_params(),
    )(dz, dya, z, z, z, h, h, conv_w, conv_b, wr, br, wi, bi, lam)


def _bwd_in(dz, x, dh1, w_in_st, g1):
    t = x.shape[0]

    def body(dz_ref, x_ref, dh1_ref, w_ref, g_ref, dx_ref, dg1_ref):
        @pl.when(pl.program_id(0) == 0)
        def _():
            dg1_ref[...] = jnp.zeros_like(dg1_ref)

        dn1 = jnp.zeros((TOKEN_TILE, D_MODEL), F32)
        for k in range(N_CHIPS):
            dn1 = dn1 + _dot_nt(dz_ref[:, k * IN_SHARD:(k + 1) * IN_SHARD], w_ref[k])
        xhat, r1 = _rms(x_ref[...])
        dg1_ref[...] = dg1_ref[...] + _col_sum(dn1 * xhat)
        dx_ref[...] = dh1_ref[...] + _rms_bwd(dn1 * g_ref[...], xhat, r1)

    tile = pl.BlockSpec((TOKEN_TILE, D_MODEL), lambda i: (i, 0))
    return pl.pallas_call(
        body, name="bwd_in", grid=(t // TOKEN_TILE,),
        in_specs=[pl.BlockSpec((TOKEN_TILE, D_IN), lambda i: (i, 0)), tile, tile,
                  _resident((N_CHIPS, D_MODEL, IN_SHARD)), _const((1, D_MODEL))],
        out_specs=[tile, _const((1, D_MODEL))],
        out_shape=[jax.ShapeDtypeStruct((t, D_MODEL), F32), jax.ShapeDtypeStruct((1, D_MODEL), F32)],
        compiler_params=_params(),
    )(dz, x, dh1, w_in_st, g1)


def _weight_grad(name, a, b, n_blocks, a_varies, b_varies, width):
    t = a.shape[0]
    n_t = t // DW_TOKEN_TILE

    def body(a_ref, b_ref, o_ref, acc_ref):
        s = pl.program_id(1)

        @pl.when(s == 0)
        def _():
            acc_ref[...] = jnp.zeros_like(acc_ref)

        acc_ref[...] = acc_ref[...] + _dot_tn(a_ref[...], b_ref[...])

        @pl.when(s == n_t - 1)
        def _():
            o_ref[...] = acc_ref[...].astype(BF16)

    return pl.pallas_call(
        body, name=name, grid=(n_blocks, n_t),
        in_specs=[pl.BlockSpec((DW_TOKEN_TILE, D_MODEL), (lambda j, s: (s, j)) if a_varies else (lambda j, s: (s, 0))),
                  pl.BlockSpec((DW_TOKEN_TILE, width), (lambda j, s: (s, j)) if b_varies else (lambda j, s: (s, 0)))],
        out_specs=pl.BlockSpec((None, D_MODEL, width), lambda j, s: (j, 0, 0)),
        out_shape=jax.ShapeDtypeStruct((n_blocks, D_MODEL, width), BF16),
        scratch_shapes=[pltpu.VMEM((D_MODEL, width), F32)],
        compiler_params=_params(2),
    )(a, b)


def _place():
    x, y, c = lax.axis_index("x"), lax.axis_index("y"), lax.axis_index("c")
    other_chips = [(1 - x, y), (x, 1 - y), (1 - x, 1 - y)]
    return x, y, c, other_chips


def _chip_index(px, py):
    return 2 * px + py


ANY = pl.BlockSpec(memory_space=pl.ANY)


def _gather_weights(shards, small):
    n = len(shards)
    halves = [s.shape[0] // 2 for s in shards]

    def body(*refs):
        ins, small_in = refs[:n], refs[n]
        outs, small_out = refs[n + 1:2 * n + 1], refs[2 * n + 1]
        send, recv, local_sem, small_send, small_recv = refs[2 * n + 2:]
        x, y, c, chips = _place()
        me, sibling = (x, y, c), (x, y, 1 - c)

        def block(w, px, py, pc):
            return outs[w].at[_chip_index(px, py), pl.ds(pc * halves[w], halves[w]), :]

        def copy(w, k, blk, to, src=None):
            return pltpu.make_async_remote_copy(
                src_ref=block(w, *blk) if src is None else src, dst_ref=block(w, *blk),
                send_sem=send.at[7 * w + k], recv_sem=recv.at[7 * w + k], device_id=to, device_id_type=MESH)

        def small_copy(j, chip_from, to):
            return pltpu.make_async_remote_copy(
                src_ref=small_in, dst_ref=small_out.at[_chip_index(*chip_from)],
                send_sem=small_send.at[j], recv_sem=small_recv.at[j], device_id=to, device_id_type=MESH)

        local, first = [], []
        for w in range(n):
            src = ins[w].at[pl.ds(c * halves[w], halves[w]), :]
            cp = pltpu.make_async_copy(src, block(w, *me), local_sem.at[w])
            cp.start()
            local.append(cp)
            mine = [copy(w, 0, me, sibling, src)]
            mine += [copy(w, 1 + j, me, (*chip, c), src) for j, chip in enumerate(chips)]
            for cp in mine:
                cp.start()
            first += mine
        cp = pltpu.make_async_copy(small_in, small_out.at[_chip_index(x, y)], local_sem.at[n])
        cp.start()
        local.append(cp)
        small_sends = [small_copy(j, (x, y), (*chip, c)) for j, chip in enumerate(chips)]
        for cp in small_sends:
            cp.start()
        passed = []
        for w in range(n):
            for j, chip in enumerate(chips):
                copy(w, 1 + j, (*chip, c), me).wait_recv()
                cp = copy(w, 4 + j, (*chip, c), sibling)
                cp.start()
                passed.append(cp)
        for w in range(n):
            copy(w, 0, sibling, me).wait_recv()
            for j, chip in enumerate(chips):
                copy(w, 4 + j, (*chip, 1 - c), me).wait_recv()
        for j, chip in enumerate(chips):
            small_copy(j, chip, me).wait_recv()
        for cp in first + passed + small_sends:
            cp.wait_send()
        for cp in local:
            cp.wait()

    out_shape = [jax.ShapeDtypeStruct((N_CHIPS,) + s.shape, s.dtype) for s in shards]
    out_shape.append(jax.ShapeDtypeStruct((N_CHIPS,) + small.shape, small.dtype))
    out = pl.pallas_call(
        body, name="gather_weights", in_specs=[ANY] * (n + 1), out_specs=[ANY] * (n + 1), out_shape=out_shape,
        scratch_shapes=[pltpu.SemaphoreType.DMA((7 * n,)), pltpu.SemaphoreType.DMA((7 * n,)),
                        pltpu.SemaphoreType.DMA((n + 1,)), pltpu.SemaphoreType.DMA((3,)),
                        pltpu.SemaphoreType.DMA((3,))],
    )(*shards, small)
    return out[:n], out[n]


def _send_sibling_halves(grads):
    n = len(grads)
    halves = [g.shape[1] // 2 for g in grads]

    def body(*refs):
        ins, outs = refs[:n], refs[n:2 * n]
        send, recv = refs[2 * n:]
        x, y, c, _ = _place()
        copies = []
        for w in range(n):
            cp = pltpu.make_async_remote_copy(
                src_ref=ins[w].at[:, pl.ds((1 - c) * halves[w], halves[w]), :], dst_ref=outs[w],
                send_sem=send.at[w], recv_sem=recv.at[w], device_id=(x, y, 1 - c), device_id_type=MESH)
            cp.start()
            copies.append(cp)
        for cp in copies:
            cp.wait_recv()
        for cp in copies:
            cp.wait_send()

    return pl.pallas_call(
        body, name="send_sibling_halves", in_specs=[ANY] * n, out_specs=[ANY] * n,
        out_shape=[jax.ShapeDtypeStruct((N_CHIPS, h, g.shape[2]), g.dtype) for g, h in zip(grads, halves)],
        scratch_shapes=[pltpu.SemaphoreType.DMA((n,)), pltpu.SemaphoreType.DMA((n,))],
    )(*grads)


def _row_block(rows):
    return min(rows, 256)


def _pair_add(name, core, mine, theirs):
    _, _, h, cols = mine.shape
    rb = _row_block(h)

    def body(core_ref, a_ref, b_ref, o_ref):
        del core_ref
        o_ref[...] = (a_ref[...].astype(F32) + b_ref[...].astype(F32)).astype(BF16)

    return pl.pallas_call(
        body, name=name,
        grid_spec=pltpu.PrefetchScalarGridSpec(
            num_scalar_prefetch=1, grid=(N_CHIPS, h // rb),
            in_specs=[pl.BlockSpec((None, None, rb, cols), lambda k, r, core_ref: (k, core_ref[0], r, 0)),
                      pl.BlockSpec((None, rb, cols), lambda k, r, core_ref: (k, r, 0))],
            out_specs=pl.BlockSpec((None, rb, cols), lambda k, r, core_ref: (k, r, 0))),
        out_shape=jax.ShapeDtypeStruct(theirs.shape, BF16),
        compiler_params=_params(2),
    )(core, mine, theirs)


def _exchange_chip_sums(sums):
    n = len(sums)

    def body(*refs):
        ins, outs = refs[:n], refs[n:2 * n]
        send, recv = refs[2 * n:]
        x, y, c, chips = _place()
        k_me = _chip_index(x, y)

        def copy(w, j, chip):
            return pltpu.make_async_remote_copy(
                src_ref=ins[w].at[_chip_index(*chip)], dst_ref=outs[w].at[j], send_sem=send.at[3 * w + j],
                recv_sem=recv.at[3 * w + j], device_id=(*chip, c), device_id_type=MESH)

        copies = [copy(w, j, chip) for w in range(n) for j, chip in enumerate(chips)]
        for cp in copies:
            cp.start()
        for cp in copies:
            cp.wait_recv()
        for cp in copies:
            cp.wait_send()

    return pl.pallas_call(
        body, name="exchange_chip_sums", in_specs=[ANY] * n, out_specs=[ANY] * n,
        out_shape=[jax.ShapeDtypeStruct((N_CHIPS - 1,) + s.shape[1:], s.dtype) for s in sums],
        scratch_shapes=[pltpu.SemaphoreType.DMA((3 * n,)), pltpu.SemaphoreType.DMA((3 * n,))],
    )(*sums)


def _chip_sum(name, place, mine, theirs):
    _, h, cols = mine.shape
    rb = _row_block(h)

    def body(place_ref, p_ref, q_ref, o_ref):
        del place_ref
        acc = p_ref[...].astype(F32)
        for j in range(N_CHIPS - 1):
            acc = acc + q_ref[j].astype(F32)
        o_ref[...] = acc

    return pl.pallas_call(
        body, name=name,
        grid_spec=pltpu.PrefetchScalarGridSpec(
            num_scalar_prefetch=1, grid=(h // rb,),
            in_specs=[pl.BlockSpec((None, rb, cols), lambda r, place_ref: (place_ref[0], r, 0)),
                      pl.BlockSpec((N_CHIPS - 1, rb, cols), lambda r, place_ref: (0, r, 0))],
            out_specs=pl.BlockSpec((None, rb, cols), lambda r, place_ref: (place_ref[1], r, 0))),
        out_shape=jax.ShapeDtypeStruct((2, h, cols), F32),
        compiler_params=_params(),
    )(place, mine, theirs)


def _share_with_sibling(bufs):
    n = len(bufs)

    def body(*refs):
        outs = refs[n:2 * n]
        send, recv = refs[2 * n:]
        x, y, c, _ = _place()
        copies = []
        for w in range(n):
            cp = pltpu.make_async_remote_copy(
                src_ref=outs[w].at[c], dst_ref=outs[w].at[c], send_sem=send.at[w], recv_sem=recv.at[w],
                device_id=(x, y, 1 - c), device_id_type=MESH)
            cp.start()
            copies.append(cp)
        for w in range(n):
            pltpu.make_async_remote_copy(
                src_ref=outs[w].at[c], dst_ref=outs[w].at[1 - c], send_sem=send.at[w], recv_sem=recv.at[w],
                device_id=(x, y, 1 - c), device_id_type=MESH).wait_recv()
        for cp in copies:
            cp.wait_send()

    return pl.pallas_call(
        body, name="share_with_sibling", in_specs=[ANY] * n, out_specs=[ANY] * n,
        out_shape=[jax.ShapeDtypeStruct(b.shape, b.dtype) for b in bufs],
        scratch_shapes=[pltpu.SemaphoreType.DMA((n,)), pltpu.SemaphoreType.DMA((n,))],
        input_output_aliases={w: w for w in range(n)},
    )(*bufs)


SMALL_ROWS = 24
ROW_G1, ROW_CW, ROW_CB, ROW_BR, ROW_BI, ROW_LAM, ROW_LG, ROW_LB, ROW_G2, ROW_G3, ROW_LOSS, ROW_BS = (
    0, 1, 5, 6, 7, 8, 9, 10, 11, 12, 13, 16)
N_DEV = 8


def _all_reduce_small(dg1, dcw, dcb, dbr, dbi, dlam, dlg, dlb, dg2, dg3, loss, dbs, dws):
    def body(dg1_ref, dcw_ref, dcb_ref, dbr_ref, dbi_ref, dlam_ref, dlg_ref, dlb_ref, dg2_ref, dg3_ref, loss_ref,
             dbs_ref, dws_ref, vec_out, ws_out, vec_all, ws_all, send, recv):
        x, y, c, chips = _place()
        me, sibling = (x, y, c), (x, y, 1 - c)

        def dev(px, py, pc):
            return 4 * px + 2 * py + pc

        mine = vec_all.at[dev(*me)]
        mine[...] = jnp.zeros((SMALL_ROWS, D_MODEL), F32)
        for row, ref in ((ROW_G1, dg1_ref), (ROW_CB, dcb_ref), (ROW_BR, dbr_ref), (ROW_BI, dbi_ref),
                         (ROW_LAM, dlam_ref), (ROW_LG, dlg_ref), (ROW_LB, dlb_ref), (ROW_G2, dg2_ref),
                         (ROW_G3, dg3_ref)):
            mine[row:row + 1, :] = ref[...]
        mine[ROW_CW:ROW_CW + CONV_WIDTH, :] = dcw_ref[0:CONV_WIDTH, :]
        mine[ROW_LOSS:ROW_LOSS + 1, 0:128] = loss_ref[0:1, :]
        mine[ROW_BS:ROW_BS + GROUPS, 0:128] = jnp.transpose(dbs_ref[...])[0:GROUPS, :]
        ws_all.at[dev(*me)][...] = dws_ref[...]

        def copies(k, blk, to):
            d = dev(*blk)
            return [pltpu.make_async_remote_copy(
                src_ref=buf.at[d], dst_ref=buf.at[d], send_sem=send.at[2 * k + q], recv_sem=recv.at[2 * k + q],
                device_id=to, device_id_type=MESH) for q, buf in enumerate((vec_all, ws_all))]

        first = copies(0, me, sibling)
        for j, chip in enumerate(chips):
            first += copies(1 + j, me, (*chip, c))
        for cp in first:
            cp.start()
        passed = []
        for j, chip in enumerate(chips):
            for cp in copies(1 + j, (*chip, c), me):
                cp.wait_recv()
            fwd = copies(4 + j, (*chip, c), sibling)
            for cp in fwd:
                cp.start()
            passed += fwd
        for cp in copies(0, sibling, me):
            cp.wait_recv()
        for j, chip in enumerate(chips):
            for cp in copies(4 + j, (*chip, 1 - c), me):
                cp.wait_recv()
        for cp in first + passed:
            cp.wait_send()
        vec = vec_all[0]
        ws = ws_all[0]
        for d in range(1, N_DEV):
            vec = vec + vec_all[d]
            ws = ws + ws_all[d]
        vec_out[...] = vec
        ws_out[...] = ws

    vm = pl.BlockSpec(memory_space=pltpu.VMEM)
    return pl.pallas_call(
        body, name="all_reduce_small", in_specs=[vm] * 13, out_specs=[vm, vm],
        out_shape=[jax.ShapeDtypeStruct((SMALL_ROWS, D_MODEL), F32),
                   jax.ShapeDtypeStruct((CHUNK, GROUPS * CHUNK), F32)],
        scratch_shapes=[pltpu.VMEM((N_DEV, SMALL_ROWS, D_MODEL), F32), pltpu.VMEM((N_DEV, CHUNK, GROUPS * CHUNK), F32),
                        pltpu.SemaphoreType.DMA((14,)), pltpu.SemaphoreType.DMA((14,))],
        compiler_params=pltpu.CompilerParams(vmem_limit_bytes=VMEM_LIMIT_BYTES),
    )(dg1, dcw, dcb, dbr, dbi, dlam, dlg, dlb, dg2, dg3, loss, dbs, dws)


def _adamw_math(w, g, m, v):
    m = ADAM_B1 * m + (1.0 - ADAM_B1) * g
    v = ADAM_B2 * v + (1.0 - ADAM_B2) * (g * g)
    m_hat = m / (1.0 - ADAM_B1 ** ADAM_STEP)
    v_hat = v / (1.0 - ADAM_B2 ** ADAM_STEP)
    delta = (-ADAM_LR) * (m_hat / (jnp.sqrt(v_hat) + ADAM_EPS) + ADAM_WD * w)
    return delta, m, v


def _adamw(name, g, w, m, v):
    rows, cols = w.shape
    rb = _row_block(rows)

    def body(g_ref, w_ref, m_ref, v_ref, d_ref, nm_ref, nv_ref):
        d_ref[...], nm_ref[...], nv_ref[...] = _adamw_math(w_ref[...], g_ref[...], m_ref[...], v_ref[...])

    blk = pl.BlockSpec((rb, cols), lambda r: (r, 0))
    return pl.pallas_call(
        body, name=name, grid=(rows // rb,), in_specs=[blk] * 4, out_specs=[blk] * 3,
        out_shape=[jax.ShapeDtypeStruct(w.shape, F32)] * 3, compiler_params=_params(),
    )(g, w, m, v)


def _adamw_small(grads, ws, ms, vs):
    n = len(grads)

    def body(*refs):
        g_refs, w_refs, m_refs, v_refs = refs[:n], refs[n:2 * n], refs[2 * n:3 * n], refs[3 * n:4 * n]
        outs = refs[4 * n:]
        for p in range(n):
            d, nm, nv = _adamw_math(w_refs[p][...], g_refs[p][...], m_refs[p][...], v_refs[p][...])
            outs[p][...] = d
            outs[n + p][...] = nm
            outs[2 * n + p][...] = nv

    vm = pl.BlockSpec(memory_space=pltpu.VMEM)
    shapes = [jax.ShapeDtypeStruct(w.shape, F32) for w in ws]
    out = pl.pallas_call(
        body, name="adamw_small", in_specs=[vm] * (4 * n), out_specs=[vm] * (3 * n), out_shape=shapes * 3,
    )(*grads, *ws, *ms, *vs)
    return out[:n], out[n:2 * n], out[2 * n:]


def _unstack_heads(w_st):
    per = HEAD_DIM // N_CHIPS
    return w_st.reshape(N_CHIPS, HEADS, per, HEAD_DIM).transpose(1, 0, 2, 3).reshape(HEADS, HEAD_DIM, HEAD_DIM)


def _stack_heads(w):
    per = HEAD_DIM // N_CHIPS
    return w.reshape(HEADS, N_CHIPS, per, HEAD_DIM).transpose(1, 0, 2, 3).reshape(N_CHIPS, HEADS * per, HEAD_DIM)


def _local_step(x, target, wt):
    z, n1 = _fwd_in(x, wt["g1"], wt["w_in"])
    h, ya = _fwd_lru(z, wt["conv_w"], wt["conv_b"], wt["wr"], wt["br"], wt["wi"], wt["bi"], wt["lam"])
    yb = _fwd_sgu(z, wt["ln_g"], wt["ln_b"], wt["w_s"], wt["bias_s"])
    pa, pb, h1, n2 = _fwd_merge(ya, yb, z, x, wt["w_oa"], wt["w_ob"], wt["w_out"], wt["g2"])
    up, act, dh2, dh2b, loss, dg3 = _fwd_mlp(n2, h1, target, wt["w_up"], wt["w_down"], wt["g3"])

    dup, dh1, dg2 = _bwd_mlp(dh2, dh2b, up, h1, wt["w_up"], wt["w_down"], wt["g2"])
    d_up = _weight_grad("dw_up", n2, dup, N_CHIPS, False, True, D_MODEL)
    d_down = _weight_grad("dw_down", act, dh2b, N_CHIPS, True, False, D_MODEL)
    dz, dya, dyb, merged, dpa, dpb, dh1b = _bwd_merge(dh1, pa, pb, z, wt["w_oa"], wt["w_ob"], wt["w_out"])
    d_out = _weight_grad("dw_out", merged, dh1b, 1, False, False, D_MODEL)
    d_oa = _weight_grad("dw_out_a", ya, dpa, 1, False, False, D_MODEL)
    d_ob = _weight_grad("dw_out_b", yb, dpb, 1, False, False, D_MODEL)
    dz, dlg, dlb, dws, dbs = _bwd_sgu(dz, dyb, z, wt["ln_g"], wt["ln_b"], wt["w_s"], wt["bias_s"])
    dz, dcw, dcb, dwr, dbr, dwi, dbi, dlam = _bwd_lru(
        dz, dya, z, h, wt["conv_w"], wt["conv_b"], wt["wr"], wt["br"], wt["wi"], wt["bi"], wt["lam"])
    d_in = _weight_grad("dw_in", n1, dz, N_CHIPS, False, True, IN_SHARD)
    dx, dg1 = _bwd_in(dz, x, dh1, wt["w_in"], wt["g1"])

    quarter = lambda g: g.reshape(N_CHIPS, D_MODEL // N_CHIPS, D_MODEL)
    big = [d_in, _stack_heads(dwr).astype(BF16), _stack_heads(dwi).astype(BF16), quarter(d_oa), quarter(d_ob),
           quarter(d_out), d_up, d_down]
    small = (dg1, dcw, dcb, dbr, dbi, dlam, dlg, dlb, dg2, dg3, loss, dbs, dws)
    return dx, big, small


def kernel(x, norm_mix_g, w_in, conv_w, conv_b, w_rgate, b_rgate, w_igate, b_igate, lru_lambda, w_out_a, sgu_ln_g, sgu_ln_b, sgu_w_s, sgu_b_s, w_out_b, w_out, norm_mlp_g, w_up, w_down, norm_final_g, loss_target, m_norm_mix_g, m_w_in, m_conv_w, m_conv_b, m_w_rgate, m_b_rgate, m_w_igate, m_b_igate, m_lru_lambda, m_w_out_a, m_sgu_ln_g, m_sgu_ln_b, m_sgu_w_s, m_sgu_b_s, m_w_out_b, m_w_out, m_norm_mlp_g, m_w_up, m_w_down, m_norm_final_g, v_norm_mix_g, v_w_in, v_conv_w, v_conv_b, v_w_rgate, v_b_rgate, v_w_igate, v_b_igate, v_lru_lambda, v_w_out_a, v_sgu_ln_g, v_sgu_ln_b, v_sgu_w_s, v_sgu_b_s, v_w_out_b, v_w_out, v_norm_mlp_g, v_w_up, v_w_down, v_norm_final_g):
    chip = _chip_index(lax.axis_index("x"), lax.axis_index("y"))
    core = lax.axis_index("c")
    quarter_h = HEAD_DIM // N_CHIPS
    quarter_d = D_MODEL // N_CHIPS

    as_2d = lambda a: a.reshape(-1, a.shape[-1])
    big_w = [as_2d(w) for w in (w_in, w_rgate, w_igate, w_out_a, w_out_b, w_out, w_up, w_down)]
    big_m = [as_2d(w) for w in (m_w_in, m_w_rgate, m_w_igate, m_w_out_a, m_w_out_b, m_w_out, m_w_up, m_w_down)]
    big_v = [as_2d(w) for w in (v_w_in, v_w_rgate, v_w_igate, v_w_out_a, v_w_out_b, v_w_out, v_w_up, v_w_down)]

    packed = jnp.concatenate([conv_w[0], b_rgate[0], b_igate[0]], axis=1)
    packed = jnp.concatenate([packed, jnp.zeros_like(packed)], axis=0)
    stacked, packed_all = _gather_weights([w.astype(BF16) for w in big_w], packed)
    w_in_st, wr_st, wi_st, w_oa_st, w_ob_st, w_out_st, w_up_st, w_down_st = stacked
    pick = lambda lo, hi: packed_all[:, :HEADS, lo:hi].transpose(1, 0, 2).reshape(HEADS, -1)
    conv_w_full = pick(0, quarter_d)
    br_full = pick(quarter_d, quarter_d + quarter_h).reshape(1, D_MODEL)
    bi_full = pick(quarter_d + quarter_h, quarter_d + 2 * quarter_h).reshape(1, D_MODEL)
    bias_s = jnp.broadcast_to(jnp.transpose(sgu_b_s[0])[:, :, None], (CHUNK, GROUPS, GROUP_DIM)).reshape(CHUNK, D_MODEL)
    wt = dict(
        g1=norm_mix_g, w_in=w_in_st, conv_w=conv_w_full, conv_b=conv_b, wr=_unstack_heads(wr_st), br=br_full,
        wi=_unstack_heads(wi_st), bi=bi_full, lam=lru_lambda, w_oa=w_oa_st.reshape(D_MODEL, D_MODEL), ln_g=sgu_ln_g,
        ln_b=sgu_ln_b, w_s=sgu_w_s[0], bias_s=bias_s, w_ob=w_ob_st.reshape(D_MODEL, D_MODEL),
        w_out=w_out_st.reshape(D_MODEL, D_MODEL), g2=norm_mlp_g, w_up=w_up_st, w_down=w_down_st.reshape(D_FF, D_MODEL),
        g3=norm_final_g.reshape(1, D_MODEL))

    grad_x, big_g, small_g = _local_step(x[0], loss_target[0], wt)

    names = ("w_in", "w_rgate", "w_igate", "w_out_a", "w_out_b", "w_out", "w_up", "w_down")
    from_sibling = _send_sibling_halves(big_g)
    core_arr = core.reshape(1).astype(jnp.int32)
    pair = [_pair_add("pair_add_" + nm, core_arr, g.reshape(N_CHIPS, 2, g.shape[1] // 2, g.shape[2]), s)
            for nm, g, s in zip(names, big_g, from_sibling)]
    parts = _exchange_chip_sums(pair)
    place = jnp.stack([chip, core]).astype(jnp.int32)
    halves = [_chip_sum("chip_sum_" + nm, place, p, q) for nm, p, q in zip(names, pair, parts)]
    full = [f.reshape(w.shape) for f, w in zip(_share_with_sibling(halves), big_w)]
    big_out = [_adamw("adamw_" + nm, g, w, m, v) for nm, g, w, m, v in zip(names, full, big_w, big_m, big_v)]

    vec, ws_sum = _all_reduce_small(*small_g)
    row = lambda r: vec[r:r + 1]
    shard = lambda a, width: lax.dynamic_slice_in_dim(a, chip * width, width, axis=1)
    g_small = dict(
        norm_mix_g=row(ROW_G1), conv_w=shard(vec[ROW_CW:ROW_CW + CONV_WIDTH], quarter_d), conv_b=row(ROW_CB),
        b_rgate=shard(row(ROW_BR).reshape(HEADS, HEAD_DIM), quarter_h),
        b_igate=shard(row(ROW_BI).reshape(HEADS, HEAD_DIM), quarter_h), lru_lambda=row(ROW_LAM),
        sgu_ln_g=row(ROW_LG), sgu_ln_b=row(ROW_LB),
        sgu_w_s=ws_sum.reshape(CHUNK, GROUPS, CHUNK).transpose(1, 0, 2).reshape(GROUPS * CHUNK, CHUNK),
        sgu_b_s=vec[ROW_BS:ROW_BS + GROUPS, 0:CHUNK], norm_mlp_g=row(ROW_G2), norm_final_g=row(ROW_G3))
    loss = vec[ROW_LOSS, 0]
    small_names = list(g_small)
    given = dict(
        norm_mix_g=(norm_mix_g, m_norm_mix_g, v_norm_mix_g), conv_w=(conv_w, m_conv_w, v_conv_w),
        conv_b=(conv_b, m_conv_b, v_conv_b), b_rgate=(b_rgate, m_b_rgate, v_b_rgate),
        b_igate=(b_igate, m_b_igate, v_b_igate), lru_lambda=(lru_lambda, m_lru_lambda, v_lru_lambda),
        sgu_ln_g=(sgu_ln_g, m_sgu_ln_g, v_sgu_ln_g), sgu_ln_b=(sgu_ln_b, m_sgu_ln_b, v_sgu_ln_b),
        sgu_w_s=(sgu_w_s, m_sgu_w_s, v_sgu_w_s), sgu_b_s=(sgu_b_s, m_sgu_b_s, v_sgu_b_s),
        norm_mlp_g=(norm_mlp_g, m_norm_mlp_g, v_norm_mlp_g), norm_final_g=(norm_final_g, m_norm_final_g, v_norm_final_g))
    g2d = [g_small[nm] for nm in small_names]
    to2d = lambda a, g: a.reshape(g.shape)
    d_s, m_s, v_s = _adamw_small(
        g2d, *[[to2d(given[nm][q], g) for nm, g in zip(small_names, g2d)] for q in range(3)])

    shapes = dict(
        norm_mix_g=norm_mix_g, w_in=w_in, conv_w=conv_w, conv_b=conv_b, w_rgate=w_rgate, b_rgate=b_rgate,
        w_igate=w_igate, b_igate=b_igate, lru_lambda=lru_lambda, w_out_a=w_out_a, sgu_ln_g=sgu_ln_g,
        sgu_ln_b=sgu_ln_b, sgu_w_s=sgu_w_s, sgu_b_s=sgu_b_s, w_out_b=w_out_b, w_out=w_out, norm_mlp_g=norm_mlp_g,
        w_up=w_up, w_down=w_down, norm_final_g=norm_final_g)
    grads, deltas, new_m, new_v = {}, {}, {}, {}
    for nm, g, (d, nmom, nvar) in zip(names, full, big_out):
        grads[nm], deltas[nm], new_m[nm], new_v[nm] = g, d, nmom, nvar
    for p, nm in enumerate(small_names):
        grads[nm], deltas[nm], new_m[nm], new_v[nm] = g2d[p], d_s[p], m_s[p], v_s[p]
    order = list(shapes)
    out = [loss, grad_x[None]]
    for group in (grads, deltas, new_m, new_v):
        out += [group[nm].reshape(shapes[nm].shape) for nm in order]
    return tuple(out)
```

```python
import functools

import jax
import jax.numpy as jnp
from jax import lax
from jax.experimental import pallas as pl
from jax.experimental.pallas import tpu as pltpu

F32 = jnp.float32
BF16 = jnp.bfloat16
MESH = pl.DeviceIdType.MESH

D_MODEL = 1024
D_IN = 6 * D_MODEL
D_FF = 4 * D_MODEL
N_CHIPS = 4
IN_SHARD = D_IN // N_CHIPS
HEADS = 4
HEAD_DIM = D_MODEL // HEADS
GROUPS = 4
GROUP_DIM = D_MODEL // GROUPS
CHUNK = 128
CONV_WIDTH = 4
LRU_C = 8.0
NORM_EPS = 1e-6
LN_EPS = 1e-5

ADAM_LR = 0.001
ADAM_B1 = 0.9
ADAM_B2 = 0.999
ADAM_EPS = 1e-08
ADAM_WD = 0.01
ADAM_STEP = 10

SUBLANES = 8
TOKEN_TILE = 256
DW_TOKEN_TILE = 512
VMEM_LIMIT_BYTES = 56 * 1024 * 1024

GELU_K0 = 0.7978845608028654
GELU_K1 = 0.044715


def _params(n_grid_axes=1):
    return pltpu.CompilerParams(
        dimension_semantics=("arbitrary",) * n_grid_axes, vmem_limit_bytes=VMEM_LIMIT_BYTES)


def _resident(shape):
    nd = len(shape)
    return pl.BlockSpec(shape, lambda *_: (0,) * nd, pipeline_mode=pl.Buffered(1))


def _const(shape):
    nd = len(shape)
    return pl.BlockSpec(shape, lambda *_: (0,) * nd)


def _dot(a, b):
    return jnp.dot(a, b, preferred_element_type=F32)


def _dot_nt(a, b):
    return lax.dot_general(a, b, (((1,), (1,)), ((), ())), preferred_element_type=F32)


def _dot_tn(a, b):
    return lax.dot_general(a, b, (((0,), (0,)), ((), ())), preferred_element_type=F32)


def _gelu(x):
    t = jnp.tanh(GELU_K0 * x * (1.0 + GELU_K1 * x * x))
    return 0.5 * x * (1.0 + t)


def _gelu_and_grad(x):
    x2 = x * x
    t = jnp.tanh(GELU_K0 * x * (1.0 + GELU_K1 * x2))
    g = 0.5 * x * (1.0 + t)
    dg = 0.5 * (1.0 + t) + 0.5 * x * (1.0 - t * t) * (GELU_K0 * (1.0 + 3.0 * GELU_K1 * x2))
    return g, dg


def _rms(x):
    r = lax.rsqrt(jnp.mean(x * x, axis=-1, keepdims=True) + NORM_EPS)
    return x * r, r


def _rms_bwd(dn, xhat, r):
    return r * (dn - xhat * jnp.mean(dn * xhat, axis=-1, keepdims=True))


def _col_sum(v):
    return jnp.sum(v, axis=0, keepdims=True)


def _shift_down(x, tail8, k):
    xs = pltpu.roll(x, k, 0)
    ts = pltpu.roll(tail8, k, 0)
    ridx = lax.broadcasted_iota(jnp.int32, tail8.shape, 0)
    head = jnp.where(ridx < k, ts, xs[0:SUBLANES])
    return jnp.concatenate([head, xs[SUBLANES:]], axis=0)


def _shift_up(x, head8, k):
    n = x.shape[0]
    xs = pltpu.roll(x, n - k, 0)
    hs = pltpu.roll(head8, SUBLANES - k, 0)
    ridx = lax.broadcasted_iota(jnp.int32, head8.shape, 0)
    last = jnp.where(ridx >= SUBLANES - k, hs, xs[n - SUBLANES:n])
    return jnp.concatenate([xs[:n - SUBLANES], last], axis=0)


def _scan_forward(a, b, carry):
    n = a.shape[0]
    sub = lax.broadcasted_iota(jnp.int32, a.shape, 0) & (SUBLANES - 1)
    for s in (1, 2, 4):
        a_s = pltpu.roll(a, s, 0)
        b_s = pltpu.roll(b, s, 0)
        m = sub >= s
        b = jnp.where(m, a * b_s + b, b)
        a = jnp.where(m, a * a_s, a)
    out = []
    for g in range(n // SUBLANES):
        rows = slice(g * SUBLANES, (g + 1) * SUBLANES)
        h = a[rows] * carry + b[rows]
        out.append(h)
        carry = h[SUBLANES - 1:SUBLANES]
    return jnp.concatenate(out, axis=0), carry


def _scan_backward(a, b, carry):
    n = a.shape[0]
    sub = lax.broadcasted_iota(jnp.int32, a.shape, 0) & (SUBLANES - 1)
    for s in (1, 2, 4):
        a_s = pltpu.roll(a, n - s, 0)
        b_s = pltpu.roll(b, n - s, 0)
        m = sub < SUBLANES - s
        b = jnp.where(m, a * b_s + b, b)
        a = jnp.where(m, a * a_s, a)
    out = [None] * (n // SUBLANES)
    for g in reversed(range(n // SUBLANES)):
        rows = slice(g * SUBLANES, (g + 1) * SUBLANES)
        h = a[rows] * carry + b[rows]
        out[g] = h
        carry = h[0:1]
    return jnp.concatenate(out, axis=0), carry


def _softplus_neg(lam):
    e = jnp.exp(-jnp.abs(lam))
    u = 1.0 + e
    log1p_e = jnp.where(u == 1.0, e, jnp.log(u) * (e / jnp.where(u == 1.0, 1.0, u - 1.0)))
    return jnp.maximum(-lam, 0.0) + log1p_e


def _one_minus_exp_neg(x):
    poly = x * (1.0 - 0.5 * x * (1.0 - (1.0 / 3.0) * x * (1.0 - 0.25 * x * (1.0 - 0.2 * x))))
    return jnp.where(x < 0.02, poly, 1.0 - jnp.exp(-x))


def _lru_gates(xa, tail8, cw_ref, cb_ref, wr_ref, br_ref, wi_ref, bi_ref, lam_ref):
    cw = cw_ref[...]
    xs = [xa] + [_shift_down(xa, tail8, k) for k in range(1, CONV_WIDTH)]
    xc = cb_ref[...] + cw[0:1] * xs[0]
    for k in range(1, CONV_WIDTH):
        xc = xc + cw[k:k + 1] * xs[k]
    xcb = xc.astype(BF16)
    pre_r, pre_i = [], []
    for h in range(HEADS):
        cols = slice(h * HEAD_DIM, (h + 1) * HEAD_DIM)
        pre_r.append(_dot(xcb[:, cols], wr_ref[h]))
        pre_i.append(_dot(xcb[:, cols], wi_ref[h]))
    r = jax.nn.sigmoid(jnp.concatenate(pre_r, axis=1) + br_ref[...])
    ig = jax.nn.sigmoid(jnp.concatenate(pre_i, axis=1) + bi_ref[...])
    sp = _softplus_neg(lam_ref[...])
    log_a = (-LRU_C) * sp * r
    a = jnp.exp(log_a)
    mult = jnp.sqrt(_one_minus_exp_neg(-2.0 * log_a))
    return xs, xc, xcb, r, ig, sp, a, mult


class _Job:
    def __init__(self, inputs, out_shape, n_sem, copies, aliases=None, n_local=0):
        self.inputs, self.out_shape, self.n_sem, self.copies = list(inputs), list(out_shape), n_sem, copies
        self.aliases, self.n_local = dict(aliases or {}), n_local


def _fused_call(body, jobs, *, name, grid, in_specs, out_specs, out_shape, scratch_shapes=(),
                input_output_aliases=None, compiler_params=None):
    single = not isinstance(out_shape, (list, tuple))
    out_specs = [out_specs] if single else list(out_specs)
    out_shape = [out_shape] if single else list(out_shape)
    n_scr = len(scratch_shapes)
    in_specs, scratch_shapes = list(in_specs), list(scratch_shapes)
    n_in, n_out = len(in_specs), len(out_shape)
    aliases = dict(input_output_aliases or {})
    in_at, out_at = [], []
    for job in jobs:
        in_at.append(len(in_specs))
        out_at.append(len(out_shape))
        for i, o in job.aliases.items():
            aliases[len(in_specs) + i] = len(out_shape) + o
        in_specs += [ANY] * len(job.inputs)
        out_specs += [ANY] * len(job.out_shape)
        out_shape += job.out_shape
        scratch_shapes += [pltpu.SemaphoreType.DMA((job.n_sem,)), pltpu.SemaphoreType.DMA((job.n_sem,)),
                           pltpu.SemaphoreType.DMA((max(job.n_local, 1),))]
    n_in_all, n_out_all = len(in_specs), len(out_shape)

    def full_body(*refs):
        ins, outs, scr = refs[:n_in_all], refs[n_in_all:n_in_all + n_out_all], refs[n_in_all + n_out_all:]

        def copies(q):
            job = jobs[q]
            return job.copies(ins[in_at[q]:in_at[q] + len(job.inputs)], outs[out_at[q]:out_at[q] + len(job.out_shape)],
                              *scr[n_scr + 3 * q:n_scr + 3 * q + 3])

        def start():
            for q in range(len(jobs)):
                sends, _, local = copies(q)
                for cp in local + sends:
                    cp.start()

        def finish():
            every = [copies(q) for q in range(len(jobs))]
            for _, arrivals, _ in every:
                for cp in arrivals:
                    cp.wait_recv()
            for sends, _, local in every:
                for cp in sends:
                    cp.wait_send()
                for cp in local:
                    cp.wait()

        if not grid:
            start()
            finish()
            return
        ids = [pl.program_id(a) for a in range(len(grid))]
        if jobs:
            pl.when(functools.reduce(jnp.logical_and, [i == 0 for i in ids]))(start)
        body(*ins[:n_in], *outs[:n_out], *scr[:n_scr])
        if jobs:
            pl.when(functools.reduce(jnp.logical_and, [i == g - 1 for i, g in zip(ids, grid)]))(finish)

    call = pl.pallas_call(
        full_body, name=name, grid=grid, in_specs=in_specs, out_specs=out_specs, out_shape=out_shape,
        scratch_shapes=scratch_shapes, input_output_aliases=aliases, compiler_params=compiler_params)

    def run(*args):
        res = call(*args, *[a for job in jobs for a in job.inputs])
        mine = res[0] if single else list(res[:n_out])
        return mine, [list(res[at:at + len(job.out_shape)]) for at, job in zip(out_at, jobs)]

    return run


def _fwd_in(x, g1, w_in_st, jobs=()):
    t = x.shape[0]

    def body(x_ref, g_ref, w_ref, z_ref, n_ref):
        xhat, _ = _rms(x_ref[...])
        n = (xhat * g_ref[...]).astype(BF16)
        n_ref[...] = n
        for k in range(N_CHIPS):
            z_ref[:, k * IN_SHARD:(k + 1) * IN_SHARD] = _dot(n, w_ref[k])

    return _fused_call(
        body, jobs, name="fwd_in", grid=(t // TOKEN_TILE,),
        in_specs=[pl.BlockSpec((TOKEN_TILE, D_MODEL), lambda i: (i, 0)), _const((1, D_MODEL)),
                  _resident((N_CHIPS, D_MODEL, IN_SHARD))],
        out_specs=[pl.BlockSpec((TOKEN_TILE, D_IN), lambda i: (i, 0)),
                   pl.BlockSpec((TOKEN_TILE, D_MODEL), lambda i: (i, 0))],
        out_shape=[jax.ShapeDtypeStruct((t, D_IN), F32), jax.ShapeDtypeStruct((t, D_MODEL), BF16)],
        compiler_params=_params(),
    )(x, g1, w_in_st)


def _fwd_lru(z, conv_w, conv_b, wr, br, wi, bi, lam, jobs=()):
    t = z.shape[0]

    def body(xa_ref, ga_ref, cw_ref, cb_ref, wr_ref, br_ref, wi_ref, bi_ref, lam_ref, h_ref, ya_ref,
             tail_ref, carry_ref):
        @pl.when(pl.program_id(0) == 0)
        def _():
            tail_ref[...] = jnp.zeros_like(tail_ref)
            carry_ref[...] = jnp.zeros_like(carry_ref)

        xa = xa_ref[...]
        _, xc, _, _, ig, _, a, mult = _lru_gates(
            xa, tail_ref[...], cw_ref, cb_ref, wr_ref, br_ref, wi_ref, bi_ref, lam_ref)
        tail_ref[...] = xa[TOKEN_TILE - SUBLANES:]
        h, carry = _scan_forward(a, xc * ig * mult, carry_ref[...])
        carry_ref[...] = carry
        h_ref[...] = h
        ya_ref[...] = (h * _gelu(ga_ref[...])).astype(BF16)

    tile = lambda j: pl.BlockSpec((TOKEN_TILE, D_MODEL), lambda i: (i, j))
    return _fused_call(
        body, jobs, name="fwd_lru", grid=(t // TOKEN_TILE,),
        in_specs=[tile(0), tile(1), _const((CONV_WIDTH, D_MODEL)), _const((1, D_MODEL)),
                  _resident((HEADS, HEAD_DIM, HEAD_DIM)), _const((1, D_MODEL)),
                  _resident((HEADS, HEAD_DIM, HEAD_DIM)), _const((1, D_MODEL)), _const((1, D_MODEL))],
        out_specs=[tile(0), tile(0)],
        out_shape=[jax.ShapeDtypeStruct((t, D_MODEL), F32), jax.ShapeDtypeStruct((t, D_MODEL), BF16)],
        scratch_shapes=[pltpu.VMEM((SUBLANES, D_MODEL), F32), pltpu.VMEM((1, D_MODEL), F32)],
        compiler_params=_params(),
    )(z, z, conv_w, conv_b, wr, br, wi, bi, lam)


def _sgu_forward_parts(ub, vb, lg_ref, lb_ref):
    u, du = _gelu_and_grad(ub)
    vg, dvg = _gelu_and_grad(vb)
    mu = jnp.mean(vg, axis=-1, keepdims=True)
    d = vg - mu
    rstd = lax.rsqrt(jnp.mean(d * d, axis=-1, keepdims=True) + LN_EPS)
    vhat = d * rstd
    vn = (vhat * lg_ref[...] + lb_ref[...]).astype(BF16)
    return u, du, dvg, rstd, vhat, vn


def _causal_mask():
    rows = lax.broadcasted_iota(jnp.int32, (CHUNK, CHUNK), 0)
    cols = lax.broadcasted_iota(jnp.int32, (CHUNK, CHUNK), 1)
    return rows >= cols


def _fwd_sgu(z, ln_g, ln_b, w_s, bias_full, jobs=()):
    t = z.shape[0]

    def body(ub_ref, vb_ref, lg_ref, lb_ref, ws_ref, bias_ref, yb_ref):
        u, _, _, _, _, vn = _sgu_forward_parts(ub_ref[...], vb_ref[...], lg_ref, lb_ref)
        mask = _causal_mask()
        wm = [jnp.where(mask, ws_ref[g], 0.0).astype(BF16) for g in range(GROUPS)]
        for c in range(TOKEN_TILE // CHUNK):
            rows = slice(c * CHUNK, (c + 1) * CHUNK)
            for g in range(GROUPS):
                cols = slice(g * GROUP_DIM, (g + 1) * GROUP_DIM)
                sp = _dot(wm[g], vn[rows, cols]) + bias_ref[:, cols]
                yb_ref[rows, cols] = (u[rows, cols] * sp).astype(BF16)

    tile = lambda j: pl.BlockSpec((TOKEN_TILE, D_MODEL), lambda i: (i, j))
    return _fused_call(
        body, jobs, name="fwd_sgu", grid=(t // TOKEN_TILE,),
        in_specs=[tile(2), tile(3), _const((1, D_MODEL)), _const((1, D_MODEL)),
                  _const((GROUPS, CHUNK, CHUNK)), _const((CHUNK, D_MODEL))],
        out_specs=tile(0),
        out_shape=jax.ShapeDtypeStruct((t, D_MODEL), BF16),
        compiler_params=_params(),
    )(z, z, ln_g, ln_b, w_s, bias_full)


def _fwd_merge(ya, yb, z, x, w_oa, w_ob, w_out, g2, jobs=()):
    t = x.shape[0]

    def body(ya_ref, yb_ref, m_ref, x_ref, woa_ref, wob_ref, wout_ref, g_ref, pa_ref, pb_ref, h1_ref, n2_ref):
        pa = _dot(ya_ref[...], woa_ref[...])
        pb = _dot(yb_ref[...], wob_ref[...])
        pa_ref[...] = pa
        pb_ref[...] = pb
        merged = jax.nn.sigmoid(m_ref[:, :D_MODEL]) * pa + jax.nn.sigmoid(m_ref[:, D_MODEL:]) * pb
        h1 = x_ref[...] + _dot(merged.astype(BF16), wout_ref[...])
        h1_ref[...] = h1
        xhat, _ = _rms(h1)
        n2_ref[...] = (xhat * g_ref[...]).astype(BF16)

    tile = pl.BlockSpec((TOKEN_TILE, D_MODEL), lambda i: (i, 0))
    sq = _resident((D_MODEL, D_MODEL))
    return _fused_call(
        body, jobs, name="fwd_merge", grid=(t // TOKEN_TILE,),
        in_specs=[tile, tile, pl.BlockSpec((TOKEN_TILE, 2 * D_MODEL), lambda i: (i, 2)), tile, sq, sq, sq,
                  _const((1, D_MODEL))],
        out_specs=[tile, tile, tile, tile],
        out_shape=[jax.ShapeDtypeStruct((t, D_MODEL), F32)] * 3 + [jax.ShapeDtypeStruct((t, D_MODEL), BF16)],
        compiler_params=_params(),
    )(ya, yb, z, x, w_oa, w_ob, w_out, g2)


def _fwd_mlp(n2, h1, target, w_up_st, w_down, g3, jobs=()):
    t = n2.shape[0]

    def body(n2_ref, h1_ref, tgt_ref, wup_ref, wdown_ref, g_ref, up_ref, act_ref, dh2_ref, dh2b_ref, loss_ref,
             dg3_ref):
        @pl.when(pl.program_id(0) == 0)
        def _():
            loss_ref[...] = jnp.zeros_like(loss_ref)
            dg3_ref[...] = jnp.zeros_like(dg3_ref)

        n2 = n2_ref[...]
        h2 = h1_ref[...]
        for k in range(N_CHIPS):
            cols = slice(k * D_MODEL, (k + 1) * D_MODEL)
            up = _dot(n2, wup_ref[k])
            up_ref[:, cols] = up
            r = jnp.maximum(up, 0.0)
            act = (r * r).astype(BF16)
            act_ref[:, cols] = act
            h2 = h2 + _dot(act, wdown_ref[cols, :])
        xhat, r3 = _rms(h2)
        diff = xhat * g_ref[...] - tgt_ref[...]
        sq = jnp.sum(diff * diff, axis=1, keepdims=True)
        loss_ref[...] = loss_ref[...] + (0.5 / D_MODEL) * jnp.sum(sq, axis=0, keepdims=True)
        dy = diff * (1.0 / D_MODEL)
        dg3_ref[...] = dg3_ref[...] + _col_sum(dy * xhat)
        dh2 = _rms_bwd(dy * g_ref[...], xhat, r3)
        dh2_ref[...] = dh2
        dh2b_ref[...] = dh2.astype(BF16)

    tile = pl.BlockSpec((TOKEN_TILE, D_MODEL), lambda i: (i, 0))
    wide = pl.BlockSpec((TOKEN_TILE, D_FF), lambda i: (i, 0))
    return _fused_call(
        body, jobs, name="fwd_mlp", grid=(t // TOKEN_TILE,),
        in_specs=[tile, tile, tile, _resident((N_CHIPS, D_MODEL, D_MODEL)), _resident((D_FF, D_MODEL)),
                  _const((1, D_MODEL))],
        out_specs=[wide, wide, tile, tile, _const((SUBLANES, 128)), _const((1, D_MODEL))],
        out_shape=[jax.ShapeDtypeStruct((t, D_FF), F32), jax.ShapeDtypeStruct((t, D_FF), BF16),
                   jax.ShapeDtypeStruct((t, D_MODEL), F32), jax.ShapeDtypeStruct((t, D_MODEL), BF16),
                   jax.ShapeDtypeStruct((SUBLANES, 128), F32), jax.ShapeDtypeStruct((1, D_MODEL), F32)],
        compiler_params=_params(),
    )(n2, h1, target, w_up_st, w_down, g3)


def _bwd_mlp(dh2, dh2b, up, h1, w_up_st, w_down, g2, jobs=()):
    t = dh2.shape[0]

    def body(dh2_ref, dh2b_ref, up_ref, h1_ref, wup_ref, wdown_ref, g_ref, dup_ref, dh1_ref, dg2_ref):
        @pl.when(pl.program_id(0) == 0)
        def _():
            dg2_ref[...] = jnp.zeros_like(dg2_ref)

        dh2b = dh2b_ref[...]
        dn2 = jnp.zeros((TOKEN_TILE, D_MODEL), F32)
        for k in range(N_CHIPS):
            cols = slice(k * D_MODEL, (k + 1) * D_MODEL)
            dact = _dot_nt(dh2b, wdown_ref[cols, :])
            dup = (dact * (2.0 * jnp.maximum(up_ref[:, cols], 0.0))).astype(BF16)
            dup_ref[:, cols] = dup
            dn2 = dn2 + _dot_nt(dup, wup_ref[k])
        xhat, r2 = _rms(h1_ref[...])
        dg2_ref[...] = dg2_ref[...] + _col_sum(dn2 * xhat)
        dh1_ref[...] = dh2_ref[...] + _rms_bwd(dn2 * g_ref[...], xhat, r2)

    tile = pl.BlockSpec((TOKEN_TILE, D_MODEL), lambda i: (i, 0))
    wide = pl.BlockSpec((TOKEN_TILE, D_FF), lambda i: (i, 0))
    return _fused_call(
        body, jobs, name="bwd_mlp", grid=(t // TOKEN_TILE,),
        in_specs=[tile, tile, wide, tile, _resident((N_CHIPS, D_MODEL, D_MODEL)), _resident((D_FF, D_MODEL)),
                  _const((1, D_MODEL))],
        out_specs=[wide, tile, _const((1, D_MODEL))],
        out_shape=[jax.ShapeDtypeStruct((t, D_FF), BF16), jax.ShapeDtypeStruct((t, D_MODEL), F32),
                   jax.ShapeDtypeStruct((1, D_MODEL), F32)],
        compiler_params=_params(),
    )(dh2, dh2b, up, h1, w_up_st, w_down, g2)


def _bwd_merge(dh1, pa, pb, z, w_oa, w_ob, w_out, jobs=()):
    t = dh1.shape[0]

    def body(dh1_ref, pa_ref, pb_ref, m_ref, woa_ref, wob_ref, wout_ref, dz_ref, dya_ref, dyb_ref, mg_ref,
             dpa_ref, dpb_ref, dh1b_ref):
        dh1b = dh1_ref[...].astype(BF16)
        dh1b_ref[...] = dh1b
        dm = _dot_nt(dh1b, wout_ref[...])
        pa = pa_ref[...]
        pb = pb_ref[...]
        sa = jax.nn.sigmoid(m_ref[:, :D_MODEL])
        sb = jax.nn.sigmoid(m_ref[:, D_MODEL:])
        mg_ref[...] = (sa * pa + sb * pb).astype(BF16)
        dz_ref[:, :D_MODEL] = (dm * pa * sa * (1.0 - sa)).astype(BF16)
        dz_ref[:, D_MODEL:] = (dm * pb * sb * (1.0 - sb)).astype(BF16)
        dpa = (dm * sa).astype(BF16)
        dpb = (dm * sb).astype(BF16)
        dpa_ref[...] = dpa
        dpb_ref[...] = dpb
        dya_ref[...] = _dot_nt(dpa, woa_ref[...])
        dyb_ref[...] = _dot_nt(dpb, wob_ref[...])

    tile = pl.BlockSpec((TOKEN_TILE, D_MODEL), lambda i: (i, 0))
    pair = pl.BlockSpec((TOKEN_TILE, 2 * D_MODEL), lambda i: (i, 2))
    sq = _resident((D_MODEL, D_MODEL))
    act_bf = jax.ShapeDtypeStruct((t, D_MODEL), BF16)
    return _fused_call(
        body, jobs, name="bwd_merge", grid=(t // TOKEN_TILE,),
        in_specs=[tile, tile, tile, pair, sq, sq, sq],
        out_specs=[pair, tile, tile, tile, tile, tile, tile],
        out_shape=[jax.ShapeDtypeStruct((t, D_IN), BF16), jax.ShapeDtypeStruct((t, D_MODEL), F32),
                   jax.ShapeDtypeStruct((t, D_MODEL), F32), act_bf, act_bf, act_bf, act_bf],
        compiler_params=_params(),
    )(dh1, pa, pb, z, w_oa, w_ob, w_out)


def _bwd_sgu(dz, dyb, z, ln_g, ln_b, w_s, bias_full, jobs=()):
    t = dyb.shape[0]
    n_tiles = t // TOKEN_TILE

    def body(dz_any, dyb_ref, ub_ref, vb_ref, lg_ref, lb_ref, ws_ref, bias_ref, dz_ref, dlg_ref, dlb_ref, dws_ref,
             dbs_ref, dvn_ref, dsp_acc):
        del dz_any
        i = pl.program_id(0)

        @pl.when(i == 0)
        def _():
            dlg_ref[...] = jnp.zeros_like(dlg_ref)
            dlb_ref[...] = jnp.zeros_like(dlb_ref)
            dws_ref[...] = jnp.zeros_like(dws_ref)
            dsp_acc[...] = jnp.zeros_like(dsp_acc)

        u, du, dvg, rstd, vhat, vn = _sgu_forward_parts(ub_ref[...], vb_ref[...], lg_ref, lb_ref)
        dyb = dyb_ref[...]
        mask = _causal_mask()
        wm = [jnp.where(mask, ws_ref[g], 0.0).astype(BF16) for g in range(GROUPS)]
        for c in range(TOKEN_TILE // CHUNK):
            rows = slice(c * CHUNK, (c + 1) * CHUNK)
            for g in range(GROUPS):
                cols = slice(g * GROUP_DIM, (g + 1) * GROUP_DIM)
                vn_blk = vn[rows, cols]
                sp = _dot(wm[g], vn_blk) + bias_ref[:, cols]
                dyb_blk = dyb[rows, cols]
                dz_ref[rows, cols] = (dyb_blk * sp * du[rows, cols]).astype(BF16)
                dsp = dyb_blk * u[rows, cols]
                dsp_acc[:, cols] = dsp_acc[:, cols] + dsp
                dspb = dsp.astype(BF16)
                dvn_ref[rows, cols] = _dot_tn(wm[g], dspb)
                wcols = slice(g * CHUNK, (g + 1) * CHUNK)
                dws_ref[:, wcols] = dws_ref[:, wcols] + jnp.where(mask, _dot_nt(dspb, vn_blk), 0.0)
        dvn = dvn_ref[...]
        dlg_ref[...] = dlg_ref[...] + _col_sum(dvn * vhat)
        dlb_ref[...] = dlb_ref[...] + _col_sum(dvn)
        dvhat = dvn * lg_ref[...]
        dvgel = rstd * (dvhat - jnp.mean(dvhat, axis=-1, keepdims=True)
                        - vhat * jnp.mean(dvhat * vhat, axis=-1, keepdims=True))
        dz_ref[:, D_MODEL:] = (dvgel * dvg).astype(BF16)

        @pl.when(i == n_tiles - 1)
        def _():
            lane = lax.broadcasted_iota(jnp.int32, (CHUNK, 128), 1)
            out = jnp.zeros((CHUNK, 128), F32)
            for g in range(GROUPS):
                s = jnp.sum(dsp_acc[:, g * GROUP_DIM:(g + 1) * GROUP_DIM], axis=1, keepdims=True)
                out = out + jnp.where(lane == g, s, 0.0)
            dbs_ref[...] = out

    tile = lambda j: pl.BlockSpec((TOKEN_TILE, D_MODEL), lambda i: (i, j))
    return _fused_call(
        body, jobs, name="bwd_sgu", grid=(n_tiles,),
        in_specs=[pl.BlockSpec(memory_space=pl.ANY), tile(0), tile(2), tile(3), _const((1, D_MODEL)),
                  _const((1, D_MODEL)), _const((GROUPS, CHUNK, CHUNK)), _const((CHUNK, D_MODEL))],
        out_specs=[pl.BlockSpec((TOKEN_TILE, 2 * D_MODEL), lambda i: (i, 1)), _const((1, D_MODEL)),
                   _const((1, D_MODEL)), _const((CHUNK, GROUPS * CHUNK)), _const((CHUNK, 128))],
        out_shape=[jax.ShapeDtypeStruct((t, D_IN), BF16), jax.ShapeDtypeStruct((1, D_MODEL), F32),
                   jax.ShapeDtypeStruct((1, D_MODEL), F32), jax.ShapeDtypeStruct((CHUNK, GROUPS * CHUNK), F32),
                   jax.ShapeDtypeStruct((CHUNK, 128), F32)],
        scratch_shapes=[pltpu.VMEM((TOKEN_TILE, D_MODEL), F32), pltpu.VMEM((CHUNK, D_MODEL), F32)],
        input_output_aliases={0: 0},
        compiler_params=_params(),
    )(dz, dyb, z, z, ln_g, ln_b, w_s, bias_full)


def _bwd_lru(dz, dya, z, h, conv_w, conv_b, wr, br, wi, bi, lam, jobs=()):
    t = dya.shape[0]
    n_tiles = t // TOKEN_TILE
    per_tile = TOKEN_TILE // SUBLANES

    def body(dz_any, dya_ref, xa_ref, xa_prev_ref, ga_ref, h_ref, h_prev_ref, cw_ref, cb_ref, wr_ref, br_ref, wi_ref,
             bi_ref, lam_ref, dz_ref, dcw_ref, dcb_ref, dwr_ref, dbr_ref, dwi_ref, dbi_ref, dlam_ref, lam_carry,
             dxc_head):
        del dz_any
        i = pl.program_id(0)

        @pl.when(i == 0)
        def _():
            for ref in (dcw_ref, dcb_ref, dwr_ref, dbr_ref, dwi_ref, dbi_ref, dlam_ref, lam_carry, dxc_head):
                ref[...] = jnp.zeros_like(ref)

        first_tile = i == n_tiles - 1
        xa = xa_ref[...]
        tail = jnp.where(first_tile, 0.0, xa_prev_ref[...])
        h_tail = jnp.where(first_tile, 0.0, h_prev_ref[...])
        xs, xc, xcb, r, ig, sp, a, mult = _lru_gates(xa, tail, cw_ref, cb_ref, wr_ref, br_ref, wi_ref, bi_ref, lam_ref)
        h = h_ref[...]
        h_prev = _shift_down(h, h_tail, 1)
        dya = dya_ref[...]
        gg, dgg = _gelu_and_grad(ga_ref[...])
        dz_ref[:, D_MODEL:] = (dya * h * dgg).astype(BF16)
        ones = jnp.ones((SUBLANES, D_MODEL), F32)
        lam_t, lam_first = _scan_backward(_shift_up(a, ones, 1), dya * gg, lam_carry[...])
        lam_carry[...] = a[0:1] * lam_first
        dmult = lam_t * xc * ig
        dla = lam_t * h_prev * a - dmult * (a * a) / mult
        dr = dla * ((-LRU_C) * sp)
        dlam_ref[...] = dlam_ref[...] + _col_sum(dla * r) * (LRU_C * jax.nn.sigmoid(-lam_ref[...]))
        dpr = dr * r * (1.0 - r)
        dpi = lam_t * xc * mult * ig * (1.0 - ig)
        dbr_ref[...] = dbr_ref[...] + _col_sum(dpr)
        dbi_ref[...] = dbi_ref[...] + _col_sum(dpi)
        dprb = dpr.astype(BF16)
        dpib = dpi.astype(BF16)
        dxc_gate = []
        for hd in range(HEADS):
            cols = slice(hd * HEAD_DIM, (hd + 1) * HEAD_DIM)
            dxc_gate.append(_dot_nt(dprb[:, cols], wr_ref[hd]) + _dot_nt(dpib[:, cols], wi_ref[hd]))
            dwr_ref[hd] = dwr_ref[hd] + _dot_tn(xcb[:, cols], dprb[:, cols])
            dwi_ref[hd] = dwi_ref[hd] + _dot_tn(xcb[:, cols], dpib[:, cols])
        dxc = lam_t * ig * mult + jnp.concatenate(dxc_gate, axis=1)
        dcb_ref[...] = dcb_ref[...] + _col_sum(dxc)
        cw = cw_ref[...]
        head = dxc_head[...]
        dxa = cw[0:1] * dxc
        for k in range(CONV_WIDTH):
            dcw_ref[k:k + 1, :] = dcw_ref[k:k + 1, :] + _col_sum(dxc * xs[k])
            if k:
                dxa = dxa + cw[k:k + 1] * _shift_up(dxc, head, k)
        dxc_head[...] = dxc[0:SUBLANES]
        dz_ref[:, :D_MODEL] = dxa.astype(BF16)

    rev = lambda i: n_tiles - 1 - i
    tile = lambda j: pl.BlockSpec((TOKEN_TILE, D_MODEL), lambda i: (rev(i), j))
    prev8 = pl.BlockSpec((SUBLANES, D_MODEL), lambda i: (jnp.maximum(rev(i) * per_tile - 1, 0), 0))
    vec = _const((1, D_MODEL))
    gate_w = _resident((HEADS, HEAD_DIM, HEAD_DIM))
    vec_shape = jax.ShapeDtypeStruct((1, D_MODEL), F32)
    gate_shape = jax.ShapeDtypeStruct((HEADS, HEAD_DIM, HEAD_DIM), F32)
    return _fused_call(
        body, jobs, name="bwd_lru", grid=(n_tiles,),
        in_specs=[pl.BlockSpec(memory_space=pl.ANY), tile(0), tile(0), prev8, tile(1), tile(0), prev8,
                  _const((CONV_WIDTH, D_MODEL)), vec, gate_w, vec, gate_w, vec, vec],
        out_specs=[pl.BlockSpec((TOKEN_TILE, 2 * D_MODEL), lambda i: (rev(i), 0)), _const((SUBLANES, D_MODEL)), vec,
                   _const((HEADS, HEAD_DIM, HEAD_DIM)), vec, _const((HEADS, HEAD_DIM, HEAD_DIM)), vec, vec],
        out_shape=[jax.ShapeDtypeStruct((t, D_IN), BF16), jax.ShapeDtypeStruct((SUBLANES, D_MODEL), F32), vec_shape,
                   gate_shape, vec_shape, gate_shape, vec_shape, vec_shape],
        scratch_shapes=[pltpu.VMEM((1, D_MODEL), F32), pltpu.VMEM((SUBLANES, D_MODEL), F32)],
        input_output_aliases={0: 0},
        compiler_params=_params(),
    )(dz, dya, z, z, z, h, h, conv_w, conv_b, wr, br, wi, bi, lam)


def _bwd_in(dz, x, dh1, w_in_st, g1, jobs=()):
    t = x.shape[0]

    def body(dz_ref, x_ref, dh1_ref, w_ref, g_ref, dx_ref, dg1_ref):
        @pl.when(pl.program_id(0) == 0)
        def _():
            dg1_ref[...] = jnp.zeros_like(dg1_ref)

        dn1 = jnp.zeros((TOKEN_TILE, D_MODEL), F32)
        for k in range(N_CHIPS):
            dn1 = dn1 + _dot_nt(dz_ref[:, k * IN_SHARD:(k + 1) * IN_SHARD], w_ref[k])
        xhat, r1 = _rms(x_ref[...])
        dg1_ref[...] = dg1_ref[...] + _col_sum(dn1 * xhat)
        dx_ref[...] = dh1_ref[...] + _rms_bwd(dn1 * g_ref[...], xhat, r1)

    tile = pl.BlockSpec((TOKEN_TILE, D_MODEL), lambda i: (i, 0))
    return _fused_call(
        body, jobs, name="bwd_in", grid=(t // TOKEN_TILE,),
        in_specs=[pl.BlockSpec((TOKEN_TILE, D_IN), lambda i: (i, 0)), tile, tile,
                  _resident((N_CHIPS, D_MODEL, IN_SHARD)), _const((1, D_MODEL))],
        out_specs=[tile, _const((1, D_MODEL))],
        out_shape=[jax.ShapeDtypeStruct((t, D_MODEL), F32), jax.ShapeDtypeStruct((1, D_MODEL), F32)],
        compiler_params=_params(),
    )(dz, x, dh1, w_in_st, g1)


def _weight_grad(name, a, b, n_blocks, a_varies, b_varies, width, jobs=()):
    t = a.shape[0]
    n_t = t // DW_TOKEN_TILE

    def body(a_ref, b_ref, o_ref, acc_ref):
        s = pl.program_id(1)

        @pl.when(s == 0)
        def _():
            acc_ref[...] = jnp.zeros_like(acc_ref)

        acc_ref[...] = acc_ref[...] + _dot_tn(a_ref[...], b_ref[...])

        @pl.when(s == n_t - 1)
        def _():
            o_ref[...] = acc_ref[...].astype(BF16)

    return _fused_call(
        body, jobs, name=name, grid=(n_blocks, n_t),
        in_specs=[pl.BlockSpec((DW_TOKEN_TILE, D_MODEL), (lambda j, s: (s, j)) if a_varies else (lambda j, s: (s, 0))),
                  pl.BlockSpec((DW_TOKEN_TILE, width), (lambda j, s: (s, j)) if b_varies else (lambda j, s: (s, 0)))],
        out_specs=pl.BlockSpec((None, D_MODEL, width), lambda j, s: (j, 0, 0)),
        out_shape=jax.ShapeDtypeStruct((n_blocks, D_MODEL, width), BF16),
        scratch_shapes=[pltpu.VMEM((D_MODEL, width), F32)],
        compiler_params=_params(2),
    )(a, b)


def _place():
    x, y, c = lax.axis_index("x"), lax.axis_index("y"), lax.axis_index("c")
    other_chips = [(1 - x, y), (x, 1 - y), (1 - x, 1 - y)]
    return x, y, c, other_chips


def _chip_index(px, py):
    return 2 * px + py


ANY = pl.BlockSpec(memory_space=pl.ANY)


def _comm_call(name, jobs):
    return _fused_call(None, jobs, name=name, grid=(), in_specs=[], out_specs=[], out_shape=[])()[1]


def _gather_ici_job(shards):
    n = len(shards)
    halves = [s.shape[0] // 2 for s in shards]

    def copies(ins, outs, send, recv, local):
        x, y, c, chips = _place()
        me, sibling = (x, y, c), (x, y, 1 - c)

        def block(w, place):
            return outs[w].at[_chip_index(place[0], place[1]), pl.ds(place[2] * halves[w], halves[w]), :]

        def copy(w, k, blk, to, src=None):
            return pltpu.make_async_remote_copy(
                src_ref=block(w, blk) if src is None else src, dst_ref=block(w, blk),
                send_sem=send.at[4 * w + k], recv_sem=recv.at[4 * w + k], device_id=to, device_id_type=MESH)

        sends, arrivals, own = [], [], []
        for w in range(n):
            src = ins[w].at[pl.ds(c * halves[w], halves[w]), :]
            own.append(pltpu.make_async_copy(src, block(w, me), local.at[w]))
            sends.append(copy(w, 0, me, sibling, src))
            arrivals.append(copy(w, 0, sibling, me))
            for j, chip in enumerate(chips):
                sends.append(copy(w, 1 + j, me, (*chip, c), src))
                arrivals.append(copy(w, 1 + j, (*chip, c), me))
        return sends, arrivals, own

    return _Job(shards, [jax.ShapeDtypeStruct((N_CHIPS,) + s.shape, s.dtype) for s in shards], 4 * n, copies,
                n_local=n)


def _gather_pass_job(stacked):
    n = len(stacked)
    halves = [s.shape[1] // 2 for s in stacked]

    def copies(ins, outs, send, recv, local):
        del ins, local
        x, y, c, chips = _place()

        def copy(w, j, chip, pc, to):
            blk = outs[w].at[_chip_index(*chip), pl.ds(pc * halves[w], halves[w]), :]
            return pltpu.make_async_remote_copy(
                src_ref=blk, dst_ref=blk, send_sem=send.at[3 * w + j], recv_sem=recv.at[3 * w + j], device_id=to,
                device_id_type=MESH)

        sends = [copy(w, j, chip, c, (x, y, 1 - c)) for w in range(n) for j, chip in enumerate(chips)]
        arrivals = [copy(w, j, chip, 1 - c, (x, y, c)) for w in range(n) for j, chip in enumerate(chips)]
        return sends, arrivals, []

    return _Job(stacked, [jax.ShapeDtypeStruct(s.shape, s.dtype) for s in stacked], 3 * n, copies,
                aliases={w: w for w in range(n)})


def _gather_small_job(block):
    def copies(ins, outs, send, recv, local):
        x, y, c, chips = _place()

        def copy(j, chip_from, to):
            return pltpu.make_async_remote_copy(
                src_ref=ins[0], dst_ref=outs[0].at[_chip_index(*chip_from)], send_sem=send.at[j],
                recv_sem=recv.at[j], device_id=to, device_id_type=MESH)

        own = [pltpu.make_async_copy(ins[0], outs[0].at[_chip_index(x, y)], local.at[0])]
        sends = [copy(j, (x, y), (*chip, c)) for j, chip in enumerate(chips)]
        arrivals = [copy(j, chip, (x, y, c)) for j, chip in enumerate(chips)]
        return sends, arrivals, own

    return _Job([block], [jax.ShapeDtypeStruct((N_CHIPS,) + block.shape, block.dtype)], 3, copies, n_local=1)


def _pair_send_job(grads):
    n = len(grads)
    halves = [g.shape[1] // 2 for g in grads]

    def copies(ins, outs, send, recv, local):
        del local
        x, y, c, _ = _place()
        sends = [pltpu.make_async_remote_copy(
            src_ref=ins[w].at[:, pl.ds((1 - c) * halves[w], halves[w]), :], dst_ref=outs[w], send_sem=send.at[w],
            recv_sem=recv.at[w], device_id=(x, y, 1 - c), device_id_type=MESH) for w in range(n)]
        return sends, sends, []

    return _Job(grads, [jax.ShapeDtypeStruct((N_CHIPS, h, g.shape[2]), g.dtype) for g, h in zip(grads, halves)], n,
                copies)


def _row_block(rows):
    return min(rows, 256)


def _pair_add(name, core, mine, theirs):
    _, _, h, cols = mine.shape
    rb = _row_block(h)

    def body(core_ref, a_ref, b_ref, o_ref):
        del core_ref
        o_ref[...] = (a_ref[...].astype(F32) + b_ref[...].astype(F32)).astype(BF16)

    return pl.pallas_call(
        body, name=name,
        grid_spec=pltpu.PrefetchScalarGridSpec(
            num_scalar_prefetch=1, grid=(N_CHIPS, h // rb),
            in_specs=[pl.BlockSpec((None, None, rb, cols), lambda k, r, core_ref: (k, core_ref[0], r, 0)),
                      pl.BlockSpec((None, rb, cols), lambda k, r, core_ref: (k, r, 0))],
            out_specs=pl.BlockSpec((None, rb, cols), lambda k, r, core_ref: (k, r, 0))),
        out_shape=jax.ShapeDtypeStruct(theirs.shape, BF16),
        compiler_params=_params(2),
    )(core, mine, theirs)


def _chip_exchange_job(sums):
    n = len(sums)

    def copies(ins, outs, send, recv, local):
        del local
        _, _, c, chips = _place()
        sends = [pltpu.make_async_remote_copy(
            src_ref=ins[w].at[_chip_index(*chip)], dst_ref=outs[w].at[j], send_sem=send.at[3 * w + j],
            recv_sem=recv.at[3 * w + j], device_id=(*chip, c), device_id_type=MESH)
            for w in range(n) for j, chip in enumerate(chips)]
        return sends, sends, []

    return _Job(sums, [jax.ShapeDtypeStruct((N_CHIPS - 1,) + s.shape[1:], s.dtype) for s in sums], 3 * n, copies)


def _chip_sum(name, place, mine, theirs):
    _, h, cols = mine.shape
    rb = _row_block(h)

    def body(place_ref, p_ref, q_ref, o_ref):
        del place_ref
        acc = p_ref[...].astype(F32)
        for j in range(N_CHIPS - 1):
            acc = acc + q_ref[j].astype(F32)
        o_ref[...] = acc

    return pl.pallas_call(
        body, name=name,
        grid_spec=pltpu.PrefetchScalarGridSpec(
            num_scalar_prefetch=1, grid=(h // rb,),
            in_specs=[pl.BlockSpec((None, rb, cols), lambda r, place_ref: (place_ref[0], r, 0)),
                      pl.BlockSpec((N_CHIPS - 1, rb, cols), lambda r, place_ref: (0, r, 0))],
            out_specs=pl.BlockSpec((None, rb, cols), lambda r, place_ref: (place_ref[1], r, 0))),
        out_shape=jax.ShapeDtypeStruct((2, h, cols), F32),
        compiler_params=_params(),
    )(place, mine, theirs)


def _share_job(bufs):
    n = len(bufs)

    def copies(ins, outs, send, recv, local):
        del ins, local
        x, y, c, _ = _place()

        def copy(w, half):
            return pltpu.make_async_remote_copy(
                src_ref=outs[w].at[half], dst_ref=outs[w].at[half], send_sem=send.at[w], recv_sem=recv.at[w],
                device_id=(x, y, 1 - c), device_id_type=MESH)

        return [copy(w, c) for w in range(n)], [copy(w, 1 - c) for w in range(n)], []

    return _Job(bufs, [jax.ShapeDtypeStruct(b.shape, b.dtype) for b in bufs], n, copies,
                aliases={w: w for w in range(n)})


SMALL_ROWS = 24
ROW_G1, ROW_CW, ROW_CB, ROW_BR, ROW_BI, ROW_LAM, ROW_LG, ROW_LB, ROW_G2, ROW_G3, ROW_LOSS, ROW_BS = (
    0, 1, 5, 6, 7, 8, 9, 10, 11, 12, 13, 16)
N_DEV = 8


def _all_reduce_small(dg1, dcw, dcb, dbr, dbi, dlam, dlg, dlb, dg2, dg3, loss, dbs, dws):
    def body(dg1_ref, dcw_ref, dcb_ref, dbr_ref, dbi_ref, dlam_ref, dlg_ref, dlb_ref, dg2_ref, dg3_ref, loss_ref,
             dbs_ref, dws_ref, vec_out, ws_out, vec_all, ws_all, send, recv):
        x, y, c, chips = _place()
        me, sibling = (x, y, c), (x, y, 1 - c)

        def dev(px, py, pc):
            return 4 * px + 2 * py + pc

        mine = vec_all.at[dev(*me)]
        mine[...] = jnp.zeros((SMALL_ROWS, D_MODEL), F32)
        for row, ref in ((ROW_G1, dg1_ref), (ROW_CB, dcb_ref), (ROW_BR, dbr_ref), (ROW_BI, dbi_ref),
                         (ROW_LAM, dlam_ref), (ROW_LG, dlg_ref), (ROW_LB, dlb_ref), (ROW_G2, dg2_ref),
                         (ROW_G3, dg3_ref)):
            mine[row:row + 1, :] = ref[...]
        mine[ROW_CW:ROW_CW + CONV_WIDTH, :] = dcw_ref[0:CONV_WIDTH, :]
        mine[ROW_LOSS:ROW_LOSS + 1, 0:128] = loss_ref[0:1, :]
        mine[ROW_BS:ROW_BS + GROUPS, 0:128] = jnp.transpose(dbs_ref[...])[0:GROUPS, :]
        ws_all.at[dev(*me)][...] = dws_ref[...]

        def copies(k, blk, to):
            d = dev(*blk)
            return [pltpu.make_async_remote_copy(
                src_ref=buf.at[d], dst_ref=buf.at[d], send_sem=send.at[2 * k + q], recv_sem=recv.at[2 * k + q],
                device_id=to, device_id_type=MESH) for q, buf in enumerate((vec_all, ws_all))]

        first = copies(0, me, sibling)
        for j, chip in enumerate(chips):
            first += copies(1 + j, me, (*chip, c))
        for cp in first:
            cp.start()
        passed = []
        for j, chip in enumerate(chips):
            for cp in copies(1 + j, (*chip, c), me):
                cp.wait_recv()
            fwd = copies(4 + j, (*chip, c), sibling)
            for cp in fwd:
                cp.start()
            passed += fwd
        for cp in copies(0, sibling, me):
            cp.wait_recv()
        for j, chip in enumerate(chips):
            for cp in copies(4 + j, (*chip, 1 - c), me):
                cp.wait_recv()
        for cp in first + passed:
            cp.wait_send()
        vec = vec_all[0]
        ws = ws_all[0]
        for d in range(1, N_DEV):
            vec = vec + vec_all[d]
            ws = ws + ws_all[d]
        vec_out[...] = vec
        ws_out[...] = ws

    vm = pl.BlockSpec(memory_space=pltpu.VMEM)
    return pl.pallas_call(
        body, name="all_reduce_small", in_specs=[vm] * 13, out_specs=[vm, vm],
        out_shape=[jax.ShapeDtypeStruct((SMALL_ROWS, D_MODEL), F32),
                   jax.ShapeDtypeStruct((CHUNK, GROUPS * CHUNK), F32)],
        scratch_shapes=[pltpu.VMEM((N_DEV, SMALL_ROWS, D_MODEL), F32), pltpu.VMEM((N_DEV, CHUNK, GROUPS * CHUNK), F32),
                        pltpu.SemaphoreType.DMA((14,)), pltpu.SemaphoreType.DMA((14,))],
        compiler_params=pltpu.CompilerParams(vmem_limit_bytes=VMEM_LIMIT_BYTES),
    )(dg1, dcw, dcb, dbr, dbi, dlam, dlg, dlb, dg2, dg3, loss, dbs, dws)


def _adamw_math(w, g, m, v):
    m = ADAM_B1 * m + (1.0 - ADAM_B1) * g
    v = ADAM_B2 * v + (1.0 - ADAM_B2) * (g * g)
    m_hat = m / (1.0 - ADAM_B1 ** ADAM_STEP)
    v_hat = v / (1.0 - ADAM_B2 ** ADAM_STEP)
    delta = (-ADAM_LR) * (m_hat / (jnp.sqrt(v_hat) + ADAM_EPS) + ADAM_WD * w)
    return delta, m, v


def _adamw(name, g, w, m, v):
    rows, cols = w.shape
    rb = _row_block(rows)

    def body(g_ref, w_ref, m_ref, v_ref, d_ref, nm_ref, nv_ref):
        d_ref[...], nm_ref[...], nv_ref[...] = _adamw_math(w_ref[...], g_ref[...], m_ref[...], v_ref[...])

    blk = pl.BlockSpec((rb, cols), lambda r: (r, 0))
    return pl.pallas_call(
        body, name=name, grid=(rows // rb,), in_specs=[blk] * 4, out_specs=[blk] * 3,
        out_shape=[jax.ShapeDtypeStruct(w.shape, F32)] * 3, compiler_params=_params(),
    )(g, w, m, v)


def _adamw_small(grads, ws, ms, vs):
    n = len(grads)

    def body(*refs):
        g_refs, w_refs, m_refs, v_refs = refs[:n], refs[n:2 * n], refs[2 * n:3 * n], refs[3 * n:4 * n]
        outs = refs[4 * n:]
        for p in range(n):
            d, nm, nv = _adamw_math(w_refs[p][...], g_refs[p][...], m_refs[p][...], v_refs[p][...])
            outs[p][...] = d
            outs[n + p][...] = nm
            outs[2 * n + p][...] = nv

    vm = pl.BlockSpec(memory_space=pltpu.VMEM)
    shapes = [jax.ShapeDtypeStruct(w.shape, F32) for w in ws]
    out = pl.pallas_call(
        body, name="adamw_small", in_specs=[vm] * (4 * n), out_specs=[vm] * (3 * n), out_shape=shapes * 3,
    )(*grads, *ws, *ms, *vs)
    return out[:n], out[n:2 * n], out[2 * n:]


def _unstack_heads(w_st):
    per = HEAD_DIM // N_CHIPS
    return w_st.reshape(N_CHIPS, HEADS, per, HEAD_DIM).transpose(1, 0, 2, 3).reshape(HEADS, HEAD_DIM, HEAD_DIM)


def _stack_heads(w):
    per = HEAD_DIM // N_CHIPS
    return w.reshape(HEADS, N_CHIPS, per, HEAD_DIM).transpose(1, 0, 2, 3).reshape(N_CHIPS, HEADS * per, HEAD_DIM)


def kernel(x, norm_mix_g, w_in, conv_w, conv_b, w_rgate, b_rgate, w_igate, b_igate, lru_lambda, w_out_a, sgu_ln_g, sgu_ln_b, sgu_w_s, sgu_b_s, w_out_b, w_out, norm_mlp_g, w_up, w_down, norm_final_g, loss_target, m_norm_mix_g, m_w_in, m_conv_w, m_conv_b, m_w_rgate, m_b_rgate, m_w_igate, m_b_igate, m_lru_lambda, m_w_out_a, m_sgu_ln_g, m_sgu_ln_b, m_sgu_w_s, m_sgu_b_s, m_w_out_b, m_w_out, m_norm_mlp_g, m_w_up, m_w_down, m_norm_final_g, v_norm_mix_g, v_w_in, v_conv_w, v_conv_b, v_w_rgate, v_b_rgate, v_w_igate, v_b_igate, v_lru_lambda, v_w_out_a, v_sgu_ln_g, v_sgu_ln_b, v_sgu_w_s, v_sgu_b_s, v_w_out_b, v_w_out, v_norm_mlp_g, v_w_up, v_w_down, v_norm_final_g):
    chip = _chip_index(lax.axis_index("x"), lax.axis_index("y"))
    core = lax.axis_index("c")
    quarter_h = HEAD_DIM // N_CHIPS
    quarter_d = D_MODEL // N_CHIPS

    as_2d = lambda a: a.reshape(-1, a.shape[-1])
    big_w = [as_2d(w) for w in (w_in, w_rgate, w_igate, w_out_a, w_out_b, w_out, w_up, w_down)]
    big_m = [as_2d(w) for w in (m_w_in, m_w_rgate, m_w_igate, m_w_out_a, m_w_out_b, m_w_out, m_w_up, m_w_down)]
    big_v = [as_2d(w) for w in (v_w_in, v_w_rgate, v_w_igate, v_w_out_a, v_w_out_b, v_w_out, v_w_up, v_w_down)]

    packed = jnp.concatenate([conv_w[0], b_rgate[0], b_igate[0]], axis=1)
    packed = jnp.concatenate([packed, jnp.zeros_like(packed)], axis=0)
    s_in, s_r, s_i, s_oa, s_ob, s_out, s_up, s_down = [w.astype(BF16) for w in big_w]
    xs, target = x[0], loss_target[0]
    g3 = norm_final_g.reshape(1, D_MODEL)
    bias_s = jnp.broadcast_to(jnp.transpose(sgu_b_s[0])[:, :, None], (CHUNK, GROUPS, GROUP_DIM)).reshape(CHUNK, D_MODEL)
    core_arr = core.reshape(1).astype(jnp.int32)
    place = jnp.stack([chip, core]).astype(jnp.int32)
    quarter = lambda g: g.reshape(N_CHIPS, D_MODEL // N_CHIPS, D_MODEL)

    def pair_add(nm, g, from_sibling):
        return _pair_add("pair_add_" + nm, core_arr, g.reshape(N_CHIPS, 2, g.shape[1] // 2, g.shape[2]), from_sibling)

    def chip_sum(nm, pair, from_chips):
        return _chip_sum("chip_sum_" + nm, place, pair, from_chips)

    head, (packed_all,) = _comm_call("gather_head", [_gather_ici_job([s_in, s_r, s_i]), _gather_small_job(packed)])
    (w_in_st, wr_st, wi_st), = _comm_call("gather_head_pass", [_gather_pass_job(head)])
    pick = lambda lo, hi: packed_all[:, :HEADS, lo:hi].transpose(1, 0, 2).reshape(HEADS, -1)
    conv_w_full = pick(0, quarter_d)
    br_full = pick(quarter_d, quarter_d + quarter_h).reshape(1, D_MODEL)
    bi_full = pick(quarter_d + quarter_h, quarter_d + 2 * quarter_h).reshape(1, D_MODEL)
    wr, wi = _unstack_heads(wr_st), _unstack_heads(wi_st)
    lru = (conv_w_full, conv_b, wr, br_full, wi, bi_full, lru_lambda)
    sgu = (sgu_ln_g, sgu_ln_b, sgu_w_s[0], bias_s)

    (z, n1), (mid,) = _fwd_in(xs, norm_mix_g, w_in_st, jobs=[_gather_ici_job([s_oa, s_ob, s_out, s_up])])
    (h, ya), (mid, (down,)) = _fwd_lru(z, *lru, jobs=[_gather_pass_job(mid), _gather_ici_job([s_down])])
    yb, ((down,),) = _fwd_sgu(z, *sgu, jobs=[_gather_pass_job([down])])
    w_oa, w_ob, w_o = [w.reshape(D_MODEL, D_MODEL) for w in mid[:3]]
    w_up_st, w_dn = mid[3], down.reshape(D_FF, D_MODEL)
    (pa, pb, h1, n2), _ = _fwd_merge(ya, yb, z, xs, w_oa, w_ob, w_o, norm_mlp_g)
    (up, act, dh2, dh2b, loss_part, dg3), _ = _fwd_mlp(n2, h1, target, w_up_st, w_dn, g3)

    (dup, dh1, dg2), _ = _bwd_mlp(dh2, dh2b, up, h1, w_up_st, w_dn, norm_mlp_g)
    d_up, _ = _weight_grad("dw_up", n2, dup, N_CHIPS, False, True, D_MODEL)
    d_down, ((r_up,),) = _weight_grad("dw_down", act, dh2b, N_CHIPS, True, False, D_MODEL,
                                      jobs=[_pair_send_job([d_up])])
    p_up = pair_add("w_up", d_up, r_up)
    (dz, dya, dyb, merged, dpa, dpb, dh1b), ((r_down,), (q_up,)) = _bwd_merge(
        dh1, pa, pb, z, w_oa, w_ob, w_o, jobs=[_pair_send_job([d_down]), _chip_exchange_job([p_up])])
    p_down = pair_add("w_down", d_down, r_down)
    half_up = chip_sum("w_up", p_up, q_up)
    d_out, ((full_up,),) = _weight_grad("dw_out", merged, dh1b, 1, False, False, D_MODEL, jobs=[_share_job([half_up])])
    d_oa, _ = _weight_grad("dw_out_a", ya, dpa, 1, False, False, D_MODEL)
    d_ob, _ = _weight_grad("dw_out_b", yb, dpb, 1, False, False, D_MODEL)
    mids = [quarter(d_oa), quarter(d_ob), quarter(d_out)]
    (dz, dlg, dlb, dws, dbs), ((q_down,), r_mids) = _bwd_sgu(
        dz, dyb, z, *sgu, jobs=[_chip_exchange_job([p_down]), _pair_send_job(mids)])
    mid_names = ("w_out_a", "w_out_b", "w_out")
    p_mids = [pair_add(nm, g, r) for nm, g, r in zip(mid_names, mids, r_mids)]
    half_down = chip_sum("w_down", p_down, q_down)
    (dz, dcw, dcb, dwr, dbr, dwi, dbi, dlam), (q_mids, (full_down,)) = _bwd_lru(
        dz, dya, z, h, *lru, jobs=[_chip_exchange_job(p_mids), _share_job([half_down])])
    half_mids = [chip_sum(nm, p, q) for nm, p, q in zip(mid_names, p_mids, q_mids)]
    gates = [_stack_heads(dwr).astype(BF16), _stack_heads(dwi).astype(BF16)]
    d_in, (full_mids, r_gates) = _weight_grad(
        "dw_in", n1, dz, N_CHIPS, False, True, IN_SHARD, jobs=[_share_job(half_mids), _pair_send_job(gates)])
    (r_in,), = _comm_call("send_w_in", [_pair_send_job([d_in])])
    last_names = ("w_in", "w_rgate", "w_igate")
    p_last = [pair_add(nm, g, r) for nm, g, r in zip(last_names, [d_in] + gates, [r_in] + r_gates)]
    (grad_x, dg1), (q_last,) = _bwd_in(dz, xs, dh1, w_in_st, norm_mix_g, jobs=[_chip_exchange_job(p_last)])
    half_last = [chip_sum(nm, p, q) for nm, p, q in zip(last_names, p_last, q_last)]
    full_last, = _comm_call("share_last", [_share_job(half_last)])

    names = ("w_in", "w_rgate", "w_igate", "w_out_a", "w_out_b", "w_out", "w_up", "w_down")
    full = [f.reshape(w.shape) for f, w in zip(full_last + full_mids + [full_up, full_down], big_w)]
    big_out = [_adamw("adamw_" + nm, g, w, m, v) for nm, g, w, m, v in zip(names, full, big_w, big_m, big_v)]

    vec, ws_sum = _all_reduce_small(dg1, dcw, dcb, dbr, dbi, dlam, dlg, dlb, dg2, dg3, loss_part, dbs, dws)
    row = lambda r: vec[r:r + 1]
    shard = lambda a, width: lax.dynamic_slice_in_dim(a, chip * width, width, axis=1)
    g_small = dict(
        norm_mix_g=row(ROW_G1), conv_w=shard(vec[ROW_CW:ROW_CW + CONV_WIDTH], quarter_d), conv_b=row(ROW_CB),
        b_rgate=shard(row(ROW_BR).reshape(HEADS, HEAD_DIM), quarter_h),
        b_igate=shard(row(ROW_BI).reshape(HEADS, HEAD_DIM), quarter_h), lru_lambda=row(ROW_LAM),
        sgu_ln_g=row(ROW_LG), sgu_ln_b=row(ROW_LB),
        sgu_w_s=ws_sum.reshape(CHUNK, GROUPS, CHUNK).transpose(1, 0, 2).reshape(GROUPS * CHUNK, CHUNK),
        sgu_b_s=vec[ROW_BS:ROW_BS + GROUPS, 0:CHUNK], norm_mlp_g=row(ROW_G2), norm_final_g=row(ROW_G3))
    loss = vec[ROW_LOSS, 0]
    small_names = list(g_small)
    given = dict(
        norm_mix_g=(norm_mix_g, m_norm_mix_g, v_norm_mix_g), conv_w=(conv_w, m_conv_w, v_conv_w),
        conv_b=(conv_b, m_conv_b, v_conv_b), b_rgate=(b_rgate, m_b_rgate, v_b_rgate),
        b_igate=(b_igate, m_b_igate, v_b_igate), lru_lambda=(lru_lambda, m_lru_lambda, v_lru_lambda),
        sgu_ln_g=(sgu_ln_g, m_sgu_ln_g, v_sgu_ln_g), sgu_ln_b=(sgu_ln_b, m_sgu_ln_b, v_sgu_ln_b),
        sgu_w_s=(sgu_w_s, m_sgu_w_s, v_sgu_w_s), sgu_b_s=(sgu_b_s, m_sgu_b_s, v_sgu_b_s),
        norm_mlp_g=(norm_mlp_g, m_norm_mlp_g, v_norm_mlp_g), norm_final_g=(norm_final_g, m_norm_final_g, v_norm_final_g))
    g2d = [g_small[nm] for nm in small_names]
    to2d = lambda a, g: a.reshape(g.shape)
    d_s, m_s, v_s = _adamw_small(
        g2d, *[[to2d(given[nm][q], g) for nm, g in zip(small_names, g2d)] for q in range(3)])

    shapes = dict(
        norm_mix_g=norm_mix_g, w_in=w_in, conv_w=conv_w, conv_b=conv_b, w_rgate=w_rgate, b_rgate=b_rgate,
        w_igate=w_igate, b_igate=b_igate, lru_lambda=lru_lambda, w_out_a=w_out_a, sgu_ln_g=sgu_ln_g,
        sgu_ln_b=sgu_ln_b, sgu_w_s=sgu_w_s, sgu_b_s=sgu_b_s, w_out_b=w_out_b, w_out=w_out, norm_mlp_g=norm_mlp_g,
        w_up=w_up, w_down=w_down, norm_final_g=norm_final_g)
    grads, deltas, new_m, new_v = {}, {}, {}, {}
    for nm, g, (d, nmom, nvar) in zip(names, full, big_out):
        grads[nm], deltas[nm], new_m[nm], new_v[nm] = g, d, nmom, nvar
    for p, nm in enumerate(small_names):
        grads[nm], deltas[nm], new_m[nm], new_v[nm] = g2d[p], d_s[p], m_s[p], v_s[p]
    order = list(shapes)
    out = [loss, grad_x[None]]
    for group in (grads, deltas, new_m, new_v):
        out += [group[nm].reshape(shapes[nm].shape) for nm in order]
    return tuple(out)
```

```python
import functools

import jax
import jax.numpy as jnp
from jax import lax
from jax.experimental import pallas as pl
from jax.experimental.pallas import tpu as pltpu

F32 = jnp.float32
BF16 = jnp.bfloat16
MESH = pl.DeviceIdType.MESH

D_MODEL = 1024
D_IN = 6 * D_MODEL
D_FF = 4 * D_MODEL
N_CHIPS = 4
IN_SHARD = D_IN // N_CHIPS
HEADS = 4
HEAD_DIM = D_MODEL // HEADS
GROUPS = 4
GROUP_DIM = D_MODEL // GROUPS
CHUNK = 128
CONV_WIDTH = 4
LRU_C = 8.0
NORM_EPS = 1e-6
LN_EPS = 1e-5

ADAM_LR = 0.001
ADAM_B1 = 0.9
ADAM_B2 = 0.999
ADAM_EPS = 1e-08
ADAM_WD = 0.01
ADAM_STEP = 10

SUBLANES = 8
MM_TILE = 512
SEQ_TILE = 256
DW_TILE = 2048
VMEM_LIMIT_BYTES = 56 * 1024 * 1024

GELU_K0 = 0.7978845608028654
GELU_K1 = 0.044715


def _params(n_grid_axes=1):
    return pltpu.CompilerParams(
        dimension_semantics=("arbitrary",) * n_grid_axes, vmem_limit_bytes=VMEM_LIMIT_BYTES)


def _resident(shape):
    nd = len(shape)
    return pl.BlockSpec(shape, lambda *_: (0,) * nd, pipeline_mode=pl.Buffered(1))


def _const(shape):
    nd = len(shape)
    return pl.BlockSpec(shape, lambda *_: (0,) * nd)


def _dot(a, b):
    return jnp.dot(a, b, preferred_element_type=F32)


def _dot_nt(a, b):
    return lax.dot_general(a, b, (((1,), (1,)), ((), ())), preferred_element_type=F32)


def _dot_tn(a, b):
    return lax.dot_general(a, b, (((0,), (0,)), ((), ())), preferred_element_type=F32)


def _gelu(x):
    t = jnp.tanh(GELU_K0 * x * (1.0 + GELU_K1 * x * x))
    return 0.5 * x * (1.0 + t)


def _gelu_and_grad(x):
    x2 = x * x
    t = jnp.tanh(GELU_K0 * x * (1.0 + GELU_K1 * x2))
    g = 0.5 * x * (1.0 + t)
    dg = 0.5 * (1.0 + t) + 0.5 * x * (1.0 - t * t) * (GELU_K0 * (1.0 + 3.0 * GELU_K1 * x2))
    return g, dg


def _rms(x):
    r = lax.rsqrt(jnp.mean(x * x, axis=-1, keepdims=True) + NORM_EPS)
    return x * r, r


def _rms_bwd(dn, xhat, r):
    return r * (dn - xhat * jnp.mean(dn * xhat, axis=-1, keepdims=True))


def _col_sum(v):
    return jnp.sum(v, axis=0, keepdims=True)


def _shift_down(x, tail8, k):
    xs = pltpu.roll(x, k, 0)
    ts = pltpu.roll(tail8, k, 0)
    ridx = lax.broadcasted_iota(jnp.int32, tail8.shape, 0)
    head = jnp.where(ridx < k, ts, xs[0:SUBLANES])
    return jnp.concatenate([head, xs[SUBLANES:]], axis=0)


def _shift_up(x, head8, k):
    n = x.shape[0]
    xs = pltpu.roll(x, n - k, 0)
    hs = pltpu.roll(head8, SUBLANES - k, 0)
    ridx = lax.broadcasted_iota(jnp.int32, head8.shape, 0)
    last = jnp.where(ridx >= SUBLANES - k, hs, xs[n - SUBLANES:n])
    return jnp.concatenate([xs[:n - SUBLANES], last], axis=0)


def _scan_forward(a, b, carry):
    n = a.shape[0]
    sub = lax.broadcasted_iota(jnp.int32, a.shape, 0) & (SUBLANES - 1)
    for s in (1, 2, 4):
        a_s = pltpu.roll(a, s, 0)
        b_s = pltpu.roll(b, s, 0)
        m = sub >= s
        b = jnp.where(m, a * b_s + b, b)
        a = jnp.where(m, a * a_s, a)
    out = []
    for g in range(n // SUBLANES):
        rows = slice(g * SUBLANES, (g + 1) * SUBLANES)
        h = a[rows] * carry + b[rows]
        out.append(h)
        carry = h[SUBLANES - 1:SUBLANES]
    return jnp.concatenate(out, axis=0), carry


def _scan_backward(a, b, carry):
    n = a.shape[0]
    sub = lax.broadcasted_iota(jnp.int32, a.shape, 0) & (SUBLANES - 1)
    for s in (1, 2, 4):
        a_s = pltpu.roll(a, n - s, 0)
        b_s = pltpu.roll(b, n - s, 0)
        m = sub < SUBLANES - s
        b = jnp.where(m, a * b_s + b, b)
        a = jnp.where(m, a * a_s, a)
    out = [None] * (n // SUBLANES)
    for g in reversed(range(n // SUBLANES)):
        rows = slice(g * SUBLANES, (g + 1) * SUBLANES)
        h = a[rows] * carry + b[rows]
        out[g] = h
        carry = h[0:1]
    return jnp.concatenate(out, axis=0), carry


def _softplus_neg(lam):
    e = jnp.exp(-jnp.abs(lam))
    u = 1.0 + e
    log1p_e = jnp.where(u == 1.0, e, jnp.log(u) * (e / jnp.where(u == 1.0, 1.0, u - 1.0)))
    return jnp.maximum(-lam, 0.0) + log1p_e


def _one_minus_exp_neg(x):
    poly = x * (1.0 - 0.5 * x * (1.0 - (1.0 / 3.0) * x * (1.0 - 0.25 * x * (1.0 - 0.2 * x))))
    return jnp.where(x < 0.02, poly, 1.0 - jnp.exp(-x))


def _lru_gates(xa, tail8, cw_ref, cb_ref, wr_ref, br_ref, wi_ref, bi_ref, lam_ref):
    cw = cw_ref[...]
    xs = [xa] + [_shift_down(xa, tail8, k) for k in range(1, CONV_WIDTH)]
    xc = cb_ref[...] + cw[0:1] * xs[0]
    for k in range(1, CONV_WIDTH):
        xc = xc + cw[k:k + 1] * xs[k]
    xcb = xc.astype(BF16)
    pre_r, pre_i = [], []
    for h in range(HEADS):
        cols = slice(h * HEAD_DIM, (h + 1) * HEAD_DIM)
        pre_r.append(_dot(xcb[:, cols], wr_ref[h]))
        pre_i.append(_dot(xcb[:, cols], wi_ref[h]))
    r = jax.nn.sigmoid(jnp.concatenate(pre_r, axis=1) + br_ref[...])
    ig = jax.nn.sigmoid(jnp.concatenate(pre_i, axis=1) + bi_ref[...])
    sp = _softplus_neg(lam_ref[...])
    log_a = (-LRU_C) * sp * r
    a = jnp.exp(log_a)
    mult = jnp.sqrt(_one_minus_exp_neg(-2.0 * log_a))
    return xs, xc, xcb, r, ig, sp, a, mult


class _Job:
    def __init__(self, inputs, out_shape, n_sem, copies, aliases=None, n_local=0):
        self.inputs, self.out_shape, self.n_sem, self.copies = list(inputs), list(out_shape), n_sem, copies
        self.aliases, self.n_local = dict(aliases or {}), n_local


def _fused_call(body, jobs, *, name, grid, in_specs, out_specs, out_shape, scratch_shapes=(),
                input_output_aliases=None, compiler_params=None):
    single = not isinstance(out_shape, (list, tuple))
    out_specs = [out_specs] if single else list(out_specs)
    out_shape = [out_shape] if single else list(out_shape)
    n_scr = len(scratch_shapes)
    in_specs, scratch_shapes = list(in_specs), list(scratch_shapes)
    n_in, n_out = len(in_specs), len(out_shape)
    aliases = dict(input_output_aliases or {})
    in_at, out_at = [], []
    for job in jobs:
        in_at.append(len(in_specs))
        out_at.append(len(out_shape))
        for i, o in job.aliases.items():
            aliases[len(in_specs) + i] = len(out_shape) + o
        in_specs += [ANY] * len(job.inputs)
        out_specs += [ANY] * len(job.out_shape)
        out_shape += job.out_shape
        scratch_shapes += [pltpu.SemaphoreType.DMA((job.n_sem,)), pltpu.SemaphoreType.DMA((job.n_sem,)),
                           pltpu.SemaphoreType.DMA((max(job.n_local, 1),))]
    n_in_all, n_out_all = len(in_specs), len(out_shape)

    def full_body(*refs):
        ins, outs, scr = refs[:n_in_all], refs[n_in_all:n_in_all + n_out_all], refs[n_in_all + n_out_all:]

        def copies(q):
            job = jobs[q]
            return job.copies(ins[in_at[q]:in_at[q] + len(job.inputs)], outs[out_at[q]:out_at[q] + len(job.out_shape)],
                              *scr[n_scr + 3 * q:n_scr + 3 * q + 3])

        def start():
            for q in range(len(jobs)):
                sends, _, local = copies(q)
                for cp in local + sends:
                    cp.start()

        def finish():
            every = [copies(q) for q in range(len(jobs))]
            for _, arrivals, _ in every:
                for cp in arrivals:
                    cp.wait_recv()
            for sends, _, local in every:
                for cp in sends:
                    cp.wait_send()
                for cp in local:
                    cp.wait()

        if not grid:
            start()
            finish()
            return
        ids = [pl.program_id(a) for a in range(len(grid))]
        if jobs:
            pl.when(functools.reduce(jnp.logical_and, [i == 0 for i in ids]))(start)
        body(*ins[:n_in], *outs[:n_out], *scr[:n_scr])
        if jobs:
            pl.when(functools.reduce(jnp.logical_and, [i == g - 1 for i, g in zip(ids, grid)]))(finish)

    call = pl.pallas_call(
        full_body, name=name, grid=grid, in_specs=in_specs, out_specs=out_specs, out_shape=out_shape,
        scratch_shapes=scratch_shapes, input_output_aliases=aliases, compiler_params=compiler_params)

    def run(*args):
        res = call(*args, *[a for job in jobs for a in job.inputs])
        mine = res[0] if single else list(res[:n_out])
        return mine, [list(res[at:at + len(job.out_shape)]) for at, job in zip(out_at, jobs)]

    return run


def _fwd_in(x, g1, w_in_st, jobs=()):
    t = x.shape[0]

    def body(x_ref, g_ref, w_ref, z_ref, n_ref):
        xhat, _ = _rms(x_ref[...])
        n = (xhat * g_ref[...]).astype(BF16)
        n_ref[...] = n
        for k in range(N_CHIPS):
            z_ref[:, k * IN_SHARD:(k + 1) * IN_SHARD] = _dot(n, w_ref[k])

    return _fused_call(
        body, jobs, name="fwd_in", grid=(t // MM_TILE,),
        in_specs=[pl.BlockSpec((MM_TILE, D_MODEL), lambda i: (i, 0)), _const((1, D_MODEL)),
                  _resident((N_CHIPS, D_MODEL, IN_SHARD))],
        out_specs=[pl.BlockSpec((MM_TILE, D_IN), lambda i: (i, 0)),
                   pl.BlockSpec((MM_TILE, D_MODEL), lambda i: (i, 0))],
        out_shape=[jax.ShapeDtypeStruct((t, D_IN), F32), jax.ShapeDtypeStruct((t, D_MODEL), BF16)],
        compiler_params=_params(),
    )(x, g1, w_in_st)


def _fwd_lru(z, conv_w, conv_b, wr, br, wi, bi, lam, jobs=()):
    t = z.shape[0]

    def body(xa_ref, ga_ref, cw_ref, cb_ref, wr_ref, br_ref, wi_ref, bi_ref, lam_ref, h_ref, ya_ref,
             tail_ref, carry_ref):
        @pl.when(pl.program_id(0) == 0)
        def _():
            tail_ref[...] = jnp.zeros_like(tail_ref)
            carry_ref[...] = jnp.zeros_like(carry_ref)

        xa = xa_ref[...]
        _, xc, _, _, ig, _, a, mult = _lru_gates(
            xa, tail_ref[...], cw_ref, cb_ref, wr_ref, br_ref, wi_ref, bi_ref, lam_ref)
        tail_ref[...] = xa[SEQ_TILE - SUBLANES:]
        h, carry = _scan_forward(a, xc * ig * mult, carry_ref[...])
        carry_ref[...] = carry
        h_ref[...] = h
        ya_ref[...] = (h * _gelu(ga_ref[...])).astype(BF16)

    tile = lambda j: pl.BlockSpec((SEQ_TILE, D_MODEL), lambda i: (i, j))
    return _fused_call(
        body, jobs, name="fwd_lru", grid=(t // SEQ_TILE,),
        in_specs=[tile(0), tile(1), _const((CONV_WIDTH, D_MODEL)), _const((1, D_MODEL)),
                  _resident((HEADS, HEAD_DIM, HEAD_DIM)), _const((1, D_MODEL)),
                  _resident((HEADS, HEAD_DIM, HEAD_DIM)), _const((1, D_MODEL)), _const((1, D_MODEL))],
        out_specs=[tile(0), tile(0)],
        out_shape=[jax.ShapeDtypeStruct((t, D_MODEL), F32), jax.ShapeDtypeStruct((t, D_MODEL), BF16)],
        scratch_shapes=[pltpu.VMEM((SUBLANES, D_MODEL), F32), pltpu.VMEM((1, D_MODEL), F32)],
        compiler_params=_params(),
    )(z, z, conv_w, conv_b, wr, br, wi, bi, lam)


def _sgu_forward_parts(ub, vb, lg_ref, lb_ref):
    u, du = _gelu_and_grad(ub)
    vg, dvg = _gelu_and_grad(vb)
    mu = jnp.mean(vg, axis=-1, keepdims=True)
    d = vg - mu
    rstd = lax.rsqrt(jnp.mean(d * d, axis=-1, keepdims=True) + LN_EPS)
    vhat = d * rstd
    vn = (vhat * lg_ref[...] + lb_ref[...]).astype(BF16)
    return u, du, dvg, rstd, vhat, vn


def _causal_mask():
    rows = lax.broadcasted_iota(jnp.int32, (CHUNK, CHUNK), 0)
    cols = lax.broadcasted_iota(jnp.int32, (CHUNK, CHUNK), 1)
    return rows >= cols


def _fwd_sgu(z, ln_g, ln_b, w_s, bias_full, jobs=()):
    t = z.shape[0]

    def body(ub_ref, vb_ref, lg_ref, lb_ref, ws_ref, bias_ref, yb_ref):
        u, _, _, _, _, vn = _sgu_forward_parts(ub_ref[...], vb_ref[...], lg_ref, lb_ref)
        mask = _causal_mask()
        wm = [jnp.where(mask, ws_ref[g], 0.0).astype(BF16) for g in range(GROUPS)]
        for c in range(SEQ_TILE // CHUNK):
            rows = slice(c * CHUNK, (c + 1) * CHUNK)
            for g in range(GROUPS):
                cols = slice(g * GROUP_DIM, (g + 1) * GROUP_DIM)
                sp = _dot(wm[g], vn[rows, cols]) + bias_ref[:, cols]
                yb_ref[rows, cols] = (u[rows, cols] * sp).astype(BF16)

    tile = lambda j: pl.BlockSpec((SEQ_TILE, D_MODEL), lambda i: (i, j))
    return _fused_call(
        body, jobs, name="fwd_sgu", grid=(t // SEQ_TILE,),
        in_specs=[tile(2), tile(3), _const((1, D_MODEL)), _const((1, D_MODEL)),
                  _const((GROUPS, CHUNK, CHUNK)), _const((CHUNK, D_MODEL))],
        out_specs=tile(0),
        out_shape=jax.ShapeDtypeStruct((t, D_MODEL), BF16),
        compiler_params=_params(),
    )(z, z, ln_g, ln_b, w_s, bias_full)


def _fwd_merge(ya, yb, z, x, w_oa, w_ob, w_out, g2, jobs=()):
    t = x.shape[0]

    def body(ya_ref, yb_ref, m_ref, x_ref, woa_ref, wob_ref, wout_ref, g_ref, pa_ref, pb_ref, h1_ref, n2_ref):
        pa = _dot(ya_ref[...], woa_ref[...])
        pb = _dot(yb_ref[...], wob_ref[...])
        pa_ref[...] = pa
        pb_ref[...] = pb
        merged = jax.nn.sigmoid(m_ref[:, :D_MODEL]) * pa + jax.nn.sigmoid(m_ref[:, D_MODEL:]) * pb
        h1 = x_ref[...] + _dot(merged.astype(BF16), wout_ref[...])
        h1_ref[...] = h1
        xhat, _ = _rms(h1)
        n2_ref[...] = (xhat * g_ref[...]).astype(BF16)

    tile = pl.BlockSpec((MM_TILE, D_MODEL), lambda i: (i, 0))
    sq = _resident((D_MODEL, D_MODEL))
    return _fused_call(
        body, jobs, name="fwd_merge", grid=(t // MM_TILE,),
        in_specs=[tile, tile, pl.BlockSpec((MM_TILE, 2 * D_MODEL), lambda i: (i, 2)), tile, sq, sq, sq,
                  _const((1, D_MODEL))],
        out_specs=[tile, tile, tile, tile],
        out_shape=[jax.ShapeDtypeStruct((t, D_MODEL), F32)] * 3 + [jax.ShapeDtypeStruct((t, D_MODEL), BF16)],
        compiler_params=_params(),
    )(ya, yb, z, x, w_oa, w_ob, w_out, g2)


def _fwd_mlp(n2, h1, target, w_up_st, w_down, g3, jobs=()):
    t = n2.shape[0]

    def body(n2_ref, h1_ref, tgt_ref, wup_ref, wdown_ref, g_ref, up_ref, act_ref, dh2_ref, dh2b_ref, loss_ref,
             dg3_ref):
        @pl.when(pl.program_id(0) == 0)
        def _():
            loss_ref[...] = jnp.zeros_like(loss_ref)
            dg3_ref[...] = jnp.zeros_like(dg3_ref)

        n2 = n2_ref[...]
        h2 = h1_ref[...]
        for k in range(N_CHIPS):
            cols = slice(k * D_MODEL, (k + 1) * D_MODEL)
            up = _dot(n2, wup_ref[k])
            up_ref[:, cols] = up.astype(BF16)
            r = jnp.maximum(up, 0.0)
            act = (r * r).astype(BF16)
            act_ref[:, cols] = act
            h2 = h2 + _dot(act, wdown_ref[cols, :])
        xhat, r3 = _rms(h2)
        diff = xhat * g_ref[...] - tgt_ref[...]
        sq = jnp.sum(diff * diff, axis=1, keepdims=True)
        loss_ref[...] = loss_ref[...] + (0.5 / D_MODEL) * jnp.sum(sq, axis=0, keepdims=True)
        dy = diff * (1.0 / D_MODEL)
        dg3_ref[...] = dg3_ref[...] + _col_sum(dy * xhat)
        dh2 = _rms_bwd(dy * g_ref[...], xhat, r3)
        dh2_ref[...] = dh2
        dh2b_ref[...] = dh2.astype(BF16)

    tile = pl.BlockSpec((MM_TILE, D_MODEL), lambda i: (i, 0))
    wide = pl.BlockSpec((MM_TILE, D_FF), lambda i: (i, 0))
    return _fused_call(
        body, jobs, name="fwd_mlp", grid=(t // MM_TILE,),
        in_specs=[tile, tile, tile, _resident((N_CHIPS, D_MODEL, D_MODEL)), _resident((D_FF, D_MODEL)),
                  _const((1, D_MODEL))],
        out_specs=[wide, wide, tile, tile, _const((SUBLANES, 128)), _const((1, D_MODEL))],
        out_shape=[jax.ShapeDtypeStruct((t, D_FF), BF16), jax.ShapeDtypeStruct((t, D_FF), BF16),
                   jax.ShapeDtypeStruct((t, D_MODEL), F32), jax.ShapeDtypeStruct((t, D_MODEL), BF16),
                   jax.ShapeDtypeStruct((SUBLANES, 128), F32), jax.ShapeDtypeStruct((1, D_MODEL), F32)],
        compiler_params=_params(),
    )(n2, h1, target, w_up_st, w_down, g3)


def _bwd_mlp(dh2, dh2b, up, h1, w_up_st, w_down, g2, jobs=()):
    t = dh2.shape[0]

    def body(dh2_ref, dh2b_ref, up_ref, h1_ref, wup_ref, wdown_ref, g_ref, dup_ref, dh1_ref, dg2_ref):
        @pl.when(pl.program_id(0) == 0)
        def _():
            dg2_ref[...] = jnp.zeros_like(dg2_ref)

        dh2b = dh2b_ref[...]
        dn2 = jnp.zeros((MM_TILE, D_MODEL), F32)
        for k in range(N_CHIPS):
            cols = slice(k * D_MODEL, (k + 1) * D_MODEL)
            dact = _dot_nt(dh2b, wdown_ref[cols, :])
            dup = (dact * (2.0 * jnp.maximum(up_ref[:, cols].astype(F32), 0.0))).astype(BF16)
            dup_ref[:, cols] = dup
            dn2 = dn2 + _dot_nt(dup, wup_ref[k])
        xhat, r2 = _rms(h1_ref[...])
        dg2_ref[...] = dg2_ref[...] + _col_sum(dn2 * xhat)
        dh1_ref[...] = dh2_ref[...] + _rms_bwd(dn2 * g_ref[...], xhat, r2)

    tile = pl.BlockSpec((MM_TILE, D_MODEL), lambda i: (i, 0))
    wide = pl.BlockSpec((MM_TILE, D_FF), lambda i: (i, 0))
    return _fused_call(
        body, jobs, name="bwd_mlp", grid=(t // MM_TILE,),
        in_specs=[tile, tile, wide, tile, _resident((N_CHIPS, D_MODEL, D_MODEL)), _resident((D_FF, D_MODEL)),
                  _const((1, D_MODEL))],
        out_specs=[wide, tile, _const((1, D_MODEL))],
        out_shape=[jax.ShapeDtypeStruct((t, D_FF), BF16), jax.ShapeDtypeStruct((t, D_MODEL), F32),
                   jax.ShapeDtypeStruct((1, D_MODEL), F32)],
        compiler_params=_params(),
    )(dh2, dh2b, up, h1, w_up_st, w_down, g2)


def _bwd_merge(dh1, pa, pb, z, w_oa, w_ob, w_out, jobs=()):
    t = dh1.shape[0]

    def body(dh1_ref, pa_ref, pb_ref, m_ref, woa_ref, wob_ref, wout_ref, dz_ref, dya_ref, dyb_ref, mg_ref,
             dpa_ref, dpb_ref, dh1b_ref):
        dh1b = dh1_ref[...].astype(BF16)
        dh1b_ref[...] = dh1b
        dm = _dot_nt(dh1b, wout_ref[...])
        pa = pa_ref[...]
        pb = pb_ref[...]
        sa = jax.nn.sigmoid(m_ref[:, :D_MODEL])
        sb = jax.nn.sigmoid(m_ref[:, D_MODEL:])
        mg_ref[...] = (sa * pa + sb * pb).astype(BF16)
        dz_ref[:, :D_MODEL] = (dm * pa * sa * (1.0 - sa)).astype(BF16)
        dz_ref[:, D_MODEL:] = (dm * pb * sb * (1.0 - sb)).astype(BF16)
        dpa = (dm * sa).astype(BF16)
        dpb = (dm * sb).astype(BF16)
        dpa_ref[...] = dpa
        dpb_ref[...] = dpb
        dya_ref[...] = _dot_nt(dpa, woa_ref[...])
        dyb_ref[...] = _dot_nt(dpb, wob_ref[...])

    tile = pl.BlockSpec((MM_TILE, D_MODEL), lambda i: (i, 0))
    pair = pl.BlockSpec((MM_TILE, 2 * D_MODEL), lambda i: (i, 2))
    sq = _resident((D_MODEL, D_MODEL))
    act_bf = jax.ShapeDtypeStruct((t, D_MODEL), BF16)
    return _fused_call(
        body, jobs, name="bwd_merge", grid=(t // MM_TILE,),
        in_specs=[tile, tile, tile, pair, sq, sq, sq],
        out_specs=[pair, tile, tile, tile, tile, tile, tile],
        out_shape=[jax.ShapeDtypeStruct((t, D_IN), BF16), jax.ShapeDtypeStruct((t, D_MODEL), F32),
                   jax.ShapeDtypeStruct((t, D_MODEL), F32), act_bf, act_bf, act_bf, act_bf],
        compiler_params=_params(),
    )(dh1, pa, pb, z, w_oa, w_ob, w_out)


def _bwd_sgu(dz, dyb, z, ln_g, ln_b, w_s, bias_full, jobs=()):
    t = dyb.shape[0]
    n_tiles = t // SEQ_TILE

    def body(dz_any, dyb_ref, ub_ref, vb_ref, lg_ref, lb_ref, ws_ref, bias_ref, dz_ref, dlg_ref, dlb_ref, dws_ref,
             dbs_ref, dvn_ref, dsp_acc):
        del dz_any
        i = pl.program_id(0)

        @pl.when(i == 0)
        def _():
            dlg_ref[...] = jnp.zeros_like(dlg_ref)
            dlb_ref[...] = jnp.zeros_like(dlb_ref)
            dws_ref[...] = jnp.zeros_like(dws_ref)
            dsp_acc[...] = jnp.zeros_like(dsp_acc)

        u, du, dvg, rstd, vhat, vn = _sgu_forward_parts(ub_ref[...], vb_ref[...], lg_ref, lb_ref)
        dyb = dyb_ref[...]
        mask = _causal_mask()
        wm = [jnp.where(mask, ws_ref[g], 0.0).astype(BF16) for g in range(GROUPS)]
        for c in range(SEQ_TILE // CHUNK):
            rows = slice(c * CHUNK, (c + 1) * CHUNK)
            for g in range(GROUPS):
                cols = slice(g * GROUP_DIM, (g + 1) * GROUP_DIM)
                vn_blk = vn[rows, cols]
                sp = _dot(wm[g], vn_blk) + bias_ref[:, cols]
                dyb_blk = dyb[rows, cols]
                dz_ref[rows, cols] = (dyb_blk * sp * du[rows, cols]).astype(BF16)
                dsp = dyb_blk * u[rows, cols]
                dsp_acc[:, cols] = dsp_acc[:, cols] + dsp
                dspb = dsp.astype(BF16)
                dvn_ref[rows, cols] = _dot_tn(wm[g], dspb)
                wcols = slice(g * CHUNK, (g + 1) * CHUNK)
                dws_ref[:, wcols] = dws_ref[:, wcols] + jnp.where(mask, _dot_nt(dspb, vn_blk), 0.0)
        dvn = dvn_ref[...]
        dlg_ref[...] = dlg_ref[...] + _col_sum(dvn * vhat)
        dlb_ref[...] = dlb_ref[...] + _col_sum(dvn)
        dvhat = dvn * lg_ref[...]
        dvgel = rstd * (dvhat - jnp.mean(dvhat, axis=-1, keepdims=True)
                        - vhat * jnp.mean(dvhat * vhat, axis=-1, keepdims=True))
        dz_ref[:, D_MODEL:] = (dvgel * dvg).astype(BF16)

        @pl.when(i == n_tiles - 1)
        def _():
            lane = lax.broadcasted_iota(jnp.int32, (CHUNK, 128), 1)
            out = jnp.zeros((CHUNK, 128), F32)
            for g in range(GROUPS):
                s = jnp.sum(dsp_acc[:, g * GROUP_DIM:(g + 1) * GROUP_DIM], axis=1, keepdims=True)
                out = out + jnp.where(lane == g, s, 0.0)
            dbs_ref[...] = out

    tile = lambda j: pl.BlockSpec((SEQ_TILE, D_MODEL), lambda i: (i, j))
    return _fused_call(
        body, jobs, name="bwd_sgu", grid=(n_tiles,),
        in_specs=[pl.BlockSpec(memory_space=pl.ANY), tile(0), tile(2), tile(3), _const((1, D_MODEL)),
                  _const((1, D_MODEL)), _const((GROUPS, CHUNK, CHUNK)), _const((CHUNK, D_MODEL))],
        out_specs=[pl.BlockSpec((SEQ_TILE, 2 * D_MODEL), lambda i: (i, 1)), _const((1, D_MODEL)),
                   _const((1, D_MODEL)), _const((CHUNK, GROUPS * CHUNK)), _const((CHUNK, 128))],
        out_shape=[jax.ShapeDtypeStruct((t, D_IN), BF16), jax.ShapeDtypeStruct((1, D_MODEL), F32),
                   jax.ShapeDtypeStruct((1, D_MODEL), F32), jax.ShapeDtypeStruct((CHUNK, GROUPS * CHUNK), F32),
                   jax.ShapeDtypeStruct((CHUNK, 128), F32)],
        scratch_shapes=[pltpu.VMEM((SEQ_TILE, D_MODEL), F32), pltpu.VMEM((CHUNK, D_MODEL), F32)],
        input_output_aliases={0: 0},
        compiler_params=_params(),
    )(dz, dyb, z, z, ln_g, ln_b, w_s, bias_full)


def _bwd_lru(dz, dya, z, h, conv_w, conv_b, wr, br, wi, bi, lam, jobs=()):
    t = dya.shape[0]
    n_tiles = t // SEQ_TILE
    per_tile = SEQ_TILE // SUBLANES

    def body(dz_any, dya_ref, xa_ref, xa_prev_ref, ga_ref, h_ref, h_prev_ref, cw_ref, cb_ref, wr_ref, br_ref, wi_ref,
             bi_ref, lam_ref, dz_ref, dcw_ref, dcb_ref, dwr_ref, dbr_ref, dwi_ref, dbi_ref, dlam_ref, lam_carry,
             dxc_head):
        del dz_any
        i = pl.program_id(0)

        @pl.when(i == 0)
        def _():
            for ref in (dcw_ref, dcb_ref, dwr_ref, dbr_ref, dwi_ref, dbi_ref, dlam_ref, lam_carry, dxc_head):
                ref[...] = jnp.zeros_like(ref)

        first_tile = i == n_tiles - 1
        xa = xa_ref[...]
        tail = jnp.where(first_tile, 0.0, xa_prev_ref[...])
        h_tail = jnp.where(first_tile, 0.0, h_prev_ref[...])
        xs, xc, xcb, r, ig, sp, a, mult = _lru_gates(xa, tail, cw_ref, cb_ref, wr_ref, br_ref, wi_ref, bi_ref, lam_ref)
        h = h_ref[...]
        h_prev = _shift_down(h, h_tail, 1)
        dya = dya_ref[...]
        gg, dgg = _gelu_and_grad(ga_ref[...])
        dz_ref[:, D_MODEL:] = (dya * h * dgg).astype(BF16)
        ones = jnp.ones((SUBLANES, D_MODEL), F32)
        lam_t, lam_first = _scan_backward(_shift_up(a, ones, 1), dya * gg, lam_carry[...])
        lam_carry[...] = a[0:1] * lam_first
        dmult = lam_t * xc * ig
        dla = lam_t * h_prev * a - dmult * (a * a) / mult
        dr = dla * ((-LRU_C) * sp)
        dlam_ref[...] = dlam_ref[...] + _col_sum(dla * r) * (LRU_C * jax.nn.sigmoid(-lam_ref[...]))
        dpr = dr * r * (1.0 - r)
        dpi = lam_t * xc * mult * ig * (1.0 - ig)
        dbr_ref[...] = dbr_ref[...] + _col_sum(dpr)
        dbi_ref[...] = dbi_ref[...] + _col_sum(dpi)
        dprb = dpr.astype(BF16)
        dpib = dpi.astype(BF16)
        dxc_gate = []
        for hd in range(HEADS):
            cols = slice(hd * HEAD_DIM, (hd + 1) * HEAD_DIM)
            dxc_gate.append(_dot_nt(dprb[:, cols], wr_ref[hd]) + _dot_nt(dpib[:, cols], wi_ref[hd]))
            dwr_ref[hd] = dwr_ref[hd] + _dot_tn(xcb[:, cols], dprb[:, cols])
            dwi_ref[hd] = dwi_ref[hd] + _dot_tn(xcb[:, cols], dpib[:, cols])
        dxc = lam_t * ig * mult + jnp.concatenate(dxc_gate, axis=1)
        dcb_ref[...] = dcb_ref[...] + _col_sum(dxc)
        cw = cw_ref[...]
        head = dxc_head[...]
        dxa = cw[0:1] * dxc
        for k in range(CONV_WIDTH):
            dcw_ref[k:k + 1, :] = dcw_ref[k:k + 1, :] + _col_sum(dxc * xs[k])
            if k:
                dxa = dxa + cw[k:k + 1] * _shift_up(dxc, head, k)
        dxc_head[...] = dxc[0:SUBLANES]
        dz_ref[:, :D_MODEL] = dxa.astype(BF16)

    rev = lambda i: n_tiles - 1 - i
    tile = lambda j: pl.BlockSpec((SEQ_TILE, D_MODEL), lambda i: (rev(i), j))
    prev8 = pl.BlockSpec((SUBLANES, D_MODEL), lambda i: (jnp.maximum(rev(i) * per_tile - 1, 0), 0))
    vec = _const((1, D_MODEL))
    gate_w = _resident((HEADS, HEAD_DIM, HEAD_DIM))
    vec_shape = jax.ShapeDtypeStruct((1, D_MODEL), F32)
    gate_shape = jax.ShapeDtypeStruct((HEADS, HEAD_DIM, HEAD_DIM), F32)
    return _fused_call(
        body, jobs, name="bwd_lru", grid=(n_tiles,),
        in_specs=[pl.BlockSpec(memory_space=pl.ANY), tile(0), tile(0), prev8, tile(1), tile(0), prev8,
                  _const((CONV_WIDTH, D_MODEL)), vec, gate_w, vec, gate_w, vec, vec],
        out_specs=[pl.BlockSpec((SEQ_TILE, 2 * D_MODEL), lambda i: (rev(i), 0)), _const((SUBLANES, D_MODEL)), vec,
                   _const((HEADS, HEAD_DIM, HEAD_DIM)), vec, _const((HEADS, HEAD_DIM, HEAD_DIM)), vec, vec],
        out_shape=[jax.ShapeDtypeStruct((t, D_IN), BF16), jax.ShapeDtypeStruct((SUBLANES, D_MODEL), F32), vec_shape,
                   gate_shape, vec_shape, gate_shape, vec_shape, vec_shape],
        scratch_shapes=[pltpu.VMEM((1, D_MODEL), F32), pltpu.VMEM((SUBLANES, D_MODEL), F32)],
        input_output_aliases={0: 0},
        compiler_params=_params(),
    )(dz, dya, z, z, z, h, h, conv_w, conv_b, wr, br, wi, bi, lam)


def _bwd_in(dz, x, dh1, w_in_st, g1, jobs=()):
    t = x.shape[0]

    def body(dz_ref, x_ref, dh1_ref, w_ref, g_ref, dx_ref, dg1_ref):
        @pl.when(pl.program_id(0) == 0)
        def _():
            dg1_ref[...] = jnp.zeros_like(dg1_ref)

        dn1 = jnp.zeros((MM_TILE, D_MODEL), F32)
        for k in range(N_CHIPS):
            dn1 = dn1 + _dot_nt(dz_ref[:, k * IN_SHARD:(k + 1) * IN_SHARD], w_ref[k])
        xhat, r1 = _rms(x_ref[...])
        dg1_ref[...] = dg1_ref[...] + _col_sum(dn1 * xhat)
        dx_ref[...] = dh1_ref[...] + _rms_bwd(dn1 * g_ref[...], xhat, r1)

    tile = pl.BlockSpec((MM_TILE, D_MODEL), lambda i: (i, 0))
    return _fused_call(
        body, jobs, name="bwd_in", grid=(t // MM_TILE,),
        in_specs=[pl.BlockSpec((MM_TILE, D_IN), lambda i: (i, 0)), tile, tile,
                  _resident((N_CHIPS, D_MODEL, IN_SHARD)), _const((1, D_MODEL))],
        out_specs=[tile, _const((1, D_MODEL))],
        out_shape=[jax.ShapeDtypeStruct((t, D_MODEL), F32), jax.ShapeDtypeStruct((1, D_MODEL), F32)],
        compiler_params=_params(),
    )(dz, x, dh1, w_in_st, g1)


def _weight_grad(name, a, b, n_blocks, a_varies, b_varies, width, jobs=()):
    t = a.shape[0]
    rows = min(DW_TILE, t)
    n_t = t // rows

    def body(a_ref, b_ref, o_ref, acc_ref):
        s = pl.program_id(1)
        part = _dot_tn(a_ref[...], b_ref[...])

        @pl.when(s == 0)
        def _():
            acc_ref[...] = part

        @pl.when(s > 0)
        def _():
            acc_ref[...] = acc_ref[...] + part

        @pl.when(s == n_t - 1)
        def _():
            o_ref[...] = acc_ref[...].astype(BF16)

    return _fused_call(
        body, jobs, name=name, grid=(n_blocks, n_t),
        in_specs=[pl.BlockSpec((rows, D_MODEL), (lambda j, s: (s, j)) if a_varies else (lambda j, s: (s, 0))),
                  pl.BlockSpec((rows, width), (lambda j, s: (s, j)) if b_varies else (lambda j, s: (s, 0)))],
        out_specs=pl.BlockSpec((None, D_MODEL, width), lambda j, s: (j, 0, 0)),
        out_shape=jax.ShapeDtypeStruct((n_blocks, D_MODEL, width), BF16),
        scratch_shapes=[pltpu.VMEM((D_MODEL, width), F32)],
        compiler_params=_params(2),
    )(a, b)


def _place():
    x, y, c = lax.axis_index("x"), lax.axis_index("y"), lax.axis_index("c")
    other_chips = [(1 - x, y), (x, 1 - y), (1 - x, 1 - y)]
    return x, y, c, other_chips


def _chip_index(px, py):
    return 2 * px + py


ANY = pl.BlockSpec(memory_space=pl.ANY)


def _comm_call(name, jobs):
    return _fused_call(None, jobs, name=name, grid=(), in_specs=[], out_specs=[], out_shape=[])()[1]


def _gather_ici_job(shards):
    n = len(shards)
    halves = [s.shape[0] // 2 for s in shards]

    def copies(ins, outs, send, recv, local):
        x, y, c, chips = _place()
        me, sibling = (x, y, c), (x, y, 1 - c)

        def block(w, place):
            return outs[w].at[_chip_index(place[0], place[1]), pl.ds(place[2] * halves[w], halves[w]), :]

        def copy(w, k, blk, to, src=None):
            return pltpu.make_async_remote_copy(
                src_ref=block(w, blk) if src is None else src, dst_ref=block(w, blk),
                send_sem=send.at[4 * w + k], recv_sem=recv.at[4 * w + k], device_id=to, device_id_type=MESH)

        sends, arrivals, own = [], [], []
        for w in range(n):
            src = ins[w].at[pl.ds(c * halves[w], halves[w]), :]
            own.append(pltpu.make_async_copy(src, block(w, me), local.at[w]))
            sends.append(copy(w, 0, me, sibling, src))
            arrivals.append(copy(w, 0, sibling, me))
            for j, chip in enumerate(chips):
                sends.append(copy(w, 1 + j, me, (*chip, c), src))
                arrivals.append(copy(w, 1 + j, (*chip, c), me))
        return sends, arrivals, own

    return _Job(shards, [jax.ShapeDtypeStruct((N_CHIPS,) + s.shape, s.dtype) for s in shards], 4 * n, copies,
                n_local=n)


def _gather_pass_job(stacked):
    n = len(stacked)
    halves = [s.shape[1] // 2 for s in stacked]

    def copies(ins, outs, send, recv, local):
        del ins, local
        x, y, c, chips = _place()

        def copy(w, j, chip, pc, to):
            blk = outs[w].at[_chip_index(*chip), pl.ds(pc * halves[w], halves[w]), :]
            return pltpu.make_async_remote_copy(
                src_ref=blk, dst_ref=blk, send_sem=send.at[3 * w + j], recv_sem=recv.at[3 * w + j], device_id=to,
                device_id_type=MESH)

        sends = [copy(w, j, chip, c, (x, y, 1 - c)) for w in range(n) for j, chip in enumerate(chips)]
        arrivals = [copy(w, j, chip, 1 - c, (x, y, c)) for w in range(n) for j, chip in enumerate(chips)]
        return sends, arrivals, []

    return _Job(stacked, [jax.ShapeDtypeStruct(s.shape, s.dtype) for s in stacked], 3 * n, copies,
                aliases={w: w for w in range(n)})


def _gather_small_job(block):
    def copies(ins, outs, send, recv, local):
        x, y, c, chips = _place()

        def copy(j, chip_from, to):
            return pltpu.make_async_remote_copy(
                src_ref=ins[0], dst_ref=outs[0].at[_chip_index(*chip_from)], send_sem=send.at[j],
                recv_sem=recv.at[j], device_id=to, device_id_type=MESH)

        own = [pltpu.make_async_copy(ins[0], outs[0].at[_chip_index(x, y)], local.at[0])]
        sends = [copy(j, (x, y), (*chip, c)) for j, chip in enumerate(chips)]
        arrivals = [copy(j, chip, (x, y, c)) for j, chip in enumerate(chips)]
        return sends, arrivals, own

    return _Job([block], [jax.ShapeDtypeStruct((N_CHIPS,) + block.shape, block.dtype)], 3, copies, n_local=1)


def _pair_send_job(grads):
    n = len(grads)
    halves = [g.shape[1] // 2 for g in grads]

    def copies(ins, outs, send, recv, local):
        del local
        x, y, c, _ = _place()
        sends = [pltpu.make_async_remote_copy(
            src_ref=ins[w].at[:, pl.ds((1 - c) * halves[w], halves[w]), :], dst_ref=outs[w], send_sem=send.at[w],
            recv_sem=recv.at[w], device_id=(x, y, 1 - c), device_id_type=MESH) for w in range(n)]
        return sends, sends, []

    return _Job(grads, [jax.ShapeDtypeStruct((N_CHIPS, h, g.shape[2]), g.dtype) for g, h in zip(grads, halves)], n,
                copies)


def _row_block(rows, limit=256):
    return min(rows, limit)


def _pair_add(name, core, mine, theirs):
    _, _, h, cols = mine.shape
    rb = _row_block(h, 512)

    def body(core_ref, a_ref, b_ref, o_ref):
        del core_ref
        o_ref[...] = (a_ref[...].astype(F32) + b_ref[...].astype(F32)).astype(BF16)

    return pl.pallas_call(
        body, name=name,
        grid_spec=pltpu.PrefetchScalarGridSpec(
            num_scalar_prefetch=1, grid=(N_CHIPS, h // rb),
            in_specs=[pl.BlockSpec((None, None, rb, cols), lambda k, r, core_ref: (k, core_ref[0], r, 0)),
                      pl.BlockSpec((None, rb, cols), lambda k, r, core_ref: (k, r, 0))],
            out_specs=pl.BlockSpec((None, rb, cols), lambda k, r, core_ref: (k, r, 0))),
        out_shape=jax.ShapeDtypeStruct(theirs.shape, BF16),
        compiler_params=_params(2),
    )(core, mine, theirs)


def _chip_exchange_job(sums):
    n = len(sums)

    def copies(ins, outs, send, recv, local):
        del local
        _, _, c, chips = _place()
        sends = [pltpu.make_async_remote_copy(
            src_ref=ins[w].at[_chip_index(*chip)], dst_ref=outs[w].at[j], send_sem=send.at[3 * w + j],
            recv_sem=recv.at[3 * w + j], device_id=(*chip, c), device_id_type=MESH)
            for w in range(n) for j, chip in enumerate(chips)]
        return sends, sends, []

    return _Job(sums, [jax.ShapeDtypeStruct((N_CHIPS - 1,) + s.shape[1:], s.dtype) for s in sums], 3 * n, copies)


def _chip_sum(name, place, mine, theirs):
    _, h, cols = mine.shape
    rb = _row_block(h, 512)

    def body(place_ref, p_ref, q_ref, o_ref):
        del place_ref
        acc = p_ref[...].astype(F32)
        for j in range(N_CHIPS - 1):
            acc = acc + q_ref[j].astype(F32)
        o_ref[...] = acc

    return pl.pallas_call(
        body, name=name,
        grid_spec=pltpu.PrefetchScalarGridSpec(
            num_scalar_prefetch=1, grid=(h // rb,),
            in_specs=[pl.BlockSpec((None, rb, cols), lambda r, place_ref: (place_ref[0], r, 0)),
                      pl.BlockSpec((N_CHIPS - 1, rb, cols), lambda r, place_ref: (0, r, 0))],
            out_specs=pl.BlockSpec((None, rb, cols), lambda r, place_ref: (place_ref[1], r, 0))),
        out_shape=jax.ShapeDtypeStruct((2, h, cols), F32),
        compiler_params=_params(),
    )(place, mine, theirs)


def _share_job(bufs):
    n = len(bufs)

    def copies(ins, outs, send, recv, local):
        del ins, local
        x, y, c, _ = _place()

        def copy(w, half):
            return pltpu.make_async_remote_copy(
                src_ref=outs[w].at[half], dst_ref=outs[w].at[half], send_sem=send.at[w], recv_sem=recv.at[w],
                device_id=(x, y, 1 - c), device_id_type=MESH)

        return [copy(w, c) for w in range(n)], [copy(w, 1 - c) for w in range(n)], []

    return _Job(bufs, [jax.ShapeDtypeStruct(b.shape, b.dtype) for b in bufs], n, copies,
                aliases={w: w for w in range(n)})


SMALL_ROWS = 24
ROW_G1, ROW_CW, ROW_CB, ROW_BR, ROW_BI, ROW_LAM, ROW_LG, ROW_LB, ROW_G2, ROW_G3, ROW_LOSS, ROW_BS = (
    0, 1, 5, 6, 7, 8, 9, 10, 11, 12, 13, 16)
N_DEV = 8


def _all_reduce_small(dg1, dcw, dcb, dbr, dbi, dlam, dlg, dlb, dg2, dg3, loss, dbs, dws):
    def body(dg1_ref, dcw_ref, dcb_ref, dbr_ref, dbi_ref, dlam_ref, dlg_ref, dlb_ref, dg2_ref, dg3_ref, loss_ref,
             dbs_ref, dws_ref, vec_out, ws_out, vec_all, ws_all, send, recv):
        x, y, c, chips = _place()
        me, sibling = (x, y, c), (x, y, 1 - c)

        def dev(px, py, pc):
            return 4 * px + 2 * py + pc

        mine = vec_all.at[dev(*me)]
        mine[...] = jnp.zeros((SMALL_ROWS, D_MODEL), F32)
        for row, ref in ((ROW_G1, dg1_ref), (ROW_CB, dcb_ref), (ROW_BR, dbr_ref), (ROW_BI, dbi_ref),
                         (ROW_LAM, dlam_ref), (ROW_LG, dlg_ref), (ROW_LB, dlb_ref), (ROW_G2, dg2_ref),
                         (ROW_G3, dg3_ref)):
            mine[row:row + 1, :] = ref[...]
        mine[ROW_CW:ROW_CW + CONV_WIDTH, :] = dcw_ref[0:CONV_WIDTH, :]
        mine[ROW_LOSS:ROW_LOSS + 1, 0:128] = loss_ref[0:1, :]
        mine[ROW_BS:ROW_BS + GROUPS, 0:128] = jnp.transpose(dbs_ref[...])[0:GROUPS, :]
        ws_all.at[dev(*me)][...] = dws_ref[...]

        def copies(k, blk, to):
            d = dev(*blk)
            return [pltpu.make_async_remote_copy(
                src_ref=buf.at[d], dst_ref=buf.at[d], send_sem=send.at[2 * k + q], recv_sem=recv.at[2 * k + q],
                device_id=to, device_id_type=MESH) for q, buf in enumerate((vec_all, ws_all))]

        first = copies(0, me, sibling)
        for j, chip in enumerate(chips):
            first += copies(1 + j, me, (*chip, c))
        for cp in first:
            cp.start()
        passed = []
        for j, chip in enumerate(chips):
            for cp in copies(1 + j, (*chip, c), me):
                cp.wait_recv()
            fwd = copies(4 + j, (*chip, c), sibling)
            for cp in fwd:
                cp.start()
            passed += fwd
        for cp in copies(0, sibling, me):
            cp.wait_recv()
        for j, chip in enumerate(chips):
            for cp in copies(4 + j, (*chip, 1 - c), me):
                cp.wait_recv()
        for cp in first + passed:
            cp.wait_send()
        vec = vec_all[0]
        ws = ws_all[0]
        for d in range(1, N_DEV):
            vec = vec + vec_all[d]
            ws = ws + ws_all[d]
        vec_out[...] = vec
        ws_out[...] = ws

    vm = pl.BlockSpec(memory_space=pltpu.VMEM)
    return pl.pallas_call(
        body, name="all_reduce_small", in_specs=[vm] * 13, out_specs=[vm, vm],
        out_shape=[jax.ShapeDtypeStruct((SMALL_ROWS, D_MODEL), F32),
                   jax.ShapeDtypeStruct((CHUNK, GROUPS * CHUNK), F32)],
        scratch_shapes=[pltpu.VMEM((N_DEV, SMALL_ROWS, D_MODEL), F32), pltpu.VMEM((N_DEV, CHUNK, GROUPS * CHUNK), F32),
                        pltpu.SemaphoreType.DMA((14,)), pltpu.SemaphoreType.DMA((14,))],
        compiler_params=pltpu.CompilerParams(vmem_limit_bytes=VMEM_LIMIT_BYTES),
    )(dg1, dcw, dcb, dbr, dbi, dlam, dlg, dlb, dg2, dg3, loss, dbs, dws)


def _adamw_math(w, g, m, v):
    m = ADAM_B1 * m + (1.0 - ADAM_B1) * g
    v = ADAM_B2 * v + (1.0 - ADAM_B2) * (g * g)
    m_hat = m / (1.0 - ADAM_B1 ** ADAM_STEP)
    v_hat = v / (1.0 - ADAM_B2 ** ADAM_STEP)
    delta = (-ADAM_LR) * (m_hat / (jnp.sqrt(v_hat) + ADAM_EPS) + ADAM_WD * w)
    return delta, m, v


def _adamw(name, g, w, m, v):
    rows, cols = w.shape
    rb = _row_block(rows)

    def body(g_ref, w_ref, m_ref, v_ref, d_ref, nm_ref, nv_ref):
        d_ref[...], nm_ref[...], nv_ref[...] = _adamw_math(w_ref[...], g_ref[...], m_ref[...], v_ref[...])

    blk = pl.BlockSpec((rb, cols), lambda r: (r, 0))
    return pl.pallas_call(
        body, name=name, grid=(rows // rb,), in_specs=[blk] * 4, out_specs=[blk] * 3,
        out_shape=[jax.ShapeDtypeStruct(w.shape, F32)] * 3, compiler_params=_params(),
    )(g, w, m, v)


def _adamw_small(grads, ws, ms, vs):
    n = len(grads)

    def body(*refs):
        g_refs, w_refs, m_refs, v_refs = refs[:n], refs[n:2 * n], refs[2 * n:3 * n], refs[3 * n:4 * n]
        outs = refs[4 * n:]
        for p in range(n):
            d, nm, nv = _adamw_math(w_refs[p][...], g_refs[p][...], m_refs[p][...], v_refs[p][...])
            outs[p][...] = d
            outs[n + p][...] = nm
            outs[2 * n + p][...] = nv

    vm = pl.BlockSpec(memory_space=pltpu.VMEM)
    shapes = [jax.ShapeDtypeStruct(w.shape, F32) for w in ws]
    out = pl.pallas_call(
        body, name="adamw_small", in_specs=[vm] * (4 * n), out_specs=[vm] * (3 * n), out_shape=shapes * 3,
    )(*grads, *ws, *ms, *vs)
    return out[:n], out[n:2 * n], out[2 * n:]


def _unstack_heads(w_st):
    per = HEAD_DIM // N_CHIPS
    return w_st.reshape(N_CHIPS, HEADS, per, HEAD_DIM).transpose(1, 0, 2, 3).reshape(HEADS, HEAD_DIM, HEAD_DIM)


def _stack_heads(w):
    per = HEAD_DIM // N_CHIPS
    return w.reshape(HEADS, N_CHIPS, per, HEAD_DIM).transpose(1, 0, 2, 3).reshape(N_CHIPS, HEADS * per, HEAD_DIM)


def kernel(x, norm_mix_g, w_in, conv_w, conv_b, w_rgate, b_rgate, w_igate, b_igate, lru_lambda, w_out_a, sgu_ln_g, sgu_ln_b, sgu_w_s, sgu_b_s, w_out_b, w_out, norm_mlp_g, w_up, w_down, norm_final_g, loss_target, m_norm_mix_g, m_w_in, m_conv_w, m_conv_b, m_w_rgate, m_b_rgate, m_w_igate, m_b_igate, m_lru_lambda, m_w_out_a, m_sgu_ln_g, m_sgu_ln_b, m_sgu_w_s, m_sgu_b_s, m_w_out_b, m_w_out, m_norm_mlp_g, m_w_up, m_w_down, m_norm_final_g, v_norm_mix_g, v_w_in, v_conv_w, v_conv_b, v_w_rgate, v_b_rgate, v_w_igate, v_b_igate, v_lru_lambda, v_w_out_a, v_sgu_ln_g, v_sgu_ln_b, v_sgu_w_s, v_sgu_b_s, v_w_out_b, v_w_out, v_norm_mlp_g, v_w_up, v_w_down, v_norm_final_g):
    chip = _chip_index(lax.axis_index("x"), lax.axis_index("y"))
    core = lax.axis_index("c")
    quarter_h = HEAD_DIM // N_CHIPS
    quarter_d = D_MODEL // N_CHIPS

    as_2d = lambda a: a.reshape(-1, a.shape[-1])
    big_w = [as_2d(w) for w in (w_in, w_rgate, w_igate, w_out_a, w_out_b, w_out, w_up, w_down)]
    big_m = [as_2d(w) for w in (m_w_in, m_w_rgate, m_w_igate, m_w_out_a, m_w_out_b, m_w_out, m_w_up, m_w_down)]
    big_v = [as_2d(w) for w in (v_w_in, v_w_rgate, v_w_igate, v_w_out_a, v_w_out_b, v_w_out, v_w_up, v_w_down)]

    packed = jnp.concatenate([conv_w[0], b_rgate[0], b_igate[0]], axis=1)
    packed = jnp.concatenate([packed, jnp.zeros_like(packed)], axis=0)
    s_in, s_r, s_i, s_oa, s_ob, s_out, s_up, s_down = [w.astype(BF16) for w in big_w]
    xs, target = x[0], loss_target[0]
    g3 = norm_final_g.reshape(1, D_MODEL)
    bias_s = jnp.broadcast_to(jnp.transpose(sgu_b_s[0])[:, :, None], (CHUNK, GROUPS, GROUP_DIM)).reshape(CHUNK, D_MODEL)
    core_arr = core.reshape(1).astype(jnp.int32)
    place = jnp.stack([chip, core]).astype(jnp.int32)
    quarter = lambda g: g.reshape(N_CHIPS, D_MODEL // N_CHIPS, D_MODEL)

    def pair_add(nm, g, from_sibling):
        return _pair_add("pair_add_" + nm, core_arr, g.reshape(N_CHIPS, 2, g.shape[1] // 2, g.shape[2]), from_sibling)

    def chip_sum(nm, pair, from_chips):
        return _chip_sum("chip_sum_" + nm, place, pair, from_chips)

    head, (packed_all,) = _comm_call("gather_head", [_gather_ici_job([s_in, s_r, s_i]), _gather_small_job(packed)])
    (w_in_st, wr_st, wi_st), = _comm_call("gather_head_pass", [_gather_pass_job(head)])
    pick = lambda lo, hi: packed_all[:, :HEADS, lo:hi].transpose(1, 0, 2).reshape(HEADS, -1)
    conv_w_full = pick(0, quarter_d)
    br_full = pick(quarter_d, quarter_d + quarter_h).reshape(1, D_MODEL)
    bi_full = pick(quarter_d + quarter_h, quarter_d + 2 * quarter_h).reshape(1, D_MODEL)
    wr, wi = _unstack_heads(wr_st), _unstack_heads(wi_st)
    lru = (conv_w_full, conv_b, wr, br_full, wi, bi_full, lru_lambda)
    sgu = (sgu_ln_g, sgu_ln_b, sgu_w_s[0], bias_s)

    (z, n1), (mid,) = _fwd_in(xs, norm_mix_g, w_in_st, jobs=[_gather_ici_job([s_oa, s_ob, s_out, s_up])])
    (h, ya), (mid, (down,)) = _fwd_lru(z, *lru, jobs=[_gather_pass_job(mid), _gather_ici_job([s_down])])
    yb, ((down,),) = _fwd_sgu(z, *sgu, jobs=[_gather_pass_job([down])])
    w_oa, w_ob, w_o = [w.reshape(D_MODEL, D_MODEL) for w in mid[:3]]
    w_up_st, w_dn = mid[3], down.reshape(D_FF, D_MODEL)
    (pa, pb, h1, n2), _ = _fwd_merge(ya, yb, z, xs, w_oa, w_ob, w_o, norm_mlp_g)
    (up, act, dh2, dh2b, loss_part, dg3), _ = _fwd_mlp(n2, h1, target, w_up_st, w_dn, g3)

    (dup, dh1, dg2), _ = _bwd_mlp(dh2, dh2b, up, h1, w_up_st, w_dn, norm_mlp_g)
    d_up, _ = _weight_grad("dw_up", n2, dup, N_CHIPS, False, True, D_MODEL)
    d_down, ((r_up,),) = _weight_grad("dw_down", act, dh2b, N_CHIPS, True, False, D_MODEL,
                                      jobs=[_pair_send_job([d_up])])
    p_up = pair_add("w_up", d_up, r_up)
    (dz, dya, dyb, merged, dpa, dpb, dh1b), ((r_down,), (q_up,)) = _bwd_merge(
        dh1, pa, pb, z, w_oa, w_ob, w_o, jobs=[_pair_send_job([d_down]), _chip_exchange_job([p_up])])
    p_down = pair_add("w_down", d_down, r_down)
    half_up = chip_sum("w_up", p_up, q_up)
    d_out, ((full_up,),) = _weight_grad("dw_out", merged, dh1b, 1, False, False, D_MODEL, jobs=[_share_job([half_up])])
    d_oa, _ = _weight_grad("dw_out_a", ya, dpa, 1, False, False, D_MODEL)
    d_ob, _ = _weight_grad("dw_out_b", yb, dpb, 1, False, False, D_MODEL)
    mids = [quarter(d_oa), quarter(d_ob), quarter(d_out)]
    (dz, dlg, dlb, dws, dbs), ((q_down,), r_mids) = _bwd_sgu(
        dz, dyb, z, *sgu, jobs=[_chip_exchange_job([p_down]), _pair_send_job(mids)])
    mid_names = ("w_out_a", "w_out_b", "w_out")
    p_mids = [pair_add(nm, g, r) for nm, g, r in zip(mid_names, mids, r_mids)]
    half_down = chip_sum("w_down", p_down, q_down)
    (dz, dcw, dcb, dwr, dbr, dwi, dbi, dlam), (q_mids, (full_down,)) = _bwd_lru(
        dz, dya, z, h, *lru, jobs=[_chip_exchange_job(p_mids), _share_job([half_down])])
    half_mids = [chip_sum(nm, p, q) for nm, p, q in zip(mid_names, p_mids, q_mids)]
    gates = [_stack_heads(dwr).astype(BF16), _stack_heads(dwi).astype(BF16)]
    d_in, (full_mids, r_gates) = _weight_grad(
        "dw_in", n1, dz, N_CHIPS, False, True, IN_SHARD, jobs=[_share_job(half_mids), _pair_send_job(gates)])
    (r_in,), = _comm_call("send_w_in", [_pair_send_job([d_in])])
    last_names = ("w_in", "w_rgate", "w_igate")
    p_last = [pair_add(nm, g, r) for nm, g, r in zip(last_names, [d_in] + gates, [r_in] + r_gates)]
    (grad_x, dg1), (q_last,) = _bwd_in(dz, xs, dh1, w_in_st, norm_mix_g, jobs=[_chip_exchange_job(p_last)])
    half_last = [chip_sum(nm, p, q) for nm, p, q in zip(last_names, p_last, q_last)]
    full_last, = _comm_call("share_last", [_share_job(half_last)])

    names = ("w_in", "w_rgate", "w_igate", "w_out_a", "w_out_b", "w_out", "w_up", "w_down")
    full = [f.reshape(w.shape) for f, w in zip(full_last + full_mids + [full_up, full_down], big_w)]
    big_out = [_adamw("adamw_" + nm, g, w, m, v) for nm, g, w, m, v in zip(names, full, big_w, big_m, big_v)]

    vec, ws_sum = _all_reduce_small(dg1, dcw, dcb, dbr, dbi, dlam, dlg, dlb, dg2, dg3, loss_part, dbs, dws)
    row = lambda r: vec[r:r + 1]
    shard = lambda a, width: lax.dynamic_slice_in_dim(a, chip * width, width, axis=1)
    g_small = dict(
        norm_mix_g=row(ROW_G1), conv_w=shard(vec[ROW_CW:ROW_CW + CONV_WIDTH], quarter_d), conv_b=row(ROW_CB),
        b_rgate=shard(row(ROW_BR).reshape(HEADS, HEAD_DIM), quarter_h),
        b_igate=shard(row(ROW_BI).reshape(HEADS, HEAD_DIM), quarter_h), lru_lambda=row(ROW_LAM),
        sgu_ln_g=row(ROW_LG), sgu_ln_b=row(ROW_LB),
        sgu_w_s=ws_sum.reshape(CHUNK, GROUPS, CHUNK).transpose(1, 0, 2).reshape(GROUPS * CHUNK, CHUNK),
        sgu_b_s=vec[ROW_BS:ROW_BS + GROUPS, 0:CHUNK], norm_mlp_g=row(ROW_G2), norm_final_g=row(ROW_G3))
    loss = vec[ROW_LOSS, 0]
    small_names = list(g_small)
    given = dict(
        norm_mix_g=(norm_mix_g, m_norm_mix_g, v_norm_mix_g), conv_w=(conv_w, m_conv_w, v_conv_w),
        conv_b=(conv_b, m_conv_b, v_conv_b), b_rgate=(b_rgate, m_b_rgate, v_b_rgate),
        b_igate=(b_igate, m_b_igate, v_b_igate), lru_lambda=(lru_lambda, m_lru_lambda, v_lru_lambda),
        sgu_ln_g=(sgu_ln_g, m_sgu_ln_g, v_sgu_ln_g), sgu_ln_b=(sgu_ln_b, m_sgu_ln_b, v_sgu_ln_b),
        sgu_w_s=(sgu_w_s, m_sgu_w_s, v_sgu_w_s), sgu_b_s=(sgu_b_s, m_sgu_b_s, v_sgu_b_s),
        norm_mlp_g=(norm_mlp_g, m_norm_mlp_g, v_norm_mlp_g), norm_final_g=(norm_final_g, m_norm_final_g, v_norm_final_g))
    g2d = [g_small[nm] for nm in small_names]
    to2d = lambda a, g: a.reshape(g.shape)
    d_s, m_s, v_s = _adamw_small(
        g2d, *[[to2d(given[nm][q], g) for nm, g in zip(small_names, g2d)] for q in range(3)])

    shapes = dict(
        norm_mix_g=norm_mix_g, w_in=w_in, conv_w=conv_w, conv_b=conv_b, w_rgate=w_rgate, b_rgate=b_rgate,
        w_igate=w_igate, b_igate=b_igate, lru_lambda=lru_lambda, w_out_a=w_out_a, sgu_ln_g=sgu_ln_g,
        sgu_ln_b=sgu_ln_b, sgu_w_s=sgu_w_s, sgu_b_s=sgu_b_s, w_out_b=w_out_b, w_out=w_out, norm_mlp_g=norm_mlp_g,
        w_up=w_up, w_down=w_down, norm_final_g=norm_final_g)
    grads, deltas, new_m, new_v = {}, {}, {}, {}
    for nm, g, (d, nmom, nvar) in zip(names, full, big_out):
        grads[nm], deltas[nm], new_m[nm], new_v[nm] = g, d, nmom, nvar
    for p, nm in enumerate(small_names):
        grads[nm], deltas[nm], new_m[nm], new_v[nm] = g2d[p], d_s[p], m_s[p], v_s[p]
    order = list(shapes)
    out = [loss, grad_x[None]]
    for group in (grads, deltas, new_m, new_v):
        out += [group[nm].reshape(shapes[nm].shape) for nm in order]
    return tuple(out)
```

```python
import functools

import jax
import jax.numpy as jnp
from jax import lax
from jax.experimental import pallas as pl
from jax.experimental.pallas import tpu as pltpu

F32 = jnp.float32
BF16 = jnp.bfloat16
MESH = pl.DeviceIdType.MESH

D_MODEL = 1024
D_IN = 6 * D_MODEL
D_FF = 4 * D_MODEL
N_CHIPS = 4
IN_SHARD = D_IN // N_CHIPS
HEADS = 4
HEAD_DIM = D_MODEL // HEADS
GROUPS = 4
GROUP_DIM = D_MODEL // GROUPS
CHUNK = 128
CONV_WIDTH = 4
LRU_C = 8.0
NORM_EPS = 1e-6
LN_EPS = 1e-5

ADAM_LR = 0.001
ADAM_B1 = 0.9
ADAM_B2 = 0.999
ADAM_EPS = 1e-08
ADAM_WD = 0.01
ADAM_STEP = 10

SUBLANES = 8
MM_TILE = 512
SEQ_TILE = 256
DW_TILE = 2048
VMEM_LIMIT_BYTES = 56 * 1024 * 1024

GELU_K0 = 0.7978845608028654
GELU_K1 = 0.044715


def _params(n_grid_axes=1):
    return pltpu.CompilerParams(
        dimension_semantics=("arbitrary",) * n_grid_axes, vmem_limit_bytes=VMEM_LIMIT_BYTES)


def _resident(shape):
    nd = len(shape)
    return pl.BlockSpec(shape, lambda *_: (0,) * nd, pipeline_mode=pl.Buffered(1))


def _const(shape):
    nd = len(shape)
    return pl.BlockSpec(shape, lambda *_: (0,) * nd)


def _dot(a, b):
    return jnp.dot(a, b, preferred_element_type=F32)


def _dot_nt(a, b):
    return lax.dot_general(a, b, (((1,), (1,)), ((), ())), preferred_element_type=F32)


def _dot_tn(a, b):
    return lax.dot_general(a, b, (((0,), (0,)), ((), ())), preferred_element_type=F32)


def _gelu(x):
    t = jnp.tanh(GELU_K0 * x * (1.0 + GELU_K1 * x * x))
    return 0.5 * x * (1.0 + t)


def _gelu_and_grad(x):
    x2 = x * x
    t = jnp.tanh(GELU_K0 * x * (1.0 + GELU_K1 * x2))
    g = 0.5 * x * (1.0 + t)
    dg = 0.5 * (1.0 + t) + 0.5 * x * (1.0 - t * t) * (GELU_K0 * (1.0 + 3.0 * GELU_K1 * x2))
    return g, dg


def _rms(x):
    r = lax.rsqrt(jnp.mean(x * x, axis=-1, keepdims=True) + NORM_EPS)
    return x * r, r


def _rms_bwd(dn, xhat, r):
    return r * (dn - xhat * jnp.mean(dn * xhat, axis=-1, keepdims=True))


def _col_sum(v):
    return jnp.sum(v, axis=0, keepdims=True)


def _shift_down(x, tail8, k):
    xs = pltpu.roll(x, k, 0)
    ts = pltpu.roll(tail8, k, 0)
    ridx = lax.broadcasted_iota(jnp.int32, tail8.shape, 0)
    head = jnp.where(ridx < k, ts, xs[0:SUBLANES])
    return jnp.concatenate([head, xs[SUBLANES:]], axis=0)


def _shift_up(x, head8, k):
    n = x.shape[0]
    xs = pltpu.roll(x, n - k, 0)
    hs = pltpu.roll(head8, SUBLANES - k, 0)
    ridx = lax.broadcasted_iota(jnp.int32, head8.shape, 0)
    last = jnp.where(ridx >= SUBLANES - k, hs, xs[n - SUBLANES:n])
    return jnp.concatenate([xs[:n - SUBLANES], last], axis=0)


def _scan_forward(a, b, carry):
    n, cols = a.shape
    groups = n // SUBLANES
    a = a.reshape(groups, SUBLANES, cols)
    b = b.reshape(groups, SUBLANES, cols)
    sub = lax.broadcasted_iota(jnp.int32, a.shape, 1)
    for s in (1, 2, 4):
        a_s = pltpu.roll(a, s, 1)
        b_s = pltpu.roll(b, s, 1)
        m = sub >= s
        b = jnp.where(m, a * b_s + b, b)
        a = jnp.where(m, a * a_s, a)
    out = []
    for g in range(groups):
        h = a[g] * carry + b[g]
        out.append(h)
        carry = h[SUBLANES - 1:SUBLANES]
    return jnp.concatenate(out, axis=0), carry


def _scan_backward(a, b, carry):
    n, cols = a.shape
    groups = n // SUBLANES
    a = a.reshape(groups, SUBLANES, cols)
    b = b.reshape(groups, SUBLANES, cols)
    sub = lax.broadcasted_iota(jnp.int32, a.shape, 1)
    for s in (1, 2, 4):
        a_s = pltpu.roll(a, SUBLANES - s, 1)
        b_s = pltpu.roll(b, SUBLANES - s, 1)
        m = sub < SUBLANES - s
        b = jnp.where(m, a * b_s + b, b)
        a = jnp.where(m, a * a_s, a)
    out = [None] * groups
    for g in reversed(range(groups)):
        h = a[g] * carry + b[g]
        out[g] = h
        carry = h[0:1]
    return jnp.concatenate(out, axis=0), carry


def _softplus_neg(lam):
    e = jnp.exp(-jnp.abs(lam))
    u = 1.0 + e
    log1p_e = jnp.where(u == 1.0, e, jnp.log(u) * (e / jnp.where(u == 1.0, 1.0, u - 1.0)))
    return jnp.maximum(-lam, 0.0) + log1p_e


def _lru_gates(xa, tail8, cw_ref, cb_ref, wr_ref, br_ref, wi_ref, bi_ref, lam_ref):
    cw = cw_ref[...]
    xc = cb_ref[...] + cw[0:1] * xa
    for k in range(1, CONV_WIDTH):
        xc = xc + cw[k:k + 1] * _shift_down(xa, tail8, k)
    xcb = xc.astype(BF16)
    pre_r, pre_i = [], []
    for h in range(HEADS):
        cols = slice(h * HEAD_DIM, (h + 1) * HEAD_DIM)
        pre_r.append(_dot(xcb[:, cols], wr_ref[h]))
        pre_i.append(_dot(xcb[:, cols], wi_ref[h]))
    r = jax.nn.sigmoid(jnp.concatenate(pre_r, axis=1) + br_ref[...])
    ig = jax.nn.sigmoid(jnp.concatenate(pre_i, axis=1) + bi_ref[...])
    log_a = ((-LRU_C) * _softplus_neg(lam_ref[...])) * r
    a = jnp.exp(log_a)
    th = jnp.tanh(log_a)
    mult = jnp.sqrt((-2.0 * th) / (1.0 - th))
    return xc, r, ig, a, mult


class _Job:
    def __init__(self, inputs, out_shape, n_sem, copies, aliases=None, n_local=0):
        self.inputs, self.out_shape, self.n_sem, self.copies = list(inputs), list(out_shape), n_sem, copies
        self.aliases, self.n_local = dict(aliases or {}), n_local


def _fused_call(body, jobs, *, name, grid, in_specs, out_specs, out_shape, scratch_shapes=(),
                input_output_aliases=None, compiler_params=None):
    single = not isinstance(out_shape, (list, tuple))
    out_specs = [out_specs] if single else list(out_specs)
    out_shape = [out_shape] if single else list(out_shape)
    n_scr = len(scratch_shapes)
    in_specs, scratch_shapes = list(in_specs), list(scratch_shapes)
    n_in, n_out = len(in_specs), len(out_shape)
    aliases = dict(input_output_aliases or {})
    in_at, out_at = [], []
    for job in jobs:
        in_at.append(len(in_specs))
        out_at.append(len(out_shape))
        for i, o in job.aliases.items():
            aliases[len(in_specs) + i] = len(out_shape) + o
        in_specs += [ANY] * len(job.inputs)
        out_specs += [ANY] * len(job.out_shape)
        out_shape += job.out_shape
        scratch_shapes += [pltpu.SemaphoreType.DMA((job.n_sem,)), pltpu.SemaphoreType.DMA((job.n_sem,)),
                           pltpu.SemaphoreType.DMA((max(job.n_local, 1),))]
    n_in_all, n_out_all = len(in_specs), len(out_shape)

    def full_body(*refs):
        ins, outs, scr = refs[:n_in_all], refs[n_in_all:n_in_all + n_out_all], refs[n_in_all + n_out_all:]

        def copies(q):
            job = jobs[q]
            return job.copies(ins[in_at[q]:in_at[q] + len(job.inputs)], outs[out_at[q]:out_at[q] + len(job.out_shape)],
                              *scr[n_scr + 3 * q:n_scr + 3 * q + 3])

        def start():
            for q in range(len(jobs)):
                sends, _, local = copies(q)
                for cp in local + sends:
                    cp.start()

        def finish():
            every = [copies(q) for q in range(len(jobs))]
            for _, arrivals, _ in every:
                for cp in arrivals:
                    cp.wait_recv()
            for sends, _, local in every:
                for cp in sends:
                    cp.wait_send()
                for cp in local:
                    cp.wait()

        if not grid:
            start()
            finish()
            return
        ids = [pl.program_id(a) for a in range(len(grid))]
        if jobs:
            pl.when(functools.reduce(jnp.logical_and, [i == 0 for i in ids]))(start)
        body(*ins[:n_in], *outs[:n_out], *scr[:n_scr])
        if jobs:
            pl.when(functools.reduce(jnp.logical_and, [i == g - 1 for i, g in zip(ids, grid)]))(finish)

    call = pl.pallas_call(
        full_body, name=name, grid=grid, in_specs=in_specs, out_specs=out_specs, out_shape=out_shape,
        scratch_shapes=scratch_shapes, input_output_aliases=aliases, compiler_params=compiler_params)

    def run(*args):
        res = call(*args, *[a for job in jobs for a in job.inputs])
        mine = res[0] if single else list(res[:n_out])
        return mine, [list(res[at:at + len(job.out_shape)]) for at, job in zip(out_at, jobs)]

    return run


def _fwd_in(x, g1, w_in_st, jobs=()):
    t = x.shape[0]

    def body(x_ref, g_ref, w_ref, z_ref, n_ref):
        xhat, _ = _rms(x_ref[...])
        n = (xhat * g_ref[...]).astype(BF16)
        n_ref[...] = n
        for k in range(N_CHIPS):
            z_ref[:, k * IN_SHARD:(k + 1) * IN_SHARD] = _dot(n, w_ref[k])

    return _fused_call(
        body, jobs, name="fwd_in", grid=(t // MM_TILE,),
        in_specs=[pl.BlockSpec((MM_TILE, D_MODEL), lambda i: (i, 0)), _const((1, D_MODEL)),
                  _resident((N_CHIPS, D_MODEL, IN_SHARD))],
        out_specs=[pl.BlockSpec((MM_TILE, D_IN), lambda i: (i, 0)),
                   pl.BlockSpec((MM_TILE, D_MODEL), lambda i: (i, 0))],
        out_shape=[jax.ShapeDtypeStruct((t, D_IN), F32), jax.ShapeDtypeStruct((t, D_MODEL), BF16)],
        compiler_params=_params(),
    )(x, g1, w_in_st)


def _fwd_lru(z, conv_w, conv_b, wr, br, wi, bi, lam, jobs=()):
    t = z.shape[0]

    def body(xa_ref, ga_ref, cw_ref, cb_ref, wr_ref, br_ref, wi_ref, bi_ref, lam_ref, ya_ref, h_ref, xc_ref, r_ref,
             ig_ref, a_ref, mult_ref, tail_ref, carry_ref):
        @pl.when(pl.program_id(0) == 0)
        def _():
            tail_ref[...] = jnp.zeros_like(tail_ref)
            carry_ref[...] = jnp.zeros_like(carry_ref)

        xa = xa_ref[...]
        xc, r, ig, a, mult = _lru_gates(xa, tail_ref[...], cw_ref, cb_ref, wr_ref, br_ref, wi_ref, bi_ref, lam_ref)
        tail_ref[...] = xa[SEQ_TILE - SUBLANES:]
        xc_ref[...], r_ref[...], ig_ref[...], a_ref[...], mult_ref[...] = xc, r, ig, a, mult
        h, carry = _scan_forward(a, xc * ig * mult, carry_ref[...])
        carry_ref[...] = carry
        h_ref[...] = h
        ya_ref[...] = (h * _gelu(ga_ref[...])).astype(BF16)

    tile = lambda j: pl.BlockSpec((SEQ_TILE, D_MODEL), lambda i: (i, j))
    return _fused_call(
        body, jobs, name="fwd_lru", grid=(t // SEQ_TILE,),
        in_specs=[tile(0), tile(1), _const((CONV_WIDTH, D_MODEL)), _const((1, D_MODEL)),
                  _resident((HEADS, HEAD_DIM, HEAD_DIM)), _const((1, D_MODEL)),
                  _resident((HEADS, HEAD_DIM, HEAD_DIM)), _const((1, D_MODEL)), _const((1, D_MODEL))],
        out_specs=[tile(0)] * 7,
        out_shape=[jax.ShapeDtypeStruct((t, D_MODEL), BF16)] + [jax.ShapeDtypeStruct((t, D_MODEL), F32)] * 6,
        scratch_shapes=[pltpu.VMEM((SUBLANES, D_MODEL), F32), pltpu.VMEM((1, D_MODEL), F32)],
        compiler_params=_params(),
    )(z, z, conv_w, conv_b, wr, br, wi, bi, lam)


def _sgu_forward_parts(ub, vb, lg_ref, lb_ref):
    u, du = _gelu_and_grad(ub)
    vg, dvg = _gelu_and_grad(vb)
    mu = jnp.mean(vg, axis=-1, keepdims=True)
    d = vg - mu
    rstd = lax.rsqrt(jnp.mean(d * d, axis=-1, keepdims=True) + LN_EPS)
    vhat = d * rstd
    vn = (vhat * lg_ref[...] + lb_ref[...]).astype(BF16)
    return u, du, dvg, rstd, vhat, vn


def _causal_mask():
    rows = lax.broadcasted_iota(jnp.int32, (CHUNK, CHUNK), 0)
    cols = lax.broadcasted_iota(jnp.int32, (CHUNK, CHUNK), 1)
    return rows >= cols


def _fwd_sgu(z, ln_g, ln_b, w_s, bias_full, jobs=()):
    t = z.shape[0]

    def body(ub_ref, vb_ref, lg_ref, lb_ref, ws_ref, bias_ref, yb_ref):
        u, _, _, _, _, vn = _sgu_forward_parts(ub_ref[...], vb_ref[...], lg_ref, lb_ref)
        mask = _causal_mask()
        wm = [jnp.where(mask, ws_ref[g], 0.0).astype(BF16) for g in range(GROUPS)]
        for c in range(SEQ_TILE // CHUNK):
            rows = slice(c * CHUNK, (c + 1) * CHUNK)
            for g in range(GROUPS):
                cols = slice(g * GROUP_DIM, (g + 1) * GROUP_DIM)
                sp = _dot(wm[g], vn[rows, cols]) + bias_ref[:, cols]
                yb_ref[rows, cols] = (u[rows, cols] * sp).astype(BF16)

    tile = lambda j: pl.BlockSpec((SEQ_TILE, D_MODEL), lambda i: (i, j))
    return _fused_call(
        body, jobs, name="fwd_sgu", grid=(t // SEQ_TILE,),
        in_specs=[tile(2), tile(3), _const((1, D_MODEL)), _const((1, D_MODEL)),
                  _const((GROUPS, CHUNK, CHUNK)), _const((CHUNK, D_MODEL))],
        out_specs=tile(0),
        out_shape=jax.ShapeDtypeStruct((t, D_MODEL), BF16),
        compiler_params=_params(),
    )(z, z, ln_g, ln_b, w_s, bias_full)


def _fwd_merge(ya, yb, z, x, w_oa, w_ob, w_out, g2, jobs=()):
    t = x.shape[0]

    def body(ya_ref, yb_ref, m_ref, x_ref, woa_ref, wob_ref, wout_ref, g_ref, pa_ref, pb_ref, h1_ref, n2_ref):
        pa = _dot(ya_ref[...], woa_ref[...])
        pb = _dot(yb_ref[...], wob_ref[...])
        pa_ref[...] = pa
        pb_ref[...] = pb
        merged = jax.nn.sigmoid(m_ref[:, :D_MODEL]) * pa + jax.nn.sigmoid(m_ref[:, D_MODEL:]) * pb
        h1 = x_ref[...] + _dot(merged.astype(BF16), wout_ref[...])
        h1_ref[...] = h1
        xhat, _ = _rms(h1)
        n2_ref[...] = (xhat * g_ref[...]).astype(BF16)

    tile = pl.BlockSpec((MM_TILE, D_MODEL), lambda i: (i, 0))
    sq = _resident((D_MODEL, D_MODEL))
    return _fused_call(
        body, jobs, name="fwd_merge", grid=(t // MM_TILE,),
        in_specs=[tile, tile, pl.BlockSpec((MM_TILE, 2 * D_MODEL), lambda i: (i, 2)), tile, sq, sq, sq,
                  _const((1, D_MODEL))],
        out_specs=[tile, tile, tile, tile],
        out_shape=[jax.ShapeDtypeStruct((t, D_MODEL), F32)] * 3 + [jax.ShapeDtypeStruct((t, D_MODEL), BF16)],
        compiler_params=_params(),
    )(ya, yb, z, x, w_oa, w_ob, w_out, g2)


def _fwd_mlp(n2, h1, target, w_up_st, w_down, g3, jobs=()):
    t = n2.shape[0]

    def body(n2_ref, h1_ref, tgt_ref, wup_ref, wdown_ref, g_ref, up_ref, act_ref, dh2_ref, dh2b_ref, loss_ref,
             dg3_ref):
        @pl.when(pl.program_id(0) == 0)
        def _():
            loss_ref[...] = jnp.zeros_like(loss_ref)
            dg3_ref[...] = jnp.zeros_like(dg3_ref)

        n2 = n2_ref[...]
        h2 = h1_ref[...]
        for k in range(N_CHIPS):
            cols = slice(k * D_MODEL, (k + 1) * D_MODEL)
            up = _dot(n2, wup_ref[k])
            up_ref[:, cols] = up.astype(BF16)
            r = jnp.maximum(up, 0.0)
            act = (r * r).astype(BF16)
            act_ref[:, cols] = act
            h2 = h2 + _dot(act, wdown_ref[cols, :])
        xhat, r3 = _rms(h2)
        diff = xhat * g_ref[...] - tgt_ref[...]
        sq = jnp.sum(diff * diff, axis=1, keepdims=True)
        loss_ref[...] = loss_ref[...] + (0.5 / D_MODEL) * jnp.sum(sq, axis=0, keepdims=True)
        dy = diff * (1.0 / D_MODEL)
        dg3_ref[...] = dg3_ref[...] + _col_sum(dy * xhat)
        dh2 = _rms_bwd(dy * g_ref[...], xhat, r3)
        dh2_ref[...] = dh2
        dh2b_ref[...] = dh2.astype(BF16)

    tile = pl.BlockSpec((MM_TILE, D_MODEL), lambda i: (i, 0))
    wide = pl.BlockSpec((MM_TILE, D_FF), lambda i: (i, 0))
    return _fused_call(
        body, jobs, name="fwd_mlp", grid=(t // MM_TILE,),
        in_specs=[tile, tile, tile, _resident((N_CHIPS, D_MODEL, D_MODEL)), _resident((D_FF, D_MODEL)),
                  _const((1, D_MODEL))],
        out_specs=[wide, wide, tile, tile, _const((SUBLANES, 128)), _const((1, D_MODEL))],
        out_shape=[jax.ShapeDtypeStruct((t, D_FF), BF16), jax.ShapeDtypeStruct((t, D_FF), BF16),
                   jax.ShapeDtypeStruct((t, D_MODEL), F32), jax.ShapeDtypeStruct((t, D_MODEL), BF16),
                   jax.ShapeDtypeStruct((SUBLANES, 128), F32), jax.ShapeDtypeStruct((1, D_MODEL), F32)],
        compiler_params=_params(),
    )(n2, h1, target, w_up_st, w_down, g3)


def _bwd_mlp(dh2, dh2b, up, h1, w_up_st, w_down, g2, jobs=()):
    t = dh2.shape[0]

    def body(dh2_ref, dh2b_ref, up_ref, h1_ref, wup_ref, wdown_ref, g_ref, dup_ref, dh1_ref, dg2_ref):
        @pl.when(pl.program_id(0) == 0)
        def _():
            dg2_ref[...] = jnp.zeros_like(dg2_ref)

        dh2b = dh2b_ref[...]
        dn2 = jnp.zeros((MM_TILE, D_MODEL), F32)
        for k in range(N_CHIPS):
            cols = slice(k * D_MODEL, (k + 1) * D_MODEL)
            dact = _dot_nt(dh2b, wdown_ref[cols, :])
            dup = (dact * (2.0 * jnp.maximum(up_ref[:, cols].astype(F32), 0.0))).astype(BF16)
            dup_ref[:, cols] = dup
            dn2 = dn2 + _dot_nt(dup, wup_ref[k])
        xhat, r2 = _rms(h1_ref[...])
        dg2_ref[...] = dg2_ref[...] + _col_sum(dn2 * xhat)
        dh1_ref[...] = dh2_ref[...] + _rms_bwd(dn2 * g_ref[...], xhat, r2)

    tile = pl.BlockSpec((MM_TILE, D_MODEL), lambda i: (i, 0))
    wide = pl.BlockSpec((MM_TILE, D_FF), lambda i: (i, 0))
    return _fused_call(
        body, jobs, name="bwd_mlp", grid=(t // MM_TILE,),
        in_specs=[tile, tile, wide, tile, _resident((N_CHIPS, D_MODEL, D_MODEL)), _resident((D_FF, D_MODEL)),
                  _const((1, D_MODEL))],
        out_specs=[wide, tile, _const((1, D_MODEL))],
        out_shape=[jax.ShapeDtypeStruct((t, D_FF), BF16), jax.ShapeDtypeStruct((t, D_MODEL), F32),
                   jax.ShapeDtypeStruct((1, D_MODEL), F32)],
        compiler_params=_params(),
    )(dh2, dh2b, up, h1, w_up_st, w_down, g2)


def _bwd_merge(dh1, pa, pb, z, w_oa, w_ob, w_out, jobs=()):
    t = dh1.shape[0]

    def body(dh1_ref, pa_ref, pb_ref, m_ref, woa_ref, wob_ref, wout_ref, dz_ref, dya_ref, dyb_ref, mg_ref,
             dpa_ref, dpb_ref, dh1b_ref):
        dh1b = dh1_ref[...].astype(BF16)
        dh1b_ref[...] = dh1b
        dm = _dot_nt(dh1b, wout_ref[...])
        pa = pa_ref[...]
        pb = pb_ref[...]
        sa = jax.nn.sigmoid(m_ref[:, :D_MODEL])
        sb = jax.nn.sigmoid(m_ref[:, D_MODEL:])
        mg_ref[...] = (sa * pa + sb * pb).astype(BF16)
        dz_ref[:, :D_MODEL] = (dm * pa * sa * (1.0 - sa)).astype(BF16)
        dz_ref[:, D_MODEL:] = (dm * pb * sb * (1.0 - sb)).astype(BF16)
        dpa = (dm * sa).astype(BF16)
        dpb = (dm * sb).astype(BF16)
        dpa_ref[...] = dpa
        dpb_ref[...] = dpb
        dya_ref[...] = _dot_nt(dpa, woa_ref[...])
        dyb_ref[...] = _dot_nt(dpb, wob_ref[...])

    tile = pl.BlockSpec((MM_TILE, D_MODEL), lambda i: (i, 0))
    pair = pl.BlockSpec((MM_TILE, 2 * D_MODEL), lambda i: (i, 2))
    sq = _resident((D_MODEL, D_MODEL))
    act_bf = jax.ShapeDtypeStruct((t, D_MODEL), BF16)
    return _fused_call(
        body, jobs, name="bwd_merge", grid=(t // MM_TILE,),
        in_specs=[tile, tile, tile, pair, sq, sq, sq],
        out_specs=[pair, tile, tile, tile, tile, tile, tile],
        out_shape=[jax.ShapeDtypeStruct((t, D_IN), BF16), jax.ShapeDtypeStruct((t, D_MODEL), F32),
                   jax.ShapeDtypeStruct((t, D_MODEL), F32), act_bf, act_bf, act_bf, act_bf],
        compiler_params=_params(),
    )(dh1, pa, pb, z, w_oa, w_ob, w_out)


def _bwd_sgu(dz, dyb, z, ln_g, ln_b, w_s, bias_full, jobs=()):
    t = dyb.shape[0]
    n_tiles = t // SEQ_TILE

    def body(dz_any, dyb_ref, ub_ref, vb_ref, lg_ref, lb_ref, ws_ref, bias_ref, dz_ref, dlg_ref, dlb_ref, dws_ref,
             dbs_ref, dvn_ref, dsp_acc):
        del dz_any
        i = pl.program_id(0)

        @pl.when(i == 0)
        def _():
            dlg_ref[...] = jnp.zeros_like(dlg_ref)
            dlb_ref[...] = jnp.zeros_like(dlb_ref)
            dws_ref[...] = jnp.zeros_like(dws_ref)
            dsp_acc[...] = jnp.zeros_like(dsp_acc)

        u, du, dvg, rstd, vhat, vn = _sgu_forward_parts(ub_ref[...], vb_ref[...], lg_ref, lb_ref)
        dyb = dyb_ref[...]
        mask = _causal_mask()
        wm = [jnp.where(mask, ws_ref[g], 0.0).astype(BF16) for g in range(GROUPS)]
        for c in range(SEQ_TILE // CHUNK):
            rows = slice(c * CHUNK, (c + 1) * CHUNK)
            for g in range(GROUPS):
                cols = slice(g * GROUP_DIM, (g + 1) * GROUP_DIM)
                vn_blk = vn[rows, cols]
                sp = _dot(wm[g], vn_blk) + bias_ref[:, cols]
                dyb_blk = dyb[rows, cols]
                dz_ref[rows, cols] = (dyb_blk * sp * du[rows, cols]).astype(BF16)
                dsp = dyb_blk * u[rows, cols]
                dsp_acc[:, cols] = dsp_acc[:, cols] + dsp
                dspb = dsp.astype(BF16)
                dvn_ref[rows, cols] = _dot_tn(wm[g], dspb)
                wcols = slice(g * CHUNK, (g + 1) * CHUNK)
                dws_ref[:, wcols] = dws_ref[:, wcols] + jnp.where(mask, _dot_nt(dspb, vn_blk), 0.0)
        dvn = dvn_ref[...]
        dlg_ref[...] = dlg_ref[...] + _col_sum(dvn * vhat)
        dlb_ref[...] = dlb_ref[...] + _col_sum(dvn)
        dvhat = dvn * lg_ref[...]
        dvgel = rstd * (dvhat - jnp.mean(dvhat, axis=-1, keepdims=True)
                        - vhat * jnp.mean(dvhat * vhat, axis=-1, keepdims=True))
        dz_ref[:, D_MODEL:] = (dvgel * dvg).astype(BF16)

        @pl.when(i == n_tiles - 1)
        def _():
            lane = lax.broadcasted_iota(jnp.int32, (CHUNK, 128), 1)
            out = jnp.zeros((CHUNK, 128), F32)
            for g in range(GROUPS):
                s = jnp.sum(dsp_acc[:, g * GROUP_DIM:(g + 1) * GROUP_DIM], axis=1, keepdims=True)
                out = out + jnp.where(lane == g, s, 0.0)
            dbs_ref[...] = out

    tile = lambda j: pl.BlockSpec((SEQ_TILE, D_MODEL), lambda i: (i, j))
    return _fused_call(
        body, jobs, name="bwd_sgu", grid=(n_tiles,),
        in_specs=[pl.BlockSpec(memory_space=pl.ANY), tile(0), tile(2), tile(3), _const((1, D_MODEL)),
                  _const((1, D_MODEL)), _const((GROUPS, CHUNK, CHUNK)), _const((CHUNK, D_MODEL))],
        out_specs=[pl.BlockSpec((SEQ_TILE, 2 * D_MODEL), lambda i: (i, 1)), _const((1, D_MODEL)),
                   _const((1, D_MODEL)), _const((CHUNK, GROUPS * CHUNK)), _const((CHUNK, 128))],
        out_shape=[jax.ShapeDtypeStruct((t, D_IN), BF16), jax.ShapeDtypeStruct((1, D_MODEL), F32),
                   jax.ShapeDtypeStruct((1, D_MODEL), F32), jax.ShapeDtypeStruct((CHUNK, GROUPS * CHUNK), F32),
                   jax.ShapeDtypeStruct((CHUNK, 128), F32)],
        scratch_shapes=[pltpu.VMEM((SEQ_TILE, D_MODEL), F32), pltpu.VMEM((CHUNK, D_MODEL), F32)],
        input_output_aliases={0: 0},
        compiler_params=_params(),
    )(dz, dyb, z, z, ln_g, ln_b, w_s, bias_full)


def _bwd_lru(dz, dya, z, h, xc, r, ig, a, mult, conv_w, wr, wi, lam, jobs=()):
    t = dya.shape[0]
    n_tiles = t // SEQ_TILE
    per_tile = SEQ_TILE // SUBLANES

    def body(dz_any, dya_ref, xa_ref, ga_ref, h_ref, h_prev_ref, xc_ref, r_ref, ig_ref, a_ref, mult_ref, cw_ref, wr_ref,
             wi_ref, lam_ref, dz_ref, dcw_ref, dcb_ref, dwr_ref, dbr_ref, dwi_ref, dbi_ref, dlam_ref, lam_carry,
             dxc_head):
        del dz_any
        i = pl.program_id(0)

        @pl.when(i == 0)
        def _():
            for ref in (dcw_ref, dcb_ref, dwr_ref, dbr_ref, dwi_ref, dbi_ref, dlam_ref, lam_carry, dxc_head):
                ref[...] = jnp.zeros_like(ref)

        first_tile = i == n_tiles - 1
        h_tail = jnp.where(first_tile, 0.0, h_prev_ref[...])
        xc, r, ig, a, mult = xc_ref[...], r_ref[...], ig_ref[...], a_ref[...], mult_ref[...]
        xcb = xc.astype(BF16)
        sp = _softplus_neg(lam_ref[...])
        h = h_ref[...]
        h_prev = _shift_down(h, h_tail, 1)
        dya = dya_ref[...]
        gg, dgg = _gelu_and_grad(ga_ref[...])
        dz_ref[:, D_MODEL:] = (dya * h * dgg).astype(BF16)
        ones = jnp.ones((SUBLANES, D_MODEL), F32)
        lam_t, lam_first = _scan_backward(_shift_up(a, ones, 1), dya * gg, lam_carry[...])
        lam_carry[...] = a[0:1] * lam_first
        dmult = lam_t * xc * ig
        dla = lam_t * h_prev * a - dmult * (a * a) / mult
        dr = dla * ((-LRU_C) * sp)
        dlam_ref[...] = dlam_ref[...] + _col_sum(dla * r) * (LRU_C * jax.nn.sigmoid(-lam_ref[...]))
        dpr = dr * r * (1.0 - r)
        dpi = lam_t * xc * mult * ig * (1.0 - ig)
        dbr_ref[...] = dbr_ref[...] + _col_sum(dpr)
        dbi_ref[...] = dbi_ref[...] + _col_sum(dpi)
        dprb = dpr.astype(BF16)
        dpib = dpi.astype(BF16)
        dxc_gate = []
        for hd in range(HEADS):
            cols = slice(hd * HEAD_DIM, (hd + 1) * HEAD_DIM)
            dxc_gate.append(_dot_nt(dprb[:, cols], wr_ref[hd]) + _dot_nt(dpib[:, cols], wi_ref[hd]))
            dwr_ref[hd] = dwr_ref[hd] + _dot_tn(xcb[:, cols], dprb[:, cols])
            dwi_ref[hd] = dwi_ref[hd] + _dot_tn(xcb[:, cols], dpib[:, cols])
        dxc = lam_t * ig * mult + jnp.concatenate(dxc_gate, axis=1)
        dcb_ref[...] = dcb_ref[...] + _col_sum(dxc)
        cw = cw_ref[...]
        head = dxc_head[...]
        xa = xa_ref[...]
        dxa = cw[0:1] * dxc
        dcw_ref[0:1, :] = dcw_ref[0:1, :] + _col_sum(dxc * xa)
        for k in range(1, CONV_WIDTH):
            dxc_k = _shift_up(dxc, head, k)
            dxa = dxa + cw[k:k + 1] * dxc_k
            dcw_ref[k:k + 1, :] = dcw_ref[k:k + 1, :] + _col_sum(dxc_k * xa)
        dxc_head[...] = dxc[0:SUBLANES]
        dz_ref[:, :D_MODEL] = dxa.astype(BF16)

    rev = lambda i: n_tiles - 1 - i
    tile = lambda j: pl.BlockSpec((SEQ_TILE, D_MODEL), lambda i: (rev(i), j))
    prev8 = pl.BlockSpec((SUBLANES, D_MODEL), lambda i: (jnp.maximum(rev(i) * per_tile - 1, 0), 0))
    vec = _const((1, D_MODEL))
    gate_w = _resident((HEADS, HEAD_DIM, HEAD_DIM))
    vec_shape = jax.ShapeDtypeStruct((1, D_MODEL), F32)
    gate_shape = jax.ShapeDtypeStruct((HEADS, HEAD_DIM, HEAD_DIM), F32)
    return _fused_call(
        body, jobs, name="bwd_lru", grid=(n_tiles,),
        in_specs=[pl.BlockSpec(memory_space=pl.ANY), tile(0), tile(0), tile(1), tile(0), prev8] + [tile(0)] * 5
                 + [_const((CONV_WIDTH, D_MODEL)), gate_w, gate_w, vec],
        out_specs=[pl.BlockSpec((SEQ_TILE, 2 * D_MODEL), lambda i: (rev(i), 0)), _const((SUBLANES, D_MODEL)), vec,
                   _const((HEADS, HEAD_DIM, HEAD_DIM)), vec, _const((HEADS, HEAD_DIM, HEAD_DIM)), vec, vec],
        out_shape=[jax.ShapeDtypeStruct((t, D_IN), BF16), jax.ShapeDtypeStruct((SUBLANES, D_MODEL), F32), vec_shape,
                   gate_shape, vec_shape, gate_shape, vec_shape, vec_shape],
        scratch_shapes=[pltpu.VMEM((1, D_MODEL), F32), pltpu.VMEM((SUBLANES, D_MODEL), F32)],
        input_output_aliases={0: 0},
        compiler_params=_params(),
    )(dz, dya, z, z, h, h, xc, r, ig, a, mult, conv_w, wr, wi, lam)


def _bwd_in(dz, x, dh1, w_in_st, g1, jobs=()):
    t = x.shape[0]

    def body(dz_ref, x_ref, dh1_ref, w_ref, g_ref, dx_ref, dg1_ref):
        @pl.when(pl.program_id(0) == 0)
        def _():
            dg1_ref[...] = jnp.zeros_like(dg1_ref)

        dn1 = jnp.zeros((MM_TILE, D_MODEL), F32)
        for k in range(N_CHIPS):
            dn1 = dn1 + _dot_nt(dz_ref[:, k * IN_SHARD:(k + 1) * IN_SHARD], w_ref[k])
        xhat, r1 = _rms(x_ref[...])
        dg1_ref[...] = dg1_ref[...] + _col_sum(dn1 * xhat)
        dx_ref[...] = dh1_ref[...] + _rms_bwd(dn1 * g_ref[...], xhat, r1)

    tile = pl.BlockSpec((MM_TILE, D_MODEL), lambda i: (i, 0))
    return _fused_call(
        body, jobs, name="bwd_in", grid=(t // MM_TILE,),
        in_specs=[pl.BlockSpec((MM_TILE, D_IN), lambda i: (i, 0)), tile, tile,
                  _resident((N_CHIPS, D_MODEL, IN_SHARD)), _const((1, D_MODEL))],
        out_specs=[tile, _const((1, D_MODEL))],
        out_shape=[jax.ShapeDtypeStruct((t, D_MODEL), F32), jax.ShapeDtypeStruct((1, D_MODEL), F32)],
        compiler_params=_params(),
    )(dz, x, dh1, w_in_st, g1)


def _weight_grad(name, a, b, n_blocks, a_varies, b_varies, width, jobs=()):
    t = a.shape[0]
    rows = min(DW_TILE, t)
    n_t = t // rows

    def body(a_ref, b_ref, o_ref, acc_ref):
        s = pl.program_id(1)
        part = _dot_tn(a_ref[...], b_ref[...])

        @pl.when(s == 0)
        def _():
            acc_ref[...] = part

        @pl.when(s > 0)
        def _():
            acc_ref[...] = acc_ref[...] + part

        @pl.when(s == n_t - 1)
        def _():
            o_ref[...] = acc_ref[...].astype(BF16)

    return _fused_call(
        body, jobs, name=name, grid=(n_blocks, n_t),
        in_specs=[pl.BlockSpec((rows, D_MODEL), (lambda j, s: (s, j)) if a_varies else (lambda j, s: (s, 0))),
                  pl.BlockSpec((rows, width), (lambda j, s: (s, j)) if b_varies else (lambda j, s: (s, 0)))],
        out_specs=pl.BlockSpec((None, D_MODEL, width), lambda j, s: (j, 0, 0)),
        out_shape=jax.ShapeDtypeStruct((n_blocks, D_MODEL, width), BF16),
        scratch_shapes=[pltpu.VMEM((D_MODEL, width), F32)],
        compiler_params=_params(2),
    )(a, b)


def _place():
    x, y, c = lax.axis_index("x"), lax.axis_index("y"), lax.axis_index("c")
    other_chips = [(1 - x, y), (x, 1 - y), (1 - x, 1 - y)]
    return x, y, c, other_chips


def _chip_index(px, py):
    return 2 * px + py


ANY = pl.BlockSpec(memory_space=pl.ANY)


def _comm_call(name, jobs):
    return _fused_call(None, jobs, name=name, grid=(), in_specs=[], out_specs=[], out_shape=[])()[1]


def _gather_ici_job(shards):
    n = len(shards)
    halves = [s.shape[0] // 2 for s in shards]

    def copies(ins, outs, send, recv, local):
        x, y, c, chips = _place()
        me, sibling = (x, y, c), (x, y, 1 - c)

        def block(w, place):
            return outs[w].at[_chip_index(place[0], place[1]), pl.ds(place[2] * halves[w], halves[w]), :]

        def copy(w, k, blk, to, src=None):
            return pltpu.make_async_remote_copy(
                src_ref=block(w, blk) if src is None else src, dst_ref=block(w, blk),
                send_sem=send.at[4 * w + k], recv_sem=recv.at[4 * w + k], device_id=to, device_id_type=MESH)

        sends, arrivals, own = [], [], []
        for w in range(n):
            src = ins[w].at[pl.ds(c * halves[w], halves[w]), :]
            own.append(pltpu.make_async_copy(src, block(w, me), local.at[w]))
            sends.append(copy(w, 0, me, sibling, src))
            arrivals.append(copy(w, 0, sibling, me))
            for j, chip in enumerate(chips):
                sends.append(copy(w, 1 + j, me, (*chip, c), src))
                arrivals.append(copy(w, 1 + j, (*chip, c), me))
        return sends, arrivals, own

    return _Job(shards, [jax.ShapeDtypeStruct((N_CHIPS,) + s.shape, s.dtype) for s in shards], 4 * n, copies,
                n_local=n)


def _gather_pass_job(stacked):
    n = len(stacked)
    halves = [s.shape[1] // 2 for s in stacked]

    def copies(ins, outs, send, recv, local):
        del ins, local
        x, y, c, chips = _place()

        def copy(w, j, chip, pc, to):
            blk = outs[w].at[_chip_index(*chip), pl.ds(pc * halves[w], halves[w]), :]
            return pltpu.make_async_remote_copy(
                src_ref=blk, dst_ref=blk, send_sem=send.at[3 * w + j], recv_sem=recv.at[3 * w + j], device_id=to,
                device_id_type=MESH)

        sends = [copy(w, j, chip, c, (x, y, 1 - c)) for w in range(n) for j, chip in enumerate(chips)]
        arrivals = [copy(w, j, chip, 1 - c, (x, y, c)) for w in range(n) for j, chip in enumerate(chips)]
        return sends, arrivals, []

    return _Job(stacked, [jax.ShapeDtypeStruct(s.shape, s.dtype) for s in stacked], 3 * n, copies,
                aliases={w: w for w in range(n)})


def _gather_small_job(block):
    def copies(ins, outs, send, recv, local):
        x, y, c, chips = _place()

        def copy(j, chip_from, to):
            return pltpu.make_async_remote_copy(
                src_ref=ins[0], dst_ref=outs[0].at[_chip_index(*chip_from)], send_sem=send.at[j],
                recv_sem=recv.at[j], device_id=to, device_id_type=MESH)

        own = [pltpu.make_async_copy(ins[0], outs[0].at[_chip_index(x, y)], local.at[0])]
        sends = [copy(j, (x, y), (*chip, c)) for j, chip in enumerate(chips)]
        arrivals = [copy(j, chip, (x, y, c)) for j, chip in enumerate(chips)]
        return sends, arrivals, own

    return _Job([block], [jax.ShapeDtypeStruct((N_CHIPS,) + block.shape, block.dtype)], 3, copies, n_local=1)


def _pair_send_job(grads):
    n = len(grads)
    halves = [g.shape[1] // 2 for g in grads]

    def copies(ins, outs, send, recv, local):
        del local
        x, y, c, _ = _place()
        sends = [pltpu.make_async_remote_copy(
            src_ref=ins[w].at[:, pl.ds((1 - c) * halves[w], halves[w]), :], dst_ref=outs[w], send_sem=send.at[w],
            recv_sem=recv.at[w], device_id=(x, y, 1 - c), device_id_type=MESH) for w in range(n)]
        return sends, sends, []

    return _Job(grads, [jax.ShapeDtypeStruct((N_CHIPS, h, g.shape[2]), g.dtype) for g, h in zip(grads, halves)], n,
                copies)


def _row_block(rows, limit=256):
    return min(rows, limit)


def _pair_add(name, core, mine, theirs):
    _, _, h, cols = mine.shape
    rb = _row_block(h, 512)

    def body(core_ref, a_ref, b_ref, o_ref):
        del core_ref
        o_ref[...] = (a_ref[...].astype(F32) + b_ref[...].astype(F32)).astype(BF16)

    return pl.pallas_call(
        body, name=name,
        grid_spec=pltpu.PrefetchScalarGridSpec(
            num_scalar_prefetch=1, grid=(N_CHIPS, h // rb),
            in_specs=[pl.BlockSpec((None, None, rb, cols), lambda k, r, core_ref: (k, core_ref[0], r, 0)),
                      pl.BlockSpec((None, rb, cols), lambda k, r, core_ref: (k, r, 0))],
            out_specs=pl.BlockSpec((None, rb, cols), lambda k, r, core_ref: (k, r, 0))),
        out_shape=jax.ShapeDtypeStruct(theirs.shape, BF16),
        compiler_params=_params(2),
    )(core, mine, theirs)


def _chip_exchange_job(sums):
    n = len(sums)

    def copies(ins, outs, send, recv, local):
        del local
        _, _, c, chips = _place()
        sends = [pltpu.make_async_remote_copy(
            src_ref=ins[w].at[_chip_index(*chip)], dst_ref=outs[w].at[j], send_sem=send.at[3 * w + j],
            recv_sem=recv.at[3 * w + j], device_id=(*chip, c), device_id_type=MESH)
            for w in range(n) for j, chip in enumerate(chips)]
        return sends, sends, []

    return _Job(sums, [jax.ShapeDtypeStruct((N_CHIPS - 1,) + s.shape[1:], s.dtype) for s in sums], 3 * n, copies)


def _chip_sum(name, place, mine, theirs):
    _, h, cols = mine.shape
    rb = _row_block(h, 512)

    def body(place_ref, p_ref, q_ref, o_ref):
        del place_ref
        acc = p_ref[...].astype(F32)
        for j in range(N_CHIPS - 1):
            acc = acc + q_ref[j].astype(F32)
        o_ref[...] = acc

    return pl.pallas_call(
        body, name=name,
        grid_spec=pltpu.PrefetchScalarGridSpec(
            num_scalar_prefetch=1, grid=(h // rb,),
            in_specs=[pl.BlockSpec((None, rb, cols), lambda r, place_ref: (place_ref[0], r, 0)),
                      pl.BlockSpec((N_CHIPS - 1, rb, cols), lambda r, place_ref: (0, r, 0))],
            out_specs=pl.BlockSpec((None, rb, cols), lambda r, place_ref: (place_ref[1], r, 0))),
        out_shape=jax.ShapeDtypeStruct((2, h, cols), F32),
        compiler_params=_params(),
    )(place, mine, theirs)


def _share_job(bufs):
    n = len(bufs)

    def copies(ins, outs, send, recv, local):
        del ins, local
        x, y, c, _ = _place()

        def copy(w, half):
            return pltpu.make_async_remote_copy(
                src_ref=outs[w].at[half], dst_ref=outs[w].at[half], send_sem=send.at[w], recv_sem=recv.at[w],
                device_id=(x, y, 1 - c), device_id_type=MESH)

        return [copy(w, c) for w in range(n)], [copy(w, 1 - c) for w in range(n)], []

    return _Job(bufs, [jax.ShapeDtypeStruct(b.shape, b.dtype) for b in bufs], n, copies,
                aliases={w: w for w in range(n)})


SMALL_ROWS = 24
ROW_G1, ROW_CW, ROW_CB, ROW_BR, ROW_BI, ROW_LAM, ROW_LG, ROW_LB, ROW_G2, ROW_G3, ROW_LOSS, ROW_BS = (
    0, 1, 5, 6, 7, 8, 9, 10, 11, 12, 13, 16)
N_DEV = 8


def _pack_small(dcw, dcb, dbr, dbi, dlam, dlg, dlb, dg2, dg3, loss, dbs):
    def body(dcw_ref, dcb_ref, dbr_ref, dbi_ref, dlam_ref, dlg_ref, dlb_ref, dg2_ref, dg3_ref, loss_ref, dbs_ref, out):
        out[...] = jnp.zeros((SMALL_ROWS, D_MODEL), F32)
        for row, ref in ((ROW_CB, dcb_ref), (ROW_BR, dbr_ref), (ROW_BI, dbi_ref), (ROW_LAM, dlam_ref),
                         (ROW_LG, dlg_ref), (ROW_LB, dlb_ref), (ROW_G2, dg2_ref), (ROW_G3, dg3_ref)):
            out[row:row + 1, :] = ref[...]
        out[ROW_CW:ROW_CW + CONV_WIDTH, :] = dcw_ref[0:CONV_WIDTH, :]
        out[ROW_LOSS:ROW_LOSS + 1, 0:128] = loss_ref[0:1, :]
        out[ROW_BS:ROW_BS + GROUPS, 0:128] = jnp.transpose(dbs_ref[...])[0:GROUPS, :]

    vm = pl.BlockSpec(memory_space=pltpu.VMEM)
    return pl.pallas_call(
        body, name="pack_small", in_specs=[vm] * 11, out_specs=vm,
        out_shape=jax.ShapeDtypeStruct((SMALL_ROWS, D_MODEL), F32),
    )(dcw, dcb, dbr, dbi, dlam, dlg, dlb, dg2, dg3, loss, dbs)


def _gather_all_job(blocks):
    n = len(blocks)
    flips = [(dx, dy, dc) for dx in (0, 1) for dy in (0, 1) for dc in (0, 1)][1:]

    def copies(ins, outs, send, recv, local):
        x, y, c, _ = _place()
        me = 4 * x + 2 * y + c
        sends, arrivals, own = [], [], []
        for w in range(n):
            own.append(pltpu.make_async_copy(ins[w], outs[w].at[me], local.at[w]))
            for k, (dx, dy, dc) in enumerate(flips):
                peer = (x ^ dx, y ^ dy, c ^ dc)
                sem = dict(send_sem=send.at[7 * w + k], recv_sem=recv.at[7 * w + k])
                sends.append(pltpu.make_async_remote_copy(
                    src_ref=ins[w], dst_ref=outs[w].at[me], device_id=peer, device_id_type=MESH, **sem))
                arrivals.append(pltpu.make_async_remote_copy(
                    src_ref=ins[w], dst_ref=outs[w].at[4 * peer[0] + 2 * peer[1] + peer[2]], device_id=peer,
                    device_id_type=MESH, **sem))
        return sends, arrivals, own

    return _Job(blocks, [jax.ShapeDtypeStruct((N_DEV,) + b.shape, b.dtype) for b in blocks], 7 * n, copies, n_local=n)


def _sum_small(vec_all, ws_all, dg1_all):
    def body(vec_ref, ws_ref, dg1_ref, vec_out, ws_out):
        vec, ws, dg1 = vec_ref[0], ws_ref[0], dg1_ref[0]
        for d in range(1, N_DEV):
            vec, ws, dg1 = vec + vec_ref[d], ws + ws_ref[d], dg1 + dg1_ref[d]
        vec_out[...] = vec
        vec_out[ROW_G1:ROW_G1 + 1, :] = dg1
        ws_out[...] = ws

    vm = pl.BlockSpec(memory_space=pltpu.VMEM)
    return pl.pallas_call(
        body, name="sum_small", in_specs=[vm] * 3, out_specs=[vm, vm],
        out_shape=[jax.ShapeDtypeStruct(vec_all.shape[1:], F32), jax.ShapeDtypeStruct(ws_all.shape[1:], F32)],
    )(vec_all, ws_all, dg1_all)


def _adamw_math(w, g, m, v):
    m = ADAM_B1 * m + (1.0 - ADAM_B1) * g
    v = ADAM_B2 * v + (1.0 - ADAM_B2) * (g * g)
    m_hat = m / (1.0 - ADAM_B1 ** ADAM_STEP)
    v_hat = v / (1.0 - ADAM_B2 ** ADAM_STEP)
    delta = (-ADAM_LR) * (m_hat / (jnp.sqrt(v_hat) + ADAM_EPS) + ADAM_WD * w)
    return delta, m, v


def _adamw(name, g, w, m, v, jobs=()):
    rows, cols = w.shape
    rb = _row_block(rows)

    def body(g_ref, w_ref, m_ref, v_ref, d_ref, nm_ref, nv_ref):
        d_ref[...], nm_ref[...], nv_ref[...] = _adamw_math(w_ref[...], g_ref[...], m_ref[...], v_ref[...])

    blk = pl.BlockSpec((rb, cols), lambda r: (r, 0))
    return _fused_call(
        body, jobs, name=name, grid=(rows // rb,), in_specs=[blk] * 4, out_specs=[blk] * 3,
        out_shape=[jax.ShapeDtypeStruct(w.shape, F32)] * 3, compiler_params=_params(),
    )(g, w, m, v)


def _adamw_small(grads, ws, ms, vs):
    n = len(grads)

    def body(*refs):
        g_refs, w_refs, m_refs, v_refs = refs[:n], refs[n:2 * n], refs[2 * n:3 * n], refs[3 * n:4 * n]
        outs = refs[4 * n:]
        for p in range(n):
            d, nm, nv = _adamw_math(w_refs[p][...], g_refs[p][...], m_refs[p][...], v_refs[p][...])
            outs[p][...] = d
            outs[n + p][...] = nm
            outs[2 * n + p][...] = nv

    vm = pl.BlockSpec(memory_space=pltpu.VMEM)
    shapes = [jax.ShapeDtypeStruct(w.shape, F32) for w in ws]
    out = pl.pallas_call(
        body, name="adamw_small", in_specs=[vm] * (4 * n), out_specs=[vm] * (3 * n), out_shape=shapes * 3,
    )(*grads, *ws, *ms, *vs)
    return out[:n], out[n:2 * n], out[2 * n:]


def _unstack_heads(w_st):
    per = HEAD_DIM // N_CHIPS
    return w_st.reshape(N_CHIPS, HEADS, per, HEAD_DIM).transpose(1, 0, 2, 3).reshape(HEADS, HEAD_DIM, HEAD_DIM)


def _stack_heads(w):
    per = HEAD_DIM // N_CHIPS
    return w.reshape(HEADS, N_CHIPS, per, HEAD_DIM).transpose(1, 0, 2, 3).reshape(N_CHIPS, HEADS * per, HEAD_DIM)


def kernel(x, norm_mix_g, w_in, conv_w, conv_b, w_rgate, b_rgate, w_igate, b_igate, lru_lambda, w_out_a, sgu_ln_g, sgu_ln_b, sgu_w_s, sgu_b_s, w_out_b, w_out, norm_mlp_g, w_up, w_down, norm_final_g, loss_target, m_norm_mix_g, m_w_in, m_conv_w, m_conv_b, m_w_rgate, m_b_rgate, m_w_igate, m_b_igate, m_lru_lambda, m_w_out_a, m_sgu_ln_g, m_sgu_ln_b, m_sgu_w_s, m_sgu_b_s, m_w_out_b, m_w_out, m_norm_mlp_g, m_w_up, m_w_down, m_norm_final_g, v_norm_mix_g, v_w_in, v_conv_w, v_conv_b, v_w_rgate, v_b_rgate, v_w_igate, v_b_igate, v_lru_lambda, v_w_out_a, v_sgu_ln_g, v_sgu_ln_b, v_sgu_w_s, v_sgu_b_s, v_w_out_b, v_w_out, v_norm_mlp_g, v_w_up, v_w_down, v_norm_final_g):
    chip = _chip_index(lax.axis_index("x"), lax.axis_index("y"))
    core = lax.axis_index("c")
    quarter_h = HEAD_DIM // N_CHIPS
    quarter_d = D_MODEL // N_CHIPS

    as_2d = lambda a: a.reshape(-1, a.shape[-1])
    big_w = [as_2d(w) for w in (w_in, w_rgate, w_igate, w_out_a, w_out_b, w_out, w_up, w_down)]
    big_m = [as_2d(w) for w in (m_w_in, m_w_rgate, m_w_igate, m_w_out_a, m_w_out_b, m_w_out, m_w_up, m_w_down)]
    big_v = [as_2d(w) for w in (v_w_in, v_w_rgate, v_w_igate, v_w_out_a, v_w_out_b, v_w_out, v_w_up, v_w_down)]

    packed = jnp.concatenate([conv_w[0], b_rgate[0], b_igate[0]], axis=1)
    packed = jnp.concatenate([packed, jnp.zeros_like(packed)], axis=0)
    s_in, s_r, s_i, s_oa, s_ob, s_out, s_up, s_down = [w.astype(BF16) for w in big_w]
    xs, target = x[0], loss_target[0]
    g3 = norm_final_g.reshape(1, D_MODEL)
    bias_s = jnp.broadcast_to(jnp.transpose(sgu_b_s[0])[:, :, None], (CHUNK, GROUPS, GROUP_DIM)).reshape(CHUNK, D_MODEL)
    core_arr = core.reshape(1).astype(jnp.int32)
    place = jnp.stack([chip, core]).astype(jnp.int32)
    quarter = lambda g: g.reshape(N_CHIPS, D_MODEL // N_CHIPS, D_MODEL)

    def pair_add(nm, g, from_sibling):
        return _pair_add("pair_add_" + nm, core_arr, g.reshape(N_CHIPS, 2, g.shape[1] // 2, g.shape[2]), from_sibling)

    def chip_sum(nm, pair, from_chips):
        return _chip_sum("chip_sum_" + nm, place, pair, from_chips)

    head, (packed_all,) = _comm_call("gather_head", [_gather_ici_job([s_in, s_r, s_i]), _gather_small_job(packed)])
    (w_in_st, wr_st, wi_st), = _comm_call("gather_head_pass", [_gather_pass_job(head)])
    pick = lambda lo, hi: packed_all[:, :HEADS, lo:hi].transpose(1, 0, 2).reshape(HEADS, -1)
    conv_w_full = pick(0, quarter_d)
    br_full = pick(quarter_d, quarter_d + quarter_h).reshape(1, D_MODEL)
    bi_full = pick(quarter_d + quarter_h, quarter_d + 2 * quarter_h).reshape(1, D_MODEL)
    wr, wi = _unstack_heads(wr_st), _unstack_heads(wi_st)
    lru = (conv_w_full, conv_b, wr, br_full, wi, bi_full, lru_lambda)
    sgu = (sgu_ln_g, sgu_ln_b, sgu_w_s[0], bias_s)

    (z, n1), (mid,) = _fwd_in(xs, norm_mix_g, w_in_st, jobs=[_gather_ici_job([s_oa, s_ob, s_out, s_up])])
    (ya, *saved), (mid, (down,)) = _fwd_lru(z, *lru, jobs=[_gather_pass_job(mid), _gather_ici_job([s_down])])
    yb, ((down,),) = _fwd_sgu(z, *sgu, jobs=[_gather_pass_job([down])])
    w_oa, w_ob, w_o = [w.reshape(D_MODEL, D_MODEL) for w in mid[:3]]
    w_up_st, w_dn = mid[3], down.reshape(D_FF, D_MODEL)
    (pa, pb, h1, n2), _ = _fwd_merge(ya, yb, z, xs, w_oa, w_ob, w_o, norm_mlp_g)
    (up, act, dh2, dh2b, loss_part, dg3), _ = _fwd_mlp(n2, h1, target, w_up_st, w_dn, g3)

    (dup, dh1, dg2), _ = _bwd_mlp(dh2, dh2b, up, h1, w_up_st, w_dn, norm_mlp_g)
    d_up, _ = _weight_grad("dw_up", n2, dup, N_CHIPS, False, True, D_MODEL)
    d_down, ((r_up,),) = _weight_grad("dw_down", act, dh2b, N_CHIPS, True, False, D_MODEL,
                                      jobs=[_pair_send_job([d_up])])
    p_up = pair_add("w_up", d_up, r_up)
    (dz, dya, dyb, merged, dpa, dpb, dh1b), ((r_down,), (q_up,)) = _bwd_merge(
        dh1, pa, pb, z, w_oa, w_ob, w_o, jobs=[_pair_send_job([d_down]), _chip_exchange_job([p_up])])
    p_down = pair_add("w_down", d_down, r_down)
    half_up = chip_sum("w_up", p_up, q_up)
    d_out, ((full_up,),) = _weight_grad("dw_out", merged, dh1b, 1, False, False, D_MODEL, jobs=[_share_job([half_up])])
    d_oa, _ = _weight_grad("dw_out_a", ya, dpa, 1, False, False, D_MODEL)
    d_ob, _ = _weight_grad("dw_out_b", yb, dpb, 1, False, False, D_MODEL)
    mids = [quarter(d_oa), quarter(d_ob), quarter(d_out)]
    (dz, dlg, dlb, dws, dbs), ((q_down,), r_mids) = _bwd_sgu(
        dz, dyb, z, *sgu, jobs=[_chip_exchange_job([p_down]), _pair_send_job(mids)])
    mid_names = ("w_out_a", "w_out_b", "w_out")
    p_mids = [pair_add(nm, g, r) for nm, g, r in zip(mid_names, mids, r_mids)]
    half_down = chip_sum("w_down", p_down, q_down)
    (dz, dcw, dcb, dwr, dbr, dwi, dbi, dlam), (q_mids, (full_down,)) = _bwd_lru(
        dz, dya, z, *saved, conv_w_full, wr, wi, lru_lambda, jobs=[_chip_exchange_job(p_mids), _share_job([half_down])])
    half_mids = [chip_sum(nm, p, q) for nm, p, q in zip(mid_names, p_mids, q_mids)]
    gates = [_stack_heads(dwr).astype(BF16), _stack_heads(dwi).astype(BF16)]
    small = _pack_small(dcw, dcb, dbr, dbi, dlam, dlg, dlb, dg2, dg3, loss_part, dbs)
    d_in, (full_mids, r_gates, (vec_all, ws_all)) = _weight_grad(
        "dw_in", n1, dz, N_CHIPS, False, True, IN_SHARD,
        jobs=[_share_job(half_mids), _pair_send_job(gates), _gather_all_job([small, dws])])
    names = ("w_in", "w_rgate", "w_igate", "w_out_a", "w_out_b", "w_out", "w_up", "w_down")
    adam_args = {nm: (w, m, v) for nm, w, m, v in zip(names, big_w, big_m, big_v)}

    def adamw(nm, g, jobs=()):
        w, m, v = adam_args[nm]
        g = g.reshape(w.shape)
        return (g,) + tuple(x for x in _adamw("adamw_" + nm, g, w, m, v, jobs))

    g_up, out_up, ((r_in,),) = adamw("w_up", full_up, [_pair_send_job([d_in])])
    last_names = ("w_in", "w_rgate", "w_igate")
    p_last = [pair_add(nm, g, r) for nm, g, r in zip(last_names, [d_in] + gates, [r_in] + r_gates)]
    (grad_x, dg1), (q_last,) = _bwd_in(dz, xs, dh1, w_in_st, norm_mix_g, jobs=[_chip_exchange_job(p_last)])
    half_last = [chip_sum(nm, p, q) for nm, p, q in zip(last_names, p_last, q_last)]
    g_down, out_down, (full_last, (dg1_all,)) = adamw(
        "w_down", full_down, [_share_job(half_last), _gather_all_job([dg1])])
    big = {"w_up": (g_up, out_up), "w_down": (g_down, out_down)}
    for nm, f in zip(names[:6], full_last + full_mids):
        g, out, _ = adamw(nm, f)
        big[nm] = (g, out)
    full = [big[nm][0] for nm in names]
    big_out = [big[nm][1] for nm in names]

    vec, ws_sum = _sum_small(vec_all, ws_all, dg1_all)
    row = lambda r: vec[r:r + 1]
    shard = lambda a, width: lax.dynamic_slice_in_dim(a, chip * width, width, axis=1)
    g_small = dict(
        norm_mix_g=row(ROW_G1), conv_w=shard(vec[ROW_CW:ROW_CW + CONV_WIDTH], quarter_d), conv_b=row(ROW_CB),
        b_rgate=shard(row(ROW_BR).reshape(HEADS, HEAD_DIM), quarter_h),
        b_igate=shard(row(ROW_BI).reshape(HEADS, HEAD_DIM), quarter_h), lru_lambda=row(ROW_LAM),
        sgu_ln_g=row(ROW_LG), sgu_ln_b=row(ROW_LB),
        sgu_w_s=ws_sum.reshape(CHUNK, GROUPS, CHUNK).transpose(1, 0, 2).reshape(GROUPS * CHUNK, CHUNK),
        sgu_b_s=vec[ROW_BS:ROW_BS + GROUPS, 0:CHUNK], norm_mlp_g=row(ROW_G2), norm_final_g=row(ROW_G3))
    loss = vec[ROW_LOSS, 0]
    small_names = list(g_small)
    given = dict(
        norm_mix_g=(norm_mix_g, m_norm_mix_g, v_norm_mix_g), conv_w=(conv_w, m_conv_w, v_conv_w),
        conv_b=(conv_b, m_conv_b, v_conv_b), b_rgate=(b_rgate, m_b_rgate, v_b_rgate),
        b_igate=(b_igate, m_b_igate, v_b_igate), lru_lambda=(lru_lambda, m_lru_lambda, v_lru_lambda),
        sgu_ln_g=(sgu_ln_g, m_sgu_ln_g, v_sgu_ln_g), sgu_ln_b=(sgu_ln_b, m_sgu_ln_b, v_sgu_ln_b),
        sgu_w_s=(sgu_w_s, m_sgu_w_s, v_sgu_w_s), sgu_b_s=(sgu_b_s, m_sgu_b_s, v_sgu_b_s),
        norm_mlp_g=(norm_mlp_g, m_norm_mlp_g, v_norm_mlp_g), norm_final_g=(norm_final_g, m_norm_final_g, v_norm_final_g))
    g2d = [g_small[nm] for nm in small_names]
    to2d = lambda a, g: a.reshape(g.shape)
    d_s, m_s, v_s = _adamw_small(
        g2d, *[[to2d(given[nm][q], g) for nm, g in zip(small_names, g2d)] for q in range(3)])

    shapes = dict(
        norm_mix_g=norm_mix_g, w_in=w_in, conv_w=conv_w, conv_b=conv_b, w_rgate=w_rgate, b_rgate=b_rgate,
        w_igate=w_igate, b_igate=b_igate, lru_lambda=lru_lambda, w_out_a=w_out_a, sgu_ln_g=sgu_ln_g,
        sgu_ln_b=sgu_ln_b, sgu_w_s=sgu_w_s, sgu_b_s=sgu_b_s, w_out_b=w_out_b, w_out=w_out, norm_mlp_g=norm_mlp_g,
        w_up=w_up, w_down=w_down, norm_final_g=norm_final_g)
    grads, deltas, new_m, new_v = {}, {}, {}, {}
    for nm, g, (d, nmom, nvar) in zip(names, full, big_out):
        grads[nm], deltas[nm], new_m[nm], new_v[nm] = g, d, nmom, nvar
    for p, nm in enumerate(small_names):
        grads[nm], deltas[nm], new_m[nm], new_v[nm] = g2d[p], d_s[p], m_s[p], v_s[p]
    order = list(shapes)
    out = [loss, grad_x[None]]
    for group in (grads, deltas, new_m, new_v):
        out += [group[nm].reshape(shapes[nm].shape) for nm in order]
    return tuple(out)
```

```python
import functools

import jax
import jax.numpy as jnp
from jax import lax
from jax.experimental import pallas as pl
from jax.experimental.pallas import tpu as pltpu

F32 = jnp.float32
BF16 = jnp.bfloat16
MESH = pl.DeviceIdType.MESH

D_MODEL = 1024
D_IN = 6 * D_MODEL
D_FF = 4 * D_MODEL
N_CHIPS = 4
IN_SHARD = D_IN // N_CHIPS
HEADS = 4
HEAD_DIM = D_MODEL // HEADS
GROUPS = 4
GROUP_DIM = D_MODEL // GROUPS
CHUNK = 128
CONV_WIDTH = 4
LRU_C = 8.0
NORM_EPS = 1e-6
LN_EPS = 1e-5

ADAM_LR = 0.001
ADAM_B1 = 0.9
ADAM_B2 = 0.999
ADAM_EPS = 1e-08
ADAM_WD = 0.01
ADAM_STEP = 10

SUBLANES = 8
MM_TILE = 512
SEQ_TILE = 256
DW_TILE = 2048
VMEM_LIMIT_BYTES = 56 * 1024 * 1024

GELU_K0 = 0.7978845608028654
GELU_K1 = 0.044715


def _params(n_grid_axes=1):
    return pltpu.CompilerParams(
        dimension_semantics=("arbitrary",) * n_grid_axes, vmem_limit_bytes=VMEM_LIMIT_BYTES)


def _resident(shape):
    nd = len(shape)
    return pl.BlockSpec(shape, lambda *_: (0,) * nd, pipeline_mode=pl.Buffered(1))


def _const(shape):
    nd = len(shape)
    return pl.BlockSpec(shape, lambda *_: (0,) * nd)


def _dot(a, b):
    return jnp.dot(a, b, preferred_element_type=F32)


def _dot_nt(a, b):
    return lax.dot_general(a, b, (((1,), (1,)), ((), ())), preferred_element_type=F32)


def _dot_tn(a, b):
    return lax.dot_general(a, b, (((0,), (0,)), ((), ())), preferred_element_type=F32)


def _gelu(x):
    t = jnp.tanh(GELU_K0 * x * (1.0 + GELU_K1 * x * x))
    return 0.5 * x * (1.0 + t)


def _gelu_and_grad(x):
    x2 = x * x
    t = jnp.tanh(GELU_K0 * x * (1.0 + GELU_K1 * x2))
    g = 0.5 * x * (1.0 + t)
    dg = 0.5 * (1.0 + t) + 0.5 * x * (1.0 - t * t) * (GELU_K0 * (1.0 + 3.0 * GELU_K1 * x2))
    return g, dg


def _rms(x):
    r = lax.rsqrt(jnp.mean(x * x, axis=-1, keepdims=True) + NORM_EPS)
    return x * r, r


def _rms_bwd(dn, xhat, r):
    return r * (dn - xhat * jnp.mean(dn * xhat, axis=-1, keepdims=True))


def _col_sum(v):
    return jnp.sum(v, axis=0, keepdims=True)


def _shift_down(x, tail8, k):
    xs = pltpu.roll(x, k, 0)
    ts = pltpu.roll(tail8, k, 0)
    ridx = lax.broadcasted_iota(jnp.int32, tail8.shape, 0)
    head = jnp.where(ridx < k, ts, xs[0:SUBLANES])
    return jnp.concatenate([head, xs[SUBLANES:]], axis=0)


def _shift_up(x, head8, k):
    n = x.shape[0]
    xs = pltpu.roll(x, n - k, 0)
    hs = pltpu.roll(head8, SUBLANES - k, 0)
    ridx = lax.broadcasted_iota(jnp.int32, head8.shape, 0)
    last = jnp.where(ridx >= SUBLANES - k, hs, xs[n - SUBLANES:n])
    return jnp.concatenate([xs[:n - SUBLANES], last], axis=0)


def _scan_forward(a, b, carry):
    n, cols = a.shape
    groups = n // SUBLANES
    a = a.reshape(groups, SUBLANES, cols)
    b = b.reshape(groups, SUBLANES, cols)
    sub = lax.broadcasted_iota(jnp.int32, a.shape, 1)
    for s in (1, 2, 4):
        a_s = pltpu.roll(a, s, 1)
        b_s = pltpu.roll(b, s, 1)
        m = sub >= s
        b = jnp.where(m, a * b_s + b, b)
        a = jnp.where(m, a * a_s, a)
    out = []
    for g in range(groups):
        h = a[g] * carry + b[g]
        out.append(h)
        carry = h[SUBLANES - 1:SUBLANES]
    return jnp.concatenate(out, axis=0), carry


def _scan_backward(a, b, carry):
    n, cols = a.shape
    groups = n // SUBLANES
    a = a.reshape(groups, SUBLANES, cols)
    b = b.reshape(groups, SUBLANES, cols)
    sub = lax.broadcasted_iota(jnp.int32, a.shape, 1)
    for s in (1, 2, 4):
        a_s = pltpu.roll(a, SUBLANES - s, 1)
        b_s = pltpu.roll(b, SUBLANES - s, 1)
        m = sub < SUBLANES - s
        b = jnp.where(m, a * b_s + b, b)
        a = jnp.where(m, a * a_s, a)
    out = [None] * groups
    for g in reversed(range(groups)):
        h = a[g] * carry + b[g]
        out[g] = h
        carry = h[0:1]
    return jnp.concatenate(out, axis=0), carry


def _softplus_neg(lam):
    e = jnp.exp(-jnp.abs(lam))
    u = 1.0 + e
    log1p_e = jnp.where(u == 1.0, e, jnp.log(u) * (e / jnp.where(u == 1.0, 1.0, u - 1.0)))
    return jnp.maximum(-lam, 0.0) + log1p_e


def _lru_gates(xa, tail8, cw_ref, cb_ref, wr_ref, br_ref, wi_ref, bi_ref, lam_ref):
    cw = cw_ref[...]
    xc = cb_ref[...] + cw[0:1] * xa
    for k in range(1, CONV_WIDTH):
        xc = xc + cw[k:k + 1] * _shift_down(xa, tail8, k)
    xcb = xc.astype(BF16)
    pre_r, pre_i = [], []
    for h in range(HEADS):
        cols = slice(h * HEAD_DIM, (h + 1) * HEAD_DIM)
        pre_r.append(_dot(xcb[:, cols], wr_ref[h]))
        pre_i.append(_dot(xcb[:, cols], wi_ref[h]))
    r = jax.nn.sigmoid(jnp.concatenate(pre_r, axis=1) + br_ref[...])
    ig = jax.nn.sigmoid(jnp.concatenate(pre_i, axis=1) + bi_ref[...])
    _, a, mult = _decay(r, lam_ref)
    return xc, r, ig, a, mult


def _decay(r, lam_ref):
    sp = _softplus_neg(lam_ref[...])
    log_a = ((-LRU_C) * sp) * r
    a = jnp.exp(log_a)
    th = jnp.tanh(log_a)
    return sp, a, jnp.sqrt((-2.0 * th) / (1.0 - th))


class _Job:
    def __init__(self, inputs, out_shape, n_sem, copies, aliases=None, n_local=0):
        self.inputs, self.out_shape, self.n_sem, self.copies = list(inputs), list(out_shape), n_sem, copies
        self.aliases, self.n_local = dict(aliases or {}), n_local


def _fused_call(body, jobs, *, name, grid, in_specs, out_specs, out_shape, scratch_shapes=(),
                input_output_aliases=None, compiler_params=None):
    single = not isinstance(out_shape, (list, tuple))
    out_specs = [out_specs] if single else list(out_specs)
    out_shape = [out_shape] if single else list(out_shape)
    n_scr = len(scratch_shapes)
    in_specs, scratch_shapes = list(in_specs), list(scratch_shapes)
    n_in, n_out = len(in_specs), len(out_shape)
    aliases = dict(input_output_aliases or {})
    in_at, out_at = [], []
    for job in jobs:
        in_at.append(len(in_specs))
        out_at.append(len(out_shape))
        for i, o in job.aliases.items():
            aliases[len(in_specs) + i] = len(out_shape) + o
        in_specs += [ANY] * len(job.inputs)
        out_specs += [ANY] * len(job.out_shape)
        out_shape += job.out_shape
        scratch_shapes += [pltpu.SemaphoreType.DMA((job.n_sem,)), pltpu.SemaphoreType.DMA((job.n_sem,)),
                           pltpu.SemaphoreType.DMA((max(job.n_local, 1),))]
    n_in_all, n_out_all = len(in_specs), len(out_shape)

    def full_body(*refs):
        ins, outs, scr = refs[:n_in_all], refs[n_in_all:n_in_all + n_out_all], refs[n_in_all + n_out_all:]

        def copies(q):
            job = jobs[q]
            return job.copies(ins[in_at[q]:in_at[q] + len(job.inputs)], outs[out_at[q]:out_at[q] + len(job.out_shape)],
                              *scr[n_scr + 3 * q:n_scr + 3 * q + 3])

        def start():
            for q in range(len(jobs)):
                sends, _, local = copies(q)
                for cp in local + sends:
                    cp.start()

        def finish():
            every = [copies(q) for q in range(len(jobs))]
            for _, arrivals, _ in every:
                for cp in arrivals:
                    cp.wait_recv()
            for sends, _, local in every:
                for cp in sends:
                    cp.wait_send()
                for cp in local:
                    cp.wait()

        if not grid:
            start()
            finish()
            return
        ids = [pl.program_id(a) for a in range(len(grid))]
        if jobs:
            pl.when(functools.reduce(jnp.logical_and, [i == 0 for i in ids]))(start)
        body(*ins[:n_in], *outs[:n_out], *scr[:n_scr])
        if jobs:
            pl.when(functools.reduce(jnp.logical_and, [i == g - 1 for i, g in zip(ids, grid)]))(finish)

    call = pl.pallas_call(
        full_body, name=name, grid=grid, in_specs=in_specs, out_specs=out_specs, out_shape=out_shape,
        scratch_shapes=scratch_shapes, input_output_aliases=aliases, compiler_params=compiler_params)

    def run(*args):
        res = call(*args, *[a for job in jobs for a in job.inputs])
        mine = res[0] if single else list(res[:n_out])
        return mine, [list(res[at:at + len(job.out_shape)]) for at, job in zip(out_at, jobs)]

    return run


def _fwd_in(x, g1, w_in_st, jobs=()):
    t = x.shape[0]

    def body(x_ref, g_ref, w_ref, z_ref, n_ref):
        xhat, _ = _rms(x_ref[...])
        n = (xhat * g_ref[...]).astype(BF16)
        n_ref[...] = n
        for k in range(N_CHIPS):
            z_ref[:, k * IN_SHARD:(k + 1) * IN_SHARD] = _dot(n, w_ref[k])

    return _fused_call(
        body, jobs, name="fwd_in", grid=(t // MM_TILE,),
        in_specs=[pl.BlockSpec((MM_TILE, D_MODEL), lambda i: (i, 0)), _const((1, D_MODEL)),
                  _resident((N_CHIPS, D_MODEL, IN_SHARD))],
        out_specs=[pl.BlockSpec((MM_TILE, D_IN), lambda i: (i, 0)),
                   pl.BlockSpec((MM_TILE, D_MODEL), lambda i: (i, 0))],
        out_shape=[jax.ShapeDtypeStruct((t, D_IN), F32), jax.ShapeDtypeStruct((t, D_MODEL), BF16)],
        compiler_params=_params(),
    )(x, g1, w_in_st)


def _fwd_lru(z, conv_w, conv_b, wr, br, wi, bi, lam, jobs=()):
    t = z.shape[0]

    def body(xa_ref, ga_ref, cw_ref, cb_ref, wr_ref, br_ref, wi_ref, bi_ref, lam_ref, ya_ref, h_ref, xc_ref, r_ref,
             ig_ref, tail_ref, carry_ref):
        @pl.when(pl.program_id(0) == 0)
        def _():
            tail_ref[...] = jnp.zeros_like(tail_ref)
            carry_ref[...] = jnp.zeros_like(carry_ref)

        xa = xa_ref[...]
        xc, r, ig, a, mult = _lru_gates(xa, tail_ref[...], cw_ref, cb_ref, wr_ref, br_ref, wi_ref, bi_ref, lam_ref)
        tail_ref[...] = xa[SEQ_TILE - SUBLANES:]
        xc_ref[...], r_ref[...], ig_ref[...] = xc, r, ig
        h, carry = _scan_forward(a, xc * ig * mult, carry_ref[...])
        carry_ref[...] = carry
        h_ref[...] = h
        ya_ref[...] = (h * _gelu(ga_ref[...])).astype(BF16)

    tile = lambda j: pl.BlockSpec((SEQ_TILE, D_MODEL), lambda i: (i, j))
    return _fused_call(
        body, jobs, name="fwd_lru", grid=(t // SEQ_TILE,),
        in_specs=[tile(0), tile(1), _const((CONV_WIDTH, D_MODEL)), _const((1, D_MODEL)),
                  _resident((HEADS, HEAD_DIM, HEAD_DIM)), _const((1, D_MODEL)),
                  _resident((HEADS, HEAD_DIM, HEAD_DIM)), _const((1, D_MODEL)), _const((1, D_MODEL))],
        out_specs=[tile(0)] * 5,
        out_shape=[jax.ShapeDtypeStruct((t, D_MODEL), BF16)] + [jax.ShapeDtypeStruct((t, D_MODEL), F32)] * 4,
        scratch_shapes=[pltpu.VMEM((SUBLANES, D_MODEL), F32), pltpu.VMEM((1, D_MODEL), F32)],
        compiler_params=_params(),
    )(z, z, conv_w, conv_b, wr, br, wi, bi, lam)


def _sgu_forward_parts(ub, vb, lg_ref, lb_ref):
    u, du = _gelu_and_grad(ub)
    vg, dvg = _gelu_and_grad(vb)
    mu = jnp.mean(vg, axis=-1, keepdims=True)
    d = vg - mu
    rstd = lax.rsqrt(jnp.mean(d * d, axis=-1, keepdims=True) + LN_EPS)
    vhat = d * rstd
    vn = (vhat * lg_ref[...] + lb_ref[...]).astype(BF16)
    return u, du, dvg, rstd, vhat, vn


def _causal_mask():
    rows = lax.broadcasted_iota(jnp.int32, (CHUNK, CHUNK), 0)
    cols = lax.broadcasted_iota(jnp.int32, (CHUNK, CHUNK), 1)
    return rows >= cols


def _fwd_sgu(z, ln_g, ln_b, w_s, bias_full, jobs=()):
    t = z.shape[0]

    def body(ub_ref, vb_ref, lg_ref, lb_ref, ws_ref, bias_ref, yb_ref):
        u, _, _, _, _, vn = _sgu_forward_parts(ub_ref[...], vb_ref[...], lg_ref, lb_ref)
        mask = _causal_mask()
        wm = [jnp.where(mask, ws_ref[g], 0.0).astype(BF16) for g in range(GROUPS)]
        for c in range(SEQ_TILE // CHUNK):
            rows = slice(c * CHUNK, (c + 1) * CHUNK)
            for g in range(GROUPS):
                cols = slice(g * GROUP_DIM, (g + 1) * GROUP_DIM)
                sp = _dot(wm[g], vn[rows, cols]) + bias_ref[:, cols]
                yb_ref[rows, cols] = (u[rows, cols] * sp).astype(BF16)

    tile = lambda j: pl.BlockSpec((SEQ_TILE, D_MODEL), lambda i: (i, j))
    return _fused_call(
        body, jobs, name="fwd_sgu", grid=(t // SEQ_TILE,),
        in_specs=[tile(2), tile(3), _const((1, D_MODEL)), _const((1, D_MODEL)),
                  _const((GROUPS, CHUNK, CHUNK)), _const((CHUNK, D_MODEL))],
        out_specs=tile(0),
        out_shape=jax.ShapeDtypeStruct((t, D_MODEL), BF16),
        compiler_params=_params(),
    )(z, z, ln_g, ln_b, w_s, bias_full)


def _fwd_merge(ya, yb, z, x, w_oa, w_ob, w_out, g2, jobs=()):
    t = x.shape[0]

    def body(ya_ref, yb_ref, m_ref, x_ref, woa_ref, wob_ref, wout_ref, g_ref, pa_ref, pb_ref, h1_ref, n2_ref):
        pa = _dot(ya_ref[...], woa_ref[...])
        pb = _dot(yb_ref[...], wob_ref[...])
        pa_ref[...] = pa
        pb_ref[...] = pb
        merged = jax.nn.sigmoid(m_ref[:, :D_MODEL]) * pa + jax.nn.sigmoid(m_ref[:, D_MODEL:]) * pb
        h1 = x_ref[...] + _dot(merged.astype(BF16), wout_ref[...])
        h1_ref[...] = h1
        xhat, _ = _rms(h1)
        n2_ref[...] = (xhat * g_ref[...]).astype(BF16)

    tile = pl.BlockSpec((MM_TILE, D_MODEL), lambda i: (i, 0))
    sq = _resident((D_MODEL, D_MODEL))
    return _fused_call(
        body, jobs, name="fwd_merge", grid=(t // MM_TILE,),
        in_specs=[tile, tile, pl.BlockSpec((MM_TILE, 2 * D_MODEL), lambda i: (i, 2)), tile, sq, sq, sq,
                  _const((1, D_MODEL))],
        out_specs=[tile, tile, tile, tile],
        out_shape=[jax.ShapeDtypeStruct((t, D_MODEL), F32)] * 3 + [jax.ShapeDtypeStruct((t, D_MODEL), BF16)],
        compiler_params=_params(),
    )(ya, yb, z, x, w_oa, w_ob, w_out, g2)


def _fwd_mlp(n2, h1, target, w_up_st, w_down, g3, jobs=()):
    t = n2.shape[0]

    def body(n2_ref, h1_ref, tgt_ref, wup_ref, wdown_ref, g_ref, up_ref, act_ref, dh2_ref, dh2b_ref, loss_ref,
             dg3_ref):
        @pl.when(pl.program_id(0) == 0)
        def _():
            loss_ref[...] = jnp.zeros_like(loss_ref)
            dg3_ref[...] = jnp.zeros_like(dg3_ref)

        n2 = n2_ref[...]
        h2 = h1_ref[...]
        for k in range(N_CHIPS):
            cols = slice(k * D_MODEL, (k + 1) * D_MODEL)
            up = _dot(n2, wup_ref[k])
            up_ref[:, cols] = up.astype(BF16)
            r = jnp.maximum(up, 0.0)
            act = (r * r).astype(BF16)
            act_ref[:, cols] = act
            h2 = h2 + _dot(act, wdown_ref[cols, :])
        xhat, r3 = _rms(h2)
        diff = xhat * g_ref[...] - tgt_ref[...]
        sq = jnp.sum(diff * diff, axis=1, keepdims=True)
        loss_ref[...] = loss_ref[...] + (0.5 / D_MODEL) * jnp.sum(sq, axis=0, keepdims=True)
        dy = diff * (1.0 / D_MODEL)
        dg3_ref[...] = dg3_ref[...] + _col_sum(dy * xhat)
        dh2 = _rms_bwd(dy * g_ref[...], xhat, r3)
        dh2_ref[...] = dh2
        dh2b_ref[...] = dh2.astype(BF16)

    tile = pl.BlockSpec((MM_TILE, D_MODEL), lambda i: (i, 0))
    wide = pl.BlockSpec((MM_TILE, D_FF), lambda i: (i, 0))
    return _fused_call(
        body, jobs, name="fwd_mlp", grid=(t // MM_TILE,),
        in_specs=[tile, tile, tile, _resident((N_CHIPS, D_MODEL, D_MODEL)), _resident((D_FF, D_MODEL)),
                  _const((1, D_MODEL))],
        out_specs=[wide, wide, tile, tile, _const((SUBLANES, 128)), _const((1, D_MODEL))],
        out_shape=[jax.ShapeDtypeStruct((t, D_FF), BF16), jax.ShapeDtypeStruct((t, D_FF), BF16),
                   jax.ShapeDtypeStruct((t, D_MODEL), F32), jax.ShapeDtypeStruct((t, D_MODEL), BF16),
                   jax.ShapeDtypeStruct((SUBLANES, 128), F32), jax.ShapeDtypeStruct((1, D_MODEL), F32)],
        compiler_params=_params(),
    )(n2, h1, target, w_up_st, w_down, g3)


def _bwd_mlp(dh2, dh2b, up, h1, w_up_st, w_down, g2, jobs=()):
    t = dh2.shape[0]

    def body(dh2_ref, dh2b_ref, up_ref, h1_ref, wup_ref, wdown_ref, g_ref, dup_ref, dh1_ref, dg2_ref):
        @pl.when(pl.program_id(0) == 0)
        def _():
            dg2_ref[...] = jnp.zeros_like(dg2_ref)

        dh2b = dh2b_ref[...]
        dn2 = jnp.zeros((MM_TILE, D_MODEL), F32)
        for k in range(N_CHIPS):
            cols = slice(k * D_MODEL, (k + 1) * D_MODEL)
            dact = _dot_nt(dh2b, wdown_ref[cols, :])
            dup = (dact * (2.0 * jnp.maximum(up_ref[:, cols].astype(F32), 0.0))).astype(BF16)
            dup_ref[:, cols] = dup
            dn2 = dn2 + _dot_nt(dup, wup_ref[k])
        xhat, r2 = _rms(h1_ref[...])
        dg2_ref[...] = dg2_ref[...] + _col_sum(dn2 * xhat)
        dh1_ref[...] = dh2_ref[...] + _rms_bwd(dn2 * g_ref[...], xhat, r2)

    tile = pl.BlockSpec((MM_TILE, D_MODEL), lambda i: (i, 0))
    wide = pl.BlockSpec((MM_TILE, D_FF), lambda i: (i, 0))
    return _fused_call(
        body, jobs, name="bwd_mlp", grid=(t // MM_TILE,),
        in_specs=[tile, tile, wide, tile, _resident((N_CHIPS, D_MODEL, D_MODEL)), _resident((D_FF, D_MODEL)),
                  _const((1, D_MODEL))],
        out_specs=[wide, tile, _const((1, D_MODEL))],
        out_shape=[jax.ShapeDtypeStruct((t, D_FF), BF16), jax.ShapeDtypeStruct((t, D_MODEL), F32),
                   jax.ShapeDtypeStruct((1, D_MODEL), F32)],
        compiler_params=_params(),
    )(dh2, dh2b, up, h1, w_up_st, w_down, g2)


def _bwd_merge(dh1, pa, pb, z, w_oa, w_ob, w_out, jobs=()):
    t = dh1.shape[0]

    def body(dh1_ref, pa_ref, pb_ref, m_ref, woa_ref, wob_ref, wout_ref, dz_ref, dya_ref, dyb_ref, mg_ref,
             dpa_ref, dpb_ref, dh1b_ref):
        dh1b = dh1_ref[...].astype(BF16)
        dh1b_ref[...] = dh1b
        dm = _dot_nt(dh1b, wout_ref[...])
        pa = pa_ref[...]
        pb = pb_ref[...]
        sa = jax.nn.sigmoid(m_ref[:, :D_MODEL])
        sb = jax.nn.sigmoid(m_ref[:, D_MODEL:])
        mg_ref[...] = (sa * pa + sb * pb).astype(BF16)
        dz_ref[:, :D_MODEL] = (dm * pa * sa * (1.0 - sa)).astype(BF16)
        dz_ref[:, D_MODEL:] = (dm * pb * sb * (1.0 - sb)).astype(BF16)
        dpa = (dm * sa).astype(BF16)
        dpb = (dm * sb).astype(BF16)
        dpa_ref[...] = dpa
        dpb_ref[...] = dpb
        dya_ref[...] = _dot_nt(dpa, woa_ref[...])
        dyb_ref[...] = _dot_nt(dpb, wob_ref[...])

    tile = pl.BlockSpec((MM_TILE, D_MODEL), lambda i: (i, 0))
    pair = pl.BlockSpec((MM_TILE, 2 * D_MODEL), lambda i: (i, 2))
    sq = _resident((D_MODEL, D_MODEL))
    act_bf = jax.ShapeDtypeStruct((t, D_MODEL), BF16)
    return _fused_call(
        body, jobs, name="bwd_merge", grid=(t // MM_TILE,),
        in_specs=[tile, tile, tile, pair, sq, sq, sq],
        out_specs=[pair, tile, tile, tile, tile, tile, tile],
        out_shape=[jax.ShapeDtypeStruct((t, D_IN), BF16), jax.ShapeDtypeStruct((t, D_MODEL), F32),
                   jax.ShapeDtypeStruct((t, D_MODEL), F32), act_bf, act_bf, act_bf, act_bf],
        compiler_params=_params(),
    )(dh1, pa, pb, z, w_oa, w_ob, w_out)


def _bwd_sgu(dz, dyb, z, ln_g, ln_b, w_s, bias_full, jobs=()):
    t = dyb.shape[0]
    n_tiles = t // SEQ_TILE

    def body(dz_any, dyb_ref, ub_ref, vb_ref, lg_ref, lb_ref, ws_ref, bias_ref, dz_ref, dlg_ref, dlb_ref, dws_ref,
             dbs_ref, dvn_ref, dsp_acc):
        del dz_any
        i = pl.program_id(0)

        @pl.when(i == 0)
        def _():
            dlg_ref[...] = jnp.zeros_like(dlg_ref)
            dlb_ref[...] = jnp.zeros_like(dlb_ref)
            dws_ref[...] = jnp.zeros_like(dws_ref)
            dsp_acc[...] = jnp.zeros_like(dsp_acc)

        u, du, dvg, rstd, vhat, vn = _sgu_forward_parts(ub_ref[...], vb_ref[...], lg_ref, lb_ref)
        dyb = dyb_ref[...]
        mask = _causal_mask()
        wm = [jnp.where(mask, ws_ref[g], 0.0).astype(BF16) for g in range(GROUPS)]
        for c in range(SEQ_TILE // CHUNK):
            rows = slice(c * CHUNK, (c + 1) * CHUNK)
            for g in range(GROUPS):
                cols = slice(g * GROUP_DIM, (g + 1) * GROUP_DIM)
                vn_blk = vn[rows, cols]
                sp = _dot(wm[g], vn_blk) + bias_ref[:, cols]
                dyb_blk = dyb[rows, cols]
                dz_ref[rows, cols] = (dyb_blk * sp * du[rows, cols]).astype(BF16)
                dsp = dyb_blk * u[rows, cols]
                dsp_acc[:, cols] = dsp_acc[:, cols] + dsp
                dspb = dsp.astype(BF16)
                dvn_ref[rows, cols] = _dot_tn(wm[g], dspb)
                wcols = slice(g * CHUNK, (g + 1) * CHUNK)
                dws_ref[:, wcols] = dws_ref[:, wcols] + jnp.where(mask, _dot_nt(dspb, vn_blk), 0.0)
        dvn = dvn_ref[...]
        dlg_ref[...] = dlg_ref[...] + _col_sum(dvn * vhat)
        dlb_ref[...] = dlb_ref[...] + _col_sum(dvn)
        dvhat = dvn * lg_ref[...]
        dvgel = rstd * (dvhat - jnp.mean(dvhat, axis=-1, keepdims=True)
                        - vhat * jnp.mean(dvhat * vhat, axis=-1, keepdims=True))
        dz_ref[:, D_MODEL:] = (dvgel * dvg).astype(BF16)

        @pl.when(i == n_tiles - 1)
        def _():
            lane = lax.broadcasted_iota(jnp.int32, (CHUNK, 128), 1)
            out = jnp.zeros((CHUNK, 128), F32)
            for g in range(GROUPS):
                s = jnp.sum(dsp_acc[:, g * GROUP_DIM:(g + 1) * GROUP_DIM], axis=1, keepdims=True)
                out = out + jnp.where(lane == g, s, 0.0)
            dbs_ref[...] = out

    tile = lambda j: pl.BlockSpec((SEQ_TILE, D_MODEL), lambda i: (i, j))
    return _fused_call(
        body, jobs, name="bwd_sgu", grid=(n_tiles,),
        in_specs=[pl.BlockSpec(memory_space=pl.ANY), tile(0), tile(2), tile(3), _const((1, D_MODEL)),
                  _const((1, D_MODEL)), _const((GROUPS, CHUNK, CHUNK)), _const((CHUNK, D_MODEL))],
        out_specs=[pl.BlockSpec((SEQ_TILE, 2 * D_MODEL), lambda i: (i, 1)), _const((1, D_MODEL)),
                   _const((1, D_MODEL)), _const((CHUNK, GROUPS * CHUNK)), _const((CHUNK, 128))],
        out_shape=[jax.ShapeDtypeStruct((t, D_IN), BF16), jax.ShapeDtypeStruct((1, D_MODEL), F32),
                   jax.ShapeDtypeStruct((1, D_MODEL), F32), jax.ShapeDtypeStruct((CHUNK, GROUPS * CHUNK), F32),
                   jax.ShapeDtypeStruct((CHUNK, 128), F32)],
        scratch_shapes=[pltpu.VMEM((SEQ_TILE, D_MODEL), F32), pltpu.VMEM((CHUNK, D_MODEL), F32)],
        input_output_aliases={0: 0},
        compiler_params=_params(),
    )(dz, dyb, z, z, ln_g, ln_b, w_s, bias_full)


def _bwd_lru(dz, dya, z, h, xc, r, ig, conv_w, wr, wi, lam, jobs=()):
    t = dya.shape[0]
    n_tiles = t // SEQ_TILE
    per_tile = SEQ_TILE // SUBLANES

    def body(dz_any, dya_ref, xa_ref, ga_ref, h_ref, h_prev_ref, xc_ref, r_ref, ig_ref, cw_ref, wr_ref, wi_ref, lam_ref,
             dz_ref, dcw_ref, dcb_ref, dwr_ref, dbr_ref, dwi_ref, dbi_ref, dlam_ref, lam_carry, dxc_head):
        del dz_any
        i = pl.program_id(0)

        @pl.when(i == 0)
        def _():
            for ref in (dcw_ref, dcb_ref, dwr_ref, dbr_ref, dwi_ref, dbi_ref, dlam_ref, lam_carry, dxc_head):
                ref[...] = jnp.zeros_like(ref)

        first_tile = i == n_tiles - 1
        h_tail = jnp.where(first_tile, 0.0, h_prev_ref[...])
        xc, r, ig = xc_ref[...], r_ref[...], ig_ref[...]
        xcb = xc.astype(BF16)
        sp, a, mult = _decay(r, lam_ref)
        h = h_ref[...]
        h_prev = _shift_down(h, h_tail, 1)
        dya = dya_ref[...]
        gg, dgg = _gelu_and_grad(ga_ref[...])
        dz_ref[:, D_MODEL:] = (dya * h * dgg).astype(BF16)
        ones = jnp.ones((SUBLANES, D_MODEL), F32)
        lam_t, lam_first = _scan_backward(_shift_up(a, ones, 1), dya * gg, lam_carry[...])
        lam_carry[...] = a[0:1] * lam_first
        dmult = lam_t * xc * ig
        dla = lam_t * h_prev * a - dmult * (a * a) / mult
        dr = dla * ((-LRU_C) * sp)
        dlam_ref[...] = dlam_ref[...] + _col_sum(dla * r) * (LRU_C * jax.nn.sigmoid(-lam_ref[...]))
        dpr = dr * r * (1.0 - r)
        dpi = lam_t * xc * mult * ig * (1.0 - ig)
        dbr_ref[...] = dbr_ref[...] + _col_sum(dpr)
        dbi_ref[...] = dbi_ref[...] + _col_sum(dpi)
        dprb = dpr.astype(BF16)
        dpib = dpi.astype(BF16)
        dxc_gate = []
        for hd in range(HEADS):
            cols = slice(hd * HEAD_DIM, (hd + 1) * HEAD_DIM)
            dxc_gate.append(_dot_nt(dprb[:, cols], wr_ref[hd]) + _dot_nt(dpib[:, cols], wi_ref[hd]))
            dwr_ref[hd] = dwr_ref[hd] + _dot_tn(xcb[:, cols], dprb[:, cols])
            dwi_ref[hd] = dwi_ref[hd] + _dot_tn(xcb[:, cols], dpib[:, cols])
        dxc = lam_t * ig * mult + jnp.concatenate(dxc_gate, axis=1)
        dcb_ref[...] = dcb_ref[...] + _col_sum(dxc)
        cw = cw_ref[...]
        head = dxc_head[...]
        xa = xa_ref[...]
        dxa = cw[0:1] * dxc
        dcw_ref[0:1, :] = dcw_ref[0:1, :] + _col_sum(dxc * xa)
        for k in range(1, CONV_WIDTH):
            dxc_k = _shift_up(dxc, head, k)
            dxa = dxa + cw[k:k + 1] * dxc_k
            dcw_ref[k:k + 1, :] = dcw_ref[k:k + 1, :] + _col_sum(dxc_k * xa)
        dxc_head[...] = dxc[0:SUBLANES]
        dz_ref[:, :D_MODEL] = dxa.astype(BF16)

    rev = lambda i: n_tiles - 1 - i
    tile = lambda j: pl.BlockSpec((SEQ_TILE, D_MODEL), lambda i: (rev(i), j))
    prev8 = pl.BlockSpec((SUBLANES, D_MODEL), lambda i: (jnp.maximum(rev(i) * per_tile - 1, 0), 0))
    vec = _const((1, D_MODEL))
    gate_w = _resident((HEADS, HEAD_DIM, HEAD_DIM))
    vec_shape = jax.ShapeDtypeStruct((1, D_MODEL), F32)
    gate_shape = jax.ShapeDtypeStruct((HEADS, HEAD_DIM, HEAD_DIM), F32)
    return _fused_call(
        body, jobs, name="bwd_lru", grid=(n_tiles,),
        in_specs=[pl.BlockSpec(memory_space=pl.ANY), tile(0), tile(0), tile(1), tile(0), prev8] + [tile(0)] * 3
                 + [_const((CONV_WIDTH, D_MODEL)), gate_w, gate_w, vec],
        out_specs=[pl.BlockSpec((SEQ_TILE, 2 * D_MODEL), lambda i: (rev(i), 0)), _const((SUBLANES, D_MODEL)), vec,
                   _const((HEADS, HEAD_DIM, HEAD_DIM)), vec, _const((HEADS, HEAD_DIM, HEAD_DIM)), vec, vec],
        out_shape=[jax.ShapeDtypeStruct((t, D_IN), BF16), jax.ShapeDtypeStruct((SUBLANES, D_MODEL), F32), vec_shape,
                   gate_shape, vec_shape, gate_shape, vec_shape, vec_shape],
        scratch_shapes=[pltpu.VMEM((1, D_MODEL), F32), pltpu.VMEM((SUBLANES, D_MODEL), F32)],
        input_output_aliases={0: 0},
        compiler_params=_params(),
    )(dz, dya, z, z, h, h, xc, r, ig, conv_w, wr, wi, lam)


def _bwd_in(dz, x, dh1, w_in_st, g1, jobs=()):
    t = x.shape[0]

    def body(dz_ref, x_ref, dh1_ref, w_ref, g_ref, dx_ref, dg1_ref):
        @pl.when(pl.program_id(0) == 0)
        def _():
            dg1_ref[...] = jnp.zeros_like(dg1_ref)

        dn1 = jnp.zeros((MM_TILE, D_MODEL), F32)
        for k in range(N_CHIPS):
            dn1 = dn1 + _dot_nt(dz_ref[:, k * IN_SHARD:(k + 1) * IN_SHARD], w_ref[k])
        xhat, r1 = _rms(x_ref[...])
        dg1_ref[...] = dg1_ref[...] + _col_sum(dn1 * xhat)
        dx_ref[...] = dh1_ref[...] + _rms_bwd(dn1 * g_ref[...], xhat, r1)

    tile = pl.BlockSpec((MM_TILE, D_MODEL), lambda i: (i, 0))
    return _fused_call(
        body, jobs, name="bwd_in", grid=(t // MM_TILE,),
        in_specs=[pl.BlockSpec((MM_TILE, D_IN), lambda i: (i, 0)), tile, tile,
                  _resident((N_CHIPS, D_MODEL, IN_SHARD)), _const((1, D_MODEL))],
        out_specs=[tile, _const((1, D_MODEL))],
        out_shape=[jax.ShapeDtypeStruct((t, D_MODEL), F32), jax.ShapeDtypeStruct((1, D_MODEL), F32)],
        compiler_params=_params(),
    )(dz, x, dh1, w_in_st, g1)


def _weight_grad(name, a, b, n_blocks, a_varies, b_varies, width, jobs=()):
    t = a.shape[0]
    rows = min(DW_TILE, t)
    n_t = t // rows

    def body(a_ref, b_ref, o_ref, acc_ref):
        s = pl.program_id(1)
        part = _dot_tn(a_ref[...], b_ref[...])

        @pl.when(s == 0)
        def _():
            acc_ref[...] = part

        @pl.when(s > 0)
        def _():
            acc_ref[...] = acc_ref[...] + part

        @pl.when(s == n_t - 1)
        def _():
            o_ref[...] = acc_ref[...].astype(BF16)

    return _fused_call(
        body, jobs, name=name, grid=(n_blocks, n_t),
        in_specs=[pl.BlockSpec((rows, D_MODEL), (lambda j, s: (s, j)) if a_varies else (lambda j, s: (s, 0))),
                  pl.BlockSpec((rows, width), (lambda j, s: (s, j)) if b_varies else (lambda j, s: (s, 0)))],
        out_specs=pl.BlockSpec((None, D_MODEL, width), lambda j, s: (j, 0, 0)),
        out_shape=jax.ShapeDtypeStruct((n_blocks, D_MODEL, width), BF16),
        scratch_shapes=[pltpu.VMEM((D_MODEL, width), F32)],
        compiler_params=_params(2),
    )(a, b)


def _place():
    x, y, c = lax.axis_index("x"), lax.axis_index("y"), lax.axis_index("c")
    other_chips = [(1 - x, y), (x, 1 - y), (1 - x, 1 - y)]
    return x, y, c, other_chips


def _chip_index(px, py):
    return 2 * px + py


ANY = pl.BlockSpec(memory_space=pl.ANY)


def _comm_call(name, jobs):
    return _fused_call(None, jobs, name=name, grid=(), in_specs=[], out_specs=[], out_shape=[])()[1]


def _gather_ici_job(shards):
    n = len(shards)
    halves = [s.shape[0] // 2 for s in shards]

    def copies(ins, outs, send, recv, local):
        x, y, c, chips = _place()
        me, sibling = (x, y, c), (x, y, 1 - c)

        def block(w, place):
            return outs[w].at[_chip_index(place[0], place[1]), pl.ds(place[2] * halves[w], halves[w]), :]

        def copy(w, k, blk, to, src=None):
            return pltpu.make_async_remote_copy(
                src_ref=block(w, blk) if src is None else src, dst_ref=block(w, blk),
                send_sem=send.at[4 * w + k], recv_sem=recv.at[4 * w + k], device_id=to, device_id_type=MESH)

        sends, arrivals, own = [], [], []
        for w in range(n):
            src = ins[w].at[pl.ds(c * halves[w], halves[w]), :]
            own.append(pltpu.make_async_copy(src, block(w, me), local.at[w]))
            sends.append(copy(w, 0, me, sibling, src))
            arrivals.append(copy(w, 0, sibling, me))
            for j, chip in enumerate(chips):
                sends.append(copy(w, 1 + j, me, (*chip, c), src))
                arrivals.append(copy(w, 1 + j, (*chip, c), me))
        return sends, arrivals, own

    return _Job(shards, [jax.ShapeDtypeStruct((N_CHIPS,) + s.shape, s.dtype) for s in shards], 4 * n, copies,
                n_local=n)


def _gather_pass_job(stacked):
    n = len(stacked)
    halves = [s.shape[1] // 2 for s in stacked]

    def copies(ins, outs, send, recv, local):
        del ins, local
        x, y, c, chips = _place()

        def copy(w, j, chip, pc, to):
            blk = outs[w].at[_chip_index(*chip), pl.ds(pc * halves[w], halves[w]), :]
            return pltpu.make_async_remote_copy(
                src_ref=blk, dst_ref=blk, send_sem=send.at[3 * w + j], recv_sem=recv.at[3 * w + j], device_id=to,
                device_id_type=MESH)

        sends = [copy(w, j, chip, c, (x, y, 1 - c)) for w in range(n) for j, chip in enumerate(chips)]
        arrivals = [copy(w, j, chip, 1 - c, (x, y, c)) for w in range(n) for j, chip in enumerate(chips)]
        return sends, arrivals, []

    return _Job(stacked, [jax.ShapeDtypeStruct(s.shape, s.dtype) for s in stacked], 3 * n, copies,
                aliases={w: w for w in range(n)})


def _gather_small_job(block):
    def copies(ins, outs, send, recv, local):
        x, y, c, chips = _place()

        def copy(j, chip_from, to):
            return pltpu.make_async_remote_copy(
                src_ref=ins[0], dst_ref=outs[0].at[_chip_index(*chip_from)], send_sem=send.at[j],
                recv_sem=recv.at[j], device_id=to, device_id_type=MESH)

        own = [pltpu.make_async_copy(ins[0], outs[0].at[_chip_index(x, y)], local.at[0])]
        sends = [copy(j, (x, y), (*chip, c)) for j, chip in enumerate(chips)]
        arrivals = [copy(j, chip, (x, y, c)) for j, chip in enumerate(chips)]
        return sends, arrivals, own

    return _Job([block], [jax.ShapeDtypeStruct((N_CHIPS,) + block.shape, block.dtype)], 3, copies, n_local=1)


def _pair_send_job(grads):
    n = len(grads)
    halves = [g.shape[1] // 2 for g in grads]

    def copies(ins, outs, send, recv, local):
        del local
        x, y, c, _ = _place()
        sends = [pltpu.make_async_remote_copy(
            src_ref=ins[w].at[:, pl.ds((1 - c) * halves[w], halves[w]), :], dst_ref=outs[w], send_sem=send.at[w],
            recv_sem=recv.at[w], device_id=(x, y, 1 - c), device_id_type=MESH) for w in range(n)]
        return sends, sends, []

    return _Job(grads, [jax.ShapeDtypeStruct((N_CHIPS, h, g.shape[2]), g.dtype) for g, h in zip(grads, halves)], n,
                copies)


def _row_block(rows, limit=256):
    return min(rows, limit)


def _pair_add(name, core, mine, theirs):
    _, _, h, cols = mine.shape
    rb = _row_block(h, 512)

    def body(core_ref, a_ref, b_ref, o_ref):
        del core_ref
        o_ref[...] = (a_ref[...].astype(F32) + b_ref[...].astype(F32)).astype(BF16)

    return pl.pallas_call(
        body, name=name,
        grid_spec=pltpu.PrefetchScalarGridSpec(
            num_scalar_prefetch=1, grid=(N_CHIPS, h // rb),
            in_specs=[pl.BlockSpec((None, None, rb, cols), lambda k, r, core_ref: (k, core_ref[0], r, 0)),
                      pl.BlockSpec((None, rb, cols), lambda k, r, core_ref: (k, r, 0))],
            out_specs=pl.BlockSpec((None, rb, cols), lambda k, r, core_ref: (k, r, 0))),
        out_shape=jax.ShapeDtypeStruct(theirs.shape, BF16),
        compiler_params=_params(2),
    )(core, mine, theirs)


def _chip_exchange_job(sums):
    n = len(sums)

    def copies(ins, outs, send, recv, local):
        del local
        _, _, c, chips = _place()
        sends = [pltpu.make_async_remote_copy(
            src_ref=ins[w].at[_chip_index(*chip)], dst_ref=outs[w].at[j], send_sem=send.at[3 * w + j],
            recv_sem=recv.at[3 * w + j], device_id=(*chip, c), device_id_type=MESH)
            for w in range(n) for j, chip in enumerate(chips)]
        return sends, sends, []

    return _Job(sums, [jax.ShapeDtypeStruct((N_CHIPS - 1,) + s.shape[1:], s.dtype) for s in sums], 3 * n, copies)


def _chip_sum(name, place, mine, theirs):
    _, h, cols = mine.shape
    rb = _row_block(h, 512)

    def body(place_ref, p_ref, q_ref, o_ref):
        del place_ref
        acc = p_ref[...].astype(F32)
        for j in range(N_CHIPS - 1):
            acc = acc + q_ref[j].astype(F32)
        o_ref[...] = acc

    return pl.pallas_call(
        body, name=name,
        grid_spec=pltpu.PrefetchScalarGridSpec(
            num_scalar_prefetch=1, grid=(h // rb,),
            in_specs=[pl.BlockSpec((None, rb, cols), lambda r, place_ref: (place_ref[0], r, 0)),
                      pl.BlockSpec((N_CHIPS - 1, rb, cols), lambda r, place_ref: (0, r, 0))],
            out_specs=pl.BlockSpec((None, rb, cols), lambda r, place_ref: (place_ref[1], r, 0))),
        out_shape=jax.ShapeDtypeStruct((2, h, cols), F32),
        compiler_params=_params(),
    )(place, mine, theirs)


def _share_job(bufs):
    n = len(bufs)

    def copies(ins, outs, send, recv, local):
        del ins, local
        x, y, c, _ = _place()

        def copy(w, half):
            return pltpu.make_async_remote_copy(
                src_ref=outs[w].at[half], dst_ref=outs[w].at[half], send_sem=send.at[w], recv_sem=recv.at[w],
                device_id=(x, y, 1 - c), device_id_type=MESH)

        return [copy(w, c) for w in range(n)], [copy(w, 1 - c) for w in range(n)], []

    return _Job(bufs, [jax.ShapeDtypeStruct(b.shape, b.dtype) for b in bufs], n, copies,
                aliases={w: w for w in range(n)})


SMALL_ROWS = 24
ROW_G1, ROW_CW, ROW_CB, ROW_BR, ROW_BI, ROW_LAM, ROW_LG, ROW_LB, ROW_G2, ROW_G3, ROW_LOSS, ROW_BS = (
    0, 1, 5, 6, 7, 8, 9, 10, 11, 12, 13, 16)
N_DEV = 8


def _pack_small(dcw, dcb, dbr, dbi, dlam, dlg, dlb, dg2, dg3, loss, dbs):
    def body(dcw_ref, dcb_ref, dbr_ref, dbi_ref, dlam_ref, dlg_ref, dlb_ref, dg2_ref, dg3_ref, loss_ref, dbs_ref, out):
        out[...] = jnp.zeros((SMALL_ROWS, D_MODEL), F32)
        for row, ref in ((ROW_CB, dcb_ref), (ROW_BR, dbr_ref), (ROW_BI, dbi_ref), (ROW_LAM, dlam_ref),
                         (ROW_LG, dlg_ref), (ROW_LB, dlb_ref), (ROW_G2, dg2_ref), (ROW_G3, dg3_ref)):
            out[row:row + 1, :] = ref[...]
        out[ROW_CW:ROW_CW + CONV_WIDTH, :] = dcw_ref[0:CONV_WIDTH, :]
        out[ROW_LOSS:ROW_LOSS + 1, 0:128] = loss_ref[0:1, :]
        out[ROW_BS:ROW_BS + GROUPS, 0:128] = jnp.transpose(dbs_ref[...])[0:GROUPS, :]

    vm = pl.BlockSpec(memory_space=pltpu.VMEM)
    return pl.pallas_call(
        body, name="pack_small", in_specs=[vm] * 11, out_specs=vm,
        out_shape=jax.ShapeDtypeStruct((SMALL_ROWS, D_MODEL), F32),
    )(dcw, dcb, dbr, dbi, dlam, dlg, dlb, dg2, dg3, loss, dbs)


def _gather_all_job(blocks):
    n = len(blocks)
    flips = [(dx, dy, dc) for dx in (0, 1) for dy in (0, 1) for dc in (0, 1)][1:]

    def copies(ins, outs, send, recv, local):
        x, y, c, _ = _place()
        me = 4 * x + 2 * y + c
        sends, arrivals, own = [], [], []
        for w in range(n):
            own.append(pltpu.make_async_copy(ins[w], outs[w].at[me], local.at[w]))
            for k, (dx, dy, dc) in enumerate(flips):
                peer = (x ^ dx, y ^ dy, c ^ dc)
                sem = dict(send_sem=send.at[7 * w + k], recv_sem=recv.at[7 * w + k])
                sends.append(pltpu.make_async_remote_copy(
                    src_ref=ins[w], dst_ref=outs[w].at[me], device_id=peer, device_id_type=MESH, **sem))
                arrivals.append(pltpu.make_async_remote_copy(
                    src_ref=ins[w], dst_ref=outs[w].at[4 * peer[0] + 2 * peer[1] + peer[2]], device_id=peer,
                    device_id_type=MESH, **sem))
        return sends, arrivals, own

    return _Job(blocks, [jax.ShapeDtypeStruct((N_DEV,) + b.shape, b.dtype) for b in blocks], 7 * n, copies, n_local=n)


def _sum_small(vec_all, ws_all, dg1_all):
    def body(vec_ref, ws_ref, dg1_ref, vec_out, ws_out):
        vec, ws, dg1 = vec_ref[0], ws_ref[0], dg1_ref[0]
        for d in range(1, N_DEV):
            vec, ws, dg1 = vec + vec_ref[d], ws + ws_ref[d], dg1 + dg1_ref[d]
        vec_out[...] = vec
        vec_out[ROW_G1:ROW_G1 + 1, :] = dg1
        ws_out[...] = ws

    vm = pl.BlockSpec(memory_space=pltpu.VMEM)
    return pl.pallas_call(
        body, name="sum_small", in_specs=[vm] * 3, out_specs=[vm, vm],
        out_shape=[jax.ShapeDtypeStruct(vec_all.shape[1:], F32), jax.ShapeDtypeStruct(ws_all.shape[1:], F32)],
    )(vec_all, ws_all, dg1_all)


def _adamw_math(w, g, m, v):
    m = ADAM_B1 * m + (1.0 - ADAM_B1) * g
    v = ADAM_B2 * v + (1.0 - ADAM_B2) * (g * g)
    m_hat = m / (1.0 - ADAM_B1 ** ADAM_STEP)
    v_hat = v / (1.0 - ADAM_B2 ** ADAM_STEP)
    delta = (-ADAM_LR) * (m_hat / (jnp.sqrt(v_hat) + ADAM_EPS) + ADAM_WD * w)
    return delta, m, v


def _adamw(name, g, w, m, v, jobs=()):
    rows, cols = w.shape
    rb = _row_block(rows)

    def body(g_ref, w_ref, m_ref, v_ref, d_ref, nm_ref, nv_ref):
        d_ref[...], nm_ref[...], nv_ref[...] = _adamw_math(w_ref[...], g_ref[...], m_ref[...], v_ref[...])

    blk = pl.BlockSpec((rb, cols), lambda r: (r, 0))
    return _fused_call(
        body, jobs, name=name, grid=(rows // rb,), in_specs=[blk] * 4, out_specs=[blk] * 3,
        out_shape=[jax.ShapeDtypeStruct(w.shape, F32)] * 3, compiler_params=_params(),
    )(g, w, m, v)


def _adamw_small(grads, ws, ms, vs):
    n = len(grads)

    def body(*refs):
        g_refs, w_refs, m_refs, v_refs = refs[:n], refs[n:2 * n], refs[2 * n:3 * n], refs[3 * n:4 * n]
        outs = refs[4 * n:]
        for p in range(n):
            d, nm, nv = _adamw_math(w_refs[p][...], g_refs[p][...], m_refs[p][...], v_refs[p][...])
            outs[p][...] = d
            outs[n + p][...] = nm
            outs[2 * n + p][...] = nv

    vm = pl.BlockSpec(memory_space=pltpu.VMEM)
    shapes = [jax.ShapeDtypeStruct(w.shape, F32) for w in ws]
    out = pl.pallas_call(
        body, name="adamw_small", in_specs=[vm] * (4 * n), out_specs=[vm] * (3 * n), out_shape=shapes * 3,
    )(*grads, *ws, *ms, *vs)
    return out[:n], out[n:2 * n], out[2 * n:]


def _unstack_heads(w_st):
    per = HEAD_DIM // N_CHIPS
    return w_st.reshape(N_CHIPS, HEADS, per, HEAD_DIM).transpose(1, 0, 2, 3).reshape(HEADS, HEAD_DIM, HEAD_DIM)


def _stack_heads(w):
    per = HEAD_DIM // N_CHIPS
    return w.reshape(HEADS, N_CHIPS, per, HEAD_DIM).transpose(1, 0, 2, 3).reshape(N_CHIPS, HEADS * per, HEAD_DIM)


def kernel(x, norm_mix_g, w_in, conv_w, conv_b, w_rgate, b_rgate, w_igate, b_igate, lru_lambda, w_out_a, sgu_ln_g, sgu_ln_b, sgu_w_s, sgu_b_s, w_out_b, w_out, norm_mlp_g, w_up, w_down, norm_final_g, loss_target, m_norm_mix_g, m_w_in, m_conv_w, m_conv_b, m_w_rgate, m_b_rgate, m_w_igate, m_b_igate, m_lru_lambda, m_w_out_a, m_sgu_ln_g, m_sgu_ln_b, m_sgu_w_s, m_sgu_b_s, m_w_out_b, m_w_out, m_norm_mlp_g, m_w_up, m_w_down, m_norm_final_g, v_norm_mix_g, v_w_in, v_conv_w, v_conv_b, v_w_rgate, v_b_rgate, v_w_igate, v_b_igate, v_lru_lambda, v_w_out_a, v_sgu_ln_g, v_sgu_ln_b, v_sgu_w_s, v_sgu_b_s, v_w_out_b, v_w_out, v_norm_mlp_g, v_w_up, v_w_down, v_norm_final_g):
    chip = _chip_index(lax.axis_index("x"), lax.axis_index("y"))
    core = lax.axis_index("c")
    quarter_h = HEAD_DIM // N_CHIPS
    quarter_d = D_MODEL // N_CHIPS

    as_2d = lambda a: a.reshape(-1, a.shape[-1])
    big_w = [as_2d(w) for w in (w_in, w_rgate, w_igate, w_out_a, w_out_b, w_out, w_up, w_down)]
    big_m = [as_2d(w) for w in (m_w_in, m_w_rgate, m_w_igate, m_w_out_a, m_w_out_b, m_w_out, m_w_up, m_w_down)]
    big_v = [as_2d(w) for w in (v_w_in, v_w_rgate, v_w_igate, v_w_out_a, v_w_out_b, v_w_out, v_w_up, v_w_down)]

    packed = jnp.concatenate([conv_w[0], b_rgate[0], b_igate[0]], axis=1)
    packed = jnp.concatenate([packed, jnp.zeros_like(packed)], axis=0)
    s_in, s_r, s_i, s_oa, s_ob, s_out, s_up, s_down = [w.astype(BF16) for w in big_w]
    xs, target = x[0], loss_target[0]
    g3 = norm_final_g.reshape(1, D_MODEL)
    bias_s = jnp.broadcast_to(jnp.transpose(sgu_b_s[0])[:, :, None], (CHUNK, GROUPS, GROUP_DIM)).reshape(CHUNK, D_MODEL)
    core_arr = core.reshape(1).astype(jnp.int32)
    place = jnp.stack([chip, core]).astype(jnp.int32)
    quarter = lambda g: g.reshape(N_CHIPS, D_MODEL // N_CHIPS, D_MODEL)

    def pair_add(nm, g, from_sibling):
        return _pair_add("pair_add_" + nm, core_arr, g.reshape(N_CHIPS, 2, g.shape[1] // 2, g.shape[2]), from_sibling)

    def chip_sum(nm, pair, from_chips):
        return _chip_sum("chip_sum_" + nm, place, pair, from_chips)

    head, (packed_all,) = _comm_call("gather_head", [_gather_ici_job([s_in, s_r, s_i]), _gather_small_job(packed)])
    (w_in_st, wr_st, wi_st), = _comm_call("gather_head_pass", [_gather_pass_job(head)])
    pick = lambda lo, hi: packed_all[:, :HEADS, lo:hi].transpose(1, 0, 2).reshape(HEADS, -1)
    conv_w_full = pick(0, quarter_d)
    br_full = pick(quarter_d, quarter_d + quarter_h).reshape(1, D_MODEL)
    bi_full = pick(quarter_d + quarter_h, quarter_d + 2 * quarter_h).reshape(1, D_MODEL)
    wr, wi = _unstack_heads(wr_st), _unstack_heads(wi_st)
    lru = (conv_w_full, conv_b, wr, br_full, wi, bi_full, lru_lambda)
    sgu = (sgu_ln_g, sgu_ln_b, sgu_w_s[0], bias_s)

    (z, n1), (mid,) = _fwd_in(xs, norm_mix_g, w_in_st, jobs=[_gather_ici_job([s_oa, s_ob, s_out, s_up])])
    (ya, *saved), (mid, (down,)) = _fwd_lru(z, *lru, jobs=[_gather_pass_job(mid), _gather_ici_job([s_down])])
    yb, ((down,),) = _fwd_sgu(z, *sgu, jobs=[_gather_pass_job([down])])
    w_oa, w_ob, w_o = [w.reshape(D_MODEL, D_MODEL) for w in mid[:3]]
    w_up_st, w_dn = mid[3], down.reshape(D_FF, D_MODEL)
    (pa, pb, h1, n2), _ = _fwd_merge(ya, yb, z, xs, w_oa, w_ob, w_o, norm_mlp_g)
    (up, act, dh2, dh2b, loss_part, dg3), _ = _fwd_mlp(n2, h1, target, w_up_st, w_dn, g3)

    (dup, dh1, dg2), _ = _bwd_mlp(dh2, dh2b, up, h1, w_up_st, w_dn, norm_mlp_g)
    d_up, _ = _weight_grad("dw_up", n2, dup, N_CHIPS, False, True, D_MODEL)
    d_down, ((r_up,),) = _weight_grad("dw_down", act, dh2b, N_CHIPS, True, False, D_MODEL,
                                      jobs=[_pair_send_job([d_up])])
    p_up = pair_add("w_up", d_up, r_up)
    (dz, dya, dyb, merged, dpa, dpb, dh1b), ((r_down,), (q_up,)) = _bwd_merge(
        dh1, pa, pb, z, w_oa, w_ob, w_o, jobs=[_pair_send_job([d_down]), _chip_exchange_job([p_up])])
    p_down = pair_add("w_down", d_down, r_down)
    half_up = chip_sum("w_up", p_up, q_up)
    d_out, ((full_up,),) = _weight_grad("dw_out", merged, dh1b, 1, False, False, D_MODEL, jobs=[_share_job([half_up])])
    d_oa, _ = _weight_grad("dw_out_a", ya, dpa, 1, False, False, D_MODEL)
    d_ob, _ = _weight_grad("dw_out_b", yb, dpb, 1, False, False, D_MODEL)
    mids = [quarter(d_oa), quarter(d_ob), quarter(d_out)]
    (dz, dlg, dlb, dws, dbs), ((q_down,), r_mids) = _bwd_sgu(
        dz, dyb, z, *sgu, jobs=[_chip_exchange_job([p_down]), _pair_send_job(mids)])
    mid_names = ("w_out_a", "w_out_b", "w_out")
    p_mids = [pair_add(nm, g, r) for nm, g, r in zip(mid_names, mids, r_mids)]
    half_down = chip_sum("w_down", p_down, q_down)
    (dz, dcw, dcb, dwr, dbr, dwi, dbi, dlam), (q_mids, (full_down,)) = _bwd_lru(
        dz, dya, z, *saved, conv_w_full, wr, wi, lru_lambda, jobs=[_chip_exchange_job(p_mids), _share_job([half_down])])
    half_mids = [chip_sum(nm, p, q) for nm, p, q in zip(mid_names, p_mids, q_mids)]
    gates = [_stack_heads(dwr).astype(BF16), _stack_heads(dwi).astype(BF16)]
    small = _pack_small(dcw, dcb, dbr, dbi, dlam, dlg, dlb, dg2, dg3, loss_part, dbs)
    d_in, (full_mids, r_gates, (vec_all, ws_all)) = _weight_grad(
        "dw_in", n1, dz, N_CHIPS, False, True, IN_SHARD,
        jobs=[_share_job(half_mids), _pair_send_job(gates), _gather_all_job([small, dws])])
    names = ("w_in", "w_rgate", "w_igate", "w_out_a", "w_out_b", "w_out", "w_up", "w_down")
    adam_args = {nm: (w, m, v) for nm, w, m, v in zip(names, big_w, big_m, big_v)}

    def adamw(nm, g, jobs=()):
        w, m, v = adam_args[nm]
        g = g.reshape(w.shape)
        return (g,) + tuple(x for x in _adamw("adamw_" + nm, g, w, m, v, jobs))

    (r_in,), = _comm_call("send_w_in", [_pair_send_job([d_in])])
    last_names = ("w_in", "w_rgate", "w_igate")
    p_last = [pair_add(nm, g, r) for nm, g, r in zip(last_names, [d_in] + gates, [r_in] + r_gates)]
    (grad_x, dg1), (q_last,) = _bwd_in(dz, xs, dh1, w_in_st, norm_mix_g, jobs=[_chip_exchange_job(p_last)])
    half_last = [chip_sum(nm, p, q) for nm, p, q in zip(last_names, p_last, q_last)]
    full_last, (dg1_all,) = _comm_call("share_last", [_share_job(half_last), _gather_all_job([dg1])])
    full, big_out = [], []
    for nm, f in zip(names, full_last + full_mids + [full_up, full_down]):
        g, out, _ = adamw(nm, f)
        full.append(g)
        big_out.append(out)

    vec, ws_sum = _sum_small(vec_all, ws_all, dg1_all)
    row = lambda r: vec[r:r + 1]
    shard = lambda a, width: lax.dynamic_slice_in_dim(a, chip * width, width, axis=1)
    g_small = dict(
        norm_mix_g=row(ROW_G1), conv_w=shard(vec[ROW_CW:ROW_CW + CONV_WIDTH], quarter_d), conv_b=row(ROW_CB),
        b_rgate=shard(row(ROW_BR).reshape(HEADS, HEAD_DIM), quarter_h),
        b_igate=shard(row(ROW_BI).reshape(HEADS, HEAD_DIM), quarter_h), lru_lambda=row(ROW_LAM),
        sgu_ln_g=row(ROW_LG), sgu_ln_b=row(ROW_LB),
        sgu_w_s=ws_sum.reshape(CHUNK, GROUPS, CHUNK).transpose(1, 0, 2).reshape(GROUPS * CHUNK, CHUNK),
        sgu_b_s=vec[ROW_BS:ROW_BS + GROUPS, 0:CHUNK], norm_mlp_g=row(ROW_G2), norm_final_g=row(ROW_G3))
    loss = vec[ROW_LOSS, 0]
    small_names = list(g_small)
    given = dict(
        norm_mix_g=(norm_mix_g, m_norm_mix_g, v_norm_mix_g), conv_w=(conv_w, m_conv_w, v_conv_w),
        conv_b=(conv_b, m_conv_b, v_conv_b), b_rgate=(b_rgate, m_b_rgate, v_b_rgate),
        b_igate=(b_igate, m_b_igate, v_b_igate), lru_lambda=(lru_lambda, m_lru_lambda, v_lru_lambda),
        sgu_ln_g=(sgu_ln_g, m_sgu_ln_g, v_sgu_ln_g), sgu_ln_b=(sgu_ln_b, m_sgu_ln_b, v_sgu_ln_b),
        sgu_w_s=(sgu_w_s, m_sgu_w_s, v_sgu_w_s), sgu_b_s=(sgu_b_s, m_sgu_b_s, v_sgu_b_s),
        norm_mlp_g=(norm_mlp_g, m_norm_mlp_g, v_norm_mlp_g), norm_final_g=(norm_final_g, m_norm_final_g, v_norm_final_g))
    g2d = [g_small[nm] for nm in small_names]
    to2d = lambda a, g: a.reshape(g.shape)
    d_s, m_s, v_s = _adamw_small(
        g2d, *[[to2d(given[nm][q], g) for nm, g in zip(small_names, g2d)] for q in range(3)])

    shapes = dict(
        norm_mix_g=norm_mix_g, w_in=w_in, conv_w=conv_w, conv_b=conv_b, w_rgate=w_rgate, b_rgate=b_rgate,
        w_igate=w_igate, b_igate=b_igate, lru_lambda=lru_lambda, w_out_a=w_out_a, sgu_ln_g=sgu_ln_g,
        sgu_ln_b=sgu_ln_b, sgu_w_s=sgu_w_s, sgu_b_s=sgu_b_s, w_out_b=w_out_b, w_out=w_out, norm_mlp_g=norm_mlp_g,
        w_up=w_up, w_down=w_down, norm_final_g=norm_final_g)
    grads, deltas, new_m, new_v = {}, {}, {}, {}
    for nm, g, (d, nmom, nvar) in zip(names, full, big_out):
        grads[nm], deltas[nm], new_m[nm], new_v[nm] = g, d, nmom, nvar
    for p, nm in enumerate(small_names):
        grads[nm], deltas[nm], new_m[nm], new_v[nm] = g2d[p], d_s[p], m_s[p], v_s[p]
    order = list(shapes)
    out = [loss, grad_x[None]]
    for group in (grads, deltas, new_m, new_v):
        out += [group[nm].reshape(shapes[nm].shape) for nm in order]
    return tuple(out)
```

```python
import functools

import jax
import jax.numpy as jnp
from jax import lax
from jax.experimental import pallas as pl
from jax.experimental.pallas import tpu as pltpu

F32 = jnp.float32
BF16 = jnp.bfloat16
MESH = pl.DeviceIdType.MESH

D_MODEL = 1024
D_IN = 6 * D_MODEL
D_FF = 4 * D_MODEL
N_CHIPS = 4
IN_SHARD = D_IN // N_CHIPS
HEADS = 4
HEAD_DIM = D_MODEL // HEADS
GROUPS = 4
GROUP_DIM = D_MODEL // GROUPS
CHUNK = 128
CONV_WIDTH = 4
LRU_C = 8.0
NORM_EPS = 1e-6
LN_EPS = 1e-5

ADAM_LR = 0.001
ADAM_B1 = 0.9
ADAM_B2 = 0.999
ADAM_EPS = 1e-08
ADAM_WD = 0.01
ADAM_STEP = 10

SUBLANES = 8
MM_TILE = 512
SEQ_TILE = 256
DW_TILE = 2048
VMEM_LIMIT_BYTES = 56 * 1024 * 1024

GELU_K0 = 0.7978845608028654
GELU_K1 = 0.044715


def _params(n_grid_axes=1):
    return pltpu.CompilerParams(
        dimension_semantics=("arbitrary",) * n_grid_axes, vmem_limit_bytes=VMEM_LIMIT_BYTES)


def _resident(shape):
    nd = len(shape)
    return pl.BlockSpec(shape, lambda *_: (0,) * nd, pipeline_mode=pl.Buffered(1))


def _const(shape):
    nd = len(shape)
    return pl.BlockSpec(shape, lambda *_: (0,) * nd)


def _dot(a, b):
    return jnp.dot(a, b, preferred_element_type=F32)


def _dot_nt(a, b):
    return lax.dot_general(a, b, (((1,), (1,)), ((), ())), preferred_element_type=F32)


def _dot_tn(a, b):
    return lax.dot_general(a, b, (((0,), (0,)), ((), ())), preferred_element_type=F32)


def _gelu(x):
    t = jnp.tanh(GELU_K0 * x * (1.0 + GELU_K1 * x * x))
    return 0.5 * x * (1.0 + t)


def _gelu_and_grad(x):
    x2 = x * x
    t = jnp.tanh(GELU_K0 * x * (1.0 + GELU_K1 * x2))
    g = 0.5 * x * (1.0 + t)
    dg = 0.5 * (1.0 + t) + 0.5 * x * (1.0 - t * t) * (GELU_K0 * (1.0 + 3.0 * GELU_K1 * x2))
    return g, dg


def _rms(x):
    r = lax.rsqrt(jnp.mean(x * x, axis=-1, keepdims=True) + NORM_EPS)
    return x * r, r


def _rms_bwd(dn, xhat, r):
    return r * (dn - xhat * jnp.mean(dn * xhat, axis=-1, keepdims=True))


def _col_sum(v):
    return jnp.sum(v, axis=0, keepdims=True)


def _shift_down(x, tail8, k):
    xs = pltpu.roll(x, k, 0)
    ts = pltpu.roll(tail8, k, 0)
    ridx = lax.broadcasted_iota(jnp.int32, tail8.shape, 0)
    head = jnp.where(ridx < k, ts, xs[0:SUBLANES])
    return jnp.concatenate([head, xs[SUBLANES:]], axis=0)


def _shift_up(x, head8, k):
    n = x.shape[0]
    xs = pltpu.roll(x, n - k, 0)
    hs = pltpu.roll(head8, SUBLANES - k, 0)
    ridx = lax.broadcasted_iota(jnp.int32, head8.shape, 0)
    last = jnp.where(ridx >= SUBLANES - k, hs, xs[n - SUBLANES:n])
    return jnp.concatenate([xs[:n - SUBLANES], last], axis=0)


def _scan_forward(a, b, carry):
    n, cols = a.shape
    groups = n // SUBLANES
    a = a.reshape(groups, SUBLANES, cols)
    b = b.reshape(groups, SUBLANES, cols)
    sub = lax.broadcasted_iota(jnp.int32, a.shape, 1)
    for s in (1, 2, 4):
        a_s = pltpu.roll(a, s, 1)
        b_s = pltpu.roll(b, s, 1)
        m = sub >= s
        b = jnp.where(m, a * b_s + b, b)
        a = jnp.where(m, a * a_s, a)
    out = []
    for g in range(groups):
        h = a[g] * carry + b[g]
        out.append(h)
        carry = h[SUBLANES - 1:SUBLANES]
    return jnp.concatenate(out, axis=0), carry


def _scan_backward(a, b, carry):
    n, cols = a.shape
    groups = n // SUBLANES
    a = a.reshape(groups, SUBLANES, cols)
    b = b.reshape(groups, SUBLANES, cols)
    sub = lax.broadcasted_iota(jnp.int32, a.shape, 1)
    for s in (1, 2, 4):
        a_s = pltpu.roll(a, SUBLANES - s, 1)
        b_s = pltpu.roll(b, SUBLANES - s, 1)
        m = sub < SUBLANES - s
        b = jnp.where(m, a * b_s + b, b)
        a = jnp.where(m, a * a_s, a)
    out = [None] * groups
    for g in reversed(range(groups)):
        h = a[g] * carry + b[g]
        out[g] = h
        carry = h[0:1]
    return jnp.concatenate(out, axis=0), carry


def _softplus_neg(lam):
    e = jnp.exp(-jnp.abs(lam))
    u = 1.0 + e
    log1p_e = jnp.where(u == 1.0, e, jnp.log(u) * (e / jnp.where(u == 1.0, 1.0, u - 1.0)))
    return jnp.maximum(-lam, 0.0) + log1p_e


def _lru_gates(xa, tail8, cw_ref, cb_ref, wr_ref, br_ref, wi_ref, bi_ref, lam_ref):
    cw = cw_ref[...]
    xc = cb_ref[...] + cw[0:1] * xa
    for k in range(1, CONV_WIDTH):
        xc = xc + cw[k:k + 1] * _shift_down(xa, tail8, k)
    xcb = xc.astype(BF16)
    pre_r, pre_i = [], []
    for h in range(HEADS):
        cols = slice(h * HEAD_DIM, (h + 1) * HEAD_DIM)
        pre_r.append(_dot(xcb[:, cols], wr_ref[h]))
        pre_i.append(_dot(xcb[:, cols], wi_ref[h]))
    r = jax.nn.sigmoid(jnp.concatenate(pre_r, axis=1) + br_ref[...])
    ig = jax.nn.sigmoid(jnp.concatenate(pre_i, axis=1) + bi_ref[...])
    _, a, mult = _decay(r, lam_ref)
    return xc, r, ig, a, mult


def _decay(r, lam_ref):
    sp = _softplus_neg(lam_ref[...])
    log_a = ((-LRU_C) * sp) * r
    a = jnp.exp(log_a)
    th = jnp.tanh(log_a)
    return sp, a, jnp.sqrt((-2.0 * th) / (1.0 - th))


class _Job:
    def __init__(self, inputs, out_shape, n_sem, copies, aliases=None, n_local=0):
        self.inputs, self.out_shape, self.n_sem, self.copies = list(inputs), list(out_shape), n_sem, copies
        self.aliases, self.n_local = dict(aliases or {}), n_local


def _fused_call(body, jobs, *, name, grid, in_specs, out_specs, out_shape, scratch_shapes=(),
                input_output_aliases=None, compiler_params=None, n_prefetch=0, start_jobs_after_body=False):
    single = not isinstance(out_shape, (list, tuple))
    out_specs = [out_specs] if single else list(out_specs)
    out_shape = [out_shape] if single else list(out_shape)
    n_scr = len(scratch_shapes)
    in_specs, scratch_shapes = list(in_specs), list(scratch_shapes)
    n_in, n_out = len(in_specs), len(out_shape)
    aliases = dict(input_output_aliases or {})
    in_at, out_at = [], []
    for job in jobs:
        in_at.append(len(in_specs))
        out_at.append(len(out_shape))
        for i, o in job.aliases.items():
            aliases[n_prefetch + len(in_specs) + i] = len(out_shape) + o
        in_specs += [ANY] * len(job.inputs)
        out_specs += [ANY] * len(job.out_shape)
        out_shape += job.out_shape
        scratch_shapes += [pltpu.SemaphoreType.DMA((job.n_sem,)), pltpu.SemaphoreType.DMA((job.n_sem,)),
                           pltpu.SemaphoreType.DMA((max(job.n_local, 1),))]
    n_in_all, n_out_all = len(in_specs), len(out_shape)

    def full_body(*refs):
        prefetch, refs = refs[:n_prefetch], refs[n_prefetch:]
        ins, outs, scr = refs[:n_in_all], refs[n_in_all:n_in_all + n_out_all], refs[n_in_all + n_out_all:]

        def copies(q):
            job = jobs[q]
            return job.copies(ins[in_at[q]:in_at[q] + len(job.inputs)], outs[out_at[q]:out_at[q] + len(job.out_shape)],
                              *scr[n_scr + 3 * q:n_scr + 3 * q + 3])

        def start():
            for q in range(len(jobs)):
                sends, _, local = copies(q)
                for cp in local + sends:
                    cp.start()

        def finish():
            every = [copies(q) for q in range(len(jobs))]
            for _, arrivals, _ in every:
                for cp in arrivals:
                    cp.wait_recv()
            for sends, _, local in every:
                for cp in sends:
                    cp.wait_send()
                for cp in local:
                    cp.wait()

        if not grid:
            start()
            finish()
            return
        ids = [pl.program_id(a) for a in range(len(grid))]
        first = functools.reduce(jnp.logical_and, [i == 0 for i in ids])
        if jobs and not start_jobs_after_body:
            pl.when(first)(start)
        body(*prefetch, *ins[:n_in], *outs[:n_out], *scr[:n_scr])
        if jobs and start_jobs_after_body:
            pl.when(first)(start)
        if jobs:
            pl.when(functools.reduce(jnp.logical_and, [i == g - 1 for i, g in zip(ids, grid)]))(finish)

    if n_prefetch:
        layout = dict(grid_spec=pltpu.PrefetchScalarGridSpec(
            num_scalar_prefetch=n_prefetch, grid=grid, in_specs=in_specs, out_specs=out_specs,
            scratch_shapes=scratch_shapes))
    else:
        layout = dict(grid=grid, in_specs=in_specs, out_specs=out_specs, scratch_shapes=scratch_shapes)
    call = pl.pallas_call(
        full_body, name=name, out_shape=out_shape, input_output_aliases=aliases, compiler_params=compiler_params,
        **layout)

    def run(*args):
        res = call(*args, *[a for job in jobs for a in job.inputs])
        mine = res[0] if single else list(res[:n_out])
        return mine, [list(res[at:at + len(job.out_shape)]) for at, job in zip(out_at, jobs)]

    return run


def _fwd_in(x, g1, shards, order, jobs=()):
    t = x.shape[0]
    n_tiles = t // MM_TILE
    n = len(shards)
    halves = [s.shape[0] // 2 for s in shards]

    def body(order_ref, x_ref, g_ref, *refs):
        del order_ref
        ins, (z_ref, n_ref), outs = refs[:n], refs[n:n + 2], refs[n + 2:2 * n + 2]
        wbuf, send, recv, local = refs[2 * n + 2:]
        s, i = pl.program_id(0), pl.program_id(1)
        x_, y_, c, chips = _place()
        k_me = _chip_index(x_, y_)

        def block(w, chip, pc):
            return outs[w].at[_chip_index(*chip), pl.ds(pc * halves[w], halves[w]), :]

        def over_ici(w, j, landing):
            return pltpu.make_async_remote_copy(
                src_ref=ins[w].at[pl.ds(c * halves[w], halves[w]), :],
                dst_ref=block(w, chips[j] if landing else (x_, y_), c), send_sem=send.at[6 * w + j],
                recv_sem=recv.at[6 * w + j], device_id=(*chips[j], c), device_id_type=MESH)

        def to_sibling(w, j, landing):
            blk = block(w, chips[j], 1 - c if landing else c)
            return pltpu.make_async_remote_copy(
                src_ref=blk, dst_ref=blk, send_sem=send.at[6 * w + 3 + j], recv_sem=recv.at[6 * w + 3 + j],
                device_id=(x_, y_, 1 - c), device_id_type=MESH)

        own = [pltpu.make_async_copy(wbuf, outs[0].at[k_me], local.at[0])]
        own += [pltpu.make_async_copy(ins[w], outs[w].at[k_me], local.at[w]) for w in range(1, n)]

        @pl.when((s == 0) & (i == 0))
        def _():
            for j in range(N_CHIPS - 1):
                for w in range(n):
                    over_ici(w, j, False).start()
            load = pltpu.make_async_copy(ins[0], wbuf, local.at[n])
            load.start()
            load.wait()
            for cp in own:
                cp.start()

        for j in range(N_CHIPS - 1):
            @pl.when((s == j + 1) & (i == 0))
            def _(j=j):
                for w in range(n):
                    over_ici(w, j, True).wait_recv()
                for w in range(n):
                    to_sibling(w, j, False).start()
                if j == 0:
                    own[0].wait()
                for w in range(n):
                    to_sibling(w, j, True).wait_recv()
                load = pltpu.make_async_copy(outs[0].at[_chip_index(*chips[j])], wbuf, local.at[n])
                load.start()
                load.wait()

        xhat, _ = _rms(x_ref[...])
        nrm = (xhat * g_ref[...]).astype(BF16)
        z_ref[...] = _dot(nrm, wbuf[...])

        @pl.when(s == 0)
        def _():
            n_ref[...] = nrm

        @pl.when((s == N_CHIPS - 1) & (i == n_tiles - 1))
        def _():
            for j in range(N_CHIPS - 1):
                for w in range(n):
                    over_ici(w, j, False).wait_send()
                    to_sibling(w, j, False).wait_send()
            for cp in own[1:]:
                cp.wait()

    (z, n1, *stacked), job_outs = _fused_call(
        body, jobs, name="fwd_in", grid=(N_CHIPS, n_tiles), n_prefetch=1,
        in_specs=[pl.BlockSpec((MM_TILE, D_MODEL), lambda s, i, order: (i, 0)), _const((1, D_MODEL))] + [ANY] * n,
        out_specs=[pl.BlockSpec((MM_TILE, IN_SHARD), lambda s, i, order: (i, order[s])),
                   pl.BlockSpec((MM_TILE, D_MODEL), lambda s, i, order: (jnp.where(s == 0, i, n_tiles - 1), 0))]
        + [ANY] * n,
        out_shape=[jax.ShapeDtypeStruct((t, D_IN), F32), jax.ShapeDtypeStruct((t, D_MODEL), BF16)]
        + [jax.ShapeDtypeStruct((N_CHIPS,) + s.shape, s.dtype) for s in shards],
        scratch_shapes=[pltpu.VMEM(shards[0].shape, BF16), pltpu.SemaphoreType.DMA((6 * n,)),
                        pltpu.SemaphoreType.DMA((6 * n,)), pltpu.SemaphoreType.DMA((n + 1,))],
        compiler_params=_params(2), start_jobs_after_body=True,
    )(order, x, g1, *shards)
    return (z, n1, stacked), job_outs


def _fwd_lru(z, conv_w, conv_b, wr, br, wi, bi, lam, jobs=()):
    t = z.shape[0]

    def body(xa_ref, ga_ref, cw_ref, cb_ref, wr_ref, br_ref, wi_ref, bi_ref, lam_ref, ya_ref, h_ref, xc_ref, r_ref,
             ig_ref, tail_ref, carry_ref):
        @pl.when(pl.program_id(0) == 0)
        def _():
            tail_ref[...] = jnp.zeros_like(tail_ref)
            carry_ref[...] = jnp.zeros_like(carry_ref)

        xa = xa_ref[...]
        xc, r, ig, a, mult = _lru_gates(xa, tail_ref[...], cw_ref, cb_ref, wr_ref, br_ref, wi_ref, bi_ref, lam_ref)
        tail_ref[...] = xa[SEQ_TILE - SUBLANES:]
        xc_ref[...], r_ref[...], ig_ref[...] = xc, r, ig
        h, carry = _scan_forward(a, xc * ig * mult, carry_ref[...])
        carry_ref[...] = carry
        h_ref[...] = h
        ya_ref[...] = (h * _gelu(ga_ref[...])).astype(BF16)

    tile = lambda j: pl.BlockSpec((SEQ_TILE, D_MODEL), lambda i: (i, j))
    return _fused_call(
        body, jobs, name="fwd_lru", grid=(t // SEQ_TILE,),
        in_specs=[tile(0), tile(1), _const((CONV_WIDTH, D_MODEL)), _const((1, D_MODEL)),
                  _resident((HEADS, HEAD_DIM, HEAD_DIM)), _const((1, D_MODEL)),
                  _resident((HEADS, HEAD_DIM, HEAD_DIM)), _const((1, D_MODEL)), _const((1, D_MODEL))],
        out_specs=[tile(0)] * 5,
        out_shape=[jax.ShapeDtypeStruct((t, D_MODEL), BF16)] + [jax.ShapeDtypeStruct((t, D_MODEL), F32)] * 4,
        scratch_shapes=[pltpu.VMEM((SUBLANES, D_MODEL), F32), pltpu.VMEM((1, D_MODEL), F32)],
        compiler_params=_params(),
    )(z, z, conv_w, conv_b, wr, br, wi, bi, lam)


def _sgu_forward_parts(ub, vb, lg_ref, lb_ref):
    u, du = _gelu_and_grad(ub)
    vg, dvg = _gelu_and_grad(vb)
    mu = jnp.mean(vg, axis=-1, keepdims=True)
    d = vg - mu
    rstd = lax.rsqrt(jnp.mean(d * d, axis=-1, keepdims=True) + LN_EPS)
    vhat = d * rstd
    vn = (vhat * lg_ref[...] + lb_ref[...]).astype(BF16)
    return u, du, dvg, rstd, vhat, vn


def _causal_mask():
    rows = lax.broadcasted_iota(jnp.int32, (CHUNK, CHUNK), 0)
    cols = lax.broadcasted_iota(jnp.int32, (CHUNK, CHUNK), 1)
    return rows >= cols


def _fwd_sgu(z, ln_g, ln_b, w_s, bias_full, jobs=()):
    t = z.shape[0]

    def body(ub_ref, vb_ref, lg_ref, lb_ref, ws_ref, bias_ref, yb_ref):
        u, _, _, _, _, vn = _sgu_forward_parts(ub_ref[...], vb_ref[...], lg_ref, lb_ref)
        mask = _causal_mask()
        wm = [jnp.where(mask, ws_ref[g], 0.0).astype(BF16) for g in range(GROUPS)]
        for c in range(SEQ_TILE // CHUNK):
            rows = slice(c * CHUNK, (c + 1) * CHUNK)
            for g in range(GROUPS):
                cols = slice(g * GROUP_DIM, (g + 1) * GROUP_DIM)
                sp = _dot(wm[g], vn[rows, cols]) + bias_ref[:, cols]
                yb_ref[rows, cols] = (u[rows, cols] * sp).astype(BF16)

    tile = lambda j: pl.BlockSpec((SEQ_TILE, D_MODEL), lambda i: (i, j))
    return _fused_call(
        body, jobs, name="fwd_sgu", grid=(t // SEQ_TILE,),
        in_specs=[tile(2), tile(3), _const((1, D_MODEL)), _const((1, D_MODEL)),
                  _const((GROUPS, CHUNK, CHUNK)), _const((CHUNK, D_MODEL))],
        out_specs=tile(0),
        out_shape=jax.ShapeDtypeStruct((t, D_MODEL), BF16),
        compiler_params=_params(),
    )(z, z, ln_g, ln_b, w_s, bias_full)


def _fwd_merge(ya, yb, z, x, w_oa, w_ob, w_out, g2, jobs=()):
    t = x.shape[0]

    def body(ya_ref, yb_ref, m_ref, x_ref, woa_ref, wob_ref, wout_ref, g_ref, pa_ref, pb_ref, h1_ref, n2_ref):
        pa = _dot(ya_ref[...], woa_ref[...])
        pb = _dot(yb_ref[...], wob_ref[...])
        pa_ref[...] = pa
        pb_ref[...] = pb
        merged = jax.nn.sigmoid(m_ref[:, :D_MODEL]) * pa + jax.nn.sigmoid(m_ref[:, D_MODEL:]) * pb
        h1 = x_ref[...] + _dot(merged.astype(BF16), wout_ref[...])
        h1_ref[...] = h1
        xhat, _ = _rms(h1)
        n2_ref[...] = (xhat * g_ref[...]).astype(BF16)

    tile = pl.BlockSpec((MM_TILE, D_MODEL), lambda i: (i, 0))
    sq = _resident((D_MODEL, D_MODEL))
    return _fused_call(
        body, jobs, name="fwd_merge", grid=(t // MM_TILE,),
        in_specs=[tile, tile, pl.BlockSpec((MM_TILE, 2 * D_MODEL), lambda i: (i, 2)), tile, sq, sq, sq,
                  _const((1, D_MODEL))],
        out_specs=[tile, tile, tile, tile],
        out_shape=[jax.ShapeDtypeStruct((t, D_MODEL), F32)] * 3 + [jax.ShapeDtypeStruct((t, D_MODEL), BF16)],
        compiler_params=_params(),
    )(ya, yb, z, x, w_oa, w_ob, w_out, g2)


def _fwd_mlp(n2, h1, target, w_up_st, w_down, g3, jobs=()):
    t = n2.shape[0]

    def body(n2_ref, h1_ref, tgt_ref, wup_ref, wdown_ref, g_ref, up_ref, act_ref, dh2_ref, dh2b_ref, loss_ref,
             dg3_ref):
        @pl.when(pl.program_id(0) == 0)
        def _():
            loss_ref[...] = jnp.zeros_like(loss_ref)
            dg3_ref[...] = jnp.zeros_like(dg3_ref)

        n2 = n2_ref[...]
        h2 = h1_ref[...]
        for k in range(N_CHIPS):
            cols = slice(k * D_MODEL, (k + 1) * D_MODEL)
            up = _dot(n2, wup_ref[k])
            up_ref[:, cols] = up.astype(BF16)
            r = jnp.maximum(up, 0.0)
            act = (r * r).astype(BF16)
            act_ref[:, cols] = act
            h2 = h2 + _dot(act, wdown_ref[cols, :])
        xhat, r3 = _rms(h2)
        diff = xhat * g_ref[...] - tgt_ref[...]
        sq = jnp.sum(diff * diff, axis=1, keepdims=True)
        loss_ref[...] = loss_ref[...] + (0.5 / D_MODEL) * jnp.sum(sq, axis=0, keepdims=True)
        dy = diff * (1.0 / D_MODEL)
        dg3_ref[...] = dg3_ref[...] + _col_sum(dy * xhat)
        dh2 = _rms_bwd(dy * g_ref[...], xhat, r3)
        dh2_ref[...] = dh2
        dh2b_ref[...] = dh2.astype(BF16)

    tile = pl.BlockSpec((MM_TILE, D_MODEL), lambda i: (i, 0))
    wide = pl.BlockSpec((MM_TILE, D_FF), lambda i: (i, 0))
    return _fused_call(
        body, jobs, name="fwd_mlp", grid=(t // MM_TILE,),
        in_specs=[tile, tile, tile, _resident((N_CHIPS, D_MODEL, D_MODEL)), _resident((D_FF, D_MODEL)),
                  _const((1, D_MODEL))],
        out_specs=[wide, wide, tile, tile, _const((SUBLANES, 128)), _const((1, D_MODEL))],
        out_shape=[jax.ShapeDtypeStruct((t, D_FF), BF16), jax.ShapeDtypeStruct((t, D_FF), BF16),
                   jax.ShapeDtypeStruct((t, D_MODEL), F32), jax.ShapeDtypeStruct((t, D_MODEL), BF16),
                   jax.ShapeDtypeStruct((SUBLANES, 128), F32), jax.ShapeDtypeStruct((1, D_MODEL), F32)],
        compiler_params=_params(),
    )(n2, h1, target, w_up_st, w_down, g3)


def _bwd_mlp(dh2, dh2b, up, h1, w_up_st, w_down, g2, jobs=()):
    t = dh2.shape[0]

    def body(dh2_ref, dh2b_ref, up_ref, h1_ref, wup_ref, wdown_ref, g_ref, dup_ref, dh1_ref, dg2_ref):
        @pl.when(pl.program_id(0) == 0)
        def _():
            dg2_ref[...] = jnp.zeros_like(dg2_ref)

        dh2b = dh2b_ref[...]
        dn2 = jnp.zeros((MM_TILE, D_MODEL), F32)
        for k in range(N_CHIPS):
            cols = slice(k * D_MODEL, (k + 1) * D_MODEL)
            dact = _dot_nt(dh2b, wdown_ref[cols, :])
            dup = (dact * (2.0 * jnp.maximum(up_ref[:, cols].astype(F32), 0.0))).astype(BF16)
            dup_ref[:, cols] = dup
            dn2 = dn2 + _dot_nt(dup, wup_ref[k])
        xhat, r2 = _rms(h1_ref[...])
        dg2_ref[...] = dg2_ref[...] + _col_sum(dn2 * xhat)
        dh1_ref[...] = dh2_ref[...] + _rms_bwd(dn2 * g_ref[...], xhat, r2)

    tile = pl.BlockSpec((MM_TILE, D_MODEL), lambda i: (i, 0))
    wide = pl.BlockSpec((MM_TILE, D_FF), lambda i: (i, 0))
    return _fused_call(
        body, jobs, name="bwd_mlp", grid=(t // MM_TILE,),
        in_specs=[tile, tile, wide, tile, _resident((N_CHIPS, D_MODEL, D_MODEL)), _resident((D_FF, D_MODEL)),
                  _const((1, D_MODEL))],
        out_specs=[wide, tile, _const((1, D_MODEL))],
        out_shape=[jax.ShapeDtypeStruct((t, D_FF), BF16), jax.ShapeDtypeStruct((t, D_MODEL), F32),
                   jax.ShapeDtypeStruct((1, D_MODEL), F32)],
        compiler_params=_params(),
    )(dh2, dh2b, up, h1, w_up_st, w_down, g2)


def _bwd_merge(dh1, pa, pb, z, w_oa, w_ob, w_out, jobs=()):
    t = dh1.shape[0]

    def body(dh1_ref, pa_ref, pb_ref, m_ref, woa_ref, wob_ref, wout_ref, dz_ref, dya_ref, dyb_ref, mg_ref,
             dpa_ref, dpb_ref, dh1b_ref):
        dh1b = dh1_ref[...].astype(BF16)
        dh1b_ref[...] = dh1b
        dm = _dot_nt(dh1b, wout_ref[...])
        pa = pa_ref[...]
        pb = pb_ref[...]
        sa = jax.nn.sigmoid(m_ref[:, :D_MODEL])
        sb = jax.nn.sigmoid(m_ref[:, D_MODEL:])
        mg_ref[...] = (sa * pa + sb * pb).astype(BF16)
        dz_ref[:, :D_MODEL] = (dm * pa * sa * (1.0 - sa)).astype(BF16)
        dz_ref[:, D_MODEL:] = (dm * pb * sb * (1.0 - sb)).astype(BF16)
        dpa = (dm * sa).astype(BF16)
        dpb = (dm * sb).astype(BF16)
        dpa_ref[...] = dpa
        dpb_ref[...] = dpb
        dya_ref[...] = _dot_nt(dpa, woa_ref[...])
        dyb_ref[...] = _dot_nt(dpb, wob_ref[...])

    tile = pl.BlockSpec((MM_TILE, D_MODEL), lambda i: (i, 0))
    pair = pl.BlockSpec((MM_TILE, 2 * D_MODEL), lambda i: (i, 2))
    sq = _resident((D_MODEL, D_MODEL))
    act_bf = jax.ShapeDtypeStruct((t, D_MODEL), BF16)
    return _fused_call(
        body, jobs, name="bwd_merge", grid=(t // MM_TILE,),
        in_specs=[tile, tile, tile, pair, sq, sq, sq],
        out_specs=[pair, tile, tile, tile, tile, tile, tile],
        out_shape=[jax.ShapeDtypeStruct((t, D_IN), BF16), jax.ShapeDtypeStruct((t, D_MODEL), F32),
                   jax.ShapeDtypeStruct((t, D_MODEL), F32), act_bf, act_bf, act_bf, act_bf],
        compiler_params=_params(),
    )(dh1, pa, pb, z, w_oa, w_ob, w_out)


def _bwd_sgu(dz, dyb, z, ln_g, ln_b, w_s, bias_full, jobs=()):
    t = dyb.shape[0]
    n_tiles = t // SEQ_TILE

    def body(dz_any, dyb_ref, ub_ref, vb_ref, lg_ref, lb_ref, ws_ref, bias_ref, dz_ref, dlg_ref, dlb_ref, dws_ref,
             dbs_ref, dvn_ref, dsp_acc):
        del dz_any
        i = pl.program_id(0)

        @pl.when(i == 0)
        def _():
            dlg_ref[...] = jnp.zeros_like(dlg_ref)
            dlb_ref[...] = jnp.zeros_like(dlb_ref)
            dws_ref[...] = jnp.zeros_like(dws_ref)
            dsp_acc[...] = jnp.zeros_like(dsp_acc)

        u, du, dvg, rstd, vhat, vn = _sgu_forward_parts(ub_ref[...], vb_ref[...], lg_ref, lb_ref)
        dyb = dyb_ref[...]
        mask = _causal_mask()
        wm = [jnp.where(mask, ws_ref[g], 0.0).astype(BF16) for g in range(GROUPS)]
        for c in range(SEQ_TILE // CHUNK):
            rows = slice(c * CHUNK, (c + 1) * CHUNK)
            for g in range(GROUPS):
                cols = slice(g * GROUP_DIM, (g + 1) * GROUP_DIM)
                vn_blk = vn[rows, cols]
                sp = _dot(wm[g], vn_blk) + bias_ref[:, cols]
                dyb_blk = dyb[rows, cols]
                dz_ref[rows, cols] = (dyb_blk * sp * du[rows, cols]).astype(BF16)
                dsp = dyb_blk * u[rows, cols]
                dsp_acc[:, cols] = dsp_acc[:, cols] + dsp
                dspb = dsp.astype(BF16)
                dvn_ref[rows, cols] = _dot_tn(wm[g], dspb)
                wcols = slice(g * CHUNK, (g + 1) * CHUNK)
                dws_ref[:, wcols] = dws_ref[:, wcols] + jnp.where(mask, _dot_nt(dspb, vn_blk), 0.0)
        dvn = dvn_ref[...]
        dlg_ref[...] = dlg_ref[...] + _col_sum(dvn * vhat)
        dlb_ref[...] = dlb_ref[...] + _col_sum(dvn)
        dvhat = dvn * lg_ref[...]
        dvgel = rstd * (dvhat - jnp.mean(dvhat, axis=-1, keepdims=True)
                        - vhat * jnp.mean(dvhat * vhat, axis=-1, keepdims=True))
        dz_ref[:, D_MODEL:] = (dvgel * dvg).astype(BF16)

        @pl.when(i == n_tiles - 1)
        def _():
            lane = lax.broadcasted_iota(jnp.int32, (CHUNK, 128), 1)
            out = jnp.zeros((CHUNK, 128), F32)
            for g in range(GROUPS):
                s = jnp.sum(dsp_acc[:, g * GROUP_DIM:(g + 1) * GROUP_DIM], axis=1, keepdims=True)
                out = out + jnp.where(lane == g, s, 0.0)
            dbs_ref[...] = out

    tile = lambda j: pl.BlockSpec((SEQ_TILE, D_MODEL), lambda i: (i, j))
    return _fused_call(
        body, jobs, name="bwd_sgu", grid=(n_tiles,),
        in_specs=[pl.BlockSpec(memory_space=pl.ANY), tile(0), tile(2), tile(3), _const((1, D_MODEL)),
                  _const((1, D_MODEL)), _const((GROUPS, CHUNK, CHUNK)), _const((CHUNK, D_MODEL))],
        out_specs=[pl.BlockSpec((SEQ_TILE, 2 * D_MODEL), lambda i: (i, 1)), _const((1, D_MODEL)),
                   _const((1, D_MODEL)), _const((CHUNK, GROUPS * CHUNK)), _const((CHUNK, 128))],
        out_shape=[jax.ShapeDtypeStruct((t, D_IN), BF16), jax.ShapeDtypeStruct((1, D_MODEL), F32),
                   jax.ShapeDtypeStruct((1, D_MODEL), F32), jax.ShapeDtypeStruct((CHUNK, GROUPS * CHUNK), F32),
                   jax.ShapeDtypeStruct((CHUNK, 128), F32)],
        scratch_shapes=[pltpu.VMEM((SEQ_TILE, D_MODEL), F32), pltpu.VMEM((CHUNK, D_MODEL), F32)],
        input_output_aliases={0: 0},
        compiler_params=_params(),
    )(dz, dyb, z, z, ln_g, ln_b, w_s, bias_full)


def _bwd_lru(dz, dya, z, h, xc, r, ig, conv_w, wr, wi, lam, jobs=()):
    t = dya.shape[0]
    n_tiles = t // SEQ_TILE
    per_tile = SEQ_TILE // SUBLANES

    def body(dz_any, dya_ref, xa_ref, ga_ref, h_ref, h_prev_ref, xc_ref, r_ref, ig_ref, cw_ref, wr_ref, wi_ref, lam_ref,
             dz_ref, dcw_ref, dcb_ref, dwr_ref, dbr_ref, dwi_ref, dbi_ref, dlam_ref, lam_carry, dxc_head):
        del dz_any
        i = pl.program_id(0)

        @pl.when(i == 0)
        def _():
            for ref in (dcw_ref, dcb_ref, dwr_ref, dbr_ref, dwi_ref, dbi_ref, dlam_ref, lam_carry, dxc_head):
                ref[...] = jnp.zeros_like(ref)

        first_tile = i == n_tiles - 1
        h_tail = jnp.where(first_tile, 0.0, h_prev_ref[...])
        xc, r, ig = xc_ref[...], r_ref[...], ig_ref[...]
        xcb = xc.astype(BF16)
        sp, a, mult = _decay(r, lam_ref)
        h = h_ref[...]
        h_prev = _shift_down(h, h_tail, 1)
        dya = dya_ref[...]
        gg, dgg = _gelu_and_grad(ga_ref[...])
        dz_ref[:, D_MODEL:] = (dya * h * dgg).astype(BF16)
        ones = jnp.ones((SUBLANES, D_MODEL), F32)
        lam_t, lam_first = _scan_backward(_shift_up(a, ones, 1), dya * gg, lam_carry[...])
        lam_carry[...] = a[0:1] * lam_first
        dmult = lam_t * xc * ig
        dla = lam_t * h_prev * a - dmult * (a * a) / mult
        dr = dla * ((-LRU_C) * sp)
        dlam_ref[...] = dlam_ref[...] + _col_sum(dla * r) * (LRU_C * jax.nn.sigmoid(-lam_ref[...]))
        dpr = dr * r * (1.0 - r)
        dpi = lam_t * xc * mult * ig * (1.0 - ig)
        dbr_ref[...] = dbr_ref[...] + _col_sum(dpr)
        dbi_ref[...] = dbi_ref[...] + _col_sum(dpi)
        dprb = dpr.astype(BF16)
        dpib = dpi.astype(BF16)
        dxc_gate = []
        for hd in range(HEADS):
            cols = slice(hd * HEAD_DIM, (hd + 1) * HEAD_DIM)
            dxc_gate.append(_dot_nt(dprb[:, cols], wr_ref[hd]) + _dot_nt(dpib[:, cols], wi_ref[hd]))
            dwr_ref[hd] = dwr_ref[hd] + _dot_tn(xcb[:, cols], dprb[:, cols])
            dwi_ref[hd] = dwi_ref[hd] + _dot_tn(xcb[:, cols], dpib[:, cols])
        dxc = lam_t * ig * mult + jnp.concatenate(dxc_gate, axis=1)
        dcb_ref[...] = dcb_ref[...] + _col_sum(dxc)
        cw = cw_ref[...]
        head = dxc_head[...]
        xa = xa_ref[...]
        dxa = cw[0:1] * dxc
        dcw_ref[0:1, :] = dcw_ref[0:1, :] + _col_sum(dxc * xa)
        for k in range(1, CONV_WIDTH):
            dxc_k = _shift_up(dxc, head, k)
            dxa = dxa + cw[k:k + 1] * dxc_k
            dcw_ref[k:k + 1, :] = dcw_ref[k:k + 1, :] + _col_sum(dxc_k * xa)
        dxc_head[...] = dxc[0:SUBLANES]
        dz_ref[:, :D_MODEL] = dxa.astype(BF16)

    rev = lambda i: n_tiles - 1 - i
    tile = lambda j: pl.BlockSpec((SEQ_TILE, D_MODEL), lambda i: (rev(i), j))
    prev8 = pl.BlockSpec((SUBLANES, D_MODEL), lambda i: (jnp.maximum(rev(i) * per_tile - 1, 0), 0))
    vec = _const((1, D_MODEL))
    gate_w = _resident((HEADS, HEAD_DIM, HEAD_DIM))
    vec_shape = jax.ShapeDtypeStruct((1, D_MODEL), F32)
    gate_shape = jax.ShapeDtypeStruct((HEADS, HEAD_DIM, HEAD_DIM), F32)
    return _fused_call(
        body, jobs, name="bwd_lru", grid=(n_tiles,),
        in_specs=[pl.BlockSpec(memory_space=pl.ANY), tile(0), tile(0), tile(1), tile(0), prev8] + [tile(0)] * 3
                 + [_const((CONV_WIDTH, D_MODEL)), gate_w, gate_w, vec],
        out_specs=[pl.BlockSpec((SEQ_TILE, 2 * D_MODEL), lambda i: (rev(i), 0)), _const((SUBLANES, D_MODEL)), vec,
                   _const((HEADS, HEAD_DIM, HEAD_DIM)), vec, _const((HEADS, HEAD_DIM, HEAD_DIM)), vec, vec],
        out_shape=[jax.ShapeDtypeStruct((t, D_IN), BF16), jax.ShapeDtypeStruct((SUBLANES, D_MODEL), F32), vec_shape,
                   gate_shape, vec_shape, gate_shape, vec_shape, vec_shape],
        scratch_shapes=[pltpu.VMEM((1, D_MODEL), F32), pltpu.VMEM((SUBLANES, D_MODEL), F32)],
        input_output_aliases={0: 0},
        compiler_params=_params(),
    )(dz, dya, z, z, h, h, xc, r, ig, conv_w, wr, wi, lam)


def _bwd_in(dz, x, dh1, w_in_st, g1, jobs=()):
    t = x.shape[0]

    def body(dz_ref, x_ref, dh1_ref, w_ref, g_ref, dx_ref, dg1_ref):
        @pl.when(pl.program_id(0) == 0)
        def _():
            dg1_ref[...] = jnp.zeros_like(dg1_ref)

        dn1 = jnp.zeros((MM_TILE, D_MODEL), F32)
        for k in range(N_CHIPS):
            dn1 = dn1 + _dot_nt(dz_ref[:, k * IN_SHARD:(k + 1) * IN_SHARD], w_ref[k])
        xhat, r1 = _rms(x_ref[...])
        dg1_ref[...] = dg1_ref[...] + _col_sum(dn1 * xhat)
        dx_ref[...] = dh1_ref[...] + _rms_bwd(dn1 * g_ref[...], xhat, r1)

    tile = pl.BlockSpec((MM_TILE, D_MODEL), lambda i: (i, 0))
    return _fused_call(
        body, jobs, name="bwd_in", grid=(t // MM_TILE,),
        in_specs=[pl.BlockSpec((MM_TILE, D_IN), lambda i: (i, 0)), tile, tile,
                  _resident((N_CHIPS, D_MODEL, IN_SHARD)), _const((1, D_MODEL))],
        out_specs=[tile, _const((1, D_MODEL))],
        out_shape=[jax.ShapeDtypeStruct((t, D_MODEL), F32), jax.ShapeDtypeStruct((1, D_MODEL), F32)],
        compiler_params=_params(),
    )(dz, x, dh1, w_in_st, g1)


def _weight_grad(name, a, b, n_blocks, a_varies, b_varies, width, jobs=()):
    t = a.shape[0]
    rows = min(DW_TILE, t)
    n_t = t // rows

    def body(a_ref, b_ref, o_ref, acc_ref):
        s = pl.program_id(1)
        part = _dot_tn(a_ref[...], b_ref[...])

        @pl.when(s == 0)
        def _():
            acc_ref[...] = part

        @pl.when(s > 0)
        def _():
            acc_ref[...] = acc_ref[...] + part

        @pl.when(s == n_t - 1)
        def _():
            o_ref[...] = acc_ref[...].astype(BF16)

    return _fused_call(
        body, jobs, name=name, grid=(n_blocks, n_t),
        in_specs=[pl.BlockSpec((rows, D_MODEL), (lambda j, s: (s, j)) if a_varies else (lambda j, s: (s, 0))),
                  pl.BlockSpec((rows, width), (lambda j, s: (s, j)) if b_varies else (lambda j, s: (s, 0)))],
        out_specs=pl.BlockSpec((None, D_MODEL, width), lambda j, s: (j, 0, 0)),
        out_shape=jax.ShapeDtypeStruct((n_blocks, D_MODEL, width), BF16),
        scratch_shapes=[pltpu.VMEM((D_MODEL, width), F32)],
        compiler_params=_params(2),
    )(a, b)


def _place():
    x, y, c = lax.axis_index("x"), lax.axis_index("y"), lax.axis_index("c")
    other_chips = [(1 - x, y), (x, 1 - y), (1 - x, 1 - y)]
    return x, y, c, other_chips


def _chip_index(px, py):
    return 2 * px + py


ANY = pl.BlockSpec(memory_space=pl.ANY)


def _comm_call(name, jobs):
    return _fused_call(None, jobs, name=name, grid=(), in_specs=[], out_specs=[], out_shape=[])()[1]


def _gather_ici_job(shards):
    n = len(shards)
    halves = [s.shape[0] // 2 for s in shards]

    def copies(ins, outs, send, recv, local):
        x, y, c, chips = _place()
        me, sibling = (x, y, c), (x, y, 1 - c)

        def block(w, place):
            return outs[w].at[_chip_index(place[0], place[1]), pl.ds(place[2] * halves[w], halves[w]), :]

        def copy(w, k, blk, to, src=None):
            return pltpu.make_async_remote_copy(
                src_ref=block(w, blk) if src is None else src, dst_ref=block(w, blk),
                send_sem=send.at[4 * w + k], recv_sem=recv.at[4 * w + k], device_id=to, device_id_type=MESH)

        sends, arrivals, own = [], [], []
        for w in range(n):
            src = ins[w].at[pl.ds(c * halves[w], halves[w]), :]
            own.append(pltpu.make_async_copy(src, block(w, me), local.at[w]))
            sends.append(copy(w, 0, me, sibling, src))
            arrivals.append(copy(w, 0, sibling, me))
            for j, chip in enumerate(chips):
                sends.append(copy(w, 1 + j, me, (*chip, c), src))
                arrivals.append(copy(w, 1 + j, (*chip, c), me))
        return sends, arrivals, own

    return _Job(shards, [jax.ShapeDtypeStruct((N_CHIPS,) + s.shape, s.dtype) for s in shards], 4 * n, copies,
                n_local=n)


def _gather_pass_job(stacked):
    n = len(stacked)
    halves = [s.shape[1] // 2 for s in stacked]

    def copies(ins, outs, send, recv, local):
        del ins, local
        x, y, c, chips = _place()

        def copy(w, j, chip, pc, to):
            blk = outs[w].at[_chip_index(*chip), pl.ds(pc * halves[w], halves[w]), :]
            return pltpu.make_async_remote_copy(
                src_ref=blk, dst_ref=blk, send_sem=send.at[3 * w + j], recv_sem=recv.at[3 * w + j], device_id=to,
                device_id_type=MESH)

        sends = [copy(w, j, chip, c, (x, y, 1 - c)) for w in range(n) for j, chip in enumerate(chips)]
        arrivals = [copy(w, j, chip, 1 - c, (x, y, c)) for w in range(n) for j, chip in enumerate(chips)]
        return sends, arrivals, []

    return _Job(stacked, [jax.ShapeDtypeStruct(s.shape, s.dtype) for s in stacked], 3 * n, copies,
                aliases={w: w for w in range(n)})


def _gather_small_job(block):
    def copies(ins, outs, send, recv, local):
        x, y, c, chips = _place()

        def copy(j, chip_from, to):
            return pltpu.make_async_remote_copy(
                src_ref=ins[0], dst_ref=outs[0].at[_chip_index(*chip_from)], send_sem=send.at[j],
                recv_sem=recv.at[j], device_id=to, device_id_type=MESH)

        own = [pltpu.make_async_copy(ins[0], outs[0].at[_chip_index(x, y)], local.at[0])]
        sends = [copy(j, (x, y), (*chip, c)) for j, chip in enumerate(chips)]
        arrivals = [copy(j, chip, (x, y, c)) for j, chip in enumerate(chips)]
        return sends, arrivals, own

    return _Job([block], [jax.ShapeDtypeStruct((N_CHIPS,) + block.shape, block.dtype)], 3, copies, n_local=1)


def _pair_send_job(grads):
    n = len(grads)
    halves = [g.shape[1] // 2 for g in grads]

    def copies(ins, outs, send, recv, local):
        del local
        x, y, c, _ = _place()
        sends = [pltpu.make_async_remote_copy(
            src_ref=ins[w].at[:, pl.ds((1 - c) * halves[w], halves[w]), :], dst_ref=outs[w], send_sem=send.at[w],
            recv_sem=recv.at[w], device_id=(x, y, 1 - c), device_id_type=MESH) for w in range(n)]
        return sends, sends, []

    return _Job(grads, [jax.ShapeDtypeStruct((N_CHIPS, h, g.shape[2]), g.dtype) for g, h in zip(grads, halves)], n,
                copies)


def _row_block(rows, limit=256):
    return min(rows, limit)


def _pair_add(name, core, mine, theirs):
    _, _, h, cols = mine.shape
    rb = _row_block(h, 512)

    def body(core_ref, a_ref, b_ref, o_ref):
        del core_ref
        o_ref[...] = (a_ref[...].astype(F32) + b_ref[...].astype(F32)).astype(BF16)

    return pl.pallas_call(
        body, name=name,
        grid_spec=pltpu.PrefetchScalarGridSpec(
            num_scalar_prefetch=1, grid=(N_CHIPS, h // rb),
            in_specs=[pl.BlockSpec((None, None, rb, cols), lambda k, r, core_ref: (k, core_ref[0], r, 0)),
                      pl.BlockSpec((None, rb, cols), lambda k, r, core_ref: (k, r, 0))],
            out_specs=pl.BlockSpec((None, rb, cols), lambda k, r, core_ref: (k, r, 0))),
        out_shape=jax.ShapeDtypeStruct(theirs.shape, BF16),
        compiler_params=_params(2),
    )(core, mine, theirs)


def _chip_exchange_job(sums):
    n = len(sums)

    def copies(ins, outs, send, recv, local):
        del local
        _, _, c, chips = _place()
        sends = [pltpu.make_async_remote_copy(
            src_ref=ins[w].at[_chip_index(*chip)], dst_ref=outs[w].at[j], send_sem=send.at[3 * w + j],
            recv_sem=recv.at[3 * w + j], device_id=(*chip, c), device_id_type=MESH)
            for w in range(n) for j, chip in enumerate(chips)]
        return sends, sends, []

    return _Job(sums, [jax.ShapeDtypeStruct((N_CHIPS - 1,) + s.shape[1:], s.dtype) for s in sums], 3 * n, copies)


def _chip_sum(name, place, mine, theirs):
    _, h, cols = mine.shape
    rb = _row_block(h, 512)

    def body(place_ref, p_ref, q_ref, o_ref):
        del place_ref
        acc = p_ref[...].astype(F32)
        for j in range(N_CHIPS - 1):
            acc = acc + q_ref[j].astype(F32)
        o_ref[...] = acc

    return pl.pallas_call(
        body, name=name,
        grid_spec=pltpu.PrefetchScalarGridSpec(
            num_scalar_prefetch=1, grid=(h // rb,),
            in_specs=[pl.BlockSpec((None, rb, cols), lambda r, place_ref: (place_ref[0], r, 0)),
                      pl.BlockSpec((N_CHIPS - 1, rb, cols), lambda r, place_ref: (0, r, 0))],
            out_specs=pl.BlockSpec((None, rb, cols), lambda r, place_ref: (place_ref[1], r, 0))),
        out_shape=jax.ShapeDtypeStruct((2, h, cols), F32),
        compiler_params=_params(),
    )(place, mine, theirs)


def _share_job(bufs):
    n = len(bufs)

    def copies(ins, outs, send, recv, local):
        del ins, local
        x, y, c, _ = _place()

        def copy(w, half):
            return pltpu.make_async_remote_copy(
                src_ref=outs[w].at[half], dst_ref=outs[w].at[half], send_sem=send.at[w], recv_sem=recv.at[w],
                device_id=(x, y, 1 - c), device_id_type=MESH)

        return [copy(w, c) for w in range(n)], [copy(w, 1 - c) for w in range(n)], []

    return _Job(bufs, [jax.ShapeDtypeStruct(b.shape, b.dtype) for b in bufs], n, copies,
                aliases={w: w for w in range(n)})


SMALL_ROWS = 24
ROW_G1, ROW_CW, ROW_CB, ROW_BR, ROW_BI, ROW_LAM, ROW_LG, ROW_LB, ROW_G2, ROW_G3, ROW_LOSS, ROW_BS = (
    0, 1, 5, 6, 7, 8, 9, 10, 11, 12, 13, 16)
N_DEV = 8


def _pack_small(dcw, dcb, dbr, dbi, dlam, dlg, dlb, dg2, dg3, loss, dbs):
    def body(dcw_ref, dcb_ref, dbr_ref, dbi_ref, dlam_ref, dlg_ref, dlb_ref, dg2_ref, dg3_ref, loss_ref, dbs_ref, out):
        out[...] = jnp.zeros((SMALL_ROWS, D_MODEL), F32)
        for row, ref in ((ROW_CB, dcb_ref), (ROW_BR, dbr_ref), (ROW_BI, dbi_ref), (ROW_LAM, dlam_ref),
                         (ROW_LG, dlg_ref), (ROW_LB, dlb_ref), (ROW_G2, dg2_ref), (ROW_G3, dg3_ref)):
            out[row:row + 1, :] = ref[...]
        out[ROW_CW:ROW_CW + CONV_WIDTH, :] = dcw_ref[0:CONV_WIDTH, :]
        out[ROW_LOSS:ROW_LOSS + 1, 0:128] = loss_ref[0:1, :]
        out[ROW_BS:ROW_BS + GROUPS, 0:128] = jnp.transpose(dbs_ref[...])[0:GROUPS, :]

    vm = pl.BlockSpec(memory_space=pltpu.VMEM)
    return pl.pallas_call(
        body, name="pack_small", in_specs=[vm] * 11, out_specs=vm,
        out_shape=jax.ShapeDtypeStruct((SMALL_ROWS, D_MODEL), F32),
    )(dcw, dcb, dbr, dbi, dlam, dlg, dlb, dg2, dg3, loss, dbs)


def _gather_all_job(blocks):
    n = len(blocks)
    flips = [(dx, dy, dc) for dx in (0, 1) for dy in (0, 1) for dc in (0, 1)][1:]

    def copies(ins, outs, send, recv, local):
        x, y, c, _ = _place()
        me = 4 * x + 2 * y + c
        sends, arrivals, own = [], [], []
        for w in range(n):
            own.append(pltpu.make_async_copy(ins[w], outs[w].at[me], local.at[w]))
            for k, (dx, dy, dc) in enumerate(flips):
                peer = (x ^ dx, y ^ dy, c ^ dc)
                sem = dict(send_sem=send.at[7 * w + k], recv_sem=recv.at[7 * w + k])
                sends.append(pltpu.make_async_remote_copy(
                    src_ref=ins[w], dst_ref=outs[w].at[me], device_id=peer, device_id_type=MESH, **sem))
                arrivals.append(pltpu.make_async_remote_copy(
                    src_ref=ins[w], dst_ref=outs[w].at[4 * peer[0] + 2 * peer[1] + peer[2]], device_id=peer,
                    device_id_type=MESH, **sem))
        return sends, arrivals, own

    return _Job(blocks, [jax.ShapeDtypeStruct((N_DEV,) + b.shape, b.dtype) for b in blocks], 7 * n, copies, n_local=n)


def _sum_small(vec_all, ws_all, dg1_all):
    def body(vec_ref, ws_ref, dg1_ref, vec_out, ws_out):
        vec, ws, dg1 = vec_ref[0], ws_ref[0], dg1_ref[0]
        for d in range(1, N_DEV):
            vec, ws, dg1 = vec + vec_ref[d], ws + ws_ref[d], dg1 + dg1_ref[d]
        vec_out[...] = vec
        vec_out[ROW_G1:ROW_G1 + 1, :] = dg1
        ws_out[...] = ws

    vm = pl.BlockSpec(memory_space=pltpu.VMEM)
    return pl.pallas_call(
        body, name="sum_small", in_specs=[vm] * 3, out_specs=[vm, vm],
        out_shape=[jax.ShapeDtypeStruct(vec_all.shape[1:], F32), jax.ShapeDtypeStruct(ws_all.shape[1:], F32)],
    )(vec_all, ws_all, dg1_all)


def _adamw_math(w, g, m, v):
    m = ADAM_B1 * m + (1.0 - ADAM_B1) * g
    v = ADAM_B2 * v + (1.0 - ADAM_B2) * (g * g)
    m_hat = m / (1.0 - ADAM_B1 ** ADAM_STEP)
    v_hat = v / (1.0 - ADAM_B2 ** ADAM_STEP)
    delta = (-ADAM_LR) * (m_hat / (jnp.sqrt(v_hat) + ADAM_EPS) + ADAM_WD * w)
    return delta, m, v


def _adamw(name, g, w, m, v, jobs=()):
    rows, cols = w.shape
    rb = _row_block(rows)

    def body(g_ref, w_ref, m_ref, v_ref, d_ref, nm_ref, nv_ref):
        d_ref[...], nm_ref[...], nv_ref[...] = _adamw_math(w_ref[...], g_ref[...], m_ref[...], v_ref[...])

    blk = pl.BlockSpec((rb, cols), lambda r: (r, 0))
    return _fused_call(
        body, jobs, name=name, grid=(rows // rb,), in_specs=[blk] * 4, out_specs=[blk] * 3,
        out_shape=[jax.ShapeDtypeStruct(w.shape, F32)] * 3, compiler_params=_params(),
    )(g, w, m, v)


def _adamw_small(grads, ws, ms, vs):
    n = len(grads)

    def body(*refs):
        g_refs, w_refs, m_refs, v_refs = refs[:n], refs[n:2 * n], refs[2 * n:3 * n], refs[3 * n:4 * n]
        outs = refs[4 * n:]
        for p in range(n):
            d, nm, nv = _adamw_math(w_refs[p][...], g_refs[p][...], m_refs[p][...], v_refs[p][...])
            outs[p][...] = d
            outs[n + p][...] = nm
            outs[2 * n + p][...] = nv

    vm = pl.BlockSpec(memory_space=pltpu.VMEM)
    shapes = [jax.ShapeDtypeStruct(w.shape, F32) for w in ws]
    out = pl.pallas_call(
        body, name="adamw_small", in_specs=[vm] * (4 * n), out_specs=[vm] * (3 * n), out_shape=shapes * 3,
    )(*grads, *ws, *ms, *vs)
    return out[:n], out[n:2 * n], out[2 * n:]


def _unstack_heads(w_st):
    per = HEAD_DIM // N_CHIPS
    return w_st.reshape(N_CHIPS, HEADS, per, HEAD_DIM).transpose(1, 0, 2, 3).reshape(HEADS, HEAD_DIM, HEAD_DIM)


def _stack_heads(w):
    per = HEAD_DIM // N_CHIPS
    return w.reshape(HEADS, N_CHIPS, per, HEAD_DIM).transpose(1, 0, 2, 3).reshape(N_CHIPS, HEADS * per, HEAD_DIM)


def kernel(x, norm_mix_g, w_in, conv_w, conv_b, w_rgate, b_rgate, w_igate, b_igate, lru_lambda, w_out_a, sgu_ln_g, sgu_ln_b, sgu_w_s, sgu_b_s, w_out_b, w_out, norm_mlp_g, w_up, w_down, norm_final_g, loss_target, m_norm_mix_g, m_w_in, m_conv_w, m_conv_b, m_w_rgate, m_b_rgate, m_w_igate, m_b_igate, m_lru_lambda, m_w_out_a, m_sgu_ln_g, m_sgu_ln_b, m_sgu_w_s, m_sgu_b_s, m_w_out_b, m_w_out, m_norm_mlp_g, m_w_up, m_w_down, m_norm_final_g, v_norm_mix_g, v_w_in, v_conv_w, v_conv_b, v_w_rgate, v_b_rgate, v_w_igate, v_b_igate, v_lru_lambda, v_w_out_a, v_sgu_ln_g, v_sgu_ln_b, v_sgu_w_s, v_sgu_b_s, v_w_out_b, v_w_out, v_norm_mlp_g, v_w_up, v_w_down, v_norm_final_g):
    chip = _chip_index(lax.axis_index("x"), lax.axis_index("y"))
    core = lax.axis_index("c")
    quarter_h = HEAD_DIM // N_CHIPS
    quarter_d = D_MODEL // N_CHIPS

    as_2d = lambda a: a.reshape(-1, a.shape[-1])
    big_w = [as_2d(w) for w in (w_in, w_rgate, w_igate, w_out_a, w_out_b, w_out, w_up, w_down)]
    big_m = [as_2d(w) for w in (m_w_in, m_w_rgate, m_w_igate, m_w_out_a, m_w_out_b, m_w_out, m_w_up, m_w_down)]
    big_v = [as_2d(w) for w in (v_w_in, v_w_rgate, v_w_igate, v_w_out_a, v_w_out_b, v_w_out, v_w_up, v_w_down)]

    packed = jnp.concatenate([conv_w[0], b_rgate[0], b_igate[0]], axis=1)
    packed = jnp.concatenate([packed, jnp.zeros_like(packed)], axis=0)
    s_in, s_r, s_i, s_oa, s_ob, s_out, s_up, s_down = [w.astype(BF16) for w in big_w]
    xs, target = x[0], loss_target[0]
    g3 = norm_final_g.reshape(1, D_MODEL)
    bias_s = jnp.broadcast_to(jnp.transpose(sgu_b_s[0])[:, :, None], (CHUNK, GROUPS, GROUP_DIM)).reshape(CHUNK, D_MODEL)
    core_arr = core.reshape(1).astype(jnp.int32)
    place = jnp.stack([chip, core]).astype(jnp.int32)
    quarter = lambda g: g.reshape(N_CHIPS, D_MODEL // N_CHIPS, D_MODEL)

    def pair_add(nm, g, from_sibling):
        return _pair_add("pair_add_" + nm, core_arr, g.reshape(N_CHIPS, 2, g.shape[1] // 2, g.shape[2]), from_sibling)

    def chip_sum(nm, pair, from_chips):
        return _chip_sum("chip_sum_" + nm, place, pair, from_chips)

    order = jnp.stack([chip, chip ^ 2, chip ^ 1, chip ^ 3]).astype(jnp.int32)
    (z, n1, (w_in_st, wr_st, wi_st)), ((packed_all,), ab) = _fwd_in(
        xs, norm_mix_g, [s_in, s_r, s_i], order, jobs=[_gather_small_job(packed), _gather_ici_job([s_oa, s_ob])])
    pick = lambda lo, hi: packed_all[:, :HEADS, lo:hi].transpose(1, 0, 2).reshape(HEADS, -1)
    conv_w_full = pick(0, quarter_d)
    br_full = pick(quarter_d, quarter_d + quarter_h).reshape(1, D_MODEL)
    bi_full = pick(quarter_d + quarter_h, quarter_d + 2 * quarter_h).reshape(1, D_MODEL)
    wr, wi = _unstack_heads(wr_st), _unstack_heads(wi_st)
    lru = (conv_w_full, conv_b, wr, br_full, wi, bi_full, lru_lambda)
    sgu = (sgu_ln_g, sgu_ln_b, sgu_w_s[0], bias_s)

    (ya, *saved), (ab, rest) = _fwd_lru(z, *lru, jobs=[_gather_pass_job(ab), _gather_ici_job([s_out, s_up])])
    yb, (rest, (down,)) = _fwd_sgu(z, *sgu, jobs=[_gather_pass_job(rest), _gather_ici_job([s_down])])
    w_oa, w_ob, w_o = [w.reshape(D_MODEL, D_MODEL) for w in ab + rest[:1]]
    w_up_st = rest[1]
    (pa, pb, h1, n2), ((down,),) = _fwd_merge(ya, yb, z, xs, w_oa, w_ob, w_o, norm_mlp_g,
                                              jobs=[_gather_pass_job([down])])
    w_dn = down.reshape(D_FF, D_MODEL)
    (up, act, dh2, dh2b, loss_part, dg3), _ = _fwd_mlp(n2, h1, target, w_up_st, w_dn, g3)

    (dup, dh1, dg2), _ = _bwd_mlp(dh2, dh2b, up, h1, w_up_st, w_dn, norm_mlp_g)
    d_up, _ = _weight_grad("dw_up", n2, dup, N_CHIPS, False, True, D_MODEL)
    d_down, ((r_up,),) = _weight_grad("dw_down", act, dh2b, N_CHIPS, True, False, D_MODEL,
                                      jobs=[_pair_send_job([d_up])])
    p_up = pair_add("w_up", d_up, r_up)
    (dz, dya, dyb, merged, dpa, dpb, dh1b), ((r_down,), (q_up,)) = _bwd_merge(
        dh1, pa, pb, z, w_oa, w_ob, w_o, jobs=[_pair_send_job([d_down]), _chip_exchange_job([p_up])])
    p_down = pair_add("w_down", d_down, r_down)
    half_up = chip_sum("w_up", p_up, q_up)
    d_out, ((full_up,),) = _weight_grad("dw_out", merged, dh1b, 1, False, False, D_MODEL, jobs=[_share_job([half_up])])
    d_oa, _ = _weight_grad("dw_out_a", ya, dpa, 1, False, False, D_MODEL)
    d_ob, _ = _weight_grad("dw_out_b", yb, dpb, 1, False, False, D_MODEL)
    mids = [quarter(d_oa), quarter(d_ob), quarter(d_out)]
    (dz, dlg, dlb, dws, dbs), ((q_down,), r_mids) = _bwd_sgu(
        dz, dyb, z, *sgu, jobs=[_chip_exchange_job([p_down]), _pair_send_job(mids)])
    mid_names = ("w_out_a", "w_out_b", "w_out")
    p_mids = [pair_add(nm, g, r) for nm, g, r in zip(mid_names, mids, r_mids)]
    half_down = chip_sum("w_down", p_down, q_down)
    (dz, dcw, dcb, dwr, dbr, dwi, dbi, dlam), (q_mids, (full_down,)) = _bwd_lru(
        dz, dya, z, *saved, conv_w_full, wr, wi, lru_lambda, jobs=[_chip_exchange_job(p_mids), _share_job([half_down])])
    half_mids = [chip_sum(nm, p, q) for nm, p, q in zip(mid_names, p_mids, q_mids)]
    gates = [_stack_heads(dwr).astype(BF16), _stack_heads(dwi).astype(BF16)]
    small = _pack_small(dcw, dcb, dbr, dbi, dlam, dlg, dlb, dg2, dg3, loss_part, dbs)
    d_in, (full_mids, r_gates, (vec_all, ws_all)) = _weight_grad(
        "dw_in", n1, dz, N_CHIPS, False, True, IN_SHARD,
        jobs=[_share_job(half_mids), _pair_send_job(gates), _gather_all_job([small, dws])])
    names = ("w_in", "w_rgate", "w_igate", "w_out_a", "w_out_b", "w_out", "w_up", "w_down")
    adam_args = {nm: (w, m, v) for nm, w, m, v in zip(names, big_w, big_m, big_v)}

    def adamw(nm, g, jobs=()):
        w, m, v = adam_args[nm]
        g = g.reshape(w.shape)
        return (g,) + tuple(x for x in _adamw("adamw_" + nm, g, w, m, v, jobs))

    (r_in,), = _comm_call("send_w_in", [_pair_send_job([d_in])])
    last_names = ("w_in", "w_rgate", "w_igate")
    p_last = [pair_add(nm, g, r) for nm, g, r in zip(last_names, [d_in] + gates, [r_in] + r_gates)]
    (grad_x, dg1), (q_last,) = _bwd_in(dz, xs, dh1, w_in_st, norm_mix_g, jobs=[_chip_exchange_job(p_last)])
    half_last = [chip_sum(nm, p, q) for nm, p, q in zip(last_names, p_last, q_last)]
    full_last, (dg1_all,) = _comm_call("share_last", [_share_job(half_last), _gather_all_job([dg1])])
    full, big_out = [], []
    for nm, f in zip(names, full_last + full_mids + [full_up, full_down]):
        g, out, _ = adamw(nm, f)
        full.append(g)
        big_out.append(out)

    vec, ws_sum = _sum_small(vec_all, ws_all, dg1_all)
    row = lambda r: vec[r:r + 1]
    shard = lambda a, width: lax.dynamic_slice_in_dim(a, chip * width, width, axis=1)
    g_small = dict(
        norm_mix_g=row(ROW_G1), conv_w=shard(vec[ROW_CW:ROW_CW + CONV_WIDTH], quarter_d), conv_b=row(ROW_CB),
        b_rgate=shard(row(ROW_BR).reshape(HEADS, HEAD_DIM), quarter_h),
        b_igate=shard(row(ROW_BI).reshape(HEADS, HEAD_DIM), quarter_h), lru_lambda=row(ROW_LAM),
        sgu_ln_g=row(ROW_LG), sgu_ln_b=row(ROW_LB),
        sgu_w_s=ws_sum.reshape(CHUNK, GROUPS, CHUNK).transpose(1, 0, 2).reshape(GROUPS * CHUNK, CHUNK),
        sgu_b_s=vec[ROW_BS:ROW_BS + GROUPS, 0:CHUNK], norm_mlp_g=row(ROW_G2), norm_final_g=row(ROW_G3))
    loss = vec[ROW_LOSS, 0]
    small_names = list(g_small)
    given = dict(
        norm_mix_g=(norm_mix_g, m_norm_mix_g, v_norm_mix_g), conv_w=(conv_w, m_conv_w, v_conv_w),
        conv_b=(conv_b, m_conv_b, v_conv_b), b_rgate=(b_rgate, m_b_rgate, v_b_rgate),
        b_igate=(b_igate, m_b_igate, v_b_igate), lru_lambda=(lru_lambda, m_lru_lambda, v_lru_lambda),
        sgu_ln_g=(sgu_ln_g, m_sgu_ln_g, v_sgu_ln_g), sgu_ln_b=(sgu_ln_b, m_sgu_ln_b, v_sgu_ln_b),
        sgu_w_s=(sgu_w_s, m_sgu_w_s, v_sgu_w_s), sgu_b_s=(sgu_b_s, m_sgu_b_s, v_sgu_b_s),
        norm_mlp_g=(norm_mlp_g, m_norm_mlp_g, v_norm_mlp_g), norm_final_g=(norm_final_g, m_norm_final_g, v_norm_final_g))
    g2d = [g_small[nm] for nm in small_names]
    to2d = lambda a, g: a.reshape(g.shape)
    d_s, m_s, v_s = _adamw_small(
        g2d, *[[to2d(given[nm][q], g) for nm, g in zip(small_names, g2d)] for q in range(3)])

    shapes = dict(
        norm_mix_g=norm_mix_g, w_in=w_in, conv_w=conv_w, conv_b=conv_b, w_rgate=w_rgate, b_rgate=b_rgate,
        w_igate=w_igate, b_igate=b_igate, lru_lambda=lru_lambda, w_out_a=w_out_a, sgu_ln_g=sgu_ln_g,
        sgu_ln_b=sgu_ln_b, sgu_w_s=sgu_w_s, sgu_b_s=sgu_b_s, w_out_b=w_out_b, w_out=w_out, norm_mlp_g=norm_mlp_g,
        w_up=w_up, w_down=w_down, norm_final_g=norm_final_g)
    grads, deltas, new_m, new_v = {}, {}, {}, {}
    for nm, g, (d, nmom, nvar) in zip(names, full, big_out):
        grads[nm], deltas[nm], new_m[nm], new_v[nm] = g, d, nmom, nvar
    for p, nm in enumerate(small_names):
        grads[nm], deltas[nm], new_m[nm], new_v[nm] = g2d[p], d_s[p], m_s[p], v_s[p]
    order = list(shapes)
    out = [loss, grad_x[None]]
    for group in (grads, deltas, new_m, new_v):
        out += [group[nm].reshape(shapes[nm].shape) for nm in order]
    return tuple(out)
```

```python
import functools

import jax
import jax.numpy as jnp
from jax import lax
from jax.experimental import pallas as pl
from jax.experimental.pallas import tpu as pltpu

F32 = jnp.float32
BF16 = jnp.bfloat16
MESH = pl.DeviceIdType.MESH

D_MODEL = 1024
D_IN = 6 * D_MODEL
D_FF = 4 * D_MODEL
N_CHIPS = 4
IN_SHARD = D_IN // N_CHIPS
HEADS = 4
HEAD_DIM = D_MODEL // HEADS
GROUPS = 4
GROUP_DIM = D_MODEL // GROUPS
CHUNK = 128
CONV_WIDTH = 4
LRU_C = 8.0
NORM_EPS = 1e-6
LN_EPS = 1e-5

ADAM_LR = 0.001
ADAM_B1 = 0.9
ADAM_B2 = 0.999
ADAM_EPS = 1e-08
ADAM_WD = 0.01
ADAM_STEP = 10

SUBLANES = 8
MM_TILE = 512
SEQ_TILE = 256
DW_TILE = 2048
VMEM_LIMIT_BYTES = 56 * 1024 * 1024

GELU_K0 = 0.7978845608028654
GELU_K1 = 0.044715


def _params(n_grid_axes=1):
    return pltpu.CompilerParams(
        dimension_semantics=("arbitrary",) * n_grid_axes, vmem_limit_bytes=VMEM_LIMIT_BYTES)


def _resident(shape):
    nd = len(shape)
    return pl.BlockSpec(shape, lambda *_: (0,) * nd, pipeline_mode=pl.Buffered(1))


def _const(shape):
    nd = len(shape)
    return pl.BlockSpec(shape, lambda *_: (0,) * nd)


def _dot(a, b):
    return jnp.dot(a, b, preferred_element_type=F32)


def _dot_nt(a, b):
    return lax.dot_general(a, b, (((1,), (1,)), ((), ())), preferred_element_type=F32)


def _dot_tn(a, b):
    return lax.dot_general(a, b, (((0,), (0,)), ((), ())), preferred_element_type=F32)


def _gelu(x):
    t = jnp.tanh(GELU_K0 * x * (1.0 + GELU_K1 * x * x))
    return 0.5 * x * (1.0 + t)


def _gelu_and_grad(x):
    x2 = x * x
    t = jnp.tanh(GELU_K0 * x * (1.0 + GELU_K1 * x2))
    g = 0.5 * x * (1.0 + t)
    dg = 0.5 * (1.0 + t) + 0.5 * x * (1.0 - t * t) * (GELU_K0 * (1.0 + 3.0 * GELU_K1 * x2))
    return g, dg


def _rms(x):
    r = lax.rsqrt(jnp.mean(x * x, axis=-1, keepdims=True) + NORM_EPS)
    return x * r, r


def _rms_bwd(dn, xhat, r):
    return r * (dn - xhat * jnp.mean(dn * xhat, axis=-1, keepdims=True))


def _col_sum(v):
    return jnp.sum(v, axis=0, keepdims=True)


def _shift_down(x, tail8, k):
    xs = pltpu.roll(x, k, 0)
    ts = pltpu.roll(tail8, k, 0)
    ridx = lax.broadcasted_iota(jnp.int32, tail8.shape, 0)
    head = jnp.where(ridx < k, ts, xs[0:SUBLANES])
    return jnp.concatenate([head, xs[SUBLANES:]], axis=0)


def _shift_up(x, head8, k):
    n = x.shape[0]
    xs = pltpu.roll(x, n - k, 0)
    hs = pltpu.roll(head8, SUBLANES - k, 0)
    ridx = lax.broadcasted_iota(jnp.int32, head8.shape, 0)
    last = jnp.where(ridx >= SUBLANES - k, hs, xs[n - SUBLANES:n])
    return jnp.concatenate([xs[:n - SUBLANES], last], axis=0)


def _scan_forward(a, b, carry):
    n, cols = a.shape
    groups = n // SUBLANES
    a = a.reshape(groups, SUBLANES, cols)
    b = b.reshape(groups, SUBLANES, cols)
    sub = lax.broadcasted_iota(jnp.int32, a.shape, 1)
    for s in (1, 2, 4):
        a_s = pltpu.roll(a, s, 1)
        b_s = pltpu.roll(b, s, 1)
        m = sub >= s
        b = jnp.where(m, a * b_s + b, b)
        a = jnp.where(m, a * a_s, a)
    out = []
    for g in range(groups):
        h = a[g] * carry + b[g]
        out.append(h)
        carry = h[SUBLANES - 1:SUBLANES]
    return jnp.concatenate(out, axis=0), carry


def _scan_backward(a, b, carry):
    n, cols = a.shape
    groups = n // SUBLANES
    a = a.reshape(groups, SUBLANES, cols)
    b = b.reshape(groups, SUBLANES, cols)
    sub = lax.broadcasted_iota(jnp.int32, a.shape, 1)
    for s in (1, 2, 4):
        a_s = pltpu.roll(a, SUBLANES - s, 1)
        b_s = pltpu.roll(b, SUBLANES - s, 1)
        m = sub < SUBLANES - s
        b = jnp.where(m, a * b_s + b, b)
        a = jnp.where(m, a * a_s, a)
    out = [None] * groups
    for g in reversed(range(groups)):
        h = a[g] * carry + b[g]
        out[g] = h
        carry = h[0:1]
    return jnp.concatenate(out, axis=0), carry


def _softplus_neg(lam):
    e = jnp.exp(-jnp.abs(lam))
    u = 1.0 + e
    log1p_e = jnp.where(u == 1.0, e, jnp.log(u) * (e / jnp.where(u == 1.0, 1.0, u - 1.0)))
    return jnp.maximum(-lam, 0.0) + log1p_e


def _lru_gates(xa, tail8, cw_ref, cb_ref, wr_ref, br_ref, wi_ref, bi_ref, lam_ref):
    cw = cw_ref[...]
    xc = cb_ref[...] + cw[0:1] * xa
    for k in range(1, CONV_WIDTH):
        xc = xc + cw[k:k + 1] * _shift_down(xa, tail8, k)
    xcb = xc.astype(BF16)
    pre_r, pre_i = [], []
    for h in range(HEADS):
        cols = slice(h * HEAD_DIM, (h + 1) * HEAD_DIM)
        pre_r.append(_dot(xcb[:, cols], wr_ref[h]))
        pre_i.append(_dot(xcb[:, cols], wi_ref[h]))
    r = jax.nn.sigmoid(jnp.concatenate(pre_r, axis=1) + br_ref[...])
    ig = jax.nn.sigmoid(jnp.concatenate(pre_i, axis=1) + bi_ref[...])
    _, a, mult = _decay(r, lam_ref)
    return xc, r, ig, a, mult


def _decay(r, lam_ref):
    sp = _softplus_neg(lam_ref[...])
    log_a = ((-LRU_C) * sp) * r
    a = jnp.exp(log_a)
    th = jnp.tanh(log_a)
    return sp, a, jnp.sqrt((-2.0 * th) / (1.0 - th))


class _Job:
    def __init__(self, inputs, out_shape, n_sem, copies, aliases=None, n_local=0):
        self.inputs, self.out_shape, self.n_sem, self.copies = list(inputs), list(out_shape), n_sem, copies
        self.aliases, self.n_local = dict(aliases or {}), n_local


def _fused_call(body, jobs, *, name, grid, in_specs, out_specs, out_shape, scratch_shapes=(),
                input_output_aliases=None, compiler_params=None, n_prefetch=0, jobs_start_after=None):
    single = not isinstance(out_shape, (list, tuple))
    out_specs = [out_specs] if single else list(out_specs)
    out_shape = [out_shape] if single else list(out_shape)
    n_scr = len(scratch_shapes)
    in_specs, scratch_shapes = list(in_specs), list(scratch_shapes)
    n_in, n_out = len(in_specs), len(out_shape)
    aliases = dict(input_output_aliases or {})
    in_at, out_at = [], []
    for job in jobs:
        in_at.append(len(in_specs))
        out_at.append(len(out_shape))
        for i, o in job.aliases.items():
            aliases[n_prefetch + len(in_specs) + i] = len(out_shape) + o
        in_specs += [ANY] * len(job.inputs)
        out_specs += [ANY] * len(job.out_shape)
        out_shape += job.out_shape
        scratch_shapes += [pltpu.SemaphoreType.DMA((job.n_sem,)), pltpu.SemaphoreType.DMA((job.n_sem,)),
                           pltpu.SemaphoreType.DMA((max(job.n_local, 1),))]
    n_in_all, n_out_all = len(in_specs), len(out_shape)

    def full_body(*refs):
        prefetch, refs = refs[:n_prefetch], refs[n_prefetch:]
        ins, outs, scr = refs[:n_in_all], refs[n_in_all:n_in_all + n_out_all], refs[n_in_all + n_out_all:]

        def copies(q):
            job = jobs[q]
            return job.copies(ins[in_at[q]:in_at[q] + len(job.inputs)], outs[out_at[q]:out_at[q] + len(job.out_shape)],
                              *scr[n_scr + 3 * q:n_scr + 3 * q + 3])

        def start():
            for q in range(len(jobs)):
                sends, _, local = copies(q)
                for cp in local + sends:
                    cp.start()

        def finish():
            every = [copies(q) for q in range(len(jobs))]
            for _, arrivals, _ in every:
                for cp in arrivals:
                    cp.wait_recv()
            for sends, _, local in every:
                for cp in sends:
                    cp.wait_send()
                for cp in local:
                    cp.wait()

        if not grid:
            start()
            finish()
            return
        ids = [pl.program_id(a) for a in range(len(grid))]
        at_step = lambda step: functools.reduce(jnp.logical_and, [i == k for i, k in zip(ids, step)])
        if jobs and jobs_start_after is None:
            pl.when(at_step((0,) * len(grid)))(start)
        body(*prefetch, *ins[:n_in], *outs[:n_out], *scr[:n_scr])
        if jobs and jobs_start_after is not None:
            pl.when(at_step(jobs_start_after))(start)
        if jobs:
            pl.when(functools.reduce(jnp.logical_and, [i == g - 1 for i, g in zip(ids, grid)]))(finish)

    if n_prefetch:
        layout = dict(grid_spec=pltpu.PrefetchScalarGridSpec(
            num_scalar_prefetch=n_prefetch, grid=grid, in_specs=in_specs, out_specs=out_specs,
            scratch_shapes=scratch_shapes))
    else:
        layout = dict(grid=grid, in_specs=in_specs, out_specs=out_specs, scratch_shapes=scratch_shapes)
    call = pl.pallas_call(
        full_body, name=name, out_shape=out_shape, input_output_aliases=aliases, compiler_params=compiler_params,
        **layout)

    def run(*args):
        res = call(*args, *[a for job in jobs for a in job.inputs])
        mine = res[0] if single else list(res[:n_out])
        return mine, [list(res[at:at + len(job.out_shape)]) for at, job in zip(out_at, jobs)]

    return run


def _fwd_in(x, g1, shards, order, jobs=()):
    t = x.shape[0]
    n_tiles = t // MM_TILE
    n = len(shards)
    halves = [s.shape[0] // 2 for s in shards]

    def body(order_ref, x_ref, g_ref, *refs):
        del order_ref
        ins, (z_ref, n_ref), outs = refs[:n], refs[n:n + 2], refs[n + 2:2 * n + 2]
        wbuf, send, recv, local = refs[2 * n + 2:]
        s, i = pl.program_id(0), pl.program_id(1)
        x_, y_, c, chips = _place()
        k_me = _chip_index(x_, y_)

        def block(w, chip, pc):
            return outs[w].at[_chip_index(*chip), pl.ds(pc * halves[w], halves[w]), :]

        def over_ici(w, j, landing):
            return pltpu.make_async_remote_copy(
                src_ref=ins[w].at[pl.ds(c * halves[w], halves[w]), :],
                dst_ref=block(w, chips[j] if landing else (x_, y_), c), send_sem=send.at[6 * w + j],
                recv_sem=recv.at[6 * w + j], device_id=(*chips[j], c), device_id_type=MESH)

        def to_sibling(w, j, landing):
            blk = block(w, chips[j], 1 - c if landing else c)
            return pltpu.make_async_remote_copy(
                src_ref=blk, dst_ref=blk, send_sem=send.at[6 * w + 3 + j], recv_sem=recv.at[6 * w + 3 + j],
                device_id=(x_, y_, 1 - c), device_id_type=MESH)

        own = [pltpu.make_async_copy(wbuf, outs[0].at[k_me], local.at[0])]
        own += [pltpu.make_async_copy(ins[w], outs[w].at[k_me], local.at[w]) for w in range(1, n)]

        @pl.when((s == 0) & (i == 0))
        def _():
            for j in range(2):
                for w in range(n):
                    over_ici(w, j, False).start()
            load = pltpu.make_async_copy(ins[0], wbuf, local.at[n])
            load.start()
            load.wait()
            for cp in own:
                cp.start()

        for j in range(N_CHIPS - 1):
            @pl.when((s == j + 1) & (i == 0))
            def _(j=j):
                for w in range(n):
                    over_ici(w, j, True).wait_recv()
                for w in range(n):
                    to_sibling(w, j, False).start()
                if j == 0:
                    for w in range(n):
                        over_ici(w, 2, False).start()
                    own[0].wait()
                for w in range(n):
                    to_sibling(w, j, True).wait_recv()
                load = pltpu.make_async_copy(outs[0].at[_chip_index(*chips[j])], wbuf, local.at[n])
                load.start()
                load.wait()

        xhat, _ = _rms(x_ref[...])
        nrm = (xhat * g_ref[...]).astype(BF16)
        z_ref[...] = _dot(nrm, wbuf[...])

        @pl.when(s == 0)
        def _():
            n_ref[...] = nrm

        @pl.when((s == N_CHIPS - 1) & (i == n_tiles - 1))
        def _():
            for j in range(N_CHIPS - 1):
                for w in range(n):
                    over_ici(w, j, False).wait_send()
                    to_sibling(w, j, False).wait_send()
            for cp in own[1:]:
                cp.wait()

    (z, n1, *stacked), job_outs = _fused_call(
        body, jobs, name="fwd_in", grid=(N_CHIPS, n_tiles), n_prefetch=1,
        in_specs=[pl.BlockSpec((MM_TILE, D_MODEL), lambda s, i, order: (i, 0)), _const((1, D_MODEL))] + [ANY] * n,
        out_specs=[pl.BlockSpec((MM_TILE, IN_SHARD), lambda s, i, order: (i, order[s])),
                   pl.BlockSpec((MM_TILE, D_MODEL), lambda s, i, order: (jnp.where(s == 0, i, n_tiles - 1), 0))]
        + [ANY] * n,
        out_shape=[jax.ShapeDtypeStruct((t, D_IN), F32), jax.ShapeDtypeStruct((t, D_MODEL), BF16)]
        + [jax.ShapeDtypeStruct((N_CHIPS,) + s.shape, s.dtype) for s in shards],
        scratch_shapes=[pltpu.VMEM(shards[0].shape, BF16), pltpu.SemaphoreType.DMA((6 * n,)),
                        pltpu.SemaphoreType.DMA((6 * n,)), pltpu.SemaphoreType.DMA((n + 1,))],
        compiler_params=_params(2), jobs_start_after=(1, 0),
    )(order, x, g1, *shards)
    return (z, n1, stacked), job_outs


def _fwd_lru(z, conv_w, conv_b, wr, br, wi, bi, lam, jobs=()):
    t = z.shape[0]

    def body(xa_ref, ga_ref, cw_ref, cb_ref, wr_ref, br_ref, wi_ref, bi_ref, lam_ref, ya_ref, h_ref, xc_ref, r_ref,
             ig_ref, tail_ref, carry_ref):
        @pl.when(pl.program_id(0) == 0)
        def _():
            tail_ref[...] = jnp.zeros_like(tail_ref)
            carry_ref[...] = jnp.zeros_like(carry_ref)

        xa = xa_ref[...]
        xc, r, ig, a, mult = _lru_gates(xa, tail_ref[...], cw_ref, cb_ref, wr_ref, br_ref, wi_ref, bi_ref, lam_ref)
        tail_ref[...] = xa[SEQ_TILE - SUBLANES:]
        xc_ref[...], r_ref[...], ig_ref[...] = xc, r, ig
        h, carry = _scan_forward(a, xc * ig * mult, carry_ref[...])
        carry_ref[...] = carry
        h_ref[...] = h
        ya_ref[...] = (h * _gelu(ga_ref[...])).astype(BF16)

    tile = lambda j: pl.BlockSpec((SEQ_TILE, D_MODEL), lambda i: (i, j))
    return _fused_call(
        body, jobs, name="fwd_lru", grid=(t // SEQ_TILE,),
        in_specs=[tile(0), tile(1), _const((CONV_WIDTH, D_MODEL)), _const((1, D_MODEL)),
                  _resident((HEADS, HEAD_DIM, HEAD_DIM)), _const((1, D_MODEL)),
                  _resident((HEADS, HEAD_DIM, HEAD_DIM)), _const((1, D_MODEL)), _const((1, D_MODEL))],
        out_specs=[tile(0)] * 5,
        out_shape=[jax.ShapeDtypeStruct((t, D_MODEL), BF16)] + [jax.ShapeDtypeStruct((t, D_MODEL), F32)] * 4,
        scratch_shapes=[pltpu.VMEM((SUBLANES, D_MODEL), F32), pltpu.VMEM((1, D_MODEL), F32)],
        compiler_params=_params(),
    )(z, z, conv_w, conv_b, wr, br, wi, bi, lam)


def _sgu_forward_parts(ub, vb, lg_ref, lb_ref):
    u, du = _gelu_and_grad(ub)
    vg, dvg = _gelu_and_grad(vb)
    mu = jnp.mean(vg, axis=-1, keepdims=True)
    d = vg - mu
    rstd = lax.rsqrt(jnp.mean(d * d, axis=-1, keepdims=True) + LN_EPS)
    vhat = d * rstd
    vn = (vhat * lg_ref[...] + lb_ref[...]).astype(BF16)
    return u, du, dvg, rstd, vhat, vn


def _causal_mask():
    rows = lax.broadcasted_iota(jnp.int32, (CHUNK, CHUNK), 0)
    cols = lax.broadcasted_iota(jnp.int32, (CHUNK, CHUNK), 1)
    return rows >= cols


def _fwd_sgu(z, ln_g, ln_b, w_s, bias_full, jobs=()):
    t = z.shape[0]

    def body(ub_ref, vb_ref, lg_ref, lb_ref, ws_ref, bias_ref, yb_ref):
        u, _, _, _, _, vn = _sgu_forward_parts(ub_ref[...], vb_ref[...], lg_ref, lb_ref)
        mask = _causal_mask()
        wm = [jnp.where(mask, ws_ref[g], 0.0).astype(BF16) for g in range(GROUPS)]
        for c in range(SEQ_TILE // CHUNK):
            rows = slice(c * CHUNK, (c + 1) * CHUNK)
            for g in range(GROUPS):
                cols = slice(g * GROUP_DIM, (g + 1) * GROUP_DIM)
                sp = _dot(wm[g], vn[rows, cols]) + bias_ref[:, cols]
                yb_ref[rows, cols] = (u[rows, cols] * sp).astype(BF16)

    tile = lambda j: pl.BlockSpec((SEQ_TILE, D_MODEL), lambda i: (i, j))
    return _fused_call(
        body, jobs, name="fwd_sgu", grid=(t // SEQ_TILE,),
        in_specs=[tile(2), tile(3), _const((1, D_MODEL)), _const((1, D_MODEL)),
                  _const((GROUPS, CHUNK, CHUNK)), _const((CHUNK, D_MODEL))],
        out_specs=tile(0),
        out_shape=jax.ShapeDtypeStruct((t, D_MODEL), BF16),
        compiler_params=_params(),
    )(z, z, ln_g, ln_b, w_s, bias_full)


def _fwd_merge(ya, yb, z, x, w_oa, w_ob, w_out, g2, jobs=()):
    t = x.shape[0]

    def body(ya_ref, yb_ref, m_ref, x_ref, woa_ref, wob_ref, wout_ref, g_ref, pa_ref, pb_ref, h1_ref, n2_ref):
        pa = _dot(ya_ref[...], woa_ref[...])
        pb = _dot(yb_ref[...], wob_ref[...])
        pa_ref[...] = pa
        pb_ref[...] = pb
        merged = jax.nn.sigmoid(m_ref[:, :D_MODEL]) * pa + jax.nn.sigmoid(m_ref[:, D_MODEL:]) * pb
        h1 = x_ref[...] + _dot(merged.astype(BF16), wout_ref[...])
        h1_ref[...] = h1
        xhat, _ = _rms(h1)
        n2_ref[...] = (xhat * g_ref[...]).astype(BF16)

    tile = pl.BlockSpec((MM_TILE, D_MODEL), lambda i: (i, 0))
    sq = _resident((D_MODEL, D_MODEL))
    return _fused_call(
        body, jobs, name="fwd_merge", grid=(t // MM_TILE,),
        in_specs=[tile, tile, pl.BlockSpec((MM_TILE, 2 * D_MODEL), lambda i: (i, 2)), tile, sq, sq, sq,
                  _const((1, D_MODEL))],
        out_specs=[tile, tile, tile, tile],
        out_shape=[jax.ShapeDtypeStruct((t, D_MODEL), F32)] * 3 + [jax.ShapeDtypeStruct((t, D_MODEL), BF16)],
        compiler_params=_params(),
    )(ya, yb, z, x, w_oa, w_ob, w_out, g2)


def _fwd_mlp(n2, h1, target, w_up_st, w_down, g3, jobs=()):
    t = n2.shape[0]

    def body(n2_ref, h1_ref, tgt_ref, wup_ref, wdown_ref, g_ref, up_ref, act_ref, dh2_ref, dh2b_ref, loss_ref,
             dg3_ref):
        @pl.when(pl.program_id(0) == 0)
        def _():
            loss_ref[...] = jnp.zeros_like(loss_ref)
            dg3_ref[...] = jnp.zeros_like(dg3_ref)

        n2 = n2_ref[...]
        h2 = h1_ref[...]
        for k in range(N_CHIPS):
            cols = slice(k * D_MODEL, (k + 1) * D_MODEL)
            up = _dot(n2, wup_ref[k])
            up_ref[:, cols] = up.astype(BF16)
            r = jnp.maximum(up, 0.0)
            act = (r * r).astype(BF16)
            act_ref[:, cols] = act
            h2 = h2 + _dot(act, wdown_ref[cols, :])
        xhat, r3 = _rms(h2)
        diff = xhat * g_ref[...] - tgt_ref[...]
        sq = jnp.sum(diff * diff, axis=1, keepdims=True)
        loss_ref[...] = loss_ref[...] + (0.5 / D_MODEL) * jnp.sum(sq, axis=0, keepdims=True)
        dy = diff * (1.0 / D_MODEL)
        dg3_ref[...] = dg3_ref[...] + _col_sum(dy * xhat)
        dh2 = _rms_bwd(dy * g_ref[...], xhat, r3)
        dh2_ref[...] = dh2
        dh2b_ref[...] = dh2.astype(BF16)

    tile = pl.BlockSpec((MM_TILE, D_MODEL), lambda i: (i, 0))
    wide = pl.BlockSpec((MM_TILE, D_FF), lambda i: (i, 0))
    return _fused_call(
        body, jobs, name="fwd_mlp", grid=(t // MM_TILE,),
        in_specs=[tile, tile, tile, _resident((N_CHIPS, D_MODEL, D_MODEL)), _resident((D_FF, D_MODEL)),
                  _const((1, D_MODEL))],
        out_specs=[wide, wide, tile, tile, _const((SUBLANES, 128)), _const((1, D_MODEL))],
        out_shape=[jax.ShapeDtypeStruct((t, D_FF), BF16), jax.ShapeDtypeStruct((t, D_FF), BF16),
                   jax.ShapeDtypeStruct((t, D_MODEL), F32), jax.ShapeDtypeStruct((t, D_MODEL), BF16),
                   jax.ShapeDtypeStruct((SUBLANES, 128), F32), jax.ShapeDtypeStruct((1, D_MODEL), F32)],
        compiler_params=_params(),
    )(n2, h1, target, w_up_st, w_down, g3)


def _bwd_mlp(dh2, dh2b, up, h1, w_up_st, w_down, g2, jobs=()):
    t = dh2.shape[0]

    def body(dh2_ref, dh2b_ref, up_ref, h1_ref, wup_ref, wdown_ref, g_ref, dup_ref, dh1_ref, dg2_ref):
        @pl.when(pl.program_id(0) == 0)
        def _():
            dg2_ref[...] = jnp.zeros_like(dg2_ref)

        dh2b = dh2b_ref[...]
        dn2 = jnp.zeros((MM_TILE, D_MODEL), F32)
        for k in range(N_CHIPS):
            cols = slice(k * D_MODEL, (k + 1) * D_MODEL)
            dact = _dot_nt(dh2b, wdown_ref[cols, :])
            dup = (dact * (2.0 * jnp.maximum(up_ref[:, cols].astype(F32), 0.0))).astype(BF16)
            dup_ref[:, cols] = dup
            dn2 = dn2 + _dot_nt(dup, wup_ref[k])
        xhat, r2 = _rms(h1_ref[...])
        dg2_ref[...] = dg2_ref[...] + _col_sum(dn2 * xhat)
        dh1_ref[...] = dh2_ref[...] + _rms_bwd(dn2 * g_ref[...], xhat, r2)

    tile = pl.BlockSpec((MM_TILE, D_MODEL), lambda i: (i, 0))
    wide = pl.BlockSpec((MM_TILE, D_FF), lambda i: (i, 0))
    return _fused_call(
        body, jobs, name="bwd_mlp", grid=(t // MM_TILE,),
        in_specs=[tile, tile, wide, tile, _resident((N_CHIPS, D_MODEL, D_MODEL)), _resident((D_FF, D_MODEL)),
                  _const((1, D_MODEL))],
        out_specs=[wide, tile, _const((1, D_MODEL))],
        out_shape=[jax.ShapeDtypeStruct((t, D_FF), BF16), jax.ShapeDtypeStruct((t, D_MODEL), F32),
                   jax.ShapeDtypeStruct((1, D_MODEL), F32)],
        compiler_params=_params(),
    )(dh2, dh2b, up, h1, w_up_st, w_down, g2)


def _bwd_merge(dh1, pa, pb, z, w_oa, w_ob, w_out, jobs=()):
    t = dh1.shape[0]

    def body(dh1_ref, pa_ref, pb_ref, m_ref, woa_ref, wob_ref, wout_ref, dz_ref, dya_ref, dyb_ref, mg_ref,
             dpa_ref, dpb_ref, dh1b_ref):
        dh1b = dh1_ref[...].astype(BF16)
        dh1b_ref[...] = dh1b
        dm = _dot_nt(dh1b, wout_ref[...])
        pa = pa_ref[...]
        pb = pb_ref[...]
        sa = jax.nn.sigmoid(m_ref[:, :D_MODEL])
        sb = jax.nn.sigmoid(m_ref[:, D_MODEL:])
        mg_ref[...] = (sa * pa + sb * pb).astype(BF16)
        dz_ref[:, :D_MODEL] = (dm * pa * sa * (1.0 - sa)).astype(BF16)
        dz_ref[:, D_MODEL:] = (dm * pb * sb * (1.0 - sb)).astype(BF16)
        dpa = (dm * sa).astype(BF16)
        dpb = (dm * sb).astype(BF16)
        dpa_ref[...] = dpa
        dpb_ref[...] = dpb
        dya_ref[...] = _dot_nt(dpa, woa_ref[...])
        dyb_ref[...] = _dot_nt(dpb, wob_ref[...])

    tile = pl.BlockSpec((MM_TILE, D_MODEL), lambda i: (i, 0))
    pair = pl.BlockSpec((MM_TILE, 2 * D_MODEL), lambda i: (i, 2))
    sq = _resident((D_MODEL, D_MODEL))
    act_bf = jax.ShapeDtypeStruct((t, D_MODEL), BF16)
    return _fused_call(
        body, jobs, name="bwd_merge", grid=(t // MM_TILE,),
        in_specs=[tile, tile, tile, pair, sq, sq, sq],
        out_specs=[pair, tile, tile, tile, tile, tile, tile],
        out_shape=[jax.ShapeDtypeStruct((t, D_IN), BF16), jax.ShapeDtypeStruct((t, D_MODEL), F32),
                   jax.ShapeDtypeStruct((t, D_MODEL), F32), act_bf, act_bf, act_bf, act_bf],
        compiler_params=_params(),
    )(dh1, pa, pb, z, w_oa, w_ob, w_out)


def _bwd_sgu(dz, dyb, z, ln_g, ln_b, w_s, bias_full, jobs=()):
    t = dyb.shape[0]
    n_tiles = t // SEQ_TILE

    def body(dz_any, dyb_ref, ub_ref, vb_ref, lg_ref, lb_ref, ws_ref, bias_ref, dz_ref, dlg_ref, dlb_ref, dws_ref,
             dbs_ref, dvn_ref, dsp_acc):
        del dz_any
        i = pl.program_id(0)

        @pl.when(i == 0)
        def _():
            dlg_ref[...] = jnp.zeros_like(dlg_ref)
            dlb_ref[...] = jnp.zeros_like(dlb_ref)
            dws_ref[...] = jnp.zeros_like(dws_ref)
            dsp_acc[...] = jnp.zeros_like(dsp_acc)

        u, du, dvg, rstd, vhat, vn = _sgu_forward_parts(ub_ref[...], vb_ref[...], lg_ref, lb_ref)
        dyb = dyb_ref[...]
        mask = _causal_mask()
        wm = [jnp.where(mask, ws_ref[g], 0.0).astype(BF16) for g in range(GROUPS)]
        for c in range(SEQ_TILE // CHUNK):
            rows = slice(c * CHUNK, (c + 1) * CHUNK)
            for g in range(GROUPS):
                cols = slice(g * GROUP_DIM, (g + 1) * GROUP_DIM)
                vn_blk = vn[rows, cols]
                sp = _dot(wm[g], vn_blk) + bias_ref[:, cols]
                dyb_blk = dyb[rows, cols]
                dz_ref[rows, cols] = (dyb_blk * sp * du[rows, cols]).astype(BF16)
                dsp = dyb_blk * u[rows, cols]
                dsp_acc[:, cols] = dsp_acc[:, cols] + dsp
                dspb = dsp.astype(BF16)
                dvn_ref[rows, cols] = _dot_tn(wm[g], dspb)
                wcols = slice(g * CHUNK, (g + 1) * CHUNK)
                dws_ref[:, wcols] = dws_ref[:, wcols] + jnp.where(mask, _dot_nt(dspb, vn_blk), 0.0)
        dvn = dvn_ref[...]
        dlg_ref[...] = dlg_ref[...] + _col_sum(dvn * vhat)
        dlb_ref[...] = dlb_ref[...] + _col_sum(dvn)
        dvhat = dvn * lg_ref[...]
        dvgel = rstd * (dvhat - jnp.mean(dvhat, axis=-1, keepdims=True)
                        - vhat * jnp.mean(dvhat * vhat, axis=-1, keepdims=True))
        dz_ref[:, D_MODEL:] = (dvgel * dvg).astype(BF16)

        @pl.when(i == n_tiles - 1)
        def _():
            lane = lax.broadcasted_iota(jnp.int32, (CHUNK, 128), 1)
            out = jnp.zeros((CHUNK, 128), F32)
            for g in range(GROUPS):
                s = jnp.sum(dsp_acc[:, g * GROUP_DIM:(g + 1) * GROUP_DIM], axis=1, keepdims=True)
                out = out + jnp.where(lane == g, s, 0.0)
            dbs_ref[...] = out

    tile = lambda j: pl.BlockSpec((SEQ_TILE, D_MODEL), lambda i: (i, j))
    return _fused_call(
        body, jobs, name="bwd_sgu", grid=(n_tiles,),
        in_specs=[pl.BlockSpec(memory_space=pl.ANY), tile(0), tile(2), tile(3), _const((1, D_MODEL)),
                  _const((1, D_MODEL)), _const((GROUPS, CHUNK, CHUNK)), _const((CHUNK, D_MODEL))],
        out_specs=[pl.BlockSpec((SEQ_TILE, 2 * D_MODEL), lambda i: (i, 1)), _const((1, D_MODEL)),
                   _const((1, D_MODEL)), _const((CHUNK, GROUPS * CHUNK)), _const((CHUNK, 128))],
        out_shape=[jax.ShapeDtypeStruct((t, D_IN), BF16), jax.ShapeDtypeStruct((1, D_MODEL), F32),
                   jax.ShapeDtypeStruct((1, D_MODEL), F32), jax.ShapeDtypeStruct((CHUNK, GROUPS * CHUNK), F32),
                   jax.ShapeDtypeStruct((CHUNK, 128), F32)],
        scratch_shapes=[pltpu.VMEM((SEQ_TILE, D_MODEL), F32), pltpu.VMEM((CHUNK, D_MODEL), F32)],
        input_output_aliases={0: 0},
        compiler_params=_params(),
    )(dz, dyb, z, z, ln_g, ln_b, w_s, bias_full)


def _bwd_lru(dz, dya, z, h, xc, r, ig, conv_w, wr, wi, lam, jobs=()):
    t = dya.shape[0]
    n_tiles = t // SEQ_TILE
    per_tile = SEQ_TILE // SUBLANES

    def body(dz_any, dya_ref, xa_ref, ga_ref, h_ref, h_prev_ref, xc_ref, r_ref, ig_ref, cw_ref, wr_ref, wi_ref, lam_ref,
             dz_ref, dcw_ref, dcb_ref, dwr_ref, dbr_ref, dwi_ref, dbi_ref, dlam_ref, lam_carry, dxc_head):
        del dz_any
        i = pl.program_id(0)

        @pl.when(i == 0)
        def _():
            for ref in (dcw_ref, dcb_ref, dwr_ref, dbr_ref, dwi_ref, dbi_ref, dlam_ref, lam_carry, dxc_head):
                ref[...] = jnp.zeros_like(ref)

        first_tile = i == n_tiles - 1
        h_tail = jnp.where(first_tile, 0.0, h_prev_ref[...])
        xc, r, ig = xc_ref[...], r_ref[...], ig_ref[...]
        xcb = xc.astype(BF16)
        sp, a, mult = _decay(r, lam_ref)
        h = h_ref[...]
        h_prev = _shift_down(h, h_tail, 1)
        dya = dya_ref[...]
        gg, dgg = _gelu_and_grad(ga_ref[...])
        dz_ref[:, D_MODEL:] = (dya * h * dgg).astype(BF16)
        ones = jnp.ones((SUBLANES, D_MODEL), F32)
        lam_t, lam_first = _scan_backward(_shift_up(a, ones, 1), dya * gg, lam_carry[...])
        lam_carry[...] = a[0:1] * lam_first
        dmult = lam_t * xc * ig
        dla = lam_t * h_prev * a - dmult * (a * a) / mult
        dr = dla * ((-LRU_C) * sp)
        dlam_ref[...] = dlam_ref[...] + _col_sum(dla * r) * (LRU_C * jax.nn.sigmoid(-lam_ref[...]))
        dpr = dr * r * (1.0 - r)
        dpi = lam_t * xc * mult * ig * (1.0 - ig)
        dbr_ref[...] = dbr_ref[...] + _col_sum(dpr)
        dbi_ref[...] = dbi_ref[...] + _col_sum(dpi)
        dprb = dpr.astype(BF16)
        dpib = dpi.astype(BF16)
        dxc_gate = []
        for hd in range(HEADS):
            cols = slice(hd * HEAD_DIM, (hd + 1) * HEAD_DIM)
            dxc_gate.append(_dot_nt(dprb[:, cols], wr_ref[hd]) + _dot_nt(dpib[:, cols], wi_ref[hd]))
            dwr_ref[hd] = dwr_ref[hd] + _dot_tn(xcb[:, cols], dprb[:, cols])
            dwi_ref[hd] = dwi_ref[hd] + _dot_tn(xcb[:, cols], dpib[:, cols])
        dxc = lam_t * ig * mult + jnp.concatenate(dxc_gate, axis=1)
        dcb_ref[...] = dcb_ref[...] + _col_sum(dxc)
        cw = cw_ref[...]
        head = dxc_head[...]
        xa = xa_ref[...]
        dxa = cw[0:1] * dxc
        dcw_ref[0:1, :] = dcw_ref[0:1, :] + _col_sum(dxc * xa)
        for k in range(1, CONV_WIDTH):
            dxc_k = _shift_up(dxc, head, k)
            dxa = dxa + cw[k:k + 1] * dxc_k
            dcw_ref[k:k + 1, :] = dcw_ref[k:k + 1, :] + _col_sum(dxc_k * xa)
        dxc_head[...] = dxc[0:SUBLANES]
        dz_ref[:, :D_MODEL] = dxa.astype(BF16)

    rev = lambda i: n_tiles - 1 - i
    tile = lambda j: pl.BlockSpec((SEQ_TILE, D_MODEL), lambda i: (rev(i), j))
    prev8 = pl.BlockSpec((SUBLANES, D_MODEL), lambda i: (jnp.maximum(rev(i) * per_tile - 1, 0), 0))
    vec = _const((1, D_MODEL))
    gate_w = _resident((HEADS, HEAD_DIM, HEAD_DIM))
    vec_shape = jax.ShapeDtypeStruct((1, D_MODEL), F32)
    gate_shape = jax.ShapeDtypeStruct((HEADS, HEAD_DIM, HEAD_DIM), F32)
    return _fused_call(
        body, jobs, name="bwd_lru", grid=(n_tiles,),
        in_specs=[pl.BlockSpec(memory_space=pl.ANY), tile(0), tile(0), tile(1), tile(0), prev8] + [tile(0)] * 3
                 + [_const((CONV_WIDTH, D_MODEL)), gate_w, gate_w, vec],
        out_specs=[pl.BlockSpec((SEQ_TILE, 2 * D_MODEL), lambda i: (rev(i), 0)), _const((SUBLANES, D_MODEL)), vec,
                   _const((HEADS, HEAD_DIM, HEAD_DIM)), vec, _const((HEADS, HEAD_DIM, HEAD_DIM)), vec, vec],
        out_shape=[jax.ShapeDtypeStruct((t, D_IN), BF16), jax.ShapeDtypeStruct((SUBLANES, D_MODEL), F32), vec_shape,
                   gate_shape, vec_shape, gate_shape, vec_shape, vec_shape],
        scratch_shapes=[pltpu.VMEM((1, D_MODEL), F32), pltpu.VMEM((SUBLANES, D_MODEL), F32)],
        input_output_aliases={0: 0},
        compiler_params=_params(),
    )(dz, dya, z, z, h, h, xc, r, ig, conv_w, wr, wi, lam)


def _bwd_in(dz, x, dh1, w_in_st, g1, jobs=()):
    t = x.shape[0]

    def body(dz_ref, x_ref, dh1_ref, w_ref, g_ref, dx_ref, dg1_ref):
        @pl.when(pl.program_id(0) == 0)
        def _():
            dg1_ref[...] = jnp.zeros_like(dg1_ref)

        dn1 = jnp.zeros((MM_TILE, D_MODEL), F32)
        for k in range(N_CHIPS):
            dn1 = dn1 + _dot_nt(dz_ref[:, k * IN_SHARD:(k + 1) * IN_SHARD], w_ref[k])
        xhat, r1 = _rms(x_ref[...])
        dg1_ref[...] = dg1_ref[...] + _col_sum(dn1 * xhat)
        dx_ref[...] = dh1_ref[...] + _rms_bwd(dn1 * g_ref[...], xhat, r1)

    tile = pl.BlockSpec((MM_TILE, D_MODEL), lambda i: (i, 0))
    return _fused_call(
        body, jobs, name="bwd_in", grid=(t // MM_TILE,),
        in_specs=[pl.BlockSpec((MM_TILE, D_IN), lambda i: (i, 0)), tile, tile,
                  _resident((N_CHIPS, D_MODEL, IN_SHARD)), _const((1, D_MODEL))],
        out_specs=[tile, _const((1, D_MODEL))],
        out_shape=[jax.ShapeDtypeStruct((t, D_MODEL), F32), jax.ShapeDtypeStruct((1, D_MODEL), F32)],
        compiler_params=_params(),
    )(dz, x, dh1, w_in_st, g1)


def _weight_grad(name, a, b, n_blocks, a_varies, b_varies, width, jobs=()):
    t = a.shape[0]
    rows = min(DW_TILE, t)
    n_t = t // rows

    def body(a_ref, b_ref, o_ref, acc_ref):
        s = pl.program_id(1)
        part = _dot_tn(a_ref[...], b_ref[...])

        @pl.when(s == 0)
        def _():
            acc_ref[...] = part

        @pl.when(s > 0)
        def _():
            acc_ref[...] = acc_ref[...] + part

        @pl.when(s == n_t - 1)
        def _():
            o_ref[...] = acc_ref[...].astype(BF16)

    return _fused_call(
        body, jobs, name=name, grid=(n_blocks, n_t),
        in_specs=[pl.BlockSpec((rows, D_MODEL), (lambda j, s: (s, j)) if a_varies else (lambda j, s: (s, 0))),
                  pl.BlockSpec((rows, width), (lambda j, s: (s, j)) if b_varies else (lambda j, s: (s, 0)))],
        out_specs=pl.BlockSpec((None, D_MODEL, width), lambda j, s: (j, 0, 0)),
        out_shape=jax.ShapeDtypeStruct((n_blocks, D_MODEL, width), BF16),
        scratch_shapes=[pltpu.VMEM((D_MODEL, width), F32)],
        compiler_params=_params(2),
    )(a, b)


def _place():
    x, y, c = lax.axis_index("x"), lax.axis_index("y"), lax.axis_index("c")
    other_chips = [(1 - x, y), (x, 1 - y), (1 - x, 1 - y)]
    return x, y, c, other_chips


def _chip_index(px, py):
    return 2 * px + py


ANY = pl.BlockSpec(memory_space=pl.ANY)


def _comm_call(name, jobs):
    return _fused_call(None, jobs, name=name, grid=(), in_specs=[], out_specs=[], out_shape=[])()[1]


def _gather_ici_job(shards):
    n = len(shards)
    halves = [s.shape[0] // 2 for s in shards]

    def copies(ins, outs, send, recv, local):
        x, y, c, chips = _place()
        me, sibling = (x, y, c), (x, y, 1 - c)

        def block(w, place):
            return outs[w].at[_chip_index(place[0], place[1]), pl.ds(place[2] * halves[w], halves[w]), :]

        def copy(w, k, blk, to, src=None):
            return pltpu.make_async_remote_copy(
                src_ref=block(w, blk) if src is None else src, dst_ref=block(w, blk),
                send_sem=send.at[4 * w + k], recv_sem=recv.at[4 * w + k], device_id=to, device_id_type=MESH)

        sends, arrivals, own = [], [], []
        for w in range(n):
            src = ins[w].at[pl.ds(c * halves[w], halves[w]), :]
            own.append(pltpu.make_async_copy(src, block(w, me), local.at[w]))
            sends.append(copy(w, 0, me, sibling, src))
            arrivals.append(copy(w, 0, sibling, me))
            for j, chip in enumerate(chips):
                sends.append(copy(w, 1 + j, me, (*chip, c), src))
                arrivals.append(copy(w, 1 + j, (*chip, c), me))
        return sends, arrivals, own

    return _Job(shards, [jax.ShapeDtypeStruct((N_CHIPS,) + s.shape, s.dtype) for s in shards], 4 * n, copies,
                n_local=n)


def _gather_pass_job(stacked):
    n = len(stacked)
    halves = [s.shape[1] // 2 for s in stacked]

    def copies(ins, outs, send, recv, local):
        del ins, local
        x, y, c, chips = _place()

        def copy(w, j, chip, pc, to):
            blk = outs[w].at[_chip_index(*chip), pl.ds(pc * halves[w], halves[w]), :]
            return pltpu.make_async_remote_copy(
                src_ref=blk, dst_ref=blk, send_sem=send.at[3 * w + j], recv_sem=recv.at[3 * w + j], device_id=to,
                device_id_type=MESH)

        sends = [copy(w, j, chip, c, (x, y, 1 - c)) for w in range(n) for j, chip in enumerate(chips)]
        arrivals = [copy(w, j, chip, 1 - c, (x, y, c)) for w in range(n) for j, chip in enumerate(chips)]
        return sends, arrivals, []

    return _Job(stacked, [jax.ShapeDtypeStruct(s.shape, s.dtype) for s in stacked], 3 * n, copies,
                aliases={w: w for w in range(n)})


def _gather_small_job(block):
    def copies(ins, outs, send, recv, local):
        x, y, c, chips = _place()

        def copy(j, chip_from, to):
            return pltpu.make_async_remote_copy(
                src_ref=ins[0], dst_ref=outs[0].at[_chip_index(*chip_from)], send_sem=send.at[j],
                recv_sem=recv.at[j], device_id=to, device_id_type=MESH)

        own = [pltpu.make_async_copy(ins[0], outs[0].at[_chip_index(x, y)], local.at[0])]
        sends = [copy(j, (x, y), (*chip, c)) for j, chip in enumerate(chips)]
        arrivals = [copy(j, chip, (x, y, c)) for j, chip in enumerate(chips)]
        return sends, arrivals, own

    return _Job([block], [jax.ShapeDtypeStruct((N_CHIPS,) + block.shape, block.dtype)], 3, copies, n_local=1)


def _pair_send_job(grads):
    n = len(grads)
    halves = [g.shape[1] // 2 for g in grads]

    def copies(ins, outs, send, recv, local):
        del local
        x, y, c, _ = _place()
        sends = [pltpu.make_async_remote_copy(
            src_ref=ins[w].at[:, pl.ds((1 - c) * halves[w], halves[w]), :], dst_ref=outs[w], send_sem=send.at[w],
            recv_sem=recv.at[w], device_id=(x, y, 1 - c), device_id_type=MESH) for w in range(n)]
        return sends, sends, []

    return _Job(grads, [jax.ShapeDtypeStruct((N_CHIPS, h, g.shape[2]), g.dtype) for g, h in zip(grads, halves)], n,
                copies)


def _row_block(rows, limit=256):
    return min(rows, limit)


def _pair_add(name, core, mine, theirs):
    _, _, h, cols = mine.shape
    rb = _row_block(h, 512)

    def body(core_ref, a_ref, b_ref, o_ref):
        del core_ref
        o_ref[...] = (a_ref[...].astype(F32) + b_ref[...].astype(F32)).astype(BF16)

    return pl.pallas_call(
        body, name=name,
        grid_spec=pltpu.PrefetchScalarGridSpec(
            num_scalar_prefetch=1, grid=(N_CHIPS, h // rb),
            in_specs=[pl.BlockSpec((None, None, rb, cols), lambda k, r, core_ref: (k, core_ref[0], r, 0)),
                      pl.BlockSpec((None, rb, cols), lambda k, r, core_ref: (k, r, 0))],
            out_specs=pl.BlockSpec((None, rb, cols), lambda k, r, core_ref: (k, r, 0))),
        out_shape=jax.ShapeDtypeStruct(theirs.shape, BF16),
        compiler_params=_params(2),
    )(core, mine, theirs)


def _chip_exchange_job(sums):
    n = len(sums)

    def copies(ins, outs, send, recv, local):
        del local
        _, _, c, chips = _place()
        sends = [pltpu.make_async_remote_copy(
            src_ref=ins[w].at[_chip_index(*chip)], dst_ref=outs[w].at[j], send_sem=send.at[3 * w + j],
            recv_sem=recv.at[3 * w + j], device_id=(*chip, c), device_id_type=MESH)
            for w in range(n) for j, chip in enumerate(chips)]
        return sends, sends, []

    return _Job(sums, [jax.ShapeDtypeStruct((N_CHIPS - 1,) + s.shape[1:], s.dtype) for s in sums], 3 * n, copies)


def _chip_sum(name, place, mine, theirs):
    _, h, cols = mine.shape
    rb = _row_block(h, 512)

    def body(place_ref, p_ref, q_ref, o_ref):
        del place_ref
        acc = p_ref[...].astype(F32)
        for j in range(N_CHIPS - 1):
            acc = acc + q_ref[j].astype(F32)
        o_ref[...] = acc

    return pl.pallas_call(
        body, name=name,
        grid_spec=pltpu.PrefetchScalarGridSpec(
            num_scalar_prefetch=1, grid=(h // rb,),
            in_specs=[pl.BlockSpec((None, rb, cols), lambda r, place_ref: (place_ref[0], r, 0)),
                      pl.BlockSpec((N_CHIPS - 1, rb, cols), lambda r, place_ref: (0, r, 0))],
            out_specs=pl.BlockSpec((None, rb, cols), lambda r, place_ref: (place_ref[1], r, 0))),
        out_shape=jax.ShapeDtypeStruct((2, h, cols), F32),
        compiler_params=_params(),
    )(place, mine, theirs)


def _share_job(bufs):
    n = len(bufs)

    def copies(ins, outs, send, recv, local):
        del ins, local
        x, y, c, _ = _place()

        def copy(w, half):
            return pltpu.make_async_remote_copy(
                src_ref=outs[w].at[half], dst_ref=outs[w].at[half], send_sem=send.at[w], recv_sem=recv.at[w],
                device_id=(x, y, 1 - c), device_id_type=MESH)

        return [copy(w, c) for w in range(n)], [copy(w, 1 - c) for w in range(n)], []

    return _Job(bufs, [jax.ShapeDtypeStruct(b.shape, b.dtype) for b in bufs], n, copies,
                aliases={w: w for w in range(n)})


SMALL_ROWS = 24
ROW_G1, ROW_CW, ROW_CB, ROW_BR, ROW_BI, ROW_LAM, ROW_LG, ROW_LB, ROW_G2, ROW_G3, ROW_LOSS, ROW_BS = (
    0, 1, 5, 6, 7, 8, 9, 10, 11, 12, 13, 16)
N_DEV = 8


def _pack_small(dcw, dcb, dbr, dbi, dlam, dlg, dlb, dg2, dg3, loss, dbs):
    def body(dcw_ref, dcb_ref, dbr_ref, dbi_ref, dlam_ref, dlg_ref, dlb_ref, dg2_ref, dg3_ref, loss_ref, dbs_ref, out):
        out[...] = jnp.zeros((SMALL_ROWS, D_MODEL), F32)
        for row, ref in ((ROW_CB, dcb_ref), (ROW_BR, dbr_ref), (ROW_BI, dbi_ref), (ROW_LAM, dlam_ref),
                         (ROW_LG, dlg_ref), (ROW_LB, dlb_ref), (ROW_G2, dg2_ref), (ROW_G3, dg3_ref)):
            out[row:row + 1, :] = ref[...]
        out[ROW_CW:ROW_CW + CONV_WIDTH, :] = dcw_ref[0:CONV_WIDTH, :]
        out[ROW_LOSS:ROW_LOSS + 1, 0:128] = loss_ref[0:1, :]
        out[ROW_BS:ROW_BS + GROUPS, 0:128] = jnp.transpose(dbs_ref[...])[0:GROUPS, :]

    vm = pl.BlockSpec(memory_space=pltpu.VMEM)
    return pl.pallas_call(
        body, name="pack_small", in_specs=[vm] * 11, out_specs=vm,
        out_shape=jax.ShapeDtypeStruct((SMALL_ROWS, D_MODEL), F32),
    )(dcw, dcb, dbr, dbi, dlam, dlg, dlb, dg2, dg3, loss, dbs)


def _gather_all_job(blocks):
    n = len(blocks)
    flips = [(dx, dy, dc) for dx in (0, 1) for dy in (0, 1) for dc in (0, 1)][1:]

    def copies(ins, outs, send, recv, local):
        x, y, c, _ = _place()
        me = 4 * x + 2 * y + c
        sends, arrivals, own = [], [], []
        for w in range(n):
            own.append(pltpu.make_async_copy(ins[w], outs[w].at[me], local.at[w]))
            for k, (dx, dy, dc) in enumerate(flips):
                peer = (x ^ dx, y ^ dy, c ^ dc)
                sem = dict(send_sem=send.at[7 * w + k], recv_sem=recv.at[7 * w + k])
                sends.append(pltpu.make_async_remote_copy(
                    src_ref=ins[w], dst_ref=outs[w].at[me], device_id=peer, device_id_type=MESH, **sem))
                arrivals.append(pltpu.make_async_remote_copy(
                    src_ref=ins[w], dst_ref=outs[w].at[4 * peer[0] + 2 * peer[1] + peer[2]], device_id=peer,
                    device_id_type=MESH, **sem))
        return sends, arrivals, own

    return _Job(blocks, [jax.ShapeDtypeStruct((N_DEV,) + b.shape, b.dtype) for b in blocks], 7 * n, copies, n_local=n)


def _sum_small(vec_all, ws_all, dg1_all):
    def body(vec_ref, ws_ref, dg1_ref, vec_out, ws_out):
        vec, ws, dg1 = vec_ref[0], ws_ref[0], dg1_ref[0]
        for d in range(1, N_DEV):
            vec, ws, dg1 = vec + vec_ref[d], ws + ws_ref[d], dg1 + dg1_ref[d]
        vec_out[...] = vec
        vec_out[ROW_G1:ROW_G1 + 1, :] = dg1
        ws_out[...] = ws

    vm = pl.BlockSpec(memory_space=pltpu.VMEM)
    return pl.pallas_call(
        body, name="sum_small", in_specs=[vm] * 3, out_specs=[vm, vm],
        out_shape=[jax.ShapeDtypeStruct(vec_all.shape[1:], F32), jax.ShapeDtypeStruct(ws_all.shape[1:], F32)],
    )(vec_all, ws_all, dg1_all)


def _adamw_math(w, g, m, v):
    m = ADAM_B1 * m + (1.0 - ADAM_B1) * g
    v = ADAM_B2 * v + (1.0 - ADAM_B2) * (g * g)
    m_hat = m / (1.0 - ADAM_B1 ** ADAM_STEP)
    v_hat = v / (1.0 - ADAM_B2 ** ADAM_STEP)
    delta = (-ADAM_LR) * (m_hat / (jnp.sqrt(v_hat) + ADAM_EPS) + ADAM_WD * w)
    return delta, m, v


def _adamw(name, g, w, m, v, jobs=()):
    rows, cols = w.shape
    rb = _row_block(rows)

    def body(g_ref, w_ref, m_ref, v_ref, d_ref, nm_ref, nv_ref):
        d_ref[...], nm_ref[...], nv_ref[...] = _adamw_math(w_ref[...], g_ref[...], m_ref[...], v_ref[...])

    blk = pl.BlockSpec((rb, cols), lambda r: (r, 0))
    return _fused_call(
        body, jobs, name=name, grid=(rows // rb,), in_specs=[blk] * 4, out_specs=[blk] * 3,
        out_shape=[jax.ShapeDtypeStruct(w.shape, F32)] * 3, compiler_params=_params(),
    )(g, w, m, v)


def _adamw_small(grads, ws, ms, vs):
    n = len(grads)

    def body(*refs):
        g_refs, w_refs, m_refs, v_refs = refs[:n], refs[n:2 * n], refs[2 * n:3 * n], refs[3 * n:4 * n]
        outs = refs[4 * n:]
        for p in range(n):
            d, nm, nv = _adamw_math(w_refs[p][...], g_refs[p][...], m_refs[p][...], v_refs[p][...])
            outs[p][...] = d
            outs[n + p][...] = nm
            outs[2 * n + p][...] = nv

    vm = pl.BlockSpec(memory_space=pltpu.VMEM)
    shapes = [jax.ShapeDtypeStruct(w.shape, F32) for w in ws]
    out = pl.pallas_call(
        body, name="adamw_small", in_specs=[vm] * (4 * n), out_specs=[vm] * (3 * n), out_shape=shapes * 3,
    )(*grads, *ws, *ms, *vs)
    return out[:n], out[n:2 * n], out[2 * n:]


def _unstack_heads(w_st):
    per = HEAD_DIM // N_CHIPS
    return w_st.reshape(N_CHIPS, HEADS, per, HEAD_DIM).transpose(1, 0, 2, 3).reshape(HEADS, HEAD_DIM, HEAD_DIM)


def _stack_heads(w):
    per = HEAD_DIM // N_CHIPS
    return w.reshape(HEADS, N_CHIPS, per, HEAD_DIM).transpose(1, 0, 2, 3).reshape(N_CHIPS, HEADS * per, HEAD_DIM)


def kernel(x, norm_mix_g, w_in, conv_w, conv_b, w_rgate, b_rgate, w_igate, b_igate, lru_lambda, w_out_a, sgu_ln_g, sgu_ln_b, sgu_w_s, sgu_b_s, w_out_b, w_out, norm_mlp_g, w_up, w_down, norm_final_g, loss_target, m_norm_mix_g, m_w_in, m_conv_w, m_conv_b, m_w_rgate, m_b_rgate, m_w_igate, m_b_igate, m_lru_lambda, m_w_out_a, m_sgu_ln_g, m_sgu_ln_b, m_sgu_w_s, m_sgu_b_s, m_w_out_b, m_w_out, m_norm_mlp_g, m_w_up, m_w_down, m_norm_final_g, v_norm_mix_g, v_w_in, v_conv_w, v_conv_b, v_w_rgate, v_b_rgate, v_w_igate, v_b_igate, v_lru_lambda, v_w_out_a, v_sgu_ln_g, v_sgu_ln_b, v_sgu_w_s, v_sgu_b_s, v_w_out_b, v_w_out, v_norm_mlp_g, v_w_up, v_w_down, v_norm_final_g):
    chip = _chip_index(lax.axis_index("x"), lax.axis_index("y"))
    core = lax.axis_index("c")
    quarter_h = HEAD_DIM // N_CHIPS
    quarter_d = D_MODEL // N_CHIPS

    as_2d = lambda a: a.reshape(-1, a.shape[-1])
    big_w = [as_2d(w) for w in (w_in, w_rgate, w_igate, w_out_a, w_out_b, w_out, w_up, w_down)]
    big_m = [as_2d(w) for w in (m_w_in, m_w_rgate, m_w_igate, m_w_out_a, m_w_out_b, m_w_out, m_w_up, m_w_down)]
    big_v = [as_2d(w) for w in (v_w_in, v_w_rgate, v_w_igate, v_w_out_a, v_w_out_b, v_w_out, v_w_up, v_w_down)]

    packed = jnp.concatenate([conv_w[0], b_rgate[0], b_igate[0]], axis=1)
    packed = jnp.concatenate([packed, jnp.zeros_like(packed)], axis=0)
    s_in, s_r, s_i, s_oa, s_ob, s_out, s_up, s_down = [w.astype(BF16) for w in big_w]
    xs, target = x[0], loss_target[0]
    g3 = norm_final_g.reshape(1, D_MODEL)
    bias_s = jnp.broadcast_to(jnp.transpose(sgu_b_s[0])[:, :, None], (CHUNK, GROUPS, GROUP_DIM)).reshape(CHUNK, D_MODEL)
    core_arr = core.reshape(1).astype(jnp.int32)
    place = jnp.stack([chip, core]).astype(jnp.int32)
    quarter = lambda g: g.reshape(N_CHIPS, D_MODEL // N_CHIPS, D_MODEL)

    def pair_add(nm, g, from_sibling):
        return _pair_add("pair_add_" + nm, core_arr, g.reshape(N_CHIPS, 2, g.shape[1] // 2, g.shape[2]), from_sibling)

    def chip_sum(nm, pair, from_chips):
        return _chip_sum("chip_sum_" + nm, place, pair, from_chips)

    order = jnp.stack([chip, chip ^ 2, chip ^ 1, chip ^ 3]).astype(jnp.int32)
    (z, n1, (w_in_st, wr_st, wi_st)), ((packed_all,), ab) = _fwd_in(
        xs, norm_mix_g, [s_in, s_r, s_i], order, jobs=[_gather_small_job(packed), _gather_ici_job([s_oa, s_ob])])
    pick = lambda lo, hi: packed_all[:, :HEADS, lo:hi].transpose(1, 0, 2).reshape(HEADS, -1)
    conv_w_full = pick(0, quarter_d)
    br_full = pick(quarter_d, quarter_d + quarter_h).reshape(1, D_MODEL)
    bi_full = pick(quarter_d + quarter_h, quarter_d + 2 * quarter_h).reshape(1, D_MODEL)
    wr, wi = _unstack_heads(wr_st), _unstack_heads(wi_st)
    lru = (conv_w_full, conv_b, wr, br_full, wi, bi_full, lru_lambda)
    sgu = (sgu_ln_g, sgu_ln_b, sgu_w_s[0], bias_s)

    (ya, *saved), (ab, rest) = _fwd_lru(z, *lru, jobs=[_gather_pass_job(ab), _gather_ici_job([s_out, s_up])])
    yb, (rest, (down,)) = _fwd_sgu(z, *sgu, jobs=[_gather_pass_job(rest), _gather_ici_job([s_down])])
    w_oa, w_ob, w_o = [w.reshape(D_MODEL, D_MODEL) for w in ab + rest[:1]]
    w_up_st = rest[1]
    (pa, pb, h1, n2), ((down,),) = _fwd_merge(ya, yb, z, xs, w_oa, w_ob, w_o, norm_mlp_g,
                                              jobs=[_gather_pass_job([down])])
    w_dn = down.reshape(D_FF, D_MODEL)
    (up, act, dh2, dh2b, loss_part, dg3), _ = _fwd_mlp(n2, h1, target, w_up_st, w_dn, g3)

    (dup, dh1, dg2), _ = _bwd_mlp(dh2, dh2b, up, h1, w_up_st, w_dn, norm_mlp_g)
    d_up, _ = _weight_grad("dw_up", n2, dup, N_CHIPS, False, True, D_MODEL)
    d_down, ((r_up,),) = _weight_grad("dw_down", act, dh2b, N_CHIPS, True, False, D_MODEL,
                                      jobs=[_pair_send_job([d_up])])
    p_up = pair_add("w_up", d_up, r_up)
    (dz, dya, dyb, merged, dpa, dpb, dh1b), ((r_down,), (q_up,)) = _bwd_merge(
        dh1, pa, pb, z, w_oa, w_ob, w_o, jobs=[_pair_send_job([d_down]), _chip_exchange_job([p_up])])
    p_down = pair_add("w_down", d_down, r_down)
    half_up = chip_sum("w_up", p_up, q_up)
    d_out, ((full_up,),) = _weight_grad("dw_out", merged, dh1b, 1, False, False, D_MODEL, jobs=[_share_job([half_up])])
    d_oa, _ = _weight_grad("dw_out_a", ya, dpa, 1, False, False, D_MODEL)
    d_ob, _ = _weight_grad("dw_out_b", yb, dpb, 1, False, False, D_MODEL)
    mids = [quarter(d_oa), quarter(d_ob), quarter(d_out)]
    (dz, dlg, dlb, dws, dbs), ((q_down,), r_mids) = _bwd_sgu(
        dz, dyb, z, *sgu, jobs=[_chip_exchange_job([p_down]), _pair_send_job(mids)])
    mid_names = ("w_out_a", "w_out_b", "w_out")
    p_mids = [pair_add(nm, g, r) for nm, g, r in zip(mid_names, mids, r_mids)]
    half_down = chip_sum("w_down", p_down, q_down)
    (dz, dcw, dcb, dwr, dbr, dwi, dbi, dlam), (q_mids, (full_down,)) = _bwd_lru(
        dz, dya, z, *saved, conv_w_full, wr, wi, lru_lambda, jobs=[_chip_exchange_job(p_mids), _share_job([half_down])])
    half_mids = [chip_sum(nm, p, q) for nm, p, q in zip(mid_names, p_mids, q_mids)]
    gates = [_stack_heads(dwr).astype(BF16), _stack_heads(dwi).astype(BF16)]
    small = _pack_small(dcw, dcb, dbr, dbi, dlam, dlg, dlb, dg2, dg3, loss_part, dbs)
    d_in, (full_mids, r_gates, (vec_all, ws_all)) = _weight_grad(
        "dw_in", n1, dz, N_CHIPS, False, True, IN_SHARD,
        jobs=[_share_job(half_mids), _pair_send_job(gates), _gather_all_job([small, dws])])
    names = ("w_in", "w_rgate", "w_igate", "w_out_a", "w_out_b", "w_out", "w_up", "w_down")
    adam_args = {nm: (w, m, v) for nm, w, m, v in zip(names, big_w, big_m, big_v)}

    def adamw(nm, g, jobs=()):
        w, m, v = adam_args[nm]
        g = g.reshape(w.shape)
        return (g,) + tuple(x for x in _adamw("adamw_" + nm, g, w, m, v, jobs))

    (r_in,), = _comm_call("send_w_in", [_pair_send_job([d_in])])
    last_names = ("w_in", "w_rgate", "w_igate")
    p_last = [pair_add(nm, g, r) for nm, g, r in zip(last_names, [d_in] + gates, [r_in] + r_gates)]
    (grad_x, dg1), (q_last,) = _bwd_in(dz, xs, dh1, w_in_st, norm_mix_g, jobs=[_chip_exchange_job(p_last)])
    half_last = [chip_sum(nm, p, q) for nm, p, q in zip(last_names, p_last, q_last)]
    full_last, (dg1_all,) = _comm_call("share_last", [_share_job(half_last), _gather_all_job([dg1])])
    full, big_out = [], []
    for nm, f in zip(names, full_last + full_mids + [full_up, full_down]):
        g, out, _ = adamw(nm, f)
        full.append(g)
        big_out.append(out)

    vec, ws_sum = _sum_small(vec_all, ws_all, dg1_all)
    row = lambda r: vec[r:r + 1]
    shard = lambda a, width: lax.dynamic_slice_in_dim(a, chip * width, width, axis=1)
    g_small = dict(
        norm_mix_g=row(ROW_G1), conv_w=shard(vec[ROW_CW:ROW_CW + CONV_WIDTH], quarter_d), conv_b=row(ROW_CB),
        b_rgate=shard(row(ROW_BR).reshape(HEADS, HEAD_DIM), quarter_h),
        b_igate=shard(row(ROW_BI).reshape(HEADS, HEAD_DIM), quarter_h), lru_lambda=row(ROW_LAM),
        sgu_ln_g=row(ROW_LG), sgu_ln_b=row(ROW_LB),
        sgu_w_s=ws_sum.reshape(CHUNK, GROUPS, CHUNK).transpose(1, 0, 2).reshape(GROUPS * CHUNK, CHUNK),
        sgu_b_s=vec[ROW_BS:ROW_BS + GROUPS, 0:CHUNK], norm_mlp_g=row(ROW_G2), norm_final_g=row(ROW_G3))
    loss = vec[ROW_LOSS, 0]
    small_names = list(g_small)
    given = dict(
        norm_mix_g=(norm_mix_g, m_norm_mix_g, v_norm_mix_g), conv_w=(conv_w, m_conv_w, v_conv_w),
        conv_b=(conv_b, m_conv_b, v_conv_b), b_rgate=(b_rgate, m_b_rgate, v_b_rgate),
        b_igate=(b_igate, m_b_igate, v_b_igate), lru_lambda=(lru_lambda, m_lru_lambda, v_lru_lambda),
        sgu_ln_g=(sgu_ln_g, m_sgu_ln_g, v_sgu_ln_g), sgu_ln_b=(sgu_ln_b, m_sgu_ln_b, v_sgu_ln_b),
        sgu_w_s=(sgu_w_s, m_sgu_w_s, v_sgu_w_s), sgu_b_s=(sgu_b_s, m_sgu_b_s, v_sgu_b_s),
        norm_mlp_g=(norm_mlp_g, m_norm_mlp_g, v_norm_mlp_g), norm_final_g=(norm_final_g, m_norm_final_g, v_norm_final_g))
    g2d = [g_small[nm] for nm in small_names]
    to2d = lambda a, g: a.reshape(g.shape)
    d_s, m_s, v_s = _adamw_small(
        g2d, *[[to2d(given[nm][q], g) for nm, g in zip(small_names, g2d)] for q in range(3)])

    shapes = dict(
        norm_mix_g=norm_mix_g, w_in=w_in, conv_w=conv_w, conv_b=conv_b, w_rgate=w_rgate, b_rgate=b_rgate,
        w_igate=w_igate, b_igate=b_igate, lru_lambda=lru_lambda, w_out_a=w_out_a, sgu_ln_g=sgu_ln_g,
        sgu_ln_b=sgu_ln_b, sgu_w_s=sgu_w_s, sgu_b_s=sgu_b_s, w_out_b=w_out_b, w_out=w_out, norm_mlp_g=norm_mlp_g,
        w_up=w_up, w_down=w_down, norm_final_g=norm_final_g)
    grads, deltas, new_m, new_v = {}, {}, {}, {}
    for nm, g, (d, nmom, nvar) in zip(names, full, big_out):
        grads[nm], deltas[nm], new_m[nm], new_v[nm] = g, d, nmom, nvar
    for p, nm in enumerate(small_names):
        grads[nm], deltas[nm], new_m[nm], new_v[nm] = g2d[p], d_s[p], m_s[p], v_s[p]
    order = list(shapes)
    out = [loss, grad_x[None]]
    for group in (grads, deltas, new_m, new_v):
        out += [group[nm].reshape(shapes[nm].shape) for nm in order]
    return tuple(out)
```

```python
import functools

import jax
import jax.numpy as jnp
from jax import lax
from jax.experimental import pallas as pl
from jax.experimental.pallas import tpu as pltpu

F32 = jnp.float32
BF16 = jnp.bfloat16
MESH = pl.DeviceIdType.MESH

D_MODEL = 1024
D_IN = 6 * D_MODEL
D_FF = 4 * D_MODEL
N_CHIPS = 4
IN_SHARD = D_IN // N_CHIPS
HEADS = 4
HEAD_DIM = D_MODEL // HEADS
GROUPS = 4
GROUP_DIM = D_MODEL // GROUPS
CHUNK = 128
CONV_WIDTH = 4
LRU_C = 8.0
NORM_EPS = 1e-6
LN_EPS = 1e-5

ADAM_LR = 0.001
ADAM_B1 = 0.9
ADAM_B2 = 0.999
ADAM_EPS = 1e-08
ADAM_WD = 0.01
ADAM_STEP = 10

SUBLANES = 8
MM_TILE = 512
SEQ_TILE = 256
DW_TILE = 2048
VMEM_LIMIT_BYTES = 56 * 1024 * 1024

GELU_K0 = 0.7978845608028654
GELU_K1 = 0.044715


def _params(n_grid_axes=1):
    return pltpu.CompilerParams(
        dimension_semantics=("arbitrary",) * n_grid_axes, vmem_limit_bytes=VMEM_LIMIT_BYTES)


def _resident(shape):
    nd = len(shape)
    return pl.BlockSpec(shape, lambda *_: (0,) * nd, pipeline_mode=pl.Buffered(1))


def _const(shape):
    nd = len(shape)
    return pl.BlockSpec(shape, lambda *_: (0,) * nd)


def _dot(a, b):
    return jnp.dot(a, b, preferred_element_type=F32)


def _dot_nt(a, b):
    return lax.dot_general(a, b, (((1,), (1,)), ((), ())), preferred_element_type=F32)


def _dot_tn(a, b):
    return lax.dot_general(a, b, (((0,), (0,)), ((), ())), preferred_element_type=F32)


def _gelu(x):
    t = jnp.tanh(GELU_K0 * x * (1.0 + GELU_K1 * x * x))
    return 0.5 * x * (1.0 + t)


def _gelu_and_grad(x):
    x2 = x * x
    t = jnp.tanh(GELU_K0 * x * (1.0 + GELU_K1 * x2))
    g = 0.5 * x * (1.0 + t)
    dg = 0.5 * (1.0 + t) + 0.5 * x * (1.0 - t * t) * (GELU_K0 * (1.0 + 3.0 * GELU_K1 * x2))
    return g, dg


def _rms(x):
    r = lax.rsqrt(jnp.mean(x * x, axis=-1, keepdims=True) + NORM_EPS)
    return x * r, r


def _rms_bwd(dn, xhat, r):
    return r * (dn - xhat * jnp.mean(dn * xhat, axis=-1, keepdims=True))


def _col_sum(v):
    return jnp.sum(v, axis=0, keepdims=True)


def _shift_down(x, tail8, k):
    xs = pltpu.roll(x, k, 0)
    ts = pltpu.roll(tail8, k, 0)
    ridx = lax.broadcasted_iota(jnp.int32, tail8.shape, 0)
    head = jnp.where(ridx < k, ts, xs[0:SUBLANES])
    return jnp.concatenate([head, xs[SUBLANES:]], axis=0)


def _shift_up(x, head8, k):
    n = x.shape[0]
    xs = pltpu.roll(x, n - k, 0)
    hs = pltpu.roll(head8, SUBLANES - k, 0)
    ridx = lax.broadcasted_iota(jnp.int32, head8.shape, 0)
    last = jnp.where(ridx >= SUBLANES - k, hs, xs[n - SUBLANES:n])
    return jnp.concatenate([xs[:n - SUBLANES], last], axis=0)


def _scan_forward(a, b, carry):
    n, cols = a.shape
    groups = n // SUBLANES
    a = a.reshape(groups, SUBLANES, cols)
    b = b.reshape(groups, SUBLANES, cols)
    sub = lax.broadcasted_iota(jnp.int32, a.shape, 1)
    for s in (1, 2, 4):
        a_s = pltpu.roll(a, s, 1)
        b_s = pltpu.roll(b, s, 1)
        m = sub >= s
        b = jnp.where(m, a * b_s + b, b)
        a = jnp.where(m, a * a_s, a)
    out = []
    for g in range(groups):
        h = a[g] * carry + b[g]
        out.append(h)
        carry = h[SUBLANES - 1:SUBLANES]
    return jnp.concatenate(out, axis=0), carry


def _scan_backward(a, b, carry):
    n, cols = a.shape
    groups = n // SUBLANES
    a = a.reshape(groups, SUBLANES, cols)
    b = b.reshape(groups, SUBLANES, cols)
    sub = lax.broadcasted_iota(jnp.int32, a.shape, 1)
    for s in (1, 2, 4):
        a_s = pltpu.roll(a, SUBLANES - s, 1)
        b_s = pltpu.roll(b, SUBLANES - s, 1)
        m = sub < SUBLANES - s
        b = jnp.where(m, a * b_s + b, b)
        a = jnp.where(m, a * a_s, a)
    out = [None] * groups
    for g in reversed(range(groups)):
        h = a[g] * carry + b[g]
        out[g] = h
        carry = h[0:1]
    return jnp.concatenate(out, axis=0), carry


def _softplus_neg(lam):
    e = jnp.exp(-jnp.abs(lam))
    u = 1.0 + e
    log1p_e = jnp.where(u == 1.0, e, jnp.log(u) * (e / jnp.where(u == 1.0, 1.0, u - 1.0)))
    return jnp.maximum(-lam, 0.0) + log1p_e


def _lru_gates(xa, tail8, cw_ref, cb_ref, wr_ref, br_ref, wi_ref, bi_ref, lam_ref):
    cw = cw_ref[...]
    xc = cb_ref[...] + cw[0:1] * xa
    for k in range(1, CONV_WIDTH):
        xc = xc + cw[k:k + 1] * _shift_down(xa, tail8, k)
    xcb = xc.astype(BF16)
    pre_r, pre_i = [], []
    for h in range(HEADS):
        cols = slice(h * HEAD_DIM, (h + 1) * HEAD_DIM)
        pre_r.append(_dot(xcb[:, cols], wr_ref[h]))
        pre_i.append(_dot(xcb[:, cols], wi_ref[h]))
    r = jax.nn.sigmoid(jnp.concatenate(pre_r, axis=1) + br_ref[...])
    ig = jax.nn.sigmoid(jnp.concatenate(pre_i, axis=1) + bi_ref[...])
    _, a, mult = _decay(r, lam_ref)
    return xc, r, ig, a, mult


def _decay(r, lam_ref):
    sp = _softplus_neg(lam_ref[...])
    log_a = ((-LRU_C) * sp) * r
    a = jnp.exp(log_a)
    th = jnp.tanh(log_a)
    return sp, a, jnp.sqrt((-2.0 * th) / (1.0 - th))


class _Job:
    def __init__(self, inputs, out_shape, n_sem, copies, aliases=None, n_local=0):
        self.inputs, self.out_shape, self.n_sem, self.copies = list(inputs), list(out_shape), n_sem, copies
        self.aliases, self.n_local = dict(aliases or {}), n_local


def _fused_call(body, jobs, *, name, grid, in_specs, out_specs, out_shape, scratch_shapes=(),
                input_output_aliases=None, compiler_params=None, n_prefetch=0, jobs_start_after=None):
    single = not isinstance(out_shape, (list, tuple))
    out_specs = [out_specs] if single else list(out_specs)
    out_shape = [out_shape] if single else list(out_shape)
    n_scr = len(scratch_shapes)
    in_specs, scratch_shapes = list(in_specs), list(scratch_shapes)
    n_in, n_out = len(in_specs), len(out_shape)
    aliases = dict(input_output_aliases or {})
    in_at, out_at = [], []
    for job in jobs:
        in_at.append(len(in_specs))
        out_at.append(len(out_shape))
        for i, o in job.aliases.items():
            aliases[n_prefetch + len(in_specs) + i] = len(out_shape) + o
        in_specs += [ANY] * len(job.inputs)
        out_specs += [ANY] * len(job.out_shape)
        out_shape += job.out_shape
        scratch_shapes += [pltpu.SemaphoreType.DMA((job.n_sem,)), pltpu.SemaphoreType.DMA((job.n_sem,)),
                           pltpu.SemaphoreType.DMA((max(job.n_local, 1),))]
    n_in_all, n_out_all = len(in_specs), len(out_shape)

    def full_body(*refs):
        prefetch, refs = refs[:n_prefetch], refs[n_prefetch:]
        ins, outs, scr = refs[:n_in_all], refs[n_in_all:n_in_all + n_out_all], refs[n_in_all + n_out_all:]

        def copies(q):
            job = jobs[q]
            return job.copies(ins[in_at[q]:in_at[q] + len(job.inputs)], outs[out_at[q]:out_at[q] + len(job.out_shape)],
                              *scr[n_scr + 3 * q:n_scr + 3 * q + 3])

        def start():
            for q in range(len(jobs)):
                sends, _, local = copies(q)
                for cp in local + sends:
                    cp.start()

        def finish():
            every = [copies(q) for q in range(len(jobs))]
            for _, arrivals, _ in every:
                for cp in arrivals:
                    cp.wait_recv()
            for sends, _, local in every:
                for cp in sends:
                    cp.wait_send()
                for cp in local:
                    cp.wait()

        if not grid:
            start()
            finish()
            return
        ids = [pl.program_id(a) for a in range(len(grid))]
        at_step = lambda step: functools.reduce(jnp.logical_and, [i == k for i, k in zip(ids, step)])
        if jobs and jobs_start_after is None:
            pl.when(at_step((0,) * len(grid)))(start)
        body(*prefetch, *ins[:n_in], *outs[:n_out], *scr[:n_scr])
        if jobs and jobs_start_after is not None:
            pl.when(at_step(jobs_start_after))(start)
        if jobs:
            pl.when(functools.reduce(jnp.logical_and, [i == g - 1 for i, g in zip(ids, grid)]))(finish)

    if n_prefetch:
        layout = dict(grid_spec=pltpu.PrefetchScalarGridSpec(
            num_scalar_prefetch=n_prefetch, grid=grid, in_specs=in_specs, out_specs=out_specs,
            scratch_shapes=scratch_shapes))
    else:
        layout = dict(grid=grid, in_specs=in_specs, out_specs=out_specs, scratch_shapes=scratch_shapes)
    call = pl.pallas_call(
        full_body, name=name, out_shape=out_shape, input_output_aliases=aliases, compiler_params=compiler_params,
        **layout)

    def run(*args):
        res = call(*args, *[a for job in jobs for a in job.inputs])
        mine = res[0] if single else list(res[:n_out])
        return mine, [list(res[at:at + len(job.out_shape)]) for at, job in zip(out_at, jobs)]

    return run


def _fwd_in(x, g1, shards, order, jobs=()):
    t = x.shape[0]
    n_tiles = t // MM_TILE
    n = len(shards)
    halves = [s.shape[0] // 2 for s in shards]

    def body(order_ref, x_ref, g_ref, *refs):
        del order_ref
        ins, (z_ref, n_ref), outs = refs[:n], refs[n:n + 2], refs[n + 2:2 * n + 2]
        wbuf, nbuf, send, recv, local = refs[2 * n + 2:]
        s, i = pl.program_id(0), pl.program_id(1)
        x_, y_, c, chips = _place()
        k_me = _chip_index(x_, y_)

        def block(w, chip, pc):
            return outs[w].at[_chip_index(*chip), pl.ds(pc * halves[w], halves[w]), :]

        def over_ici(w, j, landing):
            return pltpu.make_async_remote_copy(
                src_ref=ins[w].at[pl.ds(c * halves[w], halves[w]), :],
                dst_ref=block(w, chips[j] if landing else (x_, y_), c), send_sem=send.at[6 * w + j],
                recv_sem=recv.at[6 * w + j], device_id=(*chips[j], c), device_id_type=MESH)

        def to_sibling(w, j, landing):
            blk = block(w, chips[j], 1 - c if landing else c)
            return pltpu.make_async_remote_copy(
                src_ref=blk, dst_ref=blk, send_sem=send.at[6 * w + 3 + j], recv_sem=recv.at[6 * w + 3 + j],
                device_id=(x_, y_, 1 - c), device_id_type=MESH)

        own = [pltpu.make_async_copy(wbuf, outs[0].at[k_me], local.at[0])]
        own += [pltpu.make_async_copy(ins[w], outs[w].at[k_me], local.at[w]) for w in range(1, n)]

        @pl.when((s == 0) & (i == 0))
        def _():
            for j in range(2):
                for w in range(n):
                    over_ici(w, j, False).start()
            load = pltpu.make_async_copy(ins[0], wbuf, local.at[n])
            load.start()
            load.wait()
            for cp in own:
                cp.start()

        for j in range(N_CHIPS - 1):
            @pl.when((s == j + 1) & (i == 0))
            def _(j=j):
                for w in range(n):
                    over_ici(w, j, True).wait_recv()
                for w in range(n):
                    to_sibling(w, j, False).start()
                if j == 0:
                    for w in range(n):
                        over_ici(w, 2, False).start()
                    own[0].wait()
                for w in range(n):
                    to_sibling(w, j, True).wait_recv()
                load = pltpu.make_async_copy(outs[0].at[_chip_index(*chips[j])], wbuf, local.at[n])
                load.start()
                load.wait()

        rows = pl.ds(pl.multiple_of(i * MM_TILE, MM_TILE), MM_TILE)

        @pl.when(s == 0)
        def _():
            xhat, _ = _rms(x_ref[...])
            nrm = (xhat * g_ref[...]).astype(BF16)
            nbuf[rows, :] = nrm
            n_ref[...] = nrm

        z_ref[...] = _dot(nbuf[rows, :], wbuf[...])

        @pl.when((s == N_CHIPS - 1) & (i == n_tiles - 1))
        def _():
            for j in range(N_CHIPS - 1):
                for w in range(n):
                    over_ici(w, j, False).wait_send()
                    to_sibling(w, j, False).wait_send()
            for cp in own[1:]:
                cp.wait()

    once = lambda s, i, order: (jnp.where(s == 0, i, n_tiles - 1), 0)
    (z, n1, *stacked), job_outs = _fused_call(
        body, jobs, name="fwd_in", grid=(N_CHIPS, n_tiles), n_prefetch=1,
        in_specs=[pl.BlockSpec((MM_TILE, D_MODEL), once), _const((1, D_MODEL))] + [ANY] * n,
        out_specs=[pl.BlockSpec((MM_TILE, IN_SHARD), lambda s, i, order: (i, order[s])),
                   pl.BlockSpec((MM_TILE, D_MODEL), once)] + [ANY] * n,
        out_shape=[jax.ShapeDtypeStruct((t, D_IN), F32), jax.ShapeDtypeStruct((t, D_MODEL), BF16)]
        + [jax.ShapeDtypeStruct((N_CHIPS,) + s.shape, s.dtype) for s in shards],
        scratch_shapes=[pltpu.VMEM(shards[0].shape, BF16), pltpu.VMEM((t, D_MODEL), BF16),
                        pltpu.SemaphoreType.DMA((6 * n,)),
                        pltpu.SemaphoreType.DMA((6 * n,)), pltpu.SemaphoreType.DMA((n + 1,))],
        compiler_params=_params(2), jobs_start_after=(1, 0),
    )(order, x, g1, *shards)
    return (z, n1, stacked), job_outs


def _fwd_lru(z, conv_w, conv_b, wr, br, wi, bi, lam, jobs=()):
    t = z.shape[0]

    def body(xa_ref, ga_ref, cw_ref, cb_ref, wr_ref, br_ref, wi_ref, bi_ref, lam_ref, ya_ref, h_ref, xc_ref, r_ref,
             ig_ref, tail_ref, carry_ref):
        @pl.when(pl.program_id(0) == 0)
        def _():
            tail_ref[...] = jnp.zeros_like(tail_ref)
            carry_ref[...] = jnp.zeros_like(carry_ref)

        xa = xa_ref[...]
        xc, r, ig, a, mult = _lru_gates(xa, tail_ref[...], cw_ref, cb_ref, wr_ref, br_ref, wi_ref, bi_ref, lam_ref)
        tail_ref[...] = xa[SEQ_TILE - SUBLANES:]
        xc_ref[...], r_ref[...], ig_ref[...] = xc, r, ig
        h, carry = _scan_forward(a, xc * ig * mult, carry_ref[...])
        carry_ref[...] = carry
        h_ref[...] = h
        ya_ref[...] = (h * _gelu(ga_ref[...])).astype(BF16)

    tile = lambda j: pl.BlockSpec((SEQ_TILE, D_MODEL), lambda i: (i, j))
    return _fused_call(
        body, jobs, name="fwd_lru", grid=(t // SEQ_TILE,),
        in_specs=[tile(0), tile(1), _const((CONV_WIDTH, D_MODEL)), _const((1, D_MODEL)),
                  _resident((HEADS, HEAD_DIM, HEAD_DIM)), _const((1, D_MODEL)),
                  _resident((HEADS, HEAD_DIM, HEAD_DIM)), _const((1, D_MODEL)), _const((1, D_MODEL))],
        out_specs=[tile(0)] * 5,
        out_shape=[jax.ShapeDtypeStruct((t, D_MODEL), BF16)] + [jax.ShapeDtypeStruct((t, D_MODEL), F32)] * 4,
        scratch_shapes=[pltpu.VMEM((SUBLANES, D_MODEL), F32), pltpu.VMEM((1, D_MODEL), F32)],
        compiler_params=_params(),
    )(z, z, conv_w, conv_b, wr, br, wi, bi, lam)


def _sgu_forward_parts(ub, vb, lg_ref, lb_ref):
    u, du = _gelu_and_grad(ub)
    vg, dvg = _gelu_and_grad(vb)
    mu = jnp.mean(vg, axis=-1, keepdims=True)
    d = vg - mu
    rstd = lax.rsqrt(jnp.mean(d * d, axis=-1, keepdims=True) + LN_EPS)
    vhat = d * rstd
    vn = (vhat * lg_ref[...] + lb_ref[...]).astype(BF16)
    return u, du, dvg, rstd, vhat, vn


def _causal_mask():
    rows = lax.broadcasted_iota(jnp.int32, (CHUNK, CHUNK), 0)
    cols = lax.broadcasted_iota(jnp.int32, (CHUNK, CHUNK), 1)
    return rows >= cols


def _fwd_sgu(z, ln_g, ln_b, w_s, bias_full, jobs=()):
    t = z.shape[0]

    def body(ub_ref, vb_ref, lg_ref, lb_ref, ws_ref, bias_ref, yb_ref):
        u, _, _, _, _, vn = _sgu_forward_parts(ub_ref[...], vb_ref[...], lg_ref, lb_ref)
        mask = _causal_mask()
        wm = [jnp.where(mask, ws_ref[g], 0.0).astype(BF16) for g in range(GROUPS)]
        for c in range(SEQ_TILE // CHUNK):
            rows = slice(c * CHUNK, (c + 1) * CHUNK)
            for g in range(GROUPS):
                cols = slice(g * GROUP_DIM, (g + 1) * GROUP_DIM)
                sp = _dot(wm[g], vn[rows, cols]) + bias_ref[:, cols]
                yb_ref[rows, cols] = (u[rows, cols] * sp).astype(BF16)

    tile = lambda j: pl.BlockSpec((SEQ_TILE, D_MODEL), lambda i: (i, j))
    return _fused_call(
        body, jobs, name="fwd_sgu", grid=(t // SEQ_TILE,),
        in_specs=[tile(2), tile(3), _const((1, D_MODEL)), _const((1, D_MODEL)),
                  _const((GROUPS, CHUNK, CHUNK)), _const((CHUNK, D_MODEL))],
        out_specs=tile(0),
        out_shape=jax.ShapeDtypeStruct((t, D_MODEL), BF16),
        compiler_params=_params(),
    )(z, z, ln_g, ln_b, w_s, bias_full)


def _fwd_merge(ya, yb, z, x, w_oa, w_ob, w_out, g2, jobs=()):
    t = x.shape[0]

    def body(ya_ref, yb_ref, m_ref, x_ref, woa_ref, wob_ref, wout_ref, g_ref, pa_ref, pb_ref, h1_ref, n2_ref):
        pa = _dot(ya_ref[...], woa_ref[...])
        pb = _dot(yb_ref[...], wob_ref[...])
        pa_ref[...] = pa
        pb_ref[...] = pb
        merged = jax.nn.sigmoid(m_ref[:, :D_MODEL]) * pa + jax.nn.sigmoid(m_ref[:, D_MODEL:]) * pb
        h1 = x_ref[...] + _dot(merged.astype(BF16), wout_ref[...])
        h1_ref[...] = h1
        xhat, _ = _rms(h1)
        n2_ref[...] = (xhat * g_ref[...]).astype(BF16)

    tile = pl.BlockSpec((MM_TILE, D_MODEL), lambda i: (i, 0))
    sq = _resident((D_MODEL, D_MODEL))
    return _fused_call(
        body, jobs, name="fwd_merge", grid=(t // MM_TILE,),
        in_specs=[tile, tile, pl.BlockSpec((MM_TILE, 2 * D_MODEL), lambda i: (i, 2)), tile, sq, sq, sq,
                  _const((1, D_MODEL))],
        out_specs=[tile, tile, tile, tile],
        out_shape=[jax.ShapeDtypeStruct((t, D_MODEL), F32)] * 3 + [jax.ShapeDtypeStruct((t, D_MODEL), BF16)],
        compiler_params=_params(),
    )(ya, yb, z, x, w_oa, w_ob, w_out, g2)


def _fwd_mlp(n2, h1, target, w_up_st, w_down, g3, jobs=()):
    t = n2.shape[0]

    def body(n2_ref, h1_ref, tgt_ref, wup_ref, wdown_ref, g_ref, up_ref, act_ref, dh2_ref, dh2b_ref, loss_ref,
             dg3_ref):
        @pl.when(pl.program_id(0) == 0)
        def _():
            loss_ref[...] = jnp.zeros_like(loss_ref)
            dg3_ref[...] = jnp.zeros_like(dg3_ref)

        n2 = n2_ref[...]
        h2 = h1_ref[...]
        for k in range(N_CHIPS):
            cols = slice(k * D_MODEL, (k + 1) * D_MODEL)
            up = _dot(n2, wup_ref[k])
            up_ref[:, cols] = up.astype(BF16)
            r = jnp.maximum(up, 0.0)
            act = (r * r).astype(BF16)
            act_ref[:, cols] = act
            h2 = h2 + _dot(act, wdown_ref[cols, :])
        xhat, r3 = _rms(h2)
        diff = xhat * g_ref[...] - tgt_ref[...]
        sq = jnp.sum(diff * diff, axis=1, keepdims=True)
        loss_ref[...] = loss_ref[...] + (0.5 / D_MODEL) * jnp.sum(sq, axis=0, keepdims=True)
        dy = diff * (1.0 / D_MODEL)
        dg3_ref[...] = dg3_ref[...] + _col_sum(dy * xhat)
        dh2 = _rms_bwd(dy * g_ref[...], xhat, r3)
        dh2_ref[...] = dh2
        dh2b_ref[...] = dh2.astype(BF16)

    tile = pl.BlockSpec((MM_TILE, D_MODEL), lambda i: (i, 0))
    wide = pl.BlockSpec((MM_TILE, D_FF), lambda i: (i, 0))
    return _fused_call(
        body, jobs, name="fwd_mlp", grid=(t // MM_TILE,),
        in_specs=[tile, tile, tile, _resident((N_CHIPS, D_MODEL, D_MODEL)), _resident((D_FF, D_MODEL)),
                  _const((1, D_MODEL))],
        out_specs=[wide, wide, tile, tile, _const((SUBLANES, 128)), _const((1, D_MODEL))],
        out_shape=[jax.ShapeDtypeStruct((t, D_FF), BF16), jax.ShapeDtypeStruct((t, D_FF), BF16),
                   jax.ShapeDtypeStruct((t, D_MODEL), F32), jax.ShapeDtypeStruct((t, D_MODEL), BF16),
                   jax.ShapeDtypeStruct((SUBLANES, 128), F32), jax.ShapeDtypeStruct((1, D_MODEL), F32)],
        compiler_params=_params(),
    )(n2, h1, target, w_up_st, w_down, g3)


def _bwd_mlp(dh2, dh2b, up, h1, w_up_st, w_down, g2, jobs=()):
    t = dh2.shape[0]

    def body(dh2_ref, dh2b_ref, up_ref, h1_ref, wup_ref, wdown_ref, g_ref, dup_ref, dh1_ref, dg2_ref):
        @pl.when(pl.program_id(0) == 0)
        def _():
            dg2_ref[...] = jnp.zeros_like(dg2_ref)

        dh2b = dh2b_ref[...]
        dn2 = jnp.zeros((MM_TILE, D_MODEL), F32)
        for k in range(N_CHIPS):
            cols = slice(k * D_MODEL, (k + 1) * D_MODEL)
            dact = _dot_nt(dh2b, wdown_ref[cols, :])
            dup = (dact * (2.0 * jnp.maximum(up_ref[:, cols].astype(F32), 0.0))).astype(BF16)
            dup_ref[:, cols] = dup
            dn2 = dn2 + _dot_nt(dup, wup_ref[k])
        xhat, r2 = _rms(h1_ref[...])
        dg2_ref[...] = dg2_ref[...] + _col_sum(dn2 * xhat)
        dh1_ref[...] = dh2_ref[...] + _rms_bwd(dn2 * g_ref[...], xhat, r2)

    tile = pl.BlockSpec((MM_TILE, D_MODEL), lambda i: (i, 0))
    wide = pl.BlockSpec((MM_TILE, D_FF), lambda i: (i, 0))
    return _fused_call(
        body, jobs, name="bwd_mlp", grid=(t // MM_TILE,),
        in_specs=[tile, tile, wide, tile, _resident((N_CHIPS, D_MODEL, D_MODEL)), _resident((D_FF, D_MODEL)),
                  _const((1, D_MODEL))],
        out_specs=[wide, tile, _const((1, D_MODEL))],
        out_shape=[jax.ShapeDtypeStruct((t, D_FF), BF16), jax.ShapeDtypeStruct((t, D_MODEL), F32),
                   jax.ShapeDtypeStruct((1, D_MODEL), F32)],
        compiler_params=_params(),
    )(dh2, dh2b, up, h1, w_up_st, w_down, g2)


def _bwd_merge(dh1, pa, pb, z, w_oa, w_ob, w_out, jobs=()):
    t = dh1.shape[0]

    def body(dh1_ref, pa_ref, pb_ref, m_ref, woa_ref, wob_ref, wout_ref, dz_ref, dya_ref, dyb_ref, mg_ref,
             dpa_ref, dpb_ref, dh1b_ref):
        dh1b = dh1_ref[...].astype(BF16)
        dh1b_ref[...] = dh1b
        dm = _dot_nt(dh1b, wout_ref[...])
        pa = pa_ref[...]
        pb = pb_ref[...]
        sa = jax.nn.sigmoid(m_ref[:, :D_MODEL])
        sb = jax.nn.sigmoid(m_ref[:, D_MODEL:])
        mg_ref[...] = (sa * pa + sb * pb).astype(BF16)
        dz_ref[:, :D_MODEL] = (dm * pa * sa * (1.0 - sa)).astype(BF16)
        dz_ref[:, D_MODEL:] = (dm * pb * sb * (1.0 - sb)).astype(BF16)
        dpa = (dm * sa).astype(BF16)
        dpb = (dm * sb).astype(BF16)
        dpa_ref[...] = dpa
        dpb_ref[...] = dpb
        dya_ref[...] = _dot_nt(dpa, woa_ref[...])
        dyb_ref[...] = _dot_nt(dpb, wob_ref[...])

    tile = pl.BlockSpec((MM_TILE, D_MODEL), lambda i: (i, 0))
    pair = pl.BlockSpec((MM_TILE, 2 * D_MODEL), lambda i: (i, 2))
    sq = _resident((D_MODEL, D_MODEL))
    act_bf = jax.ShapeDtypeStruct((t, D_MODEL), BF16)
    return _fused_call(
        body, jobs, name="bwd_merge", grid=(t // MM_TILE,),
        in_specs=[tile, tile, tile, pair, sq, sq, sq],
        out_specs=[pair, tile, tile, tile, tile, tile, tile],
        out_shape=[jax.ShapeDtypeStruct((t, D_IN), BF16), jax.ShapeDtypeStruct((t, D_MODEL), F32),
                   jax.ShapeDtypeStruct((t, D_MODEL), F32), act_bf, act_bf, act_bf, act_bf],
        compiler_params=_params(),
    )(dh1, pa, pb, z, w_oa, w_ob, w_out)


def _bwd_sgu(dz, dyb, z, ln_g, ln_b, w_s, bias_full, jobs=()):
    t = dyb.shape[0]
    n_tiles = t // SEQ_TILE

    def body(dz_any, dyb_ref, ub_ref, vb_ref, lg_ref, lb_ref, ws_ref, bias_ref, dz_ref, dlg_ref, dlb_ref, dws_ref,
             dbs_ref, dvn_ref, dsp_acc):
        del dz_any
        i = pl.program_id(0)

        @pl.when(i == 0)
        def _():
            dlg_ref[...] = jnp.zeros_like(dlg_ref)
            dlb_ref[...] = jnp.zeros_like(dlb_ref)
            dws_ref[...] = jnp.zeros_like(dws_ref)
            dsp_acc[...] = jnp.zeros_like(dsp_acc)

        u, du, dvg, rstd, vhat, vn = _sgu_forward_parts(ub_ref[...], vb_ref[...], lg_ref, lb_ref)
        dyb = dyb_ref[...]
        mask = _causal_mask()
        wm = [jnp.where(mask, ws_ref[g], 0.0).astype(BF16) for g in range(GROUPS)]
        for c in range(SEQ_TILE // CHUNK):
            rows = slice(c * CHUNK, (c + 1) * CHUNK)
            for g in range(GROUPS):
                cols = slice(g * GROUP_DIM, (g + 1) * GROUP_DIM)
                vn_blk = vn[rows, cols]
                sp = _dot(wm[g], vn_blk) + bias_ref[:, cols]
                dyb_blk = dyb[rows, cols]
                dz_ref[rows, cols] = (dyb_blk * sp * du[rows, cols]).astype(BF16)
                dsp = dyb_blk * u[rows, cols]
                dsp_acc[:, cols] = dsp_acc[:, cols] + dsp
                dspb = dsp.astype(BF16)
                dvn_ref[rows, cols] = _dot_tn(wm[g], dspb)
                wcols = slice(g * CHUNK, (g + 1) * CHUNK)
                dws_ref[:, wcols] = dws_ref[:, wcols] + jnp.where(mask, _dot_nt(dspb, vn_blk), 0.0)
        dvn = dvn_ref[...]
        dlg_ref[...] = dlg_ref[...] + _col_sum(dvn * vhat)
        dlb_ref[...] = dlb_ref[...] + _col_sum(dvn)
        dvhat = dvn * lg_ref[...]
        dvgel = rstd * (dvhat - jnp.mean(dvhat, axis=-1, keepdims=True)
                        - vhat * jnp.mean(dvhat * vhat, axis=-1, keepdims=True))
        dz_ref[:, D_MODEL:] = (dvgel * dvg).astype(BF16)

        @pl.when(i == n_tiles - 1)
        def _():
            lane = lax.broadcasted_iota(jnp.int32, (CHUNK, 128), 1)
            out = jnp.zeros((CHUNK, 128), F32)
            for g in range(GROUPS):
                s = jnp.sum(dsp_acc[:, g * GROUP_DIM:(g + 1) * GROUP_DIM], axis=1, keepdims=True)
                out = out + jnp.where(lane == g, s, 0.0)
            dbs_ref[...] = out

    tile = lambda j: pl.BlockSpec((SEQ_TILE, D_MODEL), lambda i: (i, j))
    return _fused_call(
        body, jobs, name="bwd_sgu", grid=(n_tiles,),
        in_specs=[pl.BlockSpec(memory_space=pl.ANY), tile(0), tile(2), tile(3), _const((1, D_MODEL)),
                  _const((1, D_MODEL)), _const((GROUPS, CHUNK, CHUNK)), _const((CHUNK, D_MODEL))],
        out_specs=[pl.BlockSpec((SEQ_TILE, 2 * D_MODEL), lambda i: (i, 1)), _const((1, D_MODEL)),
                   _const((1, D_MODEL)), _const((CHUNK, GROUPS * CHUNK)), _const((CHUNK, 128))],
        out_shape=[jax.ShapeDtypeStruct((t, D_IN), BF16), jax.ShapeDtypeStruct((1, D_MODEL), F32),
                   jax.ShapeDtypeStruct((1, D_MODEL), F32), jax.ShapeDtypeStruct((CHUNK, GROUPS * CHUNK), F32),
                   jax.ShapeDtypeStruct((CHUNK, 128), F32)],
        scratch_shapes=[pltpu.VMEM((SEQ_TILE, D_MODEL), F32), pltpu.VMEM((CHUNK, D_MODEL), F32)],
        input_output_aliases={0: 0},
        compiler_params=_params(),
    )(dz, dyb, z, z, ln_g, ln_b, w_s, bias_full)


def _bwd_lru(dz, dya, z, h, xc, r, ig, conv_w, wr, wi, lam, jobs=()):
    t = dya.shape[0]
    n_tiles = t // SEQ_TILE
    per_tile = SEQ_TILE // SUBLANES

    def body(dz_any, dya_ref, xa_ref, ga_ref, h_ref, h_prev_ref, xc_ref, r_ref, ig_ref, cw_ref, wr_ref, wi_ref, lam_ref,
             dz_ref, dcw_ref, dcb_ref, dwr_ref, dbr_ref, dwi_ref, dbi_ref, dlam_ref, lam_carry, dxc_head):
        del dz_any
        i = pl.program_id(0)

        @pl.when(i == 0)
        def _():
            for ref in (dcw_ref, dcb_ref, dwr_ref, dbr_ref, dwi_ref, dbi_ref, dlam_ref, lam_carry, dxc_head):
                ref[...] = jnp.zeros_like(ref)

        first_tile = i == n_tiles - 1
        h_tail = jnp.where(first_tile, 0.0, h_prev_ref[...])
        xc, r, ig = xc_ref[...], r_ref[...], ig_ref[...]
        xcb = xc.astype(BF16)
        sp, a, mult = _decay(r, lam_ref)
        h = h_ref[...]
        h_prev = _shift_down(h, h_tail, 1)
        dya = dya_ref[...]
        gg, dgg = _gelu_and_grad(ga_ref[...])
        dz_ref[:, D_MODEL:] = (dya * h * dgg).astype(BF16)
        ones = jnp.ones((SUBLANES, D_MODEL), F32)
        lam_t, lam_first = _scan_backward(_shift_up(a, ones, 1), dya * gg, lam_carry[...])
        lam_carry[...] = a[0:1] * lam_first
        dmult = lam_t * xc * ig
        dla = lam_t * h_prev * a - dmult * (a * a) / mult
        dr = dla * ((-LRU_C) * sp)
        dlam_ref[...] = dlam_ref[...] + _col_sum(dla * r) * (LRU_C * jax.nn.sigmoid(-lam_ref[...]))
        dpr = dr * r * (1.0 - r)
        dpi = lam_t * xc * mult * ig * (1.0 - ig)
        dbr_ref[...] = dbr_ref[...] + _col_sum(dpr)
        dbi_ref[...] = dbi_ref[...] + _col_sum(dpi)
        dprb = dpr.astype(BF16)
        dpib = dpi.astype(BF16)
        dxc_gate = []
        for hd in range(HEADS):
            cols = slice(hd * HEAD_DIM, (hd + 1) * HEAD_DIM)
            dxc_gate.append(_dot_nt(dprb[:, cols], wr_ref[hd]) + _dot_nt(dpib[:, cols], wi_ref[hd]))
            dwr_ref[hd] = dwr_ref[hd] + _dot_tn(xcb[:, cols], dprb[:, cols])
            dwi_ref[hd] = dwi_ref[hd] + _dot_tn(xcb[:, cols], dpib[:, cols])
        dxc = lam_t * ig * mult + jnp.concatenate(dxc_gate, axis=1)
        dcb_ref[...] = dcb_ref[...] + _col_sum(dxc)
        cw = cw_ref[...]
        head = dxc_head[...]
        xa = xa_ref[...]
        dxa = cw[0:1] * dxc
        dcw_ref[0:1, :] = dcw_ref[0:1, :] + _col_sum(dxc * xa)
        for k in range(1, CONV_WIDTH):
            dxc_k = _shift_up(dxc, head, k)
            dxa = dxa + cw[k:k + 1] * dxc_k
            dcw_ref[k:k + 1, :] = dcw_ref[k:k + 1, :] + _col_sum(dxc_k * xa)
        dxc_head[...] = dxc[0:SUBLANES]
        dz_ref[:, :D_MODEL] = dxa.astype(BF16)

    rev = lambda i: n_tiles - 1 - i
    tile = lambda j: pl.BlockSpec((SEQ_TILE, D_MODEL), lambda i: (rev(i), j))
    prev8 = pl.BlockSpec((SUBLANES, D_MODEL), lambda i: (jnp.maximum(rev(i) * per_tile - 1, 0), 0))
    vec = _const((1, D_MODEL))
    gate_w = _resident((HEADS, HEAD_DIM, HEAD_DIM))
    vec_shape = jax.ShapeDtypeStruct((1, D_MODEL), F32)
    gate_shape = jax.ShapeDtypeStruct((HEADS, HEAD_DIM, HEAD_DIM), F32)
    return _fused_call(
        body, jobs, name="bwd_lru", grid=(n_tiles,),
        in_specs=[pl.BlockSpec(memory_space=pl.ANY), tile(0), tile(0), tile(1), tile(0), prev8] + [tile(0)] * 3
                 + [_const((CONV_WIDTH, D_MODEL)), gate_w, gate_w, vec],
        out_specs=[pl.BlockSpec((SEQ_TILE, 2 * D_MODEL), lambda i: (rev(i), 0)), _const((SUBLANES, D_MODEL)), vec,
                   _const((HEADS, HEAD_DIM, HEAD_DIM)), vec, _const((HEADS, HEAD_DIM, HEAD_DIM)), vec, vec],
        out_shape=[jax.ShapeDtypeStruct((t, D_IN), BF16), jax.ShapeDtypeStruct((SUBLANES, D_MODEL), F32), vec_shape,
                   gate_shape, vec_shape, gate_shape, vec_shape, vec_shape],
        scratch_shapes=[pltpu.VMEM((1, D_MODEL), F32), pltpu.VMEM((SUBLANES, D_MODEL), F32)],
        input_output_aliases={0: 0},
        compiler_params=_params(),
    )(dz, dya, z, z, h, h, xc, r, ig, conv_w, wr, wi, lam)


def _bwd_in(dz, x, dh1, w_in_st, g1, jobs=()):
    t = x.shape[0]

    def body(dz_ref, x_ref, dh1_ref, w_ref, g_ref, dx_ref, dg1_ref):
        @pl.when(pl.program_id(0) == 0)
        def _():
            dg1_ref[...] = jnp.zeros_like(dg1_ref)

        dn1 = jnp.zeros((MM_TILE, D_MODEL), F32)
        for k in range(N_CHIPS):
            dn1 = dn1 + _dot_nt(dz_ref[:, k * IN_SHARD:(k + 1) * IN_SHARD], w_ref[k])
        xhat, r1 = _rms(x_ref[...])
        dg1_ref[...] = dg1_ref[...] + _col_sum(dn1 * xhat)
        dx_ref[...] = dh1_ref[...] + _rms_bwd(dn1 * g_ref[...], xhat, r1)

    tile = pl.BlockSpec((MM_TILE, D_MODEL), lambda i: (i, 0))
    return _fused_call(
        body, jobs, name="bwd_in", grid=(t // MM_TILE,),
        in_specs=[pl.BlockSpec((MM_TILE, D_IN), lambda i: (i, 0)), tile, tile,
                  _resident((N_CHIPS, D_MODEL, IN_SHARD)), _const((1, D_MODEL))],
        out_specs=[tile, _const((1, D_MODEL))],
        out_shape=[jax.ShapeDtypeStruct((t, D_MODEL), F32), jax.ShapeDtypeStruct((1, D_MODEL), F32)],
        compiler_params=_params(),
    )(dz, x, dh1, w_in_st, g1)


def _weight_grad(name, a, b, n_blocks, a_varies, b_varies, width, jobs=()):
    t = a.shape[0]
    rows = min(DW_TILE, t)
    n_t = t // rows

    def body(a_ref, b_ref, o_ref, acc_ref):
        s = pl.program_id(1)
        part = _dot_tn(a_ref[...], b_ref[...])

        @pl.when(s == 0)
        def _():
            acc_ref[...] = part

        @pl.when(s > 0)
        def _():
            acc_ref[...] = acc_ref[...] + part

        @pl.when(s == n_t - 1)
        def _():
            o_ref[...] = acc_ref[...].astype(BF16)

    return _fused_call(
        body, jobs, name=name, grid=(n_blocks, n_t),
        in_specs=[pl.BlockSpec((rows, D_MODEL), (lambda j, s: (s, j)) if a_varies else (lambda j, s: (s, 0))),
                  pl.BlockSpec((rows, width), (lambda j, s: (s, j)) if b_varies else (lambda j, s: (s, 0)))],
        out_specs=pl.BlockSpec((None, D_MODEL, width), lambda j, s: (j, 0, 0)),
        out_shape=jax.ShapeDtypeStruct((n_blocks, D_MODEL, width), BF16),
        scratch_shapes=[pltpu.VMEM((D_MODEL, width), F32)],
        compiler_params=_params(2),
    )(a, b)


def _place():
    x, y, c = lax.axis_index("x"), lax.axis_index("y"), lax.axis_index("c")
    other_chips = [(1 - x, y), (x, 1 - y), (1 - x, 1 - y)]
    return x, y, c, other_chips


def _chip_index(px, py):
    return 2 * px + py


ANY = pl.BlockSpec(memory_space=pl.ANY)


def _comm_call(name, jobs):
    return _fused_call(None, jobs, name=name, grid=(), in_specs=[], out_specs=[], out_shape=[])()[1]


def _near_far(x, y, c):
    return (x ^ (1 - c), y ^ c), (x ^ c, y ^ (1 - c))


def _gather_near_job(shards):
    n = len(shards)
    halves = [s.shape[0] // 2 for s in shards]

    def copies(ins, outs, send, recv, local):
        x, y, c, _ = _place()
        near, _ = _near_far(x, y, c)

        def block(w, chip, pc):
            return outs[w].at[_chip_index(*chip), pl.ds(pc * halves[w], halves[w]), :]

        def copy(w, k, chip, pc, to, src=None):
            return pltpu.make_async_remote_copy(
                src_ref=block(w, chip, pc) if src is None else src, dst_ref=block(w, chip, pc),
                send_sem=send.at[2 * w + k], recv_sem=recv.at[2 * w + k], device_id=to, device_id_type=MESH)

        sends, arrivals, own = [], [], []
        for w in range(n):
            src = ins[w].at[pl.ds(c * halves[w], halves[w]), :]
            own.append(pltpu.make_async_copy(src, block(w, (x, y), c), local.at[w]))
            sends += [copy(w, 0, (x, y), c, (*near, c), src), copy(w, 1, (x, y), c, (x, y, 1 - c), src)]
            arrivals += [copy(w, 0, near, c, (x, y, c)), copy(w, 1, (x, y), 1 - c, (x, y, c))]
        return sends, arrivals, own

    return _Job(shards, [jax.ShapeDtypeStruct((N_CHIPS,) + s.shape, s.dtype) for s in shards], 2 * n, copies,
                n_local=n)


def _gather_far_job(stacked):
    n = len(stacked)
    halves = [s.shape[1] // 2 for s in stacked]

    def copies(ins, outs, send, recv, local):
        del ins, local
        x, y, c, _ = _place()
        near, far = _near_far(x, y, c)

        def copy(w, k, chip):
            blk = outs[w].at[_chip_index(*chip), pl.ds(c * halves[w], halves[w]), :]
            return pltpu.make_async_remote_copy(
                src_ref=blk, dst_ref=blk, send_sem=send.at[2 * w + k], recv_sem=recv.at[2 * w + k],
                device_id=(*far, c), device_id_type=MESH)

        sends = [copy(w, k, chip) for w in range(n) for k, chip in enumerate(((x, y), near))]
        arrivals = [copy(w, k, chip) for w in range(n) for k, chip in enumerate((far, (1 - x, 1 - y)))]
        return sends, arrivals, []

    return _Job(stacked, [jax.ShapeDtypeStruct(s.shape, s.dtype) for s in stacked], 2 * n, copies,
                aliases={w: w for w in range(n)})


def _gather_pass_job(stacked):
    n = len(stacked)
    halves = [s.shape[1] // 2 for s in stacked]

    def copies(ins, outs, send, recv, local):
        del ins, local
        x, y, c, chips = _place()

        def copy(w, j, chip, pc, to):
            blk = outs[w].at[_chip_index(*chip), pl.ds(pc * halves[w], halves[w]), :]
            return pltpu.make_async_remote_copy(
                src_ref=blk, dst_ref=blk, send_sem=send.at[3 * w + j], recv_sem=recv.at[3 * w + j], device_id=to,
                device_id_type=MESH)

        sends = [copy(w, j, chip, c, (x, y, 1 - c)) for w in range(n) for j, chip in enumerate(chips)]
        arrivals = [copy(w, j, chip, 1 - c, (x, y, c)) for w in range(n) for j, chip in enumerate(chips)]
        return sends, arrivals, []

    return _Job(stacked, [jax.ShapeDtypeStruct(s.shape, s.dtype) for s in stacked], 3 * n, copies,
                aliases={w: w for w in range(n)})


def _gather_small_job(block):
    def copies(ins, outs, send, recv, local):
        x, y, c, chips = _place()

        def copy(j, chip_from, to):
            return pltpu.make_async_remote_copy(
                src_ref=ins[0], dst_ref=outs[0].at[_chip_index(*chip_from)], send_sem=send.at[j],
                recv_sem=recv.at[j], device_id=to, device_id_type=MESH)

        own = [pltpu.make_async_copy(ins[0], outs[0].at[_chip_index(x, y)], local.at[0])]
        sends = [copy(j, (x, y), (*chip, c)) for j, chip in enumerate(chips)]
        arrivals = [copy(j, chip, (x, y, c)) for j, chip in enumerate(chips)]
        return sends, arrivals, own

    return _Job([block], [jax.ShapeDtypeStruct((N_CHIPS,) + block.shape, block.dtype)], 3, copies, n_local=1)


def _pair_send_job(grads):
    n = len(grads)
    halves = [g.shape[1] // 2 for g in grads]

    def copies(ins, outs, send, recv, local):
        del local
        x, y, c, _ = _place()
        sends = [pltpu.make_async_remote_copy(
            src_ref=ins[w].at[:, pl.ds((1 - c) * halves[w], halves[w]), :], dst_ref=outs[w], send_sem=send.at[w],
            recv_sem=recv.at[w], device_id=(x, y, 1 - c), device_id_type=MESH) for w in range(n)]
        return sends, sends, []

    return _Job(grads, [jax.ShapeDtypeStruct((N_CHIPS, h, g.shape[2]), g.dtype) for g, h in zip(grads, halves)], n,
                copies)


def _row_block(rows, limit=256):
    return min(rows, limit)


def _pair_add(name, core, mine, theirs):
    _, _, h, cols = mine.shape
    rb = _row_block(h, 512)

    def body(core_ref, a_ref, b_ref, o_ref):
        del core_ref
        o_ref[...] = (a_ref[...].astype(F32) + b_ref[...].astype(F32)).astype(BF16)

    return pl.pallas_call(
        body, name=name,
        grid_spec=pltpu.PrefetchScalarGridSpec(
            num_scalar_prefetch=1, grid=(N_CHIPS, h // rb),
            in_specs=[pl.BlockSpec((None, None, rb, cols), lambda k, r, core_ref: (k, core_ref[0], r, 0)),
                      pl.BlockSpec((None, rb, cols), lambda k, r, core_ref: (k, r, 0))],
            out_specs=pl.BlockSpec((None, rb, cols), lambda k, r, core_ref: (k, r, 0))),
        out_shape=jax.ShapeDtypeStruct(theirs.shape, BF16),
        compiler_params=_params(2),
    )(core, mine, theirs)


def _chip_exchange_job(sums):
    n = len(sums)

    def copies(ins, outs, send, recv, local):
        del local
        _, _, c, chips = _place()
        sends = [pltpu.make_async_remote_copy(
            src_ref=ins[w].at[_chip_index(*chip)], dst_ref=outs[w].at[j], send_sem=send.at[3 * w + j],
            recv_sem=recv.at[3 * w + j], device_id=(*chip, c), device_id_type=MESH)
            for w in range(n) for j, chip in enumerate(chips)]
        return sends, sends, []

    return _Job(sums, [jax.ShapeDtypeStruct((N_CHIPS - 1,) + s.shape[1:], s.dtype) for s in sums], 3 * n, copies)


def _chip_sum(name, place, mine, theirs):
    _, h, cols = mine.shape
    rb = _row_block(h, 512)

    def body(place_ref, p_ref, q_ref, o_ref):
        del place_ref
        acc = p_ref[...].astype(F32)
        for j in range(N_CHIPS - 1):
            acc = acc + q_ref[j].astype(F32)
        o_ref[...] = acc

    return pl.pallas_call(
        body, name=name,
        grid_spec=pltpu.PrefetchScalarGridSpec(
            num_scalar_prefetch=1, grid=(h // rb,),
            in_specs=[pl.BlockSpec((None, rb, cols), lambda r, place_ref: (place_ref[0], r, 0)),
                      pl.BlockSpec((N_CHIPS - 1, rb, cols), lambda r, place_ref: (0, r, 0))],
            out_specs=pl.BlockSpec((None, rb, cols), lambda r, place_ref: (place_ref[1], r, 0))),
        out_shape=jax.ShapeDtypeStruct((2, h, cols), F32),
        compiler_params=_params(),
    )(place, mine, theirs)


def _share_job(bufs):
    n = len(bufs)

    def copies(ins, outs, send, recv, local):
        del ins, local
        x, y, c, _ = _place()

        def copy(w, half):
            return pltpu.make_async_remote_copy(
                src_ref=outs[w].at[half], dst_ref=outs[w].at[half], send_sem=send.at[w], recv_sem=recv.at[w],
                device_id=(x, y, 1 - c), device_id_type=MESH)

        return [copy(w, c) for w in range(n)], [copy(w, 1 - c) for w in range(n)], []

    return _Job(bufs, [jax.ShapeDtypeStruct(b.shape, b.dtype) for b in bufs], n, copies,
                aliases={w: w for w in range(n)})


SMALL_ROWS = 24
ROW_G1, ROW_CW, ROW_CB, ROW_BR, ROW_BI, ROW_LAM, ROW_LG, ROW_LB, ROW_G2, ROW_G3, ROW_LOSS, ROW_BS = (
    0, 1, 5, 6, 7, 8, 9, 10, 11, 12, 13, 16)
N_DEV = 8


def _pack_small(dcw, dcb, dbr, dbi, dlam, dlg, dlb, dg2, dg3, loss, dbs):
    def body(dcw_ref, dcb_ref, dbr_ref, dbi_ref, dlam_ref, dlg_ref, dlb_ref, dg2_ref, dg3_ref, loss_ref, dbs_ref, out):
        out[...] = jnp.zeros((SMALL_ROWS, D_MODEL), F32)
        for row, ref in ((ROW_CB, dcb_ref), (ROW_BR, dbr_ref), (ROW_BI, dbi_ref), (ROW_LAM, dlam_ref),
                         (ROW_LG, dlg_ref), (ROW_LB, dlb_ref), (ROW_G2, dg2_ref), (ROW_G3, dg3_ref)):
            out[row:row + 1, :] = ref[...]
        out[ROW_CW:ROW_CW + CONV_WIDTH, :] = dcw_ref[0:CONV_WIDTH, :]
        out[ROW_LOSS:ROW_LOSS + 1, 0:128] = loss_ref[0:1, :]
        out[ROW_BS:ROW_BS + GROUPS, 0:128] = jnp.transpose(dbs_ref[...])[0:GROUPS, :]

    vm = pl.BlockSpec(memory_space=pltpu.VMEM)
    return pl.pallas_call(
        body, name="pack_small", in_specs=[vm] * 11, out_specs=vm,
        out_shape=jax.ShapeDtypeStruct((SMALL_ROWS, D_MODEL), F32),
    )(dcw, dcb, dbr, dbi, dlam, dlg, dlb, dg2, dg3, loss, dbs)


def _gather_all_job(blocks):
    n = len(blocks)
    flips = [(dx, dy, dc) for dx in (0, 1) for dy in (0, 1) for dc in (0, 1)][1:]

    def copies(ins, outs, send, recv, local):
        x, y, c, _ = _place()
        me = 4 * x + 2 * y + c
        sends, arrivals, own = [], [], []
        for w in range(n):
            own.append(pltpu.make_async_copy(ins[w], outs[w].at[me], local.at[w]))
            for k, (dx, dy, dc) in enumerate(flips):
                peer = (x ^ dx, y ^ dy, c ^ dc)
                sem = dict(send_sem=send.at[7 * w + k], recv_sem=recv.at[7 * w + k])
                sends.append(pltpu.make_async_remote_copy(
                    src_ref=ins[w], dst_ref=outs[w].at[me], device_id=peer, device_id_type=MESH, **sem))
                arrivals.append(pltpu.make_async_remote_copy(
                    src_ref=ins[w], dst_ref=outs[w].at[4 * peer[0] + 2 * peer[1] + peer[2]], device_id=peer,
                    device_id_type=MESH, **sem))
        return sends, arrivals, own

    return _Job(blocks, [jax.ShapeDtypeStruct((N_DEV,) + b.shape, b.dtype) for b in blocks], 7 * n, copies, n_local=n)


def _sum_small(vec_all, ws_all, dg1_all):
    def body(vec_ref, ws_ref, dg1_ref, vec_out, ws_out):
        vec, ws, dg1 = vec_ref[0], ws_ref[0], dg1_ref[0]
        for d in range(1, N_DEV):
            vec, ws, dg1 = vec + vec_ref[d], ws + ws_ref[d], dg1 + dg1_ref[d]
        vec_out[...] = vec
        vec_out[ROW_G1:ROW_G1 + 1, :] = dg1
        ws_out[...] = ws

    vm = pl.BlockSpec(memory_space=pltpu.VMEM)
    return pl.pallas_call(
        body, name="sum_small", in_specs=[vm] * 3, out_specs=[vm, vm],
        out_shape=[jax.ShapeDtypeStruct(vec_all.shape[1:], F32), jax.ShapeDtypeStruct(ws_all.shape[1:], F32)],
    )(vec_all, ws_all, dg1_all)


def _adamw_math(w, g, m, v):
    m = ADAM_B1 * m + (1.0 - ADAM_B1) * g
    v = ADAM_B2 * v + (1.0 - ADAM_B2) * (g * g)
    m_hat = m / (1.0 - ADAM_B1 ** ADAM_STEP)
    v_hat = v / (1.0 - ADAM_B2 ** ADAM_STEP)
    delta = (-ADAM_LR) * (m_hat / (jnp.sqrt(v_hat) + ADAM_EPS) + ADAM_WD * w)
    return delta, m, v


def _adamw(name, g, w, m, v, jobs=()):
    rows, cols = w.shape
    rb = _row_block(rows)

    def body(g_ref, w_ref, m_ref, v_ref, d_ref, nm_ref, nv_ref):
        d_ref[...], nm_ref[...], nv_ref[...] = _adamw_math(w_ref[...], g_ref[...], m_ref[...], v_ref[...])

    blk = pl.BlockSpec((rb, cols), lambda r: (r, 0))
    return _fused_call(
        body, jobs, name=name, grid=(rows // rb,), in_specs=[blk] * 4, out_specs=[blk] * 3,
        out_shape=[jax.ShapeDtypeStruct(w.shape, F32)] * 3, compiler_params=_params(),
    )(g, w, m, v)


def _adamw_small(grads, ws, ms, vs):
    n = len(grads)

    def body(*refs):
        g_refs, w_refs, m_refs, v_refs = refs[:n], refs[n:2 * n], refs[2 * n:3 * n], refs[3 * n:4 * n]
        outs = refs[4 * n:]
        for p in range(n):
            d, nm, nv = _adamw_math(w_refs[p][...], g_refs[p][...], m_refs[p][...], v_refs[p][...])
            outs[p][...] = d
            outs[n + p][...] = nm
            outs[2 * n + p][...] = nv

    vm = pl.BlockSpec(memory_space=pltpu.VMEM)
    shapes = [jax.ShapeDtypeStruct(w.shape, F32) for w in ws]
    out = pl.pallas_call(
        body, name="adamw_small", in_specs=[vm] * (4 * n), out_specs=[vm] * (3 * n), out_shape=shapes * 3,
    )(*grads, *ws, *ms, *vs)
    return out[:n], out[n:2 * n], out[2 * n:]


def _unstack_heads(w_st):
    per = HEAD_DIM // N_CHIPS
    return w_st.reshape(N_CHIPS, HEADS, per, HEAD_DIM).transpose(1, 0, 2, 3).reshape(HEADS, HEAD_DIM, HEAD_DIM)


def _stack_heads(w):
    per = HEAD_DIM // N_CHIPS
    return w.reshape(HEADS, N_CHIPS, per, HEAD_DIM).transpose(1, 0, 2, 3).reshape(N_CHIPS, HEADS * per, HEAD_DIM)


def kernel(x, norm_mix_g, w_in, conv_w, conv_b, w_rgate, b_rgate, w_igate, b_igate, lru_lambda, w_out_a, sgu_ln_g, sgu_ln_b, sgu_w_s, sgu_b_s, w_out_b, w_out, norm_mlp_g, w_up, w_down, norm_final_g, loss_target, m_norm_mix_g, m_w_in, m_conv_w, m_conv_b, m_w_rgate, m_b_rgate, m_w_igate, m_b_igate, m_lru_lambda, m_w_out_a, m_sgu_ln_g, m_sgu_ln_b, m_sgu_w_s, m_sgu_b_s, m_w_out_b, m_w_out, m_norm_mlp_g, m_w_up, m_w_down, m_norm_final_g, v_norm_mix_g, v_w_in, v_conv_w, v_conv_b, v_w_rgate, v_b_rgate, v_w_igate, v_b_igate, v_lru_lambda, v_w_out_a, v_sgu_ln_g, v_sgu_ln_b, v_sgu_w_s, v_sgu_b_s, v_w_out_b, v_w_out, v_norm_mlp_g, v_w_up, v_w_down, v_norm_final_g):
    chip = _chip_index(lax.axis_index("x"), lax.axis_index("y"))
    core = lax.axis_index("c")
    quarter_h = HEAD_DIM // N_CHIPS
    quarter_d = D_MODEL // N_CHIPS

    as_2d = lambda a: a.reshape(-1, a.shape[-1])
    big_w = [as_2d(w) for w in (w_in, w_rgate, w_igate, w_out_a, w_out_b, w_out, w_up, w_down)]
    big_m = [as_2d(w) for w in (m_w_in, m_w_rgate, m_w_igate, m_w_out_a, m_w_out_b, m_w_out, m_w_up, m_w_down)]
    big_v = [as_2d(w) for w in (v_w_in, v_w_rgate, v_w_igate, v_w_out_a, v_w_out_b, v_w_out, v_w_up, v_w_down)]

    packed = jnp.concatenate([conv_w[0], b_rgate[0], b_igate[0]], axis=1)
    packed = jnp.concatenate([packed, jnp.zeros_like(packed)], axis=0)
    s_in, s_r, s_i, s_oa, s_ob, s_out, s_up, s_down = [w.astype(BF16) for w in big_w]
    xs, target = x[0], loss_target[0]
    g3 = norm_final_g.reshape(1, D_MODEL)
    bias_s = jnp.broadcast_to(jnp.transpose(sgu_b_s[0])[:, :, None], (CHUNK, GROUPS, GROUP_DIM)).reshape(CHUNK, D_MODEL)
    core_arr = core.reshape(1).astype(jnp.int32)
    place = jnp.stack([chip, core]).astype(jnp.int32)
    quarter = lambda g: g.reshape(N_CHIPS, D_MODEL // N_CHIPS, D_MODEL)

    def pair_add(nm, g, from_sibling):
        return _pair_add("pair_add_" + nm, core_arr, g.reshape(N_CHIPS, 2, g.shape[1] // 2, g.shape[2]), from_sibling)

    def chip_sum(nm, pair, from_chips):
        return _chip_sum("chip_sum_" + nm, place, pair, from_chips)

    order = jnp.stack([chip, chip ^ 2, chip ^ 1, chip ^ 3]).astype(jnp.int32)
    (z, n1, (w_in_st, wr_st, wi_st)), ((packed_all,), late) = _fwd_in(
        xs, norm_mix_g, [s_in, s_r, s_i], order,
        jobs=[_gather_small_job(packed), _gather_near_job([s_oa, s_ob, s_out, s_up, s_down])])
    pick = lambda lo, hi: packed_all[:, :HEADS, lo:hi].transpose(1, 0, 2).reshape(HEADS, -1)
    conv_w_full = pick(0, quarter_d)
    br_full = pick(quarter_d, quarter_d + quarter_h).reshape(1, D_MODEL)
    bi_full = pick(quarter_d + quarter_h, quarter_d + 2 * quarter_h).reshape(1, D_MODEL)
    wr, wi = _unstack_heads(wr_st), _unstack_heads(wi_st)
    lru = (conv_w_full, conv_b, wr, br_full, wi, bi_full, lru_lambda)
    sgu = (sgu_ln_g, sgu_ln_b, sgu_w_s[0], bias_s)

    (ya, *saved), (late,) = _fwd_lru(z, *lru, jobs=[_gather_far_job(late)])
    yb, (late,) = _fwd_sgu(z, *sgu, jobs=[_gather_pass_job(late)])
    w_oa, w_ob, w_o = [w.reshape(D_MODEL, D_MODEL) for w in late[:3]]
    w_up_st, w_dn = late[3], late[4].reshape(D_FF, D_MODEL)
    (pa, pb, h1, n2), _ = _fwd_merge(ya, yb, z, xs, w_oa, w_ob, w_o, norm_mlp_g)
    (up, act, dh2, dh2b, loss_part, dg3), _ = _fwd_mlp(n2, h1, target, w_up_st, w_dn, g3)

    (dup, dh1, dg2), _ = _bwd_mlp(dh2, dh2b, up, h1, w_up_st, w_dn, norm_mlp_g)
    d_up, _ = _weight_grad("dw_up", n2, dup, N_CHIPS, False, True, D_MODEL)
    d_down, ((r_up,),) = _weight_grad("dw_down", act, dh2b, N_CHIPS, True, False, D_MODEL,
                                      jobs=[_pair_send_job([d_up])])
    p_up = pair_add("w_up", d_up, r_up)
    (dz, dya, dyb, merged, dpa, dpb, dh1b), ((r_down,), (q_up,)) = _bwd_merge(
        dh1, pa, pb, z, w_oa, w_ob, w_o, jobs=[_pair_send_job([d_down]), _chip_exchange_job([p_up])])
    p_down = pair_add("w_down", d_down, r_down)
    half_up = chip_sum("w_up", p_up, q_up)
    d_out, ((full_up,),) = _weight_grad("dw_out", merged, dh1b, 1, False, False, D_MODEL, jobs=[_share_job([half_up])])
    d_oa, _ = _weight_grad("dw_out_a", ya, dpa, 1, False, False, D_MODEL)
    d_ob, _ = _weight_grad("dw_out_b", yb, dpb, 1, False, False, D_MODEL)
    mids = [quarter(d_oa), quarter(d_ob), quarter(d_out)]
    (dz, dlg, dlb, dws, dbs), ((q_down,), r_mids) = _bwd_sgu(
        dz, dyb, z, *sgu, jobs=[_chip_exchange_job([p_down]), _pair_send_job(mids)])
    mid_names = ("w_out_a", "w_out_b", "w_out")
    p_mids = [pair_add(nm, g, r) for nm, g, r in zip(mid_names, mids, r_mids)]
    half_down = chip_sum("w_down", p_down, q_down)
    (dz, dcw, dcb, dwr, dbr, dwi, dbi, dlam), (q_mids, (full_down,)) = _bwd_lru(
        dz, dya, z, *saved, conv_w_full, wr, wi, lru_lambda, jobs=[_chip_exchange_job(p_mids), _share_job([half_down])])
    half_mids = [chip_sum(nm, p, q) for nm, p, q in zip(mid_names, p_mids, q_mids)]
    gates = [_stack_heads(dwr).astype(BF16), _stack_heads(dwi).astype(BF16)]
    small = _pack_small(dcw, dcb, dbr, dbi, dlam, dlg, dlb, dg2, dg3, loss_part, dbs)
    d_in, (full_mids, r_gates, (vec_all, ws_all)) = _weight_grad(
        "dw_in", n1, dz, N_CHIPS, False, True, IN_SHARD,
        jobs=[_share_job(half_mids), _pair_send_job(gates), _gather_all_job([small, dws])])
    names = ("w_in", "w_rgate", "w_igate", "w_out_a", "w_out_b", "w_out", "w_up", "w_down")
    adam_args = {nm: (w, m, v) for nm, w, m, v in zip(names, big_w, big_m, big_v)}

    def adamw(nm, g, jobs=()):
        w, m, v = adam_args[nm]
        g = g.reshape(w.shape)
        return (g,) + tuple(x for x in _adamw("adamw_" + nm, g, w, m, v, jobs))

    (r_in,), = _comm_call("send_w_in", [_pair_send_job([d_in])])
    last_names = ("w_in", "w_rgate", "w_igate")
    p_last = [pair_add(nm, g, r) for nm, g, r in zip(last_names, [d_in] + gates, [r_in] + r_gates)]
    (grad_x, dg1), (q_last,) = _bwd_in(dz, xs, dh1, w_in_st, norm_mix_g, jobs=[_chip_exchange_job(p_last)])
    half_last = [chip_sum(nm, p, q) for nm, p, q in zip(last_names, p_last, q_last)]
    full_last, (dg1_all,) = _comm_call("share_last", [_share_job(half_last), _gather_all_job([dg1])])
    full, big_out = [], []
    for nm, f in zip(names, full_last + full_mids + [full_up, full_down]):
        g, out, _ = adamw(nm, f)
        full.append(g)
        big_out.append(out)

    vec, ws_sum = _sum_small(vec_all, ws_all, dg1_all)
    row = lambda r: vec[r:r + 1]
    shard = lambda a, width: lax.dynamic_slice_in_dim(a, chip * width, width, axis=1)
    g_small = dict(
        norm_mix_g=row(ROW_G1), conv_w=shard(vec[ROW_CW:ROW_CW + CONV_WIDTH], quarter_d), conv_b=row(ROW_CB),
        b_rgate=shard(row(ROW_BR).reshape(HEADS, HEAD_DIM), quarter_h),
        b_igate=shard(row(ROW_BI).reshape(HEADS, HEAD_DIM), quarter_h), lru_lambda=row(ROW_LAM),
        sgu_ln_g=row(ROW_LG), sgu_ln_b=row(ROW_LB),
        sgu_w_s=ws_sum.reshape(CHUNK, GROUPS, CHUNK).transpose(1, 0, 2).reshape(GROUPS * CHUNK, CHUNK),
        sgu_b_s=vec[ROW_BS:ROW_BS + GROUPS, 0:CHUNK], norm_mlp_g=row(ROW_G2), norm_final_g=row(ROW_G3))
    loss = vec[ROW_LOSS, 0]
    small_names = list(g_small)
    given = dict(
        norm_mix_g=(norm_mix_g, m_norm_mix_g, v_norm_mix_g), conv_w=(conv_w, m_conv_w, v_conv_w),
        conv_b=(conv_b, m_conv_b, v_conv_b), b_rgate=(b_rgate, m_b_rgate, v_b_rgate),
        b_igate=(b_igate, m_b_igate, v_b_igate), lru_lambda=(lru_lambda, m_lru_lambda, v_lru_lambda),
        sgu_ln_g=(sgu_ln_g, m_sgu_ln_g, v_sgu_ln_g), sgu_ln_b=(sgu_ln_b, m_sgu_ln_b, v_sgu_ln_b),
        sgu_w_s=(sgu_w_s, m_sgu_w_s, v_sgu_w_s), sgu_b_s=(sgu_b_s, m_sgu_b_s, v_sgu_b_s),
        norm_mlp_g=(norm_mlp_g, m_norm_mlp_g, v_norm_mlp_g), norm_final_g=(norm_final_g, m_norm_final_g, v_norm_final_g))
    g2d = [g_small[nm] for nm in small_names]
    to2d = lambda a, g: a.reshape(g.shape)
    d_s, m_s, v_s = _adamw_small(
        g2d, *[[to2d(given[nm][q], g) for nm, g in zip(small_names, g2d)] for q in range(3)])

    shapes = dict(
        norm_mix_g=norm_mix_g, w_in=w_in, conv_w=conv_w, conv_b=conv_b, w_rgate=w_rgate, b_rgate=b_rgate,
        w_igate=w_igate, b_igate=b_igate, lru_lambda=lru_lambda, w_out_a=w_out_a, sgu_ln_g=sgu_ln_g,
        sgu_ln_b=sgu_ln_b, sgu_w_s=sgu_w_s, sgu_b_s=sgu_b_s, w_out_b=w_out_b, w_out=w_out, norm_mlp_g=norm_mlp_g,
        w_up=w_up, w_down=w_down, norm_final_g=norm_final_g)
    grads, deltas, new_m, new_v = {}, {}, {}, {}
    for nm, g, (d, nmom, nvar) in zip(names, full, big_out):
        grads[nm], deltas[nm], new_m[nm], new_v[nm] = g, d, nmom, nvar
    for p, nm in enumerate(small_names):
        grads[nm], deltas[nm], new_m[nm], new_v[nm] = g2d[p], d_s[p], m_s[p], v_s[p]
    order = list(shapes)
    out = [loss, grad_x[None]]
    for group in (grads, deltas, new_m, new_v):
        out += [group[nm].reshape(shapes[nm].shape) for nm in order]
    return tuple(out)
```

```python
import functools

import jax
import jax.numpy as jnp
from jax import lax
from jax.experimental import pallas as pl
from jax.experimental.pallas import tpu as pltpu

F32 = jnp.float32
BF16 = jnp.bfloat16
MESH = pl.DeviceIdType.MESH

D_MODEL = 1024
D_IN = 6 * D_MODEL
D_FF = 4 * D_MODEL
N_CHIPS = 4
IN_SHARD = D_IN // N_CHIPS
HEADS = 4
HEAD_DIM = D_MODEL // HEADS
GROUPS = 4
GROUP_DIM = D_MODEL // GROUPS
CHUNK = 128
CONV_WIDTH = 4
LRU_C = 8.0
NORM_EPS = 1e-6
LN_EPS = 1e-5

ADAM_LR = 0.001
ADAM_B1 = 0.9
ADAM_B2 = 0.999
ADAM_EPS = 1e-08
ADAM_WD = 0.01
ADAM_STEP = 10

SUBLANES = 8
MM_TILE = 512
SEQ_TILE = 256
DW_TILE = 2048
VMEM_LIMIT_BYTES = 56 * 1024 * 1024

GELU_K0 = 0.7978845608028654
GELU_K1 = 0.044715


def _params(n_grid_axes=1):
    return pltpu.CompilerParams(
        dimension_semantics=("arbitrary",) * n_grid_axes, vmem_limit_bytes=VMEM_LIMIT_BYTES)


def _resident(shape):
    nd = len(shape)
    return pl.BlockSpec(shape, lambda *_: (0,) * nd, pipeline_mode=pl.Buffered(1))


def _const(shape):
    nd = len(shape)
    return pl.BlockSpec(shape, lambda *_: (0,) * nd)


def _dot(a, b):
    return jnp.dot(a, b, preferred_element_type=F32)


def _dot_nt(a, b):
    return lax.dot_general(a, b, (((1,), (1,)), ((), ())), preferred_element_type=F32)


def _dot_tn(a, b):
    return lax.dot_general(a, b, (((0,), (0,)), ((), ())), preferred_element_type=F32)


def _gelu(x):
    t = jnp.tanh(GELU_K0 * x * (1.0 + GELU_K1 * x * x))
    return 0.5 * x * (1.0 + t)


def _gelu_and_grad(x):
    x2 = x * x
    t = jnp.tanh(GELU_K0 * x * (1.0 + GELU_K1 * x2))
    g = 0.5 * x * (1.0 + t)
    dg = 0.5 * (1.0 + t) + 0.5 * x * (1.0 - t * t) * (GELU_K0 * (1.0 + 3.0 * GELU_K1 * x2))
    return g, dg


def _rms(x):
    r = lax.rsqrt(jnp.mean(x * x, axis=-1, keepdims=True) + NORM_EPS)
    return x * r, r


def _rms_bwd(dn, xhat, r):
    return r * (dn - xhat * jnp.mean(dn * xhat, axis=-1, keepdims=True))


def _col_sum(v):
    return jnp.sum(v, axis=0, keepdims=True)


def _shift_down(x, tail8, k):
    xs = pltpu.roll(x, k, 0)
    ts = pltpu.roll(tail8, k, 0)
    ridx = lax.broadcasted_iota(jnp.int32, tail8.shape, 0)
    head = jnp.where(ridx < k, ts, xs[0:SUBLANES])
    return jnp.concatenate([head, xs[SUBLANES:]], axis=0)


def _shift_up(x, head8, k):
    n = x.shape[0]
    xs = pltpu.roll(x, n - k, 0)
    hs = pltpu.roll(head8, SUBLANES - k, 0)
    ridx = lax.broadcasted_iota(jnp.int32, head8.shape, 0)
    last = jnp.where(ridx >= SUBLANES - k, hs, xs[n - SUBLANES:n])
    return jnp.concatenate([xs[:n - SUBLANES], last], axis=0)


def _scan_forward(a, b, carry):
    n, cols = a.shape
    groups = n // SUBLANES
    a = a.reshape(groups, SUBLANES, cols)
    b = b.reshape(groups, SUBLANES, cols)
    sub = lax.broadcasted_iota(jnp.int32, a.shape, 1)
    for s in (1, 2, 4):
        a_s = pltpu.roll(a, s, 1)
        b_s = pltpu.roll(b, s, 1)
        m = sub >= s
        b = jnp.where(m, a * b_s + b, b)
        a = jnp.where(m, a * a_s, a)
    out = []
    for g in range(groups):
        h = a[g] * carry + b[g]
        out.append(h)
        carry = h[SUBLANES - 1:SUBLANES]
    return jnp.concatenate(out, axis=0), carry


def _scan_backward(a, b, carry):
    n, cols = a.shape
    groups = n // SUBLANES
    a = a.reshape(groups, SUBLANES, cols)
    b = b.reshape(groups, SUBLANES, cols)
    sub = lax.broadcasted_iota(jnp.int32, a.shape, 1)
    for s in (1, 2, 4):
        a_s = pltpu.roll(a, SUBLANES - s, 1)
        b_s = pltpu.roll(b, SUBLANES - s, 1)
        m = sub < SUBLANES - s
        b = jnp.where(m, a * b_s + b, b)
        a = jnp.where(m, a * a_s, a)
    out = [None] * groups
    for g in reversed(range(groups)):
        h = a[g] * carry + b[g]
        out[g] = h
        carry = h[0:1]
    return jnp.concatenate(out, axis=0), carry


def _softplus_neg(lam):
    e = jnp.exp(-jnp.abs(lam))
    u = 1.0 + e
    log1p_e = jnp.where(u == 1.0, e, jnp.log(u) * (e / jnp.where(u == 1.0, 1.0, u - 1.0)))
    return jnp.maximum(-lam, 0.0) + log1p_e


def _lru_gates(xa, tail8, cw_ref, cb_ref, wr_ref, br_ref, wi_ref, bi_ref, lam_ref):
    cw = cw_ref[...]
    xc = cb_ref[...] + cw[0:1] * xa
    for k in range(1, CONV_WIDTH):
        xc = xc + cw[k:k + 1] * _shift_down(xa, tail8, k)
    xcb = xc.astype(BF16)
    pre_r, pre_i = [], []
    for h in range(HEADS):
        cols = slice(h * HEAD_DIM, (h + 1) * HEAD_DIM)
        pre_r.append(_dot(xcb[:, cols], wr_ref[h]))
        pre_i.append(_dot(xcb[:, cols], wi_ref[h]))
    r = jax.nn.sigmoid(jnp.concatenate(pre_r, axis=1) + br_ref[...])
    ig = jax.nn.sigmoid(jnp.concatenate(pre_i, axis=1) + bi_ref[...])
    _, a, mult = _decay(r, lam_ref)
    return xc, r, ig, a, mult


def _decay(r, lam_ref):
    sp = _softplus_neg(lam_ref[...])
    log_a = ((-LRU_C) * sp) * r
    a = jnp.exp(log_a)
    th = jnp.tanh(log_a)
    return sp, a, jnp.sqrt((-2.0 * th) / (1.0 - th))


class _Job:
    def __init__(self, inputs, out_shape, n_sem, copies, aliases=None, n_local=0):
        self.inputs, self.out_shape, self.n_sem, self.copies = list(inputs), list(out_shape), n_sem, copies
        self.aliases, self.n_local = dict(aliases or {}), n_local


def _fused_call(body, jobs, *, name, grid, in_specs, out_specs, out_shape, scratch_shapes=(),
                input_output_aliases=None, compiler_params=None, n_prefetch=0, jobs_start_after=None):
    single = not isinstance(out_shape, (list, tuple))
    out_specs = [out_specs] if single else list(out_specs)
    out_shape = [out_shape] if single else list(out_shape)
    n_scr = len(scratch_shapes)
    in_specs, scratch_shapes = list(in_specs), list(scratch_shapes)
    n_in, n_out = len(in_specs), len(out_shape)
    aliases = dict(input_output_aliases or {})
    in_at, out_at = [], []
    for job in jobs:
        in_at.append(len(in_specs))
        out_at.append(len(out_shape))
        for i, o in job.aliases.items():
            aliases[n_prefetch + len(in_specs) + i] = len(out_shape) + o
        in_specs += [ANY] * len(job.inputs)
        out_specs += [ANY] * len(job.out_shape)
        out_shape += job.out_shape
        scratch_shapes += [pltpu.SemaphoreType.DMA((job.n_sem,)), pltpu.SemaphoreType.DMA((job.n_sem,)),
                           pltpu.SemaphoreType.DMA((max(job.n_local, 1),))]
    n_in_all, n_out_all = len(in_specs), len(out_shape)

    def full_body(*refs):
        prefetch, refs = refs[:n_prefetch], refs[n_prefetch:]
        ins, outs, scr = refs[:n_in_all], refs[n_in_all:n_in_all + n_out_all], refs[n_in_all + n_out_all:]

        def copies(q):
            job = jobs[q]
            return job.copies(ins[in_at[q]:in_at[q] + len(job.inputs)], outs[out_at[q]:out_at[q] + len(job.out_shape)],
                              *scr[n_scr + 3 * q:n_scr + 3 * q + 3])

        def start():
            for q in range(len(jobs)):
                sends, _, local = copies(q)
                for cp in local + sends:
                    cp.start()

        def finish():
            every = [copies(q) for q in range(len(jobs))]
            for _, arrivals, _ in every:
                for cp in arrivals:
                    cp.wait_recv()
            for sends, _, local in every:
                for cp in sends:
                    cp.wait_send()
                for cp in local:
                    cp.wait()

        if not grid:
            start()
            finish()
            return
        ids = [pl.program_id(a) for a in range(len(grid))]
        at_step = lambda step: functools.reduce(jnp.logical_and, [i == k for i, k in zip(ids, step)])
        if jobs and jobs_start_after is None:
            pl.when(at_step((0,) * len(grid)))(start)
        body(*prefetch, *ins[:n_in], *outs[:n_out], *scr[:n_scr])
        if jobs and jobs_start_after is not None:
            pl.when(at_step(jobs_start_after))(start)
        if jobs:
            pl.when(functools.reduce(jnp.logical_and, [i == g - 1 for i, g in zip(ids, grid)]))(finish)

    if n_prefetch:
        layout = dict(grid_spec=pltpu.PrefetchScalarGridSpec(
            num_scalar_prefetch=n_prefetch, grid=grid, in_specs=in_specs, out_specs=out_specs,
            scratch_shapes=scratch_shapes))
    else:
        layout = dict(grid=grid, in_specs=in_specs, out_specs=out_specs, scratch_shapes=scratch_shapes)
    call = pl.pallas_call(
        full_body, name=name, out_shape=out_shape, input_output_aliases=aliases, compiler_params=compiler_params,
        **layout)

    def run(*args):
        res = call(*args, *[a for job in jobs for a in job.inputs])
        mine = res[0] if single else list(res[:n_out])
        return mine, [list(res[at:at + len(job.out_shape)]) for at, job in zip(out_at, jobs)]

    return run


def _fwd_in(x, g1, shards, order, jobs=()):
    t = x.shape[0]
    n_tiles = t // MM_TILE
    n = len(shards)
    halves = [s.shape[0] // 2 for s in shards]

    def body(order_ref, x_ref, g_ref, *refs):
        del order_ref
        ins, (z_ref, n_ref), outs = refs[:n], refs[n:n + 2], refs[n + 2:2 * n + 2]
        wbuf, nbuf, send, recv, local = refs[2 * n + 2:]
        s, i = pl.program_id(0), pl.program_id(1)
        x_, y_, c, chips = _place()
        k_me = _chip_index(x_, y_)

        def block(w, chip, pc):
            return outs[w].at[_chip_index(*chip), pl.ds(pc * halves[w], halves[w]), :]

        def over_ici(w, j, landing):
            return pltpu.make_async_remote_copy(
                src_ref=ins[w].at[pl.ds(c * halves[w], halves[w]), :],
                dst_ref=block(w, chips[j] if landing else (x_, y_), c), send_sem=send.at[6 * w + j],
                recv_sem=recv.at[6 * w + j], device_id=(*chips[j], c), device_id_type=MESH)

        def to_sibling(w, j, landing):
            blk = block(w, chips[j], 1 - c if landing else c)
            return pltpu.make_async_remote_copy(
                src_ref=blk, dst_ref=blk, send_sem=send.at[6 * w + 3 + j], recv_sem=recv.at[6 * w + 3 + j],
                device_id=(x_, y_, 1 - c), device_id_type=MESH)

        own = [pltpu.make_async_copy(wbuf, outs[0].at[k_me], local.at[0])]
        own += [pltpu.make_async_copy(ins[w], outs[w].at[k_me], local.at[w]) for w in range(1, n)]

        @pl.when((s == 0) & (i == 0))
        def _():
            for j in range(2):
                for w in range(n):
                    over_ici(w, j, False).start()
            load = pltpu.make_async_copy(ins[0], wbuf, local.at[n])
            load.start()
            load.wait()
            for cp in own:
                cp.start()

        for j in range(N_CHIPS - 1):
            @pl.when((s == j + 1) & (i == 0))
            def _(j=j):
                for w in range(n):
                    over_ici(w, j, True).wait_recv()
                for w in range(n):
                    to_sibling(w, j, False).start()
                if j == 0:
                    for w in range(n):
                        over_ici(w, 2, False).start()
                    own[0].wait()
                for w in range(n):
                    to_sibling(w, j, True).wait_recv()
                load = pltpu.make_async_copy(outs[0].at[_chip_index(*chips[j])], wbuf, local.at[n])
                load.start()
                load.wait()

        rows = pl.ds(pl.multiple_of(i * MM_TILE, MM_TILE), MM_TILE)

        @pl.when(s == 0)
        def _():
            xhat, _ = _rms(x_ref[...])
            nrm = (xhat * g_ref[...]).astype(BF16)
            nbuf[rows, :] = nrm
            n_ref[...] = nrm

        z_ref[...] = _dot(nbuf[rows, :], wbuf[...])

        @pl.when((s == N_CHIPS - 1) & (i == n_tiles - 1))
        def _():
            for j in range(N_CHIPS - 1):
                for w in range(n):
                    over_ici(w, j, False).wait_send()
                    to_sibling(w, j, False).wait_send()
            for cp in own[1:]:
                cp.wait()

    once = lambda s, i, order: (jnp.where(s == 0, i, n_tiles - 1), 0)
    (z, n1, *stacked), job_outs = _fused_call(
        body, jobs, name="fwd_in", grid=(N_CHIPS, n_tiles), n_prefetch=1,
        in_specs=[pl.BlockSpec((MM_TILE, D_MODEL), once), _const((1, D_MODEL))] + [ANY] * n,
        out_specs=[pl.BlockSpec((MM_TILE, IN_SHARD), lambda s, i, order: (i, order[s])),
                   pl.BlockSpec((MM_TILE, D_MODEL), once)] + [ANY] * n,
        out_shape=[jax.ShapeDtypeStruct((t, D_IN), F32), jax.ShapeDtypeStruct((t, D_MODEL), BF16)]
        + [jax.ShapeDtypeStruct((N_CHIPS,) + s.shape, s.dtype) for s in shards],
        scratch_shapes=[pltpu.VMEM(shards[0].shape, BF16), pltpu.VMEM((t, D_MODEL), BF16),
                        pltpu.SemaphoreType.DMA((6 * n,)),
                        pltpu.SemaphoreType.DMA((6 * n,)), pltpu.SemaphoreType.DMA((n + 1,))],
        compiler_params=_params(2), jobs_start_after=(1, 0),
    )(order, x, g1, *shards)
    return (z, n1, stacked), job_outs


def _fwd_lru(z, conv_w, conv_b, wr, br, wi, bi, lam, jobs=()):
    t = z.shape[0]

    def body(xa_ref, ga_ref, cw_ref, cb_ref, wr_ref, br_ref, wi_ref, bi_ref, lam_ref, ya_ref, h_ref, xc_ref, r_ref,
             ig_ref, tail_ref, carry_ref):
        @pl.when(pl.program_id(0) == 0)
        def _():
            tail_ref[...] = jnp.zeros_like(tail_ref)
            carry_ref[...] = jnp.zeros_like(carry_ref)

        xa = xa_ref[...]
        xc, r, ig, a, mult = _lru_gates(xa, tail_ref[...], cw_ref, cb_ref, wr_ref, br_ref, wi_ref, bi_ref, lam_ref)
        tail_ref[...] = xa[SEQ_TILE - SUBLANES:]
        xc_ref[...], r_ref[...], ig_ref[...] = xc, r, ig
        h, carry = _scan_forward(a, xc * ig * mult, carry_ref[...])
        carry_ref[...] = carry
        h_ref[...] = h
        ya_ref[...] = (h * _gelu(ga_ref[...])).astype(BF16)

    tile = lambda j: pl.BlockSpec((SEQ_TILE, D_MODEL), lambda i: (i, j))
    return _fused_call(
        body, jobs, name="fwd_lru", grid=(t // SEQ_TILE,),
        in_specs=[tile(0), tile(1), _const((CONV_WIDTH, D_MODEL)), _const((1, D_MODEL)),
                  _resident((HEADS, HEAD_DIM, HEAD_DIM)), _const((1, D_MODEL)),
                  _resident((HEADS, HEAD_DIM, HEAD_DIM)), _const((1, D_MODEL)), _const((1, D_MODEL))],
        out_specs=[tile(0)] * 5,
        out_shape=[jax.ShapeDtypeStruct((t, D_MODEL), BF16)] + [jax.ShapeDtypeStruct((t, D_MODEL), F32)] * 4,
        scratch_shapes=[pltpu.VMEM((SUBLANES, D_MODEL), F32), pltpu.VMEM((1, D_MODEL), F32)],
        compiler_params=_params(),
    )(z, z, conv_w, conv_b, wr, br, wi, bi, lam)


def _sgu_forward_parts(ub, vb, lg_ref, lb_ref):
    u, du = _gelu_and_grad(ub)
    vg, dvg = _gelu_and_grad(vb)
    mu = jnp.mean(vg, axis=-1, keepdims=True)
    d = vg - mu
    rstd = lax.rsqrt(jnp.mean(d * d, axis=-1, keepdims=True) + LN_EPS)
    vhat = d * rstd
    vn = (vhat * lg_ref[...] + lb_ref[...]).astype(BF16)
    return u, du, dvg, rstd, vhat, vn


def _causal_mask():
    rows = lax.broadcasted_iota(jnp.int32, (CHUNK, CHUNK), 0)
    cols = lax.broadcasted_iota(jnp.int32, (CHUNK, CHUNK), 1)
    return rows >= cols


def _fwd_sgu(z, ln_g, ln_b, w_s, bias_full, jobs=()):
    t = z.shape[0]

    def body(ub_ref, vb_ref, lg_ref, lb_ref, ws_ref, bias_ref, yb_ref):
        u, _, _, _, _, vn = _sgu_forward_parts(ub_ref[...], vb_ref[...], lg_ref, lb_ref)
        mask = _causal_mask()
        wm = [jnp.where(mask, ws_ref[g], 0.0).astype(BF16) for g in range(GROUPS)]
        for c in range(SEQ_TILE // CHUNK):
            rows = slice(c * CHUNK, (c + 1) * CHUNK)
            for g in range(GROUPS):
                cols = slice(g * GROUP_DIM, (g + 1) * GROUP_DIM)
                sp = _dot(wm[g], vn[rows, cols]) + bias_ref[:, cols]
                yb_ref[rows, cols] = (u[rows, cols] * sp).astype(BF16)

    tile = lambda j: pl.BlockSpec((SEQ_TILE, D_MODEL), lambda i: (i, j))
    return _fused_call(
        body, jobs, name="fwd_sgu", grid=(t // SEQ_TILE,),
        in_specs=[tile(2), tile(3), _const((1, D_MODEL)), _const((1, D_MODEL)),
                  _const((GROUPS, CHUNK, CHUNK)), _const((CHUNK, D_MODEL))],
        out_specs=tile(0),
        out_shape=jax.ShapeDtypeStruct((t, D_MODEL), BF16),
        compiler_params=_params(),
    )(z, z, ln_g, ln_b, w_s, bias_full)


def _fwd_merge(ya, yb, z, x, w_oa, w_ob, w_out, g2, jobs=()):
    t = x.shape[0]

    def body(ya_ref, yb_ref, m_ref, x_ref, woa_ref, wob_ref, wout_ref, g_ref, pa_ref, pb_ref, h1_ref, n2_ref):
        pa = _dot(ya_ref[...], woa_ref[...])
        pb = _dot(yb_ref[...], wob_ref[...])
        pa_ref[...] = pa
        pb_ref[...] = pb
        merged = jax.nn.sigmoid(m_ref[:, :D_MODEL]) * pa + jax.nn.sigmoid(m_ref[:, D_MODEL:]) * pb
        h1 = x_ref[...] + _dot(merged.astype(BF16), wout_ref[...])
        h1_ref[...] = h1
        xhat, _ = _rms(h1)
        n2_ref[...] = (xhat * g_ref[...]).astype(BF16)

    tile = pl.BlockSpec((MM_TILE, D_MODEL), lambda i: (i, 0))
    sq = _resident((D_MODEL, D_MODEL))
    return _fused_call(
        body, jobs, name="fwd_merge", grid=(t // MM_TILE,),
        in_specs=[tile, tile, pl.BlockSpec((MM_TILE, 2 * D_MODEL), lambda i: (i, 2)), tile, sq, sq, sq,
                  _const((1, D_MODEL))],
        out_specs=[tile, tile, tile, tile],
        out_shape=[jax.ShapeDtypeStruct((t, D_MODEL), F32)] * 3 + [jax.ShapeDtypeStruct((t, D_MODEL), BF16)],
        compiler_params=_params(),
    )(ya, yb, z, x, w_oa, w_ob, w_out, g2)


def _mlp(n2, h1, target, w_up_st, w_down, g2, g3, jobs=()):
    t = n2.shape[0]

    def body(n2_ref, h1_ref, tgt_ref, wup_ref, wdown_ref, g2_ref, g3_ref, act_ref, dup_ref, dh2b_ref, dh1_ref,
             loss_ref, dg3_ref, dg2_ref, relu_ref):
        @pl.when(pl.program_id(0) == 0)
        def _():
            for ref in (loss_ref, dg3_ref, dg2_ref):
                ref[...] = jnp.zeros_like(ref)

        n2 = n2_ref[...]
        h1 = h1_ref[...]
        h2 = h1
        for k in range(N_CHIPS):
            cols = slice(k * D_MODEL, (k + 1) * D_MODEL)
            r = jnp.maximum(_dot(n2, wup_ref[k]), 0.0)
            relu_ref[:, cols] = r
            act = (r * r).astype(BF16)
            act_ref[:, cols] = act
            h2 = h2 + _dot(act, wdown_ref[cols, :])
        xhat, r3 = _rms(h2)
        diff = xhat * g3_ref[...] - tgt_ref[...]
        sq = jnp.sum(diff * diff, axis=1, keepdims=True)
        loss_ref[...] = loss_ref[...] + (0.5 / D_MODEL) * jnp.sum(sq, axis=0, keepdims=True)
        dy = diff * (1.0 / D_MODEL)
        dg3_ref[...] = dg3_ref[...] + _col_sum(dy * xhat)
        dh2 = _rms_bwd(dy * g3_ref[...], xhat, r3)
        dh2b = dh2.astype(BF16)
        dh2b_ref[...] = dh2b
        dn2 = jnp.zeros((SEQ_TILE, D_MODEL), F32)
        for k in range(N_CHIPS):
            cols = slice(k * D_MODEL, (k + 1) * D_MODEL)
            dup = (_dot_nt(dh2b, wdown_ref[cols, :]) * (2.0 * relu_ref[:, cols])).astype(BF16)
            dup_ref[:, cols] = dup
            dn2 = dn2 + _dot_nt(dup, wup_ref[k])
        xhat, r2 = _rms(h1)
        dg2_ref[...] = dg2_ref[...] + _col_sum(dn2 * xhat)
        dh1_ref[...] = dh2 + _rms_bwd(dn2 * g2_ref[...], xhat, r2)

    tile = pl.BlockSpec((SEQ_TILE, D_MODEL), lambda i: (i, 0))
    wide = pl.BlockSpec((SEQ_TILE, D_FF), lambda i: (i, 0))
    vec = _const((1, D_MODEL))
    vec_shape = jax.ShapeDtypeStruct((1, D_MODEL), F32)
    return _fused_call(
        body, jobs, name="mlp", grid=(t // SEQ_TILE,),
        in_specs=[tile, tile, tile, _resident((N_CHIPS, D_MODEL, D_MODEL)), _resident((D_FF, D_MODEL)), vec, vec],
        out_specs=[wide, wide, tile, tile, _const((SUBLANES, 128)), vec, vec],
        out_shape=[jax.ShapeDtypeStruct((t, D_FF), BF16), jax.ShapeDtypeStruct((t, D_FF), BF16),
                   jax.ShapeDtypeStruct((t, D_MODEL), BF16), jax.ShapeDtypeStruct((t, D_MODEL), F32),
                   jax.ShapeDtypeStruct((SUBLANES, 128), F32), vec_shape, vec_shape],
        scratch_shapes=[pltpu.VMEM((SEQ_TILE, D_FF), F32)],
        compiler_params=_params(),
    )(n2, h1, target, w_up_st, w_down, g2, g3)


def _bwd_mix(dh1, pa, pb, z, h, xc, r, ig, w_oa, w_ob, w_out, ln_g, ln_b, w_s, bias_full, conv_w, wr, wi, lam, jobs=()):
    t = dh1.shape[0]
    n_tiles = t // SEQ_TILE
    per_tile = SEQ_TILE // SUBLANES

    def merge_part(dh1_ref, pa_ref, pb_ref, m_ref, woa_ref, wob_ref, wout_ref, dz_ref, dya_ref, dyb_ref, mg_ref,
                   dpa_ref, dpb_ref, dh1b_ref):
        dh1b = dh1_ref[...].astype(BF16)
        dh1b_ref[...] = dh1b
        dm = _dot_nt(dh1b, wout_ref[...])
        pa = pa_ref[...]
        pb = pb_ref[...]
        sa = jax.nn.sigmoid(m_ref[:, :D_MODEL])
        sb = jax.nn.sigmoid(m_ref[:, D_MODEL:])
        mg_ref[...] = (sa * pa + sb * pb).astype(BF16)
        dz_ref[:, :D_MODEL] = (dm * pa * sa * (1.0 - sa)).astype(BF16)
        dz_ref[:, D_MODEL:] = (dm * pb * sb * (1.0 - sb)).astype(BF16)
        dpa = (dm * sa).astype(BF16)
        dpb = (dm * sb).astype(BF16)
        dpa_ref[...] = dpa
        dpb_ref[...] = dpb
        dya_ref[...] = _dot_nt(dpa, woa_ref[...])
        dyb_ref[...] = _dot_nt(dpb, wob_ref[...])

    def sgu_part(dyb_ref, ub_ref, vb_ref, lg_ref, lb_ref, ws_ref, bias_ref, dz_ref, dlg_ref, dlb_ref, dws_ref, dbs_ref,
                 dvn_ref, dsp_acc):
        i = pl.program_id(0)

        @pl.when(i == 0)
        def _():
            dlg_ref[...] = jnp.zeros_like(dlg_ref)
            dlb_ref[...] = jnp.zeros_like(dlb_ref)
            dws_ref[...] = jnp.zeros_like(dws_ref)
            dsp_acc[...] = jnp.zeros_like(dsp_acc)

        u, du, dvg, rstd, vhat, vn = _sgu_forward_parts(ub_ref[...], vb_ref[...], lg_ref, lb_ref)
        dyb = dyb_ref[...]
        mask = _causal_mask()
        wm = [jnp.where(mask, ws_ref[g], 0.0).astype(BF16) for g in range(GROUPS)]
        for c in range(SEQ_TILE // CHUNK):
            rows = slice(c * CHUNK, (c + 1) * CHUNK)
            for g in range(GROUPS):
                cols = slice(g * GROUP_DIM, (g + 1) * GROUP_DIM)
                vn_blk = vn[rows, cols]
                sp = _dot(wm[g], vn_blk) + bias_ref[:, cols]
                dyb_blk = dyb[rows, cols]
                dz_ref[rows, cols] = (dyb_blk * sp * du[rows, cols]).astype(BF16)
                dsp = dyb_blk * u[rows, cols]
                dsp_acc[:, cols] = dsp_acc[:, cols] + dsp
                dspb = dsp.astype(BF16)
                dvn_ref[rows, cols] = _dot_tn(wm[g], dspb)
                wcols = slice(g * CHUNK, (g + 1) * CHUNK)
                dws_ref[:, wcols] = dws_ref[:, wcols] + jnp.where(mask, _dot_nt(dspb, vn_blk), 0.0)
        dvn = dvn_ref[...]
        dlg_ref[...] = dlg_ref[...] + _col_sum(dvn * vhat)
        dlb_ref[...] = dlb_ref[...] + _col_sum(dvn)
        dvhat = dvn * lg_ref[...]
        dvgel = rstd * (dvhat - jnp.mean(dvhat, axis=-1, keepdims=True)
                        - vhat * jnp.mean(dvhat * vhat, axis=-1, keepdims=True))
        dz_ref[:, D_MODEL:] = (dvgel * dvg).astype(BF16)

        @pl.when(i == n_tiles - 1)
        def _():
            lane = lax.broadcasted_iota(jnp.int32, (CHUNK, 128), 1)
            out = jnp.zeros((CHUNK, 128), F32)
            for g in range(GROUPS):
                s = jnp.sum(dsp_acc[:, g * GROUP_DIM:(g + 1) * GROUP_DIM], axis=1, keepdims=True)
                out = out + jnp.where(lane == g, s, 0.0)
            dbs_ref[...] = out

    def lru_part(dya_ref, xa_ref, ga_ref, h_ref, h_prev_ref, xc_ref, r_ref, ig_ref, cw_ref, wr_ref, wi_ref, lam_ref,
                 dz_ref, dcw_ref, dcb_ref, dwr_ref, dbr_ref, dwi_ref, dbi_ref, dlam_ref, lam_carry, dxc_head):
        i = pl.program_id(0)

        @pl.when(i == 0)
        def _():
            for ref in (dcw_ref, dcb_ref, dwr_ref, dbr_ref, dwi_ref, dbi_ref, dlam_ref, lam_carry, dxc_head):
                ref[...] = jnp.zeros_like(ref)

        first_tile = i == n_tiles - 1
        h_tail = jnp.where(first_tile, 0.0, h_prev_ref[...])
        xc, r, ig = xc_ref[...], r_ref[...], ig_ref[...]
        xcb = xc.astype(BF16)
        sp, a, mult = _decay(r, lam_ref)
        h = h_ref[...]
        h_prev = _shift_down(h, h_tail, 1)
        dya = dya_ref[...]
        gg, dgg = _gelu_and_grad(ga_ref[...])
        dz_ref[:, D_MODEL:] = (dya * h * dgg).astype(BF16)
        ones = jnp.ones((SUBLANES, D_MODEL), F32)
        lam_t, lam_first = _scan_backward(_shift_up(a, ones, 1), dya * gg, lam_carry[...])
        lam_carry[...] = a[0:1] * lam_first
        dmult = lam_t * xc * ig
        dla = lam_t * h_prev * a - dmult * (a * a) / mult
        dr = dla * ((-LRU_C) * sp)
        dlam_ref[...] = dlam_ref[...] + _col_sum(dla * r) * (LRU_C * jax.nn.sigmoid(-lam_ref[...]))
        dpr = dr * r * (1.0 - r)
        dpi = lam_t * xc * mult * ig * (1.0 - ig)
        dbr_ref[...] = dbr_ref[...] + _col_sum(dpr)
        dbi_ref[...] = dbi_ref[...] + _col_sum(dpi)
        dprb = dpr.astype(BF16)
        dpib = dpi.astype(BF16)
        dxc_gate = []
        for hd in range(HEADS):
            cols = slice(hd * HEAD_DIM, (hd + 1) * HEAD_DIM)
            dxc_gate.append(_dot_nt(dprb[:, cols], wr_ref[hd]) + _dot_nt(dpib[:, cols], wi_ref[hd]))
            dwr_ref[hd] = dwr_ref[hd] + _dot_tn(xcb[:, cols], dprb[:, cols])
            dwi_ref[hd] = dwi_ref[hd] + _dot_tn(xcb[:, cols], dpib[:, cols])
        dxc = lam_t * ig * mult + jnp.concatenate(dxc_gate, axis=1)
        dcb_ref[...] = dcb_ref[...] + _col_sum(dxc)
        cw = cw_ref[...]
        head = dxc_head[...]
        xa = xa_ref[...]
        dxa = cw[0:1] * dxc
        dcw_ref[0:1, :] = dcw_ref[0:1, :] + _col_sum(dxc * xa)
        for k in range(1, CONV_WIDTH):
            dxc_k = _shift_up(dxc, head, k)
            dxa = dxa + cw[k:k + 1] * dxc_k
            dcw_ref[k:k + 1, :] = dcw_ref[k:k + 1, :] + _col_sum(dxc_k * xa)
        dxc_head[...] = dxc[0:SUBLANES]
        dz_ref[:, :D_MODEL] = dxa.astype(BF16)

    def body(dh1_ref, pa_ref, pb_ref, z_ref, h_ref, h_prev_ref, xc_ref, r_ref, ig_ref, woa_ref, wob_ref, wout_ref,
             lg_ref, lb_ref, ws_ref, bias_ref, cw_ref, wr_ref, wi_ref, lam_ref, dz_ref, mg_ref, dpa_ref, dpb_ref,
             dh1b_ref, dlg_ref, dlb_ref, dws_ref, dbs_ref, dcw_ref, dcb_ref, dwr_ref, dbr_ref, dwi_ref, dbi_ref,
             dlam_ref, dya_ref, dyb_ref, dvn_ref, dsp_acc, lam_carry, dxc_head):
        def cols(ref, first, count):
            return ref.at[:, pl.ds(first * D_MODEL, count * D_MODEL)]

        merge_part(dh1_ref, pa_ref, pb_ref, cols(z_ref, 4, 2), woa_ref, wob_ref, wout_ref, cols(dz_ref, 4, 2), dya_ref,
                   dyb_ref, mg_ref, dpa_ref, dpb_ref, dh1b_ref)
        sgu_part(dyb_ref, cols(z_ref, 2, 1), cols(z_ref, 3, 1), lg_ref, lb_ref, ws_ref, bias_ref, cols(dz_ref, 2, 2),
                 dlg_ref, dlb_ref, dws_ref, dbs_ref, dvn_ref, dsp_acc)
        lru_part(dya_ref, cols(z_ref, 0, 1), cols(z_ref, 1, 1), h_ref, h_prev_ref, xc_ref, r_ref, ig_ref, cw_ref, wr_ref,
                 wi_ref, lam_ref, cols(dz_ref, 0, 2), dcw_ref, dcb_ref, dwr_ref, dbr_ref, dwi_ref, dbi_ref, dlam_ref,
                 lam_carry, dxc_head)

    rev = lambda i: n_tiles - 1 - i
    tile = pl.BlockSpec((SEQ_TILE, D_MODEL), lambda i: (rev(i), 0))
    row = pl.BlockSpec((SEQ_TILE, D_IN), lambda i: (rev(i), 0))
    prev8 = pl.BlockSpec((SUBLANES, D_MODEL), lambda i: (jnp.maximum(rev(i) * per_tile - 1, 0), 0))
    vec = _const((1, D_MODEL))
    sq = _resident((D_MODEL, D_MODEL))
    gate_w = _resident((HEADS, HEAD_DIM, HEAD_DIM))
    gate_acc = _const((HEADS, HEAD_DIM, HEAD_DIM))
    vec_shape = jax.ShapeDtypeStruct((1, D_MODEL), F32)
    gate_shape = jax.ShapeDtypeStruct((HEADS, HEAD_DIM, HEAD_DIM), F32)
    act_bf = jax.ShapeDtypeStruct((t, D_MODEL), BF16)
    return _fused_call(
        body, jobs, name="bwd_mix", grid=(n_tiles,),
        in_specs=[tile, tile, tile, row, tile, prev8, tile, tile, tile, sq, sq, sq, vec, vec,
                  _const((GROUPS, CHUNK, CHUNK)), _const((CHUNK, D_MODEL)), _const((CONV_WIDTH, D_MODEL)), gate_w, gate_w,
                  vec],
        out_specs=[row, tile, tile, tile, tile, vec, vec, _const((CHUNK, GROUPS * CHUNK)), _const((CHUNK, 128)),
                   _const((SUBLANES, D_MODEL)), vec, gate_acc, vec, gate_acc, vec, vec],
        out_shape=[jax.ShapeDtypeStruct((t, D_IN), BF16), act_bf, act_bf, act_bf, act_bf, vec_shape, vec_shape,
                   jax.ShapeDtypeStruct((CHUNK, GROUPS * CHUNK), F32), jax.ShapeDtypeStruct((CHUNK, 128), F32),
                   jax.ShapeDtypeStruct((SUBLANES, D_MODEL), F32), vec_shape, gate_shape, vec_shape, gate_shape,
                   vec_shape, vec_shape],
        scratch_shapes=[pltpu.VMEM((SEQ_TILE, D_MODEL), F32), pltpu.VMEM((SEQ_TILE, D_MODEL), F32),
                        pltpu.VMEM((SEQ_TILE, D_MODEL), F32), pltpu.VMEM((CHUNK, D_MODEL), F32),
                        pltpu.VMEM((1, D_MODEL), F32), pltpu.VMEM((SUBLANES, D_MODEL), F32)],
        compiler_params=_params(),
    )(dh1, pa, pb, z, h, h, xc, r, ig, w_oa, w_ob, w_out, ln_g, ln_b, w_s, bias_full, conv_w, wr, wi, lam)


def _bwd_in(dz, x, dh1, w_in_st, g1, jobs=()):
    t = x.shape[0]

    def body(dz_ref, x_ref, dh1_ref, w_ref, g_ref, dx_ref, dg1_ref):
        @pl.when(pl.program_id(0) == 0)
        def _():
            dg1_ref[...] = jnp.zeros_like(dg1_ref)

        dn1 = jnp.zeros((MM_TILE, D_MODEL), F32)
        for k in range(N_CHIPS):
            dn1 = dn1 + _dot_nt(dz_ref[:, k * IN_SHARD:(k + 1) * IN_SHARD], w_ref[k])
        xhat, r1 = _rms(x_ref[...])
        dg1_ref[...] = dg1_ref[...] + _col_sum(dn1 * xhat)
        dx_ref[...] = dh1_ref[...] + _rms_bwd(dn1 * g_ref[...], xhat, r1)

    tile = pl.BlockSpec((MM_TILE, D_MODEL), lambda i: (i, 0))
    return _fused_call(
        body, jobs, name="bwd_in", grid=(t // MM_TILE,),
        in_specs=[pl.BlockSpec((MM_TILE, D_IN), lambda i: (i, 0)), tile, tile,
                  _resident((N_CHIPS, D_MODEL, IN_SHARD)), _const((1, D_MODEL))],
        out_specs=[tile, _const((1, D_MODEL))],
        out_shape=[jax.ShapeDtypeStruct((t, D_MODEL), F32), jax.ShapeDtypeStruct((1, D_MODEL), F32)],
        compiler_params=_params(),
    )(dz, x, dh1, w_in_st, g1)


def _weight_grad(name, a, b, n_blocks, a_varies, b_varies, width, jobs=()):
    t = a.shape[0]
    rows = min(DW_TILE, t)
    n_t = t // rows

    def body(a_ref, b_ref, o_ref, acc_ref):
        s = pl.program_id(1)
        part = _dot_tn(a_ref[...], b_ref[...])

        @pl.when(s == 0)
        def _():
            acc_ref[...] = part

        @pl.when(s > 0)
        def _():
            acc_ref[...] = acc_ref[...] + part

        @pl.when(s == n_t - 1)
        def _():
            o_ref[...] = acc_ref[...].astype(BF16)

    return _fused_call(
        body, jobs, name=name, grid=(n_blocks, n_t),
        in_specs=[pl.BlockSpec((rows, D_MODEL), (lambda j, s: (s, j)) if a_varies else (lambda j, s: (s, 0))),
                  pl.BlockSpec((rows, width), (lambda j, s: (s, j)) if b_varies else (lambda j, s: (s, 0)))],
        out_specs=pl.BlockSpec((None, D_MODEL, width), lambda j, s: (j, 0, 0)),
        out_shape=jax.ShapeDtypeStruct((n_blocks, D_MODEL, width), BF16),
        scratch_shapes=[pltpu.VMEM((D_MODEL, width), F32)],
        compiler_params=_params(2),
    )(a, b)


def _place():
    x, y, c = lax.axis_index("x"), lax.axis_index("y"), lax.axis_index("c")
    other_chips = [(1 - x, y), (x, 1 - y), (1 - x, 1 - y)]
    return x, y, c, other_chips


def _chip_index(px, py):
    return 2 * px + py


ANY = pl.BlockSpec(memory_space=pl.ANY)


def _comm_call(name, jobs):
    return _fused_call(None, jobs, name=name, grid=(), in_specs=[], out_specs=[], out_shape=[])()[1]


def _near_far(x, y, c):
    return (x ^ (1 - c), y ^ c), (x ^ c, y ^ (1 - c))


def _gather_near_job(shards):
    n = len(shards)
    halves = [s.shape[0] // 2 for s in shards]

    def copies(ins, outs, send, recv, local):
        x, y, c, _ = _place()
        near, _ = _near_far(x, y, c)

        def block(w, chip, pc):
            return outs[w].at[_chip_index(*chip), pl.ds(pc * halves[w], halves[w]), :]

        def copy(w, k, chip, pc, to, src=None):
            return pltpu.make_async_remote_copy(
                src_ref=block(w, chip, pc) if src is None else src, dst_ref=block(w, chip, pc),
                send_sem=send.at[2 * w + k], recv_sem=recv.at[2 * w + k], device_id=to, device_id_type=MESH)

        sends, arrivals, own = [], [], []
        for w in range(n):
            src = ins[w].at[pl.ds(c * halves[w], halves[w]), :]
            own.append(pltpu.make_async_copy(src, block(w, (x, y), c), local.at[w]))
            sends += [copy(w, 0, (x, y), c, (*near, c), src), copy(w, 1, (x, y), c, (x, y, 1 - c), src)]
            arrivals += [copy(w, 0, near, c, (x, y, c)), copy(w, 1, (x, y), 1 - c, (x, y, c))]
        return sends, arrivals, own

    return _Job(shards, [jax.ShapeDtypeStruct((N_CHIPS,) + s.shape, s.dtype) for s in shards], 2 * n, copies,
                n_local=n)


def _gather_far_job(stacked):
    n = len(stacked)
    halves = [s.shape[1] // 2 for s in stacked]

    def copies(ins, outs, send, recv, local):
        del ins, local
        x, y, c, _ = _place()
        near, far = _near_far(x, y, c)

        def copy(w, k, chip):
            blk = outs[w].at[_chip_index(*chip), pl.ds(c * halves[w], halves[w]), :]
            return pltpu.make_async_remote_copy(
                src_ref=blk, dst_ref=blk, send_sem=send.at[2 * w + k], recv_sem=recv.at[2 * w + k],
                device_id=(*far, c), device_id_type=MESH)

        sends = [copy(w, k, chip) for w in range(n) for k, chip in enumerate(((x, y), near))]
        arrivals = [copy(w, k, chip) for w in range(n) for k, chip in enumerate((far, (1 - x, 1 - y)))]
        return sends, arrivals, []

    return _Job(stacked, [jax.ShapeDtypeStruct(s.shape, s.dtype) for s in stacked], 2 * n, copies,
                aliases={w: w for w in range(n)})


def _gather_pass_job(stacked):
    n = len(stacked)
    halves = [s.shape[1] // 2 for s in stacked]

    def copies(ins, outs, send, recv, local):
        del ins, local
        x, y, c, chips = _place()

        def copy(w, j, chip, pc, to):
            blk = outs[w].at[_chip_index(*chip), pl.ds(pc * halves[w], halves[w]), :]
            return pltpu.make_async_remote_copy(
                src_ref=blk, dst_ref=blk, send_sem=send.at[3 * w + j], recv_sem=recv.at[3 * w + j], device_id=to,
                device_id_type=MESH)

        sends = [copy(w, j, chip, c, (x, y, 1 - c)) for w in range(n) for j, chip in enumerate(chips)]
        arrivals = [copy(w, j, chip, 1 - c, (x, y, c)) for w in range(n) for j, chip in enumerate(chips)]
        return sends, arrivals, []

    return _Job(stacked, [jax.ShapeDtypeStruct(s.shape, s.dtype) for s in stacked], 3 * n, copies,
                aliases={w: w for w in range(n)})


def _gather_small_job(block):
    def copies(ins, outs, send, recv, local):
        x, y, c, chips = _place()

        def copy(j, chip_from, to):
            return pltpu.make_async_remote_copy(
                src_ref=ins[0], dst_ref=outs[0].at[_chip_index(*chip_from)], send_sem=send.at[j],
                recv_sem=recv.at[j], device_id=to, device_id_type=MESH)

        own = [pltpu.make_async_copy(ins[0], outs[0].at[_chip_index(x, y)], local.at[0])]
        sends = [copy(j, (x, y), (*chip, c)) for j, chip in enumerate(chips)]
        arrivals = [copy(j, chip, (x, y, c)) for j, chip in enumerate(chips)]
        return sends, arrivals, own

    return _Job([block], [jax.ShapeDtypeStruct((N_CHIPS,) + block.shape, block.dtype)], 3, copies, n_local=1)


def _pair_send_job(grads):
    n = len(grads)
    halves = [g.shape[1] // 2 for g in grads]

    def copies(ins, outs, send, recv, local):
        del local
        x, y, c, _ = _place()
        sends = [pltpu.make_async_remote_copy(
            src_ref=ins[w].at[:, pl.ds((1 - c) * halves[w], halves[w]), :], dst_ref=outs[w], send_sem=send.at[w],
            recv_sem=recv.at[w], device_id=(x, y, 1 - c), device_id_type=MESH) for w in range(n)]
        return sends, sends, []

    return _Job(grads, [jax.ShapeDtypeStruct((N_CHIPS, h, g.shape[2]), g.dtype) for g, h in zip(grads, halves)], n,
                copies)


def _row_block(rows, limit=256):
    return min(rows, limit)


def _pair_add(name, core, mine, theirs):
    _, _, h, cols = mine.shape
    rb = _row_block(h, 512)

    def body(core_ref, a_ref, b_ref, o_ref):
        del core_ref
        o_ref[...] = (a_ref[...].astype(F32) + b_ref[...].astype(F32)).astype(BF16)

    return pl.pallas_call(
        body, name=name,
        grid_spec=pltpu.PrefetchScalarGridSpec(
            num_scalar_prefetch=1, grid=(N_CHIPS, h // rb),
            in_specs=[pl.BlockSpec((None, None, rb, cols), lambda k, r, core_ref: (k, core_ref[0], r, 0)),
                      pl.BlockSpec((None, rb, cols), lambda k, r, core_ref: (k, r, 0))],
            out_specs=pl.BlockSpec((None, rb, cols), lambda k, r, core_ref: (k, r, 0))),
        out_shape=jax.ShapeDtypeStruct(theirs.shape, BF16),
        compiler_params=_params(2),
    )(core, mine, theirs)


def _chip_exchange_job(sums):
    n = len(sums)

    def copies(ins, outs, send, recv, local):
        del local
        _, _, c, chips = _place()
        sends = [pltpu.make_async_remote_copy(
            src_ref=ins[w].at[_chip_index(*chip)], dst_ref=outs[w].at[j], send_sem=send.at[3 * w + j],
            recv_sem=recv.at[3 * w + j], device_id=(*chip, c), device_id_type=MESH)
            for w in range(n) for j, chip in enumerate(chips)]
        return sends, sends, []

    return _Job(sums, [jax.ShapeDtypeStruct((N_CHIPS - 1,) + s.shape[1:], s.dtype) for s in sums], 3 * n, copies)


def _chip_sum(name, place, mine, theirs):
    _, h, cols = mine.shape
    rb = _row_block(h, 512)

    def body(place_ref, p_ref, q_ref, o_ref):
        del place_ref
        acc = p_ref[...].astype(F32)
        for j in range(N_CHIPS - 1):
            acc = acc + q_ref[j].astype(F32)
        o_ref[...] = acc

    return pl.pallas_call(
        body, name=name,
        grid_spec=pltpu.PrefetchScalarGridSpec(
            num_scalar_prefetch=1, grid=(h // rb,),
            in_specs=[pl.BlockSpec((None, rb, cols), lambda r, place_ref: (place_ref[0], r, 0)),
                      pl.BlockSpec((N_CHIPS - 1, rb, cols), lambda r, place_ref: (0, r, 0))],
            out_specs=pl.BlockSpec((None, rb, cols), lambda r, place_ref: (place_ref[1], r, 0))),
        out_shape=jax.ShapeDtypeStruct((2, h, cols), F32),
        compiler_params=_params(),
    )(place, mine, theirs)


def _share_job(bufs):
    n = len(bufs)

    def copies(ins, outs, send, recv, local):
        del ins, local
        x, y, c, _ = _place()

        def copy(w, half):
            return pltpu.make_async_remote_copy(
                src_ref=outs[w].at[half], dst_ref=outs[w].at[half], send_sem=send.at[w], recv_sem=recv.at[w],
                device_id=(x, y, 1 - c), device_id_type=MESH)

        return [copy(w, c) for w in range(n)], [copy(w, 1 - c) for w in range(n)], []

    return _Job(bufs, [jax.ShapeDtypeStruct(b.shape, b.dtype) for b in bufs], n, copies,
                aliases={w: w for w in range(n)})


SMALL_ROWS = 24
ROW_G1, ROW_CW, ROW_CB, ROW_BR, ROW_BI, ROW_LAM, ROW_LG, ROW_LB, ROW_G2, ROW_G3, ROW_LOSS, ROW_BS = (
    0, 1, 5, 6, 7, 8, 9, 10, 11, 12, 13, 16)
N_DEV = 8


def _pack_small(dcw, dcb, dbr, dbi, dlam, dlg, dlb, dg2, dg3, loss, dbs):
    def body(dcw_ref, dcb_ref, dbr_ref, dbi_ref, dlam_ref, dlg_ref, dlb_ref, dg2_ref, dg3_ref, loss_ref, dbs_ref, out):
        out[...] = jnp.zeros((SMALL_ROWS, D_MODEL), F32)
        for row, ref in ((ROW_CB, dcb_ref), (ROW_BR, dbr_ref), (ROW_BI, dbi_ref), (ROW_LAM, dlam_ref),
                         (ROW_LG, dlg_ref), (ROW_LB, dlb_ref), (ROW_G2, dg2_ref), (ROW_G3, dg3_ref)):
            out[row:row + 1, :] = ref[...]
        out[ROW_CW:ROW_CW + CONV_WIDTH, :] = dcw_ref[0:CONV_WIDTH, :]
        out[ROW_LOSS:ROW_LOSS + 1, 0:128] = loss_ref[0:1, :]
        out[ROW_BS:ROW_BS + GROUPS, 0:128] = jnp.transpose(dbs_ref[...])[0:GROUPS, :]

    vm = pl.BlockSpec(memory_space=pltpu.VMEM)
    return pl.pallas_call(
        body, name="pack_small", in_specs=[vm] * 11, out_specs=vm,
        out_shape=jax.ShapeDtypeStruct((SMALL_ROWS, D_MODEL), F32),
    )(dcw, dcb, dbr, dbi, dlam, dlg, dlb, dg2, dg3, loss, dbs)


def _gather_all_job(blocks):
    n = len(blocks)
    flips = [(dx, dy, dc) for dx in (0, 1) for dy in (0, 1) for dc in (0, 1)][1:]

    def copies(ins, outs, send, recv, local):
        x, y, c, _ = _place()
        me = 4 * x + 2 * y + c
        sends, arrivals, own = [], [], []
        for w in range(n):
            own.append(pltpu.make_async_copy(ins[w], outs[w].at[me], local.at[w]))
            for k, (dx, dy, dc) in enumerate(flips):
                peer = (x ^ dx, y ^ dy, c ^ dc)
                sem = dict(send_sem=send.at[7 * w + k], recv_sem=recv.at[7 * w + k])
                sends.append(pltpu.make_async_remote_copy(
                    src_ref=ins[w], dst_ref=outs[w].at[me], device_id=peer, device_id_type=MESH, **sem))
                arrivals.append(pltpu.make_async_remote_copy(
                    src_ref=ins[w], dst_ref=outs[w].at[4 * peer[0] + 2 * peer[1] + peer[2]], device_id=peer,
                    device_id_type=MESH, **sem))
        return sends, arrivals, own

    return _Job(blocks, [jax.ShapeDtypeStruct((N_DEV,) + b.shape, b.dtype) for b in blocks], 7 * n, copies, n_local=n)


def _sum_small(vec_all, ws_all, dg1_all):
    def body(vec_ref, ws_ref, dg1_ref, vec_out, ws_out):
        vec, ws, dg1 = vec_ref[0], ws_ref[0], dg1_ref[0]
        for d in range(1, N_DEV):
            vec, ws, dg1 = vec + vec_ref[d], ws + ws_ref[d], dg1 + dg1_ref[d]
        vec_out[...] = vec
        vec_out[ROW_G1:ROW_G1 + 1, :] = dg1
        ws_out[...] = ws

    vm = pl.BlockSpec(memory_space=pltpu.VMEM)
    return pl.pallas_call(
        body, name="sum_small", in_specs=[vm] * 3, out_specs=[vm, vm],
        out_shape=[jax.ShapeDtypeStruct(vec_all.shape[1:], F32), jax.ShapeDtypeStruct(ws_all.shape[1:], F32)],
    )(vec_all, ws_all, dg1_all)


def _adamw_math(w, g, m, v):
    m = ADAM_B1 * m + (1.0 - ADAM_B1) * g
    v = ADAM_B2 * v + (1.0 - ADAM_B2) * (g * g)
    m_hat = m / (1.0 - ADAM_B1 ** ADAM_STEP)
    v_hat = v / (1.0 - ADAM_B2 ** ADAM_STEP)
    delta = (-ADAM_LR) * (m_hat / (jnp.sqrt(v_hat) + ADAM_EPS) + ADAM_WD * w)
    return delta, m, v


def _adamw(name, g, w, m, v, jobs=()):
    rows, cols = w.shape
    rb = _row_block(rows)

    def body(g_ref, w_ref, m_ref, v_ref, d_ref, nm_ref, nv_ref):
        d_ref[...], nm_ref[...], nv_ref[...] = _adamw_math(w_ref[...], g_ref[...], m_ref[...], v_ref[...])

    blk = pl.BlockSpec((rb, cols), lambda r: (r, 0))
    return _fused_call(
        body, jobs, name=name, grid=(rows // rb,), in_specs=[blk] * 4, out_specs=[blk] * 3,
        out_shape=[jax.ShapeDtypeStruct(w.shape, F32)] * 3, compiler_params=_params(),
    )(g, w, m, v)


def _adamw_small(grads, ws, ms, vs):
    n = len(grads)

    def body(*refs):
        g_refs, w_refs, m_refs, v_refs = refs[:n], refs[n:2 * n], refs[2 * n:3 * n], refs[3 * n:4 * n]
        outs = refs[4 * n:]
        for p in range(n):
            d, nm, nv = _adamw_math(w_refs[p][...], g_refs[p][...], m_refs[p][...], v_refs[p][...])
            outs[p][...] = d
            outs[n + p][...] = nm
            outs[2 * n + p][...] = nv

    vm = pl.BlockSpec(memory_space=pltpu.VMEM)
    shapes = [jax.ShapeDtypeStruct(w.shape, F32) for w in ws]
    out = pl.pallas_call(
        body, name="adamw_small", in_specs=[vm] * (4 * n), out_specs=[vm] * (3 * n), out_shape=shapes * 3,
    )(*grads, *ws, *ms, *vs)
    return out[:n], out[n:2 * n], out[2 * n:]


def _unstack_heads(w_st):
    per = HEAD_DIM // N_CHIPS
    return w_st.reshape(N_CHIPS, HEADS, per, HEAD_DIM).transpose(1, 0, 2, 3).reshape(HEADS, HEAD_DIM, HEAD_DIM)


def _stack_heads(w):
    per = HEAD_DIM // N_CHIPS
    return w.reshape(HEADS, N_CHIPS, per, HEAD_DIM).transpose(1, 0, 2, 3).reshape(N_CHIPS, HEADS * per, HEAD_DIM)


def kernel(x, norm_mix_g, w_in, conv_w, conv_b, w_rgate, b_rgate, w_igate, b_igate, lru_lambda, w_out_a, sgu_ln_g, sgu_ln_b, sgu_w_s, sgu_b_s, w_out_b, w_out, norm_mlp_g, w_up, w_down, norm_final_g, loss_target, m_norm_mix_g, m_w_in, m_conv_w, m_conv_b, m_w_rgate, m_b_rgate, m_w_igate, m_b_igate, m_lru_lambda, m_w_out_a, m_sgu_ln_g, m_sgu_ln_b, m_sgu_w_s, m_sgu_b_s, m_w_out_b, m_w_out, m_norm_mlp_g, m_w_up, m_w_down, m_norm_final_g, v_norm_mix_g, v_w_in, v_conv_w, v_conv_b, v_w_rgate, v_b_rgate, v_w_igate, v_b_igate, v_lru_lambda, v_w_out_a, v_sgu_ln_g, v_sgu_ln_b, v_sgu_w_s, v_sgu_b_s, v_w_out_b, v_w_out, v_norm_mlp_g, v_w_up, v_w_down, v_norm_final_g):
    chip = _chip_index(lax.axis_index("x"), lax.axis_index("y"))
    core = lax.axis_index("c")
    quarter_h = HEAD_DIM // N_CHIPS
    quarter_d = D_MODEL // N_CHIPS

    as_2d = lambda a: a.reshape(-1, a.shape[-1])
    big_w = [as_2d(w) for w in (w_in, w_rgate, w_igate, w_out_a, w_out_b, w_out, w_up, w_down)]
    big_m = [as_2d(w) for w in (m_w_in, m_w_rgate, m_w_igate, m_w_out_a, m_w_out_b, m_w_out, m_w_up, m_w_down)]
    big_v = [as_2d(w) for w in (v_w_in, v_w_rgate, v_w_igate, v_w_out_a, v_w_out_b, v_w_out, v_w_up, v_w_down)]

    packed = jnp.concatenate([conv_w[0], b_rgate[0], b_igate[0]], axis=1)
    packed = jnp.concatenate([packed, jnp.zeros_like(packed)], axis=0)
    s_in, s_r, s_i, s_oa, s_ob, s_out, s_up, s_down = [w.astype(BF16) for w in big_w]
    xs, target = x[0], loss_target[0]
    g3 = norm_final_g.reshape(1, D_MODEL)
    bias_s = jnp.broadcast_to(jnp.transpose(sgu_b_s[0])[:, :, None], (CHUNK, GROUPS, GROUP_DIM)).reshape(CHUNK, D_MODEL)
    core_arr = core.reshape(1).astype(jnp.int32)
    place = jnp.stack([chip, core]).astype(jnp.int32)
    quarter = lambda g: g.reshape(N_CHIPS, D_MODEL // N_CHIPS, D_MODEL)

    def pair_add(nm, g, from_sibling):
        return _pair_add("pair_add_" + nm, core_arr, g.reshape(N_CHIPS, 2, g.shape[1] // 2, g.shape[2]), from_sibling)

    def chip_sum(nm, pair, from_chips):
        return _chip_sum("chip_sum_" + nm, place, pair, from_chips)

    order = jnp.stack([chip, chip ^ 2, chip ^ 1, chip ^ 3]).astype(jnp.int32)
    (z, n1, (w_in_st, wr_st, wi_st)), ((packed_all,), late) = _fwd_in(
        xs, norm_mix_g, [s_in, s_r, s_i], order,
        jobs=[_gather_small_job(packed), _gather_near_job([s_oa, s_ob, s_out, s_up, s_down])])
    pick = lambda lo, hi: packed_all[:, :HEADS, lo:hi].transpose(1, 0, 2).reshape(HEADS, -1)
    conv_w_full = pick(0, quarter_d)
    br_full = pick(quarter_d, quarter_d + quarter_h).reshape(1, D_MODEL)
    bi_full = pick(quarter_d + quarter_h, quarter_d + 2 * quarter_h).reshape(1, D_MODEL)
    wr, wi = _unstack_heads(wr_st), _unstack_heads(wi_st)
    lru = (conv_w_full, conv_b, wr, br_full, wi, bi_full, lru_lambda)
    sgu = (sgu_ln_g, sgu_ln_b, sgu_w_s[0], bias_s)

    (ya, *saved), (late,) = _fwd_lru(z, *lru, jobs=[_gather_far_job(late)])
    yb, (late,) = _fwd_sgu(z, *sgu, jobs=[_gather_pass_job(late)])
    w_oa, w_ob, w_o = [w.reshape(D_MODEL, D_MODEL) for w in late[:3]]
    w_up_st, w_dn = late[3], late[4].reshape(D_FF, D_MODEL)
    (pa, pb, h1, n2), _ = _fwd_merge(ya, yb, z, xs, w_oa, w_ob, w_o, norm_mlp_g)
    (act, dup, dh2b, dh1, loss_part, dg3, dg2), _ = _mlp(n2, h1, target, w_up_st, w_dn, norm_mlp_g, g3)

    d_up, _ = _weight_grad("dw_up", n2, dup, N_CHIPS, False, True, D_MODEL)
    d_down, ((r_up,),) = _weight_grad("dw_down", act, dh2b, N_CHIPS, True, False, D_MODEL,
                                      jobs=[_pair_send_job([d_up])])
    p_up = pair_add("w_up", d_up, r_up)
    (dz, merged, dpa, dpb, dh1b, dlg, dlb, dws, dbs, dcw, dcb, dwr, dbr, dwi, dbi, dlam), ((r_down,), (q_up,)) = _bwd_mix(
        dh1, pa, pb, z, *saved, w_oa, w_ob, w_o, *sgu, conv_w_full, wr, wi, lru_lambda,
        jobs=[_pair_send_job([d_down]), _chip_exchange_job([p_up])])
    p_down = pair_add("w_down", d_down, r_down)
    half_up = chip_sum("w_up", p_up, q_up)
    d_out, ((q_down,), (full_up,)) = _weight_grad(
        "dw_out", merged, dh1b, 1, False, False, D_MODEL, jobs=[_chip_exchange_job([p_down]), _share_job([half_up])])
    d_oa, _ = _weight_grad("dw_out_a", ya, dpa, 1, False, False, D_MODEL)
    d_ob, _ = _weight_grad("dw_out_b", yb, dpb, 1, False, False, D_MODEL)
    half_down = chip_sum("w_down", p_down, q_down)
    mids = [quarter(d_oa), quarter(d_ob), quarter(d_out)]
    gates = [_stack_heads(dwr).astype(BF16), _stack_heads(dwi).astype(BF16)]
    small = _pack_small(dcw, dcb, dbr, dbi, dlam, dlg, dlb, dg2, dg3, loss_part, dbs)
    d_in, (r_late, (full_down,), (vec_all, ws_all)) = _weight_grad(
        "dw_in", n1, dz, N_CHIPS, False, True, IN_SHARD,
        jobs=[_pair_send_job(gates + mids), _share_job([half_down]), _gather_all_job([small, dws])])
    names = ("w_in", "w_rgate", "w_igate", "w_out_a", "w_out_b", "w_out", "w_up", "w_down")
    adam_args = {nm: (w, m, v) for nm, w, m, v in zip(names, big_w, big_m, big_v)}

    def adamw(nm, g):
        w, m, v = adam_args[nm]
        g = g.reshape(w.shape)
        return g, _adamw("adamw_" + nm, g, w, m, v)[0]

    (r_in,), = _comm_call("send_w_in", [_pair_send_job([d_in])])
    last_names = names[:6]
    p_last = [pair_add(nm, g, r) for nm, g, r in zip(last_names, [d_in] + gates + mids, [r_in] + r_late)]
    (grad_x, dg1), (q_last,) = _bwd_in(dz, xs, dh1, w_in_st, norm_mix_g, jobs=[_chip_exchange_job(p_last)])
    half_last = [chip_sum(nm, p, q) for nm, p, q in zip(last_names, p_last, q_last)]
    full_last, (dg1_all,) = _comm_call("share_last", [_share_job(half_last), _gather_all_job([dg1])])
    full, big_out = [], []
    for nm, f in zip(names, full_last + [full_up, full_down]):
        g, out = adamw(nm, f)
        full.append(g)
        big_out.append(out)

    vec, ws_sum = _sum_small(vec_all, ws_all, dg1_all)
    row = lambda r: vec[r:r + 1]
    shard = lambda a, width: lax.dynamic_slice_in_dim(a, chip * width, width, axis=1)
    g_small = dict(
        norm_mix_g=row(ROW_G1), conv_w=shard(vec[ROW_CW:ROW_CW + CONV_WIDTH], quarter_d), conv_b=row(ROW_CB),
        b_rgate=shard(row(ROW_BR).reshape(HEADS, HEAD_DIM), quarter_h),
        b_igate=shard(row(ROW_BI).reshape(HEADS, HEAD_DIM), quarter_h), lru_lambda=row(ROW_LAM),
        sgu_ln_g=row(ROW_LG), sgu_ln_b=row(ROW_LB),
        sgu_w_s=ws_sum.reshape(CHUNK, GROUPS, CHUNK).transpose(1, 0, 2).reshape(GROUPS * CHUNK, CHUNK),
        sgu_b_s=vec[ROW_BS:ROW_BS + GROUPS, 0:CHUNK], norm_mlp_g=row(ROW_G2), norm_final_g=row(ROW_G3))
    loss = vec[ROW_LOSS, 0]
    small_names = list(g_small)
    given = dict(
        norm_mix_g=(norm_mix_g, m_norm_mix_g, v_norm_mix_g), conv_w=(conv_w, m_conv_w, v_conv_w),
        conv_b=(conv_b, m_conv_b, v_conv_b), b_rgate=(b_rgate, m_b_rgate, v_b_rgate),
        b_igate=(b_igate, m_b_igate, v_b_igate), lru_lambda=(lru_lambda, m_lru_lambda, v_lru_lambda),
        sgu_ln_g=(sgu_ln_g, m_sgu_ln_g, v_sgu_ln_g), sgu_ln_b=(sgu_ln_b, m_sgu_ln_b, v_sgu_ln_b),
        sgu_w_s=(sgu_w_s, m_sgu_w_s, v_sgu_w_s), sgu_b_s=(sgu_b_s, m_sgu_b_s, v_sgu_b_s),
        norm_mlp_g=(norm_mlp_g, m_norm_mlp_g, v_norm_mlp_g), norm_final_g=(norm_final_g, m_norm_final_g, v_norm_final_g))
    g2d = [g_small[nm] for nm in small_names]
    to2d = lambda a, g: a.reshape(g.shape)
    d_s, m_s, v_s = _adamw_small(
        g2d, *[[to2d(given[nm][q], g) for nm, g in zip(small_names, g2d)] for q in range(3)])

    shapes = dict(
        norm_mix_g=norm_mix_g, w_in=w_in, conv_w=conv_w, conv_b=conv_b, w_rgate=w_rgate, b_rgate=b_rgate,
        w_igate=w_igate, b_igate=b_igate, lru_lambda=lru_lambda, w_out_a=w_out_a, sgu_ln_g=sgu_ln_g,
        sgu_ln_b=sgu_ln_b, sgu_w_s=sgu_w_s, sgu_b_s=sgu_b_s, w_out_b=w_out_b, w_out=w_out, norm_mlp_g=norm_mlp_g,
        w_up=w_up, w_down=w_down, norm_final_g=norm_final_g)
    grads, deltas, new_m, new_v = {}, {}, {}, {}
    for nm, g, (d, nmom, nvar) in zip(names, full, big_out):
        grads[nm], deltas[nm], new_m[nm], new_v[nm] = g, d, nmom, nvar
    for p, nm in enumerate(small_names):
        grads[nm], deltas[nm], new_m[nm], new_v[nm] = g2d[p], d_s[p], m_s[p], v_s[p]
    order = list(shapes)
    out = [loss, grad_x[None]]
    for group in (grads, deltas, new_m, new_v):
        out += [group[nm].reshape(shapes[nm].shape) for nm in order]
    return tuple(out)
```

```python
import functools

import jax
import jax.numpy as jnp
from jax import lax
from jax.experimental import pallas as pl
from jax.experimental.pallas import tpu as pltpu

F32 = jnp.float32
BF16 = jnp.bfloat16
MESH = pl.DeviceIdType.MESH

D_MODEL = 1024
D_IN = 6 * D_MODEL
D_FF = 4 * D_MODEL
N_CHIPS = 4
IN_SHARD = D_IN // N_CHIPS
HEADS = 4
HEAD_DIM = D_MODEL // HEADS
GROUPS = 4
GROUP_DIM = D_MODEL // GROUPS
CHUNK = 128
CONV_WIDTH = 4
LRU_C = 8.0
NORM_EPS = 1e-6
LN_EPS = 1e-5

ADAM_LR = 0.001
ADAM_B1 = 0.9
ADAM_B2 = 0.999
ADAM_EPS = 1e-08
ADAM_WD = 0.01
ADAM_STEP = 10

SUBLANES = 8
MM_TILE = 512
SEQ_TILE = 256
DW_TILE = 2048
VMEM_LIMIT_BYTES = 56 * 1024 * 1024

GELU_K0 = 0.7978845608028654
GELU_K1 = 0.044715


def _params(n_grid_axes=1):
    return pltpu.CompilerParams(
        dimension_semantics=("arbitrary",) * n_grid_axes, vmem_limit_bytes=VMEM_LIMIT_BYTES)


def _resident(shape):
    nd = len(shape)
    return pl.BlockSpec(shape, lambda *_: (0,) * nd, pipeline_mode=pl.Buffered(1))


def _const(shape):
    nd = len(shape)
    return pl.BlockSpec(shape, lambda *_: (0,) * nd)


def _dot(a, b):
    return jnp.dot(a, b, preferred_element_type=F32)


def _dot_nt(a, b):
    return lax.dot_general(a, b, (((1,), (1,)), ((), ())), preferred_element_type=F32)


def _dot_tn(a, b):
    return lax.dot_general(a, b, (((0,), (0,)), ((), ())), preferred_element_type=F32)


def _gelu(x):
    t = jnp.tanh(GELU_K0 * x * (1.0 + GELU_K1 * x * x))
    return 0.5 * x * (1.0 + t)


def _gelu_and_grad(x):
    x2 = x * x
    t = jnp.tanh(GELU_K0 * x * (1.0 + GELU_K1 * x2))
    g = 0.5 * x * (1.0 + t)
    dg = 0.5 * (1.0 + t) + 0.5 * x * (1.0 - t * t) * (GELU_K0 * (1.0 + 3.0 * GELU_K1 * x2))
    return g, dg


def _rms(x):
    r = lax.rsqrt(jnp.mean(x * x, axis=-1, keepdims=True) + NORM_EPS)
    return x * r, r


def _rms_bwd(dn, xhat, r):
    return r * (dn - xhat * jnp.mean(dn * xhat, axis=-1, keepdims=True))


def _col_sum(v):
    return jnp.sum(v, axis=0, keepdims=True)


def _shift_down(x, tail8, k):
    xs = pltpu.roll(x, k, 0)
    ts = pltpu.roll(tail8, k, 0)
    ridx = lax.broadcasted_iota(jnp.int32, tail8.shape, 0)
    head = jnp.where(ridx < k, ts, xs[0:SUBLANES])
    return jnp.concatenate([head, xs[SUBLANES:]], axis=0)


def _shift_up(x, head8, k):
    n = x.shape[0]
    xs = pltpu.roll(x, n - k, 0)
    hs = pltpu.roll(head8, SUBLANES - k, 0)
    ridx = lax.broadcasted_iota(jnp.int32, head8.shape, 0)
    last = jnp.where(ridx >= SUBLANES - k, hs, xs[n - SUBLANES:n])
    return jnp.concatenate([xs[:n - SUBLANES], last], axis=0)


def _scan_forward(a, b, carry):
    n, cols = a.shape
    groups = n // SUBLANES
    a = a.reshape(groups, SUBLANES, cols)
    b = b.reshape(groups, SUBLANES, cols)
    sub = lax.broadcasted_iota(jnp.int32, a.shape, 1)
    for s in (1, 2, 4):
        a_s = pltpu.roll(a, s, 1)
        b_s = pltpu.roll(b, s, 1)
        m = sub >= s
        b = jnp.where(m, a * b_s + b, b)
        a = jnp.where(m, a * a_s, a)
    out = []
    for g in range(groups):
        h = a[g] * carry + b[g]
        out.append(h)
        carry = h[SUBLANES - 1:SUBLANES]
    return jnp.concatenate(out, axis=0), carry


def _scan_backward(a, b, carry):
    n, cols = a.shape
    groups = n // SUBLANES
    a = a.reshape(groups, SUBLANES, cols)
    b = b.reshape(groups, SUBLANES, cols)
    sub = lax.broadcasted_iota(jnp.int32, a.shape, 1)
    for s in (1, 2, 4):
        a_s = pltpu.roll(a, SUBLANES - s, 1)
        b_s = pltpu.roll(b, SUBLANES - s, 1)
        m = sub < SUBLANES - s
        b = jnp.where(m, a * b_s + b, b)
        a = jnp.where(m, a * a_s, a)
    out = [None] * groups
    for g in reversed(range(groups)):
        h = a[g] * carry + b[g]
        out[g] = h
        carry = h[0:1]
    return jnp.concatenate(out, axis=0), carry


def _softplus_neg(lam):
    e = jnp.exp(-jnp.abs(lam))
    u = 1.0 + e
    log1p_e = jnp.where(u == 1.0, e, jnp.log(u) * (e / jnp.where(u == 1.0, 1.0, u - 1.0)))
    return jnp.maximum(-lam, 0.0) + log1p_e


def _lru_gates(xa, tail8, cw_ref, cb_ref, wr_ref, br_ref, wi_ref, bi_ref, lam_ref):
    cw = cw_ref[...]
    xc = cb_ref[...] + cw[0:1] * xa
    for k in range(1, CONV_WIDTH):
        xc = xc + cw[k:k + 1] * _shift_down(xa, tail8, k)
    xcb = xc.astype(BF16)
    pre_r, pre_i = [], []
    for h in range(HEADS):
        cols = slice(h * HEAD_DIM, (h + 1) * HEAD_DIM)
        pre_r.append(_dot(xcb[:, cols], wr_ref[h]))
        pre_i.append(_dot(xcb[:, cols], wi_ref[h]))
    r = jax.nn.sigmoid(jnp.concatenate(pre_r, axis=1) + br_ref[...])
    ig = jax.nn.sigmoid(jnp.concatenate(pre_i, axis=1) + bi_ref[...])
    _, a, mult = _decay(r, lam_ref)
    return xc, r, ig, a, mult


def _decay(r, lam_ref):
    sp = _softplus_neg(lam_ref[...])
    log_a = ((-LRU_C) * sp) * r
    a = jnp.exp(log_a)
    th = jnp.tanh(log_a)
    return sp, a, jnp.sqrt((-2.0 * th) / (1.0 - th))


class _Job:
    def __init__(self, inputs, out_shape, n_sem, copies, aliases=None, n_local=0):
        self.inputs, self.out_shape, self.n_sem, self.copies = list(inputs), list(out_shape), n_sem, copies
        self.aliases, self.n_local = dict(aliases or {}), n_local


def _fused_call(body, jobs, *, name, grid, in_specs, out_specs, out_shape, scratch_shapes=(),
                input_output_aliases=None, compiler_params=None, n_prefetch=0, jobs_start_after=None):
    single = not isinstance(out_shape, (list, tuple))
    out_specs = [out_specs] if single else list(out_specs)
    out_shape = [out_shape] if single else list(out_shape)
    n_scr = len(scratch_shapes)
    in_specs, scratch_shapes = list(in_specs), list(scratch_shapes)
    n_in, n_out = len(in_specs), len(out_shape)
    aliases = dict(input_output_aliases or {})
    in_at, out_at = [], []
    for job in jobs:
        in_at.append(len(in_specs))
        out_at.append(len(out_shape))
        for i, o in job.aliases.items():
            aliases[n_prefetch + len(in_specs) + i] = len(out_shape) + o
        in_specs += [ANY] * len(job.inputs)
        out_specs += [ANY] * len(job.out_shape)
        out_shape += job.out_shape
        scratch_shapes += [pltpu.SemaphoreType.DMA((job.n_sem,)), pltpu.SemaphoreType.DMA((job.n_sem,)),
                           pltpu.SemaphoreType.DMA((max(job.n_local, 1),))]
    n_in_all, n_out_all = len(in_specs), len(out_shape)

    def full_body(*refs):
        prefetch, refs = refs[:n_prefetch], refs[n_prefetch:]
        ins, outs, scr = refs[:n_in_all], refs[n_in_all:n_in_all + n_out_all], refs[n_in_all + n_out_all:]

        def copies(q):
            job = jobs[q]
            return job.copies(ins[in_at[q]:in_at[q] + len(job.inputs)], outs[out_at[q]:out_at[q] + len(job.out_shape)],
                              *scr[n_scr + 3 * q:n_scr + 3 * q + 3])

        def start():
            for q in range(len(jobs)):
                sends, _, local = copies(q)
                for cp in local + sends:
                    cp.start()

        def finish():
            every = [copies(q) for q in range(len(jobs))]
            for _, arrivals, _ in every:
                for cp in arrivals:
                    cp.wait_recv()
            for sends, _, local in every:
                for cp in sends:
                    cp.wait_send()
                for cp in local:
                    cp.wait()

        if not grid:
            start()
            finish()
            return
        ids = [pl.program_id(a) for a in range(len(grid))]
        at_step = lambda step: functools.reduce(jnp.logical_and, [i == k for i, k in zip(ids, step)])
        if jobs and jobs_start_after is None:
            pl.when(at_step((0,) * len(grid)))(start)
        body(*prefetch, *ins[:n_in], *outs[:n_out], *scr[:n_scr])
        if jobs and jobs_start_after is not None:
            pl.when(at_step(jobs_start_after))(start)
        if jobs:
            pl.when(functools.reduce(jnp.logical_and, [i == g - 1 for i, g in zip(ids, grid)]))(finish)

    if n_prefetch:
        layout = dict(grid_spec=pltpu.PrefetchScalarGridSpec(
            num_scalar_prefetch=n_prefetch, grid=grid, in_specs=in_specs, out_specs=out_specs,
            scratch_shapes=scratch_shapes))
    else:
        layout = dict(grid=grid, in_specs=in_specs, out_specs=out_specs, scratch_shapes=scratch_shapes)
    call = pl.pallas_call(
        full_body, name=name, out_shape=out_shape, input_output_aliases=aliases, compiler_params=compiler_params,
        **layout)

    def run(*args):
        res = call(*args, *[a for job in jobs for a in job.inputs])
        mine = res[0] if single else list(res[:n_out])
        return mine, [list(res[at:at + len(job.out_shape)]) for at, job in zip(out_at, jobs)]

    return run


def _fwd_in(x, g1, shards, order, jobs=()):
    t = x.shape[0]
    n_tiles = t // MM_TILE
    n = len(shards)
    halves = [s.shape[0] // 2 for s in shards]

    def body(order_ref, x_ref, g_ref, *refs):
        del order_ref
        ins, (z_ref, n_ref), outs = refs[:n], refs[n:n + 2], refs[n + 2:2 * n + 2]
        wbuf, nbuf, send, recv, local = refs[2 * n + 2:]
        s, i = pl.program_id(0), pl.program_id(1)
        x_, y_, c, chips = _place()
        k_me = _chip_index(x_, y_)

        def block(w, chip, pc):
            return outs[w].at[_chip_index(*chip), pl.ds(pc * halves[w], halves[w]), :]

        def over_ici(w, j, landing):
            return pltpu.make_async_remote_copy(
                src_ref=ins[w].at[pl.ds(c * halves[w], halves[w]), :],
                dst_ref=block(w, chips[j] if landing else (x_, y_), c), send_sem=send.at[6 * w + j],
                recv_sem=recv.at[6 * w + j], device_id=(*chips[j], c), device_id_type=MESH)

        def to_sibling(w, j, landing):
            blk = block(w, chips[j], 1 - c if landing else c)
            return pltpu.make_async_remote_copy(
                src_ref=blk, dst_ref=blk, send_sem=send.at[6 * w + 3 + j], recv_sem=recv.at[6 * w + 3 + j],
                device_id=(x_, y_, 1 - c), device_id_type=MESH)

        own = [pltpu.make_async_copy(wbuf, outs[0].at[k_me], local.at[0])]
        own += [pltpu.make_async_copy(ins[w], outs[w].at[k_me], local.at[w]) for w in range(1, n)]

        @pl.when((s == 0) & (i == 0))
        def _():
            for j in range(2):
                for w in range(n):
                    over_ici(w, j, False).start()
            load = pltpu.make_async_copy(ins[0], wbuf, local.at[n])
            load.start()
            load.wait()
            for cp in own:
                cp.start()

        for j in range(N_CHIPS - 1):
            @pl.when((s == j + 1) & (i == 0))
            def _(j=j):
                for w in range(n):
                    over_ici(w, j, True).wait_recv()
                for w in range(n):
                    to_sibling(w, j, False).start()
                if j == 0:
                    for w in range(n):
                        over_ici(w, 2, False).start()
                    own[0].wait()
                for w in range(n):
                    to_sibling(w, j, True).wait_recv()
                load = pltpu.make_async_copy(outs[0].at[_chip_index(*chips[j])], wbuf, local.at[n])
                load.start()
                load.wait()

        rows = pl.ds(pl.multiple_of(i * MM_TILE, MM_TILE), MM_TILE)

        @pl.when(s == 0)
        def _():
            xhat, _ = _rms(x_ref[...])
            nrm = (xhat * g_ref[...]).astype(BF16)
            nbuf[rows, :] = nrm
            n_ref[...] = nrm

        z_ref[...] = _dot(nbuf[rows, :], wbuf[...])

        @pl.when((s == N_CHIPS - 1) & (i == n_tiles - 1))
        def _():
            for j in range(N_CHIPS - 1):
                for w in range(n):
                    over_ici(w, j, False).wait_send()
                    to_sibling(w, j, False).wait_send()
            for cp in own[1:]:
                cp.wait()

    once = lambda s, i, order: (jnp.where(s == 0, i, n_tiles - 1), 0)
    (z, n1, *stacked), job_outs = _fused_call(
        body, jobs, name="fwd_in", grid=(N_CHIPS, n_tiles), n_prefetch=1,
        in_specs=[pl.BlockSpec((MM_TILE, D_MODEL), once), _const((1, D_MODEL))] + [ANY] * n,
        out_specs=[pl.BlockSpec((MM_TILE, IN_SHARD), lambda s, i, order: (i, order[s])),
                   pl.BlockSpec((MM_TILE, D_MODEL), once)] + [ANY] * n,
        out_shape=[jax.ShapeDtypeStruct((t, D_IN), F32), jax.ShapeDtypeStruct((t, D_MODEL), BF16)]
        + [jax.ShapeDtypeStruct((N_CHIPS,) + s.shape, s.dtype) for s in shards],
        scratch_shapes=[pltpu.VMEM(shards[0].shape, BF16), pltpu.VMEM((t, D_MODEL), BF16),
                        pltpu.SemaphoreType.DMA((6 * n,)),
                        pltpu.SemaphoreType.DMA((6 * n,)), pltpu.SemaphoreType.DMA((n + 1,))],
        compiler_params=_params(2), jobs_start_after=(1, 0),
    )(order, x, g1, *shards)
    return (z, n1, stacked), job_outs


def _fwd_lru(z, conv_w, conv_b, wr, br, wi, bi, lam, jobs=()):
    t = z.shape[0]

    def body(xa_ref, ga_ref, cw_ref, cb_ref, wr_ref, br_ref, wi_ref, bi_ref, lam_ref, ya_ref, h_ref, xc_ref, r_ref,
             ig_ref, tail_ref, carry_ref):
        @pl.when(pl.program_id(0) == 0)
        def _():
            tail_ref[...] = jnp.zeros_like(tail_ref)
            carry_ref[...] = jnp.zeros_like(carry_ref)

        xa = xa_ref[...]
        xc, r, ig, a, mult = _lru_gates(xa, tail_ref[...], cw_ref, cb_ref, wr_ref, br_ref, wi_ref, bi_ref, lam_ref)
        tail_ref[...] = xa[SEQ_TILE - SUBLANES:]
        xc_ref[...], r_ref[...], ig_ref[...] = xc, r, ig
        h, carry = _scan_forward(a, xc * ig * mult, carry_ref[...])
        carry_ref[...] = carry
        h_ref[...] = h
        ya_ref[...] = (h * _gelu(ga_ref[...])).astype(BF16)

    tile = lambda j: pl.BlockSpec((SEQ_TILE, D_MODEL), lambda i: (i, j))
    return _fused_call(
        body, jobs, name="fwd_lru", grid=(t // SEQ_TILE,),
        in_specs=[tile(0), tile(1), _const((CONV_WIDTH, D_MODEL)), _const((1, D_MODEL)),
                  _resident((HEADS, HEAD_DIM, HEAD_DIM)), _const((1, D_MODEL)),
                  _resident((HEADS, HEAD_DIM, HEAD_DIM)), _const((1, D_MODEL)), _const((1, D_MODEL))],
        out_specs=[tile(0)] * 5,
        out_shape=[jax.ShapeDtypeStruct((t, D_MODEL), BF16)] + [jax.ShapeDtypeStruct((t, D_MODEL), F32)] * 4,
        scratch_shapes=[pltpu.VMEM((SUBLANES, D_MODEL), F32), pltpu.VMEM((1, D_MODEL), F32)],
        compiler_params=_params(),
    )(z, z, conv_w, conv_b, wr, br, wi, bi, lam)


def _sgu_forward_parts(ub, vb, lg_ref, lb_ref):
    u, du = _gelu_and_grad(ub)
    vg, dvg = _gelu_and_grad(vb)
    mu = jnp.mean(vg, axis=-1, keepdims=True)
    d = vg - mu
    rstd = lax.rsqrt(jnp.mean(d * d, axis=-1, keepdims=True) + LN_EPS)
    vhat = d * rstd
    vn = (vhat * lg_ref[...] + lb_ref[...]).astype(BF16)
    return u, du, dvg, rstd, vhat, vn


def _causal_mask():
    rows = lax.broadcasted_iota(jnp.int32, (CHUNK, CHUNK), 0)
    cols = lax.broadcasted_iota(jnp.int32, (CHUNK, CHUNK), 1)
    return rows >= cols


def _fwd_sgu(z, ln_g, ln_b, w_s, bias_full, jobs=()):
    t = z.shape[0]

    def body(ub_ref, vb_ref, lg_ref, lb_ref, ws_ref, bias_ref, yb_ref):
        u, _, _, _, _, vn = _sgu_forward_parts(ub_ref[...], vb_ref[...], lg_ref, lb_ref)
        mask = _causal_mask()
        wm = [jnp.where(mask, ws_ref[g], 0.0).astype(BF16) for g in range(GROUPS)]
        for c in range(SEQ_TILE // CHUNK):
            rows = slice(c * CHUNK, (c + 1) * CHUNK)
            for g in range(GROUPS):
                cols = slice(g * GROUP_DIM, (g + 1) * GROUP_DIM)
                sp = _dot(wm[g], vn[rows, cols]) + bias_ref[:, cols]
                yb_ref[rows, cols] = (u[rows, cols] * sp).astype(BF16)

    tile = lambda j: pl.BlockSpec((SEQ_TILE, D_MODEL), lambda i: (i, j))
    return _fused_call(
        body, jobs, name="fwd_sgu", grid=(t // SEQ_TILE,),
        in_specs=[tile(2), tile(3), _const((1, D_MODEL)), _const((1, D_MODEL)),
                  _const((GROUPS, CHUNK, CHUNK)), _const((CHUNK, D_MODEL))],
        out_specs=tile(0),
        out_shape=jax.ShapeDtypeStruct((t, D_MODEL), BF16),
        compiler_params=_params(),
    )(z, z, ln_g, ln_b, w_s, bias_full)


def _fwd_merge(ya, yb, z, x, w_oa, w_ob, w_out, g2, jobs=()):
    t = x.shape[0]

    def body(ya_ref, yb_ref, m_ref, x_ref, woa_ref, wob_ref, wout_ref, g_ref, pa_ref, pb_ref, h1_ref, n2_ref):
        pa = _dot(ya_ref[...], woa_ref[...])
        pb = _dot(yb_ref[...], wob_ref[...])
        pa_ref[...] = pa
        pb_ref[...] = pb
        merged = jax.nn.sigmoid(m_ref[:, :D_MODEL]) * pa + jax.nn.sigmoid(m_ref[:, D_MODEL:]) * pb
        h1 = x_ref[...] + _dot(merged.astype(BF16), wout_ref[...])
        h1_ref[...] = h1
        xhat, _ = _rms(h1)
        n2_ref[...] = (xhat * g_ref[...]).astype(BF16)

    tile = pl.BlockSpec((MM_TILE, D_MODEL), lambda i: (i, 0))
    sq = _resident((D_MODEL, D_MODEL))
    return _fused_call(
        body, jobs, name="fwd_merge", grid=(t // MM_TILE,),
        in_specs=[tile, tile, pl.BlockSpec((MM_TILE, 2 * D_MODEL), lambda i: (i, 2)), tile, sq, sq, sq,
                  _const((1, D_MODEL))],
        out_specs=[tile, tile, tile, tile],
        out_shape=[jax.ShapeDtypeStruct((t, D_MODEL), F32)] * 3 + [jax.ShapeDtypeStruct((t, D_MODEL), BF16)],
        compiler_params=_params(),
    )(ya, yb, z, x, w_oa, w_ob, w_out, g2)


def _mlp(n2, h1, target, w_up_st, w_down, g2, g3, jobs=()):
    t = n2.shape[0]

    def body(n2_ref, h1_ref, tgt_ref, wup_ref, wdown_ref, g2_ref, g3_ref, act_ref, dup_ref, dh2b_ref, dh1_ref,
             loss_ref, dg3_ref, dg2_ref, relu_ref):
        @pl.when(pl.program_id(0) == 0)
        def _():
            for ref in (loss_ref, dg3_ref, dg2_ref):
                ref[...] = jnp.zeros_like(ref)

        n2 = n2_ref[...]
        h1 = h1_ref[...]
        h2 = h1
        for k in range(N_CHIPS):
            cols = slice(k * D_MODEL, (k + 1) * D_MODEL)
            r = jnp.maximum(_dot(n2, wup_ref[k]), 0.0)
            relu_ref[:, cols] = r
            act = (r * r).astype(BF16)
            act_ref[:, cols] = act
            h2 = h2 + _dot(act, wdown_ref[cols, :])
        xhat, r3 = _rms(h2)
        diff = xhat * g3_ref[...] - tgt_ref[...]
        sq = jnp.sum(diff * diff, axis=1, keepdims=True)
        loss_ref[...] = loss_ref[...] + (0.5 / D_MODEL) * jnp.sum(sq, axis=0, keepdims=True)
        dy = diff * (1.0 / D_MODEL)
        dg3_ref[...] = dg3_ref[...] + _col_sum(dy * xhat)
        dh2 = _rms_bwd(dy * g3_ref[...], xhat, r3)
        dh2b = dh2.astype(BF16)
        dh2b_ref[...] = dh2b
        dn2 = jnp.zeros((SEQ_TILE, D_MODEL), F32)
        for k in range(N_CHIPS):
            cols = slice(k * D_MODEL, (k + 1) * D_MODEL)
            dup = (_dot_nt(dh2b, wdown_ref[cols, :]) * (2.0 * relu_ref[:, cols])).astype(BF16)
            dup_ref[:, cols] = dup
            dn2 = dn2 + _dot_nt(dup, wup_ref[k])
        xhat, r2 = _rms(h1)
        dg2_ref[...] = dg2_ref[...] + _col_sum(dn2 * xhat)
        dh1_ref[...] = dh2 + _rms_bwd(dn2 * g2_ref[...], xhat, r2)

    tile = pl.BlockSpec((SEQ_TILE, D_MODEL), lambda i: (i, 0))
    wide = pl.BlockSpec((SEQ_TILE, D_FF), lambda i: (i, 0))
    vec = _const((1, D_MODEL))
    vec_shape = jax.ShapeDtypeStruct((1, D_MODEL), F32)
    return _fused_call(
        body, jobs, name="mlp", grid=(t // SEQ_TILE,),
        in_specs=[tile, tile, tile, _resident((N_CHIPS, D_MODEL, D_MODEL)), _resident((D_FF, D_MODEL)), vec, vec],
        out_specs=[wide, wide, tile, tile, _const((SUBLANES, 128)), vec, vec],
        out_shape=[jax.ShapeDtypeStruct((t, D_FF), BF16), jax.ShapeDtypeStruct((t, D_FF), BF16),
                   jax.ShapeDtypeStruct((t, D_MODEL), BF16), jax.ShapeDtypeStruct((t, D_MODEL), F32),
                   jax.ShapeDtypeStruct((SUBLANES, 128), F32), vec_shape, vec_shape],
        scratch_shapes=[pltpu.VMEM((SEQ_TILE, D_FF), F32)],
        compiler_params=_params(),
    )(n2, h1, target, w_up_st, w_down, g2, g3)


def _bwd_mix(dh1, pa, pb, z, h, xc, r, ig, w_oa, w_ob, w_out, ln_g, ln_b, w_s, bias_full, conv_w, wr, wi, lam, jobs=()):
    t = dh1.shape[0]
    n_tiles = t // SEQ_TILE
    per_tile = SEQ_TILE // SUBLANES

    def merge_part(dh1_ref, pa_ref, pb_ref, m_ref, woa_ref, wob_ref, wout_ref, dz_ref, dya_ref, dyb_ref, mg_ref,
                   dpa_ref, dpb_ref, dh1b_ref):
        dh1b = dh1_ref[...].astype(BF16)
        dh1b_ref[...] = dh1b
        dm = _dot_nt(dh1b, wout_ref[...])
        pa = pa_ref[...]
        pb = pb_ref[...]
        sa = jax.nn.sigmoid(m_ref[:, :D_MODEL])
        sb = jax.nn.sigmoid(m_ref[:, D_MODEL:])
        mg_ref[...] = (sa * pa + sb * pb).astype(BF16)
        dz_ref[:, :D_MODEL] = (dm * pa * sa * (1.0 - sa)).astype(BF16)
        dz_ref[:, D_MODEL:] = (dm * pb * sb * (1.0 - sb)).astype(BF16)
        dpa = (dm * sa).astype(BF16)
        dpb = (dm * sb).astype(BF16)
        dpa_ref[...] = dpa
        dpb_ref[...] = dpb
        dya_ref[...] = _dot_nt(dpa, woa_ref[...])
        dyb_ref[...] = _dot_nt(dpb, wob_ref[...])

    def sgu_part(dyb_ref, ub_ref, vb_ref, lg_ref, lb_ref, ws_ref, bias_ref, dz_ref, dlg_ref, dlb_ref, dws_ref, dbs_ref,
                 dvn_ref, dsp_acc):
        i = pl.program_id(0)

        @pl.when(i == 0)
        def _():
            dlg_ref[...] = jnp.zeros_like(dlg_ref)
            dlb_ref[...] = jnp.zeros_like(dlb_ref)
            dws_ref[...] = jnp.zeros_like(dws_ref)
            dsp_acc[...] = jnp.zeros_like(dsp_acc)

        u, du, dvg, rstd, vhat, vn = _sgu_forward_parts(ub_ref[...], vb_ref[...], lg_ref, lb_ref)
        dyb = dyb_ref[...]
        mask = _causal_mask()
        wm = [jnp.where(mask, ws_ref[g], 0.0).astype(BF16) for g in range(GROUPS)]
        for c in range(SEQ_TILE // CHUNK):
            rows = slice(c * CHUNK, (c + 1) * CHUNK)
            for g in range(GROUPS):
                cols = slice(g * GROUP_DIM, (g + 1) * GROUP_DIM)
                vn_blk = vn[rows, cols]
                sp = _dot(wm[g], vn_blk) + bias_ref[:, cols]
                dyb_blk = dyb[rows, cols]
                dz_ref[rows, cols] = (dyb_blk * sp * du[rows, cols]).astype(BF16)
                dsp = dyb_blk * u[rows, cols]
                dsp_acc[:, cols] = dsp_acc[:, cols] + dsp
                dspb = dsp.astype(BF16)
                dvn_ref[rows, cols] = _dot_tn(wm[g], dspb)
                wcols = slice(g * CHUNK, (g + 1) * CHUNK)
                dws_ref[:, wcols] = dws_ref[:, wcols] + jnp.where(mask, _dot_nt(dspb, vn_blk), 0.0)
        dvn = dvn_ref[...]
        dlg_ref[...] = dlg_ref[...] + _col_sum(dvn * vhat)
        dlb_ref[...] = dlb_ref[...] + _col_sum(dvn)
        dvhat = dvn * lg_ref[...]
        dvgel = rstd * (dvhat - jnp.mean(dvhat, axis=-1, keepdims=True)
                        - vhat * jnp.mean(dvhat * vhat, axis=-1, keepdims=True))
        dz_ref[:, D_MODEL:] = (dvgel * dvg).astype(BF16)

        @pl.when(i == n_tiles - 1)
        def _():
            lane = lax.broadcasted_iota(jnp.int32, (CHUNK, 128), 1)
            out = jnp.zeros((CHUNK, 128), F32)
            for g in range(GROUPS):
                s = jnp.sum(dsp_acc[:, g * GROUP_DIM:(g + 1) * GROUP_DIM], axis=1, keepdims=True)
                out = out + jnp.where(lane == g, s, 0.0)
            dbs_ref[...] = out

    def lru_part(dya_ref, xa_ref, ga_ref, h_ref, h_prev_ref, xc_ref, r_ref, ig_ref, cw_ref, wr_ref, wi_ref, lam_ref,
                 dz_ref, dcw_ref, dcb_ref, dwr_ref, dbr_ref, dwi_ref, dbi_ref, dlam_ref, lam_carry, dxc_head):
        i = pl.program_id(0)

        @pl.when(i == 0)
        def _():
            for ref in (dcw_ref, dcb_ref, dwr_ref, dbr_ref, dwi_ref, dbi_ref, dlam_ref, lam_carry, dxc_head):
                ref[...] = jnp.zeros_like(ref)

        first_tile = i == n_tiles - 1
        h_tail = jnp.where(first_tile, 0.0, h_prev_ref[...])
        xc, r, ig = xc_ref[...], r_ref[...], ig_ref[...]
        xcb = xc.astype(BF16)
        sp, a, mult = _decay(r, lam_ref)
        h = h_ref[...]
        h_prev = _shift_down(h, h_tail, 1)
        dya = dya_ref[...]
        gg, dgg = _gelu_and_grad(ga_ref[...])
        dz_ref[:, D_MODEL:] = (dya * h * dgg).astype(BF16)
        ones = jnp.ones((SUBLANES, D_MODEL), F32)
        lam_t, lam_first = _scan_backward(_shift_up(a, ones, 1), dya * gg, lam_carry[...])
        lam_carry[...] = a[0:1] * lam_first
        dmult = lam_t * xc * ig
        dla = lam_t * h_prev * a - dmult * (a * a) / mult
        dr = dla * ((-LRU_C) * sp)
        dlam_ref[...] = dlam_ref[...] + _col_sum(dla * r) * (LRU_C * jax.nn.sigmoid(-lam_ref[...]))
        dpr = dr * r * (1.0 - r)
        dpi = lam_t * xc * mult * ig * (1.0 - ig)
        dbr_ref[...] = dbr_ref[...] + _col_sum(dpr)
        dbi_ref[...] = dbi_ref[...] + _col_sum(dpi)
        dprb = dpr.astype(BF16)
        dpib = dpi.astype(BF16)
        dxc_gate = []
        for hd in range(HEADS):
            cols = slice(hd * HEAD_DIM, (hd + 1) * HEAD_DIM)
            dxc_gate.append(_dot_nt(dprb[:, cols], wr_ref[hd]) + _dot_nt(dpib[:, cols], wi_ref[hd]))
            dwr_ref[hd] = dwr_ref[hd] + _dot_tn(xcb[:, cols], dprb[:, cols])
            dwi_ref[hd] = dwi_ref[hd] + _dot_tn(xcb[:, cols], dpib[:, cols])
        dxc = lam_t * ig * mult + jnp.concatenate(dxc_gate, axis=1)
        dcb_ref[...] = dcb_ref[...] + _col_sum(dxc)
        cw = cw_ref[...]
        head = dxc_head[...]
        xa = xa_ref[...]
        dxa = cw[0:1] * dxc
        dcw_ref[0:1, :] = dcw_ref[0:1, :] + _col_sum(dxc * xa)
        for k in range(1, CONV_WIDTH):
            dxc_k = _shift_up(dxc, head, k)
            dxa = dxa + cw[k:k + 1] * dxc_k
            dcw_ref[k:k + 1, :] = dcw_ref[k:k + 1, :] + _col_sum(dxc_k * xa)
        dxc_head[...] = dxc[0:SUBLANES]
        dz_ref[:, :D_MODEL] = dxa.astype(BF16)

    def body(dh1_ref, pa_ref, pb_ref, z_ref, h_ref, h_prev_ref, xc_ref, r_ref, ig_ref, woa_ref, wob_ref, wout_ref,
             lg_ref, lb_ref, ws_ref, bias_ref, cw_ref, wr_ref, wi_ref, lam_ref, dz_ref, mg_ref, dpa_ref, dpb_ref,
             dh1b_ref, dlg_ref, dlb_ref, dws_ref, dbs_ref, dcw_ref, dcb_ref, dwr_ref, dbr_ref, dwi_ref, dbi_ref,
             dlam_ref, dya_ref, dyb_ref, dvn_ref, dsp_acc, lam_carry, dxc_head):
        def cols(ref, first, count):
            return ref.at[:, pl.ds(first * D_MODEL, count * D_MODEL)]

        merge_part(dh1_ref, pa_ref, pb_ref, cols(z_ref, 4, 2), woa_ref, wob_ref, wout_ref, cols(dz_ref, 4, 2), dya_ref,
                   dyb_ref, mg_ref, dpa_ref, dpb_ref, dh1b_ref)
        sgu_part(dyb_ref, cols(z_ref, 2, 1), cols(z_ref, 3, 1), lg_ref, lb_ref, ws_ref, bias_ref, cols(dz_ref, 2, 2),
                 dlg_ref, dlb_ref, dws_ref, dbs_ref, dvn_ref, dsp_acc)
        lru_part(dya_ref, cols(z_ref, 0, 1), cols(z_ref, 1, 1), h_ref, h_prev_ref, xc_ref, r_ref, ig_ref, cw_ref, wr_ref,
                 wi_ref, lam_ref, cols(dz_ref, 0, 2), dcw_ref, dcb_ref, dwr_ref, dbr_ref, dwi_ref, dbi_ref, dlam_ref,
                 lam_carry, dxc_head)

    rev = lambda i: n_tiles - 1 - i
    tile = pl.BlockSpec((SEQ_TILE, D_MODEL), lambda i: (rev(i), 0))
    row = pl.BlockSpec((SEQ_TILE, D_IN), lambda i: (rev(i), 0))
    prev8 = pl.BlockSpec((SUBLANES, D_MODEL), lambda i: (jnp.maximum(rev(i) * per_tile - 1, 0), 0))
    vec = _const((1, D_MODEL))
    sq = _resident((D_MODEL, D_MODEL))
    gate_w = _resident((HEADS, HEAD_DIM, HEAD_DIM))
    gate_acc = _const((HEADS, HEAD_DIM, HEAD_DIM))
    vec_shape = jax.ShapeDtypeStruct((1, D_MODEL), F32)
    gate_shape = jax.ShapeDtypeStruct((HEADS, HEAD_DIM, HEAD_DIM), F32)
    act_bf = jax.ShapeDtypeStruct((t, D_MODEL), BF16)
    return _fused_call(
        body, jobs, name="bwd_mix", grid=(n_tiles,),
        in_specs=[tile, tile, tile, row, tile, prev8, tile, tile, tile, sq, sq, sq, vec, vec,
                  _const((GROUPS, CHUNK, CHUNK)), _const((CHUNK, D_MODEL)), _const((CONV_WIDTH, D_MODEL)), gate_w, gate_w,
                  vec],
        out_specs=[row, tile, tile, tile, tile, vec, vec, _const((CHUNK, GROUPS * CHUNK)), _const((CHUNK, 128)),
                   _const((SUBLANES, D_MODEL)), vec, gate_acc, vec, gate_acc, vec, vec],
        out_shape=[jax.ShapeDtypeStruct((t, D_IN), BF16), act_bf, act_bf, act_bf, act_bf, vec_shape, vec_shape,
                   jax.ShapeDtypeStruct((CHUNK, GROUPS * CHUNK), F32), jax.ShapeDtypeStruct((CHUNK, 128), F32),
                   jax.ShapeDtypeStruct((SUBLANES, D_MODEL), F32), vec_shape, gate_shape, vec_shape, gate_shape,
                   vec_shape, vec_shape],
        scratch_shapes=[pltpu.VMEM((SEQ_TILE, D_MODEL), F32), pltpu.VMEM((SEQ_TILE, D_MODEL), F32),
                        pltpu.VMEM((SEQ_TILE, D_MODEL), F32), pltpu.VMEM((CHUNK, D_MODEL), F32),
                        pltpu.VMEM((1, D_MODEL), F32), pltpu.VMEM((SUBLANES, D_MODEL), F32)],
        compiler_params=_params(),
    )(dh1, pa, pb, z, h, h, xc, r, ig, w_oa, w_ob, w_out, ln_g, ln_b, w_s, bias_full, conv_w, wr, wi, lam)


def _bwd_in(dz, x, dh1, w_in_st, g1, jobs=()):
    t = x.shape[0]

    def body(dz_ref, x_ref, dh1_ref, w_ref, g_ref, dx_ref, dg1_ref):
        @pl.when(pl.program_id(0) == 0)
        def _():
            dg1_ref[...] = jnp.zeros_like(dg1_ref)

        dn1 = jnp.zeros((MM_TILE, D_MODEL), F32)
        for k in range(N_CHIPS):
            dn1 = dn1 + _dot_nt(dz_ref[:, k * IN_SHARD:(k + 1) * IN_SHARD], w_ref[k])
        xhat, r1 = _rms(x_ref[...])
        dg1_ref[...] = dg1_ref[...] + _col_sum(dn1 * xhat)
        dx_ref[...] = dh1_ref[...] + _rms_bwd(dn1 * g_ref[...], xhat, r1)

    tile = pl.BlockSpec((MM_TILE, D_MODEL), lambda i: (i, 0))
    return _fused_call(
        body, jobs, name="bwd_in", grid=(t // MM_TILE,),
        in_specs=[pl.BlockSpec((MM_TILE, D_IN), lambda i: (i, 0)), tile, tile,
                  _resident((N_CHIPS, D_MODEL, IN_SHARD)), _const((1, D_MODEL))],
        out_specs=[tile, _const((1, D_MODEL))],
        out_shape=[jax.ShapeDtypeStruct((t, D_MODEL), F32), jax.ShapeDtypeStruct((1, D_MODEL), F32)],
        compiler_params=_params(),
    )(dz, x, dh1, w_in_st, g1)


def _weight_grad(name, a, b, n_blocks, a_varies, b_varies, width, jobs=()):
    t = a.shape[0]
    rows = min(DW_TILE, t)
    n_t = t // rows

    def body(a_ref, b_ref, o_ref, acc_ref):
        s = pl.program_id(1)
        part = _dot_tn(a_ref[...], b_ref[...])

        @pl.when(s == 0)
        def _():
            acc_ref[...] = part

        @pl.when(s > 0)
        def _():
            acc_ref[...] = acc_ref[...] + part

        @pl.when(s == n_t - 1)
        def _():
            o_ref[...] = acc_ref[...].astype(BF16)

    return _fused_call(
        body, jobs, name=name, grid=(n_blocks, n_t),
        in_specs=[pl.BlockSpec((rows, D_MODEL), (lambda j, s: (s, j)) if a_varies else (lambda j, s: (s, 0))),
                  pl.BlockSpec((rows, width), (lambda j, s: (s, j)) if b_varies else (lambda j, s: (s, 0)))],
        out_specs=pl.BlockSpec((None, D_MODEL, width), lambda j, s: (j, 0, 0)),
        out_shape=jax.ShapeDtypeStruct((n_blocks, D_MODEL, width), BF16),
        scratch_shapes=[pltpu.VMEM((D_MODEL, width), F32)],
        compiler_params=_params(2),
    )(a, b)


def _place():
    x, y, c = lax.axis_index("x"), lax.axis_index("y"), lax.axis_index("c")
    other_chips = [(1 - x, y), (x, 1 - y), (1 - x, 1 - y)]
    return x, y, c, other_chips


def _chip_index(px, py):
    return 2 * px + py


ANY = pl.BlockSpec(memory_space=pl.ANY)


def _comm_call(name, jobs):
    return _fused_call(None, jobs, name=name, grid=(), in_specs=[], out_specs=[], out_shape=[])()[1]


def _near_far(x, y, c):
    return (x ^ (1 - c), y ^ c), (x ^ c, y ^ (1 - c))


def _gather_near_job(shards):
    n = len(shards)
    halves = [s.shape[0] // 2 for s in shards]

    def copies(ins, outs, send, recv, local):
        x, y, c, _ = _place()
        near, _ = _near_far(x, y, c)

        def block(w, chip, pc):
            return outs[w].at[_chip_index(*chip), pl.ds(pc * halves[w], halves[w]), :]

        def copy(w, k, chip, pc, to, src=None):
            return pltpu.make_async_remote_copy(
                src_ref=block(w, chip, pc) if src is None else src, dst_ref=block(w, chip, pc),
                send_sem=send.at[2 * w + k], recv_sem=recv.at[2 * w + k], device_id=to, device_id_type=MESH)

        sends, arrivals, own = [], [], []
        for w in range(n):
            src = ins[w].at[pl.ds(c * halves[w], halves[w]), :]
            own.append(pltpu.make_async_copy(src, block(w, (x, y), c), local.at[w]))
            sends += [copy(w, 0, (x, y), c, (*near, c), src), copy(w, 1, (x, y), c, (x, y, 1 - c), src)]
            arrivals += [copy(w, 0, near, c, (x, y, c)), copy(w, 1, (x, y), 1 - c, (x, y, c))]
        return sends, arrivals, own

    return _Job(shards, [jax.ShapeDtypeStruct((N_CHIPS,) + s.shape, s.dtype) for s in shards], 2 * n, copies,
                n_local=n)


def _gather_far_job(stacked):
    n = len(stacked)
    halves = [s.shape[1] // 2 for s in stacked]

    def copies(ins, outs, send, recv, local):
        del ins, local
        x, y, c, _ = _place()
        near, far = _near_far(x, y, c)

        def copy(w, k, chip):
            blk = outs[w].at[_chip_index(*chip), pl.ds(c * halves[w], halves[w]), :]
            return pltpu.make_async_remote_copy(
                src_ref=blk, dst_ref=blk, send_sem=send.at[2 * w + k], recv_sem=recv.at[2 * w + k],
                device_id=(*far, c), device_id_type=MESH)

        sends = [copy(w, k, chip) for w in range(n) for k, chip in enumerate(((x, y), near))]
        arrivals = [copy(w, k, chip) for w in range(n) for k, chip in enumerate((far, (1 - x, 1 - y)))]
        return sends, arrivals, []

    return _Job(stacked, [jax.ShapeDtypeStruct(s.shape, s.dtype) for s in stacked], 2 * n, copies,
                aliases={w: w for w in range(n)})


def _gather_pass_job(stacked):
    n = len(stacked)
    halves = [s.shape[1] // 2 for s in stacked]

    def copies(ins, outs, send, recv, local):
        del ins, local
        x, y, c, chips = _place()

        def copy(w, j, chip, pc, to):
            blk = outs[w].at[_chip_index(*chip), pl.ds(pc * halves[w], halves[w]), :]
            return pltpu.make_async_remote_copy(
                src_ref=blk, dst_ref=blk, send_sem=send.at[3 * w + j], recv_sem=recv.at[3 * w + j], device_id=to,
                device_id_type=MESH)

        sends = [copy(w, j, chip, c, (x, y, 1 - c)) for w in range(n) for j, chip in enumerate(chips)]
        arrivals = [copy(w, j, chip, 1 - c, (x, y, c)) for w in range(n) for j, chip in enumerate(chips)]
        return sends, arrivals, []

    return _Job(stacked, [jax.ShapeDtypeStruct(s.shape, s.dtype) for s in stacked], 3 * n, copies,
                aliases={w: w for w in range(n)})


def _gather_small_job(block):
    def copies(ins, outs, send, recv, local):
        x, y, c, chips = _place()

        def copy(j, chip_from, to):
            return pltpu.make_async_remote_copy(
                src_ref=ins[0], dst_ref=outs[0].at[_chip_index(*chip_from)], send_sem=send.at[j],
                recv_sem=recv.at[j], device_id=to, device_id_type=MESH)

        own = [pltpu.make_async_copy(ins[0], outs[0].at[_chip_index(x, y)], local.at[0])]
        sends = [copy(j, (x, y), (*chip, c)) for j, chip in enumerate(chips)]
        arrivals = [copy(j, chip, (x, y, c)) for j, chip in enumerate(chips)]
        return sends, arrivals, own

    return _Job([block], [jax.ShapeDtypeStruct((N_CHIPS,) + block.shape, block.dtype)], 3, copies, n_local=1)


def _pair_send_job(grads):
    n = len(grads)
    halves = [g.shape[1] // 2 for g in grads]

    def copies(ins, outs, send, recv, local):
        del local
        x, y, c, _ = _place()
        sends = [pltpu.make_async_remote_copy(
            src_ref=ins[w].at[:, pl.ds((1 - c) * halves[w], halves[w]), :], dst_ref=outs[w], send_sem=send.at[w],
            recv_sem=recv.at[w], device_id=(x, y, 1 - c), device_id_type=MESH) for w in range(n)]
        return sends, sends, []

    return _Job(grads, [jax.ShapeDtypeStruct((N_CHIPS, h, g.shape[2]), g.dtype) for g, h in zip(grads, halves)], n,
                copies)


def _row_block(rows, limit=256):
    return min(rows, limit)


def _pair_add(name, core, mine, theirs):
    _, _, h, cols = mine.shape
    rb = _row_block(h, 512)

    def body(core_ref, a_ref, b_ref, o_ref):
        del core_ref
        o_ref[...] = (a_ref[...].astype(F32) + b_ref[...].astype(F32)).astype(BF16)

    return pl.pallas_call(
        body, name=name,
        grid_spec=pltpu.PrefetchScalarGridSpec(
            num_scalar_prefetch=1, grid=(N_CHIPS, h // rb),
            in_specs=[pl.BlockSpec((None, None, rb, cols), lambda k, r, core_ref: (k, core_ref[0], r, 0)),
                      pl.BlockSpec((None, rb, cols), lambda k, r, core_ref: (k, r, 0))],
            out_specs=pl.BlockSpec((None, rb, cols), lambda k, r, core_ref: (k, r, 0))),
        out_shape=jax.ShapeDtypeStruct(theirs.shape, BF16),
        compiler_params=_params(2),
    )(core, mine, theirs)


def _chip_exchange_job(sums):
    n = len(sums)

    def copies(ins, outs, send, recv, local):
        del local
        _, _, c, chips = _place()
        sends = [pltpu.make_async_remote_copy(
            src_ref=ins[w].at[_chip_index(*chip)], dst_ref=outs[w].at[j], send_sem=send.at[3 * w + j],
            recv_sem=recv.at[3 * w + j], device_id=(*chip, c), device_id_type=MESH)
            for w in range(n) for j, chip in enumerate(chips)]
        return sends, sends, []

    return _Job(sums, [jax.ShapeDtypeStruct((N_CHIPS - 1,) + s.shape[1:], s.dtype) for s in sums], 3 * n, copies)


def _chip_sum(name, place, mine, theirs):
    _, h, cols = mine.shape
    rb = _row_block(h, 512)

    def body(place_ref, p_ref, q_ref, o_ref):
        del place_ref
        acc = p_ref[...].astype(F32)
        for j in range(N_CHIPS - 1):
            acc = acc + q_ref[j].astype(F32)
        o_ref[...] = acc

    return pl.pallas_call(
        body, name=name,
        grid_spec=pltpu.PrefetchScalarGridSpec(
            num_scalar_prefetch=1, grid=(h // rb,),
            in_specs=[pl.BlockSpec((None, rb, cols), lambda r, place_ref: (place_ref[0], r, 0)),
                      pl.BlockSpec((N_CHIPS - 1, rb, cols), lambda r, place_ref: (0, r, 0))],
            out_specs=pl.BlockSpec((None, rb, cols), lambda r, place_ref: (place_ref[1], r, 0))),
        out_shape=jax.ShapeDtypeStruct((2, h, cols), F32),
        compiler_params=_params(),
    )(place, mine, theirs)


def _share_job(bufs):
    n = len(bufs)

    def copies(ins, outs, send, recv, local):
        del ins, local
        x, y, c, _ = _place()

        def copy(w, half):
            return pltpu.make_async_remote_copy(
                src_ref=outs[w].at[half], dst_ref=outs[w].at[half], send_sem=send.at[w], recv_sem=recv.at[w],
                device_id=(x, y, 1 - c), device_id_type=MESH)

        return [copy(w, c) for w in range(n)], [copy(w, 1 - c) for w in range(n)], []

    return _Job(bufs, [jax.ShapeDtypeStruct(b.shape, b.dtype) for b in bufs], n, copies,
                aliases={w: w for w in range(n)})


SMALL_ROWS = 24
ROW_G1, ROW_CW, ROW_CB, ROW_BR, ROW_BI, ROW_LAM, ROW_LG, ROW_LB, ROW_G2, ROW_G3, ROW_LOSS, ROW_BS = (
    0, 1, 5, 6, 7, 8, 9, 10, 11, 12, 13, 16)
N_DEV = 8


def _pack_small(dcw, dcb, dbr, dbi, dlam, dlg, dlb, dg2, dg3, loss, dbs):
    def body(dcw_ref, dcb_ref, dbr_ref, dbi_ref, dlam_ref, dlg_ref, dlb_ref, dg2_ref, dg3_ref, loss_ref, dbs_ref, out):
        out[...] = jnp.zeros((SMALL_ROWS, D_MODEL), F32)
        for row, ref in ((ROW_CB, dcb_ref), (ROW_BR, dbr_ref), (ROW_BI, dbi_ref), (ROW_LAM, dlam_ref),
                         (ROW_LG, dlg_ref), (ROW_LB, dlb_ref), (ROW_G2, dg2_ref), (ROW_G3, dg3_ref)):
            out[row:row + 1, :] = ref[...]
        out[ROW_CW:ROW_CW + CONV_WIDTH, :] = dcw_ref[0:CONV_WIDTH, :]
        out[ROW_LOSS:ROW_LOSS + 1, 0:128] = loss_ref[0:1, :]
        out[ROW_BS:ROW_BS + GROUPS, 0:128] = jnp.transpose(dbs_ref[...])[0:GROUPS, :]

    vm = pl.BlockSpec(memory_space=pltpu.VMEM)
    return pl.pallas_call(
        body, name="pack_small", in_specs=[vm] * 11, out_specs=vm,
        out_shape=jax.ShapeDtypeStruct((SMALL_ROWS, D_MODEL), F32),
    )(dcw, dcb, dbr, dbi, dlam, dlg, dlb, dg2, dg3, loss, dbs)


def _gather_all_job(blocks):
    n = len(blocks)
    flips = [(dx, dy, dc) for dx in (0, 1) for dy in (0, 1) for dc in (0, 1)][1:]

    def copies(ins, outs, send, recv, local):
        x, y, c, _ = _place()
        me = 4 * x + 2 * y + c
        sends, arrivals, own = [], [], []
        for w in range(n):
            own.append(pltpu.make_async_copy(ins[w], outs[w].at[me], local.at[w]))
            for k, (dx, dy, dc) in enumerate(flips):
                peer = (x ^ dx, y ^ dy, c ^ dc)
                sem = dict(send_sem=send.at[7 * w + k], recv_sem=recv.at[7 * w + k])
                sends.append(pltpu.make_async_remote_copy(
                    src_ref=ins[w], dst_ref=outs[w].at[me], device_id=peer, device_id_type=MESH, **sem))
                arrivals.append(pltpu.make_async_remote_copy(
                    src_ref=ins[w], dst_ref=outs[w].at[4 * peer[0] + 2 * peer[1] + peer[2]], device_id=peer,
                    device_id_type=MESH, **sem))
        return sends, arrivals, own

    return _Job(blocks, [jax.ShapeDtypeStruct((N_DEV,) + b.shape, b.dtype) for b in blocks], 7 * n, copies, n_local=n)


def _sum_small(vec_all, ws_all, dg1_all):
    def body(vec_ref, ws_ref, dg1_ref, vec_out, ws_out):
        vec, ws, dg1 = vec_ref[0], ws_ref[0], dg1_ref[0]
        for d in range(1, N_DEV):
            vec, ws, dg1 = vec + vec_ref[d], ws + ws_ref[d], dg1 + dg1_ref[d]
        vec_out[...] = vec
        vec_out[ROW_G1:ROW_G1 + 1, :] = dg1
        ws_out[...] = ws

    vm = pl.BlockSpec(memory_space=pltpu.VMEM)
    return pl.pallas_call(
        body, name="sum_small", in_specs=[vm] * 3, out_specs=[vm, vm],
        out_shape=[jax.ShapeDtypeStruct(vec_all.shape[1:], F32), jax.ShapeDtypeStruct(ws_all.shape[1:], F32)],
    )(vec_all, ws_all, dg1_all)


def _adamw_math(w, g, m, v):
    m = ADAM_B1 * m + (1.0 - ADAM_B1) * g
    v = ADAM_B2 * v + (1.0 - ADAM_B2) * (g * g)
    m_hat = m / (1.0 - ADAM_B1 ** ADAM_STEP)
    v_hat = v / (1.0 - ADAM_B2 ** ADAM_STEP)
    delta = (-ADAM_LR) * (m_hat / (jnp.sqrt(v_hat) + ADAM_EPS) + ADAM_WD * w)
    return delta, m, v


def _adamw(name, g, w, m, v, jobs=()):
    rows, cols = w.shape
    rb = _row_block(rows)

    def body(g_ref, w_ref, m_ref, v_ref, d_ref, nm_ref, nv_ref):
        d_ref[...], nm_ref[...], nv_ref[...] = _adamw_math(w_ref[...], g_ref[...], m_ref[...], v_ref[...])

    blk = pl.BlockSpec((rb, cols), lambda r: (r, 0))
    return _fused_call(
        body, jobs, name=name, grid=(rows // rb,), in_specs=[blk] * 4, out_specs=[blk] * 3,
        out_shape=[jax.ShapeDtypeStruct(w.shape, F32)] * 3, compiler_params=_params(),
    )(g, w, m, v)


def _adamw_small(grads, ws, ms, vs):
    n = len(grads)

    def body(*refs):
        g_refs, w_refs, m_refs, v_refs = refs[:n], refs[n:2 * n], refs[2 * n:3 * n], refs[3 * n:4 * n]
        outs = refs[4 * n:]
        for p in range(n):
            d, nm, nv = _adamw_math(w_refs[p][...], g_refs[p][...], m_refs[p][...], v_refs[p][...])
            outs[p][...] = d
            outs[n + p][...] = nm
            outs[2 * n + p][...] = nv

    vm = pl.BlockSpec(memory_space=pltpu.VMEM)
    shapes = [jax.ShapeDtypeStruct(w.shape, F32) for w in ws]
    out = pl.pallas_call(
        body, name="adamw_small", in_specs=[vm] * (4 * n), out_specs=[vm] * (3 * n), out_shape=shapes * 3,
    )(*grads, *ws, *ms, *vs)
    return out[:n], out[n:2 * n], out[2 * n:]


def _unstack_heads(w_st):
    per = HEAD_DIM // N_CHIPS
    return w_st.reshape(N_CHIPS, HEADS, per, HEAD_DIM).transpose(1, 0, 2, 3).reshape(HEADS, HEAD_DIM, HEAD_DIM)


def _stack_heads(w):
    per = HEAD_DIM // N_CHIPS
    return w.reshape(HEADS, N_CHIPS, per, HEAD_DIM).transpose(1, 0, 2, 3).reshape(N_CHIPS, HEADS * per, HEAD_DIM)


def kernel(x, norm_mix_g, w_in, conv_w, conv_b, w_rgate, b_rgate, w_igate, b_igate, lru_lambda, w_out_a, sgu_ln_g, sgu_ln_b, sgu_w_s, sgu_b_s, w_out_b, w_out, norm_mlp_g, w_up, w_down, norm_final_g, loss_target, m_norm_mix_g, m_w_in, m_conv_w, m_conv_b, m_w_rgate, m_b_rgate, m_w_igate, m_b_igate, m_lru_lambda, m_w_out_a, m_sgu_ln_g, m_sgu_ln_b, m_sgu_w_s, m_sgu_b_s, m_w_out_b, m_w_out, m_norm_mlp_g, m_w_up, m_w_down, m_norm_final_g, v_norm_mix_g, v_w_in, v_conv_w, v_conv_b, v_w_rgate, v_b_rgate, v_w_igate, v_b_igate, v_lru_lambda, v_w_out_a, v_sgu_ln_g, v_sgu_ln_b, v_sgu_w_s, v_sgu_b_s, v_w_out_b, v_w_out, v_norm_mlp_g, v_w_up, v_w_down, v_norm_final_g):
    chip = _chip_index(lax.axis_index("x"), lax.axis_index("y"))
    core = lax.axis_index("c")
    quarter_h = HEAD_DIM // N_CHIPS
    quarter_d = D_MODEL // N_CHIPS

    as_2d = lambda a: a.reshape(-1, a.shape[-1])
    big_w = [as_2d(w) for w in (w_in, w_rgate, w_igate, w_out_a, w_out_b, w_out, w_up, w_down)]
    big_m = [as_2d(w) for w in (m_w_in, m_w_rgate, m_w_igate, m_w_out_a, m_w_out_b, m_w_out, m_w_up, m_w_down)]
    big_v = [as_2d(w) for w in (v_w_in, v_w_rgate, v_w_igate, v_w_out_a, v_w_out_b, v_w_out, v_w_up, v_w_down)]

    packed = jnp.concatenate([conv_w[0], b_rgate[0], b_igate[0]], axis=1)
    packed = jnp.concatenate([packed, jnp.zeros_like(packed)], axis=0)
    s_in, s_r, s_i, s_oa, s_ob, s_out, s_up, s_down = [w.astype(BF16) for w in big_w]
    xs, target = x[0], loss_target[0]
    g3 = norm_final_g.reshape(1, D_MODEL)
    bias_s = jnp.broadcast_to(jnp.transpose(sgu_b_s[0])[:, :, None], (CHUNK, GROUPS, GROUP_DIM)).reshape(CHUNK, D_MODEL)
    core_arr = core.reshape(1).astype(jnp.int32)
    place = jnp.stack([chip, core]).astype(jnp.int32)
    quarter = lambda g: g.reshape(N_CHIPS, D_MODEL // N_CHIPS, D_MODEL)

    def pair_add(nm, g, from_sibling):
        return _pair_add("pair_add_" + nm, core_arr, g.reshape(N_CHIPS, 2, g.shape[1] // 2, g.shape[2]), from_sibling)

    def chip_sum(nm, pair, from_chips):
        return _chip_sum("chip_sum_" + nm, place, pair, from_chips)

    order = jnp.stack([chip, chip ^ 2, chip ^ 1, chip ^ 3]).astype(jnp.int32)
    (z, n1, (w_in_st, wr_st, wi_st)), ((packed_all,), late) = _fwd_in(
        xs, norm_mix_g, [s_in, s_r, s_i], order,
        jobs=[_gather_small_job(packed), _gather_near_job([s_oa, s_ob, s_out, s_up, s_down])])
    pick = lambda lo, hi: packed_all[:, :HEADS, lo:hi].transpose(1, 0, 2).reshape(HEADS, -1)
    conv_w_full = pick(0, quarter_d)
    br_full = pick(quarter_d, quarter_d + quarter_h).reshape(1, D_MODEL)
    bi_full = pick(quarter_d + quarter_h, quarter_d + 2 * quarter_h).reshape(1, D_MODEL)
    wr, wi = _unstack_heads(wr_st), _unstack_heads(wi_st)
    lru = (conv_w_full, conv_b, wr, br_full, wi, bi_full, lru_lambda)
    sgu = (sgu_ln_g, sgu_ln_b, sgu_w_s[0], bias_s)

    (ya, *saved), (late,) = _fwd_lru(z, *lru, jobs=[_gather_far_job(late)])
    yb, (late,) = _fwd_sgu(z, *sgu, jobs=[_gather_pass_job(late)])
    w_oa, w_ob, w_o = [w.reshape(D_MODEL, D_MODEL) for w in late[:3]]
    w_up_st, w_dn = late[3], late[4].reshape(D_FF, D_MODEL)
    (pa, pb, h1, n2), _ = _fwd_merge(ya, yb, z, xs, w_oa, w_ob, w_o, norm_mlp_g)
    (act, dup, dh2b, dh1, loss_part, dg3, dg2), _ = _mlp(n2, h1, target, w_up_st, w_dn, norm_mlp_g, g3)

    d_up, _ = _weight_grad("dw_up", n2, dup, N_CHIPS, False, True, D_MODEL)
    d_down, ((r_up,),) = _weight_grad("dw_down", act, dh2b, N_CHIPS, True, False, D_MODEL,
                                      jobs=[_pair_send_job([d_up])])
    (r_down,), = _comm_call("send_w_down", [_pair_send_job([d_down])])
    p_up, p_down = pair_add("w_up", d_up, r_up), pair_add("w_down", d_down, r_down)
    (dz, merged, dpa, dpb, dh1b, dlg, dlb, dws, dbs, dcw, dcb, dwr, dbr, dwi, dbi, dlam), ((q_up, q_down),) = _bwd_mix(
        dh1, pa, pb, z, *saved, w_oa, w_ob, w_o, *sgu, conv_w_full, wr, wi, lru_lambda,
        jobs=[_chip_exchange_job([p_up, p_down])])
    half_up, half_down = chip_sum("w_up", p_up, q_up), chip_sum("w_down", p_down, q_down)
    gates = [_stack_heads(dwr).astype(BF16), _stack_heads(dwi).astype(BF16)]
    small = _pack_small(dcw, dcb, dbr, dbi, dlam, dlg, dlb, dg2, dg3, loss_part, dbs)
    d_in, ((full_up, full_down), r_gates, (vec_all, ws_all)) = _weight_grad(
        "dw_in", n1, dz, N_CHIPS, False, True, IN_SHARD,
        jobs=[_share_job([half_up, half_down]), _pair_send_job(gates), _gather_all_job([small, dws])])
    d_out, ((r_in,),) = _weight_grad("dw_out", merged, dh1b, 1, False, False, D_MODEL, jobs=[_pair_send_job([d_in])])
    d_oa, _ = _weight_grad("dw_out_a", ya, dpa, 1, False, False, D_MODEL)
    d_ob, _ = _weight_grad("dw_out_b", yb, dpb, 1, False, False, D_MODEL)
    mids = [quarter(d_oa), quarter(d_ob), quarter(d_out)]
    names = ("w_in", "w_rgate", "w_igate", "w_out_a", "w_out_b", "w_out", "w_up", "w_down")
    adam_args = {nm: (w, m, v) for nm, w, m, v in zip(names, big_w, big_m, big_v)}

    def adamw(nm, g):
        w, m, v = adam_args[nm]
        g = g.reshape(w.shape)
        return g, _adamw("adamw_" + nm, g, w, m, v)[0]

    p_first = [pair_add(nm, g, r) for nm, g, r in zip(names[:3], [d_in] + gates, [r_in] + r_gates)]
    (grad_x, dg1), (q_first, r_mids) = _bwd_in(
        dz, xs, dh1, w_in_st, norm_mix_g, jobs=[_chip_exchange_job(p_first), _pair_send_job(mids)])
    half_first = [chip_sum(nm, p, q) for nm, p, q in zip(names[:3], p_first, q_first)]
    p_mids = [pair_add(nm, g, r) for nm, g, r in zip(names[3:6], mids, r_mids)]
    q_mids, = _comm_call("exchange_mids", [_chip_exchange_job(p_mids)])
    half_mids = [chip_sum(nm, p, q) for nm, p, q in zip(names[3:6], p_mids, q_mids)]
    full_last, (dg1_all,) = _comm_call("share_last", [_share_job(half_first + half_mids), _gather_all_job([dg1])])
    full, big_out = [], []
    for nm, f in zip(names, full_last + [full_up, full_down]):
        g, out = adamw(nm, f)
        full.append(g)
        big_out.append(out)

    vec, ws_sum = _sum_small(vec_all, ws_all, dg1_all)
    row = lambda r: vec[r:r + 1]
    shard = lambda a, width: lax.dynamic_slice_in_dim(a, chip * width, width, axis=1)
    g_small = dict(
        norm_mix_g=row(ROW_G1), conv_w=shard(vec[ROW_CW:ROW_CW + CONV_WIDTH], quarter_d), conv_b=row(ROW_CB),
        b_rgate=shard(row(ROW_BR).reshape(HEADS, HEAD_DIM), quarter_h),
        b_igate=shard(row(ROW_BI).reshape(HEADS, HEAD_DIM), quarter_h), lru_lambda=row(ROW_LAM),
        sgu_ln_g=row(ROW_LG), sgu_ln_b=row(ROW_LB),
        sgu_w_s=ws_sum.reshape(CHUNK, GROUPS, CHUNK).transpose(1, 0, 2).reshape(GROUPS * CHUNK, CHUNK),
        sgu_b_s=vec[ROW_BS:ROW_BS + GROUPS, 0:CHUNK], norm_mlp_g=row(ROW_G2), norm_final_g=row(ROW_G3))
    loss = vec[ROW_LOSS, 0]
    small_names = list(g_small)
    given = dict(
        norm_mix_g=(norm_mix_g, m_norm_mix_g, v_norm_mix_g), conv_w=(conv_w, m_conv_w, v_conv_w),
        conv_b=(conv_b, m_conv_b, v_conv_b), b_rgate=(b_rgate, m_b_rgate, v_b_rgate),
        b_igate=(b_igate, m_b_igate, v_b_igate), lru_lambda=(lru_lambda, m_lru_lambda, v_lru_lambda),
        sgu_ln_g=(sgu_ln_g, m_sgu_ln_g, v_sgu_ln_g), sgu_ln_b=(sgu_ln_b, m_sgu_ln_b, v_sgu_ln_b),
        sgu_w_s=(sgu_w_s, m_sgu_w_s, v_sgu_w_s), sgu_b_s=(sgu_b_s, m_sgu_b_s, v_sgu_b_s),
        norm_mlp_g=(norm_mlp_g, m_norm_mlp_g, v_norm_mlp_g), norm_final_g=(norm_final_g, m_norm_final_g, v_norm_final_g))
    g2d = [g_small[nm] for nm in small_names]
    to2d = lambda a, g: a.reshape(g.shape)
    d_s, m_s, v_s = _adamw_small(
        g2d, *[[to2d(given[nm][q], g) for nm, g in zip(small_names, g2d)] for q in range(3)])

    shapes = dict(
        norm_mix_g=norm_mix_g, w_in=w_in, conv_w=conv_w, conv_b=conv_b, w_rgate=w_rgate, b_rgate=b_rgate,
        w_igate=w_igate, b_igate=b_igate, lru_lambda=lru_lambda, w_out_a=w_out_a, sgu_ln_g=sgu_ln_g,
        sgu_ln_b=sgu_ln_b, sgu_w_s=sgu_w_s, sgu_b_s=sgu_b_s, w_out_b=w_out_b, w_out=w_out, norm_mlp_g=norm_mlp_g,
        w_up=w_up, w_down=w_down, norm_final_g=norm_final_g)
    grads, deltas, new_m, new_v = {}, {}, {}, {}
    for nm, g, (d, nmom, nvar) in zip(names, full, big_out):
        grads[nm], deltas[nm], new_m[nm], new_v[nm] = g, d, nmom, nvar
    for p, nm in enumerate(small_names):
        grads[nm], deltas[nm], new_m[nm], new_v[nm] = g2d[p], d_s[p], m_s[p], v_s[p]
    order = list(shapes)
    out = [loss, grad_x[None]]
    for group in (grads, deltas, new_m, new_v):
        out += [group[nm].reshape(shapes[nm].shape) for nm in order]
    return tuple(out)
```

```python
import functools

import jax
import jax.numpy as jnp
from jax import lax
from jax.experimental import pallas as pl
from jax.experimental.pallas import tpu as pltpu

F32 = jnp.float32
BF16 = jnp.bfloat16
MESH = pl.DeviceIdType.MESH

D_MODEL = 1024
D_IN = 6 * D_MODEL
D_FF = 4 * D_MODEL
N_CHIPS = 4
IN_SHARD = D_IN // N_CHIPS
HEADS = 4
HEAD_DIM = D_MODEL // HEADS
GROUPS = 4
GROUP_DIM = D_MODEL // GROUPS
CHUNK = 128
CONV_WIDTH = 4
LRU_C = 8.0
NORM_EPS = 1e-6
LN_EPS = 1e-5

ADAM_LR = 0.001
ADAM_B1 = 0.9
ADAM_B2 = 0.999
ADAM_EPS = 1e-08
ADAM_WD = 0.01
ADAM_STEP = 10

SUBLANES = 8
MM_TILE = 512
SEQ_TILE = 256
DW_TILE = 2048
VMEM_LIMIT_BYTES = 56 * 1024 * 1024

GELU_K0 = 0.7978845608028654
GELU_K1 = 0.044715


def _params(n_grid_axes=1):
    return pltpu.CompilerParams(
        dimension_semantics=("arbitrary",) * n_grid_axes, vmem_limit_bytes=VMEM_LIMIT_BYTES)


def _resident(shape):
    nd = len(shape)
    return pl.BlockSpec(shape, lambda *_: (0,) * nd, pipeline_mode=pl.Buffered(1))


def _const(shape):
    nd = len(shape)
    return pl.BlockSpec(shape, lambda *_: (0,) * nd)


def _dot(a, b):
    return jnp.dot(a, b, preferred_element_type=F32)


def _dot_nt(a, b):
    return lax.dot_general(a, b, (((1,), (1,)), ((), ())), preferred_element_type=F32)


def _dot_tn(a, b):
    return lax.dot_general(a, b, (((0,), (0,)), ((), ())), preferred_element_type=F32)


def _gelu(x):
    t = jnp.tanh(GELU_K0 * x * (1.0 + GELU_K1 * x * x))
    return 0.5 * x * (1.0 + t)


def _gelu_and_grad(x):
    x2 = x * x
    t = jnp.tanh(GELU_K0 * x * (1.0 + GELU_K1 * x2))
    g = 0.5 * x * (1.0 + t)
    dg = 0.5 * (1.0 + t) + 0.5 * x * (1.0 - t * t) * (GELU_K0 * (1.0 + 3.0 * GELU_K1 * x2))
    return g, dg


def _rms(x):
    r = lax.rsqrt(jnp.mean(x * x, axis=-1, keepdims=True) + NORM_EPS)
    return x * r, r


def _rms_bwd(dn, xhat, r):
    return r * (dn - xhat * jnp.mean(dn * xhat, axis=-1, keepdims=True))


def _col_sum(v):
    return jnp.sum(v, axis=0, keepdims=True)


def _shift_down(x, tail8, k):
    xs = pltpu.roll(x, k, 0)
    ts = pltpu.roll(tail8, k, 0)
    ridx = lax.broadcasted_iota(jnp.int32, tail8.shape, 0)
    head = jnp.where(ridx < k, ts, xs[0:SUBLANES])
    return jnp.concatenate([head, xs[SUBLANES:]], axis=0)


def _shift_up(x, head8, k):
    n = x.shape[0]
    xs = pltpu.roll(x, n - k, 0)
    hs = pltpu.roll(head8, SUBLANES - k, 0)
    ridx = lax.broadcasted_iota(jnp.int32, head8.shape, 0)
    last = jnp.where(ridx >= SUBLANES - k, hs, xs[n - SUBLANES:n])
    return jnp.concatenate([xs[:n - SUBLANES], last], axis=0)


def _scan_forward(a, b, carry):
    n, cols = a.shape
    groups = n // SUBLANES
    a = a.reshape(groups, SUBLANES, cols)
    b = b.reshape(groups, SUBLANES, cols)
    sub = lax.broadcasted_iota(jnp.int32, a.shape, 1)
    for s in (1, 2, 4):
        a_s = pltpu.roll(a, s, 1)
        b_s = pltpu.roll(b, s, 1)
        m = sub >= s
        b = jnp.where(m, a * b_s + b, b)
        a = jnp.where(m, a * a_s, a)
    out = []
    for g in range(groups):
        h = a[g] * carry + b[g]
        out.append(h)
        carry = h[SUBLANES - 1:SUBLANES]
    return jnp.concatenate(out, axis=0), carry


def _scan_backward(a, b, carry):
    n, cols = a.shape
    groups = n // SUBLANES
    a = a.reshape(groups, SUBLANES, cols)
    b = b.reshape(groups, SUBLANES, cols)
    sub = lax.broadcasted_iota(jnp.int32, a.shape, 1)
    for s in (1, 2, 4):
        a_s = pltpu.roll(a, SUBLANES - s, 1)
        b_s = pltpu.roll(b, SUBLANES - s, 1)
        m = sub < SUBLANES - s
        b = jnp.where(m, a * b_s + b, b)
        a = jnp.where(m, a * a_s, a)
    out = [None] * groups
    for g in reversed(range(groups)):
        h = a[g] * carry + b[g]
        out[g] = h
        carry = h[0:1]
    return jnp.concatenate(out, axis=0), carry


def _softplus_neg(lam):
    e = jnp.exp(-jnp.abs(lam))
    u = 1.0 + e
    log1p_e = jnp.where(u == 1.0, e, jnp.log(u) * (e / jnp.where(u == 1.0, 1.0, u - 1.0)))
    return jnp.maximum(-lam, 0.0) + log1p_e


def _lru_gates(xa, tail8, cw_ref, cb_ref, wr_ref, br_ref, wi_ref, bi_ref, lam_ref):
    cw = cw_ref[...]
    xc = cb_ref[...] + cw[0:1] * xa
    for k in range(1, CONV_WIDTH):
        xc = xc + cw[k:k + 1] * _shift_down(xa, tail8, k)
    xcb = xc.astype(BF16)
    pre_r, pre_i = [], []
    for h in range(HEADS):
        cols = slice(h * HEAD_DIM, (h + 1) * HEAD_DIM)
        pre_r.append(_dot(xcb[:, cols], wr_ref[h]))
        pre_i.append(_dot(xcb[:, cols], wi_ref[h]))
    r = jax.nn.sigmoid(jnp.concatenate(pre_r, axis=1) + br_ref[...])
    ig = jax.nn.sigmoid(jnp.concatenate(pre_i, axis=1) + bi_ref[...])
    _, a, mult = _decay(r, lam_ref)
    return xc, r, ig, a, mult


def _decay(r, lam_ref):
    sp = _softplus_neg(lam_ref[...])
    log_a = ((-LRU_C) * sp) * r
    a = jnp.exp(log_a)
    th = jnp.tanh(log_a)
    return sp, a, jnp.sqrt((-2.0 * th) / (1.0 - th))


class _Job:
    def __init__(self, inputs, out_shape, n_sem, copies, aliases=None, n_local=0):
        self.inputs, self.out_shape, self.n_sem, self.copies = list(inputs), list(out_shape), n_sem, copies
        self.aliases, self.n_local = dict(aliases or {}), n_local


def _fused_call(body, jobs, *, name, grid, in_specs, out_specs, out_shape, scratch_shapes=(),
                input_output_aliases=None, compiler_params=None, n_prefetch=0, jobs_start_after=None):
    single = not isinstance(out_shape, (list, tuple))
    out_specs = [out_specs] if single else list(out_specs)
    out_shape = [out_shape] if single else list(out_shape)
    n_scr = len(scratch_shapes)
    in_specs, scratch_shapes = list(in_specs), list(scratch_shapes)
    n_in, n_out = len(in_specs), len(out_shape)
    aliases = dict(input_output_aliases or {})
    in_at, out_at = [], []
    for job in jobs:
        in_at.append(len(in_specs))
        out_at.append(len(out_shape))
        for i, o in job.aliases.items():
            aliases[n_prefetch + len(in_specs) + i] = len(out_shape) + o
        in_specs += [ANY] * len(job.inputs)
        out_specs += [ANY] * len(job.out_shape)
        out_shape += job.out_shape
        scratch_shapes += [pltpu.SemaphoreType.DMA((job.n_sem,)), pltpu.SemaphoreType.DMA((job.n_sem,)),
                           pltpu.SemaphoreType.DMA((max(job.n_local, 1),))]
    n_in_all, n_out_all = len(in_specs), len(out_shape)

    def full_body(*refs):
        prefetch, refs = refs[:n_prefetch], refs[n_prefetch:]
        ins, outs, scr = refs[:n_in_all], refs[n_in_all:n_in_all + n_out_all], refs[n_in_all + n_out_all:]

        def copies(q):
            job = jobs[q]
            return job.copies(ins[in_at[q]:in_at[q] + len(job.inputs)], outs[out_at[q]:out_at[q] + len(job.out_shape)],
                              *scr[n_scr + 3 * q:n_scr + 3 * q + 3])

        def start():
            for q in range(len(jobs)):
                sends, _, local = copies(q)
                for cp in local + sends:
                    cp.start()

        def finish():
            every = [copies(q) for q in range(len(jobs))]
            for _, arrivals, _ in every:
                for cp in arrivals:
                    cp.wait_recv()
            for sends, _, local in every:
                for cp in sends:
                    cp.wait_send()
                for cp in local:
                    cp.wait()

        if not grid:
            start()
            finish()
            return
        ids = [pl.program_id(a) for a in range(len(grid))]
        at_step = lambda step: functools.reduce(jnp.logical_and, [i == k for i, k in zip(ids, step)])
        if jobs and jobs_start_after is None:
            pl.when(at_step((0,) * len(grid)))(start)
        body(*prefetch, *ins[:n_in], *outs[:n_out], *scr[:n_scr])
        if jobs and jobs_start_after is not None:
            pl.when(at_step(jobs_start_after))(start)
        if jobs:
            pl.when(functools.reduce(jnp.logical_and, [i == g - 1 for i, g in zip(ids, grid)]))(finish)

    if n_prefetch:
        layout = dict(grid_spec=pltpu.PrefetchScalarGridSpec(
            num_scalar_prefetch=n_prefetch, grid=grid, in_specs=in_specs, out_specs=out_specs,
            scratch_shapes=scratch_shapes))
    else:
        layout = dict(grid=grid, in_specs=in_specs, out_specs=out_specs, scratch_shapes=scratch_shapes)
    call = pl.pallas_call(
        full_body, name=name, out_shape=out_shape, input_output_aliases=aliases, compiler_params=compiler_params,
        **layout)

    def run(*args):
        res = call(*args, *[a for job in jobs for a in job.inputs])
        mine = res[0] if single else list(res[:n_out])
        return mine, [list(res[at:at + len(job.out_shape)]) for at, job in zip(out_at, jobs)]

    return run


def _fwd_in(x, g1, shards, order, jobs=()):
    t = x.shape[0]
    n_tiles = t // MM_TILE
    n = len(shards)
    halves = [s.shape[0] // 2 for s in shards]

    def body(order_ref, x_ref, g_ref, *refs):
        del order_ref
        ins, (z_ref, n_ref), outs = refs[:n], refs[n:n + 2], refs[n + 2:2 * n + 2]
        wbuf, nbuf, send, recv, local = refs[2 * n + 2:]
        s, i = pl.program_id(0), pl.program_id(1)
        x_, y_, c, chips = _place()
        k_me = _chip_index(x_, y_)

        def block(w, chip, pc):
            return outs[w].at[_chip_index(*chip), pl.ds(pc * halves[w], halves[w]), :]

        def over_ici(w, j, landing):
            return pltpu.make_async_remote_copy(
                src_ref=ins[w].at[pl.ds(c * halves[w], halves[w]), :],
                dst_ref=block(w, chips[j] if landing else (x_, y_), c), send_sem=send.at[6 * w + j],
                recv_sem=recv.at[6 * w + j], device_id=(*chips[j], c), device_id_type=MESH)

        def to_sibling(w, j, landing):
            blk = block(w, chips[j], 1 - c if landing else c)
            return pltpu.make_async_remote_copy(
                src_ref=blk, dst_ref=blk, send_sem=send.at[6 * w + 3 + j], recv_sem=recv.at[6 * w + 3 + j],
                device_id=(x_, y_, 1 - c), device_id_type=MESH)

        own = [pltpu.make_async_copy(wbuf, outs[0].at[k_me], local.at[0])]
        own += [pltpu.make_async_copy(ins[w], outs[w].at[k_me], local.at[w]) for w in range(1, n)]

        @pl.when((s == 0) & (i == 0))
        def _():
            for j in range(2):
                for w in range(n):
                    over_ici(w, j, False).start()
            load = pltpu.make_async_copy(ins[0], wbuf, local.at[n])
            load.start()
            load.wait()
            for cp in own:
                cp.start()

        for j in range(N_CHIPS - 1):
            @pl.when((s == j + 1) & (i == 0))
            def _(j=j):
                for w in range(n):
                    over_ici(w, j, True).wait_recv()
                for w in range(n):
                    to_sibling(w, j, False).start()
                if j == 0:
                    for w in range(n):
                        over_ici(w, 2, False).start()
                    own[0].wait()
                for w in range(n):
                    to_sibling(w, j, True).wait_recv()
                load = pltpu.make_async_copy(outs[0].at[_chip_index(*chips[j])], wbuf, local.at[n])
                load.start()
                load.wait()

        rows = pl.ds(pl.multiple_of(i * MM_TILE, MM_TILE), MM_TILE)

        @pl.when(s == 0)
        def _():
            xhat, _ = _rms(x_ref[...])
            nrm = (xhat * g_ref[...]).astype(BF16)
            nbuf[rows, :] = nrm
            n_ref[...] = nrm

        z_ref[...] = _dot(nbuf[rows, :], wbuf[...])

        @pl.when((s == N_CHIPS - 1) & (i == n_tiles - 1))
        def _():
            for j in range(N_CHIPS - 1):
                for w in range(n):
                    over_ici(w, j, False).wait_send()
                    to_sibling(w, j, False).wait_send()
            for cp in own[1:]:
                cp.wait()

    once = lambda s, i, order: (jnp.where(s == 0, i, n_tiles - 1), 0)
    (z, n1, *stacked), job_outs = _fused_call(
        body, jobs, name="fwd_in", grid=(N_CHIPS, n_tiles), n_prefetch=1,
        in_specs=[pl.BlockSpec((MM_TILE, D_MODEL), once), _const((1, D_MODEL))] + [ANY] * n,
        out_specs=[pl.BlockSpec((MM_TILE, IN_SHARD), lambda s, i, order: (i, order[s])),
                   pl.BlockSpec((MM_TILE, D_MODEL), once)] + [ANY] * n,
        out_shape=[jax.ShapeDtypeStruct((t, D_IN), F32), jax.ShapeDtypeStruct((t, D_MODEL), BF16)]
        + [jax.ShapeDtypeStruct((N_CHIPS,) + s.shape, s.dtype) for s in shards],
        scratch_shapes=[pltpu.VMEM(shards[0].shape, BF16), pltpu.VMEM((t, D_MODEL), BF16),
                        pltpu.SemaphoreType.DMA((6 * n,)),
                        pltpu.SemaphoreType.DMA((6 * n,)), pltpu.SemaphoreType.DMA((n + 1,))],
        compiler_params=_params(2), jobs_start_after=(N_CHIPS - 1, 0),
    )(order, x, g1, *shards)
    return (z, n1, stacked), job_outs


def _fwd_lru(z, conv_w, conv_b, wr, br, wi, bi, lam, jobs=()):
    t = z.shape[0]

    def body(xa_ref, ga_ref, cw_ref, cb_ref, wr_ref, br_ref, wi_ref, bi_ref, lam_ref, ya_ref, h_ref, xc_ref, r_ref,
             ig_ref, tail_ref, carry_ref):
        @pl.when(pl.program_id(0) == 0)
        def _():
            tail_ref[...] = jnp.zeros_like(tail_ref)
            carry_ref[...] = jnp.zeros_like(carry_ref)

        xa = xa_ref[...]
        xc, r, ig, a, mult = _lru_gates(xa, tail_ref[...], cw_ref, cb_ref, wr_ref, br_ref, wi_ref, bi_ref, lam_ref)
        tail_ref[...] = xa[SEQ_TILE - SUBLANES:]
        xc_ref[...], r_ref[...], ig_ref[...] = xc, r, ig
        h, carry = _scan_forward(a, xc * ig * mult, carry_ref[...])
        carry_ref[...] = carry
        h_ref[...] = h
        ya_ref[...] = (h * _gelu(ga_ref[...])).astype(BF16)

    tile = lambda j: pl.BlockSpec((SEQ_TILE, D_MODEL), lambda i: (i, j))
    return _fused_call(
        body, jobs, name="fwd_lru", grid=(t // SEQ_TILE,),
        in_specs=[tile(0), tile(1), _const((CONV_WIDTH, D_MODEL)), _const((1, D_MODEL)),
                  _resident((HEADS, HEAD_DIM, HEAD_DIM)), _const((1, D_MODEL)),
                  _resident((HEADS, HEAD_DIM, HEAD_DIM)), _const((1, D_MODEL)), _const((1, D_MODEL))],
        out_specs=[tile(0)] * 5,
        out_shape=[jax.ShapeDtypeStruct((t, D_MODEL), BF16)] + [jax.ShapeDtypeStruct((t, D_MODEL), F32)] * 4,
        scratch_shapes=[pltpu.VMEM((SUBLANES, D_MODEL), F32), pltpu.VMEM((1, D_MODEL), F32)],
        compiler_params=_params(),
    )(z, z, conv_w, conv_b, wr, br, wi, bi, lam)


def _sgu_forward_parts(ub, vb, lg_ref, lb_ref):
    u, du = _gelu_and_grad(ub)
    vg, dvg = _gelu_and_grad(vb)
    mu = jnp.mean(vg, axis=-1, keepdims=True)
    d = vg - mu
    rstd = lax.rsqrt(jnp.mean(d * d, axis=-1, keepdims=True) + LN_EPS)
    vhat = d * rstd
    vn = (vhat * lg_ref[...] + lb_ref[...]).astype(BF16)
    return u, du, dvg, rstd, vhat, vn


def _causal_mask():
    rows = lax.broadcasted_iota(jnp.int32, (CHUNK, CHUNK), 0)
    cols = lax.broadcasted_iota(jnp.int32, (CHUNK, CHUNK), 1)
    return rows >= cols


def _fwd_sgu(z, ln_g, ln_b, w_s, bias_full, jobs=()):
    t = z.shape[0]

    def body(ub_ref, vb_ref, lg_ref, lb_ref, ws_ref, bias_ref, yb_ref):
        u, _, _, _, _, vn = _sgu_forward_parts(ub_ref[...], vb_ref[...], lg_ref, lb_ref)
        mask = _causal_mask()
        wm = [jnp.where(mask, ws_ref[g], 0.0).astype(BF16) for g in range(GROUPS)]
        for c in range(SEQ_TILE // CHUNK):
            rows = slice(c * CHUNK, (c + 1) * CHUNK)
            for g in range(GROUPS):
                cols = slice(g * GROUP_DIM, (g + 1) * GROUP_DIM)
                sp = _dot(wm[g], vn[rows, cols]) + bias_ref[:, cols]
                yb_ref[rows, cols] = (u[rows, cols] * sp).astype(BF16)

    tile = lambda j: pl.BlockSpec((SEQ_TILE, D_MODEL), lambda i: (i, j))
    return _fused_call(
        body, jobs, name="fwd_sgu", grid=(t // SEQ_TILE,),
        in_specs=[tile(2), tile(3), _const((1, D_MODEL)), _const((1, D_MODEL)),
                  _const((GROUPS, CHUNK, CHUNK)), _const((CHUNK, D_MODEL))],
        out_specs=tile(0),
        out_shape=jax.ShapeDtypeStruct((t, D_MODEL), BF16),
        compiler_params=_params(),
    )(z, z, ln_g, ln_b, w_s, bias_full)


def _fwd_merge(ya, yb, z, x, w_oa, w_ob, w_out, g2, jobs=()):
    t = x.shape[0]

    def body(ya_ref, yb_ref, m_ref, x_ref, woa_ref, wob_ref, wout_ref, g_ref, pa_ref, pb_ref, h1_ref, n2_ref):
        pa = _dot(ya_ref[...], woa_ref[...])
        pb = _dot(yb_ref[...], wob_ref[...])
        pa_ref[...] = pa
        pb_ref[...] = pb
        merged = jax.nn.sigmoid(m_ref[:, :D_MODEL]) * pa + jax.nn.sigmoid(m_ref[:, D_MODEL:]) * pb
        h1 = x_ref[...] + _dot(merged.astype(BF16), wout_ref[...])
        h1_ref[...] = h1
        xhat, _ = _rms(h1)
        n2_ref[...] = (xhat * g_ref[...]).astype(BF16)

    tile = pl.BlockSpec((MM_TILE, D_MODEL), lambda i: (i, 0))
    sq = _resident((D_MODEL, D_MODEL))
    return _fused_call(
        body, jobs, name="fwd_merge", grid=(t // MM_TILE,),
        in_specs=[tile, tile, pl.BlockSpec((MM_TILE, 2 * D_MODEL), lambda i: (i, 2)), tile, sq, sq, sq,
                  _const((1, D_MODEL))],
        out_specs=[tile, tile, tile, tile],
        out_shape=[jax.ShapeDtypeStruct((t, D_MODEL), F32)] * 3 + [jax.ShapeDtypeStruct((t, D_MODEL), BF16)],
        compiler_params=_params(),
    )(ya, yb, z, x, w_oa, w_ob, w_out, g2)


def _mlp(n2, h1, target, w_up_st, w_down, g2, g3, jobs=()):
    t = n2.shape[0]

    def body(n2_ref, h1_ref, tgt_ref, wup_ref, wdown_ref, g2_ref, g3_ref, act_ref, dup_ref, dh2b_ref, dh1_ref,
             loss_ref, dg3_ref, dg2_ref, relu_ref):
        @pl.when(pl.program_id(0) == 0)
        def _():
            for ref in (loss_ref, dg3_ref, dg2_ref):
                ref[...] = jnp.zeros_like(ref)

        n2 = n2_ref[...]
        h1 = h1_ref[...]
        h2 = h1
        for k in range(N_CHIPS):
            cols = slice(k * D_MODEL, (k + 1) * D_MODEL)
            r = jnp.maximum(_dot(n2, wup_ref[k]), 0.0)
            relu_ref[:, cols] = r
            act = (r * r).astype(BF16)
            act_ref[:, cols] = act
            h2 = h2 + _dot(act, wdown_ref[cols, :])
        xhat, r3 = _rms(h2)
        diff = xhat * g3_ref[...] - tgt_ref[...]
        sq = jnp.sum(diff * diff, axis=1, keepdims=True)
        loss_ref[...] = loss_ref[...] + (0.5 / D_MODEL) * jnp.sum(sq, axis=0, keepdims=True)
        dy = diff * (1.0 / D_MODEL)
        dg3_ref[...] = dg3_ref[...] + _col_sum(dy * xhat)
        dh2 = _rms_bwd(dy * g3_ref[...], xhat, r3)
        dh2b = dh2.astype(BF16)
        dh2b_ref[...] = dh2b
        dn2 = jnp.zeros((SEQ_TILE, D_MODEL), F32)
        for k in range(N_CHIPS):
            cols = slice(k * D_MODEL, (k + 1) * D_MODEL)
            dup = (_dot_nt(dh2b, wdown_ref[cols, :]) * (2.0 * relu_ref[:, cols])).astype(BF16)
            dup_ref[:, cols] = dup
            dn2 = dn2 + _dot_nt(dup, wup_ref[k])
        xhat, r2 = _rms(h1)
        dg2_ref[...] = dg2_ref[...] + _col_sum(dn2 * xhat)
        dh1_ref[...] = dh2 + _rms_bwd(dn2 * g2_ref[...], xhat, r2)

    tile = pl.BlockSpec((SEQ_TILE, D_MODEL), lambda i: (i, 0))
    wide = pl.BlockSpec((SEQ_TILE, D_FF), lambda i: (i, 0))
    vec = _const((1, D_MODEL))
    vec_shape = jax.ShapeDtypeStruct((1, D_MODEL), F32)
    return _fused_call(
        body, jobs, name="mlp", grid=(t // SEQ_TILE,),
        in_specs=[tile, tile, tile, _resident((N_CHIPS, D_MODEL, D_MODEL)), _resident((D_FF, D_MODEL)), vec, vec],
        out_specs=[wide, wide, tile, tile, _const((SUBLANES, 128)), vec, vec],
        out_shape=[jax.ShapeDtypeStruct((t, D_FF), BF16), jax.ShapeDtypeStruct((t, D_FF), BF16),
                   jax.ShapeDtypeStruct((t, D_MODEL), BF16), jax.ShapeDtypeStruct((t, D_MODEL), F32),
                   jax.ShapeDtypeStruct((SUBLANES, 128), F32), vec_shape, vec_shape],
        scratch_shapes=[pltpu.VMEM((SEQ_TILE, D_FF), F32)],
        compiler_params=_params(),
    )(n2, h1, target, w_up_st, w_down, g2, g3)


def _bwd_mix(dh1, pa, pb, z, h, xc, r, ig, w_oa, w_ob, w_out, ln_g, ln_b, w_s, bias_full, conv_w, wr, wi, lam, jobs=()):
    t = dh1.shape[0]
    n_tiles = t // SEQ_TILE
    per_tile = SEQ_TILE // SUBLANES

    def merge_part(dh1_ref, pa_ref, pb_ref, m_ref, woa_ref, wob_ref, wout_ref, dz_ref, dya_ref, dyb_ref, mg_ref,
                   dpa_ref, dpb_ref, dh1b_ref):
        dh1b = dh1_ref[...].astype(BF16)
        dh1b_ref[...] = dh1b
        dm = _dot_nt(dh1b, wout_ref[...])
        pa = pa_ref[...]
        pb = pb_ref[...]
        sa = jax.nn.sigmoid(m_ref[:, :D_MODEL])
        sb = jax.nn.sigmoid(m_ref[:, D_MODEL:])
        mg_ref[...] = (sa * pa + sb * pb).astype(BF16)
        dz_ref[:, :D_MODEL] = (dm * pa * sa * (1.0 - sa)).astype(BF16)
        dz_ref[:, D_MODEL:] = (dm * pb * sb * (1.0 - sb)).astype(BF16)
        dpa = (dm * sa).astype(BF16)
        dpb = (dm * sb).astype(BF16)
        dpa_ref[...] = dpa
        dpb_ref[...] = dpb
        dya_ref[...] = _dot_nt(dpa, woa_ref[...])
        dyb_ref[...] = _dot_nt(dpb, wob_ref[...])

    def sgu_part(dyb_ref, ub_ref, vb_ref, lg_ref, lb_ref, ws_ref, bias_ref, dz_ref, dlg_ref, dlb_ref, dws_ref, dbs_ref,
                 dvn_ref, dsp_acc):
        i = pl.program_id(0)

        @pl.when(i == 0)
        def _():
            dlg_ref[...] = jnp.zeros_like(dlg_ref)
            dlb_ref[...] = jnp.zeros_like(dlb_ref)
            dws_ref[...] = jnp.zeros_like(dws_ref)
            dsp_acc[...] = jnp.zeros_like(dsp_acc)

        u, du, dvg, rstd, vhat, vn = _sgu_forward_parts(ub_ref[...], vb_ref[...], lg_ref, lb_ref)
        dyb = dyb_ref[...]
        mask = _causal_mask()
        wm = [jnp.where(mask, ws_ref[g], 0.0).astype(BF16) for g in range(GROUPS)]
        for c in range(SEQ_TILE // CHUNK):
            rows = slice(c * CHUNK, (c + 1) * CHUNK)
            for g in range(GROUPS):
                cols = slice(g * GROUP_DIM, (g + 1) * GROUP_DIM)
                vn_blk = vn[rows, cols]
                sp = _dot(wm[g], vn_blk) + bias_ref[:, cols]
                dyb_blk = dyb[rows, cols]
                dz_ref[rows, cols] = (dyb_blk * sp * du[rows, cols]).astype(BF16)
                dsp = dyb_blk * u[rows, cols]
                dsp_acc[:, cols] = dsp_acc[:, cols] + dsp
                dspb = dsp.astype(BF16)
                dvn_ref[rows, cols] = _dot_tn(wm[g], dspb)
                wcols = slice(g * CHUNK, (g + 1) * CHUNK)
                dws_ref[:, wcols] = dws_ref[:, wcols] + jnp.where(mask, _dot_nt(dspb, vn_blk), 0.0)
        dvn = dvn_ref[...]
        dlg_ref[...] = dlg_ref[...] + _col_sum(dvn * vhat)
        dlb_ref[...] = dlb_ref[...] + _col_sum(dvn)
        dvhat = dvn * lg_ref[...]
        dvgel = rstd * (dvhat - jnp.mean(dvhat, axis=-1, keepdims=True)
                        - vhat * jnp.mean(dvhat * vhat, axis=-1, keepdims=True))
        dz_ref[:, D_MODEL:] = (dvgel * dvg).astype(BF16)

        @pl.when(i == n_tiles - 1)
        def _():
            lane = lax.broadcasted_iota(jnp.int32, (CHUNK, 128), 1)
            out = jnp.zeros((CHUNK, 128), F32)
            for g in range(GROUPS):
                s = jnp.sum(dsp_acc[:, g * GROUP_DIM:(g + 1) * GROUP_DIM], axis=1, keepdims=True)
                out = out + jnp.where(lane == g, s, 0.0)
            dbs_ref[...] = out

    def lru_part(dya_ref, xa_ref, ga_ref, h_ref, h_prev_ref, xc_ref, r_ref, ig_ref, cw_ref, wr_ref, wi_ref, lam_ref,
                 dz_ref, dcw_ref, dcb_ref, dwr_ref, dbr_ref, dwi_ref, dbi_ref, dlam_ref, lam_carry, dxc_head):
        i = pl.program_id(0)

        @pl.when(i == 0)
        def _():
            for ref in (dcw_ref, dcb_ref, dwr_ref, dbr_ref, dwi_ref, dbi_ref, dlam_ref, lam_carry, dxc_head):
                ref[...] = jnp.zeros_like(ref)

        first_tile = i == n_tiles - 1
        h_tail = jnp.where(first_tile, 0.0, h_prev_ref[...])
        xc, r, ig = xc_ref[...], r_ref[...], ig_ref[...]
        xcb = xc.astype(BF16)
        sp, a, mult = _decay(r, lam_ref)
        h = h_ref[...]
        h_prev = _shift_down(h, h_tail, 1)
        dya = dya_ref[...]
        gg, dgg = _gelu_and_grad(ga_ref[...])
        dz_ref[:, D_MODEL:] = (dya * h * dgg).astype(BF16)
        ones = jnp.ones((SUBLANES, D_MODEL), F32)
        lam_t, lam_first = _scan_backward(_shift_up(a, ones, 1), dya * gg, lam_carry[...])
        lam_carry[...] = a[0:1] * lam_first
        dmult = lam_t * xc * ig
        dla = lam_t * h_prev * a - dmult * (a * a) / mult
        dr = dla * ((-LRU_C) * sp)
        dlam_ref[...] = dlam_ref[...] + _col_sum(dla * r) * (LRU_C * jax.nn.sigmoid(-lam_ref[...]))
        dpr = dr * r * (1.0 - r)
        dpi = lam_t * xc * mult * ig * (1.0 - ig)
        dbr_ref[...] = dbr_ref[...] + _col_sum(dpr)
        dbi_ref[...] = dbi_ref[...] + _col_sum(dpi)
        dprb = dpr.astype(BF16)
        dpib = dpi.astype(BF16)
        dxc_gate = []
        for hd in range(HEADS):
            cols = slice(hd * HEAD_DIM, (hd + 1) * HEAD_DIM)
            dxc_gate.append(_dot_nt(dprb[:, cols], wr_ref[hd]) + _dot_nt(dpib[:, cols], wi_ref[hd]))
            dwr_ref[hd] = dwr_ref[hd] + _dot_tn(xcb[:, cols], dprb[:, cols])
            dwi_ref[hd] = dwi_ref[hd] + _dot_tn(xcb[:, cols], dpib[:, cols])
        dxc = lam_t * ig * mult + jnp.concatenate(dxc_gate, axis=1)
        dcb_ref[...] = dcb_ref[...] + _col_sum(dxc)
        cw = cw_ref[...]
        head = dxc_head[...]
        xa = xa_ref[...]
        dxa = cw[0:1] * dxc
        dcw_ref[0:1, :] = dcw_ref[0:1, :] + _col_sum(dxc * xa)
        for k in range(1, CONV_WIDTH):
            dxc_k = _shift_up(dxc, head, k)
            dxa = dxa + cw[k:k + 1] * dxc_k
            dcw_ref[k:k + 1, :] = dcw_ref[k:k + 1, :] + _col_sum(dxc_k * xa)
        dxc_head[...] = dxc[0:SUBLANES]
        dz_ref[:, :D_MODEL] = dxa.astype(BF16)

    def body(dh1_ref, pa_ref, pb_ref, z_ref, h_ref, h_prev_ref, xc_ref, r_ref, ig_ref, woa_ref, wob_ref, wout_ref,
             lg_ref, lb_ref, ws_ref, bias_ref, cw_ref, wr_ref, wi_ref, lam_ref, dz_ref, mg_ref, dpa_ref, dpb_ref,
             dh1b_ref, dlg_ref, dlb_ref, dws_ref, dbs_ref, dcw_ref, dcb_ref, dwr_ref, dbr_ref, dwi_ref, dbi_ref,
             dlam_ref, dya_ref, dyb_ref, dvn_ref, dsp_acc, lam_carry, dxc_head):
        def cols(ref, first, count):
            return ref.at[:, pl.ds(first * D_MODEL, count * D_MODEL)]

        merge_part(dh1_ref, pa_ref, pb_ref, cols(z_ref, 4, 2), woa_ref, wob_ref, wout_ref, cols(dz_ref, 4, 2), dya_ref,
                   dyb_ref, mg_ref, dpa_ref, dpb_ref, dh1b_ref)
        sgu_part(dyb_ref, cols(z_ref, 2, 1), cols(z_ref, 3, 1), lg_ref, lb_ref, ws_ref, bias_ref, cols(dz_ref, 2, 2),
                 dlg_ref, dlb_ref, dws_ref, dbs_ref, dvn_ref, dsp_acc)
        lru_part(dya_ref, cols(z_ref, 0, 1), cols(z_ref, 1, 1), h_ref, h_prev_ref, xc_ref, r_ref, ig_ref, cw_ref, wr_ref,
                 wi_ref, lam_ref, cols(dz_ref, 0, 2), dcw_ref, dcb_ref, dwr_ref, dbr_ref, dwi_ref, dbi_ref, dlam_ref,
                 lam_carry, dxc_head)

    rev = lambda i: n_tiles - 1 - i
    tile = pl.BlockSpec((SEQ_TILE, D_MODEL), lambda i: (rev(i), 0))
    row = pl.BlockSpec((SEQ_TILE, D_IN), lambda i: (rev(i), 0))
    prev8 = pl.BlockSpec((SUBLANES, D_MODEL), lambda i: (jnp.maximum(rev(i) * per_tile - 1, 0), 0))
    vec = _const((1, D_MODEL))
    sq = _resident((D_MODEL, D_MODEL))
    gate_w = _resident((HEADS, HEAD_DIM, HEAD_DIM))
    gate_acc = _const((HEADS, HEAD_DIM, HEAD_DIM))
    vec_shape = jax.ShapeDtypeStruct((1, D_MODEL), F32)
    gate_shape = jax.ShapeDtypeStruct((HEADS, HEAD_DIM, HEAD_DIM), F32)
    act_bf = jax.ShapeDtypeStruct((t, D_MODEL), BF16)
    return _fused_call(
        body, jobs, name="bwd_mix", grid=(n_tiles,),
        in_specs=[tile, tile, tile, row, tile, prev8, tile, tile, tile, sq, sq, sq, vec, vec,
                  _const((GROUPS, CHUNK, CHUNK)), _const((CHUNK, D_MODEL)), _const((CONV_WIDTH, D_MODEL)), gate_w, gate_w,
                  vec],
        out_specs=[row, tile, tile, tile, tile, vec, vec, _const((CHUNK, GROUPS * CHUNK)), _const((CHUNK, 128)),
                   _const((SUBLANES, D_MODEL)), vec, gate_acc, vec, gate_acc, vec, vec],
        out_shape=[jax.ShapeDtypeStruct((t, D_IN), BF16), act_bf, act_bf, act_bf, act_bf, vec_shape, vec_shape,
                   jax.ShapeDtypeStruct((CHUNK, GROUPS * CHUNK), F32), jax.ShapeDtypeStruct((CHUNK, 128), F32),
                   jax.ShapeDtypeStruct((SUBLANES, D_MODEL), F32), vec_shape, gate_shape, vec_shape, gate_shape,
                   vec_shape, vec_shape],
        scratch_shapes=[pltpu.VMEM((SEQ_TILE, D_MODEL), F32), pltpu.VMEM((SEQ_TILE, D_MODEL), F32),
                        pltpu.VMEM((SEQ_TILE, D_MODEL), F32), pltpu.VMEM((CHUNK, D_MODEL), F32),
                        pltpu.VMEM((1, D_MODEL), F32), pltpu.VMEM((SUBLANES, D_MODEL), F32)],
        compiler_params=_params(),
    )(dh1, pa, pb, z, h, h, xc, r, ig, w_oa, w_ob, w_out, ln_g, ln_b, w_s, bias_full, conv_w, wr, wi, lam)


def _bwd_in(dz, x, dh1, w_in_st, g1, jobs=()):
    t = x.shape[0]

    def body(dz_ref, x_ref, dh1_ref, w_ref, g_ref, dx_ref, dg1_ref):
        @pl.when(pl.program_id(0) == 0)
        def _():
            dg1_ref[...] = jnp.zeros_like(dg1_ref)

        dn1 = jnp.zeros((MM_TILE, D_MODEL), F32)
        for k in range(N_CHIPS):
            dn1 = dn1 + _dot_nt(dz_ref[:, k * IN_SHARD:(k + 1) * IN_SHARD], w_ref[k])
        xhat, r1 = _rms(x_ref[...])
        dg1_ref[...] = dg1_ref[...] + _col_sum(dn1 * xhat)
        dx_ref[...] = dh1_ref[...] + _rms_bwd(dn1 * g_ref[...], xhat, r1)

    tile = pl.BlockSpec((MM_TILE, D_MODEL), lambda i: (i, 0))
    return _fused_call(
        body, jobs, name="bwd_in", grid=(t // MM_TILE,),
        in_specs=[pl.BlockSpec((MM_TILE, D_IN), lambda i: (i, 0)), tile, tile,
                  _resident((N_CHIPS, D_MODEL, IN_SHARD)), _const((1, D_MODEL))],
        out_specs=[tile, _const((1, D_MODEL))],
        out_shape=[jax.ShapeDtypeStruct((t, D_MODEL), F32), jax.ShapeDtypeStruct((1, D_MODEL), F32)],
        compiler_params=_params(),
    )(dz, x, dh1, w_in_st, g1)


def _weight_grad(name, a, b, n_blocks, a_varies, b_varies, width, jobs=()):
    t = a.shape[0]
    rows = min(DW_TILE, t)
    n_t = t // rows

    def body(a_ref, b_ref, o_ref, acc_ref):
        s = pl.program_id(1)
        part = _dot_tn(a_ref[...], b_ref[...])

        @pl.when(s == 0)
        def _():
            acc_ref[...] = part

        @pl.when(s > 0)
        def _():
            acc_ref[...] = acc_ref[...] + part

        @pl.when(s == n_t - 1)
        def _():
            o_ref[...] = acc_ref[...].astype(BF16)

    return _fused_call(
        body, jobs, name=name, grid=(n_blocks, n_t),
        in_specs=[pl.BlockSpec((rows, D_MODEL), (lambda j, s: (s, j)) if a_varies else (lambda j, s: (s, 0))),
                  pl.BlockSpec((rows, width), (lambda j, s: (s, j)) if b_varies else (lambda j, s: (s, 0)))],
        out_specs=pl.BlockSpec((None, D_MODEL, width), lambda j, s: (j, 0, 0)),
        out_shape=jax.ShapeDtypeStruct((n_blocks, D_MODEL, width), BF16),
        scratch_shapes=[pltpu.VMEM((D_MODEL, width), F32)],
        compiler_params=_params(2),
    )(a, b)


def _place():
    x, y, c = lax.axis_index("x"), lax.axis_index("y"), lax.axis_index("c")
    other_chips = [(1 - x, y), (x, 1 - y), (1 - x, 1 - y)]
    return x, y, c, other_chips


def _chip_index(px, py):
    return 2 * px + py


ANY = pl.BlockSpec(memory_space=pl.ANY)


def _comm_call(name, jobs):
    return _fused_call(None, jobs, name=name, grid=(), in_specs=[], out_specs=[], out_shape=[])()[1]


def _near_far(x, y, c):
    return (x ^ (1 - c), y ^ c), (x ^ c, y ^ (1 - c))


def _gather_near_job(shards):
    n = len(shards)
    halves = [s.shape[0] // 2 for s in shards]

    def copies(ins, outs, send, recv, local):
        x, y, c, _ = _place()
        near, _ = _near_far(x, y, c)

        def block(w, chip, pc):
            return outs[w].at[_chip_index(*chip), pl.ds(pc * halves[w], halves[w]), :]

        def copy(w, k, chip, pc, to, src=None):
            return pltpu.make_async_remote_copy(
                src_ref=block(w, chip, pc) if src is None else src, dst_ref=block(w, chip, pc),
                send_sem=send.at[2 * w + k], recv_sem=recv.at[2 * w + k], device_id=to, device_id_type=MESH)

        sends, arrivals, own = [], [], []
        for w in range(n):
            src = ins[w].at[pl.ds(c * halves[w], halves[w]), :]
            own.append(pltpu.make_async_copy(src, block(w, (x, y), c), local.at[w]))
            sends += [copy(w, 0, (x, y), c, (*near, c), src), copy(w, 1, (x, y), c, (x, y, 1 - c), src)]
            arrivals += [copy(w, 0, near, c, (x, y, c)), copy(w, 1, (x, y), 1 - c, (x, y, c))]
        return sends, arrivals, own

    return _Job(shards, [jax.ShapeDtypeStruct((N_CHIPS,) + s.shape, s.dtype) for s in shards], 2 * n, copies,
                n_local=n)


def _gather_far_job(stacked):
    n = len(stacked)
    halves = [s.shape[1] // 2 for s in stacked]

    def copies(ins, outs, send, recv, local):
        del ins, local
        x, y, c, _ = _place()
        near, far = _near_far(x, y, c)

        def copy(w, k, chip):
            blk = outs[w].at[_chip_index(*chip), pl.ds(c * halves[w], halves[w]), :]
            return pltpu.make_async_remote_copy(
                src_ref=blk, dst_ref=blk, send_sem=send.at[2 * w + k], recv_sem=recv.at[2 * w + k],
                device_id=(*far, c), device_id_type=MESH)

        sends = [copy(w, k, chip) for w in range(n) for k, chip in enumerate(((x, y), near))]
        arrivals = [copy(w, k, chip) for w in range(n) for k, chip in enumerate((far, (1 - x, 1 - y)))]
        return sends, arrivals, []

    return _Job(stacked, [jax.ShapeDtypeStruct(s.shape, s.dtype) for s in stacked], 2 * n, copies,
                aliases={w: w for w in range(n)})


def _gather_pass_job(stacked):
    n = len(stacked)
    halves = [s.shape[1] // 2 for s in stacked]

    def copies(ins, outs, send, recv, local):
        del ins, local
        x, y, c, chips = _place()

        def copy(w, j, chip, pc, to):
            blk = outs[w].at[_chip_index(*chip), pl.ds(pc * halves[w], halves[w]), :]
            return pltpu.make_async_remote_copy(
                src_ref=blk, dst_ref=blk, send_sem=send.at[3 * w + j], recv_sem=recv.at[3 * w + j], device_id=to,
                device_id_type=MESH)

        sends = [copy(w, j, chip, c, (x, y, 1 - c)) for w in range(n) for j, chip in enumerate(chips)]
        arrivals = [copy(w, j, chip, 1 - c, (x, y, c)) for w in range(n) for j, chip in enumerate(chips)]
        return sends, arrivals, []

    return _Job(stacked, [jax.ShapeDtypeStruct(s.shape, s.dtype) for s in stacked], 3 * n, copies,
                aliases={w: w for w in range(n)})


def _gather_small_job(block):
    def copies(ins, outs, send, recv, local):
        x, y, c, chips = _place()

        def copy(j, chip_from, to):
            return pltpu.make_async_remote_copy(
                src_ref=ins[0], dst_ref=outs[0].at[_chip_index(*chip_from)], send_sem=send.at[j],
                recv_sem=recv.at[j], device_id=to, device_id_type=MESH)

        own = [pltpu.make_async_copy(ins[0], outs[0].at[_chip_index(x, y)], local.at[0])]
        sends = [copy(j, (x, y), (*chip, c)) for j, chip in enumerate(chips)]
        arrivals = [copy(j, chip, (x, y, c)) for j, chip in enumerate(chips)]
        return sends, arrivals, own

    return _Job([block], [jax.ShapeDtypeStruct((N_CHIPS,) + block.shape, block.dtype)], 3, copies, n_local=1)


def _pair_send_job(grads):
    n = len(grads)
    halves = [g.shape[1] // 2 for g in grads]

    def copies(ins, outs, send, recv, local):
        del local
        x, y, c, _ = _place()
        sends = [pltpu.make_async_remote_copy(
            src_ref=ins[w].at[:, pl.ds((1 - c) * halves[w], halves[w]), :], dst_ref=outs[w], send_sem=send.at[w],
            recv_sem=recv.at[w], device_id=(x, y, 1 - c), device_id_type=MESH) for w in range(n)]
        return sends, sends, []

    return _Job(grads, [jax.ShapeDtypeStruct((N_CHIPS, h, g.shape[2]), g.dtype) for g, h in zip(grads, halves)], n,
                copies)


def _row_block(rows, limit=256):
    return min(rows, limit)


def _pair_add(name, core, mine, theirs):
    _, _, h, cols = mine.shape
    rb = _row_block(h, 512)

    def body(core_ref, a_ref, b_ref, o_ref):
        del core_ref
        o_ref[...] = (a_ref[...].astype(F32) + b_ref[...].astype(F32)).astype(BF16)

    return pl.pallas_call(
        body, name=name,
        grid_spec=pltpu.PrefetchScalarGridSpec(
            num_scalar_prefetch=1, grid=(N_CHIPS, h // rb),
            in_specs=[pl.BlockSpec((None, None, rb, cols), lambda k, r, core_ref: (k, core_ref[0], r, 0)),
                      pl.BlockSpec((None, rb, cols), lambda k, r, core_ref: (k, r, 0))],
            out_specs=pl.BlockSpec((None, rb, cols), lambda k, r, core_ref: (k, r, 0))),
        out_shape=jax.ShapeDtypeStruct(theirs.shape, BF16),
        compiler_params=_params(2),
    )(core, mine, theirs)


def _chip_exchange_job(sums):
    n = len(sums)

    def copies(ins, outs, send, recv, local):
        del local
        _, _, c, chips = _place()
        sends = [pltpu.make_async_remote_copy(
            src_ref=ins[w].at[_chip_index(*chip)], dst_ref=outs[w].at[j], send_sem=send.at[3 * w + j],
            recv_sem=recv.at[3 * w + j], device_id=(*chip, c), device_id_type=MESH)
            for w in range(n) for j, chip in enumerate(chips)]
        return sends, sends, []

    return _Job(sums, [jax.ShapeDtypeStruct((N_CHIPS - 1,) + s.shape[1:], s.dtype) for s in sums], 3 * n, copies)


def _chip_sum(name, place, mine, theirs):
    _, h, cols = mine.shape
    rb = _row_block(h, 512)

    def body(place_ref, p_ref, q_ref, o_ref):
        del place_ref
        acc = p_ref[...].astype(F32)
        for j in range(N_CHIPS - 1):
            acc = acc + q_ref[j].astype(F32)
        o_ref[...] = acc

    return pl.pallas_call(
        body, name=name,
        grid_spec=pltpu.PrefetchScalarGridSpec(
            num_scalar_prefetch=1, grid=(h // rb,),
            in_specs=[pl.BlockSpec((None, rb, cols), lambda r, place_ref: (place_ref[0], r, 0)),
                      pl.BlockSpec((N_CHIPS - 1, rb, cols), lambda r, place_ref: (0, r, 0))],
            out_specs=pl.BlockSpec((None, rb, cols), lambda r, place_ref: (place_ref[1], r, 0))),
        out_shape=jax.ShapeDtypeStruct((2, h, cols), F32),
        compiler_params=_params(),
    )(place, mine, theirs)


def _share_job(bufs):
    n = len(bufs)

    def copies(ins, outs, send, recv, local):
        del ins, local
        x, y, c, _ = _place()

        def copy(w, half):
            return pltpu.make_async_remote_copy(
                src_ref=outs[w].at[half], dst_ref=outs[w].at[half], send_sem=send.at[w], recv_sem=recv.at[w],
                device_id=(x, y, 1 - c), device_id_type=MESH)

        return [copy(w, c) for w in range(n)], [copy(w, 1 - c) for w in range(n)], []

    return _Job(bufs, [jax.ShapeDtypeStruct(b.shape, b.dtype) for b in bufs], n, copies,
                aliases={w: w for w in range(n)})


SMALL_ROWS = 24
ROW_G1, ROW_CW, ROW_CB, ROW_BR, ROW_BI, ROW_LAM, ROW_LG, ROW_LB, ROW_G2, ROW_G3, ROW_LOSS, ROW_BS = (
    0, 1, 5, 6, 7, 8, 9, 10, 11, 12, 13, 16)
N_DEV = 8


def _pack_small(dcw, dcb, dbr, dbi, dlam, dlg, dlb, dg2, dg3, loss, dbs):
    def body(dcw_ref, dcb_ref, dbr_ref, dbi_ref, dlam_ref, dlg_ref, dlb_ref, dg2_ref, dg3_ref, loss_ref, dbs_ref, out):
        out[...] = jnp.zeros((SMALL_ROWS, D_MODEL), F32)
        for row, ref in ((ROW_CB, dcb_ref), (ROW_BR, dbr_ref), (ROW_BI, dbi_ref), (ROW_LAM, dlam_ref),
                         (ROW_LG, dlg_ref), (ROW_LB, dlb_ref), (ROW_G2, dg2_ref), (ROW_G3, dg3_ref)):
            out[row:row + 1, :] = ref[...]
        out[ROW_CW:ROW_CW + CONV_WIDTH, :] = dcw_ref[0:CONV_WIDTH, :]
        out[ROW_LOSS:ROW_LOSS + 1, 0:128] = loss_ref[0:1, :]
        out[ROW_BS:ROW_BS + GROUPS, 0:128] = jnp.transpose(dbs_ref[...])[0:GROUPS, :]

    vm = pl.BlockSpec(memory_space=pltpu.VMEM)
    return pl.pallas_call(
        body, name="pack_small", in_specs=[vm] * 11, out_specs=vm,
        out_shape=jax.ShapeDtypeStruct((SMALL_ROWS, D_MODEL), F32),
    )(dcw, dcb, dbr, dbi, dlam, dlg, dlb, dg2, dg3, loss, dbs)


def _gather_all_job(blocks):
    n = len(blocks)
    flips = [(dx, dy, dc) for dx in (0, 1) for dy in (0, 1) for dc in (0, 1)][1:]

    def copies(ins, outs, send, recv, local):
        x, y, c, _ = _place()
        me = 4 * x + 2 * y + c
        sends, arrivals, own = [], [], []
        for w in range(n):
            own.append(pltpu.make_async_copy(ins[w], outs[w].at[me], local.at[w]))
            for k, (dx, dy, dc) in enumerate(flips):
                peer = (x ^ dx, y ^ dy, c ^ dc)
                sem = dict(send_sem=send.at[7 * w + k], recv_sem=recv.at[7 * w + k])
                sends.append(pltpu.make_async_remote_copy(
                    src_ref=ins[w], dst_ref=outs[w].at[me], device_id=peer, device_id_type=MESH, **sem))
                arrivals.append(pltpu.make_async_remote_copy(
                    src_ref=ins[w], dst_ref=outs[w].at[4 * peer[0] + 2 * peer[1] + peer[2]], device_id=peer,
                    device_id_type=MESH, **sem))
        return sends, arrivals, own

    return _Job(blocks, [jax.ShapeDtypeStruct((N_DEV,) + b.shape, b.dtype) for b in blocks], 7 * n, copies, n_local=n)


def _sum_small(vec_all, ws_all, dg1_all):
    def body(vec_ref, ws_ref, dg1_ref, vec_out, ws_out):
        vec, ws, dg1 = vec_ref[0], ws_ref[0], dg1_ref[0]
        for d in range(1, N_DEV):
            vec, ws, dg1 = vec + vec_ref[d], ws + ws_ref[d], dg1 + dg1_ref[d]
        vec_out[...] = vec
        vec_out[ROW_G1:ROW_G1 + 1, :] = dg1
        ws_out[...] = ws

    vm = pl.BlockSpec(memory_space=pltpu.VMEM)
    return pl.pallas_call(
        body, name="sum_small", in_specs=[vm] * 3, out_specs=[vm, vm],
        out_shape=[jax.ShapeDtypeStruct(vec_all.shape[1:], F32), jax.ShapeDtypeStruct(ws_all.shape[1:], F32)],
    )(vec_all, ws_all, dg1_all)


def _adamw_math(w, g, m, v):
    m = ADAM_B1 * m + (1.0 - ADAM_B1) * g
    v = ADAM_B2 * v + (1.0 - ADAM_B2) * (g * g)
    m_hat = m / (1.0 - ADAM_B1 ** ADAM_STEP)
    v_hat = v / (1.0 - ADAM_B2 ** ADAM_STEP)
    delta = (-ADAM_LR) * (m_hat / (jnp.sqrt(v_hat) + ADAM_EPS) + ADAM_WD * w)
    return delta, m, v


def _adamw(name, g, w, m, v, jobs=()):
    rows, cols = w.shape
    rb = _row_block(rows)

    def body(g_ref, w_ref, m_ref, v_ref, d_ref, nm_ref, nv_ref):
        d_ref[...], nm_ref[...], nv_ref[...] = _adamw_math(w_ref[...], g_ref[...], m_ref[...], v_ref[...])

    blk = pl.BlockSpec((rb, cols), lambda r: (r, 0))
    return _fused_call(
        body, jobs, name=name, grid=(rows // rb,), in_specs=[blk] * 4, out_specs=[blk] * 3,
        out_shape=[jax.ShapeDtypeStruct(w.shape, F32)] * 3, compiler_params=_params(),
    )(g, w, m, v)


def _adamw_small(grads, ws, ms, vs):
    n = len(grads)

    def body(*refs):
        g_refs, w_refs, m_refs, v_refs = refs[:n], refs[n:2 * n], refs[2 * n:3 * n], refs[3 * n:4 * n]
        outs = refs[4 * n:]
        for p in range(n):
            d, nm, nv = _adamw_math(w_refs[p][...], g_refs[p][...], m_refs[p][...], v_refs[p][...])
            outs[p][...] = d
            outs[n + p][...] = nm
            outs[2 * n + p][...] = nv

    vm = pl.BlockSpec(memory_space=pltpu.VMEM)
    shapes = [jax.ShapeDtypeStruct(w.shape, F32) for w in ws]
    out = pl.pallas_call(
        body, name="adamw_small", in_specs=[vm] * (4 * n), out_specs=[vm] * (3 * n), out_shape=shapes * 3,
    )(*grads, *ws, *ms, *vs)
    return out[:n], out[n:2 * n], out[2 * n:]


def _unstack_heads(w_st):
    per = HEAD_DIM // N_CHIPS
    return w_st.reshape(N_CHIPS, HEADS, per, HEAD_DIM).transpose(1, 0, 2, 3).reshape(HEADS, HEAD_DIM, HEAD_DIM)


def _stack_heads(w):
    per = HEAD_DIM // N_CHIPS
    return w.reshape(HEADS, N_CHIPS, per, HEAD_DIM).transpose(1, 0, 2, 3).reshape(N_CHIPS, HEADS * per, HEAD_DIM)


def kernel(x, norm_mix_g, w_in, conv_w, conv_b, w_rgate, b_rgate, w_igate, b_igate, lru_lambda, w_out_a, sgu_ln_g, sgu_ln_b, sgu_w_s, sgu_b_s, w_out_b, w_out, norm_mlp_g, w_up, w_down, norm_final_g, loss_target, m_norm_mix_g, m_w_in, m_conv_w, m_conv_b, m_w_rgate, m_b_rgate, m_w_igate, m_b_igate, m_lru_lambda, m_w_out_a, m_sgu_ln_g, m_sgu_ln_b, m_sgu_w_s, m_sgu_b_s, m_w_out_b, m_w_out, m_norm_mlp_g, m_w_up, m_w_down, m_norm_final_g, v_norm_mix_g, v_w_in, v_conv_w, v_conv_b, v_w_rgate, v_b_rgate, v_w_igate, v_b_igate, v_lru_lambda, v_w_out_a, v_sgu_ln_g, v_sgu_ln_b, v_sgu_w_s, v_sgu_b_s, v_w_out_b, v_w_out, v_norm_mlp_g, v_w_up, v_w_down, v_norm_final_g):
    chip = _chip_index(lax.axis_index("x"), lax.axis_index("y"))
    core = lax.axis_index("c")
    quarter_h = HEAD_DIM // N_CHIPS
    quarter_d = D_MODEL // N_CHIPS

    as_2d = lambda a: a.reshape(-1, a.shape[-1])
    big_w = [as_2d(w) for w in (w_in, w_rgate, w_igate, w_out_a, w_out_b, w_out, w_up, w_down)]
    big_m = [as_2d(w) for w in (m_w_in, m_w_rgate, m_w_igate, m_w_out_a, m_w_out_b, m_w_out, m_w_up, m_w_down)]
    big_v = [as_2d(w) for w in (v_w_in, v_w_rgate, v_w_igate, v_w_out_a, v_w_out_b, v_w_out, v_w_up, v_w_down)]

    packed = jnp.concatenate([conv_w[0], b_rgate[0], b_igate[0]], axis=1)
    packed = jnp.concatenate([packed, jnp.zeros_like(packed)], axis=0)
    s_in, s_r, s_i, s_oa, s_ob, s_out, s_up, s_down = [w.astype(BF16) for w in big_w]
    xs, target = x[0], loss_target[0]
    g3 = norm_final_g.reshape(1, D_MODEL)
    bias_s = jnp.broadcast_to(jnp.transpose(sgu_b_s[0])[:, :, None], (CHUNK, GROUPS, GROUP_DIM)).reshape(CHUNK, D_MODEL)
    core_arr = core.reshape(1).astype(jnp.int32)
    place = jnp.stack([chip, core]).astype(jnp.int32)
    quarter = lambda g: g.reshape(N_CHIPS, D_MODEL // N_CHIPS, D_MODEL)

    def pair_add(nm, g, from_sibling):
        return _pair_add("pair_add_" + nm, core_arr, g.reshape(N_CHIPS, 2, g.shape[1] // 2, g.shape[2]), from_sibling)

    def chip_sum(nm, pair, from_chips):
        return _chip_sum("chip_sum_" + nm, place, pair, from_chips)

    order = jnp.stack([chip, chip ^ 2, chip ^ 1, chip ^ 3]).astype(jnp.int32)
    (z, n1, (w_in_st, wr_st, wi_st)), ((packed_all,), late) = _fwd_in(
        xs, norm_mix_g, [s_in, s_r, s_i], order,
        jobs=[_gather_small_job(packed), _gather_near_job([s_oa, s_ob, s_out])])
    pick = lambda lo, hi: packed_all[:, :HEADS, lo:hi].transpose(1, 0, 2).reshape(HEADS, -1)
    conv_w_full = pick(0, quarter_d)
    br_full = pick(quarter_d, quarter_d + quarter_h).reshape(1, D_MODEL)
    bi_full = pick(quarter_d + quarter_h, quarter_d + 2 * quarter_h).reshape(1, D_MODEL)
    wr, wi = _unstack_heads(wr_st), _unstack_heads(wi_st)
    lru = (conv_w_full, conv_b, wr, br_full, wi, bi_full, lru_lambda)
    sgu = (sgu_ln_g, sgu_ln_b, sgu_w_s[0], bias_s)

    (ya, *saved), (late, mlp_w) = _fwd_lru(
        z, *lru, jobs=[_gather_far_job(late), _gather_near_job([s_up, s_down])])
    yb, (late, mlp_w) = _fwd_sgu(z, *sgu, jobs=[_gather_pass_job(late), _gather_far_job(mlp_w)])
    w_oa, w_ob, w_o = [w.reshape(D_MODEL, D_MODEL) for w in late]
    (pa, pb, h1, n2), ((w_up_st, w_dn),) = _fwd_merge(ya, yb, z, xs, w_oa, w_ob, w_o, norm_mlp_g,
                                                      jobs=[_gather_pass_job(mlp_w)])
    w_dn = w_dn.reshape(D_FF, D_MODEL)
    (act, dup, dh2b, dh1, loss_part, dg3, dg2), _ = _mlp(n2, h1, target, w_up_st, w_dn, norm_mlp_g, g3)

    d_up, _ = _weight_grad("dw_up", n2, dup, N_CHIPS, False, True, D_MODEL)
    d_down, ((r_up,),) = _weight_grad("dw_down", act, dh2b, N_CHIPS, True, False, D_MODEL,
                                      jobs=[_pair_send_job([d_up])])
    (r_down,), = _comm_call("send_w_down", [_pair_send_job([d_down])])
    p_up, p_down = pair_add("w_up", d_up, r_up), pair_add("w_down", d_down, r_down)
    (dz, merged, dpa, dpb, dh1b, dlg, dlb, dws, dbs, dcw, dcb, dwr, dbr, dwi, dbi, dlam), ((q_up, q_down),) = _bwd_mix(
        dh1, pa, pb, z, *saved, w_oa, w_ob, w_o, *sgu, conv_w_full, wr, wi, lru_lambda,
        jobs=[_chip_exchange_job([p_up, p_down])])
    half_up, half_down = chip_sum("w_up", p_up, q_up), chip_sum("w_down", p_down, q_down)
    gates = [_stack_heads(dwr).astype(BF16), _stack_heads(dwi).astype(BF16)]
    small = _pack_small(dcw, dcb, dbr, dbi, dlam, dlg, dlb, dg2, dg3, loss_part, dbs)
    d_in, ((full_up, full_down), r_gates, (vec_all, ws_all)) = _weight_grad(
        "dw_in", n1, dz, N_CHIPS, False, True, IN_SHARD,
        jobs=[_share_job([half_up, half_down]), _pair_send_job(gates), _gather_all_job([small, dws])])
    d_out, ((r_in,),) = _weight_grad("dw_out", merged, dh1b, 1, False, False, D_MODEL, jobs=[_pair_send_job([d_in])])
    d_oa, _ = _weight_grad("dw_out_a", ya, dpa, 1, False, False, D_MODEL)
    d_ob, _ = _weight_grad("dw_out_b", yb, dpb, 1, False, False, D_MODEL)
    mids = [quarter(d_oa), quarter(d_ob), quarter(d_out)]
    names = ("w_in", "w_rgate", "w_igate", "w_out_a", "w_out_b", "w_out", "w_up", "w_down")
    adam_args = {nm: (w, m, v) for nm, w, m, v in zip(names, big_w, big_m, big_v)}

    def adamw(nm, g):
        w, m, v = adam_args[nm]
        g = g.reshape(w.shape)
        return g, _adamw("adamw_" + nm, g, w, m, v)[0]

    p_first = [pair_add(nm, g, r) for nm, g, r in zip(names[:3], [d_in] + gates, [r_in] + r_gates)]
    (grad_x, dg1), (q_first, r_mids) = _bwd_in(
        dz, xs, dh1, w_in_st, norm_mix_g, jobs=[_chip_exchange_job(p_first), _pair_send_job(mids)])
    half_first = [chip_sum(nm, p, q) for nm, p, q in zip(names[:3], p_first, q_first)]
    p_mids = [pair_add(nm, g, r) for nm, g, r in zip(names[3:6], mids, r_mids)]
    q_mids, = _comm_call("exchange_mids", [_chip_exchange_job(p_mids)])
    half_mids = [chip_sum(nm, p, q) for nm, p, q in zip(names[3:6], p_mids, q_mids)]
    full_last, (dg1_all,) = _comm_call("share_last", [_share_job(half_first + half_mids), _gather_all_job([dg1])])
    full, big_out = [], []
    for nm, f in zip(names, full_last + [full_up, full_down]):
        g, out = adamw(nm, f)
        full.append(g)
        big_out.append(out)

    vec, ws_sum = _sum_small(vec_all, ws_all, dg1_all)
    row = lambda r: vec[r:r + 1]
    shard = lambda a, width: lax.dynamic_slice_in_dim(a, chip * width, width, axis=1)
    g_small = dict(
        norm_mix_g=row(ROW_G1), conv_w=shard(vec[ROW_CW:ROW_CW + CONV_WIDTH], quarter_d), conv_b=row(ROW_CB),
        b_rgate=shard(row(ROW_BR).reshape(HEADS, HEAD_DIM), quarter_h),
        b_igate=shard(row(ROW_BI).reshape(HEADS, HEAD_DIM), quarter_h), lru_lambda=row(ROW_LAM),
        sgu_ln_g=row(ROW_LG), sgu_ln_b=row(ROW_LB),
        sgu_w_s=ws_sum.reshape(CHUNK, GROUPS, CHUNK).transpose(1, 0, 2).reshape(GROUPS * CHUNK, CHUNK),
        sgu_b_s=vec[ROW_BS:ROW_BS + GROUPS, 0:CHUNK], norm_mlp_g=row(ROW_G2), norm_final_g=row(ROW_G3))
    loss = vec[ROW_LOSS, 0]
    small_names = list(g_small)
    given = dict(
        norm_mix_g=(norm_mix_g, m_norm_mix_g, v_norm_mix_g), conv_w=(conv_w, m_conv_w, v_conv_w),
        conv_b=(conv_b, m_conv_b, v_conv_b), b_rgate=(b_rgate, m_b_rgate, v_b_rgate),
        b_igate=(b_igate, m_b_igate, v_b_igate), lru_lambda=(lru_lambda, m_lru_lambda, v_lru_lambda),
        sgu_ln_g=(sgu_ln_g, m_sgu_ln_g, v_sgu_ln_g), sgu_ln_b=(sgu_ln_b, m_sgu_ln_b, v_sgu_ln_b),
        sgu_w_s=(sgu_w_s, m_sgu_w_s, v_sgu_w_s), sgu_b_s=(sgu_b_s, m_sgu_b_s, v_sgu_b_s),
        norm_mlp_g=(norm_mlp_g, m_norm_mlp_g, v_norm_mlp_g), norm_final_g=(norm_final_g, m_norm_final_g, v_norm_final_g))
    g2d = [g_small[nm] for nm in small_names]
    to2d = lambda a, g: a.reshape(g.shape)
    d_s, m_s, v_s = _adamw_small(
        g2d, *[[to2d(given[nm][q], g) for nm, g in zip(small_names, g2d)] for q in range(3)])

    shapes = dict(
        norm_mix_g=norm_mix_g, w_in=w_in, conv_w=conv_w, conv_b=conv_b, w_rgate=w_rgate, b_rgate=b_rgate,
        w_igate=w_igate, b_igate=b_igate, lru_lambda=lru_lambda, w_out_a=w_out_a, sgu_ln_g=sgu_ln_g,
        sgu_ln_b=sgu_ln_b, sgu_w_s=sgu_w_s, sgu_b_s=sgu_b_s, w_out_b=w_out_b, w_out=w_out, norm_mlp_g=norm_mlp_g,
        w_up=w_up, w_down=w_down, norm_final_g=norm_final_g)
    grads, deltas, new_m, new_v = {}, {}, {}, {}
    for nm, g, (d, nmom, nvar) in zip(names, full, big_out):
        grads[nm], deltas[nm], new_m[nm], new_v[nm] = g, d, nmom, nvar
    for p, nm in enumerate(small_names):
        grads[nm], deltas[nm], new_m[nm], new_v[nm] = g2d[p], d_s[p], m_s[p], v_s[p]
    order = list(shapes)
    out = [loss, grad_x[None]]
    for group in (grads, deltas, new_m, new_v):
        out += [group[nm].reshape(shapes[nm].shape) for nm in order]
    return tuple(out)
```

```python
import functools

import jax
import jax.numpy as jnp
from jax import lax
from jax.experimental import pallas as pl
from jax.experimental.pallas import tpu as pltpu

F32 = jnp.float32
BF16 = jnp.bfloat16
MESH = pl.DeviceIdType.MESH

D_MODEL = 1024
D_IN = 6 * D_MODEL
D_FF = 4 * D_MODEL
N_CHIPS = 4
IN_SHARD = D_IN // N_CHIPS
HEADS = 4
HEAD_DIM = D_MODEL // HEADS
GROUPS = 4
GROUP_DIM = D_MODEL // GROUPS
CHUNK = 128
CONV_WIDTH = 4
LRU_C = 8.0
NORM_EPS = 1e-6
LN_EPS = 1e-5

ADAM_LR = 0.001
ADAM_B1 = 0.9
ADAM_B2 = 0.999
ADAM_EPS = 1e-08
ADAM_WD = 0.01
ADAM_STEP = 10

SUBLANES = 8
MM_TILE = 512
IN_TILE = 1024
SEQ_TILE = 256
DW_TILE = 2048
VMEM_LIMIT_BYTES = 56 * 1024 * 1024

GELU_K0 = 0.7978845608028654
GELU_K1 = 0.044715


def _params(n_grid_axes=1):
    return pltpu.CompilerParams(
        dimension_semantics=("arbitrary",) * n_grid_axes, vmem_limit_bytes=VMEM_LIMIT_BYTES)


def _resident(shape):
    nd = len(shape)
    return pl.BlockSpec(shape, lambda *_: (0,) * nd, pipeline_mode=pl.Buffered(1))


def _const(shape):
    nd = len(shape)
    return pl.BlockSpec(shape, lambda *_: (0,) * nd)


def _dot(a, b):
    return jnp.dot(a, b, preferred_element_type=F32)


def _dot_nt(a, b):
    return lax.dot_general(a, b, (((1,), (1,)), ((), ())), preferred_element_type=F32)


def _dot_tn(a, b):
    return lax.dot_general(a, b, (((0,), (0,)), ((), ())), preferred_element_type=F32)


def _gelu(x):
    t = jnp.tanh(GELU_K0 * x * (1.0 + GELU_K1 * x * x))
    return 0.5 * x * (1.0 + t)


def _gelu_and_grad(x):
    x2 = x * x
    t = jnp.tanh(GELU_K0 * x * (1.0 + GELU_K1 * x2))
    g = 0.5 * x * (1.0 + t)
    dg = 0.5 * (1.0 + t) + 0.5 * x * (1.0 - t * t) * (GELU_K0 * (1.0 + 3.0 * GELU_K1 * x2))
    return g, dg


def _rms(x):
    r = lax.rsqrt(jnp.mean(x * x, axis=-1, keepdims=True) + NORM_EPS)
    return x * r, r


def _rms_bwd(dn, xhat, r):
    return r * (dn - xhat * jnp.mean(dn * xhat, axis=-1, keepdims=True))


def _col_sum(v):
    return jnp.sum(v, axis=0, keepdims=True)


def _shift_down(x, tail8, k):
    xs = pltpu.roll(x, k, 0)
    ts = pltpu.roll(tail8, k, 0)
    ridx = lax.broadcasted_iota(jnp.int32, tail8.shape, 0)
    head = jnp.where(ridx < k, ts, xs[0:SUBLANES])
    return jnp.concatenate([head, xs[SUBLANES:]], axis=0)


def _shift_up(x, head8, k):
    n = x.shape[0]
    xs = pltpu.roll(x, n - k, 0)
    hs = pltpu.roll(head8, SUBLANES - k, 0)
    ridx = lax.broadcasted_iota(jnp.int32, head8.shape, 0)
    last = jnp.where(ridx >= SUBLANES - k, hs, xs[n - SUBLANES:n])
    return jnp.concatenate([xs[:n - SUBLANES], last], axis=0)


def _scan_forward(a, b, carry):
    n, cols = a.shape
    groups = n // SUBLANES
    a = a.reshape(groups, SUBLANES, cols)
    b = b.reshape(groups, SUBLANES, cols)
    sub = lax.broadcasted_iota(jnp.int32, a.shape, 1)
    for s in (1, 2, 4):
        a_s = pltpu.roll(a, s, 1)
        b_s = pltpu.roll(b, s, 1)
        m = sub >= s
        b = jnp.where(m, a * b_s + b, b)
        a = jnp.where(m, a * a_s, a)
    out = []
    for g in range(groups):
        h = a[g] * carry + b[g]
        out.append(h)
        carry = h[SUBLANES - 1:SUBLANES]
    return jnp.concatenate(out, axis=0), carry


def _scan_backward(a, b, carry):
    n, cols = a.shape
    groups = n // SUBLANES
    a = a.reshape(groups, SUBLANES, cols)
    b = b.reshape(groups, SUBLANES, cols)
    sub = lax.broadcasted_iota(jnp.int32, a.shape, 1)
    for s in (1, 2, 4):
        a_s = pltpu.roll(a, SUBLANES - s, 1)
        b_s = pltpu.roll(b, SUBLANES - s, 1)
        m = sub < SUBLANES - s
        b = jnp.where(m, a * b_s + b, b)
        a = jnp.where(m, a * a_s, a)
    out = [None] * groups
    for g in reversed(range(groups)):
        h = a[g] * carry + b[g]
        out[g] = h
        carry = h[0:1]
    return jnp.concatenate(out, axis=0), carry


def _softplus_neg(lam):
    e = jnp.exp(-jnp.abs(lam))
    u = 1.0 + e
    log1p_e = jnp.where(u == 1.0, e, jnp.log(u) * (e / jnp.where(u == 1.0, 1.0, u - 1.0)))
    return jnp.maximum(-lam, 0.0) + log1p_e


def _lru_gates(xa, tail8, cw_ref, cb_ref, wr_ref, br_ref, wi_ref, bi_ref, lam_ref):
    cw = cw_ref[...]
    xc = cb_ref[...] + cw[0:1] * xa
    for k in range(1, CONV_WIDTH):
        xc = xc + cw[k:k + 1] * _shift_down(xa, tail8, k)
    xcb = xc.astype(BF16)
    pre_r, pre_i = [], []
    for h in range(HEADS):
        cols = slice(h * HEAD_DIM, (h + 1) * HEAD_DIM)
        pre_r.append(_dot(xcb[:, cols], wr_ref[h]))
        pre_i.append(_dot(xcb[:, cols], wi_ref[h]))
    r = jax.nn.sigmoid(jnp.concatenate(pre_r, axis=1) + br_ref[...])
    ig = jax.nn.sigmoid(jnp.concatenate(pre_i, axis=1) + bi_ref[...])
    _, a, mult = _decay(r, lam_ref)
    return xc, r, ig, a, mult


def _decay(r, lam_ref):
    sp = _softplus_neg(lam_ref[...])
    log_a = ((-LRU_C) * sp) * r
    a = jnp.exp(log_a)
    th = jnp.tanh(log_a)
    return sp, a, jnp.sqrt((-2.0 * th) / (1.0 - th))


class _Job:
    def __init__(self, inputs, out_shape, n_sem, copies, aliases=None, n_local=0):
        self.inputs, self.out_shape, self.n_sem, self.copies = list(inputs), list(out_shape), n_sem, copies
        self.aliases, self.n_local = dict(aliases or {}), n_local


def _fused_call(body, jobs, *, name, grid, in_specs, out_specs, out_shape, scratch_shapes=(),
                input_output_aliases=None, compiler_params=None, n_prefetch=0, jobs_start_after=None):
    single = not isinstance(out_shape, (list, tuple))
    out_specs = [out_specs] if single else list(out_specs)
    out_shape = [out_shape] if single else list(out_shape)
    n_scr = len(scratch_shapes)
    in_specs, scratch_shapes = list(in_specs), list(scratch_shapes)
    n_in, n_out = len(in_specs), len(out_shape)
    aliases = dict(input_output_aliases or {})
    in_at, out_at = [], []
    for job in jobs:
        in_at.append(len(in_specs))
        out_at.append(len(out_shape))
        for i, o in job.aliases.items():
            aliases[n_prefetch + len(in_specs) + i] = len(out_shape) + o
        in_specs += [ANY] * len(job.inputs)
        out_specs += [ANY] * len(job.out_shape)
        out_shape += job.out_shape
        scratch_shapes += [pltpu.SemaphoreType.DMA((job.n_sem,)), pltpu.SemaphoreType.DMA((job.n_sem,)),
                           pltpu.SemaphoreType.DMA((max(job.n_local, 1),))]
    n_in_all, n_out_all = len(in_specs), len(out_shape)

    def full_body(*refs):
        prefetch, refs = refs[:n_prefetch], refs[n_prefetch:]
        ins, outs, scr = refs[:n_in_all], refs[n_in_all:n_in_all + n_out_all], refs[n_in_all + n_out_all:]

        def copies(q):
            job = jobs[q]
            return job.copies(ins[in_at[q]:in_at[q] + len(job.inputs)], outs[out_at[q]:out_at[q] + len(job.out_shape)],
                              *scr[n_scr + 3 * q:n_scr + 3 * q + 3])

        def start():
            for q in range(len(jobs)):
                sends, _, local = copies(q)
                for cp in local + sends:
                    cp.start()

        def finish():
            every = [copies(q) for q in range(len(jobs))]
            for _, arrivals, _ in every:
                for cp in arrivals:
                    cp.wait_recv()
            for sends, _, local in every:
                for cp in sends:
                    cp.wait_send()
                for cp in local:
                    cp.wait()

        if not grid:
            start()
            finish()
            return
        ids = [pl.program_id(a) for a in range(len(grid))]
        at_step = lambda step: functools.reduce(jnp.logical_and, [i == k for i, k in zip(ids, step)])
        if jobs and jobs_start_after is None:
            pl.when(at_step((0,) * len(grid)))(start)
        body(*prefetch, *ins[:n_in], *outs[:n_out], *scr[:n_scr])
        if jobs and jobs_start_after is not None:
            pl.when(at_step(jobs_start_after))(start)
        if jobs:
            pl.when(functools.reduce(jnp.logical_and, [i == g - 1 for i, g in zip(ids, grid)]))(finish)

    if n_prefetch:
        layout = dict(grid_spec=pltpu.PrefetchScalarGridSpec(
            num_scalar_prefetch=n_prefetch, grid=grid, in_specs=in_specs, out_specs=out_specs,
            scratch_shapes=scratch_shapes))
    else:
        layout = dict(grid=grid, in_specs=in_specs, out_specs=out_specs, scratch_shapes=scratch_shapes)
    call = pl.pallas_call(
        full_body, name=name, out_shape=out_shape, input_output_aliases=aliases, compiler_params=compiler_params,
        **layout)

    def run(*args):
        res = call(*args, *[a for job in jobs for a in job.inputs])
        mine = res[0] if single else list(res[:n_out])
        return mine, [list(res[at:at + len(job.out_shape)]) for at, job in zip(out_at, jobs)]

    return run


def _fwd_in(x, g1, shards, order, jobs=()):
    t = x.shape[0]
    rows_per_step = min(IN_TILE, t)
    n_tiles = t // rows_per_step
    n = len(shards)
    halves = [s.shape[0] // 2 for s in shards]

    def body(order_ref, x_ref, g_ref, *refs):
        del order_ref
        ins, (z_ref, n_ref), outs = refs[:n], refs[n:n + 2], refs[n + 2:2 * n + 2]
        wbuf, nbuf, send, recv, local = refs[2 * n + 2:]
        s, i = pl.program_id(0), pl.program_id(1)
        x_, y_, c, chips = _place()
        k_me = _chip_index(x_, y_)

        def block(w, chip, pc):
            return outs[w].at[_chip_index(*chip), pl.ds(pc * halves[w], halves[w]), :]

        def over_ici(w, j, landing):
            return pltpu.make_async_remote_copy(
                src_ref=ins[w].at[pl.ds(c * halves[w], halves[w]), :],
                dst_ref=block(w, chips[j] if landing else (x_, y_), c), send_sem=send.at[6 * w + j],
                recv_sem=recv.at[6 * w + j], device_id=(*chips[j], c), device_id_type=MESH)

        def to_sibling(w, j, landing):
            blk = block(w, chips[j], 1 - c if landing else c)
            return pltpu.make_async_remote_copy(
                src_ref=blk, dst_ref=blk, send_sem=send.at[6 * w + 3 + j], recv_sem=recv.at[6 * w + 3 + j],
                device_id=(x_, y_, 1 - c), device_id_type=MESH)

        own = [pltpu.make_async_copy(wbuf, outs[0].at[k_me], local.at[0])]
        own += [pltpu.make_async_copy(ins[w], outs[w].at[k_me], local.at[w]) for w in range(1, n)]

        @pl.when((s == 0) & (i == 0))
        def _():
            for j in range(2):
                for w in range(n):
                    over_ici(w, j, False).start()
            load = pltpu.make_async_copy(ins[0], wbuf, local.at[n])
            load.start()
            load.wait()
            for cp in own:
                cp.start()

        for j in range(N_CHIPS - 1):
            @pl.when((s == j + 1) & (i == 0))
            def _(j=j):
                for w in range(n):
                    over_ici(w, j, True).wait_recv()
                for w in range(n):
                    to_sibling(w, j, False).start()
                if j == 0:
                    for w in range(n):
                        over_ici(w, 2, False).start()
                    own[0].wait()
                for w in range(n):
                    to_sibling(w, j, True).wait_recv()
                load = pltpu.make_async_copy(outs[0].at[_chip_index(*chips[j])], wbuf, local.at[n])
                load.start()
                load.wait()

        rows = pl.ds(pl.multiple_of(i * rows_per_step, rows_per_step), rows_per_step)

        @pl.when(s == 0)
        def _():
            xhat, _ = _rms(x_ref[...])
            nrm = (xhat * g_ref[...]).astype(BF16)
            nbuf[rows, :] = nrm
            n_ref[...] = nrm

        z_ref[...] = _dot(nbuf[rows, :], wbuf[...])

        @pl.when((s == N_CHIPS - 1) & (i == n_tiles - 1))
        def _():
            for j in range(N_CHIPS - 1):
                for w in range(n):
                    over_ici(w, j, False).wait_send()
                    to_sibling(w, j, False).wait_send()
            for cp in own[1:]:
                cp.wait()

    once = lambda s, i, order: (jnp.where(s == 0, i, n_tiles - 1), 0)
    (z, n1, *stacked), job_outs = _fused_call(
        body, jobs, name="fwd_in", grid=(N_CHIPS, n_tiles), n_prefetch=1,
        in_specs=[pl.BlockSpec((rows_per_step, D_MODEL), once), _const((1, D_MODEL))] + [ANY] * n,
        out_specs=[pl.BlockSpec((rows_per_step, IN_SHARD), lambda s, i, order: (i, order[s])),
                   pl.BlockSpec((rows_per_step, D_MODEL), once)] + [ANY] * n,
        out_shape=[jax.ShapeDtypeStruct((t, D_IN), F32), jax.ShapeDtypeStruct((t, D_MODEL), BF16)]
        + [jax.ShapeDtypeStruct((N_CHIPS,) + s.shape, s.dtype) for s in shards],
        scratch_shapes=[pltpu.VMEM(shards[0].shape, BF16), pltpu.VMEM((t, D_MODEL), BF16),
                        pltpu.SemaphoreType.DMA((6 * n,)),
                        pltpu.SemaphoreType.DMA((6 * n,)), pltpu.SemaphoreType.DMA((n + 1,))],
        compiler_params=_params(2), jobs_start_after=(1, 0),
    )(order, x, g1, *shards)
    return (z, n1, stacked), job_outs


def _fwd_lru(z, conv_w, conv_b, wr, br, wi, bi, lam, jobs=()):
    t = z.shape[0]

    def body(xa_ref, ga_ref, cw_ref, cb_ref, wr_ref, br_ref, wi_ref, bi_ref, lam_ref, ya_ref, h_ref, xc_ref, r_ref,
             ig_ref, tail_ref, carry_ref):
        @pl.when(pl.program_id(0) == 0)
        def _():
            tail_ref[...] = jnp.zeros_like(tail_ref)
            carry_ref[...] = jnp.zeros_like(carry_ref)

        xa = xa_ref[...]
        xc, r, ig, a, mult = _lru_gates(xa, tail_ref[...], cw_ref, cb_ref, wr_ref, br_ref, wi_ref, bi_ref, lam_ref)
        tail_ref[...] = xa[SEQ_TILE - SUBLANES:]
        xc_ref[...], r_ref[...], ig_ref[...] = xc, r, ig
        h, carry = _scan_forward(a, xc * ig * mult, carry_ref[...])
        carry_ref[...] = carry
        h_ref[...] = h
        ya_ref[...] = (h * _gelu(ga_ref[...])).astype(BF16)

    tile = lambda j: pl.BlockSpec((SEQ_TILE, D_MODEL), lambda i: (i, j))
    return _fused_call(
        body, jobs, name="fwd_lru", grid=(t // SEQ_TILE,),
        in_specs=[tile(0), tile(1), _const((CONV_WIDTH, D_MODEL)), _const((1, D_MODEL)),
                  _resident((HEADS, HEAD_DIM, HEAD_DIM)), _const((1, D_MODEL)),
                  _resident((HEADS, HEAD_DIM, HEAD_DIM)), _const((1, D_MODEL)), _const((1, D_MODEL))],
        out_specs=[tile(0)] * 5,
        out_shape=[jax.ShapeDtypeStruct((t, D_MODEL), BF16)] + [jax.ShapeDtypeStruct((t, D_MODEL), F32)] * 4,
        scratch_shapes=[pltpu.VMEM((SUBLANES, D_MODEL), F32), pltpu.VMEM((1, D_MODEL), F32)],
        compiler_params=_params(),
    )(z, z, conv_w, conv_b, wr, br, wi, bi, lam)


def _sgu_forward_parts(ub, vb, lg_ref, lb_ref):
    u, du = _gelu_and_grad(ub)
    vg, dvg = _gelu_and_grad(vb)
    mu = jnp.mean(vg, axis=-1, keepdims=True)
    d = vg - mu
    rstd = lax.rsqrt(jnp.mean(d * d, axis=-1, keepdims=True) + LN_EPS)
    vhat = d * rstd
    vn = (vhat * lg_ref[...] + lb_ref[...]).astype(BF16)
    return u, du, dvg, rstd, vhat, vn


def _causal_mask():
    rows = lax.broadcasted_iota(jnp.int32, (CHUNK, CHUNK), 0)
    cols = lax.broadcasted_iota(jnp.int32, (CHUNK, CHUNK), 1)
    return rows >= cols


def _fwd_sgu(z, ln_g, ln_b, w_s, bias_full, jobs=()):
    t = z.shape[0]

    def body(ub_ref, vb_ref, lg_ref, lb_ref, ws_ref, bias_ref, yb_ref):
        u, _, _, _, _, vn = _sgu_forward_parts(ub_ref[...], vb_ref[...], lg_ref, lb_ref)
        mask = _causal_mask()
        wm = [jnp.where(mask, ws_ref[g], 0.0).astype(BF16) for g in range(GROUPS)]
        for c in range(SEQ_TILE // CHUNK):
            rows = slice(c * CHUNK, (c + 1) * CHUNK)
            for g in range(GROUPS):
                cols = slice(g * GROUP_DIM, (g + 1) * GROUP_DIM)
                sp = _dot(wm[g], vn[rows, cols]) + bias_ref[:, cols]
                yb_ref[rows, cols] = (u[rows, cols] * sp).astype(BF16)

    tile = lambda j: pl.BlockSpec((SEQ_TILE, D_MODEL), lambda i: (i, j))
    return _fused_call(
        body, jobs, name="fwd_sgu", grid=(t // SEQ_TILE,),
        in_specs=[tile(2), tile(3), _const((1, D_MODEL)), _const((1, D_MODEL)),
                  _const((GROUPS, CHUNK, CHUNK)), _const((CHUNK, D_MODEL))],
        out_specs=tile(0),
        out_shape=jax.ShapeDtypeStruct((t, D_MODEL), BF16),
        compiler_params=_params(),
    )(z, z, ln_g, ln_b, w_s, bias_full)


def _fwd_merge(ya, yb, z, x, w_oa, w_ob, w_out, g2, jobs=()):
    t = x.shape[0]

    def body(ya_ref, yb_ref, m_ref, x_ref, woa_ref, wob_ref, wout_ref, g_ref, pa_ref, pb_ref, h1_ref, n2_ref):
        pa = _dot(ya_ref[...], woa_ref[...])
        pb = _dot(yb_ref[...], wob_ref[...])
        pa_ref[...] = pa
        pb_ref[...] = pb
        merged = jax.nn.sigmoid(m_ref[:, :D_MODEL]) * pa + jax.nn.sigmoid(m_ref[:, D_MODEL:]) * pb
        h1 = x_ref[...] + _dot(merged.astype(BF16), wout_ref[...])
        h1_ref[...] = h1
        xhat, _ = _rms(h1)
        n2_ref[...] = (xhat * g_ref[...]).astype(BF16)

    tile = pl.BlockSpec((MM_TILE, D_MODEL), lambda i: (i, 0))
    sq = _resident((D_MODEL, D_MODEL))
    return _fused_call(
        body, jobs, name="fwd_merge", grid=(t // MM_TILE,),
        in_specs=[tile, tile, pl.BlockSpec((MM_TILE, 2 * D_MODEL), lambda i: (i, 2)), tile, sq, sq, sq,
                  _const((1, D_MODEL))],
        out_specs=[tile, tile, tile, tile],
        out_shape=[jax.ShapeDtypeStruct((t, D_MODEL), F32)] * 3 + [jax.ShapeDtypeStruct((t, D_MODEL), BF16)],
        compiler_params=_params(),
    )(ya, yb, z, x, w_oa, w_ob, w_out, g2)


def _mlp(n2, h1, target, w_up_st, w_down, g2, g3, jobs=()):
    t = n2.shape[0]

    def body(n2_ref, h1_ref, tgt_ref, wup_ref, wdown_ref, g2_ref, g3_ref, act_ref, dup_ref, dh2b_ref, dh1_ref,
             loss_ref, dg3_ref, dg2_ref, relu_ref):
        @pl.when(pl.program_id(0) == 0)
        def _():
            for ref in (loss_ref, dg3_ref, dg2_ref):
                ref[...] = jnp.zeros_like(ref)

        n2 = n2_ref[...]
        h1 = h1_ref[...]
        h2 = h1
        for k in range(N_CHIPS):
            cols = slice(k * D_MODEL, (k + 1) * D_MODEL)
            r = jnp.maximum(_dot(n2, wup_ref[k]), 0.0)
            relu_ref[:, cols] = r
            act = (r * r).astype(BF16)
            act_ref[:, cols] = act
            h2 = h2 + _dot(act, wdown_ref[cols, :])
        xhat, r3 = _rms(h2)
        diff = xhat * g3_ref[...] - tgt_ref[...]
        sq = jnp.sum(diff * diff, axis=1, keepdims=True)
        loss_ref[...] = loss_ref[...] + (0.5 / D_MODEL) * jnp.sum(sq, axis=0, keepdims=True)
        dy = diff * (1.0 / D_MODEL)
        dg3_ref[...] = dg3_ref[...] + _col_sum(dy * xhat)
        dh2 = _rms_bwd(dy * g3_ref[...], xhat, r3)
        dh2b = dh2.astype(BF16)
        dh2b_ref[...] = dh2b
        dn2 = jnp.zeros((SEQ_TILE, D_MODEL), F32)
        for k in range(N_CHIPS):
            cols = slice(k * D_MODEL, (k + 1) * D_MODEL)
            dup = (_dot_nt(dh2b, wdown_ref[cols, :]) * (2.0 * relu_ref[:, cols])).astype(BF16)
            dup_ref[:, cols] = dup
            dn2 = dn2 + _dot_nt(dup, wup_ref[k])
        xhat, r2 = _rms(h1)
        dg2_ref[...] = dg2_ref[...] + _col_sum(dn2 * xhat)
        dh1_ref[...] = dh2 + _rms_bwd(dn2 * g2_ref[...], xhat, r2)

    tile = pl.BlockSpec((SEQ_TILE, D_MODEL), lambda i: (i, 0))
    wide = pl.BlockSpec((SEQ_TILE, D_FF), lambda i: (i, 0))
    vec = _const((1, D_MODEL))
    vec_shape = jax.ShapeDtypeStruct((1, D_MODEL), F32)
    return _fused_call(
        body, jobs, name="mlp", grid=(t // SEQ_TILE,),
        in_specs=[tile, tile, tile, _resident((N_CHIPS, D_MODEL, D_MODEL)), _resident((D_FF, D_MODEL)), vec, vec],
        out_specs=[wide, wide, tile, tile, _const((SUBLANES, 128)), vec, vec],
        out_shape=[jax.ShapeDtypeStruct((t, D_FF), BF16), jax.ShapeDtypeStruct((t, D_FF), BF16),
                   jax.ShapeDtypeStruct((t, D_MODEL), BF16), jax.ShapeDtypeStruct((t, D_MODEL), F32),
                   jax.ShapeDtypeStruct((SUBLANES, 128), F32), vec_shape, vec_shape],
        scratch_shapes=[pltpu.VMEM((SEQ_TILE, D_FF), F32)],
        compiler_params=_params(),
    )(n2, h1, target, w_up_st, w_down, g2, g3)


def _bwd_mix(dh1, pa, pb, z, h, xc, r, ig, w_oa, w_ob, w_out, ln_g, ln_b, w_s, bias_full, conv_w, wr, wi, lam, jobs=()):
    t = dh1.shape[0]
    n_tiles = t // SEQ_TILE
    per_tile = SEQ_TILE // SUBLANES

    def merge_part(dh1_ref, pa_ref, pb_ref, m_ref, woa_ref, wob_ref, wout_ref, dz_ref, dya_ref, dyb_ref, mg_ref,
                   dpa_ref, dpb_ref, dh1b_ref):
        dh1b = dh1_ref[...].astype(BF16)
        dh1b_ref[...] = dh1b
        dm = _dot_nt(dh1b, wout_ref[...])
        pa = pa_ref[...]
        pb = pb_ref[...]
        sa = jax.nn.sigmoid(m_ref[:, :D_MODEL])
        sb = jax.nn.sigmoid(m_ref[:, D_MODEL:])
        mg_ref[...] = (sa * pa + sb * pb).astype(BF16)
        dz_ref[:, :D_MODEL] = (dm * pa * sa * (1.0 - sa)).astype(BF16)
        dz_ref[:, D_MODEL:] = (dm * pb * sb * (1.0 - sb)).astype(BF16)
        dpa = (dm * sa).astype(BF16)
        dpb = (dm * sb).astype(BF16)
        dpa_ref[...] = dpa
        dpb_ref[...] = dpb
        dya_ref[...] = _dot_nt(dpa, woa_ref[...])
        dyb_ref[...] = _dot_nt(dpb, wob_ref[...])

    def sgu_part(dyb_ref, ub_ref, vb_ref, lg_ref, lb_ref, ws_ref, bias_ref, dz_ref, dlg_ref, dlb_ref, dws_ref, dbs_ref,
                 dvn_ref, dsp_acc):
        i = pl.program_id(0)

        @pl.when(i == 0)
        def _():
            dlg_ref[...] = jnp.zeros_like(dlg_ref)
            dlb_ref[...] = jnp.zeros_like(dlb_ref)
            dws_ref[...] = jnp.zeros_like(dws_ref)
            dsp_acc[...] = jnp.zeros_like(dsp_acc)

        u, du, dvg, rstd, vhat, vn = _sgu_forward_parts(ub_ref[...], vb_ref[...], lg_ref, lb_ref)
        dyb = dyb_ref[...]
        mask = _causal_mask()
        wm = [jnp.where(mask, ws_ref[g], 0.0).astype(BF16) for g in range(GROUPS)]
        for c in range(SEQ_TILE // CHUNK):
            rows = slice(c * CHUNK, (c + 1) * CHUNK)
            for g in range(GROUPS):
                cols = slice(g * GROUP_DIM, (g + 1) * GROUP_DIM)
                vn_blk = vn[rows, cols]
                sp = _dot(wm[g], vn_blk) + bias_ref[:, cols]
                dyb_blk = dyb[rows, cols]
                dz_ref[rows, cols] = (dyb_blk * sp * du[rows, cols]).astype(BF16)
                dsp = dyb_blk * u[rows, cols]
                dsp_acc[:, cols] = dsp_acc[:, cols] + dsp
                dspb = dsp.astype(BF16)
                dvn_ref[rows, cols] = _dot_tn(wm[g], dspb)
                wcols = slice(g * CHUNK, (g + 1) * CHUNK)
                dws_ref[:, wcols] = dws_ref[:, wcols] + jnp.where(mask, _dot_nt(dspb, vn_blk), 0.0)
        dvn = dvn_ref[...]
        dlg_ref[...] = dlg_ref[...] + _col_sum(dvn * vhat)
        dlb_ref[...] = dlb_ref[...] + _col_sum(dvn)
        dvhat = dvn * lg_ref[...]
        dvgel = rstd * (dvhat - jnp.mean(dvhat, axis=-1, keepdims=True)
                        - vhat * jnp.mean(dvhat * vhat, axis=-1, keepdims=True))
        dz_ref[:, D_MODEL:] = (dvgel * dvg).astype(BF16)

        @pl.when(i == n_tiles - 1)
        def _():
            lane = lax.broadcasted_iota(jnp.int32, (CHUNK, 128), 1)
            out = jnp.zeros((CHUNK, 128), F32)
            for g in range(GROUPS):
                s = jnp.sum(dsp_acc[:, g * GROUP_DIM:(g + 1) * GROUP_DIM], axis=1, keepdims=True)
                out = out + jnp.where(lane == g, s, 0.0)
            dbs_ref[...] = out

    def lru_part(dya_ref, xa_ref, ga_ref, h_ref, h_prev_ref, xc_ref, r_ref, ig_ref, cw_ref, wr_ref, wi_ref, lam_ref,
                 dz_ref, dcw_ref, dcb_ref, dwr_ref, dbr_ref, dwi_ref, dbi_ref, dlam_ref, lam_carry, dxc_head):
        i = pl.program_id(0)

        @pl.when(i == 0)
        def _():
            for ref in (dcw_ref, dcb_ref, dwr_ref, dbr_ref, dwi_ref, dbi_ref, dlam_ref, lam_carry, dxc_head):
                ref[...] = jnp.zeros_like(ref)

        first_tile = i == n_tiles - 1
        h_tail = jnp.where(first_tile, 0.0, h_prev_ref[...])
        xc, r, ig = xc_ref[...], r_ref[...], ig_ref[...]
        xcb = xc.astype(BF16)
        sp, a, mult = _decay(r, lam_ref)
        h = h_ref[...]
        h_prev = _shift_down(h, h_tail, 1)
        dya = dya_ref[...]
        gg, dgg = _gelu_and_grad(ga_ref[...])
        dz_ref[:, D_MODEL:] = (dya * h * dgg).astype(BF16)
        ones = jnp.ones((SUBLANES, D_MODEL), F32)
        lam_t, lam_first = _scan_backward(_shift_up(a, ones, 1), dya * gg, lam_carry[...])
        lam_carry[...] = a[0:1] * lam_first
        dmult = lam_t * xc * ig
        dla = lam_t * h_prev * a - dmult * (a * a) / mult
        dr = dla * ((-LRU_C) * sp)
        dlam_ref[...] = dlam_ref[...] + _col_sum(dla * r) * (LRU_C * jax.nn.sigmoid(-lam_ref[...]))
        dpr = dr * r * (1.0 - r)
        dpi = lam_t * xc * mult * ig * (1.0 - ig)
        dbr_ref[...] = dbr_ref[...] + _col_sum(dpr)
        dbi_ref[...] = dbi_ref[...] + _col_sum(dpi)
        dprb = dpr.astype(BF16)
        dpib = dpi.astype(BF16)
        dxc_gate = []
        for hd in range(HEADS):
            cols = slice(hd * HEAD_DIM, (hd + 1) * HEAD_DIM)
            dxc_gate.append(_dot_nt(dprb[:, cols], wr_ref[hd]) + _dot_nt(dpib[:, cols], wi_ref[hd]))
            dwr_ref[hd] = dwr_ref[hd] + _dot_tn(xcb[:, cols], dprb[:, cols])
            dwi_ref[hd] = dwi_ref[hd] + _dot_tn(xcb[:, cols], dpib[:, cols])
        dxc = lam_t * ig * mult + jnp.concatenate(dxc_gate, axis=1)
        dcb_ref[...] = dcb_ref[...] + _col_sum(dxc)
        cw = cw_ref[...]
        head = dxc_head[...]
        xa = xa_ref[...]
        dxa = cw[0:1] * dxc
        dcw_ref[0:1, :] = dcw_ref[0:1, :] + _col_sum(dxc * xa)
        for k in range(1, CONV_WIDTH):
            dxc_k = _shift_up(dxc, head, k)
            dxa = dxa + cw[k:k + 1] * dxc_k
            dcw_ref[k:k + 1, :] = dcw_ref[k:k + 1, :] + _col_sum(dxc_k * xa)
        dxc_head[...] = dxc[0:SUBLANES]
        dz_ref[:, :D_MODEL] = dxa.astype(BF16)

    def body(dh1_ref, pa_ref, pb_ref, z_ref, h_ref, h_prev_ref, xc_ref, r_ref, ig_ref, woa_ref, wob_ref, wout_ref,
             lg_ref, lb_ref, ws_ref, bias_ref, cw_ref, wr_ref, wi_ref, lam_ref, dz_ref, mg_ref, dpa_ref, dpb_ref,
             dh1b_ref, dlg_ref, dlb_ref, dws_ref, dbs_ref, dcw_ref, dcb_ref, dwr_ref, dbr_ref, dwi_ref, dbi_ref,
             dlam_ref, dya_ref, dyb_ref, dvn_ref, dsp_acc, lam_carry, dxc_head):
        def cols(ref, first, count):
            return ref.at[:, pl.ds(first * D_MODEL, count * D_MODEL)]

        merge_part(dh1_ref, pa_ref, pb_ref, cols(z_ref, 4, 2), woa_ref, wob_ref, wout_ref, cols(dz_ref, 4, 2), dya_ref,
                   dyb_ref, mg_ref, dpa_ref, dpb_ref, dh1b_ref)
        sgu_part(dyb_ref, cols(z_ref, 2, 1), cols(z_ref, 3, 1), lg_ref, lb_ref, ws_ref, bias_ref, cols(dz_ref, 2, 2),
                 dlg_ref, dlb_ref, dws_ref, dbs_ref, dvn_ref, dsp_acc)
        lru_part(dya_ref, cols(z_ref, 0, 1), cols(z_ref, 1, 1), h_ref, h_prev_ref, xc_ref, r_ref, ig_ref, cw_ref, wr_ref,
                 wi_ref, lam_ref, cols(dz_ref, 0, 2), dcw_ref, dcb_ref, dwr_ref, dbr_ref, dwi_ref, dbi_ref, dlam_ref,
                 lam_carry, dxc_head)

    rev = lambda i: n_tiles - 1 - i
    tile = pl.BlockSpec((SEQ_TILE, D_MODEL), lambda i: (rev(i), 0))
    row = pl.BlockSpec((SEQ_TILE, D_IN), lambda i: (rev(i), 0))
    prev8 = pl.BlockSpec((SUBLANES, D_MODEL), lambda i: (jnp.maximum(rev(i) * per_tile - 1, 0), 0))
    vec = _const((1, D_MODEL))
    sq = _resident((D_MODEL, D_MODEL))
    gate_w = _resident((HEADS, HEAD_DIM, HEAD_DIM))
    gate_acc = _const((HEADS, HEAD_DIM, HEAD_DIM))
    vec_shape = jax.ShapeDtypeStruct((1, D_MODEL), F32)
    gate_shape = jax.ShapeDtypeStruct((HEADS, HEAD_DIM, HEAD_DIM), F32)
    act_bf = jax.ShapeDtypeStruct((t, D_MODEL), BF16)
    return _fused_call(
        body, jobs, name="bwd_mix", grid=(n_tiles,),
        in_specs=[tile, tile, tile, row, tile, prev8, tile, tile, tile, sq, sq, sq, vec, vec,
                  _const((GROUPS, CHUNK, CHUNK)), _const((CHUNK, D_MODEL)), _const((CONV_WIDTH, D_MODEL)), gate_w, gate_w,
                  vec],
        out_specs=[row, tile, tile, tile, tile, vec, vec, _const((CHUNK, GROUPS * CHUNK)), _const((CHUNK, 128)),
                   _const((SUBLANES, D_MODEL)), vec, gate_acc, vec, gate_acc, vec, vec],
        out_shape=[jax.ShapeDtypeStruct((t, D_IN), BF16), act_bf, act_bf, act_bf, act_bf, vec_shape, vec_shape,
                   jax.ShapeDtypeStruct((CHUNK, GROUPS * CHUNK), F32), jax.ShapeDtypeStruct((CHUNK, 128), F32),
                   jax.ShapeDtypeStruct((SUBLANES, D_MODEL), F32), vec_shape, gate_shape, vec_shape, gate_shape,
                   vec_shape, vec_shape],
        scratch_shapes=[pltpu.VMEM((SEQ_TILE, D_MODEL), F32), pltpu.VMEM((SEQ_TILE, D_MODEL), F32),
                        pltpu.VMEM((SEQ_TILE, D_MODEL), F32), pltpu.VMEM((CHUNK, D_MODEL), F32),
                        pltpu.VMEM((1, D_MODEL), F32), pltpu.VMEM((SUBLANES, D_MODEL), F32)],
        compiler_params=_params(),
    )(dh1, pa, pb, z, h, h, xc, r, ig, w_oa, w_ob, w_out, ln_g, ln_b, w_s, bias_full, conv_w, wr, wi, lam)


def _bwd_in(dz, x, dh1, w_in_st, g1, jobs=()):
    t = x.shape[0]

    def body(dz_ref, x_ref, dh1_ref, w_ref, g_ref, dx_ref, dg1_ref):
        @pl.when(pl.program_id(0) == 0)
        def _():
            dg1_ref[...] = jnp.zeros_like(dg1_ref)

        dn1 = jnp.zeros((MM_TILE, D_MODEL), F32)
        for k in range(N_CHIPS):
            dn1 = dn1 + _dot_nt(dz_ref[:, k * IN_SHARD:(k + 1) * IN_SHARD], w_ref[k])
        xhat, r1 = _rms(x_ref[...])
        dg1_ref[...] = dg1_ref[...] + _col_sum(dn1 * xhat)
        dx_ref[...] = dh1_ref[...] + _rms_bwd(dn1 * g_ref[...], xhat, r1)

    tile = pl.BlockSpec((MM_TILE, D_MODEL), lambda i: (i, 0))
    return _fused_call(
        body, jobs, name="bwd_in", grid=(t // MM_TILE,),
        in_specs=[pl.BlockSpec((MM_TILE, D_IN), lambda i: (i, 0)), tile, tile,
                  _resident((N_CHIPS, D_MODEL, IN_SHARD)), _const((1, D_MODEL))],
        out_specs=[tile, _const((1, D_MODEL))],
        out_shape=[jax.ShapeDtypeStruct((t, D_MODEL), F32), jax.ShapeDtypeStruct((1, D_MODEL), F32)],
        compiler_params=_params(),
    )(dz, x, dh1, w_in_st, g1)


def _weight_grad(name, a, b, n_blocks, a_varies, b_varies, width, jobs=()):
    t = a.shape[0]
    rows = min(DW_TILE, t)
    n_t = t // rows

    def body(a_ref, b_ref, o_ref, acc_ref):
        s = pl.program_id(1)
        part = _dot_tn(a_ref[...], b_ref[...])

        @pl.when(s == 0)
        def _():
            acc_ref[...] = part

        @pl.when(s > 0)
        def _():
            acc_ref[...] = acc_ref[...] + part

        @pl.when(s == n_t - 1)
        def _():
            o_ref[...] = acc_ref[...].astype(BF16)

    return _fused_call(
        body, jobs, name=name, grid=(n_blocks, n_t),
        in_specs=[pl.BlockSpec((rows, D_MODEL), (lambda j, s: (s, j)) if a_varies else (lambda j, s: (s, 0))),
                  pl.BlockSpec((rows, width), (lambda j, s: (s, j)) if b_varies else (lambda j, s: (s, 0)))],
        out_specs=pl.BlockSpec((None, D_MODEL, width), lambda j, s: (j, 0, 0)),
        out_shape=jax.ShapeDtypeStruct((n_blocks, D_MODEL, width), BF16),
        scratch_shapes=[pltpu.VMEM((D_MODEL, width), F32)],
        compiler_params=_params(2),
    )(a, b)


def _place():
    x, y, c = lax.axis_index("x"), lax.axis_index("y"), lax.axis_index("c")
    other_chips = [(1 - x, y), (x, 1 - y), (1 - x, 1 - y)]
    return x, y, c, other_chips


def _chip_index(px, py):
    return 2 * px + py


ANY = pl.BlockSpec(memory_space=pl.ANY)


def _comm_call(name, jobs):
    return _fused_call(None, jobs, name=name, grid=(), in_specs=[], out_specs=[], out_shape=[])()[1]


def _near_far(x, y, c):
    return (x ^ (1 - c), y ^ c), (x ^ c, y ^ (1 - c))


def _gather_near_job(shards):
    n = len(shards)
    halves = [s.shape[0] // 2 for s in shards]

    def copies(ins, outs, send, recv, local):
        x, y, c, _ = _place()
        near, _ = _near_far(x, y, c)

        def block(w, chip, pc):
            return outs[w].at[_chip_index(*chip), pl.ds(pc * halves[w], halves[w]), :]

        def copy(w, k, chip, pc, to, src=None):
            return pltpu.make_async_remote_copy(
                src_ref=block(w, chip, pc) if src is None else src, dst_ref=block(w, chip, pc),
                send_sem=send.at[2 * w + k], recv_sem=recv.at[2 * w + k], device_id=to, device_id_type=MESH)

        sends, arrivals, own = [], [], []
        for w in range(n):
            src = ins[w].at[pl.ds(c * halves[w], halves[w]), :]
            own.append(pltpu.make_async_copy(src, block(w, (x, y), c), local.at[w]))
            sends += [copy(w, 0, (x, y), c, (*near, c), src), copy(w, 1, (x, y), c, (x, y, 1 - c), src)]
            arrivals += [copy(w, 0, near, c, (x, y, c)), copy(w, 1, (x, y), 1 - c, (x, y, c))]
        return sends, arrivals, own

    return _Job(shards, [jax.ShapeDtypeStruct((N_CHIPS,) + s.shape, s.dtype) for s in shards], 2 * n, copies,
                n_local=n)


def _gather_far_job(stacked):
    n = len(stacked)
    halves = [s.shape[1] // 2 for s in stacked]

    def copies(ins, outs, send, recv, local):
        del ins, local
        x, y, c, _ = _place()
        near, far = _near_far(x, y, c)

        def copy(w, k, chip):
            blk = outs[w].at[_chip_index(*chip), pl.ds(c * halves[w], halves[w]), :]
            return pltpu.make_async_remote_copy(
                src_ref=blk, dst_ref=blk, send_sem=send.at[2 * w + k], recv_sem=recv.at[2 * w + k],
                device_id=(*far, c), device_id_type=MESH)

        sends = [copy(w, k, chip) for w in range(n) for k, chip in enumerate(((x, y), near))]
        arrivals = [copy(w, k, chip) for w in range(n) for k, chip in enumerate((far, (1 - x, 1 - y)))]
        return sends, arrivals, []

    return _Job(stacked, [jax.ShapeDtypeStruct(s.shape, s.dtype) for s in stacked], 2 * n, copies,
                aliases={w: w for w in range(n)})


def _gather_pass_job(stacked):
    n = len(stacked)
    halves = [s.shape[1] // 2 for s in stacked]

    def copies(ins, outs, send, recv, local):
        del ins, local
        x, y, c, chips = _place()

        def copy(w, j, chip, pc, to):
            blk = outs[w].at[_chip_index(*chip), pl.ds(pc * halves[w], halves[w]), :]
            return pltpu.make_async_remote_copy(
                src_ref=blk, dst_ref=blk, send_sem=send.at[3 * w + j], recv_sem=recv.at[3 * w + j], device_id=to,
                device_id_type=MESH)

        sends = [copy(w, j, chip, c, (x, y, 1 - c)) for w in range(n) for j, chip in enumerate(chips)]
        arrivals = [copy(w, j, chip, 1 - c, (x, y, c)) for w in range(n) for j, chip in enumerate(chips)]
        return sends, arrivals, []

    return _Job(stacked, [jax.ShapeDtypeStruct(s.shape, s.dtype) for s in stacked], 3 * n, copies,
                aliases={w: w for w in range(n)})


def _gather_small_job(block):
    def copies(ins, outs, send, recv, local):
        x, y, c, chips = _place()

        def copy(j, chip_from, to):
            return pltpu.make_async_remote_copy(
                src_ref=ins[0], dst_ref=outs[0].at[_chip_index(*chip_from)], send_sem=send.at[j],
                recv_sem=recv.at[j], device_id=to, device_id_type=MESH)

        own = [pltpu.make_async_copy(ins[0], outs[0].at[_chip_index(x, y)], local.at[0])]
        sends = [copy(j, (x, y), (*chip, c)) for j, chip in enumerate(chips)]
        arrivals = [copy(j, chip, (x, y, c)) for j, chip in enumerate(chips)]
        return sends, arrivals, own

    return _Job([block], [jax.ShapeDtypeStruct((N_CHIPS,) + block.shape, block.dtype)], 3, copies, n_local=1)


def _pair_send_job(grads):
    n = len(grads)
    halves = [g.shape[1] // 2 for g in grads]

    def copies(ins, outs, send, recv, local):
        del local
        x, y, c, _ = _place()
        sends = [pltpu.make_async_remote_copy(
            src_ref=ins[w].at[:, pl.ds((1 - c) * halves[w], halves[w]), :], dst_ref=outs[w], send_sem=send.at[w],
            recv_sem=recv.at[w], device_id=(x, y, 1 - c), device_id_type=MESH) for w in range(n)]
        return sends, sends, []

    return _Job(grads, [jax.ShapeDtypeStruct((N_CHIPS, h, g.shape[2]), g.dtype) for g, h in zip(grads, halves)], n,
                copies)


def _row_block(rows, limit=256):
    return min(rows, limit)


def _pair_add(name, core, mine, theirs):
    _, _, h, cols = mine.shape
    rb = _row_block(h, 512)

    def body(core_ref, a_ref, b_ref, o_ref):
        del core_ref
        o_ref[...] = (a_ref[...].astype(F32) + b_ref[...].astype(F32)).astype(BF16)

    return pl.pallas_call(
        body, name=name,
        grid_spec=pltpu.PrefetchScalarGridSpec(
            num_scalar_prefetch=1, grid=(N_CHIPS, h // rb),
            in_specs=[pl.BlockSpec((None, None, rb, cols), lambda k, r, core_ref: (k, core_ref[0], r, 0)),
                      pl.BlockSpec((None, rb, cols), lambda k, r, core_ref: (k, r, 0))],
            out_specs=pl.BlockSpec((None, rb, cols), lambda k, r, core_ref: (k, r, 0))),
        out_shape=jax.ShapeDtypeStruct(theirs.shape, BF16),
        compiler_params=_params(2),
    )(core, mine, theirs)


def _chip_exchange_job(sums):
    n = len(sums)

    def copies(ins, outs, send, recv, local):
        del local
        _, _, c, chips = _place()
        sends = [pltpu.make_async_remote_copy(
            src_ref=ins[w].at[_chip_index(*chip)], dst_ref=outs[w].at[j], send_sem=send.at[3 * w + j],
            recv_sem=recv.at[3 * w + j], device_id=(*chip, c), device_id_type=MESH)
            for w in range(n) for j, chip in enumerate(chips)]
        return sends, sends, []

    return _Job(sums, [jax.ShapeDtypeStruct((N_CHIPS - 1,) + s.shape[1:], s.dtype) for s in sums], 3 * n, copies)


def _chip_sum(name, place, mine, theirs):
    _, h, cols = mine.shape
    rb = _row_block(h, 512)

    def body(place_ref, p_ref, q_ref, o_ref):
        del place_ref
        acc = p_ref[...].astype(F32)
        for j in range(N_CHIPS - 1):
            acc = acc + q_ref[j].astype(F32)
        o_ref[...] = acc

    return pl.pallas_call(
        body, name=name,
        grid_spec=pltpu.PrefetchScalarGridSpec(
            num_scalar_prefetch=1, grid=(h // rb,),
            in_specs=[pl.BlockSpec((None, rb, cols), lambda r, place_ref: (place_ref[0], r, 0)),
                      pl.BlockSpec((N_CHIPS - 1, rb, cols), lambda r, place_ref: (0, r, 0))],
            out_specs=pl.BlockSpec((None, rb, cols), lambda r, place_ref: (place_ref[1], r, 0))),
        out_shape=jax.ShapeDtypeStruct((2, h, cols), F32),
        compiler_params=_params(),
    )(place, mine, theirs)


def _share_job(bufs):
    n = len(bufs)

    def copies(ins, outs, send, recv, local):
        del ins, local
        x, y, c, _ = _place()

        def copy(w, half):
            return pltpu.make_async_remote_copy(
                src_ref=outs[w].at[half], dst_ref=outs[w].at[half], send_sem=send.at[w], recv_sem=recv.at[w],
                device_id=(x, y, 1 - c), device_id_type=MESH)

        return [copy(w, c) for w in range(n)], [copy(w, 1 - c) for w in range(n)], []

    return _Job(bufs, [jax.ShapeDtypeStruct(b.shape, b.dtype) for b in bufs], n, copies,
                aliases={w: w for w in range(n)})


SMALL_ROWS = 24
ROW_G1, ROW_CW, ROW_CB, ROW_BR, ROW_BI, ROW_LAM, ROW_LG, ROW_LB, ROW_G2, ROW_G3, ROW_LOSS, ROW_BS = (
    0, 1, 5, 6, 7, 8, 9, 10, 11, 12, 13, 16)
N_DEV = 8


def _pack_small(dcw, dcb, dbr, dbi, dlam, dlg, dlb, dg2, dg3, loss, dbs):
    def body(dcw_ref, dcb_ref, dbr_ref, dbi_ref, dlam_ref, dlg_ref, dlb_ref, dg2_ref, dg3_ref, loss_ref, dbs_ref, out):
        out[...] = jnp.zeros((SMALL_ROWS, D_MODEL), F32)
        for row, ref in ((ROW_CB, dcb_ref), (ROW_BR, dbr_ref), (ROW_BI, dbi_ref), (ROW_LAM, dlam_ref),
                         (ROW_LG, dlg_ref), (ROW_LB, dlb_ref), (ROW_G2, dg2_ref), (ROW_G3, dg3_ref)):
            out[row:row + 1, :] = ref[...]
        out[ROW_CW:ROW_CW + CONV_WIDTH, :] = dcw_ref[0:CONV_WIDTH, :]
        out[ROW_LOSS:ROW_LOSS + 1, 0:128] = loss_ref[0:1, :]
        out[ROW_BS:ROW_BS + GROUPS, 0:128] = jnp.transpose(dbs_ref[...])[0:GROUPS, :]

    vm = pl.BlockSpec(memory_space=pltpu.VMEM)
    return pl.pallas_call(
        body, name="pack_small", in_specs=[vm] * 11, out_specs=vm,
        out_shape=jax.ShapeDtypeStruct((SMALL_ROWS, D_MODEL), F32),
    )(dcw, dcb, dbr, dbi, dlam, dlg, dlb, dg2, dg3, loss, dbs)


def _gather_all_job(blocks):
    n = len(blocks)
    flips = [(dx, dy, dc) for dx in (0, 1) for dy in (0, 1) for dc in (0, 1)][1:]

    def copies(ins, outs, send, recv, local):
        x, y, c, _ = _place()
        me = 4 * x + 2 * y + c
        sends, arrivals, own = [], [], []
        for w in range(n):
            own.append(pltpu.make_async_copy(ins[w], outs[w].at[me], local.at[w]))
            for k, (dx, dy, dc) in enumerate(flips):
                peer = (x ^ dx, y ^ dy, c ^ dc)
                sem = dict(send_sem=send.at[7 * w + k], recv_sem=recv.at[7 * w + k])
                sends.append(pltpu.make_async_remote_copy(
                    src_ref=ins[w], dst_ref=outs[w].at[me], device_id=peer, device_id_type=MESH, **sem))
                arrivals.append(pltpu.make_async_remote_copy(
                    src_ref=ins[w], dst_ref=outs[w].at[4 * peer[0] + 2 * peer[1] + peer[2]], device_id=peer,
                    device_id_type=MESH, **sem))
        return sends, arrivals, own

    return _Job(blocks, [jax.ShapeDtypeStruct((N_DEV,) + b.shape, b.dtype) for b in blocks], 7 * n, copies, n_local=n)


def _sum_small(vec_all, ws_all, dg1_all):
    def body(vec_ref, ws_ref, dg1_ref, vec_out, ws_out):
        vec, ws, dg1 = vec_ref[0], ws_ref[0], dg1_ref[0]
        for d in range(1, N_DEV):
            vec, ws, dg1 = vec + vec_ref[d], ws + ws_ref[d], dg1 + dg1_ref[d]
        vec_out[...] = vec
        vec_out[ROW_G1:ROW_G1 + 1, :] = dg1
        ws_out[...] = ws

    vm = pl.BlockSpec(memory_space=pltpu.VMEM)
    return pl.pallas_call(
        body, name="sum_small", in_specs=[vm] * 3, out_specs=[vm, vm],
        out_shape=[jax.ShapeDtypeStruct(vec_all.shape[1:], F32), jax.ShapeDtypeStruct(ws_all.shape[1:], F32)],
    )(vec_all, ws_all, dg1_all)


def _adamw_math(w, g, m, v):
    m = ADAM_B1 * m + (1.0 - ADAM_B1) * g
    v = ADAM_B2 * v + (1.0 - ADAM_B2) * (g * g)
    m_hat = m / (1.0 - ADAM_B1 ** ADAM_STEP)
    v_hat = v / (1.0 - ADAM_B2 ** ADAM_STEP)
    delta = (-ADAM_LR) * (m_hat / (jnp.sqrt(v_hat) + ADAM_EPS) + ADAM_WD * w)
    return delta, m, v


def _adamw(name, g, w, m, v, jobs=()):
    rows, cols = w.shape
    rb = _row_block(rows)

    def body(g_ref, w_ref, m_ref, v_ref, d_ref, nm_ref, nv_ref):
        d_ref[...], nm_ref[...], nv_ref[...] = _adamw_math(w_ref[...], g_ref[...], m_ref[...], v_ref[...])

    blk = pl.BlockSpec((rb, cols), lambda r: (r, 0))
    return _fused_call(
        body, jobs, name=name, grid=(rows // rb,), in_specs=[blk] * 4, out_specs=[blk] * 3,
        out_shape=[jax.ShapeDtypeStruct(w.shape, F32)] * 3, compiler_params=_params(),
    )(g, w, m, v)


def _adamw_small(grads, ws, ms, vs):
    n = len(grads)

    def body(*refs):
        g_refs, w_refs, m_refs, v_refs = refs[:n], refs[n:2 * n], refs[2 * n:3 * n], refs[3 * n:4 * n]
        outs = refs[4 * n:]
        for p in range(n):
            d, nm, nv = _adamw_math(w_refs[p][...], g_refs[p][...], m_refs[p][...], v_refs[p][...])
            outs[p][...] = d
            outs[n + p][...] = nm
            outs[2 * n + p][...] = nv

    vm = pl.BlockSpec(memory_space=pltpu.VMEM)
    shapes = [jax.ShapeDtypeStruct(w.shape, F32) for w in ws]
    out = pl.pallas_call(
        body, name="adamw_small", in_specs=[vm] * (4 * n), out_specs=[vm] * (3 * n), out_shape=shapes * 3,
    )(*grads, *ws, *ms, *vs)
    return out[:n], out[n:2 * n], out[2 * n:]


def _unstack_heads(w_st):
    per = HEAD_DIM // N_CHIPS
    return w_st.reshape(N_CHIPS, HEADS, per, HEAD_DIM).transpose(1, 0, 2, 3).reshape(HEADS, HEAD_DIM, HEAD_DIM)


def _stack_heads(w):
    per = HEAD_DIM // N_CHIPS
    return w.reshape(HEADS, N_CHIPS, per, HEAD_DIM).transpose(1, 0, 2, 3).reshape(N_CHIPS, HEADS * per, HEAD_DIM)


def kernel(x, norm_mix_g, w_in, conv_w, conv_b, w_rgate, b_rgate, w_igate, b_igate, lru_lambda, w_out_a, sgu_ln_g, sgu_ln_b, sgu_w_s, sgu_b_s, w_out_b, w_out, norm_mlp_g, w_up, w_down, norm_final_g, loss_target, m_norm_mix_g, m_w_in, m_conv_w, m_conv_b, m_w_rgate, m_b_rgate, m_w_igate, m_b_igate, m_lru_lambda, m_w_out_a, m_sgu_ln_g, m_sgu_ln_b, m_sgu_w_s, m_sgu_b_s, m_w_out_b, m_w_out, m_norm_mlp_g, m_w_up, m_w_down, m_norm_final_g, v_norm_mix_g, v_w_in, v_conv_w, v_conv_b, v_w_rgate, v_b_rgate, v_w_igate, v_b_igate, v_lru_lambda, v_w_out_a, v_sgu_ln_g, v_sgu_ln_b, v_sgu_w_s, v_sgu_b_s, v_w_out_b, v_w_out, v_norm_mlp_g, v_w_up, v_w_down, v_norm_final_g):
    chip = _chip_index(lax.axis_index("x"), lax.axis_index("y"))
    core = lax.axis_index("c")
    quarter_h = HEAD_DIM // N_CHIPS
    quarter_d = D_MODEL // N_CHIPS

    as_2d = lambda a: a.reshape(-1, a.shape[-1])
    big_w = [as_2d(w) for w in (w_in, w_rgate, w_igate, w_out_a, w_out_b, w_out, w_up, w_down)]
    big_m = [as_2d(w) for w in (m_w_in, m_w_rgate, m_w_igate, m_w_out_a, m_w_out_b, m_w_out, m_w_up, m_w_down)]
    big_v = [as_2d(w) for w in (v_w_in, v_w_rgate, v_w_igate, v_w_out_a, v_w_out_b, v_w_out, v_w_up, v_w_down)]

    packed = jnp.concatenate([conv_w[0], b_rgate[0], b_igate[0]], axis=1)
    packed = jnp.concatenate([packed, jnp.zeros_like(packed)], axis=0)
    s_in, s_r, s_i, s_oa, s_ob, s_out, s_up, s_down = [w.astype(BF16) for w in big_w]
    xs, target = x[0], loss_target[0]
    g3 = norm_final_g.reshape(1, D_MODEL)
    bias_s = jnp.broadcast_to(jnp.transpose(sgu_b_s[0])[:, :, None], (CHUNK, GROUPS, GROUP_DIM)).reshape(CHUNK, D_MODEL)
    core_arr = core.reshape(1).astype(jnp.int32)
    place = jnp.stack([chip, core]).astype(jnp.int32)
    quarter = lambda g: g.reshape(N_CHIPS, D_MODEL // N_CHIPS, D_MODEL)

    def pair_add(nm, g, from_sibling):
        return _pair_add("pair_add_" + nm, core_arr, g.reshape(N_CHIPS, 2, g.shape[1] // 2, g.shape[2]), from_sibling)

    def chip_sum(nm, pair, from_chips):
        return _chip_sum("chip_sum_" + nm, place, pair, from_chips)

    order = jnp.stack([chip, chip ^ 2, chip ^ 1, chip ^ 3]).astype(jnp.int32)
    (z, n1, (w_in_st, wr_st, wi_st)), ((packed_all,), late) = _fwd_in(
        xs, norm_mix_g, [s_in, s_r, s_i], order,
        jobs=[_gather_small_job(packed), _gather_near_job([s_oa, s_ob, s_out, s_up, s_down])])
    pick = lambda lo, hi: packed_all[:, :HEADS, lo:hi].transpose(1, 0, 2).reshape(HEADS, -1)
    conv_w_full = pick(0, quarter_d)
    br_full = pick(quarter_d, quarter_d + quarter_h).reshape(1, D_MODEL)
    bi_full = pick(quarter_d + quarter_h, quarter_d + 2 * quarter_h).reshape(1, D_MODEL)
    wr, wi = _unstack_heads(wr_st), _unstack_heads(wi_st)
    lru = (conv_w_full, conv_b, wr, br_full, wi, bi_full, lru_lambda)
    sgu = (sgu_ln_g, sgu_ln_b, sgu_w_s[0], bias_s)

    (ya, *saved), (late,) = _fwd_lru(z, *lru, jobs=[_gather_far_job(late)])
    yb, (late,) = _fwd_sgu(z, *sgu, jobs=[_gather_pass_job(late)])
    w_oa, w_ob, w_o = [w.reshape(D_MODEL, D_MODEL) for w in late[:3]]
    w_up_st, w_dn = late[3], late[4].reshape(D_FF, D_MODEL)
    (pa, pb, h1, n2), _ = _fwd_merge(ya, yb, z, xs, w_oa, w_ob, w_o, norm_mlp_g)
    (act, dup, dh2b, dh1, loss_part, dg3, dg2), _ = _mlp(n2, h1, target, w_up_st, w_dn, norm_mlp_g, g3)

    d_up, _ = _weight_grad("dw_up", n2, dup, N_CHIPS, False, True, D_MODEL)
    d_down, ((r_up,),) = _weight_grad("dw_down", act, dh2b, N_CHIPS, True, False, D_MODEL,
                                      jobs=[_pair_send_job([d_up])])
    (r_down,), = _comm_call("send_w_down", [_pair_send_job([d_down])])
    p_up, p_down = pair_add("w_up", d_up, r_up), pair_add("w_down", d_down, r_down)
    (dz, merged, dpa, dpb, dh1b, dlg, dlb, dws, dbs, dcw, dcb, dwr, dbr, dwi, dbi, dlam), ((q_up, q_down),) = _bwd_mix(
        dh1, pa, pb, z, *saved, w_oa, w_ob, w_o, *sgu, conv_w_full, wr, wi, lru_lambda,
        jobs=[_chip_exchange_job([p_up, p_down])])
    half_up, half_down = chip_sum("w_up", p_up, q_up), chip_sum("w_down", p_down, q_down)
    gates = [_stack_heads(dwr).astype(BF16), _stack_heads(dwi).astype(BF16)]
    small = _pack_small(dcw, dcb, dbr, dbi, dlam, dlg, dlb, dg2, dg3, loss_part, dbs)
    d_in, ((full_up, full_down), r_gates, (vec_all, ws_all)) = _weight_grad(
        "dw_in", n1, dz, N_CHIPS, False, True, IN_SHARD,
        jobs=[_share_job([half_up, half_down]), _pair_send_job(gates), _gather_all_job([small, dws])])
    d_out, ((r_in,),) = _weight_grad("dw_out", merged, dh1b, 1, False, False, D_MODEL, jobs=[_pair_send_job([d_in])])
    d_oa, _ = _weight_grad("dw_out_a", ya, dpa, 1, False, False, D_MODEL)
    d_ob, _ = _weight_grad("dw_out_b", yb, dpb, 1, False, False, D_MODEL)
    mids = [quarter(d_oa), quarter(d_ob), quarter(d_out)]
    names = ("w_in", "w_rgate", "w_igate", "w_out_a", "w_out_b", "w_out", "w_up", "w_down")
    adam_args = {nm: (w, m, v) for nm, w, m, v in zip(names, big_w, big_m, big_v)}

    def adamw(nm, g):
        w, m, v = adam_args[nm]
        g = g.reshape(w.shape)
        return g, _adamw("adamw_" + nm, g, w, m, v)[0]

    p_first = [pair_add(nm, g, r) for nm, g, r in zip(names[:3], [d_in] + gates, [r_in] + r_gates)]
    (grad_x, dg1), (q_first, r_mids) = _bwd_in(
        dz, xs, dh1, w_in_st, norm_mix_g, jobs=[_chip_exchange_job(p_first), _pair_send_job(mids)])
    half_first = [chip_sum(nm, p, q) for nm, p, q in zip(names[:3], p_first, q_first)]
    p_mids = [pair_add(nm, g, r) for nm, g, r in zip(names[3:6], mids, r_mids)]
    q_mids, = _comm_call("exchange_mids", [_chip_exchange_job(p_mids)])
    half_mids = [chip_sum(nm, p, q) for nm, p, q in zip(names[3:6], p_mids, q_mids)]
    full_last, (dg1_all,) = _comm_call("share_last", [_share_job(half_first + half_mids), _gather_all_job([dg1])])
    full, big_out = [], []
    for nm, f in zip(names, full_last + [full_up, full_down]):
        g, out = adamw(nm, f)
        full.append(g)
        big_out.append(out)

    vec, ws_sum = _sum_small(vec_all, ws_all, dg1_all)
    row = lambda r: vec[r:r + 1]
    shard = lambda a, width: lax.dynamic_slice_in_dim(a, chip * width, width, axis=1)
    g_small = dict(
        norm_mix_g=row(ROW_G1), conv_w=shard(vec[ROW_CW:ROW_CW + CONV_WIDTH], quarter_d), conv_b=row(ROW_CB),
        b_rgate=shard(row(ROW_BR).reshape(HEADS, HEAD_DIM), quarter_h),
        b_igate=shard(row(ROW_BI).reshape(HEADS, HEAD_DIM), quarter_h), lru_lambda=row(ROW_LAM),
        sgu_ln_g=row(ROW_LG), sgu_ln_b=row(ROW_LB),
        sgu_w_s=ws_sum.reshape(CHUNK, GROUPS, CHUNK).transpose(1, 0, 2).reshape(GROUPS * CHUNK, CHUNK),
        sgu_b_s=vec[ROW_BS:ROW_BS + GROUPS, 0:CHUNK], norm_mlp_g=row(ROW_G2), norm_final_g=row(ROW_G3))
    loss = vec[ROW_LOSS, 0]
    small_names = list(g_small)
    given = dict(
        norm_mix_g=(norm_mix_g, m_norm_mix_g, v_norm_mix_g), conv_w=(conv_w, m_conv_w, v_conv_w),
        conv_b=(conv_b, m_conv_b, v_conv_b), b_rgate=(b_rgate, m_b_rgate, v_b_rgate),
        b_igate=(b_igate, m_b_igate, v_b_igate), lru_lambda=(lru_lambda, m_lru_lambda, v_lru_lambda),
        sgu_ln_g=(sgu_ln_g, m_sgu_ln_g, v_sgu_ln_g), sgu_ln_b=(sgu_ln_b, m_sgu_ln_b, v_sgu_ln_b),
        sgu_w_s=(sgu_w_s, m_sgu_w_s, v_sgu_w_s), sgu_b_s=(sgu_b_s, m_sgu_b_s, v_sgu_b_s),
        norm_mlp_g=(norm_mlp_g, m_norm_mlp_g, v_norm_mlp_g), norm_final_g=(norm_final_g, m_norm_final_g, v_norm_final_g))
    g2d = [g_small[nm] for nm in small_names]
    to2d = lambda a, g: a.reshape(g.shape)
    d_s, m_s, v_s = _adamw_small(
        g2d, *[[to2d(given[nm][q], g) for nm, g in zip(small_names, g2d)] for q in range(3)])

    shapes = dict(
        norm_mix_g=norm_mix_g, w_in=w_in, conv_w=conv_w, conv_b=conv_b, w_rgate=w_rgate, b_rgate=b_rgate,
        w_igate=w_igate, b_igate=b_igate, lru_lambda=lru_lambda, w_out_a=w_out_a, sgu_ln_g=sgu_ln_g,
        sgu_ln_b=sgu_ln_b, sgu_w_s=sgu_w_s, sgu_b_s=sgu_b_s, w_out_b=w_out_b, w_out=w_out, norm_mlp_g=norm_mlp_g,
        w_up=w_up, w_down=w_down, norm_final_g=norm_final_g)
    grads, deltas, new_m, new_v = {}, {}, {}, {}
    for nm, g, (d, nmom, nvar) in zip(names, full, big_out):
        grads[nm], deltas[nm], new_m[nm], new_v[nm] = g, d, nmom, nvar
    for p, nm in enumerate(small_names):
        grads[nm], deltas[nm], new_m[nm], new_v[nm] = g2d[p], d_s[p], m_s[p], v_s[p]
    order = list(shapes)
    out = [loss, grad_x[None]]
    for group in (grads, deltas, new_m, new_v):
        out += [group[nm].reshape(shapes[nm].shape) for nm in order]
    return tuple(out)
```

```python
import functools

import jax
import jax.numpy as jnp
from jax import lax
from jax.experimental import pallas as pl
from jax.experimental.pallas import tpu as pltpu
from jax.experimental.pallas import tpu_sc as plsc

F32 = jnp.float32
BF16 = jnp.bfloat16
MESH = pl.DeviceIdType.MESH

D_MODEL = 1024
D_IN = 6 * D_MODEL
D_FF = 4 * D_MODEL
N_CHIPS = 4
IN_SHARD = D_IN // N_CHIPS
HEADS = 4
HEAD_DIM = D_MODEL // HEADS
GROUPS = 4
GROUP_DIM = D_MODEL // GROUPS
CHUNK = 128
CONV_WIDTH = 4
LRU_C = 8.0
NORM_EPS = 1e-6
LN_EPS = 1e-5

ADAM_LR = 0.001
ADAM_B1 = 0.9
ADAM_B2 = 0.999
ADAM_EPS = 1e-08
ADAM_WD = 0.01
ADAM_STEP = 10

SUBLANES = 8
MM_TILE = 512
IN_TILE = 1024
SEQ_TILE = 256
DW_TILE = 2048
VMEM_LIMIT_BYTES = 56 * 1024 * 1024

GELU_K0 = 0.7978845608028654
GELU_K1 = 0.044715


def _params(n_grid_axes=1):
    return pltpu.CompilerParams(
        dimension_semantics=("arbitrary",) * n_grid_axes, vmem_limit_bytes=VMEM_LIMIT_BYTES)


def _resident(shape):
    nd = len(shape)
    return pl.BlockSpec(shape, lambda *_: (0,) * nd, pipeline_mode=pl.Buffered(1))


def _const(shape):
    nd = len(shape)
    return pl.BlockSpec(shape, lambda *_: (0,) * nd)


def _dot(a, b):
    return jnp.dot(a, b, preferred_element_type=F32)


def _dot_nt(a, b):
    return lax.dot_general(a, b, (((1,), (1,)), ((), ())), preferred_element_type=F32)


def _dot_tn(a, b):
    return lax.dot_general(a, b, (((0,), (0,)), ((), ())), preferred_element_type=F32)


def _gelu(x):
    t = jnp.tanh(GELU_K0 * x * (1.0 + GELU_K1 * x * x))
    return 0.5 * x * (1.0 + t)


def _gelu_and_grad(x):
    x2 = x * x
    t = jnp.tanh(GELU_K0 * x * (1.0 + GELU_K1 * x2))
    g = 0.5 * x * (1.0 + t)
    dg = 0.5 * (1.0 + t) + 0.5 * x * (1.0 - t * t) * (GELU_K0 * (1.0 + 3.0 * GELU_K1 * x2))
    return g, dg


def _rms(x):
    r = lax.rsqrt(jnp.mean(x * x, axis=-1, keepdims=True) + NORM_EPS)
    return x * r, r


def _rms_bwd(dn, xhat, r):
    return r * (dn - xhat * jnp.mean(dn * xhat, axis=-1, keepdims=True))


def _col_sum(v):
    return jnp.sum(v, axis=0, keepdims=True)


def _shift_down(x, tail8, k):
    xs = pltpu.roll(x, k, 0)
    ts = pltpu.roll(tail8, k, 0)
    ridx = lax.broadcasted_iota(jnp.int32, tail8.shape, 0)
    head = jnp.where(ridx < k, ts, xs[0:SUBLANES])
    return jnp.concatenate([head, xs[SUBLANES:]], axis=0)


def _shift_up(x, head8, k):
    n = x.shape[0]
    xs = pltpu.roll(x, n - k, 0)
    hs = pltpu.roll(head8, SUBLANES - k, 0)
    ridx = lax.broadcasted_iota(jnp.int32, head8.shape, 0)
    last = jnp.where(ridx >= SUBLANES - k, hs, xs[n - SUBLANES:n])
    return jnp.concatenate([xs[:n - SUBLANES], last], axis=0)


def _scan_forward(a, b, carry):
    n, cols = a.shape
    groups = n // SUBLANES
    a = a.reshape(groups, SUBLANES, cols)
    b = b.reshape(groups, SUBLANES, cols)
    sub = lax.broadcasted_iota(jnp.int32, a.shape, 1)
    for s in (1, 2, 4):
        a_s = pltpu.roll(a, s, 1)
        b_s = pltpu.roll(b, s, 1)
        m = sub >= s
        b = jnp.where(m, a * b_s + b, b)
        a = jnp.where(m, a * a_s, a)
    out = []
    for g in range(groups):
        h = a[g] * carry + b[g]
        out.append(h)
        carry = h[SUBLANES - 1:SUBLANES]
    return jnp.concatenate(out, axis=0), carry


def _scan_backward(a, b, carry):
    n, cols = a.shape
    groups = n // SUBLANES
    a = a.reshape(groups, SUBLANES, cols)
    b = b.reshape(groups, SUBLANES, cols)
    sub = lax.broadcasted_iota(jnp.int32, a.shape, 1)
    for s in (1, 2, 4):
        a_s = pltpu.roll(a, SUBLANES - s, 1)
        b_s = pltpu.roll(b, SUBLANES - s, 1)
        m = sub < SUBLANES - s
        b = jnp.where(m, a * b_s + b, b)
        a = jnp.where(m, a * a_s, a)
    out = [None] * groups
    for g in reversed(range(groups)):
        h = a[g] * carry + b[g]
        out[g] = h
        carry = h[0:1]
    return jnp.concatenate(out, axis=0), carry


def _softplus_neg(lam):
    e = jnp.exp(-jnp.abs(lam))
    u = 1.0 + e
    log1p_e = jnp.where(u == 1.0, e, jnp.log(u) * (e / jnp.where(u == 1.0, 1.0, u - 1.0)))
    return jnp.maximum(-lam, 0.0) + log1p_e


def _lru_gates(xa, tail8, cw_ref, cb_ref, wr_ref, br_ref, wi_ref, bi_ref, lam_ref):
    cw = cw_ref[...]
    xc = cb_ref[...] + cw[0:1] * xa
    for k in range(1, CONV_WIDTH):
        xc = xc + cw[k:k + 1] * _shift_down(xa, tail8, k)
    xcb = xc.astype(BF16)
    pre_r, pre_i = [], []
    for h in range(HEADS):
        cols = slice(h * HEAD_DIM, (h + 1) * HEAD_DIM)
        pre_r.append(_dot(xcb[:, cols], wr_ref[h]))
        pre_i.append(_dot(xcb[:, cols], wi_ref[h]))
    r = jax.nn.sigmoid(jnp.concatenate(pre_r, axis=1) + br_ref[...])
    ig = jax.nn.sigmoid(jnp.concatenate(pre_i, axis=1) + bi_ref[...])
    _, a, mult = _decay(r, lam_ref)
    return xc, r, ig, a, mult


def _decay(r, lam_ref):
    sp = _softplus_neg(lam_ref[...])
    log_a = ((-LRU_C) * sp) * r
    a = jnp.exp(log_a)
    th = jnp.tanh(log_a)
    return sp, a, jnp.sqrt((-2.0 * th) / (1.0 - th))


class _Job:
    def __init__(self, inputs, out_shape, n_sem, copies, aliases=None, n_local=0):
        self.inputs, self.out_shape, self.n_sem, self.copies = list(inputs), list(out_shape), n_sem, copies
        self.aliases, self.n_local = dict(aliases or {}), n_local


def _fused_call(body, jobs, *, name, grid, in_specs, out_specs, out_shape, scratch_shapes=(),
                input_output_aliases=None, compiler_params=None, n_prefetch=0, jobs_start_after=None):
    single = not isinstance(out_shape, (list, tuple))
    out_specs = [out_specs] if single else list(out_specs)
    out_shape = [out_shape] if single else list(out_shape)
    n_scr = len(scratch_shapes)
    in_specs, scratch_shapes = list(in_specs), list(scratch_shapes)
    n_in, n_out = len(in_specs), len(out_shape)
    aliases = dict(input_output_aliases or {})
    in_at, out_at = [], []
    for job in jobs:
        in_at.append(len(in_specs))
        out_at.append(len(out_shape))
        for i, o in job.aliases.items():
            aliases[n_prefetch + len(in_specs) + i] = len(out_shape) + o
        in_specs += [ANY] * len(job.inputs)
        out_specs += [ANY] * len(job.out_shape)
        out_shape += job.out_shape
        scratch_shapes += [pltpu.SemaphoreType.DMA((job.n_sem,)), pltpu.SemaphoreType.DMA((job.n_sem,)),
                           pltpu.SemaphoreType.DMA((max(job.n_local, 1),))]
    n_in_all, n_out_all = len(in_specs), len(out_shape)

    def full_body(*refs):
        prefetch, refs = refs[:n_prefetch], refs[n_prefetch:]
        ins, outs, scr = refs[:n_in_all], refs[n_in_all:n_in_all + n_out_all], refs[n_in_all + n_out_all:]

        def copies(q):
            job = jobs[q]
            return job.copies(ins[in_at[q]:in_at[q] + len(job.inputs)], outs[out_at[q]:out_at[q] + len(job.out_shape)],
                              *scr[n_scr + 3 * q:n_scr + 3 * q + 3])

        def start():
            for q in range(len(jobs)):
                sends, _, local = copies(q)
                for cp in local + sends:
                    cp.start()

        def finish():
            every = [copies(q) for q in range(len(jobs))]
            for _, arrivals, _ in every:
                for cp in arrivals:
                    cp.wait_recv()
            for sends, _, local in every:
                for cp in sends:
                    cp.wait_send()
                for cp in local:
                    cp.wait()

        if not grid:
            start()
            finish()
            return
        ids = [pl.program_id(a) for a in range(len(grid))]
        at_step = lambda step: functools.reduce(jnp.logical_and, [i == k for i, k in zip(ids, step)])
        if jobs and jobs_start_after is None:
            pl.when(at_step((0,) * len(grid)))(start)
        body(*prefetch, *ins[:n_in], *outs[:n_out], *scr[:n_scr])
        if jobs and jobs_start_after is not None:
            pl.when(at_step(jobs_start_after))(start)
        if jobs:
            pl.when(functools.reduce(jnp.logical_and, [i == g - 1 for i, g in zip(ids, grid)]))(finish)

    if n_prefetch:
        layout = dict(grid_spec=pltpu.PrefetchScalarGridSpec(
            num_scalar_prefetch=n_prefetch, grid=grid, in_specs=in_specs, out_specs=out_specs,
            scratch_shapes=scratch_shapes))
    else:
        layout = dict(grid=grid, in_specs=in_specs, out_specs=out_specs, scratch_shapes=scratch_shapes)
    call = pl.pallas_call(
        full_body, name=name, out_shape=out_shape, input_output_aliases=aliases, compiler_params=compiler_params,
        **layout)

    def run(*args):
        res = call(*args, *[a for job in jobs for a in job.inputs])
        mine = res[0] if single else list(res[:n_out])
        return mine, [list(res[at:at + len(job.out_shape)]) for at, job in zip(out_at, jobs)]

    return run


def _fwd_in(x, g1, shards, order, jobs=()):
    t = x.shape[0]
    rows_per_step = min(IN_TILE, t)
    n_tiles = t // rows_per_step
    n = len(shards)
    halves = [s.shape[0] // 2 for s in shards]

    def body(order_ref, x_ref, g_ref, *refs):
        del order_ref
        ins, (z_ref, n_ref), outs = refs[:n], refs[n:n + 2], refs[n + 2:2 * n + 2]
        wbuf, nbuf, send, recv, local = refs[2 * n + 2:]
        s, i = pl.program_id(0), pl.program_id(1)
        x_, y_, c, chips = _place()
        k_me = _chip_index(x_, y_)

        def block(w, chip, pc):
            return outs[w].at[_chip_index(*chip), pl.ds(pc * halves[w], halves[w]), :]

        def over_ici(w, j, landing):
            return pltpu.make_async_remote_copy(
                src_ref=ins[w].at[pl.ds(c * halves[w], halves[w]), :],
                dst_ref=block(w, chips[j] if landing else (x_, y_), c), send_sem=send.at[6 * w + j],
                recv_sem=recv.at[6 * w + j], device_id=(*chips[j], c), device_id_type=MESH)

        def to_sibling(w, j, landing):
            blk = block(w, chips[j], 1 - c if landing else c)
            return pltpu.make_async_remote_copy(
                src_ref=blk, dst_ref=blk, send_sem=send.at[6 * w + 3 + j], recv_sem=recv.at[6 * w + 3 + j],
                device_id=(x_, y_, 1 - c), device_id_type=MESH)

        own = [pltpu.make_async_copy(wbuf, outs[0].at[k_me], local.at[0])]
        own += [pltpu.make_async_copy(ins[w], outs[w].at[k_me], local.at[w]) for w in range(1, n)]

        @pl.when((s == 0) & (i == 0))
        def _():
            for j in range(2):
                for w in range(n):
                    over_ici(w, j, False).start()
            load = pltpu.make_async_copy(ins[0], wbuf, local.at[n])
            load.start()
            load.wait()
            for cp in own:
                cp.start()

        for j in range(N_CHIPS - 1):
            @pl.when((s == j + 1) & (i == 0))
            def _(j=j):
                for w in range(n):
                    over_ici(w, j, True).wait_recv()
                for w in range(n):
                    to_sibling(w, j, False).start()
                if j == 0:
                    for w in range(n):
                        over_ici(w, 2, False).start()
                    own[0].wait()
                for w in range(n):
                    to_sibling(w, j, True).wait_recv()
                load = pltpu.make_async_copy(outs[0].at[_chip_index(*chips[j])], wbuf, local.at[n])
                load.start()
                load.wait()

        rows = pl.ds(pl.multiple_of(i * rows_per_step, rows_per_step), rows_per_step)

        @pl.when(s == 0)
        def _():
            xhat, _ = _rms(x_ref[...])
            nrm = (xhat * g_ref[...]).astype(BF16)
            nbuf[rows, :] = nrm
            n_ref[...] = nrm

        z_ref[...] = _dot(nbuf[rows, :], wbuf[...])

        @pl.when((s == N_CHIPS - 1) & (i == n_tiles - 1))
        def _():
            for j in range(N_CHIPS - 1):
                for w in range(n):
                    over_ici(w, j, False).wait_send()
                    to_sibling(w, j, False).wait_send()
            for cp in own[1:]:
                cp.wait()

    once = lambda s, i, order: (jnp.where(s == 0, i, n_tiles - 1), 0)
    (z, n1, *stacked), job_outs = _fused_call(
        body, jobs, name="fwd_in", grid=(N_CHIPS, n_tiles), n_prefetch=1,
        in_specs=[pl.BlockSpec((rows_per_step, D_MODEL), once), _const((1, D_MODEL))] + [ANY] * n,
        out_specs=[pl.BlockSpec((rows_per_step, IN_SHARD), lambda s, i, order: (i, order[s])),
                   pl.BlockSpec((rows_per_step, D_MODEL), once)] + [ANY] * n,
        out_shape=[jax.ShapeDtypeStruct((t, D_IN), F32), jax.ShapeDtypeStruct((t, D_MODEL), BF16)]
        + [jax.ShapeDtypeStruct((N_CHIPS,) + s.shape, s.dtype) for s in shards],
        scratch_shapes=[pltpu.VMEM(shards[0].shape, BF16), pltpu.VMEM((t, D_MODEL), BF16),
                        pltpu.SemaphoreType.DMA((6 * n,)),
                        pltpu.SemaphoreType.DMA((6 * n,)), pltpu.SemaphoreType.DMA((n + 1,))],
        compiler_params=_params(2), jobs_start_after=(1, 0),
    )(order, x, g1, *shards)
    return (z, n1, stacked), job_outs


def _fwd_lru(z, conv_w, conv_b, wr, br, wi, bi, lam, jobs=()):
    t = z.shape[0]

    def body(xa_ref, ga_ref, cw_ref, cb_ref, wr_ref, br_ref, wi_ref, bi_ref, lam_ref, ya_ref, h_ref, xc_ref, r_ref,
             ig_ref, tail_ref, carry_ref):
        @pl.when(pl.program_id(0) == 0)
        def _():
            tail_ref[...] = jnp.zeros_like(tail_ref)
            carry_ref[...] = jnp.zeros_like(carry_ref)

        xa = xa_ref[...]
        xc, r, ig, a, mult = _lru_gates(xa, tail_ref[...], cw_ref, cb_ref, wr_ref, br_ref, wi_ref, bi_ref, lam_ref)
        tail_ref[...] = xa[SEQ_TILE - SUBLANES:]
        xc_ref[...], r_ref[...], ig_ref[...] = xc, r, ig
        h, carry = _scan_forward(a, xc * ig * mult, carry_ref[...])
        carry_ref[...] = carry
        h_ref[...] = h
        ya_ref[...] = (h * _gelu(ga_ref[...])).astype(BF16)

    tile = lambda j: pl.BlockSpec((SEQ_TILE, D_MODEL), lambda i: (i, j))
    return _fused_call(
        body, jobs, name="fwd_lru", grid=(t // SEQ_TILE,),
        in_specs=[tile(0), tile(1), _const((CONV_WIDTH, D_MODEL)), _const((1, D_MODEL)),
                  _resident((HEADS, HEAD_DIM, HEAD_DIM)), _const((1, D_MODEL)),
                  _resident((HEADS, HEAD_DIM, HEAD_DIM)), _const((1, D_MODEL)), _const((1, D_MODEL))],
        out_specs=[tile(0)] * 5,
        out_shape=[jax.ShapeDtypeStruct((t, D_MODEL), BF16)] + [jax.ShapeDtypeStruct((t, D_MODEL), F32)] * 4,
        scratch_shapes=[pltpu.VMEM((SUBLANES, D_MODEL), F32), pltpu.VMEM((1, D_MODEL), F32)],
        compiler_params=_params(),
    )(z, z, conv_w, conv_b, wr, br, wi, bi, lam)


def _sgu_forward_parts(ub, vb, lg_ref, lb_ref):
    u, du = _gelu_and_grad(ub)
    vg, dvg = _gelu_and_grad(vb)
    mu = jnp.mean(vg, axis=-1, keepdims=True)
    d = vg - mu
    rstd = lax.rsqrt(jnp.mean(d * d, axis=-1, keepdims=True) + LN_EPS)
    vhat = d * rstd
    vn = (vhat * lg_ref[...] + lb_ref[...]).astype(BF16)
    return u, du, dvg, rstd, vhat, vn


def _causal_mask():
    rows = lax.broadcasted_iota(jnp.int32, (CHUNK, CHUNK), 0)
    cols = lax.broadcasted_iota(jnp.int32, (CHUNK, CHUNK), 1)
    return rows >= cols


def _fwd_sgu(z, ln_g, ln_b, w_s, bias_full, jobs=()):
    t = z.shape[0]

    def body(ub_ref, vb_ref, lg_ref, lb_ref, ws_ref, bias_ref, yb_ref):
        u, _, _, _, _, vn = _sgu_forward_parts(ub_ref[...], vb_ref[...], lg_ref, lb_ref)
        mask = _causal_mask()
        wm = [jnp.where(mask, ws_ref[g], 0.0).astype(BF16) for g in range(GROUPS)]
        for c in range(SEQ_TILE // CHUNK):
            rows = slice(c * CHUNK, (c + 1) * CHUNK)
            for g in range(GROUPS):
                cols = slice(g * GROUP_DIM, (g + 1) * GROUP_DIM)
                sp = _dot(wm[g], vn[rows, cols]) + bias_ref[:, cols]
                yb_ref[rows, cols] = (u[rows, cols] * sp).astype(BF16)

    tile = lambda j: pl.BlockSpec((SEQ_TILE, D_MODEL), lambda i: (i, j))
    return _fused_call(
        body, jobs, name="fwd_sgu", grid=(t // SEQ_TILE,),
        in_specs=[tile(2), tile(3), _const((1, D_MODEL)), _const((1, D_MODEL)),
                  _const((GROUPS, CHUNK, CHUNK)), _const((CHUNK, D_MODEL))],
        out_specs=tile(0),
        out_shape=jax.ShapeDtypeStruct((t, D_MODEL), BF16),
        compiler_params=_params(),
    )(z, z, ln_g, ln_b, w_s, bias_full)


def _fwd_merge(ya, yb, z, x, w_oa, w_ob, w_out, g2, jobs=()):
    t = x.shape[0]

    def body(ya_ref, yb_ref, m_ref, x_ref, woa_ref, wob_ref, wout_ref, g_ref, pa_ref, pb_ref, h1_ref, n2_ref):
        pa = _dot(ya_ref[...], woa_ref[...])
        pb = _dot(yb_ref[...], wob_ref[...])
        pa_ref[...] = pa
        pb_ref[...] = pb
        merged = jax.nn.sigmoid(m_ref[:, :D_MODEL]) * pa + jax.nn.sigmoid(m_ref[:, D_MODEL:]) * pb
        h1 = x_ref[...] + _dot(merged.astype(BF16), wout_ref[...])
        h1_ref[...] = h1
        xhat, _ = _rms(h1)
        n2_ref[...] = (xhat * g_ref[...]).astype(BF16)

    tile = pl.BlockSpec((MM_TILE, D_MODEL), lambda i: (i, 0))
    sq = _resident((D_MODEL, D_MODEL))
    return _fused_call(
        body, jobs, name="fwd_merge", grid=(t // MM_TILE,),
        in_specs=[tile, tile, pl.BlockSpec((MM_TILE, 2 * D_MODEL), lambda i: (i, 2)), tile, sq, sq, sq,
                  _const((1, D_MODEL))],
        out_specs=[tile, tile, tile, tile],
        out_shape=[jax.ShapeDtypeStruct((t, D_MODEL), F32)] * 3 + [jax.ShapeDtypeStruct((t, D_MODEL), BF16)],
        compiler_params=_params(),
    )(ya, yb, z, x, w_oa, w_ob, w_out, g2)


def _mlp(n2, h1, target, w_up_st, w_down, g2, g3, jobs=()):
    t = n2.shape[0]

    def body(n2_ref, h1_ref, tgt_ref, wup_ref, wdown_ref, g2_ref, g3_ref, act_ref, dup_ref, dh2b_ref, dh1_ref,
             loss_ref, dg3_ref, dg2_ref, relu_ref):
        @pl.when(pl.program_id(0) == 0)
        def _():
            for ref in (loss_ref, dg3_ref, dg2_ref):
                ref[...] = jnp.zeros_like(ref)

        n2 = n2_ref[...]
        h1 = h1_ref[...]
        h2 = h1
        for k in range(N_CHIPS):
            cols = slice(k * D_MODEL, (k + 1) * D_MODEL)
            r = jnp.maximum(_dot(n2, wup_ref[k]), 0.0)
            relu_ref[:, cols] = r
            act = (r * r).astype(BF16)
            act_ref[:, cols] = act
            h2 = h2 + _dot(act, wdown_ref[cols, :])
        xhat, r3 = _rms(h2)
        diff = xhat * g3_ref[...] - tgt_ref[...]
        sq = jnp.sum(diff * diff, axis=1, keepdims=True)
        loss_ref[...] = loss_ref[...] + (0.5 / D_MODEL) * jnp.sum(sq, axis=0, keepdims=True)
        dy = diff * (1.0 / D_MODEL)
        dg3_ref[...] = dg3_ref[...] + _col_sum(dy * xhat)
        dh2 = _rms_bwd(dy * g3_ref[...], xhat, r3)
        dh2b = dh2.astype(BF16)
        dh2b_ref[...] = dh2b
        dn2 = jnp.zeros((SEQ_TILE, D_MODEL), F32)
        for k in range(N_CHIPS):
            cols = slice(k * D_MODEL, (k + 1) * D_MODEL)
            dup = (_dot_nt(dh2b, wdown_ref[cols, :]) * (2.0 * relu_ref[:, cols])).astype(BF16)
            dup_ref[:, cols] = dup
            dn2 = dn2 + _dot_nt(dup, wup_ref[k])
        xhat, r2 = _rms(h1)
        dg2_ref[...] = dg2_ref[...] + _col_sum(dn2 * xhat)
        dh1_ref[...] = dh2 + _rms_bwd(dn2 * g2_ref[...], xhat, r2)

    tile = pl.BlockSpec((SEQ_TILE, D_MODEL), lambda i: (i, 0))
    wide = pl.BlockSpec((SEQ_TILE, D_FF), lambda i: (i, 0))
    vec = _const((1, D_MODEL))
    vec_shape = jax.ShapeDtypeStruct((1, D_MODEL), F32)
    return _fused_call(
        body, jobs, name="mlp", grid=(t // SEQ_TILE,),
        in_specs=[tile, tile, tile, _resident((N_CHIPS, D_MODEL, D_MODEL)), _resident((D_FF, D_MODEL)), vec, vec],
        out_specs=[wide, wide, tile, tile, _const((SUBLANES, 128)), vec, vec],
        out_shape=[jax.ShapeDtypeStruct((t, D_FF), BF16), jax.ShapeDtypeStruct((t, D_FF), BF16),
                   jax.ShapeDtypeStruct((t, D_MODEL), BF16), jax.ShapeDtypeStruct((t, D_MODEL), F32),
                   jax.ShapeDtypeStruct((SUBLANES, 128), F32), vec_shape, vec_shape],
        scratch_shapes=[pltpu.VMEM((SEQ_TILE, D_FF), F32)],
        compiler_params=_params(),
    )(n2, h1, target, w_up_st, w_down, g2, g3)


def _bwd_mix(dh1, pa, pb, z, h, xc, r, ig, w_oa, w_ob, w_out, ln_g, ln_b, w_s, bias_full, conv_w, wr, wi, lam, jobs=()):
    t = dh1.shape[0]
    n_tiles = t // SEQ_TILE
    per_tile = SEQ_TILE // SUBLANES

    def merge_part(dh1_ref, pa_ref, pb_ref, m_ref, woa_ref, wob_ref, wout_ref, dz_ref, dya_ref, dyb_ref, mg_ref,
                   dpa_ref, dpb_ref, dh1b_ref):
        dh1b = dh1_ref[...].astype(BF16)
        dh1b_ref[...] = dh1b
        dm = _dot_nt(dh1b, wout_ref[...])
        pa = pa_ref[...]
        pb = pb_ref[...]
        sa = jax.nn.sigmoid(m_ref[:, :D_MODEL])
        sb = jax.nn.sigmoid(m_ref[:, D_MODEL:])
        mg_ref[...] = (sa * pa + sb * pb).astype(BF16)
        dz_ref[:, :D_MODEL] = (dm * pa * sa * (1.0 - sa)).astype(BF16)
        dz_ref[:, D_MODEL:] = (dm * pb * sb * (1.0 - sb)).astype(BF16)
        dpa = (dm * sa).astype(BF16)
        dpb = (dm * sb).astype(BF16)
        dpa_ref[...] = dpa
        dpb_ref[...] = dpb
        dya_ref[...] = _dot_nt(dpa, woa_ref[...])
        dyb_ref[...] = _dot_nt(dpb, wob_ref[...])

    def sgu_part(dyb_ref, ub_ref, vb_ref, lg_ref, lb_ref, ws_ref, bias_ref, dz_ref, dlg_ref, dlb_ref, dws_ref, dbs_ref,
                 dvn_ref, dsp_acc):
        i = pl.program_id(0)

        @pl.when(i == 0)
        def _():
            dlg_ref[...] = jnp.zeros_like(dlg_ref)
            dlb_ref[...] = jnp.zeros_like(dlb_ref)
            dws_ref[...] = jnp.zeros_like(dws_ref)
            dsp_acc[...] = jnp.zeros_like(dsp_acc)

        u, du, dvg, rstd, vhat, vn = _sgu_forward_parts(ub_ref[...], vb_ref[...], lg_ref, lb_ref)
        dyb = dyb_ref[...]
        mask = _causal_mask()
        wm = [jnp.where(mask, ws_ref[g], 0.0).astype(BF16) for g in range(GROUPS)]
        for c in range(SEQ_TILE // CHUNK):
            rows = slice(c * CHUNK, (c + 1) * CHUNK)
            for g in range(GROUPS):
                cols = slice(g * GROUP_DIM, (g + 1) * GROUP_DIM)
                vn_blk = vn[rows, cols]
                sp = _dot(wm[g], vn_blk) + bias_ref[:, cols]
                dyb_blk = dyb[rows, cols]
                dz_ref[rows, cols] = (dyb_blk * sp * du[rows, cols]).astype(BF16)
                dsp = dyb_blk * u[rows, cols]
                dsp_acc[:, cols] = dsp_acc[:, cols] + dsp
                dspb = dsp.astype(BF16)
                dvn_ref[rows, cols] = _dot_tn(wm[g], dspb)
                wcols = slice(g * CHUNK, (g + 1) * CHUNK)
                dws_ref[:, wcols] = dws_ref[:, wcols] + jnp.where(mask, _dot_nt(dspb, vn_blk), 0.0)
        dvn = dvn_ref[...]
        dlg_ref[...] = dlg_ref[...] + _col_sum(dvn * vhat)
        dlb_ref[...] = dlb_ref[...] + _col_sum(dvn)
        dvhat = dvn * lg_ref[...]
        dvgel = rstd * (dvhat - jnp.mean(dvhat, axis=-1, keepdims=True)
                        - vhat * jnp.mean(dvhat * vhat, axis=-1, keepdims=True))
        dz_ref[:, D_MODEL:] = (dvgel * dvg).astype(BF16)

        @pl.when(i == n_tiles - 1)
        def _():
            lane = lax.broadcasted_iota(jnp.int32, (CHUNK, 128), 1)
            out = jnp.zeros((CHUNK, 128), F32)
            for g in range(GROUPS):
                s = jnp.sum(dsp_acc[:, g * GROUP_DIM:(g + 1) * GROUP_DIM], axis=1, keepdims=True)
                out = out + jnp.where(lane == g, s, 0.0)
            dbs_ref[...] = out

    def lru_part(dya_ref, xa_ref, ga_ref, h_ref, h_prev_ref, xc_ref, r_ref, ig_ref, cw_ref, wr_ref, wi_ref, lam_ref,
                 dz_ref, dcw_ref, dcb_ref, dwr_ref, dbr_ref, dwi_ref, dbi_ref, dlam_ref, lam_carry, dxc_head):
        i = pl.program_id(0)

        @pl.when(i == 0)
        def _():
            for ref in (dcw_ref, dcb_ref, dwr_ref, dbr_ref, dwi_ref, dbi_ref, dlam_ref, lam_carry, dxc_head):
                ref[...] = jnp.zeros_like(ref)

        first_tile = i == n_tiles - 1
        h_tail = jnp.where(first_tile, 0.0, h_prev_ref[...])
        xc, r, ig = xc_ref[...], r_ref[...], ig_ref[...]
        xcb = xc.astype(BF16)
        sp, a, mult = _decay(r, lam_ref)
        h = h_ref[...]
        h_prev = _shift_down(h, h_tail, 1)
        dya = dya_ref[...]
        gg, dgg = _gelu_and_grad(ga_ref[...])
        dz_ref[:, D_MODEL:] = (dya * h * dgg).astype(BF16)
        ones = jnp.ones((SUBLANES, D_MODEL), F32)
        lam_t, lam_first = _scan_backward(_shift_up(a, ones, 1), dya * gg, lam_carry[...])
        lam_carry[...] = a[0:1] * lam_first
        dmult = lam_t * xc * ig
        dla = lam_t * h_prev * a - dmult * (a * a) / mult
        dr = dla * ((-LRU_C) * sp)
        dlam_ref[...] = dlam_ref[...] + _col_sum(dla * r) * (LRU_C * jax.nn.sigmoid(-lam_ref[...]))
        dpr = dr * r * (1.0 - r)
        dpi = lam_t * xc * mult * ig * (1.0 - ig)
        dbr_ref[...] = dbr_ref[...] + _col_sum(dpr)
        dbi_ref[...] = dbi_ref[...] + _col_sum(dpi)
        dprb = dpr.astype(BF16)
        dpib = dpi.astype(BF16)
        dxc_gate = []
        for hd in range(HEADS):
            cols = slice(hd * HEAD_DIM, (hd + 1) * HEAD_DIM)
            dxc_gate.append(_dot_nt(dprb[:, cols], wr_ref[hd]) + _dot_nt(dpib[:, cols], wi_ref[hd]))
            dwr_ref[hd] = dwr_ref[hd] + _dot_tn(xcb[:, cols], dprb[:, cols])
            dwi_ref[hd] = dwi_ref[hd] + _dot_tn(xcb[:, cols], dpib[:, cols])
        dxc = lam_t * ig * mult + jnp.concatenate(dxc_gate, axis=1)
        dcb_ref[...] = dcb_ref[...] + _col_sum(dxc)
        cw = cw_ref[...]
        head = dxc_head[...]
        xa = xa_ref[...]
        dxa = cw[0:1] * dxc
        dcw_ref[0:1, :] = dcw_ref[0:1, :] + _col_sum(dxc * xa)
        for k in range(1, CONV_WIDTH):
            dxc_k = _shift_up(dxc, head, k)
            dxa = dxa + cw[k:k + 1] * dxc_k
            dcw_ref[k:k + 1, :] = dcw_ref[k:k + 1, :] + _col_sum(dxc_k * xa)
        dxc_head[...] = dxc[0:SUBLANES]
        dz_ref[:, :D_MODEL] = dxa.astype(BF16)

    def body(dh1_ref, pa_ref, pb_ref, z_ref, h_ref, h_prev_ref, xc_ref, r_ref, ig_ref, woa_ref, wob_ref, wout_ref,
             lg_ref, lb_ref, ws_ref, bias_ref, cw_ref, wr_ref, wi_ref, lam_ref, dz_ref, mg_ref, dpa_ref, dpb_ref,
             dh1b_ref, dlg_ref, dlb_ref, dws_ref, dbs_ref, dcw_ref, dcb_ref, dwr_ref, dbr_ref, dwi_ref, dbi_ref,
             dlam_ref, dya_ref, dyb_ref, dvn_ref, dsp_acc, lam_carry, dxc_head):
        def cols(ref, first, count):
            return ref.at[:, pl.ds(first * D_MODEL, count * D_MODEL)]

        merge_part(dh1_ref, pa_ref, pb_ref, cols(z_ref, 4, 2), woa_ref, wob_ref, wout_ref, cols(dz_ref, 4, 2), dya_ref,
                   dyb_ref, mg_ref, dpa_ref, dpb_ref, dh1b_ref)
        sgu_part(dyb_ref, cols(z_ref, 2, 1), cols(z_ref, 3, 1), lg_ref, lb_ref, ws_ref, bias_ref, cols(dz_ref, 2, 2),
                 dlg_ref, dlb_ref, dws_ref, dbs_ref, dvn_ref, dsp_acc)
        lru_part(dya_ref, cols(z_ref, 0, 1), cols(z_ref, 1, 1), h_ref, h_prev_ref, xc_ref, r_ref, ig_ref, cw_ref, wr_ref,
                 wi_ref, lam_ref, cols(dz_ref, 0, 2), dcw_ref, dcb_ref, dwr_ref, dbr_ref, dwi_ref, dbi_ref, dlam_ref,
                 lam_carry, dxc_head)

    rev = lambda i: n_tiles - 1 - i
    tile = pl.BlockSpec((SEQ_TILE, D_MODEL), lambda i: (rev(i), 0))
    row = pl.BlockSpec((SEQ_TILE, D_IN), lambda i: (rev(i), 0))
    prev8 = pl.BlockSpec((SUBLANES, D_MODEL), lambda i: (jnp.maximum(rev(i) * per_tile - 1, 0), 0))
    vec = _const((1, D_MODEL))
    sq = _resident((D_MODEL, D_MODEL))
    gate_w = _resident((HEADS, HEAD_DIM, HEAD_DIM))
    gate_acc = _const((HEADS, HEAD_DIM, HEAD_DIM))
    vec_shape = jax.ShapeDtypeStruct((1, D_MODEL), F32)
    gate_shape = jax.ShapeDtypeStruct((HEADS, HEAD_DIM, HEAD_DIM), F32)
    act_bf = jax.ShapeDtypeStruct((t, D_MODEL), BF16)
    return _fused_call(
        body, jobs, name="bwd_mix", grid=(n_tiles,),
        in_specs=[tile, tile, tile, row, tile, prev8, tile, tile, tile, sq, sq, sq, vec, vec,
                  _const((GROUPS, CHUNK, CHUNK)), _const((CHUNK, D_MODEL)), _const((CONV_WIDTH, D_MODEL)), gate_w, gate_w,
                  vec],
        out_specs=[row, tile, tile, tile, tile, vec, vec, _const((CHUNK, GROUPS * CHUNK)), _const((CHUNK, 128)),
                   _const((SUBLANES, D_MODEL)), vec, gate_acc, vec, gate_acc, vec, vec],
        out_shape=[jax.ShapeDtypeStruct((t, D_IN), BF16), act_bf, act_bf, act_bf, act_bf, vec_shape, vec_shape,
                   jax.ShapeDtypeStruct((CHUNK, GROUPS * CHUNK), F32), jax.ShapeDtypeStruct((CHUNK, 128), F32),
                   jax.ShapeDtypeStruct((SUBLANES, D_MODEL), F32), vec_shape, gate_shape, vec_shape, gate_shape,
                   vec_shape, vec_shape],
        scratch_shapes=[pltpu.VMEM((SEQ_TILE, D_MODEL), F32), pltpu.VMEM((SEQ_TILE, D_MODEL), F32),
                        pltpu.VMEM((SEQ_TILE, D_MODEL), F32), pltpu.VMEM((CHUNK, D_MODEL), F32),
                        pltpu.VMEM((1, D_MODEL), F32), pltpu.VMEM((SUBLANES, D_MODEL), F32)],
        compiler_params=_params(),
    )(dh1, pa, pb, z, h, h, xc, r, ig, w_oa, w_ob, w_out, ln_g, ln_b, w_s, bias_full, conv_w, wr, wi, lam)


def _bwd_in(dz, x, dh1, w_in_st, g1, jobs=()):
    t = x.shape[0]

    def body(dz_ref, x_ref, dh1_ref, w_ref, g_ref, dx_ref, dg1_ref):
        @pl.when(pl.program_id(0) == 0)
        def _():
            dg1_ref[...] = jnp.zeros_like(dg1_ref)

        dn1 = jnp.zeros((MM_TILE, D_MODEL), F32)
        for k in range(N_CHIPS):
            dn1 = dn1 + _dot_nt(dz_ref[:, k * IN_SHARD:(k + 1) * IN_SHARD], w_ref[k])
        xhat, r1 = _rms(x_ref[...])
        dg1_ref[...] = dg1_ref[...] + _col_sum(dn1 * xhat)
        dx_ref[...] = dh1_ref[...] + _rms_bwd(dn1 * g_ref[...], xhat, r1)

    tile = pl.BlockSpec((MM_TILE, D_MODEL), lambda i: (i, 0))
    return _fused_call(
        body, jobs, name="bwd_in", grid=(t // MM_TILE,),
        in_specs=[pl.BlockSpec((MM_TILE, D_IN), lambda i: (i, 0)), tile, tile,
                  _resident((N_CHIPS, D_MODEL, IN_SHARD)), _const((1, D_MODEL))],
        out_specs=[tile, _const((1, D_MODEL))],
        out_shape=[jax.ShapeDtypeStruct((t, D_MODEL), F32), jax.ShapeDtypeStruct((1, D_MODEL), F32)],
        compiler_params=_params(),
    )(dz, x, dh1, w_in_st, g1)


def _weight_grad(name, a, b, n_blocks, a_varies, b_varies, width, jobs=()):
    t = a.shape[0]
    rows = min(DW_TILE, t)
    n_t = t // rows

    def body(a_ref, b_ref, o_ref, acc_ref):
        s = pl.program_id(1)
        part = _dot_tn(a_ref[...], b_ref[...])

        @pl.when(s == 0)
        def _():
            acc_ref[...] = part

        @pl.when(s > 0)
        def _():
            acc_ref[...] = acc_ref[...] + part

        @pl.when(s == n_t - 1)
        def _():
            o_ref[...] = acc_ref[...].astype(BF16)

    return _fused_call(
        body, jobs, name=name, grid=(n_blocks, n_t),
        in_specs=[pl.BlockSpec((rows, D_MODEL), (lambda j, s: (s, j)) if a_varies else (lambda j, s: (s, 0))),
                  pl.BlockSpec((rows, width), (lambda j, s: (s, j)) if b_varies else (lambda j, s: (s, 0)))],
        out_specs=pl.BlockSpec((None, D_MODEL, width), lambda j, s: (j, 0, 0)),
        out_shape=jax.ShapeDtypeStruct((n_blocks, D_MODEL, width), BF16),
        scratch_shapes=[pltpu.VMEM((D_MODEL, width), F32)],
        compiler_params=_params(2),
    )(a, b)


def _place():
    x, y, c = lax.axis_index("x"), lax.axis_index("y"), lax.axis_index("c")
    other_chips = [(1 - x, y), (x, 1 - y), (1 - x, 1 - y)]
    return x, y, c, other_chips


def _chip_index(px, py):
    return 2 * px + py


ANY = pl.BlockSpec(memory_space=pl.ANY)


def _comm_call(name, jobs):
    return _fused_call(None, jobs, name=name, grid=(), in_specs=[], out_specs=[], out_shape=[])()[1]


def _near_far(x, y, c):
    return (x ^ (1 - c), y ^ c), (x ^ c, y ^ (1 - c))


def _gather_near_job(shards):
    n = len(shards)
    halves = [s.shape[0] // 2 for s in shards]

    def copies(ins, outs, send, recv, local):
        x, y, c, _ = _place()
        near, _ = _near_far(x, y, c)

        def block(w, chip, pc):
            return outs[w].at[_chip_index(*chip), pl.ds(pc * halves[w], halves[w]), :]

        def copy(w, k, chip, pc, to, src=None):
            return pltpu.make_async_remote_copy(
                src_ref=block(w, chip, pc) if src is None else src, dst_ref=block(w, chip, pc),
                send_sem=send.at[2 * w + k], recv_sem=recv.at[2 * w + k], device_id=to, device_id_type=MESH)

        sends, arrivals, own = [], [], []
        for w in range(n):
            src = ins[w].at[pl.ds(c * halves[w], halves[w]), :]
            own.append(pltpu.make_async_copy(src, block(w, (x, y), c), local.at[w]))
            sends += [copy(w, 0, (x, y), c, (*near, c), src), copy(w, 1, (x, y), c, (x, y, 1 - c), src)]
            arrivals += [copy(w, 0, near, c, (x, y, c)), copy(w, 1, (x, y), 1 - c, (x, y, c))]
        return sends, arrivals, own

    return _Job(shards, [jax.ShapeDtypeStruct((N_CHIPS,) + s.shape, s.dtype) for s in shards], 2 * n, copies,
                n_local=n)


def _gather_far_job(stacked):
    n = len(stacked)
    halves = [s.shape[1] // 2 for s in stacked]

    def copies(ins, outs, send, recv, local):
        del ins, local
        x, y, c, _ = _place()
        near, far = _near_far(x, y, c)

        def copy(w, k, chip):
            blk = outs[w].at[_chip_index(*chip), pl.ds(c * halves[w], halves[w]), :]
            return pltpu.make_async_remote_copy(
                src_ref=blk, dst_ref=blk, send_sem=send.at[2 * w + k], recv_sem=recv.at[2 * w + k],
                device_id=(*far, c), device_id_type=MESH)

        sends = [copy(w, k, chip) for w in range(n) for k, chip in enumerate(((x, y), near))]
        arrivals = [copy(w, k, chip) for w in range(n) for k, chip in enumerate((far, (1 - x, 1 - y)))]
        return sends, arrivals, []

    return _Job(stacked, [jax.ShapeDtypeStruct(s.shape, s.dtype) for s in stacked], 2 * n, copies,
                aliases={w: w for w in range(n)})


def _gather_pass_job(stacked):
    n = len(stacked)
    halves = [s.shape[1] // 2 for s in stacked]

    def copies(ins, outs, send, recv, local):
        del ins, local
        x, y, c, chips = _place()

        def copy(w, j, chip, pc, to):
            blk = outs[w].at[_chip_index(*chip), pl.ds(pc * halves[w], halves[w]), :]
            return pltpu.make_async_remote_copy(
                src_ref=blk, dst_ref=blk, send_sem=send.at[3 * w + j], recv_sem=recv.at[3 * w + j], device_id=to,
                device_id_type=MESH)

        sends = [copy(w, j, chip, c, (x, y, 1 - c)) for w in range(n) for j, chip in enumerate(chips)]
        arrivals = [copy(w, j, chip, 1 - c, (x, y, c)) for w in range(n) for j, chip in enumerate(chips)]
        return sends, arrivals, []

    return _Job(stacked, [jax.ShapeDtypeStruct(s.shape, s.dtype) for s in stacked], 3 * n, copies,
                aliases={w: w for w in range(n)})


def _gather_small_job(block):
    def copies(ins, outs, send, recv, local):
        x, y, c, chips = _place()

        def copy(j, chip_from, to):
            return pltpu.make_async_remote_copy(
                src_ref=ins[0], dst_ref=outs[0].at[_chip_index(*chip_from)], send_sem=send.at[j],
                recv_sem=recv.at[j], device_id=to, device_id_type=MESH)

        own = [pltpu.make_async_copy(ins[0], outs[0].at[_chip_index(x, y)], local.at[0])]
        sends = [copy(j, (x, y), (*chip, c)) for j, chip in enumerate(chips)]
        arrivals = [copy(j, chip, (x, y, c)) for j, chip in enumerate(chips)]
        return sends, arrivals, own

    return _Job([block], [jax.ShapeDtypeStruct((N_CHIPS,) + block.shape, block.dtype)], 3, copies, n_local=1)


def _pair_send_job(grads):
    n = len(grads)
    halves = [g.shape[1] // 2 for g in grads]

    def copies(ins, outs, send, recv, local):
        del local
        x, y, c, _ = _place()
        sends = [pltpu.make_async_remote_copy(
            src_ref=ins[w].at[:, pl.ds((1 - c) * halves[w], halves[w]), :], dst_ref=outs[w], send_sem=send.at[w],
            recv_sem=recv.at[w], device_id=(x, y, 1 - c), device_id_type=MESH) for w in range(n)]
        return sends, sends, []

    return _Job(grads, [jax.ShapeDtypeStruct((N_CHIPS, h, g.shape[2]), g.dtype) for g, h in zip(grads, halves)], n,
                copies)


def _row_block(rows, limit=256):
    return min(rows, limit)


def _pair_add(name, core, mine, theirs):
    _, _, h, cols = mine.shape
    rb = _row_block(h, 512)

    def body(core_ref, a_ref, b_ref, o_ref):
        del core_ref
        o_ref[...] = (a_ref[...].astype(F32) + b_ref[...].astype(F32)).astype(BF16)

    return pl.pallas_call(
        body, name=name,
        grid_spec=pltpu.PrefetchScalarGridSpec(
            num_scalar_prefetch=1, grid=(N_CHIPS, h // rb),
            in_specs=[pl.BlockSpec((None, None, rb, cols), lambda k, r, core_ref: (k, core_ref[0], r, 0)),
                      pl.BlockSpec((None, rb, cols), lambda k, r, core_ref: (k, r, 0))],
            out_specs=pl.BlockSpec((None, rb, cols), lambda k, r, core_ref: (k, r, 0))),
        out_shape=jax.ShapeDtypeStruct(theirs.shape, BF16),
        compiler_params=_params(2),
    )(core, mine, theirs)


def _sequencer_chip_exchange(name, sums, collective_id):
    n = len(sums)
    ins = [jax.new_ref(s, memory_space=pltpu.MemorySpace.HBM) for s in sums]
    outs = [jax.empty_ref(jax.ShapeDtypeStruct((N_CHIPS - 1,) + s.shape[1:], s.dtype),
                          memory_space=pltpu.MemorySpace.HBM) for s in sums]

    @pl.kernel(mesh=plsc.ScalarSubcoreMesh(axis_name="sequencer", num_cores=1), name=name,
               scratch_types=(pltpu.SemaphoreType.DMA((3 * n,)), pltpu.SemaphoreType.DMA((3 * n,))),
               compiler_params=pltpu.CompilerParams(collective_id=collective_id))
    def launch(send, recv):
        _, _, c, chips = _place()
        barrier = pltpu.get_barrier_semaphore()
        for chip in chips:
            pl.semaphore_signal(barrier, inc=1, device_id=(*chip, c), device_id_type=MESH)
        pl.semaphore_wait(barrier, N_CHIPS - 1)
        copies = [pltpu.make_async_remote_copy(
            src_ref=ins[w].at[_chip_index(*chip)], dst_ref=outs[w].at[j], send_sem=send.at[3 * w + j],
            recv_sem=recv.at[3 * w + j], device_id=(*chip, c), device_id_type=MESH)
            for w in range(n) for j, chip in enumerate(chips)]
        for cp in copies:
            cp.start()
        for cp in copies:
            cp.wait()

    launch()
    return [ref[...] for ref in outs]


def _chip_exchange_job(sums):
    n = len(sums)

    def copies(ins, outs, send, recv, local):
        del local
        _, _, c, chips = _place()
        sends = [pltpu.make_async_remote_copy(
            src_ref=ins[w].at[_chip_index(*chip)], dst_ref=outs[w].at[j], send_sem=send.at[3 * w + j],
            recv_sem=recv.at[3 * w + j], device_id=(*chip, c), device_id_type=MESH)
            for w in range(n) for j, chip in enumerate(chips)]
        return sends, sends, []

    return _Job(sums, [jax.ShapeDtypeStruct((N_CHIPS - 1,) + s.shape[1:], s.dtype) for s in sums], 3 * n, copies)


def _chip_sum(name, place, mine, theirs):
    _, h, cols = mine.shape
    rb = _row_block(h, 512)

    def body(place_ref, p_ref, q_ref, o_ref):
        del place_ref
        acc = p_ref[...].astype(F32)
        for j in range(N_CHIPS - 1):
            acc = acc + q_ref[j].astype(F32)
        o_ref[...] = acc

    return pl.pallas_call(
        body, name=name,
        grid_spec=pltpu.PrefetchScalarGridSpec(
            num_scalar_prefetch=1, grid=(h // rb,),
            in_specs=[pl.BlockSpec((None, rb, cols), lambda r, place_ref: (place_ref[0], r, 0)),
                      pl.BlockSpec((N_CHIPS - 1, rb, cols), lambda r, place_ref: (0, r, 0))],
            out_specs=pl.BlockSpec((None, rb, cols), lambda r, place_ref: (place_ref[1], r, 0))),
        out_shape=jax.ShapeDtypeStruct((2, h, cols), F32),
        compiler_params=_params(),
    )(place, mine, theirs)


def _share_job(bufs):
    n = len(bufs)

    def copies(ins, outs, send, recv, local):
        del ins, local
        x, y, c, _ = _place()

        def copy(w, half):
            return pltpu.make_async_remote_copy(
                src_ref=outs[w].at[half], dst_ref=outs[w].at[half], send_sem=send.at[w], recv_sem=recv.at[w],
                device_id=(x, y, 1 - c), device_id_type=MESH)

        return [copy(w, c) for w in range(n)], [copy(w, 1 - c) for w in range(n)], []

    return _Job(bufs, [jax.ShapeDtypeStruct(b.shape, b.dtype) for b in bufs], n, copies,
                aliases={w: w for w in range(n)})


SMALL_ROWS = 24
ROW_G1, ROW_CW, ROW_CB, ROW_BR, ROW_BI, ROW_LAM, ROW_LG, ROW_LB, ROW_G2, ROW_G3, ROW_LOSS, ROW_BS = (
    0, 1, 5, 6, 7, 8, 9, 10, 11, 12, 13, 16)
N_DEV = 8


def _pack_small(dcw, dcb, dbr, dbi, dlam, dlg, dlb, dg2, dg3, loss, dbs):
    def body(dcw_ref, dcb_ref, dbr_ref, dbi_ref, dlam_ref, dlg_ref, dlb_ref, dg2_ref, dg3_ref, loss_ref, dbs_ref, out):
        out[...] = jnp.zeros((SMALL_ROWS, D_MODEL), F32)
        for row, ref in ((ROW_CB, dcb_ref), (ROW_BR, dbr_ref), (ROW_BI, dbi_ref), (ROW_LAM, dlam_ref),
                         (ROW_LG, dlg_ref), (ROW_LB, dlb_ref), (ROW_G2, dg2_ref), (ROW_G3, dg3_ref)):
            out[row:row + 1, :] = ref[...]
        out[ROW_CW:ROW_CW + CONV_WIDTH, :] = dcw_ref[0:CONV_WIDTH, :]
        out[ROW_LOSS:ROW_LOSS + 1, 0:128] = loss_ref[0:1, :]
        out[ROW_BS:ROW_BS + GROUPS, 0:128] = jnp.transpose(dbs_ref[...])[0:GROUPS, :]

    vm = pl.BlockSpec(memory_space=pltpu.VMEM)
    return pl.pallas_call(
        body, name="pack_small", in_specs=[vm] * 11, out_specs=vm,
        out_shape=jax.ShapeDtypeStruct((SMALL_ROWS, D_MODEL), F32),
    )(dcw, dcb, dbr, dbi, dlam, dlg, dlb, dg2, dg3, loss, dbs)


def _gather_all_job(blocks):
    n = len(blocks)
    flips = [(dx, dy, dc) for dx in (0, 1) for dy in (0, 1) for dc in (0, 1)][1:]

    def copies(ins, outs, send, recv, local):
        x, y, c, _ = _place()
        me = 4 * x + 2 * y + c
        sends, arrivals, own = [], [], []
        for w in range(n):
            own.append(pltpu.make_async_copy(ins[w], outs[w].at[me], local.at[w]))
            for k, (dx, dy, dc) in enumerate(flips):
                peer = (x ^ dx, y ^ dy, c ^ dc)
                sem = dict(send_sem=send.at[7 * w + k], recv_sem=recv.at[7 * w + k])
                sends.append(pltpu.make_async_remote_copy(
                    src_ref=ins[w], dst_ref=outs[w].at[me], device_id=peer, device_id_type=MESH, **sem))
                arrivals.append(pltpu.make_async_remote_copy(
                    src_ref=ins[w], dst_ref=outs[w].at[4 * peer[0] + 2 * peer[1] + peer[2]], device_id=peer,
                    device_id_type=MESH, **sem))
        return sends, arrivals, own

    return _Job(blocks, [jax.ShapeDtypeStruct((N_DEV,) + b.shape, b.dtype) for b in blocks], 7 * n, copies, n_local=n)


def _sum_small(vec_all, ws_all, dg1_all):
    def body(vec_ref, ws_ref, dg1_ref, vec_out, ws_out):
        vec, ws, dg1 = vec_ref[0], ws_ref[0], dg1_ref[0]
        for d in range(1, N_DEV):
            vec, ws, dg1 = vec + vec_ref[d], ws + ws_ref[d], dg1 + dg1_ref[d]
        vec_out[...] = vec
        vec_out[ROW_G1:ROW_G1 + 1, :] = dg1
        ws_out[...] = ws

    vm = pl.BlockSpec(memory_space=pltpu.VMEM)
    return pl.pallas_call(
        body, name="sum_small", in_specs=[vm] * 3, out_specs=[vm, vm],
        out_shape=[jax.ShapeDtypeStruct(vec_all.shape[1:], F32), jax.ShapeDtypeStruct(ws_all.shape[1:], F32)],
    )(vec_all, ws_all, dg1_all)


def _adamw_math(w, g, m, v):
    m = ADAM_B1 * m + (1.0 - ADAM_B1) * g
    v = ADAM_B2 * v + (1.0 - ADAM_B2) * (g * g)
    m_hat = m / (1.0 - ADAM_B1 ** ADAM_STEP)
    v_hat = v / (1.0 - ADAM_B2 ** ADAM_STEP)
    delta = (-ADAM_LR) * (m_hat / (jnp.sqrt(v_hat) + ADAM_EPS) + ADAM_WD * w)
    return delta, m, v


def _adamw(name, g, w, m, v, jobs=()):
    rows, cols = w.shape
    rb = _row_block(rows)

    def body(g_ref, w_ref, m_ref, v_ref, d_ref, nm_ref, nv_ref):
        d_ref[...], nm_ref[...], nv_ref[...] = _adamw_math(w_ref[...], g_ref[...], m_ref[...], v_ref[...])

    blk = pl.BlockSpec((rb, cols), lambda r: (r, 0))
    return _fused_call(
        body, jobs, name=name, grid=(rows // rb,), in_specs=[blk] * 4, out_specs=[blk] * 3,
        out_shape=[jax.ShapeDtypeStruct(w.shape, F32)] * 3, compiler_params=_params(),
    )(g, w, m, v)


def _adamw_small(grads, ws, ms, vs):
    n = len(grads)

    def body(*refs):
        g_refs, w_refs, m_refs, v_refs = refs[:n], refs[n:2 * n], refs[2 * n:3 * n], refs[3 * n:4 * n]
        outs = refs[4 * n:]
        for p in range(n):
            d, nm, nv = _adamw_math(w_refs[p][...], g_refs[p][...], m_refs[p][...], v_refs[p][...])
            outs[p][...] = d
            outs[n + p][...] = nm
            outs[2 * n + p][...] = nv

    vm = pl.BlockSpec(memory_space=pltpu.VMEM)
    shapes = [jax.ShapeDtypeStruct(w.shape, F32) for w in ws]
    out = pl.pallas_call(
        body, name="adamw_small", in_specs=[vm] * (4 * n), out_specs=[vm] * (3 * n), out_shape=shapes * 3,
    )(*grads, *ws, *ms, *vs)
    return out[:n], out[n:2 * n], out[2 * n:]


def _unstack_heads(w_st):
    per = HEAD_DIM // N_CHIPS
    return w_st.reshape(N_CHIPS, HEADS, per, HEAD_DIM).transpose(1, 0, 2, 3).reshape(HEADS, HEAD_DIM, HEAD_DIM)


def _stack_heads(w):
    per = HEAD_DIM // N_CHIPS
    return w.reshape(HEADS, N_CHIPS, per, HEAD_DIM).transpose(1, 0, 2, 3).reshape(N_CHIPS, HEADS * per, HEAD_DIM)


def kernel(x, norm_mix_g, w_in, conv_w, conv_b, w_rgate, b_rgate, w_igate, b_igate, lru_lambda, w_out_a, sgu_ln_g, sgu_ln_b, sgu_w_s, sgu_b_s, w_out_b, w_out, norm_mlp_g, w_up, w_down, norm_final_g, loss_target, m_norm_mix_g, m_w_in, m_conv_w, m_conv_b, m_w_rgate, m_b_rgate, m_w_igate, m_b_igate, m_lru_lambda, m_w_out_a, m_sgu_ln_g, m_sgu_ln_b, m_sgu_w_s, m_sgu_b_s, m_w_out_b, m_w_out, m_norm_mlp_g, m_w_up, m_w_down, m_norm_final_g, v_norm_mix_g, v_w_in, v_conv_w, v_conv_b, v_w_rgate, v_b_rgate, v_w_igate, v_b_igate, v_lru_lambda, v_w_out_a, v_sgu_ln_g, v_sgu_ln_b, v_sgu_w_s, v_sgu_b_s, v_w_out_b, v_w_out, v_norm_mlp_g, v_w_up, v_w_down, v_norm_final_g):
    chip = _chip_index(lax.axis_index("x"), lax.axis_index("y"))
    core = lax.axis_index("c")
    quarter_h = HEAD_DIM // N_CHIPS
    quarter_d = D_MODEL // N_CHIPS

    as_2d = lambda a: a.reshape(-1, a.shape[-1])
    big_w = [as_2d(w) for w in (w_in, w_rgate, w_igate, w_out_a, w_out_b, w_out, w_up, w_down)]
    big_m = [as_2d(w) for w in (m_w_in, m_w_rgate, m_w_igate, m_w_out_a, m_w_out_b, m_w_out, m_w_up, m_w_down)]
    big_v = [as_2d(w) for w in (v_w_in, v_w_rgate, v_w_igate, v_w_out_a, v_w_out_b, v_w_out, v_w_up, v_w_down)]

    packed = jnp.concatenate([conv_w[0], b_rgate[0], b_igate[0]], axis=1)
    packed = jnp.concatenate([packed, jnp.zeros_like(packed)], axis=0)
    s_in, s_r, s_i, s_oa, s_ob, s_out, s_up, s_down = [w.astype(BF16) for w in big_w]
    xs, target = x[0], loss_target[0]
    g3 = norm_final_g.reshape(1, D_MODEL)
    bias_s = jnp.broadcast_to(jnp.transpose(sgu_b_s[0])[:, :, None], (CHUNK, GROUPS, GROUP_DIM)).reshape(CHUNK, D_MODEL)
    core_arr = core.reshape(1).astype(jnp.int32)
    place = jnp.stack([chip, core]).astype(jnp.int32)
    quarter = lambda g: g.reshape(N_CHIPS, D_MODEL // N_CHIPS, D_MODEL)

    def pair_add(nm, g, from_sibling):
        return _pair_add("pair_add_" + nm, core_arr, g.reshape(N_CHIPS, 2, g.shape[1] // 2, g.shape[2]), from_sibling)

    def chip_sum(nm, pair, from_chips):
        return _chip_sum("chip_sum_" + nm, place, pair, from_chips)

    order = jnp.stack([chip, chip ^ 2, chip ^ 1, chip ^ 3]).astype(jnp.int32)
    (z, n1, (w_in_st, wr_st, wi_st)), ((packed_all,), late) = _fwd_in(
        xs, norm_mix_g, [s_in, s_r, s_i], order,
        jobs=[_gather_small_job(packed), _gather_near_job([s_oa, s_ob, s_out, s_up, s_down])])
    pick = lambda lo, hi: packed_all[:, :HEADS, lo:hi].transpose(1, 0, 2).reshape(HEADS, -1)
    conv_w_full = pick(0, quarter_d)
    br_full = pick(quarter_d, quarter_d + quarter_h).reshape(1, D_MODEL)
    bi_full = pick(quarter_d + quarter_h, quarter_d + 2 * quarter_h).reshape(1, D_MODEL)
    wr, wi = _unstack_heads(wr_st), _unstack_heads(wi_st)
    lru = (conv_w_full, conv_b, wr, br_full, wi, bi_full, lru_lambda)
    sgu = (sgu_ln_g, sgu_ln_b, sgu_w_s[0], bias_s)

    (ya, *saved), (late,) = _fwd_lru(z, *lru, jobs=[_gather_far_job(late)])
    yb, (late,) = _fwd_sgu(z, *sgu, jobs=[_gather_pass_job(late)])
    w_oa, w_ob, w_o = [w.reshape(D_MODEL, D_MODEL) for w in late[:3]]
    w_up_st, w_dn = late[3], late[4].reshape(D_FF, D_MODEL)
    (pa, pb, h1, n2), _ = _fwd_merge(ya, yb, z, xs, w_oa, w_ob, w_o, norm_mlp_g)
    (act, dup, dh2b, dh1, loss_part, dg3, dg2), _ = _mlp(n2, h1, target, w_up_st, w_dn, norm_mlp_g, g3)

    d_up, _ = _weight_grad("dw_up", n2, dup, N_CHIPS, False, True, D_MODEL)
    d_down, ((r_up,),) = _weight_grad("dw_down", act, dh2b, N_CHIPS, True, False, D_MODEL,
                                      jobs=[_pair_send_job([d_up])])
    (r_down,), = _comm_call("send_w_down", [_pair_send_job([d_down])])
    p_up, p_down = pair_add("w_up", d_up, r_up), pair_add("w_down", d_down, r_down)
    (dz, merged, dpa, dpb, dh1b, dlg, dlb, dws, dbs, dcw, dcb, dwr, dbr, dwi, dbi, dlam), ((q_up, q_down),) = _bwd_mix(
        dh1, pa, pb, z, *saved, w_oa, w_ob, w_o, *sgu, conv_w_full, wr, wi, lru_lambda,
        jobs=[_chip_exchange_job([p_up, p_down])])
    half_up, half_down = chip_sum("w_up", p_up, q_up), chip_sum("w_down", p_down, q_down)
    gates = [_stack_heads(dwr).astype(BF16), _stack_heads(dwi).astype(BF16)]
    small = _pack_small(dcw, dcb, dbr, dbi, dlam, dlg, dlb, dg2, dg3, loss_part, dbs)
    d_in, ((full_up, full_down), r_gates, (vec_all, ws_all)) = _weight_grad(
        "dw_in", n1, dz, N_CHIPS, False, True, IN_SHARD,
        jobs=[_share_job([half_up, half_down]), _pair_send_job(gates), _gather_all_job([small, dws])])
    d_out, ((r_in,),) = _weight_grad("dw_out", merged, dh1b, 1, False, False, D_MODEL, jobs=[_pair_send_job([d_in])])
    d_oa, _ = _weight_grad("dw_out_a", ya, dpa, 1, False, False, D_MODEL)
    d_ob, _ = _weight_grad("dw_out_b", yb, dpb, 1, False, False, D_MODEL)
    mids = [quarter(d_oa), quarter(d_ob), quarter(d_out)]
    names = ("w_in", "w_rgate", "w_igate", "w_out_a", "w_out_b", "w_out", "w_up", "w_down")
    adam_args = {nm: (w, m, v) for nm, w, m, v in zip(names, big_w, big_m, big_v)}

    def adamw(nm, g):
        w, m, v = adam_args[nm]
        g = g.reshape(w.shape)
        return g, _adamw("adamw_" + nm, g, w, m, v)[0]

    p_first = [pair_add(nm, g, r) for nm, g, r in zip(names[:3], [d_in] + gates, [r_in] + r_gates)]
    (grad_x, dg1), (q_first, r_mids) = _bwd_in(
        dz, xs, dh1, w_in_st, norm_mix_g, jobs=[_chip_exchange_job(p_first), _pair_send_job(mids)])
    half_first = [chip_sum(nm, p, q) for nm, p, q in zip(names[:3], p_first, q_first)]
    p_mids = [pair_add(nm, g, r) for nm, g, r in zip(names[3:6], mids, r_mids)]
    q_mids = _sequencer_chip_exchange("exchange_mids", p_mids, 1)
    half_mids = [chip_sum(nm, p, q) for nm, p, q in zip(names[3:6], p_mids, q_mids)]
    full_last, (dg1_all,) = _comm_call("share_last", [_share_job(half_first + half_mids), _gather_all_job([dg1])])
    full, big_out = [], []
    for nm, f in zip(names, full_last + [full_up, full_down]):
        g, out = adamw(nm, f)
        full.append(g)
        big_out.append(out)

    vec, ws_sum = _sum_small(vec_all, ws_all, dg1_all)
    row = lambda r: vec[r:r + 1]
    shard = lambda a, width: lax.dynamic_slice_in_dim(a, chip * width, width, axis=1)
    g_small = dict(
        norm_mix_g=row(ROW_G1), conv_w=shard(vec[ROW_CW:ROW_CW + CONV_WIDTH], quarter_d), conv_b=row(ROW_CB),
        b_rgate=shard(row(ROW_BR).reshape(HEADS, HEAD_DIM), quarter_h),
        b_igate=shard(row(ROW_BI).reshape(HEADS, HEAD_DIM), quarter_h), lru_lambda=row(ROW_LAM),
        sgu_ln_g=row(ROW_LG), sgu_ln_b=row(ROW_LB),
        sgu_w_s=ws_sum.reshape(CHUNK, GROUPS, CHUNK).transpose(1, 0, 2).reshape(GROUPS * CHUNK, CHUNK),
        sgu_b_s=vec[ROW_BS:ROW_BS + GROUPS, 0:CHUNK], norm_mlp_g=row(ROW_G2), norm_final_g=row(ROW_G3))
    loss = vec[ROW_LOSS, 0]
    small_names = list(g_small)
    given = dict(
        norm_mix_g=(norm_mix_g, m_norm_mix_g, v_norm_mix_g), conv_w=(conv_w, m_conv_w, v_conv_w),
        conv_b=(conv_b, m_conv_b, v_conv_b), b_rgate=(b_rgate, m_b_rgate, v_b_rgate),
        b_igate=(b_igate, m_b_igate, v_b_igate), lru_lambda=(lru_lambda, m_lru_lambda, v_lru_lambda),
        sgu_ln_g=(sgu_ln_g, m_sgu_ln_g, v_sgu_ln_g), sgu_ln_b=(sgu_ln_b, m_sgu_ln_b, v_sgu_ln_b),
        sgu_w_s=(sgu_w_s, m_sgu_w_s, v_sgu_w_s), sgu_b_s=(sgu_b_s, m_sgu_b_s, v_sgu_b_s),
        norm_mlp_g=(norm_mlp_g, m_norm_mlp_g, v_norm_mlp_g), norm_final_g=(norm_final_g, m_norm_final_g, v_norm_final_g))
    g2d = [g_small[nm] for nm in small_names]
    to2d = lambda a, g: a.reshape(g.shape)
    d_s, m_s, v_s = _adamw_small(
        g2d, *[[to2d(given[nm][q], g) for nm, g in zip(small_names, g2d)] for q in range(3)])

    shapes = dict(
        norm_mix_g=norm_mix_g, w_in=w_in, conv_w=conv_w, conv_b=conv_b, w_rgate=w_rgate, b_rgate=b_rgate,
        w_igate=w_igate, b_igate=b_igate, lru_lambda=lru_lambda, w_out_a=w_out_a, sgu_ln_g=sgu_ln_g,
        sgu_ln_b=sgu_ln_b, sgu_w_s=sgu_w_s, sgu_b_s=sgu_b_s, w_out_b=w_out_b, w_out=w_out, norm_mlp_g=norm_mlp_g,
        w_up=w_up, w_down=w_down, norm_final_g=norm_final_g)
    grads, deltas, new_m, new_v = {}, {}, {}, {}
    for nm, g, (d, nmom, nvar) in zip(names, full, big_out):
        grads[nm], deltas[nm], new_m[nm], new_v[nm] = g, d, nmom, nvar
    for p, nm in enumerate(small_names):
        grads[nm], deltas[nm], new_m[nm], new_v[nm] = g2d[p], d_s[p], m_s[p], v_s[p]
    order = list(shapes)
    out = [loss, grad_x[None]]
    for group in (grads, deltas, new_m, new_v):
        out += [group[nm].reshape(shapes[nm].shape) for nm in order]
    return tuple(out)
```

```python
import functools

import jax
import jax.numpy as jnp
from jax import lax
from jax.experimental import pallas as pl
from jax.experimental.pallas import tpu as pltpu
from jax.experimental.pallas import tpu_sc as plsc

F32 = jnp.float32
BF16 = jnp.bfloat16
MESH = pl.DeviceIdType.MESH

D_MODEL = 1024
D_IN = 6 * D_MODEL
D_FF = 4 * D_MODEL
N_CHIPS = 4
IN_SHARD = D_IN // N_CHIPS
HEADS = 4
HEAD_DIM = D_MODEL // HEADS
GROUPS = 4
GROUP_DIM = D_MODEL // GROUPS
CHUNK = 128
CONV_WIDTH = 4
LRU_C = 8.0
NORM_EPS = 1e-6
LN_EPS = 1e-5

ADAM_LR = 0.001
ADAM_B1 = 0.9
ADAM_B2 = 0.999
ADAM_EPS = 1e-08
ADAM_WD = 0.01
ADAM_STEP = 10

SUBLANES = 8
MM_TILE = 512
IN_TILE = 1024
SEQ_TILE = 256
DW_TILE = 2048
VMEM_LIMIT_BYTES = 56 * 1024 * 1024

GELU_K0 = 0.7978845608028654
GELU_K1 = 0.044715


def _params(n_grid_axes=1):
    return pltpu.CompilerParams(
        dimension_semantics=("arbitrary",) * n_grid_axes, vmem_limit_bytes=VMEM_LIMIT_BYTES)


def _resident(shape):
    nd = len(shape)
    return pl.BlockSpec(shape, lambda *_: (0,) * nd, pipeline_mode=pl.Buffered(1))


def _const(shape):
    nd = len(shape)
    return pl.BlockSpec(shape, lambda *_: (0,) * nd)


def _dot(a, b):
    return jnp.dot(a, b, preferred_element_type=F32)


def _dot_nt(a, b):
    return lax.dot_general(a, b, (((1,), (1,)), ((), ())), preferred_element_type=F32)


def _dot_tn(a, b):
    return lax.dot_general(a, b, (((0,), (0,)), ((), ())), preferred_element_type=F32)


def _gelu(x):
    t = jnp.tanh(GELU_K0 * x * (1.0 + GELU_K1 * x * x))
    return 0.5 * x * (1.0 + t)


def _gelu_and_grad(x):
    x2 = x * x
    t = jnp.tanh(GELU_K0 * x * (1.0 + GELU_K1 * x2))
    g = 0.5 * x * (1.0 + t)
    dg = 0.5 * (1.0 + t) + 0.5 * x * (1.0 - t * t) * (GELU_K0 * (1.0 + 3.0 * GELU_K1 * x2))
    return g, dg


def _rms(x):
    r = lax.rsqrt(jnp.mean(x * x, axis=-1, keepdims=True) + NORM_EPS)
    return x * r, r


def _rms_bwd(dn, xhat, r):
    return r * (dn - xhat * jnp.mean(dn * xhat, axis=-1, keepdims=True))


def _col_sum(v):
    return jnp.sum(v, axis=0, keepdims=True)


def _shift_down(x, tail8, k):
    xs = pltpu.roll(x, k, 0)
    ts = pltpu.roll(tail8, k, 0)
    ridx = lax.broadcasted_iota(jnp.int32, tail8.shape, 0)
    head = jnp.where(ridx < k, ts, xs[0:SUBLANES])
    return jnp.concatenate([head, xs[SUBLANES:]], axis=0)


def _shift_up(x, head8, k):
    n = x.shape[0]
    xs = pltpu.roll(x, n - k, 0)
    hs = pltpu.roll(head8, SUBLANES - k, 0)
    ridx = lax.broadcasted_iota(jnp.int32, head8.shape, 0)
    last = jnp.where(ridx >= SUBLANES - k, hs, xs[n - SUBLANES:n])
    return jnp.concatenate([xs[:n - SUBLANES], last], axis=0)


def _scan_forward(a, b, carry):
    n, cols = a.shape
    groups = n // SUBLANES
    a = a.reshape(groups, SUBLANES, cols)
    b = b.reshape(groups, SUBLANES, cols)
    sub = lax.broadcasted_iota(jnp.int32, a.shape, 1)
    for s in (1, 2, 4):
        a_s = pltpu.roll(a, s, 1)
        b_s = pltpu.roll(b, s, 1)
        m = sub >= s
        b = jnp.where(m, a * b_s + b, b)
        a = jnp.where(m, a * a_s, a)
    out = []
    for g in range(groups):
        h = a[g] * carry + b[g]
        out.append(h)
        carry = h[SUBLANES - 1:SUBLANES]
    return jnp.concatenate(out, axis=0), carry


def _scan_backward(a, b, carry):
    n, cols = a.shape
    groups = n // SUBLANES
    a = a.reshape(groups, SUBLANES, cols)
    b = b.reshape(groups, SUBLANES, cols)
    sub = lax.broadcasted_iota(jnp.int32, a.shape, 1)
    for s in (1, 2, 4):
        a_s = pltpu.roll(a, SUBLANES - s, 1)
        b_s = pltpu.roll(b, SUBLANES - s, 1)
        m = sub < SUBLANES - s
        b = jnp.where(m, a * b_s + b, b)
        a = jnp.where(m, a * a_s, a)
    out = [None] * groups
    for g in reversed(range(groups)):
        h = a[g] * carry + b[g]
        out[g] = h
        carry = h[0:1]
    return jnp.concatenate(out, axis=0), carry


def _softplus_neg(lam):
    e = jnp.exp(-jnp.abs(lam))
    u = 1.0 + e
    log1p_e = jnp.where(u == 1.0, e, jnp.log(u) * (e / jnp.where(u == 1.0, 1.0, u - 1.0)))
    return jnp.maximum(-lam, 0.0) + log1p_e


def _lru_gates(xa, tail8, cw_ref, cb_ref, wr_ref, br_ref, wi_ref, bi_ref, lam_ref):
    cw = cw_ref[...]
    xc = cb_ref[...] + cw[0:1] * xa
    for k in range(1, CONV_WIDTH):
        xc = xc + cw[k:k + 1] * _shift_down(xa, tail8, k)
    xcb = xc.astype(BF16)
    pre_r, pre_i = [], []
    for h in range(HEADS):
        cols = slice(h * HEAD_DIM, (h + 1) * HEAD_DIM)
        pre_r.append(_dot(xcb[:, cols], wr_ref[h]))
        pre_i.append(_dot(xcb[:, cols], wi_ref[h]))
    r = jax.nn.sigmoid(jnp.concatenate(pre_r, axis=1) + br_ref[...])
    ig = jax.nn.sigmoid(jnp.concatenate(pre_i, axis=1) + bi_ref[...])
    _, a, mult = _decay(r, lam_ref)
    return xc, r, ig, a, mult


def _decay(r, lam_ref):
    sp = _softplus_neg(lam_ref[...])
    log_a = ((-LRU_C) * sp) * r
    a = jnp.exp(log_a)
    th = jnp.tanh(log_a)
    return sp, a, jnp.sqrt((-2.0 * th) / (1.0 - th))


class _Job:
    def __init__(self, inputs, out_shape, n_sem, copies, peers, aliases=None, n_local=0):
        self.inputs, self.out_shape, self.n_sem, self.copies = list(inputs), list(out_shape), n_sem, copies
        self.aliases, self.n_local = dict(aliases or {}), n_local
        self.peers = tuple(peers)


def _fused_call(body, jobs, *, name, grid, in_specs, out_specs, out_shape, scratch_shapes=(),
                input_output_aliases=None, compiler_params=None, n_prefetch=0, jobs_start_after=None):
    single = not isinstance(out_shape, (list, tuple))
    out_specs = [out_specs] if single else list(out_specs)
    out_shape = [out_shape] if single else list(out_shape)
    n_scr = len(scratch_shapes)
    in_specs, scratch_shapes = list(in_specs), list(scratch_shapes)
    n_in, n_out = len(in_specs), len(out_shape)
    aliases = dict(input_output_aliases or {})
    in_at, out_at = [], []
    for job in jobs:
        in_at.append(len(in_specs))
        out_at.append(len(out_shape))
        for i, o in job.aliases.items():
            aliases[n_prefetch + len(in_specs) + i] = len(out_shape) + o
        in_specs += [ANY] * len(job.inputs)
        out_specs += [ANY] * len(job.out_shape)
        out_shape += job.out_shape
        scratch_shapes += [pltpu.SemaphoreType.DMA((job.n_sem,)), pltpu.SemaphoreType.DMA((job.n_sem,)),
                           pltpu.SemaphoreType.DMA((max(job.n_local, 1),))]
    n_in_all, n_out_all = len(in_specs), len(out_shape)

    def full_body(*refs):
        prefetch, refs = refs[:n_prefetch], refs[n_prefetch:]
        ins, outs, scr = refs[:n_in_all], refs[n_in_all:n_in_all + n_out_all], refs[n_in_all + n_out_all:]

        def copies(q):
            job = jobs[q]
            return job.copies(ins[in_at[q]:in_at[q] + len(job.inputs)], outs[out_at[q]:out_at[q] + len(job.out_shape)],
                              *scr[n_scr + 3 * q:n_scr + 3 * q + 3])

        def start():
            for q in range(len(jobs)):
                sends, _, local = copies(q)
                for cp in local + sends:
                    cp.start()

        def finish():
            every = [copies(q) for q in range(len(jobs))]
            for _, arrivals, _ in every:
                for cp in arrivals:
                    cp.wait_recv()
            for sends, _, local in every:
                for cp in sends:
                    cp.wait_send()
                for cp in local:
                    cp.wait()

        if not grid:
            start()
            finish()
            return
        ids = [pl.program_id(a) for a in range(len(grid))]
        at_step = lambda step: functools.reduce(jnp.logical_and, [i == k for i, k in zip(ids, step)])
        if jobs and jobs_start_after is None:
            pl.when(at_step((0,) * len(grid)))(start)
        body(*prefetch, *ins[:n_in], *outs[:n_out], *scr[:n_scr])
        if jobs and jobs_start_after is not None:
            pl.when(at_step(jobs_start_after))(start)
        if jobs:
            pl.when(functools.reduce(jnp.logical_and, [i == g - 1 for i, g in zip(ids, grid)]))(finish)

    if n_prefetch:
        layout = dict(grid_spec=pltpu.PrefetchScalarGridSpec(
            num_scalar_prefetch=n_prefetch, grid=grid, in_specs=in_specs, out_specs=out_specs,
            scratch_shapes=scratch_shapes))
    else:
        layout = dict(grid=grid, in_specs=in_specs, out_specs=out_specs, scratch_shapes=scratch_shapes)
    call = pl.pallas_call(
        full_body, name=name, out_shape=out_shape, input_output_aliases=aliases, compiler_params=compiler_params,
        **layout)

    def run(*args):
        res = call(*args, *[a for job in jobs for a in job.inputs])
        mine = res[0] if single else list(res[:n_out])
        return mine, [list(res[at:at + len(job.out_shape)]) for at, job in zip(out_at, jobs)]

    return run


def _fwd_in(x, g1, shards, order, jobs=()):
    t = x.shape[0]
    rows_per_step = min(IN_TILE, t)
    n_tiles = t // rows_per_step
    n = len(shards)
    halves = [s.shape[0] // 2 for s in shards]

    def body(order_ref, x_ref, g_ref, *refs):
        del order_ref
        ins, (z_ref, n_ref), outs = refs[:n], refs[n:n + 2], refs[n + 2:2 * n + 2]
        wbuf, nbuf, send, recv, local = refs[2 * n + 2:]
        s, i = pl.program_id(0), pl.program_id(1)
        x_, y_, c, chips = _place()
        k_me = _chip_index(x_, y_)

        def block(w, chip, pc):
            return outs[w].at[_chip_index(*chip), pl.ds(pc * halves[w], halves[w]), :]

        def over_ici(w, j, landing):
            return pltpu.make_async_remote_copy(
                src_ref=ins[w].at[pl.ds(c * halves[w], halves[w]), :],
                dst_ref=block(w, chips[j] if landing else (x_, y_), c), send_sem=send.at[6 * w + j],
                recv_sem=recv.at[6 * w + j], device_id=(*chips[j], c), device_id_type=MESH)

        def to_sibling(w, j, landing):
            blk = block(w, chips[j], 1 - c if landing else c)
            return pltpu.make_async_remote_copy(
                src_ref=blk, dst_ref=blk, send_sem=send.at[6 * w + 3 + j], recv_sem=recv.at[6 * w + 3 + j],
                device_id=(x_, y_, 1 - c), device_id_type=MESH)

        own = [pltpu.make_async_copy(wbuf, outs[0].at[k_me], local.at[0])]
        own += [pltpu.make_async_copy(ins[w], outs[w].at[k_me], local.at[w]) for w in range(1, n)]

        @pl.when((s == 0) & (i == 0))
        def _():
            for j in range(2):
                for w in range(n):
                    over_ici(w, j, False).start()
            load = pltpu.make_async_copy(ins[0], wbuf, local.at[n])
            load.start()
            load.wait()
            for cp in own:
                cp.start()

        for j in range(N_CHIPS - 1):
            @pl.when((s == j + 1) & (i == 0))
            def _(j=j):
                for w in range(n):
                    over_ici(w, j, True).wait_recv()
                for w in range(n):
                    to_sibling(w, j, False).start()
                if j == 0:
                    for w in range(n):
                        over_ici(w, 2, False).start()
                    own[0].wait()
                for w in range(n):
                    to_sibling(w, j, True).wait_recv()
                load = pltpu.make_async_copy(outs[0].at[_chip_index(*chips[j])], wbuf, local.at[n])
                load.start()
                load.wait()

        rows = pl.ds(pl.multiple_of(i * rows_per_step, rows_per_step), rows_per_step)

        @pl.when(s == 0)
        def _():
            xhat, _ = _rms(x_ref[...])
            nrm = (xhat * g_ref[...]).astype(BF16)
            nbuf[rows, :] = nrm
            n_ref[...] = nrm

        z_ref[...] = _dot(nbuf[rows, :], wbuf[...])

        @pl.when((s == N_CHIPS - 1) & (i == n_tiles - 1))
        def _():
            for j in range(N_CHIPS - 1):
                for w in range(n):
                    over_ici(w, j, False).wait_send()
                    to_sibling(w, j, False).wait_send()
            for cp in own[1:]:
                cp.wait()

    once = lambda s, i, order: (jnp.where(s == 0, i, n_tiles - 1), 0)
    (z, n1, *stacked), job_outs = _fused_call(
        body, jobs, name="fwd_in", grid=(N_CHIPS, n_tiles), n_prefetch=1,
        in_specs=[pl.BlockSpec((rows_per_step, D_MODEL), once), _const((1, D_MODEL))] + [ANY] * n,
        out_specs=[pl.BlockSpec((rows_per_step, IN_SHARD), lambda s, i, order: (i, order[s])),
                   pl.BlockSpec((rows_per_step, D_MODEL), once)] + [ANY] * n,
        out_shape=[jax.ShapeDtypeStruct((t, D_IN), F32), jax.ShapeDtypeStruct((t, D_MODEL), BF16)]
        + [jax.ShapeDtypeStruct((N_CHIPS,) + s.shape, s.dtype) for s in shards],
        scratch_shapes=[pltpu.VMEM(shards[0].shape, BF16), pltpu.VMEM((t, D_MODEL), BF16),
                        pltpu.SemaphoreType.DMA((6 * n,)),
                        pltpu.SemaphoreType.DMA((6 * n,)), pltpu.SemaphoreType.DMA((n + 1,))],
        compiler_params=_params(2), jobs_start_after=(1, 0),
    )(order, x, g1, *shards)
    return (z, n1, stacked), job_outs


def _fwd_lru(z, conv_w, conv_b, wr, br, wi, bi, lam, jobs=()):
    t = z.shape[0]

    def body(xa_ref, ga_ref, cw_ref, cb_ref, wr_ref, br_ref, wi_ref, bi_ref, lam_ref, ya_ref, h_ref, xc_ref, r_ref,
             ig_ref, tail_ref, carry_ref):
        @pl.when(pl.program_id(0) == 0)
        def _():
            tail_ref[...] = jnp.zeros_like(tail_ref)
            carry_ref[...] = jnp.zeros_like(carry_ref)

        xa = xa_ref[...]
        xc, r, ig, a, mult = _lru_gates(xa, tail_ref[...], cw_ref, cb_ref, wr_ref, br_ref, wi_ref, bi_ref, lam_ref)
        tail_ref[...] = xa[SEQ_TILE - SUBLANES:]
        xc_ref[...], r_ref[...], ig_ref[...] = xc, r, ig
        h, carry = _scan_forward(a, xc * ig * mult, carry_ref[...])
        carry_ref[...] = carry
        h_ref[...] = h
        ya_ref[...] = (h * _gelu(ga_ref[...])).astype(BF16)

    tile = lambda j: pl.BlockSpec((SEQ_TILE, D_MODEL), lambda i: (i, j))
    return _fused_call(
        body, jobs, name="fwd_lru", grid=(t // SEQ_TILE,),
        in_specs=[tile(0), tile(1), _const((CONV_WIDTH, D_MODEL)), _const((1, D_MODEL)),
                  _resident((HEADS, HEAD_DIM, HEAD_DIM)), _const((1, D_MODEL)),
                  _resident((HEADS, HEAD_DIM, HEAD_DIM)), _const((1, D_MODEL)), _const((1, D_MODEL))],
        out_specs=[tile(0)] * 5,
        out_shape=[jax.ShapeDtypeStruct((t, D_MODEL), BF16)] + [jax.ShapeDtypeStruct((t, D_MODEL), F32)] * 4,
        scratch_shapes=[pltpu.VMEM((SUBLANES, D_MODEL), F32), pltpu.VMEM((1, D_MODEL), F32)],
        compiler_params=_params(),
    )(z, z, conv_w, conv_b, wr, br, wi, bi, lam)


def _sgu_forward_parts(ub, vb, lg_ref, lb_ref):
    u, du = _gelu_and_grad(ub)
    vg, dvg = _gelu_and_grad(vb)
    mu = jnp.mean(vg, axis=-1, keepdims=True)
    d = vg - mu
    rstd = lax.rsqrt(jnp.mean(d * d, axis=-1, keepdims=True) + LN_EPS)
    vhat = d * rstd
    vn = (vhat * lg_ref[...] + lb_ref[...]).astype(BF16)
    return u, du, dvg, rstd, vhat, vn


def _causal_mask():
    rows = lax.broadcasted_iota(jnp.int32, (CHUNK, CHUNK), 0)
    cols = lax.broadcasted_iota(jnp.int32, (CHUNK, CHUNK), 1)
    return rows >= cols


def _fwd_sgu(z, ln_g, ln_b, w_s, bias_full, jobs=()):
    t = z.shape[0]

    def body(ub_ref, vb_ref, lg_ref, lb_ref, ws_ref, bias_ref, yb_ref):
        u, _, _, _, _, vn = _sgu_forward_parts(ub_ref[...], vb_ref[...], lg_ref, lb_ref)
        mask = _causal_mask()
        wm = [jnp.where(mask, ws_ref[g], 0.0).astype(BF16) for g in range(GROUPS)]
        for c in range(SEQ_TILE // CHUNK):
            rows = slice(c * CHUNK, (c + 1) * CHUNK)
            for g in range(GROUPS):
                cols = slice(g * GROUP_DIM, (g + 1) * GROUP_DIM)
                sp = _dot(wm[g], vn[rows, cols]) + bias_ref[:, cols]
                yb_ref[rows, cols] = (u[rows, cols] * sp).astype(BF16)

    tile = lambda j: pl.BlockSpec((SEQ_TILE, D_MODEL), lambda i: (i, j))
    return _fused_call(
        body, jobs, name="fwd_sgu", grid=(t // SEQ_TILE,),
        in_specs=[tile(2), tile(3), _const((1, D_MODEL)), _const((1, D_MODEL)),
                  _const((GROUPS, CHUNK, CHUNK)), _const((CHUNK, D_MODEL))],
        out_specs=tile(0),
        out_shape=jax.ShapeDtypeStruct((t, D_MODEL), BF16),
        compiler_params=_params(),
    )(z, z, ln_g, ln_b, w_s, bias_full)


def _fwd_merge(ya, yb, z, x, w_oa, w_ob, w_out, g2, jobs=()):
    t = x.shape[0]

    def body(ya_ref, yb_ref, m_ref, x_ref, woa_ref, wob_ref, wout_ref, g_ref, pa_ref, pb_ref, h1_ref, n2_ref):
        pa = _dot(ya_ref[...], woa_ref[...])
        pb = _dot(yb_ref[...], wob_ref[...])
        pa_ref[...] = pa
        pb_ref[...] = pb
        merged = jax.nn.sigmoid(m_ref[:, :D_MODEL]) * pa + jax.nn.sigmoid(m_ref[:, D_MODEL:]) * pb
        h1 = x_ref[...] + _dot(merged.astype(BF16), wout_ref[...])
        h1_ref[...] = h1
        xhat, _ = _rms(h1)
        n2_ref[...] = (xhat * g_ref[...]).astype(BF16)

    tile = pl.BlockSpec((MM_TILE, D_MODEL), lambda i: (i, 0))
    sq = _resident((D_MODEL, D_MODEL))
    return _fused_call(
        body, jobs, name="fwd_merge", grid=(t // MM_TILE,),
        in_specs=[tile, tile, pl.BlockSpec((MM_TILE, 2 * D_MODEL), lambda i: (i, 2)), tile, sq, sq, sq,
                  _const((1, D_MODEL))],
        out_specs=[tile, tile, tile, tile],
        out_shape=[jax.ShapeDtypeStruct((t, D_MODEL), F32)] * 3 + [jax.ShapeDtypeStruct((t, D_MODEL), BF16)],
        compiler_params=_params(),
    )(ya, yb, z, x, w_oa, w_ob, w_out, g2)


def _mlp(n2, h1, target, w_up_st, w_down, g2, g3, jobs=()):
    t = n2.shape[0]

    def body(n2_ref, h1_ref, tgt_ref, wup_ref, wdown_ref, g2_ref, g3_ref, act_ref, dup_ref, dh2b_ref, dh1_ref,
             loss_ref, dg3_ref, dg2_ref, relu_ref):
        @pl.when(pl.program_id(0) == 0)
        def _():
            for ref in (loss_ref, dg3_ref, dg2_ref):
                ref[...] = jnp.zeros_like(ref)

        n2 = n2_ref[...]
        h1 = h1_ref[...]
        h2 = h1
        for k in range(N_CHIPS):
            cols = slice(k * D_MODEL, (k + 1) * D_MODEL)
            r = jnp.maximum(_dot(n2, wup_ref[k]), 0.0)
            relu_ref[:, cols] = r
            act = (r * r).astype(BF16)
            act_ref[:, cols] = act
            h2 = h2 + _dot(act, wdown_ref[cols, :])
        xhat, r3 = _rms(h2)
        diff = xhat * g3_ref[...] - tgt_ref[...]
        sq = jnp.sum(diff * diff, axis=1, keepdims=True)
        loss_ref[...] = loss_ref[...] + (0.5 / D_MODEL) * jnp.sum(sq, axis=0, keepdims=True)
        dy = diff * (1.0 / D_MODEL)
        dg3_ref[...] = dg3_ref[...] + _col_sum(dy * xhat)
        dh2 = _rms_bwd(dy * g3_ref[...], xhat, r3)
        dh2b = dh2.astype(BF16)
        dh2b_ref[...] = dh2b
        dn2 = jnp.zeros((SEQ_TILE, D_MODEL), F32)
        for k in range(N_CHIPS):
            cols = slice(k * D_MODEL, (k + 1) * D_MODEL)
            dup = (_dot_nt(dh2b, wdown_ref[cols, :]) * (2.0 * relu_ref[:, cols])).astype(BF16)
            dup_ref[:, cols] = dup
            dn2 = dn2 + _dot_nt(dup, wup_ref[k])
        xhat, r2 = _rms(h1)
        dg2_ref[...] = dg2_ref[...] + _col_sum(dn2 * xhat)
        dh1_ref[...] = dh2 + _rms_bwd(dn2 * g2_ref[...], xhat, r2)

    tile = pl.BlockSpec((SEQ_TILE, D_MODEL), lambda i: (i, 0))
    wide = pl.BlockSpec((SEQ_TILE, D_FF), lambda i: (i, 0))
    vec = _const((1, D_MODEL))
    vec_shape = jax.ShapeDtypeStruct((1, D_MODEL), F32)
    return _fused_call(
        body, jobs, name="mlp", grid=(t // SEQ_TILE,),
        in_specs=[tile, tile, tile, _resident((N_CHIPS, D_MODEL, D_MODEL)), _resident((D_FF, D_MODEL)), vec, vec],
        out_specs=[wide, wide, tile, tile, _const((SUBLANES, 128)), vec, vec],
        out_shape=[jax.ShapeDtypeStruct((t, D_FF), BF16), jax.ShapeDtypeStruct((t, D_FF), BF16),
                   jax.ShapeDtypeStruct((t, D_MODEL), BF16), jax.ShapeDtypeStruct((t, D_MODEL), F32),
                   jax.ShapeDtypeStruct((SUBLANES, 128), F32), vec_shape, vec_shape],
        scratch_shapes=[pltpu.VMEM((SEQ_TILE, D_FF), F32)],
        compiler_params=_params(),
    )(n2, h1, target, w_up_st, w_down, g2, g3)


def _bwd_mix(dh1, pa, pb, z, h, xc, r, ig, w_oa, w_ob, w_out, ln_g, ln_b, w_s, bias_full, conv_w, wr, wi, lam, jobs=()):
    t = dh1.shape[0]
    n_tiles = t // SEQ_TILE
    per_tile = SEQ_TILE // SUBLANES

    def merge_part(dh1_ref, pa_ref, pb_ref, m_ref, woa_ref, wob_ref, wout_ref, dz_ref, dya_ref, dyb_ref, mg_ref,
                   dpa_ref, dpb_ref, dh1b_ref):
        dh1b = dh1_ref[...].astype(BF16)
        dh1b_ref[...] = dh1b
        dm = _dot_nt(dh1b, wout_ref[...])
        pa = pa_ref[...]
        pb = pb_ref[...]
        sa = jax.nn.sigmoid(m_ref[:, :D_MODEL])
        sb = jax.nn.sigmoid(m_ref[:, D_MODEL:])
        mg_ref[...] = (sa * pa + sb * pb).astype(BF16)
        dz_ref[:, :D_MODEL] = (dm * pa * sa * (1.0 - sa)).astype(BF16)
        dz_ref[:, D_MODEL:] = (dm * pb * sb * (1.0 - sb)).astype(BF16)
        dpa = (dm * sa).astype(BF16)
        dpb = (dm * sb).astype(BF16)
        dpa_ref[...] = dpa
        dpb_ref[...] = dpb
        dya_ref[...] = _dot_nt(dpa, woa_ref[...])
        dyb_ref[...] = _dot_nt(dpb, wob_ref[...])

    def sgu_part(dyb_ref, ub_ref, vb_ref, lg_ref, lb_ref, ws_ref, bias_ref, dz_ref, dlg_ref, dlb_ref, dws_ref, dbs_ref,
                 dvn_ref, dsp_acc):
        i = pl.program_id(0)

        @pl.when(i == 0)
        def _():
            dlg_ref[...] = jnp.zeros_like(dlg_ref)
            dlb_ref[...] = jnp.zeros_like(dlb_ref)
            dws_ref[...] = jnp.zeros_like(dws_ref)
            dsp_acc[...] = jnp.zeros_like(dsp_acc)

        u, du, dvg, rstd, vhat, vn = _sgu_forward_parts(ub_ref[...], vb_ref[...], lg_ref, lb_ref)
        dyb = dyb_ref[...]
        mask = _causal_mask()
        wm = [jnp.where(mask, ws_ref[g], 0.0).astype(BF16) for g in range(GROUPS)]
        for c in range(SEQ_TILE // CHUNK):
            rows = slice(c * CHUNK, (c + 1) * CHUNK)
            for g in range(GROUPS):
                cols = slice(g * GROUP_DIM, (g + 1) * GROUP_DIM)
                vn_blk = vn[rows, cols]
                sp = _dot(wm[g], vn_blk) + bias_ref[:, cols]
                dyb_blk = dyb[rows, cols]
                dz_ref[rows, cols] = (dyb_blk * sp * du[rows, cols]).astype(BF16)
                dsp = dyb_blk * u[rows, cols]
                dsp_acc[:, cols] = dsp_acc[:, cols] + dsp
                dspb = dsp.astype(BF16)
                dvn_ref[rows, cols] = _dot_tn(wm[g], dspb)
                wcols = slice(g * CHUNK, (g + 1) * CHUNK)
                dws_ref[:, wcols] = dws_ref[:, wcols] + jnp.where(mask, _dot_nt(dspb, vn_blk), 0.0)
        dvn = dvn_ref[...]
        dlg_ref[...] = dlg_ref[...] + _col_sum(dvn * vhat)
        dlb_ref[...] = dlb_ref[...] + _col_sum(dvn)
        dvhat = dvn * lg_ref[...]
        dvgel = rstd * (dvhat - jnp.mean(dvhat, axis=-1, keepdims=True)
                        - vhat * jnp.mean(dvhat * vhat, axis=-1, keepdims=True))
        dz_ref[:, D_MODEL:] = (dvgel * dvg).astype(BF16)

        @pl.when(i == n_tiles - 1)
        def _():
            lane = lax.broadcasted_iota(jnp.int32, (CHUNK, 128), 1)
            out = jnp.zeros((CHUNK, 128), F32)
            for g in range(GROUPS):
                s = jnp.sum(dsp_acc[:, g * GROUP_DIM:(g + 1) * GROUP_DIM], axis=1, keepdims=True)
                out = out + jnp.where(lane == g, s, 0.0)
            dbs_ref[...] = out

    def lru_part(dya_ref, xa_ref, ga_ref, h_ref, h_prev_ref, xc_ref, r_ref, ig_ref, cw_ref, wr_ref, wi_ref, lam_ref,
                 dz_ref, dcw_ref, dcb_ref, dwr_ref, dbr_ref, dwi_ref, dbi_ref, dlam_ref, lam_carry, dxc_head):
        i = pl.program_id(0)

        @pl.when(i == 0)
        def _():
            for ref in (dcw_ref, dcb_ref, dwr_ref, dbr_ref, dwi_ref, dbi_ref, dlam_ref, lam_carry, dxc_head):
                ref[...] = jnp.zeros_like(ref)

        first_tile = i == n_tiles - 1
        h_tail = jnp.where(first_tile, 0.0, h_prev_ref[...])
        xc, r, ig = xc_ref[...], r_ref[...], ig_ref[...]
        xcb = xc.astype(BF16)
        sp, a, mult = _decay(r, lam_ref)
        h = h_ref[...]
        h_prev = _shift_down(h, h_tail, 1)
        dya = dya_ref[...]
        gg, dgg = _gelu_and_grad(ga_ref[...])
        dz_ref[:, D_MODEL:] = (dya * h * dgg).astype(BF16)
        ones = jnp.ones((SUBLANES, D_MODEL), F32)
        lam_t, lam_first = _scan_backward(_shift_up(a, ones, 1), dya * gg, lam_carry[...])
        lam_carry[...] = a[0:1] * lam_first
        dmult = lam_t * xc * ig
        dla = lam_t * h_prev * a - dmult * (a * a) / mult
        dr = dla * ((-LRU_C) * sp)
        dlam_ref[...] = dlam_ref[...] + _col_sum(dla * r) * (LRU_C * jax.nn.sigmoid(-lam_ref[...]))
        dpr = dr * r * (1.0 - r)
        dpi = lam_t * xc * mult * ig * (1.0 - ig)
        dbr_ref[...] = dbr_ref[...] + _col_sum(dpr)
        dbi_ref[...] = dbi_ref[...] + _col_sum(dpi)
        dprb = dpr.astype(BF16)
        dpib = dpi.astype(BF16)
        dxc_gate = []
        for hd in range(HEADS):
            cols = slice(hd * HEAD_DIM, (hd + 1) * HEAD_DIM)
            dxc_gate.append(_dot_nt(dprb[:, cols], wr_ref[hd]) + _dot_nt(dpib[:, cols], wi_ref[hd]))
            dwr_ref[hd] = dwr_ref[hd] + _dot_tn(xcb[:, cols], dprb[:, cols])
            dwi_ref[hd] = dwi_ref[hd] + _dot_tn(xcb[:, cols], dpib[:, cols])
        dxc = lam_t * ig * mult + jnp.concatenate(dxc_gate, axis=1)
        dcb_ref[...] = dcb_ref[...] + _col_sum(dxc)
        cw = cw_ref[...]
        head = dxc_head[...]
        xa = xa_ref[...]
        dxa = cw[0:1] * dxc
        dcw_ref[0:1, :] = dcw_ref[0:1, :] + _col_sum(dxc * xa)
        for k in range(1, CONV_WIDTH):
            dxc_k = _shift_up(dxc, head, k)
            dxa = dxa + cw[k:k + 1] * dxc_k
            dcw_ref[k:k + 1, :] = dcw_ref[k:k + 1, :] + _col_sum(dxc_k * xa)
        dxc_head[...] = dxc[0:SUBLANES]
        dz_ref[:, :D_MODEL] = dxa.astype(BF16)

    def body(dh1_ref, pa_ref, pb_ref, z_ref, h_ref, h_prev_ref, xc_ref, r_ref, ig_ref, woa_ref, wob_ref, wout_ref,
             lg_ref, lb_ref, ws_ref, bias_ref, cw_ref, wr_ref, wi_ref, lam_ref, dz_ref, mg_ref, dpa_ref, dpb_ref,
             dh1b_ref, dlg_ref, dlb_ref, dws_ref, dbs_ref, dcw_ref, dcb_ref, dwr_ref, dbr_ref, dwi_ref, dbi_ref,
             dlam_ref, dya_ref, dyb_ref, dvn_ref, dsp_acc, lam_carry, dxc_head):
        def cols(ref, first, count):
            return ref.at[:, pl.ds(first * D_MODEL, count * D_MODEL)]

        merge_part(dh1_ref, pa_ref, pb_ref, cols(z_ref, 4, 2), woa_ref, wob_ref, wout_ref, cols(dz_ref, 4, 2), dya_ref,
                   dyb_ref, mg_ref, dpa_ref, dpb_ref, dh1b_ref)
        sgu_part(dyb_ref, cols(z_ref, 2, 1), cols(z_ref, 3, 1), lg_ref, lb_ref, ws_ref, bias_ref, cols(dz_ref, 2, 2),
                 dlg_ref, dlb_ref, dws_ref, dbs_ref, dvn_ref, dsp_acc)
        lru_part(dya_ref, cols(z_ref, 0, 1), cols(z_ref, 1, 1), h_ref, h_prev_ref, xc_ref, r_ref, ig_ref, cw_ref, wr_ref,
                 wi_ref, lam_ref, cols(dz_ref, 0, 2), dcw_ref, dcb_ref, dwr_ref, dbr_ref, dwi_ref, dbi_ref, dlam_ref,
                 lam_carry, dxc_head)

    rev = lambda i: n_tiles - 1 - i
    tile = pl.BlockSpec((SEQ_TILE, D_MODEL), lambda i: (rev(i), 0))
    row = pl.BlockSpec((SEQ_TILE, D_IN), lambda i: (rev(i), 0))
    prev8 = pl.BlockSpec((SUBLANES, D_MODEL), lambda i: (jnp.maximum(rev(i) * per_tile - 1, 0), 0))
    vec = _const((1, D_MODEL))
    sq = _resident((D_MODEL, D_MODEL))
    gate_w = _resident((HEADS, HEAD_DIM, HEAD_DIM))
    gate_acc = _const((HEADS, HEAD_DIM, HEAD_DIM))
    vec_shape = jax.ShapeDtypeStruct((1, D_MODEL), F32)
    gate_shape = jax.ShapeDtypeStruct((HEADS, HEAD_DIM, HEAD_DIM), F32)
    act_bf = jax.ShapeDtypeStruct((t, D_MODEL), BF16)
    return _fused_call(
        body, jobs, name="bwd_mix", grid=(n_tiles,),
        in_specs=[tile, tile, tile, row, tile, prev8, tile, tile, tile, sq, sq, sq, vec, vec,
                  _const((GROUPS, CHUNK, CHUNK)), _const((CHUNK, D_MODEL)), _const((CONV_WIDTH, D_MODEL)), gate_w, gate_w,
                  vec],
        out_specs=[row, tile, tile, tile, tile, vec, vec, _const((CHUNK, GROUPS * CHUNK)), _const((CHUNK, 128)),
                   _const((SUBLANES, D_MODEL)), vec, gate_acc, vec, gate_acc, vec, vec],
        out_shape=[jax.ShapeDtypeStruct((t, D_IN), BF16), act_bf, act_bf, act_bf, act_bf, vec_shape, vec_shape,
                   jax.ShapeDtypeStruct((CHUNK, GROUPS * CHUNK), F32), jax.ShapeDtypeStruct((CHUNK, 128), F32),
                   jax.ShapeDtypeStruct((SUBLANES, D_MODEL), F32), vec_shape, gate_shape, vec_shape, gate_shape,
                   vec_shape, vec_shape],
        scratch_shapes=[pltpu.VMEM((SEQ_TILE, D_MODEL), F32), pltpu.VMEM((SEQ_TILE, D_MODEL), F32),
                        pltpu.VMEM((SEQ_TILE, D_MODEL), F32), pltpu.VMEM((CHUNK, D_MODEL), F32),
                        pltpu.VMEM((1, D_MODEL), F32), pltpu.VMEM((SUBLANES, D_MODEL), F32)],
        compiler_params=_params(),
    )(dh1, pa, pb, z, h, h, xc, r, ig, w_oa, w_ob, w_out, ln_g, ln_b, w_s, bias_full, conv_w, wr, wi, lam)


def _bwd_in(dz, x, dh1, w_in_st, g1, jobs=()):
    t = x.shape[0]

    def body(dz_ref, x_ref, dh1_ref, w_ref, g_ref, dx_ref, dg1_ref):
        @pl.when(pl.program_id(0) == 0)
        def _():
            dg1_ref[...] = jnp.zeros_like(dg1_ref)

        dn1 = jnp.zeros((MM_TILE, D_MODEL), F32)
        for k in range(N_CHIPS):
            dn1 = dn1 + _dot_nt(dz_ref[:, k * IN_SHARD:(k + 1) * IN_SHARD], w_ref[k])
        xhat, r1 = _rms(x_ref[...])
        dg1_ref[...] = dg1_ref[...] + _col_sum(dn1 * xhat)
        dx_ref[...] = dh1_ref[...] + _rms_bwd(dn1 * g_ref[...], xhat, r1)

    tile = pl.BlockSpec((MM_TILE, D_MODEL), lambda i: (i, 0))
    return _fused_call(
        body, jobs, name="bwd_in", grid=(t // MM_TILE,),
        in_specs=[pl.BlockSpec((MM_TILE, D_IN), lambda i: (i, 0)), tile, tile,
                  _resident((N_CHIPS, D_MODEL, IN_SHARD)), _const((1, D_MODEL))],
        out_specs=[tile, _const((1, D_MODEL))],
        out_shape=[jax.ShapeDtypeStruct((t, D_MODEL), F32), jax.ShapeDtypeStruct((1, D_MODEL), F32)],
        compiler_params=_params(),
    )(dz, x, dh1, w_in_st, g1)


def _weight_grad(name, a, b, n_blocks, a_varies, b_varies, width, jobs=()):
    t = a.shape[0]
    rows = min(DW_TILE, t)
    n_t = t // rows

    def body(a_ref, b_ref, o_ref, acc_ref):
        s = pl.program_id(1)
        part = _dot_tn(a_ref[...], b_ref[...])

        @pl.when(s == 0)
        def _():
            acc_ref[...] = part

        @pl.when(s > 0)
        def _():
            acc_ref[...] = acc_ref[...] + part

        @pl.when(s == n_t - 1)
        def _():
            o_ref[...] = acc_ref[...].astype(BF16)

    return _fused_call(
        body, jobs, name=name, grid=(n_blocks, n_t),
        in_specs=[pl.BlockSpec((rows, D_MODEL), (lambda j, s: (s, j)) if a_varies else (lambda j, s: (s, 0))),
                  pl.BlockSpec((rows, width), (lambda j, s: (s, j)) if b_varies else (lambda j, s: (s, 0)))],
        out_specs=pl.BlockSpec((None, D_MODEL, width), lambda j, s: (j, 0, 0)),
        out_shape=jax.ShapeDtypeStruct((n_blocks, D_MODEL, width), BF16),
        scratch_shapes=[pltpu.VMEM((D_MODEL, width), F32)],
        compiler_params=_params(2),
    )(a, b)


def _place():
    x, y, c = lax.axis_index("x"), lax.axis_index("y"), lax.axis_index("c")
    other_chips = [(1 - x, y), (x, 1 - y), (1 - x, 1 - y)]
    return x, y, c, other_chips


def _chip_index(px, py):
    return 2 * px + py


ANY = pl.BlockSpec(memory_space=pl.ANY)
SIBLING = ((0, 0, 1),)
NEIGHBOURS = ((1, 0, 0), (0, 1, 0))
OTHER_CHIPS = NEIGHBOURS + ((1, 1, 0),)


def _comm_call(name, jobs):
    return _fused_call(None, jobs, name=name, grid=(), in_specs=[], out_specs=[], out_shape=[])()[1]


def _near_far(x, y, c):
    return (x ^ (1 - c), y ^ c), (x ^ c, y ^ (1 - c))


def _gather_near_job(shards):
    n = len(shards)
    halves = [s.shape[0] // 2 for s in shards]

    def copies(ins, outs, send, recv, local):
        x, y, c, _ = _place()
        near, _ = _near_far(x, y, c)

        def block(w, chip, pc):
            return outs[w].at[_chip_index(*chip), pl.ds(pc * halves[w], halves[w]), :]

        def copy(w, k, chip, pc, to, src=None):
            return pltpu.make_async_remote_copy(
                src_ref=block(w, chip, pc) if src is None else src, dst_ref=block(w, chip, pc),
                send_sem=send.at[2 * w + k], recv_sem=recv.at[2 * w + k], device_id=to, device_id_type=MESH)

        sends, arrivals, own = [], [], []
        for w in range(n):
            src = ins[w].at[pl.ds(c * halves[w], halves[w]), :]
            own.append(pltpu.make_async_copy(src, block(w, (x, y), c), local.at[w]))
            sends += [copy(w, 0, (x, y), c, (*near, c), src), copy(w, 1, (x, y), c, (x, y, 1 - c), src)]
            arrivals += [copy(w, 0, near, c, (x, y, c)), copy(w, 1, (x, y), 1 - c, (x, y, c))]
        return sends, arrivals, own

    return _Job(shards, [jax.ShapeDtypeStruct((N_CHIPS,) + s.shape, s.dtype) for s in shards], 2 * n, copies,
                NEIGHBOURS + SIBLING, n_local=n)


def _gather_far_job(stacked):
    n = len(stacked)
    halves = [s.shape[1] // 2 for s in stacked]

    def copies(ins, outs, send, recv, local):
        del ins, local
        x, y, c, _ = _place()
        near, far = _near_far(x, y, c)

        def copy(w, k, chip):
            blk = outs[w].at[_chip_index(*chip), pl.ds(c * halves[w], halves[w]), :]
            return pltpu.make_async_remote_copy(
                src_ref=blk, dst_ref=blk, send_sem=send.at[2 * w + k], recv_sem=recv.at[2 * w + k],
                device_id=(*far, c), device_id_type=MESH)

        sends = [copy(w, k, chip) for w in range(n) for k, chip in enumerate(((x, y), near))]
        arrivals = [copy(w, k, chip) for w in range(n) for k, chip in enumerate((far, (1 - x, 1 - y)))]
        return sends, arrivals, []

    return _Job(stacked, [jax.ShapeDtypeStruct(s.shape, s.dtype) for s in stacked], 2 * n, copies, NEIGHBOURS,
                aliases={w: w for w in range(n)})


def _gather_pass_job(stacked):
    n = len(stacked)
    halves = [s.shape[1] // 2 for s in stacked]

    def copies(ins, outs, send, recv, local):
        del ins, local
        x, y, c, chips = _place()

        def copy(w, j, chip, pc, to):
            blk = outs[w].at[_chip_index(*chip), pl.ds(pc * halves[w], halves[w]), :]
            return pltpu.make_async_remote_copy(
                src_ref=blk, dst_ref=blk, send_sem=send.at[3 * w + j], recv_sem=recv.at[3 * w + j], device_id=to,
                device_id_type=MESH)

        sends = [copy(w, j, chip, c, (x, y, 1 - c)) for w in range(n) for j, chip in enumerate(chips)]
        arrivals = [copy(w, j, chip, 1 - c, (x, y, c)) for w in range(n) for j, chip in enumerate(chips)]
        return sends, arrivals, []

    return _Job(stacked, [jax.ShapeDtypeStruct(s.shape, s.dtype) for s in stacked], 3 * n, copies, SIBLING,
                aliases={w: w for w in range(n)})


def _gather_small_job(block):
    def copies(ins, outs, send, recv, local):
        x, y, c, chips = _place()

        def copy(j, chip_from, to):
            return pltpu.make_async_remote_copy(
                src_ref=ins[0], dst_ref=outs[0].at[_chip_index(*chip_from)], send_sem=send.at[j],
                recv_sem=recv.at[j], device_id=to, device_id_type=MESH)

        own = [pltpu.make_async_copy(ins[0], outs[0].at[_chip_index(x, y)], local.at[0])]
        sends = [copy(j, (x, y), (*chip, c)) for j, chip in enumerate(chips)]
        arrivals = [copy(j, chip, (x, y, c)) for j, chip in enumerate(chips)]
        return sends, arrivals, own

    return _Job([block], [jax.ShapeDtypeStruct((N_CHIPS,) + block.shape, block.dtype)], 3, copies, OTHER_CHIPS,
                n_local=1)


def _pair_send_job(grads):
    n = len(grads)
    halves = [g.shape[1] // 2 for g in grads]

    def copies(ins, outs, send, recv, local):
        del local
        x, y, c, _ = _place()
        sends = [pltpu.make_async_remote_copy(
            src_ref=ins[w].at[:, pl.ds((1 - c) * halves[w], halves[w]), :], dst_ref=outs[w], send_sem=send.at[w],
            recv_sem=recv.at[w], device_id=(x, y, 1 - c), device_id_type=MESH) for w in range(n)]
        return sends, sends, []

    return _Job(grads, [jax.ShapeDtypeStruct((N_CHIPS, h, g.shape[2]), g.dtype) for g, h in zip(grads, halves)], n,
                copies, SIBLING)


def _row_block(rows, limit=256):
    return min(rows, limit)


def _pair_add(name, core, mine, theirs):
    _, _, h, cols = mine.shape
    rb = _row_block(h, 512)

    def body(core_ref, a_ref, b_ref, o_ref):
        del core_ref
        o_ref[...] = (a_ref[...].astype(F32) + b_ref[...].astype(F32)).astype(BF16)

    return pl.pallas_call(
        body, name=name,
        grid_spec=pltpu.PrefetchScalarGridSpec(
            num_scalar_prefetch=1, grid=(N_CHIPS, h // rb),
            in_specs=[pl.BlockSpec((None, None, rb, cols), lambda k, r, core_ref: (k, core_ref[0], r, 0)),
                      pl.BlockSpec((None, rb, cols), lambda k, r, core_ref: (k, r, 0))],
            out_specs=pl.BlockSpec((None, rb, cols), lambda k, r, core_ref: (k, r, 0))),
        out_shape=jax.ShapeDtypeStruct(theirs.shape, BF16),
        compiler_params=_params(2),
    )(core, mine, theirs)


def _sequencer_call(name, collective_id, job):
    ins = [jax.new_ref(a, memory_space=pltpu.MemorySpace.HBM) for a in job.inputs]
    outs = [ins[{o: i for i, o in job.aliases.items()}[k]] if k in job.aliases.values()
            else jax.empty_ref(shape, memory_space=pltpu.MemorySpace.HBM) for k, shape in enumerate(job.out_shape)]

    @pl.kernel(mesh=plsc.ScalarSubcoreMesh(axis_name="sequencer", num_cores=1), name=name,
               scratch_types=(pltpu.SemaphoreType.DMA((job.n_sem,)), pltpu.SemaphoreType.DMA((job.n_sem,)),
                              pltpu.SemaphoreType.DMA((max(job.n_local, 1),))),
               compiler_params=pltpu.CompilerParams(collective_id=collective_id))
    def launch(send, recv, local):
        x, y, c, _ = _place()
        barrier = pltpu.get_barrier_semaphore()
        for dx, dy, dc in job.peers:
            pl.semaphore_signal(barrier, inc=1, device_id=(x ^ dx, y ^ dy, c ^ dc), device_id_type=MESH)
        pl.semaphore_wait(barrier, len(job.peers))
        sends, arrivals, own = job.copies(ins, outs, send, recv, local)
        for cp in own + sends:
            cp.start()
        for cp in arrivals:
            cp.wait_recv()
        for cp in sends:
            cp.wait_send()
        for cp in own:
            cp.wait()

    launch()
    return [ref[...] for ref in outs]


def _chip_exchange_job(sums):
    n = len(sums)

    def copies(ins, outs, send, recv, local):
        del local
        _, _, c, chips = _place()
        sends = [pltpu.make_async_remote_copy(
            src_ref=ins[w].at[_chip_index(*chip)], dst_ref=outs[w].at[j], send_sem=send.at[3 * w + j],
            recv_sem=recv.at[3 * w + j], device_id=(*chip, c), device_id_type=MESH)
            for w in range(n) for j, chip in enumerate(chips)]
        return sends, sends, []

    return _Job(sums, [jax.ShapeDtypeStruct((N_CHIPS - 1,) + s.shape[1:], s.dtype) for s in sums], 3 * n, copies,
                OTHER_CHIPS)


def _chip_sum(name, place, mine, theirs):
    _, h, cols = mine.shape
    rb = _row_block(h, 512)

    def body(place_ref, p_ref, q_ref, o_ref):
        del place_ref
        acc = p_ref[...].astype(F32)
        for j in range(N_CHIPS - 1):
            acc = acc + q_ref[j].astype(F32)
        o_ref[...] = acc

    return pl.pallas_call(
        body, name=name,
        grid_spec=pltpu.PrefetchScalarGridSpec(
            num_scalar_prefetch=1, grid=(h // rb,),
            in_specs=[pl.BlockSpec((None, rb, cols), lambda r, place_ref: (place_ref[0], r, 0)),
                      pl.BlockSpec((N_CHIPS - 1, rb, cols), lambda r, place_ref: (0, r, 0))],
            out_specs=pl.BlockSpec((None, rb, cols), lambda r, place_ref: (place_ref[1], r, 0))),
        out_shape=jax.ShapeDtypeStruct((2, h, cols), F32),
        compiler_params=_params(),
    )(place, mine, theirs)


def _share_job(bufs):
    n = len(bufs)

    def copies(ins, outs, send, recv, local):
        del ins, local
        x, y, c, _ = _place()

        def copy(w, half):
            return pltpu.make_async_remote_copy(
                src_ref=outs[w].at[half], dst_ref=outs[w].at[half], send_sem=send.at[w], recv_sem=recv.at[w],
                device_id=(x, y, 1 - c), device_id_type=MESH)

        return [copy(w, c) for w in range(n)], [copy(w, 1 - c) for w in range(n)], []

    return _Job(bufs, [jax.ShapeDtypeStruct(b.shape, b.dtype) for b in bufs], n, copies, SIBLING,
                aliases={w: w for w in range(n)})


SMALL_ROWS = 24
ROW_G1, ROW_CW, ROW_CB, ROW_BR, ROW_BI, ROW_LAM, ROW_LG, ROW_LB, ROW_G2, ROW_G3, ROW_LOSS, ROW_BS = (
    0, 1, 5, 6, 7, 8, 9, 10, 11, 12, 13, 16)
N_DEV = 8


def _pack_small(dcw, dcb, dbr, dbi, dlam, dlg, dlb, dg2, dg3, loss, dbs):
    def body(dcw_ref, dcb_ref, dbr_ref, dbi_ref, dlam_ref, dlg_ref, dlb_ref, dg2_ref, dg3_ref, loss_ref, dbs_ref, out):
        out[...] = jnp.zeros((SMALL_ROWS, D_MODEL), F32)
        for row, ref in ((ROW_CB, dcb_ref), (ROW_BR, dbr_ref), (ROW_BI, dbi_ref), (ROW_LAM, dlam_ref),
                         (ROW_LG, dlg_ref), (ROW_LB, dlb_ref), (ROW_G2, dg2_ref), (ROW_G3, dg3_ref)):
            out[row:row + 1, :] = ref[...]
        out[ROW_CW:ROW_CW + CONV_WIDTH, :] = dcw_ref[0:CONV_WIDTH, :]
        out[ROW_LOSS:ROW_LOSS + 1, 0:128] = loss_ref[0:1, :]
        out[ROW_BS:ROW_BS + GROUPS, 0:128] = jnp.transpose(dbs_ref[...])[0:GROUPS, :]

    vm = pl.BlockSpec(memory_space=pltpu.VMEM)
    return pl.pallas_call(
        body, name="pack_small", in_specs=[vm] * 11, out_specs=vm,
        out_shape=jax.ShapeDtypeStruct((SMALL_ROWS, D_MODEL), F32),
    )(dcw, dcb, dbr, dbi, dlam, dlg, dlb, dg2, dg3, loss, dbs)


def _gather_all_job(blocks):
    n = len(blocks)
    flips = [(dx, dy, dc) for dx in (0, 1) for dy in (0, 1) for dc in (0, 1)][1:]

    def copies(ins, outs, send, recv, local):
        x, y, c, _ = _place()
        me = 4 * x + 2 * y + c
        sends, arrivals, own = [], [], []
        for w in range(n):
            own.append(pltpu.make_async_copy(ins[w], outs[w].at[me], local.at[w]))
            for k, (dx, dy, dc) in enumerate(flips):
                peer = (x ^ dx, y ^ dy, c ^ dc)
                sem = dict(send_sem=send.at[7 * w + k], recv_sem=recv.at[7 * w + k])
                sends.append(pltpu.make_async_remote_copy(
                    src_ref=ins[w], dst_ref=outs[w].at[me], device_id=peer, device_id_type=MESH, **sem))
                arrivals.append(pltpu.make_async_remote_copy(
                    src_ref=ins[w], dst_ref=outs[w].at[4 * peer[0] + 2 * peer[1] + peer[2]], device_id=peer,
                    device_id_type=MESH, **sem))
        return sends, arrivals, own

    return _Job(blocks, [jax.ShapeDtypeStruct((N_DEV,) + b.shape, b.dtype) for b in blocks], 7 * n, copies,
                OTHER_CHIPS + SIBLING + tuple((dx, dy, 1) for dx, dy, _ in OTHER_CHIPS), n_local=n)


def _sum_small(vec_all, ws_all, dg1_all):
    def body(vec_ref, ws_ref, dg1_ref, vec_out, ws_out):
        vec, ws, dg1 = vec_ref[0], ws_ref[0], dg1_ref[0]
        for d in range(1, N_DEV):
            vec, ws, dg1 = vec + vec_ref[d], ws + ws_ref[d], dg1 + dg1_ref[d]
        vec_out[...] = vec
        vec_out[ROW_G1:ROW_G1 + 1, :] = dg1
        ws_out[...] = ws

    vm = pl.BlockSpec(memory_space=pltpu.VMEM)
    return pl.pallas_call(
        body, name="sum_small", in_specs=[vm] * 3, out_specs=[vm, vm],
        out_shape=[jax.ShapeDtypeStruct(vec_all.shape[1:], F32), jax.ShapeDtypeStruct(ws_all.shape[1:], F32)],
    )(vec_all, ws_all, dg1_all)


def _adamw_math(w, g, m, v):
    m = ADAM_B1 * m + (1.0 - ADAM_B1) * g
    v = ADAM_B2 * v + (1.0 - ADAM_B2) * (g * g)
    m_hat = m / (1.0 - ADAM_B1 ** ADAM_STEP)
    v_hat = v / (1.0 - ADAM_B2 ** ADAM_STEP)
    delta = (-ADAM_LR) * (m_hat / (jnp.sqrt(v_hat) + ADAM_EPS) + ADAM_WD * w)
    return delta, m, v


def _adamw(name, g, w, m, v, jobs=()):
    rows, cols = w.shape
    rb = _row_block(rows)

    def body(g_ref, w_ref, m_ref, v_ref, d_ref, nm_ref, nv_ref):
        d_ref[...], nm_ref[...], nv_ref[...] = _adamw_math(w_ref[...], g_ref[...], m_ref[...], v_ref[...])

    blk = pl.BlockSpec((rb, cols), lambda r: (r, 0))
    return _fused_call(
        body, jobs, name=name, grid=(rows // rb,), in_specs=[blk] * 4, out_specs=[blk] * 3,
        out_shape=[jax.ShapeDtypeStruct(w.shape, F32)] * 3, compiler_params=_params(),
    )(g, w, m, v)


def _adamw_small(grads, ws, ms, vs):
    n = len(grads)

    def body(*refs):
        g_refs, w_refs, m_refs, v_refs = refs[:n], refs[n:2 * n], refs[2 * n:3 * n], refs[3 * n:4 * n]
        outs = refs[4 * n:]
        for p in range(n):
            d, nm, nv = _adamw_math(w_refs[p][...], g_refs[p][...], m_refs[p][...], v_refs[p][...])
            outs[p][...] = d
            outs[n + p][...] = nm
            outs[2 * n + p][...] = nv

    vm = pl.BlockSpec(memory_space=pltpu.VMEM)
    shapes = [jax.ShapeDtypeStruct(w.shape, F32) for w in ws]
    out = pl.pallas_call(
        body, name="adamw_small", in_specs=[vm] * (4 * n), out_specs=[vm] * (3 * n), out_shape=shapes * 3,
    )(*grads, *ws, *ms, *vs)
    return out[:n], out[n:2 * n], out[2 * n:]


def _unstack_heads(w_st):
    per = HEAD_DIM // N_CHIPS
    return w_st.reshape(N_CHIPS, HEADS, per, HEAD_DIM).transpose(1, 0, 2, 3).reshape(HEADS, HEAD_DIM, HEAD_DIM)


def _stack_heads(w):
    per = HEAD_DIM // N_CHIPS
    return w.reshape(HEADS, N_CHIPS, per, HEAD_DIM).transpose(1, 0, 2, 3).reshape(N_CHIPS, HEADS * per, HEAD_DIM)


def kernel(x, norm_mix_g, w_in, conv_w, conv_b, w_rgate, b_rgate, w_igate, b_igate, lru_lambda, w_out_a, sgu_ln_g, sgu_ln_b, sgu_w_s, sgu_b_s, w_out_b, w_out, norm_mlp_g, w_up, w_down, norm_final_g, loss_target, m_norm_mix_g, m_w_in, m_conv_w, m_conv_b, m_w_rgate, m_b_rgate, m_w_igate, m_b_igate, m_lru_lambda, m_w_out_a, m_sgu_ln_g, m_sgu_ln_b, m_sgu_w_s, m_sgu_b_s, m_w_out_b, m_w_out, m_norm_mlp_g, m_w_up, m_w_down, m_norm_final_g, v_norm_mix_g, v_w_in, v_conv_w, v_conv_b, v_w_rgate, v_b_rgate, v_w_igate, v_b_igate, v_lru_lambda, v_w_out_a, v_sgu_ln_g, v_sgu_ln_b, v_sgu_w_s, v_sgu_b_s, v_w_out_b, v_w_out, v_norm_mlp_g, v_w_up, v_w_down, v_norm_final_g):
    chip = _chip_index(lax.axis_index("x"), lax.axis_index("y"))
    core = lax.axis_index("c")
    quarter_h = HEAD_DIM // N_CHIPS
    quarter_d = D_MODEL // N_CHIPS

    as_2d = lambda a: a.reshape(-1, a.shape[-1])
    big_w = [as_2d(w) for w in (w_in, w_rgate, w_igate, w_out_a, w_out_b, w_out, w_up, w_down)]
    big_m = [as_2d(w) for w in (m_w_in, m_w_rgate, m_w_igate, m_w_out_a, m_w_out_b, m_w_out, m_w_up, m_w_down)]
    big_v = [as_2d(w) for w in (v_w_in, v_w_rgate, v_w_igate, v_w_out_a, v_w_out_b, v_w_out, v_w_up, v_w_down)]

    packed = jnp.concatenate([conv_w[0], b_rgate[0], b_igate[0]], axis=1)
    packed = jnp.concatenate([packed, jnp.zeros_like(packed)], axis=0)
    s_in, s_r, s_i, s_oa, s_ob, s_out, s_up, s_down = [w.astype(BF16) for w in big_w]
    xs, target = x[0], loss_target[0]
    g3 = norm_final_g.reshape(1, D_MODEL)
    bias_s = jnp.broadcast_to(jnp.transpose(sgu_b_s[0])[:, :, None], (CHUNK, GROUPS, GROUP_DIM)).reshape(CHUNK, D_MODEL)
    core_arr = core.reshape(1).astype(jnp.int32)
    place = jnp.stack([chip, core]).astype(jnp.int32)
    quarter = lambda g: g.reshape(N_CHIPS, D_MODEL // N_CHIPS, D_MODEL)

    def pair_add(nm, g, from_sibling):
        return _pair_add("pair_add_" + nm, core_arr, g.reshape(N_CHIPS, 2, g.shape[1] // 2, g.shape[2]), from_sibling)

    def chip_sum(nm, pair, from_chips):
        return _chip_sum("chip_sum_" + nm, place, pair, from_chips)

    order = jnp.stack([chip, chip ^ 2, chip ^ 1, chip ^ 3]).astype(jnp.int32)
    (z, n1, (w_in_st, wr_st, wi_st)), ((packed_all,), late) = _fwd_in(
        xs, norm_mix_g, [s_in, s_r, s_i], order,
        jobs=[_gather_small_job(packed), _gather_near_job([s_oa, s_ob, s_out, s_up, s_down])])
    pick = lambda lo, hi: packed_all[:, :HEADS, lo:hi].transpose(1, 0, 2).reshape(HEADS, -1)
    conv_w_full = pick(0, quarter_d)
    br_full = pick(quarter_d, quarter_d + quarter_h).reshape(1, D_MODEL)
    bi_full = pick(quarter_d + quarter_h, quarter_d + 2 * quarter_h).reshape(1, D_MODEL)
    wr, wi = _unstack_heads(wr_st), _unstack_heads(wi_st)
    lru = (conv_w_full, conv_b, wr, br_full, wi, bi_full, lru_lambda)
    sgu = (sgu_ln_g, sgu_ln_b, sgu_w_s[0], bias_s)

    (ya, *saved), (late,) = _fwd_lru(z, *lru, jobs=[_gather_far_job(late)])
    yb, (late,) = _fwd_sgu(z, *sgu, jobs=[_gather_pass_job(late)])
    w_oa, w_ob, w_o = [w.reshape(D_MODEL, D_MODEL) for w in late[:3]]
    w_up_st, w_dn = late[3], late[4].reshape(D_FF, D_MODEL)
    (pa, pb, h1, n2), _ = _fwd_merge(ya, yb, z, xs, w_oa, w_ob, w_o, norm_mlp_g)
    (act, dup, dh2b, dh1, loss_part, dg3, dg2), _ = _mlp(n2, h1, target, w_up_st, w_dn, norm_mlp_g, g3)

    d_up, _ = _weight_grad("dw_up", n2, dup, N_CHIPS, False, True, D_MODEL)
    d_down, ((r_up,),) = _weight_grad("dw_down", act, dh2b, N_CHIPS, True, False, D_MODEL,
                                      jobs=[_pair_send_job([d_up])])
    (r_down,), = _comm_call("send_w_down", [_pair_send_job([d_down])])
    p_up, p_down = pair_add("w_up", d_up, r_up), pair_add("w_down", d_down, r_down)
    (dz, merged, dpa, dpb, dh1b, dlg, dlb, dws, dbs, dcw, dcb, dwr, dbr, dwi, dbi, dlam), ((q_up, q_down),) = _bwd_mix(
        dh1, pa, pb, z, *saved, w_oa, w_ob, w_o, *sgu, conv_w_full, wr, wi, lru_lambda,
        jobs=[_chip_exchange_job([p_up, p_down])])
    half_up, half_down = chip_sum("w_up", p_up, q_up), chip_sum("w_down", p_down, q_down)
    names = ("w_in", "w_rgate", "w_igate", "w_out_a", "w_out_b", "w_out", "w_up", "w_down")
    d_out, ((full_up, full_down),) = _weight_grad(
        "dw_out", merged, dh1b, 1, False, False, D_MODEL, jobs=[_share_job([half_up, half_down])])
    d_oa, _ = _weight_grad("dw_out_a", ya, dpa, 1, False, False, D_MODEL)
    d_ob, _ = _weight_grad("dw_out_b", yb, dpb, 1, False, False, D_MODEL)
    mids = [quarter(d_oa), quarter(d_ob), quarter(d_out)]
    r_mids = _sequencer_call("send_mids", 1, _pair_send_job(mids))
    gates = [_stack_heads(dwr).astype(BF16), _stack_heads(dwi).astype(BF16)]
    small = _pack_small(dcw, dcb, dbr, dbi, dlam, dlg, dlb, dg2, dg3, loss_part, dbs)
    p_mids = [pair_add(nm, g, r) for nm, g, r in zip(names[3:6], mids, r_mids)]
    q_mids = _sequencer_call("exchange_mids", 2, _chip_exchange_job(p_mids))
    d_in, (r_gates, (vec_all, ws_all)) = _weight_grad(
        "dw_in", n1, dz, N_CHIPS, False, True, IN_SHARD,
        jobs=[_pair_send_job(gates), _gather_all_job([small, dws])])
    r_in, = _sequencer_call("send_w_in", 3, _pair_send_job([d_in]))
    adam_args = {nm: (w, m, v) for nm, w, m, v in zip(names, big_w, big_m, big_v)}

    def adamw(nm, g):
        w, m, v = adam_args[nm]
        g = g.reshape(w.shape)
        return g, _adamw("adamw_" + nm, g, w, m, v)[0]

    p_gates = [pair_add(nm, g, r) for nm, g, r in zip(names[1:3], gates, r_gates)]
    half_mids = [chip_sum(nm, p, q) for nm, p, q in zip(names[3:6], p_mids, q_mids)]
    p_first = [pair_add("w_in", d_in, r_in)] + p_gates
    q_first = _sequencer_call("exchange_w_in", 4, _chip_exchange_job(p_first))
    (grad_x, dg1), _ = _bwd_in(dz, xs, dh1, w_in_st, norm_mix_g)
    half_first = [chip_sum(nm, p, q) for nm, p, q in zip(names[:3], p_first, q_first)]
    full_last = _sequencer_call("share_last", 5, _share_job(half_first + half_mids))
    dg1_all, = _sequencer_call("gather_dg1", 6, _gather_all_job([dg1]))
    full, big_out = [], []
    for nm, f in zip(names, full_last + [full_up, full_down]):
        g, out = adamw(nm, f)
        full.append(g)
        big_out.append(out)

    vec, ws_sum = _sum_small(vec_all, ws_all, dg1_all)
    row = lambda r: vec[r:r + 1]
    shard = lambda a, width: lax.dynamic_slice_in_dim(a, chip * width, width, axis=1)
    g_small = dict(
        norm_mix_g=row(ROW_G1), conv_w=shard(vec[ROW_CW:ROW_CW + CONV_WIDTH], quarter_d), conv_b=row(ROW_CB),
        b_rgate=shard(row(ROW_BR).reshape(HEADS, HEAD_DIM), quarter_h),
        b_igate=shard(row(ROW_BI).reshape(HEADS, HEAD_DIM), quarter_h), lru_lambda=row(ROW_LAM),
        sgu_ln_g=row(ROW_LG), sgu_ln_b=row(ROW_LB),
        sgu_w_s=ws_sum.reshape(CHUNK, GROUPS, CHUNK).transpose(1, 0, 2).reshape(GROUPS * CHUNK, CHUNK),
        sgu_b_s=vec[ROW_BS:ROW_BS + GROUPS, 0:CHUNK], norm_mlp_g=row(ROW_G2), norm_final_g=row(ROW_G3))
    loss = vec[ROW_LOSS, 0]
    small_names = list(g_small)
    given = dict(
        norm_mix_g=(norm_mix_g, m_norm_mix_g, v_norm_mix_g), conv_w=(conv_w, m_conv_w, v_conv_w),
        conv_b=(conv_b, m_conv_b, v_conv_b), b_rgate=(b_rgate, m_b_rgate, v_b_rgate),
        b_igate=(b_igate, m_b_igate, v_b_igate), lru_lambda=(lru_lambda, m_lru_lambda, v_lru_lambda),
        sgu_ln_g=(sgu_ln_g, m_sgu_ln_g, v_sgu_ln_g), sgu_ln_b=(sgu_ln_b, m_sgu_ln_b, v_sgu_ln_b),
        sgu_w_s=(sgu_w_s, m_sgu_w_s, v_sgu_w_s), sgu_b_s=(sgu_b_s, m_sgu_b_s, v_sgu_b_s),
        norm_mlp_g=(norm_mlp_g, m_norm_mlp_g, v_norm_mlp_g), norm_final_g=(norm_final_g, m_norm_final_g, v_norm_final_g))
    g2d = [g_small[nm] for nm in small_names]
    to2d = lambda a, g: a.reshape(g.shape)
    d_s, m_s, v_s = _adamw_small(
        g2d, *[[to2d(given[nm][q], g) for nm, g in zip(small_names, g2d)] for q in range(3)])

    shapes = dict(
        norm_mix_g=norm_mix_g, w_in=w_in, conv_w=conv_w, conv_b=conv_b, w_rgate=w_rgate, b_rgate=b_rgate,
        w_igate=w_igate, b_igate=b_igate, lru_lambda=lru_lambda, w_out_a=w_out_a, sgu_ln_g=sgu_ln_g,
        sgu_ln_b=sgu_ln_b, sgu_w_s=sgu_w_s, sgu_b_s=sgu_b_s, w_out_b=w_out_b, w_out=w_out, norm_mlp_g=norm_mlp_g,
        w_up=w_up, w_down=w_down, norm_final_g=norm_final_g)
    grads, deltas, new_m, new_v = {}, {}, {}, {}
    for nm, g, (d, nmom, nvar) in zip(names, full, big_out):
        grads[nm], deltas[nm], new_m[nm], new_v[nm] = g, d, nmom, nvar
    for p, nm in enumerate(small_names):
        grads[nm], deltas[nm], new_m[nm], new_v[nm] = g2d[p], d_s[p], m_s[p], v_s[p]
    order = list(shapes)
    out = [loss, grad_x[None]]
    for group in (grads, deltas, new_m, new_v):
        out += [group[nm].reshape(shapes[nm].shape) for nm in order]
    return tuple(out)
```

```python
import functools

import jax
import jax.numpy as jnp
from jax import lax
from jax.experimental import pallas as pl
from jax.experimental.pallas import tpu as pltpu
from jax.experimental.pallas import tpu_sc as plsc

F32 = jnp.float32
BF16 = jnp.bfloat16
MESH = pl.DeviceIdType.MESH

D_MODEL = 1024
D_IN = 6 * D_MODEL
D_FF = 4 * D_MODEL
N_CHIPS = 4
IN_SHARD = D_IN // N_CHIPS
HEADS = 4
HEAD_DIM = D_MODEL // HEADS
GROUPS = 4
GROUP_DIM = D_MODEL // GROUPS
CHUNK = 128
CONV_WIDTH = 4
LRU_C = 8.0
NORM_EPS = 1e-6
LN_EPS = 1e-5

ADAM_LR = 0.001
ADAM_B1 = 0.9
ADAM_B2 = 0.999
ADAM_EPS = 1e-08
ADAM_WD = 0.01
ADAM_STEP = 10

SUBLANES = 8
MM_TILE = 512
IN_TILE = 1024
SEQ_TILE = 256
DW_TILE = 2048
VMEM_LIMIT_BYTES = 56 * 1024 * 1024

GELU_K0 = 0.7978845608028654
GELU_K1 = 0.044715


def _params(n_grid_axes=1):
    return pltpu.CompilerParams(
        dimension_semantics=("arbitrary",) * n_grid_axes, vmem_limit_bytes=VMEM_LIMIT_BYTES)


def _resident(shape):
    nd = len(shape)
    return pl.BlockSpec(shape, lambda *_: (0,) * nd, pipeline_mode=pl.Buffered(1))


def _const(shape):
    nd = len(shape)
    return pl.BlockSpec(shape, lambda *_: (0,) * nd)


def _dot(a, b):
    return jnp.dot(a, b, preferred_element_type=F32)


def _dot_nt(a, b):
    return lax.dot_general(a, b, (((1,), (1,)), ((), ())), preferred_element_type=F32)


def _dot_tn(a, b):
    return lax.dot_general(a, b, (((0,), (0,)), ((), ())), preferred_element_type=F32)


def _gelu(x):
    t = jnp.tanh(GELU_K0 * x * (1.0 + GELU_K1 * x * x))
    return 0.5 * x * (1.0 + t)


def _gelu_and_grad(x):
    x2 = x * x
    t = jnp.tanh(GELU_K0 * x * (1.0 + GELU_K1 * x2))
    g = 0.5 * x * (1.0 + t)
    dg = 0.5 * (1.0 + t) + 0.5 * x * (1.0 - t * t) * (GELU_K0 * (1.0 + 3.0 * GELU_K1 * x2))
    return g, dg


def _rms(x):
    r = lax.rsqrt(jnp.mean(x * x, axis=-1, keepdims=True) + NORM_EPS)
    return x * r, r


def _rms_bwd(dn, xhat, r):
    return r * (dn - xhat * jnp.mean(dn * xhat, axis=-1, keepdims=True))


def _col_sum(v):
    return jnp.sum(v, axis=0, keepdims=True)


def _shift_down(x, tail8, k):
    xs = pltpu.roll(x, k, 0)
    ts = pltpu.roll(tail8, k, 0)
    ridx = lax.broadcasted_iota(jnp.int32, tail8.shape, 0)
    head = jnp.where(ridx < k, ts, xs[0:SUBLANES])
    return jnp.concatenate([head, xs[SUBLANES:]], axis=0)


def _shift_up(x, head8, k):
    n = x.shape[0]
    xs = pltpu.roll(x, n - k, 0)
    hs = pltpu.roll(head8, SUBLANES - k, 0)
    ridx = lax.broadcasted_iota(jnp.int32, head8.shape, 0)
    last = jnp.where(ridx >= SUBLANES - k, hs, xs[n - SUBLANES:n])
    return jnp.concatenate([xs[:n - SUBLANES], last], axis=0)


def _scan_forward(a, b, carry):
    n, cols = a.shape
    groups = n // SUBLANES
    a = a.reshape(groups, SUBLANES, cols)
    b = b.reshape(groups, SUBLANES, cols)
    sub = lax.broadcasted_iota(jnp.int32, a.shape, 1)
    for s in (1, 2, 4):
        a_s = pltpu.roll(a, s, 1)
        b_s = pltpu.roll(b, s, 1)
        m = sub >= s
        b = jnp.where(m, a * b_s + b, b)
        a = jnp.where(m, a * a_s, a)
    out = []
    for g in range(groups):
        h = a[g] * carry + b[g]
        out.append(h)
        carry = h[SUBLANES - 1:SUBLANES]
    return jnp.concatenate(out, axis=0), carry


def _scan_backward(a, b, carry):
    n, cols = a.shape
    groups = n // SUBLANES
    a = a.reshape(groups, SUBLANES, cols)
    b = b.reshape(groups, SUBLANES, cols)
    sub = lax.broadcasted_iota(jnp.int32, a.shape, 1)
    for s in (1, 2, 4):
        a_s = pltpu.roll(a, SUBLANES - s, 1)
        b_s = pltpu.roll(b, SUBLANES - s, 1)
        m = sub < SUBLANES - s
        b = jnp.where(m, a * b_s + b, b)
        a = jnp.where(m, a * a_s, a)
    out = [None] * groups
    for g in reversed(range(groups)):
        h = a[g] * carry + b[g]
        out[g] = h
        carry = h[0:1]
    return jnp.concatenate(out, axis=0), carry


def _softplus_neg(lam):
    e = jnp.exp(-jnp.abs(lam))
    u = 1.0 + e
    log1p_e = jnp.where(u == 1.0, e, jnp.log(u) * (e / jnp.where(u == 1.0, 1.0, u - 1.0)))
    return jnp.maximum(-lam, 0.0) + log1p_e


def _lru_gates(xa, tail8, cw_ref, cb_ref, wr_ref, br_ref, wi_ref, bi_ref, lam_ref):
    cw = cw_ref[...]
    xc = cb_ref[...] + cw[0:1] * xa
    for k in range(1, CONV_WIDTH):
        xc = xc + cw[k:k + 1] * _shift_down(xa, tail8, k)
    xcb = xc.astype(BF16)
    pre_r, pre_i = [], []
    for h in range(HEADS):
        cols = slice(h * HEAD_DIM, (h + 1) * HEAD_DIM)
        pre_r.append(_dot(xcb[:, cols], wr_ref[h]))
        pre_i.append(_dot(xcb[:, cols], wi_ref[h]))
    r = jax.nn.sigmoid(jnp.concatenate(pre_r, axis=1) + br_ref[...])
    ig = jax.nn.sigmoid(jnp.concatenate(pre_i, axis=1) + bi_ref[...])
    _, a, mult = _decay(r, lam_ref)
    return xc, r, ig, a, mult


def _decay(r, lam_ref):
    sp = _softplus_neg(lam_ref[...])
    log_a = ((-LRU_C) * sp) * r
    a = jnp.exp(log_a)
    th = jnp.tanh(log_a)
    return sp, a, jnp.sqrt((-2.0 * th) / (1.0 - th))


class _Job:
    def __init__(self, inputs, out_shape, n_sem, copies, peers, aliases=None, n_local=0):
        self.inputs, self.out_shape, self.n_sem, self.copies = list(inputs), list(out_shape), n_sem, copies
        self.aliases, self.n_local = dict(aliases or {}), n_local
        self.peers = tuple(peers)


def _fused_call(body, jobs, *, name, grid, in_specs, out_specs, out_shape, scratch_shapes=(),
                input_output_aliases=None, compiler_params=None, n_prefetch=0, jobs_start_after=None):
    single = not isinstance(out_shape, (list, tuple))
    out_specs = [out_specs] if single else list(out_specs)
    out_shape = [out_shape] if single else list(out_shape)
    n_scr = len(scratch_shapes)
    in_specs, scratch_shapes = list(in_specs), list(scratch_shapes)
    n_in, n_out = len(in_specs), len(out_shape)
    aliases = dict(input_output_aliases or {})
    in_at, out_at = [], []
    for job in jobs:
        in_at.append(len(in_specs))
        out_at.append(len(out_shape))
        for i, o in job.aliases.items():
            aliases[n_prefetch + len(in_specs) + i] = len(out_shape) + o
        in_specs += [ANY] * len(job.inputs)
        out_specs += [ANY] * len(job.out_shape)
        out_shape += job.out_shape
        scratch_shapes += [pltpu.SemaphoreType.DMA((job.n_sem,)), pltpu.SemaphoreType.DMA((job.n_sem,)),
                           pltpu.SemaphoreType.DMA((max(job.n_local, 1),))]
    n_in_all, n_out_all = len(in_specs), len(out_shape)

    def full_body(*refs):
        prefetch, refs = refs[:n_prefetch], refs[n_prefetch:]
        ins, outs, scr = refs[:n_in_all], refs[n_in_all:n_in_all + n_out_all], refs[n_in_all + n_out_all:]

        def copies(q):
            job = jobs[q]
            return job.copies(ins[in_at[q]:in_at[q] + len(job.inputs)], outs[out_at[q]:out_at[q] + len(job.out_shape)],
                              *scr[n_scr + 3 * q:n_scr + 3 * q + 3])

        def start():
            for q in range(len(jobs)):
                sends, _, local = copies(q)
                for cp in local + sends:
                    cp.start()

        def finish():
            every = [copies(q) for q in range(len(jobs))]
            for _, arrivals, _ in every:
                for cp in arrivals:
                    cp.wait_recv()
            for sends, _, local in every:
                for cp in sends:
                    cp.wait_send()
                for cp in local:
                    cp.wait()

        if not grid:
            start()
            finish()
            return
        ids = [pl.program_id(a) for a in range(len(grid))]
        at_step = lambda step: functools.reduce(jnp.logical_and, [i == k for i, k in zip(ids, step)])
        if jobs and jobs_start_after is None:
            pl.when(at_step((0,) * len(grid)))(start)
        body(*prefetch, *ins[:n_in], *outs[:n_out], *scr[:n_scr])
        if jobs and jobs_start_after is not None:
            pl.when(at_step(jobs_start_after))(start)
        if jobs:
            pl.when(functools.reduce(jnp.logical_and, [i == g - 1 for i, g in zip(ids, grid)]))(finish)

    if n_prefetch:
        layout = dict(grid_spec=pltpu.PrefetchScalarGridSpec(
            num_scalar_prefetch=n_prefetch, grid=grid, in_specs=in_specs, out_specs=out_specs,
            scratch_shapes=scratch_shapes))
    else:
        layout = dict(grid=grid, in_specs=in_specs, out_specs=out_specs, scratch_shapes=scratch_shapes)
    call = pl.pallas_call(
        full_body, name=name, out_shape=out_shape, input_output_aliases=aliases, compiler_params=compiler_params,
        **layout)

    def run(*args):
        res = call(*args, *[a for job in jobs for a in job.inputs])
        mine = res[0] if single else list(res[:n_out])
        return mine, [list(res[at:at + len(job.out_shape)]) for at, job in zip(out_at, jobs)]

    return run


def _fwd_in(x, g1, shards, order, jobs=()):
    t = x.shape[0]
    rows_per_step = min(IN_TILE, t)
    n_tiles = t // rows_per_step
    n = len(shards)
    halves = [s.shape[0] // 2 for s in shards]

    def body(order_ref, x_ref, g_ref, *refs):
        del order_ref
        ins, (z_ref, n_ref), outs = refs[:n], refs[n:n + 2], refs[n + 2:2 * n + 2]
        wbuf, nbuf, send, recv, local = refs[2 * n + 2:]
        s, i = pl.program_id(0), pl.program_id(1)
        x_, y_, c, chips = _place()
        k_me = _chip_index(x_, y_)

        def block(w, chip, pc):
            return outs[w].at[_chip_index(*chip), pl.ds(pc * halves[w], halves[w]), :]

        def over_ici(w, j, landing):
            return pltpu.make_async_remote_copy(
                src_ref=ins[w].at[pl.ds(c * halves[w], halves[w]), :],
                dst_ref=block(w, chips[j] if landing else (x_, y_), c), send_sem=send.at[6 * w + j],
                recv_sem=recv.at[6 * w + j], device_id=(*chips[j], c), device_id_type=MESH)

        def to_sibling(w, j, landing):
            blk = block(w, chips[j], 1 - c if landing else c)
            return pltpu.make_async_remote_copy(
                src_ref=blk, dst_ref=blk, send_sem=send.at[6 * w + 3 + j], recv_sem=recv.at[6 * w + 3 + j],
                device_id=(x_, y_, 1 - c), device_id_type=MESH)

        own = [pltpu.make_async_copy(wbuf, outs[0].at[k_me], local.at[0])]
        own += [pltpu.make_async_copy(ins[w], outs[w].at[k_me], local.at[w]) for w in range(1, n)]

        @pl.when((s == 0) & (i == 0))
        def _():
            for j in range(2):
                for w in range(n):
                    over_ici(w, j, False).start()
            load = pltpu.make_async_copy(ins[0], wbuf, local.at[n])
            load.start()
            load.wait()
            for cp in own:
                cp.start()

        for j in range(N_CHIPS - 1):
            @pl.when((s == j + 1) & (i == 0))
            def _(j=j):
                for w in range(n):
                    over_ici(w, j, True).wait_recv()
                for w in range(n):
                    to_sibling(w, j, False).start()
                if j == 0:
                    for w in range(n):
                        over_ici(w, 2, False).start()
                    own[0].wait()
                for w in range(n):
                    to_sibling(w, j, True).wait_recv()
                load = pltpu.make_async_copy(outs[0].at[_chip_index(*chips[j])], wbuf, local.at[n])
                load.start()
                load.wait()

        rows = pl.ds(pl.multiple_of(i * rows_per_step, rows_per_step), rows_per_step)

        @pl.when(s == 0)
        def _():
            xhat, _ = _rms(x_ref[...])
            nrm = (xhat * g_ref[...]).astype(BF16)
            nbuf[rows, :] = nrm
            n_ref[...] = nrm

        z_ref[...] = _dot(nbuf[rows, :], wbuf[...])

        @pl.when((s == N_CHIPS - 1) & (i == n_tiles - 1))
        def _():
            for j in range(N_CHIPS - 1):
                for w in range(n):
                    over_ici(w, j, False).wait_send()
                    to_sibling(w, j, False).wait_send()
            for cp in own[1:]:
                cp.wait()

    once = lambda s, i, order: (jnp.where(s == 0, i, n_tiles - 1), 0)
    (z, n1, *stacked), job_outs = _fused_call(
        body, jobs, name="fwd_in", grid=(N_CHIPS, n_tiles), n_prefetch=1,
        in_specs=[pl.BlockSpec((rows_per_step, D_MODEL), once), _const((1, D_MODEL))] + [ANY] * n,
        out_specs=[pl.BlockSpec((rows_per_step, IN_SHARD), lambda s, i, order: (i, order[s])),
                   pl.BlockSpec((rows_per_step, D_MODEL), once)] + [ANY] * n,
        out_shape=[jax.ShapeDtypeStruct((t, D_IN), F32), jax.ShapeDtypeStruct((t, D_MODEL), BF16)]
        + [jax.ShapeDtypeStruct((N_CHIPS,) + s.shape, s.dtype) for s in shards],
        scratch_shapes=[pltpu.VMEM(shards[0].shape, BF16), pltpu.VMEM((t, D_MODEL), BF16),
                        pltpu.SemaphoreType.DMA((6 * n,)),
                        pltpu.SemaphoreType.DMA((6 * n,)), pltpu.SemaphoreType.DMA((n + 1,))],
        compiler_params=_params(2), jobs_start_after=(1, 0),
    )(order, x, g1, *shards)
    return (z, n1, stacked), job_outs


def _fwd_lru(z, conv_w, conv_b, wr, br, wi, bi, lam, jobs=()):
    t = z.shape[0]

    def body(xa_ref, ga_ref, cw_ref, cb_ref, wr_ref, br_ref, wi_ref, bi_ref, lam_ref, ya_ref, h_ref, xc_ref, r_ref,
             ig_ref, tail_ref, carry_ref):
        @pl.when(pl.program_id(0) == 0)
        def _():
            tail_ref[...] = jnp.zeros_like(tail_ref)
            carry_ref[...] = jnp.zeros_like(carry_ref)

        xa = xa_ref[...]
        xc, r, ig, a, mult = _lru_gates(xa, tail_ref[...], cw_ref, cb_ref, wr_ref, br_ref, wi_ref, bi_ref, lam_ref)
        tail_ref[...] = xa[SEQ_TILE - SUBLANES:]
        xc_ref[...], r_ref[...], ig_ref[...] = xc, r, ig
        h, carry = _scan_forward(a, xc * ig * mult, carry_ref[...])
        carry_ref[...] = carry
        h_ref[...] = h
        ya_ref[...] = (h * _gelu(ga_ref[...])).astype(BF16)

    tile = lambda j: pl.BlockSpec((SEQ_TILE, D_MODEL), lambda i: (i, j))
    return _fused_call(
        body, jobs, name="fwd_lru", grid=(t // SEQ_TILE,),
        in_specs=[tile(0), tile(1), _const((CONV_WIDTH, D_MODEL)), _const((1, D_MODEL)),
                  _resident((HEADS, HEAD_DIM, HEAD_DIM)), _const((1, D_MODEL)),
                  _resident((HEADS, HEAD_DIM, HEAD_DIM)), _const((1, D_MODEL)), _const((1, D_MODEL))],
        out_specs=[tile(0)] * 5,
        out_shape=[jax.ShapeDtypeStruct((t, D_MODEL), BF16)] + [jax.ShapeDtypeStruct((t, D_MODEL), F32)] * 4,
        scratch_shapes=[pltpu.VMEM((SUBLANES, D_MODEL), F32), pltpu.VMEM((1, D_MODEL), F32)],
        compiler_params=_params(),
    )(z, z, conv_w, conv_b, wr, br, wi, bi, lam)


def _sgu_forward_parts(ub, vb, lg_ref, lb_ref):
    u, du = _gelu_and_grad(ub)
    vg, dvg = _gelu_and_grad(vb)
    mu = jnp.mean(vg, axis=-1, keepdims=True)
    d = vg - mu
    rstd = lax.rsqrt(jnp.mean(d * d, axis=-1, keepdims=True) + LN_EPS)
    vhat = d * rstd
    vn = (vhat * lg_ref[...] + lb_ref[...]).astype(BF16)
    return u, du, dvg, rstd, vhat, vn


def _causal_mask():
    rows = lax.broadcasted_iota(jnp.int32, (CHUNK, CHUNK), 0)
    cols = lax.broadcasted_iota(jnp.int32, (CHUNK, CHUNK), 1)
    return rows >= cols


def _fwd_sgu(z, ln_g, ln_b, w_s, bias_full, jobs=()):
    t = z.shape[0]

    def body(ub_ref, vb_ref, lg_ref, lb_ref, ws_ref, bias_ref, yb_ref):
        u, _, _, _, _, vn = _sgu_forward_parts(ub_ref[...], vb_ref[...], lg_ref, lb_ref)
        mask = _causal_mask()
        wm = [jnp.where(mask, ws_ref[g], 0.0).astype(BF16) for g in range(GROUPS)]
        for c in range(SEQ_TILE // CHUNK):
            rows = slice(c * CHUNK, (c + 1) * CHUNK)
            for g in range(GROUPS):
                cols = slice(g * GROUP_DIM, (g + 1) * GROUP_DIM)
                sp = _dot(wm[g], vn[rows, cols]) + bias_ref[:, cols]
                yb_ref[rows, cols] = (u[rows, cols] * sp).astype(BF16)

    tile = lambda j: pl.BlockSpec((SEQ_TILE, D_MODEL), lambda i: (i, j))
    return _fused_call(
        body, jobs, name="fwd_sgu", grid=(t // SEQ_TILE,),
        in_specs=[tile(2), tile(3), _const((1, D_MODEL)), _const((1, D_MODEL)),
                  _const((GROUPS, CHUNK, CHUNK)), _const((CHUNK, D_MODEL))],
        out_specs=tile(0),
        out_shape=jax.ShapeDtypeStruct((t, D_MODEL), BF16),
        compiler_params=_params(),
    )(z, z, ln_g, ln_b, w_s, bias_full)


def _fwd_merge(ya, yb, z, x, w_oa, w_ob, w_out, g2, jobs=()):
    t = x.shape[0]

    def body(ya_ref, yb_ref, m_ref, x_ref, woa_ref, wob_ref, wout_ref, g_ref, pa_ref, pb_ref, h1_ref, n2_ref):
        pa = _dot(ya_ref[...], woa_ref[...])
        pb = _dot(yb_ref[...], wob_ref[...])
        pa_ref[...] = pa
        pb_ref[...] = pb
        merged = jax.nn.sigmoid(m_ref[:, :D_MODEL]) * pa + jax.nn.sigmoid(m_ref[:, D_MODEL:]) * pb
        h1 = x_ref[...] + _dot(merged.astype(BF16), wout_ref[...])
        h1_ref[...] = h1
        xhat, _ = _rms(h1)
        n2_ref[...] = (xhat * g_ref[...]).astype(BF16)

    tile = pl.BlockSpec((MM_TILE, D_MODEL), lambda i: (i, 0))
    sq = _resident((D_MODEL, D_MODEL))
    return _fused_call(
        body, jobs, name="fwd_merge", grid=(t // MM_TILE,),
        in_specs=[tile, tile, pl.BlockSpec((MM_TILE, 2 * D_MODEL), lambda i: (i, 2)), tile, sq, sq, sq,
                  _const((1, D_MODEL))],
        out_specs=[tile, tile, tile, tile],
        out_shape=[jax.ShapeDtypeStruct((t, D_MODEL), F32)] * 3 + [jax.ShapeDtypeStruct((t, D_MODEL), BF16)],
        compiler_params=_params(),
    )(ya, yb, z, x, w_oa, w_ob, w_out, g2)


def _mlp(n2, h1, target, w_up_st, w_down, g2, g3, jobs=()):
    t = n2.shape[0]

    def body(n2_ref, h1_ref, tgt_ref, wup_ref, wdown_ref, g2_ref, g3_ref, act_ref, dup_ref, dh2b_ref, dh1_ref,
             loss_ref, dg3_ref, dg2_ref, relu_ref):
        @pl.when(pl.program_id(0) == 0)
        def _():
            for ref in (loss_ref, dg3_ref, dg2_ref):
                ref[...] = jnp.zeros_like(ref)

        n2 = n2_ref[...]
        h1 = h1_ref[...]
        h2 = h1
        for k in range(N_CHIPS):
            cols = slice(k * D_MODEL, (k + 1) * D_MODEL)
            r = jnp.maximum(_dot(n2, wup_ref[k]), 0.0)
            relu_ref[:, cols] = r
            act = (r * r).astype(BF16)
            act_ref[:, cols] = act
            h2 = h2 + _dot(act, wdown_ref[cols, :])
        xhat, r3 = _rms(h2)
        diff = xhat * g3_ref[...] - tgt_ref[...]
        sq = jnp.sum(diff * diff, axis=1, keepdims=True)
        loss_ref[...] = loss_ref[...] + (0.5 / D_MODEL) * jnp.sum(sq, axis=0, keepdims=True)
        dy = diff * (1.0 / D_MODEL)
        dg3_ref[...] = dg3_ref[...] + _col_sum(dy * xhat)
        dh2 = _rms_bwd(dy * g3_ref[...], xhat, r3)
        dh2b = dh2.astype(BF16)
        dh2b_ref[...] = dh2b
        dn2 = jnp.zeros((SEQ_TILE, D_MODEL), F32)
        for k in range(N_CHIPS):
            cols = slice(k * D_MODEL, (k + 1) * D_MODEL)
            dup = (_dot_nt(dh2b, wdown_ref[cols, :]) * (2.0 * relu_ref[:, cols])).astype(BF16)
            dup_ref[:, cols] = dup
            dn2 = dn2 + _dot_nt(dup, wup_ref[k])
        xhat, r2 = _rms(h1)
        dg2_ref[...] = dg2_ref[...] + _col_sum(dn2 * xhat)
        dh1_ref[...] = dh2 + _rms_bwd(dn2 * g2_ref[...], xhat, r2)

    tile = pl.BlockSpec((SEQ_TILE, D_MODEL), lambda i: (i, 0))
    wide = pl.BlockSpec((SEQ_TILE, D_FF), lambda i: (i, 0))
    vec = _const((1, D_MODEL))
    vec_shape = jax.ShapeDtypeStruct((1, D_MODEL), F32)
    return _fused_call(
        body, jobs, name="mlp", grid=(t // SEQ_TILE,),
        in_specs=[tile, tile, tile, _resident((N_CHIPS, D_MODEL, D_MODEL)), _resident((D_FF, D_MODEL)), vec, vec],
        out_specs=[wide, wide, tile, tile, _const((SUBLANES, 128)), vec, vec],
        out_shape=[jax.ShapeDtypeStruct((t, D_FF), BF16), jax.ShapeDtypeStruct((t, D_FF), BF16),
                   jax.ShapeDtypeStruct((t, D_MODEL), BF16), jax.ShapeDtypeStruct((t, D_MODEL), F32),
                   jax.ShapeDtypeStruct((SUBLANES, 128), F32), vec_shape, vec_shape],
        scratch_shapes=[pltpu.VMEM((SEQ_TILE, D_FF), F32)],
        compiler_params=_params(),
    )(n2, h1, target, w_up_st, w_down, g2, g3)


def _bwd_mix(dh1, pa, pb, z, h, xc, r, ig, w_oa, w_ob, w_out, ln_g, ln_b, w_s, bias_full, conv_w, wr, wi, lam, jobs=()):
    t = dh1.shape[0]
    n_tiles = t // SEQ_TILE
    per_tile = SEQ_TILE // SUBLANES

    def merge_part(dh1_ref, pa_ref, pb_ref, m_ref, woa_ref, wob_ref, wout_ref, dz_ref, dya_ref, dyb_ref, mg_ref,
                   dpa_ref, dpb_ref, dh1b_ref):
        dh1b = dh1_ref[...].astype(BF16)
        dh1b_ref[...] = dh1b
        dm = _dot_nt(dh1b, wout_ref[...])
        pa = pa_ref[...]
        pb = pb_ref[...]
        sa = jax.nn.sigmoid(m_ref[:, :D_MODEL])
        sb = jax.nn.sigmoid(m_ref[:, D_MODEL:])
        mg_ref[...] = (sa * pa + sb * pb).astype(BF16)
        dz_ref[:, :D_MODEL] = (dm * pa * sa * (1.0 - sa)).astype(BF16)
        dz_ref[:, D_MODEL:] = (dm * pb * sb * (1.0 - sb)).astype(BF16)
        dpa = (dm * sa).astype(BF16)
        dpb = (dm * sb).astype(BF16)
        dpa_ref[...] = dpa
        dpb_ref[...] = dpb
        dya_ref[...] = _dot_nt(dpa, woa_ref[...])
        dyb_ref[...] = _dot_nt(dpb, wob_ref[...])

    def sgu_part(dyb_ref, ub_ref, vb_ref, lg_ref, lb_ref, ws_ref, bias_ref, dz_ref, dlg_ref, dlb_ref, dws_ref, dbs_ref,
                 dvn_ref, dsp_acc):
        i = pl.program_id(0)

        @pl.when(i == 0)
        def _():
            dlg_ref[...] = jnp.zeros_like(dlg_ref)
            dlb_ref[...] = jnp.zeros_like(dlb_ref)
            dws_ref[...] = jnp.zeros_like(dws_ref)
            dsp_acc[...] = jnp.zeros_like(dsp_acc)

        u, du, dvg, rstd, vhat, vn = _sgu_forward_parts(ub_ref[...], vb_ref[...], lg_ref, lb_ref)
        dyb = dyb_ref[...]
        mask = _causal_mask()
        wm = [jnp.where(mask, ws_ref[g], 0.0).astype(BF16) for g in range(GROUPS)]
        for c in range(SEQ_TILE // CHUNK):
            rows = slice(c * CHUNK, (c + 1) * CHUNK)
            for g in range(GROUPS):
                cols = slice(g * GROUP_DIM, (g + 1) * GROUP_DIM)
                vn_blk = vn[rows, cols]
                sp = _dot(wm[g], vn_blk) + bias_ref[:, cols]
                dyb_blk = dyb[rows, cols]
                dz_ref[rows, cols] = (dyb_blk * sp * du[rows, cols]).astype(BF16)
                dsp = dyb_blk * u[rows, cols]
                dsp_acc[:, cols] = dsp_acc[:, cols] + dsp
                dspb = dsp.astype(BF16)
                dvn_ref[rows, cols] = _dot_tn(wm[g], dspb)
                wcols = slice(g * CHUNK, (g + 1) * CHUNK)
                dws_ref[:, wcols] = dws_ref[:, wcols] + jnp.where(mask, _dot_nt(dspb, vn_blk), 0.0)
        dvn = dvn_ref[...]
        dlg_ref[...] = dlg_ref[...] + _col_sum(dvn * vhat)
        dlb_ref[...] = dlb_ref[...] + _col_sum(dvn)
        dvhat = dvn * lg_ref[...]
        dvgel = rstd * (dvhat - jnp.mean(dvhat, axis=-1, keepdims=True)
                        - vhat * jnp.mean(dvhat * vhat, axis=-1, keepdims=True))
        dz_ref[:, D_MODEL:] = (dvgel * dvg).astype(BF16)

        @pl.when(i == n_tiles - 1)
        def _():
            lane = lax.broadcasted_iota(jnp.int32, (CHUNK, 128), 1)
            out = jnp.zeros((CHUNK, 128), F32)
            for g in range(GROUPS):
                s = jnp.sum(dsp_acc[:, g * GROUP_DIM:(g + 1) * GROUP_DIM], axis=1, keepdims=True)
                out = out + jnp.where(lane == g, s, 0.0)
            dbs_ref[...] = out

    def lru_part(dya_ref, xa_ref, ga_ref, h_ref, h_prev_ref, xc_ref, r_ref, ig_ref, cw_ref, wr_ref, wi_ref, lam_ref,
                 dz_ref, dcw_ref, dcb_ref, dwr_ref, dbr_ref, dwi_ref, dbi_ref, dlam_ref, lam_carry, dxc_head):
        i = pl.program_id(0)

        @pl.when(i == 0)
        def _():
            for ref in (dcw_ref, dcb_ref, dwr_ref, dbr_ref, dwi_ref, dbi_ref, dlam_ref, lam_carry, dxc_head):
                ref[...] = jnp.zeros_like(ref)

        first_tile = i == n_tiles - 1
        h_tail = jnp.where(first_tile, 0.0, h_prev_ref[...])
        xc, r, ig = xc_ref[...], r_ref[...], ig_ref[...]
        xcb = xc.astype(BF16)
        sp, a, mult = _decay(r, lam_ref)
        h = h_ref[...]
        h_prev = _shift_down(h, h_tail, 1)
        dya = dya_ref[...]
        gg, dgg = _gelu_and_grad(ga_ref[...])
        dz_ref[:, D_MODEL:] = (dya * h * dgg).astype(BF16)
        ones = jnp.ones((SUBLANES, D_MODEL), F32)
        lam_t, lam_first = _scan_backward(_shift_up(a, ones, 1), dya * gg, lam_carry[...])
        lam_carry[...] = a[0:1] * lam_first
        dmult = lam_t * xc * ig
        dla = lam_t * h_prev * a - dmult * (a * a) / mult
        dr = dla * ((-LRU_C) * sp)
        dlam_ref[...] = dlam_ref[...] + _col_sum(dla * r) * (LRU_C * jax.nn.sigmoid(-lam_ref[...]))
        dpr = dr * r * (1.0 - r)
        dpi = lam_t * xc * mult * ig * (1.0 - ig)
        dbr_ref[...] = dbr_ref[...] + _col_sum(dpr)
        dbi_ref[...] = dbi_ref[...] + _col_sum(dpi)
        dprb = dpr.astype(BF16)
        dpib = dpi.astype(BF16)
        dxc_gate = []
        for hd in range(HEADS):
            cols = slice(hd * HEAD_DIM, (hd + 1) * HEAD_DIM)
            dxc_gate.append(_dot_nt(dprb[:, cols], wr_ref[hd]) + _dot_nt(dpib[:, cols], wi_ref[hd]))
            dwr_ref[hd] = dwr_ref[hd] + _dot_tn(xcb[:, cols], dprb[:, cols])
            dwi_ref[hd] = dwi_ref[hd] + _dot_tn(xcb[:, cols], dpib[:, cols])
        dxc = lam_t * ig * mult + jnp.concatenate(dxc_gate, axis=1)
        dcb_ref[...] = dcb_ref[...] + _col_sum(dxc)
        cw = cw_ref[...]
        head = dxc_head[...]
        xa = xa_ref[...]
        dxa = cw[0:1] * dxc
        dcw_ref[0:1, :] = dcw_ref[0:1, :] + _col_sum(dxc * xa)
        for k in range(1, CONV_WIDTH):
            dxc_k = _shift_up(dxc, head, k)
            dxa = dxa + cw[k:k + 1] * dxc_k
            dcw_ref[k:k + 1, :] = dcw_ref[k:k + 1, :] + _col_sum(dxc_k * xa)
        dxc_head[...] = dxc[0:SUBLANES]
        dz_ref[:, :D_MODEL] = dxa.astype(BF16)

    def body(dh1_ref, pa_ref, pb_ref, z_ref, h_ref, h_prev_ref, xc_ref, r_ref, ig_ref, woa_ref, wob_ref, wout_ref,
             lg_ref, lb_ref, ws_ref, bias_ref, cw_ref, wr_ref, wi_ref, lam_ref, dz_ref, mg_ref, dpa_ref, dpb_ref,
             dh1b_ref, dlg_ref, dlb_ref, dws_ref, dbs_ref, dcw_ref, dcb_ref, dwr_ref, dbr_ref, dwi_ref, dbi_ref,
             dlam_ref, dya_ref, dyb_ref, dvn_ref, dsp_acc, lam_carry, dxc_head):
        def cols(ref, first, count):
            return ref.at[:, pl.ds(first * D_MODEL, count * D_MODEL)]

        merge_part(dh1_ref, pa_ref, pb_ref, cols(z_ref, 4, 2), woa_ref, wob_ref, wout_ref, cols(dz_ref, 4, 2), dya_ref,
                   dyb_ref, mg_ref, dpa_ref, dpb_ref, dh1b_ref)
        sgu_part(dyb_ref, cols(z_ref, 2, 1), cols(z_ref, 3, 1), lg_ref, lb_ref, ws_ref, bias_ref, cols(dz_ref, 2, 2),
                 dlg_ref, dlb_ref, dws_ref, dbs_ref, dvn_ref, dsp_acc)
        lru_part(dya_ref, cols(z_ref, 0, 1), cols(z_ref, 1, 1), h_ref, h_prev_ref, xc_ref, r_ref, ig_ref, cw_ref, wr_ref,
                 wi_ref, lam_ref, cols(dz_ref, 0, 2), dcw_ref, dcb_ref, dwr_ref, dbr_ref, dwi_ref, dbi_ref, dlam_ref,
                 lam_carry, dxc_head)

    rev = lambda i: n_tiles - 1 - i
    tile = pl.BlockSpec((SEQ_TILE, D_MODEL), lambda i: (rev(i), 0))
    row = pl.BlockSpec((SEQ_TILE, D_IN), lambda i: (rev(i), 0))
    prev8 = pl.BlockSpec((SUBLANES, D_MODEL), lambda i: (jnp.maximum(rev(i) * per_tile - 1, 0), 0))
    vec = _const((1, D_MODEL))
    sq = _resident((D_MODEL, D_MODEL))
    gate_w = _resident((HEADS, HEAD_DIM, HEAD_DIM))
    gate_acc = _const((HEADS, HEAD_DIM, HEAD_DIM))
    vec_shape = jax.ShapeDtypeStruct((1, D_MODEL), F32)
    gate_shape = jax.ShapeDtypeStruct((HEADS, HEAD_DIM, HEAD_DIM), F32)
    act_bf = jax.ShapeDtypeStruct((t, D_MODEL), BF16)
    return _fused_call(
        body, jobs, name="bwd_mix", grid=(n_tiles,),
        in_specs=[tile, tile, tile, row, tile, prev8, tile, tile, tile, sq, sq, sq, vec, vec,
                  _const((GROUPS, CHUNK, CHUNK)), _const((CHUNK, D_MODEL)), _const((CONV_WIDTH, D_MODEL)), gate_w, gate_w,
                  vec],
        out_specs=[row, tile, tile, tile, tile, vec, vec, _const((CHUNK, GROUPS * CHUNK)), _const((CHUNK, 128)),
                   _const((SUBLANES, D_MODEL)), vec, gate_acc, vec, gate_acc, vec, vec],
        out_shape=[jax.ShapeDtypeStruct((t, D_IN), BF16), act_bf, act_bf, act_bf, act_bf, vec_shape, vec_shape,
                   jax.ShapeDtypeStruct((CHUNK, GROUPS * CHUNK), F32), jax.ShapeDtypeStruct((CHUNK, 128), F32),
                   jax.ShapeDtypeStruct((SUBLANES, D_MODEL), F32), vec_shape, gate_shape, vec_shape, gate_shape,
                   vec_shape, vec_shape],
        scratch_shapes=[pltpu.VMEM((SEQ_TILE, D_MODEL), F32), pltpu.VMEM((SEQ_TILE, D_MODEL), F32),
                        pltpu.VMEM((SEQ_TILE, D_MODEL), F32), pltpu.VMEM((CHUNK, D_MODEL), F32),
                        pltpu.VMEM((1, D_MODEL), F32), pltpu.VMEM((SUBLANES, D_MODEL), F32)],
        compiler_params=_params(),
    )(dh1, pa, pb, z, h, h, xc, r, ig, w_oa, w_ob, w_out, ln_g, ln_b, w_s, bias_full, conv_w, wr, wi, lam)


def _bwd_in(dz, x, dh1, w_in_st, g1, jobs=()):
    t = x.shape[0]

    def body(dz_ref, x_ref, dh1_ref, w_ref, g_ref, dx_ref, dg1_ref):
        @pl.when(pl.program_id(0) == 0)
        def _():
            dg1_ref[...] = jnp.zeros_like(dg1_ref)

        dn1 = jnp.zeros((MM_TILE, D_MODEL), F32)
        for k in range(N_CHIPS):
            dn1 = dn1 + _dot_nt(dz_ref[:, k * IN_SHARD:(k + 1) * IN_SHARD], w_ref[k])
        xhat, r1 = _rms(x_ref[...])
        dg1_ref[...] = dg1_ref[...] + _col_sum(dn1 * xhat)
        dx_ref[...] = dh1_ref[...] + _rms_bwd(dn1 * g_ref[...], xhat, r1)

    tile = pl.BlockSpec((MM_TILE, D_MODEL), lambda i: (i, 0))
    return _fused_call(
        body, jobs, name="bwd_in", grid=(t // MM_TILE,),
        in_specs=[pl.BlockSpec((MM_TILE, D_IN), lambda i: (i, 0)), tile, tile,
                  _resident((N_CHIPS, D_MODEL, IN_SHARD)), _const((1, D_MODEL))],
        out_specs=[tile, _const((1, D_MODEL))],
        out_shape=[jax.ShapeDtypeStruct((t, D_MODEL), F32), jax.ShapeDtypeStruct((1, D_MODEL), F32)],
        compiler_params=_params(),
    )(dz, x, dh1, w_in_st, g1)


def _weight_grad(name, a, b, n_blocks, a_varies, b_varies, width, jobs=()):
    t = a.shape[0]
    rows = min(DW_TILE, t)
    n_t = t // rows

    def body(a_ref, b_ref, o_ref, acc_ref):
        s = pl.program_id(1)
        part = _dot_tn(a_ref[...], b_ref[...])

        @pl.when(s == 0)
        def _():
            acc_ref[...] = part

        @pl.when(s > 0)
        def _():
            acc_ref[...] = acc_ref[...] + part

        @pl.when(s == n_t - 1)
        def _():
            o_ref[...] = acc_ref[...].astype(BF16)

    return _fused_call(
        body, jobs, name=name, grid=(n_blocks, n_t),
        in_specs=[pl.BlockSpec((rows, D_MODEL), (lambda j, s: (s, j)) if a_varies else (lambda j, s: (s, 0))),
                  pl.BlockSpec((rows, width), (lambda j, s: (s, j)) if b_varies else (lambda j, s: (s, 0)))],
        out_specs=pl.BlockSpec((None, D_MODEL, width), lambda j, s: (j, 0, 0)),
        out_shape=jax.ShapeDtypeStruct((n_blocks, D_MODEL, width), BF16),
        scratch_shapes=[pltpu.VMEM((D_MODEL, width), F32)],
        compiler_params=_params(2),
    )(a, b)


def _place():
    x, y, c = lax.axis_index("x"), lax.axis_index("y"), lax.axis_index("c")
    other_chips = [(1 - x, y), (x, 1 - y), (1 - x, 1 - y)]
    return x, y, c, other_chips


def _chip_index(px, py):
    return 2 * px + py


ANY = pl.BlockSpec(memory_space=pl.ANY)
SIBLING = ((0, 0, 1),)
NEIGHBOURS = ((1, 0, 0), (0, 1, 0))
OTHER_CHIPS = NEIGHBOURS + ((1, 1, 0),)


def _comm_call(name, jobs):
    return _fused_call(None, jobs, name=name, grid=(), in_specs=[], out_specs=[], out_shape=[])()[1]


def _near_far(x, y, c):
    return (x ^ (1 - c), y ^ c), (x ^ c, y ^ (1 - c))


def _gather_near_job(shards):
    n = len(shards)
    halves = [s.shape[0] // 2 for s in shards]

    def copies(ins, outs, send, recv, local):
        x, y, c, _ = _place()
        near, _ = _near_far(x, y, c)

        def block(w, chip, pc):
            return outs[w].at[_chip_index(*chip), pl.ds(pc * halves[w], halves[w]), :]

        def copy(w, k, chip, pc, to, src=None):
            return pltpu.make_async_remote_copy(
                src_ref=block(w, chip, pc) if src is None else src, dst_ref=block(w, chip, pc),
                send_sem=send.at[2 * w + k], recv_sem=recv.at[2 * w + k], device_id=to, device_id_type=MESH)

        sends, arrivals, own = [], [], []
        for w in range(n):
            src = ins[w].at[pl.ds(c * halves[w], halves[w]), :]
            own.append(pltpu.make_async_copy(src, block(w, (x, y), c), local.at[w]))
            sends += [copy(w, 0, (x, y), c, (*near, c), src), copy(w, 1, (x, y), c, (x, y, 1 - c), src)]
            arrivals += [copy(w, 0, near, c, (x, y, c)), copy(w, 1, (x, y), 1 - c, (x, y, c))]
        return sends, arrivals, own

    return _Job(shards, [jax.ShapeDtypeStruct((N_CHIPS,) + s.shape, s.dtype) for s in shards], 2 * n, copies,
                NEIGHBOURS + SIBLING, n_local=n)


def _gather_far_job(stacked):
    n = len(stacked)
    halves = [s.shape[1] // 2 for s in stacked]

    def copies(ins, outs, send, recv, local):
        del ins, local
        x, y, c, _ = _place()
        near, far = _near_far(x, y, c)

        def copy(w, k, chip):
            blk = outs[w].at[_chip_index(*chip), pl.ds(c * halves[w], halves[w]), :]
            return pltpu.make_async_remote_copy(
                src_ref=blk, dst_ref=blk, send_sem=send.at[2 * w + k], recv_sem=recv.at[2 * w + k],
                device_id=(*far, c), device_id_type=MESH)

        sends = [copy(w, k, chip) for w in range(n) for k, chip in enumerate(((x, y), near))]
        arrivals = [copy(w, k, chip) for w in range(n) for k, chip in enumerate((far, (1 - x, 1 - y)))]
        return sends, arrivals, []

    return _Job(stacked, [jax.ShapeDtypeStruct(s.shape, s.dtype) for s in stacked], 2 * n, copies, NEIGHBOURS,
                aliases={w: w for w in range(n)})


def _gather_pass_job(stacked):
    n = len(stacked)
    halves = [s.shape[1] // 2 for s in stacked]

    def copies(ins, outs, send, recv, local):
        del ins, local
        x, y, c, chips = _place()

        def copy(w, j, chip, pc, to):
            blk = outs[w].at[_chip_index(*chip), pl.ds(pc * halves[w], halves[w]), :]
            return pltpu.make_async_remote_copy(
                src_ref=blk, dst_ref=blk, send_sem=send.at[3 * w + j], recv_sem=recv.at[3 * w + j], device_id=to,
                device_id_type=MESH)

        sends = [copy(w, j, chip, c, (x, y, 1 - c)) for w in range(n) for j, chip in enumerate(chips)]
        arrivals = [copy(w, j, chip, 1 - c, (x, y, c)) for w in range(n) for j, chip in enumerate(chips)]
        return sends, arrivals, []

    return _Job(stacked, [jax.ShapeDtypeStruct(s.shape, s.dtype) for s in stacked], 3 * n, copies, SIBLING,
                aliases={w: w for w in range(n)})


def _gather_small_job(block):
    def copies(ins, outs, send, recv, local):
        x, y, c, chips = _place()

        def copy(j, chip_from, to):
            return pltpu.make_async_remote_copy(
                src_ref=ins[0], dst_ref=outs[0].at[_chip_index(*chip_from)], send_sem=send.at[j],
                recv_sem=recv.at[j], device_id=to, device_id_type=MESH)

        own = [pltpu.make_async_copy(ins[0], outs[0].at[_chip_index(x, y)], local.at[0])]
        sends = [copy(j, (x, y), (*chip, c)) for j, chip in enumerate(chips)]
        arrivals = [copy(j, chip, (x, y, c)) for j, chip in enumerate(chips)]
        return sends, arrivals, own

    return _Job([block], [jax.ShapeDtypeStruct((N_CHIPS,) + block.shape, block.dtype)], 3, copies, OTHER_CHIPS,
                n_local=1)


def _pair_send_job(grads):
    n = len(grads)
    halves = [g.shape[1] // 2 for g in grads]

    def copies(ins, outs, send, recv, local):
        del local
        x, y, c, _ = _place()
        sends = [pltpu.make_async_remote_copy(
            src_ref=ins[w].at[:, pl.ds((1 - c) * halves[w], halves[w]), :], dst_ref=outs[w], send_sem=send.at[w],
            recv_sem=recv.at[w], device_id=(x, y, 1 - c), device_id_type=MESH) for w in range(n)]
        return sends, sends, []

    return _Job(grads, [jax.ShapeDtypeStruct((N_CHIPS, h, g.shape[2]), g.dtype) for g, h in zip(grads, halves)], n,
                copies, SIBLING)


def _row_block(rows, limit=256):
    return min(rows, limit)


def _pair_add(name, core, mine, theirs):
    _, _, h, cols = mine.shape
    rb = _row_block(h, 512)

    def body(core_ref, a_ref, b_ref, o_ref):
        del core_ref
        o_ref[...] = (a_ref[...].astype(F32) + b_ref[...].astype(F32)).astype(BF16)

    return pl.pallas_call(
        body, name=name,
        grid_spec=pltpu.PrefetchScalarGridSpec(
            num_scalar_prefetch=1, grid=(N_CHIPS, h // rb),
            in_specs=[pl.BlockSpec((None, None, rb, cols), lambda k, r, core_ref: (k, core_ref[0], r, 0)),
                      pl.BlockSpec((None, rb, cols), lambda k, r, core_ref: (k, r, 0))],
            out_specs=pl.BlockSpec((None, rb, cols), lambda k, r, core_ref: (k, r, 0))),
        out_shape=jax.ShapeDtypeStruct(theirs.shape, BF16),
        compiler_params=_params(2),
    )(core, mine, theirs)


def _sequencer_call(name, collective_id, job):
    ins = [jax.new_ref(a, memory_space=pltpu.MemorySpace.HBM) for a in job.inputs]
    outs = [ins[{o: i for i, o in job.aliases.items()}[k]] if k in job.aliases.values()
            else jax.empty_ref(shape, memory_space=pltpu.MemorySpace.HBM) for k, shape in enumerate(job.out_shape)]

    @pl.kernel(mesh=plsc.ScalarSubcoreMesh(axis_name="sequencer", num_cores=1), name=name,
               scratch_types=(pltpu.SemaphoreType.DMA((job.n_sem,)), pltpu.SemaphoreType.DMA((job.n_sem,)),
                              pltpu.SemaphoreType.DMA((max(job.n_local, 1),))),
               compiler_params=pltpu.CompilerParams(collective_id=collective_id))
    def launch(send, recv, local):
        x, y, c, _ = _place()
        barrier = pltpu.get_barrier_semaphore()
        for dx, dy, dc in job.peers:
            pl.semaphore_signal(barrier, inc=1, device_id=(x ^ dx, y ^ dy, c ^ dc), device_id_type=MESH)
        pl.semaphore_wait(barrier, len(job.peers))
        sends, arrivals, own = job.copies(ins, outs, send, recv, local)
        for cp in own + sends:
            cp.start()
        for cp in arrivals:
            cp.wait_recv()
        for cp in sends:
            cp.wait_send()
        for cp in own:
            cp.wait()

    launch()
    return [ref[...] for ref in outs]


def _chip_exchange_job(sums):
    n = len(sums)

    def copies(ins, outs, send, recv, local):
        del local
        _, _, c, chips = _place()
        sends = [pltpu.make_async_remote_copy(
            src_ref=ins[w].at[_chip_index(*chip)], dst_ref=outs[w].at[j], send_sem=send.at[3 * w + j],
            recv_sem=recv.at[3 * w + j], device_id=(*chip, c), device_id_type=MESH)
            for w in range(n) for j, chip in enumerate(chips)]
        return sends, sends, []

    return _Job(sums, [jax.ShapeDtypeStruct((N_CHIPS - 1,) + s.shape[1:], s.dtype) for s in sums], 3 * n, copies,
                OTHER_CHIPS)


def _chip_sum(name, place, mine, theirs):
    _, h, cols = mine.shape
    rb = _row_block(h, 512)

    def body(place_ref, p_ref, q_ref, o_ref):
        del place_ref
        acc = p_ref[...].astype(F32)
        for j in range(N_CHIPS - 1):
            acc = acc + q_ref[j].astype(F32)
        o_ref[...] = acc

    return pl.pallas_call(
        body, name=name,
        grid_spec=pltpu.PrefetchScalarGridSpec(
            num_scalar_prefetch=1, grid=(h // rb,),
            in_specs=[pl.BlockSpec((None, rb, cols), lambda r, place_ref: (place_ref[0], r, 0)),
                      pl.BlockSpec((N_CHIPS - 1, rb, cols), lambda r, place_ref: (0, r, 0))],
            out_specs=pl.BlockSpec((None, rb, cols), lambda r, place_ref: (place_ref[1], r, 0))),
        out_shape=jax.ShapeDtypeStruct((2, h, cols), F32),
        compiler_params=_params(),
    )(place, mine, theirs)


def _share_job(bufs):
    n = len(bufs)

    def copies(ins, outs, send, recv, local):
        del ins, local
        x, y, c, _ = _place()

        def copy(w, half):
            return pltpu.make_async_remote_copy(
                src_ref=outs[w].at[half], dst_ref=outs[w].at[half], send_sem=send.at[w], recv_sem=recv.at[w],
                device_id=(x, y, 1 - c), device_id_type=MESH)

        return [copy(w, c) for w in range(n)], [copy(w, 1 - c) for w in range(n)], []

    return _Job(bufs, [jax.ShapeDtypeStruct(b.shape, b.dtype) for b in bufs], n, copies, SIBLING,
                aliases={w: w for w in range(n)})


SMALL_ROWS = 24
ROW_G1, ROW_CW, ROW_CB, ROW_BR, ROW_BI, ROW_LAM, ROW_LG, ROW_LB, ROW_G2, ROW_G3, ROW_LOSS, ROW_BS = (
    0, 1, 5, 6, 7, 8, 9, 10, 11, 12, 13, 16)
N_DEV = 8


def _pack_small(dcw, dcb, dbr, dbi, dlam, dlg, dlb, dg2, dg3, loss, dbs):
    def body(dcw_ref, dcb_ref, dbr_ref, dbi_ref, dlam_ref, dlg_ref, dlb_ref, dg2_ref, dg3_ref, loss_ref, dbs_ref, out):
        out[...] = jnp.zeros((SMALL_ROWS, D_MODEL), F32)
        for row, ref in ((ROW_CB, dcb_ref), (ROW_BR, dbr_ref), (ROW_BI, dbi_ref), (ROW_LAM, dlam_ref),
                         (ROW_LG, dlg_ref), (ROW_LB, dlb_ref), (ROW_G2, dg2_ref), (ROW_G3, dg3_ref)):
            out[row:row + 1, :] = ref[...]
        out[ROW_CW:ROW_CW + CONV_WIDTH, :] = dcw_ref[0:CONV_WIDTH, :]
        out[ROW_LOSS:ROW_LOSS + 1, 0:128] = loss_ref[0:1, :]
        out[ROW_BS:ROW_BS + GROUPS, 0:128] = jnp.transpose(dbs_ref[...])[0:GROUPS, :]

    vm = pl.BlockSpec(memory_space=pltpu.VMEM)
    return pl.pallas_call(
        body, name="pack_small", in_specs=[vm] * 11, out_specs=vm,
        out_shape=jax.ShapeDtypeStruct((SMALL_ROWS, D_MODEL), F32),
    )(dcw, dcb, dbr, dbi, dlam, dlg, dlb, dg2, dg3, loss, dbs)


def _gather_all_job(blocks):
    n = len(blocks)
    flips = [(dx, dy, dc) for dx in (0, 1) for dy in (0, 1) for dc in (0, 1)][1:]

    def copies(ins, outs, send, recv, local):
        x, y, c, _ = _place()
        me = 4 * x + 2 * y + c
        sends, arrivals, own = [], [], []
        for w in range(n):
            own.append(pltpu.make_async_copy(ins[w], outs[w].at[me], local.at[w]))
            for k, (dx, dy, dc) in enumerate(flips):
                peer = (x ^ dx, y ^ dy, c ^ dc)
                sem = dict(send_sem=send.at[7 * w + k], recv_sem=recv.at[7 * w + k])
                sends.append(pltpu.make_async_remote_copy(
                    src_ref=ins[w], dst_ref=outs[w].at[me], device_id=peer, device_id_type=MESH, **sem))
                arrivals.append(pltpu.make_async_remote_copy(
                    src_ref=ins[w], dst_ref=outs[w].at[4 * peer[0] + 2 * peer[1] + peer[2]], device_id=peer,
                    device_id_type=MESH, **sem))
        return sends, arrivals, own

    return _Job(blocks, [jax.ShapeDtypeStruct((N_DEV,) + b.shape, b.dtype) for b in blocks], 7 * n, copies,
                OTHER_CHIPS + SIBLING + tuple((dx, dy, 1) for dx, dy, _ in OTHER_CHIPS), n_local=n)


def _sum_small(vec_all, ws_all, dg1_all):
    def body(vec_ref, ws_ref, dg1_ref, vec_out, ws_out):
        vec, ws, dg1 = vec_ref[0], ws_ref[0], dg1_ref[0]
        for d in range(1, N_DEV):
            vec, ws, dg1 = vec + vec_ref[d], ws + ws_ref[d], dg1 + dg1_ref[d]
        vec_out[...] = vec
        vec_out[ROW_G1:ROW_G1 + 1, :] = dg1
        ws_out[...] = ws

    vm = pl.BlockSpec(memory_space=pltpu.VMEM)
    return pl.pallas_call(
        body, name="sum_small", in_specs=[vm] * 3, out_specs=[vm, vm],
        out_shape=[jax.ShapeDtypeStruct(vec_all.shape[1:], F32), jax.ShapeDtypeStruct(ws_all.shape[1:], F32)],
    )(vec_all, ws_all, dg1_all)


def _adamw_math(w, g, m, v):
    m = ADAM_B1 * m + (1.0 - ADAM_B1) * g
    v = ADAM_B2 * v + (1.0 - ADAM_B2) * (g * g)
    m_hat = m / (1.0 - ADAM_B1 ** ADAM_STEP)
    v_hat = v / (1.0 - ADAM_B2 ** ADAM_STEP)
    delta = (-ADAM_LR) * (m_hat / (jnp.sqrt(v_hat) + ADAM_EPS) + ADAM_WD * w)
    return delta, m, v


def _adamw(name, g, w, m, v, jobs=()):
    rows, cols = w.shape
    rb = _row_block(rows)

    def body(g_ref, w_ref, m_ref, v_ref, d_ref, nm_ref, nv_ref):
        d_ref[...], nm_ref[...], nv_ref[...] = _adamw_math(w_ref[...], g_ref[...], m_ref[...], v_ref[...])

    blk = pl.BlockSpec((rb, cols), lambda r: (r, 0))
    return _fused_call(
        body, jobs, name=name, grid=(rows // rb,), in_specs=[blk] * 4, out_specs=[blk] * 3,
        out_shape=[jax.ShapeDtypeStruct(w.shape, F32)] * 3, compiler_params=_params(),
    )(g, w, m, v)


def _adamw_small(grads, ws, ms, vs):
    n = len(grads)

    def body(*refs):
        g_refs, w_refs, m_refs, v_refs = refs[:n], refs[n:2 * n], refs[2 * n:3 * n], refs[3 * n:4 * n]
        outs = refs[4 * n:]
        for p in range(n):
            d, nm, nv = _adamw_math(w_refs[p][...], g_refs[p][...], m_refs[p][...], v_refs[p][...])
            outs[p][...] = d
            outs[n + p][...] = nm
            outs[2 * n + p][...] = nv

    vm = pl.BlockSpec(memory_space=pltpu.VMEM)
    shapes = [jax.ShapeDtypeStruct(w.shape, F32) for w in ws]
    out = pl.pallas_call(
        body, name="adamw_small", in_specs=[vm] * (4 * n), out_specs=[vm] * (3 * n), out_shape=shapes * 3,
    )(*grads, *ws, *ms, *vs)
    return out[:n], out[n:2 * n], out[2 * n:]


def _unstack_heads(w_st):
    per = HEAD_DIM // N_CHIPS
    return w_st.reshape(N_CHIPS, HEADS, per, HEAD_DIM).transpose(1, 0, 2, 3).reshape(HEADS, HEAD_DIM, HEAD_DIM)


def _stack_heads(w):
    per = HEAD_DIM // N_CHIPS
    return w.reshape(HEADS, N_CHIPS, per, HEAD_DIM).transpose(1, 0, 2, 3).reshape(N_CHIPS, HEADS * per, HEAD_DIM)


def kernel(x, norm_mix_g, w_in, conv_w, conv_b, w_rgate, b_rgate, w_igate, b_igate, lru_lambda, w_out_a, sgu_ln_g, sgu_ln_b, sgu_w_s, sgu_b_s, w_out_b, w_out, norm_mlp_g, w_up, w_down, norm_final_g, loss_target, m_norm_mix_g, m_w_in, m_conv_w, m_conv_b, m_w_rgate, m_b_rgate, m_w_igate, m_b_igate, m_lru_lambda, m_w_out_a, m_sgu_ln_g, m_sgu_ln_b, m_sgu_w_s, m_sgu_b_s, m_w_out_b, m_w_out, m_norm_mlp_g, m_w_up, m_w_down, m_norm_final_g, v_norm_mix_g, v_w_in, v_conv_w, v_conv_b, v_w_rgate, v_b_rgate, v_w_igate, v_b_igate, v_lru_lambda, v_w_out_a, v_sgu_ln_g, v_sgu_ln_b, v_sgu_w_s, v_sgu_b_s, v_w_out_b, v_w_out, v_norm_mlp_g, v_w_up, v_w_down, v_norm_final_g):
    chip = _chip_index(lax.axis_index("x"), lax.axis_index("y"))
    core = lax.axis_index("c")
    quarter_h = HEAD_DIM // N_CHIPS
    quarter_d = D_MODEL // N_CHIPS

    as_2d = lambda a: a.reshape(-1, a.shape[-1])
    big_w = [as_2d(w) for w in (w_in, w_rgate, w_igate, w_out_a, w_out_b, w_out, w_up, w_down)]
    big_m = [as_2d(w) for w in (m_w_in, m_w_rgate, m_w_igate, m_w_out_a, m_w_out_b, m_w_out, m_w_up, m_w_down)]
    big_v = [as_2d(w) for w in (v_w_in, v_w_rgate, v_w_igate, v_w_out_a, v_w_out_b, v_w_out, v_w_up, v_w_down)]

    packed = jnp.concatenate([conv_w[0], b_rgate[0], b_igate[0]], axis=1)
    packed = jnp.concatenate([packed, jnp.zeros_like(packed)], axis=0)
    s_in, s_r, s_i, s_oa, s_ob, s_out, s_up, s_down = [w.astype(BF16) for w in big_w]
    xs, target = x[0], loss_target[0]
    g3 = norm_final_g.reshape(1, D_MODEL)
    bias_s = jnp.broadcast_to(jnp.transpose(sgu_b_s[0])[:, :, None], (CHUNK, GROUPS, GROUP_DIM)).reshape(CHUNK, D_MODEL)
    core_arr = core.reshape(1).astype(jnp.int32)
    place = jnp.stack([chip, core]).astype(jnp.int32)
    quarter = lambda g: g.reshape(N_CHIPS, D_MODEL // N_CHIPS, D_MODEL)

    def pair_add(nm, g, from_sibling):
        return _pair_add("pair_add_" + nm, core_arr, g.reshape(N_CHIPS, 2, g.shape[1] // 2, g.shape[2]), from_sibling)

    def chip_sum(nm, pair, from_chips):
        return _chip_sum("chip_sum_" + nm, place, pair, from_chips)

    order = jnp.stack([chip, chip ^ 2, chip ^ 1, chip ^ 3]).astype(jnp.int32)
    (z, n1, (w_in_st, wr_st, wi_st)), ((packed_all,), late) = _fwd_in(
        xs, norm_mix_g, [s_in, s_r, s_i], order,
        jobs=[_gather_small_job(packed), _gather_near_job([s_oa, s_ob, s_out])])
    pick = lambda lo, hi: packed_all[:, :HEADS, lo:hi].transpose(1, 0, 2).reshape(HEADS, -1)
    conv_w_full = pick(0, quarter_d)
    br_full = pick(quarter_d, quarter_d + quarter_h).reshape(1, D_MODEL)
    bi_full = pick(quarter_d + quarter_h, quarter_d + 2 * quarter_h).reshape(1, D_MODEL)
    wr, wi = _unstack_heads(wr_st), _unstack_heads(wi_st)
    lru = (conv_w_full, conv_b, wr, br_full, wi, bi_full, lru_lambda)
    sgu = (sgu_ln_g, sgu_ln_b, sgu_w_s[0], bias_s)

    s_up, s_down, _ = lax.optimization_barrier((s_up, s_down, n1))
    mlp_w = _sequencer_call("gather_mlp_near", 7, _gather_near_job([s_up, s_down]))
    mlp_w = _sequencer_call("gather_mlp_far", 8, _gather_far_job(mlp_w))
    w_up_st, w_dn = _sequencer_call("gather_mlp_pass", 9, _gather_pass_job(mlp_w))
    w_dn = w_dn.reshape(D_FF, D_MODEL)
    (ya, *saved), (late,) = _fwd_lru(z, *lru, jobs=[_gather_far_job(late)])
    yb, (late,) = _fwd_sgu(z, *sgu, jobs=[_gather_pass_job(late)])
    w_oa, w_ob, w_o = [w.reshape(D_MODEL, D_MODEL) for w in late]
    (pa, pb, h1, n2), _ = _fwd_merge(ya, yb, z, xs, w_oa, w_ob, w_o, norm_mlp_g)
    (act, dup, dh2b, dh1, loss_part, dg3, dg2), _ = _mlp(n2, h1, target, w_up_st, w_dn, norm_mlp_g, g3)

    d_down, _ = _weight_grad("dw_down", act, dh2b, N_CHIPS, True, False, D_MODEL)
    r_down, = _sequencer_call("send_w_down", 10, _pair_send_job([d_down]))
    d_up, _ = _weight_grad("dw_up", n2, dup, N_CHIPS, False, True, D_MODEL)
    r_up, = _sequencer_call("send_w_up", 11, _pair_send_job([d_up]))
    p_down, p_up = pair_add("w_down", d_down, r_down), pair_add("w_up", d_up, r_up)
    (dz, merged, dpa, dpb, dh1b, dlg, dlb, dws, dbs, dcw, dcb, dwr, dbr, dwi, dbi, dlam), ((q_up, q_down),) = _bwd_mix(
        dh1, pa, pb, z, *saved, w_oa, w_ob, w_o, *sgu, conv_w_full, wr, wi, lru_lambda,
        jobs=[_chip_exchange_job([p_up, p_down])])
    half_up, half_down = chip_sum("w_up", p_up, q_up), chip_sum("w_down", p_down, q_down)
    names = ("w_in", "w_rgate", "w_igate", "w_out_a", "w_out_b", "w_out", "w_up", "w_down")
    d_out, ((full_up, full_down),) = _weight_grad(
        "dw_out", merged, dh1b, 1, False, False, D_MODEL, jobs=[_share_job([half_up, half_down])])
    d_oa, _ = _weight_grad("dw_out_a", ya, dpa, 1, False, False, D_MODEL)
    d_ob, _ = _weight_grad("dw_out_b", yb, dpb, 1, False, False, D_MODEL)
    mids = [quarter(d_oa), quarter(d_ob), quarter(d_out)]
    r_mids = _sequencer_call("send_mids", 1, _pair_send_job(mids))
    gates = [_stack_heads(dwr).astype(BF16), _stack_heads(dwi).astype(BF16)]
    small = _pack_small(dcw, dcb, dbr, dbi, dlam, dlg, dlb, dg2, dg3, loss_part, dbs)
    p_mids = [pair_add(nm, g, r) for nm, g, r in zip(names[3:6], mids, r_mids)]
    q_mids = _sequencer_call("exchange_mids", 2, _chip_exchange_job(p_mids))
    d_in, (r_gates, (vec_all, ws_all)) = _weight_grad(
        "dw_in", n1, dz, N_CHIPS, False, True, IN_SHARD,
        jobs=[_pair_send_job(gates), _gather_all_job([small, dws])])
    r_in, = _sequencer_call("send_w_in", 3, _pair_send_job([d_in]))
    adam_args = {nm: (w, m, v) for nm, w, m, v in zip(names, big_w, big_m, big_v)}

    def adamw(nm, g):
        w, m, v = adam_args[nm]
        g = g.reshape(w.shape)
        return g, _adamw("adamw_" + nm, g, w, m, v)[0]

    p_gates = [pair_add(nm, g, r) for nm, g, r in zip(names[1:3], gates, r_gates)]
    half_mids = [chip_sum(nm, p, q) for nm, p, q in zip(names[3:6], p_mids, q_mids)]
    full_mids = _sequencer_call("share_mids", 12, _share_job(half_mids))
    p_first = [pair_add("w_in", d_in, r_in)] + p_gates
    q_first = _sequencer_call("exchange_w_in", 4, _chip_exchange_job(p_first))
    (grad_x, dg1), _ = _bwd_in(dz, xs, dh1, w_in_st, norm_mix_g)
    dg1_all, = _sequencer_call("gather_dg1", 6, _gather_all_job([dg1]))
    done ={nm: adamw(nm, f) for nm, f in zip(("w_up", "w_down") + names[3:6], [full_up, full_down] + full_mids)}
    half_first = [chip_sum(nm, p, q) for nm, p, q in zip(names[:3], p_first, q_first)]
    full_first = _sequencer_call("share_last", 5, _share_job(half_first))
    done.update({nm: adamw(nm, f) for nm, f in zip(names[:3], full_first)})
    full, big_out = [done[nm][0] for nm in names], [done[nm][1] for nm in names]

    vec, ws_sum = _sum_small(vec_all, ws_all, dg1_all)
    row = lambda r: vec[r:r + 1]
    shard = lambda a, width: lax.dynamic_slice_in_dim(a, chip * width, width, axis=1)
    g_small = dict(
        norm_mix_g=row(ROW_G1), conv_w=shard(vec[ROW_CW:ROW_CW + CONV_WIDTH], quarter_d), conv_b=row(ROW_CB),
        b_rgate=shard(row(ROW_BR).reshape(HEADS, HEAD_DIM), quarter_h),
        b_igate=shard(row(ROW_BI).reshape(HEADS, HEAD_DIM), quarter_h), lru_lambda=row(ROW_LAM),
        sgu_ln_g=row(ROW_LG), sgu_ln_b=row(ROW_LB),
        sgu_w_s=ws_sum.reshape(CHUNK, GROUPS, CHUNK).transpose(1, 0, 2).reshape(GROUPS * CHUNK, CHUNK),
        sgu_b_s=vec[ROW_BS:ROW_BS + GROUPS, 0:CHUNK], norm_mlp_g=row(ROW_G2), norm_final_g=row(ROW_G3))
    loss = vec[ROW_LOSS, 0]
    small_names = list(g_small)
    given = dict(
        norm_mix_g=(norm_mix_g, m_norm_mix_g, v_norm_mix_g), conv_w=(conv_w, m_conv_w, v_conv_w),
        conv_b=(conv_b, m_conv_b, v_conv_b), b_rgate=(b_rgate, m_b_rgate, v_b_rgate),
        b_igate=(b_igate, m_b_igate, v_b_igate), lru_lambda=(lru_lambda, m_lru_lambda, v_lru_lambda),
        sgu_ln_g=(sgu_ln_g, m_sgu_ln_g, v_sgu_ln_g), sgu_ln_b=(sgu_ln_b, m_sgu_ln_b, v_sgu_ln_b),
        sgu_w_s=(sgu_w_s, m_sgu_w_s, v_sgu_w_s), sgu_b_s=(sgu_b_s, m_sgu_b_s, v_sgu_b_s),
        norm_mlp_g=(norm_mlp_g, m_norm_mlp_g, v_norm_mlp_g), norm_final_g=(norm_final_g, m_norm_final_g, v_norm_final_g))
    g2d = [g_small[nm] for nm in small_names]
    to2d = lambda a, g: a.reshape(g.shape)
    d_s, m_s, v_s = _adamw_small(
        g2d, *[[to2d(given[nm][q], g) for nm, g in zip(small_names, g2d)] for q in range(3)])

    shapes = dict(
        norm_mix_g=norm_mix_g, w_in=w_in, conv_w=conv_w, conv_b=conv_b, w_rgate=w_rgate, b_rgate=b_rgate,
        w_igate=w_igate, b_igate=b_igate, lru_lambda=lru_lambda, w_out_a=w_out_a, sgu_ln_g=sgu_ln_g,
        sgu_ln_b=sgu_ln_b, sgu_w_s=sgu_w_s, sgu_b_s=sgu_b_s, w_out_b=w_out_b, w_out=w_out, norm_mlp_g=norm_mlp_g,
        w_up=w_up, w_down=w_down, norm_final_g=norm_final_g)
    grads, deltas, new_m, new_v = {}, {}, {}, {}
    for nm, g, (d, nmom, nvar) in zip(names, full, big_out):
        grads[nm], deltas[nm], new_m[nm], new_v[nm] = g, d, nmom, nvar
    for p, nm in enumerate(small_names):
        grads[nm], deltas[nm], new_m[nm], new_v[nm] = g2d[p], d_s[p], m_s[p], v_s[p]
    order = list(shapes)
    out = [loss, grad_x[None]]
    for group in (grads, deltas, new_m, new_v):
        out += [group[nm].reshape(shapes[nm].shape) for nm in order]
    return tuple(out)
```

```python
import functools

import jax
import jax.numpy as jnp
from jax import lax
from jax.experimental import pallas as pl
from jax.experimental.pallas import tpu as pltpu
from jax.experimental.pallas import tpu_sc as plsc

F32 = jnp.float32
BF16 = jnp.bfloat16
MESH = pl.DeviceIdType.MESH

D_MODEL = 1024
D_IN = 6 * D_MODEL
D_FF = 4 * D_MODEL
N_CHIPS = 4
IN_SHARD = D_IN // N_CHIPS
HEADS = 4
HEAD_DIM = D_MODEL // HEADS
GROUPS = 4
GROUP_DIM = D_MODEL // GROUPS
CHUNK = 128
CONV_WIDTH = 4
LRU_C = 8.0
NORM_EPS = 1e-6
LN_EPS = 1e-5

ADAM_LR = 0.001
ADAM_B1 = 0.9
ADAM_B2 = 0.999
ADAM_EPS = 1e-08
ADAM_WD = 0.01
ADAM_STEP = 10

SUBLANES = 8
MM_TILE = 512
IN_TILE = 1024
SEQ_TILE = 256
DW_TILE = 2048
VMEM_LIMIT_BYTES = 56 * 1024 * 1024

GELU_K0 = 0.7978845608028654
GELU_K1 = 0.044715


def _params(n_grid_axes=1):
    return pltpu.CompilerParams(
        dimension_semantics=("arbitrary",) * n_grid_axes, vmem_limit_bytes=VMEM_LIMIT_BYTES)


def _resident(shape):
    nd = len(shape)
    return pl.BlockSpec(shape, lambda *_: (0,) * nd, pipeline_mode=pl.Buffered(1))


def _const(shape):
    nd = len(shape)
    return pl.BlockSpec(shape, lambda *_: (0,) * nd)


def _dot(a, b):
    return jnp.dot(a, b, preferred_element_type=F32)


def _dot_nt(a, b):
    return lax.dot_general(a, b, (((1,), (1,)), ((), ())), preferred_element_type=F32)


def _dot_tn(a, b):
    return lax.dot_general(a, b, (((0,), (0,)), ((), ())), preferred_element_type=F32)


def _gelu(x):
    t = jnp.tanh(GELU_K0 * x * (1.0 + GELU_K1 * x * x))
    return 0.5 * x * (1.0 + t)


def _gelu_and_grad(x):
    x2 = x * x
    t = jnp.tanh(GELU_K0 * x * (1.0 + GELU_K1 * x2))
    g = 0.5 * x * (1.0 + t)
    dg = 0.5 * (1.0 + t) + 0.5 * x * (1.0 - t * t) * (GELU_K0 * (1.0 + 3.0 * GELU_K1 * x2))
    return g, dg


def _rms(x):
    r = lax.rsqrt(jnp.mean(x * x, axis=-1, keepdims=True) + NORM_EPS)
    return x * r, r


def _rms_bwd(dn, xhat, r):
    return r * (dn - xhat * jnp.mean(dn * xhat, axis=-1, keepdims=True))


def _col_sum(v):
    return jnp.sum(v, axis=0, keepdims=True)


def _shift_down(x, tail8, k):
    xs = pltpu.roll(x, k, 0)
    ts = pltpu.roll(tail8, k, 0)
    ridx = lax.broadcasted_iota(jnp.int32, tail8.shape, 0)
    head = jnp.where(ridx < k, ts, xs[0:SUBLANES])
    return jnp.concatenate([head, xs[SUBLANES:]], axis=0)


def _shift_up(x, head8, k):
    n = x.shape[0]
    xs = pltpu.roll(x, n - k, 0)
    hs = pltpu.roll(head8, SUBLANES - k, 0)
    ridx = lax.broadcasted_iota(jnp.int32, head8.shape, 0)
    last = jnp.where(ridx >= SUBLANES - k, hs, xs[n - SUBLANES:n])
    return jnp.concatenate([xs[:n - SUBLANES], last], axis=0)


def _scan_forward(a, b, carry):
    n, cols = a.shape
    groups = n // SUBLANES
    a = a.reshape(groups, SUBLANES, cols)
    b = b.reshape(groups, SUBLANES, cols)
    sub = lax.broadcasted_iota(jnp.int32, a.shape, 1)
    for s in (1, 2, 4):
        a_s = pltpu.roll(a, s, 1)
        b_s = pltpu.roll(b, s, 1)
        m = sub >= s
        b = jnp.where(m, a * b_s + b, b)
        a = jnp.where(m, a * a_s, a)
    out = []
    for g in range(groups):
        h = a[g] * carry + b[g]
        out.append(h)
        carry = h[SUBLANES - 1:SUBLANES]
    return jnp.concatenate(out, axis=0), carry


def _scan_backward(a, b, carry):
    n, cols = a.shape
    groups = n // SUBLANES
    a = a.reshape(groups, SUBLANES, cols)
    b = b.reshape(groups, SUBLANES, cols)
    sub = lax.broadcasted_iota(jnp.int32, a.shape, 1)
    for s in (1, 2, 4):
        a_s = pltpu.roll(a, SUBLANES - s, 1)
        b_s = pltpu.roll(b, SUBLANES - s, 1)
        m = sub < SUBLANES - s
        b = jnp.where(m, a * b_s + b, b)
        a = jnp.where(m, a * a_s, a)
    out = [None] * groups
    for g in reversed(range(groups)):
        h = a[g] * carry + b[g]
        out[g] = h
        carry = h[0:1]
    return jnp.concatenate(out, axis=0), carry


def _softplus_neg(lam):
    e = jnp.exp(-jnp.abs(lam))
    u = 1.0 + e
    log1p_e = jnp.where(u == 1.0, e, jnp.log(u) * (e / jnp.where(u == 1.0, 1.0, u - 1.0)))
    return jnp.maximum(-lam, 0.0) + log1p_e


def _lru_gates(xa, tail8, cw_ref, cb_ref, wr_ref, br_ref, wi_ref, bi_ref, lam_ref):
    cw = cw_ref[...]
    xc = cb_ref[...] + cw[0:1] * xa
    for k in range(1, CONV_WIDTH):
        xc = xc + cw[k:k + 1] * _shift_down(xa, tail8, k)
    xcb = xc.astype(BF16)
    pre_r, pre_i = [], []
    for h in range(HEADS):
        cols = slice(h * HEAD_DIM, (h + 1) * HEAD_DIM)
        pre_r.append(_dot(xcb[:, cols], wr_ref[h]))
        pre_i.append(_dot(xcb[:, cols], wi_ref[h]))
    r = jax.nn.sigmoid(jnp.concatenate(pre_r, axis=1) + br_ref[...])
    ig = jax.nn.sigmoid(jnp.concatenate(pre_i, axis=1) + bi_ref[...])
    _, a, mult = _decay(r, lam_ref)
    return xc, r, ig, a, mult


def _decay(r, lam_ref):
    sp = _softplus_neg(lam_ref[...])
    log_a = ((-LRU_C) * sp) * r
    a = jnp.exp(log_a)
    th = jnp.tanh(log_a)
    return sp, a, jnp.sqrt((-2.0 * th) / (1.0 - th))


class _Job:
    def __init__(self, inputs, out_shape, n_sem, copies, peers, aliases=None, n_local=0):
        self.inputs, self.out_shape, self.n_sem, self.copies = list(inputs), list(out_shape), n_sem, copies
        self.aliases, self.n_local = dict(aliases or {}), n_local
        self.peers = tuple(peers)


def _fused_call(body, jobs, *, name, grid, in_specs, out_specs, out_shape, scratch_shapes=(),
                input_output_aliases=None, compiler_params=None, n_prefetch=0, jobs_start_after=None):
    single = not isinstance(out_shape, (list, tuple))
    out_specs = [out_specs] if single else list(out_specs)
    out_shape = [out_shape] if single else list(out_shape)
    n_scr = len(scratch_shapes)
    in_specs, scratch_shapes = list(in_specs), list(scratch_shapes)
    n_in, n_out = len(in_specs), len(out_shape)
    aliases = dict(input_output_aliases or {})
    in_at, out_at = [], []
    for job in jobs:
        in_at.append(len(in_specs))
        out_at.append(len(out_shape))
        for i, o in job.aliases.items():
            aliases[n_prefetch + len(in_specs) + i] = len(out_shape) + o
        in_specs += [ANY] * len(job.inputs)
        out_specs += [ANY] * len(job.out_shape)
        out_shape += job.out_shape
        scratch_shapes += [pltpu.SemaphoreType.DMA((job.n_sem,)), pltpu.SemaphoreType.DMA((job.n_sem,)),
                           pltpu.SemaphoreType.DMA((max(job.n_local, 1),))]
    n_in_all, n_out_all = len(in_specs), len(out_shape)

    def full_body(*refs):
        prefetch, refs = refs[:n_prefetch], refs[n_prefetch:]
        ins, outs, scr = refs[:n_in_all], refs[n_in_all:n_in_all + n_out_all], refs[n_in_all + n_out_all:]

        def copies(q):
            job = jobs[q]
            return job.copies(ins[in_at[q]:in_at[q] + len(job.inputs)], outs[out_at[q]:out_at[q] + len(job.out_shape)],
                              *scr[n_scr + 3 * q:n_scr + 3 * q + 3])

        def start():
            for q in range(len(jobs)):
                sends, _, local = copies(q)
                for cp in local + sends:
                    cp.start()

        def finish():
            every = [copies(q) for q in range(len(jobs))]
            for _, arrivals, _ in every:
                for cp in arrivals:
                    cp.wait_recv()
            for sends, _, local in every:
                for cp in sends:
                    cp.wait_send()
                for cp in local:
                    cp.wait()

        if not grid:
            start()
            finish()
            return
        ids = [pl.program_id(a) for a in range(len(grid))]
        at_step = lambda step: functools.reduce(jnp.logical_and, [i == k for i, k in zip(ids, step)])
        if jobs and jobs_start_after is None:
            pl.when(at_step((0,) * len(grid)))(start)
        body(*prefetch, *ins[:n_in], *outs[:n_out], *scr[:n_scr])
        if jobs and jobs_start_after is not None:
            pl.when(at_step(jobs_start_after))(start)
        if jobs:
            pl.when(functools.reduce(jnp.logical_and, [i == g - 1 for i, g in zip(ids, grid)]))(finish)

    if n_prefetch:
        layout = dict(grid_spec=pltpu.PrefetchScalarGridSpec(
            num_scalar_prefetch=n_prefetch, grid=grid, in_specs=in_specs, out_specs=out_specs,
            scratch_shapes=scratch_shapes))
    else:
        layout = dict(grid=grid, in_specs=in_specs, out_specs=out_specs, scratch_shapes=scratch_shapes)
    call = pl.pallas_call(
        full_body, name=name, out_shape=out_shape, input_output_aliases=aliases, compiler_params=compiler_params,
        **layout)

    def run(*args):
        res = call(*args, *[a for job in jobs for a in job.inputs])
        mine = res[0] if single else list(res[:n_out])
        return mine, [list(res[at:at + len(job.out_shape)]) for at, job in zip(out_at, jobs)]

    return run


def _fwd_in(x, g1, shards, order, jobs=()):
    t = x.shape[0]
    rows_per_step = min(IN_TILE, t)
    n_tiles = t // rows_per_step
    n = len(shards)
    halves = [s.shape[0] // 2 for s in shards]

    def body(order_ref, x_ref, g_ref, *refs):
        del order_ref
        ins, (z_ref, n_ref), outs = refs[:n], refs[n:n + 2], refs[n + 2:2 * n + 2]
        wbuf, nbuf, send, recv, local = refs[2 * n + 2:]
        s, i = pl.program_id(0), pl.program_id(1)
        x_, y_, c, chips = _place()
        k_me = _chip_index(x_, y_)

        def block(w, chip, pc):
            return outs[w].at[_chip_index(*chip), pl.ds(pc * halves[w], halves[w]), :]

        def over_ici(w, j, landing):
            return pltpu.make_async_remote_copy(
                src_ref=ins[w].at[pl.ds(c * halves[w], halves[w]), :],
                dst_ref=block(w, chips[j] if landing else (x_, y_), c), send_sem=send.at[6 * w + j],
                recv_sem=recv.at[6 * w + j], device_id=(*chips[j], c), device_id_type=MESH)

        def to_sibling(w, j, landing):
            blk = block(w, chips[j], 1 - c if landing else c)
            return pltpu.make_async_remote_copy(
                src_ref=blk, dst_ref=blk, send_sem=send.at[6 * w + 3 + j], recv_sem=recv.at[6 * w + 3 + j],
                device_id=(x_, y_, 1 - c), device_id_type=MESH)

        own = [pltpu.make_async_copy(wbuf, outs[0].at[k_me], local.at[0])]
        own += [pltpu.make_async_copy(ins[w], outs[w].at[k_me], local.at[w]) for w in range(1, n)]

        @pl.when((s == 0) & (i == 0))
        def _():
            for j in range(2):
                for w in range(n):
                    over_ici(w, j, False).start()
            load = pltpu.make_async_copy(ins[0], wbuf, local.at[n])
            load.start()
            load.wait()
            for cp in own:
                cp.start()

        for j in range(N_CHIPS - 1):
            @pl.when((s == j + 1) & (i == 0))
            def _(j=j):
                for w in range(n):
                    over_ici(w, j, True).wait_recv()
                for w in range(n):
                    to_sibling(w, j, False).start()
                if j == 0:
                    for w in range(n):
                        over_ici(w, 2, False).start()
                    own[0].wait()
                for w in range(n):
                    to_sibling(w, j, True).wait_recv()
                load = pltpu.make_async_copy(outs[0].at[_chip_index(*chips[j])], wbuf, local.at[n])
                load.start()
                load.wait()

        rows = pl.ds(pl.multiple_of(i * rows_per_step, rows_per_step), rows_per_step)

        @pl.when(s == 0)
        def _():
            xhat, _ = _rms(x_ref[...])
            nrm = (xhat * g_ref[...]).astype(BF16)
            nbuf[rows, :] = nrm
            n_ref[...] = nrm

        z_ref[...] = _dot(nbuf[rows, :], wbuf[...])

        @pl.when((s == N_CHIPS - 1) & (i == n_tiles - 1))
        def _():
            for j in range(N_CHIPS - 1):
                for w in range(n):
                    over_ici(w, j, False).wait_send()
                    to_sibling(w, j, False).wait_send()
            for cp in own[1:]:
                cp.wait()

    once = lambda s, i, order: (jnp.where(s == 0, i, n_tiles - 1), 0)
    (z, n1, *stacked), job_outs = _fused_call(
        body, jobs, name="fwd_in", grid=(N_CHIPS, n_tiles), n_prefetch=1,
        in_specs=[pl.BlockSpec((rows_per_step, D_MODEL), once), _const((1, D_MODEL))] + [ANY] * n,
        out_specs=[pl.BlockSpec((rows_per_step, IN_SHARD), lambda s, i, order: (i, order[s])),
                   pl.BlockSpec((rows_per_step, D_MODEL), once)] + [ANY] * n,
        out_shape=[jax.ShapeDtypeStruct((t, D_IN), F32), jax.ShapeDtypeStruct((t, D_MODEL), BF16)]
        + [jax.ShapeDtypeStruct((N_CHIPS,) + s.shape, s.dtype) for s in shards],
        scratch_shapes=[pltpu.VMEM(shards[0].shape, BF16), pltpu.VMEM((t, D_MODEL), BF16),
                        pltpu.SemaphoreType.DMA((6 * n,)),
                        pltpu.SemaphoreType.DMA((6 * n,)), pltpu.SemaphoreType.DMA((n + 1,))],
        compiler_params=_params(2), jobs_start_after=(1, 0),
    )(order, x, g1, *shards)
    return (z, n1, stacked), job_outs


def _fwd_lru(z, conv_w, conv_b, wr, br, wi, bi, lam, jobs=()):
    t = z.shape[0]

    def body(xa_ref, ga_ref, cw_ref, cb_ref, wr_ref, br_ref, wi_ref, bi_ref, lam_ref, ya_ref, h_ref, xc_ref, r_ref,
             ig_ref, tail_ref, carry_ref):
        @pl.when(pl.program_id(0) == 0)
        def _():
            tail_ref[...] = jnp.zeros_like(tail_ref)
            carry_ref[...] = jnp.zeros_like(carry_ref)

        xa = xa_ref[...]
        xc, r, ig, a, mult = _lru_gates(xa, tail_ref[...], cw_ref, cb_ref, wr_ref, br_ref, wi_ref, bi_ref, lam_ref)
        tail_ref[...] = xa[SEQ_TILE - SUBLANES:]
        xc_ref[...], r_ref[...], ig_ref[...] = xc, r, ig
        h, carry = _scan_forward(a, xc * ig * mult, carry_ref[...])
        carry_ref[...] = carry
        h_ref[...] = h
        ya_ref[...] = (h * _gelu(ga_ref[...])).astype(BF16)

    tile = lambda j: pl.BlockSpec((SEQ_TILE, D_MODEL), lambda i: (i, j))
    return _fused_call(
        body, jobs, name="fwd_lru", grid=(t // SEQ_TILE,),
        in_specs=[tile(0), tile(1), _const((CONV_WIDTH, D_MODEL)), _const((1, D_MODEL)),
                  _resident((HEADS, HEAD_DIM, HEAD_DIM)), _const((1, D_MODEL)),
                  _resident((HEADS, HEAD_DIM, HEAD_DIM)), _const((1, D_MODEL)), _const((1, D_MODEL))],
        out_specs=[tile(0)] * 5,
        out_shape=[jax.ShapeDtypeStruct((t, D_MODEL), BF16)] + [jax.ShapeDtypeStruct((t, D_MODEL), F32)] * 4,
        scratch_shapes=[pltpu.VMEM((SUBLANES, D_MODEL), F32), pltpu.VMEM((1, D_MODEL), F32)],
        compiler_params=_params(),
    )(z, z, conv_w, conv_b, wr, br, wi, bi, lam)


def _sgu_forward_parts(ub, vb, lg_ref, lb_ref):
    u, du = _gelu_and_grad(ub)
    vg, dvg = _gelu_and_grad(vb)
    mu = jnp.mean(vg, axis=-1, keepdims=True)
    d = vg - mu
    rstd = lax.rsqrt(jnp.mean(d * d, axis=-1, keepdims=True) + LN_EPS)
    vhat = d * rstd
    vn = (vhat * lg_ref[...] + lb_ref[...]).astype(BF16)
    return u, du, dvg, rstd, vhat, vn


def _causal_mask():
    rows = lax.broadcasted_iota(jnp.int32, (CHUNK, CHUNK), 0)
    cols = lax.broadcasted_iota(jnp.int32, (CHUNK, CHUNK), 1)
    return rows >= cols


def _fwd_sgu(z, ln_g, ln_b, w_s, bias_full, jobs=()):
    t = z.shape[0]

    def body(ub_ref, vb_ref, lg_ref, lb_ref, ws_ref, bias_ref, yb_ref):
        u, _, _, _, _, vn = _sgu_forward_parts(ub_ref[...], vb_ref[...], lg_ref, lb_ref)
        mask = _causal_mask()
        wm = [jnp.where(mask, ws_ref[g], 0.0).astype(BF16) for g in range(GROUPS)]
        for c in range(SEQ_TILE // CHUNK):
            rows = slice(c * CHUNK, (c + 1) * CHUNK)
            for g in range(GROUPS):
                cols = slice(g * GROUP_DIM, (g + 1) * GROUP_DIM)
                sp = _dot(wm[g], vn[rows, cols]) + bias_ref[:, cols]
                yb_ref[rows, cols] = (u[rows, cols] * sp).astype(BF16)

    tile = lambda j: pl.BlockSpec((SEQ_TILE, D_MODEL), lambda i: (i, j))
    return _fused_call(
        body, jobs, name="fwd_sgu", grid=(t // SEQ_TILE,),
        in_specs=[tile(2), tile(3), _const((1, D_MODEL)), _const((1, D_MODEL)),
                  _const((GROUPS, CHUNK, CHUNK)), _const((CHUNK, D_MODEL))],
        out_specs=tile(0),
        out_shape=jax.ShapeDtypeStruct((t, D_MODEL), BF16),
        compiler_params=_params(),
    )(z, z, ln_g, ln_b, w_s, bias_full)


def _fwd_merge(ya, yb, z, x, w_oa, w_ob, w_out, g2, jobs=()):
    t = x.shape[0]

    def body(ya_ref, yb_ref, m_ref, x_ref, woa_ref, wob_ref, wout_ref, g_ref, pa_ref, pb_ref, h1_ref, n2_ref):
        pa = _dot(ya_ref[...], woa_ref[...])
        pb = _dot(yb_ref[...], wob_ref[...])
        pa_ref[...] = pa
        pb_ref[...] = pb
        merged = jax.nn.sigmoid(m_ref[:, :D_MODEL]) * pa + jax.nn.sigmoid(m_ref[:, D_MODEL:]) * pb
        h1 = x_ref[...] + _dot(merged.astype(BF16), wout_ref[...])
        h1_ref[...] = h1
        xhat, _ = _rms(h1)
        n2_ref[...] = (xhat * g_ref[...]).astype(BF16)

    tile = pl.BlockSpec((MM_TILE, D_MODEL), lambda i: (i, 0))
    sq = _resident((D_MODEL, D_MODEL))
    return _fused_call(
        body, jobs, name="fwd_merge", grid=(t // MM_TILE,),
        in_specs=[tile, tile, pl.BlockSpec((MM_TILE, 2 * D_MODEL), lambda i: (i, 2)), tile, sq, sq, sq,
                  _const((1, D_MODEL))],
        out_specs=[tile, tile, tile, tile],
        out_shape=[jax.ShapeDtypeStruct((t, D_MODEL), F32)] * 3 + [jax.ShapeDtypeStruct((t, D_MODEL), BF16)],
        compiler_params=_params(),
    )(ya, yb, z, x, w_oa, w_ob, w_out, g2)


def _mlp(n2, h1, target, w_up_st, w_down, g2, g3, jobs=()):
    t = n2.shape[0]

    def body(n2_ref, h1_ref, tgt_ref, wup_ref, wdown_ref, g2_ref, g3_ref, act_ref, dup_ref, dh2b_ref, dh1_ref,
             loss_ref, dg3_ref, dg2_ref, relu_ref):
        @pl.when(pl.program_id(0) == 0)
        def _():
            for ref in (loss_ref, dg3_ref, dg2_ref):
                ref[...] = jnp.zeros_like(ref)

        n2 = n2_ref[...]
        h1 = h1_ref[...]
        h2 = h1
        for k in range(N_CHIPS):
            cols = slice(k * D_MODEL, (k + 1) * D_MODEL)
            r = jnp.maximum(_dot(n2, wup_ref[k]), 0.0)
            relu_ref[:, cols] = r
            act = (r * r).astype(BF16)
            act_ref[:, cols] = act
            h2 = h2 + _dot(act, wdown_ref[cols, :])
        xhat, r3 = _rms(h2)
        diff = xhat * g3_ref[...] - tgt_ref[...]
        sq = jnp.sum(diff * diff, axis=1, keepdims=True)
        loss_ref[...] = loss_ref[...] + (0.5 / D_MODEL) * jnp.sum(sq, axis=0, keepdims=True)
        dy = diff * (1.0 / D_MODEL)
        dg3_ref[...] = dg3_ref[...] + _col_sum(dy * xhat)
        dh2 = _rms_bwd(dy * g3_ref[...], xhat, r3)
        dh2b = dh2.astype(BF16)
        dh2b_ref[...] = dh2b
        dn2 = jnp.zeros((SEQ_TILE, D_MODEL), F32)
        for k in range(N_CHIPS):
            cols = slice(k * D_MODEL, (k + 1) * D_MODEL)
            dup = (_dot_nt(dh2b, wdown_ref[cols, :]) * (2.0 * relu_ref[:, cols])).astype(BF16)
            dup_ref[:, cols] = dup
            dn2 = dn2 + _dot_nt(dup, wup_ref[k])
        xhat, r2 = _rms(h1)
        dg2_ref[...] = dg2_ref[...] + _col_sum(dn2 * xhat)
        dh1_ref[...] = dh2 + _rms_bwd(dn2 * g2_ref[...], xhat, r2)

    tile = pl.BlockSpec((SEQ_TILE, D_MODEL), lambda i: (i, 0))
    wide = pl.BlockSpec((SEQ_TILE, D_FF), lambda i: (i, 0))
    vec = _const((1, D_MODEL))
    vec_shape = jax.ShapeDtypeStruct((1, D_MODEL), F32)
    return _fused_call(
        body, jobs, name="mlp", grid=(t // SEQ_TILE,),
        in_specs=[tile, tile, tile, _resident((N_CHIPS, D_MODEL, D_MODEL)), _resident((D_FF, D_MODEL)), vec, vec],
        out_specs=[wide, wide, tile, tile, _const((SUBLANES, 128)), vec, vec],
        out_shape=[jax.ShapeDtypeStruct((t, D_FF), BF16), jax.ShapeDtypeStruct((t, D_FF), BF16),
                   jax.ShapeDtypeStruct((t, D_MODEL), BF16), jax.ShapeDtypeStruct((t, D_MODEL), F32),
                   jax.ShapeDtypeStruct((SUBLANES, 128), F32), vec_shape, vec_shape],
        scratch_shapes=[pltpu.VMEM((SEQ_TILE, D_FF), F32)],
        compiler_params=_params(),
    )(n2, h1, target, w_up_st, w_down, g2, g3)


def _bwd_mix(dh1, pa, pb, z, h, xc, r, ig, w_oa, w_ob, w_out, ln_g, ln_b, w_s, bias_full, conv_w, wr, wi, lam, jobs=()):
    t = dh1.shape[0]
    n_tiles = t // SEQ_TILE
    per_tile = SEQ_TILE // SUBLANES

    def merge_part(dh1_ref, pa_ref, pb_ref, m_ref, woa_ref, wob_ref, wout_ref, dz_ref, dya_ref, dyb_ref, mg_ref,
                   dpa_ref, dpb_ref, dh1b_ref):
        dh1b = dh1_ref[...].astype(BF16)
        dh1b_ref[...] = dh1b
        dm = _dot_nt(dh1b, wout_ref[...])
        pa = pa_ref[...]
        pb = pb_ref[...]
        sa = jax.nn.sigmoid(m_ref[:, :D_MODEL])
        sb = jax.nn.sigmoid(m_ref[:, D_MODEL:])
        mg_ref[...] = (sa * pa + sb * pb).astype(BF16)
        dz_ref[:, :D_MODEL] = (dm * pa * sa * (1.0 - sa)).astype(BF16)
        dz_ref[:, D_MODEL:] = (dm * pb * sb * (1.0 - sb)).astype(BF16)
        dpa = (dm * sa).astype(BF16)
        dpb = (dm * sb).astype(BF16)
        dpa_ref[...] = dpa
        dpb_ref[...] = dpb
        dya_ref[...] = _dot_nt(dpa, woa_ref[...])
        dyb_ref[...] = _dot_nt(dpb, wob_ref[...])

    def sgu_part(dyb_ref, ub_ref, vb_ref, lg_ref, lb_ref, ws_ref, bias_ref, dz_ref, dlg_ref, dlb_ref, dws_ref, dbs_ref,
                 dvn_ref, dsp_acc):
        i = pl.program_id(0)

        @pl.when(i == 0)
        def _():
            dlg_ref[...] = jnp.zeros_like(dlg_ref)
            dlb_ref[...] = jnp.zeros_like(dlb_ref)
            dws_ref[...] = jnp.zeros_like(dws_ref)
            dsp_acc[...] = jnp.zeros_like(dsp_acc)

        u, du, dvg, rstd, vhat, vn = _sgu_forward_parts(ub_ref[...], vb_ref[...], lg_ref, lb_ref)
        dyb = dyb_ref[...]
        mask = _causal_mask()
        wm = [jnp.where(mask, ws_ref[g], 0.0).astype(BF16) for g in range(GROUPS)]
        for c in range(SEQ_TILE // CHUNK):
            rows = slice(c * CHUNK, (c + 1) * CHUNK)
            for g in range(GROUPS):
                cols = slice(g * GROUP_DIM, (g + 1) * GROUP_DIM)
                vn_blk = vn[rows, cols]
                sp = _dot(wm[g], vn_blk) + bias_ref[:, cols]
                dyb_blk = dyb[rows, cols]
                dz_ref[rows, cols] = (dyb_blk * sp * du[rows, cols]).astype(BF16)
                dsp = dyb_blk * u[rows, cols]
                dsp_acc[:, cols] = dsp_acc[:, cols] + dsp
                dspb = dsp.astype(BF16)
                dvn_ref[rows, cols] = _dot_tn(wm[g], dspb)
                wcols = slice(g * CHUNK, (g + 1) * CHUNK)
                dws_ref[:, wcols] = dws_ref[:, wcols] + jnp.where(mask, _dot_nt(dspb, vn_blk), 0.0)
        dvn = dvn_ref[...]
        dlg_ref[...] = dlg_ref[...] + _col_sum(dvn * vhat)
        dlb_ref[...] = dlb_ref[...] + _col_sum(dvn)
        dvhat = dvn * lg_ref[...]
        dvgel = rstd * (dvhat - jnp.mean(dvhat, axis=-1, keepdims=True)
                        - vhat * jnp.mean(dvhat * vhat, axis=-1, keepdims=True))
        dz_ref[:, D_MODEL:] = (dvgel * dvg).astype(BF16)

        @pl.when(i == n_tiles - 1)
        def _():
            lane = lax.broadcasted_iota(jnp.int32, (CHUNK, 128), 1)
            out = jnp.zeros((CHUNK, 128), F32)
            for g in range(GROUPS):
                s = jnp.sum(dsp_acc[:, g * GROUP_DIM:(g + 1) * GROUP_DIM], axis=1, keepdims=True)
                out = out + jnp.where(lane == g, s, 0.0)
            dbs_ref[...] = out

    def lru_part(dya_ref, xa_ref, ga_ref, h_ref, h_prev_ref, xc_ref, r_ref, ig_ref, cw_ref, wr_ref, wi_ref, lam_ref,
                 dz_ref, dcw_ref, dcb_ref, dwr_ref, dbr_ref, dwi_ref, dbi_ref, dlam_ref, lam_carry, dxc_head):
        i = pl.program_id(0)

        @pl.when(i == 0)
        def _():
            for ref in (dcw_ref, dcb_ref, dwr_ref, dbr_ref, dwi_ref, dbi_ref, dlam_ref, lam_carry, dxc_head):
                ref[...] = jnp.zeros_like(ref)

        first_tile = i == n_tiles - 1
        h_tail = jnp.where(first_tile, 0.0, h_prev_ref[...])
        xc, r, ig = xc_ref[...], r_ref[...], ig_ref[...]
        xcb = xc.astype(BF16)
        sp, a, mult = _decay(r, lam_ref)
        h = h_ref[...]
        h_prev = _shift_down(h, h_tail, 1)
        dya = dya_ref[...]
        gg, dgg = _gelu_and_grad(ga_ref[...])
        dz_ref[:, D_MODEL:] = (dya * h * dgg).astype(BF16)
        ones = jnp.ones((SUBLANES, D_MODEL), F32)
        lam_t, lam_first = _scan_backward(_shift_up(a, ones, 1), dya * gg, lam_carry[...])
        lam_carry[...] = a[0:1] * lam_first
        dmult = lam_t * xc * ig
        dla = lam_t * h_prev * a - dmult * (a * a) / mult
        dr = dla * ((-LRU_C) * sp)
        dlam_ref[...] = dlam_ref[...] + _col_sum(dla * r) * (LRU_C * jax.nn.sigmoid(-lam_ref[...]))
        dpr = dr * r * (1.0 - r)
        dpi = lam_t * xc * mult * ig * (1.0 - ig)
        dbr_ref[...] = dbr_ref[...] + _col_sum(dpr)
        dbi_ref[...] = dbi_ref[...] + _col_sum(dpi)
        dprb = dpr.astype(BF16)
        dpib = dpi.astype(BF16)
        dxc_gate = []
        for hd in range(HEADS):
            cols = slice(hd * HEAD_DIM, (hd + 1) * HEAD_DIM)
            dxc_gate.append(_dot_nt(dprb[:, cols], wr_ref[hd]) + _dot_nt(dpib[:, cols], wi_ref[hd]))
            dwr_ref[hd] = dwr_ref[hd] + _dot_tn(xcb[:, cols], dprb[:, cols])
            dwi_ref[hd] = dwi_ref[hd] + _dot_tn(xcb[:, cols], dpib[:, cols])
        dxc = lam_t * ig * mult + jnp.concatenate(dxc_gate, axis=1)
        dcb_ref[...] = dcb_ref[...] + _col_sum(dxc)
        cw = cw_ref[...]
        head = dxc_head[...]
        xa = xa_ref[...]
        dxa = cw[0:1] * dxc
        dcw_ref[0:1, :] = dcw_ref[0:1, :] + _col_sum(dxc * xa)
        for k in range(1, CONV_WIDTH):
            dxc_k = _shift_up(dxc, head, k)
            dxa = dxa + cw[k:k + 1] * dxc_k
            dcw_ref[k:k + 1, :] = dcw_ref[k:k + 1, :] + _col_sum(dxc_k * xa)
        dxc_head[...] = dxc[0:SUBLANES]
        dz_ref[:, :D_MODEL] = dxa.astype(BF16)

    def body(dh1_ref, pa_ref, pb_ref, z_ref, h_ref, h_prev_ref, xc_ref, r_ref, ig_ref, woa_ref, wob_ref, wout_ref,
             lg_ref, lb_ref, ws_ref, bias_ref, cw_ref, wr_ref, wi_ref, lam_ref, dz_ref, mg_ref, dpa_ref, dpb_ref,
             dh1b_ref, dlg_ref, dlb_ref, dws_ref, dbs_ref, dcw_ref, dcb_ref, dwr_ref, dbr_ref, dwi_ref, dbi_ref,
             dlam_ref, dya_ref, dyb_ref, dvn_ref, dsp_acc, lam_carry, dxc_head):
        def cols(ref, first, count):
            return ref.at[:, pl.ds(first * D_MODEL, count * D_MODEL)]

        merge_part(dh1_ref, pa_ref, pb_ref, cols(z_ref, 4, 2), woa_ref, wob_ref, wout_ref, cols(dz_ref, 4, 2), dya_ref,
                   dyb_ref, mg_ref, dpa_ref, dpb_ref, dh1b_ref)
        sgu_part(dyb_ref, cols(z_ref, 2, 1), cols(z_ref, 3, 1), lg_ref, lb_ref, ws_ref, bias_ref, cols(dz_ref, 2, 2),
                 dlg_ref, dlb_ref, dws_ref, dbs_ref, dvn_ref, dsp_acc)
        lru_part(dya_ref, cols(z_ref, 0, 1), cols(z_ref, 1, 1), h_ref, h_prev_ref, xc_ref, r_ref, ig_ref, cw_ref, wr_ref,
                 wi_ref, lam_ref, cols(dz_ref, 0, 2), dcw_ref, dcb_ref, dwr_ref, dbr_ref, dwi_ref, dbi_ref, dlam_ref,
                 lam_carry, dxc_head)

    rev = lambda i: n_tiles - 1 - i
    tile = pl.BlockSpec((SEQ_TILE, D_MODEL), lambda i: (rev(i), 0))
    row = pl.BlockSpec((SEQ_TILE, D_IN), lambda i: (rev(i), 0))
    prev8 = pl.BlockSpec((SUBLANES, D_MODEL), lambda i: (jnp.maximum(rev(i) * per_tile - 1, 0), 0))
    vec = _const((1, D_MODEL))
    sq = _resident((D_MODEL, D_MODEL))
    gate_w = _resident((HEADS, HEAD_DIM, HEAD_DIM))
    gate_acc = _const((HEADS, HEAD_DIM, HEAD_DIM))
    vec_shape = jax.ShapeDtypeStruct((1, D_MODEL), F32)
    gate_shape = jax.ShapeDtypeStruct((HEADS, HEAD_DIM, HEAD_DIM), F32)
    act_bf = jax.ShapeDtypeStruct((t, D_MODEL), BF16)
    return _fused_call(
        body, jobs, name="bwd_mix", grid=(n_tiles,),
        in_specs=[tile, tile, tile, row, tile, prev8, tile, tile, tile, sq, sq, sq, vec, vec,
                  _const((GROUPS, CHUNK, CHUNK)), _const((CHUNK, D_MODEL)), _const((CONV_WIDTH, D_MODEL)), gate_w, gate_w,
                  vec],
        out_specs=[row, tile, tile, tile, tile, vec, vec, _const((CHUNK, GROUPS * CHUNK)), _const((CHUNK, 128)),
                   _const((SUBLANES, D_MODEL)), vec, gate_acc, vec, gate_acc, vec, vec],
        out_shape=[jax.ShapeDtypeStruct((t, D_IN), BF16), act_bf, act_bf, act_bf, act_bf, vec_shape, vec_shape,
                   jax.ShapeDtypeStruct((CHUNK, GROUPS * CHUNK), F32), jax.ShapeDtypeStruct((CHUNK, 128), F32),
                   jax.ShapeDtypeStruct((SUBLANES, D_MODEL), F32), vec_shape, gate_shape, vec_shape, gate_shape,
                   vec_shape, vec_shape],
        scratch_shapes=[pltpu.VMEM((SEQ_TILE, D_MODEL), F32), pltpu.VMEM((SEQ_TILE, D_MODEL), F32),
                        pltpu.VMEM((SEQ_TILE, D_MODEL), F32), pltpu.VMEM((CHUNK, D_MODEL), F32),
                        pltpu.VMEM((1, D_MODEL), F32), pltpu.VMEM((SUBLANES, D_MODEL), F32)],
        compiler_params=_params(),
    )(dh1, pa, pb, z, h, h, xc, r, ig, w_oa, w_ob, w_out, ln_g, ln_b, w_s, bias_full, conv_w, wr, wi, lam)


def _bwd_in(dz, x, dh1, w_in_st, g1, jobs=()):
    t = x.shape[0]

    def body(dz_ref, x_ref, dh1_ref, w_ref, g_ref, dx_ref, dg1_ref):
        @pl.when(pl.program_id(0) == 0)
        def _():
            dg1_ref[...] = jnp.zeros_like(dg1_ref)

        dn1 = jnp.zeros((MM_TILE, D_MODEL), F32)
        for k in range(N_CHIPS):
            dn1 = dn1 + _dot_nt(dz_ref[:, k * IN_SHARD:(k + 1) * IN_SHARD], w_ref[k])
        xhat, r1 = _rms(x_ref[...])
        dg1_ref[...] = dg1_ref[...] + _col_sum(dn1 * xhat)
        dx_ref[...] = dh1_ref[...] + _rms_bwd(dn1 * g_ref[...], xhat, r1)

    tile = pl.BlockSpec((MM_TILE, D_MODEL), lambda i: (i, 0))
    return _fused_call(
        body, jobs, name="bwd_in", grid=(t // MM_TILE,),
        in_specs=[pl.BlockSpec((MM_TILE, D_IN), lambda i: (i, 0)), tile, tile,
                  _resident((N_CHIPS, D_MODEL, IN_SHARD)), _const((1, D_MODEL))],
        out_specs=[tile, _const((1, D_MODEL))],
        out_shape=[jax.ShapeDtypeStruct((t, D_MODEL), F32), jax.ShapeDtypeStruct((1, D_MODEL), F32)],
        compiler_params=_params(),
    )(dz, x, dh1, w_in_st, g1)


def _weight_grad(name, a, b, n_blocks, a_varies, b_varies, width, jobs=()):
    t = a.shape[0]
    rows = min(DW_TILE, t)
    n_t = t // rows

    def body(a_ref, b_ref, o_ref, acc_ref):
        s = pl.program_id(1)
        part = _dot_tn(a_ref[...], b_ref[...])

        @pl.when(s == 0)
        def _():
            acc_ref[...] = part

        @pl.when(s > 0)
        def _():
            acc_ref[...] = acc_ref[...] + part

        @pl.when(s == n_t - 1)
        def _():
            o_ref[...] = acc_ref[...].astype(BF16)

    return _fused_call(
        body, jobs, name=name, grid=(n_blocks, n_t),
        in_specs=[pl.BlockSpec((rows, D_MODEL), (lambda j, s: (s, j)) if a_varies else (lambda j, s: (s, 0))),
                  pl.BlockSpec((rows, width), (lambda j, s: (s, j)) if b_varies else (lambda j, s: (s, 0)))],
        out_specs=pl.BlockSpec((None, D_MODEL, width), lambda j, s: (j, 0, 0)),
        out_shape=jax.ShapeDtypeStruct((n_blocks, D_MODEL, width), BF16),
        scratch_shapes=[pltpu.VMEM((D_MODEL, width), F32)],
        compiler_params=_params(2),
    )(a, b)


def _place():
    x, y, c = lax.axis_index("x"), lax.axis_index("y"), lax.axis_index("c")
    other_chips = [(1 - x, y), (x, 1 - y), (1 - x, 1 - y)]
    return x, y, c, other_chips


def _chip_index(px, py):
    return 2 * px + py


ANY = pl.BlockSpec(memory_space=pl.ANY)
SIBLING = ((0, 0, 1),)
NEIGHBOURS = ((1, 0, 0), (0, 1, 0))
OTHER_CHIPS = NEIGHBOURS + ((1, 1, 0),)


def _comm_call(name, jobs):
    return _fused_call(None, jobs, name=name, grid=(), in_specs=[], out_specs=[], out_shape=[])()[1]


def _near_far(x, y, c):
    return (x ^ (1 - c), y ^ c), (x ^ c, y ^ (1 - c))


def _gather_near_job(shards):
    n = len(shards)
    halves = [s.shape[0] // 2 for s in shards]

    def copies(ins, outs, send, recv, local):
        x, y, c, _ = _place()
        near, _ = _near_far(x, y, c)

        def block(w, chip, pc):
            return outs[w].at[_chip_index(*chip), pl.ds(pc * halves[w], halves[w]), :]

        def copy(w, k, chip, pc, to, src=None):
            return pltpu.make_async_remote_copy(
                src_ref=block(w, chip, pc) if src is None else src, dst_ref=block(w, chip, pc),
                send_sem=send.at[2 * w + k], recv_sem=recv.at[2 * w + k], device_id=to, device_id_type=MESH)

        sends, arrivals, own = [], [], []
        for w in range(n):
            src = ins[w].at[pl.ds(c * halves[w], halves[w]), :]
            own.append(pltpu.make_async_copy(src, block(w, (x, y), c), local.at[w]))
            sends += [copy(w, 0, (x, y), c, (*near, c), src), copy(w, 1, (x, y), c, (x, y, 1 - c), src)]
            arrivals += [copy(w, 0, near, c, (x, y, c)), copy(w, 1, (x, y), 1 - c, (x, y, c))]
        return sends, arrivals, own

    return _Job(shards, [jax.ShapeDtypeStruct((N_CHIPS,) + s.shape, s.dtype) for s in shards], 2 * n, copies,
                NEIGHBOURS + SIBLING, n_local=n)


def _gather_far_job(stacked):
    n = len(stacked)
    halves = [s.shape[1] // 2 for s in stacked]

    def copies(ins, outs, send, recv, local):
        del ins, local
        x, y, c, _ = _place()
        near, far = _near_far(x, y, c)

        def copy(w, k, chip):
            blk = outs[w].at[_chip_index(*chip), pl.ds(c * halves[w], halves[w]), :]
            return pltpu.make_async_remote_copy(
                src_ref=blk, dst_ref=blk, send_sem=send.at[2 * w + k], recv_sem=recv.at[2 * w + k],
                device_id=(*far, c), device_id_type=MESH)

        sends = [copy(w, k, chip) for w in range(n) for k, chip in enumerate(((x, y), near))]
        arrivals = [copy(w, k, chip) for w in range(n) for k, chip in enumerate((far, (1 - x, 1 - y)))]
        return sends, arrivals, []

    return _Job(stacked, [jax.ShapeDtypeStruct(s.shape, s.dtype) for s in stacked], 2 * n, copies, NEIGHBOURS,
                aliases={w: w for w in range(n)})


def _gather_pass_job(stacked):
    n = len(stacked)
    halves = [s.shape[1] // 2 for s in stacked]

    def copies(ins, outs, send, recv, local):
        del ins, local
        x, y, c, chips = _place()

        def copy(w, j, chip, pc, to):
            blk = outs[w].at[_chip_index(*chip), pl.ds(pc * halves[w], halves[w]), :]
            return pltpu.make_async_remote_copy(
                src_ref=blk, dst_ref=blk, send_sem=send.at[3 * w + j], recv_sem=recv.at[3 * w + j], device_id=to,
                device_id_type=MESH)

        sends = [copy(w, j, chip, c, (x, y, 1 - c)) for w in range(n) for j, chip in enumerate(chips)]
        arrivals = [copy(w, j, chip, 1 - c, (x, y, c)) for w in range(n) for j, chip in enumerate(chips)]
        return sends, arrivals, []

    return _Job(stacked, [jax.ShapeDtypeStruct(s.shape, s.dtype) for s in stacked], 3 * n, copies, SIBLING,
                aliases={w: w for w in range(n)})


def _gather_small_job(block):
    def copies(ins, outs, send, recv, local):
        x, y, c, chips = _place()

        def copy(j, chip_from, to):
            return pltpu.make_async_remote_copy(
                src_ref=ins[0], dst_ref=outs[0].at[_chip_index(*chip_from)], send_sem=send.at[j],
                recv_sem=recv.at[j], device_id=to, device_id_type=MESH)

        own = [pltpu.make_async_copy(ins[0], outs[0].at[_chip_index(x, y)], local.at[0])]
        sends = [copy(j, (x, y), (*chip, c)) for j, chip in enumerate(chips)]
        arrivals = [copy(j, chip, (x, y, c)) for j, chip in enumerate(chips)]
        return sends, arrivals, own

    return _Job([block], [jax.ShapeDtypeStruct((N_CHIPS,) + block.shape, block.dtype)], 3, copies, OTHER_CHIPS,
                n_local=1)


def _pair_send_job(grads):
    n = len(grads)
    halves = [g.shape[1] // 2 for g in grads]

    def copies(ins, outs, send, recv, local):
        del local
        x, y, c, _ = _place()
        sends = [pltpu.make_async_remote_copy(
            src_ref=ins[w].at[:, pl.ds((1 - c) * halves[w], halves[w]), :], dst_ref=outs[w], send_sem=send.at[w],
            recv_sem=recv.at[w], device_id=(x, y, 1 - c), device_id_type=MESH) for w in range(n)]
        return sends, sends, []

    return _Job(grads, [jax.ShapeDtypeStruct((N_CHIPS, h, g.shape[2]), g.dtype) for g, h in zip(grads, halves)], n,
                copies, SIBLING)


def _row_block(rows, limit=256):
    return min(rows, limit)


def _pair_add(name, core, mine, theirs):
    _, _, h, cols = mine.shape
    rb = _row_block(h, 512)

    def body(core_ref, a_ref, b_ref, o_ref):
        del core_ref
        o_ref[...] = (a_ref[...].astype(F32) + b_ref[...].astype(F32)).astype(BF16)

    return pl.pallas_call(
        body, name=name,
        grid_spec=pltpu.PrefetchScalarGridSpec(
            num_scalar_prefetch=1, grid=(N_CHIPS, h // rb),
            in_specs=[pl.BlockSpec((None, None, rb, cols), lambda k, r, core_ref: (k, core_ref[0], r, 0)),
                      pl.BlockSpec((None, rb, cols), lambda k, r, core_ref: (k, r, 0))],
            out_specs=pl.BlockSpec((None, rb, cols), lambda k, r, core_ref: (k, r, 0))),
        out_shape=jax.ShapeDtypeStruct(theirs.shape, BF16),
        compiler_params=_params(2),
    )(core, mine, theirs)


def _sequencer_call(name, collective_id, job):
    ins = [jax.new_ref(a, memory_space=pltpu.MemorySpace.HBM) for a in job.inputs]
    outs = [ins[{o: i for i, o in job.aliases.items()}[k]] if k in job.aliases.values()
            else jax.empty_ref(shape, memory_space=pltpu.MemorySpace.HBM) for k, shape in enumerate(job.out_shape)]

    @pl.kernel(mesh=plsc.ScalarSubcoreMesh(axis_name="sequencer", num_cores=1), name=name,
               scratch_types=(pltpu.SemaphoreType.DMA((job.n_sem,)), pltpu.SemaphoreType.DMA((job.n_sem,)),
                              pltpu.SemaphoreType.DMA((max(job.n_local, 1),))),
               compiler_params=pltpu.CompilerParams(collective_id=collective_id))
    def launch(send, recv, local):
        x, y, c, _ = _place()
        barrier = pltpu.get_barrier_semaphore()
        for dx, dy, dc in job.peers:
            pl.semaphore_signal(barrier, inc=1, device_id=(x ^ dx, y ^ dy, c ^ dc), device_id_type=MESH)
        pl.semaphore_wait(barrier, len(job.peers))
        sends, arrivals, own = job.copies(ins, outs, send, recv, local)
        for cp in own + sends:
            cp.start()
        for cp in arrivals:
            cp.wait_recv()
        for cp in sends:
            cp.wait_send()
        for cp in own:
            cp.wait()

    launch()
    return [ref[...] for ref in outs]


def _chip_exchange_job(sums):
    n = len(sums)

    def copies(ins, outs, send, recv, local):
        del local
        _, _, c, chips = _place()
        sends = [pltpu.make_async_remote_copy(
            src_ref=ins[w].at[_chip_index(*chip)], dst_ref=outs[w].at[j], send_sem=send.at[3 * w + j],
            recv_sem=recv.at[3 * w + j], device_id=(*chip, c), device_id_type=MESH)
            for w in range(n) for j, chip in enumerate(chips)]
        return sends, sends, []

    return _Job(sums, [jax.ShapeDtypeStruct((N_CHIPS - 1,) + s.shape[1:], s.dtype) for s in sums], 3 * n, copies,
                OTHER_CHIPS)


def _chip_sum(name, place, mine, theirs):
    _, h, cols = mine.shape
    rb = _row_block(h, 512)

    def body(place_ref, p_ref, q_ref, o_ref):
        del place_ref
        acc = p_ref[...].astype(F32)
        for j in range(N_CHIPS - 1):
            acc = acc + q_ref[j].astype(F32)
        o_ref[...] = acc

    return pl.pallas_call(
        body, name=name,
        grid_spec=pltpu.PrefetchScalarGridSpec(
            num_scalar_prefetch=1, grid=(h // rb,),
            in_specs=[pl.BlockSpec((None, rb, cols), lambda r, place_ref: (place_ref[0], r, 0)),
                      pl.BlockSpec((N_CHIPS - 1, rb, cols), lambda r, place_ref: (0, r, 0))],
            out_specs=pl.BlockSpec((None, rb, cols), lambda r, place_ref: (place_ref[1], r, 0))),
        out_shape=jax.ShapeDtypeStruct((2, h, cols), F32),
        compiler_params=_params(),
    )(place, mine, theirs)


def _share_job(bufs):
    n = len(bufs)

    def copies(ins, outs, send, recv, local):
        del ins, local
        x, y, c, _ = _place()

        def copy(w, half):
            return pltpu.make_async_remote_copy(
                src_ref=outs[w].at[half], dst_ref=outs[w].at[half], send_sem=send.at[w], recv_sem=recv.at[w],
                device_id=(x, y, 1 - c), device_id_type=MESH)

        return [copy(w, c) for w in range(n)], [copy(w, 1 - c) for w in range(n)], []

    return _Job(bufs, [jax.ShapeDtypeStruct(b.shape, b.dtype) for b in bufs], n, copies, SIBLING,
                aliases={w: w for w in range(n)})


SMALL_ROWS = 24
ROW_G1, ROW_CW, ROW_CB, ROW_BR, ROW_BI, ROW_LAM, ROW_LG, ROW_LB, ROW_G2, ROW_G3, ROW_LOSS, ROW_BS = (
    0, 1, 5, 6, 7, 8, 9, 10, 11, 12, 13, 16)
N_DEV = 8


def _pack_small(dcw, dcb, dbr, dbi, dlam, dlg, dlb, dg2, dg3, loss, dbs):
    def body(dcw_ref, dcb_ref, dbr_ref, dbi_ref, dlam_ref, dlg_ref, dlb_ref, dg2_ref, dg3_ref, loss_ref, dbs_ref, out):
        out[...] = jnp.zeros((SMALL_ROWS, D_MODEL), F32)
        for row, ref in ((ROW_CB, dcb_ref), (ROW_BR, dbr_ref), (ROW_BI, dbi_ref), (ROW_LAM, dlam_ref),
                         (ROW_LG, dlg_ref), (ROW_LB, dlb_ref), (ROW_G2, dg2_ref), (ROW_G3, dg3_ref)):
            out[row:row + 1, :] = ref[...]
        out[ROW_CW:ROW_CW + CONV_WIDTH, :] = dcw_ref[0:CONV_WIDTH, :]
        out[ROW_LOSS:ROW_LOSS + 1, 0:128] = loss_ref[0:1, :]
        out[ROW_BS:ROW_BS + GROUPS, 0:128] = jnp.transpose(dbs_ref[...])[0:GROUPS, :]

    vm = pl.BlockSpec(memory_space=pltpu.VMEM)
    return pl.pallas_call(
        body, name="pack_small", in_specs=[vm] * 11, out_specs=vm,
        out_shape=jax.ShapeDtypeStruct((SMALL_ROWS, D_MODEL), F32),
    )(dcw, dcb, dbr, dbi, dlam, dlg, dlb, dg2, dg3, loss, dbs)


def _gather_all_job(blocks):
    n = len(blocks)
    flips = [(dx, dy, dc) for dx in (0, 1) for dy in (0, 1) for dc in (0, 1)][1:]

    def copies(ins, outs, send, recv, local):
        x, y, c, _ = _place()
        me = 4 * x + 2 * y + c
        sends, arrivals, own = [], [], []
        for w in range(n):
            own.append(pltpu.make_async_copy(ins[w], outs[w].at[me], local.at[w]))
            for k, (dx, dy, dc) in enumerate(flips):
                peer = (x ^ dx, y ^ dy, c ^ dc)
                sem = dict(send_sem=send.at[7 * w + k], recv_sem=recv.at[7 * w + k])
                sends.append(pltpu.make_async_remote_copy(
                    src_ref=ins[w], dst_ref=outs[w].at[me], device_id=peer, device_id_type=MESH, **sem))
                arrivals.append(pltpu.make_async_remote_copy(
                    src_ref=ins[w], dst_ref=outs[w].at[4 * peer[0] + 2 * peer[1] + peer[2]], device_id=peer,
                    device_id_type=MESH, **sem))
        return sends, arrivals, own

    return _Job(blocks, [jax.ShapeDtypeStruct((N_DEV,) + b.shape, b.dtype) for b in blocks], 7 * n, copies,
                OTHER_CHIPS + SIBLING + tuple((dx, dy, 1) for dx, dy, _ in OTHER_CHIPS), n_local=n)


def _sum_small(vec_all, ws_all, dg1_all):
    def body(vec_ref, ws_ref, dg1_ref, vec_out, ws_out):
        vec, ws, dg1 = vec_ref[0], ws_ref[0], dg1_ref[0]
        for d in range(1, N_DEV):
            vec, ws, dg1 = vec + vec_ref[d], ws + ws_ref[d], dg1 + dg1_ref[d]
        vec_out[...] = vec
        vec_out[ROW_G1:ROW_G1 + 1, :] = dg1
        ws_out[...] = ws

    vm = pl.BlockSpec(memory_space=pltpu.VMEM)
    return pl.pallas_call(
        body, name="sum_small", in_specs=[vm] * 3, out_specs=[vm, vm],
        out_shape=[jax.ShapeDtypeStruct(vec_all.shape[1:], F32), jax.ShapeDtypeStruct(ws_all.shape[1:], F32)],
    )(vec_all, ws_all, dg1_all)


def _adamw_math(w, g, m, v):
    m = ADAM_B1 * m + (1.0 - ADAM_B1) * g
    v = ADAM_B2 * v + (1.0 - ADAM_B2) * (g * g)
    m_hat = m / (1.0 - ADAM_B1 ** ADAM_STEP)
    v_hat = v / (1.0 - ADAM_B2 ** ADAM_STEP)
    delta = (-ADAM_LR) * (m_hat / (jnp.sqrt(v_hat) + ADAM_EPS) + ADAM_WD * w)
    return delta, m, v


def _adamw(name, g, w, m, v, jobs=()):
    rows, cols = w.shape
    rb = _row_block(rows)

    def body(g_ref, w_ref, m_ref, v_ref, d_ref, nm_ref, nv_ref):
        d_ref[...], nm_ref[...], nv_ref[...] = _adamw_math(w_ref[...], g_ref[...], m_ref[...], v_ref[...])

    blk = pl.BlockSpec((rb, cols), lambda r: (r, 0))
    return _fused_call(
        body, jobs, name=name, grid=(rows // rb,), in_specs=[blk] * 4, out_specs=[blk] * 3,
        out_shape=[jax.ShapeDtypeStruct(w.shape, F32)] * 3, compiler_params=_params(),
    )(g, w, m, v)


def _adamw_small(grads, ws, ms, vs):
    n = len(grads)

    def body(*refs):
        g_refs, w_refs, m_refs, v_refs = refs[:n], refs[n:2 * n], refs[2 * n:3 * n], refs[3 * n:4 * n]
        outs = refs[4 * n:]
        for p in range(n):
            d, nm, nv = _adamw_math(w_refs[p][...], g_refs[p][...], m_refs[p][...], v_refs[p][...])
            outs[p][...] = d
            outs[n + p][...] = nm
            outs[2 * n + p][...] = nv

    vm = pl.BlockSpec(memory_space=pltpu.VMEM)
    shapes = [jax.ShapeDtypeStruct(w.shape, F32) for w in ws]
    out = pl.pallas_call(
        body, name="adamw_small", in_specs=[vm] * (4 * n), out_specs=[vm] * (3 * n), out_shape=shapes * 3,
    )(*grads, *ws, *ms, *vs)
    return out[:n], out[n:2 * n], out[2 * n:]


def _unstack_heads(w_st):
    per = HEAD_DIM // N_CHIPS
    return w_st.reshape(N_CHIPS, HEADS, per, HEAD_DIM).transpose(1, 0, 2, 3).reshape(HEADS, HEAD_DIM, HEAD_DIM)


def _stack_heads(w):
    per = HEAD_DIM // N_CHIPS
    return w.reshape(HEADS, N_CHIPS, per, HEAD_DIM).transpose(1, 0, 2, 3).reshape(N_CHIPS, HEADS * per, HEAD_DIM)


def kernel(x, norm_mix_g, w_in, conv_w, conv_b, w_rgate, b_rgate, w_igate, b_igate, lru_lambda, w_out_a, sgu_ln_g, sgu_ln_b, sgu_w_s, sgu_b_s, w_out_b, w_out, norm_mlp_g, w_up, w_down, norm_final_g, loss_target, m_norm_mix_g, m_w_in, m_conv_w, m_conv_b, m_w_rgate, m_b_rgate, m_w_igate, m_b_igate, m_lru_lambda, m_w_out_a, m_sgu_ln_g, m_sgu_ln_b, m_sgu_w_s, m_sgu_b_s, m_w_out_b, m_w_out, m_norm_mlp_g, m_w_up, m_w_down, m_norm_final_g, v_norm_mix_g, v_w_in, v_conv_w, v_conv_b, v_w_rgate, v_b_rgate, v_w_igate, v_b_igate, v_lru_lambda, v_w_out_a, v_sgu_ln_g, v_sgu_ln_b, v_sgu_w_s, v_sgu_b_s, v_w_out_b, v_w_out, v_norm_mlp_g, v_w_up, v_w_down, v_norm_final_g):
    chip = _chip_index(lax.axis_index("x"), lax.axis_index("y"))
    core = lax.axis_index("c")
    quarter_h = HEAD_DIM // N_CHIPS
    quarter_d = D_MODEL // N_CHIPS

    as_2d = lambda a: a.reshape(-1, a.shape[-1])
    big_w = [as_2d(w) for w in (w_in, w_rgate, w_igate, w_out_a, w_out_b, w_out, w_up, w_down)]
    big_m = [as_2d(w) for w in (m_w_in, m_w_rgate, m_w_igate, m_w_out_a, m_w_out_b, m_w_out, m_w_up, m_w_down)]
    big_v = [as_2d(w) for w in (v_w_in, v_w_rgate, v_w_igate, v_w_out_a, v_w_out_b, v_w_out, v_w_up, v_w_down)]

    packed = jnp.concatenate([conv_w[0], b_rgate[0], b_igate[0]], axis=1)
    packed = jnp.concatenate([packed, jnp.zeros_like(packed)], axis=0)
    s_in, s_r, s_i, s_oa, s_ob, s_out, s_up, s_down = [w.astype(BF16) for w in big_w]
    xs, target = x[0], loss_target[0]
    g3 = norm_final_g.reshape(1, D_MODEL)
    bias_s = jnp.broadcast_to(jnp.transpose(sgu_b_s[0])[:, :, None], (CHUNK, GROUPS, GROUP_DIM)).reshape(CHUNK, D_MODEL)
    core_arr = core.reshape(1).astype(jnp.int32)
    place = jnp.stack([chip, core]).astype(jnp.int32)
    quarter = lambda g: g.reshape(N_CHIPS, D_MODEL // N_CHIPS, D_MODEL)

    def pair_add(nm, g, from_sibling):
        return _pair_add("pair_add_" + nm, core_arr, g.reshape(N_CHIPS, 2, g.shape[1] // 2, g.shape[2]), from_sibling)

    def chip_sum(nm, pair, from_chips):
        return _chip_sum("chip_sum_" + nm, place, pair, from_chips)

    order = jnp.stack([chip, chip ^ 2, chip ^ 1, chip ^ 3]).astype(jnp.int32)
    (z, n1, (w_in_st, wr_st, wi_st)), ((packed_all,), late) = _fwd_in(
        xs, norm_mix_g, [s_in, s_r, s_i], order,
        jobs=[_gather_small_job(packed), _gather_near_job([s_oa, s_ob, s_out])])
    pick = lambda lo, hi: packed_all[:, :HEADS, lo:hi].transpose(1, 0, 2).reshape(HEADS, -1)
    conv_w_full = pick(0, quarter_d)
    br_full = pick(quarter_d, quarter_d + quarter_h).reshape(1, D_MODEL)
    bi_full = pick(quarter_d + quarter_h, quarter_d + 2 * quarter_h).reshape(1, D_MODEL)
    wr, wi = _unstack_heads(wr_st), _unstack_heads(wi_st)
    lru = (conv_w_full, conv_b, wr, br_full, wi, bi_full, lru_lambda)
    sgu = (sgu_ln_g, sgu_ln_b, sgu_w_s[0], bias_s)

    after = lambda arrays, result: lax.optimization_barrier((arrays, result))[0]
    mlp_w = _sequencer_call("gather_mlp_near", 7, _gather_near_job(after([s_up, s_down], n1)))
    (ya, *saved), (late,) = _fwd_lru(z, *lru, jobs=[_gather_far_job(late)])
    mlp_w = _sequencer_call("gather_mlp_far", 8, _gather_far_job(after(mlp_w, ya)))
    yb, (late,) = _fwd_sgu(z, *sgu, jobs=[_gather_pass_job(late)])
    w_up_st, w_dn = _sequencer_call("gather_mlp_pass", 9, _gather_pass_job(after(mlp_w, yb)))
    w_dn = w_dn.reshape(D_FF, D_MODEL)
    w_oa, w_ob, w_o = [w.reshape(D_MODEL, D_MODEL) for w in late]
    (pa, pb, h1, n2), _ = _fwd_merge(ya, yb, z, xs, w_oa, w_ob, w_o, norm_mlp_g)
    (act, dup, dh2b, dh1, loss_part, dg3, dg2), _ = _mlp(n2, h1, target, w_up_st, w_dn, norm_mlp_g, g3)

    d_down, _ = _weight_grad("dw_down", act, dh2b, N_CHIPS, True, False, D_MODEL)
    r_down, = _sequencer_call("send_w_down", 10, _pair_send_job([d_down]))
    d_up, _ = _weight_grad("dw_up", n2, dup, N_CHIPS, False, True, D_MODEL)
    r_up, = _sequencer_call("send_w_up", 11, _pair_send_job([d_up]))
    p_down, p_up = pair_add("w_down", d_down, r_down), pair_add("w_up", d_up, r_up)
    (dz, merged, dpa, dpb, dh1b, dlg, dlb, dws, dbs, dcw, dcb, dwr, dbr, dwi, dbi, dlam), ((q_up, q_down),) = _bwd_mix(
        dh1, pa, pb, z, *saved, w_oa, w_ob, w_o, *sgu, conv_w_full, wr, wi, lru_lambda,
        jobs=[_chip_exchange_job([p_up, p_down])])
    half_up, half_down = chip_sum("w_up", p_up, q_up), chip_sum("w_down", p_down, q_down)
    names = ("w_in", "w_rgate", "w_igate", "w_out_a", "w_out_b", "w_out", "w_up", "w_down")
    d_out, ((full_up, full_down),) = _weight_grad(
        "dw_out", merged, dh1b, 1, False, False, D_MODEL, jobs=[_share_job([half_up, half_down])])
    d_oa, _ = _weight_grad("dw_out_a", ya, dpa, 1, False, False, D_MODEL)
    d_ob, _ = _weight_grad("dw_out_b", yb, dpb, 1, False, False, D_MODEL)
    mids = [quarter(d_oa), quarter(d_ob), quarter(d_out)]
    r_mids = _sequencer_call("send_mids", 1, _pair_send_job(mids))
    gates = [_stack_heads(dwr).astype(BF16), _stack_heads(dwi).astype(BF16)]
    small = _pack_small(dcw, dcb, dbr, dbi, dlam, dlg, dlb, dg2, dg3, loss_part, dbs)
    p_mids = [pair_add(nm, g, r) for nm, g, r in zip(names[3:6], mids, r_mids)]
    q_mids = _sequencer_call("exchange_mids", 2, _chip_exchange_job(p_mids))
    d_in, (r_gates, (vec_all, ws_all)) = _weight_grad(
        "dw_in", n1, dz, N_CHIPS, False, True, IN_SHARD,
        jobs=[_pair_send_job(gates), _gather_all_job([small, dws])])
    r_in, = _sequencer_call("send_w_in", 3, _pair_send_job([d_in]))
    adam_args = {nm: (w, m, v) for nm, w, m, v in zip(names, big_w, big_m, big_v)}

    def adamw(nm, g):
        w, m, v = adam_args[nm]
        g = g.reshape(w.shape)
        return g, _adamw("adamw_" + nm, g, w, m, v)[0]

    p_gates = [pair_add(nm, g, r) for nm, g, r in zip(names[1:3], gates, r_gates)]
    half_mids = [chip_sum(nm, p, q) for nm, p, q in zip(names[3:6], p_mids, q_mids)]
    full_mids = _sequencer_call("share_mids", 12, _share_job(half_mids))
    p_first = [pair_add("w_in", d_in, r_in)] + p_gates
    q_first = _sequencer_call("exchange_w_in", 4, _chip_exchange_job(p_first))
    (grad_x, dg1), _ = _bwd_in(dz, xs, dh1, w_in_st, norm_mix_g)
    dg1_all, = _sequencer_call("gather_dg1", 6, _gather_all_job([dg1]))
    done = {nm: adamw(nm, f) for nm, f in zip(("w_up", "w_down") + names[3:6], [full_up, full_down] + full_mids)}
    q_first = after(q_first, [out[0] for _, out in done.values()])
    half_first = [chip_sum(nm, p, q) for nm, p, q in zip(names[:3], p_first, q_first)]
    full_first = _sequencer_call("share_last", 5, _share_job(half_first))
    done.update({nm: adamw(nm, f) for nm, f in zip(names[:3], full_first)})
    full, big_out = [done[nm][0] for nm in names], [done[nm][1] for nm in names]

    vec, ws_sum = _sum_small(vec_all, ws_all, dg1_all)
    row = lambda r: vec[r:r + 1]
    shard = lambda a, width: lax.dynamic_slice_in_dim(a, chip * width, width, axis=1)
    g_small = dict(
        norm_mix_g=row(ROW_G1), conv_w=shard(vec[ROW_CW:ROW_CW + CONV_WIDTH], quarter_d), conv_b=row(ROW_CB),
        b_rgate=shard(row(ROW_BR).reshape(HEADS, HEAD_DIM), quarter_h),
        b_igate=shard(row(ROW_BI).reshape(HEADS, HEAD_DIM), quarter_h), lru_lambda=row(ROW_LAM),
        sgu_ln_g=row(ROW_LG), sgu_ln_b=row(ROW_LB),
        sgu_w_s=ws_sum.reshape(CHUNK, GROUPS, CHUNK).transpose(1, 0, 2).reshape(GROUPS * CHUNK, CHUNK),
        sgu_b_s=vec[ROW_BS:ROW_BS + GROUPS, 0:CHUNK], norm_mlp_g=row(ROW_G2), norm_final_g=row(ROW_G3))
    loss = vec[ROW_LOSS, 0]
    small_names = list(g_small)
    given = dict(
        norm_mix_g=(norm_mix_g, m_norm_mix_g, v_norm_mix_g), conv_w=(conv_w, m_conv_w, v_conv_w),
        conv_b=(conv_b, m_conv_b, v_conv_b), b_rgate=(b_rgate, m_b_rgate, v_b_rgate),
        b_igate=(b_igate, m_b_igate, v_b_igate), lru_lambda=(lru_lambda, m_lru_lambda, v_lru_lambda),
        sgu_ln_g=(sgu_ln_g, m_sgu_ln_g, v_sgu_ln_g), sgu_ln_b=(sgu_ln_b, m_sgu_ln_b, v_sgu_ln_b),
        sgu_w_s=(sgu_w_s, m_sgu_w_s, v_sgu_w_s), sgu_b_s=(sgu_b_s, m_sgu_b_s, v_sgu_b_s),
        norm_mlp_g=(norm_mlp_g, m_norm_mlp_g, v_norm_mlp_g), norm_final_g=(norm_final_g, m_norm_final_g, v_norm_final_g))
    g2d = [g_small[nm] for nm in small_names]
    to2d = lambda a, g: a.reshape(g.shape)
    d_s, m_s, v_s = _adamw_small(
        g2d, *[[to2d(given[nm][q], g) for nm, g in zip(small_names, g2d)] for q in range(3)])

    shapes = dict(
        norm_mix_g=norm_mix_g, w_in=w_in, conv_w=conv_w, conv_b=conv_b, w_rgate=w_rgate, b_rgate=b_rgate,
        w_igate=w_igate, b_igate=b_igate, lru_lambda=lru_lambda, w_out_a=w_out_a, sgu_ln_g=sgu_ln_g,
        sgu_ln_b=sgu_ln_b, sgu_w_s=sgu_w_s, sgu_b_s=sgu_b_s, w_out_b=w_out_b, w_out=w_out, norm_mlp_g=norm_mlp_g,
        w_up=w_up, w_down=w_down, norm_final_g=norm_final_g)
    grads, deltas, new_m, new_v = {}, {}, {}, {}
    for nm, g, (d, nmom, nvar) in zip(names, full, big_out):
        grads[nm], deltas[nm], new_m[nm], new_v[nm] = g, d, nmom, nvar
    for p, nm in enumerate(small_names):
        grads[nm], deltas[nm], new_m[nm], new_v[nm] = g2d[p], d_s[p], m_s[p], v_s[p]
    order = list(shapes)
    out = [loss, grad_x[None]]
    for group in (grads, deltas, new_m, new_v):
        out += [group[nm].reshape(shapes[nm].shape) for nm in order]
    return tuple(out)
```

```python
import functools

import jax
import jax.numpy as jnp
from jax import lax
from jax.experimental import pallas as pl
from jax.experimental.pallas import tpu as pltpu
from jax.experimental.pallas import tpu_sc as plsc

F32 = jnp.float32
BF16 = jnp.bfloat16
MESH = pl.DeviceIdType.MESH

D_MODEL = 1024
D_IN = 6 * D_MODEL
D_FF = 4 * D_MODEL
N_CHIPS = 4
IN_SHARD = D_IN // N_CHIPS
HEADS = 4
HEAD_DIM = D_MODEL // HEADS
GROUPS = 4
GROUP_DIM = D_MODEL // GROUPS
CHUNK = 128
CONV_WIDTH = 4
LRU_C = 8.0
NORM_EPS = 1e-6
LN_EPS = 1e-5

ADAM_LR = 0.001
ADAM_B1 = 0.9
ADAM_B2 = 0.999
ADAM_EPS = 1e-08
ADAM_WD = 0.01
ADAM_STEP = 10

SUBLANES = 8
MM_TILE = 512
IN_TILE = 1024
SEQ_TILE = 256
DW_TILE = 2048
VMEM_LIMIT_BYTES = 56 * 1024 * 1024

GELU_K0 = 0.7978845608028654
GELU_K1 = 0.044715


def _params(n_grid_axes=1):
    return pltpu.CompilerParams(
        dimension_semantics=("arbitrary",) * n_grid_axes, vmem_limit_bytes=VMEM_LIMIT_BYTES)


def _resident(shape):
    nd = len(shape)
    return pl.BlockSpec(shape, lambda *_: (0,) * nd, pipeline_mode=pl.Buffered(1))


def _const(shape):
    nd = len(shape)
    return pl.BlockSpec(shape, lambda *_: (0,) * nd)


def _dot(a, b):
    return jnp.dot(a, b, preferred_element_type=F32)


def _dot_nt(a, b):
    return lax.dot_general(a, b, (((1,), (1,)), ((), ())), preferred_element_type=F32)


def _dot_tn(a, b):
    return lax.dot_general(a, b, (((0,), (0,)), ((), ())), preferred_element_type=F32)


def _gelu(x):
    t = jnp.tanh(GELU_K0 * x * (1.0 + GELU_K1 * x * x))
    return 0.5 * x * (1.0 + t)


def _gelu_and_grad(x):
    x2 = x * x
    t = jnp.tanh(GELU_K0 * x * (1.0 + GELU_K1 * x2))
    g = 0.5 * x * (1.0 + t)
    dg = 0.5 * (1.0 + t) + 0.5 * x * (1.0 - t * t) * (GELU_K0 * (1.0 + 3.0 * GELU_K1 * x2))
    return g, dg


def _rms(x):
    r = lax.rsqrt(jnp.mean(x * x, axis=-1, keepdims=True) + NORM_EPS)
    return x * r, r


def _rms_bwd(dn, xhat, r):
    return r * (dn - xhat * jnp.mean(dn * xhat, axis=-1, keepdims=True))


def _col_sum(v):
    return jnp.sum(v, axis=0, keepdims=True)


def _shift_down(x, tail8, k):
    xs = pltpu.roll(x, k, 0)
    ts = pltpu.roll(tail8, k, 0)
    ridx = lax.broadcasted_iota(jnp.int32, tail8.shape, 0)
    head = jnp.where(ridx < k, ts, xs[0:SUBLANES])
    return jnp.concatenate([head, xs[SUBLANES:]], axis=0)


def _shift_up(x, head8, k):
    n = x.shape[0]
    xs = pltpu.roll(x, n - k, 0)
    hs = pltpu.roll(head8, SUBLANES - k, 0)
    ridx = lax.broadcasted_iota(jnp.int32, head8.shape, 0)
    last = jnp.where(ridx >= SUBLANES - k, hs, xs[n - SUBLANES:n])
    return jnp.concatenate([xs[:n - SUBLANES], last], axis=0)


def _scan_forward(a, b, carry):
    n, cols = a.shape
    groups = n // SUBLANES
    a = a.reshape(groups, SUBLANES, cols)
    b = b.reshape(groups, SUBLANES, cols)
    sub = lax.broadcasted_iota(jnp.int32, a.shape, 1)
    for s in (1, 2, 4):
        a_s = pltpu.roll(a, s, 1)
        b_s = pltpu.roll(b, s, 1)
        m = sub >= s
        b = jnp.where(m, a * b_s + b, b)
        a = jnp.where(m, a * a_s, a)
    out = []
    for g in range(groups):
        h = a[g] * carry + b[g]
        out.append(h)
        carry = h[SUBLANES - 1:SUBLANES]
    return jnp.concatenate(out, axis=0), carry


def _scan_backward(a, b, carry):
    n, cols = a.shape
    groups = n // SUBLANES
    a = a.reshape(groups, SUBLANES, cols)
    b = b.reshape(groups, SUBLANES, cols)
    sub = lax.broadcasted_iota(jnp.int32, a.shape, 1)
    for s in (1, 2, 4):
        a_s = pltpu.roll(a, SUBLANES - s, 1)
        b_s = pltpu.roll(b, SUBLANES - s, 1)
        m = sub < SUBLANES - s
        b = jnp.where(m, a * b_s + b, b)
        a = jnp.where(m, a * a_s, a)
    out = [None] * groups
    for g in reversed(range(groups)):
        h = a[g] * carry + b[g]
        out[g] = h
        carry = h[0:1]
    return jnp.concatenate(out, axis=0), carry


def _softplus_neg(lam):
    e = jnp.exp(-jnp.abs(lam))
    u = 1.0 + e
    log1p_e = jnp.where(u == 1.0, e, jnp.log(u) * (e / jnp.where(u == 1.0, 1.0, u - 1.0)))
    return jnp.maximum(-lam, 0.0) + log1p_e


def _lru_gates(xa, tail8, cw_ref, cb_ref, wr_ref, br_ref, wi_ref, bi_ref, lam_ref):
    cw = cw_ref[...]
    xc = cb_ref[...] + cw[0:1] * xa
    for k in range(1, CONV_WIDTH):
        xc = xc + cw[k:k + 1] * _shift_down(xa, tail8, k)
    xcb = xc.astype(BF16)
    pre_r, pre_i = [], []
    for h in range(HEADS):
        cols = slice(h * HEAD_DIM, (h + 1) * HEAD_DIM)
        pre_r.append(_dot(xcb[:, cols], wr_ref[h]))
        pre_i.append(_dot(xcb[:, cols], wi_ref[h]))
    r = jax.nn.sigmoid(jnp.concatenate(pre_r, axis=1) + br_ref[...])
    ig = jax.nn.sigmoid(jnp.concatenate(pre_i, axis=1) + bi_ref[...])
    _, a, mult = _decay(r, lam_ref)
    return xc, r, ig, a, mult


def _decay(r, lam_ref):
    sp = _softplus_neg(lam_ref[...])
    log_a = ((-LRU_C) * sp) * r
    a = jnp.exp(log_a)
    th = jnp.tanh(log_a)
    return sp, a, jnp.sqrt((-2.0 * th) / (1.0 - th))


class _Job:
    def __init__(self, inputs, out_shape, n_sem, copies, peers, aliases=None, n_local=0):
        self.inputs, self.out_shape, self.n_sem, self.copies = list(inputs), list(out_shape), n_sem, copies
        self.aliases, self.n_local = dict(aliases or {}), n_local
        self.peers = tuple(peers)


def _fused_call(body, jobs, *, name, grid, in_specs, out_specs, out_shape, scratch_shapes=(),
                input_output_aliases=None, compiler_params=None, n_prefetch=0, jobs_start_after=None):
    single = not isinstance(out_shape, (list, tuple))
    out_specs = [out_specs] if single else list(out_specs)
    out_shape = [out_shape] if single else list(out_shape)
    n_scr = len(scratch_shapes)
    in_specs, scratch_shapes = list(in_specs), list(scratch_shapes)
    n_in, n_out = len(in_specs), len(out_shape)
    aliases = dict(input_output_aliases or {})
    in_at, out_at = [], []
    for job in jobs:
        in_at.append(len(in_specs))
        out_at.append(len(out_shape))
        for i, o in job.aliases.items():
            aliases[n_prefetch + len(in_specs) + i] = len(out_shape) + o
        in_specs += [ANY] * len(job.inputs)
        out_specs += [ANY] * len(job.out_shape)
        out_shape += job.out_shape
        scratch_shapes += [pltpu.SemaphoreType.DMA((job.n_sem,)), pltpu.SemaphoreType.DMA((job.n_sem,)),
                           pltpu.SemaphoreType.DMA((max(job.n_local, 1),))]
    n_in_all, n_out_all = len(in_specs), len(out_shape)

    def full_body(*refs):
        prefetch, refs = refs[:n_prefetch], refs[n_prefetch:]
        ins, outs, scr = refs[:n_in_all], refs[n_in_all:n_in_all + n_out_all], refs[n_in_all + n_out_all:]

        def copies(q):
            job = jobs[q]
            return job.copies(ins[in_at[q]:in_at[q] + len(job.inputs)], outs[out_at[q]:out_at[q] + len(job.out_shape)],
                              *scr[n_scr + 3 * q:n_scr + 3 * q + 3])

        def start():
            for q in range(len(jobs)):
                sends, _, local = copies(q)
                for cp in local + sends:
                    cp.start()

        def finish():
            every = [copies(q) for q in range(len(jobs))]
            for _, arrivals, _ in every:
                for cp in arrivals:
                    cp.wait_recv()
            for sends, _, local in every:
                for cp in sends:
                    cp.wait_send()
                for cp in local:
                    cp.wait()

        if not grid:
            start()
            finish()
            return
        ids = [pl.program_id(a) for a in range(len(grid))]
        at_step = lambda step: functools.reduce(jnp.logical_and, [i == k for i, k in zip(ids, step)])
        if jobs and jobs_start_after is None:
            pl.when(at_step((0,) * len(grid)))(start)
        body(*prefetch, *ins[:n_in], *outs[:n_out], *scr[:n_scr])
        if jobs and jobs_start_after is not None:
            pl.when(at_step(jobs_start_after))(start)
        if jobs:
            pl.when(functools.reduce(jnp.logical_and, [i == g - 1 for i, g in zip(ids, grid)]))(finish)

    if n_prefetch:
        layout = dict(grid_spec=pltpu.PrefetchScalarGridSpec(
            num_scalar_prefetch=n_prefetch, grid=grid, in_specs=in_specs, out_specs=out_specs,
            scratch_shapes=scratch_shapes))
    else:
        layout = dict(grid=grid, in_specs=in_specs, out_specs=out_specs, scratch_shapes=scratch_shapes)
    call = pl.pallas_call(
        full_body, name=name, out_shape=out_shape, input_output_aliases=aliases, compiler_params=compiler_params,
        **layout)

    def run(*args):
        res = call(*args, *[a for job in jobs for a in job.inputs])
        mine = res[0] if single else list(res[:n_out])
        return mine, [list(res[at:at + len(job.out_shape)]) for at, job in zip(out_at, jobs)]

    return run


def _fwd_in(x, g1, shards, order, jobs=()):
    t = x.shape[0]
    rows_per_step = min(IN_TILE, t)
    n_tiles = t // rows_per_step
    n = len(shards)
    halves = [s.shape[0] // 2 for s in shards]

    def body(order_ref, x_ref, g_ref, *refs):
        del order_ref
        ins, (z_ref, n_ref), outs = refs[:n], refs[n:n + 2], refs[n + 2:2 * n + 2]
        wbuf, nbuf, send, recv, local = refs[2 * n + 2:]
        s, i = pl.program_id(0), pl.program_id(1)
        x_, y_, c, chips = _place()
        k_me = _chip_index(x_, y_)

        def block(w, chip, pc):
            return outs[w].at[_chip_index(*chip), pl.ds(pc * halves[w], halves[w]), :]

        def over_ici(w, j, landing):
            return pltpu.make_async_remote_copy(
                src_ref=ins[w].at[pl.ds(c * halves[w], halves[w]), :],
                dst_ref=block(w, chips[j] if landing else (x_, y_), c), send_sem=send.at[6 * w + j],
                recv_sem=recv.at[6 * w + j], device_id=(*chips[j], c), device_id_type=MESH)

        def to_sibling(w, j, landing):
            blk = block(w, chips[j], 1 - c if landing else c)
            return pltpu.make_async_remote_copy(
                src_ref=blk, dst_ref=blk, send_sem=send.at[6 * w + 3 + j], recv_sem=recv.at[6 * w + 3 + j],
                device_id=(x_, y_, 1 - c), device_id_type=MESH)

        own = [pltpu.make_async_copy(wbuf, outs[0].at[k_me], local.at[0])]
        own += [pltpu.make_async_copy(ins[w], outs[w].at[k_me], local.at[w]) for w in range(1, n)]

        @pl.when((s == 0) & (i == 0))
        def _():
            for j in range(2):
                for w in range(n):
                    over_ici(w, j, False).start()
            load = pltpu.make_async_copy(ins[0], wbuf, local.at[n])
            load.start()
            load.wait()
            for cp in own:
                cp.start()

        for j in range(N_CHIPS - 1):
            @pl.when((s == j + 1) & (i == 0))
            def _(j=j):
                for w in range(n):
                    over_ici(w, j, True).wait_recv()
                for w in range(n):
                    to_sibling(w, j, False).start()
                if j == 0:
                    for w in range(n):
                        over_ici(w, 2, False).start()
                    own[0].wait()
                for w in range(n):
                    to_sibling(w, j, True).wait_recv()
                load = pltpu.make_async_copy(outs[0].at[_chip_index(*chips[j])], wbuf, local.at[n])
                load.start()
                load.wait()

        rows = pl.ds(pl.multiple_of(i * rows_per_step, rows_per_step), rows_per_step)

        @pl.when(s == 0)
        def _():
            xhat, _ = _rms(x_ref[...])
            nrm = (xhat * g_ref[...]).astype(BF16)
            nbuf[rows, :] = nrm
            n_ref[...] = nrm

        z_ref[...] = _dot(nbuf[rows, :], wbuf[...])

        @pl.when((s == N_CHIPS - 1) & (i == n_tiles - 1))
        def _():
            for j in range(N_CHIPS - 1):
                for w in range(n):
                    over_ici(w, j, False).wait_send()
                    to_sibling(w, j, False).wait_send()
            for cp in own[1:]:
                cp.wait()

    once = lambda s, i, order: (jnp.where(s == 0, i, n_tiles - 1), 0)
    (z, n1, *stacked), job_outs = _fused_call(
        body, jobs, name="fwd_in", grid=(N_CHIPS, n_tiles), n_prefetch=1,
        in_specs=[pl.BlockSpec((rows_per_step, D_MODEL), once), _const((1, D_MODEL))] + [ANY] * n,
        out_specs=[pl.BlockSpec((rows_per_step, IN_SHARD), lambda s, i, order: (i, order[s])),
                   pl.BlockSpec((rows_per_step, D_MODEL), once)] + [ANY] * n,
        out_shape=[jax.ShapeDtypeStruct((t, D_IN), F32), jax.ShapeDtypeStruct((t, D_MODEL), BF16)]
        + [jax.ShapeDtypeStruct((N_CHIPS,) + s.shape, s.dtype) for s in shards],
        scratch_shapes=[pltpu.VMEM(shards[0].shape, BF16), pltpu.VMEM((t, D_MODEL), BF16),
                        pltpu.SemaphoreType.DMA((6 * n,)),
                        pltpu.SemaphoreType.DMA((6 * n,)), pltpu.SemaphoreType.DMA((n + 1,))],
        compiler_params=_params(2), jobs_start_after=(1, 0),
    )(order, x, g1, *shards)
    return (z, n1, stacked), job_outs


def _fwd_lru(z, conv_w, conv_b, wr, br, wi, bi, lam, jobs=()):
    t = z.shape[0]

    def body(xa_ref, ga_ref, cw_ref, cb_ref, wr_ref, br_ref, wi_ref, bi_ref, lam_ref, ya_ref, h_ref, xc_ref, r_ref,
             ig_ref, tail_ref, carry_ref):
        @pl.when(pl.program_id(0) == 0)
        def _():
            tail_ref[...] = jnp.zeros_like(tail_ref)
            carry_ref[...] = jnp.zeros_like(carry_ref)

        xa = xa_ref[...]
        xc, r, ig, a, mult = _lru_gates(xa, tail_ref[...], cw_ref, cb_ref, wr_ref, br_ref, wi_ref, bi_ref, lam_ref)
        tail_ref[...] = xa[SEQ_TILE - SUBLANES:]
        xc_ref[...], r_ref[...], ig_ref[...] = xc, r, ig
        h, carry = _scan_forward(a, xc * ig * mult, carry_ref[...])
        carry_ref[...] = carry
        h_ref[...] = h
        ya_ref[...] = (h * _gelu(ga_ref[...])).astype(BF16)

    tile = lambda j: pl.BlockSpec((SEQ_TILE, D_MODEL), lambda i: (i, j))
    return _fused_call(
        body, jobs, name="fwd_lru", grid=(t // SEQ_TILE,),
        in_specs=[tile(0), tile(1), _const((CONV_WIDTH, D_MODEL)), _const((1, D_MODEL)),
                  _resident((HEADS, HEAD_DIM, HEAD_DIM)), _const((1, D_MODEL)),
                  _resident((HEADS, HEAD_DIM, HEAD_DIM)), _const((1, D_MODEL)), _const((1, D_MODEL))],
        out_specs=[tile(0)] * 5,
        out_shape=[jax.ShapeDtypeStruct((t, D_MODEL), BF16)] + [jax.ShapeDtypeStruct((t, D_MODEL), F32)] * 4,
        scratch_shapes=[pltpu.VMEM((SUBLANES, D_MODEL), F32), pltpu.VMEM((1, D_MODEL), F32)],
        compiler_params=_params(),
    )(z, z, conv_w, conv_b, wr, br, wi, bi, lam)


def _sgu_forward_parts(ub, vb, lg_ref, lb_ref):
    u, du = _gelu_and_grad(ub)
    vg, dvg = _gelu_and_grad(vb)
    mu = jnp.mean(vg, axis=-1, keepdims=True)
    d = vg - mu
    rstd = lax.rsqrt(jnp.mean(d * d, axis=-1, keepdims=True) + LN_EPS)
    vhat = d * rstd
    vn = (vhat * lg_ref[...] + lb_ref[...]).astype(BF16)
    return u, du, dvg, rstd, vhat, vn


def _causal_mask():
    rows = lax.broadcasted_iota(jnp.int32, (CHUNK, CHUNK), 0)
    cols = lax.broadcasted_iota(jnp.int32, (CHUNK, CHUNK), 1)
    return rows >= cols


def _fwd_sgu(z, ln_g, ln_b, w_s, bias_full, jobs=()):
    t = z.shape[0]

    def body(ub_ref, vb_ref, lg_ref, lb_ref, ws_ref, bias_ref, yb_ref):
        u, _, _, _, _, vn = _sgu_forward_parts(ub_ref[...], vb_ref[...], lg_ref, lb_ref)
        mask = _causal_mask()
        wm = [jnp.where(mask, ws_ref[g], 0.0).astype(BF16) for g in range(GROUPS)]
        for c in range(SEQ_TILE // CHUNK):
            rows = slice(c * CHUNK, (c + 1) * CHUNK)
            for g in range(GROUPS):
                cols = slice(g * GROUP_DIM, (g + 1) * GROUP_DIM)
                sp = _dot(wm[g], vn[rows, cols]) + bias_ref[:, cols]
                yb_ref[rows, cols] = (u[rows, cols] * sp).astype(BF16)

    tile = lambda j: pl.BlockSpec((SEQ_TILE, D_MODEL), lambda i: (i, j))
    return _fused_call(
        body, jobs, name="fwd_sgu", grid=(t // SEQ_TILE,),
        in_specs=[tile(2), tile(3), _const((1, D_MODEL)), _const((1, D_MODEL)),
                  _const((GROUPS, CHUNK, CHUNK)), _const((CHUNK, D_MODEL))],
        out_specs=tile(0),
        out_shape=jax.ShapeDtypeStruct((t, D_MODEL), BF16),
        compiler_params=_params(),
    )(z, z, ln_g, ln_b, w_s, bias_full)


def _fwd_merge(ya, yb, z, x, w_oa, w_ob, w_out, g2, jobs=()):
    t = x.shape[0]

    def body(ya_ref, yb_ref, m_ref, x_ref, woa_ref, wob_ref, wout_ref, g_ref, pa_ref, pb_ref, h1_ref, n2_ref):
        pa = _dot(ya_ref[...], woa_ref[...])
        pb = _dot(yb_ref[...], wob_ref[...])
        pa_ref[...] = pa
        pb_ref[...] = pb
        merged = jax.nn.sigmoid(m_ref[:, :D_MODEL]) * pa + jax.nn.sigmoid(m_ref[:, D_MODEL:]) * pb
        h1 = x_ref[...] + _dot(merged.astype(BF16), wout_ref[...])
        h1_ref[...] = h1
        xhat, _ = _rms(h1)
        n2_ref[...] = (xhat * g_ref[...]).astype(BF16)

    tile = pl.BlockSpec((MM_TILE, D_MODEL), lambda i: (i, 0))
    sq = _resident((D_MODEL, D_MODEL))
    return _fused_call(
        body, jobs, name="fwd_merge", grid=(t // MM_TILE,),
        in_specs=[tile, tile, pl.BlockSpec((MM_TILE, 2 * D_MODEL), lambda i: (i, 2)), tile, sq, sq, sq,
                  _const((1, D_MODEL))],
        out_specs=[tile, tile, tile, tile],
        out_shape=[jax.ShapeDtypeStruct((t, D_MODEL), F32)] * 3 + [jax.ShapeDtypeStruct((t, D_MODEL), BF16)],
        compiler_params=_params(),
    )(ya, yb, z, x, w_oa, w_ob, w_out, g2)


def _mlp(n2, h1, target, w_up_st, w_down, g2, g3, jobs=()):
    t = n2.shape[0]

    def body(n2_ref, h1_ref, tgt_ref, wup_ref, wdown_ref, g2_ref, g3_ref, act_ref, dup_ref, dh2b_ref, dh1_ref,
             loss_ref, dg3_ref, dg2_ref, relu_ref):
        @pl.when(pl.program_id(0) == 0)
        def _():
            for ref in (loss_ref, dg3_ref, dg2_ref):
                ref[...] = jnp.zeros_like(ref)

        n2 = n2_ref[...]
        h1 = h1_ref[...]
        h2 = h1
        for k in range(N_CHIPS):
            cols = slice(k * D_MODEL, (k + 1) * D_MODEL)
            r = jnp.maximum(_dot(n2, wup_ref[k]), 0.0)
            relu_ref[:, cols] = r
            act = (r * r).astype(BF16)
            act_ref[:, cols] = act
            h2 = h2 + _dot(act, wdown_ref[cols, :])
        xhat, r3 = _rms(h2)
        diff = xhat * g3_ref[...] - tgt_ref[...]
        sq = jnp.sum(diff * diff, axis=1, keepdims=True)
        loss_ref[...] = loss_ref[...] + (0.5 / D_MODEL) * jnp.sum(sq, axis=0, keepdims=True)
        dy = diff * (1.0 / D_MODEL)
        dg3_ref[...] = dg3_ref[...] + _col_sum(dy * xhat)
        dh2 = _rms_bwd(dy * g3_ref[...], xhat, r3)
        dh2b = dh2.astype(BF16)
        dh2b_ref[...] = dh2b
        dn2 = jnp.zeros((SEQ_TILE, D_MODEL), F32)
        for k in range(N_CHIPS):
            cols = slice(k * D_MODEL, (k + 1) * D_MODEL)
            dup = (_dot_nt(dh2b, wdown_ref[cols, :]) * (2.0 * relu_ref[:, cols])).astype(BF16)
            dup_ref[:, cols] = dup
            dn2 = dn2 + _dot_nt(dup, wup_ref[k])
        xhat, r2 = _rms(h1)
        dg2_ref[...] = dg2_ref[...] + _col_sum(dn2 * xhat)
        dh1_ref[...] = dh2 + _rms_bwd(dn2 * g2_ref[...], xhat, r2)

    tile = pl.BlockSpec((SEQ_TILE, D_MODEL), lambda i: (i, 0))
    wide = pl.BlockSpec((SEQ_TILE, D_FF), lambda i: (i, 0))
    vec = _const((1, D_MODEL))
    vec_shape = jax.ShapeDtypeStruct((1, D_MODEL), F32)
    return _fused_call(
        body, jobs, name="mlp", grid=(t // SEQ_TILE,),
        in_specs=[tile, tile, tile, _resident((N_CHIPS, D_MODEL, D_MODEL)), _resident((D_FF, D_MODEL)), vec, vec],
        out_specs=[wide, wide, tile, tile, _const((SUBLANES, 128)), vec, vec],
        out_shape=[jax.ShapeDtypeStruct((t, D_FF), BF16), jax.ShapeDtypeStruct((t, D_FF), BF16),
                   jax.ShapeDtypeStruct((t, D_MODEL), BF16), jax.ShapeDtypeStruct((t, D_MODEL), F32),
                   jax.ShapeDtypeStruct((SUBLANES, 128), F32), vec_shape, vec_shape],
        scratch_shapes=[pltpu.VMEM((SEQ_TILE, D_FF), F32)],
        compiler_params=_params(),
    )(n2, h1, target, w_up_st, w_down, g2, g3)


def _bwd_mix(dh1, pa, pb, z, h, xc, r, ig, w_oa, w_ob, w_out, ln_g, ln_b, w_s, bias_full, conv_w, wr, wi, lam, jobs=()):
    t = dh1.shape[0]
    n_tiles = t // SEQ_TILE
    per_tile = SEQ_TILE // SUBLANES

    def merge_part(dh1_ref, pa_ref, pb_ref, m_ref, woa_ref, wob_ref, wout_ref, dz_ref, dya_ref, dyb_ref, mg_ref,
                   dpa_ref, dpb_ref, dh1b_ref):
        dh1b = dh1_ref[...].astype(BF16)
        dh1b_ref[...] = dh1b
        dm = _dot_nt(dh1b, wout_ref[...])
        pa = pa_ref[...]
        pb = pb_ref[...]
        sa = jax.nn.sigmoid(m_ref[:, :D_MODEL])
        sb = jax.nn.sigmoid(m_ref[:, D_MODEL:])
        mg_ref[...] = (sa * pa + sb * pb).astype(BF16)
        dz_ref[:, :D_MODEL] = (dm * pa * sa * (1.0 - sa)).astype(BF16)
        dz_ref[:, D_MODEL:] = (dm * pb * sb * (1.0 - sb)).astype(BF16)
        dpa = (dm * sa).astype(BF16)
        dpb = (dm * sb).astype(BF16)
        dpa_ref[...] = dpa
        dpb_ref[...] = dpb
        dya_ref[...] = _dot_nt(dpa, woa_ref[...])
        dyb_ref[...] = _dot_nt(dpb, wob_ref[...])

    def sgu_part(dyb_ref, ub_ref, vb_ref, lg_ref, lb_ref, ws_ref, bias_ref, dz_ref, dlg_ref, dlb_ref, dws_ref, dbs_ref,
                 dvn_ref, dsp_acc):
        i = pl.program_id(0)

        @pl.when(i == 0)
        def _():
            dlg_ref[...] = jnp.zeros_like(dlg_ref)
            dlb_ref[...] = jnp.zeros_like(dlb_ref)
            dws_ref[...] = jnp.zeros_like(dws_ref)
            dsp_acc[...] = jnp.zeros_like(dsp_acc)

        u, du, dvg, rstd, vhat, vn = _sgu_forward_parts(ub_ref[...], vb_ref[...], lg_ref, lb_ref)
        dyb = dyb_ref[...]
        mask = _causal_mask()
        wm = [jnp.where(mask, ws_ref[g], 0.0).astype(BF16) for g in range(GROUPS)]
        for c in range(SEQ_TILE // CHUNK):
            rows = slice(c * CHUNK, (c + 1) * CHUNK)
            for g in range(GROUPS):
                cols = slice(g * GROUP_DIM, (g + 1) * GROUP_DIM)
                vn_blk = vn[rows, cols]
                sp = _dot(wm[g], vn_blk) + bias_ref[:, cols]
                dyb_blk = dyb[rows, cols]
                dz_ref[rows, cols] = (dyb_blk * sp * du[rows, cols]).astype(BF16)
                dsp = dyb_blk * u[rows, cols]
                dsp_acc[:, cols] = dsp_acc[:, cols] + dsp
                dspb = dsp.astype(BF16)
                dvn_ref[rows, cols] = _dot_tn(wm[g], dspb)
                wcols = slice(g * CHUNK, (g + 1) * CHUNK)
                dws_ref[:, wcols] = dws_ref[:, wcols] + jnp.where(mask, _dot_nt(dspb, vn_blk), 0.0)
        dvn = dvn_ref[...]
        dlg_ref[...] = dlg_ref[...] + _col_sum(dvn * vhat)
        dlb_ref[...] = dlb_ref[...] + _col_sum(dvn)
        dvhat = dvn * lg_ref[...]
        dvgel = rstd * (dvhat - jnp.mean(dvhat, axis=-1, keepdims=True)
                        - vhat * jnp.mean(dvhat * vhat, axis=-1, keepdims=True))
        dz_ref[:, D_MODEL:] = (dvgel * dvg).astype(BF16)

        @pl.when(i == n_tiles - 1)
        def _():
            lane = lax.broadcasted_iota(jnp.int32, (CHUNK, 128), 1)
            out = jnp.zeros((CHUNK, 128), F32)
            for g in range(GROUPS):
                s = jnp.sum(dsp_acc[:, g * GROUP_DIM:(g + 1) * GROUP_DIM], axis=1, keepdims=True)
                out = out + jnp.where(lane == g, s, 0.0)
            dbs_ref[...] = out

    def lru_part(dya_ref, xa_ref, ga_ref, h_ref, h_prev_ref, xc_ref, r_ref, ig_ref, cw_ref, wr_ref, wi_ref, lam_ref,
                 dz_ref, dcw_ref, dcb_ref, dwr_ref, dbr_ref, dwi_ref, dbi_ref, dlam_ref, lam_carry, dxc_head):
        i = pl.program_id(0)

        @pl.when(i == 0)
        def _():
            for ref in (dcw_ref, dcb_ref, dwr_ref, dbr_ref, dwi_ref, dbi_ref, dlam_ref, lam_carry, dxc_head):
                ref[...] = jnp.zeros_like(ref)

        first_tile = i == n_tiles - 1
        h_tail = jnp.where(first_tile, 0.0, h_prev_ref[...])
        xc, r, ig = xc_ref[...], r_ref[...], ig_ref[...]
        xcb = xc.astype(BF16)
        sp, a, mult = _decay(r, lam_ref)
        h = h_ref[...]
        h_prev = _shift_down(h, h_tail, 1)
        dya = dya_ref[...]
        gg, dgg = _gelu_and_grad(ga_ref[...])
        dz_ref[:, D_MODEL:] = (dya * h * dgg).astype(BF16)
        ones = jnp.ones((SUBLANES, D_MODEL), F32)
        lam_t, lam_first = _scan_backward(_shift_up(a, ones, 1), dya * gg, lam_carry[...])
        lam_carry[...] = a[0:1] * lam_first
        dmult = lam_t * xc * ig
        dla = lam_t * h_prev * a - dmult * (a * a) / mult
        dr = dla * ((-LRU_C) * sp)
        dlam_ref[...] = dlam_ref[...] + _col_sum(dla * r) * (LRU_C * jax.nn.sigmoid(-lam_ref[...]))
        dpr = dr * r * (1.0 - r)
        dpi = lam_t * xc * mult * ig * (1.0 - ig)
        dbr_ref[...] = dbr_ref[...] + _col_sum(dpr)
        dbi_ref[...] = dbi_ref[...] + _col_sum(dpi)
        dprb = dpr.astype(BF16)
        dpib = dpi.astype(BF16)
        dxc_gate = []
        for hd in range(HEADS):
            cols = slice(hd * HEAD_DIM, (hd + 1) * HEAD_DIM)
            dxc_gate.append(_dot_nt(dprb[:, cols], wr_ref[hd]) + _dot_nt(dpib[:, cols], wi_ref[hd]))
            dwr_ref[hd] = dwr_ref[hd] + _dot_tn(xcb[:, cols], dprb[:, cols])
            dwi_ref[hd] = dwi_ref[hd] + _dot_tn(xcb[:, cols], dpib[:, cols])
        dxc = lam_t * ig * mult + jnp.concatenate(dxc_gate, axis=1)
        dcb_ref[...] = dcb_ref[...] + _col_sum(dxc)
        cw = cw_ref[...]
        head = dxc_head[...]
        xa = xa_ref[...]
        dxa = cw[0:1] * dxc
        dcw_ref[0:1, :] = dcw_ref[0:1, :] + _col_sum(dxc * xa)
        for k in range(1, CONV_WIDTH):
            dxc_k = _shift_up(dxc, head, k)
            dxa = dxa + cw[k:k + 1] * dxc_k
            dcw_ref[k:k + 1, :] = dcw_ref[k:k + 1, :] + _col_sum(dxc_k * xa)
        dxc_head[...] = dxc[0:SUBLANES]
        dz_ref[:, :D_MODEL] = dxa.astype(BF16)

    def body(dh1_ref, pa_ref, pb_ref, z_ref, h_ref, h_prev_ref, xc_ref, r_ref, ig_ref, woa_ref, wob_ref, wout_ref,
             lg_ref, lb_ref, ws_ref, bias_ref, cw_ref, wr_ref, wi_ref, lam_ref, dz_ref, mg_ref, dpa_ref, dpb_ref,
             dh1b_ref, dlg_ref, dlb_ref, dws_ref, dbs_ref, dcw_ref, dcb_ref, dwr_ref, dbr_ref, dwi_ref, dbi_ref,
             dlam_ref, dya_ref, dyb_ref, dvn_ref, dsp_acc, lam_carry, dxc_head):
        def cols(ref, first, count):
            return ref.at[:, pl.ds(first * D_MODEL, count * D_MODEL)]

        merge_part(dh1_ref, pa_ref, pb_ref, cols(z_ref, 4, 2), woa_ref, wob_ref, wout_ref, cols(dz_ref, 4, 2), dya_ref,
                   dyb_ref, mg_ref, dpa_ref, dpb_ref, dh1b_ref)
        sgu_part(dyb_ref, cols(z_ref, 2, 1), cols(z_ref, 3, 1), lg_ref, lb_ref, ws_ref, bias_ref, cols(dz_ref, 2, 2),
                 dlg_ref, dlb_ref, dws_ref, dbs_ref, dvn_ref, dsp_acc)
        lru_part(dya_ref, cols(z_ref, 0, 1), cols(z_ref, 1, 1), h_ref, h_prev_ref, xc_ref, r_ref, ig_ref, cw_ref, wr_ref,
                 wi_ref, lam_ref, cols(dz_ref, 0, 2), dcw_ref, dcb_ref, dwr_ref, dbr_ref, dwi_ref, dbi_ref, dlam_ref,
                 lam_carry, dxc_head)

    rev = lambda i: n_tiles - 1 - i
    tile = pl.BlockSpec((SEQ_TILE, D_MODEL), lambda i: (rev(i), 0))
    row = pl.BlockSpec((SEQ_TILE, D_IN), lambda i: (rev(i), 0))
    prev8 = pl.BlockSpec((SUBLANES, D_MODEL), lambda i: (jnp.maximum(rev(i) * per_tile - 1, 0), 0))
    vec = _const((1, D_MODEL))
    sq = _resident((D_MODEL, D_MODEL))
    gate_w = _resident((HEADS, HEAD_DIM, HEAD_DIM))
    gate_acc = _const((HEADS, HEAD_DIM, HEAD_DIM))
    vec_shape = jax.ShapeDtypeStruct((1, D_MODEL), F32)
    gate_shape = jax.ShapeDtypeStruct((HEADS, HEAD_DIM, HEAD_DIM), F32)
    act_bf = jax.ShapeDtypeStruct((t, D_MODEL), BF16)
    return _fused_call(
        body, jobs, name="bwd_mix", grid=(n_tiles,),
        in_specs=[tile, tile, tile, row, tile, prev8, tile, tile, tile, sq, sq, sq, vec, vec,
                  _const((GROUPS, CHUNK, CHUNK)), _const((CHUNK, D_MODEL)), _const((CONV_WIDTH, D_MODEL)), gate_w, gate_w,
                  vec],
        out_specs=[row, tile, tile, tile, tile, vec, vec, _const((CHUNK, GROUPS * CHUNK)), _const((CHUNK, 128)),
                   _const((SUBLANES, D_MODEL)), vec, gate_acc, vec, gate_acc, vec, vec],
        out_shape=[jax.ShapeDtypeStruct((t, D_IN), BF16), act_bf, act_bf, act_bf, act_bf, vec_shape, vec_shape,
                   jax.ShapeDtypeStruct((CHUNK, GROUPS * CHUNK), F32), jax.ShapeDtypeStruct((CHUNK, 128), F32),
                   jax.ShapeDtypeStruct((SUBLANES, D_MODEL), F32), vec_shape, gate_shape, vec_shape, gate_shape,
                   vec_shape, vec_shape],
        scratch_shapes=[pltpu.VMEM((SEQ_TILE, D_MODEL), F32), pltpu.VMEM((SEQ_TILE, D_MODEL), F32),
                        pltpu.VMEM((SEQ_TILE, D_MODEL), F32), pltpu.VMEM((CHUNK, D_MODEL), F32),
                        pltpu.VMEM((1, D_MODEL), F32), pltpu.VMEM((SUBLANES, D_MODEL), F32)],
        compiler_params=_params(),
    )(dh1, pa, pb, z, h, h, xc, r, ig, w_oa, w_ob, w_out, ln_g, ln_b, w_s, bias_full, conv_w, wr, wi, lam)


def _bwd_in(dz, x, dh1, w_in_st, g1, jobs=()):
    t = x.shape[0]

    def body(dz_ref, x_ref, dh1_ref, w_ref, g_ref, dx_ref, dg1_ref):
        @pl.when(pl.program_id(0) == 0)
        def _():
            dg1_ref[...] = jnp.zeros_like(dg1_ref)

        dn1 = jnp.zeros((MM_TILE, D_MODEL), F32)
        for k in range(N_CHIPS):
            dn1 = dn1 + _dot_nt(dz_ref[:, k * IN_SHARD:(k + 1) * IN_SHARD], w_ref[k])
        xhat, r1 = _rms(x_ref[...])
        dg1_ref[...] = dg1_ref[...] + _col_sum(dn1 * xhat)
        dx_ref[...] = dh1_ref[...] + _rms_bwd(dn1 * g_ref[...], xhat, r1)

    tile = pl.BlockSpec((MM_TILE, D_MODEL), lambda i: (i, 0))
    return _fused_call(
        body, jobs, name="bwd_in", grid=(t // MM_TILE,),
        in_specs=[pl.BlockSpec((MM_TILE, D_IN), lambda i: (i, 0)), tile, tile,
                  _resident((N_CHIPS, D_MODEL, IN_SHARD)), _const((1, D_MODEL))],
        out_specs=[tile, _const((1, D_MODEL))],
        out_shape=[jax.ShapeDtypeStruct((t, D_MODEL), F32), jax.ShapeDtypeStruct((1, D_MODEL), F32)],
        compiler_params=_params(),
    )(dz, x, dh1, w_in_st, g1)


def _weight_grad(name, a, b, n_blocks, a_varies, b_varies, width, jobs=()):
    t = a.shape[0]
    rows = min(DW_TILE, t)
    n_t = t // rows

    def body(a_ref, b_ref, o_ref, acc_ref):
        s = pl.program_id(1)
        part = _dot_tn(a_ref[...], b_ref[...])

        @pl.when(s == 0)
        def _():
            acc_ref[...] = part

        @pl.when(s > 0)
        def _():
            acc_ref[...] = acc_ref[...] + part

        @pl.when(s == n_t - 1)
        def _():
            o_ref[...] = acc_ref[...].astype(BF16)

    return _fused_call(
        body, jobs, name=name, grid=(n_blocks, n_t),
        in_specs=[pl.BlockSpec((rows, D_MODEL), (lambda j, s: (s, j)) if a_varies else (lambda j, s: (s, 0))),
                  pl.BlockSpec((rows, width), (lambda j, s: (s, j)) if b_varies else (lambda j, s: (s, 0)))],
        out_specs=pl.BlockSpec((None, D_MODEL, width), lambda j, s: (j, 0, 0)),
        out_shape=jax.ShapeDtypeStruct((n_blocks, D_MODEL, width), BF16),
        scratch_shapes=[pltpu.VMEM((D_MODEL, width), F32)],
        compiler_params=_params(2),
    )(a, b)


def _place():
    x, y, c = lax.axis_index("x"), lax.axis_index("y"), lax.axis_index("c")
    other_chips = [(1 - x, y), (x, 1 - y), (1 - x, 1 - y)]
    return x, y, c, other_chips


def _chip_index(px, py):
    return 2 * px + py


ANY = pl.BlockSpec(memory_space=pl.ANY)
SIBLING = ((0, 0, 1),)
NEIGHBOURS = ((1, 0, 0), (0, 1, 0))
OTHER_CHIPS = NEIGHBOURS + ((1, 1, 0),)


def _comm_call(name, jobs):
    return _fused_call(None, jobs, name=name, grid=(), in_specs=[], out_specs=[], out_shape=[])()[1]


def _near_far(x, y, c):
    return (x ^ (1 - c), y ^ c), (x ^ c, y ^ (1 - c))


def _gather_near_job(shards):
    n = len(shards)
    halves = [s.shape[0] // 2 for s in shards]

    def copies(ins, outs, send, recv, local):
        x, y, c, _ = _place()
        near, _ = _near_far(x, y, c)

        def block(w, chip, pc):
            return outs[w].at[_chip_index(*chip), pl.ds(pc * halves[w], halves[w]), :]

        def copy(w, k, chip, pc, to, src=None):
            return pltpu.make_async_remote_copy(
                src_ref=block(w, chip, pc) if src is None else src, dst_ref=block(w, chip, pc),
                send_sem=send.at[2 * w + k], recv_sem=recv.at[2 * w + k], device_id=to, device_id_type=MESH)

        sends, arrivals, own = [], [], []
        for w in range(n):
            src = ins[w].at[pl.ds(c * halves[w], halves[w]), :]
            own.append(pltpu.make_async_copy(src, block(w, (x, y), c), local.at[w]))
            sends += [copy(w, 0, (x, y), c, (*near, c), src), copy(w, 1, (x, y), c, (x, y, 1 - c), src)]
            arrivals += [copy(w, 0, near, c, (x, y, c)), copy(w, 1, (x, y), 1 - c, (x, y, c))]
        return sends, arrivals, own

    return _Job(shards, [jax.ShapeDtypeStruct((N_CHIPS,) + s.shape, s.dtype) for s in shards], 2 * n, copies,
                NEIGHBOURS + SIBLING, n_local=n)


def _gather_far_job(stacked):
    n = len(stacked)
    halves = [s.shape[1] // 2 for s in stacked]

    def copies(ins, outs, send, recv, local):
        del ins, local
        x, y, c, _ = _place()
        near, far = _near_far(x, y, c)

        def copy(w, k, chip):
            blk = outs[w].at[_chip_index(*chip), pl.ds(c * halves[w], halves[w]), :]
            return pltpu.make_async_remote_copy(
                src_ref=blk, dst_ref=blk, send_sem=send.at[2 * w + k], recv_sem=recv.at[2 * w + k],
                device_id=(*far, c), device_id_type=MESH)

        sends = [copy(w, k, chip) for w in range(n) for k, chip in enumerate(((x, y), near))]
        arrivals = [copy(w, k, chip) for w in range(n) for k, chip in enumerate((far, (1 - x, 1 - y)))]
        return sends, arrivals, []

    return _Job(stacked, [jax.ShapeDtypeStruct(s.shape, s.dtype) for s in stacked], 2 * n, copies, NEIGHBOURS,
                aliases={w: w for w in range(n)})


def _gather_pass_job(stacked):
    n = len(stacked)
    halves = [s.shape[1] // 2 for s in stacked]

    def copies(ins, outs, send, recv, local):
        del ins, local
        x, y, c, chips = _place()

        def copy(w, j, chip, pc, to):
            blk = outs[w].at[_chip_index(*chip), pl.ds(pc * halves[w], halves[w]), :]
            return pltpu.make_async_remote_copy(
                src_ref=blk, dst_ref=blk, send_sem=send.at[3 * w + j], recv_sem=recv.at[3 * w + j], device_id=to,
                device_id_type=MESH)

        sends = [copy(w, j, chip, c, (x, y, 1 - c)) for w in range(n) for j, chip in enumerate(chips)]
        arrivals = [copy(w, j, chip, 1 - c, (x, y, c)) for w in range(n) for j, chip in enumerate(chips)]
        return sends, arrivals, []

    return _Job(stacked, [jax.ShapeDtypeStruct(s.shape, s.dtype) for s in stacked], 3 * n, copies, SIBLING,
                aliases={w: w for w in range(n)})


def _gather_small_job(block):
    def copies(ins, outs, send, recv, local):
        x, y, c, chips = _place()

        def copy(j, chip_from, to):
            return pltpu.make_async_remote_copy(
                src_ref=ins[0], dst_ref=outs[0].at[_chip_index(*chip_from)], send_sem=send.at[j],
                recv_sem=recv.at[j], device_id=to, device_id_type=MESH)

        own = [pltpu.make_async_copy(ins[0], outs[0].at[_chip_index(x, y)], local.at[0])]
        sends = [copy(j, (x, y), (*chip, c)) for j, chip in enumerate(chips)]
        arrivals = [copy(j, chip, (x, y, c)) for j, chip in enumerate(chips)]
        return sends, arrivals, own

    return _Job([block], [jax.ShapeDtypeStruct((N_CHIPS,) + block.shape, block.dtype)], 3, copies, OTHER_CHIPS,
                n_local=1)


def _pair_send_job(grads):
    n = len(grads)
    halves = [g.shape[1] // 2 for g in grads]

    def copies(ins, outs, send, recv, local):
        del local
        x, y, c, _ = _place()
        sends = [pltpu.make_async_remote_copy(
            src_ref=ins[w].at[:, pl.ds((1 - c) * halves[w], halves[w]), :], dst_ref=outs[w], send_sem=send.at[w],
            recv_sem=recv.at[w], device_id=(x, y, 1 - c), device_id_type=MESH) for w in range(n)]
        return sends, sends, []

    return _Job(grads, [jax.ShapeDtypeStruct((N_CHIPS, h, g.shape[2]), g.dtype) for g, h in zip(grads, halves)], n,
                copies, SIBLING)


def _row_block(rows, limit=256):
    return min(rows, limit)


def _pair_add(name, core, mine, theirs):
    _, _, h, cols = mine.shape
    rb = _row_block(h, 512)

    def body(core_ref, a_ref, b_ref, o_ref):
        del core_ref
        o_ref[...] = (a_ref[...].astype(F32) + b_ref[...].astype(F32)).astype(BF16)

    return pl.pallas_call(
        body, name=name,
        grid_spec=pltpu.PrefetchScalarGridSpec(
            num_scalar_prefetch=1, grid=(N_CHIPS, h // rb),
            in_specs=[pl.BlockSpec((None, None, rb, cols), lambda k, r, core_ref: (k, core_ref[0], r, 0)),
                      pl.BlockSpec((None, rb, cols), lambda k, r, core_ref: (k, r, 0))],
            out_specs=pl.BlockSpec((None, rb, cols), lambda k, r, core_ref: (k, r, 0))),
        out_shape=jax.ShapeDtypeStruct(theirs.shape, BF16),
        compiler_params=_params(2),
    )(core, mine, theirs)


def _sequencer_call(name, collective_id, job, then=()):
    steps = [job]
    for make in then:
        steps.append(make(steps[-1].out_shape))
    peers = sorted(set(p for step in steps for p in step.peers))
    ins = [jax.new_ref(a, memory_space=pltpu.MemorySpace.HBM) for a in job.inputs]
    outs = [ins[{o: i for i, o in job.aliases.items()}[k]] if k in job.aliases.values()
            else jax.empty_ref(shape, memory_space=pltpu.MemorySpace.HBM) for k, shape in enumerate(job.out_shape)]
    sems = [pltpu.SemaphoreType.DMA((n,)) for step in steps for n in (step.n_sem, step.n_sem, max(step.n_local, 1))]

    @pl.kernel(mesh=plsc.ScalarSubcoreMesh(axis_name="sequencer", num_cores=1), name=name, scratch_types=tuple(sems),
               compiler_params=pltpu.CompilerParams(collective_id=collective_id))
    def launch(*sem_refs):
        x, y, c, _ = _place()
        barrier = pltpu.get_barrier_semaphore()
        for dx, dy, dc in peers:
            pl.semaphore_signal(barrier, inc=1, device_id=(x ^ dx, y ^ dy, c ^ dc), device_id_type=MESH)
        pl.semaphore_wait(barrier, len(peers))
        for k, step in enumerate(steps):
            sends, arrivals, own = step.copies(ins if k == 0 else outs, outs, *sem_refs[3 * k:3 * k + 3])
            for cp in own + sends:
                cp.start()
            for cp in arrivals:
                cp.wait_recv()
            for cp in sends:
                cp.wait_send()
            for cp in own:
                cp.wait()

    launch()
    return [ref[...] for ref in outs]


def _chip_exchange_job(sums):
    n = len(sums)

    def copies(ins, outs, send, recv, local):
        del local
        _, _, c, chips = _place()
        sends = [pltpu.make_async_remote_copy(
            src_ref=ins[w].at[_chip_index(*chip)], dst_ref=outs[w].at[j], send_sem=send.at[3 * w + j],
            recv_sem=recv.at[3 * w + j], device_id=(*chip, c), device_id_type=MESH)
            for w in range(n) for j, chip in enumerate(chips)]
        return sends, sends, []

    return _Job(sums, [jax.ShapeDtypeStruct((N_CHIPS - 1,) + s.shape[1:], s.dtype) for s in sums], 3 * n, copies,
                OTHER_CHIPS)


def _chip_sum(name, place, mine, theirs):
    _, h, cols = mine.shape
    rb = _row_block(h, 512)

    def body(place_ref, p_ref, q_ref, o_ref):
        del place_ref
        acc = p_ref[...].astype(F32)
        for j in range(N_CHIPS - 1):
            acc = acc + q_ref[j].astype(F32)
        o_ref[...] = acc

    return pl.pallas_call(
        body, name=name,
        grid_spec=pltpu.PrefetchScalarGridSpec(
            num_scalar_prefetch=1, grid=(h // rb,),
            in_specs=[pl.BlockSpec((None, rb, cols), lambda r, place_ref: (place_ref[0], r, 0)),
                      pl.BlockSpec((N_CHIPS - 1, rb, cols), lambda r, place_ref: (0, r, 0))],
            out_specs=pl.BlockSpec((None, rb, cols), lambda r, place_ref: (place_ref[1], r, 0))),
        out_shape=jax.ShapeDtypeStruct((2, h, cols), F32),
        compiler_params=_params(),
    )(place, mine, theirs)


def _share_job(bufs):
    n = len(bufs)

    def copies(ins, outs, send, recv, local):
        del ins, local
        x, y, c, _ = _place()

        def copy(w, half):
            return pltpu.make_async_remote_copy(
                src_ref=outs[w].at[half], dst_ref=outs[w].at[half], send_sem=send.at[w], recv_sem=recv.at[w],
                device_id=(x, y, 1 - c), device_id_type=MESH)

        return [copy(w, c) for w in range(n)], [copy(w, 1 - c) for w in range(n)], []

    return _Job(bufs, [jax.ShapeDtypeStruct(b.shape, b.dtype) for b in bufs], n, copies, SIBLING,
                aliases={w: w for w in range(n)})


SMALL_ROWS = 24
ROW_G1, ROW_CW, ROW_CB, ROW_BR, ROW_BI, ROW_LAM, ROW_LG, ROW_LB, ROW_G2, ROW_G3, ROW_LOSS, ROW_BS = (
    0, 1, 5, 6, 7, 8, 9, 10, 11, 12, 13, 16)
N_DEV = 8


def _pack_small(dcw, dcb, dbr, dbi, dlam, dlg, dlb, dg2, dg3, loss, dbs):
    def body(dcw_ref, dcb_ref, dbr_ref, dbi_ref, dlam_ref, dlg_ref, dlb_ref, dg2_ref, dg3_ref, loss_ref, dbs_ref, out):
        out[...] = jnp.zeros((SMALL_ROWS, D_MODEL), F32)
        for row, ref in ((ROW_CB, dcb_ref), (ROW_BR, dbr_ref), (ROW_BI, dbi_ref), (ROW_LAM, dlam_ref),
                         (ROW_LG, dlg_ref), (ROW_LB, dlb_ref), (ROW_G2, dg2_ref), (ROW_G3, dg3_ref)):
            out[row:row + 1, :] = ref[...]
        out[ROW_CW:ROW_CW + CONV_WIDTH, :] = dcw_ref[0:CONV_WIDTH, :]
        out[ROW_LOSS:ROW_LOSS + 1, 0:128] = loss_ref[0:1, :]
        out[ROW_BS:ROW_BS + GROUPS, 0:128] = jnp.transpose(dbs_ref[...])[0:GROUPS, :]

    vm = pl.BlockSpec(memory_space=pltpu.VMEM)
    return pl.pallas_call(
        body, name="pack_small", in_specs=[vm] * 11, out_specs=vm,
        out_shape=jax.ShapeDtypeStruct((SMALL_ROWS, D_MODEL), F32),
    )(dcw, dcb, dbr, dbi, dlam, dlg, dlb, dg2, dg3, loss, dbs)


def _gather_all_job(blocks):
    n = len(blocks)
    flips = [(dx, dy, dc) for dx in (0, 1) for dy in (0, 1) for dc in (0, 1)][1:]

    def copies(ins, outs, send, recv, local):
        x, y, c, _ = _place()
        me = 4 * x + 2 * y + c
        sends, arrivals, own = [], [], []
        for w in range(n):
            own.append(pltpu.make_async_copy(ins[w], outs[w].at[me], local.at[w]))
            for k, (dx, dy, dc) in enumerate(flips):
                peer = (x ^ dx, y ^ dy, c ^ dc)
                sem = dict(send_sem=send.at[7 * w + k], recv_sem=recv.at[7 * w + k])
                sends.append(pltpu.make_async_remote_copy(
                    src_ref=ins[w], dst_ref=outs[w].at[me], device_id=peer, device_id_type=MESH, **sem))
                arrivals.append(pltpu.make_async_remote_copy(
                    src_ref=ins[w], dst_ref=outs[w].at[4 * peer[0] + 2 * peer[1] + peer[2]], device_id=peer,
                    device_id_type=MESH, **sem))
        return sends, arrivals, own

    return _Job(blocks, [jax.ShapeDtypeStruct((N_DEV,) + b.shape, b.dtype) for b in blocks], 7 * n, copies,
                OTHER_CHIPS + SIBLING + tuple((dx, dy, 1) for dx, dy, _ in OTHER_CHIPS), n_local=n)


def _sum_small(vec_all, ws_all, dg1_all):
    def body(vec_ref, ws_ref, dg1_ref, vec_out, ws_out):
        vec, ws, dg1 = vec_ref[0], ws_ref[0], dg1_ref[0]
        for d in range(1, N_DEV):
            vec, ws, dg1 = vec + vec_ref[d], ws + ws_ref[d], dg1 + dg1_ref[d]
        vec_out[...] = vec
        vec_out[ROW_G1:ROW_G1 + 1, :] = dg1
        ws_out[...] = ws

    vm = pl.BlockSpec(memory_space=pltpu.VMEM)
    return pl.pallas_call(
        body, name="sum_small", in_specs=[vm] * 3, out_specs=[vm, vm],
        out_shape=[jax.ShapeDtypeStruct(vec_all.shape[1:], F32), jax.ShapeDtypeStruct(ws_all.shape[1:], F32)],
    )(vec_all, ws_all, dg1_all)


def _adamw_math(w, g, m, v):
    m = ADAM_B1 * m + (1.0 - ADAM_B1) * g
    v = ADAM_B2 * v + (1.0 - ADAM_B2) * (g * g)
    m_hat = m / (1.0 - ADAM_B1 ** ADAM_STEP)
    v_hat = v / (1.0 - ADAM_B2 ** ADAM_STEP)
    delta = (-ADAM_LR) * (m_hat / (jnp.sqrt(v_hat) + ADAM_EPS) + ADAM_WD * w)
    return delta, m, v


def _adamw(name, g, w, m, v, jobs=()):
    rows, cols = w.shape
    rb = _row_block(rows)

    def body(g_ref, w_ref, m_ref, v_ref, d_ref, nm_ref, nv_ref):
        d_ref[...], nm_ref[...], nv_ref[...] = _adamw_math(w_ref[...], g_ref[...], m_ref[...], v_ref[...])

    blk = pl.BlockSpec((rb, cols), lambda r: (r, 0))
    return _fused_call(
        body, jobs, name=name, grid=(rows // rb,), in_specs=[blk] * 4, out_specs=[blk] * 3,
        out_shape=[jax.ShapeDtypeStruct(w.shape, F32)] * 3, compiler_params=_params(),
    )(g, w, m, v)


def _adamw_small(grads, ws, ms, vs):
    n = len(grads)

    def body(*refs):
        g_refs, w_refs, m_refs, v_refs = refs[:n], refs[n:2 * n], refs[2 * n:3 * n], refs[3 * n:4 * n]
        outs = refs[4 * n:]
        for p in range(n):
            d, nm, nv = _adamw_math(w_refs[p][...], g_refs[p][...], m_refs[p][...], v_refs[p][...])
            outs[p][...] = d
            outs[n + p][...] = nm
            outs[2 * n + p][...] = nv

    vm = pl.BlockSpec(memory_space=pltpu.VMEM)
    shapes = [jax.ShapeDtypeStruct(w.shape, F32) for w in ws]
    out = pl.pallas_call(
        body, name="adamw_small", in_specs=[vm] * (4 * n), out_specs=[vm] * (3 * n), out_shape=shapes * 3,
    )(*grads, *ws, *ms, *vs)
    return out[:n], out[n:2 * n], out[2 * n:]


def _unstack_heads(w_st):
    per = HEAD_DIM // N_CHIPS
    return w_st.reshape(N_CHIPS, HEADS, per, HEAD_DIM).transpose(1, 0, 2, 3).reshape(HEADS, HEAD_DIM, HEAD_DIM)


def _stack_heads(w):
    per = HEAD_DIM // N_CHIPS
    return w.reshape(HEADS, N_CHIPS, per, HEAD_DIM).transpose(1, 0, 2, 3).reshape(N_CHIPS, HEADS * per, HEAD_DIM)


def kernel(x, norm_mix_g, w_in, conv_w, conv_b, w_rgate, b_rgate, w_igate, b_igate, lru_lambda, w_out_a, sgu_ln_g, sgu_ln_b, sgu_w_s, sgu_b_s, w_out_b, w_out, norm_mlp_g, w_up, w_down, norm_final_g, loss_target, m_norm_mix_g, m_w_in, m_conv_w, m_conv_b, m_w_rgate, m_b_rgate, m_w_igate, m_b_igate, m_lru_lambda, m_w_out_a, m_sgu_ln_g, m_sgu_ln_b, m_sgu_w_s, m_sgu_b_s, m_w_out_b, m_w_out, m_norm_mlp_g, m_w_up, m_w_down, m_norm_final_g, v_norm_mix_g, v_w_in, v_conv_w, v_conv_b, v_w_rgate, v_b_rgate, v_w_igate, v_b_igate, v_lru_lambda, v_w_out_a, v_sgu_ln_g, v_sgu_ln_b, v_sgu_w_s, v_sgu_b_s, v_w_out_b, v_w_out, v_norm_mlp_g, v_w_up, v_w_down, v_norm_final_g):
    chip = _chip_index(lax.axis_index("x"), lax.axis_index("y"))
    core = lax.axis_index("c")
    quarter_h = HEAD_DIM // N_CHIPS
    quarter_d = D_MODEL // N_CHIPS

    as_2d = lambda a: a.reshape(-1, a.shape[-1])
    big_w = [as_2d(w) for w in (w_in, w_rgate, w_igate, w_out_a, w_out_b, w_out, w_up, w_down)]
    big_m = [as_2d(w) for w in (m_w_in, m_w_rgate, m_w_igate, m_w_out_a, m_w_out_b, m_w_out, m_w_up, m_w_down)]
    big_v = [as_2d(w) for w in (v_w_in, v_w_rgate, v_w_igate, v_w_out_a, v_w_out_b, v_w_out, v_w_up, v_w_down)]

    packed = jnp.concatenate([conv_w[0], b_rgate[0], b_igate[0]], axis=1)
    packed = jnp.concatenate([packed, jnp.zeros_like(packed)], axis=0)
    s_in, s_r, s_i, s_oa, s_ob, s_out, s_up, s_down = [w.astype(BF16) for w in big_w]
    xs, target = x[0], loss_target[0]
    g3 = norm_final_g.reshape(1, D_MODEL)
    bias_s = jnp.broadcast_to(jnp.transpose(sgu_b_s[0])[:, :, None], (CHUNK, GROUPS, GROUP_DIM)).reshape(CHUNK, D_MODEL)
    core_arr = core.reshape(1).astype(jnp.int32)
    place = jnp.stack([chip, core]).astype(jnp.int32)
    quarter = lambda g: g.reshape(N_CHIPS, D_MODEL // N_CHIPS, D_MODEL)

    def pair_add(nm, g, from_sibling):
        return _pair_add("pair_add_" + nm, core_arr, g.reshape(N_CHIPS, 2, g.shape[1] // 2, g.shape[2]), from_sibling)

    def chip_sum(nm, pair, from_chips):
        return _chip_sum("chip_sum_" + nm, place, pair, from_chips)

    order = jnp.stack([chip, chip ^ 2, chip ^ 1, chip ^ 3]).astype(jnp.int32)
    (z, n1, (w_in_st, wr_st, wi_st)), ((packed_all,),) = _fwd_in(
        xs, norm_mix_g, [s_in, s_r, s_i], order, jobs=[_gather_small_job(packed)])
    pick = lambda lo, hi: packed_all[:, :HEADS, lo:hi].transpose(1, 0, 2).reshape(HEADS, -1)
    conv_w_full = pick(0, quarter_d)
    br_full = pick(quarter_d, quarter_d + quarter_h).reshape(1, D_MODEL)
    bi_full = pick(quarter_d + quarter_h, quarter_d + 2 * quarter_h).reshape(1, D_MODEL)
    wr, wi = _unstack_heads(wr_st), _unstack_heads(wi_st)
    lru = (conv_w_full, conv_b, wr, br_full, wi, bi_full, lru_lambda)
    sgu = (sgu_ln_g, sgu_ln_b, sgu_w_s[0], bias_s)

    after = lambda arrays, result: lax.optimization_barrier((arrays, result))[0]
    late = _sequencer_call("gather_projections", 7, _gather_near_job(after([s_oa, s_ob, s_out], n1)),
                           then=(_gather_far_job, _gather_pass_job))
    w_up_st, w_dn = _sequencer_call("gather_mlp", 8, _gather_near_job(after([s_up, s_down], n1)),
                                    then=(_gather_far_job, _gather_pass_job))
    w_dn = w_dn.reshape(D_FF, D_MODEL)
    (ya, *saved), _ = _fwd_lru(z, *lru)
    yb, _ = _fwd_sgu(z, *sgu)
    w_oa, w_ob, w_o = [w.reshape(D_MODEL, D_MODEL) for w in late]
    (pa, pb, h1, n2), _ = _fwd_merge(ya, yb, z, xs, w_oa, w_ob, w_o, norm_mlp_g)
    (act, dup, dh2b, dh1, loss_part, dg3, dg2), _ = _mlp(n2, h1, target, w_up_st, w_dn, norm_mlp_g, g3)

    d_down, _ = _weight_grad("dw_down", act, dh2b, N_CHIPS, True, False, D_MODEL)
    r_down, = _sequencer_call("send_w_down", 10, _pair_send_job([d_down]))
    d_up, _ = _weight_grad("dw_up", n2, dup, N_CHIPS, False, True, D_MODEL)
    r_up, = _sequencer_call("send_w_up", 11, _pair_send_job([d_up]))
    p_down, p_up = pair_add("w_down", d_down, r_down), pair_add("w_up", d_up, r_up)
    (dz, merged, dpa, dpb, dh1b, dlg, dlb, dws, dbs, dcw, dcb, dwr, dbr, dwi, dbi, dlam), ((q_up, q_down),) = _bwd_mix(
        dh1, pa, pb, z, *saved, w_oa, w_ob, w_o, *sgu, conv_w_full, wr, wi, lru_lambda,
        jobs=[_chip_exchange_job([p_up, p_down])])
    half_up, half_down = chip_sum("w_up", p_up, q_up), chip_sum("w_down", p_down, q_down)
    names = ("w_in", "w_rgate", "w_igate", "w_out_a", "w_out_b", "w_out", "w_up", "w_down")
    d_out, ((full_up, full_down),) = _weight_grad(
        "dw_out", merged, dh1b, 1, False, False, D_MODEL, jobs=[_share_job([half_up, half_down])])
    d_oa, _ = _weight_grad("dw_out_a", ya, dpa, 1, False, False, D_MODEL)
    d_ob, _ = _weight_grad("dw_out_b", yb, dpb, 1, False, False, D_MODEL)
    mids = [quarter(d_oa), quarter(d_ob), quarter(d_out)]
    r_mids = _sequencer_call("send_mids", 1, _pair_send_job(mids))
    gates = [_stack_heads(dwr).astype(BF16), _stack_heads(dwi).astype(BF16)]
    small = _pack_small(dcw, dcb, dbr, dbi, dlam, dlg, dlb, dg2, dg3, loss_part, dbs)
    p_mids = [pair_add(nm, g, r) for nm, g, r in zip(names[3:6], mids, r_mids)]
    q_mids = _sequencer_call("exchange_mids", 2, _chip_exchange_job(p_mids))
    d_in, (r_gates, (vec_all, ws_all)) = _weight_grad(
        "dw_in", n1, dz, N_CHIPS, False, True, IN_SHARD,
        jobs=[_pair_send_job(gates), _gather_all_job([small, dws])])
    r_in, = _sequencer_call("send_w_in", 3, _pair_send_job([d_in]))
    adam_args = {nm: (w, m, v) for nm, w, m, v in zip(names, big_w, big_m, big_v)}

    def adamw(nm, g):
        w, m, v = adam_args[nm]
        g = g.reshape(w.shape)
        return g, _adamw("adamw_" + nm, g, w, m, v)[0]

    p_gates = [pair_add(nm, g, r) for nm, g, r in zip(names[1:3], gates, r_gates)]
    half_mids = [chip_sum(nm, p, q) for nm, p, q in zip(names[3:6], p_mids, q_mids)]
    full_mids = _sequencer_call("share_mids", 12, _share_job(half_mids))
    p_first = [pair_add("w_in", d_in, r_in)] + p_gates
    q_first = _sequencer_call("exchange_w_in", 4, _chip_exchange_job(p_first))
    (grad_x, dg1), _ = _bwd_in(dz, xs, dh1, w_in_st, norm_mix_g)
    dg1_all, = _sequencer_call("gather_dg1", 6, _gather_all_job([dg1]))
    done = {nm: adamw(nm, f) for nm, f in zip(("w_up", "w_down") + names[3:6], [full_up, full_down] + full_mids)}
    q_first = after(q_first, [out[0] for _, out in done.values()])
    half_first = [chip_sum(nm, p, q) for nm, p, q in zip(names[:3], p_first, q_first)]
    full_first = _sequencer_call("share_last", 5, _share_job(half_first))
    done.update({nm: adamw(nm, f) for nm, f in zip(names[:3], full_first)})
    full, big_out = [done[nm][0] for nm in names], [done[nm][1] for nm in names]

    vec, ws_sum = _sum_small(vec_all, ws_all, dg1_all)
    row = lambda r: vec[r:r + 1]
    shard = lambda a, width: lax.dynamic_slice_in_dim(a, chip * width, width, axis=1)
    g_small = dict(
        norm_mix_g=row(ROW_G1), conv_w=shard(vec[ROW_CW:ROW_CW + CONV_WIDTH], quarter_d), conv_b=row(ROW_CB),
        b_rgate=shard(row(ROW_BR).reshape(HEADS, HEAD_DIM), quarter_h),
        b_igate=shard(row(ROW_BI).reshape(HEADS, HEAD_DIM), quarter_h), lru_lambda=row(ROW_LAM),
        sgu_ln_g=row(ROW_LG), sgu_ln_b=row(ROW_LB),
        sgu_w_s=ws_sum.reshape(CHUNK, GROUPS, CHUNK).transpose(1, 0, 2).reshape(GROUPS * CHUNK, CHUNK),
        sgu_b_s=vec[ROW_BS:ROW_BS + GROUPS, 0:CHUNK], norm_mlp_g=row(ROW_G2), norm_final_g=row(ROW_G3))
    loss = vec[ROW_LOSS, 0]
    small_names = list(g_small)
    given = dict(
        norm_mix_g=(norm_mix_g, m_norm_mix_g, v_norm_mix_g), conv_w=(conv_w, m_conv_w, v_conv_w),
        conv_b=(conv_b, m_conv_b, v_conv_b), b_rgate=(b_rgate, m_b_rgate, v_b_rgate),
        b_igate=(b_igate, m_b_igate, v_b_igate), lru_lambda=(lru_lambda, m_lru_lambda, v_lru_lambda),
        sgu_ln_g=(sgu_ln_g, m_sgu_ln_g, v_sgu_ln_g), sgu_ln_b=(sgu_ln_b, m_sgu_ln_b, v_sgu_ln_b),
        sgu_w_s=(sgu_w_s, m_sgu_w_s, v_sgu_w_s), sgu_b_s=(sgu_b_s, m_sgu_b_s, v_sgu_b_s),
        norm_mlp_g=(norm_mlp_g, m_norm_mlp_g, v_norm_mlp_g), norm_final_g=(norm_final_g, m_norm_final_g, v_norm_final_g))
    g2d = [g_small[nm] for nm in small_names]
    to2d = lambda a, g: a.reshape(g.shape)
    d_s, m_s, v_s = _adamw_small(
        g2d, *[[to2d(given[nm][q], g) for nm, g in zip(small_names, g2d)] for q in range(3)])

    shapes = dict(
        norm_mix_g=norm_mix_g, w_in=w_in, conv_w=conv_w, conv_b=conv_b, w_rgate=w_rgate, b_rgate=b_rgate,
        w_igate=w_igate, b_igate=b_igate, lru_lambda=lru_lambda, w_out_a=w_out_a, sgu_ln_g=sgu_ln_g,
        sgu_ln_b=sgu_ln_b, sgu_w_s=sgu_w_s, sgu_b_s=sgu_b_s, w_out_b=w_out_b, w_out=w_out, norm_mlp_g=norm_mlp_g,
        w_up=w_up, w_down=w_down, norm_final_g=norm_final_g)
    grads, deltas, new_m, new_v = {}, {}, {}, {}
    for nm, g, (d, nmom, nvar) in zip(names, full, big_out):
        grads[nm], deltas[nm], new_m[nm], new_v[nm] = g, d, nmom, nvar
    for p, nm in enumerate(small_names):
        grads[nm], deltas[nm], new_m[nm], new_v[nm] = g2d[p], d_s[p], m_s[p], v_s[p]
    order = list(shapes)
    out = [loss, grad_x[None]]
    for group in (grads, deltas, new_m, new_v):
        out += [group[nm].reshape(shapes[nm].shape) for nm in order]
    return tuple(out)
```

```python
import functools

import jax
import jax.numpy as jnp
from jax import lax
from jax.experimental import pallas as pl
from jax.experimental.pallas import tpu as pltpu
from jax.experimental.pallas import tpu_sc as plsc

F32 = jnp.float32
BF16 = jnp.bfloat16
MESH = pl.DeviceIdType.MESH

D_MODEL = 1024
D_IN = 6 * D_MODEL
D_FF = 4 * D_MODEL
N_CHIPS = 4
IN_SHARD = D_IN // N_CHIPS
HEADS = 4
HEAD_DIM = D_MODEL // HEADS
GROUPS = 4
GROUP_DIM = D_MODEL // GROUPS
CHUNK = 128
CONV_WIDTH = 4
LRU_C = 8.0
NORM_EPS = 1e-6
LN_EPS = 1e-5

ADAM_LR = 0.001
ADAM_B1 = 0.9
ADAM_B2 = 0.999
ADAM_EPS = 1e-08
ADAM_WD = 0.01
ADAM_STEP = 10

SUBLANES = 8
MM_TILE = 512
IN_TILE = 1024
SEQ_TILE = 256
DW_TILE = 2048
VMEM_LIMIT_BYTES = 56 * 1024 * 1024

GELU_K0 = 0.7978845608028654
GELU_K1 = 0.044715


def _params(n_grid_axes=1):
    return pltpu.CompilerParams(
        dimension_semantics=("arbitrary",) * n_grid_axes, vmem_limit_bytes=VMEM_LIMIT_BYTES)


def _resident(shape):
    nd = len(shape)
    return pl.BlockSpec(shape, lambda *_: (0,) * nd, pipeline_mode=pl.Buffered(1))


def _const(shape):
    nd = len(shape)
    return pl.BlockSpec(shape, lambda *_: (0,) * nd)


def _dot(a, b):
    return jnp.dot(a, b, preferred_element_type=F32)


def _dot_nt(a, b):
    return lax.dot_general(a, b, (((1,), (1,)), ((), ())), preferred_element_type=F32)


def _dot_tn(a, b):
    return lax.dot_general(a, b, (((0,), (0,)), ((), ())), preferred_element_type=F32)


def _gelu(x):
    t = jnp.tanh(GELU_K0 * x * (1.0 + GELU_K1 * x * x))
    return 0.5 * x * (1.0 + t)


def _gelu_and_grad(x):
    x2 = x * x
    t = jnp.tanh(GELU_K0 * x * (1.0 + GELU_K1 * x2))
    g = 0.5 * x * (1.0 + t)
    dg = 0.5 * (1.0 + t) + 0.5 * x * (1.0 - t * t) * (GELU_K0 * (1.0 + 3.0 * GELU_K1 * x2))
    return g, dg


def _rms(x):
    r = lax.rsqrt(jnp.mean(x * x, axis=-1, keepdims=True) + NORM_EPS)
    return x * r, r


def _rms_bwd(dn, xhat, r):
    return r * (dn - xhat * jnp.mean(dn * xhat, axis=-1, keepdims=True))


def _col_sum(v):
    return jnp.sum(v, axis=0, keepdims=True)


def _shift_down(x, tail8, k):
    xs = pltpu.roll(x, k, 0)
    ts = pltpu.roll(tail8, k, 0)
    ridx = lax.broadcasted_iota(jnp.int32, tail8.shape, 0)
    head = jnp.where(ridx < k, ts, xs[0:SUBLANES])
    return jnp.concatenate([head, xs[SUBLANES:]], axis=0)


def _shift_up(x, head8, k):
    n = x.shape[0]
    xs = pltpu.roll(x, n - k, 0)
    hs = pltpu.roll(head8, SUBLANES - k, 0)
    ridx = lax.broadcasted_iota(jnp.int32, head8.shape, 0)
    last = jnp.where(ridx >= SUBLANES - k, hs, xs[n - SUBLANES:n])
    return jnp.concatenate([xs[:n - SUBLANES], last], axis=0)


def _scan_forward(a, b, carry):
    n, cols = a.shape
    groups = n // SUBLANES
    a = a.reshape(groups, SUBLANES, cols)
    b = b.reshape(groups, SUBLANES, cols)
    sub = lax.broadcasted_iota(jnp.int32, a.shape, 1)
    for s in (1, 2, 4):
        a_s = pltpu.roll(a, s, 1)
        b_s = pltpu.roll(b, s, 1)
        m = sub >= s
        b = jnp.where(m, a * b_s + b, b)
        a = jnp.where(m, a * a_s, a)
    out = []
    for g in range(groups):
        h = a[g] * carry + b[g]
        out.append(h)
        carry = h[SUBLANES - 1:SUBLANES]
    return jnp.concatenate(out, axis=0), carry


def _scan_backward(a, b, carry):
    n, cols = a.shape
    groups = n // SUBLANES
    a = a.reshape(groups, SUBLANES, cols)
    b = b.reshape(groups, SUBLANES, cols)
    sub = lax.broadcasted_iota(jnp.int32, a.shape, 1)
    for s in (1, 2, 4):
        a_s = pltpu.roll(a, SUBLANES - s, 1)
        b_s = pltpu.roll(b, SUBLANES - s, 1)
        m = sub < SUBLANES - s
        b = jnp.where(m, a * b_s + b, b)
        a = jnp.where(m, a * a_s, a)
    out = [None] * groups
    for g in reversed(range(groups)):
        h = a[g] * carry + b[g]
        out[g] = h
        carry = h[0:1]
    return jnp.concatenate(out, axis=0), carry


def _softplus_neg(lam):
    e = jnp.exp(-jnp.abs(lam))
    u = 1.0 + e
    log1p_e = jnp.where(u == 1.0, e, jnp.log(u) * (e / jnp.where(u == 1.0, 1.0, u - 1.0)))
    return jnp.maximum(-lam, 0.0) + log1p_e


def _lru_gates(xa, tail8, cw_ref, cb_ref, wr_ref, br_ref, wi_ref, bi_ref, lam_ref):
    cw = cw_ref[...]
    xc = cb_ref[...] + cw[0:1] * xa
    for k in range(1, CONV_WIDTH):
        xc = xc + cw[k:k + 1] * _shift_down(xa, tail8, k)
    xcb = xc.astype(BF16)
    pre_r, pre_i = [], []
    for h in range(HEADS):
        cols = slice(h * HEAD_DIM, (h + 1) * HEAD_DIM)
        pre_r.append(_dot(xcb[:, cols], wr_ref[h]))
        pre_i.append(_dot(xcb[:, cols], wi_ref[h]))
    r = jax.nn.sigmoid(jnp.concatenate(pre_r, axis=1) + br_ref[...])
    ig = jax.nn.sigmoid(jnp.concatenate(pre_i, axis=1) + bi_ref[...])
    _, a, mult = _decay(r, lam_ref)
    return xc, r, ig, a, mult


def _decay(r, lam_ref):
    sp = _softplus_neg(lam_ref[...])
    log_a = ((-LRU_C) * sp) * r
    a = jnp.exp(log_a)
    th = jnp.tanh(log_a)
    return sp, a, jnp.sqrt((-2.0 * th) / (1.0 - th))


class _Job:
    def __init__(self, inputs, out_shape, n_sem, copies, peers, aliases=None, n_local=0):
        self.inputs, self.out_shape, self.n_sem, self.copies = list(inputs), list(out_shape), n_sem, copies
        self.aliases, self.n_local = dict(aliases or {}), n_local
        self.peers = tuple(peers)


def _fused_call(body, jobs, *, name, grid, in_specs, out_specs, out_shape, scratch_shapes=(),
                input_output_aliases=None, compiler_params=None, n_prefetch=0, jobs_start_after=None):
    single = not isinstance(out_shape, (list, tuple))
    out_specs = [out_specs] if single else list(out_specs)
    out_shape = [out_shape] if single else list(out_shape)
    n_scr = len(scratch_shapes)
    in_specs, scratch_shapes = list(in_specs), list(scratch_shapes)
    n_in, n_out = len(in_specs), len(out_shape)
    aliases = dict(input_output_aliases or {})
    in_at, out_at = [], []
    for job in jobs:
        in_at.append(len(in_specs))
        out_at.append(len(out_shape))
        for i, o in job.aliases.items():
            aliases[n_prefetch + len(in_specs) + i] = len(out_shape) + o
        in_specs += [ANY] * len(job.inputs)
        out_specs += [ANY] * len(job.out_shape)
        out_shape += job.out_shape
        scratch_shapes += [pltpu.SemaphoreType.DMA((job.n_sem,)), pltpu.SemaphoreType.DMA((job.n_sem,)),
                           pltpu.SemaphoreType.DMA((max(job.n_local, 1),))]
    n_in_all, n_out_all = len(in_specs), len(out_shape)

    def full_body(*refs):
        prefetch, refs = refs[:n_prefetch], refs[n_prefetch:]
        ins, outs, scr = refs[:n_in_all], refs[n_in_all:n_in_all + n_out_all], refs[n_in_all + n_out_all:]

        def copies(q):
            job = jobs[q]
            return job.copies(ins[in_at[q]:in_at[q] + len(job.inputs)], outs[out_at[q]:out_at[q] + len(job.out_shape)],
                              *scr[n_scr + 3 * q:n_scr + 3 * q + 3])

        def start():
            for q in range(len(jobs)):
                sends, _, local = copies(q)
                for cp in local + sends:
                    cp.start()

        def finish():
            every = [copies(q) for q in range(len(jobs))]
            for _, arrivals, _ in every:
                for cp in arrivals:
                    cp.wait_recv()
            for sends, _, local in every:
                for cp in sends:
                    cp.wait_send()
                for cp in local:
                    cp.wait()

        if not grid:
            start()
            finish()
            return
        ids = [pl.program_id(a) for a in range(len(grid))]
        at_step = lambda step: functools.reduce(jnp.logical_and, [i == k for i, k in zip(ids, step)])
        if jobs and jobs_start_after is None:
            pl.when(at_step((0,) * len(grid)))(start)
        body(*prefetch, *ins[:n_in], *outs[:n_out], *scr[:n_scr])
        if jobs and jobs_start_after is not None:
            pl.when(at_step(jobs_start_after))(start)
        if jobs:
            pl.when(functools.reduce(jnp.logical_and, [i == g - 1 for i, g in zip(ids, grid)]))(finish)

    if n_prefetch:
        layout = dict(grid_spec=pltpu.PrefetchScalarGridSpec(
            num_scalar_prefetch=n_prefetch, grid=grid, in_specs=in_specs, out_specs=out_specs,
            scratch_shapes=scratch_shapes))
    else:
        layout = dict(grid=grid, in_specs=in_specs, out_specs=out_specs, scratch_shapes=scratch_shapes)
    call = pl.pallas_call(
        full_body, name=name, out_shape=out_shape, input_output_aliases=aliases, compiler_params=compiler_params,
        **layout)

    def run(*args):
        res = call(*args, *[a for job in jobs for a in job.inputs])
        mine = res[0] if single else list(res[:n_out])
        return mine, [list(res[at:at + len(job.out_shape)]) for at, job in zip(out_at, jobs)]

    return run


def _fwd_in(x, g1, shards, order, jobs=()):
    t = x.shape[0]
    rows_per_step = min(IN_TILE, t)
    n_tiles = t // rows_per_step
    n = len(shards)
    halves = [s.shape[0] // 2 for s in shards]

    def body(order_ref, x_ref, g_ref, *refs):
        del order_ref
        ins, (z_ref, n_ref), outs = refs[:n], refs[n:n + 2], refs[n + 2:2 * n + 2]
        wbuf, nbuf, send, recv, local = refs[2 * n + 2:]
        s, i = pl.program_id(0), pl.program_id(1)
        x_, y_, c, chips = _place()
        k_me = _chip_index(x_, y_)

        def block(w, chip, pc):
            return outs[w].at[_chip_index(*chip), pl.ds(pc * halves[w], halves[w]), :]

        def over_ici(w, j, landing):
            return pltpu.make_async_remote_copy(
                src_ref=ins[w].at[pl.ds(c * halves[w], halves[w]), :],
                dst_ref=block(w, chips[j] if landing else (x_, y_), c), send_sem=send.at[6 * w + j],
                recv_sem=recv.at[6 * w + j], device_id=(*chips[j], c), device_id_type=MESH)

        def to_sibling(w, j, landing):
            blk = block(w, chips[j], 1 - c if landing else c)
            return pltpu.make_async_remote_copy(
                src_ref=blk, dst_ref=blk, send_sem=send.at[6 * w + 3 + j], recv_sem=recv.at[6 * w + 3 + j],
                device_id=(x_, y_, 1 - c), device_id_type=MESH)

        own = [pltpu.make_async_copy(wbuf, outs[0].at[k_me], local.at[0])]
        own += [pltpu.make_async_copy(ins[w], outs[w].at[k_me], local.at[w]) for w in range(1, n)]

        @pl.when((s == 0) & (i == 0))
        def _():
            for j in range(2):
                for w in range(n):
                    over_ici(w, j, False).start()
            load = pltpu.make_async_copy(ins[0], wbuf, local.at[n])
            load.start()
            load.wait()
            for cp in own:
                cp.start()

        for j in range(N_CHIPS - 1):
            @pl.when((s == j + 1) & (i == 0))
            def _(j=j):
                for w in range(n):
                    over_ici(w, j, True).wait_recv()
                for w in range(n):
                    to_sibling(w, j, False).start()
                if j == 0:
                    for w in range(n):
                        over_ici(w, 2, False).start()
                    own[0].wait()
                for w in range(n):
                    to_sibling(w, j, True).wait_recv()
                load = pltpu.make_async_copy(outs[0].at[_chip_index(*chips[j])], wbuf, local.at[n])
                load.start()
                load.wait()

        rows = pl.ds(pl.multiple_of(i * rows_per_step, rows_per_step), rows_per_step)

        @pl.when(s == 0)
        def _():
            xhat, _ = _rms(x_ref[...])
            nrm = (xhat * g_ref[...]).astype(BF16)
            nbuf[rows, :] = nrm
            n_ref[...] = nrm

        z_ref[...] = _dot(nbuf[rows, :], wbuf[...])

        @pl.when((s == N_CHIPS - 1) & (i == n_tiles - 1))
        def _():
            for j in range(N_CHIPS - 1):
                for w in range(n):
                    over_ici(w, j, False).wait_send()
                    to_sibling(w, j, False).wait_send()
            for cp in own[1:]:
                cp.wait()

    once = lambda s, i, order: (jnp.where(s == 0, i, n_tiles - 1), 0)
    (z, n1, *stacked), job_outs = _fused_call(
        body, jobs, name="fwd_in", grid=(N_CHIPS, n_tiles), n_prefetch=1,
        in_specs=[pl.BlockSpec((rows_per_step, D_MODEL), once), _const((1, D_MODEL))] + [ANY] * n,
        out_specs=[pl.BlockSpec((rows_per_step, IN_SHARD), lambda s, i, order: (i, order[s])),
                   pl.BlockSpec((rows_per_step, D_MODEL), once)] + [ANY] * n,
        out_shape=[jax.ShapeDtypeStruct((t, D_IN), F32), jax.ShapeDtypeStruct((t, D_MODEL), BF16)]
        + [jax.ShapeDtypeStruct((N_CHIPS,) + s.shape, s.dtype) for s in shards],
        scratch_shapes=[pltpu.VMEM(shards[0].shape, BF16), pltpu.VMEM((t, D_MODEL), BF16),
                        pltpu.SemaphoreType.DMA((6 * n,)),
                        pltpu.SemaphoreType.DMA((6 * n,)), pltpu.SemaphoreType.DMA((n + 1,))],
        compiler_params=_params(2), jobs_start_after=(1, 0),
    )(order, x, g1, *shards)
    return (z, n1, stacked), job_outs


def _fwd_lru(z, conv_w, conv_b, wr, br, wi, bi, lam, jobs=()):
    t = z.shape[0]

    def body(xa_ref, ga_ref, cw_ref, cb_ref, wr_ref, br_ref, wi_ref, bi_ref, lam_ref, ya_ref, h_ref, xc_ref, r_ref,
             ig_ref, tail_ref, carry_ref):
        @pl.when(pl.program_id(0) == 0)
        def _():
            tail_ref[...] = jnp.zeros_like(tail_ref)
            carry_ref[...] = jnp.zeros_like(carry_ref)

        xa = xa_ref[...]
        xc, r, ig, a, mult = _lru_gates(xa, tail_ref[...], cw_ref, cb_ref, wr_ref, br_ref, wi_ref, bi_ref, lam_ref)
        tail_ref[...] = xa[SEQ_TILE - SUBLANES:]
        xc_ref[...], r_ref[...], ig_ref[...] = xc, r, ig
        h, carry = _scan_forward(a, xc * ig * mult, carry_ref[...])
        carry_ref[...] = carry
        h_ref[...] = h
        ya_ref[...] = (h * _gelu(ga_ref[...])).astype(BF16)

    tile = lambda j: pl.BlockSpec((SEQ_TILE, D_MODEL), lambda i: (i, j))
    return _fused_call(
        body, jobs, name="fwd_lru", grid=(t // SEQ_TILE,),
        in_specs=[tile(0), tile(1), _const((CONV_WIDTH, D_MODEL)), _const((1, D_MODEL)),
                  _resident((HEADS, HEAD_DIM, HEAD_DIM)), _const((1, D_MODEL)),
                  _resident((HEADS, HEAD_DIM, HEAD_DIM)), _const((1, D_MODEL)), _const((1, D_MODEL))],
        out_specs=[tile(0)] * 5,
        out_shape=[jax.ShapeDtypeStruct((t, D_MODEL), BF16)] + [jax.ShapeDtypeStruct((t, D_MODEL), F32)] * 4,
        scratch_shapes=[pltpu.VMEM((SUBLANES, D_MODEL), F32), pltpu.VMEM((1, D_MODEL), F32)],
        compiler_params=_params(),
    )(z, z, conv_w, conv_b, wr, br, wi, bi, lam)


def _sgu_forward_parts(ub, vb, lg_ref, lb_ref):
    u, du = _gelu_and_grad(ub)
    vg, dvg = _gelu_and_grad(vb)
    mu = jnp.mean(vg, axis=-1, keepdims=True)
    d = vg - mu
    rstd = lax.rsqrt(jnp.mean(d * d, axis=-1, keepdims=True) + LN_EPS)
    vhat = d * rstd
    vn = (vhat * lg_ref[...] + lb_ref[...]).astype(BF16)
    return u, du, dvg, rstd, vhat, vn


def _causal_mask():
    rows = lax.broadcasted_iota(jnp.int32, (CHUNK, CHUNK), 0)
    cols = lax.broadcasted_iota(jnp.int32, (CHUNK, CHUNK), 1)
    return rows >= cols


def _fwd_sgu(z, ln_g, ln_b, w_s, bias_full, jobs=()):
    t = z.shape[0]

    def body(ub_ref, vb_ref, lg_ref, lb_ref, ws_ref, bias_ref, yb_ref):
        u, _, _, _, _, vn = _sgu_forward_parts(ub_ref[...], vb_ref[...], lg_ref, lb_ref)
        mask = _causal_mask()
        wm = [jnp.where(mask, ws_ref[g], 0.0).astype(BF16) for g in range(GROUPS)]
        for c in range(SEQ_TILE // CHUNK):
            rows = slice(c * CHUNK, (c + 1) * CHUNK)
            for g in range(GROUPS):
                cols = slice(g * GROUP_DIM, (g + 1) * GROUP_DIM)
                sp = _dot(wm[g], vn[rows, cols]) + bias_ref[:, cols]
                yb_ref[rows, cols] = (u[rows, cols] * sp).astype(BF16)

    tile = lambda j: pl.BlockSpec((SEQ_TILE, D_MODEL), lambda i: (i, j))
    return _fused_call(
        body, jobs, name="fwd_sgu", grid=(t // SEQ_TILE,),
        in_specs=[tile(2), tile(3), _const((1, D_MODEL)), _const((1, D_MODEL)),
                  _const((GROUPS, CHUNK, CHUNK)), _const((CHUNK, D_MODEL))],
        out_specs=tile(0),
        out_shape=jax.ShapeDtypeStruct((t, D_MODEL), BF16),
        compiler_params=_params(),
    )(z, z, ln_g, ln_b, w_s, bias_full)


def _fwd_merge(ya, yb, z, x, w_oa, w_ob, w_out, g2, jobs=()):
    t = x.shape[0]

    def body(ya_ref, yb_ref, m_ref, x_ref, woa_ref, wob_ref, wout_ref, g_ref, pa_ref, pb_ref, h1_ref, n2_ref):
        pa = _dot(ya_ref[...], woa_ref[...])
        pb = _dot(yb_ref[...], wob_ref[...])
        pa_ref[...] = pa
        pb_ref[...] = pb
        merged = jax.nn.sigmoid(m_ref[:, :D_MODEL]) * pa + jax.nn.sigmoid(m_ref[:, D_MODEL:]) * pb
        h1 = x_ref[...] + _dot(merged.astype(BF16), wout_ref[...])
        h1_ref[...] = h1
        xhat, _ = _rms(h1)
        n2_ref[...] = (xhat * g_ref[...]).astype(BF16)

    tile = pl.BlockSpec((MM_TILE, D_MODEL), lambda i: (i, 0))
    sq = _resident((D_MODEL, D_MODEL))
    return _fused_call(
        body, jobs, name="fwd_merge", grid=(t // MM_TILE,),
        in_specs=[tile, tile, pl.BlockSpec((MM_TILE, 2 * D_MODEL), lambda i: (i, 2)), tile, sq, sq, sq,
                  _const((1, D_MODEL))],
        out_specs=[tile, tile, tile, tile],
        out_shape=[jax.ShapeDtypeStruct((t, D_MODEL), F32)] * 3 + [jax.ShapeDtypeStruct((t, D_MODEL), BF16)],
        compiler_params=_params(),
    )(ya, yb, z, x, w_oa, w_ob, w_out, g2)


def _mlp(n2, h1, target, w_up_st, w_down, g2, g3, jobs=()):
    t = n2.shape[0]

    def body(n2_ref, h1_ref, tgt_ref, wup_ref, wdown_ref, g2_ref, g3_ref, act_ref, dup_ref, dh2b_ref, dh1_ref,
             loss_ref, dg3_ref, dg2_ref, relu_ref):
        @pl.when(pl.program_id(0) == 0)
        def _():
            for ref in (loss_ref, dg3_ref, dg2_ref):
                ref[...] = jnp.zeros_like(ref)

        n2 = n2_ref[...]
        h1 = h1_ref[...]
        h2 = h1
        for k in range(N_CHIPS):
            cols = slice(k * D_MODEL, (k + 1) * D_MODEL)
            r = jnp.maximum(_dot(n2, wup_ref[k]), 0.0)
            relu_ref[:, cols] = r
            act = (r * r).astype(BF16)
            act_ref[:, cols] = act
            h2 = h2 + _dot(act, wdown_ref[cols, :])
        xhat, r3 = _rms(h2)
        diff = xhat * g3_ref[...] - tgt_ref[...]
        sq = jnp.sum(diff * diff, axis=1, keepdims=True)
        loss_ref[...] = loss_ref[...] + (0.5 / D_MODEL) * jnp.sum(sq, axis=0, keepdims=True)
        dy = diff * (1.0 / D_MODEL)
        dg3_ref[...] = dg3_ref[...] + _col_sum(dy * xhat)
        dh2 = _rms_bwd(dy * g3_ref[...], xhat, r3)
        dh2b = dh2.astype(BF16)
        dh2b_ref[...] = dh2b
        dn2 = jnp.zeros((SEQ_TILE, D_MODEL), F32)
        for k in range(N_CHIPS):
            cols = slice(k * D_MODEL, (k + 1) * D_MODEL)
            dup = (_dot_nt(dh2b, wdown_ref[cols, :]) * (2.0 * relu_ref[:, cols])).astype(BF16)
            dup_ref[:, cols] = dup
            dn2 = dn2 + _dot_nt(dup, wup_ref[k])
        xhat, r2 = _rms(h1)
        dg2_ref[...] = dg2_ref[...] + _col_sum(dn2 * xhat)
        dh1_ref[...] = dh2 + _rms_bwd(dn2 * g2_ref[...], xhat, r2)

    tile = pl.BlockSpec((SEQ_TILE, D_MODEL), lambda i: (i, 0))
    wide = pl.BlockSpec((SEQ_TILE, D_FF), lambda i: (i, 0))
    vec = _const((1, D_MODEL))
    vec_shape = jax.ShapeDtypeStruct((1, D_MODEL), F32)
    return _fused_call(
        body, jobs, name="mlp", grid=(t // SEQ_TILE,),
        in_specs=[tile, tile, tile, _resident((N_CHIPS, D_MODEL, D_MODEL)), _resident((D_FF, D_MODEL)), vec, vec],
        out_specs=[wide, wide, tile, tile, _const((SUBLANES, 128)), vec, vec],
        out_shape=[jax.ShapeDtypeStruct((t, D_FF), BF16), jax.ShapeDtypeStruct((t, D_FF), BF16),
                   jax.ShapeDtypeStruct((t, D_MODEL), BF16), jax.ShapeDtypeStruct((t, D_MODEL), F32),
                   jax.ShapeDtypeStruct((SUBLANES, 128), F32), vec_shape, vec_shape],
        scratch_shapes=[pltpu.VMEM((SEQ_TILE, D_FF), F32)],
        compiler_params=_params(),
    )(n2, h1, target, w_up_st, w_down, g2, g3)


def _bwd_mix(dh1, pa, pb, z, h, xc, r, ig, w_oa, w_ob, w_out, ln_g, ln_b, w_s, bias_full, conv_w, wr, wi, lam, jobs=()):
    t = dh1.shape[0]
    n_tiles = t // SEQ_TILE
    per_tile = SEQ_TILE // SUBLANES

    def merge_part(dh1_ref, pa_ref, pb_ref, m_ref, woa_ref, wob_ref, wout_ref, dz_ref, dya_ref, dyb_ref, mg_ref,
                   dpa_ref, dpb_ref, dh1b_ref):
        dh1b = dh1_ref[...].astype(BF16)
        dh1b_ref[...] = dh1b
        dm = _dot_nt(dh1b, wout_ref[...])
        pa = pa_ref[...]
        pb = pb_ref[...]
        sa = jax.nn.sigmoid(m_ref[:, :D_MODEL])
        sb = jax.nn.sigmoid(m_ref[:, D_MODEL:])
        mg_ref[...] = (sa * pa + sb * pb).astype(BF16)
        dz_ref[:, :D_MODEL] = (dm * pa * sa * (1.0 - sa)).astype(BF16)
        dz_ref[:, D_MODEL:] = (dm * pb * sb * (1.0 - sb)).astype(BF16)
        dpa = (dm * sa).astype(BF16)
        dpb = (dm * sb).astype(BF16)
        dpa_ref[...] = dpa
        dpb_ref[...] = dpb
        dya_ref[...] = _dot_nt(dpa, woa_ref[...])
        dyb_ref[...] = _dot_nt(dpb, wob_ref[...])

    def sgu_part(dyb_ref, ub_ref, vb_ref, lg_ref, lb_ref, ws_ref, bias_ref, dz_ref, dlg_ref, dlb_ref, dws_ref, dbs_ref,
                 dvn_ref, dsp_acc):
        i = pl.program_id(0)

        @pl.when(i == 0)
        def _():
            dlg_ref[...] = jnp.zeros_like(dlg_ref)
            dlb_ref[...] = jnp.zeros_like(dlb_ref)
            dws_ref[...] = jnp.zeros_like(dws_ref)
            dsp_acc[...] = jnp.zeros_like(dsp_acc)

        u, du, dvg, rstd, vhat, vn = _sgu_forward_parts(ub_ref[...], vb_ref[...], lg_ref, lb_ref)
        dyb = dyb_ref[...]
        mask = _causal_mask()
        wm = [jnp.where(mask, ws_ref[g], 0.0).astype(BF16) for g in range(GROUPS)]
        for c in range(SEQ_TILE // CHUNK):
            rows = slice(c * CHUNK, (c + 1) * CHUNK)
            for g in range(GROUPS):
                cols = slice(g * GROUP_DIM, (g + 1) * GROUP_DIM)
                vn_blk = vn[rows, cols]
                sp = _dot(wm[g], vn_blk) + bias_ref[:, cols]
                dyb_blk = dyb[rows, cols]
                dz_ref[rows, cols] = (dyb_blk * sp * du[rows, cols]).astype(BF16)
                dsp = dyb_blk * u[rows, cols]
                dsp_acc[:, cols] = dsp_acc[:, cols] + dsp
                dspb = dsp.astype(BF16)
                dvn_ref[rows, cols] = _dot_tn(wm[g], dspb)
                wcols = slice(g * CHUNK, (g + 1) * CHUNK)
                dws_ref[:, wcols] = dws_ref[:, wcols] + jnp.where(mask, _dot_nt(dspb, vn_blk), 0.0)
        dvn = dvn_ref[...]
        dlg_ref[...] = dlg_ref[...] + _col_sum(dvn * vhat)
        dlb_ref[...] = dlb_ref[...] + _col_sum(dvn)
        dvhat = dvn * lg_ref[...]
        dvgel = rstd * (dvhat - jnp.mean(dvhat, axis=-1, keepdims=True)
                        - vhat * jnp.mean(dvhat * vhat, axis=-1, keepdims=True))
        dz_ref[:, D_MODEL:] = (dvgel * dvg).astype(BF16)

        @pl.when(i == n_tiles - 1)
        def _():
            lane = lax.broadcasted_iota(jnp.int32, (CHUNK, 128), 1)
            out = jnp.zeros((CHUNK, 128), F32)
            for g in range(GROUPS):
                s = jnp.sum(dsp_acc[:, g * GROUP_DIM:(g + 1) * GROUP_DIM], axis=1, keepdims=True)
                out = out + jnp.where(lane == g, s, 0.0)
            dbs_ref[...] = out

    def lru_part(dya_ref, xa_ref, ga_ref, h_ref, h_prev_ref, xc_ref, r_ref, ig_ref, cw_ref, wr_ref, wi_ref, lam_ref,
                 dz_ref, dcw_ref, dcb_ref, dwr_ref, dbr_ref, dwi_ref, dbi_ref, dlam_ref, lam_carry, dxc_head):
        i = pl.program_id(0)

        @pl.when(i == 0)
        def _():
            for ref in (dcw_ref, dcb_ref, dwr_ref, dbr_ref, dwi_ref, dbi_ref, dlam_ref, lam_carry, dxc_head):
                ref[...] = jnp.zeros_like(ref)

        first_tile = i == n_tiles - 1
        h_tail = jnp.where(first_tile, 0.0, h_prev_ref[...])
        xc, r, ig = xc_ref[...], r_ref[...], ig_ref[...]
        xcb = xc.astype(BF16)
        sp, a, mult = _decay(r, lam_ref)
        h = h_ref[...]
        h_prev = _shift_down(h, h_tail, 1)
        dya = dya_ref[...]
        gg, dgg = _gelu_and_grad(ga_ref[...])
        dz_ref[:, D_MODEL:] = (dya * h * dgg).astype(BF16)
        ones = jnp.ones((SUBLANES, D_MODEL), F32)
        lam_t, lam_first = _scan_backward(_shift_up(a, ones, 1), dya * gg, lam_carry[...])
        lam_carry[...] = a[0:1] * lam_first
        dmult = lam_t * xc * ig
        dla = lam_t * h_prev * a - dmult * (a * a) / mult
        dr = dla * ((-LRU_C) * sp)
        dlam_ref[...] = dlam_ref[...] + _col_sum(dla * r) * (LRU_C * jax.nn.sigmoid(-lam_ref[...]))
        dpr = dr * r * (1.0 - r)
        dpi = lam_t * xc * mult * ig * (1.0 - ig)
        dbr_ref[...] = dbr_ref[...] + _col_sum(dpr)
        dbi_ref[...] = dbi_ref[...] + _col_sum(dpi)
        dprb = dpr.astype(BF16)
        dpib = dpi.astype(BF16)
        dxc_gate = []
        for hd in range(HEADS):
            cols = slice(hd * HEAD_DIM, (hd + 1) * HEAD_DIM)
            dxc_gate.append(_dot_nt(dprb[:, cols], wr_ref[hd]) + _dot_nt(dpib[:, cols], wi_ref[hd]))
            dwr_ref[hd] = dwr_ref[hd] + _dot_tn(xcb[:, cols], dprb[:, cols])
            dwi_ref[hd] = dwi_ref[hd] + _dot_tn(xcb[:, cols], dpib[:, cols])
        dxc = lam_t * ig * mult + jnp.concatenate(dxc_gate, axis=1)
        dcb_ref[...] = dcb_ref[...] + _col_sum(dxc)
        cw = cw_ref[...]
        head = dxc_head[...]
        xa = xa_ref[...]
        dxa = cw[0:1] * dxc
        dcw_ref[0:1, :] = dcw_ref[0:1, :] + _col_sum(dxc * xa)
        for k in range(1, CONV_WIDTH):
            dxc_k = _shift_up(dxc, head, k)
            dxa = dxa + cw[k:k + 1] * dxc_k
            dcw_ref[k:k + 1, :] = dcw_ref[k:k + 1, :] + _col_sum(dxc_k * xa)
        dxc_head[...] = dxc[0:SUBLANES]
        dz_ref[:, :D_MODEL] = dxa.astype(BF16)

    def body(dh1_ref, pa_ref, pb_ref, z_ref, h_ref, h_prev_ref, xc_ref, r_ref, ig_ref, woa_ref, wob_ref, wout_ref,
             lg_ref, lb_ref, ws_ref, bias_ref, cw_ref, wr_ref, wi_ref, lam_ref, dz_ref, mg_ref, dpa_ref, dpb_ref,
             dh1b_ref, dlg_ref, dlb_ref, dws_ref, dbs_ref, dcw_ref, dcb_ref, dwr_ref, dbr_ref, dwi_ref, dbi_ref,
             dlam_ref, dya_ref, dyb_ref, dvn_ref, dsp_acc, lam_carry, dxc_head):
        def cols(ref, first, count):
            return ref.at[:, pl.ds(first * D_MODEL, count * D_MODEL)]

        merge_part(dh1_ref, pa_ref, pb_ref, cols(z_ref, 4, 2), woa_ref, wob_ref, wout_ref, cols(dz_ref, 4, 2), dya_ref,
                   dyb_ref, mg_ref, dpa_ref, dpb_ref, dh1b_ref)
        sgu_part(dyb_ref, cols(z_ref, 2, 1), cols(z_ref, 3, 1), lg_ref, lb_ref, ws_ref, bias_ref, cols(dz_ref, 2, 2),
                 dlg_ref, dlb_ref, dws_ref, dbs_ref, dvn_ref, dsp_acc)
        lru_part(dya_ref, cols(z_ref, 0, 1), cols(z_ref, 1, 1), h_ref, h_prev_ref, xc_ref, r_ref, ig_ref, cw_ref, wr_ref,
                 wi_ref, lam_ref, cols(dz_ref, 0, 2), dcw_ref, dcb_ref, dwr_ref, dbr_ref, dwi_ref, dbi_ref, dlam_ref,
                 lam_carry, dxc_head)

    rev = lambda i: n_tiles - 1 - i
    tile = pl.BlockSpec((SEQ_TILE, D_MODEL), lambda i: (rev(i), 0))
    row = pl.BlockSpec((SEQ_TILE, D_IN), lambda i: (rev(i), 0))
    prev8 = pl.BlockSpec((SUBLANES, D_MODEL), lambda i: (jnp.maximum(rev(i) * per_tile - 1, 0), 0))
    vec = _const((1, D_MODEL))
    sq = _resident((D_MODEL, D_MODEL))
    gate_w = _resident((HEADS, HEAD_DIM, HEAD_DIM))
    gate_acc = _const((HEADS, HEAD_DIM, HEAD_DIM))
    vec_shape = jax.ShapeDtypeStruct((1, D_MODEL), F32)
    gate_shape = jax.ShapeDtypeStruct((HEADS, HEAD_DIM, HEAD_DIM), F32)
    act_bf = jax.ShapeDtypeStruct((t, D_MODEL), BF16)
    return _fused_call(
        body, jobs, name="bwd_mix", grid=(n_tiles,),
        in_specs=[tile, tile, tile, row, tile, prev8, tile, tile, tile, sq, sq, sq, vec, vec,
                  _const((GROUPS, CHUNK, CHUNK)), _const((CHUNK, D_MODEL)), _const((CONV_WIDTH, D_MODEL)), gate_w, gate_w,
                  vec],
        out_specs=[row, tile, tile, tile, tile, vec, vec, _const((CHUNK, GROUPS * CHUNK)), _const((CHUNK, 128)),
                   _const((SUBLANES, D_MODEL)), vec, gate_acc, vec, gate_acc, vec, vec],
        out_shape=[jax.ShapeDtypeStruct((t, D_IN), BF16), act_bf, act_bf, act_bf, act_bf, vec_shape, vec_shape,
                   jax.ShapeDtypeStruct((CHUNK, GROUPS * CHUNK), F32), jax.ShapeDtypeStruct((CHUNK, 128), F32),
                   jax.ShapeDtypeStruct((SUBLANES, D_MODEL), F32), vec_shape, gate_shape, vec_shape, gate_shape,
                   vec_shape, vec_shape],
        scratch_shapes=[pltpu.VMEM((SEQ_TILE, D_MODEL), F32), pltpu.VMEM((SEQ_TILE, D_MODEL), F32),
                        pltpu.VMEM((SEQ_TILE, D_MODEL), F32), pltpu.VMEM((CHUNK, D_MODEL), F32),
                        pltpu.VMEM((1, D_MODEL), F32), pltpu.VMEM((SUBLANES, D_MODEL), F32)],
        compiler_params=_params(),
    )(dh1, pa, pb, z, h, h, xc, r, ig, w_oa, w_ob, w_out, ln_g, ln_b, w_s, bias_full, conv_w, wr, wi, lam)


def _bwd_in(dz, x, dh1, w_in_st, g1, jobs=()):
    t = x.shape[0]

    def body(dz_ref, x_ref, dh1_ref, w_ref, g_ref, dx_ref, dg1_ref):
        @pl.when(pl.program_id(0) == 0)
        def _():
            dg1_ref[...] = jnp.zeros_like(dg1_ref)

        dn1 = jnp.zeros((MM_TILE, D_MODEL), F32)
        for k in range(N_CHIPS):
            dn1 = dn1 + _dot_nt(dz_ref[:, k * IN_SHARD:(k + 1) * IN_SHARD], w_ref[k])
        xhat, r1 = _rms(x_ref[...])
        dg1_ref[...] = dg1_ref[...] + _col_sum(dn1 * xhat)
        dx_ref[...] = dh1_ref[...] + _rms_bwd(dn1 * g_ref[...], xhat, r1)

    tile = pl.BlockSpec((MM_TILE, D_MODEL), lambda i: (i, 0))
    return _fused_call(
        body, jobs, name="bwd_in", grid=(t // MM_TILE,),
        in_specs=[pl.BlockSpec((MM_TILE, D_IN), lambda i: (i, 0)), tile, tile,
                  _resident((N_CHIPS, D_MODEL, IN_SHARD)), _const((1, D_MODEL))],
        out_specs=[tile, _const((1, D_MODEL))],
        out_shape=[jax.ShapeDtypeStruct((t, D_MODEL), F32), jax.ShapeDtypeStruct((1, D_MODEL), F32)],
        compiler_params=_params(),
    )(dz, x, dh1, w_in_st, g1)


def _weight_grad(name, a, b, n_blocks, a_varies, b_varies, width, jobs=()):
    t = a.shape[0]
    rows = min(DW_TILE, t)
    n_t = t // rows

    def body(a_ref, b_ref, o_ref, acc_ref):
        s = pl.program_id(1)
        part = _dot_tn(a_ref[...], b_ref[...])

        @pl.when(s == 0)
        def _():
            acc_ref[...] = part

        @pl.when(s > 0)
        def _():
            acc_ref[...] = acc_ref[...] + part

        @pl.when(s == n_t - 1)
        def _():
            o_ref[...] = acc_ref[...].astype(BF16)

    return _fused_call(
        body, jobs, name=name, grid=(n_blocks, n_t),
        in_specs=[pl.BlockSpec((rows, D_MODEL), (lambda j, s: (s, j)) if a_varies else (lambda j, s: (s, 0))),
                  pl.BlockSpec((rows, width), (lambda j, s: (s, j)) if b_varies else (lambda j, s: (s, 0)))],
        out_specs=pl.BlockSpec((None, D_MODEL, width), lambda j, s: (j, 0, 0)),
        out_shape=jax.ShapeDtypeStruct((n_blocks, D_MODEL, width), BF16),
        scratch_shapes=[pltpu.VMEM((D_MODEL, width), F32)],
        compiler_params=_params(2),
    )(a, b)


def _place():
    x, y, c = lax.axis_index("x"), lax.axis_index("y"), lax.axis_index("c")
    other_chips = [(1 - x, y), (x, 1 - y), (1 - x, 1 - y)]
    return x, y, c, other_chips


def _chip_index(px, py):
    return 2 * px + py


ANY = pl.BlockSpec(memory_space=pl.ANY)
SIBLING = ((0, 0, 1),)
NEIGHBOURS = ((1, 0, 0), (0, 1, 0))
OTHER_CHIPS = NEIGHBOURS + ((1, 1, 0),)


def _comm_call(name, jobs):
    return _fused_call(None, jobs, name=name, grid=(), in_specs=[], out_specs=[], out_shape=[])()[1]


def _near_far(x, y, c):
    return (x ^ (1 - c), y ^ c), (x ^ c, y ^ (1 - c))


def _gather_near_job(shards):
    n = len(shards)
    halves = [s.shape[0] // 2 for s in shards]

    def copies(ins, outs, send, recv, local):
        x, y, c, _ = _place()
        near, _ = _near_far(x, y, c)

        def block(w, chip, pc):
            return outs[w].at[_chip_index(*chip), pl.ds(pc * halves[w], halves[w]), :]

        def copy(w, k, chip, pc, to, src=None):
            return pltpu.make_async_remote_copy(
                src_ref=block(w, chip, pc) if src is None else src, dst_ref=block(w, chip, pc),
                send_sem=send.at[2 * w + k], recv_sem=recv.at[2 * w + k], device_id=to, device_id_type=MESH)

        sends, arrivals, own = [], [], []
        for w in range(n):
            src = ins[w].at[pl.ds(c * halves[w], halves[w]), :]
            own.append(pltpu.make_async_copy(src, block(w, (x, y), c), local.at[w]))
            sends += [copy(w, 0, (x, y), c, (*near, c), src), copy(w, 1, (x, y), c, (x, y, 1 - c), src)]
            arrivals += [copy(w, 0, near, c, (x, y, c)), copy(w, 1, (x, y), 1 - c, (x, y, c))]
        return sends, arrivals, own

    return _Job(shards, [jax.ShapeDtypeStruct((N_CHIPS,) + s.shape, s.dtype) for s in shards], 2 * n, copies,
                NEIGHBOURS + SIBLING, n_local=n)


def _gather_far_job(stacked):
    n = len(stacked)
    halves = [s.shape[1] // 2 for s in stacked]

    def copies(ins, outs, send, recv, local):
        del ins, local
        x, y, c, _ = _place()
        near, far = _near_far(x, y, c)

        def copy(w, k, chip):
            blk = outs[w].at[_chip_index(*chip), pl.ds(c * halves[w], halves[w]), :]
            return pltpu.make_async_remote_copy(
                src_ref=blk, dst_ref=blk, send_sem=send.at[2 * w + k], recv_sem=recv.at[2 * w + k],
                device_id=(*far, c), device_id_type=MESH)

        sends = [copy(w, k, chip) for w in range(n) for k, chip in enumerate(((x, y), near))]
        arrivals = [copy(w, k, chip) for w in range(n) for k, chip in enumerate((far, (1 - x, 1 - y)))]
        return sends, arrivals, []

    return _Job(stacked, [jax.ShapeDtypeStruct(s.shape, s.dtype) for s in stacked], 2 * n, copies, NEIGHBOURS,
                aliases={w: w for w in range(n)})


def _gather_pass_job(stacked):
    n = len(stacked)
    halves = [s.shape[1] // 2 for s in stacked]

    def copies(ins, outs, send, recv, local):
        del ins, local
        x, y, c, chips = _place()

        def copy(w, j, chip, pc, to):
            blk = outs[w].at[_chip_index(*chip), pl.ds(pc * halves[w], halves[w]), :]
            return pltpu.make_async_remote_copy(
                src_ref=blk, dst_ref=blk, send_sem=send.at[3 * w + j], recv_sem=recv.at[3 * w + j], device_id=to,
                device_id_type=MESH)

        sends = [copy(w, j, chip, c, (x, y, 1 - c)) for w in range(n) for j, chip in enumerate(chips)]
        arrivals = [copy(w, j, chip, 1 - c, (x, y, c)) for w in range(n) for j, chip in enumerate(chips)]
        return sends, arrivals, []

    return _Job(stacked, [jax.ShapeDtypeStruct(s.shape, s.dtype) for s in stacked], 3 * n, copies, SIBLING,
                aliases={w: w for w in range(n)})


def _gather_small_job(block):
    def copies(ins, outs, send, recv, local):
        x, y, c, chips = _place()

        def copy(j, chip_from, to):
            return pltpu.make_async_remote_copy(
                src_ref=ins[0], dst_ref=outs[0].at[_chip_index(*chip_from)], send_sem=send.at[j],
                recv_sem=recv.at[j], device_id=to, device_id_type=MESH)

        own = [pltpu.make_async_copy(ins[0], outs[0].at[_chip_index(x, y)], local.at[0])]
        sends = [copy(j, (x, y), (*chip, c)) for j, chip in enumerate(chips)]
        arrivals = [copy(j, chip, (x, y, c)) for j, chip in enumerate(chips)]
        return sends, arrivals, own

    return _Job([block], [jax.ShapeDtypeStruct((N_CHIPS,) + block.shape, block.dtype)], 3, copies, OTHER_CHIPS,
                n_local=1)


def _pair_send_job(grads):
    n = len(grads)
    halves = [g.shape[1] // 2 for g in grads]

    def copies(ins, outs, send, recv, local):
        del local
        x, y, c, _ = _place()
        sends = [pltpu.make_async_remote_copy(
            src_ref=ins[w].at[:, pl.ds((1 - c) * halves[w], halves[w]), :], dst_ref=outs[w], send_sem=send.at[w],
            recv_sem=recv.at[w], device_id=(x, y, 1 - c), device_id_type=MESH) for w in range(n)]
        return sends, sends, []

    return _Job(grads, [jax.ShapeDtypeStruct((N_CHIPS, h, g.shape[2]), g.dtype) for g, h in zip(grads, halves)], n,
                copies, SIBLING)


def _row_block(rows, limit=256):
    return min(rows, limit)


def _pair_add(name, core, mine, theirs):
    _, _, h, cols = mine.shape
    rb = _row_block(h, 512)

    def body(core_ref, a_ref, b_ref, o_ref):
        del core_ref
        o_ref[...] = (a_ref[...].astype(F32) + b_ref[...].astype(F32)).astype(BF16)

    return pl.pallas_call(
        body, name=name,
        grid_spec=pltpu.PrefetchScalarGridSpec(
            num_scalar_prefetch=1, grid=(N_CHIPS, h // rb),
            in_specs=[pl.BlockSpec((None, None, rb, cols), lambda k, r, core_ref: (k, core_ref[0], r, 0)),
                      pl.BlockSpec((None, rb, cols), lambda k, r, core_ref: (k, r, 0))],
            out_specs=pl.BlockSpec((None, rb, cols), lambda k, r, core_ref: (k, r, 0))),
        out_shape=jax.ShapeDtypeStruct(theirs.shape, BF16),
        compiler_params=_params(2),
    )(core, mine, theirs)


def _sequencer_call(name, collective_id, job, then=()):
    steps = [job]
    for make in then:
        steps.append(make(steps[-1].out_shape))
    peers = sorted(set(p for step in steps for p in step.peers))
    ins = [jax.new_ref(a, memory_space=pltpu.MemorySpace.HBM) for a in job.inputs]
    outs = [ins[{o: i for i, o in job.aliases.items()}[k]] if k in job.aliases.values()
            else jax.empty_ref(shape, memory_space=pltpu.MemorySpace.HBM) for k, shape in enumerate(job.out_shape)]
    sems = [pltpu.SemaphoreType.DMA((n,)) for step in steps for n in (step.n_sem, step.n_sem, max(step.n_local, 1))]

    @pl.kernel(mesh=plsc.ScalarSubcoreMesh(axis_name="sequencer", num_cores=1), name=name, scratch_types=tuple(sems),
               compiler_params=pltpu.CompilerParams(collective_id=collective_id))
    def launch(*sem_refs):
        x, y, c, _ = _place()
        barrier = pltpu.get_barrier_semaphore()
        for dx, dy, dc in peers:
            pl.semaphore_signal(barrier, inc=1, device_id=(x ^ dx, y ^ dy, c ^ dc), device_id_type=MESH)
        pl.semaphore_wait(barrier, len(peers))
        for k, step in enumerate(steps):
            sends, arrivals, own = step.copies(ins if k == 0 else outs, outs, *sem_refs[3 * k:3 * k + 3])
            for cp in own + sends:
                cp.start()
            for cp in arrivals:
                cp.wait_recv()
            for cp in sends:
                cp.wait_send()
            for cp in own:
                cp.wait()

    launch()
    return [ref[...] for ref in outs]


def _chip_exchange_job(sums):
    n = len(sums)

    def copies(ins, outs, send, recv, local):
        del local
        _, _, c, chips = _place()
        sends = [pltpu.make_async_remote_copy(
            src_ref=ins[w].at[_chip_index(*chip)], dst_ref=outs[w].at[j], send_sem=send.at[3 * w + j],
            recv_sem=recv.at[3 * w + j], device_id=(*chip, c), device_id_type=MESH)
            for w in range(n) for j, chip in enumerate(chips)]
        return sends, sends, []

    return _Job(sums, [jax.ShapeDtypeStruct((N_CHIPS - 1,) + s.shape[1:], s.dtype) for s in sums], 3 * n, copies,
                OTHER_CHIPS)


def _chip_sum(name, place, mine, theirs):
    _, h, cols = mine.shape
    rb = _row_block(h, 512)

    def body(place_ref, p_ref, q_ref, o_ref):
        del place_ref
        acc = p_ref[...].astype(F32)
        for j in range(N_CHIPS - 1):
            acc = acc + q_ref[j].astype(F32)
        o_ref[...] = acc

    return pl.pallas_call(
        body, name=name,
        grid_spec=pltpu.PrefetchScalarGridSpec(
            num_scalar_prefetch=1, grid=(h // rb,),
            in_specs=[pl.BlockSpec((None, rb, cols), lambda r, place_ref: (place_ref[0], r, 0)),
                      pl.BlockSpec((N_CHIPS - 1, rb, cols), lambda r, place_ref: (0, r, 0))],
            out_specs=pl.BlockSpec((None, rb, cols), lambda r, place_ref: (place_ref[1], r, 0))),
        out_shape=jax.ShapeDtypeStruct((2, h, cols), F32),
        compiler_params=_params(),
    )(place, mine, theirs)


def _share_job(bufs):
    n = len(bufs)

    def copies(ins, outs, send, recv, local):
        del ins, local
        x, y, c, _ = _place()

        def copy(w, half):
            return pltpu.make_async_remote_copy(
                src_ref=outs[w].at[half], dst_ref=outs[w].at[half], send_sem=send.at[w], recv_sem=recv.at[w],
                device_id=(x, y, 1 - c), device_id_type=MESH)

        return [copy(w, c) for w in range(n)], [copy(w, 1 - c) for w in range(n)], []

    return _Job(bufs, [jax.ShapeDtypeStruct(b.shape, b.dtype) for b in bufs], n, copies, SIBLING,
                aliases={w: w for w in range(n)})


SMALL_ROWS = 24
ROW_G1, ROW_CW, ROW_CB, ROW_BR, ROW_BI, ROW_LAM, ROW_LG, ROW_LB, ROW_G2, ROW_G3, ROW_LOSS, ROW_BS = (
    0, 1, 5, 6, 7, 8, 9, 10, 11, 12, 13, 16)
N_DEV = 8


def _pack_small(dcw, dcb, dbr, dbi, dlam, dlg, dlb, dg2, dg3, loss, dbs):
    def body(dcw_ref, dcb_ref, dbr_ref, dbi_ref, dlam_ref, dlg_ref, dlb_ref, dg2_ref, dg3_ref, loss_ref, dbs_ref, out):
        out[...] = jnp.zeros((SMALL_ROWS, D_MODEL), F32)
        for row, ref in ((ROW_CB, dcb_ref), (ROW_BR, dbr_ref), (ROW_BI, dbi_ref), (ROW_LAM, dlam_ref),
                         (ROW_LG, dlg_ref), (ROW_LB, dlb_ref), (ROW_G2, dg2_ref), (ROW_G3, dg3_ref)):
            out[row:row + 1, :] = ref[...]
        out[ROW_CW:ROW_CW + CONV_WIDTH, :] = dcw_ref[0:CONV_WIDTH, :]
        out[ROW_LOSS:ROW_LOSS + 1, 0:128] = loss_ref[0:1, :]
        out[ROW_BS:ROW_BS + GROUPS, 0:128] = jnp.transpose(dbs_ref[...])[0:GROUPS, :]

    vm = pl.BlockSpec(memory_space=pltpu.VMEM)
    return pl.pallas_call(
        body, name="pack_small", in_specs=[vm] * 11, out_specs=vm,
        out_shape=jax.ShapeDtypeStruct((SMALL_ROWS, D_MODEL), F32),
    )(dcw, dcb, dbr, dbi, dlam, dlg, dlb, dg2, dg3, loss, dbs)


def _gather_all_job(blocks):
    n = len(blocks)
    flips = [(dx, dy, dc) for dx in (0, 1) for dy in (0, 1) for dc in (0, 1)][1:]

    def copies(ins, outs, send, recv, local):
        x, y, c, _ = _place()
        me = 4 * x + 2 * y + c
        sends, arrivals, own = [], [], []
        for w in range(n):
            own.append(pltpu.make_async_copy(ins[w], outs[w].at[me], local.at[w]))
            for k, (dx, dy, dc) in enumerate(flips):
                peer = (x ^ dx, y ^ dy, c ^ dc)
                sem = dict(send_sem=send.at[7 * w + k], recv_sem=recv.at[7 * w + k])
                sends.append(pltpu.make_async_remote_copy(
                    src_ref=ins[w], dst_ref=outs[w].at[me], device_id=peer, device_id_type=MESH, **sem))
                arrivals.append(pltpu.make_async_remote_copy(
                    src_ref=ins[w], dst_ref=outs[w].at[4 * peer[0] + 2 * peer[1] + peer[2]], device_id=peer,
                    device_id_type=MESH, **sem))
        return sends, arrivals, own

    return _Job(blocks, [jax.ShapeDtypeStruct((N_DEV,) + b.shape, b.dtype) for b in blocks], 7 * n, copies,
                OTHER_CHIPS + SIBLING + tuple((dx, dy, 1) for dx, dy, _ in OTHER_CHIPS), n_local=n)


def _sum_small(vec_all, ws_all, dg1_all):
    def body(vec_ref, ws_ref, dg1_ref, vec_out, ws_out):
        vec, ws, dg1 = vec_ref[0], ws_ref[0], dg1_ref[0]
        for d in range(1, N_DEV):
            vec, ws, dg1 = vec + vec_ref[d], ws + ws_ref[d], dg1 + dg1_ref[d]
        vec_out[...] = vec
        vec_out[ROW_G1:ROW_G1 + 1, :] = dg1
        ws_out[...] = ws

    vm = pl.BlockSpec(memory_space=pltpu.VMEM)
    return pl.pallas_call(
        body, name="sum_small", in_specs=[vm] * 3, out_specs=[vm, vm],
        out_shape=[jax.ShapeDtypeStruct(vec_all.shape[1:], F32), jax.ShapeDtypeStruct(ws_all.shape[1:], F32)],
    )(vec_all, ws_all, dg1_all)


def _adamw_math(w, g, m, v):
    m = ADAM_B1 * m + (1.0 - ADAM_B1) * g
    v = ADAM_B2 * v + (1.0 - ADAM_B2) * (g * g)
    m_hat = m / (1.0 - ADAM_B1 ** ADAM_STEP)
    v_hat = v / (1.0 - ADAM_B2 ** ADAM_STEP)
    delta = (-ADAM_LR) * (m_hat / (jnp.sqrt(v_hat) + ADAM_EPS) + ADAM_WD * w)
    return delta, m, v


def _adamw(name, g, w, m, v, jobs=()):
    rows, cols = w.shape
    rb = _row_block(rows)

    def body(g_ref, w_ref, m_ref, v_ref, d_ref, nm_ref, nv_ref):
        d_ref[...], nm_ref[...], nv_ref[...] = _adamw_math(w_ref[...], g_ref[...], m_ref[...], v_ref[...])

    blk = pl.BlockSpec((rb, cols), lambda r: (r, 0))
    return _fused_call(
        body, jobs, name=name, grid=(rows // rb,), in_specs=[blk] * 4, out_specs=[blk] * 3,
        out_shape=[jax.ShapeDtypeStruct(w.shape, F32)] * 3, compiler_params=_params(),
    )(g, w, m, v)


def _adamw_small(grads, ws, ms, vs):
    n = len(grads)

    def body(*refs):
        g_refs, w_refs, m_refs, v_refs = refs[:n], refs[n:2 * n], refs[2 * n:3 * n], refs[3 * n:4 * n]
        outs = refs[4 * n:]
        for p in range(n):
            d, nm, nv = _adamw_math(w_refs[p][...], g_refs[p][...], m_refs[p][...], v_refs[p][...])
            outs[p][...] = d
            outs[n + p][...] = nm
            outs[2 * n + p][...] = nv

    vm = pl.BlockSpec(memory_space=pltpu.VMEM)
    shapes = [jax.ShapeDtypeStruct(w.shape, F32) for w in ws]
    out = pl.pallas_call(
        body, name="adamw_small", in_specs=[vm] * (4 * n), out_specs=[vm] * (3 * n), out_shape=shapes * 3,
    )(*grads, *ws, *ms, *vs)
    return out[:n], out[n:2 * n], out[2 * n:]


def _unstack_heads(w_st):
    per = HEAD_DIM // N_CHIPS
    return w_st.reshape(N_CHIPS, HEADS, per, HEAD_DIM).transpose(1, 0, 2, 3).reshape(HEADS, HEAD_DIM, HEAD_DIM)


def _stack_heads(w):
    per = HEAD_DIM // N_CHIPS
    return w.reshape(HEADS, N_CHIPS, per, HEAD_DIM).transpose(1, 0, 2, 3).reshape(N_CHIPS, HEADS * per, HEAD_DIM)


def kernel(x, norm_mix_g, w_in, conv_w, conv_b, w_rgate, b_rgate, w_igate, b_igate, lru_lambda, w_out_a, sgu_ln_g, sgu_ln_b, sgu_w_s, sgu_b_s, w_out_b, w_out, norm_mlp_g, w_up, w_down, norm_final_g, loss_target, m_norm_mix_g, m_w_in, m_conv_w, m_conv_b, m_w_rgate, m_b_rgate, m_w_igate, m_b_igate, m_lru_lambda, m_w_out_a, m_sgu_ln_g, m_sgu_ln_b, m_sgu_w_s, m_sgu_b_s, m_w_out_b, m_w_out, m_norm_mlp_g, m_w_up, m_w_down, m_norm_final_g, v_norm_mix_g, v_w_in, v_conv_w, v_conv_b, v_w_rgate, v_b_rgate, v_w_igate, v_b_igate, v_lru_lambda, v_w_out_a, v_sgu_ln_g, v_sgu_ln_b, v_sgu_w_s, v_sgu_b_s, v_w_out_b, v_w_out, v_norm_mlp_g, v_w_up, v_w_down, v_norm_final_g):
    chip = _chip_index(lax.axis_index("x"), lax.axis_index("y"))
    core = lax.axis_index("c")
    quarter_h = HEAD_DIM // N_CHIPS
    quarter_d = D_MODEL // N_CHIPS

    as_2d = lambda a: a.reshape(-1, a.shape[-1])
    big_w = [as_2d(w) for w in (w_in, w_rgate, w_igate, w_out_a, w_out_b, w_out, w_up, w_down)]
    big_m = [as_2d(w) for w in (m_w_in, m_w_rgate, m_w_igate, m_w_out_a, m_w_out_b, m_w_out, m_w_up, m_w_down)]
    big_v = [as_2d(w) for w in (v_w_in, v_w_rgate, v_w_igate, v_w_out_a, v_w_out_b, v_w_out, v_w_up, v_w_down)]

    packed = jnp.concatenate([conv_w[0], b_rgate[0], b_igate[0]], axis=1)
    packed = jnp.concatenate([packed, jnp.zeros_like(packed)], axis=0)
    s_in, s_r, s_i, s_oa, s_ob, s_out, s_up, s_down = [w.astype(BF16) for w in big_w]
    xs, target = x[0], loss_target[0]
    g3 = norm_final_g.reshape(1, D_MODEL)
    bias_s = jnp.broadcast_to(jnp.transpose(sgu_b_s[0])[:, :, None], (CHUNK, GROUPS, GROUP_DIM)).reshape(CHUNK, D_MODEL)
    core_arr = core.reshape(1).astype(jnp.int32)
    place = jnp.stack([chip, core]).astype(jnp.int32)
    quarter = lambda g: g.reshape(N_CHIPS, D_MODEL // N_CHIPS, D_MODEL)

    def pair_add(nm, g, from_sibling):
        return _pair_add("pair_add_" + nm, core_arr, g.reshape(N_CHIPS, 2, g.shape[1] // 2, g.shape[2]), from_sibling)

    def chip_sum(nm, pair, from_chips):
        return _chip_sum("chip_sum_" + nm, place, pair, from_chips)

    order = jnp.stack([chip, chip ^ 2, chip ^ 1, chip ^ 3]).astype(jnp.int32)
    (z, n1, (w_in_st, wr_st, wi_st)), ((packed_all,), late) = _fwd_in(
        xs, norm_mix_g, [s_in, s_r, s_i], order,
        jobs=[_gather_small_job(packed), _gather_near_job([s_oa, s_ob, s_out])])
    pick = lambda lo, hi: packed_all[:, :HEADS, lo:hi].transpose(1, 0, 2).reshape(HEADS, -1)
    conv_w_full = pick(0, quarter_d)
    br_full = pick(quarter_d, quarter_d + quarter_h).reshape(1, D_MODEL)
    bi_full = pick(quarter_d + quarter_h, quarter_d + 2 * quarter_h).reshape(1, D_MODEL)
    wr, wi = _unstack_heads(wr_st), _unstack_heads(wi_st)
    lru = (conv_w_full, conv_b, wr, br_full, wi, bi_full, lru_lambda)
    sgu = (sgu_ln_g, sgu_ln_b, sgu_w_s[0], bias_s)

    after = lambda arrays, result: lax.optimization_barrier((arrays, result))[0]
    w_up_st, w_dn = _sequencer_call("gather_mlp", 8, _gather_near_job(after([s_up, s_down], n1)),
                                    then=(_gather_far_job, _gather_pass_job))
    w_dn = w_dn.reshape(D_FF, D_MODEL)
    (ya, *saved), (late,) = _fwd_lru(z, *lru, jobs=[_gather_far_job(late)])
    yb, (late,) = _fwd_sgu(z, *sgu, jobs=[_gather_pass_job(late)])
    w_oa, w_ob, w_o = [w.reshape(D_MODEL, D_MODEL) for w in late]
    (pa, pb, h1, n2), _ = _fwd_merge(ya, yb, z, xs, w_oa, w_ob, w_o, norm_mlp_g)
    (act, dup, dh2b, dh1, loss_part, dg3, dg2), _ = _mlp(n2, h1, target, w_up_st, w_dn, norm_mlp_g, g3)

    d_down, _ = _weight_grad("dw_down", act, dh2b, N_CHIPS, True, False, D_MODEL)
    r_down, = _sequencer_call("send_w_down", 10, _pair_send_job([d_down]))
    d_up, _ = _weight_grad("dw_up", n2, dup, N_CHIPS, False, True, D_MODEL)
    r_up, = _sequencer_call("send_w_up", 11, _pair_send_job([d_up]))
    p_down, p_up = pair_add("w_down", d_down, r_down), pair_add("w_up", d_up, r_up)
    (dz, merged, dpa, dpb, dh1b, dlg, dlb, dws, dbs, dcw, dcb, dwr, dbr, dwi, dbi, dlam), ((q_up, q_down),) = _bwd_mix(
        dh1, pa, pb, z, *saved, w_oa, w_ob, w_o, *sgu, conv_w_full, wr, wi, lru_lambda,
        jobs=[_chip_exchange_job([p_up, p_down])])
    half_up, half_down = chip_sum("w_up", p_up, q_up), chip_sum("w_down", p_down, q_down)
    names = ("w_in", "w_rgate", "w_igate", "w_out_a", "w_out_b", "w_out", "w_up", "w_down")
    d_out, ((full_up, full_down),) = _weight_grad(
        "dw_out", merged, dh1b, 1, False, False, D_MODEL, jobs=[_share_job([half_up, half_down])])
    d_oa, _ = _weight_grad("dw_out_a", ya, dpa, 1, False, False, D_MODEL)
    d_ob, _ = _weight_grad("dw_out_b", yb, dpb, 1, False, False, D_MODEL)
    mids = [quarter(d_oa), quarter(d_ob), quarter(d_out)]
    r_mids = _sequencer_call("send_mids", 1, _pair_send_job(mids))
    gates = [_stack_heads(dwr).astype(BF16), _stack_heads(dwi).astype(BF16)]
    small = _pack_small(dcw, dcb, dbr, dbi, dlam, dlg, dlb, dg2, dg3, loss_part, dbs)
    p_mids = [pair_add(nm, g, r) for nm, g, r in zip(names[3:6], mids, r_mids)]
    q_mids = _sequencer_call("exchange_mids", 2, _chip_exchange_job(p_mids))
    d_in, (r_gates, (vec_all, ws_all)) = _weight_grad(
        "dw_in", n1, dz, N_CHIPS, False, True, IN_SHARD,
        jobs=[_pair_send_job(gates), _gather_all_job([small, dws])])
    r_in, = _sequencer_call("send_w_in", 3, _pair_send_job([d_in]))
    adam_args = {nm: (w, m, v) for nm, w, m, v in zip(names, big_w, big_m, big_v)}

    def adamw(nm, g):
        w, m, v = adam_args[nm]
        g = g.reshape(w.shape)
        return g, _adamw("adamw_" + nm, g, w, m, v)[0]

    p_gates = [pair_add(nm, g, r) for nm, g, r in zip(names[1:3], gates, r_gates)]
    half_mids = [chip_sum(nm, p, q) for nm, p, q in zip(names[3:6], p_mids, q_mids)]
    full_mids = _sequencer_call("share_mids", 12, _share_job(half_mids))
    p_first = [pair_add("w_in", d_in, r_in)] + p_gates
    q_first = _sequencer_call("exchange_w_in", 4, _chip_exchange_job(p_first))
    (grad_x, dg1), _ = _bwd_in(dz, xs, dh1, w_in_st, norm_mix_g)
    dg1_all, = _sequencer_call("gather_dg1", 6, _gather_all_job([dg1]))
    done = {nm: adamw(nm, f) for nm, f in zip(("w_up", "w_down") + names[3:6], [full_up, full_down] + full_mids)}
    q_first = after(q_first, [out[0] for _, out in done.values()])
    half_first = [chip_sum(nm, p, q) for nm, p, q in zip(names[:3], p_first, q_first)]
    full_first = _sequencer_call("share_last", 5, _share_job(half_first))
    done.update({nm: adamw(nm, f) for nm, f in zip(names[:3], full_first)})
    full, big_out = [done[nm][0] for nm in names], [done[nm][1] for nm in names]

    vec, ws_sum = _sum_small(vec_all, ws_all, dg1_all)
    row = lambda r: vec[r:r + 1]
    shard = lambda a, width: lax.dynamic_slice_in_dim(a, chip * width, width, axis=1)
    g_small = dict(
        norm_mix_g=row(ROW_G1), conv_w=shard(vec[ROW_CW:ROW_CW + CONV_WIDTH], quarter_d), conv_b=row(ROW_CB),
        b_rgate=shard(row(ROW_BR).reshape(HEADS, HEAD_DIM), quarter_h),
        b_igate=shard(row(ROW_BI).reshape(HEADS, HEAD_DIM), quarter_h), lru_lambda=row(ROW_LAM),
        sgu_ln_g=row(ROW_LG), sgu_ln_b=row(ROW_LB),
        sgu_w_s=ws_sum.reshape(CHUNK, GROUPS, CHUNK).transpose(1, 0, 2).reshape(GROUPS * CHUNK, CHUNK),
        sgu_b_s=vec[ROW_BS:ROW_BS + GROUPS, 0:CHUNK], norm_mlp_g=row(ROW_G2), norm_final_g=row(ROW_G3))
    loss = vec[ROW_LOSS, 0]
    small_names = list(g_small)
    given = dict(
        norm_mix_g=(norm_mix_g, m_norm_mix_g, v_norm_mix_g), conv_w=(conv_w, m_conv_w, v_conv_w),
        conv_b=(conv_b, m_conv_b, v_conv_b), b_rgate=(b_rgate, m_b_rgate, v_b_rgate),
        b_igate=(b_igate, m_b_igate, v_b_igate), lru_lambda=(lru_lambda, m_lru_lambda, v_lru_lambda),
        sgu_ln_g=(sgu_ln_g, m_sgu_ln_g, v_sgu_ln_g), sgu_ln_b=(sgu_ln_b, m_sgu_ln_b, v_sgu_ln_b),
        sgu_w_s=(sgu_w_s, m_sgu_w_s, v_sgu_w_s), sgu_b_s=(sgu_b_s, m_sgu_b_s, v_sgu_b_s),
        norm_mlp_g=(norm_mlp_g, m_norm_mlp_g, v_norm_mlp_g), norm_final_g=(norm_final_g, m_norm_final_g, v_norm_final_g))
    g2d = [g_small[nm] for nm in small_names]
    to2d = lambda a, g: a.reshape(g.shape)
    d_s, m_s, v_s = _adamw_small(
        g2d, *[[to2d(given[nm][q], g) for nm, g in zip(small_names, g2d)] for q in range(3)])

    shapes = dict(
        norm_mix_g=norm_mix_g, w_in=w_in, conv_w=conv_w, conv_b=conv_b, w_rgate=w_rgate, b_rgate=b_rgate,
        w_igate=w_igate, b_igate=b_igate, lru_lambda=lru_lambda, w_out_a=w_out_a, sgu_ln_g=sgu_ln_g,
        sgu_ln_b=sgu_ln_b, sgu_w_s=sgu_w_s, sgu_b_s=sgu_b_s, w_out_b=w_out_b, w_out=w_out, norm_mlp_g=norm_mlp_g,
        w_up=w_up, w_down=w_down, norm_final_g=norm_final_g)
    grads, deltas, new_m, new_v = {}, {}, {}, {}
    for nm, g, (d, nmom, nvar) in zip(names, full, big_out):
        grads[nm], deltas[nm], new_m[nm], new_v[nm] = g, d, nmom, nvar
    for p, nm in enumerate(small_names):
        grads[nm], deltas[nm], new_m[nm], new_v[nm] = g2d[p], d_s[p], m_s[p], v_s[p]
    order = list(shapes)
    out = [loss, grad_x[None]]
    for group in (grads, deltas, new_m, new_v):
        out += [group[nm].reshape(shapes[nm].shape) for nm in order]
    return tuple(out)
```

```python
import functools

import jax
import jax.numpy as jnp
from jax import lax
from jax.experimental import pallas as pl
from jax.experimental.pallas import tpu as pltpu
from jax.experimental.pallas import tpu_sc as plsc

F32 = jnp.float32
BF16 = jnp.bfloat16
MESH = pl.DeviceIdType.MESH

D_MODEL = 1024
D_IN = 6 * D_MODEL
D_FF = 4 * D_MODEL
N_CHIPS = 4
IN_SHARD = D_IN // N_CHIPS
HEADS = 4
HEAD_DIM = D_MODEL // HEADS
GROUPS = 4
GROUP_DIM = D_MODEL // GROUPS
CHUNK = 128
CONV_WIDTH = 4
LRU_C = 8.0
NORM_EPS = 1e-6
LN_EPS = 1e-5

ADAM_LR = 0.001
ADAM_B1 = 0.9
ADAM_B2 = 0.999
ADAM_EPS = 1e-08
ADAM_WD = 0.01
ADAM_STEP = 10

SUBLANES = 8
MM_TILE = 512
IN_TILE = 1024
SEQ_TILE = 256
DW_TILE = 2048
VMEM_LIMIT_BYTES = 56 * 1024 * 1024

GELU_K0 = 0.7978845608028654
GELU_K1 = 0.044715


def _params(n_grid_axes=1):
    return pltpu.CompilerParams(
        dimension_semantics=("arbitrary",) * n_grid_axes, vmem_limit_bytes=VMEM_LIMIT_BYTES)


def _resident(shape):
    nd = len(shape)
    return pl.BlockSpec(shape, lambda *_: (0,) * nd, pipeline_mode=pl.Buffered(1))


def _const(shape):
    nd = len(shape)
    return pl.BlockSpec(shape, lambda *_: (0,) * nd)


def _dot(a, b):
    return jnp.dot(a, b, preferred_element_type=F32)


def _dot_nt(a, b):
    return lax.dot_general(a, b, (((1,), (1,)), ((), ())), preferred_element_type=F32)


def _dot_tn(a, b):
    return lax.dot_general(a, b, (((0,), (0,)), ((), ())), preferred_element_type=F32)


def _gelu(x):
    t = jnp.tanh(GELU_K0 * x * (1.0 + GELU_K1 * x * x))
    return 0.5 * x * (1.0 + t)


def _gelu_and_grad(x):
    x2 = x * x
    t = jnp.tanh(GELU_K0 * x * (1.0 + GELU_K1 * x2))
    g = 0.5 * x * (1.0 + t)
    dg = 0.5 * (1.0 + t) + 0.5 * x * (1.0 - t * t) * (GELU_K0 * (1.0 + 3.0 * GELU_K1 * x2))
    return g, dg


def _rms(x):
    r = lax.rsqrt(jnp.mean(x * x, axis=-1, keepdims=True) + NORM_EPS)
    return x * r, r


def _rms_bwd(dn, xhat, r):
    return r * (dn - xhat * jnp.mean(dn * xhat, axis=-1, keepdims=True))


def _col_sum(v):
    return jnp.sum(v, axis=0, keepdims=True)


def _shift_down(x, tail8, k):
    xs = pltpu.roll(x, k, 0)
    ts = pltpu.roll(tail8, k, 0)
    ridx = lax.broadcasted_iota(jnp.int32, tail8.shape, 0)
    head = jnp.where(ridx < k, ts, xs[0:SUBLANES])
    return jnp.concatenate([head, xs[SUBLANES:]], axis=0)


def _shift_up(x, head8, k):
    n = x.shape[0]
    xs = pltpu.roll(x, n - k, 0)
    hs = pltpu.roll(head8, SUBLANES - k, 0)
    ridx = lax.broadcasted_iota(jnp.int32, head8.shape, 0)
    last = jnp.where(ridx >= SUBLANES - k, hs, xs[n - SUBLANES:n])
    return jnp.concatenate([xs[:n - SUBLANES], last], axis=0)


def _scan_forward(a, b, carry):
    n, cols = a.shape
    groups = n // SUBLANES
    a = a.reshape(groups, SUBLANES, cols)
    b = b.reshape(groups, SUBLANES, cols)
    sub = lax.broadcasted_iota(jnp.int32, a.shape, 1)
    for s in (1, 2, 4):
        a_s = pltpu.roll(a, s, 1)
        b_s = pltpu.roll(b, s, 1)
        m = sub >= s
        b = jnp.where(m, a * b_s + b, b)
        a = jnp.where(m, a * a_s, a)
    out = []
    for g in range(groups):
        h = a[g] * carry + b[g]
        out.append(h)
        carry = h[SUBLANES - 1:SUBLANES]
    return jnp.concatenate(out, axis=0), carry


def _scan_backward(a, b, carry):
    n, cols = a.shape
    groups = n // SUBLANES
    a = a.reshape(groups, SUBLANES, cols)
    b = b.reshape(groups, SUBLANES, cols)
    sub = lax.broadcasted_iota(jnp.int32, a.shape, 1)
    for s in (1, 2, 4):
        a_s = pltpu.roll(a, SUBLANES - s, 1)
        b_s = pltpu.roll(b, SUBLANES - s, 1)
        m = sub < SUBLANES - s
        b = jnp.where(m, a * b_s + b, b)
        a = jnp.where(m, a * a_s, a)
    out = [None] * groups
    for g in reversed(range(groups)):
        h = a[g] * carry + b[g]
        out[g] = h
        carry = h[0:1]
    return jnp.concatenate(out, axis=0), carry


def _softplus_neg(lam):
    e = jnp.exp(-jnp.abs(lam))
    u = 1.0 + e
    log1p_e = jnp.where(u == 1.0, e, jnp.log(u) * (e / jnp.where(u == 1.0, 1.0, u - 1.0)))
    return jnp.maximum(-lam, 0.0) + log1p_e


def _lru_gates(xa, tail8, cw_ref, cb_ref, wr_ref, br_ref, wi_ref, bi_ref, lam_ref):
    cw = cw_ref[...]
    xc = cb_ref[...] + cw[0:1] * xa
    for k in range(1, CONV_WIDTH):
        xc = xc + cw[k:k + 1] * _shift_down(xa, tail8, k)
    xcb = xc.astype(BF16)
    pre_r, pre_i = [], []
    for h in range(HEADS):
        cols = slice(h * HEAD_DIM, (h + 1) * HEAD_DIM)
        pre_r.append(_dot(xcb[:, cols], wr_ref[h]))
        pre_i.append(_dot(xcb[:, cols], wi_ref[h]))
    r = jax.nn.sigmoid(jnp.concatenate(pre_r, axis=1) + br_ref[...])
    ig = jax.nn.sigmoid(jnp.concatenate(pre_i, axis=1) + bi_ref[...])
    _, a, mult = _decay(r, lam_ref)
    return xc, r, ig, a, mult


def _decay(r, lam_ref):
    sp = _softplus_neg(lam_ref[...])
    log_a = ((-LRU_C) * sp) * r
    a = jnp.exp(log_a)
    th = jnp.tanh(log_a)
    return sp, a, jnp.sqrt((-2.0 * th) / (1.0 - th))


class _Job:
    def __init__(self, inputs, out_shape, n_sem, copies, peers, aliases=None, n_local=0):
        self.inputs, self.out_shape, self.n_sem, self.copies = list(inputs), list(out_shape), n_sem, copies
        self.aliases, self.n_local = dict(aliases or {}), n_local
        self.peers = tuple(peers)


def _fused_call(body, jobs, *, name, grid, in_specs, out_specs, out_shape, scratch_shapes=(),
                input_output_aliases=None, compiler_params=None, n_prefetch=0, jobs_start_after=None):
    single = not isinstance(out_shape, (list, tuple))
    out_specs = [out_specs] if single else list(out_specs)
    out_shape = [out_shape] if single else list(out_shape)
    n_scr = len(scratch_shapes)
    in_specs, scratch_shapes = list(in_specs), list(scratch_shapes)
    n_in, n_out = len(in_specs), len(out_shape)
    aliases = dict(input_output_aliases or {})
    in_at, out_at = [], []
    for job in jobs:
        in_at.append(len(in_specs))
        out_at.append(len(out_shape))
        for i, o in job.aliases.items():
            aliases[n_prefetch + len(in_specs) + i] = len(out_shape) + o
        in_specs += [ANY] * len(job.inputs)
        out_specs += [ANY] * len(job.out_shape)
        out_shape += job.out_shape
        scratch_shapes += [pltpu.SemaphoreType.DMA((job.n_sem,)), pltpu.SemaphoreType.DMA((job.n_sem,)),
                           pltpu.SemaphoreType.DMA((max(job.n_local, 1),))]
    n_in_all, n_out_all = len(in_specs), len(out_shape)

    def full_body(*refs):
        prefetch, refs = refs[:n_prefetch], refs[n_prefetch:]
        ins, outs, scr = refs[:n_in_all], refs[n_in_all:n_in_all + n_out_all], refs[n_in_all + n_out_all:]

        def copies(q):
            job = jobs[q]
            return job.copies(ins[in_at[q]:in_at[q] + len(job.inputs)], outs[out_at[q]:out_at[q] + len(job.out_shape)],
                              *scr[n_scr + 3 * q:n_scr + 3 * q + 3])

        def start():
            for q in range(len(jobs)):
                sends, _, local = copies(q)
                for cp in local + sends:
                    cp.start()

        def finish():
            every = [copies(q) for q in range(len(jobs))]
            for _, arrivals, _ in every:
                for cp in arrivals:
                    cp.wait_recv()
            for sends, _, local in every:
                for cp in sends:
                    cp.wait_send()
                for cp in local:
                    cp.wait()

        if not grid:
            start()
            finish()
            return
        ids = [pl.program_id(a) for a in range(len(grid))]
        at_step = lambda step: functools.reduce(jnp.logical_and, [i == k for i, k in zip(ids, step)])
        if jobs and jobs_start_after is None:
            pl.when(at_step((0,) * len(grid)))(start)
        body(*prefetch, *ins[:n_in], *outs[:n_out], *scr[:n_scr])
        if jobs and jobs_start_after is not None:
            pl.when(at_step(jobs_start_after))(start)
        if jobs:
            pl.when(functools.reduce(jnp.logical_and, [i == g - 1 for i, g in zip(ids, grid)]))(finish)

    if n_prefetch:
        layout = dict(grid_spec=pltpu.PrefetchScalarGridSpec(
            num_scalar_prefetch=n_prefetch, grid=grid, in_specs=in_specs, out_specs=out_specs,
            scratch_shapes=scratch_shapes))
    else:
        layout = dict(grid=grid, in_specs=in_specs, out_specs=out_specs, scratch_shapes=scratch_shapes)
    call = pl.pallas_call(
        full_body, name=name, out_shape=out_shape, input_output_aliases=aliases, compiler_params=compiler_params,
        **layout)

    def run(*args):
        res = call(*args, *[a for job in jobs for a in job.inputs])
        mine = res[0] if single else list(res[:n_out])
        return mine, [list(res[at:at + len(job.out_shape)]) for at, job in zip(out_at, jobs)]

    return run


def _fwd_in(x, g1, shards, order, jobs=()):
    t = x.shape[0]
    rows_per_step = min(IN_TILE, t)
    n_tiles = t // rows_per_step
    n = len(shards)
    halves = [s.shape[0] // 2 for s in shards]

    def body(order_ref, x_ref, g_ref, *refs):
        del order_ref
        ins, (z_ref, n_ref), outs = refs[:n], refs[n:n + 2], refs[n + 2:2 * n + 2]
        wbuf, nbuf, send, recv, local = refs[2 * n + 2:]
        s, i = pl.program_id(0), pl.program_id(1)
        x_, y_, c, chips = _place()
        k_me = _chip_index(x_, y_)

        def block(w, chip, pc):
            return outs[w].at[_chip_index(*chip), pl.ds(pc * halves[w], halves[w]), :]

        def over_ici(w, j, landing):
            return pltpu.make_async_remote_copy(
                src_ref=ins[w].at[pl.ds(c * halves[w], halves[w]), :],
                dst_ref=block(w, chips[j] if landing else (x_, y_), c), send_sem=send.at[6 * w + j],
                recv_sem=recv.at[6 * w + j], device_id=(*chips[j], c), device_id_type=MESH)

        def to_sibling(w, j, landing):
            blk = block(w, chips[j], 1 - c if landing else c)
            return pltpu.make_async_remote_copy(
                src_ref=blk, dst_ref=blk, send_sem=send.at[6 * w + 3 + j], recv_sem=recv.at[6 * w + 3 + j],
                device_id=(x_, y_, 1 - c), device_id_type=MESH)

        own = [pltpu.make_async_copy(wbuf, outs[0].at[k_me], local.at[0])]
        own += [pltpu.make_async_copy(ins[w], outs[w].at[k_me], local.at[w]) for w in range(1, n)]

        @pl.when((s == 0) & (i == 0))
        def _():
            for j in range(2):
                for w in range(n):
                    over_ici(w, j, False).start()
            load = pltpu.make_async_copy(ins[0], wbuf, local.at[n])
            load.start()
            load.wait()
            for cp in own:
                cp.start()

        for j in range(N_CHIPS - 1):
            @pl.when((s == j + 1) & (i == 0))
            def _(j=j):
                for w in range(n):
                    over_ici(w, j, True).wait_recv()
                for w in range(n):
                    to_sibling(w, j, False).start()
                if j == 0:
                    for w in range(n):
                        over_ici(w, 2, False).start()
                    own[0].wait()
                for w in range(n):
                    to_sibling(w, j, True).wait_recv()
                load = pltpu.make_async_copy(outs[0].at[_chip_index(*chips[j])], wbuf, local.at[n])
                load.start()
                load.wait()

        rows = pl.ds(pl.multiple_of(i * rows_per_step, rows_per_step), rows_per_step)

        @pl.when(s == 0)
        def _():
            xhat, _ = _rms(x_ref[...])
            nrm = (xhat * g_ref[...]).astype(BF16)
            nbuf[rows, :] = nrm
            n_ref[...] = nrm

        z_ref[...] = _dot(nbuf[rows, :], wbuf[...])

        @pl.when((s == N_CHIPS - 1) & (i == n_tiles - 1))
        def _():
            for j in range(N_CHIPS - 1):
                for w in range(n):
                    over_ici(w, j, False).wait_send()
                    to_sibling(w, j, False).wait_send()
            for cp in own[1:]:
                cp.wait()

    once = lambda s, i, order: (jnp.where(s == 0, i, n_tiles - 1), 0)
    (z, n1, *stacked), job_outs = _fused_call(
        body, jobs, name="fwd_in", grid=(N_CHIPS, n_tiles), n_prefetch=1,
        in_specs=[pl.BlockSpec((rows_per_step, D_MODEL), once), _const((1, D_MODEL))] + [ANY] * n,
        out_specs=[pl.BlockSpec((rows_per_step, IN_SHARD), lambda s, i, order: (i, order[s])),
                   pl.BlockSpec((rows_per_step, D_MODEL), once)] + [ANY] * n,
        out_shape=[jax.ShapeDtypeStruct((t, D_IN), F32), jax.ShapeDtypeStruct((t, D_MODEL), BF16)]
        + [jax.ShapeDtypeStruct((N_CHIPS,) + s.shape, s.dtype) for s in shards],
        scratch_shapes=[pltpu.VMEM(shards[0].shape, BF16), pltpu.VMEM((t, D_MODEL), BF16),
                        pltpu.SemaphoreType.DMA((6 * n,)),
                        pltpu.SemaphoreType.DMA((6 * n,)), pltpu.SemaphoreType.DMA((n + 1,))],
        compiler_params=_params(2), jobs_start_after=(1, 0),
    )(order, x, g1, *shards)
    return (z, n1, stacked), job_outs


def _fwd_lru(z, conv_w, conv_b, wr, br, wi, bi, lam, jobs=()):
    t = z.shape[0]

    def body(xa_ref, ga_ref, cw_ref, cb_ref, wr_ref, br_ref, wi_ref, bi_ref, lam_ref, ya_ref, h_ref, xc_ref, r_ref,
             ig_ref, tail_ref, carry_ref):
        @pl.when(pl.program_id(0) == 0)
        def _():
            tail_ref[...] = jnp.zeros_like(tail_ref)
            carry_ref[...] = jnp.zeros_like(carry_ref)

        xa = xa_ref[...]
        xc, r, ig, a, mult = _lru_gates(xa, tail_ref[...], cw_ref, cb_ref, wr_ref, br_ref, wi_ref, bi_ref, lam_ref)
        tail_ref[...] = xa[SEQ_TILE - SUBLANES:]
        xc_ref[...], r_ref[...], ig_ref[...] = xc, r, ig
        h, carry = _scan_forward(a, xc * ig * mult, carry_ref[...])
        carry_ref[...] = carry
        h_ref[...] = h
        ya_ref[...] = (h * _gelu(ga_ref[...])).astype(BF16)

    tile = lambda j: pl.BlockSpec((SEQ_TILE, D_MODEL), lambda i: (i, j))
    return _fused_call(
        body, jobs, name="fwd_lru", grid=(t // SEQ_TILE,),
        in_specs=[tile(0), tile(1), _const((CONV_WIDTH, D_MODEL)), _const((1, D_MODEL)),
                  _resident((HEADS, HEAD_DIM, HEAD_DIM)), _const((1, D_MODEL)),
                  _resident((HEADS, HEAD_DIM, HEAD_DIM)), _const((1, D_MODEL)), _const((1, D_MODEL))],
        out_specs=[tile(0)] * 5,
        out_shape=[jax.ShapeDtypeStruct((t, D_MODEL), BF16)] + [jax.ShapeDtypeStruct((t, D_MODEL), F32)] * 4,
        scratch_shapes=[pltpu.VMEM((SUBLANES, D_MODEL), F32), pltpu.VMEM((1, D_MODEL), F32)],
        compiler_params=_params(),
    )(z, z, conv_w, conv_b, wr, br, wi, bi, lam)


def _sgu_forward_parts(ub, vb, lg_ref, lb_ref):
    u, du = _gelu_and_grad(ub)
    vg, dvg = _gelu_and_grad(vb)
    mu = jnp.mean(vg, axis=-1, keepdims=True)
    d = vg - mu
    rstd = lax.rsqrt(jnp.mean(d * d, axis=-1, keepdims=True) + LN_EPS)
    vhat = d * rstd
    vn = (vhat * lg_ref[...] + lb_ref[...]).astype(BF16)
    return u, du, dvg, rstd, vhat, vn


def _causal_mask():
    rows = lax.broadcasted_iota(jnp.int32, (CHUNK, CHUNK), 0)
    cols = lax.broadcasted_iota(jnp.int32, (CHUNK, CHUNK), 1)
    return rows >= cols


def _fwd_sgu(z, ln_g, ln_b, w_s, bias_full, jobs=()):
    t = z.shape[0]

    def body(ub_ref, vb_ref, lg_ref, lb_ref, ws_ref, bias_ref, yb_ref):
        u, _, _, _, _, vn = _sgu_forward_parts(ub_ref[...], vb_ref[...], lg_ref, lb_ref)
        mask = _causal_mask()
        wm = [jnp.where(mask, ws_ref[g], 0.0).astype(BF16) for g in range(GROUPS)]
        for c in range(SEQ_TILE // CHUNK):
            rows = slice(c * CHUNK, (c + 1) * CHUNK)
            for g in range(GROUPS):
                cols = slice(g * GROUP_DIM, (g + 1) * GROUP_DIM)
                sp = _dot(wm[g], vn[rows, cols]) + bias_ref[:, cols]
                yb_ref[rows, cols] = (u[rows, cols] * sp).astype(BF16)

    tile = lambda j: pl.BlockSpec((SEQ_TILE, D_MODEL), lambda i: (i, j))
    return _fused_call(
        body, jobs, name="fwd_sgu", grid=(t // SEQ_TILE,),
        in_specs=[tile(2), tile(3), _const((1, D_MODEL)), _const((1, D_MODEL)),
                  _const((GROUPS, CHUNK, CHUNK)), _const((CHUNK, D_MODEL))],
        out_specs=tile(0),
        out_shape=jax.ShapeDtypeStruct((t, D_MODEL), BF16),
        compiler_params=_params(),
    )(z, z, ln_g, ln_b, w_s, bias_full)


def _fwd_merge(ya, yb, z, x, w_oa, w_ob, w_out, g2, jobs=()):
    t = x.shape[0]

    def body(ya_ref, yb_ref, m_ref, x_ref, woa_ref, wob_ref, wout_ref, g_ref, pa_ref, pb_ref, h1_ref, n2_ref):
        pa = _dot(ya_ref[...], woa_ref[...])
        pb = _dot(yb_ref[...], wob_ref[...])
        pa_ref[...] = pa
        pb_ref[...] = pb
        merged = jax.nn.sigmoid(m_ref[:, :D_MODEL]) * pa + jax.nn.sigmoid(m_ref[:, D_MODEL:]) * pb
        h1 = x_ref[...] + _dot(merged.astype(BF16), wout_ref[...])
        h1_ref[...] = h1
        xhat, _ = _rms(h1)
        n2_ref[...] = (xhat * g_ref[...]).astype(BF16)

    tile = pl.BlockSpec((MM_TILE, D_MODEL), lambda i: (i, 0))
    sq = _resident((D_MODEL, D_MODEL))
    return _fused_call(
        body, jobs, name="fwd_merge", grid=(t // MM_TILE,),
        in_specs=[tile, tile, pl.BlockSpec((MM_TILE, 2 * D_MODEL), lambda i: (i, 2)), tile, sq, sq, sq,
                  _const((1, D_MODEL))],
        out_specs=[tile, tile, tile, tile],
        out_shape=[jax.ShapeDtypeStruct((t, D_MODEL), F32)] * 3 + [jax.ShapeDtypeStruct((t, D_MODEL), BF16)],
        compiler_params=_params(),
    )(ya, yb, z, x, w_oa, w_ob, w_out, g2)


def _mlp(n2, h1, target, w_up_st, w_down, g2, g3, jobs=()):
    t = n2.shape[0]

    def body(n2_ref, h1_ref, tgt_ref, wup_ref, wdown_ref, g2_ref, g3_ref, act_ref, dup_ref, dh2b_ref, dh1_ref,
             loss_ref, dg3_ref, dg2_ref, relu_ref):
        @pl.when(pl.program_id(0) == 0)
        def _():
            for ref in (loss_ref, dg3_ref, dg2_ref):
                ref[...] = jnp.zeros_like(ref)

        n2 = n2_ref[...]
        h1 = h1_ref[...]
        h2 = h1
        for k in range(N_CHIPS):
            cols = slice(k * D_MODEL, (k + 1) * D_MODEL)
            r = jnp.maximum(_dot(n2, wup_ref[k]), 0.0)
            relu_ref[:, cols] = r
            act = (r * r).astype(BF16)
            act_ref[:, cols] = act
            h2 = h2 + _dot(act, wdown_ref[cols, :])
        xhat, r3 = _rms(h2)
        diff = xhat * g3_ref[...] - tgt_ref[...]
        sq = jnp.sum(diff * diff, axis=1, keepdims=True)
        loss_ref[...] = loss_ref[...] + (0.5 / D_MODEL) * jnp.sum(sq, axis=0, keepdims=True)
        dy = diff * (1.0 / D_MODEL)
        dg3_ref[...] = dg3_ref[...] + _col_sum(dy * xhat)
        dh2 = _rms_bwd(dy * g3_ref[...], xhat, r3)
        dh2b = dh2.astype(BF16)
        dh2b_ref[...] = dh2b
        dn2 = jnp.zeros((SEQ_TILE, D_MODEL), F32)
        for k in range(N_CHIPS):
            cols = slice(k * D_MODEL, (k + 1) * D_MODEL)
            dup = (_dot_nt(dh2b, wdown_ref[cols, :]) * (2.0 * relu_ref[:, cols])).astype(BF16)
            dup_ref[:, cols] = dup
            dn2 = dn2 + _dot_nt(dup, wup_ref[k])
        xhat, r2 = _rms(h1)
        dg2_ref[...] = dg2_ref[...] + _col_sum(dn2 * xhat)
        dh1_ref[...] = dh2 + _rms_bwd(dn2 * g2_ref[...], xhat, r2)

    tile = pl.BlockSpec((SEQ_TILE, D_MODEL), lambda i: (i, 0))
    wide = pl.BlockSpec((SEQ_TILE, D_FF), lambda i: (i, 0))
    vec = _const((1, D_MODEL))
    vec_shape = jax.ShapeDtypeStruct((1, D_MODEL), F32)
    return _fused_call(
        body, jobs, name="mlp", grid=(t // SEQ_TILE,),
        in_specs=[tile, tile, tile, _resident((N_CHIPS, D_MODEL, D_MODEL)), _resident((D_FF, D_MODEL)), vec, vec],
        out_specs=[wide, wide, tile, tile, _const((SUBLANES, 128)), vec, vec],
        out_shape=[jax.ShapeDtypeStruct((t, D_FF), BF16), jax.ShapeDtypeStruct((t, D_FF), BF16),
                   jax.ShapeDtypeStruct((t, D_MODEL), BF16), jax.ShapeDtypeStruct((t, D_MODEL), F32),
                   jax.ShapeDtypeStruct((SUBLANES, 128), F32), vec_shape, vec_shape],
        scratch_shapes=[pltpu.VMEM((SEQ_TILE, D_FF), F32)],
        compiler_params=_params(),
    )(n2, h1, target, w_up_st, w_down, g2, g3)


def _bwd_mix(dh1, pa, pb, z, h, xc, r, ig, w_oa, w_ob, w_out, ln_g, ln_b, w_s, bias_full, conv_w, wr, wi, lam, jobs=()):
    t = dh1.shape[0]
    n_tiles = t // SEQ_TILE
    per_tile = SEQ_TILE // SUBLANES

    def merge_part(dh1_ref, pa_ref, pb_ref, m_ref, woa_ref, wob_ref, wout_ref, dz_ref, dya_ref, dyb_ref, mg_ref,
                   dpa_ref, dpb_ref, dh1b_ref):
        dh1b = dh1_ref[...].astype(BF16)
        dh1b_ref[...] = dh1b
        dm = _dot_nt(dh1b, wout_ref[...])
        pa = pa_ref[...]
        pb = pb_ref[...]
        sa = jax.nn.sigmoid(m_ref[:, :D_MODEL])
        sb = jax.nn.sigmoid(m_ref[:, D_MODEL:])
        mg_ref[...] = (sa * pa + sb * pb).astype(BF16)
        dz_ref[:, :D_MODEL] = (dm * pa * sa * (1.0 - sa)).astype(BF16)
        dz_ref[:, D_MODEL:] = (dm * pb * sb * (1.0 - sb)).astype(BF16)
        dpa = (dm * sa).astype(BF16)
        dpb = (dm * sb).astype(BF16)
        dpa_ref[...] = dpa
        dpb_ref[...] = dpb
        dya_ref[...] = _dot_nt(dpa, woa_ref[...])
        dyb_ref[...] = _dot_nt(dpb, wob_ref[...])

    def sgu_part(dyb_ref, ub_ref, vb_ref, lg_ref, lb_ref, ws_ref, bias_ref, dz_ref, dlg_ref, dlb_ref, dws_ref, dbs_ref,
                 dvn_ref, dsp_acc):
        i = pl.program_id(0)

        @pl.when(i == 0)
        def _():
            dlg_ref[...] = jnp.zeros_like(dlg_ref)
            dlb_ref[...] = jnp.zeros_like(dlb_ref)
            dws_ref[...] = jnp.zeros_like(dws_ref)
            dsp_acc[...] = jnp.zeros_like(dsp_acc)

        u, du, dvg, rstd, vhat, vn = _sgu_forward_parts(ub_ref[...], vb_ref[...], lg_ref, lb_ref)
        dyb = dyb_ref[...]
        mask = _causal_mask()
        wm = [jnp.where(mask, ws_ref[g], 0.0).astype(BF16) for g in range(GROUPS)]
        for c in range(SEQ_TILE // CHUNK):
            rows = slice(c * CHUNK, (c + 1) * CHUNK)
            for g in range(GROUPS):
                cols = slice(g * GROUP_DIM, (g + 1) * GROUP_DIM)
                vn_blk = vn[rows, cols]
                sp = _dot(wm[g], vn_blk) + bias_ref[:, cols]
                dyb_blk = dyb[rows, cols]
                dz_ref[rows, cols] = (dyb_blk * sp * du[rows, cols]).astype(BF16)
                dsp = dyb_blk * u[rows, cols]
                dsp_acc[:, cols] = dsp_acc[:, cols] + dsp
                dspb = dsp.astype(BF16)
                dvn_ref[rows, cols] = _dot_tn(wm[g], dspb)
                wcols = slice(g * CHUNK, (g + 1) * CHUNK)
                dws_ref[:, wcols] = dws_ref[:, wcols] + jnp.where(mask, _dot_nt(dspb, vn_blk), 0.0)
        dvn = dvn_ref[...]
        dlg_ref[...] = dlg_ref[...] + _col_sum(dvn * vhat)
        dlb_ref[...] = dlb_ref[...] + _col_sum(dvn)
        dvhat = dvn * lg_ref[...]
        dvgel = rstd * (dvhat - jnp.mean(dvhat, axis=-1, keepdims=True)
                        - vhat * jnp.mean(dvhat * vhat, axis=-1, keepdims=True))
        dz_ref[:, D_MODEL:] = (dvgel * dvg).astype(BF16)

        @pl.when(i == n_tiles - 1)
        def _():
            lane = lax.broadcasted_iota(jnp.int32, (CHUNK, 128), 1)
            out = jnp.zeros((CHUNK, 128), F32)
            for g in range(GROUPS):
                s = jnp.sum(dsp_acc[:, g * GROUP_DIM:(g + 1) * GROUP_DIM], axis=1, keepdims=True)
                out = out + jnp.where(lane == g, s, 0.0)
            dbs_ref[...] = out

    def lru_part(dya_ref, xa_ref, ga_ref, h_ref, h_prev_ref, xc_ref, r_ref, ig_ref, cw_ref, wr_ref, wi_ref, lam_ref,
                 dz_ref, dcw_ref, dcb_ref, dwr_ref, dbr_ref, dwi_ref, dbi_ref, dlam_ref, lam_carry, dxc_head):
        i = pl.program_id(0)

        @pl.when(i == 0)
        def _():
            for ref in (dcw_ref, dcb_ref, dwr_ref, dbr_ref, dwi_ref, dbi_ref, dlam_ref, lam_carry, dxc_head):
                ref[...] = jnp.zeros_like(ref)

        first_tile = i == n_tiles - 1
        h_tail = jnp.where(first_tile, 0.0, h_prev_ref[...])
        xc, r, ig = xc_ref[...], r_ref[...], ig_ref[...]
        xcb = xc.astype(BF16)
        sp, a, mult = _decay(r, lam_ref)
        h = h_ref[...]
        h_prev = _shift_down(h, h_tail, 1)
        dya = dya_ref[...]
        gg, dgg = _gelu_and_grad(ga_ref[...])
        dz_ref[:, D_MODEL:] = (dya * h * dgg).astype(BF16)
        ones = jnp.ones((SUBLANES, D_MODEL), F32)
        lam_t, lam_first = _scan_backward(_shift_up(a, ones, 1), dya * gg, lam_carry[...])
        lam_carry[...] = a[0:1] * lam_first
        dmult = lam_t * xc * ig
        dla = lam_t * h_prev * a - dmult * (a * a) / mult
        dr = dla * ((-LRU_C) * sp)
        dlam_ref[...] = dlam_ref[...] + _col_sum(dla * r) * (LRU_C * jax.nn.sigmoid(-lam_ref[...]))
        dpr = dr * r * (1.0 - r)
        dpi = lam_t * xc * mult * ig * (1.0 - ig)
        dbr_ref[...] = dbr_ref[...] + _col_sum(dpr)
        dbi_ref[...] = dbi_ref[...] + _col_sum(dpi)
        dprb = dpr.astype(BF16)
        dpib = dpi.astype(BF16)
        dxc_gate = []
        for hd in range(HEADS):
            cols = slice(hd * HEAD_DIM, (hd + 1) * HEAD_DIM)
            dxc_gate.append(_dot_nt(dprb[:, cols], wr_ref[hd]) + _dot_nt(dpib[:, cols], wi_ref[hd]))
            dwr_ref[hd] = dwr_ref[hd] + _dot_tn(xcb[:, cols], dprb[:, cols])
            dwi_ref[hd] = dwi_ref[hd] + _dot_tn(xcb[:, cols], dpib[:, cols])
        dxc = lam_t * ig * mult + jnp.concatenate(dxc_gate, axis=1)
        dcb_ref[...] = dcb_ref[...] + _col_sum(dxc)
        cw = cw_ref[...]
        head = dxc_head[...]
        xa = xa_ref[...]
        dxa = cw[0:1] * dxc
        dcw_ref[0:1, :] = dcw_ref[0:1, :] + _col_sum(dxc * xa)
        for k in range(1, CONV_WIDTH):
            dxc_k = _shift_up(dxc, head, k)
            dxa = dxa + cw[k:k + 1] * dxc_k
            dcw_ref[k:k + 1, :] = dcw_ref[k:k + 1, :] + _col_sum(dxc_k * xa)
        dxc_head[...] = dxc[0:SUBLANES]
        dz_ref[:, :D_MODEL] = dxa.astype(BF16)

    def body(dh1_ref, pa_ref, pb_ref, z_ref, h_ref, h_prev_ref, xc_ref, r_ref, ig_ref, woa_ref, wob_ref, wout_ref,
             lg_ref, lb_ref, ws_ref, bias_ref, cw_ref, wr_ref, wi_ref, lam_ref, dz_ref, mg_ref, dpa_ref, dpb_ref,
             dh1b_ref, dlg_ref, dlb_ref, dws_ref, dbs_ref, dcw_ref, dcb_ref, dwr_ref, dbr_ref, dwi_ref, dbi_ref,
             dlam_ref, dya_ref, dyb_ref, dvn_ref, dsp_acc, lam_carry, dxc_head):
        def cols(ref, first, count):
            return ref.at[:, pl.ds(first * D_MODEL, count * D_MODEL)]

        merge_part(dh1_ref, pa_ref, pb_ref, cols(z_ref, 4, 2), woa_ref, wob_ref, wout_ref, cols(dz_ref, 4, 2), dya_ref,
                   dyb_ref, mg_ref, dpa_ref, dpb_ref, dh1b_ref)
        sgu_part(dyb_ref, cols(z_ref, 2, 1), cols(z_ref, 3, 1), lg_ref, lb_ref, ws_ref, bias_ref, cols(dz_ref, 2, 2),
                 dlg_ref, dlb_ref, dws_ref, dbs_ref, dvn_ref, dsp_acc)
        lru_part(dya_ref, cols(z_ref, 0, 1), cols(z_ref, 1, 1), h_ref, h_prev_ref, xc_ref, r_ref, ig_ref, cw_ref, wr_ref,
                 wi_ref, lam_ref, cols(dz_ref, 0, 2), dcw_ref, dcb_ref, dwr_ref, dbr_ref, dwi_ref, dbi_ref, dlam_ref,
                 lam_carry, dxc_head)

    rev = lambda i: n_tiles - 1 - i
    tile = pl.BlockSpec((SEQ_TILE, D_MODEL), lambda i: (rev(i), 0))
    row = pl.BlockSpec((SEQ_TILE, D_IN), lambda i: (rev(i), 0))
    prev8 = pl.BlockSpec((SUBLANES, D_MODEL), lambda i: (jnp.maximum(rev(i) * per_tile - 1, 0), 0))
    vec = _const((1, D_MODEL))
    sq = _resident((D_MODEL, D_MODEL))
    gate_w = _resident((HEADS, HEAD_DIM, HEAD_DIM))
    gate_acc = _const((HEADS, HEAD_DIM, HEAD_DIM))
    vec_shape = jax.ShapeDtypeStruct((1, D_MODEL), F32)
    gate_shape = jax.ShapeDtypeStruct((HEADS, HEAD_DIM, HEAD_DIM), F32)
    act_bf = jax.ShapeDtypeStruct((t, D_MODEL), BF16)
    return _fused_call(
        body, jobs, name="bwd_mix", grid=(n_tiles,),
        in_specs=[tile, tile, tile, row, tile, prev8, tile, tile, tile, sq, sq, sq, vec, vec,
                  _const((GROUPS, CHUNK, CHUNK)), _const((CHUNK, D_MODEL)), _const((CONV_WIDTH, D_MODEL)), gate_w, gate_w,
                  vec],
        out_specs=[row, tile, tile, tile, tile, vec, vec, _const((CHUNK, GROUPS * CHUNK)), _const((CHUNK, 128)),
                   _const((SUBLANES, D_MODEL)), vec, gate_acc, vec, gate_acc, vec, vec],
        out_shape=[jax.ShapeDtypeStruct((t, D_IN), BF16), act_bf, act_bf, act_bf, act_bf, vec_shape, vec_shape,
                   jax.ShapeDtypeStruct((CHUNK, GROUPS * CHUNK), F32), jax.ShapeDtypeStruct((CHUNK, 128), F32),
                   jax.ShapeDtypeStruct((SUBLANES, D_MODEL), F32), vec_shape, gate_shape, vec_shape, gate_shape,
                   vec_shape, vec_shape],
        scratch_shapes=[pltpu.VMEM((SEQ_TILE, D_MODEL), F32), pltpu.VMEM((SEQ_TILE, D_MODEL), F32),
                        pltpu.VMEM((SEQ_TILE, D_MODEL), F32), pltpu.VMEM((CHUNK, D_MODEL), F32),
                        pltpu.VMEM((1, D_MODEL), F32), pltpu.VMEM((SUBLANES, D_MODEL), F32)],
        compiler_params=_params(),
    )(dh1, pa, pb, z, h, h, xc, r, ig, w_oa, w_ob, w_out, ln_g, ln_b, w_s, bias_full, conv_w, wr, wi, lam)


def _bwd_in(dz, x, dh1, w_in_st, g1, jobs=()):
    t = x.shape[0]

    def body(dz_ref, x_ref, dh1_ref, w_ref, g_ref, dx_ref, dg1_ref):
        @pl.when(pl.program_id(0) == 0)
        def _():
            dg1_ref[...] = jnp.zeros_like(dg1_ref)

        dn1 = jnp.zeros((MM_TILE, D_MODEL), F32)
        for k in range(N_CHIPS):
            dn1 = dn1 + _dot_nt(dz_ref[:, k * IN_SHARD:(k + 1) * IN_SHARD], w_ref[k])
        xhat, r1 = _rms(x_ref[...])
        dg1_ref[...] = dg1_ref[...] + _col_sum(dn1 * xhat)
        dx_ref[...] = dh1_ref[...] + _rms_bwd(dn1 * g_ref[...], xhat, r1)

    tile = pl.BlockSpec((MM_TILE, D_MODEL), lambda i: (i, 0))
    return _fused_call(
        body, jobs, name="bwd_in", grid=(t // MM_TILE,),
        in_specs=[pl.BlockSpec((MM_TILE, D_IN), lambda i: (i, 0)), tile, tile,
                  _resident((N_CHIPS, D_MODEL, IN_SHARD)), _const((1, D_MODEL))],
        out_specs=[tile, _const((1, D_MODEL))],
        out_shape=[jax.ShapeDtypeStruct((t, D_MODEL), F32), jax.ShapeDtypeStruct((1, D_MODEL), F32)],
        compiler_params=_params(),
    )(dz, x, dh1, w_in_st, g1)


def _weight_grad(name, a, b, n_blocks, a_varies, b_varies, width, jobs=()):
    t = a.shape[0]
    rows = min(DW_TILE, t)
    n_t = t // rows

    def body(a_ref, b_ref, o_ref, acc_ref):
        s = pl.program_id(1)
        part = _dot_tn(a_ref[...], b_ref[...])

        @pl.when(s == 0)
        def _():
            acc_ref[...] = part

        @pl.when(s > 0)
        def _():
            acc_ref[...] = acc_ref[...] + part

        @pl.when(s == n_t - 1)
        def _():
            o_ref[...] = acc_ref[...].astype(BF16)

    return _fused_call(
        body, jobs, name=name, grid=(n_blocks, n_t),
        in_specs=[pl.BlockSpec((rows, D_MODEL), (lambda j, s: (s, j)) if a_varies else (lambda j, s: (s, 0))),
                  pl.BlockSpec((rows, width), (lambda j, s: (s, j)) if b_varies else (lambda j, s: (s, 0)))],
        out_specs=pl.BlockSpec((None, D_MODEL, width), lambda j, s: (j, 0, 0)),
        out_shape=jax.ShapeDtypeStruct((n_blocks, D_MODEL, width), BF16),
        scratch_shapes=[pltpu.VMEM((D_MODEL, width), F32)],
        compiler_params=_params(2),
    )(a, b)


def _place():
    x, y, c = lax.axis_index("x"), lax.axis_index("y"), lax.axis_index("c")
    other_chips = [(1 - x, y), (x, 1 - y), (1 - x, 1 - y)]
    return x, y, c, other_chips


def _chip_index(px, py):
    return 2 * px + py


ANY = pl.BlockSpec(memory_space=pl.ANY)
SIBLING = ((0, 0, 1),)
NEIGHBOURS = ((1, 0, 0), (0, 1, 0))
OTHER_CHIPS = NEIGHBOURS + ((1, 1, 0),)


def _comm_call(name, jobs):
    return _fused_call(None, jobs, name=name, grid=(), in_specs=[], out_specs=[], out_shape=[])()[1]


def _near_far(x, y, c):
    return (x ^ (1 - c), y ^ c), (x ^ c, y ^ (1 - c))


def _gather_near_job(shards):
    n = len(shards)
    halves = [s.shape[0] // 2 for s in shards]

    def copies(ins, outs, send, recv, local):
        x, y, c, _ = _place()
        near, _ = _near_far(x, y, c)

        def block(w, chip, pc):
            return outs[w].at[_chip_index(*chip), pl.ds(pc * halves[w], halves[w]), :]

        def copy(w, k, chip, pc, to, src=None):
            return pltpu.make_async_remote_copy(
                src_ref=block(w, chip, pc) if src is None else src, dst_ref=block(w, chip, pc),
                send_sem=send.at[2 * w + k], recv_sem=recv.at[2 * w + k], device_id=to, device_id_type=MESH)

        sends, arrivals, own = [], [], []
        for w in range(n):
            src = ins[w].at[pl.ds(c * halves[w], halves[w]), :]
            own.append(pltpu.make_async_copy(src, block(w, (x, y), c), local.at[w]))
            sends += [copy(w, 0, (x, y), c, (*near, c), src), copy(w, 1, (x, y), c, (x, y, 1 - c), src)]
            arrivals += [copy(w, 0, near, c, (x, y, c)), copy(w, 1, (x, y), 1 - c, (x, y, c))]
        return sends, arrivals, own

    return _Job(shards, [jax.ShapeDtypeStruct((N_CHIPS,) + s.shape, s.dtype) for s in shards], 2 * n, copies,
                NEIGHBOURS + SIBLING, n_local=n)


def _gather_far_job(stacked):
    n = len(stacked)
    halves = [s.shape[1] // 2 for s in stacked]

    def copies(ins, outs, send, recv, local):
        del ins, local
        x, y, c, _ = _place()
        near, far = _near_far(x, y, c)

        def copy(w, k, chip):
            blk = outs[w].at[_chip_index(*chip), pl.ds(c * halves[w], halves[w]), :]
            return pltpu.make_async_remote_copy(
                src_ref=blk, dst_ref=blk, send_sem=send.at[2 * w + k], recv_sem=recv.at[2 * w + k],
                device_id=(*far, c), device_id_type=MESH)

        sends = [copy(w, k, chip) for w in range(n) for k, chip in enumerate(((x, y), near))]
        arrivals = [copy(w, k, chip) for w in range(n) for k, chip in enumerate((far, (1 - x, 1 - y)))]
        return sends, arrivals, []

    return _Job(stacked, [jax.ShapeDtypeStruct(s.shape, s.dtype) for s in stacked], 2 * n, copies, NEIGHBOURS,
                aliases={w: w for w in range(n)})


def _gather_pass_job(stacked):
    n = len(stacked)
    halves = [s.shape[1] // 2 for s in stacked]

    def copies(ins, outs, send, recv, local):
        del ins, local
        x, y, c, chips = _place()

        def copy(w, j, chip, pc, to):
            blk = outs[w].at[_chip_index(*chip), pl.ds(pc * halves[w], halves[w]), :]
            return pltpu.make_async_remote_copy(
                src_ref=blk, dst_ref=blk, send_sem=send.at[3 * w + j], recv_sem=recv.at[3 * w + j], device_id=to,
                device_id_type=MESH)

        sends = [copy(w, j, chip, c, (x, y, 1 - c)) for w in range(n) for j, chip in enumerate(chips)]
        arrivals = [copy(w, j, chip, 1 - c, (x, y, c)) for w in range(n) for j, chip in enumerate(chips)]
        return sends, arrivals, []

    return _Job(stacked, [jax.ShapeDtypeStruct(s.shape, s.dtype) for s in stacked], 3 * n, copies, SIBLING,
                aliases={w: w for w in range(n)})


def _gather_small_job(block):
    def copies(ins, outs, send, recv, local):
        x, y, c, chips = _place()

        def copy(j, chip_from, to):
            return pltpu.make_async_remote_copy(
                src_ref=ins[0], dst_ref=outs[0].at[_chip_index(*chip_from)], send_sem=send.at[j],
                recv_sem=recv.at[j], device_id=to, device_id_type=MESH)

        own = [pltpu.make_async_copy(ins[0], outs[0].at[_chip_index(x, y)], local.at[0])]
        sends = [copy(j, (x, y), (*chip, c)) for j, chip in enumerate(chips)]
        arrivals = [copy(j, chip, (x, y, c)) for j, chip in enumerate(chips)]
        return sends, arrivals, own

    return _Job([block], [jax.ShapeDtypeStruct((N_CHIPS,) + block.shape, block.dtype)], 3, copies, OTHER_CHIPS,
                n_local=1)


def _pair_send_job(grads):
    n = len(grads)
    halves = [g.shape[1] // 2 for g in grads]

    def copies(ins, outs, send, recv, local):
        del local
        x, y, c, _ = _place()
        sends = [pltpu.make_async_remote_copy(
            src_ref=ins[w].at[:, pl.ds((1 - c) * halves[w], halves[w]), :], dst_ref=outs[w], send_sem=send.at[w],
            recv_sem=recv.at[w], device_id=(x, y, 1 - c), device_id_type=MESH) for w in range(n)]
        return sends, sends, []

    return _Job(grads, [jax.ShapeDtypeStruct((N_CHIPS, h, g.shape[2]), g.dtype) for g, h in zip(grads, halves)], n,
                copies, SIBLING)


def _row_block(rows, limit=256):
    return min(rows, limit)


def _pair_add(name, core, mine, theirs):
    _, _, h, cols = mine.shape
    rb = _row_block(h, 512)

    def body(core_ref, a_ref, b_ref, o_ref):
        del core_ref
        o_ref[...] = (a_ref[...].astype(F32) + b_ref[...].astype(F32)).astype(BF16)

    return pl.pallas_call(
        body, name=name,
        grid_spec=pltpu.PrefetchScalarGridSpec(
            num_scalar_prefetch=1, grid=(N_CHIPS, h // rb),
            in_specs=[pl.BlockSpec((None, None, rb, cols), lambda k, r, core_ref: (k, core_ref[0], r, 0)),
                      pl.BlockSpec((None, rb, cols), lambda k, r, core_ref: (k, r, 0))],
            out_specs=pl.BlockSpec((None, rb, cols), lambda k, r, core_ref: (k, r, 0))),
        out_shape=jax.ShapeDtypeStruct(theirs.shape, BF16),
        compiler_params=_params(2),
    )(core, mine, theirs)


def _sequencer_call(name, collective_id, job, then=()):
    steps = [job]
    for make in then:
        steps.append(make(steps[-1].out_shape))
    peers = sorted(set(p for step in steps for p in step.peers))
    ins = [jax.new_ref(a, memory_space=pltpu.MemorySpace.HBM) for a in job.inputs]
    outs = [ins[{o: i for i, o in job.aliases.items()}[k]] if k in job.aliases.values()
            else jax.empty_ref(shape, memory_space=pltpu.MemorySpace.HBM) for k, shape in enumerate(job.out_shape)]
    sems = [pltpu.SemaphoreType.DMA((n,)) for step in steps for n in (step.n_sem, step.n_sem, max(step.n_local, 1))]

    @pl.kernel(mesh=plsc.ScalarSubcoreMesh(axis_name="sequencer", num_cores=1), name=name, scratch_types=tuple(sems),
               compiler_params=pltpu.CompilerParams(collective_id=collective_id))
    def launch(*sem_refs):
        x, y, c, _ = _place()
        barrier = pltpu.get_barrier_semaphore()
        for dx, dy, dc in peers:
            pl.semaphore_signal(barrier, inc=1, device_id=(x ^ dx, y ^ dy, c ^ dc), device_id_type=MESH)
        pl.semaphore_wait(barrier, len(peers))
        for k, step in enumerate(steps):
            sends, arrivals, own = step.copies(ins if k == 0 else outs, outs, *sem_refs[3 * k:3 * k + 3])
            for cp in own + sends:
                cp.start()
            for cp in arrivals:
                cp.wait_recv()
            for cp in sends:
                cp.wait_send()
            for cp in own:
                cp.wait()

    launch()
    return [ref[...] for ref in outs]


def _chip_exchange_job(sums):
    n = len(sums)

    def copies(ins, outs, send, recv, local):
        del local
        _, _, c, chips = _place()
        sends = [pltpu.make_async_remote_copy(
            src_ref=ins[w].at[_chip_index(*chip)], dst_ref=outs[w].at[j], send_sem=send.at[3 * w + j],
            recv_sem=recv.at[3 * w + j], device_id=(*chip, c), device_id_type=MESH)
            for w in range(n) for j, chip in enumerate(chips)]
        return sends, sends, []

    return _Job(sums, [jax.ShapeDtypeStruct((N_CHIPS - 1,) + s.shape[1:], s.dtype) for s in sums], 3 * n, copies,
                OTHER_CHIPS)


def _chip_sum(name, place, mine, theirs):
    _, h, cols = mine.shape
    rb = _row_block(h, 512)

    def body(place_ref, p_ref, q_ref, o_ref):
        del place_ref
        acc = p_ref[...].astype(F32)
        for j in range(N_CHIPS - 1):
            acc = acc + q_ref[j].astype(F32)
        o_ref[...] = acc

    return pl.pallas_call(
        body, name=name,
        grid_spec=pltpu.PrefetchScalarGridSpec(
            num_scalar_prefetch=1, grid=(h // rb,),
            in_specs=[pl.BlockSpec((None, rb, cols), lambda r, place_ref: (place_ref[0], r, 0)),
                      pl.BlockSpec((N_CHIPS - 1, rb, cols), lambda r, place_ref: (0, r, 0))],
            out_specs=pl.BlockSpec((None, rb, cols), lambda r, place_ref: (place_ref[1], r, 0))),
        out_shape=jax.ShapeDtypeStruct((2, h, cols), F32),
        compiler_params=_params(),
    )(place, mine, theirs)


def _share_job(bufs):
    n = len(bufs)

    def copies(ins, outs, send, recv, local):
        del ins, local
        x, y, c, _ = _place()

        def copy(w, half):
            return pltpu.make_async_remote_copy(
                src_ref=outs[w].at[half], dst_ref=outs[w].at[half], send_sem=send.at[w], recv_sem=recv.at[w],
                device_id=(x, y, 1 - c), device_id_type=MESH)

        return [copy(w, c) for w in range(n)], [copy(w, 1 - c) for w in range(n)], []

    return _Job(bufs, [jax.ShapeDtypeStruct(b.shape, b.dtype) for b in bufs], n, copies, SIBLING,
                aliases={w: w for w in range(n)})


SMALL_ROWS = 24
ROW_G1, ROW_CW, ROW_CB, ROW_BR, ROW_BI, ROW_LAM, ROW_LG, ROW_LB, ROW_G2, ROW_G3, ROW_LOSS, ROW_BS = (
    0, 1, 5, 6, 7, 8, 9, 10, 11, 12, 13, 16)
N_DEV = 8


def _pack_small(dcw, dcb, dbr, dbi, dlam, dlg, dlb, dg2, dg3, loss, dbs):
    def body(dcw_ref, dcb_ref, dbr_ref, dbi_ref, dlam_ref, dlg_ref, dlb_ref, dg2_ref, dg3_ref, loss_ref, dbs_ref, out):
        out[...] = jnp.zeros((SMALL_ROWS, D_MODEL), F32)
        for row, ref in ((ROW_CB, dcb_ref), (ROW_BR, dbr_ref), (ROW_BI, dbi_ref), (ROW_LAM, dlam_ref),
                         (ROW_LG, dlg_ref), (ROW_LB, dlb_ref), (ROW_G2, dg2_ref), (ROW_G3, dg3_ref)):
            out[row:row + 1, :] = ref[...]
        out[ROW_CW:ROW_CW + CONV_WIDTH, :] = dcw_ref[0:CONV_WIDTH, :]
        out[ROW_LOSS:ROW_LOSS + 1, 0:128] = loss_ref[0:1, :]
        out[ROW_BS:ROW_BS + GROUPS, 0:128] = jnp.transpose(dbs_ref[...])[0:GROUPS, :]

    vm = pl.BlockSpec(memory_space=pltpu.VMEM)
    return pl.pallas_call(
        body, name="pack_small", in_specs=[vm] * 11, out_specs=vm,
        out_shape=jax.ShapeDtypeStruct((SMALL_ROWS, D_MODEL), F32),
    )(dcw, dcb, dbr, dbi, dlam, dlg, dlb, dg2, dg3, loss, dbs)


def _gather_all_job(blocks):
    n = len(blocks)
    flips = [(dx, dy, dc) for dx in (0, 1) for dy in (0, 1) for dc in (0, 1)][1:]

    def copies(ins, outs, send, recv, local):
        x, y, c, _ = _place()
        me = 4 * x + 2 * y + c
        sends, arrivals, own = [], [], []
        for w in range(n):
            own.append(pltpu.make_async_copy(ins[w], outs[w].at[me], local.at[w]))
            for k, (dx, dy, dc) in enumerate(flips):
                peer = (x ^ dx, y ^ dy, c ^ dc)
                sem = dict(send_sem=send.at[7 * w + k], recv_sem=recv.at[7 * w + k])
                sends.append(pltpu.make_async_remote_copy(
                    src_ref=ins[w], dst_ref=outs[w].at[me], device_id=peer, device_id_type=MESH, **sem))
                arrivals.append(pltpu.make_async_remote_copy(
                    src_ref=ins[w], dst_ref=outs[w].at[4 * peer[0] + 2 * peer[1] + peer[2]], device_id=peer,
                    device_id_type=MESH, **sem))
        return sends, arrivals, own

    return _Job(blocks, [jax.ShapeDtypeStruct((N_DEV,) + b.shape, b.dtype) for b in blocks], 7 * n, copies,
                OTHER_CHIPS + SIBLING + tuple((dx, dy, 1) for dx, dy, _ in OTHER_CHIPS), n_local=n)


def _sum_small(vec_all, ws_all, dg1_all):
    def body(vec_ref, ws_ref, dg1_ref, vec_out, ws_out):
        vec, ws, dg1 = vec_ref[0], ws_ref[0], dg1_ref[0]
        for d in range(1, N_DEV):
            vec, ws, dg1 = vec + vec_ref[d], ws + ws_ref[d], dg1 + dg1_ref[d]
        vec_out[...] = vec
        vec_out[ROW_G1:ROW_G1 + 1, :] = dg1
        ws_out[...] = ws

    vm = pl.BlockSpec(memory_space=pltpu.VMEM)
    return pl.pallas_call(
        body, name="sum_small", in_specs=[vm] * 3, out_specs=[vm, vm],
        out_shape=[jax.ShapeDtypeStruct(vec_all.shape[1:], F32), jax.ShapeDtypeStruct(ws_all.shape[1:], F32)],
    )(vec_all, ws_all, dg1_all)


def _adamw_math(w, g, m, v):
    m = ADAM_B1 * m + (1.0 - ADAM_B1) * g
    v = ADAM_B2 * v + (1.0 - ADAM_B2) * (g * g)
    m_hat = m / (1.0 - ADAM_B1 ** ADAM_STEP)
    v_hat = v / (1.0 - ADAM_B2 ** ADAM_STEP)
    delta = (-ADAM_LR) * (m_hat / (jnp.sqrt(v_hat) + ADAM_EPS) + ADAM_WD * w)
    return delta, m, v


def _adamw(name, g, w, m, v, jobs=()):
    rows, cols = w.shape
    rb = _row_block(rows)

    def body(g_ref, w_ref, m_ref, v_ref, d_ref, nm_ref, nv_ref):
        d_ref[...], nm_ref[...], nv_ref[...] = _adamw_math(w_ref[...], g_ref[...], m_ref[...], v_ref[...])

    blk = pl.BlockSpec((rb, cols), lambda r: (r, 0))
    return _fused_call(
        body, jobs, name=name, grid=(rows // rb,), in_specs=[blk] * 4, out_specs=[blk] * 3,
        out_shape=[jax.ShapeDtypeStruct(w.shape, F32)] * 3, compiler_params=_params(),
    )(g, w, m, v)


def _adamw_small(grads, ws, ms, vs):
    n = len(grads)

    def body(*refs):
        g_refs, w_refs, m_refs, v_refs = refs[:n], refs[n:2 * n], refs[2 * n:3 * n], refs[3 * n:4 * n]
        outs = refs[4 * n:]
        for p in range(n):
            d, nm, nv = _adamw_math(w_refs[p][...], g_refs[p][...], m_refs[p][...], v_refs[p][...])
            outs[p][...] = d
            outs[n + p][...] = nm
            outs[2 * n + p][...] = nv

    vm = pl.BlockSpec(memory_space=pltpu.VMEM)
    shapes = [jax.ShapeDtypeStruct(w.shape, F32) for w in ws]
    out = pl.pallas_call(
        body, name="adamw_small", in_specs=[vm] * (4 * n), out_specs=[vm] * (3 * n), out_shape=shapes * 3,
    )(*grads, *ws, *ms, *vs)
    return out[:n], out[n:2 * n], out[2 * n:]


def _unstack_heads(w_st):
    per = HEAD_DIM // N_CHIPS
    return w_st.reshape(N_CHIPS, HEADS, per, HEAD_DIM).transpose(1, 0, 2, 3).reshape(HEADS, HEAD_DIM, HEAD_DIM)


def _stack_heads(w):
    per = HEAD_DIM // N_CHIPS
    return w.reshape(HEADS, N_CHIPS, per, HEAD_DIM).transpose(1, 0, 2, 3).reshape(N_CHIPS, HEADS * per, HEAD_DIM)


def kernel(x, norm_mix_g, w_in, conv_w, conv_b, w_rgate, b_rgate, w_igate, b_igate, lru_lambda, w_out_a, sgu_ln_g, sgu_ln_b, sgu_w_s, sgu_b_s, w_out_b, w_out, norm_mlp_g, w_up, w_down, norm_final_g, loss_target, m_norm_mix_g, m_w_in, m_conv_w, m_conv_b, m_w_rgate, m_b_rgate, m_w_igate, m_b_igate, m_lru_lambda, m_w_out_a, m_sgu_ln_g, m_sgu_ln_b, m_sgu_w_s, m_sgu_b_s, m_w_out_b, m_w_out, m_norm_mlp_g, m_w_up, m_w_down, m_norm_final_g, v_norm_mix_g, v_w_in, v_conv_w, v_conv_b, v_w_rgate, v_b_rgate, v_w_igate, v_b_igate, v_lru_lambda, v_w_out_a, v_sgu_ln_g, v_sgu_ln_b, v_sgu_w_s, v_sgu_b_s, v_w_out_b, v_w_out, v_norm_mlp_g, v_w_up, v_w_down, v_norm_final_g):
    chip = _chip_index(lax.axis_index("x"), lax.axis_index("y"))
    core = lax.axis_index("c")
    quarter_h = HEAD_DIM // N_CHIPS
    quarter_d = D_MODEL // N_CHIPS

    as_2d = lambda a: a.reshape(-1, a.shape[-1])
    big_w = [as_2d(w) for w in (w_in, w_rgate, w_igate, w_out_a, w_out_b, w_out, w_up, w_down)]
    big_m = [as_2d(w) for w in (m_w_in, m_w_rgate, m_w_igate, m_w_out_a, m_w_out_b, m_w_out, m_w_up, m_w_down)]
    big_v = [as_2d(w) for w in (v_w_in, v_w_rgate, v_w_igate, v_w_out_a, v_w_out_b, v_w_out, v_w_up, v_w_down)]

    packed = jnp.concatenate([conv_w[0], b_rgate[0], b_igate[0]], axis=1)
    packed = jnp.concatenate([packed, jnp.zeros_like(packed)], axis=0)
    s_in, s_r, s_i, s_oa, s_ob, s_out, s_up, s_down = [w.astype(BF16) for w in big_w]
    xs, target = x[0], loss_target[0]
    g3 = norm_final_g.reshape(1, D_MODEL)
    bias_s = jnp.broadcast_to(jnp.transpose(sgu_b_s[0])[:, :, None], (CHUNK, GROUPS, GROUP_DIM)).reshape(CHUNK, D_MODEL)
    core_arr = core.reshape(1).astype(jnp.int32)
    place = jnp.stack([chip, core]).astype(jnp.int32)
    quarter = lambda g: g.reshape(N_CHIPS, D_MODEL // N_CHIPS, D_MODEL)

    def pair_add(nm, g, from_sibling):
        return _pair_add("pair_add_" + nm, core_arr, g.reshape(N_CHIPS, 2, g.shape[1] // 2, g.shape[2]), from_sibling)

    def chip_sum(nm, pair, from_chips):
        return _chip_sum("chip_sum_" + nm, place, pair, from_chips)

    order = jnp.stack([chip, chip ^ 2, chip ^ 1, chip ^ 3]).astype(jnp.int32)
    (z, n1, (w_in_st, wr_st, wi_st)), ((packed_all,), late) = _fwd_in(
        xs, norm_mix_g, [s_in, s_r, s_i], order,
        jobs=[_gather_small_job(packed), _gather_near_job([s_oa, s_ob, s_out])])
    pick = lambda lo, hi: packed_all[:, :HEADS, lo:hi].transpose(1, 0, 2).reshape(HEADS, -1)
    conv_w_full = pick(0, quarter_d)
    br_full = pick(quarter_d, quarter_d + quarter_h).reshape(1, D_MODEL)
    bi_full = pick(quarter_d + quarter_h, quarter_d + 2 * quarter_h).reshape(1, D_MODEL)
    wr, wi = _unstack_heads(wr_st), _unstack_heads(wi_st)
    lru = (conv_w_full, conv_b, wr, br_full, wi, bi_full, lru_lambda)
    sgu = (sgu_ln_g, sgu_ln_b, sgu_w_s[0], bias_s)

    after = lambda arrays, result: lax.optimization_barrier((arrays, result))[0]
    w_up_st, w_dn = _sequencer_call("gather_mlp", 8, _gather_near_job(after([s_up, s_down], n1)),
                                    then=(_gather_far_job, _gather_pass_job))
    w_dn = w_dn.reshape(D_FF, D_MODEL)
    (ya, *saved), (late,) = _fwd_lru(z, *lru, jobs=[_gather_far_job(late)])
    yb, (late,) = _fwd_sgu(z, *sgu, jobs=[_gather_pass_job(late)])
    w_oa, w_ob, w_o = [w.reshape(D_MODEL, D_MODEL) for w in late]
    (pa, pb, h1, n2), _ = _fwd_merge(ya, yb, z, xs, w_oa, w_ob, w_o, norm_mlp_g)
    (act, dup, dh2b, dh1, loss_part, dg3, dg2), _ = _mlp(n2, h1, target, w_up_st, w_dn, norm_mlp_g, g3)

    d_down, _ = _weight_grad("dw_down", act, dh2b, N_CHIPS, True, False, D_MODEL)
    r_down, = _sequencer_call("send_w_down", 10, _pair_send_job([d_down]))
    d_up, _ = _weight_grad("dw_up", n2, dup, N_CHIPS, False, True, D_MODEL)
    r_up, = _sequencer_call("send_w_up", 11, _pair_send_job([d_up]))
    p_down, p_up = pair_add("w_down", d_down, r_down), pair_add("w_up", d_up, r_up)
    q_up, q_down = _sequencer_call("exchange_mlp", 13, _chip_exchange_job([p_up, p_down]))
    (dz, merged, dpa, dpb, dh1b, dlg, dlb, dws, dbs, dcw, dcb, dwr, dbr, dwi, dbi, dlam), _ = _bwd_mix(
        dh1, pa, pb, z, *saved, w_oa, w_ob, w_o, *sgu, conv_w_full, wr, wi, lru_lambda)
    half_up, half_down = chip_sum("w_up", p_up, q_up), chip_sum("w_down", p_down, q_down)
    full_up, full_down = _sequencer_call("share_mlp", 14, _share_job([half_up, half_down]))
    names = ("w_in", "w_rgate", "w_igate", "w_out_a", "w_out_b", "w_out", "w_up", "w_down")
    d_out, _ = _weight_grad("dw_out", merged, dh1b, 1, False, False, D_MODEL)
    d_oa, _ = _weight_grad("dw_out_a", ya, dpa, 1, False, False, D_MODEL)
    d_ob, _ = _weight_grad("dw_out_b", yb, dpb, 1, False, False, D_MODEL)
    mids = [quarter(d_oa), quarter(d_ob), quarter(d_out)]
    r_mids = _sequencer_call("send_mids", 1, _pair_send_job(mids))
    gates = [_stack_heads(dwr).astype(BF16), _stack_heads(dwi).astype(BF16)]
    small = _pack_small(dcw, dcb, dbr, dbi, dlam, dlg, dlb, dg2, dg3, loss_part, dbs)
    p_mids = [pair_add(nm, g, r) for nm, g, r in zip(names[3:6], mids, r_mids)]
    q_mids = _sequencer_call("exchange_mids", 2, _chip_exchange_job(p_mids))
    r_gates = _sequencer_call("send_gates", 15, _pair_send_job(gates))
    vec_all, ws_all = _sequencer_call("gather_small_grads", 16, _gather_all_job([small, dws]))
    d_in, _ = _weight_grad("dw_in", n1, dz, N_CHIPS, False, True, IN_SHARD)
    r_in, = _sequencer_call("send_w_in", 3, _pair_send_job([d_in]))
    adam_args = {nm: (w, m, v) for nm, w, m, v in zip(names, big_w, big_m, big_v)}

    def adamw(nm, g):
        w, m, v = adam_args[nm]
        g = g.reshape(w.shape)
        return g, _adamw("adamw_" + nm, g, w, m, v)[0]

    p_gates = [pair_add(nm, g, r) for nm, g, r in zip(names[1:3], gates, r_gates)]
    half_mids = [chip_sum(nm, p, q) for nm, p, q in zip(names[3:6], p_mids, q_mids)]
    full_mids = _sequencer_call("share_mids", 12, _share_job(half_mids))
    p_first = [pair_add("w_in", d_in, r_in)] + p_gates
    q_first = _sequencer_call("exchange_w_in", 4, _chip_exchange_job(p_first))
    (grad_x, dg1), _ = _bwd_in(dz, xs, dh1, w_in_st, norm_mix_g)
    dg1_all, = _sequencer_call("gather_dg1", 6, _gather_all_job([dg1]))
    done = {nm: adamw(nm, f) for nm, f in zip(("w_up", "w_down") + names[3:6], [full_up, full_down] + full_mids)}
    q_first = after(q_first, [out[0] for _, out in done.values()])
    half_first = [chip_sum(nm, p, q) for nm, p, q in zip(names[:3], p_first, q_first)]
    full_first = _sequencer_call("share_last", 5, _share_job(half_first))
    done.update({nm: adamw(nm, f) for nm, f in zip(names[:3], full_first)})
    full, big_out = [done[nm][0] for nm in names], [done[nm][1] for nm in names]

    vec, ws_sum = _sum_small(vec_all, ws_all, dg1_all)
    row = lambda r: vec[r:r + 1]
    shard = lambda a, width: lax.dynamic_slice_in_dim(a, chip * width, width, axis=1)
    g_small = dict(
        norm_mix_g=row(ROW_G1), conv_w=shard(vec[ROW_CW:ROW_CW + CONV_WIDTH], quarter_d), conv_b=row(ROW_CB),
        b_rgate=shard(row(ROW_BR).reshape(HEADS, HEAD_DIM), quarter_h),
        b_igate=shard(row(ROW_BI).reshape(HEADS, HEAD_DIM), quarter_h), lru_lambda=row(ROW_LAM),
        sgu_ln_g=row(ROW_LG), sgu_ln_b=row(ROW_LB),
        sgu_w_s=ws_sum.reshape(CHUNK, GROUPS, CHUNK).transpose(1, 0, 2).reshape(GROUPS * CHUNK, CHUNK),
        sgu_b_s=vec[ROW_BS:ROW_BS + GROUPS, 0:CHUNK], norm_mlp_g=row(ROW_G2), norm_final_g=row(ROW_G3))
    loss = vec[ROW_LOSS, 0]
    small_names = list(g_small)
    given = dict(
        norm_mix_g=(norm_mix_g, m_norm_mix_g, v_norm_mix_g), conv_w=(conv_w, m_conv_w, v_conv_w),
        conv_b=(conv_b, m_conv_b, v_conv_b), b_rgate=(b_rgate, m_b_rgate, v_b_rgate),
        b_igate=(b_igate, m_b_igate, v_b_igate), lru_lambda=(lru_lambda, m_lru_lambda, v_lru_lambda),
        sgu_ln_g=(sgu_ln_g, m_sgu_ln_g, v_sgu_ln_g), sgu_ln_b=(sgu_ln_b, m_sgu_ln_b, v_sgu_ln_b),
        sgu_w_s=(sgu_w_s, m_sgu_w_s, v_sgu_w_s), sgu_b_s=(sgu_b_s, m_sgu_b_s, v_sgu_b_s),
        norm_mlp_g=(norm_mlp_g, m_norm_mlp_g, v_norm_mlp_g), norm_final_g=(norm_final_g, m_norm_final_g, v_norm_final_g))
    g2d = [g_small[nm] for nm in small_names]
    to2d = lambda a, g: a.reshape(g.shape)
    d_s, m_s, v_s = _adamw_small(
        g2d, *[[to2d(given[nm][q], g) for nm, g in zip(small_names, g2d)] for q in range(3)])

    shapes = dict(
        norm_mix_g=norm_mix_g, w_in=w_in, conv_w=conv_w, conv_b=conv_b, w_rgate=w_rgate, b_rgate=b_rgate,
        w_igate=w_igate, b_igate=b_igate, lru_lambda=lru_lambda, w_out_a=w_out_a, sgu_ln_g=sgu_ln_g,
        sgu_ln_b=sgu_ln_b, sgu_w_s=sgu_w_s, sgu_b_s=sgu_b_s, w_out_b=w_out_b, w_out=w_out, norm_mlp_g=norm_mlp_g,
        w_up=w_up, w_down=w_down, norm_final_g=norm_final_g)
    grads, deltas, new_m, new_v = {}, {}, {}, {}
    for nm, g, (d, nmom, nvar) in zip(names, full, big_out):
        grads[nm], deltas[nm], new_m[nm], new_v[nm] = g, d, nmom, nvar
    for p, nm in enumerate(small_names):
        grads[nm], deltas[nm], new_m[nm], new_v[nm] = g2d[p], d_s[p], m_s[p], v_s[p]
    order = list(shapes)
    out = [loss, grad_x[None]]
    for group in (grads, deltas, new_m, new_v):
        out += [group[nm].reshape(shapes[nm].shape) for nm in order]
    return tuple(out)
```

```python
import functools

import jax
import jax.numpy as jnp
from jax import lax
from jax.experimental import pallas as pl
from jax.experimental.pallas import tpu as pltpu
from jax.experimental.pallas import tpu_sc as plsc

F32 = jnp.float32
BF16 = jnp.bfloat16
MESH = pl.DeviceIdType.MESH

D_MODEL = 1024
D_IN = 6 * D_MODEL
D_FF = 4 * D_MODEL
N_CHIPS = 4
IN_SHARD = D_IN // N_CHIPS
HEADS = 4
HEAD_DIM = D_MODEL // HEADS
GROUPS = 4
GROUP_DIM = D_MODEL // GROUPS
CHUNK = 128
CONV_WIDTH = 4
LRU_C = 8.0
NORM_EPS = 1e-6
LN_EPS = 1e-5

ADAM_LR = 0.001
ADAM_B1 = 0.9
ADAM_B2 = 0.999
ADAM_EPS = 1e-08
ADAM_WD = 0.01
ADAM_STEP = 10

SUBLANES = 8
MM_TILE = 512
IN_TILE = 1024
SEQ_TILE = 256
DW_TILE = 2048
VMEM_LIMIT_BYTES = 56 * 1024 * 1024

GELU_K0 = 0.7978845608028654
GELU_K1 = 0.044715


def _params(n_grid_axes=1):
    return pltpu.CompilerParams(
        dimension_semantics=("arbitrary",) * n_grid_axes, vmem_limit_bytes=VMEM_LIMIT_BYTES)


def _resident(shape):
    nd = len(shape)
    return pl.BlockSpec(shape, lambda *_: (0,) * nd, pipeline_mode=pl.Buffered(1))


def _const(shape):
    nd = len(shape)
    return pl.BlockSpec(shape, lambda *_: (0,) * nd)


def _dot(a, b):
    return jnp.dot(a, b, preferred_element_type=F32)


def _dot_nt(a, b):
    return lax.dot_general(a, b, (((1,), (1,)), ((), ())), preferred_element_type=F32)


def _dot_tn(a, b):
    return lax.dot_general(a, b, (((0,), (0,)), ((), ())), preferred_element_type=F32)


def _gelu(x):
    t = jnp.tanh(GELU_K0 * x * (1.0 + GELU_K1 * x * x))
    return 0.5 * x * (1.0 + t)


def _gelu_and_grad(x):
    x2 = x * x
    t = jnp.tanh(GELU_K0 * x * (1.0 + GELU_K1 * x2))
    g = 0.5 * x * (1.0 + t)
    dg = 0.5 * (1.0 + t) + 0.5 * x * (1.0 - t * t) * (GELU_K0 * (1.0 + 3.0 * GELU_K1 * x2))
    return g, dg


def _rms(x):
    r = lax.rsqrt(jnp.mean(x * x, axis=-1, keepdims=True) + NORM_EPS)
    return x * r, r


def _rms_bwd(dn, xhat, r):
    return r * (dn - xhat * jnp.mean(dn * xhat, axis=-1, keepdims=True))


def _col_sum(v):
    return jnp.sum(v, axis=0, keepdims=True)


def _shift_down(x, tail8, k):
    xs = pltpu.roll(x, k, 0)
    ts = pltpu.roll(tail8, k, 0)
    ridx = lax.broadcasted_iota(jnp.int32, tail8.shape, 0)
    head = jnp.where(ridx < k, ts, xs[0:SUBLANES])
    return jnp.concatenate([head, xs[SUBLANES:]], axis=0)


def _shift_up(x, head8, k):
    n = x.shape[0]
    xs = pltpu.roll(x, n - k, 0)
    hs = pltpu.roll(head8, SUBLANES - k, 0)
    ridx = lax.broadcasted_iota(jnp.int32, head8.shape, 0)
    last = jnp.where(ridx >= SUBLANES - k, hs, xs[n - SUBLANES:n])
    return jnp.concatenate([xs[:n - SUBLANES], last], axis=0)


def _scan_forward(a, b, carry):
    n, cols = a.shape
    groups = n // SUBLANES
    a = a.reshape(groups, SUBLANES, cols)
    b = b.reshape(groups, SUBLANES, cols)
    sub = lax.broadcasted_iota(jnp.int32, a.shape, 1)
    for s in (1, 2, 4):
        a_s = pltpu.roll(a, s, 1)
        b_s = pltpu.roll(b, s, 1)
        m = sub >= s
        b = jnp.where(m, a * b_s + b, b)
        a = jnp.where(m, a * a_s, a)
    out = []
    for g in range(groups):
        h = a[g] * carry + b[g]
        out.append(h)
        carry = h[SUBLANES - 1:SUBLANES]
    return jnp.concatenate(out, axis=0), carry


def _scan_backward(a, b, carry):
    n, cols = a.shape
    groups = n // SUBLANES
    a = a.reshape(groups, SUBLANES, cols)
    b = b.reshape(groups, SUBLANES, cols)
    sub = lax.broadcasted_iota(jnp.int32, a.shape, 1)
    for s in (1, 2, 4):
        a_s = pltpu.roll(a, SUBLANES - s, 1)
        b_s = pltpu.roll(b, SUBLANES - s, 1)
        m = sub < SUBLANES - s
        b = jnp.where(m, a * b_s + b, b)
        a = jnp.where(m, a * a_s, a)
    out = [None] * groups
    for g in reversed(range(groups)):
        h = a[g] * carry + b[g]
        out[g] = h
        carry = h[0:1]
    return jnp.concatenate(out, axis=0), carry


def _softplus_neg(lam):
    e = jnp.exp(-jnp.abs(lam))
    u = 1.0 + e
    log1p_e = jnp.where(u == 1.0, e, jnp.log(u) * (e / jnp.where(u == 1.0, 1.0, u - 1.0)))
    return jnp.maximum(-lam, 0.0) + log1p_e


def _lru_gates(xa, tail8, cw_ref, cb_ref, wr_ref, br_ref, wi_ref, bi_ref, lam_ref):
    cw = cw_ref[...]
    xc = cb_ref[...] + cw[0:1] * xa
    for k in range(1, CONV_WIDTH):
        xc = xc + cw[k:k + 1] * _shift_down(xa, tail8, k)
    xcb = xc.astype(BF16)
    pre_r, pre_i = [], []
    for h in range(HEADS):
        cols = slice(h * HEAD_DIM, (h + 1) * HEAD_DIM)
        pre_r.append(_dot(xcb[:, cols], wr_ref[h]))
        pre_i.append(_dot(xcb[:, cols], wi_ref[h]))
    r = jax.nn.sigmoid(jnp.concatenate(pre_r, axis=1) + br_ref[...])
    ig = jax.nn.sigmoid(jnp.concatenate(pre_i, axis=1) + bi_ref[...])
    _, a, mult = _decay(r, lam_ref)
    return xc, r, ig, a, mult


def _decay(r, lam_ref):
    sp = _softplus_neg(lam_ref[...])
    log_a = ((-LRU_C) * sp) * r
    a = jnp.exp(log_a)
    th = jnp.tanh(log_a)
    return sp, a, jnp.sqrt((-2.0 * th) / (1.0 - th))


class _Job:
    def __init__(self, inputs, out_shape, n_sem, copies, peers, aliases=None, n_local=0):
        self.inputs, self.out_shape, self.n_sem, self.copies = list(inputs), list(out_shape), n_sem, copies
        self.aliases, self.n_local = dict(aliases or {}), n_local
        self.peers = tuple(peers)


def _fused_call(body, jobs, *, name, grid, in_specs, out_specs, out_shape, scratch_shapes=(),
                input_output_aliases=None, compiler_params=None, n_prefetch=0, jobs_start_after=None):
    single = not isinstance(out_shape, (list, tuple))
    out_specs = [out_specs] if single else list(out_specs)
    out_shape = [out_shape] if single else list(out_shape)
    n_scr = len(scratch_shapes)
    in_specs, scratch_shapes = list(in_specs), list(scratch_shapes)
    n_in, n_out = len(in_specs), len(out_shape)
    aliases = dict(input_output_aliases or {})
    in_at, out_at = [], []
    for job in jobs:
        in_at.append(len(in_specs))
        out_at.append(len(out_shape))
        for i, o in job.aliases.items():
            aliases[n_prefetch + len(in_specs) + i] = len(out_shape) + o
        in_specs += [ANY] * len(job.inputs)
        out_specs += [ANY] * len(job.out_shape)
        out_shape += job.out_shape
        scratch_shapes += [pltpu.SemaphoreType.DMA((job.n_sem,)), pltpu.SemaphoreType.DMA((job.n_sem,)),
                           pltpu.SemaphoreType.DMA((max(job.n_local, 1),))]
    n_in_all, n_out_all = len(in_specs), len(out_shape)

    def full_body(*refs):
        prefetch, refs = refs[:n_prefetch], refs[n_prefetch:]
        ins, outs, scr = refs[:n_in_all], refs[n_in_all:n_in_all + n_out_all], refs[n_in_all + n_out_all:]

        def copies(q):
            job = jobs[q]
            return job.copies(ins[in_at[q]:in_at[q] + len(job.inputs)], outs[out_at[q]:out_at[q] + len(job.out_shape)],
                              *scr[n_scr + 3 * q:n_scr + 3 * q + 3])

        def start():
            for q in range(len(jobs)):
                sends, _, local = copies(q)
                for cp in local + sends:
                    cp.start()

        def finish():
            every = [copies(q) for q in range(len(jobs))]
            for _, arrivals, _ in every:
                for cp in arrivals:
                    cp.wait_recv()
            for sends, _, local in every:
                for cp in sends:
                    cp.wait_send()
                for cp in local:
                    cp.wait()

        if not grid:
            start()
            finish()
            return
        ids = [pl.program_id(a) for a in range(len(grid))]
        at_step = lambda step: functools.reduce(jnp.logical_and, [i == k for i, k in zip(ids, step)])
        if jobs and jobs_start_after is None:
            pl.when(at_step((0,) * len(grid)))(start)
        body(*prefetch, *ins[:n_in], *outs[:n_out], *scr[:n_scr])
        if jobs and jobs_start_after is not None:
            pl.when(at_step(jobs_start_after))(start)
        if jobs:
            pl.when(functools.reduce(jnp.logical_and, [i == g - 1 for i, g in zip(ids, grid)]))(finish)

    if n_prefetch:
        layout = dict(grid_spec=pltpu.PrefetchScalarGridSpec(
            num_scalar_prefetch=n_prefetch, grid=grid, in_specs=in_specs, out_specs=out_specs,
            scratch_shapes=scratch_shapes))
    else:
        layout = dict(grid=grid, in_specs=in_specs, out_specs=out_specs, scratch_shapes=scratch_shapes)
    call = pl.pallas_call(
        full_body, name=name, out_shape=out_shape, input_output_aliases=aliases, compiler_params=compiler_params,
        **layout)

    def run(*args):
        res = call(*args, *[a for job in jobs for a in job.inputs])
        mine = res[0] if single else list(res[:n_out])
        return mine, [list(res[at:at + len(job.out_shape)]) for at, job in zip(out_at, jobs)]

    return run


def _fwd_in(x, g1, shards, order, jobs=()):
    t = x.shape[0]
    rows_per_step = min(IN_TILE, t)
    n_tiles = t // rows_per_step
    n = len(shards)
    halves = [s.shape[0] // 2 for s in shards]

    def body(order_ref, x_ref, g_ref, *refs):
        del order_ref
        ins, (z_ref, n_ref), outs = refs[:n], refs[n:n + 2], refs[n + 2:2 * n + 2]
        wbuf, nbuf, send, recv, local = refs[2 * n + 2:]
        s, i = pl.program_id(0), pl.program_id(1)
        x_, y_, c, chips = _place()
        k_me = _chip_index(x_, y_)

        def block(w, chip, pc):
            return outs[w].at[_chip_index(*chip), pl.ds(pc * halves[w], halves[w]), :]

        def over_ici(w, j, landing):
            return pltpu.make_async_remote_copy(
                src_ref=ins[w].at[pl.ds(c * halves[w], halves[w]), :],
                dst_ref=block(w, chips[j] if landing else (x_, y_), c), send_sem=send.at[6 * w + j],
                recv_sem=recv.at[6 * w + j], device_id=(*chips[j], c), device_id_type=MESH)

        def to_sibling(w, j, landing):
            blk = block(w, chips[j], 1 - c if landing else c)
            return pltpu.make_async_remote_copy(
                src_ref=blk, dst_ref=blk, send_sem=send.at[6 * w + 3 + j], recv_sem=recv.at[6 * w + 3 + j],
                device_id=(x_, y_, 1 - c), device_id_type=MESH)

        own = [pltpu.make_async_copy(wbuf, outs[0].at[k_me], local.at[0])]
        own += [pltpu.make_async_copy(ins[w], outs[w].at[k_me], local.at[w]) for w in range(1, n)]

        @pl.when((s == 0) & (i == 0))
        def _():
            for j in range(2):
                for w in range(n):
                    over_ici(w, j, False).start()
            load = pltpu.make_async_copy(ins[0], wbuf, local.at[n])
            load.start()
            load.wait()
            for cp in own:
                cp.start()

        for j in range(N_CHIPS - 1):
            @pl.when((s == j + 1) & (i == 0))
            def _(j=j):
                for w in range(n):
                    over_ici(w, j, True).wait_recv()
                for w in range(n):
                    to_sibling(w, j, False).start()
                if j == 0:
                    for w in range(n):
                        over_ici(w, 2, False).start()
                    own[0].wait()
                for w in range(n):
                    to_sibling(w, j, True).wait_recv()
                load = pltpu.make_async_copy(outs[0].at[_chip_index(*chips[j])], wbuf, local.at[n])
                load.start()
                load.wait()

        rows = pl.ds(pl.multiple_of(i * rows_per_step, rows_per_step), rows_per_step)

        @pl.when(s == 0)
        def _():
            xhat, _ = _rms(x_ref[...])
            nrm = (xhat * g_ref[...]).astype(BF16)
            nbuf[rows, :] = nrm
            n_ref[...] = nrm

        z_ref[...] = _dot(nbuf[rows, :], wbuf[...])

        @pl.when((s == N_CHIPS - 1) & (i == n_tiles - 1))
        def _():
            for j in range(N_CHIPS - 1):
                for w in range(n):
                    over_ici(w, j, False).wait_send()
                    to_sibling(w, j, False).wait_send()
            for cp in own[1:]:
                cp.wait()

    once = lambda s, i, order: (jnp.where(s == 0, i, n_tiles - 1), 0)
    (z, n1, *stacked), job_outs = _fused_call(
        body, jobs, name="fwd_in", grid=(N_CHIPS, n_tiles), n_prefetch=1,
        in_specs=[pl.BlockSpec((rows_per_step, D_MODEL), once), _const((1, D_MODEL))] + [ANY] * n,
        out_specs=[pl.BlockSpec((rows_per_step, IN_SHARD), lambda s, i, order: (i, order[s])),
                   pl.BlockSpec((rows_per_step, D_MODEL), once)] + [ANY] * n,
        out_shape=[jax.ShapeDtypeStruct((t, D_IN), F32), jax.ShapeDtypeStruct((t, D_MODEL), BF16)]
        + [jax.ShapeDtypeStruct((N_CHIPS,) + s.shape, s.dtype) for s in shards],
        scratch_shapes=[pltpu.VMEM(shards[0].shape, BF16), pltpu.VMEM((t, D_MODEL), BF16),
                        pltpu.SemaphoreType.DMA((6 * n,)),
                        pltpu.SemaphoreType.DMA((6 * n,)), pltpu.SemaphoreType.DMA((n + 1,))],
        compiler_params=_params(2), jobs_start_after=(1, 0),
    )(order, x, g1, *shards)
    return (z, n1, stacked), job_outs


def _fwd_lru(z, conv_w, conv_b, wr, br, wi, bi, lam, jobs=()):
    t = z.shape[0]

    def body(xa_ref, ga_ref, cw_ref, cb_ref, wr_ref, br_ref, wi_ref, bi_ref, lam_ref, ya_ref, h_ref, xc_ref, r_ref,
             ig_ref, tail_ref, carry_ref):
        @pl.when(pl.program_id(0) == 0)
        def _():
            tail_ref[...] = jnp.zeros_like(tail_ref)
            carry_ref[...] = jnp.zeros_like(carry_ref)

        xa = xa_ref[...]
        xc, r, ig, a, mult = _lru_gates(xa, tail_ref[...], cw_ref, cb_ref, wr_ref, br_ref, wi_ref, bi_ref, lam_ref)
        tail_ref[...] = xa[SEQ_TILE - SUBLANES:]
        xc_ref[...], r_ref[...], ig_ref[...] = xc, r, ig
        h, carry = _scan_forward(a, xc * ig * mult, carry_ref[...])
        carry_ref[...] = carry
        h_ref[...] = h
        ya_ref[...] = (h * _gelu(ga_ref[...])).astype(BF16)

    tile = lambda j: pl.BlockSpec((SEQ_TILE, D_MODEL), lambda i: (i, j))
    return _fused_call(
        body, jobs, name="fwd_lru", grid=(t // SEQ_TILE,),
        in_specs=[tile(0), tile(1), _const((CONV_WIDTH, D_MODEL)), _const((1, D_MODEL)),
                  _resident((HEADS, HEAD_DIM, HEAD_DIM)), _const((1, D_MODEL)),
                  _resident((HEADS, HEAD_DIM, HEAD_DIM)), _const((1, D_MODEL)), _const((1, D_MODEL))],
        out_specs=[tile(0)] * 5,
        out_shape=[jax.ShapeDtypeStruct((t, D_MODEL), BF16)] + [jax.ShapeDtypeStruct((t, D_MODEL), F32)] * 4,
        scratch_shapes=[pltpu.VMEM((SUBLANES, D_MODEL), F32), pltpu.VMEM((1, D_MODEL), F32)],
        compiler_params=_params(),
    )(z, z, conv_w, conv_b, wr, br, wi, bi, lam)


def _sgu_forward_parts(ub, vb, lg_ref, lb_ref):
    u, du = _gelu_and_grad(ub)
    vg, dvg = _gelu_and_grad(vb)
    mu = jnp.mean(vg, axis=-1, keepdims=True)
    d = vg - mu
    rstd = lax.rsqrt(jnp.mean(d * d, axis=-1, keepdims=True) + LN_EPS)
    vhat = d * rstd
    vn = (vhat * lg_ref[...] + lb_ref[...]).astype(BF16)
    return u, du, dvg, rstd, vhat, vn


def _causal_mask():
    rows = lax.broadcasted_iota(jnp.int32, (CHUNK, CHUNK), 0)
    cols = lax.broadcasted_iota(jnp.int32, (CHUNK, CHUNK), 1)
    return rows >= cols


def _fwd_sgu(z, ln_g, ln_b, w_s, bias_full, jobs=()):
    t = z.shape[0]

    def body(ub_ref, vb_ref, lg_ref, lb_ref, ws_ref, bias_ref, yb_ref):
        u, _, _, _, _, vn = _sgu_forward_parts(ub_ref[...], vb_ref[...], lg_ref, lb_ref)
        mask = _causal_mask()
        wm = [jnp.where(mask, ws_ref[g], 0.0).astype(BF16) for g in range(GROUPS)]
        for c in range(SEQ_TILE // CHUNK):
            rows = slice(c * CHUNK, (c + 1) * CHUNK)
            for g in range(GROUPS):
                cols = slice(g * GROUP_DIM, (g + 1) * GROUP_DIM)
                sp = _dot(wm[g], vn[rows, cols]) + bias_ref[:, cols]
                yb_ref[rows, cols] = (u[rows, cols] * sp).astype(BF16)

    tile = lambda j: pl.BlockSpec((SEQ_TILE, D_MODEL), lambda i: (i, j))
    return _fused_call(
        body, jobs, name="fwd_sgu", grid=(t // SEQ_TILE,),
        in_specs=[tile(2), tile(3), _const((1, D_MODEL)), _const((1, D_MODEL)),
                  _const((GROUPS, CHUNK, CHUNK)), _const((CHUNK, D_MODEL))],
        out_specs=tile(0),
        out_shape=jax.ShapeDtypeStruct((t, D_MODEL), BF16),
        compiler_params=_params(),
    )(z, z, ln_g, ln_b, w_s, bias_full)


def _fwd_merge(ya, yb, z, x, w_oa, w_ob, w_out, g2, jobs=()):
    t = x.shape[0]

    def body(ya_ref, yb_ref, m_ref, x_ref, woa_ref, wob_ref, wout_ref, g_ref, pa_ref, pb_ref, h1_ref, n2_ref):
        pa = _dot(ya_ref[...], woa_ref[...])
        pb = _dot(yb_ref[...], wob_ref[...])
        pa_ref[...] = pa
        pb_ref[...] = pb
        merged = jax.nn.sigmoid(m_ref[:, :D_MODEL]) * pa + jax.nn.sigmoid(m_ref[:, D_MODEL:]) * pb
        h1 = x_ref[...] + _dot(merged.astype(BF16), wout_ref[...])
        h1_ref[...] = h1
        xhat, _ = _rms(h1)
        n2_ref[...] = (xhat * g_ref[...]).astype(BF16)

    tile = pl.BlockSpec((MM_TILE, D_MODEL), lambda i: (i, 0))
    sq = _resident((D_MODEL, D_MODEL))
    return _fused_call(
        body, jobs, name="fwd_merge", grid=(t // MM_TILE,),
        in_specs=[tile, tile, pl.BlockSpec((MM_TILE, 2 * D_MODEL), lambda i: (i, 2)), tile, sq, sq, sq,
                  _const((1, D_MODEL))],
        out_specs=[tile, tile, tile, tile],
        out_shape=[jax.ShapeDtypeStruct((t, D_MODEL), F32)] * 3 + [jax.ShapeDtypeStruct((t, D_MODEL), BF16)],
        compiler_params=_params(),
    )(ya, yb, z, x, w_oa, w_ob, w_out, g2)


def _mlp(n2, h1, target, w_up_st, w_down, g2, g3, jobs=()):
    t = n2.shape[0]

    def body(n2_ref, h1_ref, tgt_ref, wup_ref, wdown_ref, g2_ref, g3_ref, act_ref, dup_ref, dh2b_ref, dh1_ref,
             loss_ref, dg3_ref, dg2_ref, relu_ref):
        @pl.when(pl.program_id(0) == 0)
        def _():
            for ref in (loss_ref, dg3_ref, dg2_ref):
                ref[...] = jnp.zeros_like(ref)

        n2 = n2_ref[...]
        h1 = h1_ref[...]
        h2 = h1
        for k in range(N_CHIPS):
            cols = slice(k * D_MODEL, (k + 1) * D_MODEL)
            r = jnp.maximum(_dot(n2, wup_ref[k]), 0.0)
            relu_ref[:, cols] = r
            act = (r * r).astype(BF16)
            act_ref[:, cols] = act
            h2 = h2 + _dot(act, wdown_ref[cols, :])
        xhat, r3 = _rms(h2)
        diff = xhat * g3_ref[...] - tgt_ref[...]
        sq = jnp.sum(diff * diff, axis=1, keepdims=True)
        loss_ref[...] = loss_ref[...] + (0.5 / D_MODEL) * jnp.sum(sq, axis=0, keepdims=True)
        dy = diff * (1.0 / D_MODEL)
        dg3_ref[...] = dg3_ref[...] + _col_sum(dy * xhat)
        dh2 = _rms_bwd(dy * g3_ref[...], xhat, r3)
        dh2b = dh2.astype(BF16)
        dh2b_ref[...] = dh2b
        dn2 = jnp.zeros((SEQ_TILE, D_MODEL), F32)
        for k in range(N_CHIPS):
            cols = slice(k * D_MODEL, (k + 1) * D_MODEL)
            dup = (_dot_nt(dh2b, wdown_ref[cols, :]) * (2.0 * relu_ref[:, cols])).astype(BF16)
            dup_ref[:, cols] = dup
            dn2 = dn2 + _dot_nt(dup, wup_ref[k])
        xhat, r2 = _rms(h1)
        dg2_ref[...] = dg2_ref[...] + _col_sum(dn2 * xhat)
        dh1_ref[...] = dh2 + _rms_bwd(dn2 * g2_ref[...], xhat, r2)

    tile = pl.BlockSpec((SEQ_TILE, D_MODEL), lambda i: (i, 0))
    wide = pl.BlockSpec((SEQ_TILE, D_FF), lambda i: (i, 0))
    vec = _const((1, D_MODEL))
    vec_shape = jax.ShapeDtypeStruct((1, D_MODEL), F32)
    return _fused_call(
        body, jobs, name="mlp", grid=(t // SEQ_TILE,),
        in_specs=[tile, tile, tile, _resident((N_CHIPS, D_MODEL, D_MODEL)), _resident((D_FF, D_MODEL)), vec, vec],
        out_specs=[wide, wide, tile, tile, _const((SUBLANES, 128)), vec, vec],
        out_shape=[jax.ShapeDtypeStruct((t, D_FF), BF16), jax.ShapeDtypeStruct((t, D_FF), BF16),
                   jax.ShapeDtypeStruct((t, D_MODEL), BF16), jax.ShapeDtypeStruct((t, D_MODEL), F32),
                   jax.ShapeDtypeStruct((SUBLANES, 128), F32), vec_shape, vec_shape],
        scratch_shapes=[pltpu.VMEM((SEQ_TILE, D_FF), F32)],
        compiler_params=_params(),
    )(n2, h1, target, w_up_st, w_down, g2, g3)


def _bwd_mix(dh1, pa, pb, z, h, xc, r, ig, w_oa, w_ob, w_out, ln_g, ln_b, w_s, bias_full, conv_w, wr, wi, lam, jobs=()):
    t = dh1.shape[0]
    n_tiles = t // SEQ_TILE
    per_tile = SEQ_TILE // SUBLANES

    def merge_part(dh1_ref, pa_ref, pb_ref, m_ref, woa_ref, wob_ref, wout_ref, dz_ref, dya_ref, dyb_ref, mg_ref,
                   dpa_ref, dpb_ref, dh1b_ref):
        dh1b = dh1_ref[...].astype(BF16)
        dh1b_ref[...] = dh1b
        dm = _dot_nt(dh1b, wout_ref[...])
        pa = pa_ref[...]
        pb = pb_ref[...]
        sa = jax.nn.sigmoid(m_ref[:, :D_MODEL])
        sb = jax.nn.sigmoid(m_ref[:, D_MODEL:])
        mg_ref[...] = (sa * pa + sb * pb).astype(BF16)
        dz_ref[:, :D_MODEL] = (dm * pa * sa * (1.0 - sa)).astype(BF16)
        dz_ref[:, D_MODEL:] = (dm * pb * sb * (1.0 - sb)).astype(BF16)
        dpa = (dm * sa).astype(BF16)
        dpb = (dm * sb).astype(BF16)
        dpa_ref[...] = dpa
        dpb_ref[...] = dpb
        dya_ref[...] = _dot_nt(dpa, woa_ref[...])
        dyb_ref[...] = _dot_nt(dpb, wob_ref[...])

    def sgu_part(dyb_ref, ub_ref, vb_ref, lg_ref, lb_ref, ws_ref, bias_ref, dz_ref, dlg_ref, dlb_ref, dws_ref, dbs_ref,
                 dvn_ref, dsp_acc):
        i = pl.program_id(0)

        @pl.when(i == 0)
        def _():
            dlg_ref[...] = jnp.zeros_like(dlg_ref)
            dlb_ref[...] = jnp.zeros_like(dlb_ref)
            dws_ref[...] = jnp.zeros_like(dws_ref)
            dsp_acc[...] = jnp.zeros_like(dsp_acc)

        u, du, dvg, rstd, vhat, vn = _sgu_forward_parts(ub_ref[...], vb_ref[...], lg_ref, lb_ref)
        dyb = dyb_ref[...]
        mask = _causal_mask()
        wm = [jnp.where(mask, ws_ref[g], 0.0).astype(BF16) for g in range(GROUPS)]
        for c in range(SEQ_TILE // CHUNK):
            rows = slice(c * CHUNK, (c + 1) * CHUNK)
            for g in range(GROUPS):
                cols = slice(g * GROUP_DIM, (g + 1) * GROUP_DIM)
                vn_blk = vn[rows, cols]
                sp = _dot(wm[g], vn_blk) + bias_ref[:, cols]
                dyb_blk = dyb[rows, cols]
                dz_ref[rows, cols] = (dyb_blk * sp * du[rows, cols]).astype(BF16)
                dsp = dyb_blk * u[rows, cols]
                dsp_acc[:, cols] = dsp_acc[:, cols] + dsp
                dspb = dsp.astype(BF16)
                dvn_ref[rows, cols] = _dot_tn(wm[g], dspb)
                wcols = slice(g * CHUNK, (g + 1) * CHUNK)
                dws_ref[:, wcols] = dws_ref[:, wcols] + jnp.where(mask, _dot_nt(dspb, vn_blk), 0.0)
        dvn = dvn_ref[...]
        dlg_ref[...] = dlg_ref[...] + _col_sum(dvn * vhat)
        dlb_ref[...] = dlb_ref[...] + _col_sum(dvn)
        dvhat = dvn * lg_ref[...]
        dvgel = rstd * (dvhat - jnp.mean(dvhat, axis=-1, keepdims=True)
                        - vhat * jnp.mean(dvhat * vhat, axis=-1, keepdims=True))
        dz_ref[:, D_MODEL:] = (dvgel * dvg).astype(BF16)

        @pl.when(i == n_tiles - 1)
        def _():
            lane = lax.broadcasted_iota(jnp.int32, (CHUNK, 128), 1)
            out = jnp.zeros((CHUNK, 128), F32)
            for g in range(GROUPS):
                s = jnp.sum(dsp_acc[:, g * GROUP_DIM:(g + 1) * GROUP_DIM], axis=1, keepdims=True)
                out = out + jnp.where(lane == g, s, 0.0)
            dbs_ref[...] = out

    def lru_part(dya_ref, xa_ref, ga_ref, h_ref, h_prev_ref, xc_ref, r_ref, ig_ref, cw_ref, wr_ref, wi_ref, lam_ref,
                 dz_ref, dcw_ref, dcb_ref, dwr_ref, dbr_ref, dwi_ref, dbi_ref, dlam_ref, lam_carry, dxc_head):
        i = pl.program_id(0)

        @pl.when(i == 0)
        def _():
            for ref in (dcw_ref, dcb_ref, dwr_ref, dbr_ref, dwi_ref, dbi_ref, dlam_ref, lam_carry, dxc_head):
                ref[...] = jnp.zeros_like(ref)

        first_tile = i == n_tiles - 1
        h_tail = jnp.where(first_tile, 0.0, h_prev_ref[...])
        xc, r, ig = xc_ref[...], r_ref[...], ig_ref[...]
        xcb = xc.astype(BF16)
        sp, a, mult = _decay(r, lam_ref)
        h = h_ref[...]
        h_prev = _shift_down(h, h_tail, 1)
        dya = dya_ref[...]
        gg, dgg = _gelu_and_grad(ga_ref[...])
        dz_ref[:, D_MODEL:] = (dya * h * dgg).astype(BF16)
        ones = jnp.ones((SUBLANES, D_MODEL), F32)
        lam_t, lam_first = _scan_backward(_shift_up(a, ones, 1), dya * gg, lam_carry[...])
        lam_carry[...] = a[0:1] * lam_first
        dmult = lam_t * xc * ig
        dla = lam_t * h_prev * a - dmult * (a * a) / mult
        dr = dla * ((-LRU_C) * sp)
        dlam_ref[...] = dlam_ref[...] + _col_sum(dla * r) * (LRU_C * jax.nn.sigmoid(-lam_ref[...]))
        dpr = dr * r * (1.0 - r)
        dpi = lam_t * xc * mult * ig * (1.0 - ig)
        dbr_ref[...] = dbr_ref[...] + _col_sum(dpr)
        dbi_ref[...] = dbi_ref[...] + _col_sum(dpi)
        dprb = dpr.astype(BF16)
        dpib = dpi.astype(BF16)
        dxc_gate = []
        for hd in range(HEADS):
            cols = slice(hd * HEAD_DIM, (hd + 1) * HEAD_DIM)
            dxc_gate.append(_dot_nt(dprb[:, cols], wr_ref[hd]) + _dot_nt(dpib[:, cols], wi_ref[hd]))
            dwr_ref[hd] = dwr_ref[hd] + _dot_tn(xcb[:, cols], dprb[:, cols])
            dwi_ref[hd] = dwi_ref[hd] + _dot_tn(xcb[:, cols], dpib[:, cols])
        dxc = lam_t * ig * mult + jnp.concatenate(dxc_gate, axis=1)
        dcb_ref[...] = dcb_ref[...] + _col_sum(dxc)
        cw = cw_ref[...]
        head = dxc_head[...]
        xa = xa_ref[...]
        dxa = cw[0:1] * dxc
        dcw_ref[0:1, :] = dcw_ref[0:1, :] + _col_sum(dxc * xa)
        for k in range(1, CONV_WIDTH):
            dxc_k = _shift_up(dxc, head, k)
            dxa = dxa + cw[k:k + 1] * dxc_k
            dcw_ref[k:k + 1, :] = dcw_ref[k:k + 1, :] + _col_sum(dxc_k * xa)
        dxc_head[...] = dxc[0:SUBLANES]
        dz_ref[:, :D_MODEL] = dxa.astype(BF16)

    def body(dh1_ref, pa_ref, pb_ref, z_ref, h_ref, h_prev_ref, xc_ref, r_ref, ig_ref, woa_ref, wob_ref, wout_ref,
             lg_ref, lb_ref, ws_ref, bias_ref, cw_ref, wr_ref, wi_ref, lam_ref, dz_ref, mg_ref, dpa_ref, dpb_ref,
             dh1b_ref, dlg_ref, dlb_ref, dws_ref, dbs_ref, dcw_ref, dcb_ref, dwr_ref, dbr_ref, dwi_ref, dbi_ref,
             dlam_ref, dya_ref, dyb_ref, dvn_ref, dsp_acc, lam_carry, dxc_head):
        def cols(ref, first, count):
            return ref.at[:, pl.ds(first * D_MODEL, count * D_MODEL)]

        merge_part(dh1_ref, pa_ref, pb_ref, cols(z_ref, 4, 2), woa_ref, wob_ref, wout_ref, cols(dz_ref, 4, 2), dya_ref,
                   dyb_ref, mg_ref, dpa_ref, dpb_ref, dh1b_ref)
        sgu_part(dyb_ref, cols(z_ref, 2, 1), cols(z_ref, 3, 1), lg_ref, lb_ref, ws_ref, bias_ref, cols(dz_ref, 2, 2),
                 dlg_ref, dlb_ref, dws_ref, dbs_ref, dvn_ref, dsp_acc)
        lru_part(dya_ref, cols(z_ref, 0, 1), cols(z_ref, 1, 1), h_ref, h_prev_ref, xc_ref, r_ref, ig_ref, cw_ref, wr_ref,
                 wi_ref, lam_ref, cols(dz_ref, 0, 2), dcw_ref, dcb_ref, dwr_ref, dbr_ref, dwi_ref, dbi_ref, dlam_ref,
                 lam_carry, dxc_head)

    rev = lambda i: n_tiles - 1 - i
    tile = pl.BlockSpec((SEQ_TILE, D_MODEL), lambda i: (rev(i), 0))
    row = pl.BlockSpec((SEQ_TILE, D_IN), lambda i: (rev(i), 0))
    prev8 = pl.BlockSpec((SUBLANES, D_MODEL), lambda i: (jnp.maximum(rev(i) * per_tile - 1, 0), 0))
    vec = _const((1, D_MODEL))
    sq = _resident((D_MODEL, D_MODEL))
    gate_w = _resident((HEADS, HEAD_DIM, HEAD_DIM))
    gate_acc = _const((HEADS, HEAD_DIM, HEAD_DIM))
    vec_shape = jax.ShapeDtypeStruct((1, D_MODEL), F32)
    gate_shape = jax.ShapeDtypeStruct((HEADS, HEAD_DIM, HEAD_DIM), F32)
    act_bf = jax.ShapeDtypeStruct((t, D_MODEL), BF16)
    return _fused_call(
        body, jobs, name="bwd_mix", grid=(n_tiles,),
        in_specs=[tile, tile, tile, row, tile, prev8, tile, tile, tile, sq, sq, sq, vec, vec,
                  _const((GROUPS, CHUNK, CHUNK)), _const((CHUNK, D_MODEL)), _const((CONV_WIDTH, D_MODEL)), gate_w, gate_w,
                  vec],
        out_specs=[row, tile, tile, tile, tile, vec, vec, _const((CHUNK, GROUPS * CHUNK)), _const((CHUNK, 128)),
                   _const((SUBLANES, D_MODEL)), vec, gate_acc, vec, gate_acc, vec, vec],
        out_shape=[jax.ShapeDtypeStruct((t, D_IN), BF16), act_bf, act_bf, act_bf, act_bf, vec_shape, vec_shape,
                   jax.ShapeDtypeStruct((CHUNK, GROUPS * CHUNK), F32), jax.ShapeDtypeStruct((CHUNK, 128), F32),
                   jax.ShapeDtypeStruct((SUBLANES, D_MODEL), F32), vec_shape, gate_shape, vec_shape, gate_shape,
                   vec_shape, vec_shape],
        scratch_shapes=[pltpu.VMEM((SEQ_TILE, D_MODEL), F32), pltpu.VMEM((SEQ_TILE, D_MODEL), F32),
                        pltpu.VMEM((SEQ_TILE, D_MODEL), F32), pltpu.VMEM((CHUNK, D_MODEL), F32),
                        pltpu.VMEM((1, D_MODEL), F32), pltpu.VMEM((SUBLANES, D_MODEL), F32)],
        compiler_params=_params(),
    )(dh1, pa, pb, z, h, h, xc, r, ig, w_oa, w_ob, w_out, ln_g, ln_b, w_s, bias_full, conv_w, wr, wi, lam)


def _bwd_in(dz, x, dh1, w_in_st, g1, jobs=()):
    t = x.shape[0]

    def body(dz_ref, x_ref, dh1_ref, w_ref, g_ref, dx_ref, dg1_ref):
        @pl.when(pl.program_id(0) == 0)
        def _():
            dg1_ref[...] = jnp.zeros_like(dg1_ref)

        dn1 = jnp.zeros((MM_TILE, D_MODEL), F32)
        for k in range(N_CHIPS):
            dn1 = dn1 + _dot_nt(dz_ref[:, k * IN_SHARD:(k + 1) * IN_SHARD], w_ref[k])
        xhat, r1 = _rms(x_ref[...])
        dg1_ref[...] = dg1_ref[...] + _col_sum(dn1 * xhat)
        dx_ref[...] = dh1_ref[...] + _rms_bwd(dn1 * g_ref[...], xhat, r1)

    tile = pl.BlockSpec((MM_TILE, D_MODEL), lambda i: (i, 0))
    return _fused_call(
        body, jobs, name="bwd_in", grid=(t // MM_TILE,),
        in_specs=[pl.BlockSpec((MM_TILE, D_IN), lambda i: (i, 0)), tile, tile,
                  _resident((N_CHIPS, D_MODEL, IN_SHARD)), _const((1, D_MODEL))],
        out_specs=[tile, _const((1, D_MODEL))],
        out_shape=[jax.ShapeDtypeStruct((t, D_MODEL), F32), jax.ShapeDtypeStruct((1, D_MODEL), F32)],
        compiler_params=_params(),
    )(dz, x, dh1, w_in_st, g1)


def _weight_grad(name, a, b, n_blocks, a_varies, b_varies, width, jobs=()):
    t = a.shape[0]
    rows = min(DW_TILE, t)
    n_t = t // rows

    def body(a_ref, b_ref, o_ref, acc_ref):
        s = pl.program_id(1)
        part = _dot_tn(a_ref[...], b_ref[...])

        @pl.when(s == 0)
        def _():
            acc_ref[...] = part

        @pl.when(s > 0)
        def _():
            acc_ref[...] = acc_ref[...] + part

        @pl.when(s == n_t - 1)
        def _():
            o_ref[...] = acc_ref[...].astype(BF16)

    return _fused_call(
        body, jobs, name=name, grid=(n_blocks, n_t),
        in_specs=[pl.BlockSpec((rows, D_MODEL), (lambda j, s: (s, j)) if a_varies else (lambda j, s: (s, 0))),
                  pl.BlockSpec((rows, width), (lambda j, s: (s, j)) if b_varies else (lambda j, s: (s, 0)))],
        out_specs=pl.BlockSpec((None, D_MODEL, width), lambda j, s: (j, 0, 0)),
        out_shape=jax.ShapeDtypeStruct((n_blocks, D_MODEL, width), BF16),
        scratch_shapes=[pltpu.VMEM((D_MODEL, width), F32)],
        compiler_params=_params(2),
    )(a, b)


def _place():
    x, y, c = lax.axis_index("x"), lax.axis_index("y"), lax.axis_index("c")
    other_chips = [(1 - x, y), (x, 1 - y), (1 - x, 1 - y)]
    return x, y, c, other_chips


def _chip_index(px, py):
    return 2 * px + py


ANY = pl.BlockSpec(memory_space=pl.ANY)
SIBLING = ((0, 0, 1),)
NEIGHBOURS = ((1, 0, 0), (0, 1, 0))
OTHER_CHIPS = NEIGHBOURS + ((1, 1, 0),)


def _comm_call(name, jobs):
    return _fused_call(None, jobs, name=name, grid=(), in_specs=[], out_specs=[], out_shape=[])()[1]


def _near_far(x, y, c):
    return (x ^ (1 - c), y ^ c), (x ^ c, y ^ (1 - c))


def _gather_near_job(shards):
    n = len(shards)
    halves = [s.shape[0] // 2 for s in shards]

    def copies(ins, outs, send, recv, local):
        x, y, c, _ = _place()
        near, _ = _near_far(x, y, c)

        def block(w, chip, pc):
            return outs[w].at[_chip_index(*chip), pl.ds(pc * halves[w], halves[w]), :]

        def copy(w, k, chip, pc, to, src=None):
            return pltpu.make_async_remote_copy(
                src_ref=block(w, chip, pc) if src is None else src, dst_ref=block(w, chip, pc),
                send_sem=send.at[2 * w + k], recv_sem=recv.at[2 * w + k], device_id=to, device_id_type=MESH)

        sends, arrivals, own = [], [], []
        for w in range(n):
            src = ins[w].at[pl.ds(c * halves[w], halves[w]), :]
            own.append(pltpu.make_async_copy(src, block(w, (x, y), c), local.at[w]))
            sends += [copy(w, 0, (x, y), c, (*near, c), src), copy(w, 1, (x, y), c, (x, y, 1 - c), src)]
            arrivals += [copy(w, 0, near, c, (x, y, c)), copy(w, 1, (x, y), 1 - c, (x, y, c))]
        return sends, arrivals, own

    return _Job(shards, [jax.ShapeDtypeStruct((N_CHIPS,) + s.shape, s.dtype) for s in shards], 2 * n, copies,
                NEIGHBOURS + SIBLING, n_local=n)


def _gather_far_job(stacked):
    n = len(stacked)
    halves = [s.shape[1] // 2 for s in stacked]

    def copies(ins, outs, send, recv, local):
        del ins, local
        x, y, c, _ = _place()
        near, far = _near_far(x, y, c)

        def copy(w, k, chip):
            blk = outs[w].at[_chip_index(*chip), pl.ds(c * halves[w], halves[w]), :]
            return pltpu.make_async_remote_copy(
                src_ref=blk, dst_ref=blk, send_sem=send.at[2 * w + k], recv_sem=recv.at[2 * w + k],
                device_id=(*far, c), device_id_type=MESH)

        sends = [copy(w, k, chip) for w in range(n) for k, chip in enumerate(((x, y), near))]
        arrivals = [copy(w, k, chip) for w in range(n) for k, chip in enumerate((far, (1 - x, 1 - y)))]
        return sends, arrivals, []

    return _Job(stacked, [jax.ShapeDtypeStruct(s.shape, s.dtype) for s in stacked], 2 * n, copies, NEIGHBOURS,
                aliases={w: w for w in range(n)})


def _gather_pass_job(stacked):
    n = len(stacked)
    halves = [s.shape[1] // 2 for s in stacked]

    def copies(ins, outs, send, recv, local):
        del ins, local
        x, y, c, chips = _place()

        def copy(w, j, chip, pc, to):
            blk = outs[w].at[_chip_index(*chip), pl.ds(pc * halves[w], halves[w]), :]
            return pltpu.make_async_remote_copy(
                src_ref=blk, dst_ref=blk, send_sem=send.at[3 * w + j], recv_sem=recv.at[3 * w + j], device_id=to,
                device_id_type=MESH)

        sends = [copy(w, j, chip, c, (x, y, 1 - c)) for w in range(n) for j, chip in enumerate(chips)]
        arrivals = [copy(w, j, chip, 1 - c, (x, y, c)) for w in range(n) for j, chip in enumerate(chips)]
        return sends, arrivals, []

    return _Job(stacked, [jax.ShapeDtypeStruct(s.shape, s.dtype) for s in stacked], 3 * n, copies, SIBLING,
                aliases={w: w for w in range(n)})


def _gather_small_job(block):
    def copies(ins, outs, send, recv, local):
        x, y, c, chips = _place()

        def copy(j, chip_from, to):
            return pltpu.make_async_remote_copy(
                src_ref=ins[0], dst_ref=outs[0].at[_chip_index(*chip_from)], send_sem=send.at[j],
                recv_sem=recv.at[j], device_id=to, device_id_type=MESH)

        own = [pltpu.make_async_copy(ins[0], outs[0].at[_chip_index(x, y)], local.at[0])]
        sends = [copy(j, (x, y), (*chip, c)) for j, chip in enumerate(chips)]
        arrivals = [copy(j, chip, (x, y, c)) for j, chip in enumerate(chips)]
        return sends, arrivals, own

    return _Job([block], [jax.ShapeDtypeStruct((N_CHIPS,) + block.shape, block.dtype)], 3, copies, OTHER_CHIPS,
                n_local=1)


def _pair_send_job(grads):
    n = len(grads)
    halves = [g.shape[1] // 2 for g in grads]

    def copies(ins, outs, send, recv, local):
        del local
        x, y, c, _ = _place()
        sends = [pltpu.make_async_remote_copy(
            src_ref=ins[w].at[:, pl.ds((1 - c) * halves[w], halves[w]), :], dst_ref=outs[w], send_sem=send.at[w],
            recv_sem=recv.at[w], device_id=(x, y, 1 - c), device_id_type=MESH) for w in range(n)]
        return sends, sends, []

    return _Job(grads, [jax.ShapeDtypeStruct((N_CHIPS, h, g.shape[2]), g.dtype) for g, h in zip(grads, halves)], n,
                copies, SIBLING)


def _row_block(rows, limit=256):
    return min(rows, limit)


def _pair_add(name, core, mine, theirs):
    _, _, h, cols = mine.shape
    rb = _row_block(h, 512)

    def body(core_ref, a_ref, b_ref, o_ref):
        del core_ref
        o_ref[...] = (a_ref[...].astype(F32) + b_ref[...].astype(F32)).astype(BF16)

    return pl.pallas_call(
        body, name=name,
        grid_spec=pltpu.PrefetchScalarGridSpec(
            num_scalar_prefetch=1, grid=(N_CHIPS, h // rb),
            in_specs=[pl.BlockSpec((None, None, rb, cols), lambda k, r, core_ref: (k, core_ref[0], r, 0)),
                      pl.BlockSpec((None, rb, cols), lambda k, r, core_ref: (k, r, 0))],
            out_specs=pl.BlockSpec((None, rb, cols), lambda k, r, core_ref: (k, r, 0))),
        out_shape=jax.ShapeDtypeStruct(theirs.shape, BF16),
        compiler_params=_params(2),
    )(core, mine, theirs)


def _sequencer_call(name, collective_id, job, then=()):
    steps = [job]
    for make in then:
        steps.append(make(steps[-1].out_shape))
    peers = sorted(set(p for step in steps for p in step.peers))
    ins = [jax.new_ref(a, memory_space=pltpu.MemorySpace.HBM) for a in job.inputs]
    outs = [ins[{o: i for i, o in job.aliases.items()}[k]] if k in job.aliases.values()
            else jax.empty_ref(shape, memory_space=pltpu.MemorySpace.HBM) for k, shape in enumerate(job.out_shape)]
    sems = [pltpu.SemaphoreType.DMA((n,)) for step in steps for n in (step.n_sem, step.n_sem, max(step.n_local, 1))]

    @pl.kernel(mesh=plsc.ScalarSubcoreMesh(axis_name="sequencer", num_cores=1), name=name, scratch_types=tuple(sems),
               compiler_params=pltpu.CompilerParams(collective_id=collective_id))
    def launch(*sem_refs):
        x, y, c, _ = _place()
        barrier = pltpu.get_barrier_semaphore()
        for dx, dy, dc in peers:
            pl.semaphore_signal(barrier, inc=1, device_id=(x ^ dx, y ^ dy, c ^ dc), device_id_type=MESH)
        pl.semaphore_wait(barrier, len(peers))
        for k, step in enumerate(steps):
            sends, arrivals, own = step.copies(ins if k == 0 else outs, outs, *sem_refs[3 * k:3 * k + 3])
            for cp in own + sends:
                cp.start()
            for cp in arrivals:
                cp.wait_recv()
            for cp in sends:
                cp.wait_send()
            for cp in own:
                cp.wait()

    launch()
    return [ref[...] for ref in outs]


def _chip_exchange_job(sums):
    n = len(sums)

    def copies(ins, outs, send, recv, local):
        del local
        _, _, c, chips = _place()
        sends = [pltpu.make_async_remote_copy(
            src_ref=ins[w].at[_chip_index(*chip)], dst_ref=outs[w].at[j], send_sem=send.at[3 * w + j],
            recv_sem=recv.at[3 * w + j], device_id=(*chip, c), device_id_type=MESH)
            for w in range(n) for j, chip in enumerate(chips)]
        return sends, sends, []

    return _Job(sums, [jax.ShapeDtypeStruct((N_CHIPS - 1,) + s.shape[1:], s.dtype) for s in sums], 3 * n, copies,
                OTHER_CHIPS)


def _chip_sum(name, place, mine, theirs):
    _, h, cols = mine.shape
    rb = _row_block(h, 512)

    def body(place_ref, p_ref, q_ref, o_ref):
        del place_ref
        acc = p_ref[...].astype(F32)
        for j in range(N_CHIPS - 1):
            acc = acc + q_ref[j].astype(F32)
        o_ref[...] = acc

    return pl.pallas_call(
        body, name=name,
        grid_spec=pltpu.PrefetchScalarGridSpec(
            num_scalar_prefetch=1, grid=(h // rb,),
            in_specs=[pl.BlockSpec((None, rb, cols), lambda r, place_ref: (place_ref[0], r, 0)),
                      pl.BlockSpec((N_CHIPS - 1, rb, cols), lambda r, place_ref: (0, r, 0))],
            out_specs=pl.BlockSpec((None, rb, cols), lambda r, place_ref: (place_ref[1], r, 0))),
        out_shape=jax.ShapeDtypeStruct((2, h, cols), F32),
        compiler_params=_params(),
    )(place, mine, theirs)


def _share_job(bufs):
    n = len(bufs)

    def copies(ins, outs, send, recv, local):
        del ins, local
        x, y, c, _ = _place()

        def copy(w, half):
            return pltpu.make_async_remote_copy(
                src_ref=outs[w].at[half], dst_ref=outs[w].at[half], send_sem=send.at[w], recv_sem=recv.at[w],
                device_id=(x, y, 1 - c), device_id_type=MESH)

        return [copy(w, c) for w in range(n)], [copy(w, 1 - c) for w in range(n)], []

    return _Job(bufs, [jax.ShapeDtypeStruct(b.shape, b.dtype) for b in bufs], n, copies, SIBLING,
                aliases={w: w for w in range(n)})


SMALL_ROWS = 24
ROW_G1, ROW_CW, ROW_CB, ROW_BR, ROW_BI, ROW_LAM, ROW_LG, ROW_LB, ROW_G2, ROW_G3, ROW_LOSS, ROW_BS = (
    0, 1, 5, 6, 7, 8, 9, 10, 11, 12, 13, 16)
N_DEV = 8


def _pack_small(dcw, dcb, dbr, dbi, dlam, dlg, dlb, dg2, dg3, loss, dbs):
    def body(dcw_ref, dcb_ref, dbr_ref, dbi_ref, dlam_ref, dlg_ref, dlb_ref, dg2_ref, dg3_ref, loss_ref, dbs_ref, out):
        out[...] = jnp.zeros((SMALL_ROWS, D_MODEL), F32)
        for row, ref in ((ROW_CB, dcb_ref), (ROW_BR, dbr_ref), (ROW_BI, dbi_ref), (ROW_LAM, dlam_ref),
                         (ROW_LG, dlg_ref), (ROW_LB, dlb_ref), (ROW_G2, dg2_ref), (ROW_G3, dg3_ref)):
            out[row:row + 1, :] = ref[...]
        out[ROW_CW:ROW_CW + CONV_WIDTH, :] = dcw_ref[0:CONV_WIDTH, :]
        out[ROW_LOSS:ROW_LOSS + 1, 0:128] = loss_ref[0:1, :]
        out[ROW_BS:ROW_BS + GROUPS, 0:128] = jnp.transpose(dbs_ref[...])[0:GROUPS, :]

    vm = pl.BlockSpec(memory_space=pltpu.VMEM)
    return pl.pallas_call(
        body, name="pack_small", in_specs=[vm] * 11, out_specs=vm,
        out_shape=jax.ShapeDtypeStruct((SMALL_ROWS, D_MODEL), F32),
    )(dcw, dcb, dbr, dbi, dlam, dlg, dlb, dg2, dg3, loss, dbs)


def _gather_all_job(blocks):
    n = len(blocks)
    flips = [(dx, dy, dc) for dx in (0, 1) for dy in (0, 1) for dc in (0, 1)][1:]

    def copies(ins, outs, send, recv, local):
        x, y, c, _ = _place()
        me = 4 * x + 2 * y + c
        sends, arrivals, own = [], [], []
        for w in range(n):
            own.append(pltpu.make_async_copy(ins[w], outs[w].at[me], local.at[w]))
            for k, (dx, dy, dc) in enumerate(flips):
                peer = (x ^ dx, y ^ dy, c ^ dc)
                sem = dict(send_sem=send.at[7 * w + k], recv_sem=recv.at[7 * w + k])
                sends.append(pltpu.make_async_remote_copy(
                    src_ref=ins[w], dst_ref=outs[w].at[me], device_id=peer, device_id_type=MESH, **sem))
                arrivals.append(pltpu.make_async_remote_copy(
                    src_ref=ins[w], dst_ref=outs[w].at[4 * peer[0] + 2 * peer[1] + peer[2]], device_id=peer,
                    device_id_type=MESH, **sem))
        return sends, arrivals, own

    return _Job(blocks, [jax.ShapeDtypeStruct((N_DEV,) + b.shape, b.dtype) for b in blocks], 7 * n, copies,
                OTHER_CHIPS + SIBLING + tuple((dx, dy, 1) for dx, dy, _ in OTHER_CHIPS), n_local=n)


def _sum_small(vec_all, ws_all, dg1_all):
    def body(vec_ref, ws_ref, dg1_ref, vec_out, ws_out):
        vec, ws, dg1 = vec_ref[0], ws_ref[0], dg1_ref[0]
        for d in range(1, N_DEV):
            vec, ws, dg1 = vec + vec_ref[d], ws + ws_ref[d], dg1 + dg1_ref[d]
        vec_out[...] = vec
        vec_out[ROW_G1:ROW_G1 + 1, :] = dg1
        ws_out[...] = ws

    vm = pl.BlockSpec(memory_space=pltpu.VMEM)
    return pl.pallas_call(
        body, name="sum_small", in_specs=[vm] * 3, out_specs=[vm, vm],
        out_shape=[jax.ShapeDtypeStruct(vec_all.shape[1:], F32), jax.ShapeDtypeStruct(ws_all.shape[1:], F32)],
    )(vec_all, ws_all, dg1_all)


def _adamw_math(w, g, m, v):
    m = ADAM_B1 * m + (1.0 - ADAM_B1) * g
    v = ADAM_B2 * v + (1.0 - ADAM_B2) * (g * g)
    m_hat = m / (1.0 - ADAM_B1 ** ADAM_STEP)
    v_hat = v / (1.0 - ADAM_B2 ** ADAM_STEP)
    delta = (-ADAM_LR) * (m_hat / (jnp.sqrt(v_hat) + ADAM_EPS) + ADAM_WD * w)
    return delta, m, v


def _adamw(name, g, w, m, v, jobs=()):
    rows, cols = w.shape
    rb = _row_block(rows)

    def body(g_ref, w_ref, m_ref, v_ref, d_ref, nm_ref, nv_ref):
        d_ref[...], nm_ref[...], nv_ref[...] = _adamw_math(w_ref[...], g_ref[...], m_ref[...], v_ref[...])

    blk = pl.BlockSpec((rb, cols), lambda r: (r, 0))
    return _fused_call(
        body, jobs, name=name, grid=(rows // rb,), in_specs=[blk] * 4, out_specs=[blk] * 3,
        out_shape=[jax.ShapeDtypeStruct(w.shape, F32)] * 3, compiler_params=_params(),
    )(g, w, m, v)


def _adamw_small(grads, ws, ms, vs):
    n = len(grads)

    def body(*refs):
        g_refs, w_refs, m_refs, v_refs = refs[:n], refs[n:2 * n], refs[2 * n:3 * n], refs[3 * n:4 * n]
        outs = refs[4 * n:]
        for p in range(n):
            d, nm, nv = _adamw_math(w_refs[p][...], g_refs[p][...], m_refs[p][...], v_refs[p][...])
            outs[p][...] = d
            outs[n + p][...] = nm
            outs[2 * n + p][...] = nv

    vm = pl.BlockSpec(memory_space=pltpu.VMEM)
    shapes = [jax.ShapeDtypeStruct(w.shape, F32) for w in ws]
    out = pl.pallas_call(
        body, name="adamw_small", in_specs=[vm] * (4 * n), out_specs=[vm] * (3 * n), out_shape=shapes * 3,
    )(*grads, *ws, *ms, *vs)
    return out[:n], out[n:2 * n], out[2 * n:]


def _unstack_heads(w_st):
    per = HEAD_DIM // N_CHIPS
    return w_st.reshape(N_CHIPS, HEADS, per, HEAD_DIM).transpose(1, 0, 2, 3).reshape(HEADS, HEAD_DIM, HEAD_DIM)


def _stack_heads(w):
    per = HEAD_DIM // N_CHIPS
    return w.reshape(HEADS, N_CHIPS, per, HEAD_DIM).transpose(1, 0, 2, 3).reshape(N_CHIPS, HEADS * per, HEAD_DIM)


def kernel(x, norm_mix_g, w_in, conv_w, conv_b, w_rgate, b_rgate, w_igate, b_igate, lru_lambda, w_out_a, sgu_ln_g, sgu_ln_b, sgu_w_s, sgu_b_s, w_out_b, w_out, norm_mlp_g, w_up, w_down, norm_final_g, loss_target, m_norm_mix_g, m_w_in, m_conv_w, m_conv_b, m_w_rgate, m_b_rgate, m_w_igate, m_b_igate, m_lru_lambda, m_w_out_a, m_sgu_ln_g, m_sgu_ln_b, m_sgu_w_s, m_sgu_b_s, m_w_out_b, m_w_out, m_norm_mlp_g, m_w_up, m_w_down, m_norm_final_g, v_norm_mix_g, v_w_in, v_conv_w, v_conv_b, v_w_rgate, v_b_rgate, v_w_igate, v_b_igate, v_lru_lambda, v_w_out_a, v_sgu_ln_g, v_sgu_ln_b, v_sgu_w_s, v_sgu_b_s, v_w_out_b, v_w_out, v_norm_mlp_g, v_w_up, v_w_down, v_norm_final_g):
    chip = _chip_index(lax.axis_index("x"), lax.axis_index("y"))
    core = lax.axis_index("c")
    quarter_h = HEAD_DIM // N_CHIPS
    quarter_d = D_MODEL // N_CHIPS

    as_2d = lambda a: a.reshape(-1, a.shape[-1])
    big_w = [as_2d(w) for w in (w_in, w_rgate, w_igate, w_out_a, w_out_b, w_out, w_up, w_down)]
    big_m = [as_2d(w) for w in (m_w_in, m_w_rgate, m_w_igate, m_w_out_a, m_w_out_b, m_w_out, m_w_up, m_w_down)]
    big_v = [as_2d(w) for w in (v_w_in, v_w_rgate, v_w_igate, v_w_out_a, v_w_out_b, v_w_out, v_w_up, v_w_down)]

    packed = jnp.concatenate([conv_w[0], b_rgate[0], b_igate[0]], axis=1)
    packed = jnp.concatenate([packed, jnp.zeros_like(packed)], axis=0)
    s_in, s_r, s_i, s_oa, s_ob, s_out, s_up, s_down = [w.astype(BF16) for w in big_w]
    xs, target = x[0], loss_target[0]
    g3 = norm_final_g.reshape(1, D_MODEL)
    bias_s = jnp.broadcast_to(jnp.transpose(sgu_b_s[0])[:, :, None], (CHUNK, GROUPS, GROUP_DIM)).reshape(CHUNK, D_MODEL)
    core_arr = core.reshape(1).astype(jnp.int32)
    place = jnp.stack([chip, core]).astype(jnp.int32)
    quarter = lambda g: g.reshape(N_CHIPS, D_MODEL // N_CHIPS, D_MODEL)

    def pair_add(nm, g, from_sibling):
        return _pair_add("pair_add_" + nm, core_arr, g.reshape(N_CHIPS, 2, g.shape[1] // 2, g.shape[2]), from_sibling)

    def chip_sum(nm, pair, from_chips):
        return _chip_sum("chip_sum_" + nm, place, pair, from_chips)

    order = jnp.stack([chip, chip ^ 2, chip ^ 1, chip ^ 3]).astype(jnp.int32)
    (z, n1, (w_in_st, wr_st, wi_st)), ((packed_all,), late) = _fwd_in(
        xs, norm_mix_g, [s_in, s_r, s_i], order,
        jobs=[_gather_small_job(packed), _gather_near_job([s_oa, s_ob, s_out])])
    pick = lambda lo, hi: packed_all[:, :HEADS, lo:hi].transpose(1, 0, 2).reshape(HEADS, -1)
    conv_w_full = pick(0, quarter_d)
    br_full = pick(quarter_d, quarter_d + quarter_h).reshape(1, D_MODEL)
    bi_full = pick(quarter_d + quarter_h, quarter_d + 2 * quarter_h).reshape(1, D_MODEL)
    wr, wi = _unstack_heads(wr_st), _unstack_heads(wi_st)
    lru = (conv_w_full, conv_b, wr, br_full, wi, bi_full, lru_lambda)
    sgu = (sgu_ln_g, sgu_ln_b, sgu_w_s[0], bias_s)

    after = lambda arrays, result: lax.optimization_barrier((arrays, result))[0]
    w_up_st, w_dn = _sequencer_call("gather_mlp", 8, _gather_near_job(after([s_up, s_down], n1)),
                                    then=(_gather_far_job, _gather_pass_job))
    w_dn = w_dn.reshape(D_FF, D_MODEL)
    (ya, *saved), (late,) = _fwd_lru(z, *lru, jobs=[_gather_far_job(late)])
    yb, (late,) = _fwd_sgu(z, *sgu, jobs=[_gather_pass_job(late)])
    w_oa, w_ob, w_o = [w.reshape(D_MODEL, D_MODEL) for w in late]
    (pa, pb, h1, n2), _ = _fwd_merge(ya, yb, z, xs, w_oa, w_ob, w_o, norm_mlp_g)
    (act, dup, dh2b, dh1, loss_part, dg3, dg2), _ = _mlp(n2, h1, target, w_up_st, w_dn, norm_mlp_g, g3)

    d_down, _ = _weight_grad("dw_down", act, dh2b, N_CHIPS, True, False, D_MODEL)
    r_down, = _sequencer_call("send_w_down", 10, _pair_send_job([d_down]))
    d_up, _ = _weight_grad("dw_up", n2, dup, N_CHIPS, False, True, D_MODEL)
    r_up, = _sequencer_call("send_w_up", 11, _pair_send_job([d_up]))
    p_down, p_up = pair_add("w_down", d_down, r_down), pair_add("w_up", d_up, r_up)
    q_up, q_down = _sequencer_call("exchange_mlp", 13, _chip_exchange_job([p_up, p_down]))
    (dz, merged, dpa, dpb, dh1b, dlg, dlb, dws, dbs, dcw, dcb, dwr, dbr, dwi, dbi, dlam), _ = _bwd_mix(
        after(dh1, [p_up, p_down]), pa, pb, z, *saved, w_oa, w_ob, w_o, *sgu, conv_w_full, wr, wi, lru_lambda)
    half_up, half_down = chip_sum("w_up", p_up, q_up), chip_sum("w_down", p_down, q_down)
    full_up, full_down = _sequencer_call("share_mlp", 14, _share_job([half_up, half_down]))
    names = ("w_in", "w_rgate", "w_igate", "w_out_a", "w_out_b", "w_out", "w_up", "w_down")
    d_out, _ = _weight_grad("dw_out", merged, dh1b, 1, False, False, D_MODEL)
    d_oa, _ = _weight_grad("dw_out_a", ya, dpa, 1, False, False, D_MODEL)
    d_ob, _ = _weight_grad("dw_out_b", yb, dpb, 1, False, False, D_MODEL)
    mids = [quarter(d_oa), quarter(d_ob), quarter(d_out)]
    r_mids = _sequencer_call("send_mids", 1, _pair_send_job(mids))
    gates = [_stack_heads(dwr).astype(BF16), _stack_heads(dwi).astype(BF16)]
    small = _pack_small(dcw, dcb, dbr, dbi, dlam, dlg, dlb, dg2, dg3, loss_part, dbs)
    p_mids = [pair_add(nm, g, r) for nm, g, r in zip(names[3:6], mids, r_mids)]
    q_mids = _sequencer_call("exchange_mids", 2, _chip_exchange_job(p_mids))
    r_gates = _sequencer_call("send_gates", 15, _pair_send_job(gates))
    vec_all, ws_all = _sequencer_call("gather_small_grads", 16, _gather_all_job([small, dws]))
    d_in, _ = _weight_grad("dw_in", after(n1, p_mids + [small]), dz, N_CHIPS, False, True, IN_SHARD)
    r_in, = _sequencer_call("send_w_in", 3, _pair_send_job([d_in]))
    adam_args = {nm: (w, m, v) for nm, w, m, v in zip(names, big_w, big_m, big_v)}

    def adamw(nm, g):
        w, m, v = adam_args[nm]
        g = g.reshape(w.shape)
        return g, _adamw("adamw_" + nm, g, w, m, v)[0]

    p_gates = [pair_add(nm, g, r) for nm, g, r in zip(names[1:3], gates, r_gates)]
    half_mids = [chip_sum(nm, p, q) for nm, p, q in zip(names[3:6], p_mids, q_mids)]
    full_mids = _sequencer_call("share_mids", 12, _share_job(half_mids))
    p_first = [pair_add("w_in", d_in, r_in)] + p_gates
    q_first = _sequencer_call("exchange_w_in", 4, _chip_exchange_job(p_first))
    (grad_x, dg1), _ = _bwd_in(dz, xs, after(dh1, p_first), w_in_st, norm_mix_g)
    dg1_all, = _sequencer_call("gather_dg1", 6, _gather_all_job([dg1]))
    done = {nm: adamw(nm, f) for nm, f in zip(("w_up", "w_down") + names[3:6], [full_up, full_down] + full_mids)}
    q_first = after(q_first, [out[0] for _, out in done.values()])
    half_first = [chip_sum(nm, p, q) for nm, p, q in zip(names[:3], p_first, q_first)]
    full_first = _sequencer_call("share_last", 5, _share_job(half_first))
    done.update({nm: adamw(nm, f) for nm, f in zip(names[:3], full_first)})
    full, big_out = [done[nm][0] for nm in names], [done[nm][1] for nm in names]

    vec, ws_sum = _sum_small(vec_all, ws_all, dg1_all)
    row = lambda r: vec[r:r + 1]
    shard = lambda a, width: lax.dynamic_slice_in_dim(a, chip * width, width, axis=1)
    g_small = dict(
        norm_mix_g=row(ROW_G1), conv_w=shard(vec[ROW_CW:ROW_CW + CONV_WIDTH], quarter_d), conv_b=row(ROW_CB),
        b_rgate=shard(row(ROW_BR).reshape(HEADS, HEAD_DIM), quarter_h),
        b_igate=shard(row(ROW_BI).reshape(HEADS, HEAD_DIM), quarter_h), lru_lambda=row(ROW_LAM),
        sgu_ln_g=row(ROW_LG), sgu_ln_b=row(ROW_LB),
        sgu_w_s=ws_sum.reshape(CHUNK, GROUPS, CHUNK).transpose(1, 0, 2).reshape(GROUPS * CHUNK, CHUNK),
        sgu_b_s=vec[ROW_BS:ROW_BS + GROUPS, 0:CHUNK], norm_mlp_g=row(ROW_G2), norm_final_g=row(ROW_G3))
    loss = vec[ROW_LOSS, 0]
    small_names = list(g_small)
    given = dict(
        norm_mix_g=(norm_mix_g, m_norm_mix_g, v_norm_mix_g), conv_w=(conv_w, m_conv_w, v_conv_w),
        conv_b=(conv_b, m_conv_b, v_conv_b), b_rgate=(b_rgate, m_b_rgate, v_b_rgate),
        b_igate=(b_igate, m_b_igate, v_b_igate), lru_lambda=(lru_lambda, m_lru_lambda, v_lru_lambda),
        sgu_ln_g=(sgu_ln_g, m_sgu_ln_g, v_sgu_ln_g), sgu_ln_b=(sgu_ln_b, m_sgu_ln_b, v_sgu_ln_b),
        sgu_w_s=(sgu_w_s, m_sgu_w_s, v_sgu_w_s), sgu_b_s=(sgu_b_s, m_sgu_b_s, v_sgu_b_s),
        norm_mlp_g=(norm_mlp_g, m_norm_mlp_g, v_norm_mlp_g), norm_final_g=(norm_final_g, m_norm_final_g, v_norm_final_g))
    g2d = [g_small[nm] for nm in small_names]
    to2d = lambda a, g: a.reshape(g.shape)
    d_s, m_s, v_s = _adamw_small(
        g2d, *[[to2d(given[nm][q], g) for nm, g in zip(small_names, g2d)] for q in range(3)])

    shapes = dict(
        norm_mix_g=norm_mix_g, w_in=w_in, conv_w=conv_w, conv_b=conv_b, w_rgate=w_rgate, b_rgate=b_rgate,
        w_igate=w_igate, b_igate=b_igate, lru_lambda=lru_lambda, w_out_a=w_out_a, sgu_ln_g=sgu_ln_g,
        sgu_ln_b=sgu_ln_b, sgu_w_s=sgu_w_s, sgu_b_s=sgu_b_s, w_out_b=w_out_b, w_out=w_out, norm_mlp_g=norm_mlp_g,
        w_up=w_up, w_down=w_down, norm_final_g=norm_final_g)
    grads, deltas, new_m, new_v = {}, {}, {}, {}
    for nm, g, (d, nmom, nvar) in zip(names, full, big_out):
        grads[nm], deltas[nm], new_m[nm], new_v[nm] = g, d, nmom, nvar
    for p, nm in enumerate(small_names):
        grads[nm], deltas[nm], new_m[nm], new_v[nm] = g2d[p], d_s[p], m_s[p], v_s[p]
    order = list(shapes)
    out = [loss, grad_x[None]]
    for group in (grads, deltas, new_m, new_v):
        out += [group[nm].reshape(shapes[nm].shape) for nm in order]
    return tuple(out)
```

```python
import functools

import jax
import jax.numpy as jnp
from jax import lax
from jax.experimental import pallas as pl
from jax.experimental.pallas import tpu as pltpu
from jax.experimental.pallas import tpu_sc as plsc

F32 = jnp.float32
BF16 = jnp.bfloat16
MESH = pl.DeviceIdType.MESH

D_MODEL = 1024
D_IN = 6 * D_MODEL
D_FF = 4 * D_MODEL
N_CHIPS = 4
IN_SHARD = D_IN // N_CHIPS
HEADS = 4
HEAD_DIM = D_MODEL // HEADS
GROUPS = 4
GROUP_DIM = D_MODEL // GROUPS
CHUNK = 128
CONV_WIDTH = 4
LRU_C = 8.0
NORM_EPS = 1e-6
LN_EPS = 1e-5

ADAM_LR = 0.001
ADAM_B1 = 0.9
ADAM_B2 = 0.999
ADAM_EPS = 1e-08
ADAM_WD = 0.01
ADAM_STEP = 10

SUBLANES = 8
MM_TILE = 512
IN_TILE = 1024
SEQ_TILE = 256
DW_TILE = 2048
VMEM_LIMIT_BYTES = 56 * 1024 * 1024

GELU_K0 = 0.7978845608028654
GELU_K1 = 0.044715


def _params(n_grid_axes=1):
    return pltpu.CompilerParams(
        dimension_semantics=("arbitrary",) * n_grid_axes, vmem_limit_bytes=VMEM_LIMIT_BYTES)


def _resident(shape):
    nd = len(shape)
    return pl.BlockSpec(shape, lambda *_: (0,) * nd, pipeline_mode=pl.Buffered(1))


def _const(shape):
    nd = len(shape)
    return pl.BlockSpec(shape, lambda *_: (0,) * nd)


def _dot(a, b):
    return jnp.dot(a, b, preferred_element_type=F32)


def _dot_nt(a, b):
    return lax.dot_general(a, b, (((1,), (1,)), ((), ())), preferred_element_type=F32)


def _dot_tn(a, b):
    return lax.dot_general(a, b, (((0,), (0,)), ((), ())), preferred_element_type=F32)


def _gelu(x):
    t = jnp.tanh(GELU_K0 * x * (1.0 + GELU_K1 * x * x))
    return 0.5 * x * (1.0 + t)


def _gelu_and_grad(x):
    x2 = x * x
    t = jnp.tanh(GELU_K0 * x * (1.0 + GELU_K1 * x2))
    g = 0.5 * x * (1.0 + t)
    dg = 0.5 * (1.0 + t) + 0.5 * x * (1.0 - t * t) * (GELU_K0 * (1.0 + 3.0 * GELU_K1 * x2))
    return g, dg


def _rms(x):
    r = lax.rsqrt(jnp.mean(x * x, axis=-1, keepdims=True) + NORM_EPS)
    return x * r, r


def _rms_bwd(dn, xhat, r):
    return r * (dn - xhat * jnp.mean(dn * xhat, axis=-1, keepdims=True))


def _col_sum(v):
    return jnp.sum(v, axis=0, keepdims=True)


def _shift_down(x, tail8, k):
    xs = pltpu.roll(x, k, 0)
    ts = pltpu.roll(tail8, k, 0)
    ridx = lax.broadcasted_iota(jnp.int32, tail8.shape, 0)
    head = jnp.where(ridx < k, ts, xs[0:SUBLANES])
    return jnp.concatenate([head, xs[SUBLANES:]], axis=0)


def _shift_up(x, head8, k):
    n = x.shape[0]
    xs = pltpu.roll(x, n - k, 0)
    hs = pltpu.roll(head8, SUBLANES - k, 0)
    ridx = lax.broadcasted_iota(jnp.int32, head8.shape, 0)
    last = jnp.where(ridx >= SUBLANES - k, hs, xs[n - SUBLANES:n])
    return jnp.concatenate([xs[:n - SUBLANES], last], axis=0)


def _scan_forward(a, b, carry):
    n, cols = a.shape
    groups = n // SUBLANES
    a = a.reshape(groups, SUBLANES, cols)
    b = b.reshape(groups, SUBLANES, cols)
    sub = lax.broadcasted_iota(jnp.int32, a.shape, 1)
    for s in (1, 2, 4):
        a_s = pltpu.roll(a, s, 1)
        b_s = pltpu.roll(b, s, 1)
        m = sub >= s
        b = jnp.where(m, a * b_s + b, b)
        a = jnp.where(m, a * a_s, a)
    out = []
    for g in range(groups):
        h = a[g] * carry + b[g]
        out.append(h)
        carry = h[SUBLANES - 1:SUBLANES]
    return jnp.concatenate(out, axis=0), carry


def _scan_backward(a, b, carry):
    n, cols = a.shape
    groups = n // SUBLANES
    a = a.reshape(groups, SUBLANES, cols)
    b = b.reshape(groups, SUBLANES, cols)
    sub = lax.broadcasted_iota(jnp.int32, a.shape, 1)
    for s in (1, 2, 4):
        a_s = pltpu.roll(a, SUBLANES - s, 1)
        b_s = pltpu.roll(b, SUBLANES - s, 1)
        m = sub < SUBLANES - s
        b = jnp.where(m, a * b_s + b, b)
        a = jnp.where(m, a * a_s, a)
    out = [None] * groups
    for g in reversed(range(groups)):
        h = a[g] * carry + b[g]
        out[g] = h
        carry = h[0:1]
    return jnp.concatenate(out, axis=0), carry


def _softplus_neg(lam):
    e = jnp.exp(-jnp.abs(lam))
    u = 1.0 + e
    log1p_e = jnp.where(u == 1.0, e, jnp.log(u) * (e / jnp.where(u == 1.0, 1.0, u - 1.0)))
    return jnp.maximum(-lam, 0.0) + log1p_e


def _lru_gates(xa, tail8, cw_ref, cb_ref, wr_ref, br_ref, wi_ref, bi_ref, lam_ref):
    cw = cw_ref[...]
    xc = cb_ref[...] + cw[0:1] * xa
    for k in range(1, CONV_WIDTH):
        xc = xc + cw[k:k + 1] * _shift_down(xa, tail8, k)
    xcb = xc.astype(BF16)
    pre_r, pre_i = [], []
    for h in range(HEADS):
        cols = slice(h * HEAD_DIM, (h + 1) * HEAD_DIM)
        pre_r.append(_dot(xcb[:, cols], wr_ref[h]))
        pre_i.append(_dot(xcb[:, cols], wi_ref[h]))
    r = jax.nn.sigmoid(jnp.concatenate(pre_r, axis=1) + br_ref[...])
    ig = jax.nn.sigmoid(jnp.concatenate(pre_i, axis=1) + bi_ref[...])
    _, a, mult = _decay(r, lam_ref)
    return xc, r, ig, a, mult


def _decay(r, lam_ref):
    sp = _softplus_neg(lam_ref[...])
    log_a = ((-LRU_C) * sp) * r
    a = jnp.exp(log_a)
    th = jnp.tanh(log_a)
    return sp, a, jnp.sqrt((-2.0 * th) / (1.0 - th))


class _Job:
    def __init__(self, inputs, out_shape, n_sem, copies, peers, aliases=None, n_local=0):
        self.inputs, self.out_shape, self.n_sem, self.copies = list(inputs), list(out_shape), n_sem, copies
        self.aliases, self.n_local = dict(aliases or {}), n_local
        self.peers = tuple(peers)


def _fused_call(body, jobs, *, name, grid, in_specs, out_specs, out_shape, scratch_shapes=(),
                input_output_aliases=None, compiler_params=None, n_prefetch=0, jobs_start_after=None):
    single = not isinstance(out_shape, (list, tuple))
    out_specs = [out_specs] if single else list(out_specs)
    out_shape = [out_shape] if single else list(out_shape)
    n_scr = len(scratch_shapes)
    in_specs, scratch_shapes = list(in_specs), list(scratch_shapes)
    n_in, n_out = len(in_specs), len(out_shape)
    aliases = dict(input_output_aliases or {})
    in_at, out_at = [], []
    for job in jobs:
        in_at.append(len(in_specs))
        out_at.append(len(out_shape))
        for i, o in job.aliases.items():
            aliases[n_prefetch + len(in_specs) + i] = len(out_shape) + o
        in_specs += [ANY] * len(job.inputs)
        out_specs += [ANY] * len(job.out_shape)
        out_shape += job.out_shape
        scratch_shapes += [pltpu.SemaphoreType.DMA((job.n_sem,)), pltpu.SemaphoreType.DMA((job.n_sem,)),
                           pltpu.SemaphoreType.DMA((max(job.n_local, 1),))]
    n_in_all, n_out_all = len(in_specs), len(out_shape)

    def full_body(*refs):
        prefetch, refs = refs[:n_prefetch], refs[n_prefetch:]
        ins, outs, scr = refs[:n_in_all], refs[n_in_all:n_in_all + n_out_all], refs[n_in_all + n_out_all:]

        def copies(q):
            job = jobs[q]
            return job.copies(ins[in_at[q]:in_at[q] + len(job.inputs)], outs[out_at[q]:out_at[q] + len(job.out_shape)],
                              *scr[n_scr + 3 * q:n_scr + 3 * q + 3])

        def start():
            for q in range(len(jobs)):
                sends, _, local = copies(q)
                for cp in local + sends:
                    cp.start()

        def finish():
            every = [copies(q) for q in range(len(jobs))]
            for _, arrivals, _ in every:
                for cp in arrivals:
                    cp.wait_recv()
            for sends, _, local in every:
                for cp in sends:
                    cp.wait_send()
                for cp in local:
                    cp.wait()

        if not grid:
            start()
            finish()
            return
        ids = [pl.program_id(a) for a in range(len(grid))]
        at_step = lambda step: functools.reduce(jnp.logical_and, [i == k for i, k in zip(ids, step)])
        if jobs and jobs_start_after is None:
            pl.when(at_step((0,) * len(grid)))(start)
        body(*prefetch, *ins[:n_in], *outs[:n_out], *scr[:n_scr])
        if jobs and jobs_start_after is not None:
            pl.when(at_step(jobs_start_after))(start)
        if jobs:
            pl.when(functools.reduce(jnp.logical_and, [i == g - 1 for i, g in zip(ids, grid)]))(finish)

    if n_prefetch:
        layout = dict(grid_spec=pltpu.PrefetchScalarGridSpec(
            num_scalar_prefetch=n_prefetch, grid=grid, in_specs=in_specs, out_specs=out_specs,
            scratch_shapes=scratch_shapes))
    else:
        layout = dict(grid=grid, in_specs=in_specs, out_specs=out_specs, scratch_shapes=scratch_shapes)
    call = pl.pallas_call(
        full_body, name=name, out_shape=out_shape, input_output_aliases=aliases, compiler_params=compiler_params,
        **layout)

    def run(*args):
        res = call(*args, *[a for job in jobs for a in job.inputs])
        mine = res[0] if single else list(res[:n_out])
        return mine, [list(res[at:at + len(job.out_shape)]) for at, job in zip(out_at, jobs)]

    return run


def _fwd_in(x, g1, shards, order, jobs=()):
    t = x.shape[0]
    rows_per_step = min(IN_TILE, t)
    n_tiles = t // rows_per_step
    n = len(shards)
    halves = [s.shape[0] // 2 for s in shards]

    def body(order_ref, x_ref, g_ref, *refs):
        del order_ref
        ins, (z_ref, n_ref), outs = refs[:n], refs[n:n + 2], refs[n + 2:2 * n + 2]
        wbuf, nbuf, send, recv, local = refs[2 * n + 2:]
        s, i = pl.program_id(0), pl.program_id(1)
        x_, y_, c, chips = _place()
        k_me = _chip_index(x_, y_)

        def block(w, chip, pc):
            return outs[w].at[_chip_index(*chip), pl.ds(pc * halves[w], halves[w]), :]

        def over_ici(w, j, landing):
            return pltpu.make_async_remote_copy(
                src_ref=ins[w].at[pl.ds(c * halves[w], halves[w]), :],
                dst_ref=block(w, chips[j] if landing else (x_, y_), c), send_sem=send.at[6 * w + j],
                recv_sem=recv.at[6 * w + j], device_id=(*chips[j], c), device_id_type=MESH)

        def to_sibling(w, j, landing):
            blk = block(w, chips[j], 1 - c if landing else c)
            return pltpu.make_async_remote_copy(
                src_ref=blk, dst_ref=blk, send_sem=send.at[6 * w + 3 + j], recv_sem=recv.at[6 * w + 3 + j],
                device_id=(x_, y_, 1 - c), device_id_type=MESH)

        own = [pltpu.make_async_copy(wbuf, outs[0].at[k_me], local.at[0])]
        own += [pltpu.make_async_copy(ins[w], outs[w].at[k_me], local.at[w]) for w in range(1, n)]

        @pl.when((s == 0) & (i == 0))
        def _():
            for j in range(2):
                for w in range(n):
                    over_ici(w, j, False).start()
            load = pltpu.make_async_copy(ins[0], wbuf, local.at[n])
            load.start()
            load.wait()
            for cp in own:
                cp.start()

        for j in range(N_CHIPS - 1):
            @pl.when((s == j + 1) & (i == 0))
            def _(j=j):
                for w in range(n):
                    over_ici(w, j, True).wait_recv()
                for w in range(n):
                    to_sibling(w, j, False).start()
                if j == 0:
                    for w in range(n):
                        over_ici(w, 2, False).start()
                    own[0].wait()
                for w in range(n):
                    to_sibling(w, j, True).wait_recv()
                load = pltpu.make_async_copy(outs[0].at[_chip_index(*chips[j])], wbuf, local.at[n])
                load.start()
                load.wait()

        rows = pl.ds(pl.multiple_of(i * rows_per_step, rows_per_step), rows_per_step)

        @pl.when(s == 0)
        def _():
            xhat, _ = _rms(x_ref[...])
            nrm = (xhat * g_ref[...]).astype(BF16)
            nbuf[rows, :] = nrm
            n_ref[...] = nrm

        z_ref[...] = _dot(nbuf[rows, :], wbuf[...])

        @pl.when((s == N_CHIPS - 1) & (i == n_tiles - 1))
        def _():
            for j in range(N_CHIPS - 1):
                for w in range(n):
                    over_ici(w, j, False).wait_send()
                    to_sibling(w, j, False).wait_send()
            for cp in own[1:]:
                cp.wait()

    once = lambda s, i, order: (jnp.where(s == 0, i, n_tiles - 1), 0)
    (z, n1, *stacked), job_outs = _fused_call(
        body, jobs, name="fwd_in", grid=(N_CHIPS, n_tiles), n_prefetch=1,
        in_specs=[pl.BlockSpec((rows_per_step, D_MODEL), once), _const((1, D_MODEL))] + [ANY] * n,
        out_specs=[pl.BlockSpec((rows_per_step, IN_SHARD), lambda s, i, order: (i, order[s])),
                   pl.BlockSpec((rows_per_step, D_MODEL), once)] + [ANY] * n,
        out_shape=[jax.ShapeDtypeStruct((t, D_IN), F32), jax.ShapeDtypeStruct((t, D_MODEL), BF16)]
        + [jax.ShapeDtypeStruct((N_CHIPS,) + s.shape, s.dtype) for s in shards],
        scratch_shapes=[pltpu.VMEM(shards[0].shape, BF16), pltpu.VMEM((t, D_MODEL), BF16),
                        pltpu.SemaphoreType.DMA((6 * n,)),
                        pltpu.SemaphoreType.DMA((6 * n,)), pltpu.SemaphoreType.DMA((n + 1,))],
        compiler_params=_params(2), jobs_start_after=(1, 0),
    )(order, x, g1, *shards)
    return (z, n1, stacked), job_outs


def _fwd_lru(z, conv_w, conv_b, wr, br, wi, bi, lam, jobs=()):
    t = z.shape[0]

    def body(xa_ref, ga_ref, cw_ref, cb_ref, wr_ref, br_ref, wi_ref, bi_ref, lam_ref, ya_ref, h_ref, xc_ref, r_ref,
             ig_ref, tail_ref, carry_ref):
        @pl.when(pl.program_id(0) == 0)
        def _():
            tail_ref[...] = jnp.zeros_like(tail_ref)
            carry_ref[...] = jnp.zeros_like(carry_ref)

        xa = xa_ref[...]
        xc, r, ig, a, mult = _lru_gates(xa, tail_ref[...], cw_ref, cb_ref, wr_ref, br_ref, wi_ref, bi_ref, lam_ref)
        tail_ref[...] = xa[SEQ_TILE - SUBLANES:]
        xc_ref[...], r_ref[...], ig_ref[...] = xc, r, ig
        h, carry = _scan_forward(a, xc * ig * mult, carry_ref[...])
        carry_ref[...] = carry
        h_ref[...] = h
        ya_ref[...] = (h * _gelu(ga_ref[...])).astype(BF16)

    tile = lambda j: pl.BlockSpec((SEQ_TILE, D_MODEL), lambda i: (i, j))
    return _fused_call(
        body, jobs, name="fwd_lru", grid=(t // SEQ_TILE,),
        in_specs=[tile(0), tile(1), _const((CONV_WIDTH, D_MODEL)), _const((1, D_MODEL)),
                  _resident((HEADS, HEAD_DIM, HEAD_DIM)), _const((1, D_MODEL)),
                  _resident((HEADS, HEAD_DIM, HEAD_DIM)), _const((1, D_MODEL)), _const((1, D_MODEL))],
        out_specs=[tile(0)] * 5,
        out_shape=[jax.ShapeDtypeStruct((t, D_MODEL), BF16)] + [jax.ShapeDtypeStruct((t, D_MODEL), F32)] * 4,
        scratch_shapes=[pltpu.VMEM((SUBLANES, D_MODEL), F32), pltpu.VMEM((1, D_MODEL), F32)],
        compiler_params=_params(),
    )(z, z, conv_w, conv_b, wr, br, wi, bi, lam)


def _sgu_forward_parts(ub, vb, lg_ref, lb_ref):
    u, du = _gelu_and_grad(ub)
    vg, dvg = _gelu_and_grad(vb)
    mu = jnp.mean(vg, axis=-1, keepdims=True)
    d = vg - mu
    rstd = lax.rsqrt(jnp.mean(d * d, axis=-1, keepdims=True) + LN_EPS)
    vhat = d * rstd
    vn = (vhat * lg_ref[...] + lb_ref[...]).astype(BF16)
    return u, du, dvg, rstd, vhat, vn


def _causal_mask():
    rows = lax.broadcasted_iota(jnp.int32, (CHUNK, CHUNK), 0)
    cols = lax.broadcasted_iota(jnp.int32, (CHUNK, CHUNK), 1)
    return rows >= cols


def _fwd_sgu(z, ln_g, ln_b, w_s, bias_full, jobs=()):
    t = z.shape[0]

    def body(ub_ref, vb_ref, lg_ref, lb_ref, ws_ref, bias_ref, yb_ref):
        u, _, _, _, _, vn = _sgu_forward_parts(ub_ref[...], vb_ref[...], lg_ref, lb_ref)
        mask = _causal_mask()
        wm = [jnp.where(mask, ws_ref[g], 0.0).astype(BF16) for g in range(GROUPS)]
        for c in range(SEQ_TILE // CHUNK):
            rows = slice(c * CHUNK, (c + 1) * CHUNK)
            for g in range(GROUPS):
                cols = slice(g * GROUP_DIM, (g + 1) * GROUP_DIM)
                sp = _dot(wm[g], vn[rows, cols]) + bias_ref[:, cols]
                yb_ref[rows, cols] = (u[rows, cols] * sp).astype(BF16)

    tile = lambda j: pl.BlockSpec((SEQ_TILE, D_MODEL), lambda i: (i, j))
    return _fused_call(
        body, jobs, name="fwd_sgu", grid=(t // SEQ_TILE,),
        in_specs=[tile(2), tile(3), _const((1, D_MODEL)), _const((1, D_MODEL)),
                  _const((GROUPS, CHUNK, CHUNK)), _const((CHUNK, D_MODEL))],
        out_specs=tile(0),
        out_shape=jax.ShapeDtypeStruct((t, D_MODEL), BF16),
        compiler_params=_params(),
    )(z, z, ln_g, ln_b, w_s, bias_full)


def _fwd_merge(ya, yb, z, x, w_oa, w_ob, w_out, g2, jobs=()):
    t = x.shape[0]

    def body(ya_ref, yb_ref, m_ref, x_ref, woa_ref, wob_ref, wout_ref, g_ref, pa_ref, pb_ref, h1_ref, n2_ref):
        pa = _dot(ya_ref[...], woa_ref[...])
        pb = _dot(yb_ref[...], wob_ref[...])
        pa_ref[...] = pa
        pb_ref[...] = pb
        merged = jax.nn.sigmoid(m_ref[:, :D_MODEL]) * pa + jax.nn.sigmoid(m_ref[:, D_MODEL:]) * pb
        h1 = x_ref[...] + _dot(merged.astype(BF16), wout_ref[...])
        h1_ref[...] = h1
        xhat, _ = _rms(h1)
        n2_ref[...] = (xhat * g_ref[...]).astype(BF16)

    tile = pl.BlockSpec((MM_TILE, D_MODEL), lambda i: (i, 0))
    sq = _resident((D_MODEL, D_MODEL))
    return _fused_call(
        body, jobs, name="fwd_merge", grid=(t // MM_TILE,),
        in_specs=[tile, tile, pl.BlockSpec((MM_TILE, 2 * D_MODEL), lambda i: (i, 2)), tile, sq, sq, sq,
                  _const((1, D_MODEL))],
        out_specs=[tile, tile, tile, tile],
        out_shape=[jax.ShapeDtypeStruct((t, D_MODEL), F32)] * 3 + [jax.ShapeDtypeStruct((t, D_MODEL), BF16)],
        compiler_params=_params(),
    )(ya, yb, z, x, w_oa, w_ob, w_out, g2)


def _mlp(n2, h1, target, w_up_st, w_down, g2, g3, jobs=()):
    t = n2.shape[0]

    def body(n2_ref, h1_ref, tgt_ref, wup_ref, wdown_ref, g2_ref, g3_ref, act_ref, dup_ref, dh2b_ref, dh1_ref,
             loss_ref, dg3_ref, dg2_ref, relu_ref):
        @pl.when(pl.program_id(0) == 0)
        def _():
            for ref in (loss_ref, dg3_ref, dg2_ref):
                ref[...] = jnp.zeros_like(ref)

        n2 = n2_ref[...]
        h1 = h1_ref[...]
        h2 = h1
        for k in range(N_CHIPS):
            cols = slice(k * D_MODEL, (k + 1) * D_MODEL)
            r = jnp.maximum(_dot(n2, wup_ref[k]), 0.0)
            relu_ref[:, cols] = r
            act = (r * r).astype(BF16)
            act_ref[:, cols] = act
            h2 = h2 + _dot(act, wdown_ref[cols, :])
        xhat, r3 = _rms(h2)
        diff = xhat * g3_ref[...] - tgt_ref[...]
        sq = jnp.sum(diff * diff, axis=1, keepdims=True)
        loss_ref[...] = loss_ref[...] + (0.5 / D_MODEL) * jnp.sum(sq, axis=0, keepdims=True)
        dy = diff * (1.0 / D_MODEL)
        dg3_ref[...] = dg3_ref[...] + _col_sum(dy * xhat)
        dh2 = _rms_bwd(dy * g3_ref[...], xhat, r3)
        dh2b = dh2.astype(BF16)
        dh2b_ref[...] = dh2b
        dn2 = jnp.zeros((SEQ_TILE, D_MODEL), F32)
        for k in range(N_CHIPS):
            cols = slice(k * D_MODEL, (k + 1) * D_MODEL)
            dup = (_dot_nt(dh2b, wdown_ref[cols, :]) * (2.0 * relu_ref[:, cols])).astype(BF16)
            dup_ref[:, cols] = dup
            dn2 = dn2 + _dot_nt(dup, wup_ref[k])
        xhat, r2 = _rms(h1)
        dg2_ref[...] = dg2_ref[...] + _col_sum(dn2 * xhat)
        dh1_ref[...] = dh2 + _rms_bwd(dn2 * g2_ref[...], xhat, r2)

    tile = pl.BlockSpec((SEQ_TILE, D_MODEL), lambda i: (i, 0))
    wide = pl.BlockSpec((SEQ_TILE, D_FF), lambda i: (i, 0))
    vec = _const((1, D_MODEL))
    vec_shape = jax.ShapeDtypeStruct((1, D_MODEL), F32)
    return _fused_call(
        body, jobs, name="mlp", grid=(t // SEQ_TILE,),
        in_specs=[tile, tile, tile, _resident((N_CHIPS, D_MODEL, D_MODEL)), _resident((D_FF, D_MODEL)), vec, vec],
        out_specs=[wide, wide, tile, tile, _const((SUBLANES, 128)), vec, vec],
        out_shape=[jax.ShapeDtypeStruct((t, D_FF), BF16), jax.ShapeDtypeStruct((t, D_FF), BF16),
                   jax.ShapeDtypeStruct((t, D_MODEL), BF16), jax.ShapeDtypeStruct((t, D_MODEL), F32),
                   jax.ShapeDtypeStruct((SUBLANES, 128), F32), vec_shape, vec_shape],
        scratch_shapes=[pltpu.VMEM((SEQ_TILE, D_FF), F32)],
        compiler_params=_params(),
    )(n2, h1, target, w_up_st, w_down, g2, g3)


def _bwd_mix(dh1, pa, pb, z, h, xc, r, ig, w_oa, w_ob, w_out, ln_g, ln_b, w_s, bias_full, conv_w, wr, wi, lam, jobs=()):
    t = dh1.shape[0]
    n_tiles = t // SEQ_TILE
    per_tile = SEQ_TILE // SUBLANES

    def merge_part(dh1_ref, pa_ref, pb_ref, m_ref, woa_ref, wob_ref, wout_ref, dz_ref, dya_ref, dyb_ref, mg_ref,
                   dpa_ref, dpb_ref, dh1b_ref):
        dh1b = dh1_ref[...].astype(BF16)
        dh1b_ref[...] = dh1b
        dm = _dot_nt(dh1b, wout_ref[...])
        pa = pa_ref[...]
        pb = pb_ref[...]
        sa = jax.nn.sigmoid(m_ref[:, :D_MODEL])
        sb = jax.nn.sigmoid(m_ref[:, D_MODEL:])
        mg_ref[...] = (sa * pa + sb * pb).astype(BF16)
        dz_ref[:, :D_MODEL] = (dm * pa * sa * (1.0 - sa)).astype(BF16)
        dz_ref[:, D_MODEL:] = (dm * pb * sb * (1.0 - sb)).astype(BF16)
        dpa = (dm * sa).astype(BF16)
        dpb = (dm * sb).astype(BF16)
        dpa_ref[...] = dpa
        dpb_ref[...] = dpb
        dya_ref[...] = _dot_nt(dpa, woa_ref[...])
        dyb_ref[...] = _dot_nt(dpb, wob_ref[...])

    def sgu_part(dyb_ref, ub_ref, vb_ref, lg_ref, lb_ref, ws_ref, bias_ref, dz_ref, dlg_ref, dlb_ref, dws_ref, dbs_ref,
                 dvn_ref, dsp_acc):
        i = pl.program_id(0)

        @pl.when(i == 0)
        def _():
            dlg_ref[...] = jnp.zeros_like(dlg_ref)
            dlb_ref[...] = jnp.zeros_like(dlb_ref)
            dws_ref[...] = jnp.zeros_like(dws_ref)
            dsp_acc[...] = jnp.zeros_like(dsp_acc)

        u, du, dvg, rstd, vhat, vn = _sgu_forward_parts(ub_ref[...], vb_ref[...], lg_ref, lb_ref)
        dyb = dyb_ref[...]
        mask = _causal_mask()
        wm = [jnp.where(mask, ws_ref[g], 0.0).astype(BF16) for g in range(GROUPS)]
        for c in range(SEQ_TILE // CHUNK):
            rows = slice(c * CHUNK, (c + 1) * CHUNK)
            for g in range(GROUPS):
                cols = slice(g * GROUP_DIM, (g + 1) * GROUP_DIM)
                vn_blk = vn[rows, cols]
                sp = _dot(wm[g], vn_blk) + bias_ref[:, cols]
                dyb_blk = dyb[rows, cols]
                dz_ref[rows, cols] = (dyb_blk * sp * du[rows, cols]).astype(BF16)
                dsp = dyb_blk * u[rows, cols]
                dsp_acc[:, cols] = dsp_acc[:, cols] + dsp
                dspb = dsp.astype(BF16)
                dvn_ref[rows, cols] = _dot_tn(wm[g], dspb)
                wcols = slice(g * CHUNK, (g + 1) * CHUNK)
                dws_ref[:, wcols] = dws_ref[:, wcols] + jnp.where(mask, _dot_nt(dspb, vn_blk), 0.0)
        dvn = dvn_ref[...]
        dlg_ref[...] = dlg_ref[...] + _col_sum(dvn * vhat)
        dlb_ref[...] = dlb_ref[...] + _col_sum(dvn)
        dvhat = dvn * lg_ref[...]
        dvgel = rstd * (dvhat - jnp.mean(dvhat, axis=-1, keepdims=True)
                        - vhat * jnp.mean(dvhat * vhat, axis=-1, keepdims=True))
        dz_ref[:, D_MODEL:] = (dvgel * dvg).astype(BF16)

        @pl.when(i == n_tiles - 1)
        def _():
            lane = lax.broadcasted_iota(jnp.int32, (CHUNK, 128), 1)
            out = jnp.zeros((CHUNK, 128), F32)
            for g in range(GROUPS):
                s = jnp.sum(dsp_acc[:, g * GROUP_DIM:(g + 1) * GROUP_DIM], axis=1, keepdims=True)
                out = out + jnp.where(lane == g, s, 0.0)
            dbs_ref[...] = out

    def lru_part(dya_ref, xa_ref, ga_ref, h_ref, h_prev_ref, xc_ref, r_ref, ig_ref, cw_ref, wr_ref, wi_ref, lam_ref,
                 dz_ref, dcw_ref, dcb_ref, dwr_ref, dbr_ref, dwi_ref, dbi_ref, dlam_ref, lam_carry, dxc_head):
        i = pl.program_id(0)

        @pl.when(i == 0)
        def _():
            for ref in (dcw_ref, dcb_ref, dwr_ref, dbr_ref, dwi_ref, dbi_ref, dlam_ref, lam_carry, dxc_head):
                ref[...] = jnp.zeros_like(ref)

        first_tile = i == n_tiles - 1
        h_tail = jnp.where(first_tile, 0.0, h_prev_ref[...])
        xc, r, ig = xc_ref[...], r_ref[...], ig_ref[...]
        xcb = xc.astype(BF16)
        sp, a, mult = _decay(r, lam_ref)
        h = h_ref[...]
        h_prev = _shift_down(h, h_tail, 1)
        dya = dya_ref[...]
        gg, dgg = _gelu_and_grad(ga_ref[...])
        dz_ref[:, D_MODEL:] = (dya * h * dgg).astype(BF16)
        ones = jnp.ones((SUBLANES, D_MODEL), F32)
        lam_t, lam_first = _scan_backward(_shift_up(a, ones, 1), dya * gg, lam_carry[...])
        lam_carry[...] = a[0:1] * lam_first
        dmult = lam_t * xc * ig
        dla = lam_t * h_prev * a - dmult * (a * a) / mult
        dr = dla * ((-LRU_C) * sp)
        dlam_ref[...] = dlam_ref[...] + _col_sum(dla * r) * (LRU_C * jax.nn.sigmoid(-lam_ref[...]))
        dpr = dr * r * (1.0 - r)
        dpi = lam_t * xc * mult * ig * (1.0 - ig)
        dbr_ref[...] = dbr_ref[...] + _col_sum(dpr)
        dbi_ref[...] = dbi_ref[...] + _col_sum(dpi)
        dprb = dpr.astype(BF16)
        dpib = dpi.astype(BF16)
        dxc_gate = []
        for hd in range(HEADS):
            cols = slice(hd * HEAD_DIM, (hd + 1) * HEAD_DIM)
            dxc_gate.append(_dot_nt(dprb[:, cols], wr_ref[hd]) + _dot_nt(dpib[:, cols], wi_ref[hd]))
            dwr_ref[hd] = dwr_ref[hd] + _dot_tn(xcb[:, cols], dprb[:, cols])
            dwi_ref[hd] = dwi_ref[hd] + _dot_tn(xcb[:, cols], dpib[:, cols])
        dxc = lam_t * ig * mult + jnp.concatenate(dxc_gate, axis=1)
        dcb_ref[...] = dcb_ref[...] + _col_sum(dxc)
        cw = cw_ref[...]
        head = dxc_head[...]
        xa = xa_ref[...]
        dxa = cw[0:1] * dxc
        dcw_ref[0:1, :] = dcw_ref[0:1, :] + _col_sum(dxc * xa)
        for k in range(1, CONV_WIDTH):
            dxc_k = _shift_up(dxc, head, k)
            dxa = dxa + cw[k:k + 1] * dxc_k
            dcw_ref[k:k + 1, :] = dcw_ref[k:k + 1, :] + _col_sum(dxc_k * xa)
        dxc_head[...] = dxc[0:SUBLANES]
        dz_ref[:, :D_MODEL] = dxa.astype(BF16)

    def body(dh1_ref, pa_ref, pb_ref, z_ref, h_ref, h_prev_ref, xc_ref, r_ref, ig_ref, woa_ref, wob_ref, wout_ref,
             lg_ref, lb_ref, ws_ref, bias_ref, cw_ref, wr_ref, wi_ref, lam_ref, dz_ref, mg_ref, dpa_ref, dpb_ref,
             dh1b_ref, dlg_ref, dlb_ref, dws_ref, dbs_ref, dcw_ref, dcb_ref, dwr_ref, dbr_ref, dwi_ref, dbi_ref,
             dlam_ref, dya_ref, dyb_ref, dvn_ref, dsp_acc, lam_carry, dxc_head):
        def cols(ref, first, count):
            return ref.at[:, pl.ds(first * D_MODEL, count * D_MODEL)]

        merge_part(dh1_ref, pa_ref, pb_ref, cols(z_ref, 4, 2), woa_ref, wob_ref, wout_ref, cols(dz_ref, 4, 2), dya_ref,
                   dyb_ref, mg_ref, dpa_ref, dpb_ref, dh1b_ref)
        sgu_part(dyb_ref, cols(z_ref, 2, 1), cols(z_ref, 3, 1), lg_ref, lb_ref, ws_ref, bias_ref, cols(dz_ref, 2, 2),
                 dlg_ref, dlb_ref, dws_ref, dbs_ref, dvn_ref, dsp_acc)
        lru_part(dya_ref, cols(z_ref, 0, 1), cols(z_ref, 1, 1), h_ref, h_prev_ref, xc_ref, r_ref, ig_ref, cw_ref, wr_ref,
                 wi_ref, lam_ref, cols(dz_ref, 0, 2), dcw_ref, dcb_ref, dwr_ref, dbr_ref, dwi_ref, dbi_ref, dlam_ref,
                 lam_carry, dxc_head)

    rev = lambda i: n_tiles - 1 - i
    tile = pl.BlockSpec((SEQ_TILE, D_MODEL), lambda i: (rev(i), 0))
    row = pl.BlockSpec((SEQ_TILE, D_IN), lambda i: (rev(i), 0))
    prev8 = pl.BlockSpec((SUBLANES, D_MODEL), lambda i: (jnp.maximum(rev(i) * per_tile - 1, 0), 0))
    vec = _const((1, D_MODEL))
    sq = _resident((D_MODEL, D_MODEL))
    gate_w = _resident((HEADS, HEAD_DIM, HEAD_DIM))
    gate_acc = _const((HEADS, HEAD_DIM, HEAD_DIM))
    vec_shape = jax.ShapeDtypeStruct((1, D_MODEL), F32)
    gate_shape = jax.ShapeDtypeStruct((HEADS, HEAD_DIM, HEAD_DIM), F32)
    act_bf = jax.ShapeDtypeStruct((t, D_MODEL), BF16)
    return _fused_call(
        body, jobs, name="bwd_mix", grid=(n_tiles,),
        in_specs=[tile, tile, tile, row, tile, prev8, tile, tile, tile, sq, sq, sq, vec, vec,
                  _const((GROUPS, CHUNK, CHUNK)), _const((CHUNK, D_MODEL)), _const((CONV_WIDTH, D_MODEL)), gate_w, gate_w,
                  vec],
        out_specs=[row, tile, tile, tile, tile, vec, vec, _const((CHUNK, GROUPS * CHUNK)), _const((CHUNK, 128)),
                   _const((SUBLANES, D_MODEL)), vec, gate_acc, vec, gate_acc, vec, vec],
        out_shape=[jax.ShapeDtypeStruct((t, D_IN), BF16), act_bf, act_bf, act_bf, act_bf, vec_shape, vec_shape,
                   jax.ShapeDtypeStruct((CHUNK, GROUPS * CHUNK), F32), jax.ShapeDtypeStruct((CHUNK, 128), F32),
                   jax.ShapeDtypeStruct((SUBLANES, D_MODEL), F32), vec_shape, gate_shape, vec_shape, gate_shape,
                   vec_shape, vec_shape],
        scratch_shapes=[pltpu.VMEM((SEQ_TILE, D_MODEL), F32), pltpu.VMEM((SEQ_TILE, D_MODEL), F32),
                        pltpu.VMEM((SEQ_TILE, D_MODEL), F32), pltpu.VMEM((CHUNK, D_MODEL), F32),
                        pltpu.VMEM((1, D_MODEL), F32), pltpu.VMEM((SUBLANES, D_MODEL), F32)],
        compiler_params=_params(),
    )(dh1, pa, pb, z, h, h, xc, r, ig, w_oa, w_ob, w_out, ln_g, ln_b, w_s, bias_full, conv_w, wr, wi, lam)


def _bwd_in(dz, x, dh1, w_in_st, g1, jobs=()):
    t = x.shape[0]

    def body(dz_ref, x_ref, dh1_ref, w_ref, g_ref, dx_ref, dg1_ref):
        @pl.when(pl.program_id(0) == 0)
        def _():
            dg1_ref[...] = jnp.zeros_like(dg1_ref)

        dn1 = jnp.zeros((MM_TILE, D_MODEL), F32)
        for k in range(N_CHIPS):
            dn1 = dn1 + _dot_nt(dz_ref[:, k * IN_SHARD:(k + 1) * IN_SHARD], w_ref[k])
        xhat, r1 = _rms(x_ref[...])
        dg1_ref[...] = dg1_ref[...] + _col_sum(dn1 * xhat)
        dx_ref[...] = dh1_ref[...] + _rms_bwd(dn1 * g_ref[...], xhat, r1)

    tile = pl.BlockSpec((MM_TILE, D_MODEL), lambda i: (i, 0))
    return _fused_call(
        body, jobs, name="bwd_in", grid=(t // MM_TILE,),
        in_specs=[pl.BlockSpec((MM_TILE, D_IN), lambda i: (i, 0)), tile, tile,
                  _resident((N_CHIPS, D_MODEL, IN_SHARD)), _const((1, D_MODEL))],
        out_specs=[tile, _const((1, D_MODEL))],
        out_shape=[jax.ShapeDtypeStruct((t, D_MODEL), F32), jax.ShapeDtypeStruct((1, D_MODEL), F32)],
        compiler_params=_params(),
    )(dz, x, dh1, w_in_st, g1)


def _weight_grad(name, a, b, n_blocks, a_varies, b_varies, width, jobs=()):
    t = a.shape[0]
    rows = min(DW_TILE, t)
    n_t = t // rows

    def body(a_ref, b_ref, o_ref, acc_ref):
        s = pl.program_id(1)
        part = _dot_tn(a_ref[...], b_ref[...])

        @pl.when(s == 0)
        def _():
            acc_ref[...] = part

        @pl.when(s > 0)
        def _():
            acc_ref[...] = acc_ref[...] + part

        @pl.when(s == n_t - 1)
        def _():
            o_ref[...] = acc_ref[...].astype(BF16)

    return _fused_call(
        body, jobs, name=name, grid=(n_blocks, n_t),
        in_specs=[pl.BlockSpec((rows, D_MODEL), (lambda j, s: (s, j)) if a_varies else (lambda j, s: (s, 0))),
                  pl.BlockSpec((rows, width), (lambda j, s: (s, j)) if b_varies else (lambda j, s: (s, 0)))],
        out_specs=pl.BlockSpec((None, D_MODEL, width), lambda j, s: (j, 0, 0)),
        out_shape=jax.ShapeDtypeStruct((n_blocks, D_MODEL, width), BF16),
        scratch_shapes=[pltpu.VMEM((D_MODEL, width), F32)],
        compiler_params=_params(2),
    )(a, b)


def _weight_grads_square(name, pairs, jobs=()):
    n = len(pairs)
    t = pairs[0][0].shape[0]
    rows = min(MM_TILE, t)
    n_t = t // rows

    def body(*refs):
        ins, outs, accs = refs[:2 * n], refs[2 * n:3 * n], refs[3 * n:]
        s = pl.program_id(0)
        for k in range(n):
            part = _dot_tn(ins[2 * k][...], ins[2 * k + 1][...])

            @pl.when(s == 0)
            def _(k=k, part=part):
                accs[k][...] = part

            @pl.when(s > 0)
            def _(k=k, part=part):
                accs[k][...] = accs[k][...] + part

            @pl.when(s == n_t - 1)
            def _(k=k):
                outs[k][...] = accs[k][...].astype(BF16)

    tile = pl.BlockSpec((rows, D_MODEL), lambda s: (s, 0))
    return _fused_call(
        body, jobs, name=name, grid=(n_t,), in_specs=[tile] * (2 * n), out_specs=[_const((D_MODEL, D_MODEL))] * n,
        out_shape=[jax.ShapeDtypeStruct((D_MODEL, D_MODEL), BF16)] * n,
        scratch_shapes=[pltpu.VMEM((D_MODEL, D_MODEL), F32)] * n,
        compiler_params=_params(),
    )(*[x for pair in pairs for x in pair])


def _place():
    x, y, c = lax.axis_index("x"), lax.axis_index("y"), lax.axis_index("c")
    other_chips = [(1 - x, y), (x, 1 - y), (1 - x, 1 - y)]
    return x, y, c, other_chips


def _chip_index(px, py):
    return 2 * px + py


ANY = pl.BlockSpec(memory_space=pl.ANY)
SIBLING = ((0, 0, 1),)
NEIGHBOURS = ((1, 0, 0), (0, 1, 0))
OTHER_CHIPS = NEIGHBOURS + ((1, 1, 0),)


def _near_far(x, y, c):
    return (x ^ (1 - c), y ^ c), (x ^ c, y ^ (1 - c))


def _gather_near_job(shards):
    n = len(shards)
    halves = [s.shape[0] // 2 for s in shards]

    def copies(ins, outs, send, recv, local):
        x, y, c, _ = _place()
        near, _ = _near_far(x, y, c)

        def block(w, chip, pc):
            return outs[w].at[_chip_index(*chip), pl.ds(pc * halves[w], halves[w]), :]

        def copy(w, k, chip, pc, to, src=None):
            return pltpu.make_async_remote_copy(
                src_ref=block(w, chip, pc) if src is None else src, dst_ref=block(w, chip, pc),
                send_sem=send.at[2 * w + k], recv_sem=recv.at[2 * w + k], device_id=to, device_id_type=MESH)

        sends, arrivals, own = [], [], []
        for w in range(n):
            src = ins[w].at[pl.ds(c * halves[w], halves[w]), :]
            own.append(pltpu.make_async_copy(src, block(w, (x, y), c), local.at[w]))
            sends += [copy(w, 0, (x, y), c, (*near, c), src), copy(w, 1, (x, y), c, (x, y, 1 - c), src)]
            arrivals += [copy(w, 0, near, c, (x, y, c)), copy(w, 1, (x, y), 1 - c, (x, y, c))]
        return sends, arrivals, own

    return _Job(shards, [jax.ShapeDtypeStruct((N_CHIPS,) + s.shape, s.dtype) for s in shards], 2 * n, copies,
                NEIGHBOURS + SIBLING, n_local=n)


def _gather_far_job(stacked):
    n = len(stacked)
    halves = [s.shape[1] // 2 for s in stacked]

    def copies(ins, outs, send, recv, local):
        del ins, local
        x, y, c, _ = _place()
        near, far = _near_far(x, y, c)

        def copy(w, k, chip):
            blk = outs[w].at[_chip_index(*chip), pl.ds(c * halves[w], halves[w]), :]
            return pltpu.make_async_remote_copy(
                src_ref=blk, dst_ref=blk, send_sem=send.at[2 * w + k], recv_sem=recv.at[2 * w + k],
                device_id=(*far, c), device_id_type=MESH)

        sends = [copy(w, k, chip) for w in range(n) for k, chip in enumerate(((x, y), near))]
        arrivals = [copy(w, k, chip) for w in range(n) for k, chip in enumerate((far, (1 - x, 1 - y)))]
        return sends, arrivals, []

    return _Job(stacked, [jax.ShapeDtypeStruct(s.shape, s.dtype) for s in stacked], 2 * n, copies, NEIGHBOURS,
                aliases={w: w for w in range(n)})


def _gather_pass_job(stacked):
    n = len(stacked)
    halves = [s.shape[1] // 2 for s in stacked]

    def copies(ins, outs, send, recv, local):
        del ins, local
        x, y, c, chips = _place()

        def copy(w, j, chip, pc, to):
            blk = outs[w].at[_chip_index(*chip), pl.ds(pc * halves[w], halves[w]), :]
            return pltpu.make_async_remote_copy(
                src_ref=blk, dst_ref=blk, send_sem=send.at[3 * w + j], recv_sem=recv.at[3 * w + j], device_id=to,
                device_id_type=MESH)

        sends = [copy(w, j, chip, c, (x, y, 1 - c)) for w in range(n) for j, chip in enumerate(chips)]
        arrivals = [copy(w, j, chip, 1 - c, (x, y, c)) for w in range(n) for j, chip in enumerate(chips)]
        return sends, arrivals, []

    return _Job(stacked, [jax.ShapeDtypeStruct(s.shape, s.dtype) for s in stacked], 3 * n, copies, SIBLING,
                aliases={w: w for w in range(n)})


def _gather_small_job(block):
    def copies(ins, outs, send, recv, local):
        x, y, c, chips = _place()

        def copy(j, chip_from, to):
            return pltpu.make_async_remote_copy(
                src_ref=ins[0], dst_ref=outs[0].at[_chip_index(*chip_from)], send_sem=send.at[j],
                recv_sem=recv.at[j], device_id=to, device_id_type=MESH)

        own = [pltpu.make_async_copy(ins[0], outs[0].at[_chip_index(x, y)], local.at[0])]
        sends = [copy(j, (x, y), (*chip, c)) for j, chip in enumerate(chips)]
        arrivals = [copy(j, chip, (x, y, c)) for j, chip in enumerate(chips)]
        return sends, arrivals, own

    return _Job([block], [jax.ShapeDtypeStruct((N_CHIPS,) + block.shape, block.dtype)], 3, copies, OTHER_CHIPS,
                n_local=1)


def _pair_send_job(grads):
    n = len(grads)
    halves = [g.shape[1] // 2 for g in grads]

    def copies(ins, outs, send, recv, local):
        del local
        x, y, c, _ = _place()
        sends = [pltpu.make_async_remote_copy(
            src_ref=ins[w].at[:, pl.ds((1 - c) * halves[w], halves[w]), :], dst_ref=outs[w], send_sem=send.at[w],
            recv_sem=recv.at[w], device_id=(x, y, 1 - c), device_id_type=MESH) for w in range(n)]
        return sends, sends, []

    return _Job(grads, [jax.ShapeDtypeStruct((N_CHIPS, h, g.shape[2]), g.dtype) for g, h in zip(grads, halves)], n,
                copies, SIBLING)


def _row_block(rows, limit=256):
    return min(rows, limit)


def _pair_add(name, core, mine, theirs):
    _, _, h, cols = mine.shape
    rb = _row_block(h, 512)

    def body(core_ref, a_ref, b_ref, o_ref):
        del core_ref
        o_ref[...] = (a_ref[...].astype(F32) + b_ref[...].astype(F32)).astype(BF16)

    return pl.pallas_call(
        body, name=name,
        grid_spec=pltpu.PrefetchScalarGridSpec(
            num_scalar_prefetch=1, grid=(N_CHIPS, h // rb),
            in_specs=[pl.BlockSpec((None, None, rb, cols), lambda k, r, core_ref: (k, core_ref[0], r, 0)),
                      pl.BlockSpec((None, rb, cols), lambda k, r, core_ref: (k, r, 0))],
            out_specs=pl.BlockSpec((None, rb, cols), lambda k, r, core_ref: (k, r, 0))),
        out_shape=jax.ShapeDtypeStruct(theirs.shape, BF16),
        compiler_params=_params(2),
    )(core, mine, theirs)


def _sequencer_call(name, collective_id, job, then=()):
    steps = [job]
    for make in then:
        steps.append(make(steps[-1].out_shape))
    peers = sorted(set(p for step in steps for p in step.peers))
    ins = [jax.new_ref(a, memory_space=pltpu.MemorySpace.HBM) for a in job.inputs]
    outs = [ins[{o: i for i, o in job.aliases.items()}[k]] if k in job.aliases.values()
            else jax.empty_ref(shape, memory_space=pltpu.MemorySpace.HBM) for k, shape in enumerate(job.out_shape)]
    sems = [pltpu.SemaphoreType.DMA((n,)) for step in steps for n in (step.n_sem, step.n_sem, max(step.n_local, 1))]

    @pl.kernel(mesh=plsc.ScalarSubcoreMesh(axis_name="sequencer", num_cores=1), name=name, scratch_types=tuple(sems),
               compiler_params=pltpu.CompilerParams(collective_id=collective_id))
    def launch(*sem_refs):
        x, y, c, _ = _place()
        barrier = pltpu.get_barrier_semaphore()
        for dx, dy, dc in peers:
            pl.semaphore_signal(barrier, inc=1, device_id=(x ^ dx, y ^ dy, c ^ dc), device_id_type=MESH)
        pl.semaphore_wait(barrier, len(peers))
        for k, step in enumerate(steps):
            sends, arrivals, own = step.copies(ins if k == 0 else outs, outs, *sem_refs[3 * k:3 * k + 3])
            for cp in own + sends:
                cp.start()
            for cp in arrivals:
                cp.wait_recv()
            for cp in sends:
                cp.wait_send()
            for cp in own:
                cp.wait()

    launch()
    return [ref[...] for ref in outs]


def _chip_exchange_job(sums):
    n = len(sums)

    def copies(ins, outs, send, recv, local):
        del local
        _, _, c, chips = _place()
        sends = [pltpu.make_async_remote_copy(
            src_ref=ins[w].at[_chip_index(*chip)], dst_ref=outs[w].at[j], send_sem=send.at[3 * w + j],
            recv_sem=recv.at[3 * w + j], device_id=(*chip, c), device_id_type=MESH)
            for w in range(n) for j, chip in enumerate(chips)]
        return sends, sends, []

    return _Job(sums, [jax.ShapeDtypeStruct((N_CHIPS - 1,) + s.shape[1:], s.dtype) for s in sums], 3 * n, copies,
                OTHER_CHIPS)


def _chip_sum(name, place, mine, theirs):
    _, h, cols = mine.shape
    rb = _row_block(h, 512)

    def body(place_ref, p_ref, q_ref, o_ref):
        del place_ref
        acc = p_ref[...].astype(F32)
        for j in range(N_CHIPS - 1):
            acc = acc + q_ref[j].astype(F32)
        o_ref[...] = acc

    return pl.pallas_call(
        body, name=name,
        grid_spec=pltpu.PrefetchScalarGridSpec(
            num_scalar_prefetch=1, grid=(h // rb,),
            in_specs=[pl.BlockSpec((None, rb, cols), lambda r, place_ref: (place_ref[0], r, 0)),
                      pl.BlockSpec((N_CHIPS - 1, rb, cols), lambda r, place_ref: (0, r, 0))],
            out_specs=pl.BlockSpec((None, rb, cols), lambda r, place_ref: (place_ref[1], r, 0))),
        out_shape=jax.ShapeDtypeStruct((2, h, cols), F32),
        compiler_params=_params(),
    )(place, mine, theirs)


def _share_job(bufs):
    n = len(bufs)

    def copies(ins, outs, send, recv, local):
        del ins, local
        x, y, c, _ = _place()

        def copy(w, half):
            return pltpu.make_async_remote_copy(
                src_ref=outs[w].at[half], dst_ref=outs[w].at[half], send_sem=send.at[w], recv_sem=recv.at[w],
                device_id=(x, y, 1 - c), device_id_type=MESH)

        return [copy(w, c) for w in range(n)], [copy(w, 1 - c) for w in range(n)], []

    return _Job(bufs, [jax.ShapeDtypeStruct(b.shape, b.dtype) for b in bufs], n, copies, SIBLING,
                aliases={w: w for w in range(n)})


SMALL_ROWS = 24
ROW_G1, ROW_CW, ROW_CB, ROW_BR, ROW_BI, ROW_LAM, ROW_LG, ROW_LB, ROW_G2, ROW_G3, ROW_LOSS, ROW_BS = (
    0, 1, 5, 6, 7, 8, 9, 10, 11, 12, 13, 16)
N_DEV = 8


def _pack_small(dcw, dcb, dbr, dbi, dlam, dlg, dlb, dg2, dg3, loss, dbs):
    def body(dcw_ref, dcb_ref, dbr_ref, dbi_ref, dlam_ref, dlg_ref, dlb_ref, dg2_ref, dg3_ref, loss_ref, dbs_ref, out):
        out[...] = jnp.zeros((SMALL_ROWS, D_MODEL), F32)
        for row, ref in ((ROW_CB, dcb_ref), (ROW_BR, dbr_ref), (ROW_BI, dbi_ref), (ROW_LAM, dlam_ref),
                         (ROW_LG, dlg_ref), (ROW_LB, dlb_ref), (ROW_G2, dg2_ref), (ROW_G3, dg3_ref)):
            out[row:row + 1, :] = ref[...]
        out[ROW_CW:ROW_CW + CONV_WIDTH, :] = dcw_ref[0:CONV_WIDTH, :]
        out[ROW_LOSS:ROW_LOSS + 1, 0:128] = loss_ref[0:1, :]
        out[ROW_BS:ROW_BS + GROUPS, 0:128] = jnp.transpose(dbs_ref[...])[0:GROUPS, :]

    vm = pl.BlockSpec(memory_space=pltpu.VMEM)
    return pl.pallas_call(
        body, name="pack_small", in_specs=[vm] * 11, out_specs=vm,
        out_shape=jax.ShapeDtypeStruct((SMALL_ROWS, D_MODEL), F32),
    )(dcw, dcb, dbr, dbi, dlam, dlg, dlb, dg2, dg3, loss, dbs)


def _gather_all_job(blocks):
    n = len(blocks)
    flips = [(dx, dy, dc) for dx in (0, 1) for dy in (0, 1) for dc in (0, 1)][1:]

    def copies(ins, outs, send, recv, local):
        x, y, c, _ = _place()
        me = 4 * x + 2 * y + c
        sends, arrivals, own = [], [], []
        for w in range(n):
            own.append(pltpu.make_async_copy(ins[w], outs[w].at[me], local.at[w]))
            for k, (dx, dy, dc) in enumerate(flips):
                peer = (x ^ dx, y ^ dy, c ^ dc)
                sem = dict(send_sem=send.at[7 * w + k], recv_sem=recv.at[7 * w + k])
                sends.append(pltpu.make_async_remote_copy(
                    src_ref=ins[w], dst_ref=outs[w].at[me], device_id=peer, device_id_type=MESH, **sem))
                arrivals.append(pltpu.make_async_remote_copy(
                    src_ref=ins[w], dst_ref=outs[w].at[4 * peer[0] + 2 * peer[1] + peer[2]], device_id=peer,
                    device_id_type=MESH, **sem))
        return sends, arrivals, own

    return _Job(blocks, [jax.ShapeDtypeStruct((N_DEV,) + b.shape, b.dtype) for b in blocks], 7 * n, copies,
                OTHER_CHIPS + SIBLING + tuple((dx, dy, 1) for dx, dy, _ in OTHER_CHIPS), n_local=n)


def _sum_small(vec_all, ws_all, dg1_all):
    def body(vec_ref, ws_ref, dg1_ref, vec_out, ws_out):
        vec, ws, dg1 = vec_ref[0], ws_ref[0], dg1_ref[0]
        for d in range(1, N_DEV):
            vec, ws, dg1 = vec + vec_ref[d], ws + ws_ref[d], dg1 + dg1_ref[d]
        vec_out[...] = vec
        vec_out[ROW_G1:ROW_G1 + 1, :] = dg1
        ws_out[...] = ws

    vm = pl.BlockSpec(memory_space=pltpu.VMEM)
    return pl.pallas_call(
        body, name="sum_small", in_specs=[vm] * 3, out_specs=[vm, vm],
        out_shape=[jax.ShapeDtypeStruct(vec_all.shape[1:], F32), jax.ShapeDtypeStruct(ws_all.shape[1:], F32)],
    )(vec_all, ws_all, dg1_all)


def _adamw_math(w, g, m, v):
    m = ADAM_B1 * m + (1.0 - ADAM_B1) * g
    v = ADAM_B2 * v + (1.0 - ADAM_B2) * (g * g)
    m_hat = m / (1.0 - ADAM_B1 ** ADAM_STEP)
    v_hat = v / (1.0 - ADAM_B2 ** ADAM_STEP)
    delta = (-ADAM_LR) * (m_hat / (jnp.sqrt(v_hat) + ADAM_EPS) + ADAM_WD * w)
    return delta, m, v


def _adamw(name, g, w, m, v, jobs=()):
    rows, cols = w.shape
    rb = _row_block(rows)

    def body(g_ref, w_ref, m_ref, v_ref, d_ref, nm_ref, nv_ref):
        d_ref[...], nm_ref[...], nv_ref[...] = _adamw_math(w_ref[...], g_ref[...], m_ref[...], v_ref[...])

    blk = pl.BlockSpec((rb, cols), lambda r: (r, 0))
    return _fused_call(
        body, jobs, name=name, grid=(rows // rb,), in_specs=[blk] * 4, out_specs=[blk] * 3,
        out_shape=[jax.ShapeDtypeStruct(w.shape, F32)] * 3, compiler_params=_params(),
    )(g, w, m, v)


def _adamw_small(grads, ws, ms, vs):
    n = len(grads)

    def body(*refs):
        g_refs, w_refs, m_refs, v_refs = refs[:n], refs[n:2 * n], refs[2 * n:3 * n], refs[3 * n:4 * n]
        outs = refs[4 * n:]
        for p in range(n):
            d, nm, nv = _adamw_math(w_refs[p][...], g_refs[p][...], m_refs[p][...], v_refs[p][...])
            outs[p][...] = d
            outs[n + p][...] = nm
            outs[2 * n + p][...] = nv

    vm = pl.BlockSpec(memory_space=pltpu.VMEM)
    shapes = [jax.ShapeDtypeStruct(w.shape, F32) for w in ws]
    out = pl.pallas_call(
        body, name="adamw_small", in_specs=[vm] * (4 * n), out_specs=[vm] * (3 * n), out_shape=shapes * 3,
    )(*grads, *ws, *ms, *vs)
    return out[:n], out[n:2 * n], out[2 * n:]


def _unstack_heads(w_st):
    per = HEAD_DIM // N_CHIPS
    return w_st.reshape(N_CHIPS, HEADS, per, HEAD_DIM).transpose(1, 0, 2, 3).reshape(HEADS, HEAD_DIM, HEAD_DIM)


def _stack_heads(w):
    per = HEAD_DIM // N_CHIPS
    return w.reshape(HEADS, N_CHIPS, per, HEAD_DIM).transpose(1, 0, 2, 3).reshape(N_CHIPS, HEADS * per, HEAD_DIM)


def kernel(x, norm_mix_g, w_in, conv_w, conv_b, w_rgate, b_rgate, w_igate, b_igate, lru_lambda, w_out_a, sgu_ln_g, sgu_ln_b, sgu_w_s, sgu_b_s, w_out_b, w_out, norm_mlp_g, w_up, w_down, norm_final_g, loss_target, m_norm_mix_g, m_w_in, m_conv_w, m_conv_b, m_w_rgate, m_b_rgate, m_w_igate, m_b_igate, m_lru_lambda, m_w_out_a, m_sgu_ln_g, m_sgu_ln_b, m_sgu_w_s, m_sgu_b_s, m_w_out_b, m_w_out, m_norm_mlp_g, m_w_up, m_w_down, m_norm_final_g, v_norm_mix_g, v_w_in, v_conv_w, v_conv_b, v_w_rgate, v_b_rgate, v_w_igate, v_b_igate, v_lru_lambda, v_w_out_a, v_sgu_ln_g, v_sgu_ln_b, v_sgu_w_s, v_sgu_b_s, v_w_out_b, v_w_out, v_norm_mlp_g, v_w_up, v_w_down, v_norm_final_g):
    chip = _chip_index(lax.axis_index("x"), lax.axis_index("y"))
    core = lax.axis_index("c")
    quarter_h = HEAD_DIM // N_CHIPS
    quarter_d = D_MODEL // N_CHIPS

    as_2d = lambda a: a.reshape(-1, a.shape[-1])
    big_w = [as_2d(w) for w in (w_in, w_rgate, w_igate, w_out_a, w_out_b, w_out, w_up, w_down)]
    big_m = [as_2d(w) for w in (m_w_in, m_w_rgate, m_w_igate, m_w_out_a, m_w_out_b, m_w_out, m_w_up, m_w_down)]
    big_v = [as_2d(w) for w in (v_w_in, v_w_rgate, v_w_igate, v_w_out_a, v_w_out_b, v_w_out, v_w_up, v_w_down)]

    packed = jnp.concatenate([conv_w[0], b_rgate[0], b_igate[0]], axis=1)
    packed = jnp.concatenate([packed, jnp.zeros_like(packed)], axis=0)
    s_in, s_r, s_i, s_oa, s_ob, s_out, s_up, s_down = [w.astype(BF16) for w in big_w]
    xs, target = x[0], loss_target[0]
    g3 = norm_final_g.reshape(1, D_MODEL)
    bias_s = jnp.broadcast_to(jnp.transpose(sgu_b_s[0])[:, :, None], (CHUNK, GROUPS, GROUP_DIM)).reshape(CHUNK, D_MODEL)
    core_arr = core.reshape(1).astype(jnp.int32)
    place = jnp.stack([chip, core]).astype(jnp.int32)
    quarter = lambda g: g.reshape(N_CHIPS, D_MODEL // N_CHIPS, D_MODEL)

    def pair_add(nm, g, from_sibling):
        return _pair_add("pair_add_" + nm, core_arr, g.reshape(N_CHIPS, 2, g.shape[1] // 2, g.shape[2]), from_sibling)

    def chip_sum(nm, pair, from_chips):
        return _chip_sum("chip_sum_" + nm, place, pair, from_chips)

    order = jnp.stack([chip, chip ^ 2, chip ^ 1, chip ^ 3]).astype(jnp.int32)
    (z, n1, (w_in_st, wr_st, wi_st)), ((packed_all,), late) = _fwd_in(
        xs, norm_mix_g, [s_in, s_r, s_i], order,
        jobs=[_gather_small_job(packed), _gather_near_job([s_oa, s_ob, s_out])])
    pick = lambda lo, hi: packed_all[:, :HEADS, lo:hi].transpose(1, 0, 2).reshape(HEADS, -1)
    conv_w_full = pick(0, quarter_d)
    br_full = pick(quarter_d, quarter_d + quarter_h).reshape(1, D_MODEL)
    bi_full = pick(quarter_d + quarter_h, quarter_d + 2 * quarter_h).reshape(1, D_MODEL)
    wr, wi = _unstack_heads(wr_st), _unstack_heads(wi_st)
    lru = (conv_w_full, conv_b, wr, br_full, wi, bi_full, lru_lambda)
    sgu = (sgu_ln_g, sgu_ln_b, sgu_w_s[0], bias_s)

    after = lambda arrays, result: lax.optimization_barrier((arrays, result))[0]
    w_up_st, w_dn = _sequencer_call("gather_mlp", 8, _gather_near_job(after([s_up, s_down], n1)),
                                    then=(_gather_far_job, _gather_pass_job))
    w_dn = w_dn.reshape(D_FF, D_MODEL)
    (ya, *saved), (late,) = _fwd_lru(z, *lru, jobs=[_gather_far_job(late)])
    yb, (late,) = _fwd_sgu(z, *sgu, jobs=[_gather_pass_job(late)])
    w_oa, w_ob, w_o = [w.reshape(D_MODEL, D_MODEL) for w in late]
    (pa, pb, h1, n2), _ = _fwd_merge(ya, yb, z, xs, w_oa, w_ob, w_o, norm_mlp_g)
    (act, dup, dh2b, dh1, loss_part, dg3, dg2), _ = _mlp(n2, h1, target, w_up_st, w_dn, norm_mlp_g, g3)

    d_down, _ = _weight_grad("dw_down", act, dh2b, N_CHIPS, True, False, D_MODEL)
    r_down, = _sequencer_call("send_w_down", 10, _pair_send_job([d_down]))
    d_up, _ = _weight_grad("dw_up", n2, dup, N_CHIPS, False, True, D_MODEL)
    r_up, = _sequencer_call("send_w_up", 11, _pair_send_job([d_up]))
    p_down, p_up = pair_add("w_down", d_down, r_down), pair_add("w_up", d_up, r_up)
    (dz, merged, dpa, dpb, dh1b, dlg, dlb, dws, dbs, dcw, dcb, dwr, dbr, dwi, dbi, dlam), ((q_up, q_down),) = _bwd_mix(
        dh1, pa, pb, z, *saved, w_oa, w_ob, w_o, *sgu, conv_w_full, wr, wi, lru_lambda,
        jobs=[_chip_exchange_job([p_up, p_down])])
    half_up, half_down = chip_sum("w_up", p_up, q_up), chip_sum("w_down", p_down, q_down)
    names = ("w_in", "w_rgate", "w_igate", "w_out_a", "w_out_b", "w_out", "w_up", "w_down")
    (d_out, d_oa, d_ob), ((full_up, full_down),) = _weight_grads_square(
        "dw_projections", [(merged, dh1b), (ya, dpa), (yb, dpb)], jobs=[_share_job([half_up, half_down])])
    mids = [quarter(d_oa), quarter(d_ob), quarter(d_out)]
    r_mids = _sequencer_call("send_mids", 1, _pair_send_job(mids))
    gates = [_stack_heads(dwr).astype(BF16), _stack_heads(dwi).astype(BF16)]
    small = _pack_small(dcw, dcb, dbr, dbi, dlam, dlg, dlb, dg2, dg3, loss_part, dbs)
    p_mids = [pair_add(nm, g, r) for nm, g, r in zip(names[3:6], mids, r_mids)]
    q_mids = _sequencer_call("exchange_mids", 2, _chip_exchange_job(p_mids))
    d_in, (r_gates, (vec_all, ws_all)) = _weight_grad(
        "dw_in", n1, dz, N_CHIPS, False, True, IN_SHARD,
        jobs=[_pair_send_job(gates), _gather_all_job([small, dws])])
    r_in, = _sequencer_call("send_w_in", 3, _pair_send_job([d_in]))
    adam_args = {nm: (w, m, v) for nm, w, m, v in zip(names, big_w, big_m, big_v)}

    def adamw(nm, g):
        w, m, v = adam_args[nm]
        g = g.reshape(w.shape)
        return g, _adamw("adamw_" + nm, g, w, m, v)[0]

    p_gates = [pair_add(nm, g, r) for nm, g, r in zip(names[1:3], gates, r_gates)]
    half_mids = [chip_sum(nm, p, q) for nm, p, q in zip(names[3:6], p_mids, q_mids)]
    full_mids = _sequencer_call("share_mids", 12, _share_job(half_mids))
    p_first = [pair_add("w_in", d_in, r_in)] + p_gates
    q_first = _sequencer_call("exchange_w_in", 4, _chip_exchange_job(p_first))
    (grad_x, dg1), _ = _bwd_in(dz, xs, dh1, w_in_st, norm_mix_g)
    dg1_all, = _sequencer_call("gather_dg1", 6, _gather_all_job([dg1]))
    done = {nm: adamw(nm, f) for nm, f in zip(("w_up", "w_down") + names[3:6], [full_up, full_down] + full_mids)}
    q_first = after(q_first, [out[0] for _, out in done.values()])
    half_first = [chip_sum(nm, p, q) for nm, p, q in zip(names[:3], p_first, q_first)]
    full_first = _sequencer_call("share_last", 5, _share_job(half_first))
    done.update({nm: adamw(nm, f) for nm, f in zip(names[:3], full_first)})
    full, big_out = [done[nm][0] for nm in names], [done[nm][1] for nm in names]

    vec, ws_sum = _sum_small(vec_all, ws_all, dg1_all)
    row = lambda r: vec[r:r + 1]
    shard = lambda a, width: lax.dynamic_slice_in_dim(a, chip * width, width, axis=1)
    g_small = dict(
        norm_mix_g=row(ROW_G1), conv_w=shard(vec[ROW_CW:ROW_CW + CONV_WIDTH], quarter_d), conv_b=row(ROW_CB),
        b_rgate=shard(row(ROW_BR).reshape(HEADS, HEAD_DIM), quarter_h),
        b_igate=shard(row(ROW_BI).reshape(HEADS, HEAD_DIM), quarter_h), lru_lambda=row(ROW_LAM),
        sgu_ln_g=row(ROW_LG), sgu_ln_b=row(ROW_LB),
        sgu_w_s=ws_sum.reshape(CHUNK, GROUPS, CHUNK).transpose(1, 0, 2).reshape(GROUPS * CHUNK, CHUNK),
        sgu_b_s=vec[ROW_BS:ROW_BS + GROUPS, 0:CHUNK], norm_mlp_g=row(ROW_G2), norm_final_g=row(ROW_G3))
    loss = vec[ROW_LOSS, 0]
    small_names = list(g_small)
    given = dict(
        norm_mix_g=(norm_mix_g, m_norm_mix_g, v_norm_mix_g), conv_w=(conv_w, m_conv_w, v_conv_w),
        conv_b=(conv_b, m_conv_b, v_conv_b), b_rgate=(b_rgate, m_b_rgate, v_b_rgate),
        b_igate=(b_igate, m_b_igate, v_b_igate), lru_lambda=(lru_lambda, m_lru_lambda, v_lru_lambda),
        sgu_ln_g=(sgu_ln_g, m_sgu_ln_g, v_sgu_ln_g), sgu_ln_b=(sgu_ln_b, m_sgu_ln_b, v_sgu_ln_b),
        sgu_w_s=(sgu_w_s, m_sgu_w_s, v_sgu_w_s), sgu_b_s=(sgu_b_s, m_sgu_b_s, v_sgu_b_s),
        norm_mlp_g=(norm_mlp_g, m_norm_mlp_g, v_norm_mlp_g), norm_final_g=(norm_final_g, m_norm_final_g, v_norm_final_g))
    g2d = [g_small[nm] for nm in small_names]
    to2d = lambda a, g: a.reshape(g.shape)
    d_s, m_s, v_s = _adamw_small(
        g2d, *[[to2d(given[nm][q], g) for nm, g in zip(small_names, g2d)] for q in range(3)])

    shapes = dict(
        norm_mix_g=norm_mix_g, w_in=w_in, conv_w=conv_w, conv_b=conv_b, w_rgate=w_rgate, b_rgate=b_rgate,
        w_igate=w_igate, b_igate=b_igate, lru_lambda=lru_lambda, w_out_a=w_out_a, sgu_ln_g=sgu_ln_g,
        sgu_ln_b=sgu_ln_b, sgu_w_s=sgu_w_s, sgu_b_s=sgu_b_s, w_out_b=w_out_b, w_out=w_out, norm_mlp_g=norm_mlp_g,
        w_up=w_up, w_down=w_down, norm_final_g=norm_final_g)
    grads, deltas, new_m, new_v = {}, {}, {}, {}
    for nm, g, (d, nmom, nvar) in zip(names, full, big_out):
        grads[nm], deltas[nm], new_m[nm], new_v[nm] = g, d, nmom, nvar
    for p, nm in enumerate(small_names):
        grads[nm], deltas[nm], new_m[nm], new_v[nm] = g2d[p], d_s[p], m_s[p], v_s[p]
    order = list(shapes)
    out = [loss, grad_x[None]]
    for group in (grads, deltas, new_m, new_v):
        out += [group[nm].reshape(shapes[nm].shape) for nm in order]
    return tuple(out)
```

```python
import functools

import jax
import jax.numpy as jnp
from jax import lax
from jax.experimental import pallas as pl
from jax.experimental.pallas import tpu as pltpu
from jax.experimental.pallas import tpu_sc as plsc

F32 = jnp.float32
BF16 = jnp.bfloat16
MESH = pl.DeviceIdType.MESH

D_MODEL = 1024
D_IN = 6 * D_MODEL
D_FF = 4 * D_MODEL
N_CHIPS = 4
IN_SHARD = D_IN // N_CHIPS
HEADS = 4
HEAD_DIM = D_MODEL // HEADS
GROUPS = 4
GROUP_DIM = D_MODEL // GROUPS
CHUNK = 128
CONV_WIDTH = 4
LRU_C = 8.0
NORM_EPS = 1e-6
LN_EPS = 1e-5

ADAM_LR = 0.001
ADAM_B1 = 0.9
ADAM_B2 = 0.999
ADAM_EPS = 1e-08
ADAM_WD = 0.01
ADAM_STEP = 10

SUBLANES = 8
MM_TILE = 512
IN_TILE = 1024
SEQ_TILE = 256
DW_TILE = 2048
VMEM_LIMIT_BYTES = 56 * 1024 * 1024

GELU_K0 = 0.7978845608028654
GELU_K1 = 0.044715


def _params(n_grid_axes=1):
    return pltpu.CompilerParams(
        dimension_semantics=("arbitrary",) * n_grid_axes, vmem_limit_bytes=VMEM_LIMIT_BYTES)


def _resident(shape):
    nd = len(shape)
    return pl.BlockSpec(shape, lambda *_: (0,) * nd, pipeline_mode=pl.Buffered(1))


def _const(shape):
    nd = len(shape)
    return pl.BlockSpec(shape, lambda *_: (0,) * nd)


def _dot(a, b):
    return jnp.dot(a, b, preferred_element_type=F32)


def _dot_nt(a, b):
    return lax.dot_general(a, b, (((1,), (1,)), ((), ())), preferred_element_type=F32)


def _dot_tn(a, b):
    return lax.dot_general(a, b, (((0,), (0,)), ((), ())), preferred_element_type=F32)


def _gelu(x):
    t = jnp.tanh(GELU_K0 * x * (1.0 + GELU_K1 * x * x))
    return 0.5 * x * (1.0 + t)


def _gelu_and_grad(x):
    x2 = x * x
    t = jnp.tanh(GELU_K0 * x * (1.0 + GELU_K1 * x2))
    g = 0.5 * x * (1.0 + t)
    dg = 0.5 * (1.0 + t) + 0.5 * x * (1.0 - t * t) * (GELU_K0 * (1.0 + 3.0 * GELU_K1 * x2))
    return g, dg


def _rms(x):
    r = lax.rsqrt(jnp.mean(x * x, axis=-1, keepdims=True) + NORM_EPS)
    return x * r, r


def _rms_bwd(dn, xhat, r):
    return r * (dn - xhat * jnp.mean(dn * xhat, axis=-1, keepdims=True))


def _col_sum(v):
    return jnp.sum(v, axis=0, keepdims=True)


def _shift_down(x, tail8, k):
    xs = pltpu.roll(x, k, 0)
    ts = pltpu.roll(tail8, k, 0)
    ridx = lax.broadcasted_iota(jnp.int32, tail8.shape, 0)
    head = jnp.where(ridx < k, ts, xs[0:SUBLANES])
    return jnp.concatenate([head, xs[SUBLANES:]], axis=0)


def _shift_up(x, head8, k):
    n = x.shape[0]
    xs = pltpu.roll(x, n - k, 0)
    hs = pltpu.roll(head8, SUBLANES - k, 0)
    ridx = lax.broadcasted_iota(jnp.int32, head8.shape, 0)
    last = jnp.where(ridx >= SUBLANES - k, hs, xs[n - SUBLANES:n])
    return jnp.concatenate([xs[:n - SUBLANES], last], axis=0)


def _scan_forward(a, b, carry):
    n, cols = a.shape
    groups = n // SUBLANES
    a = a.reshape(groups, SUBLANES, cols)
    b = b.reshape(groups, SUBLANES, cols)
    sub = lax.broadcasted_iota(jnp.int32, a.shape, 1)
    for s in (1, 2, 4):
        a_s = pltpu.roll(a, s, 1)
        b_s = pltpu.roll(b, s, 1)
        m = sub >= s
        b = jnp.where(m, a * b_s + b, b)
        a = jnp.where(m, a * a_s, a)
    out = []
    for g in range(groups):
        h = a[g] * carry + b[g]
        out.append(h)
        carry = h[SUBLANES - 1:SUBLANES]
    return jnp.concatenate(out, axis=0), carry


def _scan_backward(a, b, carry):
    n, cols = a.shape
    groups = n // SUBLANES
    a = a.reshape(groups, SUBLANES, cols)
    b = b.reshape(groups, SUBLANES, cols)
    sub = lax.broadcasted_iota(jnp.int32, a.shape, 1)
    for s in (1, 2, 4):
        a_s = pltpu.roll(a, SUBLANES - s, 1)
        b_s = pltpu.roll(b, SUBLANES - s, 1)
        m = sub < SUBLANES - s
        b = jnp.where(m, a * b_s + b, b)
        a = jnp.where(m, a * a_s, a)
    out = [None] * groups
    for g in reversed(range(groups)):
        h = a[g] * carry + b[g]
        out[g] = h
        carry = h[0:1]
    return jnp.concatenate(out, axis=0), carry


def _softplus_neg(lam):
    e = jnp.exp(-jnp.abs(lam))
    u = 1.0 + e
    log1p_e = jnp.where(u == 1.0, e, jnp.log(u) * (e / jnp.where(u == 1.0, 1.0, u - 1.0)))
    return jnp.maximum(-lam, 0.0) + log1p_e


def _lru_gates(xa, tail8, cw_ref, cb_ref, wr_ref, br_ref, wi_ref, bi_ref, lam_ref):
    cw = cw_ref[...]
    xc = cb_ref[...] + cw[0:1] * xa
    for k in range(1, CONV_WIDTH):
        xc = xc + cw[k:k + 1] * _shift_down(xa, tail8, k)
    xcb = xc.astype(BF16)
    pre_r, pre_i = [], []
    for h in range(HEADS):
        cols = slice(h * HEAD_DIM, (h + 1) * HEAD_DIM)
        pre_r.append(_dot(xcb[:, cols], wr_ref[h]))
        pre_i.append(_dot(xcb[:, cols], wi_ref[h]))
    r = jax.nn.sigmoid(jnp.concatenate(pre_r, axis=1) + br_ref[...])
    ig = jax.nn.sigmoid(jnp.concatenate(pre_i, axis=1) + bi_ref[...])
    _, a, mult = _decay(r, lam_ref)
    return xc, r, ig, a, mult


def _decay(r, lam_ref):
    sp = _softplus_neg(lam_ref[...])
    log_a = ((-LRU_C) * sp) * r
    a = jnp.exp(log_a)
    th = jnp.tanh(log_a)
    return sp, a, jnp.sqrt((-2.0 * th) / (1.0 - th))


class _Phase:
    def __init__(self, copies, n_sem, n_local, start=None, finish=None):
        self.copies, self.n_sem, self.n_local, self.start, self.finish = copies, n_sem, n_local, start, finish


class _Job:
    def __init__(self, inputs, out_shape, n_sem, copies, peers, aliases=None, n_local=0):
        self.inputs, self.out_shape = list(inputs), list(out_shape)
        self.aliases = dict(aliases or {})
        self.phases = [_Phase(copies, n_sem, n_local)]
        self.peers = tuple(peers)

    def then(self, make, at=None):
        nxt = make(self.out_shape)
        self.phases[-1].finish = at
        nxt.phases[0].start = at
        self.phases += nxt.phases
        self.peers = tuple(sorted(set(self.peers + nxt.peers)))
        return self


def _fused_call(body, jobs, *, name, grid, in_specs, out_specs, out_shape, scratch_shapes=(),
                input_output_aliases=None, compiler_params=None, n_prefetch=0, jobs_start_after=None):
    single = not isinstance(out_shape, (list, tuple))
    out_specs = [out_specs] if single else list(out_specs)
    out_shape = [out_shape] if single else list(out_shape)
    n_scr = len(scratch_shapes)
    in_specs, scratch_shapes = list(in_specs), list(scratch_shapes)
    n_in, n_out = len(in_specs), len(out_shape)
    aliases = dict(input_output_aliases or {})
    in_at, out_at, phases = [], [], []
    for q, job in enumerate(jobs):
        in_at.append(len(in_specs))
        out_at.append(len(out_shape))
        for i, o in job.aliases.items():
            aliases[n_prefetch + len(in_specs) + i] = len(out_shape) + o
        in_specs += [ANY] * len(job.inputs)
        out_specs += [ANY] * len(job.out_shape)
        out_shape += job.out_shape
        for k, phase in enumerate(job.phases):
            phases.append((q, k, phase, len(scratch_shapes)))
            scratch_shapes += [pltpu.SemaphoreType.DMA((phase.n_sem,)), pltpu.SemaphoreType.DMA((phase.n_sem,)),
                               pltpu.SemaphoreType.DMA((max(phase.n_local, 1),))]
    n_in_all, n_out_all = len(in_specs), len(out_shape)
    first_step, last_step = (0,) * len(grid), tuple(g - 1 for g in grid)

    def full_body(*refs):
        prefetch, refs = refs[:n_prefetch], refs[n_prefetch:]
        ins, outs, scr = refs[:n_in_all], refs[n_in_all:n_in_all + n_out_all], refs[n_in_all + n_out_all:]
        ids = [pl.program_id(a) for a in range(len(grid))]
        at_step = lambda step: functools.reduce(jnp.logical_and, [i == k for i, k in zip(ids, step)])

        def copies(q, k, phase, sem_at):
            job = jobs[q]
            mine = outs[out_at[q]:out_at[q] + len(job.out_shape)]
            return phase.copies(ins[in_at[q]:in_at[q] + len(job.inputs)] if k == 0 else mine, mine,
                                *scr[sem_at:sem_at + 3])

        def start(*phase):
            def go():
                sends, _, local = copies(*phase)
                for cp in local + sends:
                    cp.start()
            return go

        def finish(*phase):
            def go():
                sends, arrivals, local = copies(*phase)
                for cp in arrivals:
                    cp.wait_recv()
                for cp in sends:
                    cp.wait_send()
                for cp in local:
                    cp.wait()
            return go

        for phase in phases:
            if phase[2].start is None and jobs_start_after is None:
                pl.when(at_step(first_step))(start(*phase))
        body(*prefetch, *ins[:n_in], *outs[:n_out], *scr[:n_scr])
        for phase in phases:
            pl.when(at_step(phase[2].finish or last_step))(finish(*phase))
            nxt = phase[2].start or jobs_start_after
            if nxt is not None:
                pl.when(at_step(nxt))(start(*phase))

    if n_prefetch:
        layout = dict(grid_spec=pltpu.PrefetchScalarGridSpec(
            num_scalar_prefetch=n_prefetch, grid=grid, in_specs=in_specs, out_specs=out_specs,
            scratch_shapes=scratch_shapes))
    else:
        layout = dict(grid=grid, in_specs=in_specs, out_specs=out_specs, scratch_shapes=scratch_shapes)
    call = pl.pallas_call(
        full_body, name=name, out_shape=out_shape, input_output_aliases=aliases, compiler_params=compiler_params,
        **layout)

    def run(*args):
        res = call(*args, *[a for job in jobs for a in job.inputs])
        mine = res[0] if single else list(res[:n_out])
        return mine, [list(res[at:at + len(job.out_shape)]) for at, job in zip(out_at, jobs)]

    return run


def _fwd_in(x, g1, shards, order, jobs=()):
    t = x.shape[0]
    rows_per_step = min(IN_TILE, t)
    n_tiles = t // rows_per_step
    n = len(shards)
    halves = [s.shape[0] // 2 for s in shards]

    def body(order_ref, x_ref, g_ref, *refs):
        del order_ref
        ins, (z_ref, n_ref), outs = refs[:n], refs[n:n + 2], refs[n + 2:2 * n + 2]
        wbuf, nbuf, send, recv, local = refs[2 * n + 2:]
        s, i = pl.program_id(0), pl.program_id(1)
        x_, y_, c, chips = _place()
        k_me = _chip_index(x_, y_)

        def block(w, chip, pc):
            return outs[w].at[_chip_index(*chip), pl.ds(pc * halves[w], halves[w]), :]

        def over_ici(w, j, landing):
            return pltpu.make_async_remote_copy(
                src_ref=ins[w].at[pl.ds(c * halves[w], halves[w]), :],
                dst_ref=block(w, chips[j] if landing else (x_, y_), c), send_sem=send.at[6 * w + j],
                recv_sem=recv.at[6 * w + j], device_id=(*chips[j], c), device_id_type=MESH)

        def to_sibling(w, j, landing):
            blk = block(w, chips[j], 1 - c if landing else c)
            return pltpu.make_async_remote_copy(
                src_ref=blk, dst_ref=blk, send_sem=send.at[6 * w + 3 + j], recv_sem=recv.at[6 * w + 3 + j],
                device_id=(x_, y_, 1 - c), device_id_type=MESH)

        own = [pltpu.make_async_copy(wbuf, outs[0].at[k_me], local.at[0])]
        own += [pltpu.make_async_copy(ins[w], outs[w].at[k_me], local.at[w]) for w in range(1, n)]

        @pl.when((s == 0) & (i == 0))
        def _():
            for j in range(2):
                for w in range(n):
                    over_ici(w, j, False).start()
            load = pltpu.make_async_copy(ins[0], wbuf, local.at[n])
            load.start()
            load.wait()
            for cp in own:
                cp.start()

        for j in range(N_CHIPS - 1):
            @pl.when((s == j + 1) & (i == 0))
            def _(j=j):
                for w in range(n):
                    over_ici(w, j, True).wait_recv()
                for w in range(n):
                    to_sibling(w, j, False).start()
                if j == 0:
                    for w in range(n):
                        over_ici(w, 2, False).start()
                    own[0].wait()
                for w in range(n):
                    to_sibling(w, j, True).wait_recv()
                load = pltpu.make_async_copy(outs[0].at[_chip_index(*chips[j])], wbuf, local.at[n])
                load.start()
                load.wait()

        rows = pl.ds(pl.multiple_of(i * rows_per_step, rows_per_step), rows_per_step)

        @pl.when(s == 0)
        def _():
            xhat, _ = _rms(x_ref[...])
            nrm = (xhat * g_ref[...]).astype(BF16)
            nbuf[rows, :] = nrm
            n_ref[...] = nrm

        z_ref[...] = _dot(nbuf[rows, :], wbuf[...])

        @pl.when((s == N_CHIPS - 1) & (i == n_tiles - 1))
        def _():
            for j in range(N_CHIPS - 1):
                for w in range(n):
                    over_ici(w, j, False).wait_send()
                    to_sibling(w, j, False).wait_send()
            for cp in own[1:]:
                cp.wait()

    once = lambda s, i, order: (jnp.where(s == 0, i, n_tiles - 1), 0)
    (z, n1, *stacked), job_outs = _fused_call(
        body, jobs, name="fwd_in", grid=(N_CHIPS, n_tiles), n_prefetch=1,
        in_specs=[pl.BlockSpec((rows_per_step, D_MODEL), once), _const((1, D_MODEL))] + [ANY] * n,
        out_specs=[pl.BlockSpec((rows_per_step, IN_SHARD), lambda s, i, order: (i, order[s])),
                   pl.BlockSpec((rows_per_step, D_MODEL), once)] + [ANY] * n,
        out_shape=[jax.ShapeDtypeStruct((t, D_IN), F32), jax.ShapeDtypeStruct((t, D_MODEL), BF16)]
        + [jax.ShapeDtypeStruct((N_CHIPS,) + s.shape, s.dtype) for s in shards],
        scratch_shapes=[pltpu.VMEM(shards[0].shape, BF16), pltpu.VMEM((t, D_MODEL), BF16),
                        pltpu.SemaphoreType.DMA((6 * n,)),
                        pltpu.SemaphoreType.DMA((6 * n,)), pltpu.SemaphoreType.DMA((n + 1,))],
        compiler_params=_params(2), jobs_start_after=(1, 0),
    )(order, x, g1, *shards)
    return (z, n1, stacked), job_outs


def _fwd_lru(z, conv_w, conv_b, wr, br, wi, bi, lam, jobs=()):
    t = z.shape[0]

    def body(xa_ref, ga_ref, cw_ref, cb_ref, wr_ref, br_ref, wi_ref, bi_ref, lam_ref, ya_ref, h_ref, xc_ref, r_ref,
             ig_ref, tail_ref, carry_ref):
        @pl.when(pl.program_id(0) == 0)
        def _():
            tail_ref[...] = jnp.zeros_like(tail_ref)
            carry_ref[...] = jnp.zeros_like(carry_ref)

        xa = xa_ref[...]
        xc, r, ig, a, mult = _lru_gates(xa, tail_ref[...], cw_ref, cb_ref, wr_ref, br_ref, wi_ref, bi_ref, lam_ref)
        tail_ref[...] = xa[SEQ_TILE - SUBLANES:]
        xc_ref[...], r_ref[...], ig_ref[...] = xc, r, ig
        h, carry = _scan_forward(a, xc * ig * mult, carry_ref[...])
        carry_ref[...] = carry
        h_ref[...] = h
        ya_ref[...] = (h * _gelu(ga_ref[...])).astype(BF16)

    tile = lambda j: pl.BlockSpec((SEQ_TILE, D_MODEL), lambda i: (i, j))
    return _fused_call(
        body, jobs, name="fwd_lru", grid=(t // SEQ_TILE,),
        in_specs=[tile(0), tile(1), _const((CONV_WIDTH, D_MODEL)), _const((1, D_MODEL)),
                  _resident((HEADS, HEAD_DIM, HEAD_DIM)), _const((1, D_MODEL)),
                  _resident((HEADS, HEAD_DIM, HEAD_DIM)), _const((1, D_MODEL)), _const((1, D_MODEL))],
        out_specs=[tile(0)] * 5,
        out_shape=[jax.ShapeDtypeStruct((t, D_MODEL), BF16)] + [jax.ShapeDtypeStruct((t, D_MODEL), F32)] * 4,
        scratch_shapes=[pltpu.VMEM((SUBLANES, D_MODEL), F32), pltpu.VMEM((1, D_MODEL), F32)],
        compiler_params=_params(),
    )(z, z, conv_w, conv_b, wr, br, wi, bi, lam)


def _sgu_forward_parts(ub, vb, lg_ref, lb_ref):
    u, du = _gelu_and_grad(ub)
    vg, dvg = _gelu_and_grad(vb)
    mu = jnp.mean(vg, axis=-1, keepdims=True)
    d = vg - mu
    rstd = lax.rsqrt(jnp.mean(d * d, axis=-1, keepdims=True) + LN_EPS)
    vhat = d * rstd
    vn = (vhat * lg_ref[...] + lb_ref[...]).astype(BF16)
    return u, du, dvg, rstd, vhat, vn


def _causal_mask():
    rows = lax.broadcasted_iota(jnp.int32, (CHUNK, CHUNK), 0)
    cols = lax.broadcasted_iota(jnp.int32, (CHUNK, CHUNK), 1)
    return rows >= cols


def _fwd_sgu_merge(ya, z, x, ln_g, ln_b, w_s, bias_full, w_oa, w_ob, w_out, g2, jobs=()):
    t = x.shape[0]

    def body(ya_ref, ub_ref, vb_ref, m_ref, x_ref, lg_ref, lb_ref, ws_ref, bias_ref, woa_ref, wob_ref, wout_ref, g_ref,
             yb_ref, pa_ref, pb_ref, h1_ref, n2_ref):
        u, _, _, _, _, vn = _sgu_forward_parts(ub_ref[...], vb_ref[...], lg_ref, lb_ref)
        mask = _causal_mask()
        wm = [jnp.where(mask, ws_ref[g], 0.0).astype(BF16) for g in range(GROUPS)]
        for c in range(SEQ_TILE // CHUNK):
            rows = slice(c * CHUNK, (c + 1) * CHUNK)
            for g in range(GROUPS):
                cols = slice(g * GROUP_DIM, (g + 1) * GROUP_DIM)
                sp = _dot(wm[g], vn[rows, cols]) + bias_ref[:, cols]
                yb_ref[rows, cols] = (u[rows, cols] * sp).astype(BF16)
        pa = _dot(ya_ref[...], woa_ref[...])
        pb = _dot(yb_ref[...], wob_ref[...])
        pa_ref[...] = pa
        pb_ref[...] = pb
        merged = jax.nn.sigmoid(m_ref[:, :D_MODEL]) * pa + jax.nn.sigmoid(m_ref[:, D_MODEL:]) * pb
        h1 = x_ref[...] + _dot(merged.astype(BF16), wout_ref[...])
        h1_ref[...] = h1
        xhat, _ = _rms(h1)
        n2_ref[...] = (xhat * g_ref[...]).astype(BF16)

    tile = lambda j: pl.BlockSpec((SEQ_TILE, D_MODEL), lambda i: (i, j))
    sq = _resident((D_MODEL, D_MODEL))
    vec = _const((1, D_MODEL))
    bf, f32 = jax.ShapeDtypeStruct((t, D_MODEL), BF16), jax.ShapeDtypeStruct((t, D_MODEL), F32)
    return _fused_call(
        body, jobs, name="fwd_sgu_merge", grid=(t // SEQ_TILE,),
        in_specs=[tile(0), tile(2), tile(3), pl.BlockSpec((SEQ_TILE, 2 * D_MODEL), lambda i: (i, 2)), tile(0), vec, vec,
                  _const((GROUPS, CHUNK, CHUNK)), _const((CHUNK, D_MODEL)), sq, sq, sq, vec],
        out_specs=[tile(0)] * 5,
        out_shape=[bf, f32, f32, f32, bf],
        compiler_params=_params(),
    )(ya, z, z, z, x, ln_g, ln_b, w_s, bias_full, w_oa, w_ob, w_out, g2)


def _mlp(n2, h1, target, w_up_st, w_down, g2, g3, jobs=()):
    t = n2.shape[0]

    def body(n2_ref, h1_ref, tgt_ref, wup_ref, wdown_ref, g2_ref, g3_ref, act_ref, dup_ref, dh2b_ref, dh1_ref,
             loss_ref, dg3_ref, dg2_ref, relu_ref):
        @pl.when(pl.program_id(0) == 0)
        def _():
            for ref in (loss_ref, dg3_ref, dg2_ref):
                ref[...] = jnp.zeros_like(ref)

        n2 = n2_ref[...]
        h1 = h1_ref[...]
        h2 = h1
        for k in range(N_CHIPS):
            cols = slice(k * D_MODEL, (k + 1) * D_MODEL)
            r = jnp.maximum(_dot(n2, wup_ref[k]), 0.0)
            relu_ref[:, cols] = r
            act = (r * r).astype(BF16)
            act_ref[:, cols] = act
            h2 = h2 + _dot(act, wdown_ref[cols, :])
        xhat, r3 = _rms(h2)
        diff = xhat * g3_ref[...] - tgt_ref[...]
        sq = jnp.sum(diff * diff, axis=1, keepdims=True)
        loss_ref[...] = loss_ref[...] + (0.5 / D_MODEL) * jnp.sum(sq, axis=0, keepdims=True)
        dy = diff * (1.0 / D_MODEL)
        dg3_ref[...] = dg3_ref[...] + _col_sum(dy * xhat)
        dh2 = _rms_bwd(dy * g3_ref[...], xhat, r3)
        dh2b = dh2.astype(BF16)
        dh2b_ref[...] = dh2b
        dn2 = jnp.zeros((SEQ_TILE, D_MODEL), F32)
        for k in range(N_CHIPS):
            cols = slice(k * D_MODEL, (k + 1) * D_MODEL)
            dup = (_dot_nt(dh2b, wdown_ref[cols, :]) * (2.0 * relu_ref[:, cols])).astype(BF16)
            dup_ref[:, cols] = dup
            dn2 = dn2 + _dot_nt(dup, wup_ref[k])
        xhat, r2 = _rms(h1)
        dg2_ref[...] = dg2_ref[...] + _col_sum(dn2 * xhat)
        dh1_ref[...] = dh2 + _rms_bwd(dn2 * g2_ref[...], xhat, r2)

    tile = pl.BlockSpec((SEQ_TILE, D_MODEL), lambda i: (i, 0))
    wide = pl.BlockSpec((SEQ_TILE, D_FF), lambda i: (i, 0))
    vec = _const((1, D_MODEL))
    vec_shape = jax.ShapeDtypeStruct((1, D_MODEL), F32)
    return _fused_call(
        body, jobs, name="mlp", grid=(t // SEQ_TILE,),
        in_specs=[tile, tile, tile, _resident((N_CHIPS, D_MODEL, D_MODEL)), _resident((D_FF, D_MODEL)), vec, vec],
        out_specs=[wide, wide, tile, tile, _const((SUBLANES, 128)), vec, vec],
        out_shape=[jax.ShapeDtypeStruct((t, D_FF), BF16), jax.ShapeDtypeStruct((t, D_FF), BF16),
                   jax.ShapeDtypeStruct((t, D_MODEL), BF16), jax.ShapeDtypeStruct((t, D_MODEL), F32),
                   jax.ShapeDtypeStruct((SUBLANES, 128), F32), vec_shape, vec_shape],
        scratch_shapes=[pltpu.VMEM((SEQ_TILE, D_FF), F32)],
        compiler_params=_params(),
    )(n2, h1, target, w_up_st, w_down, g2, g3)


def _bwd_mix(dh1, pa, pb, z, h, xc, r, ig, w_oa, w_ob, w_out, ln_g, ln_b, w_s, bias_full, conv_w, wr, wi, lam, jobs=()):
    t = dh1.shape[0]
    n_tiles = t // SEQ_TILE
    per_tile = SEQ_TILE // SUBLANES

    def merge_part(dh1_ref, pa_ref, pb_ref, m_ref, woa_ref, wob_ref, wout_ref, dz_ref, dya_ref, dyb_ref, mg_ref,
                   dpa_ref, dpb_ref, dh1b_ref):
        dh1b = dh1_ref[...].astype(BF16)
        dh1b_ref[...] = dh1b
        dm = _dot_nt(dh1b, wout_ref[...])
        pa = pa_ref[...]
        pb = pb_ref[...]
        sa = jax.nn.sigmoid(m_ref[:, :D_MODEL])
        sb = jax.nn.sigmoid(m_ref[:, D_MODEL:])
        mg_ref[...] = (sa * pa + sb * pb).astype(BF16)
        dz_ref[:, :D_MODEL] = (dm * pa * sa * (1.0 - sa)).astype(BF16)
        dz_ref[:, D_MODEL:] = (dm * pb * sb * (1.0 - sb)).astype(BF16)
        dpa = (dm * sa).astype(BF16)
        dpb = (dm * sb).astype(BF16)
        dpa_ref[...] = dpa
        dpb_ref[...] = dpb
        dya_ref[...] = _dot_nt(dpa, woa_ref[...])
        dyb_ref[...] = _dot_nt(dpb, wob_ref[...])

    def sgu_part(dyb_ref, ub_ref, vb_ref, lg_ref, lb_ref, ws_ref, bias_ref, dz_ref, dlg_ref, dlb_ref, dws_ref, dbs_ref,
                 dvn_ref, dsp_acc):
        i = pl.program_id(0)

        @pl.when(i == 0)
        def _():
            dlg_ref[...] = jnp.zeros_like(dlg_ref)
            dlb_ref[...] = jnp.zeros_like(dlb_ref)
            dws_ref[...] = jnp.zeros_like(dws_ref)
            dsp_acc[...] = jnp.zeros_like(dsp_acc)

        u, du, dvg, rstd, vhat, vn = _sgu_forward_parts(ub_ref[...], vb_ref[...], lg_ref, lb_ref)
        dyb = dyb_ref[...]
        mask = _causal_mask()
        wm = [jnp.where(mask, ws_ref[g], 0.0).astype(BF16) for g in range(GROUPS)]
        for c in range(SEQ_TILE // CHUNK):
            rows = slice(c * CHUNK, (c + 1) * CHUNK)
            for g in range(GROUPS):
                cols = slice(g * GROUP_DIM, (g + 1) * GROUP_DIM)
                vn_blk = vn[rows, cols]
                sp = _dot(wm[g], vn_blk) + bias_ref[:, cols]
                dyb_blk = dyb[rows, cols]
                dz_ref[rows, cols] = (dyb_blk * sp * du[rows, cols]).astype(BF16)
                dsp = dyb_blk * u[rows, cols]
                dsp_acc[:, cols] = dsp_acc[:, cols] + dsp
                dspb = dsp.astype(BF16)
                dvn_ref[rows, cols] = _dot_tn(wm[g], dspb)
                wcols = slice(g * CHUNK, (g + 1) * CHUNK)
                dws_ref[:, wcols] = dws_ref[:, wcols] + jnp.where(mask, _dot_nt(dspb, vn_blk), 0.0)
        dvn = dvn_ref[...]
        dlg_ref[...] = dlg_ref[...] + _col_sum(dvn * vhat)
        dlb_ref[...] = dlb_ref[...] + _col_sum(dvn)
        dvhat = dvn * lg_ref[...]
        dvgel = rstd * (dvhat - jnp.mean(dvhat, axis=-1, keepdims=True)
                        - vhat * jnp.mean(dvhat * vhat, axis=-1, keepdims=True))
        dz_ref[:, D_MODEL:] = (dvgel * dvg).astype(BF16)

        @pl.when(i == n_tiles - 1)
        def _():
            lane = lax.broadcasted_iota(jnp.int32, (CHUNK, 128), 1)
            out = jnp.zeros((CHUNK, 128), F32)
            for g in range(GROUPS):
                s = jnp.sum(dsp_acc[:, g * GROUP_DIM:(g + 1) * GROUP_DIM], axis=1, keepdims=True)
                out = out + jnp.where(lane == g, s, 0.0)
            dbs_ref[...] = out

    def lru_part(dya_ref, xa_ref, ga_ref, h_ref, h_prev_ref, xc_ref, r_ref, ig_ref, cw_ref, wr_ref, wi_ref, lam_ref,
                 dz_ref, dcw_ref, dcb_ref, dwr_ref, dbr_ref, dwi_ref, dbi_ref, dlam_ref, lam_carry, dxc_head):
        i = pl.program_id(0)

        @pl.when(i == 0)
        def _():
            for ref in (dcw_ref, dcb_ref, dwr_ref, dbr_ref, dwi_ref, dbi_ref, dlam_ref, lam_carry, dxc_head):
                ref[...] = jnp.zeros_like(ref)

        first_tile = i == n_tiles - 1
        h_tail = jnp.where(first_tile, 0.0, h_prev_ref[...])
        xc, r, ig = xc_ref[...], r_ref[...], ig_ref[...]
        xcb = xc.astype(BF16)
        sp, a, mult = _decay(r, lam_ref)
        h = h_ref[...]
        h_prev = _shift_down(h, h_tail, 1)
        dya = dya_ref[...]
        gg, dgg = _gelu_and_grad(ga_ref[...])
        dz_ref[:, D_MODEL:] = (dya * h * dgg).astype(BF16)
        ones = jnp.ones((SUBLANES, D_MODEL), F32)
        lam_t, lam_first = _scan_backward(_shift_up(a, ones, 1), dya * gg, lam_carry[...])
        lam_carry[...] = a[0:1] * lam_first
        dmult = lam_t * xc * ig
        dla = lam_t * h_prev * a - dmult * (a * a) / mult
        dr = dla * ((-LRU_C) * sp)
        dlam_ref[...] = dlam_ref[...] + _col_sum(dla * r) * (LRU_C * jax.nn.sigmoid(-lam_ref[...]))
        dpr = dr * r * (1.0 - r)
        dpi = lam_t * xc * mult * ig * (1.0 - ig)
        dbr_ref[...] = dbr_ref[...] + _col_sum(dpr)
        dbi_ref[...] = dbi_ref[...] + _col_sum(dpi)
        dprb = dpr.astype(BF16)
        dpib = dpi.astype(BF16)
        dxc_gate = []
        for hd in range(HEADS):
            cols = slice(hd * HEAD_DIM, (hd + 1) * HEAD_DIM)
            dxc_gate.append(_dot_nt(dprb[:, cols], wr_ref[hd]) + _dot_nt(dpib[:, cols], wi_ref[hd]))
            dwr_ref[hd] = dwr_ref[hd] + _dot_tn(xcb[:, cols], dprb[:, cols])
            dwi_ref[hd] = dwi_ref[hd] + _dot_tn(xcb[:, cols], dpib[:, cols])
        dxc = lam_t * ig * mult + jnp.concatenate(dxc_gate, axis=1)
        dcb_ref[...] = dcb_ref[...] + _col_sum(dxc)
        cw = cw_ref[...]
        head = dxc_head[...]
        xa = xa_ref[...]
        dxa = cw[0:1] * dxc
        dcw_ref[0:1, :] = dcw_ref[0:1, :] + _col_sum(dxc * xa)
        for k in range(1, CONV_WIDTH):
            dxc_k = _shift_up(dxc, head, k)
            dxa = dxa + cw[k:k + 1] * dxc_k
            dcw_ref[k:k + 1, :] = dcw_ref[k:k + 1, :] + _col_sum(dxc_k * xa)
        dxc_head[...] = dxc[0:SUBLANES]
        dz_ref[:, :D_MODEL] = dxa.astype(BF16)

    def body(dh1_ref, pa_ref, pb_ref, z_ref, h_ref, h_prev_ref, xc_ref, r_ref, ig_ref, woa_ref, wob_ref, wout_ref,
             lg_ref, lb_ref, ws_ref, bias_ref, cw_ref, wr_ref, wi_ref, lam_ref, dz_ref, mg_ref, dpa_ref, dpb_ref,
             dh1b_ref, dlg_ref, dlb_ref, dws_ref, dbs_ref, dcw_ref, dcb_ref, dwr_ref, dbr_ref, dwi_ref, dbi_ref,
             dlam_ref, dya_ref, dyb_ref, dvn_ref, dsp_acc, lam_carry, dxc_head):
        def cols(ref, first, count):
            return ref.at[:, pl.ds(first * D_MODEL, count * D_MODEL)]

        merge_part(dh1_ref, pa_ref, pb_ref, cols(z_ref, 4, 2), woa_ref, wob_ref, wout_ref, cols(dz_ref, 4, 2), dya_ref,
                   dyb_ref, mg_ref, dpa_ref, dpb_ref, dh1b_ref)
        sgu_part(dyb_ref, cols(z_ref, 2, 1), cols(z_ref, 3, 1), lg_ref, lb_ref, ws_ref, bias_ref, cols(dz_ref, 2, 2),
                 dlg_ref, dlb_ref, dws_ref, dbs_ref, dvn_ref, dsp_acc)
        lru_part(dya_ref, cols(z_ref, 0, 1), cols(z_ref, 1, 1), h_ref, h_prev_ref, xc_ref, r_ref, ig_ref, cw_ref, wr_ref,
                 wi_ref, lam_ref, cols(dz_ref, 0, 2), dcw_ref, dcb_ref, dwr_ref, dbr_ref, dwi_ref, dbi_ref, dlam_ref,
                 lam_carry, dxc_head)

    rev = lambda i: n_tiles - 1 - i
    tile = pl.BlockSpec((SEQ_TILE, D_MODEL), lambda i: (rev(i), 0))
    row = pl.BlockSpec((SEQ_TILE, D_IN), lambda i: (rev(i), 0))
    prev8 = pl.BlockSpec((SUBLANES, D_MODEL), lambda i: (jnp.maximum(rev(i) * per_tile - 1, 0), 0))
    vec = _const((1, D_MODEL))
    sq = _resident((D_MODEL, D_MODEL))
    gate_w = _resident((HEADS, HEAD_DIM, HEAD_DIM))
    gate_acc = _const((HEADS, HEAD_DIM, HEAD_DIM))
    vec_shape = jax.ShapeDtypeStruct((1, D_MODEL), F32)
    gate_shape = jax.ShapeDtypeStruct((HEADS, HEAD_DIM, HEAD_DIM), F32)
    act_bf = jax.ShapeDtypeStruct((t, D_MODEL), BF16)
    return _fused_call(
        body, jobs, name="bwd_mix", grid=(n_tiles,),
        in_specs=[tile, tile, tile, row, tile, prev8, tile, tile, tile, sq, sq, sq, vec, vec,
                  _const((GROUPS, CHUNK, CHUNK)), _const((CHUNK, D_MODEL)), _const((CONV_WIDTH, D_MODEL)), gate_w, gate_w,
                  vec],
        out_specs=[row, tile, tile, tile, tile, vec, vec, _const((CHUNK, GROUPS * CHUNK)), _const((CHUNK, 128)),
                   _const((SUBLANES, D_MODEL)), vec, gate_acc, vec, gate_acc, vec, vec],
        out_shape=[jax.ShapeDtypeStruct((t, D_IN), BF16), act_bf, act_bf, act_bf, act_bf, vec_shape, vec_shape,
                   jax.ShapeDtypeStruct((CHUNK, GROUPS * CHUNK), F32), jax.ShapeDtypeStruct((CHUNK, 128), F32),
                   jax.ShapeDtypeStruct((SUBLANES, D_MODEL), F32), vec_shape, gate_shape, vec_shape, gate_shape,
                   vec_shape, vec_shape],
        scratch_shapes=[pltpu.VMEM((SEQ_TILE, D_MODEL), F32), pltpu.VMEM((SEQ_TILE, D_MODEL), F32),
                        pltpu.VMEM((SEQ_TILE, D_MODEL), F32), pltpu.VMEM((CHUNK, D_MODEL), F32),
                        pltpu.VMEM((1, D_MODEL), F32), pltpu.VMEM((SUBLANES, D_MODEL), F32)],
        compiler_params=_params(),
    )(dh1, pa, pb, z, h, h, xc, r, ig, w_oa, w_ob, w_out, ln_g, ln_b, w_s, bias_full, conv_w, wr, wi, lam)


def _bwd_in(dz, x, dh1, w_in_st, g1, jobs=()):
    t = x.shape[0]

    def body(dz_ref, x_ref, dh1_ref, w_ref, g_ref, dx_ref, dg1_ref):
        @pl.when(pl.program_id(0) == 0)
        def _():
            dg1_ref[...] = jnp.zeros_like(dg1_ref)

        dn1 = jnp.zeros((MM_TILE, D_MODEL), F32)
        for k in range(N_CHIPS):
            dn1 = dn1 + _dot_nt(dz_ref[:, k * IN_SHARD:(k + 1) * IN_SHARD], w_ref[k])
        xhat, r1 = _rms(x_ref[...])
        dg1_ref[...] = dg1_ref[...] + _col_sum(dn1 * xhat)
        dx_ref[...] = dh1_ref[...] + _rms_bwd(dn1 * g_ref[...], xhat, r1)

    tile = pl.BlockSpec((MM_TILE, D_MODEL), lambda i: (i, 0))
    return _fused_call(
        body, jobs, name="bwd_in", grid=(t // MM_TILE,),
        in_specs=[pl.BlockSpec((MM_TILE, D_IN), lambda i: (i, 0)), tile, tile,
                  _resident((N_CHIPS, D_MODEL, IN_SHARD)), _const((1, D_MODEL))],
        out_specs=[tile, _const((1, D_MODEL))],
        out_shape=[jax.ShapeDtypeStruct((t, D_MODEL), F32), jax.ShapeDtypeStruct((1, D_MODEL), F32)],
        compiler_params=_params(),
    )(dz, x, dh1, w_in_st, g1)


def _weight_grad(name, a, b, n_blocks, a_varies, b_varies, width, jobs=()):
    t = a.shape[0]
    rows = min(DW_TILE, t)
    n_t = t // rows

    def body(a_ref, b_ref, o_ref, acc_ref):
        s = pl.program_id(1)
        part = _dot_tn(a_ref[...], b_ref[...])

        @pl.when(s == 0)
        def _():
            acc_ref[...] = part

        @pl.when(s > 0)
        def _():
            acc_ref[...] = acc_ref[...] + part

        @pl.when(s == n_t - 1)
        def _():
            o_ref[...] = acc_ref[...].astype(BF16)

    return _fused_call(
        body, jobs, name=name, grid=(n_blocks, n_t),
        in_specs=[pl.BlockSpec((rows, D_MODEL), (lambda j, s: (s, j)) if a_varies else (lambda j, s: (s, 0))),
                  pl.BlockSpec((rows, width), (lambda j, s: (s, j)) if b_varies else (lambda j, s: (s, 0)))],
        out_specs=pl.BlockSpec((None, D_MODEL, width), lambda j, s: (j, 0, 0)),
        out_shape=jax.ShapeDtypeStruct((n_blocks, D_MODEL, width), BF16),
        scratch_shapes=[pltpu.VMEM((D_MODEL, width), F32)],
        compiler_params=_params(2),
    )(a, b)


def _weight_grads_square(name, pairs, jobs=()):
    n = len(pairs)
    t = pairs[0][0].shape[0]
    rows = min(MM_TILE, t)
    n_t = t // rows

    def body(*refs):
        ins, outs, accs = refs[:2 * n], refs[2 * n:3 * n], refs[3 * n:]
        s = pl.program_id(0)
        for k in range(n):
            part = _dot_tn(ins[2 * k][...], ins[2 * k + 1][...])

            @pl.when(s == 0)
            def _(k=k, part=part):
                accs[k][...] = part

            @pl.when(s > 0)
            def _(k=k, part=part):
                accs[k][...] = accs[k][...] + part

            @pl.when(s == n_t - 1)
            def _(k=k):
                outs[k][...] = accs[k][...].astype(BF16)

    tile = pl.BlockSpec((rows, D_MODEL), lambda s: (s, 0))
    return _fused_call(
        body, jobs, name=name, grid=(n_t,), in_specs=[tile] * (2 * n), out_specs=[_const((D_MODEL, D_MODEL))] * n,
        out_shape=[jax.ShapeDtypeStruct((D_MODEL, D_MODEL), BF16)] * n,
        scratch_shapes=[pltpu.VMEM((D_MODEL, D_MODEL), F32)] * n,
        compiler_params=_params(),
    )(*[x for pair in pairs for x in pair])


def _place():
    x, y, c = lax.axis_index("x"), lax.axis_index("y"), lax.axis_index("c")
    other_chips = [(1 - x, y), (x, 1 - y), (1 - x, 1 - y)]
    return x, y, c, other_chips


def _chip_index(px, py):
    return 2 * px + py


ANY = pl.BlockSpec(memory_space=pl.ANY)
SIBLING = ((0, 0, 1),)
NEIGHBOURS = ((1, 0, 0), (0, 1, 0))
OTHER_CHIPS = NEIGHBOURS + ((1, 1, 0),)


def _near_far(x, y, c):
    return (x ^ (1 - c), y ^ c), (x ^ c, y ^ (1 - c))


def _gather_near_job(shards):
    n = len(shards)
    halves = [s.shape[0] // 2 for s in shards]

    def copies(ins, outs, send, recv, local):
        x, y, c, _ = _place()
        near, _ = _near_far(x, y, c)

        def block(w, chip, pc):
            return outs[w].at[_chip_index(*chip), pl.ds(pc * halves[w], halves[w]), :]

        def copy(w, k, chip, pc, to, src=None):
            return pltpu.make_async_remote_copy(
                src_ref=block(w, chip, pc) if src is None else src, dst_ref=block(w, chip, pc),
                send_sem=send.at[2 * w + k], recv_sem=recv.at[2 * w + k], device_id=to, device_id_type=MESH)

        sends, arrivals, own = [], [], []
        for w in range(n):
            src = ins[w].at[pl.ds(c * halves[w], halves[w]), :]
            own.append(pltpu.make_async_copy(src, block(w, (x, y), c), local.at[w]))
            sends += [copy(w, 0, (x, y), c, (*near, c), src), copy(w, 1, (x, y), c, (x, y, 1 - c), src)]
            arrivals += [copy(w, 0, near, c, (x, y, c)), copy(w, 1, (x, y), 1 - c, (x, y, c))]
        return sends, arrivals, own

    return _Job(shards, [jax.ShapeDtypeStruct((N_CHIPS,) + s.shape, s.dtype) for s in shards], 2 * n, copies,
                NEIGHBOURS + SIBLING, n_local=n)


def _gather_far_job(stacked):
    n = len(stacked)
    halves = [s.shape[1] // 2 for s in stacked]

    def copies(ins, outs, send, recv, local):
        del ins, local
        x, y, c, _ = _place()
        near, far = _near_far(x, y, c)

        def copy(w, k, chip):
            blk = outs[w].at[_chip_index(*chip), pl.ds(c * halves[w], halves[w]), :]
            return pltpu.make_async_remote_copy(
                src_ref=blk, dst_ref=blk, send_sem=send.at[2 * w + k], recv_sem=recv.at[2 * w + k],
                device_id=(*far, c), device_id_type=MESH)

        sends = [copy(w, k, chip) for w in range(n) for k, chip in enumerate(((x, y), near))]
        arrivals = [copy(w, k, chip) for w in range(n) for k, chip in enumerate((far, (1 - x, 1 - y)))]
        return sends, arrivals, []

    return _Job(stacked, [jax.ShapeDtypeStruct(s.shape, s.dtype) for s in stacked], 2 * n, copies, NEIGHBOURS,
                aliases={w: w for w in range(n)})


def _gather_pass_job(stacked):
    n = len(stacked)
    halves = [s.shape[1] // 2 for s in stacked]

    def copies(ins, outs, send, recv, local):
        del ins, local
        x, y, c, chips = _place()

        def copy(w, j, chip, pc, to):
            blk = outs[w].at[_chip_index(*chip), pl.ds(pc * halves[w], halves[w]), :]
            return pltpu.make_async_remote_copy(
                src_ref=blk, dst_ref=blk, send_sem=send.at[3 * w + j], recv_sem=recv.at[3 * w + j], device_id=to,
                device_id_type=MESH)

        sends = [copy(w, j, chip, c, (x, y, 1 - c)) for w in range(n) for j, chip in enumerate(chips)]
        arrivals = [copy(w, j, chip, 1 - c, (x, y, c)) for w in range(n) for j, chip in enumerate(chips)]
        return sends, arrivals, []

    return _Job(stacked, [jax.ShapeDtypeStruct(s.shape, s.dtype) for s in stacked], 3 * n, copies, SIBLING,
                aliases={w: w for w in range(n)})


def _gather_small_job(block):
    def copies(ins, outs, send, recv, local):
        x, y, c, chips = _place()

        def copy(j, chip_from, to):
            return pltpu.make_async_remote_copy(
                src_ref=ins[0], dst_ref=outs[0].at[_chip_index(*chip_from)], send_sem=send.at[j],
                recv_sem=recv.at[j], device_id=to, device_id_type=MESH)

        own = [pltpu.make_async_copy(ins[0], outs[0].at[_chip_index(x, y)], local.at[0])]
        sends = [copy(j, (x, y), (*chip, c)) for j, chip in enumerate(chips)]
        arrivals = [copy(j, chip, (x, y, c)) for j, chip in enumerate(chips)]
        return sends, arrivals, own

    return _Job([block], [jax.ShapeDtypeStruct((N_CHIPS,) + block.shape, block.dtype)], 3, copies, OTHER_CHIPS,
                n_local=1)


def _pair_send_job(grads):
    n = len(grads)
    halves = [g.shape[1] // 2 for g in grads]

    def copies(ins, outs, send, recv, local):
        del local
        x, y, c, _ = _place()
        sends = [pltpu.make_async_remote_copy(
            src_ref=ins[w].at[:, pl.ds((1 - c) * halves[w], halves[w]), :], dst_ref=outs[w], send_sem=send.at[w],
            recv_sem=recv.at[w], device_id=(x, y, 1 - c), device_id_type=MESH) for w in range(n)]
        return sends, sends, []

    return _Job(grads, [jax.ShapeDtypeStruct((N_CHIPS, h, g.shape[2]), g.dtype) for g, h in zip(grads, halves)], n,
                copies, SIBLING)


def _row_block(rows, limit=256):
    return min(rows, limit)


def _pair_add(name, core, mine, theirs):
    _, _, h, cols = mine.shape
    rb = _row_block(h, 512)

    def body(core_ref, a_ref, b_ref, o_ref):
        del core_ref
        o_ref[...] = (a_ref[...].astype(F32) + b_ref[...].astype(F32)).astype(BF16)

    return pl.pallas_call(
        body, name=name,
        grid_spec=pltpu.PrefetchScalarGridSpec(
            num_scalar_prefetch=1, grid=(N_CHIPS, h // rb),
            in_specs=[pl.BlockSpec((None, None, rb, cols), lambda k, r, core_ref: (k, core_ref[0], r, 0)),
                      pl.BlockSpec((None, rb, cols), lambda k, r, core_ref: (k, r, 0))],
            out_specs=pl.BlockSpec((None, rb, cols), lambda k, r, core_ref: (k, r, 0))),
        out_shape=jax.ShapeDtypeStruct(theirs.shape, BF16),
        compiler_params=_params(2),
    )(core, mine, theirs)


def _sequencer_call(name, collective_id, job):
    steps, peers = job.phases, job.peers
    ins = [jax.new_ref(a, memory_space=pltpu.MemorySpace.HBM) for a in job.inputs]
    outs = [ins[{o: i for i, o in job.aliases.items()}[k]] if k in job.aliases.values()
            else jax.empty_ref(shape, memory_space=pltpu.MemorySpace.HBM) for k, shape in enumerate(job.out_shape)]
    sems = [pltpu.SemaphoreType.DMA((n,)) for step in steps for n in (step.n_sem, step.n_sem, max(step.n_local, 1))]

    @pl.kernel(mesh=plsc.ScalarSubcoreMesh(axis_name="sequencer", num_cores=1), name=name, scratch_types=tuple(sems),
               compiler_params=pltpu.CompilerParams(collective_id=collective_id))
    def launch(*sem_refs):
        x, y, c, _ = _place()
        barrier = pltpu.get_barrier_semaphore()
        for dx, dy, dc in peers:
            pl.semaphore_signal(barrier, inc=1, device_id=(x ^ dx, y ^ dy, c ^ dc), device_id_type=MESH)
        pl.semaphore_wait(barrier, len(peers))
        for k, step in enumerate(steps):
            sends, arrivals, own = step.copies(ins if k == 0 else outs, outs, *sem_refs[3 * k:3 * k + 3])
            for cp in own + sends:
                cp.start()
            for cp in arrivals:
                cp.wait_recv()
            for cp in sends:
                cp.wait_send()
            for cp in own:
                cp.wait()

    launch()
    return [ref[...] for ref in outs]


def _chip_exchange_job(sums):
    n = len(sums)

    def copies(ins, outs, send, recv, local):
        del local
        _, _, c, chips = _place()
        sends = [pltpu.make_async_remote_copy(
            src_ref=ins[w].at[_chip_index(*chip)], dst_ref=outs[w].at[j], send_sem=send.at[3 * w + j],
            recv_sem=recv.at[3 * w + j], device_id=(*chip, c), device_id_type=MESH)
            for w in range(n) for j, chip in enumerate(chips)]
        return sends, sends, []

    return _Job(sums, [jax.ShapeDtypeStruct((N_CHIPS - 1,) + s.shape[1:], s.dtype) for s in sums], 3 * n, copies,
                OTHER_CHIPS)


def _chip_sum(name, place, mine, theirs):
    _, h, cols = mine.shape
    rb = _row_block(h, 512)

    def body(place_ref, p_ref, q_ref, o_ref):
        del place_ref
        acc = p_ref[...].astype(F32)
        for j in range(N_CHIPS - 1):
            acc = acc + q_ref[j].astype(F32)
        o_ref[...] = acc

    return pl.pallas_call(
        body, name=name,
        grid_spec=pltpu.PrefetchScalarGridSpec(
            num_scalar_prefetch=1, grid=(h // rb,),
            in_specs=[pl.BlockSpec((None, rb, cols), lambda r, place_ref: (place_ref[0], r, 0)),
                      pl.BlockSpec((N_CHIPS - 1, rb, cols), lambda r, place_ref: (0, r, 0))],
            out_specs=pl.BlockSpec((None, rb, cols), lambda r, place_ref: (place_ref[1], r, 0))),
        out_shape=jax.ShapeDtypeStruct((2, h, cols), F32),
        compiler_params=_params(),
    )(place, mine, theirs)


def _share_job(bufs):
    n = len(bufs)

    def copies(ins, outs, send, recv, local):
        del ins, local
        x, y, c, _ = _place()

        def copy(w, half):
            return pltpu.make_async_remote_copy(
                src_ref=outs[w].at[half], dst_ref=outs[w].at[half], send_sem=send.at[w], recv_sem=recv.at[w],
                device_id=(x, y, 1 - c), device_id_type=MESH)

        return [copy(w, c) for w in range(n)], [copy(w, 1 - c) for w in range(n)], []

    return _Job(bufs, [jax.ShapeDtypeStruct(b.shape, b.dtype) for b in bufs], n, copies, SIBLING,
                aliases={w: w for w in range(n)})


SMALL_ROWS = 24
ROW_G1, ROW_CW, ROW_CB, ROW_BR, ROW_BI, ROW_LAM, ROW_LG, ROW_LB, ROW_G2, ROW_G3, ROW_LOSS, ROW_BS = (
    0, 1, 5, 6, 7, 8, 9, 10, 11, 12, 13, 16)
N_DEV = 8


def _pack_small(dcw, dcb, dbr, dbi, dlam, dlg, dlb, dg2, dg3, loss, dbs):
    def body(dcw_ref, dcb_ref, dbr_ref, dbi_ref, dlam_ref, dlg_ref, dlb_ref, dg2_ref, dg3_ref, loss_ref, dbs_ref, out):
        out[...] = jnp.zeros((SMALL_ROWS, D_MODEL), F32)
        for row, ref in ((ROW_CB, dcb_ref), (ROW_BR, dbr_ref), (ROW_BI, dbi_ref), (ROW_LAM, dlam_ref),
                         (ROW_LG, dlg_ref), (ROW_LB, dlb_ref), (ROW_G2, dg2_ref), (ROW_G3, dg3_ref)):
            out[row:row + 1, :] = ref[...]
        out[ROW_CW:ROW_CW + CONV_WIDTH, :] = dcw_ref[0:CONV_WIDTH, :]
        out[ROW_LOSS:ROW_LOSS + 1, 0:128] = loss_ref[0:1, :]
        out[ROW_BS:ROW_BS + GROUPS, 0:128] = jnp.transpose(dbs_ref[...])[0:GROUPS, :]

    vm = pl.BlockSpec(memory_space=pltpu.VMEM)
    return pl.pallas_call(
        body, name="pack_small", in_specs=[vm] * 11, out_specs=vm,
        out_shape=jax.ShapeDtypeStruct((SMALL_ROWS, D_MODEL), F32),
    )(dcw, dcb, dbr, dbi, dlam, dlg, dlb, dg2, dg3, loss, dbs)


def _gather_all_job(blocks):
    n = len(blocks)
    flips = [(dx, dy, dc) for dx in (0, 1) for dy in (0, 1) for dc in (0, 1)][1:]

    def copies(ins, outs, send, recv, local):
        x, y, c, _ = _place()
        me = 4 * x + 2 * y + c
        sends, arrivals, own = [], [], []
        for w in range(n):
            own.append(pltpu.make_async_copy(ins[w], outs[w].at[me], local.at[w]))
            for k, (dx, dy, dc) in enumerate(flips):
                peer = (x ^ dx, y ^ dy, c ^ dc)
                sem = dict(send_sem=send.at[7 * w + k], recv_sem=recv.at[7 * w + k])
                sends.append(pltpu.make_async_remote_copy(
                    src_ref=ins[w], dst_ref=outs[w].at[me], device_id=peer, device_id_type=MESH, **sem))
                arrivals.append(pltpu.make_async_remote_copy(
                    src_ref=ins[w], dst_ref=outs[w].at[4 * peer[0] + 2 * peer[1] + peer[2]], device_id=peer,
                    device_id_type=MESH, **sem))
        return sends, arrivals, own

    return _Job(blocks, [jax.ShapeDtypeStruct((N_DEV,) + b.shape, b.dtype) for b in blocks], 7 * n, copies,
                OTHER_CHIPS + SIBLING + tuple((dx, dy, 1) for dx, dy, _ in OTHER_CHIPS), n_local=n)


def _sum_small(vec_all, ws_all, dg1_all):
    def body(vec_ref, ws_ref, dg1_ref, vec_out, ws_out):
        vec, ws, dg1 = vec_ref[0], ws_ref[0], dg1_ref[0]
        for d in range(1, N_DEV):
            vec, ws, dg1 = vec + vec_ref[d], ws + ws_ref[d], dg1 + dg1_ref[d]
        vec_out[...] = vec
        vec_out[ROW_G1:ROW_G1 + 1, :] = dg1
        ws_out[...] = ws

    vm = pl.BlockSpec(memory_space=pltpu.VMEM)
    return pl.pallas_call(
        body, name="sum_small", in_specs=[vm] * 3, out_specs=[vm, vm],
        out_shape=[jax.ShapeDtypeStruct(vec_all.shape[1:], F32), jax.ShapeDtypeStruct(ws_all.shape[1:], F32)],
    )(vec_all, ws_all, dg1_all)


def _adamw_math(w, g, m, v):
    m = ADAM_B1 * m + (1.0 - ADAM_B1) * g
    v = ADAM_B2 * v + (1.0 - ADAM_B2) * (g * g)
    m_hat = m / (1.0 - ADAM_B1 ** ADAM_STEP)
    v_hat = v / (1.0 - ADAM_B2 ** ADAM_STEP)
    delta = (-ADAM_LR) * (m_hat / (jnp.sqrt(v_hat) + ADAM_EPS) + ADAM_WD * w)
    return delta, m, v


def _adamw(name, g, w, m, v, jobs=()):
    rows, cols = w.shape
    rb = _row_block(rows)

    def body(g_ref, w_ref, m_ref, v_ref, d_ref, nm_ref, nv_ref):
        d_ref[...], nm_ref[...], nv_ref[...] = _adamw_math(w_ref[...], g_ref[...], m_ref[...], v_ref[...])

    blk = pl.BlockSpec((rb, cols), lambda r: (r, 0))
    return _fused_call(
        body, jobs, name=name, grid=(rows // rb,), in_specs=[blk] * 4, out_specs=[blk] * 3,
        out_shape=[jax.ShapeDtypeStruct(w.shape, F32)] * 3, compiler_params=_params(),
    )(g, w, m, v)


def _adamw_small(grads, ws, ms, vs):
    n = len(grads)

    def body(*refs):
        g_refs, w_refs, m_refs, v_refs = refs[:n], refs[n:2 * n], refs[2 * n:3 * n], refs[3 * n:4 * n]
        outs = refs[4 * n:]
        for p in range(n):
            d, nm, nv = _adamw_math(w_refs[p][...], g_refs[p][...], m_refs[p][...], v_refs[p][...])
            outs[p][...] = d
            outs[n + p][...] = nm
            outs[2 * n + p][...] = nv

    vm = pl.BlockSpec(memory_space=pltpu.VMEM)
    shapes = [jax.ShapeDtypeStruct(w.shape, F32) for w in ws]
    out = pl.pallas_call(
        body, name="adamw_small", in_specs=[vm] * (4 * n), out_specs=[vm] * (3 * n), out_shape=shapes * 3,
    )(*grads, *ws, *ms, *vs)
    return out[:n], out[n:2 * n], out[2 * n:]


def _unstack_heads(w_st):
    per = HEAD_DIM // N_CHIPS
    return w_st.reshape(N_CHIPS, HEADS, per, HEAD_DIM).transpose(1, 0, 2, 3).reshape(HEADS, HEAD_DIM, HEAD_DIM)


def _stack_heads(w):
    per = HEAD_DIM // N_CHIPS
    return w.reshape(HEADS, N_CHIPS, per, HEAD_DIM).transpose(1, 0, 2, 3).reshape(N_CHIPS, HEADS * per, HEAD_DIM)


def kernel(x, norm_mix_g, w_in, conv_w, conv_b, w_rgate, b_rgate, w_igate, b_igate, lru_lambda, w_out_a, sgu_ln_g, sgu_ln_b, sgu_w_s, sgu_b_s, w_out_b, w_out, norm_mlp_g, w_up, w_down, norm_final_g, loss_target, m_norm_mix_g, m_w_in, m_conv_w, m_conv_b, m_w_rgate, m_b_rgate, m_w_igate, m_b_igate, m_lru_lambda, m_w_out_a, m_sgu_ln_g, m_sgu_ln_b, m_sgu_w_s, m_sgu_b_s, m_w_out_b, m_w_out, m_norm_mlp_g, m_w_up, m_w_down, m_norm_final_g, v_norm_mix_g, v_w_in, v_conv_w, v_conv_b, v_w_rgate, v_b_rgate, v_w_igate, v_b_igate, v_lru_lambda, v_w_out_a, v_sgu_ln_g, v_sgu_ln_b, v_sgu_w_s, v_sgu_b_s, v_w_out_b, v_w_out, v_norm_mlp_g, v_w_up, v_w_down, v_norm_final_g):
    chip = _chip_index(lax.axis_index("x"), lax.axis_index("y"))
    core = lax.axis_index("c")
    quarter_h = HEAD_DIM // N_CHIPS
    quarter_d = D_MODEL // N_CHIPS

    as_2d = lambda a: a.reshape(-1, a.shape[-1])
    big_w = [as_2d(w) for w in (w_in, w_rgate, w_igate, w_out_a, w_out_b, w_out, w_up, w_down)]
    big_m = [as_2d(w) for w in (m_w_in, m_w_rgate, m_w_igate, m_w_out_a, m_w_out_b, m_w_out, m_w_up, m_w_down)]
    big_v = [as_2d(w) for w in (v_w_in, v_w_rgate, v_w_igate, v_w_out_a, v_w_out_b, v_w_out, v_w_up, v_w_down)]

    packed = jnp.concatenate([conv_w[0], b_rgate[0], b_igate[0]], axis=1)
    packed = jnp.concatenate([packed, jnp.zeros_like(packed)], axis=0)
    s_in, s_r, s_i, s_oa, s_ob, s_out, s_up, s_down = [w.astype(BF16) for w in big_w]
    xs, target = x[0], loss_target[0]
    g3 = norm_final_g.reshape(1, D_MODEL)
    bias_s = jnp.broadcast_to(jnp.transpose(sgu_b_s[0])[:, :, None], (CHUNK, GROUPS, GROUP_DIM)).reshape(CHUNK, D_MODEL)
    core_arr = core.reshape(1).astype(jnp.int32)
    place = jnp.stack([chip, core]).astype(jnp.int32)
    quarter = lambda g: g.reshape(N_CHIPS, D_MODEL // N_CHIPS, D_MODEL)

    def pair_add(nm, g, from_sibling):
        return _pair_add("pair_add_" + nm, core_arr, g.reshape(N_CHIPS, 2, g.shape[1] // 2, g.shape[2]), from_sibling)

    def chip_sum(nm, pair, from_chips):
        return _chip_sum("chip_sum_" + nm, place, pair, from_chips)

    order = jnp.stack([chip, chip ^ 2, chip ^ 1, chip ^ 3]).astype(jnp.int32)
    (z, n1, (w_in_st, wr_st, wi_st)), ((packed_all,), late) = _fwd_in(
        xs, norm_mix_g, [s_in, s_r, s_i], order,
        jobs=[_gather_small_job(packed), _gather_near_job([s_oa, s_ob, s_out])])
    pick = lambda lo, hi: packed_all[:, :HEADS, lo:hi].transpose(1, 0, 2).reshape(HEADS, -1)
    conv_w_full = pick(0, quarter_d)
    br_full = pick(quarter_d, quarter_d + quarter_h).reshape(1, D_MODEL)
    bi_full = pick(quarter_d + quarter_h, quarter_d + 2 * quarter_h).reshape(1, D_MODEL)
    wr, wi = _unstack_heads(wr_st), _unstack_heads(wi_st)
    lru = (conv_w_full, conv_b, wr, br_full, wi, bi_full, lru_lambda)
    sgu = (sgu_ln_g, sgu_ln_b, sgu_w_s[0], bias_s)

    after = lambda arrays, result: lax.optimization_barrier((arrays, result))[0]
    w_up_st, w_dn = _sequencer_call(
        "gather_mlp", 8, _gather_near_job(after([s_up, s_down], n1)).then(_gather_far_job).then(_gather_pass_job))
    w_dn = w_dn.reshape(D_FF, D_MODEL)
    half_way = (xs.shape[0] // SEQ_TILE // 2 - 1,)
    (ya, *saved), (late,) = _fwd_lru(z, *lru, jobs=[_gather_far_job(late).then(_gather_pass_job, at=half_way)])
    w_oa, w_ob, w_o = [w.reshape(D_MODEL, D_MODEL) for w in late]
    (yb, pa, pb, h1, n2), _ = _fwd_sgu_merge(ya, z, xs, *sgu, w_oa, w_ob, w_o, norm_mlp_g)
    (act, dup, dh2b, dh1, loss_part, dg3, dg2), _ = _mlp(n2, h1, target, w_up_st, w_dn, norm_mlp_g, g3)

    d_down, _ = _weight_grad("dw_down", act, dh2b, N_CHIPS, True, False, D_MODEL)
    r_down, = _sequencer_call("send_w_down", 10, _pair_send_job([d_down]))
    d_up, _ = _weight_grad("dw_up", n2, dup, N_CHIPS, False, True, D_MODEL)
    r_up, = _sequencer_call("send_w_up", 11, _pair_send_job([d_up]))
    p_down, p_up = pair_add("w_down", d_down, r_down), pair_add("w_up", d_up, r_up)
    (dz, merged, dpa, dpb, dh1b, dlg, dlb, dws, dbs, dcw, dcb, dwr, dbr, dwi, dbi, dlam), ((q_up, q_down),) = _bwd_mix(
        dh1, pa, pb, z, *saved, w_oa, w_ob, w_o, *sgu, conv_w_full, wr, wi, lru_lambda,
        jobs=[_chip_exchange_job([p_up, p_down])])
    half_up, half_down = chip_sum("w_up", p_up, q_up), chip_sum("w_down", p_down, q_down)
    names = ("w_in", "w_rgate", "w_igate", "w_out_a", "w_out_b", "w_out", "w_up", "w_down")
    (d_out, d_oa, d_ob), ((full_up, full_down),) = _weight_grads_square(
        "dw_projections", [(merged, dh1b), (ya, dpa), (yb, dpb)], jobs=[_share_job([half_up, half_down])])
    mids = [quarter(d_oa), quarter(d_ob), quarter(d_out)]
    r_mids = _sequencer_call("send_mids", 1, _pair_send_job(mids))
    gates = [_stack_heads(dwr).astype(BF16), _stack_heads(dwi).astype(BF16)]
    small = _pack_small(dcw, dcb, dbr, dbi, dlam, dlg, dlb, dg2, dg3, loss_part, dbs)
    p_mids = [pair_add(nm, g, r) for nm, g, r in zip(names[3:6], mids, r_mids)]
    q_mids = _sequencer_call("exchange_mids", 2, _chip_exchange_job(p_mids))
    d_in, (r_gates, (vec_all, ws_all)) = _weight_grad(
        "dw_in", n1, dz, N_CHIPS, False, True, IN_SHARD,
        jobs=[_pair_send_job(gates), _gather_all_job([small, dws])])
    r_in, = _sequencer_call("send_w_in", 3, _pair_send_job([d_in]))
    adam_args = {nm: (w, m, v) for nm, w, m, v in zip(names, big_w, big_m, big_v)}

    def adamw(nm, g):
        w, m, v = adam_args[nm]
        g = g.reshape(w.shape)
        return g, _adamw("adamw_" + nm, g, w, m, v)[0]

    p_gates = [pair_add(nm, g, r) for nm, g, r in zip(names[1:3], gates, r_gates)]
    half_mids = [chip_sum(nm, p, q) for nm, p, q in zip(names[3:6], p_mids, q_mids)]
    full_mids = _sequencer_call("share_mids", 12, _share_job(half_mids))
    p_first = [pair_add("w_in", d_in, r_in)] + p_gates
    q_first = _sequencer_call("exchange_w_in", 4, _chip_exchange_job(p_first))
    (grad_x, dg1), _ = _bwd_in(dz, xs, dh1, w_in_st, norm_mix_g)
    dg1_all, = _sequencer_call("gather_dg1", 6, _gather_all_job([dg1]))
    done = {nm: adamw(nm, f) for nm, f in zip(("w_up", "w_down") + names[3:6], [full_up, full_down] + full_mids)}
    q_first = after(q_first, [out[0] for _, out in done.values()])
    half_first = [chip_sum(nm, p, q) for nm, p, q in zip(names[:3], p_first, q_first)]
    full_first = _sequencer_call("share_last", 5, _share_job(half_first))
    done.update({nm: adamw(nm, f) for nm, f in zip(names[:3], full_first)})
    full, big_out = [done[nm][0] for nm in names], [done[nm][1] for nm in names]

    vec, ws_sum = _sum_small(vec_all, ws_all, dg1_all)
    row = lambda r: vec[r:r + 1]
    shard = lambda a, width: lax.dynamic_slice_in_dim(a, chip * width, width, axis=1)
    g_small = dict(
        norm_mix_g=row(ROW_G1), conv_w=shard(vec[ROW_CW:ROW_CW + CONV_WIDTH], quarter_d), conv_b=row(ROW_CB),
        b_rgate=shard(row(ROW_BR).reshape(HEADS, HEAD_DIM), quarter_h),
        b_igate=shard(row(ROW_BI).reshape(HEADS, HEAD_DIM), quarter_h), lru_lambda=row(ROW_LAM),
        sgu_ln_g=row(ROW_LG), sgu_ln_b=row(ROW_LB),
        sgu_w_s=ws_sum.reshape(CHUNK, GROUPS, CHUNK).transpose(1, 0, 2).reshape(GROUPS * CHUNK, CHUNK),
        sgu_b_s=vec[ROW_BS:ROW_BS + GROUPS, 0:CHUNK], norm_mlp_g=row(ROW_G2), norm_final_g=row(ROW_G3))
    loss = vec[ROW_LOSS, 0]
    small_names = list(g_small)
    given = dict(
        norm_mix_g=(norm_mix_g, m_norm_mix_g, v_norm_mix_g), conv_w=(conv_w, m_conv_w, v_conv_w),
        conv_b=(conv_b, m_conv_b, v_conv_b), b_rgate=(b_rgate, m_b_rgate, v_b_rgate),
        b_igate=(b_igate, m_b_igate, v_b_igate), lru_lambda=(lru_lambda, m_lru_lambda, v_lru_lambda),
        sgu_ln_g=(sgu_ln_g, m_sgu_ln_g, v_sgu_ln_g), sgu_ln_b=(sgu_ln_b, m_sgu_ln_b, v_sgu_ln_b),
        sgu_w_s=(sgu_w_s, m_sgu_w_s, v_sgu_w_s), sgu_b_s=(sgu_b_s, m_sgu_b_s, v_sgu_b_s),
        norm_mlp_g=(norm_mlp_g, m_norm_mlp_g, v_norm_mlp_g), norm_final_g=(norm_final_g, m_norm_final_g, v_norm_final_g))
    g2d = [g_small[nm] for nm in small_names]
    to2d = lambda a, g: a.reshape(g.shape)
    d_s, m_s, v_s = _adamw_small(
        g2d, *[[to2d(given[nm][q], g) for nm, g in zip(small_names, g2d)] for q in range(3)])

    shapes = dict(
        norm_mix_g=norm_mix_g, w_in=w_in, conv_w=conv_w, conv_b=conv_b, w_rgate=w_rgate, b_rgate=b_rgate,
        w_igate=w_igate, b_igate=b_igate, lru_lambda=lru_lambda, w_out_a=w_out_a, sgu_ln_g=sgu_ln_g,
        sgu_ln_b=sgu_ln_b, sgu_w_s=sgu_w_s, sgu_b_s=sgu_b_s, w_out_b=w_out_b, w_out=w_out, norm_mlp_g=norm_mlp_g,
        w_up=w_up, w_down=w_down, norm_final_g=norm_final_g)
    grads, deltas, new_m, new_v = {}, {}, {}, {}
    for nm, g, (d, nmom, nvar) in zip(names, full, big_out):
        grads[nm], deltas[nm], new_m[nm], new_v[nm] = g, d, nmom, nvar
    for p, nm in enumerate(small_names):
        grads[nm], deltas[nm], new_m[nm], new_v[nm] = g2d[p], d_s[p], m_s[p], v_s[p]
    order = list(shapes)
    out = [loss, grad_x[None]]
    for group in (grads, deltas, new_m, new_v):
        out += [group[nm].reshape(shapes[nm].shape) for nm in order]
    return tuple(out)
```

```python
import functools

import jax
import jax.numpy as jnp
from jax import lax
from jax.experimental import pallas as pl
from jax.experimental.pallas import tpu as pltpu
from jax.experimental.pallas import tpu_sc as plsc

F32 = jnp.float32
BF16 = jnp.bfloat16
MESH = pl.DeviceIdType.MESH

D_MODEL = 1024
D_IN = 6 * D_MODEL
D_FF = 4 * D_MODEL
N_CHIPS = 4
IN_SHARD = D_IN // N_CHIPS
HEADS = 4
HEAD_DIM = D_MODEL // HEADS
GROUPS = 4
GROUP_DIM = D_MODEL // GROUPS
CHUNK = 128
CONV_WIDTH = 4
LRU_C = 8.0
NORM_EPS = 1e-6
LN_EPS = 1e-5

ADAM_LR = 0.001
ADAM_B1 = 0.9
ADAM_B2 = 0.999
ADAM_EPS = 1e-08
ADAM_WD = 0.01
ADAM_STEP = 10

SUBLANES = 8
MM_TILE = 512
IN_TILE = 1024
SEQ_TILE = 256
DW_TILE = 2048
VMEM_LIMIT_BYTES = 56 * 1024 * 1024

GELU_K0 = 0.7978845608028654
GELU_K1 = 0.044715


def _params(n_grid_axes=1):
    return pltpu.CompilerParams(
        dimension_semantics=("arbitrary",) * n_grid_axes, vmem_limit_bytes=VMEM_LIMIT_BYTES)


def _resident(shape):
    nd = len(shape)
    return pl.BlockSpec(shape, lambda *_: (0,) * nd, pipeline_mode=pl.Buffered(1))


def _const(shape):
    nd = len(shape)
    return pl.BlockSpec(shape, lambda *_: (0,) * nd)


def _dot(a, b):
    return jnp.dot(a, b, preferred_element_type=F32)


def _dot_nt(a, b):
    return lax.dot_general(a, b, (((1,), (1,)), ((), ())), preferred_element_type=F32)


def _dot_tn(a, b):
    return lax.dot_general(a, b, (((0,), (0,)), ((), ())), preferred_element_type=F32)


def _gelu(x):
    t = jnp.tanh(GELU_K0 * x * (1.0 + GELU_K1 * x * x))
    return 0.5 * x * (1.0 + t)


def _gelu_and_grad(x):
    x2 = x * x
    t = jnp.tanh(GELU_K0 * x * (1.0 + GELU_K1 * x2))
    g = 0.5 * x * (1.0 + t)
    dg = 0.5 * (1.0 + t) + 0.5 * x * (1.0 - t * t) * (GELU_K0 * (1.0 + 3.0 * GELU_K1 * x2))
    return g, dg


def _rms(x):
    r = lax.rsqrt(jnp.mean(x * x, axis=-1, keepdims=True) + NORM_EPS)
    return x * r, r


def _rms_bwd(dn, xhat, r):
    return r * (dn - xhat * jnp.mean(dn * xhat, axis=-1, keepdims=True))


def _col_sum(v):
    return jnp.sum(v, axis=0, keepdims=True)


def _shift_down(x, tail8, k):
    xs = pltpu.roll(x, k, 0)
    ts = pltpu.roll(tail8, k, 0)
    ridx = lax.broadcasted_iota(jnp.int32, tail8.shape, 0)
    head = jnp.where(ridx < k, ts, xs[0:SUBLANES])
    return jnp.concatenate([head, xs[SUBLANES:]], axis=0)


def _shift_up(x, head8, k):
    n = x.shape[0]
    xs = pltpu.roll(x, n - k, 0)
    hs = pltpu.roll(head8, SUBLANES - k, 0)
    ridx = lax.broadcasted_iota(jnp.int32, head8.shape, 0)
    last = jnp.where(ridx >= SUBLANES - k, hs, xs[n - SUBLANES:n])
    return jnp.concatenate([xs[:n - SUBLANES], last], axis=0)


def _scan_forward(a, b, carry):
    n, cols = a.shape
    groups = n // SUBLANES
    a = a.reshape(groups, SUBLANES, cols)
    b = b.reshape(groups, SUBLANES, cols)
    sub = lax.broadcasted_iota(jnp.int32, a.shape, 1)
    for s in (1, 2, 4):
        a_s = pltpu.roll(a, s, 1)
        b_s = pltpu.roll(b, s, 1)
        m = sub >= s
        b = jnp.where(m, a * b_s + b, b)
        a = jnp.where(m, a * a_s, a)
    out = []
    for g in range(groups):
        h = a[g] * carry + b[g]
        out.append(h)
        carry = h[SUBLANES - 1:SUBLANES]
    return jnp.concatenate(out, axis=0), carry


def _scan_backward(a, b, carry):
    n, cols = a.shape
    groups = n // SUBLANES
    a = a.reshape(groups, SUBLANES, cols)
    b = b.reshape(groups, SUBLANES, cols)
    sub = lax.broadcasted_iota(jnp.int32, a.shape, 1)
    for s in (1, 2, 4):
        a_s = pltpu.roll(a, SUBLANES - s, 1)
        b_s = pltpu.roll(b, SUBLANES - s, 1)
        m = sub < SUBLANES - s
        b = jnp.where(m, a * b_s + b, b)
        a = jnp.where(m, a * a_s, a)
    out = [None] * groups
    for g in reversed(range(groups)):
        h = a[g] * carry + b[g]
        out[g] = h
        carry = h[0:1]
    return jnp.concatenate(out, axis=0), carry


def _softplus_neg(lam):
    e = jnp.exp(-jnp.abs(lam))
    u = 1.0 + e
    log1p_e = jnp.where(u == 1.0, e, jnp.log(u) * (e / jnp.where(u == 1.0, 1.0, u - 1.0)))
    return jnp.maximum(-lam, 0.0) + log1p_e


def _lru_gates(xa, tail8, cw_ref, cb_ref, wr_ref, br_ref, wi_ref, bi_ref, lam_ref):
    cw = cw_ref[...]
    xc = cb_ref[...] + cw[0:1] * xa
    for k in range(1, CONV_WIDTH):
        xc = xc + cw[k:k + 1] * _shift_down(xa, tail8, k)
    xcb = xc.astype(BF16)
    pre_r, pre_i = [], []
    for h in range(HEADS):
        cols = slice(h * HEAD_DIM, (h + 1) * HEAD_DIM)
        pre_r.append(_dot(xcb[:, cols], wr_ref[h]))
        pre_i.append(_dot(xcb[:, cols], wi_ref[h]))
    r = jax.nn.sigmoid(jnp.concatenate(pre_r, axis=1) + br_ref[...])
    ig = jax.nn.sigmoid(jnp.concatenate(pre_i, axis=1) + bi_ref[...])
    _, a, mult = _decay(r, lam_ref)
    return xc, r, ig, a, mult


def _decay(r, lam_ref):
    sp = _softplus_neg(lam_ref[...])
    log_a = ((-LRU_C) * sp) * r
    a = jnp.exp(log_a)
    th = jnp.tanh(log_a)
    return sp, a, jnp.sqrt((-2.0 * th) / (1.0 - th))


class _Phase:
    def __init__(self, copies, n_sem, n_local, start=None, finish=None):
        self.copies, self.n_sem, self.n_local, self.start, self.finish = copies, n_sem, n_local, start, finish


class _Job:
    def __init__(self, inputs, out_shape, n_sem, copies, peers, aliases=None, n_local=0):
        self.inputs, self.out_shape = list(inputs), list(out_shape)
        self.aliases = dict(aliases or {})
        self.phases = [_Phase(copies, n_sem, n_local)]
        self.peers = tuple(peers)

    def then(self, make, at=None):
        nxt = make(self.out_shape)
        self.phases[-1].finish = at
        nxt.phases[0].start = at
        self.phases += nxt.phases
        self.peers = tuple(sorted(set(self.peers + nxt.peers)))
        return self


def _fused_call(body, jobs, *, name, grid, in_specs, out_specs, out_shape, scratch_shapes=(),
                input_output_aliases=None, compiler_params=None, n_prefetch=0, jobs_start_after=None):
    single = not isinstance(out_shape, (list, tuple))
    out_specs = [out_specs] if single else list(out_specs)
    out_shape = [out_shape] if single else list(out_shape)
    n_scr = len(scratch_shapes)
    in_specs, scratch_shapes = list(in_specs), list(scratch_shapes)
    n_in, n_out = len(in_specs), len(out_shape)
    aliases = dict(input_output_aliases or {})
    in_at, out_at, phases = [], [], []
    for q, job in enumerate(jobs):
        in_at.append(len(in_specs))
        out_at.append(len(out_shape))
        for i, o in job.aliases.items():
            aliases[n_prefetch + len(in_specs) + i] = len(out_shape) + o
        in_specs += [ANY] * len(job.inputs)
        out_specs += [ANY] * len(job.out_shape)
        out_shape += job.out_shape
        for k, phase in enumerate(job.phases):
            phases.append((q, k, phase, len(scratch_shapes)))
            scratch_shapes += [pltpu.SemaphoreType.DMA((phase.n_sem,)), pltpu.SemaphoreType.DMA((phase.n_sem,)),
                               pltpu.SemaphoreType.DMA((max(phase.n_local, 1),))]
    n_in_all, n_out_all = len(in_specs), len(out_shape)
    first_step, last_step = (0,) * len(grid), tuple(g - 1 for g in grid)

    def full_body(*refs):
        prefetch, refs = refs[:n_prefetch], refs[n_prefetch:]
        ins, outs, scr = refs[:n_in_all], refs[n_in_all:n_in_all + n_out_all], refs[n_in_all + n_out_all:]
        ids = [pl.program_id(a) for a in range(len(grid))]
        at_step = lambda step: functools.reduce(jnp.logical_and, [i == k for i, k in zip(ids, step)])

        def copies(q, k, phase, sem_at):
            job = jobs[q]
            mine = outs[out_at[q]:out_at[q] + len(job.out_shape)]
            return phase.copies(ins[in_at[q]:in_at[q] + len(job.inputs)] if k == 0 else mine, mine,
                                *scr[sem_at:sem_at + 3])

        def start(*phase):
            def go():
                sends, _, local = copies(*phase)
                for cp in local + sends:
                    cp.start()
            return go

        def finish(*phase):
            def go():
                sends, arrivals, local = copies(*phase)
                for cp in arrivals:
                    cp.wait_recv()
                for cp in sends:
                    cp.wait_send()
                for cp in local:
                    cp.wait()
            return go

        for phase in phases:
            if phase[2].start is None and jobs_start_after is None:
                pl.when(at_step(first_step))(start(*phase))
        body(*prefetch, *ins[:n_in], *outs[:n_out], *scr[:n_scr])
        for phase in phases:
            pl.when(at_step(phase[2].finish or last_step))(finish(*phase))
            nxt = phase[2].start or jobs_start_after
            if nxt is not None:
                pl.when(at_step(nxt))(start(*phase))

    if n_prefetch:
        layout = dict(grid_spec=pltpu.PrefetchScalarGridSpec(
            num_scalar_prefetch=n_prefetch, grid=grid, in_specs=in_specs, out_specs=out_specs,
            scratch_shapes=scratch_shapes))
    else:
        layout = dict(grid=grid, in_specs=in_specs, out_specs=out_specs, scratch_shapes=scratch_shapes)
    call = pl.pallas_call(
        full_body, name=name, out_shape=out_shape, input_output_aliases=aliases, compiler_params=compiler_params,
        **layout)

    def run(*args):
        res = call(*args, *[a for job in jobs for a in job.inputs])
        mine = res[0] if single else list(res[:n_out])
        return mine, [list(res[at:at + len(job.out_shape)]) for at, job in zip(out_at, jobs)]

    return run


def _fwd_in(x, g1, shards, order, jobs=()):
    t = x.shape[0]
    rows_per_step = min(IN_TILE, t)
    n_tiles = t // rows_per_step
    n = len(shards)
    halves = [s.shape[0] // 2 for s in shards]

    def body(order_ref, x_ref, g_ref, *refs):
        del order_ref
        ins, (z_ref, n_ref), outs = refs[:n], refs[n:n + 2], refs[n + 2:2 * n + 2]
        wbuf, nbuf, send, recv, local = refs[2 * n + 2:]
        s, i = pl.program_id(0), pl.program_id(1)
        x_, y_, c, chips = _place()
        k_me = _chip_index(x_, y_)

        def block(w, chip, pc):
            return outs[w].at[_chip_index(*chip), pl.ds(pc * halves[w], halves[w]), :]

        def over_ici(w, j, landing):
            return pltpu.make_async_remote_copy(
                src_ref=ins[w].at[pl.ds(c * halves[w], halves[w]), :],
                dst_ref=block(w, chips[j] if landing else (x_, y_), c), send_sem=send.at[6 * w + j],
                recv_sem=recv.at[6 * w + j], device_id=(*chips[j], c), device_id_type=MESH)

        def to_sibling(w, j, landing):
            blk = block(w, chips[j], 1 - c if landing else c)
            return pltpu.make_async_remote_copy(
                src_ref=blk, dst_ref=blk, send_sem=send.at[6 * w + 3 + j], recv_sem=recv.at[6 * w + 3 + j],
                device_id=(x_, y_, 1 - c), device_id_type=MESH)

        own = [pltpu.make_async_copy(wbuf, outs[0].at[k_me], local.at[0])]
        own += [pltpu.make_async_copy(ins[w], outs[w].at[k_me], local.at[w]) for w in range(1, n)]

        @pl.when((s == 0) & (i == 0))
        def _():
            for j in range(2):
                for w in range(n):
                    over_ici(w, j, False).start()
            load = pltpu.make_async_copy(ins[0], wbuf, local.at[n])
            load.start()
            load.wait()
            for cp in own:
                cp.start()

        for j in range(N_CHIPS - 1):
            @pl.when((s == j + 1) & (i == 0))
            def _(j=j):
                for w in range(n):
                    over_ici(w, j, True).wait_recv()
                for w in range(n):
                    to_sibling(w, j, False).start()
                if j == 0:
                    for w in range(n):
                        over_ici(w, 2, False).start()
                    own[0].wait()
                for w in range(n):
                    to_sibling(w, j, True).wait_recv()
                load = pltpu.make_async_copy(outs[0].at[_chip_index(*chips[j])], wbuf, local.at[n])
                load.start()
                load.wait()

        rows = pl.ds(pl.multiple_of(i * rows_per_step, rows_per_step), rows_per_step)

        @pl.when(s == 0)
        def _():
            xhat, _ = _rms(x_ref[...])
            nrm = (xhat * g_ref[...]).astype(BF16)
            nbuf[rows, :] = nrm
            n_ref[...] = nrm

        z_ref[...] = _dot(nbuf[rows, :], wbuf[...])

        @pl.when((s == N_CHIPS - 1) & (i == n_tiles - 1))
        def _():
            for j in range(N_CHIPS - 1):
                for w in range(n):
                    over_ici(w, j, False).wait_send()
                    to_sibling(w, j, False).wait_send()
            for cp in own[1:]:
                cp.wait()

    once = lambda s, i, order: (jnp.where(s == 0, i, n_tiles - 1), 0)
    (z, n1, *stacked), job_outs = _fused_call(
        body, jobs, name="fwd_in", grid=(N_CHIPS, n_tiles), n_prefetch=1,
        in_specs=[pl.BlockSpec((rows_per_step, D_MODEL), once), _const((1, D_MODEL))] + [ANY] * n,
        out_specs=[pl.BlockSpec((rows_per_step, IN_SHARD), lambda s, i, order: (i, order[s])),
                   pl.BlockSpec((rows_per_step, D_MODEL), once)] + [ANY] * n,
        out_shape=[jax.ShapeDtypeStruct((t, D_IN), F32), jax.ShapeDtypeStruct((t, D_MODEL), BF16)]
        + [jax.ShapeDtypeStruct((N_CHIPS,) + s.shape, s.dtype) for s in shards],
        scratch_shapes=[pltpu.VMEM(shards[0].shape, BF16), pltpu.VMEM((t, D_MODEL), BF16),
                        pltpu.SemaphoreType.DMA((6 * n,)),
                        pltpu.SemaphoreType.DMA((6 * n,)), pltpu.SemaphoreType.DMA((n + 1,))],
        compiler_params=_params(2), jobs_start_after=(1, 0),
    )(order, x, g1, *shards)
    return (z, n1, stacked), job_outs


def _fwd_lru(z, conv_w, conv_b, wr, br, wi, bi, lam, jobs=()):
    t = z.shape[0]

    def body(xa_ref, ga_ref, cw_ref, cb_ref, wr_ref, br_ref, wi_ref, bi_ref, lam_ref, ya_ref, h_ref, xc_ref, r_ref,
             ig_ref, tail_ref, carry_ref):
        @pl.when(pl.program_id(0) == 0)
        def _():
            tail_ref[...] = jnp.zeros_like(tail_ref)
            carry_ref[...] = jnp.zeros_like(carry_ref)

        xa = xa_ref[...]
        xc, r, ig, a, mult = _lru_gates(xa, tail_ref[...], cw_ref, cb_ref, wr_ref, br_ref, wi_ref, bi_ref, lam_ref)
        tail_ref[...] = xa[SEQ_TILE - SUBLANES:]
        xc_ref[...], r_ref[...], ig_ref[...] = xc, r, ig
        h, carry = _scan_forward(a, xc * ig * mult, carry_ref[...])
        carry_ref[...] = carry
        h_ref[...] = h
        ya_ref[...] = (h * _gelu(ga_ref[...])).astype(BF16)

    tile = lambda j: pl.BlockSpec((SEQ_TILE, D_MODEL), lambda i: (i, j))
    return _fused_call(
        body, jobs, name="fwd_lru", grid=(t // SEQ_TILE,),
        in_specs=[tile(0), tile(1), _const((CONV_WIDTH, D_MODEL)), _const((1, D_MODEL)),
                  _resident((HEADS, HEAD_DIM, HEAD_DIM)), _const((1, D_MODEL)),
                  _resident((HEADS, HEAD_DIM, HEAD_DIM)), _const((1, D_MODEL)), _const((1, D_MODEL))],
        out_specs=[tile(0)] * 5,
        out_shape=[jax.ShapeDtypeStruct((t, D_MODEL), BF16)] + [jax.ShapeDtypeStruct((t, D_MODEL), F32)] * 4,
        scratch_shapes=[pltpu.VMEM((SUBLANES, D_MODEL), F32), pltpu.VMEM((1, D_MODEL), F32)],
        compiler_params=_params(),
    )(z, z, conv_w, conv_b, wr, br, wi, bi, lam)


def _sgu_forward_parts(ub, vb, lg_ref, lb_ref):
    u, du = _gelu_and_grad(ub)
    vg, dvg = _gelu_and_grad(vb)
    mu = jnp.mean(vg, axis=-1, keepdims=True)
    d = vg - mu
    rstd = lax.rsqrt(jnp.mean(d * d, axis=-1, keepdims=True) + LN_EPS)
    vhat = d * rstd
    vn = (vhat * lg_ref[...] + lb_ref[...]).astype(BF16)
    return u, du, dvg, rstd, vhat, vn


def _causal_mask():
    rows = lax.broadcasted_iota(jnp.int32, (CHUNK, CHUNK), 0)
    cols = lax.broadcasted_iota(jnp.int32, (CHUNK, CHUNK), 1)
    return rows >= cols


def _fwd_sgu_merge(ya, z, x, ln_g, ln_b, w_s, bias_full, w_oa, w_ob, w_out, g2, jobs=()):
    t = x.shape[0]

    def body(ya_ref, ub_ref, vb_ref, m_ref, x_ref, lg_ref, lb_ref, ws_ref, bias_ref, woa_ref, wob_ref, wout_ref, g_ref,
             yb_ref, pa_ref, pb_ref, h1_ref, n2_ref):
        u, _, _, _, _, vn = _sgu_forward_parts(ub_ref[...], vb_ref[...], lg_ref, lb_ref)
        mask = _causal_mask()
        wm = [jnp.where(mask, ws_ref[g], 0.0).astype(BF16) for g in range(GROUPS)]
        for c in range(SEQ_TILE // CHUNK):
            rows = slice(c * CHUNK, (c + 1) * CHUNK)
            for g in range(GROUPS):
                cols = slice(g * GROUP_DIM, (g + 1) * GROUP_DIM)
                sp = _dot(wm[g], vn[rows, cols]) + bias_ref[:, cols]
                yb_ref[rows, cols] = (u[rows, cols] * sp).astype(BF16)
        pa = _dot(ya_ref[...], woa_ref[...])
        pb = _dot(yb_ref[...], wob_ref[...])
        pa_ref[...] = pa
        pb_ref[...] = pb
        merged = jax.nn.sigmoid(m_ref[:, :D_MODEL]) * pa + jax.nn.sigmoid(m_ref[:, D_MODEL:]) * pb
        h1 = x_ref[...] + _dot(merged.astype(BF16), wout_ref[...])
        h1_ref[...] = h1
        xhat, _ = _rms(h1)
        n2_ref[...] = (xhat * g_ref[...]).astype(BF16)

    tile = lambda j: pl.BlockSpec((SEQ_TILE, D_MODEL), lambda i: (i, j))
    sq = _resident((D_MODEL, D_MODEL))
    vec = _const((1, D_MODEL))
    bf, f32 = jax.ShapeDtypeStruct((t, D_MODEL), BF16), jax.ShapeDtypeStruct((t, D_MODEL), F32)
    return _fused_call(
        body, jobs, name="fwd_sgu_merge", grid=(t // SEQ_TILE,),
        in_specs=[tile(0), tile(2), tile(3), pl.BlockSpec((SEQ_TILE, 2 * D_MODEL), lambda i: (i, 2)), tile(0), vec, vec,
                  _const((GROUPS, CHUNK, CHUNK)), _const((CHUNK, D_MODEL)), sq, sq, sq, vec],
        out_specs=[tile(0)] * 5,
        out_shape=[bf, f32, f32, f32, bf],
        compiler_params=_params(),
    )(ya, z, z, z, x, ln_g, ln_b, w_s, bias_full, w_oa, w_ob, w_out, g2)


def _mlp(n2, h1, target, w_up_st, w_down, g2, g3, jobs=()):
    t = n2.shape[0]

    def body(n2_ref, h1_ref, tgt_ref, wup_ref, wdown_ref, g2_ref, g3_ref, act_ref, dup_ref, dh2b_ref, dh1_ref,
             loss_ref, dg3_ref, dg2_ref, relu_ref):
        @pl.when(pl.program_id(0) == 0)
        def _():
            for ref in (loss_ref, dg3_ref, dg2_ref):
                ref[...] = jnp.zeros_like(ref)

        n2 = n2_ref[...]
        h1 = h1_ref[...]
        h2 = h1
        for k in range(N_CHIPS):
            cols = slice(k * D_MODEL, (k + 1) * D_MODEL)
            r = jnp.maximum(_dot(n2, wup_ref[k]), 0.0)
            relu_ref[:, cols] = r
            act = (r * r).astype(BF16)
            act_ref[:, cols] = act
            h2 = h2 + _dot(act, wdown_ref[cols, :])
        xhat, r3 = _rms(h2)
        diff = xhat * g3_ref[...] - tgt_ref[...]
        sq = jnp.sum(diff * diff, axis=1, keepdims=True)
        loss_ref[...] = loss_ref[...] + (0.5 / D_MODEL) * jnp.sum(sq, axis=0, keepdims=True)
        dy = diff * (1.0 / D_MODEL)
        dg3_ref[...] = dg3_ref[...] + _col_sum(dy * xhat)
        dh2 = _rms_bwd(dy * g3_ref[...], xhat, r3)
        dh2b = dh2.astype(BF16)
        dh2b_ref[...] = dh2b
        dn2 = jnp.zeros((SEQ_TILE, D_MODEL), F32)
        for k in range(N_CHIPS):
            cols = slice(k * D_MODEL, (k + 1) * D_MODEL)
            dup = (_dot_nt(dh2b, wdown_ref[cols, :]) * (2.0 * relu_ref[:, cols])).astype(BF16)
            dup_ref[:, cols] = dup
            dn2 = dn2 + _dot_nt(dup, wup_ref[k])
        xhat, r2 = _rms(h1)
        dg2_ref[...] = dg2_ref[...] + _col_sum(dn2 * xhat)
        dh1_ref[...] = dh2 + _rms_bwd(dn2 * g2_ref[...], xhat, r2)

    tile = pl.BlockSpec((SEQ_TILE, D_MODEL), lambda i: (i, 0))
    wide = pl.BlockSpec((SEQ_TILE, D_FF), lambda i: (i, 0))
    vec = _const((1, D_MODEL))
    vec_shape = jax.ShapeDtypeStruct((1, D_MODEL), F32)
    return _fused_call(
        body, jobs, name="mlp", grid=(t // SEQ_TILE,),
        in_specs=[tile, tile, tile, _resident((N_CHIPS, D_MODEL, D_MODEL)), _resident((D_FF, D_MODEL)), vec, vec],
        out_specs=[wide, wide, tile, tile, _const((SUBLANES, 128)), vec, vec],
        out_shape=[jax.ShapeDtypeStruct((t, D_FF), BF16), jax.ShapeDtypeStruct((t, D_FF), BF16),
                   jax.ShapeDtypeStruct((t, D_MODEL), BF16), jax.ShapeDtypeStruct((t, D_MODEL), F32),
                   jax.ShapeDtypeStruct((SUBLANES, 128), F32), vec_shape, vec_shape],
        scratch_shapes=[pltpu.VMEM((SEQ_TILE, D_FF), F32)],
        compiler_params=_params(),
    )(n2, h1, target, w_up_st, w_down, g2, g3)


def _bwd_mix(dh1, pa, pb, z, h, xc, r, ig, w_oa, w_ob, w_out, ln_g, ln_b, w_s, bias_full, conv_w, wr, wi, lam, jobs=()):
    t = dh1.shape[0]
    n_tiles = t // SEQ_TILE
    per_tile = SEQ_TILE // SUBLANES

    def merge_part(dh1_ref, pa_ref, pb_ref, m_ref, woa_ref, wob_ref, wout_ref, dz_ref, dya_ref, dyb_ref, mg_ref,
                   dpa_ref, dpb_ref, dh1b_ref):
        dh1b = dh1_ref[...].astype(BF16)
        dh1b_ref[...] = dh1b
        dm = _dot_nt(dh1b, wout_ref[...])
        pa = pa_ref[...]
        pb = pb_ref[...]
        sa = jax.nn.sigmoid(m_ref[:, :D_MODEL])
        sb = jax.nn.sigmoid(m_ref[:, D_MODEL:])
        mg_ref[...] = (sa * pa + sb * pb).astype(BF16)
        dz_ref[:, :D_MODEL] = (dm * pa * sa * (1.0 - sa)).astype(BF16)
        dz_ref[:, D_MODEL:] = (dm * pb * sb * (1.0 - sb)).astype(BF16)
        dpa = (dm * sa).astype(BF16)
        dpb = (dm * sb).astype(BF16)
        dpa_ref[...] = dpa
        dpb_ref[...] = dpb
        dya_ref[...] = _dot_nt(dpa, woa_ref[...])
        dyb_ref[...] = _dot_nt(dpb, wob_ref[...])

    def sgu_part(dyb_ref, ub_ref, vb_ref, lg_ref, lb_ref, ws_ref, bias_ref, dz_ref, dlg_ref, dlb_ref, dws_ref, dbs_ref,
                 dvn_ref, dsp_acc):
        i = pl.program_id(0)

        @pl.when(i == 0)
        def _():
            dlg_ref[...] = jnp.zeros_like(dlg_ref)
            dlb_ref[...] = jnp.zeros_like(dlb_ref)
            dws_ref[...] = jnp.zeros_like(dws_ref)
            dsp_acc[...] = jnp.zeros_like(dsp_acc)

        u, du, dvg, rstd, vhat, vn = _sgu_forward_parts(ub_ref[...], vb_ref[...], lg_ref, lb_ref)
        dyb = dyb_ref[...]
        mask = _causal_mask()
        wm = [jnp.where(mask, ws_ref[g], 0.0).astype(BF16) for g in range(GROUPS)]
        for c in range(SEQ_TILE // CHUNK):
            rows = slice(c * CHUNK, (c + 1) * CHUNK)
            for g in range(GROUPS):
                cols = slice(g * GROUP_DIM, (g + 1) * GROUP_DIM)
                vn_blk = vn[rows, cols]
                sp = _dot(wm[g], vn_blk) + bias_ref[:, cols]
                dyb_blk = dyb[rows, cols]
                dz_ref[rows, cols] = (dyb_blk * sp * du[rows, cols]).astype(BF16)
                dsp = dyb_blk * u[rows, cols]
                dsp_acc[:, cols] = dsp_acc[:, cols] + dsp
                dspb = dsp.astype(BF16)
                dvn_ref[rows, cols] = _dot_tn(wm[g], dspb)
                wcols = slice(g * CHUNK, (g + 1) * CHUNK)
                dws_ref[:, wcols] = dws_ref[:, wcols] + jnp.where(mask, _dot_nt(dspb, vn_blk), 0.0)
        dvn = dvn_ref[...]
        dlg_ref[...] = dlg_ref[...] + _col_sum(dvn * vhat)
        dlb_ref[...] = dlb_ref[...] + _col_sum(dvn)
        dvhat = dvn * lg_ref[...]
        dvgel = rstd * (dvhat - jnp.mean(dvhat, axis=-1, keepdims=True)
                        - vhat * jnp.mean(dvhat * vhat, axis=-1, keepdims=True))
        dz_ref[:, D_MODEL:] = (dvgel * dvg).astype(BF16)

        @pl.when(i == n_tiles - 1)
        def _():
            lane = lax.broadcasted_iota(jnp.int32, (CHUNK, 128), 1)
            out = jnp.zeros((CHUNK, 128), F32)
            for g in range(GROUPS):
                s = jnp.sum(dsp_acc[:, g * GROUP_DIM:(g + 1) * GROUP_DIM], axis=1, keepdims=True)
                out = out + jnp.where(lane == g, s, 0.0)
            dbs_ref[...] = out

    def lru_part(dya_ref, xa_ref, ga_ref, h_ref, h_prev_ref, xc_ref, r_ref, ig_ref, cw_ref, wr_ref, wi_ref, lam_ref,
                 dz_ref, dcw_ref, dcb_ref, dwr_ref, dbr_ref, dwi_ref, dbi_ref, dlam_ref, lam_carry, dxc_head):
        i = pl.program_id(0)

        @pl.when(i == 0)
        def _():
            for ref in (dcw_ref, dcb_ref, dwr_ref, dbr_ref, dwi_ref, dbi_ref, dlam_ref, lam_carry, dxc_head):
                ref[...] = jnp.zeros_like(ref)

        first_tile = i == n_tiles - 1
        h_tail = jnp.where(first_tile, 0.0, h_prev_ref[...])
        xc, r, ig = xc_ref[...], r_ref[...], ig_ref[...]
        xcb = xc.astype(BF16)
        sp, a, mult = _decay(r, lam_ref)
        h = h_ref[...]
        h_prev = _shift_down(h, h_tail, 1)
        dya = dya_ref[...]
        gg, dgg = _gelu_and_grad(ga_ref[...])
        dz_ref[:, D_MODEL:] = (dya * h * dgg).astype(BF16)
        ones = jnp.ones((SUBLANES, D_MODEL), F32)
        lam_t, lam_first = _scan_backward(_shift_up(a, ones, 1), dya * gg, lam_carry[...])
        lam_carry[...] = a[0:1] * lam_first
        dmult = lam_t * xc * ig
        dla = lam_t * h_prev * a - dmult * (a * a) / mult
        dr = dla * ((-LRU_C) * sp)
        dlam_ref[...] = dlam_ref[...] + _col_sum(dla * r) * (LRU_C * jax.nn.sigmoid(-lam_ref[...]))
        dpr = dr * r * (1.0 - r)
        dpi = lam_t * xc * mult * ig * (1.0 - ig)
        dbr_ref[...] = dbr_ref[...] + _col_sum(dpr)
        dbi_ref[...] = dbi_ref[...] + _col_sum(dpi)
        dprb = dpr.astype(BF16)
        dpib = dpi.astype(BF16)
        dxc_gate = []
        for hd in range(HEADS):
            cols = slice(hd * HEAD_DIM, (hd + 1) * HEAD_DIM)
            dxc_gate.append(_dot_nt(dprb[:, cols], wr_ref[hd]) + _dot_nt(dpib[:, cols], wi_ref[hd]))
            dwr_ref[hd] = dwr_ref[hd] + _dot_tn(xcb[:, cols], dprb[:, cols])
            dwi_ref[hd] = dwi_ref[hd] + _dot_tn(xcb[:, cols], dpib[:, cols])
        dxc = lam_t * ig * mult + jnp.concatenate(dxc_gate, axis=1)
        dcb_ref[...] = dcb_ref[...] + _col_sum(dxc)
        cw = cw_ref[...]
        head = dxc_head[...]
        xa = xa_ref[...]
        dxa = cw[0:1] * dxc
        dcw_ref[0:1, :] = dcw_ref[0:1, :] + _col_sum(dxc * xa)
        for k in range(1, CONV_WIDTH):
            dxc_k = _shift_up(dxc, head, k)
            dxa = dxa + cw[k:k + 1] * dxc_k
            dcw_ref[k:k + 1, :] = dcw_ref[k:k + 1, :] + _col_sum(dxc_k * xa)
        dxc_head[...] = dxc[0:SUBLANES]
        dz_ref[:, :D_MODEL] = dxa.astype(BF16)

    def body(dh1_ref, pa_ref, pb_ref, z_ref, h_ref, h_prev_ref, xc_ref, r_ref, ig_ref, woa_ref, wob_ref, wout_ref,
             lg_ref, lb_ref, ws_ref, bias_ref, cw_ref, wr_ref, wi_ref, lam_ref, dz_ref, mg_ref, dpa_ref, dpb_ref,
             dh1b_ref, dlg_ref, dlb_ref, dws_ref, dbs_ref, dcw_ref, dcb_ref, dwr_ref, dbr_ref, dwi_ref, dbi_ref,
             dlam_ref, dya_ref, dyb_ref, dvn_ref, dsp_acc, lam_carry, dxc_head):
        def cols(ref, first, count):
            return ref.at[:, pl.ds(first * D_MODEL, count * D_MODEL)]

        merge_part(dh1_ref, pa_ref, pb_ref, cols(z_ref, 4, 2), woa_ref, wob_ref, wout_ref, cols(dz_ref, 4, 2), dya_ref,
                   dyb_ref, mg_ref, dpa_ref, dpb_ref, dh1b_ref)
        sgu_part(dyb_ref, cols(z_ref, 2, 1), cols(z_ref, 3, 1), lg_ref, lb_ref, ws_ref, bias_ref, cols(dz_ref, 2, 2),
                 dlg_ref, dlb_ref, dws_ref, dbs_ref, dvn_ref, dsp_acc)
        lru_part(dya_ref, cols(z_ref, 0, 1), cols(z_ref, 1, 1), h_ref, h_prev_ref, xc_ref, r_ref, ig_ref, cw_ref, wr_ref,
                 wi_ref, lam_ref, cols(dz_ref, 0, 2), dcw_ref, dcb_ref, dwr_ref, dbr_ref, dwi_ref, dbi_ref, dlam_ref,
                 lam_carry, dxc_head)

    rev = lambda i: n_tiles - 1 - i
    tile = pl.BlockSpec((SEQ_TILE, D_MODEL), lambda i: (rev(i), 0))
    row = pl.BlockSpec((SEQ_TILE, D_IN), lambda i: (rev(i), 0))
    prev8 = pl.BlockSpec((SUBLANES, D_MODEL), lambda i: (jnp.maximum(rev(i) * per_tile - 1, 0), 0))
    vec = _const((1, D_MODEL))
    sq = _resident((D_MODEL, D_MODEL))
    gate_w = _resident((HEADS, HEAD_DIM, HEAD_DIM))
    gate_acc = _const((HEADS, HEAD_DIM, HEAD_DIM))
    vec_shape = jax.ShapeDtypeStruct((1, D_MODEL), F32)
    gate_shape = jax.ShapeDtypeStruct((HEADS, HEAD_DIM, HEAD_DIM), F32)
    act_bf = jax.ShapeDtypeStruct((t, D_MODEL), BF16)
    return _fused_call(
        body, jobs, name="bwd_mix", grid=(n_tiles,),
        in_specs=[tile, tile, tile, row, tile, prev8, tile, tile, tile, sq, sq, sq, vec, vec,
                  _const((GROUPS, CHUNK, CHUNK)), _const((CHUNK, D_MODEL)), _const((CONV_WIDTH, D_MODEL)), gate_w, gate_w,
                  vec],
        out_specs=[row, tile, tile, tile, tile, vec, vec, _const((CHUNK, GROUPS * CHUNK)), _const((CHUNK, 128)),
                   _const((SUBLANES, D_MODEL)), vec, gate_acc, vec, gate_acc, vec, vec],
        out_shape=[jax.ShapeDtypeStruct((t, D_IN), BF16), act_bf, act_bf, act_bf, act_bf, vec_shape, vec_shape,
                   jax.ShapeDtypeStruct((CHUNK, GROUPS * CHUNK), F32), jax.ShapeDtypeStruct((CHUNK, 128), F32),
                   jax.ShapeDtypeStruct((SUBLANES, D_MODEL), F32), vec_shape, gate_shape, vec_shape, gate_shape,
                   vec_shape, vec_shape],
        scratch_shapes=[pltpu.VMEM((SEQ_TILE, D_MODEL), F32), pltpu.VMEM((SEQ_TILE, D_MODEL), F32),
                        pltpu.VMEM((SEQ_TILE, D_MODEL), F32), pltpu.VMEM((CHUNK, D_MODEL), F32),
                        pltpu.VMEM((1, D_MODEL), F32), pltpu.VMEM((SUBLANES, D_MODEL), F32)],
        compiler_params=_params(),
    )(dh1, pa, pb, z, h, h, xc, r, ig, w_oa, w_ob, w_out, ln_g, ln_b, w_s, bias_full, conv_w, wr, wi, lam)


def _bwd_in(dz, x, dh1, w_in_st, g1, jobs=()):
    t = x.shape[0]

    def body(dz_ref, x_ref, dh1_ref, w_ref, g_ref, dx_ref, dg1_ref):
        @pl.when(pl.program_id(0) == 0)
        def _():
            dg1_ref[...] = jnp.zeros_like(dg1_ref)

        dn1 = jnp.zeros((MM_TILE, D_MODEL), F32)
        for k in range(N_CHIPS):
            dn1 = dn1 + _dot_nt(dz_ref[:, k * IN_SHARD:(k + 1) * IN_SHARD], w_ref[k])
        xhat, r1 = _rms(x_ref[...])
        dg1_ref[...] = dg1_ref[...] + _col_sum(dn1 * xhat)
        dx_ref[...] = dh1_ref[...] + _rms_bwd(dn1 * g_ref[...], xhat, r1)

    tile = pl.BlockSpec((MM_TILE, D_MODEL), lambda i: (i, 0))
    return _fused_call(
        body, jobs, name="bwd_in", grid=(t // MM_TILE,),
        in_specs=[pl.BlockSpec((MM_TILE, D_IN), lambda i: (i, 0)), tile, tile,
                  _resident((N_CHIPS, D_MODEL, IN_SHARD)), _const((1, D_MODEL))],
        out_specs=[tile, _const((1, D_MODEL))],
        out_shape=[jax.ShapeDtypeStruct((t, D_MODEL), F32), jax.ShapeDtypeStruct((1, D_MODEL), F32)],
        compiler_params=_params(),
    )(dz, x, dh1, w_in_st, g1)


def _weight_grad(name, a, b, n_blocks, a_varies, b_varies, width, jobs=()):
    t = a.shape[0]
    rows = min(DW_TILE, t)
    n_t = t // rows

    def body(a_ref, b_ref, o_ref, acc_ref):
        s = pl.program_id(1)
        part = _dot_tn(a_ref[...], b_ref[...])

        @pl.when(s == 0)
        def _():
            acc_ref[...] = part

        @pl.when(s > 0)
        def _():
            acc_ref[...] = acc_ref[...] + part

        @pl.when(s == n_t - 1)
        def _():
            o_ref[...] = acc_ref[...].astype(BF16)

    return _fused_call(
        body, jobs, name=name, grid=(n_blocks, n_t),
        in_specs=[pl.BlockSpec((rows, D_MODEL), (lambda j, s: (s, j)) if a_varies else (lambda j, s: (s, 0))),
                  pl.BlockSpec((rows, width), (lambda j, s: (s, j)) if b_varies else (lambda j, s: (s, 0)))],
        out_specs=pl.BlockSpec((None, D_MODEL, width), lambda j, s: (j, 0, 0)),
        out_shape=jax.ShapeDtypeStruct((n_blocks, D_MODEL, width), BF16),
        scratch_shapes=[pltpu.VMEM((D_MODEL, width), F32)],
        compiler_params=_params(2),
    )(a, b)


def _weight_grads_square(name, pairs, jobs=()):
    n = len(pairs)
    t = pairs[0][0].shape[0]
    rows = min(2 * MM_TILE, t)
    n_t = t // rows

    def body(*refs):
        ins, outs, accs = refs[:2 * n], refs[2 * n:3 * n], refs[3 * n:]
        s = pl.program_id(0)
        for k in range(n):
            part = _dot_tn(ins[2 * k][...], ins[2 * k + 1][...])

            @pl.when(s == 0)
            def _(k=k, part=part):
                accs[k][...] = part

            @pl.when(s > 0)
            def _(k=k, part=part):
                accs[k][...] = accs[k][...] + part

            @pl.when(s == n_t - 1)
            def _(k=k):
                outs[k][...] = accs[k][...].astype(BF16)

    tile = pl.BlockSpec((rows, D_MODEL), lambda s: (s, 0))
    return _fused_call(
        body, jobs, name=name, grid=(n_t,), in_specs=[tile] * (2 * n), out_specs=[_const((D_MODEL, D_MODEL))] * n,
        out_shape=[jax.ShapeDtypeStruct((D_MODEL, D_MODEL), BF16)] * n,
        scratch_shapes=[pltpu.VMEM((D_MODEL, D_MODEL), F32)] * n,
        compiler_params=_params(),
    )(*[x for pair in pairs for x in pair])


def _place():
    x, y, c = lax.axis_index("x"), lax.axis_index("y"), lax.axis_index("c")
    other_chips = [(1 - x, y), (x, 1 - y), (1 - x, 1 - y)]
    return x, y, c, other_chips


def _chip_index(px, py):
    return 2 * px + py


ANY = pl.BlockSpec(memory_space=pl.ANY)
SIBLING = ((0, 0, 1),)
NEIGHBOURS = ((1, 0, 0), (0, 1, 0))
OTHER_CHIPS = NEIGHBOURS + ((1, 1, 0),)


def _near_far(x, y, c):
    return (x ^ (1 - c), y ^ c), (x ^ c, y ^ (1 - c))


def _gather_near_job(shards):
    n = len(shards)
    halves = [s.shape[0] // 2 for s in shards]

    def copies(ins, outs, send, recv, local):
        x, y, c, _ = _place()
        near, _ = _near_far(x, y, c)

        def block(w, chip, pc):
            return outs[w].at[_chip_index(*chip), pl.ds(pc * halves[w], halves[w]), :]

        def copy(w, k, chip, pc, to, src=None):
            return pltpu.make_async_remote_copy(
                src_ref=block(w, chip, pc) if src is None else src, dst_ref=block(w, chip, pc),
                send_sem=send.at[2 * w + k], recv_sem=recv.at[2 * w + k], device_id=to, device_id_type=MESH)

        sends, arrivals, own = [], [], []
        for w in range(n):
            src = ins[w].at[pl.ds(c * halves[w], halves[w]), :]
            own.append(pltpu.make_async_copy(src, block(w, (x, y), c), local.at[w]))
            sends += [copy(w, 0, (x, y), c, (*near, c), src), copy(w, 1, (x, y), c, (x, y, 1 - c), src)]
            arrivals += [copy(w, 0, near, c, (x, y, c)), copy(w, 1, (x, y), 1 - c, (x, y, c))]
        return sends, arrivals, own

    return _Job(shards, [jax.ShapeDtypeStruct((N_CHIPS,) + s.shape, s.dtype) for s in shards], 2 * n, copies,
                NEIGHBOURS + SIBLING, n_local=n)


def _gather_far_job(stacked):
    n = len(stacked)
    halves = [s.shape[1] // 2 for s in stacked]

    def copies(ins, outs, send, recv, local):
        del ins, local
        x, y, c, _ = _place()
        near, far = _near_far(x, y, c)

        def copy(w, k, chip):
            blk = outs[w].at[_chip_index(*chip), pl.ds(c * halves[w], halves[w]), :]
            return pltpu.make_async_remote_copy(
                src_ref=blk, dst_ref=blk, send_sem=send.at[2 * w + k], recv_sem=recv.at[2 * w + k],
                device_id=(*far, c), device_id_type=MESH)

        sends = [copy(w, k, chip) for w in range(n) for k, chip in enumerate(((x, y), near))]
        arrivals = [copy(w, k, chip) for w in range(n) for k, chip in enumerate((far, (1 - x, 1 - y)))]
        return sends, arrivals, []

    return _Job(stacked, [jax.ShapeDtypeStruct(s.shape, s.dtype) for s in stacked], 2 * n, copies, NEIGHBOURS,
                aliases={w: w for w in range(n)})


def _gather_pass_job(stacked):
    n = len(stacked)
    halves = [s.shape[1] // 2 for s in stacked]

    def copies(ins, outs, send, recv, local):
        del ins, local
        x, y, c, chips = _place()

        def copy(w, j, chip, pc, to):
            blk = outs[w].at[_chip_index(*chip), pl.ds(pc * halves[w], halves[w]), :]
            return pltpu.make_async_remote_copy(
                src_ref=blk, dst_ref=blk, send_sem=send.at[3 * w + j], recv_sem=recv.at[3 * w + j], device_id=to,
                device_id_type=MESH)

        sends = [copy(w, j, chip, c, (x, y, 1 - c)) for w in range(n) for j, chip in enumerate(chips)]
        arrivals = [copy(w, j, chip, 1 - c, (x, y, c)) for w in range(n) for j, chip in enumerate(chips)]
        return sends, arrivals, []

    return _Job(stacked, [jax.ShapeDtypeStruct(s.shape, s.dtype) for s in stacked], 3 * n, copies, SIBLING,
                aliases={w: w for w in range(n)})


def _gather_small_job(block):
    def copies(ins, outs, send, recv, local):
        x, y, c, chips = _place()

        def copy(j, chip_from, to):
            return pltpu.make_async_remote_copy(
                src_ref=ins[0], dst_ref=outs[0].at[_chip_index(*chip_from)], send_sem=send.at[j],
                recv_sem=recv.at[j], device_id=to, device_id_type=MESH)

        own = [pltpu.make_async_copy(ins[0], outs[0].at[_chip_index(x, y)], local.at[0])]
        sends = [copy(j, (x, y), (*chip, c)) for j, chip in enumerate(chips)]
        arrivals = [copy(j, chip, (x, y, c)) for j, chip in enumerate(chips)]
        return sends, arrivals, own

    return _Job([block], [jax.ShapeDtypeStruct((N_CHIPS,) + block.shape, block.dtype)], 3, copies, OTHER_CHIPS,
                n_local=1)


def _pair_send_job(grads):
    n = len(grads)
    halves = [g.shape[1] // 2 for g in grads]

    def copies(ins, outs, send, recv, local):
        del local
        x, y, c, _ = _place()
        sends = [pltpu.make_async_remote_copy(
            src_ref=ins[w].at[:, pl.ds((1 - c) * halves[w], halves[w]), :], dst_ref=outs[w], send_sem=send.at[w],
            recv_sem=recv.at[w], device_id=(x, y, 1 - c), device_id_type=MESH) for w in range(n)]
        return sends, sends, []

    return _Job(grads, [jax.ShapeDtypeStruct((N_CHIPS, h, g.shape[2]), g.dtype) for g, h in zip(grads, halves)], n,
                copies, SIBLING)


def _row_block(rows, limit=256):
    return min(rows, limit)


def _pair_add(name, core, mine, theirs):
    _, _, h, cols = mine.shape
    rb = _row_block(h, 512)

    def body(core_ref, a_ref, b_ref, o_ref):
        del core_ref
        o_ref[...] = (a_ref[...].astype(F32) + b_ref[...].astype(F32)).astype(BF16)

    return pl.pallas_call(
        body, name=name,
        grid_spec=pltpu.PrefetchScalarGridSpec(
            num_scalar_prefetch=1, grid=(N_CHIPS, h // rb),
            in_specs=[pl.BlockSpec((None, None, rb, cols), lambda k, r, core_ref: (k, core_ref[0], r, 0)),
                      pl.BlockSpec((None, rb, cols), lambda k, r, core_ref: (k, r, 0))],
            out_specs=pl.BlockSpec((None, rb, cols), lambda k, r, core_ref: (k, r, 0))),
        out_shape=jax.ShapeDtypeStruct(theirs.shape, BF16),
        compiler_params=_params(2),
    )(core, mine, theirs)


def _sequencer_call(name, collective_id, job):
    steps, peers = job.phases, job.peers
    ins = [jax.new_ref(a, memory_space=pltpu.MemorySpace.HBM) for a in job.inputs]
    outs = [ins[{o: i for i, o in job.aliases.items()}[k]] if k in job.aliases.values()
            else jax.empty_ref(shape, memory_space=pltpu.MemorySpace.HBM) for k, shape in enumerate(job.out_shape)]
    sems = [pltpu.SemaphoreType.DMA((n,)) for step in steps for n in (step.n_sem, step.n_sem, max(step.n_local, 1))]

    @pl.kernel(mesh=plsc.ScalarSubcoreMesh(axis_name="sequencer", num_cores=1), name=name, scratch_types=tuple(sems),
               compiler_params=pltpu.CompilerParams(collective_id=collective_id))
    def launch(*sem_refs):
        x, y, c, _ = _place()
        barrier = pltpu.get_barrier_semaphore()
        for dx, dy, dc in peers:
            pl.semaphore_signal(barrier, inc=1, device_id=(x ^ dx, y ^ dy, c ^ dc), device_id_type=MESH)
        pl.semaphore_wait(barrier, len(peers))
        for k, step in enumerate(steps):
            sends, arrivals, own = step.copies(ins if k == 0 else outs, outs, *sem_refs[3 * k:3 * k + 3])
            for cp in own + sends:
                cp.start()
            for cp in arrivals:
                cp.wait_recv()
            for cp in sends:
                cp.wait_send()
            for cp in own:
                cp.wait()

    launch()
    return [ref[...] for ref in outs]


def _chip_exchange_job(sums):
    n = len(sums)

    def copies(ins, outs, send, recv, local):
        del local
        _, _, c, chips = _place()
        sends = [pltpu.make_async_remote_copy(
            src_ref=ins[w].at[_chip_index(*chip)], dst_ref=outs[w].at[j], send_sem=send.at[3 * w + j],
            recv_sem=recv.at[3 * w + j], device_id=(*chip, c), device_id_type=MESH)
            for w in range(n) for j, chip in enumerate(chips)]
        return sends, sends, []

    return _Job(sums, [jax.ShapeDtypeStruct((N_CHIPS - 1,) + s.shape[1:], s.dtype) for s in sums], 3 * n, copies,
                OTHER_CHIPS)


def _chip_sum(name, place, mine, theirs):
    _, h, cols = mine.shape
    rb = _row_block(h, 512)

    def body(place_ref, p_ref, q_ref, o_ref):
        del place_ref
        acc = p_ref[...].astype(F32)
        for j in range(N_CHIPS - 1):
            acc = acc + q_ref[j].astype(F32)
        o_ref[...] = acc

    return pl.pallas_call(
        body, name=name,
        grid_spec=pltpu.PrefetchScalarGridSpec(
            num_scalar_prefetch=1, grid=(h // rb,),
            in_specs=[pl.BlockSpec((None, rb, cols), lambda r, place_ref: (place_ref[0], r, 0)),
                      pl.BlockSpec((N_CHIPS - 1, rb, cols), lambda r, place_ref: (0, r, 0))],
            out_specs=pl.BlockSpec((None, rb, cols), lambda r, place_ref: (place_ref[1], r, 0))),
        out_shape=jax.ShapeDtypeStruct((2, h, cols), F32),
        compiler_params=_params(),
    )(place, mine, theirs)


def _share_job(bufs):
    n = len(bufs)

    def copies(ins, outs, send, recv, local):
        del ins, local
        x, y, c, _ = _place()

        def copy(w, half):
            return pltpu.make_async_remote_copy(
                src_ref=outs[w].at[half], dst_ref=outs[w].at[half], send_sem=send.at[w], recv_sem=recv.at[w],
                device_id=(x, y, 1 - c), device_id_type=MESH)

        return [copy(w, c) for w in range(n)], [copy(w, 1 - c) for w in range(n)], []

    return _Job(bufs, [jax.ShapeDtypeStruct(b.shape, b.dtype) for b in bufs], n, copies, SIBLING,
                aliases={w: w for w in range(n)})


SMALL_ROWS = 24
ROW_G1, ROW_CW, ROW_CB, ROW_BR, ROW_BI, ROW_LAM, ROW_LG, ROW_LB, ROW_G2, ROW_G3, ROW_LOSS, ROW_BS = (
    0, 1, 5, 6, 7, 8, 9, 10, 11, 12, 13, 16)
N_DEV = 8


def _pack_small(dcw, dcb, dbr, dbi, dlam, dlg, dlb, dg2, dg3, loss, dbs):
    def body(dcw_ref, dcb_ref, dbr_ref, dbi_ref, dlam_ref, dlg_ref, dlb_ref, dg2_ref, dg3_ref, loss_ref, dbs_ref, out):
        out[...] = jnp.zeros((SMALL_ROWS, D_MODEL), F32)
        for row, ref in ((ROW_CB, dcb_ref), (ROW_BR, dbr_ref), (ROW_BI, dbi_ref), (ROW_LAM, dlam_ref),
                         (ROW_LG, dlg_ref), (ROW_LB, dlb_ref), (ROW_G2, dg2_ref), (ROW_G3, dg3_ref)):
            out[row:row + 1, :] = ref[...]
        out[ROW_CW:ROW_CW + CONV_WIDTH, :] = dcw_ref[0:CONV_WIDTH, :]
        out[ROW_LOSS:ROW_LOSS + 1, 0:128] = loss_ref[0:1, :]
        out[ROW_BS:ROW_BS + GROUPS, 0:128] = jnp.transpose(dbs_ref[...])[0:GROUPS, :]

    vm = pl.BlockSpec(memory_space=pltpu.VMEM)
    return pl.pallas_call(
        body, name="pack_small", in_specs=[vm] * 11, out_specs=vm,
        out_shape=jax.ShapeDtypeStruct((SMALL_ROWS, D_MODEL), F32),
    )(dcw, dcb, dbr, dbi, dlam, dlg, dlb, dg2, dg3, loss, dbs)


def _gather_all_job(blocks):
    n = len(blocks)
    flips = [(dx, dy, dc) for dx in (0, 1) for dy in (0, 1) for dc in (0, 1)][1:]

    def copies(ins, outs, send, recv, local):
        x, y, c, _ = _place()
        me = 4 * x + 2 * y + c
        sends, arrivals, own = [], [], []
        for w in range(n):
            own.append(pltpu.make_async_copy(ins[w], outs[w].at[me], local.at[w]))
            for k, (dx, dy, dc) in enumerate(flips):
                peer = (x ^ dx, y ^ dy, c ^ dc)
                sem = dict(send_sem=send.at[7 * w + k], recv_sem=recv.at[7 * w + k])
                sends.append(pltpu.make_async_remote_copy(
                    src_ref=ins[w], dst_ref=outs[w].at[me], device_id=peer, device_id_type=MESH, **sem))
                arrivals.append(pltpu.make_async_remote_copy(
                    src_ref=ins[w], dst_ref=outs[w].at[4 * peer[0] + 2 * peer[1] + peer[2]], device_id=peer,
                    device_id_type=MESH, **sem))
        return sends, arrivals, own

    return _Job(blocks, [jax.ShapeDtypeStruct((N_DEV,) + b.shape, b.dtype) for b in blocks], 7 * n, copies,
                OTHER_CHIPS + SIBLING + tuple((dx, dy, 1) for dx, dy, _ in OTHER_CHIPS), n_local=n)


def _sum_small(vec_all, ws_all, dg1_all):
    def body(vec_ref, ws_ref, dg1_ref, vec_out, ws_out):
        vec, ws, dg1 = vec_ref[0], ws_ref[0], dg1_ref[0]
        for d in range(1, N_DEV):
            vec, ws, dg1 = vec + vec_ref[d], ws + ws_ref[d], dg1 + dg1_ref[d]
        vec_out[...] = vec
        vec_out[ROW_G1:ROW_G1 + 1, :] = dg1
        ws_out[...] = ws

    vm = pl.BlockSpec(memory_space=pltpu.VMEM)
    return pl.pallas_call(
        body, name="sum_small", in_specs=[vm] * 3, out_specs=[vm, vm],
        out_shape=[jax.ShapeDtypeStruct(vec_all.shape[1:], F32), jax.ShapeDtypeStruct(ws_all.shape[1:], F32)],
    )(vec_all, ws_all, dg1_all)


def _adamw_math(w, g, m, v):
    m = ADAM_B1 * m + (1.0 - ADAM_B1) * g
    v = ADAM_B2 * v + (1.0 - ADAM_B2) * (g * g)
    m_hat = m / (1.0 - ADAM_B1 ** ADAM_STEP)
    v_hat = v / (1.0 - ADAM_B2 ** ADAM_STEP)
    delta = (-ADAM_LR) * (m_hat / (jnp.sqrt(v_hat) + ADAM_EPS) + ADAM_WD * w)
    return delta, m, v


def _adamw(name, g, w, m, v, jobs=()):
    rows, cols = w.shape
    rb = _row_block(rows)

    def body(g_ref, w_ref, m_ref, v_ref, d_ref, nm_ref, nv_ref):
        d_ref[...], nm_ref[...], nv_ref[...] = _adamw_math(w_ref[...], g_ref[...], m_ref[...], v_ref[...])

    blk = pl.BlockSpec((rb, cols), lambda r: (r, 0))
    return _fused_call(
        body, jobs, name=name, grid=(rows // rb,), in_specs=[blk] * 4, out_specs=[blk] * 3,
        out_shape=[jax.ShapeDtypeStruct(w.shape, F32)] * 3, compiler_params=_params(),
    )(g, w, m, v)


def _adamw_small(grads, ws, ms, vs):
    n = len(grads)

    def body(*refs):
        g_refs, w_refs, m_refs, v_refs = refs[:n], refs[n:2 * n], refs[2 * n:3 * n], refs[3 * n:4 * n]
        outs = refs[4 * n:]
        for p in range(n):
            d, nm, nv = _adamw_math(w_refs[p][...], g_refs[p][...], m_refs[p][...], v_refs[p][...])
            outs[p][...] = d
            outs[n + p][...] = nm
            outs[2 * n + p][...] = nv

    vm = pl.BlockSpec(memory_space=pltpu.VMEM)
    shapes = [jax.ShapeDtypeStruct(w.shape, F32) for w in ws]
    out = pl.pallas_call(
        body, name="adamw_small", in_specs=[vm] * (4 * n), out_specs=[vm] * (3 * n), out_shape=shapes * 3,
    )(*grads, *ws, *ms, *vs)
    return out[:n], out[n:2 * n], out[2 * n:]


def _unstack_heads(w_st):
    per = HEAD_DIM // N_CHIPS
    return w_st.reshape(N_CHIPS, HEADS, per, HEAD_DIM).transpose(1, 0, 2, 3).reshape(HEADS, HEAD_DIM, HEAD_DIM)


def _stack_heads(w):
    per = HEAD_DIM // N_CHIPS
    return w.reshape(HEADS, N_CHIPS, per, HEAD_DIM).transpose(1, 0, 2, 3).reshape(N_CHIPS, HEADS * per, HEAD_DIM)


def kernel(x, norm_mix_g, w_in, conv_w, conv_b, w_rgate, b_rgate, w_igate, b_igate, lru_lambda, w_out_a, sgu_ln_g, sgu_ln_b, sgu_w_s, sgu_b_s, w_out_b, w_out, norm_mlp_g, w_up, w_down, norm_final_g, loss_target, m_norm_mix_g, m_w_in, m_conv_w, m_conv_b, m_w_rgate, m_b_rgate, m_w_igate, m_b_igate, m_lru_lambda, m_w_out_a, m_sgu_ln_g, m_sgu_ln_b, m_sgu_w_s, m_sgu_b_s, m_w_out_b, m_w_out, m_norm_mlp_g, m_w_up, m_w_down, m_norm_final_g, v_norm_mix_g, v_w_in, v_conv_w, v_conv_b, v_w_rgate, v_b_rgate, v_w_igate, v_b_igate, v_lru_lambda, v_w_out_a, v_sgu_ln_g, v_sgu_ln_b, v_sgu_w_s, v_sgu_b_s, v_w_out_b, v_w_out, v_norm_mlp_g, v_w_up, v_w_down, v_norm_final_g):
    chip = _chip_index(lax.axis_index("x"), lax.axis_index("y"))
    core = lax.axis_index("c")
    quarter_h = HEAD_DIM // N_CHIPS
    quarter_d = D_MODEL // N_CHIPS

    as_2d = lambda a: a.reshape(-1, a.shape[-1])
    big_w = [as_2d(w) for w in (w_in, w_rgate, w_igate, w_out_a, w_out_b, w_out, w_up, w_down)]
    big_m = [as_2d(w) for w in (m_w_in, m_w_rgate, m_w_igate, m_w_out_a, m_w_out_b, m_w_out, m_w_up, m_w_down)]
    big_v = [as_2d(w) for w in (v_w_in, v_w_rgate, v_w_igate, v_w_out_a, v_w_out_b, v_w_out, v_w_up, v_w_down)]

    packed = jnp.concatenate([conv_w[0], b_rgate[0], b_igate[0]], axis=1)
    packed = jnp.concatenate([packed, jnp.zeros_like(packed)], axis=0)
    s_in, s_r, s_i, s_oa, s_ob, s_out, s_up, s_down = [w.astype(BF16) for w in big_w]
    xs, target = x[0], loss_target[0]
    g3 = norm_final_g.reshape(1, D_MODEL)
    bias_s = jnp.broadcast_to(jnp.transpose(sgu_b_s[0])[:, :, None], (CHUNK, GROUPS, GROUP_DIM)).reshape(CHUNK, D_MODEL)
    core_arr = core.reshape(1).astype(jnp.int32)
    place = jnp.stack([chip, core]).astype(jnp.int32)
    quarter = lambda g: g.reshape(N_CHIPS, D_MODEL // N_CHIPS, D_MODEL)

    def pair_add(nm, g, from_sibling):
        return _pair_add("pair_add_" + nm, core_arr, g.reshape(N_CHIPS, 2, g.shape[1] // 2, g.shape[2]), from_sibling)

    def chip_sum(nm, pair, from_chips):
        return _chip_sum("chip_sum_" + nm, place, pair, from_chips)

    order = jnp.stack([chip, chip ^ 2, chip ^ 1, chip ^ 3]).astype(jnp.int32)
    (z, n1, (w_in_st, wr_st, wi_st)), ((packed_all,), late) = _fwd_in(
        xs, norm_mix_g, [s_in, s_r, s_i], order,
        jobs=[_gather_small_job(packed), _gather_near_job([s_oa, s_ob, s_out])])
    pick = lambda lo, hi: packed_all[:, :HEADS, lo:hi].transpose(1, 0, 2).reshape(HEADS, -1)
    conv_w_full = pick(0, quarter_d)
    br_full = pick(quarter_d, quarter_d + quarter_h).reshape(1, D_MODEL)
    bi_full = pick(quarter_d + quarter_h, quarter_d + 2 * quarter_h).reshape(1, D_MODEL)
    wr, wi = _unstack_heads(wr_st), _unstack_heads(wi_st)
    lru = (conv_w_full, conv_b, wr, br_full, wi, bi_full, lru_lambda)
    sgu = (sgu_ln_g, sgu_ln_b, sgu_w_s[0], bias_s)

    after = lambda arrays, result: lax.optimization_barrier((arrays, result))[0]
    w_up_st, w_dn = _sequencer_call(
        "gather_mlp", 8, _gather_near_job(after([s_up, s_down], n1)).then(_gather_far_job).then(_gather_pass_job))
    w_dn = w_dn.reshape(D_FF, D_MODEL)
    late_step = (3 * (xs.shape[0] // SEQ_TILE) // 4,)
    (ya, *saved), (late,) = _fwd_lru(z, *lru, jobs=[_gather_far_job(late).then(_gather_pass_job, at=late_step)])
    w_oa, w_ob, w_o = [w.reshape(D_MODEL, D_MODEL) for w in late]
    (yb, pa, pb, h1, n2), _ = _fwd_sgu_merge(ya, z, xs, *sgu, w_oa, w_ob, w_o, norm_mlp_g)
    (act, dup, dh2b, dh1, loss_part, dg3, dg2), _ = _mlp(n2, h1, target, w_up_st, w_dn, norm_mlp_g, g3)

    d_down, _ = _weight_grad("dw_down", act, dh2b, N_CHIPS, True, False, D_MODEL)
    r_down, = _sequencer_call("send_w_down", 10, _pair_send_job([d_down]))
    d_up, _ = _weight_grad("dw_up", n2, dup, N_CHIPS, False, True, D_MODEL)
    r_up, = _sequencer_call("send_w_up", 11, _pair_send_job([d_up]))
    p_down, p_up = pair_add("w_down", d_down, r_down), pair_add("w_up", d_up, r_up)
    (dz, merged, dpa, dpb, dh1b, dlg, dlb, dws, dbs, dcw, dcb, dwr, dbr, dwi, dbi, dlam), ((q_up, q_down),) = _bwd_mix(
        dh1, pa, pb, z, *saved, w_oa, w_ob, w_o, *sgu, conv_w_full, wr, wi, lru_lambda,
        jobs=[_chip_exchange_job([p_up, p_down])])
    half_up, half_down = chip_sum("w_up", p_up, q_up), chip_sum("w_down", p_down, q_down)
    names = ("w_in", "w_rgate", "w_igate", "w_out_a", "w_out_b", "w_out", "w_up", "w_down")
    (d_out, d_oa, d_ob), ((full_up, full_down),) = _weight_grads_square(
        "dw_projections", [(merged, dh1b), (ya, dpa), (yb, dpb)], jobs=[_share_job([half_up, half_down])])
    mids = [quarter(d_oa), quarter(d_ob), quarter(d_out)]
    r_mids = _sequencer_call("send_mids", 1, _pair_send_job(mids))
    gates = [_stack_heads(dwr).astype(BF16), _stack_heads(dwi).astype(BF16)]
    small = _pack_small(dcw, dcb, dbr, dbi, dlam, dlg, dlb, dg2, dg3, loss_part, dbs)
    p_mids = [pair_add(nm, g, r) for nm, g, r in zip(names[3:6], mids, r_mids)]
    q_mids = _sequencer_call("exchange_mids", 2, _chip_exchange_job(p_mids))
    d_in, (r_gates, (vec_all, ws_all)) = _weight_grad(
        "dw_in", n1, dz, N_CHIPS, False, True, IN_SHARD,
        jobs=[_pair_send_job(gates), _gather_all_job([small, dws])])
    r_in, = _sequencer_call("send_w_in", 3, _pair_send_job([d_in]))
    adam_args = {nm: (w, m, v) for nm, w, m, v in zip(names, big_w, big_m, big_v)}

    def adamw(nm, g):
        w, m, v = adam_args[nm]
        g = g.reshape(w.shape)
        return g, _adamw("adamw_" + nm, g, w, m, v)[0]

    p_gates = [pair_add(nm, g, r) for nm, g, r in zip(names[1:3], gates, r_gates)]
    half_mids = [chip_sum(nm, p, q) for nm, p, q in zip(names[3:6], p_mids, q_mids)]
    full_mids = _sequencer_call("share_mids", 12, _share_job(half_mids))
    p_first = [pair_add("w_in", d_in, r_in)] + p_gates
    q_first = _sequencer_call("exchange_w_in", 4, _chip_exchange_job(p_first))
    (grad_x, dg1), _ = _bwd_in(dz, xs, dh1, w_in_st, norm_mix_g)
    dg1_all, = _sequencer_call("gather_dg1", 6, _gather_all_job([dg1]))
    done = {nm: adamw(nm, f) for nm, f in zip(("w_up", "w_down") + names[3:6], [full_up, full_down] + full_mids)}
    q_first = after(q_first, [out[0] for _, out in done.values()])
    half_first = [chip_sum(nm, p, q) for nm, p, q in zip(names[:3], p_first, q_first)]
    full_first = _sequencer_call("share_last", 5, _share_job(half_first))
    done.update({nm: adamw(nm, f) for nm, f in zip(names[:3], full_first)})
    full, big_out = [done[nm][0] for nm in names], [done[nm][1] for nm in names]

    vec, ws_sum = _sum_small(vec_all, ws_all, dg1_all)
    row = lambda r: vec[r:r + 1]
    shard = lambda a, width: lax.dynamic_slice_in_dim(a, chip * width, width, axis=1)
    g_small = dict(
        norm_mix_g=row(ROW_G1), conv_w=shard(vec[ROW_CW:ROW_CW + CONV_WIDTH], quarter_d), conv_b=row(ROW_CB),
        b_rgate=shard(row(ROW_BR).reshape(HEADS, HEAD_DIM), quarter_h),
        b_igate=shard(row(ROW_BI).reshape(HEADS, HEAD_DIM), quarter_h), lru_lambda=row(ROW_LAM),
        sgu_ln_g=row(ROW_LG), sgu_ln_b=row(ROW_LB),
        sgu_w_s=ws_sum.reshape(CHUNK, GROUPS, CHUNK).transpose(1, 0, 2).reshape(GROUPS * CHUNK, CHUNK),
        sgu_b_s=vec[ROW_BS:ROW_BS + GROUPS, 0:CHUNK], norm_mlp_g=row(ROW_G2), norm_final_g=row(ROW_G3))
    loss = vec[ROW_LOSS, 0]
    small_names = list(g_small)
    given = dict(
        norm_mix_g=(norm_mix_g, m_norm_mix_g, v_norm_mix_g), conv_w=(conv_w, m_conv_w, v_conv_w),
        conv_b=(conv_b, m_conv_b, v_conv_b), b_rgate=(b_rgate, m_b_rgate, v_b_rgate),
        b_igate=(b_igate, m_b_igate, v_b_igate), lru_lambda=(lru_lambda, m_lru_lambda, v_lru_lambda),
        sgu_ln_g=(sgu_ln_g, m_sgu_ln_g, v_sgu_ln_g), sgu_ln_b=(sgu_ln_b, m_sgu_ln_b, v_sgu_ln_b),
        sgu_w_s=(sgu_w_s, m_sgu_w_s, v_sgu_w_s), sgu_b_s=(sgu_b_s, m_sgu_b_s, v_sgu_b_s),
        norm_mlp_g=(norm_mlp_g, m_norm_mlp_g, v_norm_mlp_g), norm_final_g=(norm_final_g, m_norm_final_g, v_norm_final_g))
    g2d = [g_small[nm] for nm in small_names]
    to2d = lambda a, g: a.reshape(g.shape)
    d_s, m_s, v_s = _adamw_small(
        g2d, *[[to2d(given[nm][q], g) for nm, g in zip(small_names, g2d)] for q in range(3)])

    shapes = dict(
        norm_mix_g=norm_mix_g, w_in=w_in, conv_w=conv_w, conv_b=conv_b, w_rgate=w_rgate, b_rgate=b_rgate,
        w_igate=w_igate, b_igate=b_igate, lru_lambda=lru_lambda, w_out_a=w_out_a, sgu_ln_g=sgu_ln_g,
        sgu_ln_b=sgu_ln_b, sgu_w_s=sgu_w_s, sgu_b_s=sgu_b_s, w_out_b=w_out_b, w_out=w_out, norm_mlp_g=norm_mlp_g,
        w_up=w_up, w_down=w_down, norm_final_g=norm_final_g)
    grads, deltas, new_m, new_v = {}, {}, {}, {}
    for nm, g, (d, nmom, nvar) in zip(names, full, big_out):
        grads[nm], deltas[nm], new_m[nm], new_v[nm] = g, d, nmom, nvar
    for p, nm in enumerate(small_names):
        grads[nm], deltas[nm], new_m[nm], new_v[nm] = g2d[p], d_s[p], m_s[p], v_s[p]
    order = list(shapes)
    out = [loss, grad_x[None]]
    for group in (grads, deltas, new_m, new_v):
        out += [group[nm].reshape(shapes[nm].shape) for nm in order]
    return tuple(out)
```

```python
import functools

import jax
import jax.numpy as jnp
from jax import lax
from jax.experimental import pallas as pl
from jax.experimental.pallas import tpu as pltpu
from jax.experimental.pallas import tpu_sc as plsc

F32 = jnp.float32
BF16 = jnp.bfloat16
MESH = pl.DeviceIdType.MESH

D_MODEL = 1024
D_IN = 6 * D_MODEL
D_FF = 4 * D_MODEL
N_CHIPS = 4
IN_SHARD = D_IN // N_CHIPS
HEADS = 4
HEAD_DIM = D_MODEL // HEADS
GROUPS = 4
GROUP_DIM = D_MODEL // GROUPS
CHUNK = 128
CONV_WIDTH = 4
LRU_C = 8.0
NORM_EPS = 1e-6
LN_EPS = 1e-5

ADAM_LR = 0.001
ADAM_B1 = 0.9
ADAM_B2 = 0.999
ADAM_EPS = 1e-08
ADAM_WD = 0.01
ADAM_STEP = 10

SUBLANES = 8
MM_TILE = 512
IN_TILE = 1024
SEQ_TILE = 256
DW_TILE = 2048
VMEM_LIMIT_BYTES = 56 * 1024 * 1024

GELU_K0 = 0.7978845608028654
GELU_K1 = 0.044715


def _params(n_grid_axes=1):
    return pltpu.CompilerParams(
        dimension_semantics=("arbitrary",) * n_grid_axes, vmem_limit_bytes=VMEM_LIMIT_BYTES)


def _resident(shape):
    nd = len(shape)
    return pl.BlockSpec(shape, lambda *_: (0,) * nd, pipeline_mode=pl.Buffered(1))


def _const(shape):
    nd = len(shape)
    return pl.BlockSpec(shape, lambda *_: (0,) * nd)


def _dot(a, b):
    return jnp.dot(a, b, preferred_element_type=F32)


def _dot_nt(a, b):
    return lax.dot_general(a, b, (((1,), (1,)), ((), ())), preferred_element_type=F32)


def _dot_tn(a, b):
    return lax.dot_general(a, b, (((0,), (0,)), ((), ())), preferred_element_type=F32)


def _gelu(x):
    t = jnp.tanh(GELU_K0 * x * (1.0 + GELU_K1 * x * x))
    return 0.5 * x * (1.0 + t)


def _gelu_and_grad(x):
    x2 = x * x
    t = jnp.tanh(GELU_K0 * x * (1.0 + GELU_K1 * x2))
    g = 0.5 * x * (1.0 + t)
    dg = 0.5 * (1.0 + t) + 0.5 * x * (1.0 - t * t) * (GELU_K0 * (1.0 + 3.0 * GELU_K1 * x2))
    return g, dg


def _rms(x):
    r = lax.rsqrt(jnp.mean(x * x, axis=-1, keepdims=True) + NORM_EPS)
    return x * r, r


def _rms_bwd(dn, xhat, r):
    return r * (dn - xhat * jnp.mean(dn * xhat, axis=-1, keepdims=True))


def _col_sum(v):
    return jnp.sum(v, axis=0, keepdims=True)


def _shift_down(x, tail8, k):
    xs = pltpu.roll(x, k, 0)
    ts = pltpu.roll(tail8, k, 0)
    ridx = lax.broadcasted_iota(jnp.int32, tail8.shape, 0)
    head = jnp.where(ridx < k, ts, xs[0:SUBLANES])
    return jnp.concatenate([head, xs[SUBLANES:]], axis=0)


def _shift_up(x, head8, k):
    n = x.shape[0]
    xs = pltpu.roll(x, n - k, 0)
    hs = pltpu.roll(head8, SUBLANES - k, 0)
    ridx = lax.broadcasted_iota(jnp.int32, head8.shape, 0)
    last = jnp.where(ridx >= SUBLANES - k, hs, xs[n - SUBLANES:n])
    return jnp.concatenate([xs[:n - SUBLANES], last], axis=0)


def _scan_forward(a, b, carry):
    n, cols = a.shape
    groups = n // SUBLANES
    a = a.reshape(groups, SUBLANES, cols)
    b = b.reshape(groups, SUBLANES, cols)
    sub = lax.broadcasted_iota(jnp.int32, a.shape, 1)
    for s in (1, 2, 4):
        a_s = pltpu.roll(a, s, 1)
        b_s = pltpu.roll(b, s, 1)
        m = sub >= s
        b = jnp.where(m, a * b_s + b, b)
        a = jnp.where(m, a * a_s, a)
    out = []
    for g in range(groups):
        h = a[g] * carry + b[g]
        out.append(h)
        carry = h[SUBLANES - 1:SUBLANES]
    return jnp.concatenate(out, axis=0), carry


def _scan_backward(a, b, carry):
    n, cols = a.shape
    groups = n // SUBLANES
    a = a.reshape(groups, SUBLANES, cols)
    b = b.reshape(groups, SUBLANES, cols)
    sub = lax.broadcasted_iota(jnp.int32, a.shape, 1)
    for s in (1, 2, 4):
        a_s = pltpu.roll(a, SUBLANES - s, 1)
        b_s = pltpu.roll(b, SUBLANES - s, 1)
        m = sub < SUBLANES - s
        b = jnp.where(m, a * b_s + b, b)
        a = jnp.where(m, a * a_s, a)
    out = [None] * groups
    for g in reversed(range(groups)):
        h = a[g] * carry + b[g]
        out[g] = h
        carry = h[0:1]
    return jnp.concatenate(out, axis=0), carry


def _softplus_neg(lam):
    e = jnp.exp(-jnp.abs(lam))
    u = 1.0 + e
    log1p_e = jnp.where(u == 1.0, e, jnp.log(u) * (e / jnp.where(u == 1.0, 1.0, u - 1.0)))
    return jnp.maximum(-lam, 0.0) + log1p_e


def _lru_gates(xa, tail8, cw_ref, cb_ref, wr_ref, br_ref, wi_ref, bi_ref, lam_ref):
    cw = cw_ref[...]
    xc = cb_ref[...] + cw[0:1] * xa
    for k in range(1, CONV_WIDTH):
        xc = xc + cw[k:k + 1] * _shift_down(xa, tail8, k)
    xcb = xc.astype(BF16)
    pre_r, pre_i = [], []
    for h in range(HEADS):
        cols = slice(h * HEAD_DIM, (h + 1) * HEAD_DIM)
        pre_r.append(_dot(xcb[:, cols], wr_ref[h]))
        pre_i.append(_dot(xcb[:, cols], wi_ref[h]))
    r = jax.nn.sigmoid(jnp.concatenate(pre_r, axis=1) + br_ref[...])
    ig = jax.nn.sigmoid(jnp.concatenate(pre_i, axis=1) + bi_ref[...])
    _, a, mult = _decay(r, lam_ref)
    return xc, r, ig, a, mult


def _decay(r, lam_ref):
    sp = _softplus_neg(lam_ref[...])
    log_a = ((-LRU_C) * sp) * r
    a = jnp.exp(log_a)
    th = jnp.tanh(log_a)
    return sp, a, jnp.sqrt((-2.0 * th) / (1.0 - th))


class _Phase:
    def __init__(self, copies, n_sem, n_local, start=None, finish=None):
        self.copies, self.n_sem, self.n_local, self.start, self.finish = copies, n_sem, n_local, start, finish


class _Job:
    def __init__(self, inputs, out_shape, n_sem, copies, peers, aliases=None, n_local=0):
        self.inputs, self.out_shape = list(inputs), list(out_shape)
        self.aliases = dict(aliases or {})
        self.phases = [_Phase(copies, n_sem, n_local)]
        self.peers = tuple(peers)

    def then(self, make, at=None):
        nxt = make(self.out_shape)
        self.phases[-1].finish = at
        nxt.phases[0].start = at
        self.phases += nxt.phases
        self.peers = tuple(sorted(set(self.peers + nxt.peers)))
        return self


def _fused_call(body, jobs, *, name, grid, in_specs, out_specs, out_shape, scratch_shapes=(),
                input_output_aliases=None, compiler_params=None, n_prefetch=0, jobs_start_after=None):
    single = not isinstance(out_shape, (list, tuple))
    out_specs = [out_specs] if single else list(out_specs)
    out_shape = [out_shape] if single else list(out_shape)
    n_scr = len(scratch_shapes)
    in_specs, scratch_shapes = list(in_specs), list(scratch_shapes)
    n_in, n_out = len(in_specs), len(out_shape)
    aliases = dict(input_output_aliases or {})
    in_at, out_at, phases = [], [], []
    for q, job in enumerate(jobs):
        in_at.append(len(in_specs))
        out_at.append(len(out_shape))
        for i, o in job.aliases.items():
            aliases[n_prefetch + len(in_specs) + i] = len(out_shape) + o
        in_specs += [ANY] * len(job.inputs)
        out_specs += [ANY] * len(job.out_shape)
        out_shape += job.out_shape
        for k, phase in enumerate(job.phases):
            phases.append((q, k, phase, len(scratch_shapes)))
            scratch_shapes += [pltpu.SemaphoreType.DMA((phase.n_sem,)), pltpu.SemaphoreType.DMA((phase.n_sem,)),
                               pltpu.SemaphoreType.DMA((max(phase.n_local, 1),))]
    n_in_all, n_out_all = len(in_specs), len(out_shape)
    first_step, last_step = (0,) * len(grid), tuple(g - 1 for g in grid)

    def full_body(*refs):
        prefetch, refs = refs[:n_prefetch], refs[n_prefetch:]
        ins, outs, scr = refs[:n_in_all], refs[n_in_all:n_in_all + n_out_all], refs[n_in_all + n_out_all:]
        ids = [pl.program_id(a) for a in range(len(grid))]
        at_step = lambda step: functools.reduce(jnp.logical_and, [i == k for i, k in zip(ids, step)])

        def copies(q, k, phase, sem_at):
            job = jobs[q]
            mine = outs[out_at[q]:out_at[q] + len(job.out_shape)]
            return phase.copies(ins[in_at[q]:in_at[q] + len(job.inputs)] if k == 0 else mine, mine,
                                *scr[sem_at:sem_at + 3])

        def start(*phase):
            def go():
                sends, _, local = copies(*phase)
                for cp in local + sends:
                    cp.start()
            return go

        def finish(*phase):
            def go():
                sends, arrivals, local = copies(*phase)
                for cp in arrivals:
                    cp.wait_recv()
                for cp in sends:
                    cp.wait_send()
                for cp in local:
                    cp.wait()
            return go

        for phase in phases:
            if phase[2].start is None and jobs_start_after is None:
                pl.when(at_step(first_step))(start(*phase))
        body(*prefetch, *ins[:n_in], *outs[:n_out], *scr[:n_scr])
        for phase in phases:
            pl.when(at_step(phase[2].finish or last_step))(finish(*phase))
            nxt = phase[2].start or jobs_start_after
            if nxt is not None:
                pl.when(at_step(nxt))(start(*phase))

    if n_prefetch:
        layout = dict(grid_spec=pltpu.PrefetchScalarGridSpec(
            num_scalar_prefetch=n_prefetch, grid=grid, in_specs=in_specs, out_specs=out_specs,
            scratch_shapes=scratch_shapes))
    else:
        layout = dict(grid=grid, in_specs=in_specs, out_specs=out_specs, scratch_shapes=scratch_shapes)
    call = pl.pallas_call(
        full_body, name=name, out_shape=out_shape, input_output_aliases=aliases, compiler_params=compiler_params,
        **layout)

    def run(*args):
        res = call(*args, *[a for job in jobs for a in job.inputs])
        mine = res[0] if single else list(res[:n_out])
        return mine, [list(res[at:at + len(job.out_shape)]) for at, job in zip(out_at, jobs)]

    return run


def _fwd_in(x, g1, shards, order, jobs=()):
    t = x.shape[0]
    rows_per_step = min(IN_TILE, t)
    n_tiles = t // rows_per_step
    n = len(shards)
    halves = [s.shape[0] // 2 for s in shards]

    def body(order_ref, x_ref, g_ref, *refs):
        del order_ref
        ins, (z_ref, n_ref), outs = refs[:n], refs[n:n + 2], refs[n + 2:2 * n + 2]
        wbuf, nbuf, send, recv, local = refs[2 * n + 2:]
        s, i = pl.program_id(0), pl.program_id(1)
        x_, y_, c, chips = _place()
        k_me = _chip_index(x_, y_)

        def block(w, chip, pc):
            return outs[w].at[_chip_index(*chip), pl.ds(pc * halves[w], halves[w]), :]

        def over_ici(w, j, landing):
            return pltpu.make_async_remote_copy(
                src_ref=ins[w].at[pl.ds(c * halves[w], halves[w]), :],
                dst_ref=block(w, chips[j] if landing else (x_, y_), c), send_sem=send.at[6 * w + j],
                recv_sem=recv.at[6 * w + j], device_id=(*chips[j], c), device_id_type=MESH)

        def to_sibling(w, j, landing):
            blk = block(w, chips[j], 1 - c if landing else c)
            return pltpu.make_async_remote_copy(
                src_ref=blk, dst_ref=blk, send_sem=send.at[6 * w + 3 + j], recv_sem=recv.at[6 * w + 3 + j],
                device_id=(x_, y_, 1 - c), device_id_type=MESH)

        own = [pltpu.make_async_copy(wbuf, outs[0].at[k_me], local.at[0])]
        own += [pltpu.make_async_copy(ins[w], outs[w].at[k_me], local.at[w]) for w in range(1, n)]

        @pl.when((s == 0) & (i == 0))
        def _():
            for j in range(2):
                for w in range(n):
                    over_ici(w, j, False).start()
            load = pltpu.make_async_copy(ins[0], wbuf, local.at[n])
            load.start()
            load.wait()
            for cp in own:
                cp.start()

        for j in range(N_CHIPS - 1):
            @pl.when((s == j + 1) & (i == 0))
            def _(j=j):
                for w in range(n):
                    over_ici(w, j, True).wait_recv()
                for w in range(n):
                    to_sibling(w, j, False).start()
                if j == 0:
                    for w in range(n):
                        over_ici(w, 2, False).start()
                    own[0].wait()
                for w in range(n):
                    to_sibling(w, j, True).wait_recv()
                load = pltpu.make_async_copy(outs[0].at[_chip_index(*chips[j])], wbuf, local.at[n])
                load.start()
                load.wait()

        rows = pl.ds(pl.multiple_of(i * rows_per_step, rows_per_step), rows_per_step)

        @pl.when(s == 0)
        def _():
            xhat, _ = _rms(x_ref[...])
            nrm = (xhat * g_ref[...]).astype(BF16)
            nbuf[rows, :] = nrm
            n_ref[...] = nrm

        z_ref[...] = _dot(nbuf[rows, :], wbuf[...])

        @pl.when((s == N_CHIPS - 1) & (i == n_tiles - 1))
        def _():
            for j in range(N_CHIPS - 1):
                for w in range(n):
                    over_ici(w, j, False).wait_send()
                    to_sibling(w, j, False).wait_send()
            for cp in own[1:]:
                cp.wait()

    once = lambda s, i, order: (jnp.where(s == 0, i, n_tiles - 1), 0)
    (z, n1, *stacked), job_outs = _fused_call(
        body, jobs, name="fwd_in", grid=(N_CHIPS, n_tiles), n_prefetch=1,
        in_specs=[pl.BlockSpec((rows_per_step, D_MODEL), once), _const((1, D_MODEL))] + [ANY] * n,
        out_specs=[pl.BlockSpec((rows_per_step, IN_SHARD), lambda s, i, order: (i, order[s])),
                   pl.BlockSpec((rows_per_step, D_MODEL), once)] + [ANY] * n,
        out_shape=[jax.ShapeDtypeStruct((t, D_IN), F32), jax.ShapeDtypeStruct((t, D_MODEL), BF16)]
        + [jax.ShapeDtypeStruct((N_CHIPS,) + s.shape, s.dtype) for s in shards],
        scratch_shapes=[pltpu.VMEM(shards[0].shape, BF16), pltpu.VMEM((t, D_MODEL), BF16),
                        pltpu.SemaphoreType.DMA((6 * n,)),
                        pltpu.SemaphoreType.DMA((6 * n,)), pltpu.SemaphoreType.DMA((n + 1,))],
        compiler_params=_params(2), jobs_start_after=(1, 0),
    )(order, x, g1, *shards)
    return (z, n1, stacked), job_outs


def _fwd_mix(z, x, conv_w, conv_b, wr, br, wi, bi, lam, ln_g, ln_b, w_s, bias_full, w_oa, w_ob, w_out, g2, jobs=()):
    t = z.shape[0]

    def lru_part(xa_ref, ga_ref, cw_ref, cb_ref, wr_ref, br_ref, wi_ref, bi_ref, lam_ref, ya_ref, h_ref, xc_ref, r_ref,
                 ig_ref, tail_ref, carry_ref):
        @pl.when(pl.program_id(0) == 0)
        def _():
            tail_ref[...] = jnp.zeros_like(tail_ref)
            carry_ref[...] = jnp.zeros_like(carry_ref)

        xa = xa_ref[...]
        xc, r, ig, a, mult = _lru_gates(xa, tail_ref[...], cw_ref, cb_ref, wr_ref, br_ref, wi_ref, bi_ref, lam_ref)
        tail_ref[...] = xa[SEQ_TILE - SUBLANES:]
        xc_ref[...], r_ref[...], ig_ref[...] = xc, r, ig
        h, carry = _scan_forward(a, xc * ig * mult, carry_ref[...])
        carry_ref[...] = carry
        h_ref[...] = h
        ya_ref[...] = (h * _gelu(ga_ref[...])).astype(BF16)

    def sgu_merge_part(ya_ref, ub_ref, vb_ref, m_ref, x_ref, lg_ref, lb_ref, ws_ref, bias_ref, woa_ref, wob_ref,
                       wout_ref, g_ref, yb_ref, pa_ref, pb_ref, h1_ref, n2_ref):
        u, _, _, _, _, vn = _sgu_forward_parts(ub_ref[...], vb_ref[...], lg_ref, lb_ref)
        mask = _causal_mask()
        wm = [jnp.where(mask, ws_ref[g], 0.0).astype(BF16) for g in range(GROUPS)]
        for c in range(SEQ_TILE // CHUNK):
            rows = slice(c * CHUNK, (c + 1) * CHUNK)
            for g in range(GROUPS):
                cols = slice(g * GROUP_DIM, (g + 1) * GROUP_DIM)
                sp = _dot(wm[g], vn[rows, cols]) + bias_ref[:, cols]
                yb_ref[rows, cols] = (u[rows, cols] * sp).astype(BF16)
        pa = _dot(ya_ref[...], woa_ref[...])
        pb = _dot(yb_ref[...], wob_ref[...])
        pa_ref[...] = pa
        pb_ref[...] = pb
        merged = jax.nn.sigmoid(m_ref[:, :D_MODEL]) * pa + jax.nn.sigmoid(m_ref[:, D_MODEL:]) * pb
        h1 = x_ref[...] + _dot(merged.astype(BF16), wout_ref[...])
        h1_ref[...] = h1
        xhat, _ = _rms(h1)
        n2_ref[...] = (xhat * g_ref[...]).astype(BF16)

    def body(z_ref, x_ref, cw_ref, cb_ref, wr_ref, br_ref, wi_ref, bi_ref, lam_ref, lg_ref, lb_ref, ws_ref, bias_ref,
             woa_ref, wob_ref, wout_ref, g_ref, ya_ref, h_ref, xc_ref, r_ref, ig_ref, yb_ref, pa_ref, pb_ref, h1_ref,
             n2_ref, tail_ref, carry_ref):
        def cols(first, count):
            return z_ref.at[:, pl.ds(first * D_MODEL, count * D_MODEL)]

        lru_part(cols(0, 1), cols(1, 1), cw_ref, cb_ref, wr_ref, br_ref, wi_ref, bi_ref, lam_ref, ya_ref, h_ref, xc_ref,
                 r_ref, ig_ref, tail_ref, carry_ref)
        sgu_merge_part(ya_ref, cols(2, 1), cols(3, 1), cols(4, 2), x_ref, lg_ref, lb_ref, ws_ref, bias_ref, woa_ref,
                       wob_ref, wout_ref, g_ref, yb_ref, pa_ref, pb_ref, h1_ref, n2_ref)

    tile = pl.BlockSpec((SEQ_TILE, D_MODEL), lambda i: (i, 0))
    vec = _const((1, D_MODEL))
    sq = _resident((D_MODEL, D_MODEL))
    gate_w = _resident((HEADS, HEAD_DIM, HEAD_DIM))
    bf, f32 = jax.ShapeDtypeStruct((t, D_MODEL), BF16), jax.ShapeDtypeStruct((t, D_MODEL), F32)
    return _fused_call(
        body, jobs, name="fwd_mix", grid=(t // SEQ_TILE,),
        in_specs=[pl.BlockSpec((SEQ_TILE, D_IN), lambda i: (i, 0)), tile, _const((CONV_WIDTH, D_MODEL)), vec, gate_w, vec,
                  gate_w, vec, vec, vec, vec, _const((GROUPS, CHUNK, CHUNK)), _const((CHUNK, D_MODEL)), sq, sq, sq, vec],
        out_specs=[tile] * 10,
        out_shape=[bf, f32, f32, f32, f32, bf, f32, f32, f32, bf],
        scratch_shapes=[pltpu.VMEM((SUBLANES, D_MODEL), F32), pltpu.VMEM((1, D_MODEL), F32)],
        compiler_params=_params(),
    )(z, x, conv_w, conv_b, wr, br, wi, bi, lam, ln_g, ln_b, w_s, bias_full, w_oa, w_ob, w_out, g2)


def _sgu_forward_parts(ub, vb, lg_ref, lb_ref):
    u, du = _gelu_and_grad(ub)
    vg, dvg = _gelu_and_grad(vb)
    mu = jnp.mean(vg, axis=-1, keepdims=True)
    d = vg - mu
    rstd = lax.rsqrt(jnp.mean(d * d, axis=-1, keepdims=True) + LN_EPS)
    vhat = d * rstd
    vn = (vhat * lg_ref[...] + lb_ref[...]).astype(BF16)
    return u, du, dvg, rstd, vhat, vn


def _causal_mask():
    rows = lax.broadcasted_iota(jnp.int32, (CHUNK, CHUNK), 0)
    cols = lax.broadcasted_iota(jnp.int32, (CHUNK, CHUNK), 1)
    return rows >= cols


def _mlp(n2, h1, target, w_up_st, w_down, g2, g3, jobs=()):
    t = n2.shape[0]

    def body(n2_ref, h1_ref, tgt_ref, wup_ref, wdown_ref, g2_ref, g3_ref, act_ref, dup_ref, dh2b_ref, dh1_ref,
             loss_ref, dg3_ref, dg2_ref, relu_ref):
        @pl.when(pl.program_id(0) == 0)
        def _():
            for ref in (loss_ref, dg3_ref, dg2_ref):
                ref[...] = jnp.zeros_like(ref)

        n2 = n2_ref[...]
        h1 = h1_ref[...]
        h2 = h1
        for k in range(N_CHIPS):
            cols = slice(k * D_MODEL, (k + 1) * D_MODEL)
            r = jnp.maximum(_dot(n2, wup_ref[k]), 0.0)
            relu_ref[:, cols] = r
            act = (r * r).astype(BF16)
            act_ref[:, cols] = act
            h2 = h2 + _dot(act, wdown_ref[cols, :])
        xhat, r3 = _rms(h2)
        diff = xhat * g3_ref[...] - tgt_ref[...]
        sq = jnp.sum(diff * diff, axis=1, keepdims=True)
        loss_ref[...] = loss_ref[...] + (0.5 / D_MODEL) * jnp.sum(sq, axis=0, keepdims=True)
        dy = diff * (1.0 / D_MODEL)
        dg3_ref[...] = dg3_ref[...] + _col_sum(dy * xhat)
        dh2 = _rms_bwd(dy * g3_ref[...], xhat, r3)
        dh2b = dh2.astype(BF16)
        dh2b_ref[...] = dh2b
        dn2 = jnp.zeros((SEQ_TILE, D_MODEL), F32)
        for k in range(N_CHIPS):
            cols = slice(k * D_MODEL, (k + 1) * D_MODEL)
            dup = (_dot_nt(dh2b, wdown_ref[cols, :]) * (2.0 * relu_ref[:, cols])).astype(BF16)
            dup_ref[:, cols] = dup
            dn2 = dn2 + _dot_nt(dup, wup_ref[k])
        xhat, r2 = _rms(h1)
        dg2_ref[...] = dg2_ref[...] + _col_sum(dn2 * xhat)
        dh1_ref[...] = dh2 + _rms_bwd(dn2 * g2_ref[...], xhat, r2)

    tile = pl.BlockSpec((SEQ_TILE, D_MODEL), lambda i: (i, 0))
    wide = pl.BlockSpec((SEQ_TILE, D_FF), lambda i: (i, 0))
    vec = _const((1, D_MODEL))
    vec_shape = jax.ShapeDtypeStruct((1, D_MODEL), F32)
    return _fused_call(
        body, jobs, name="mlp", grid=(t // SEQ_TILE,),
        in_specs=[tile, tile, tile, _resident((N_CHIPS, D_MODEL, D_MODEL)), _resident((D_FF, D_MODEL)), vec, vec],
        out_specs=[wide, wide, tile, tile, _const((SUBLANES, 128)), vec, vec],
        out_shape=[jax.ShapeDtypeStruct((t, D_FF), BF16), jax.ShapeDtypeStruct((t, D_FF), BF16),
                   jax.ShapeDtypeStruct((t, D_MODEL), BF16), jax.ShapeDtypeStruct((t, D_MODEL), F32),
                   jax.ShapeDtypeStruct((SUBLANES, 128), F32), vec_shape, vec_shape],
        scratch_shapes=[pltpu.VMEM((SEQ_TILE, D_FF), F32)],
        compiler_params=_params(),
    )(n2, h1, target, w_up_st, w_down, g2, g3)


def _bwd_mix(dh1, pa, pb, z, h, xc, r, ig, w_oa, w_ob, w_out, ln_g, ln_b, w_s, bias_full, conv_w, wr, wi, lam, jobs=()):
    t = dh1.shape[0]
    n_tiles = t // SEQ_TILE
    per_tile = SEQ_TILE // SUBLANES

    def merge_part(dh1_ref, pa_ref, pb_ref, m_ref, woa_ref, wob_ref, wout_ref, dz_ref, dya_ref, dyb_ref, mg_ref,
                   dpa_ref, dpb_ref, dh1b_ref):
        dh1b = dh1_ref[...].astype(BF16)
        dh1b_ref[...] = dh1b
        dm = _dot_nt(dh1b, wout_ref[...])
        pa = pa_ref[...]
        pb = pb_ref[...]
        sa = jax.nn.sigmoid(m_ref[:, :D_MODEL])
        sb = jax.nn.sigmoid(m_ref[:, D_MODEL:])
        mg_ref[...] = (sa * pa + sb * pb).astype(BF16)
        dz_ref[:, :D_MODEL] = (dm * pa * sa * (1.0 - sa)).astype(BF16)
        dz_ref[:, D_MODEL:] = (dm * pb * sb * (1.0 - sb)).astype(BF16)
        dpa = (dm * sa).astype(BF16)
        dpb = (dm * sb).astype(BF16)
        dpa_ref[...] = dpa
        dpb_ref[...] = dpb
        dya_ref[...] = _dot_nt(dpa, woa_ref[...])
        dyb_ref[...] = _dot_nt(dpb, wob_ref[...])

    def sgu_part(dyb_ref, ub_ref, vb_ref, lg_ref, lb_ref, ws_ref, bias_ref, dz_ref, dlg_ref, dlb_ref, dws_ref, dbs_ref,
                 dvn_ref, dsp_acc):
        i = pl.program_id(0)

        @pl.when(i == 0)
        def _():
            dlg_ref[...] = jnp.zeros_like(dlg_ref)
            dlb_ref[...] = jnp.zeros_like(dlb_ref)
            dws_ref[...] = jnp.zeros_like(dws_ref)
            dsp_acc[...] = jnp.zeros_like(dsp_acc)

        u, du, dvg, rstd, vhat, vn = _sgu_forward_parts(ub_ref[...], vb_ref[...], lg_ref, lb_ref)
        dyb = dyb_ref[...]
        mask = _causal_mask()
        wm = [jnp.where(mask, ws_ref[g], 0.0).astype(BF16) for g in range(GROUPS)]
        for c in range(SEQ_TILE // CHUNK):
            rows = slice(c * CHUNK, (c + 1) * CHUNK)
            for g in range(GROUPS):
                cols = slice(g * GROUP_DIM, (g + 1) * GROUP_DIM)
                vn_blk = vn[rows, cols]
                sp = _dot(wm[g], vn_blk) + bias_ref[:, cols]
                dyb_blk = dyb[rows, cols]
                dz_ref[rows, cols] = (dyb_blk * sp * du[rows, cols]).astype(BF16)
                dsp = dyb_blk * u[rows, cols]
                dsp_acc[:, cols] = dsp_acc[:, cols] + dsp
                dspb = dsp.astype(BF16)
                dvn_ref[rows, cols] = _dot_tn(wm[g], dspb)
                wcols = slice(g * CHUNK, (g + 1) * CHUNK)
                dws_ref[:, wcols] = dws_ref[:, wcols] + jnp.where(mask, _dot_nt(dspb, vn_blk), 0.0)
        dvn = dvn_ref[...]
        dlg_ref[...] = dlg_ref[...] + _col_sum(dvn * vhat)
        dlb_ref[...] = dlb_ref[...] + _col_sum(dvn)
        dvhat = dvn * lg_ref[...]
        dvgel = rstd * (dvhat - jnp.mean(dvhat, axis=-1, keepdims=True)
                        - vhat * jnp.mean(dvhat * vhat, axis=-1, keepdims=True))
        dz_ref[:, D_MODEL:] = (dvgel * dvg).astype(BF16)

        @pl.when(i == n_tiles - 1)
        def _():
            lane = lax.broadcasted_iota(jnp.int32, (CHUNK, 128), 1)
            out = jnp.zeros((CHUNK, 128), F32)
            for g in range(GROUPS):
                s = jnp.sum(dsp_acc[:, g * GROUP_DIM:(g + 1) * GROUP_DIM], axis=1, keepdims=True)
                out = out + jnp.where(lane == g, s, 0.0)
            dbs_ref[...] = out

    def lru_part(dya_ref, xa_ref, ga_ref, h_ref, h_prev_ref, xc_ref, r_ref, ig_ref, cw_ref, wr_ref, wi_ref, lam_ref,
                 dz_ref, dcw_ref, dcb_ref, dwr_ref, dbr_ref, dwi_ref, dbi_ref, dlam_ref, lam_carry, dxc_head):
        i = pl.program_id(0)

        @pl.when(i == 0)
        def _():
            for ref in (dcw_ref, dcb_ref, dwr_ref, dbr_ref, dwi_ref, dbi_ref, dlam_ref, lam_carry, dxc_head):
                ref[...] = jnp.zeros_like(ref)

        first_tile = i == n_tiles - 1
        h_tail = jnp.where(first_tile, 0.0, h_prev_ref[...])
        xc, r, ig = xc_ref[...], r_ref[...], ig_ref[...]
        xcb = xc.astype(BF16)
        sp, a, mult = _decay(r, lam_ref)
        h = h_ref[...]
        h_prev = _shift_down(h, h_tail, 1)
        dya = dya_ref[...]
        gg, dgg = _gelu_and_grad(ga_ref[...])
        dz_ref[:, D_MODEL:] = (dya * h * dgg).astype(BF16)
        ones = jnp.ones((SUBLANES, D_MODEL), F32)
        lam_t, lam_first = _scan_backward(_shift_up(a, ones, 1), dya * gg, lam_carry[...])
        lam_carry[...] = a[0:1] * lam_first
        dmult = lam_t * xc * ig
        dla = lam_t * h_prev * a - dmult * (a * a) / mult
        dr = dla * ((-LRU_C) * sp)
        dlam_ref[...] = dlam_ref[...] + _col_sum(dla * r) * (LRU_C * jax.nn.sigmoid(-lam_ref[...]))
        dpr = dr * r * (1.0 - r)
        dpi = lam_t * xc * mult * ig * (1.0 - ig)
        dbr_ref[...] = dbr_ref[...] + _col_sum(dpr)
        dbi_ref[...] = dbi_ref[...] + _col_sum(dpi)
        dprb = dpr.astype(BF16)
        dpib = dpi.astype(BF16)
        dxc_gate = []
        for hd in range(HEADS):
            cols = slice(hd * HEAD_DIM, (hd + 1) * HEAD_DIM)
            dxc_gate.append(_dot_nt(dprb[:, cols], wr_ref[hd]) + _dot_nt(dpib[:, cols], wi_ref[hd]))
            dwr_ref[hd] = dwr_ref[hd] + _dot_tn(xcb[:, cols], dprb[:, cols])
            dwi_ref[hd] = dwi_ref[hd] + _dot_tn(xcb[:, cols], dpib[:, cols])
        dxc = lam_t * ig * mult + jnp.concatenate(dxc_gate, axis=1)
        dcb_ref[...] = dcb_ref[...] + _col_sum(dxc)
        cw = cw_ref[...]
        head = dxc_head[...]
        xa = xa_ref[...]
        dxa = cw[0:1] * dxc
        dcw_ref[0:1, :] = dcw_ref[0:1, :] + _col_sum(dxc * xa)
        for k in range(1, CONV_WIDTH):
            dxc_k = _shift_up(dxc, head, k)
            dxa = dxa + cw[k:k + 1] * dxc_k
            dcw_ref[k:k + 1, :] = dcw_ref[k:k + 1, :] + _col_sum(dxc_k * xa)
        dxc_head[...] = dxc[0:SUBLANES]
        dz_ref[:, :D_MODEL] = dxa.astype(BF16)

    def body(dh1_ref, pa_ref, pb_ref, z_ref, h_ref, h_prev_ref, xc_ref, r_ref, ig_ref, woa_ref, wob_ref, wout_ref,
             lg_ref, lb_ref, ws_ref, bias_ref, cw_ref, wr_ref, wi_ref, lam_ref, dz_ref, mg_ref, dpa_ref, dpb_ref,
             dh1b_ref, dlg_ref, dlb_ref, dws_ref, dbs_ref, dcw_ref, dcb_ref, dwr_ref, dbr_ref, dwi_ref, dbi_ref,
             dlam_ref, dya_ref, dyb_ref, dvn_ref, dsp_acc, lam_carry, dxc_head):
        def cols(ref, first, count):
            return ref.at[:, pl.ds(first * D_MODEL, count * D_MODEL)]

        merge_part(dh1_ref, pa_ref, pb_ref, cols(z_ref, 4, 2), woa_ref, wob_ref, wout_ref, cols(dz_ref, 4, 2), dya_ref,
                   dyb_ref, mg_ref, dpa_ref, dpb_ref, dh1b_ref)
        sgu_part(dyb_ref, cols(z_ref, 2, 1), cols(z_ref, 3, 1), lg_ref, lb_ref, ws_ref, bias_ref, cols(dz_ref, 2, 2),
                 dlg_ref, dlb_ref, dws_ref, dbs_ref, dvn_ref, dsp_acc)
        lru_part(dya_ref, cols(z_ref, 0, 1), cols(z_ref, 1, 1), h_ref, h_prev_ref, xc_ref, r_ref, ig_ref, cw_ref, wr_ref,
                 wi_ref, lam_ref, cols(dz_ref, 0, 2), dcw_ref, dcb_ref, dwr_ref, dbr_ref, dwi_ref, dbi_ref, dlam_ref,
                 lam_carry, dxc_head)

    rev = lambda i: n_tiles - 1 - i
    tile = pl.BlockSpec((SEQ_TILE, D_MODEL), lambda i: (rev(i), 0))
    row = pl.BlockSpec((SEQ_TILE, D_IN), lambda i: (rev(i), 0))
    prev8 = pl.BlockSpec((SUBLANES, D_MODEL), lambda i: (jnp.maximum(rev(i) * per_tile - 1, 0), 0))
    vec = _const((1, D_MODEL))
    sq = _resident((D_MODEL, D_MODEL))
    gate_w = _resident((HEADS, HEAD_DIM, HEAD_DIM))
    gate_acc = _const((HEADS, HEAD_DIM, HEAD_DIM))
    vec_shape = jax.ShapeDtypeStruct((1, D_MODEL), F32)
    gate_shape = jax.ShapeDtypeStruct((HEADS, HEAD_DIM, HEAD_DIM), F32)
    act_bf = jax.ShapeDtypeStruct((t, D_MODEL), BF16)
    return _fused_call(
        body, jobs, name="bwd_mix", grid=(n_tiles,),
        in_specs=[tile, tile, tile, row, tile, prev8, tile, tile, tile, sq, sq, sq, vec, vec,
                  _const((GROUPS, CHUNK, CHUNK)), _const((CHUNK, D_MODEL)), _const((CONV_WIDTH, D_MODEL)), gate_w, gate_w,
                  vec],
        out_specs=[row, tile, tile, tile, tile, vec, vec, _const((CHUNK, GROUPS * CHUNK)), _const((CHUNK, 128)),
                   _const((SUBLANES, D_MODEL)), vec, gate_acc, vec, gate_acc, vec, vec],
        out_shape=[jax.ShapeDtypeStruct((t, D_IN), BF16), act_bf, act_bf, act_bf, act_bf, vec_shape, vec_shape,
                   jax.ShapeDtypeStruct((CHUNK, GROUPS * CHUNK), F32), jax.ShapeDtypeStruct((CHUNK, 128), F32),
                   jax.ShapeDtypeStruct((SUBLANES, D_MODEL), F32), vec_shape, gate_shape, vec_shape, gate_shape,
                   vec_shape, vec_shape],
        scratch_shapes=[pltpu.VMEM((SEQ_TILE, D_MODEL), F32), pltpu.VMEM((SEQ_TILE, D_MODEL), F32),
                        pltpu.VMEM((SEQ_TILE, D_MODEL), F32), pltpu.VMEM((CHUNK, D_MODEL), F32),
                        pltpu.VMEM((1, D_MODEL), F32), pltpu.VMEM((SUBLANES, D_MODEL), F32)],
        compiler_params=_params(),
    )(dh1, pa, pb, z, h, h, xc, r, ig, w_oa, w_ob, w_out, ln_g, ln_b, w_s, bias_full, conv_w, wr, wi, lam)


def _bwd_in(dz, x, dh1, w_in_st, g1, jobs=()):
    t = x.shape[0]

    def body(dz_ref, x_ref, dh1_ref, w_ref, g_ref, dx_ref, dg1_ref):
        @pl.when(pl.program_id(0) == 0)
        def _():
            dg1_ref[...] = jnp.zeros_like(dg1_ref)

        dn1 = jnp.zeros((MM_TILE, D_MODEL), F32)
        for k in range(N_CHIPS):
            dn1 = dn1 + _dot_nt(dz_ref[:, k * IN_SHARD:(k + 1) * IN_SHARD], w_ref[k])
        xhat, r1 = _rms(x_ref[...])
        dg1_ref[...] = dg1_ref[...] + _col_sum(dn1 * xhat)
        dx_ref[...] = dh1_ref[...] + _rms_bwd(dn1 * g_ref[...], xhat, r1)

    tile = pl.BlockSpec((MM_TILE, D_MODEL), lambda i: (i, 0))
    return _fused_call(
        body, jobs, name="bwd_in", grid=(t // MM_TILE,),
        in_specs=[pl.BlockSpec((MM_TILE, D_IN), lambda i: (i, 0)), tile, tile,
                  _resident((N_CHIPS, D_MODEL, IN_SHARD)), _const((1, D_MODEL))],
        out_specs=[tile, _const((1, D_MODEL))],
        out_shape=[jax.ShapeDtypeStruct((t, D_MODEL), F32), jax.ShapeDtypeStruct((1, D_MODEL), F32)],
        compiler_params=_params(),
    )(dz, x, dh1, w_in_st, g1)


def _weight_grad(name, a, b, n_blocks, a_varies, b_varies, width, jobs=()):
    t = a.shape[0]
    rows = min(DW_TILE, t)
    n_t = t // rows

    def body(a_ref, b_ref, o_ref, acc_ref):
        s = pl.program_id(1)
        part = _dot_tn(a_ref[...], b_ref[...])

        @pl.when(s == 0)
        def _():
            acc_ref[...] = part

        @pl.when(s > 0)
        def _():
            acc_ref[...] = acc_ref[...] + part

        @pl.when(s == n_t - 1)
        def _():
            o_ref[...] = acc_ref[...].astype(BF16)

    return _fused_call(
        body, jobs, name=name, grid=(n_blocks, n_t),
        in_specs=[pl.BlockSpec((rows, D_MODEL), (lambda j, s: (s, j)) if a_varies else (lambda j, s: (s, 0))),
                  pl.BlockSpec((rows, width), (lambda j, s: (s, j)) if b_varies else (lambda j, s: (s, 0)))],
        out_specs=pl.BlockSpec((None, D_MODEL, width), lambda j, s: (j, 0, 0)),
        out_shape=jax.ShapeDtypeStruct((n_blocks, D_MODEL, width), BF16),
        scratch_shapes=[pltpu.VMEM((D_MODEL, width), F32)],
        compiler_params=_params(2),
    )(a, b)


def _weight_grads_square(name, pairs, jobs=()):
    n = len(pairs)
    t = pairs[0][0].shape[0]
    rows = min(2 * MM_TILE, t)
    n_t = t // rows

    def body(*refs):
        ins, outs, accs = refs[:2 * n], refs[2 * n:3 * n], refs[3 * n:]
        s = pl.program_id(0)
        for k in range(n):
            part = _dot_tn(ins[2 * k][...], ins[2 * k + 1][...])

            @pl.when(s == 0)
            def _(k=k, part=part):
                accs[k][...] = part

            @pl.when(s > 0)
            def _(k=k, part=part):
                accs[k][...] = accs[k][...] + part

            @pl.when(s == n_t - 1)
            def _(k=k):
                outs[k][...] = accs[k][...].astype(BF16)

    tile = pl.BlockSpec((rows, D_MODEL), lambda s: (s, 0))
    return _fused_call(
        body, jobs, name=name, grid=(n_t,), in_specs=[tile] * (2 * n), out_specs=[_const((D_MODEL, D_MODEL))] * n,
        out_shape=[jax.ShapeDtypeStruct((D_MODEL, D_MODEL), BF16)] * n,
        scratch_shapes=[pltpu.VMEM((D_MODEL, D_MODEL), F32)] * n,
        compiler_params=_params(),
    )(*[x for pair in pairs for x in pair])


def _place():
    x, y, c = lax.axis_index("x"), lax.axis_index("y"), lax.axis_index("c")
    other_chips = [(1 - x, y), (x, 1 - y), (1 - x, 1 - y)]
    return x, y, c, other_chips


def _chip_index(px, py):
    return 2 * px + py


ANY = pl.BlockSpec(memory_space=pl.ANY)
SIBLING = ((0, 0, 1),)
NEIGHBOURS = ((1, 0, 0), (0, 1, 0))
OTHER_CHIPS = NEIGHBOURS + ((1, 1, 0),)


def _near_far(x, y, c):
    return (x ^ (1 - c), y ^ c), (x ^ c, y ^ (1 - c))


def _gather_near_job(shards):
    n = len(shards)
    halves = [s.shape[0] // 2 for s in shards]

    def copies(ins, outs, send, recv, local):
        x, y, c, _ = _place()
        near, _ = _near_far(x, y, c)

        def block(w, chip, pc):
            return outs[w].at[_chip_index(*chip), pl.ds(pc * halves[w], halves[w]), :]

        def copy(w, k, chip, pc, to, src=None):
            return pltpu.make_async_remote_copy(
                src_ref=block(w, chip, pc) if src is None else src, dst_ref=block(w, chip, pc),
                send_sem=send.at[2 * w + k], recv_sem=recv.at[2 * w + k], device_id=to, device_id_type=MESH)

        sends, arrivals, own = [], [], []
        for w in range(n):
            src = ins[w].at[pl.ds(c * halves[w], halves[w]), :]
            own.append(pltpu.make_async_copy(src, block(w, (x, y), c), local.at[w]))
            sends += [copy(w, 0, (x, y), c, (*near, c), src), copy(w, 1, (x, y), c, (x, y, 1 - c), src)]
            arrivals += [copy(w, 0, near, c, (x, y, c)), copy(w, 1, (x, y), 1 - c, (x, y, c))]
        return sends, arrivals, own

    return _Job(shards, [jax.ShapeDtypeStruct((N_CHIPS,) + s.shape, s.dtype) for s in shards], 2 * n, copies,
                NEIGHBOURS + SIBLING, n_local=n)


def _gather_far_job(stacked):
    n = len(stacked)
    halves = [s.shape[1] // 2 for s in stacked]

    def copies(ins, outs, send, recv, local):
        del ins, local
        x, y, c, _ = _place()
        near, far = _near_far(x, y, c)

        def copy(w, k, chip):
            blk = outs[w].at[_chip_index(*chip), pl.ds(c * halves[w], halves[w]), :]
            return pltpu.make_async_remote_copy(
                src_ref=blk, dst_ref=blk, send_sem=send.at[2 * w + k], recv_sem=recv.at[2 * w + k],
                device_id=(*far, c), device_id_type=MESH)

        sends = [copy(w, k, chip) for w in range(n) for k, chip in enumerate(((x, y), near))]
        arrivals = [copy(w, k, chip) for w in range(n) for k, chip in enumerate((far, (1 - x, 1 - y)))]
        return sends, arrivals, []

    return _Job(stacked, [jax.ShapeDtypeStruct(s.shape, s.dtype) for s in stacked], 2 * n, copies, NEIGHBOURS,
                aliases={w: w for w in range(n)})


def _gather_pass_job(stacked):
    n = len(stacked)
    halves = [s.shape[1] // 2 for s in stacked]

    def copies(ins, outs, send, recv, local):
        del ins, local
        x, y, c, chips = _place()

        def copy(w, j, chip, pc, to):
            blk = outs[w].at[_chip_index(*chip), pl.ds(pc * halves[w], halves[w]), :]
            return pltpu.make_async_remote_copy(
                src_ref=blk, dst_ref=blk, send_sem=send.at[3 * w + j], recv_sem=recv.at[3 * w + j], device_id=to,
                device_id_type=MESH)

        sends = [copy(w, j, chip, c, (x, y, 1 - c)) for w in range(n) for j, chip in enumerate(chips)]
        arrivals = [copy(w, j, chip, 1 - c, (x, y, c)) for w in range(n) for j, chip in enumerate(chips)]
        return sends, arrivals, []

    return _Job(stacked, [jax.ShapeDtypeStruct(s.shape, s.dtype) for s in stacked], 3 * n, copies, SIBLING,
                aliases={w: w for w in range(n)})


def _gather_small_job(block):
    def copies(ins, outs, send, recv, local):
        x, y, c, chips = _place()

        def copy(j, chip_from, to):
            return pltpu.make_async_remote_copy(
                src_ref=ins[0], dst_ref=outs[0].at[_chip_index(*chip_from)], send_sem=send.at[j],
                recv_sem=recv.at[j], device_id=to, device_id_type=MESH)

        own = [pltpu.make_async_copy(ins[0], outs[0].at[_chip_index(x, y)], local.at[0])]
        sends = [copy(j, (x, y), (*chip, c)) for j, chip in enumerate(chips)]
        arrivals = [copy(j, chip, (x, y, c)) for j, chip in enumerate(chips)]
        return sends, arrivals, own

    return _Job([block], [jax.ShapeDtypeStruct((N_CHIPS,) + block.shape, block.dtype)], 3, copies, OTHER_CHIPS,
                n_local=1)


def _pair_send_job(grads):
    n = len(grads)
    halves = [g.shape[1] // 2 for g in grads]

    def copies(ins, outs, send, recv, local):
        del local
        x, y, c, _ = _place()
        sends = [pltpu.make_async_remote_copy(
            src_ref=ins[w].at[:, pl.ds((1 - c) * halves[w], halves[w]), :], dst_ref=outs[w], send_sem=send.at[w],
            recv_sem=recv.at[w], device_id=(x, y, 1 - c), device_id_type=MESH) for w in range(n)]
        return sends, sends, []

    return _Job(grads, [jax.ShapeDtypeStruct((N_CHIPS, h, g.shape[2]), g.dtype) for g, h in zip(grads, halves)], n,
                copies, SIBLING)


def _row_block(rows, limit=256):
    return min(rows, limit)


def _pair_add(name, core, mine, theirs):
    _, _, h, cols = mine.shape
    rb = _row_block(h, 512)

    def body(core_ref, a_ref, b_ref, o_ref):
        del core_ref
        o_ref[...] = (a_ref[...].astype(F32) + b_ref[...].astype(F32)).astype(BF16)

    return pl.pallas_call(
        body, name=name,
        grid_spec=pltpu.PrefetchScalarGridSpec(
            num_scalar_prefetch=1, grid=(N_CHIPS, h // rb),
            in_specs=[pl.BlockSpec((None, None, rb, cols), lambda k, r, core_ref: (k, core_ref[0], r, 0)),
                      pl.BlockSpec((None, rb, cols), lambda k, r, core_ref: (k, r, 0))],
            out_specs=pl.BlockSpec((None, rb, cols), lambda k, r, core_ref: (k, r, 0))),
        out_shape=jax.ShapeDtypeStruct(theirs.shape, BF16),
        compiler_params=_params(2),
    )(core, mine, theirs)


def _sequencer_call(name, collective_id, job):
    steps, peers = job.phases, job.peers
    ins = [jax.new_ref(a, memory_space=pltpu.MemorySpace.HBM) for a in job.inputs]
    outs = [ins[{o: i for i, o in job.aliases.items()}[k]] if k in job.aliases.values()
            else jax.empty_ref(shape, memory_space=pltpu.MemorySpace.HBM) for k, shape in enumerate(job.out_shape)]
    sems = [pltpu.SemaphoreType.DMA((n,)) for step in steps for n in (step.n_sem, step.n_sem, max(step.n_local, 1))]

    @pl.kernel(mesh=plsc.ScalarSubcoreMesh(axis_name="sequencer", num_cores=1), name=name, scratch_types=tuple(sems),
               compiler_params=pltpu.CompilerParams(collective_id=collective_id))
    def launch(*sem_refs):
        x, y, c, _ = _place()
        barrier = pltpu.get_barrier_semaphore()
        for dx, dy, dc in peers:
            pl.semaphore_signal(barrier, inc=1, device_id=(x ^ dx, y ^ dy, c ^ dc), device_id_type=MESH)
        pl.semaphore_wait(barrier, len(peers))
        for k, step in enumerate(steps):
            sends, arrivals, own = step.copies(ins if k == 0 else outs, outs, *sem_refs[3 * k:3 * k + 3])
            for cp in own + sends:
                cp.start()
            for cp in arrivals:
                cp.wait_recv()
            for cp in sends:
                cp.wait_send()
            for cp in own:
                cp.wait()

    launch()
    return [ref[...] for ref in outs]


def _chip_exchange_job(sums):
    n = len(sums)

    def copies(ins, outs, send, recv, local):
        del local
        _, _, c, chips = _place()
        sends = [pltpu.make_async_remote_copy(
            src_ref=ins[w].at[_chip_index(*chip)], dst_ref=outs[w].at[j], send_sem=send.at[3 * w + j],
            recv_sem=recv.at[3 * w + j], device_id=(*chip, c), device_id_type=MESH)
            for w in range(n) for j, chip in enumerate(chips)]
        return sends, sends, []

    return _Job(sums, [jax.ShapeDtypeStruct((N_CHIPS - 1,) + s.shape[1:], s.dtype) for s in sums], 3 * n, copies,
                OTHER_CHIPS)


def _chip_sum(name, place, mine, theirs):
    _, h, cols = mine.shape
    rb = _row_block(h, 512)

    def body(place_ref, p_ref, q_ref, o_ref):
        del place_ref
        acc = p_ref[...].astype(F32)
        for j in range(N_CHIPS - 1):
            acc = acc + q_ref[j].astype(F32)
        o_ref[...] = acc

    return pl.pallas_call(
        body, name=name,
        grid_spec=pltpu.PrefetchScalarGridSpec(
            num_scalar_prefetch=1, grid=(h // rb,),
            in_specs=[pl.BlockSpec((None, rb, cols), lambda r, place_ref: (place_ref[0], r, 0)),
                      pl.BlockSpec((N_CHIPS - 1, rb, cols), lambda r, place_ref: (0, r, 0))],
            out_specs=pl.BlockSpec((None, rb, cols), lambda r, place_ref: (place_ref[1], r, 0))),
        out_shape=jax.ShapeDtypeStruct((2, h, cols), F32),
        compiler_params=_params(),
    )(place, mine, theirs)


def _share_job(bufs):
    n = len(bufs)

    def copies(ins, outs, send, recv, local):
        del ins, local
        x, y, c, _ = _place()

        def copy(w, half):
            return pltpu.make_async_remote_copy(
                src_ref=outs[w].at[half], dst_ref=outs[w].at[half], send_sem=send.at[w], recv_sem=recv.at[w],
                device_id=(x, y, 1 - c), device_id_type=MESH)

        return [copy(w, c) for w in range(n)], [copy(w, 1 - c) for w in range(n)], []

    return _Job(bufs, [jax.ShapeDtypeStruct(b.shape, b.dtype) for b in bufs], n, copies, SIBLING,
                aliases={w: w for w in range(n)})


SMALL_ROWS = 24
ROW_G1, ROW_CW, ROW_CB, ROW_BR, ROW_BI, ROW_LAM, ROW_LG, ROW_LB, ROW_G2, ROW_G3, ROW_LOSS, ROW_BS = (
    0, 1, 5, 6, 7, 8, 9, 10, 11, 12, 13, 16)
N_DEV = 8


def _pack_small(dcw, dcb, dbr, dbi, dlam, dlg, dlb, dg2, dg3, loss, dbs):
    def body(dcw_ref, dcb_ref, dbr_ref, dbi_ref, dlam_ref, dlg_ref, dlb_ref, dg2_ref, dg3_ref, loss_ref, dbs_ref, out):
        out[...] = jnp.zeros((SMALL_ROWS, D_MODEL), F32)
        for row, ref in ((ROW_CB, dcb_ref), (ROW_BR, dbr_ref), (ROW_BI, dbi_ref), (ROW_LAM, dlam_ref),
                         (ROW_LG, dlg_ref), (ROW_LB, dlb_ref), (ROW_G2, dg2_ref), (ROW_G3, dg3_ref)):
            out[row:row + 1, :] = ref[...]
        out[ROW_CW:ROW_CW + CONV_WIDTH, :] = dcw_ref[0:CONV_WIDTH, :]
        out[ROW_LOSS:ROW_LOSS + 1, 0:128] = loss_ref[0:1, :]
        out[ROW_BS:ROW_BS + GROUPS, 0:128] = jnp.transpose(dbs_ref[...])[0:GROUPS, :]

    vm = pl.BlockSpec(memory_space=pltpu.VMEM)
    return pl.pallas_call(
        body, name="pack_small", in_specs=[vm] * 11, out_specs=vm,
        out_shape=jax.ShapeDtypeStruct((SMALL_ROWS, D_MODEL), F32),
    )(dcw, dcb, dbr, dbi, dlam, dlg, dlb, dg2, dg3, loss, dbs)


def _gather_all_job(blocks):
    n = len(blocks)
    flips = [(dx, dy, dc) for dx in (0, 1) for dy in (0, 1) for dc in (0, 1)][1:]

    def copies(ins, outs, send, recv, local):
        x, y, c, _ = _place()
        me = 4 * x + 2 * y + c
        sends, arrivals, own = [], [], []
        for w in range(n):
            own.append(pltpu.make_async_copy(ins[w], outs[w].at[me], local.at[w]))
            for k, (dx, dy, dc) in enumerate(flips):
                peer = (x ^ dx, y ^ dy, c ^ dc)
                sem = dict(send_sem=send.at[7 * w + k], recv_sem=recv.at[7 * w + k])
                sends.append(pltpu.make_async_remote_copy(
                    src_ref=ins[w], dst_ref=outs[w].at[me], device_id=peer, device_id_type=MESH, **sem))
                arrivals.append(pltpu.make_async_remote_copy(
                    src_ref=ins[w], dst_ref=outs[w].at[4 * peer[0] + 2 * peer[1] + peer[2]], device_id=peer,
                    device_id_type=MESH, **sem))
        return sends, arrivals, own

    return _Job(blocks, [jax.ShapeDtypeStruct((N_DEV,) + b.shape, b.dtype) for b in blocks], 7 * n, copies,
                OTHER_CHIPS + SIBLING + tuple((dx, dy, 1) for dx, dy, _ in OTHER_CHIPS), n_local=n)


def _sum_small(vec_all, ws_all, dg1_all):
    def body(vec_ref, ws_ref, dg1_ref, vec_out, ws_out):
        vec, ws, dg1 = vec_ref[0], ws_ref[0], dg1_ref[0]
        for d in range(1, N_DEV):
            vec, ws, dg1 = vec + vec_ref[d], ws + ws_ref[d], dg1 + dg1_ref[d]
        vec_out[...] = vec
        vec_out[ROW_G1:ROW_G1 + 1, :] = dg1
        ws_out[...] = ws

    vm = pl.BlockSpec(memory_space=pltpu.VMEM)
    return pl.pallas_call(
        body, name="sum_small", in_specs=[vm] * 3, out_specs=[vm, vm],
        out_shape=[jax.ShapeDtypeStruct(vec_all.shape[1:], F32), jax.ShapeDtypeStruct(ws_all.shape[1:], F32)],
    )(vec_all, ws_all, dg1_all)


def _adamw_math(w, g, m, v):
    m = ADAM_B1 * m + (1.0 - ADAM_B1) * g
    v = ADAM_B2 * v + (1.0 - ADAM_B2) * (g * g)
    m_hat = m / (1.0 - ADAM_B1 ** ADAM_STEP)
    v_hat = v / (1.0 - ADAM_B2 ** ADAM_STEP)
    delta = (-ADAM_LR) * (m_hat / (jnp.sqrt(v_hat) + ADAM_EPS) + ADAM_WD * w)
    return delta, m, v


def _adamw(name, g, w, m, v, jobs=()):
    rows, cols = w.shape
    rb = _row_block(rows)

    def body(g_ref, w_ref, m_ref, v_ref, d_ref, nm_ref, nv_ref):
        d_ref[...], nm_ref[...], nv_ref[...] = _adamw_math(w_ref[...], g_ref[...], m_ref[...], v_ref[...])

    blk = pl.BlockSpec((rb, cols), lambda r: (r, 0))
    return _fused_call(
        body, jobs, name=name, grid=(rows // rb,), in_specs=[blk] * 4, out_specs=[blk] * 3,
        out_shape=[jax.ShapeDtypeStruct(w.shape, F32)] * 3, compiler_params=_params(),
    )(g, w, m, v)


def _adamw_small(grads, ws, ms, vs):
    n = len(grads)

    def body(*refs):
        g_refs, w_refs, m_refs, v_refs = refs[:n], refs[n:2 * n], refs[2 * n:3 * n], refs[3 * n:4 * n]
        outs = refs[4 * n:]
        for p in range(n):
            d, nm, nv = _adamw_math(w_refs[p][...], g_refs[p][...], m_refs[p][...], v_refs[p][...])
            outs[p][...] = d
            outs[n + p][...] = nm
            outs[2 * n + p][...] = nv

    vm = pl.BlockSpec(memory_space=pltpu.VMEM)
    shapes = [jax.ShapeDtypeStruct(w.shape, F32) for w in ws]
    out = pl.pallas_call(
        body, name="adamw_small", in_specs=[vm] * (4 * n), out_specs=[vm] * (3 * n), out_shape=shapes * 3,
    )(*grads, *ws, *ms, *vs)
    return out[:n], out[n:2 * n], out[2 * n:]


def _unstack_heads(w_st):
    per = HEAD_DIM // N_CHIPS
    return w_st.reshape(N_CHIPS, HEADS, per, HEAD_DIM).transpose(1, 0, 2, 3).reshape(HEADS, HEAD_DIM, HEAD_DIM)


def _stack_heads(w):
    per = HEAD_DIM // N_CHIPS
    return w.reshape(HEADS, N_CHIPS, per, HEAD_DIM).transpose(1, 0, 2, 3).reshape(N_CHIPS, HEADS * per, HEAD_DIM)


def kernel(x, norm_mix_g, w_in, conv_w, conv_b, w_rgate, b_rgate, w_igate, b_igate, lru_lambda, w_out_a, sgu_ln_g, sgu_ln_b, sgu_w_s, sgu_b_s, w_out_b, w_out, norm_mlp_g, w_up, w_down, norm_final_g, loss_target, m_norm_mix_g, m_w_in, m_conv_w, m_conv_b, m_w_rgate, m_b_rgate, m_w_igate, m_b_igate, m_lru_lambda, m_w_out_a, m_sgu_ln_g, m_sgu_ln_b, m_sgu_w_s, m_sgu_b_s, m_w_out_b, m_w_out, m_norm_mlp_g, m_w_up, m_w_down, m_norm_final_g, v_norm_mix_g, v_w_in, v_conv_w, v_conv_b, v_w_rgate, v_b_rgate, v_w_igate, v_b_igate, v_lru_lambda, v_w_out_a, v_sgu_ln_g, v_sgu_ln_b, v_sgu_w_s, v_sgu_b_s, v_w_out_b, v_w_out, v_norm_mlp_g, v_w_up, v_w_down, v_norm_final_g):
    chip = _chip_index(lax.axis_index("x"), lax.axis_index("y"))
    core = lax.axis_index("c")
    quarter_h = HEAD_DIM // N_CHIPS
    quarter_d = D_MODEL // N_CHIPS

    as_2d = lambda a: a.reshape(-1, a.shape[-1])
    big_w = [as_2d(w) for w in (w_in, w_rgate, w_igate, w_out_a, w_out_b, w_out, w_up, w_down)]
    big_m = [as_2d(w) for w in (m_w_in, m_w_rgate, m_w_igate, m_w_out_a, m_w_out_b, m_w_out, m_w_up, m_w_down)]
    big_v = [as_2d(w) for w in (v_w_in, v_w_rgate, v_w_igate, v_w_out_a, v_w_out_b, v_w_out, v_w_up, v_w_down)]

    packed = jnp.concatenate([conv_w[0], b_rgate[0], b_igate[0]], axis=1)
    packed = jnp.concatenate([packed, jnp.zeros_like(packed)], axis=0)
    s_in, s_r, s_i, s_oa, s_ob, s_out, s_up, s_down = [w.astype(BF16) for w in big_w]
    xs, target = x[0], loss_target[0]
    g3 = norm_final_g.reshape(1, D_MODEL)
    bias_s = jnp.broadcast_to(jnp.transpose(sgu_b_s[0])[:, :, None], (CHUNK, GROUPS, GROUP_DIM)).reshape(CHUNK, D_MODEL)
    core_arr = core.reshape(1).astype(jnp.int32)
    place = jnp.stack([chip, core]).astype(jnp.int32)
    quarter = lambda g: g.reshape(N_CHIPS, D_MODEL // N_CHIPS, D_MODEL)

    def pair_add(nm, g, from_sibling):
        return _pair_add("pair_add_" + nm, core_arr, g.reshape(N_CHIPS, 2, g.shape[1] // 2, g.shape[2]), from_sibling)

    def chip_sum(nm, pair, from_chips):
        return _chip_sum("chip_sum_" + nm, place, pair, from_chips)

    order = jnp.stack([chip, chip ^ 2, chip ^ 1, chip ^ 3]).astype(jnp.int32)
    projections = _gather_near_job([s_oa, s_ob, s_out]).then(_gather_far_job, at=(2, 0)).then(_gather_pass_job, at=(3, 0))
    (z, n1, (w_in_st, wr_st, wi_st)), ((packed_all,), late) = _fwd_in(
        xs, norm_mix_g, [s_in, s_r, s_i], order, jobs=[_gather_small_job(packed), projections])
    pick = lambda lo, hi: packed_all[:, :HEADS, lo:hi].transpose(1, 0, 2).reshape(HEADS, -1)
    conv_w_full = pick(0, quarter_d)
    br_full = pick(quarter_d, quarter_d + quarter_h).reshape(1, D_MODEL)
    bi_full = pick(quarter_d + quarter_h, quarter_d + 2 * quarter_h).reshape(1, D_MODEL)
    wr, wi = _unstack_heads(wr_st), _unstack_heads(wi_st)
    lru = (conv_w_full, conv_b, wr, br_full, wi, bi_full, lru_lambda)
    sgu = (sgu_ln_g, sgu_ln_b, sgu_w_s[0], bias_s)

    after = lambda arrays, result: lax.optimization_barrier((arrays, result))[0]
    w_up_st, w_dn = _sequencer_call(
        "gather_mlp", 8, _gather_near_job(after([s_up, s_down], n1)).then(_gather_far_job).then(_gather_pass_job))
    w_dn = w_dn.reshape(D_FF, D_MODEL)
    w_oa, w_ob, w_o = [w.reshape(D_MODEL, D_MODEL) for w in late]
    (ya, *saved, yb, pa, pb, h1, n2), _ = _fwd_mix(z, xs, *lru, *sgu, w_oa, w_ob, w_o, norm_mlp_g)
    (act, dup, dh2b, dh1, loss_part, dg3, dg2), _ = _mlp(n2, h1, target, w_up_st, w_dn, norm_mlp_g, g3)

    d_down, _ = _weight_grad("dw_down", act, dh2b, N_CHIPS, True, False, D_MODEL)
    r_down, = _sequencer_call("send_w_down", 10, _pair_send_job([d_down]))
    d_up, _ = _weight_grad("dw_up", n2, dup, N_CHIPS, False, True, D_MODEL)
    r_up, = _sequencer_call("send_w_up", 11, _pair_send_job([d_up]))
    p_down, p_up = pair_add("w_down", d_down, r_down), pair_add("w_up", d_up, r_up)
    (dz, merged, dpa, dpb, dh1b, dlg, dlb, dws, dbs, dcw, dcb, dwr, dbr, dwi, dbi, dlam), ((q_up, q_down),) = _bwd_mix(
        dh1, pa, pb, z, *saved, w_oa, w_ob, w_o, *sgu, conv_w_full, wr, wi, lru_lambda,
        jobs=[_chip_exchange_job([p_up, p_down])])
    half_up, half_down = chip_sum("w_up", p_up, q_up), chip_sum("w_down", p_down, q_down)
    names = ("w_in", "w_rgate", "w_igate", "w_out_a", "w_out_b", "w_out", "w_up", "w_down")
    (d_out, d_oa, d_ob), ((full_up, full_down),) = _weight_grads_square(
        "dw_projections", [(merged, dh1b), (ya, dpa), (yb, dpb)], jobs=[_share_job([half_up, half_down])])
    mids = [quarter(d_oa), quarter(d_ob), quarter(d_out)]
    r_mids = _sequencer_call("send_mids", 1, _pair_send_job(mids))
    gates = [_stack_heads(dwr).astype(BF16), _stack_heads(dwi).astype(BF16)]
    small = _pack_small(dcw, dcb, dbr, dbi, dlam, dlg, dlb, dg2, dg3, loss_part, dbs)
    p_mids = [pair_add(nm, g, r) for nm, g, r in zip(names[3:6], mids, r_mids)]
    q_mids = _sequencer_call("exchange_mids", 2, _chip_exchange_job(p_mids))
    d_in, (r_gates, (vec_all, ws_all)) = _weight_grad(
        "dw_in", n1, dz, N_CHIPS, False, True, IN_SHARD,
        jobs=[_pair_send_job(gates), _gather_all_job([small, dws])])
    r_in, = _sequencer_call("send_w_in", 3, _pair_send_job([d_in]))
    adam_args = {nm: (w, m, v) for nm, w, m, v in zip(names, big_w, big_m, big_v)}

    def adamw(nm, g):
        w, m, v = adam_args[nm]
        g = g.reshape(w.shape)
        return g, _adamw("adamw_" + nm, g, w, m, v)[0]

    p_gates = [pair_add(nm, g, r) for nm, g, r in zip(names[1:3], gates, r_gates)]
    half_mids = [chip_sum(nm, p, q) for nm, p, q in zip(names[3:6], p_mids, q_mids)]
    full_mids = _sequencer_call("share_mids", 12, _share_job(half_mids))
    p_first = [pair_add("w_in", d_in, r_in)] + p_gates
    q_first = _sequencer_call("exchange_w_in", 4, _chip_exchange_job(p_first))
    (grad_x, dg1), _ = _bwd_in(dz, xs, dh1, w_in_st, norm_mix_g)
    dg1_all, = _sequencer_call("gather_dg1", 6, _gather_all_job([dg1]))
    done = {nm: adamw(nm, f) for nm, f in zip(("w_up", "w_down") + names[3:6], [full_up, full_down] + full_mids)}
    q_first = after(q_first, [out[0] for _, out in done.values()])
    half_first = [chip_sum(nm, p, q) for nm, p, q in zip(names[:3], p_first, q_first)]
    full_first = _sequencer_call("share_last", 5, _share_job(half_first))
    done.update({nm: adamw(nm, f) for nm, f in zip(names[:3], full_first)})
    full, big_out = [done[nm][0] for nm in names], [done[nm][1] for nm in names]

    vec, ws_sum = _sum_small(vec_all, ws_all, dg1_all)
    row = lambda r: vec[r:r + 1]
    shard = lambda a, width: lax.dynamic_slice_in_dim(a, chip * width, width, axis=1)
    g_small = dict(
        norm_mix_g=row(ROW_G1), conv_w=shard(vec[ROW_CW:ROW_CW + CONV_WIDTH], quarter_d), conv_b=row(ROW_CB),
        b_rgate=shard(row(ROW_BR).reshape(HEADS, HEAD_DIM), quarter_h),
        b_igate=shard(row(ROW_BI).reshape(HEADS, HEAD_DIM), quarter_h), lru_lambda=row(ROW_LAM),
        sgu_ln_g=row(ROW_LG), sgu_ln_b=row(ROW_LB),
        sgu_w_s=ws_sum.reshape(CHUNK, GROUPS, CHUNK).transpose(1, 0, 2).reshape(GROUPS * CHUNK, CHUNK),
        sgu_b_s=vec[ROW_BS:ROW_BS + GROUPS, 0:CHUNK], norm_mlp_g=row(ROW_G2), norm_final_g=row(ROW_G3))
    loss = vec[ROW_LOSS, 0]
    small_names = list(g_small)
    given = dict(
        norm_mix_g=(norm_mix_g, m_norm_mix_g, v_norm_mix_g), conv_w=(conv_w, m_conv_w, v_conv_w),
        conv_b=(conv_b, m_conv_b, v_conv_b), b_rgate=(b_rgate, m_b_rgate, v_b_rgate),
        b_igate=(b_igate, m_b_igate, v_b_igate), lru_lambda=(lru_lambda, m_lru_lambda, v_lru_lambda),
        sgu_ln_g=(sgu_ln_g, m_sgu_ln_g, v_sgu_ln_g), sgu_ln_b=(sgu_ln_b, m_sgu_ln_b, v_sgu_ln_b),
        sgu_w_s=(sgu_w_s, m_sgu_w_s, v_sgu_w_s), sgu_b_s=(sgu_b_s, m_sgu_b_s, v_sgu_b_s),
        norm_mlp_g=(norm_mlp_g, m_norm_mlp_g, v_norm_mlp_g), norm_final_g=(norm_final_g, m_norm_final_g, v_norm_final_g))
    g2d = [g_small[nm] for nm in small_names]
    to2d = lambda a, g: a.reshape(g.shape)
    d_s, m_s, v_s = _adamw_small(
        g2d, *[[to2d(given[nm][q], g) for nm, g in zip(small_names, g2d)] for q in range(3)])

    shapes = dict(
        norm_mix_g=norm_mix_g, w_in=w_in, conv_w=conv_w, conv_b=conv_b, w_rgate=w_rgate, b_rgate=b_rgate,
        w_igate=w_igate, b_igate=b_igate, lru_lambda=lru_lambda, w_out_a=w_out_a, sgu_ln_g=sgu_ln_g,
        sgu_ln_b=sgu_ln_b, sgu_w_s=sgu_w_s, sgu_b_s=sgu_b_s, w_out_b=w_out_b, w_out=w_out, norm_mlp_g=norm_mlp_g,
        w_up=w_up, w_down=w_down, norm_final_g=norm_final_g)
    grads, deltas, new_m, new_v = {}, {}, {}, {}
    for nm, g, (d, nmom, nvar) in zip(names, full, big_out):
        grads[nm], deltas[nm], new_m[nm], new_v[nm] = g, d, nmom, nvar
    for p, nm in enumerate(small_names):
        grads[nm], deltas[nm], new_m[nm], new_v[nm] = g2d[p], d_s[p], m_s[p], v_s[p]
    order = list(shapes)
    out = [loss, grad_x[None]]
    for group in (grads, deltas, new_m, new_v):
        out += [group[nm].reshape(shapes[nm].shape) for nm in order]
    return tuple(out)
```

```python
import functools

import jax
import jax.numpy as jnp
from jax import lax
from jax.experimental import pallas as pl
from jax.experimental.pallas import tpu as pltpu
from jax.experimental.pallas import tpu_sc as plsc

F32 = jnp.float32
BF16 = jnp.bfloat16
MESH = pl.DeviceIdType.MESH

D_MODEL = 1024
D_IN = 6 * D_MODEL
D_FF = 4 * D_MODEL
N_CHIPS = 4
IN_SHARD = D_IN // N_CHIPS
HEADS = 4
HEAD_DIM = D_MODEL // HEADS
GROUPS = 4
GROUP_DIM = D_MODEL // GROUPS
CHUNK = 128
CONV_WIDTH = 4
LRU_C = 8.0
NORM_EPS = 1e-6
LN_EPS = 1e-5

ADAM_LR = 0.001
ADAM_B1 = 0.9
ADAM_B2 = 0.999
ADAM_EPS = 1e-08
ADAM_WD = 0.01
ADAM_STEP = 10

SUBLANES = 8
MM_TILE = 512
IN_TILE = 1024
SEQ_TILE = 256
DW_TILE = 2048
VMEM_LIMIT_BYTES = 56 * 1024 * 1024

GELU_K0 = 0.7978845608028654
GELU_K1 = 0.044715


def _params(n_grid_axes=1):
    return pltpu.CompilerParams(
        dimension_semantics=("arbitrary",) * n_grid_axes, vmem_limit_bytes=VMEM_LIMIT_BYTES)


def _resident(shape):
    nd = len(shape)
    return pl.BlockSpec(shape, lambda *_: (0,) * nd, pipeline_mode=pl.Buffered(1))


def _const(shape):
    nd = len(shape)
    return pl.BlockSpec(shape, lambda *_: (0,) * nd)


def _dot(a, b):
    return jnp.dot(a, b, preferred_element_type=F32)


def _dot_nt(a, b):
    return lax.dot_general(a, b, (((1,), (1,)), ((), ())), preferred_element_type=F32)


def _dot_tn(a, b):
    return lax.dot_general(a, b, (((0,), (0,)), ((), ())), preferred_element_type=F32)


def _gelu(x):
    t = jnp.tanh(GELU_K0 * x * (1.0 + GELU_K1 * x * x))
    return 0.5 * x * (1.0 + t)


def _gelu_and_grad(x):
    x2 = x * x
    t = jnp.tanh(GELU_K0 * x * (1.0 + GELU_K1 * x2))
    g = 0.5 * x * (1.0 + t)
    dg = 0.5 * (1.0 + t) + 0.5 * x * (1.0 - t * t) * (GELU_K0 * (1.0 + 3.0 * GELU_K1 * x2))
    return g, dg


def _rms(x):
    r = lax.rsqrt(jnp.mean(x * x, axis=-1, keepdims=True) + NORM_EPS)
    return x * r, r


def _rms_bwd(dn, xhat, r):
    return r * (dn - xhat * jnp.mean(dn * xhat, axis=-1, keepdims=True))


def _col_sum(v):
    return jnp.sum(v, axis=0, keepdims=True)


def _shift_down(x, tail8, k):
    xs = pltpu.roll(x, k, 0)
    ts = pltpu.roll(tail8, k, 0)
    ridx = lax.broadcasted_iota(jnp.int32, tail8.shape, 0)
    head = jnp.where(ridx < k, ts, xs[0:SUBLANES])
    return jnp.concatenate([head, xs[SUBLANES:]], axis=0)


def _shift_up(x, head8, k):
    n = x.shape[0]
    xs = pltpu.roll(x, n - k, 0)
    hs = pltpu.roll(head8, SUBLANES - k, 0)
    ridx = lax.broadcasted_iota(jnp.int32, head8.shape, 0)
    last = jnp.where(ridx >= SUBLANES - k, hs, xs[n - SUBLANES:n])
    return jnp.concatenate([xs[:n - SUBLANES], last], axis=0)


def _scan_forward(a, b, carry):
    n, cols = a.shape
    groups = n // SUBLANES
    a = a.reshape(groups, SUBLANES, cols)
    b = b.reshape(groups, SUBLANES, cols)
    sub = lax.broadcasted_iota(jnp.int32, a.shape, 1)
    for s in (1, 2, 4):
        a_s = pltpu.roll(a, s, 1)
        b_s = pltpu.roll(b, s, 1)
        m = sub >= s
        b = jnp.where(m, a * b_s + b, b)
        a = jnp.where(m, a * a_s, a)
    out = []
    for g in range(groups):
        h = a[g] * carry + b[g]
        out.append(h)
        carry = h[SUBLANES - 1:SUBLANES]
    return jnp.concatenate(out, axis=0), carry


def _scan_backward(a, b, carry):
    n, cols = a.shape
    groups = n // SUBLANES
    a = a.reshape(groups, SUBLANES, cols)
    b = b.reshape(groups, SUBLANES, cols)
    sub = lax.broadcasted_iota(jnp.int32, a.shape, 1)
    for s in (1, 2, 4):
        a_s = pltpu.roll(a, SUBLANES - s, 1)
        b_s = pltpu.roll(b, SUBLANES - s, 1)
        m = sub < SUBLANES - s
        b = jnp.where(m, a * b_s + b, b)
        a = jnp.where(m, a * a_s, a)
    out = [None] * groups
    for g in reversed(range(groups)):
        h = a[g] * carry + b[g]
        out[g] = h
        carry = h[0:1]
    return jnp.concatenate(out, axis=0), carry


def _softplus_neg(lam):
    e = jnp.exp(-jnp.abs(lam))
    u = 1.0 + e
    log1p_e = jnp.where(u == 1.0, e, jnp.log(u) * (e / jnp.where(u == 1.0, 1.0, u - 1.0)))
    return jnp.maximum(-lam, 0.0) + log1p_e


def _lru_gates(xa, tail8, cw_ref, cb_ref, wr_ref, br_ref, wi_ref, bi_ref, lam_ref):
    cw = cw_ref[...]
    xc = cb_ref[...] + cw[0:1] * xa
    for k in range(1, CONV_WIDTH):
        xc = xc + cw[k:k + 1] * _shift_down(xa, tail8, k)
    xcb = xc.astype(BF16)
    pre_r, pre_i = [], []
    for h in range(HEADS):
        cols = slice(h * HEAD_DIM, (h + 1) * HEAD_DIM)
        pre_r.append(_dot(xcb[:, cols], wr_ref[h]))
        pre_i.append(_dot(xcb[:, cols], wi_ref[h]))
    r = jax.nn.sigmoid(jnp.concatenate(pre_r, axis=1) + br_ref[...])
    ig = jax.nn.sigmoid(jnp.concatenate(pre_i, axis=1) + bi_ref[...])
    _, a, mult = _decay(r, lam_ref)
    return xc, r, ig, a, mult


def _decay(r, lam_ref):
    sp = _softplus_neg(lam_ref[...])
    log_a = ((-LRU_C) * sp) * r
    a = jnp.exp(log_a)
    th = jnp.tanh(log_a)
    return sp, a, jnp.sqrt((-2.0 * th) / (1.0 - th))


class _Phase:
    def __init__(self, copies, n_sem, n_local, start=None, finish=None):
        self.copies, self.n_sem, self.n_local, self.start, self.finish = copies, n_sem, n_local, start, finish


class _Job:
    def __init__(self, inputs, out_shape, n_sem, copies, peers, aliases=None, n_local=0):
        self.inputs, self.out_shape = list(inputs), list(out_shape)
        self.aliases = dict(aliases or {})
        self.phases = [_Phase(copies, n_sem, n_local)]
        self.peers = tuple(peers)

    def then(self, make, at=None):
        nxt = make(self.out_shape)
        self.phases[-1].finish = at
        nxt.phases[0].start = at
        self.phases += nxt.phases
        self.peers = tuple(sorted(set(self.peers + nxt.peers)))
        return self


def _fused_call(body, jobs, *, name, grid, in_specs, out_specs, out_shape, scratch_shapes=(),
                input_output_aliases=None, compiler_params=None, n_prefetch=0, jobs_start_after=None):
    single = not isinstance(out_shape, (list, tuple))
    out_specs = [out_specs] if single else list(out_specs)
    out_shape = [out_shape] if single else list(out_shape)
    n_scr = len(scratch_shapes)
    in_specs, scratch_shapes = list(in_specs), list(scratch_shapes)
    n_in, n_out = len(in_specs), len(out_shape)
    aliases = dict(input_output_aliases or {})
    in_at, out_at, phases = [], [], []
    for q, job in enumerate(jobs):
        in_at.append(len(in_specs))
        out_at.append(len(out_shape))
        for i, o in job.aliases.items():
            aliases[n_prefetch + len(in_specs) + i] = len(out_shape) + o
        in_specs += [ANY] * len(job.inputs)
        out_specs += [ANY] * len(job.out_shape)
        out_shape += job.out_shape
        for k, phase in enumerate(job.phases):
            phases.append((q, k, phase, len(scratch_shapes)))
            scratch_shapes += [pltpu.SemaphoreType.DMA((phase.n_sem,)), pltpu.SemaphoreType.DMA((phase.n_sem,)),
                               pltpu.SemaphoreType.DMA((max(phase.n_local, 1),))]
    n_in_all, n_out_all = len(in_specs), len(out_shape)
    first_step, last_step = (0,) * len(grid), tuple(g - 1 for g in grid)

    def full_body(*refs):
        prefetch, refs = refs[:n_prefetch], refs[n_prefetch:]
        ins, outs, scr = refs[:n_in_all], refs[n_in_all:n_in_all + n_out_all], refs[n_in_all + n_out_all:]
        ids = [pl.program_id(a) for a in range(len(grid))]
        at_step = lambda step: functools.reduce(jnp.logical_and, [i == k for i, k in zip(ids, step)])

        def copies(q, k, phase, sem_at):
            job = jobs[q]
            mine = outs[out_at[q]:out_at[q] + len(job.out_shape)]
            return phase.copies(ins[in_at[q]:in_at[q] + len(job.inputs)] if k == 0 else mine, mine,
                                *scr[sem_at:sem_at + 3])

        def start(*phase):
            def go():
                sends, _, local = copies(*phase)
                for cp in local + sends:
                    cp.start()
            return go

        def finish(*phase):
            def go():
                sends, arrivals, local = copies(*phase)
                for cp in arrivals:
                    cp.wait_recv()
                for cp in sends:
                    cp.wait_send()
                for cp in local:
                    cp.wait()
            return go

        for phase in phases:
            if phase[2].start is None and jobs_start_after is None:
                pl.when(at_step(first_step))(start(*phase))
        body(*prefetch, *ins[:n_in], *outs[:n_out], *scr[:n_scr])
        for phase in phases:
            pl.when(at_step(phase[2].finish or last_step))(finish(*phase))
            nxt = phase[2].start or jobs_start_after
            if nxt is not None:
                pl.when(at_step(nxt))(start(*phase))

    if n_prefetch:
        layout = dict(grid_spec=pltpu.PrefetchScalarGridSpec(
            num_scalar_prefetch=n_prefetch, grid=grid, in_specs=in_specs, out_specs=out_specs,
            scratch_shapes=scratch_shapes))
    else:
        layout = dict(grid=grid, in_specs=in_specs, out_specs=out_specs, scratch_shapes=scratch_shapes)
    call = pl.pallas_call(
        full_body, name=name, out_shape=out_shape, input_output_aliases=aliases, compiler_params=compiler_params,
        **layout)

    def run(*args):
        res = call(*args, *[a for job in jobs for a in job.inputs])
        mine = res[0] if single else list(res[:n_out])
        return mine, [list(res[at:at + len(job.out_shape)]) for at, job in zip(out_at, jobs)]

    return run


def _fwd_in(x, g1, shards, order, jobs=()):
    t = x.shape[0]
    rows_per_step = min(IN_TILE, t)
    n_tiles = t // rows_per_step
    n = len(shards)
    halves = [s.shape[0] // 2 for s in shards]

    def body(order_ref, x_ref, g_ref, *refs):
        del order_ref
        ins, (z_ref, n_ref), outs = refs[:n], refs[n:n + 2], refs[n + 2:2 * n + 2]
        wbuf, nbuf, send, recv, local = refs[2 * n + 2:]
        s, i = pl.program_id(0), pl.program_id(1)
        x_, y_, c, chips = _place()
        k_me = _chip_index(x_, y_)

        def block(w, chip, pc):
            return outs[w].at[_chip_index(*chip), pl.ds(pc * halves[w], halves[w]), :]

        def over_ici(w, j, landing):
            return pltpu.make_async_remote_copy(
                src_ref=ins[w].at[pl.ds(c * halves[w], halves[w]), :],
                dst_ref=block(w, chips[j] if landing else (x_, y_), c), send_sem=send.at[6 * w + j],
                recv_sem=recv.at[6 * w + j], device_id=(*chips[j], c), device_id_type=MESH)

        def to_sibling(w, j, landing):
            blk = block(w, chips[j], 1 - c if landing else c)
            return pltpu.make_async_remote_copy(
                src_ref=blk, dst_ref=blk, send_sem=send.at[6 * w + 3 + j], recv_sem=recv.at[6 * w + 3 + j],
                device_id=(x_, y_, 1 - c), device_id_type=MESH)

        own = [pltpu.make_async_copy(wbuf, outs[0].at[k_me], local.at[0])]
        own += [pltpu.make_async_copy(ins[w], outs[w].at[k_me], local.at[w]) for w in range(1, n)]

        @pl.when((s == 0) & (i == 0))
        def _():
            for j in range(2):
                for w in range(n):
                    over_ici(w, j, False).start()
            load = pltpu.make_async_copy(ins[0], wbuf, local.at[n])
            load.start()
            load.wait()
            for cp in own:
                cp.start()

        for j in range(N_CHIPS - 1):
            @pl.when((s == j + 1) & (i == 0))
            def _(j=j):
                for w in range(n):
                    over_ici(w, j, True).wait_recv()
                for w in range(n):
                    to_sibling(w, j, False).start()
                if j == 0:
                    for w in range(n):
                        over_ici(w, 2, False).start()
                    own[0].wait()
                for w in range(n):
                    to_sibling(w, j, True).wait_recv()
                load = pltpu.make_async_copy(outs[0].at[_chip_index(*chips[j])], wbuf, local.at[n])
                load.start()
                load.wait()

        rows = pl.ds(pl.multiple_of(i * rows_per_step, rows_per_step), rows_per_step)

        @pl.when(s == 0)
        def _():
            xhat, _ = _rms(x_ref[...])
            nrm = (xhat * g_ref[...]).astype(BF16)
            nbuf[rows, :] = nrm
            n_ref[...] = nrm

        z_ref[...] = _dot(nbuf[rows, :], wbuf[...])

        @pl.when((s == N_CHIPS - 1) & (i == n_tiles - 1))
        def _():
            for j in range(N_CHIPS - 1):
                for w in range(n):
                    over_ici(w, j, False).wait_send()
                    to_sibling(w, j, False).wait_send()
            for cp in own[1:]:
                cp.wait()

    once = lambda s, i, order: (jnp.where(s == 0, i, n_tiles - 1), 0)
    (z, n1, *stacked), job_outs = _fused_call(
        body, jobs, name="fwd_in", grid=(N_CHIPS, n_tiles), n_prefetch=1,
        in_specs=[pl.BlockSpec((rows_per_step, D_MODEL), once), _const((1, D_MODEL))] + [ANY] * n,
        out_specs=[pl.BlockSpec((rows_per_step, IN_SHARD), lambda s, i, order: (i, order[s])),
                   pl.BlockSpec((rows_per_step, D_MODEL), once)] + [ANY] * n,
        out_shape=[jax.ShapeDtypeStruct((t, D_IN), F32), jax.ShapeDtypeStruct((t, D_MODEL), BF16)]
        + [jax.ShapeDtypeStruct((N_CHIPS,) + s.shape, s.dtype) for s in shards],
        scratch_shapes=[pltpu.VMEM(shards[0].shape, BF16), pltpu.VMEM((t, D_MODEL), BF16),
                        pltpu.SemaphoreType.DMA((6 * n,)),
                        pltpu.SemaphoreType.DMA((6 * n,)), pltpu.SemaphoreType.DMA((n + 1,))],
        compiler_params=_params(2), jobs_start_after=(1, 0),
    )(order, x, g1, *shards)
    return (z, n1, stacked), job_outs


def _fwd_mix(z, x, conv_w, conv_b, wr, br, wi, bi, lam, ln_g, ln_b, w_s, bias_full, w_oa, w_ob, w_out, g2, jobs=()):
    t = z.shape[0]

    def lru_part(xa_ref, ga_ref, cw_ref, cb_ref, wr_ref, br_ref, wi_ref, bi_ref, lam_ref, ya_ref, h_ref, xc_ref, r_ref,
                 ig_ref, tail_ref, carry_ref):
        @pl.when(pl.program_id(0) == 0)
        def _():
            tail_ref[...] = jnp.zeros_like(tail_ref)
            carry_ref[...] = jnp.zeros_like(carry_ref)

        xa = xa_ref[...]
        xc, r, ig, a, mult = _lru_gates(xa, tail_ref[...], cw_ref, cb_ref, wr_ref, br_ref, wi_ref, bi_ref, lam_ref)
        tail_ref[...] = xa[SEQ_TILE - SUBLANES:]
        xc_ref[...], r_ref[...], ig_ref[...] = xc, r, ig
        h, carry = _scan_forward(a, xc * ig * mult, carry_ref[...])
        carry_ref[...] = carry
        h_ref[...] = h
        ya_ref[...] = (h * _gelu(ga_ref[...])).astype(BF16)

    def sgu_merge_part(ya_ref, ub_ref, vb_ref, m_ref, x_ref, lg_ref, lb_ref, ws_ref, bias_ref, woa_ref, wob_ref,
                       wout_ref, g_ref, yb_ref, pa_ref, pb_ref, h1_ref, n2_ref):
        u, _, _, _, _, vn = _sgu_forward_parts(ub_ref[...], vb_ref[...], lg_ref, lb_ref)
        mask = _causal_mask()
        wm = [jnp.where(mask, ws_ref[g], 0.0).astype(BF16) for g in range(GROUPS)]
        for c in range(SEQ_TILE // CHUNK):
            rows = slice(c * CHUNK, (c + 1) * CHUNK)
            for g in range(GROUPS):
                cols = slice(g * GROUP_DIM, (g + 1) * GROUP_DIM)
                sp = _dot(wm[g], vn[rows, cols]) + bias_ref[:, cols]
                yb_ref[rows, cols] = (u[rows, cols] * sp).astype(BF16)
        pa = _dot(ya_ref[...], woa_ref[...])
        pb = _dot(yb_ref[...], wob_ref[...])
        pa_ref[...] = pa
        pb_ref[...] = pb
        merged = jax.nn.sigmoid(m_ref[:, :D_MODEL]) * pa + jax.nn.sigmoid(m_ref[:, D_MODEL:]) * pb
        h1 = x_ref[...] + _dot(merged.astype(BF16), wout_ref[...])
        h1_ref[...] = h1
        xhat, _ = _rms(h1)
        n2_ref[...] = (xhat * g_ref[...]).astype(BF16)

    def body(z_ref, x_ref, cw_ref, cb_ref, wr_ref, br_ref, wi_ref, bi_ref, lam_ref, lg_ref, lb_ref, ws_ref, bias_ref,
             woa_ref, wob_ref, wout_ref, g_ref, ya_ref, h_ref, xc_ref, r_ref, ig_ref, yb_ref, pa_ref, pb_ref, h1_ref,
             n2_ref, tail_ref, carry_ref):
        def cols(first, count):
            return z_ref.at[:, pl.ds(first * D_MODEL, count * D_MODEL)]

        lru_part(cols(0, 1), cols(1, 1), cw_ref, cb_ref, wr_ref, br_ref, wi_ref, bi_ref, lam_ref, ya_ref, h_ref, xc_ref,
                 r_ref, ig_ref, tail_ref, carry_ref)
        sgu_merge_part(ya_ref, cols(2, 1), cols(3, 1), cols(4, 2), x_ref, lg_ref, lb_ref, ws_ref, bias_ref, woa_ref,
                       wob_ref, wout_ref, g_ref, yb_ref, pa_ref, pb_ref, h1_ref, n2_ref)

    tile = pl.BlockSpec((SEQ_TILE, D_MODEL), lambda i: (i, 0))
    vec = _const((1, D_MODEL))
    sq = _resident((D_MODEL, D_MODEL))
    gate_w = _resident((HEADS, HEAD_DIM, HEAD_DIM))
    bf, f32 = jax.ShapeDtypeStruct((t, D_MODEL), BF16), jax.ShapeDtypeStruct((t, D_MODEL), F32)
    return _fused_call(
        body, jobs, name="fwd_mix", grid=(t // SEQ_TILE,),
        in_specs=[pl.BlockSpec((SEQ_TILE, D_IN), lambda i: (i, 0)), tile, _const((CONV_WIDTH, D_MODEL)), vec, gate_w, vec,
                  gate_w, vec, vec, vec, vec, _const((GROUPS, CHUNK, CHUNK)), _const((CHUNK, D_MODEL)), sq, sq, sq, vec],
        out_specs=[tile] * 10,
        out_shape=[bf, f32, f32, f32, f32, bf, f32, f32, f32, bf],
        scratch_shapes=[pltpu.VMEM((SUBLANES, D_MODEL), F32), pltpu.VMEM((1, D_MODEL), F32)],
        compiler_params=_params(),
    )(z, x, conv_w, conv_b, wr, br, wi, bi, lam, ln_g, ln_b, w_s, bias_full, w_oa, w_ob, w_out, g2)


def _sgu_forward_parts(ub, vb, lg_ref, lb_ref):
    u, du = _gelu_and_grad(ub)
    vg, dvg = _gelu_and_grad(vb)
    mu = jnp.mean(vg, axis=-1, keepdims=True)
    d = vg - mu
    rstd = lax.rsqrt(jnp.mean(d * d, axis=-1, keepdims=True) + LN_EPS)
    vhat = d * rstd
    vn = (vhat * lg_ref[...] + lb_ref[...]).astype(BF16)
    return u, du, dvg, rstd, vhat, vn


def _causal_mask():
    rows = lax.broadcasted_iota(jnp.int32, (CHUNK, CHUNK), 0)
    cols = lax.broadcasted_iota(jnp.int32, (CHUNK, CHUNK), 1)
    return rows >= cols


def _mlp(n2, h1, target, w_up_st, w_down, g2, g3, jobs=()):
    t = n2.shape[0]

    def body(n2_ref, h1_ref, tgt_ref, wup_ref, wdown_ref, g2_ref, g3_ref, act_ref, dup_ref, dh2b_ref, dh1_ref,
             loss_ref, dg3_ref, dg2_ref, relu_ref):
        @pl.when(pl.program_id(0) == 0)
        def _():
            for ref in (loss_ref, dg3_ref, dg2_ref):
                ref[...] = jnp.zeros_like(ref)

        n2 = n2_ref[...]
        h1 = h1_ref[...]
        h2 = h1
        for k in range(N_CHIPS):
            cols = slice(k * D_MODEL, (k + 1) * D_MODEL)
            r = jnp.maximum(_dot(n2, wup_ref[k]), 0.0)
            relu_ref[:, cols] = r
            act = (r * r).astype(BF16)
            act_ref[:, cols] = act
            h2 = h2 + _dot(act, wdown_ref[cols, :])
        xhat, r3 = _rms(h2)
        diff = xhat * g3_ref[...] - tgt_ref[...]
        sq = jnp.sum(diff * diff, axis=1, keepdims=True)
        loss_ref[...] = loss_ref[...] + (0.5 / D_MODEL) * jnp.sum(sq, axis=0, keepdims=True)
        dy = diff * (1.0 / D_MODEL)
        dg3_ref[...] = dg3_ref[...] + _col_sum(dy * xhat)
        dh2 = _rms_bwd(dy * g3_ref[...], xhat, r3)
        dh2b = dh2.astype(BF16)
        dh2b_ref[...] = dh2b
        dn2 = jnp.zeros((SEQ_TILE, D_MODEL), F32)
        for k in range(N_CHIPS):
            cols = slice(k * D_MODEL, (k + 1) * D_MODEL)
            dup = (_dot_nt(dh2b, wdown_ref[cols, :]) * (2.0 * relu_ref[:, cols])).astype(BF16)
            dup_ref[:, cols] = dup
            dn2 = dn2 + _dot_nt(dup, wup_ref[k])
        xhat, r2 = _rms(h1)
        dg2_ref[...] = dg2_ref[...] + _col_sum(dn2 * xhat)
        dh1_ref[...] = dh2 + _rms_bwd(dn2 * g2_ref[...], xhat, r2)

    tile = pl.BlockSpec((SEQ_TILE, D_MODEL), lambda i: (i, 0))
    wide = pl.BlockSpec((SEQ_TILE, D_FF), lambda i: (i, 0))
    vec = _const((1, D_MODEL))
    vec_shape = jax.ShapeDtypeStruct((1, D_MODEL), F32)
    return _fused_call(
        body, jobs, name="mlp", grid=(t // SEQ_TILE,),
        in_specs=[tile, tile, tile, _resident((N_CHIPS, D_MODEL, D_MODEL)), _resident((D_FF, D_MODEL)), vec, vec],
        out_specs=[wide, wide, tile, tile, _const((SUBLANES, 128)), vec, vec],
        out_shape=[jax.ShapeDtypeStruct((t, D_FF), BF16), jax.ShapeDtypeStruct((t, D_FF), BF16),
                   jax.ShapeDtypeStruct((t, D_MODEL), BF16), jax.ShapeDtypeStruct((t, D_MODEL), F32),
                   jax.ShapeDtypeStruct((SUBLANES, 128), F32), vec_shape, vec_shape],
        scratch_shapes=[pltpu.VMEM((SEQ_TILE, D_FF), F32)],
        compiler_params=_params(),
    )(n2, h1, target, w_up_st, w_down, g2, g3)


def _bwd_mix(dh1, pa, pb, z, h, xc, r, ig, w_oa, w_ob, w_out, ln_g, ln_b, w_s, bias_full, conv_w, wr, wi, lam, jobs=()):
    t = dh1.shape[0]
    n_tiles = t // SEQ_TILE
    per_tile = SEQ_TILE // SUBLANES

    def merge_part(dh1_ref, pa_ref, pb_ref, m_ref, woa_ref, wob_ref, wout_ref, dz_ref, dya_ref, dyb_ref, mg_ref,
                   dpa_ref, dpb_ref, dh1b_ref):
        dh1b = dh1_ref[...].astype(BF16)
        dh1b_ref[...] = dh1b
        dm = _dot_nt(dh1b, wout_ref[...])
        pa = pa_ref[...]
        pb = pb_ref[...]
        sa = jax.nn.sigmoid(m_ref[:, :D_MODEL])
        sb = jax.nn.sigmoid(m_ref[:, D_MODEL:])
        mg_ref[...] = (sa * pa + sb * pb).astype(BF16)
        dz_ref[:, :D_MODEL] = (dm * pa * sa * (1.0 - sa)).astype(BF16)
        dz_ref[:, D_MODEL:] = (dm * pb * sb * (1.0 - sb)).astype(BF16)
        dpa = (dm * sa).astype(BF16)
        dpb = (dm * sb).astype(BF16)
        dpa_ref[...] = dpa
        dpb_ref[...] = dpb
        dya_ref[...] = _dot_nt(dpa, woa_ref[...])
        dyb_ref[...] = _dot_nt(dpb, wob_ref[...])

    def sgu_part(dyb_ref, ub_ref, vb_ref, lg_ref, lb_ref, ws_ref, bias_ref, dz_ref, dlg_ref, dlb_ref, dws_ref, dbs_ref,
                 dvn_ref, dsp_acc):
        i = pl.program_id(0)

        @pl.when(i == 0)
        def _():
            dlg_ref[...] = jnp.zeros_like(dlg_ref)
            dlb_ref[...] = jnp.zeros_like(dlb_ref)
            dws_ref[...] = jnp.zeros_like(dws_ref)
            dsp_acc[...] = jnp.zeros_like(dsp_acc)

        u, du, dvg, rstd, vhat, vn = _sgu_forward_parts(ub_ref[...], vb_ref[...], lg_ref, lb_ref)
        dyb = dyb_ref[...]
        mask = _causal_mask()
        wm = [jnp.where(mask, ws_ref[g], 0.0).astype(BF16) for g in range(GROUPS)]
        for c in range(SEQ_TILE // CHUNK):
            rows = slice(c * CHUNK, (c + 1) * CHUNK)
            for g in range(GROUPS):
                cols = slice(g * GROUP_DIM, (g + 1) * GROUP_DIM)
                vn_blk = vn[rows, cols]
                sp = _dot(wm[g], vn_blk) + bias_ref[:, cols]
                dyb_blk = dyb[rows, cols]
                dz_ref[rows, cols] = (dyb_blk * sp * du[rows, cols]).astype(BF16)
                dsp = dyb_blk * u[rows, cols]
                dsp_acc[:, cols] = dsp_acc[:, cols] + dsp
                dspb = dsp.astype(BF16)
                dvn_ref[rows, cols] = _dot_tn(wm[g], dspb)
                wcols = slice(g * CHUNK, (g + 1) * CHUNK)
                dws_ref[:, wcols] = dws_ref[:, wcols] + jnp.where(mask, _dot_nt(dspb, vn_blk), 0.0)
        dvn = dvn_ref[...]
        dlg_ref[...] = dlg_ref[...] + _col_sum(dvn * vhat)
        dlb_ref[...] = dlb_ref[...] + _col_sum(dvn)
        dvhat = dvn * lg_ref[...]
        dvgel = rstd * (dvhat - jnp.mean(dvhat, axis=-1, keepdims=True)
                        - vhat * jnp.mean(dvhat * vhat, axis=-1, keepdims=True))
        dz_ref[:, D_MODEL:] = (dvgel * dvg).astype(BF16)

        @pl.when(i == n_tiles - 1)
        def _():
            lane = lax.broadcasted_iota(jnp.int32, (CHUNK, 128), 1)
            out = jnp.zeros((CHUNK, 128), F32)
            for g in range(GROUPS):
                s = jnp.sum(dsp_acc[:, g * GROUP_DIM:(g + 1) * GROUP_DIM], axis=1, keepdims=True)
                out = out + jnp.where(lane == g, s, 0.0)
            dbs_ref[...] = out

    def lru_part(dya_ref, xa_ref, ga_ref, h_ref, h_prev_ref, xc_ref, r_ref, ig_ref, cw_ref, wr_ref, wi_ref, lam_ref,
                 dz_ref, dcw_ref, dcb_ref, dwr_ref, dbr_ref, dwi_ref, dbi_ref, dlam_ref, lam_carry, dxc_head):
        i = pl.program_id(0)

        @pl.when(i == 0)
        def _():
            for ref in (dcw_ref, dcb_ref, dwr_ref, dbr_ref, dwi_ref, dbi_ref, dlam_ref, lam_carry, dxc_head):
                ref[...] = jnp.zeros_like(ref)

        first_tile = i == n_tiles - 1
        h_tail = jnp.where(first_tile, 0.0, h_prev_ref[...])
        xc, r, ig = xc_ref[...], r_ref[...], ig_ref[...]
        xcb = xc.astype(BF16)
        sp, a, mult = _decay(r, lam_ref)
        h = h_ref[...]
        h_prev = _shift_down(h, h_tail, 1)
        dya = dya_ref[...]
        gg, dgg = _gelu_and_grad(ga_ref[...])
        dz_ref[:, D_MODEL:] = (dya * h * dgg).astype(BF16)
        ones = jnp.ones((SUBLANES, D_MODEL), F32)
        lam_t, lam_first = _scan_backward(_shift_up(a, ones, 1), dya * gg, lam_carry[...])
        lam_carry[...] = a[0:1] * lam_first
        dmult = lam_t * xc * ig
        dla = lam_t * h_prev * a - dmult * (a * a) / mult
        dr = dla * ((-LRU_C) * sp)
        dlam_ref[...] = dlam_ref[...] + _col_sum(dla * r) * (LRU_C * jax.nn.sigmoid(-lam_ref[...]))
        dpr = dr * r * (1.0 - r)
        dpi = lam_t * xc * mult * ig * (1.0 - ig)
        dbr_ref[...] = dbr_ref[...] + _col_sum(dpr)
        dbi_ref[...] = dbi_ref[...] + _col_sum(dpi)
        dprb = dpr.astype(BF16)
        dpib = dpi.astype(BF16)
        dxc_gate = []
        for hd in range(HEADS):
            cols = slice(hd * HEAD_DIM, (hd + 1) * HEAD_DIM)
            dxc_gate.append(_dot_nt(dprb[:, cols], wr_ref[hd]) + _dot_nt(dpib[:, cols], wi_ref[hd]))
            dwr_ref[hd] = dwr_ref[hd] + _dot_tn(xcb[:, cols], dprb[:, cols])
            dwi_ref[hd] = dwi_ref[hd] + _dot_tn(xcb[:, cols], dpib[:, cols])
        dxc = lam_t * ig * mult + jnp.concatenate(dxc_gate, axis=1)
        dcb_ref[...] = dcb_ref[...] + _col_sum(dxc)
        cw = cw_ref[...]
        head = dxc_head[...]
        xa = xa_ref[...]
        dxa = cw[0:1] * dxc
        dcw_ref[0:1, :] = dcw_ref[0:1, :] + _col_sum(dxc * xa)
        for k in range(1, CONV_WIDTH):
            dxc_k = _shift_up(dxc, head, k)
            dxa = dxa + cw[k:k + 1] * dxc_k
            dcw_ref[k:k + 1, :] = dcw_ref[k:k + 1, :] + _col_sum(dxc_k * xa)
        dxc_head[...] = dxc[0:SUBLANES]
        dz_ref[:, :D_MODEL] = dxa.astype(BF16)

    def body(dh1_ref, pa_ref, pb_ref, z_ref, h_ref, h_prev_ref, xc_ref, r_ref, ig_ref, woa_ref, wob_ref, wout_ref,
             lg_ref, lb_ref, ws_ref, bias_ref, cw_ref, wr_ref, wi_ref, lam_ref, dz_ref, mg_ref, dpa_ref, dpb_ref,
             dh1b_ref, dlg_ref, dlb_ref, dws_ref, dbs_ref, dcw_ref, dcb_ref, dwr_ref, dbr_ref, dwi_ref, dbi_ref,
             dlam_ref, dya_ref, dyb_ref, dvn_ref, dsp_acc, lam_carry, dxc_head):
        def cols(ref, first, count):
            return ref.at[:, pl.ds(first * D_MODEL, count * D_MODEL)]

        merge_part(dh1_ref, pa_ref, pb_ref, cols(z_ref, 4, 2), woa_ref, wob_ref, wout_ref, cols(dz_ref, 4, 2), dya_ref,
                   dyb_ref, mg_ref, dpa_ref, dpb_ref, dh1b_ref)
        sgu_part(dyb_ref, cols(z_ref, 2, 1), cols(z_ref, 3, 1), lg_ref, lb_ref, ws_ref, bias_ref, cols(dz_ref, 2, 2),
                 dlg_ref, dlb_ref, dws_ref, dbs_ref, dvn_ref, dsp_acc)
        lru_part(dya_ref, cols(z_ref, 0, 1), cols(z_ref, 1, 1), h_ref, h_prev_ref, xc_ref, r_ref, ig_ref, cw_ref, wr_ref,
                 wi_ref, lam_ref, cols(dz_ref, 0, 2), dcw_ref, dcb_ref, dwr_ref, dbr_ref, dwi_ref, dbi_ref, dlam_ref,
                 lam_carry, dxc_head)

    rev = lambda i: n_tiles - 1 - i
    tile = pl.BlockSpec((SEQ_TILE, D_MODEL), lambda i: (rev(i), 0))
    row = pl.BlockSpec((SEQ_TILE, D_IN), lambda i: (rev(i), 0))
    prev8 = pl.BlockSpec((SUBLANES, D_MODEL), lambda i: (jnp.maximum(rev(i) * per_tile - 1, 0), 0))
    vec = _const((1, D_MODEL))
    sq = _resident((D_MODEL, D_MODEL))
    gate_w = _resident((HEADS, HEAD_DIM, HEAD_DIM))
    gate_acc = _const((HEADS, HEAD_DIM, HEAD_DIM))
    vec_shape = jax.ShapeDtypeStruct((1, D_MODEL), F32)
    gate_shape = jax.ShapeDtypeStruct((HEADS, HEAD_DIM, HEAD_DIM), F32)
    act_bf = jax.ShapeDtypeStruct((t, D_MODEL), BF16)
    return _fused_call(
        body, jobs, name="bwd_mix", grid=(n_tiles,),
        in_specs=[tile, tile, tile, row, tile, prev8, tile, tile, tile, sq, sq, sq, vec, vec,
                  _const((GROUPS, CHUNK, CHUNK)), _const((CHUNK, D_MODEL)), _const((CONV_WIDTH, D_MODEL)), gate_w, gate_w,
                  vec],
        out_specs=[row, tile, tile, tile, tile, vec, vec, _const((CHUNK, GROUPS * CHUNK)), _const((CHUNK, 128)),
                   _const((SUBLANES, D_MODEL)), vec, gate_acc, vec, gate_acc, vec, vec],
        out_shape=[jax.ShapeDtypeStruct((t, D_IN), BF16), act_bf, act_bf, act_bf, act_bf, vec_shape, vec_shape,
                   jax.ShapeDtypeStruct((CHUNK, GROUPS * CHUNK), F32), jax.ShapeDtypeStruct((CHUNK, 128), F32),
                   jax.ShapeDtypeStruct((SUBLANES, D_MODEL), F32), vec_shape, gate_shape, vec_shape, gate_shape,
                   vec_shape, vec_shape],
        scratch_shapes=[pltpu.VMEM((SEQ_TILE, D_MODEL), F32), pltpu.VMEM((SEQ_TILE, D_MODEL), F32),
                        pltpu.VMEM((SEQ_TILE, D_MODEL), F32), pltpu.VMEM((CHUNK, D_MODEL), F32),
                        pltpu.VMEM((1, D_MODEL), F32), pltpu.VMEM((SUBLANES, D_MODEL), F32)],
        compiler_params=_params(),
    )(dh1, pa, pb, z, h, h, xc, r, ig, w_oa, w_ob, w_out, ln_g, ln_b, w_s, bias_full, conv_w, wr, wi, lam)


def _bwd_in(dz, x, dh1, w_in_st, g1, jobs=()):
    t = x.shape[0]

    def body(dz_ref, x_ref, dh1_ref, w_ref, g_ref, dx_ref, dg1_ref):
        @pl.when(pl.program_id(0) == 0)
        def _():
            dg1_ref[...] = jnp.zeros_like(dg1_ref)

        dn1 = jnp.zeros((MM_TILE, D_MODEL), F32)
        for k in range(N_CHIPS):
            dn1 = dn1 + _dot_nt(dz_ref[:, k * IN_SHARD:(k + 1) * IN_SHARD], w_ref[k])
        xhat, r1 = _rms(x_ref[...])
        dg1_ref[...] = dg1_ref[...] + _col_sum(dn1 * xhat)
        dx_ref[...] = dh1_ref[...] + _rms_bwd(dn1 * g_ref[...], xhat, r1)

    tile = pl.BlockSpec((MM_TILE, D_MODEL), lambda i: (i, 0))
    return _fused_call(
        body, jobs, name="bwd_in", grid=(t // MM_TILE,),
        in_specs=[pl.BlockSpec((MM_TILE, D_IN), lambda i: (i, 0)), tile, tile,
                  _resident((N_CHIPS, D_MODEL, IN_SHARD)), _const((1, D_MODEL))],
        out_specs=[tile, _const((1, D_MODEL))],
        out_shape=[jax.ShapeDtypeStruct((t, D_MODEL), F32), jax.ShapeDtypeStruct((1, D_MODEL), F32)],
        compiler_params=_params(),
    )(dz, x, dh1, w_in_st, g1)


def _weight_grad(name, a, b, n_blocks, a_varies, b_varies, width, jobs=()):
    t = a.shape[0]
    rows = min(DW_TILE, t)
    n_t = t // rows

    def body(a_ref, b_ref, o_ref, acc_ref):
        s = pl.program_id(1)
        part = _dot_tn(a_ref[...], b_ref[...])

        @pl.when(s == 0)
        def _():
            acc_ref[...] = part

        @pl.when(s > 0)
        def _():
            acc_ref[...] = acc_ref[...] + part

        @pl.when(s == n_t - 1)
        def _():
            o_ref[...] = acc_ref[...].astype(BF16)

    return _fused_call(
        body, jobs, name=name, grid=(n_blocks, n_t),
        in_specs=[pl.BlockSpec((rows, D_MODEL), (lambda j, s: (s, j)) if a_varies else (lambda j, s: (s, 0))),
                  pl.BlockSpec((rows, width), (lambda j, s: (s, j)) if b_varies else (lambda j, s: (s, 0)))],
        out_specs=pl.BlockSpec((None, D_MODEL, width), lambda j, s: (j, 0, 0)),
        out_shape=jax.ShapeDtypeStruct((n_blocks, D_MODEL, width), BF16),
        scratch_shapes=[pltpu.VMEM((D_MODEL, width), F32)],
        compiler_params=_params(2),
    )(a, b)


def _weight_grads_square(name, pairs, jobs=()):
    n = len(pairs)
    t = pairs[0][0].shape[0]
    rows = min(2 * MM_TILE, t)
    n_t = t // rows

    def body(*refs):
        ins, outs, accs = refs[:2 * n], refs[2 * n:3 * n], refs[3 * n:]
        s = pl.program_id(0)
        for k in range(n):
            part = _dot_tn(ins[2 * k][...], ins[2 * k + 1][...])

            @pl.when(s == 0)
            def _(k=k, part=part):
                accs[k][...] = part

            @pl.when(s > 0)
            def _(k=k, part=part):
                accs[k][...] = accs[k][...] + part

            @pl.when(s == n_t - 1)
            def _(k=k):
                outs[k][...] = accs[k][...].astype(BF16)

    tile = pl.BlockSpec((rows, D_MODEL), lambda s: (s, 0))
    return _fused_call(
        body, jobs, name=name, grid=(n_t,), in_specs=[tile] * (2 * n), out_specs=[_const((D_MODEL, D_MODEL))] * n,
        out_shape=[jax.ShapeDtypeStruct((D_MODEL, D_MODEL), BF16)] * n,
        scratch_shapes=[pltpu.VMEM((D_MODEL, D_MODEL), F32)] * n,
        compiler_params=_params(),
    )(*[x for pair in pairs for x in pair])


def _place():
    x, y, c = lax.axis_index("x"), lax.axis_index("y"), lax.axis_index("c")
    other_chips = [(1 - x, y), (x, 1 - y), (1 - x, 1 - y)]
    return x, y, c, other_chips


def _chip_index(px, py):
    return 2 * px + py


ANY = pl.BlockSpec(memory_space=pl.ANY)
SIBLING = ((0, 0, 1),)
NEIGHBOURS = ((1, 0, 0), (0, 1, 0))
OTHER_CHIPS = NEIGHBOURS + ((1, 1, 0),)


def _near_far(x, y, c):
    return (x ^ (1 - c), y ^ c), (x ^ c, y ^ (1 - c))


def _gather_near_job(shards):
    n = len(shards)
    halves = [s.shape[0] // 2 for s in shards]

    def copies(ins, outs, send, recv, local):
        x, y, c, _ = _place()
        near, _ = _near_far(x, y, c)

        def block(w, chip, pc):
            return outs[w].at[_chip_index(*chip), pl.ds(pc * halves[w], halves[w]), :]

        def copy(w, k, chip, pc, to, src=None):
            return pltpu.make_async_remote_copy(
                src_ref=block(w, chip, pc) if src is None else src, dst_ref=block(w, chip, pc),
                send_sem=send.at[2 * w + k], recv_sem=recv.at[2 * w + k], device_id=to, device_id_type=MESH)

        sends, arrivals, own = [], [], []
        for w in range(n):
            src = ins[w].at[pl.ds(c * halves[w], halves[w]), :]
            own.append(pltpu.make_async_copy(src, block(w, (x, y), c), local.at[w]))
            sends += [copy(w, 0, (x, y), c, (*near, c), src), copy(w, 1, (x, y), c, (x, y, 1 - c), src)]
            arrivals += [copy(w, 0, near, c, (x, y, c)), copy(w, 1, (x, y), 1 - c, (x, y, c))]
        return sends, arrivals, own

    return _Job(shards, [jax.ShapeDtypeStruct((N_CHIPS,) + s.shape, s.dtype) for s in shards], 2 * n, copies,
                NEIGHBOURS + SIBLING, n_local=n)


def _gather_far_job(stacked):
    n = len(stacked)
    halves = [s.shape[1] // 2 for s in stacked]

    def copies(ins, outs, send, recv, local):
        del ins, local
        x, y, c, _ = _place()
        near, far = _near_far(x, y, c)

        def copy(w, k, chip):
            blk = outs[w].at[_chip_index(*chip), pl.ds(c * halves[w], halves[w]), :]
            return pltpu.make_async_remote_copy(
                src_ref=blk, dst_ref=blk, send_sem=send.at[2 * w + k], recv_sem=recv.at[2 * w + k],
                device_id=(*far, c), device_id_type=MESH)

        sends = [copy(w, k, chip) for w in range(n) for k, chip in enumerate(((x, y), near))]
        arrivals = [copy(w, k, chip) for w in range(n) for k, chip in enumerate((far, (1 - x, 1 - y)))]
        return sends, arrivals, []

    return _Job(stacked, [jax.ShapeDtypeStruct(s.shape, s.dtype) for s in stacked], 2 * n, copies, NEIGHBOURS,
                aliases={w: w for w in range(n)})


def _gather_pass_job(stacked):
    n = len(stacked)
    halves = [s.shape[1] // 2 for s in stacked]

    def copies(ins, outs, send, recv, local):
        del ins, local
        x, y, c, chips = _place()

        def copy(w, j, chip, pc, to):
            blk = outs[w].at[_chip_index(*chip), pl.ds(pc * halves[w], halves[w]), :]
            return pltpu.make_async_remote_copy(
                src_ref=blk, dst_ref=blk, send_sem=send.at[3 * w + j], recv_sem=recv.at[3 * w + j], device_id=to,
                device_id_type=MESH)

        sends = [copy(w, j, chip, c, (x, y, 1 - c)) for w in range(n) for j, chip in enumerate(chips)]
        arrivals = [copy(w, j, chip, 1 - c, (x, y, c)) for w in range(n) for j, chip in enumerate(chips)]
        return sends, arrivals, []

    return _Job(stacked, [jax.ShapeDtypeStruct(s.shape, s.dtype) for s in stacked], 3 * n, copies, SIBLING,
                aliases={w: w for w in range(n)})


def _gather_small_job(block):
    def copies(ins, outs, send, recv, local):
        x, y, c, chips = _place()

        def copy(j, chip_from, to):
            return pltpu.make_async_remote_copy(
                src_ref=ins[0], dst_ref=outs[0].at[_chip_index(*chip_from)], send_sem=send.at[j],
                recv_sem=recv.at[j], device_id=to, device_id_type=MESH)

        own = [pltpu.make_async_copy(ins[0], outs[0].at[_chip_index(x, y)], local.at[0])]
        sends = [copy(j, (x, y), (*chip, c)) for j, chip in enumerate(chips)]
        arrivals = [copy(j, chip, (x, y, c)) for j, chip in enumerate(chips)]
        return sends, arrivals, own

    return _Job([block], [jax.ShapeDtypeStruct((N_CHIPS,) + block.shape, block.dtype)], 3, copies, OTHER_CHIPS,
                n_local=1)


def _pair_send_job(grads):
    n = len(grads)
    halves = [g.shape[1] // 2 for g in grads]

    def copies(ins, outs, send, recv, local):
        del local
        x, y, c, _ = _place()
        sends = [pltpu.make_async_remote_copy(
            src_ref=ins[w].at[:, pl.ds((1 - c) * halves[w], halves[w]), :], dst_ref=outs[w], send_sem=send.at[w],
            recv_sem=recv.at[w], device_id=(x, y, 1 - c), device_id_type=MESH) for w in range(n)]
        return sends, sends, []

    return _Job(grads, [jax.ShapeDtypeStruct((N_CHIPS, h, g.shape[2]), g.dtype) for g, h in zip(grads, halves)], n,
                copies, SIBLING)


def _row_block(rows, limit=256):
    return min(rows, limit)


def _pair_add(name, core, mine, theirs):
    _, _, h, cols = mine.shape
    rb = _row_block(h, 512)

    def body(core_ref, a_ref, b_ref, o_ref):
        del core_ref
        o_ref[...] = (a_ref[...].astype(F32) + b_ref[...].astype(F32)).astype(BF16)

    return pl.pallas_call(
        body, name=name,
        grid_spec=pltpu.PrefetchScalarGridSpec(
            num_scalar_prefetch=1, grid=(N_CHIPS, h // rb),
            in_specs=[pl.BlockSpec((None, None, rb, cols), lambda k, r, core_ref: (k, core_ref[0], r, 0)),
                      pl.BlockSpec((None, rb, cols), lambda k, r, core_ref: (k, r, 0))],
            out_specs=pl.BlockSpec((None, rb, cols), lambda k, r, core_ref: (k, r, 0))),
        out_shape=jax.ShapeDtypeStruct(theirs.shape, BF16),
        compiler_params=_params(2),
    )(core, mine, theirs)


def _sequencer_call(name, collective_id, job):
    steps, peers = job.phases, job.peers
    ins = [jax.new_ref(a, memory_space=pltpu.MemorySpace.HBM) for a in job.inputs]
    outs = [ins[{o: i for i, o in job.aliases.items()}[k]] if k in job.aliases.values()
            else jax.empty_ref(shape, memory_space=pltpu.MemorySpace.HBM) for k, shape in enumerate(job.out_shape)]
    sems = [pltpu.SemaphoreType.DMA((n,)) for step in steps for n in (step.n_sem, step.n_sem, max(step.n_local, 1))]

    @pl.kernel(mesh=plsc.ScalarSubcoreMesh(axis_name="sequencer", num_cores=1), name=name, scratch_types=tuple(sems),
               compiler_params=pltpu.CompilerParams(collective_id=collective_id))
    def launch(*sem_refs):
        x, y, c, _ = _place()
        barrier = pltpu.get_barrier_semaphore()
        for dx, dy, dc in peers:
            pl.semaphore_signal(barrier, inc=1, device_id=(x ^ dx, y ^ dy, c ^ dc), device_id_type=MESH)
        pl.semaphore_wait(barrier, len(peers))
        for k, step in enumerate(steps):
            sends, arrivals, own = step.copies(ins if k == 0 else outs, outs, *sem_refs[3 * k:3 * k + 3])
            for cp in own + sends:
                cp.start()
            for cp in arrivals:
                cp.wait_recv()
            for cp in sends:
                cp.wait_send()
            for cp in own:
                cp.wait()

    launch()
    return [ref[...] for ref in outs]


def _chip_exchange_job(sums):
    n = len(sums)

    def copies(ins, outs, send, recv, local):
        del local
        _, _, c, chips = _place()
        sends = [pltpu.make_async_remote_copy(
            src_ref=ins[w].at[_chip_index(*chip)], dst_ref=outs[w].at[j], send_sem=send.at[3 * w + j],
            recv_sem=recv.at[3 * w + j], device_id=(*chip, c), device_id_type=MESH)
            for w in range(n) for j, chip in enumerate(chips)]
        return sends, sends, []

    return _Job(sums, [jax.ShapeDtypeStruct((N_CHIPS - 1,) + s.shape[1:], s.dtype) for s in sums], 3 * n, copies,
                OTHER_CHIPS)


def _chip_sum(name, place, mine, theirs):
    _, h, cols = mine.shape
    rb = _row_block(h, 512)

    def body(place_ref, p_ref, q_ref, o_ref):
        del place_ref
        acc = p_ref[...].astype(F32)
        for j in range(N_CHIPS - 1):
            acc = acc + q_ref[j].astype(F32)
        o_ref[...] = acc

    return pl.pallas_call(
        body, name=name,
        grid_spec=pltpu.PrefetchScalarGridSpec(
            num_scalar_prefetch=1, grid=(h // rb,),
            in_specs=[pl.BlockSpec((None, rb, cols), lambda r, place_ref: (place_ref[0], r, 0)),
                      pl.BlockSpec((N_CHIPS - 1, rb, cols), lambda r, place_ref: (0, r, 0))],
            out_specs=pl.BlockSpec((None, rb, cols), lambda r, place_ref: (place_ref[1], r, 0))),
        out_shape=jax.ShapeDtypeStruct((2, h, cols), F32),
        compiler_params=_params(),
    )(place, mine, theirs)


def _share_job(bufs):
    n = len(bufs)

    def copies(ins, outs, send, recv, local):
        del ins, local
        x, y, c, _ = _place()

        def copy(w, half):
            return pltpu.make_async_remote_copy(
                src_ref=outs[w].at[half], dst_ref=outs[w].at[half], send_sem=send.at[w], recv_sem=recv.at[w],
                device_id=(x, y, 1 - c), device_id_type=MESH)

        return [copy(w, c) for w in range(n)], [copy(w, 1 - c) for w in range(n)], []

    return _Job(bufs, [jax.ShapeDtypeStruct(b.shape, b.dtype) for b in bufs], n, copies, SIBLING,
                aliases={w: w for w in range(n)})


SMALL_ROWS = 24
ROW_G1, ROW_CW, ROW_CB, ROW_BR, ROW_BI, ROW_LAM, ROW_LG, ROW_LB, ROW_G2, ROW_G3, ROW_LOSS, ROW_BS = (
    0, 1, 5, 6, 7, 8, 9, 10, 11, 12, 13, 16)
N_DEV = 8


def _pack_small(dcw, dcb, dbr, dbi, dlam, dlg, dlb, dg2, dg3, loss, dbs):
    def body(dcw_ref, dcb_ref, dbr_ref, dbi_ref, dlam_ref, dlg_ref, dlb_ref, dg2_ref, dg3_ref, loss_ref, dbs_ref, out):
        out[...] = jnp.zeros((SMALL_ROWS, D_MODEL), F32)
        for row, ref in ((ROW_CB, dcb_ref), (ROW_BR, dbr_ref), (ROW_BI, dbi_ref), (ROW_LAM, dlam_ref),
                         (ROW_LG, dlg_ref), (ROW_LB, dlb_ref), (ROW_G2, dg2_ref), (ROW_G3, dg3_ref)):
            out[row:row + 1, :] = ref[...]
        out[ROW_CW:ROW_CW + CONV_WIDTH, :] = dcw_ref[0:CONV_WIDTH, :]
        out[ROW_LOSS:ROW_LOSS + 1, 0:128] = loss_ref[0:1, :]
        out[ROW_BS:ROW_BS + GROUPS, 0:128] = jnp.transpose(dbs_ref[...])[0:GROUPS, :]

    vm = pl.BlockSpec(memory_space=pltpu.VMEM)
    return pl.pallas_call(
        body, name="pack_small", in_specs=[vm] * 11, out_specs=vm,
        out_shape=jax.ShapeDtypeStruct((SMALL_ROWS, D_MODEL), F32),
    )(dcw, dcb, dbr, dbi, dlam, dlg, dlb, dg2, dg3, loss, dbs)


def _gather_all_job(blocks):
    n = len(blocks)
    flips = [(dx, dy, dc) for dx in (0, 1) for dy in (0, 1) for dc in (0, 1)][1:]

    def copies(ins, outs, send, recv, local):
        x, y, c, _ = _place()
        me = 4 * x + 2 * y + c
        sends, arrivals, own = [], [], []
        for w in range(n):
            own.append(pltpu.make_async_copy(ins[w], outs[w].at[me], local.at[w]))
            for k, (dx, dy, dc) in enumerate(flips):
                peer = (x ^ dx, y ^ dy, c ^ dc)
                sem = dict(send_sem=send.at[7 * w + k], recv_sem=recv.at[7 * w + k])
                sends.append(pltpu.make_async_remote_copy(
                    src_ref=ins[w], dst_ref=outs[w].at[me], device_id=peer, device_id_type=MESH, **sem))
                arrivals.append(pltpu.make_async_remote_copy(
                    src_ref=ins[w], dst_ref=outs[w].at[4 * peer[0] + 2 * peer[1] + peer[2]], device_id=peer,
                    device_id_type=MESH, **sem))
        return sends, arrivals, own

    return _Job(blocks, [jax.ShapeDtypeStruct((N_DEV,) + b.shape, b.dtype) for b in blocks], 7 * n, copies,
                OTHER_CHIPS + SIBLING + tuple((dx, dy, 1) for dx, dy, _ in OTHER_CHIPS), n_local=n)


def _sum_small(vec_all, ws_all, dg1_all):
    def body(vec_ref, ws_ref, dg1_ref, vec_out, ws_out):
        vec, ws, dg1 = vec_ref[0], ws_ref[0], dg1_ref[0]
        for d in range(1, N_DEV):
            vec, ws, dg1 = vec + vec_ref[d], ws + ws_ref[d], dg1 + dg1_ref[d]
        vec_out[...] = vec
        vec_out[ROW_G1:ROW_G1 + 1, :] = dg1
        ws_out[...] = ws

    vm = pl.BlockSpec(memory_space=pltpu.VMEM)
    return pl.pallas_call(
        body, name="sum_small", in_specs=[vm] * 3, out_specs=[vm, vm],
        out_shape=[jax.ShapeDtypeStruct(vec_all.shape[1:], F32), jax.ShapeDtypeStruct(ws_all.shape[1:], F32)],
    )(vec_all, ws_all, dg1_all)


def _adamw_math(w, g, m, v):
    m = ADAM_B1 * m + (1.0 - ADAM_B1) * g
    v = ADAM_B2 * v + (1.0 - ADAM_B2) * (g * g)
    m_hat = m / (1.0 - ADAM_B1 ** ADAM_STEP)
    v_hat = v / (1.0 - ADAM_B2 ** ADAM_STEP)
    delta = (-ADAM_LR) * (m_hat / (jnp.sqrt(v_hat) + ADAM_EPS) + ADAM_WD * w)
    return delta, m, v


def _adamw(name, g, w, m, v, jobs=()):
    rows, cols = w.shape
    rb = _row_block(rows)

    def body(g_ref, w_ref, m_ref, v_ref, d_ref, nm_ref, nv_ref):
        d_ref[...], nm_ref[...], nv_ref[...] = _adamw_math(w_ref[...], g_ref[...], m_ref[...], v_ref[...])

    blk = pl.BlockSpec((rb, cols), lambda r: (r, 0))
    return _fused_call(
        body, jobs, name=name, grid=(rows // rb,), in_specs=[blk] * 4, out_specs=[blk] * 3,
        out_shape=[jax.ShapeDtypeStruct(w.shape, F32)] * 3, compiler_params=_params(),
    )(g, w, m, v)


def _adamw_small(grads, ws, ms, vs):
    n = len(grads)

    def body(*refs):
        g_refs, w_refs, m_refs, v_refs = refs[:n], refs[n:2 * n], refs[2 * n:3 * n], refs[3 * n:4 * n]
        outs = refs[4 * n:]
        for p in range(n):
            d, nm, nv = _adamw_math(w_refs[p][...], g_refs[p][...], m_refs[p][...], v_refs[p][...])
            outs[p][...] = d
            outs[n + p][...] = nm
            outs[2 * n + p][...] = nv

    vm = pl.BlockSpec(memory_space=pltpu.VMEM)
    shapes = [jax.ShapeDtypeStruct(w.shape, F32) for w in ws]
    out = pl.pallas_call(
        body, name="adamw_small", in_specs=[vm] * (4 * n), out_specs=[vm] * (3 * n), out_shape=shapes * 3,
    )(*grads, *ws, *ms, *vs)
    return out[:n], out[n:2 * n], out[2 * n:]


def _unstack_heads(w_st):
    per = HEAD_DIM // N_CHIPS
    return w_st.reshape(N_CHIPS, HEADS, per, HEAD_DIM).transpose(1, 0, 2, 3).reshape(HEADS, HEAD_DIM, HEAD_DIM)


def _stack_heads(w):
    per = HEAD_DIM // N_CHIPS
    return w.reshape(HEADS, N_CHIPS, per, HEAD_DIM).transpose(1, 0, 2, 3).reshape(N_CHIPS, HEADS * per, HEAD_DIM)


def kernel(x, norm_mix_g, w_in, conv_w, conv_b, w_rgate, b_rgate, w_igate, b_igate, lru_lambda, w_out_a, sgu_ln_g, sgu_ln_b, sgu_w_s, sgu_b_s, w_out_b, w_out, norm_mlp_g, w_up, w_down, norm_final_g, loss_target, m_norm_mix_g, m_w_in, m_conv_w, m_conv_b, m_w_rgate, m_b_rgate, m_w_igate, m_b_igate, m_lru_lambda, m_w_out_a, m_sgu_ln_g, m_sgu_ln_b, m_sgu_w_s, m_sgu_b_s, m_w_out_b, m_w_out, m_norm_mlp_g, m_w_up, m_w_down, m_norm_final_g, v_norm_mix_g, v_w_in, v_conv_w, v_conv_b, v_w_rgate, v_b_rgate, v_w_igate, v_b_igate, v_lru_lambda, v_w_out_a, v_sgu_ln_g, v_sgu_ln_b, v_sgu_w_s, v_sgu_b_s, v_w_out_b, v_w_out, v_norm_mlp_g, v_w_up, v_w_down, v_norm_final_g):
    chip = _chip_index(lax.axis_index("x"), lax.axis_index("y"))
    core = lax.axis_index("c")
    quarter_h = HEAD_DIM // N_CHIPS
    quarter_d = D_MODEL // N_CHIPS

    as_2d = lambda a: a.reshape(-1, a.shape[-1])
    big_w = [as_2d(w) for w in (w_in, w_rgate, w_igate, w_out_a, w_out_b, w_out, w_up, w_down)]
    big_m = [as_2d(w) for w in (m_w_in, m_w_rgate, m_w_igate, m_w_out_a, m_w_out_b, m_w_out, m_w_up, m_w_down)]
    big_v = [as_2d(w) for w in (v_w_in, v_w_rgate, v_w_igate, v_w_out_a, v_w_out_b, v_w_out, v_w_up, v_w_down)]

    packed = jnp.concatenate([conv_w[0], b_rgate[0], b_igate[0]], axis=1)
    packed = jnp.concatenate([packed, jnp.zeros_like(packed)], axis=0)
    s_in, s_r, s_i, s_oa, s_ob, s_out, s_up, s_down = [w.astype(BF16) for w in big_w]
    xs, target = x[0], loss_target[0]
    g3 = norm_final_g.reshape(1, D_MODEL)
    bias_s = jnp.broadcast_to(jnp.transpose(sgu_b_s[0])[:, :, None], (CHUNK, GROUPS, GROUP_DIM)).reshape(CHUNK, D_MODEL)
    core_arr = core.reshape(1).astype(jnp.int32)
    place = jnp.stack([chip, core]).astype(jnp.int32)
    quarter = lambda g: g.reshape(N_CHIPS, D_MODEL // N_CHIPS, D_MODEL)

    def pair_add(nm, g, from_sibling):
        return _pair_add("pair_add_" + nm, core_arr, g.reshape(N_CHIPS, 2, g.shape[1] // 2, g.shape[2]), from_sibling)

    def chip_sum(nm, pair, from_chips):
        return _chip_sum("chip_sum_" + nm, place, pair, from_chips)

    order = jnp.stack([chip, chip ^ 2, chip ^ 1, chip ^ 3]).astype(jnp.int32)
    projections = _gather_near_job([s_oa, s_ob, s_out]).then(_gather_far_job, at=(2, 0)).then(_gather_pass_job, at=(3, 0))
    mlp_near = _gather_near_job([s_up, s_down])
    mlp_near.phases[0].start = (3, 0)
    (z, n1, (w_in_st, wr_st, wi_st)), ((packed_all,), late, mlp_w) = _fwd_in(
        xs, norm_mix_g, [s_in, s_r, s_i], order, jobs=[_gather_small_job(packed), projections, mlp_near])
    pick = lambda lo, hi: packed_all[:, :HEADS, lo:hi].transpose(1, 0, 2).reshape(HEADS, -1)
    conv_w_full = pick(0, quarter_d)
    br_full = pick(quarter_d, quarter_d + quarter_h).reshape(1, D_MODEL)
    bi_full = pick(quarter_d + quarter_h, quarter_d + 2 * quarter_h).reshape(1, D_MODEL)
    wr, wi = _unstack_heads(wr_st), _unstack_heads(wi_st)
    lru = (conv_w_full, conv_b, wr, br_full, wi, bi_full, lru_lambda)
    sgu = (sgu_ln_g, sgu_ln_b, sgu_w_s[0], bias_s)

    after = lambda arrays, result: lax.optimization_barrier((arrays, result))[0]
    w_up_st, w_dn = _sequencer_call("gather_mlp", 8, _gather_far_job(mlp_w).then(_gather_pass_job))
    w_dn = w_dn.reshape(D_FF, D_MODEL)
    w_oa, w_ob, w_o = [w.reshape(D_MODEL, D_MODEL) for w in late]
    (ya, *saved, yb, pa, pb, h1, n2), _ = _fwd_mix(z, xs, *lru, *sgu, w_oa, w_ob, w_o, norm_mlp_g)
    (act, dup, dh2b, dh1, loss_part, dg3, dg2), _ = _mlp(n2, h1, target, w_up_st, w_dn, norm_mlp_g, g3)

    d_down, _ = _weight_grad("dw_down", act, dh2b, N_CHIPS, True, False, D_MODEL)
    r_down, = _sequencer_call("send_w_down", 10, _pair_send_job([d_down]))
    d_up, _ = _weight_grad("dw_up", n2, dup, N_CHIPS, False, True, D_MODEL)
    r_up, = _sequencer_call("send_w_up", 11, _pair_send_job([d_up]))
    p_down, p_up = pair_add("w_down", d_down, r_down), pair_add("w_up", d_up, r_up)
    (dz, merged, dpa, dpb, dh1b, dlg, dlb, dws, dbs, dcw, dcb, dwr, dbr, dwi, dbi, dlam), ((q_up, q_down),) = _bwd_mix(
        dh1, pa, pb, z, *saved, w_oa, w_ob, w_o, *sgu, conv_w_full, wr, wi, lru_lambda,
        jobs=[_chip_exchange_job([p_up, p_down])])
    half_up, half_down = chip_sum("w_up", p_up, q_up), chip_sum("w_down", p_down, q_down)
    names = ("w_in", "w_rgate", "w_igate", "w_out_a", "w_out_b", "w_out", "w_up", "w_down")
    (d_out, d_oa, d_ob), ((full_up, full_down),) = _weight_grads_square(
        "dw_projections", [(merged, dh1b), (ya, dpa), (yb, dpb)], jobs=[_share_job([half_up, half_down])])
    mids = [quarter(d_oa), quarter(d_ob), quarter(d_out)]
    r_mids = _sequencer_call("send_mids", 1, _pair_send_job(mids))
    gates = [_stack_heads(dwr).astype(BF16), _stack_heads(dwi).astype(BF16)]
    small = _pack_small(dcw, dcb, dbr, dbi, dlam, dlg, dlb, dg2, dg3, loss_part, dbs)
    p_mids = [pair_add(nm, g, r) for nm, g, r in zip(names[3:6], mids, r_mids)]
    q_mids = _sequencer_call("exchange_mids", 2, _chip_exchange_job(p_mids))
    d_in, (r_gates, (vec_all, ws_all)) = _weight_grad(
        "dw_in", n1, dz, N_CHIPS, False, True, IN_SHARD,
        jobs=[_pair_send_job(gates), _gather_all_job([small, dws])])
    r_in, = _sequencer_call("send_w_in", 3, _pair_send_job([d_in]))
    adam_args = {nm: (w, m, v) for nm, w, m, v in zip(names, big_w, big_m, big_v)}

    def adamw(nm, g):
        w, m, v = adam_args[nm]
        g = g.reshape(w.shape)
        return g, _adamw("adamw_" + nm, g, w, m, v)[0]

    p_gates = [pair_add(nm, g, r) for nm, g, r in zip(names[1:3], gates, r_gates)]
    half_mids = [chip_sum(nm, p, q) for nm, p, q in zip(names[3:6], p_mids, q_mids)]
    full_mids = _sequencer_call("share_mids", 12, _share_job(half_mids))
    p_first = [pair_add("w_in", d_in, r_in)] + p_gates
    q_first = _sequencer_call("exchange_w_in", 4, _chip_exchange_job(p_first))
    (grad_x, dg1), _ = _bwd_in(dz, xs, dh1, w_in_st, norm_mix_g)
    dg1_all, = _sequencer_call("gather_dg1", 6, _gather_all_job([dg1]))
    done = {nm: adamw(nm, f) for nm, f in zip(("w_up", "w_down") + names[3:6], [full_up, full_down] + full_mids)}
    q_first = after(q_first, [out[0] for _, out in done.values()])
    half_first = [chip_sum(nm, p, q) for nm, p, q in zip(names[:3], p_first, q_first)]
    full_first = _sequencer_call("share_last", 5, _share_job(half_first))
    done.update({nm: adamw(nm, f) for nm, f in zip(names[:3], full_first)})
    full, big_out = [done[nm][0] for nm in names], [done[nm][1] for nm in names]

    vec, ws_sum = _sum_small(vec_all, ws_all, dg1_all)
    row = lambda r: vec[r:r + 1]
    shard = lambda a, width: lax.dynamic_slice_in_dim(a, chip * width, width, axis=1)
    g_small = dict(
        norm_mix_g=row(ROW_G1), conv_w=shard(vec[ROW_CW:ROW_CW + CONV_WIDTH], quarter_d), conv_b=row(ROW_CB),
        b_rgate=shard(row(ROW_BR).reshape(HEADS, HEAD_DIM), quarter_h),
        b_igate=shard(row(ROW_BI).reshape(HEADS, HEAD_DIM), quarter_h), lru_lambda=row(ROW_LAM),
        sgu_ln_g=row(ROW_LG), sgu_ln_b=row(ROW_LB),
        sgu_w_s=ws_sum.reshape(CHUNK, GROUPS, CHUNK).transpose(1, 0, 2).reshape(GROUPS * CHUNK, CHUNK),
        sgu_b_s=vec[ROW_BS:ROW_BS + GROUPS, 0:CHUNK], norm_mlp_g=row(ROW_G2), norm_final_g=row(ROW_G3))
    loss = vec[ROW_LOSS, 0]
    small_names = list(g_small)
    given = dict(
        norm_mix_g=(norm_mix_g, m_norm_mix_g, v_norm_mix_g), conv_w=(conv_w, m_conv_w, v_conv_w),
        conv_b=(conv_b, m_conv_b, v_conv_b), b_rgate=(b_rgate, m_b_rgate, v_b_rgate),
        b_igate=(b_igate, m_b_igate, v_b_igate), lru_lambda=(lru_lambda, m_lru_lambda, v_lru_lambda),
        sgu_ln_g=(sgu_ln_g, m_sgu_ln_g, v_sgu_ln_g), sgu_ln_b=(sgu_ln_b, m_sgu_ln_b, v_sgu_ln_b),
        sgu_w_s=(sgu_w_s, m_sgu_w_s, v_sgu_w_s), sgu_b_s=(sgu_b_s, m_sgu_b_s, v_sgu_b_s),
        norm_mlp_g=(norm_mlp_g, m_norm_mlp_g, v_norm_mlp_g), norm_final_g=(norm_final_g, m_norm_final_g, v_norm_final_g))
    g2d = [g_small[nm] for nm in small_names]
    to2d = lambda a, g: a.reshape(g.shape)
    d_s, m_s, v_s = _adamw_small(
        g2d, *[[to2d(given[nm][q], g) for nm, g in zip(small_names, g2d)] for q in range(3)])

    shapes = dict(
        norm_mix_g=norm_mix_g, w_in=w_in, conv_w=conv_w, conv_b=conv_b, w_rgate=w_rgate, b_rgate=b_rgate,
        w_igate=w_igate, b_igate=b_igate, lru_lambda=lru_lambda, w_out_a=w_out_a, sgu_ln_g=sgu_ln_g,
        sgu_ln_b=sgu_ln_b, sgu_w_s=sgu_w_s, sgu_b_s=sgu_b_s, w_out_b=w_out_b, w_out=w_out, norm_mlp_g=norm_mlp_g,
        w_up=w_up, w_down=w_down, norm_final_g=norm_final_g)
    grads, deltas, new_m, new_v = {}, {}, {}, {}
    for nm, g, (d, nmom, nvar) in zip(names, full, big_out):
        grads[nm], deltas[nm], new_m[nm], new_v[nm] = g, d, nmom, nvar
    for p, nm in enumerate(small_names):
        grads[nm], deltas[nm], new_m[nm], new_v[nm] = g2d[p], d_s[p], m_s[p], v_s[p]
    order = list(shapes)
    out = [loss, grad_x[None]]
    for group in (grads, deltas, new_m, new_v):
        out += [group[nm].reshape(shapes[nm].shape) for nm in order]
    return tuple(out)
```

```python
import functools

import jax
import jax.numpy as jnp
from jax import lax
from jax.experimental import pallas as pl
from jax.experimental.pallas import tpu as pltpu
from jax.experimental.pallas import tpu_sc as plsc

F32 = jnp.float32
BF16 = jnp.bfloat16
MESH = pl.DeviceIdType.MESH

D_MODEL = 1024
D_IN = 6 * D_MODEL
D_FF = 4 * D_MODEL
N_CHIPS = 4
IN_SHARD = D_IN // N_CHIPS
HEADS = 4
HEAD_DIM = D_MODEL // HEADS
GROUPS = 4
GROUP_DIM = D_MODEL // GROUPS
CHUNK = 128
CONV_WIDTH = 4
LRU_C = 8.0
NORM_EPS = 1e-6
LN_EPS = 1e-5

ADAM_LR = 0.001
ADAM_B1 = 0.9
ADAM_B2 = 0.999
ADAM_EPS = 1e-08
ADAM_WD = 0.01
ADAM_STEP = 10

SUBLANES = 8
MM_TILE = 512
IN_TILE = 1024
SEQ_TILE = 256
DW_TILE = 2048
VMEM_LIMIT_BYTES = 56 * 1024 * 1024

GELU_K0 = 0.7978845608028654
GELU_K1 = 0.044715


def _params(n_grid_axes=1):
    return pltpu.CompilerParams(
        dimension_semantics=("arbitrary",) * n_grid_axes, vmem_limit_bytes=VMEM_LIMIT_BYTES)


def _resident(shape):
    nd = len(shape)
    return pl.BlockSpec(shape, lambda *_: (0,) * nd, pipeline_mode=pl.Buffered(1))


def _const(shape):
    nd = len(shape)
    return pl.BlockSpec(shape, lambda *_: (0,) * nd)


def _dot(a, b):
    return jnp.dot(a, b, preferred_element_type=F32)


def _dot_nt(a, b):
    return lax.dot_general(a, b, (((1,), (1,)), ((), ())), preferred_element_type=F32)


def _dot_tn(a, b):
    return lax.dot_general(a, b, (((0,), (0,)), ((), ())), preferred_element_type=F32)


def _gelu(x):
    t = jnp.tanh(GELU_K0 * x * (1.0 + GELU_K1 * x * x))
    return 0.5 * x * (1.0 + t)


def _gelu_and_grad(x):
    x2 = x * x
    t = jnp.tanh(GELU_K0 * x * (1.0 + GELU_K1 * x2))
    g = 0.5 * x * (1.0 + t)
    dg = 0.5 * (1.0 + t) + 0.5 * x * (1.0 - t * t) * (GELU_K0 * (1.0 + 3.0 * GELU_K1 * x2))
    return g, dg


def _rms(x):
    r = lax.rsqrt(jnp.mean(x * x, axis=-1, keepdims=True) + NORM_EPS)
    return x * r, r


def _rms_bwd(dn, xhat, r):
    return r * (dn - xhat * jnp.mean(dn * xhat, axis=-1, keepdims=True))


def _col_sum(v):
    return jnp.sum(v, axis=0, keepdims=True)


def _shift_down(x, tail8, k):
    xs = pltpu.roll(x, k, 0)
    ts = pltpu.roll(tail8, k, 0)
    ridx = lax.broadcasted_iota(jnp.int32, tail8.shape, 0)
    head = jnp.where(ridx < k, ts, xs[0:SUBLANES])
    return jnp.concatenate([head, xs[SUBLANES:]], axis=0)


def _shift_up(x, head8, k):
    n = x.shape[0]
    xs = pltpu.roll(x, n - k, 0)
    hs = pltpu.roll(head8, SUBLANES - k, 0)
    ridx = lax.broadcasted_iota(jnp.int32, head8.shape, 0)
    last = jnp.where(ridx >= SUBLANES - k, hs, xs[n - SUBLANES:n])
    return jnp.concatenate([xs[:n - SUBLANES], last], axis=0)


def _scan_forward(a, b, carry):
    n, cols = a.shape
    groups = n // SUBLANES
    a = a.reshape(groups, SUBLANES, cols)
    b = b.reshape(groups, SUBLANES, cols)
    sub = lax.broadcasted_iota(jnp.int32, a.shape, 1)
    for s in (1, 2, 4):
        a_s = pltpu.roll(a, s, 1)
        b_s = pltpu.roll(b, s, 1)
        m = sub >= s
        b = jnp.where(m, a * b_s + b, b)
        a = jnp.where(m, a * a_s, a)
    out = []
    for g in range(groups):
        h = a[g] * carry + b[g]
        out.append(h)
        carry = h[SUBLANES - 1:SUBLANES]
    return jnp.concatenate(out, axis=0), carry


def _scan_backward(a, b, carry):
    n, cols = a.shape
    groups = n // SUBLANES
    a = a.reshape(groups, SUBLANES, cols)
    b = b.reshape(groups, SUBLANES, cols)
    sub = lax.broadcasted_iota(jnp.int32, a.shape, 1)
    for s in (1, 2, 4):
        a_s = pltpu.roll(a, SUBLANES - s, 1)
        b_s = pltpu.roll(b, SUBLANES - s, 1)
        m = sub < SUBLANES - s
        b = jnp.where(m, a * b_s + b, b)
        a = jnp.where(m, a * a_s, a)
    out = [None] * groups
    for g in reversed(range(groups)):
        h = a[g] * carry + b[g]
        out[g] = h
        carry = h[0:1]
    return jnp.concatenate(out, axis=0), carry


def _softplus_neg(lam):
    e = jnp.exp(-jnp.abs(lam))
    u = 1.0 + e
    log1p_e = jnp.where(u == 1.0, e, jnp.log(u) * (e / jnp.where(u == 1.0, 1.0, u - 1.0)))
    return jnp.maximum(-lam, 0.0) + log1p_e


def _lru_gates(xa, tail8, cw_ref, cb_ref, wr_ref, br_ref, wi_ref, bi_ref, lam_ref):
    cw = cw_ref[...]
    xc = cb_ref[...] + cw[0:1] * xa
    for k in range(1, CONV_WIDTH):
        xc = xc + cw[k:k + 1] * _shift_down(xa, tail8, k)
    xcb = xc.astype(BF16)
    pre_r, pre_i = [], []
    for h in range(HEADS):
        cols = slice(h * HEAD_DIM, (h + 1) * HEAD_DIM)
        pre_r.append(_dot(xcb[:, cols], wr_ref[h]))
        pre_i.append(_dot(xcb[:, cols], wi_ref[h]))
    r = jax.nn.sigmoid(jnp.concatenate(pre_r, axis=1) + br_ref[...])
    ig = jax.nn.sigmoid(jnp.concatenate(pre_i, axis=1) + bi_ref[...])
    _, a, mult = _decay(r, lam_ref)
    return xc, r, ig, a, mult


def _decay(r, lam_ref):
    sp = _softplus_neg(lam_ref[...])
    log_a = ((-LRU_C) * sp) * r
    a = jnp.exp(log_a)
    th = jnp.tanh(log_a)
    return sp, a, jnp.sqrt((-2.0 * th) / (1.0 - th))


class _Phase:
    def __init__(self, copies, n_sem, n_local, start=None, finish=None):
        self.copies, self.n_sem, self.n_local, self.start, self.finish = copies, n_sem, n_local, start, finish


class _Job:
    def __init__(self, inputs, out_shape, n_sem, copies, peers, aliases=None, n_local=0):
        self.inputs, self.out_shape = list(inputs), list(out_shape)
        self.aliases = dict(aliases or {})
        self.phases = [_Phase(copies, n_sem, n_local)]
        self.peers = tuple(peers)

    def then(self, make, at=None):
        nxt = make(self.out_shape)
        self.phases[-1].finish = at
        nxt.phases[0].start = at
        self.phases += nxt.phases
        self.peers = tuple(sorted(set(self.peers + nxt.peers)))
        return self


def _fused_call(body, jobs, *, name, grid, in_specs, out_specs, out_shape, scratch_shapes=(),
                input_output_aliases=None, compiler_params=None, n_prefetch=0, jobs_start_after=None):
    single = not isinstance(out_shape, (list, tuple))
    out_specs = [out_specs] if single else list(out_specs)
    out_shape = [out_shape] if single else list(out_shape)
    n_scr = len(scratch_shapes)
    in_specs, scratch_shapes = list(in_specs), list(scratch_shapes)
    n_in, n_out = len(in_specs), len(out_shape)
    aliases = dict(input_output_aliases or {})
    in_at, out_at, phases = [], [], []
    for q, job in enumerate(jobs):
        in_at.append(len(in_specs))
        out_at.append(len(out_shape))
        for i, o in job.aliases.items():
            aliases[n_prefetch + len(in_specs) + i] = len(out_shape) + o
        in_specs += [ANY] * len(job.inputs)
        out_specs += [ANY] * len(job.out_shape)
        out_shape += job.out_shape
        for k, phase in enumerate(job.phases):
            phases.append((q, k, phase, len(scratch_shapes)))
            scratch_shapes += [pltpu.SemaphoreType.DMA((phase.n_sem,)), pltpu.SemaphoreType.DMA((phase.n_sem,)),
                               pltpu.SemaphoreType.DMA((max(phase.n_local, 1),))]
    n_in_all, n_out_all = len(in_specs), len(out_shape)
    first_step, last_step = (0,) * len(grid), tuple(g - 1 for g in grid)

    def full_body(*refs):
        prefetch, refs = refs[:n_prefetch], refs[n_prefetch:]
        ins, outs, scr = refs[:n_in_all], refs[n_in_all:n_in_all + n_out_all], refs[n_in_all + n_out_all:]
        ids = [pl.program_id(a) for a in range(len(grid))]
        at_step = lambda step: functools.reduce(jnp.logical_and, [i == k for i, k in zip(ids, step)])

        def copies(q, k, phase, sem_at):
            job = jobs[q]
            mine = outs[out_at[q]:out_at[q] + len(job.out_shape)]
            return phase.copies(ins[in_at[q]:in_at[q] + len(job.inputs)] if k == 0 else mine, mine,
                                *scr[sem_at:sem_at + 3])

        def start(*phase):
            def go():
                sends, _, local = copies(*phase)
                for cp in local + sends:
                    cp.start()
            return go

        def finish(*phase):
            def go():
                sends, arrivals, local = copies(*phase)
                for cp in arrivals:
                    cp.wait_recv()
                for cp in sends:
                    cp.wait_send()
                for cp in local:
                    cp.wait()
            return go

        for phase in phases:
            if phase[2].start is None and jobs_start_after is None:
                pl.when(at_step(first_step))(start(*phase))
        body(*prefetch, *ins[:n_in], *outs[:n_out], *scr[:n_scr])
        for phase in phases:
            pl.when(at_step(phase[2].finish or last_step))(finish(*phase))
            nxt = phase[2].start or jobs_start_after
            if nxt is not None:
                pl.when(at_step(nxt))(start(*phase))

    if n_prefetch:
        layout = dict(grid_spec=pltpu.PrefetchScalarGridSpec(
            num_scalar_prefetch=n_prefetch, grid=grid, in_specs=in_specs, out_specs=out_specs,
            scratch_shapes=scratch_shapes))
    else:
        layout = dict(grid=grid, in_specs=in_specs, out_specs=out_specs, scratch_shapes=scratch_shapes)
    call = pl.pallas_call(
        full_body, name=name, out_shape=out_shape, input_output_aliases=aliases, compiler_params=compiler_params,
        **layout)

    def run(*args):
        res = call(*args, *[a for job in jobs for a in job.inputs])
        mine = res[0] if single else list(res[:n_out])
        return mine, [list(res[at:at + len(job.out_shape)]) for at, job in zip(out_at, jobs)]

    return run


def _fwd_in(x, g1, shards, order, jobs=()):
    t = x.shape[0]
    rows_per_step = min(IN_TILE, t)
    n_tiles = t // rows_per_step
    n = len(shards)
    halves = [s.shape[0] // 2 for s in shards]

    def body(order_ref, x_ref, g_ref, *refs):
        del order_ref
        ins, (z_ref, n_ref), outs = refs[:n], refs[n:n + 2], refs[n + 2:2 * n + 2]
        wbuf, nbuf, send, recv, local = refs[2 * n + 2:]
        s, i = pl.program_id(0), pl.program_id(1)
        x_, y_, c, chips = _place()
        near, far = _near_far(x_, y_, c)
        k_me = _chip_index(x_, y_)

        def block(w, chip, pc):
            return outs[w].at[_chip_index(*chip), pl.ds(pc * halves[w], halves[w]), :]

        def over_ici(w, j, landing):
            return pltpu.make_async_remote_copy(
                src_ref=ins[w].at[pl.ds(c * halves[w], halves[w]), :],
                dst_ref=block(w, chips[j] if landing else (x_, y_), c), send_sem=send.at[6 * w + j],
                recv_sem=recv.at[6 * w + j], device_id=(*chips[j], c), device_id_type=MESH)

        def onward(w, landing):
            blk = block(w, chips[2] if landing else near, c)
            return pltpu.make_async_remote_copy(
                src_ref=blk, dst_ref=blk, send_sem=send.at[6 * w + 2], recv_sem=recv.at[6 * w + 2],
                device_id=(*far, c), device_id_type=MESH)

        def to_sibling(w, j, landing):
            blk = block(w, chips[j], 1 - c if landing else c)
            return pltpu.make_async_remote_copy(
                src_ref=blk, dst_ref=blk, send_sem=send.at[6 * w + 3 + j], recv_sem=recv.at[6 * w + 3 + j],
                device_id=(x_, y_, 1 - c), device_id_type=MESH)

        own = [pltpu.make_async_copy(wbuf, outs[0].at[k_me], local.at[0])]
        own += [pltpu.make_async_copy(ins[w], outs[w].at[k_me], local.at[w]) for w in range(1, n)]

        @pl.when((s == 0) & (i == 0))
        def _():
            for j in range(2):
                for w in range(n):
                    over_ici(w, j, False).start()
            load = pltpu.make_async_copy(ins[0], wbuf, local.at[n])
            load.start()
            load.wait()
            for cp in own:
                cp.start()

        for j in range(N_CHIPS - 1):
            @pl.when((s == j + 1) & (i == 0))
            def _(j=j):
                if j == 0:
                    for k in range(2):
                        for w in range(n):
                            over_ici(w, k, True).wait_recv()
                    for w in range(n):
                        onward(w, False).start()
                    for k in range(2):
                        for w in range(n):
                            to_sibling(w, k, False).start()
                    own[0].wait()
                if j == 2:
                    for w in range(n):
                        onward(w, True).wait_recv()
                    for w in range(n):
                        to_sibling(w, j, False).start()
                for w in range(n):
                    to_sibling(w, j, True).wait_recv()
                load = pltpu.make_async_copy(outs[0].at[_chip_index(*chips[j])], wbuf, local.at[n])
                load.start()
                load.wait()

        rows = pl.ds(pl.multiple_of(i * rows_per_step, rows_per_step), rows_per_step)

        @pl.when(s == 0)
        def _():
            xhat, _ = _rms(x_ref[...])
            nrm = (xhat * g_ref[...]).astype(BF16)
            nbuf[rows, :] = nrm
            n_ref[...] = nrm

        z_ref[...] = _dot(nbuf[rows, :], wbuf[...])

        @pl.when((s == N_CHIPS - 1) & (i == n_tiles - 1))
        def _():
            for j in range(N_CHIPS - 1):
                for w in range(n):
                    (over_ici(w, j, False) if j < 2 else onward(w, False)).wait_send()
                    to_sibling(w, j, False).wait_send()
            for cp in own[1:]:
                cp.wait()

    once = lambda s, i, order: (jnp.where(s == 0, i, n_tiles - 1), 0)
    (z, n1, *stacked), job_outs = _fused_call(
        body, jobs, name="fwd_in", grid=(N_CHIPS, n_tiles), n_prefetch=1,
        in_specs=[pl.BlockSpec((rows_per_step, D_MODEL), once), _const((1, D_MODEL))] + [ANY] * n,
        out_specs=[pl.BlockSpec((rows_per_step, IN_SHARD), lambda s, i, order: (i, order[s])),
                   pl.BlockSpec((rows_per_step, D_MODEL), once)] + [ANY] * n,
        out_shape=[jax.ShapeDtypeStruct((t, D_IN), F32), jax.ShapeDtypeStruct((t, D_MODEL), BF16)]
        + [jax.ShapeDtypeStruct((N_CHIPS,) + s.shape, s.dtype) for s in shards],
        scratch_shapes=[pltpu.VMEM(shards[0].shape, BF16), pltpu.VMEM((t, D_MODEL), BF16),
                        pltpu.SemaphoreType.DMA((6 * n,)),
                        pltpu.SemaphoreType.DMA((6 * n,)), pltpu.SemaphoreType.DMA((n + 1,))],
        compiler_params=_params(2), jobs_start_after=(1, 0),
    )(order, x, g1, *shards)
    return (z, n1, stacked), job_outs


def _fwd_lru(z, conv_w, conv_b, wr, br, wi, bi, lam, jobs=()):
    t = z.shape[0]

    def body(xa_ref, ga_ref, cw_ref, cb_ref, wr_ref, br_ref, wi_ref, bi_ref, lam_ref, ya_ref, h_ref, xc_ref, r_ref,
             ig_ref, tail_ref, carry_ref):
        @pl.when(pl.program_id(0) == 0)
        def _():
            tail_ref[...] = jnp.zeros_like(tail_ref)
            carry_ref[...] = jnp.zeros_like(carry_ref)

        xa = xa_ref[...]
        xc, r, ig, a, mult = _lru_gates(xa, tail_ref[...], cw_ref, cb_ref, wr_ref, br_ref, wi_ref, bi_ref, lam_ref)
        tail_ref[...] = xa[SEQ_TILE - SUBLANES:]
        xc_ref[...], r_ref[...], ig_ref[...] = xc, r, ig
        h, carry = _scan_forward(a, xc * ig * mult, carry_ref[...])
        carry_ref[...] = carry
        h_ref[...] = h
        ya_ref[...] = (h * _gelu(ga_ref[...])).astype(BF16)

    tile = lambda j: pl.BlockSpec((SEQ_TILE, D_MODEL), lambda i: (i, j))
    return _fused_call(
        body, jobs, name="fwd_lru", grid=(t // SEQ_TILE,),
        in_specs=[tile(0), tile(1), _const((CONV_WIDTH, D_MODEL)), _const((1, D_MODEL)),
                  _resident((HEADS, HEAD_DIM, HEAD_DIM)), _const((1, D_MODEL)),
                  _resident((HEADS, HEAD_DIM, HEAD_DIM)), _const((1, D_MODEL)), _const((1, D_MODEL))],
        out_specs=[tile(0)] * 5,
        out_shape=[jax.ShapeDtypeStruct((t, D_MODEL), BF16)] + [jax.ShapeDtypeStruct((t, D_MODEL), F32)] * 4,
        scratch_shapes=[pltpu.VMEM((SUBLANES, D_MODEL), F32), pltpu.VMEM((1, D_MODEL), F32)],
        compiler_params=_params(),
    )(z, z, conv_w, conv_b, wr, br, wi, bi, lam)


def _sgu_forward_parts(ub, vb, lg_ref, lb_ref):
    u, du = _gelu_and_grad(ub)
    vg, dvg = _gelu_and_grad(vb)
    mu = jnp.mean(vg, axis=-1, keepdims=True)
    d = vg - mu
    rstd = lax.rsqrt(jnp.mean(d * d, axis=-1, keepdims=True) + LN_EPS)
    vhat = d * rstd
    vn = (vhat * lg_ref[...] + lb_ref[...]).astype(BF16)
    return u, du, dvg, rstd, vhat, vn


def _causal_mask():
    rows = lax.broadcasted_iota(jnp.int32, (CHUNK, CHUNK), 0)
    cols = lax.broadcasted_iota(jnp.int32, (CHUNK, CHUNK), 1)
    return rows >= cols


def _fwd_sgu_merge(ya, z, x, ln_g, ln_b, w_s, bias_full, w_oa, w_ob, w_out, g2, jobs=()):
    t = x.shape[0]

    def body(ya_ref, ub_ref, vb_ref, m_ref, x_ref, lg_ref, lb_ref, ws_ref, bias_ref, woa_ref, wob_ref, wout_ref, g_ref,
             yb_ref, pa_ref, pb_ref, h1_ref, n2_ref):
        u, _, _, _, _, vn = _sgu_forward_parts(ub_ref[...], vb_ref[...], lg_ref, lb_ref)
        mask = _causal_mask()
        wm = [jnp.where(mask, ws_ref[g], 0.0).astype(BF16) for g in range(GROUPS)]
        for c in range(SEQ_TILE // CHUNK):
            rows = slice(c * CHUNK, (c + 1) * CHUNK)
            for g in range(GROUPS):
                cols = slice(g * GROUP_DIM, (g + 1) * GROUP_DIM)
                sp = _dot(wm[g], vn[rows, cols]) + bias_ref[:, cols]
                yb_ref[rows, cols] = (u[rows, cols] * sp).astype(BF16)
        pa = _dot(ya_ref[...], woa_ref[...])
        pb = _dot(yb_ref[...], wob_ref[...])
        pa_ref[...] = pa
        pb_ref[...] = pb
        merged = jax.nn.sigmoid(m_ref[:, :D_MODEL]) * pa + jax.nn.sigmoid(m_ref[:, D_MODEL:]) * pb
        h1 = x_ref[...] + _dot(merged.astype(BF16), wout_ref[...])
        h1_ref[...] = h1
        xhat, _ = _rms(h1)
        n2_ref[...] = (xhat * g_ref[...]).astype(BF16)

    tile = lambda j: pl.BlockSpec((SEQ_TILE, D_MODEL), lambda i: (i, j))
    sq = _resident((D_MODEL, D_MODEL))
    vec = _const((1, D_MODEL))
    bf, f32 = jax.ShapeDtypeStruct((t, D_MODEL), BF16), jax.ShapeDtypeStruct((t, D_MODEL), F32)
    return _fused_call(
        body, jobs, name="fwd_sgu_merge", grid=(t // SEQ_TILE,),
        in_specs=[tile(0), tile(2), tile(3), pl.BlockSpec((SEQ_TILE, 2 * D_MODEL), lambda i: (i, 2)), tile(0), vec, vec,
                  _const((GROUPS, CHUNK, CHUNK)), _const((CHUNK, D_MODEL)), sq, sq, sq, vec],
        out_specs=[tile(0)] * 5,
        out_shape=[bf, f32, f32, f32, bf],
        compiler_params=_params(),
    )(ya, z, z, z, x, ln_g, ln_b, w_s, bias_full, w_oa, w_ob, w_out, g2)


def _mlp(n2, h1, target, w_up_st, w_down, g2, g3, jobs=()):
    t = n2.shape[0]

    def body(n2_ref, h1_ref, tgt_ref, wup_ref, wdown_ref, g2_ref, g3_ref, act_ref, dup_ref, dh2b_ref, dh1_ref,
             loss_ref, dg3_ref, dg2_ref, relu_ref):
        @pl.when(pl.program_id(0) == 0)
        def _():
            for ref in (loss_ref, dg3_ref, dg2_ref):
                ref[...] = jnp.zeros_like(ref)

        n2 = n2_ref[...]
        h1 = h1_ref[...]
        h2 = h1
        for k in range(N_CHIPS):
            cols = slice(k * D_MODEL, (k + 1) * D_MODEL)
            r = jnp.maximum(_dot(n2, wup_ref[k]), 0.0)
            relu_ref[:, cols] = r
            act = (r * r).astype(BF16)
            act_ref[:, cols] = act
            h2 = h2 + _dot(act, wdown_ref[cols, :])
        xhat, r3 = _rms(h2)
        diff = xhat * g3_ref[...] - tgt_ref[...]
        sq = jnp.sum(diff * diff, axis=1, keepdims=True)
        loss_ref[...] = loss_ref[...] + (0.5 / D_MODEL) * jnp.sum(sq, axis=0, keepdims=True)
        dy = diff * (1.0 / D_MODEL)
        dg3_ref[...] = dg3_ref[...] + _col_sum(dy * xhat)
        dh2 = _rms_bwd(dy * g3_ref[...], xhat, r3)
        dh2b = dh2.astype(BF16)
        dh2b_ref[...] = dh2b
        dn2 = jnp.zeros((SEQ_TILE, D_MODEL), F32)
        for k in range(N_CHIPS):
            cols = slice(k * D_MODEL, (k + 1) * D_MODEL)
            dup = (_dot_nt(dh2b, wdown_ref[cols, :]) * (2.0 * relu_ref[:, cols])).astype(BF16)
            dup_ref[:, cols] = dup
            dn2 = dn2 + _dot_nt(dup, wup_ref[k])
        xhat, r2 = _rms(h1)
        dg2_ref[...] = dg2_ref[...] + _col_sum(dn2 * xhat)
        dh1_ref[...] = dh2 + _rms_bwd(dn2 * g2_ref[...], xhat, r2)

    tile = pl.BlockSpec((SEQ_TILE, D_MODEL), lambda i: (i, 0))
    wide = pl.BlockSpec((SEQ_TILE, D_FF), lambda i: (i, 0))
    vec = _const((1, D_MODEL))
    vec_shape = jax.ShapeDtypeStruct((1, D_MODEL), F32)
    return _fused_call(
        body, jobs, name="mlp", grid=(t // SEQ_TILE,),
        in_specs=[tile, tile, tile, _resident((N_CHIPS, D_MODEL, D_MODEL)), _resident((D_FF, D_MODEL)), vec, vec],
        out_specs=[wide, wide, tile, tile, _const((SUBLANES, 128)), vec, vec],
        out_shape=[jax.ShapeDtypeStruct((t, D_FF), BF16), jax.ShapeDtypeStruct((t, D_FF), BF16),
                   jax.ShapeDtypeStruct((t, D_MODEL), BF16), jax.ShapeDtypeStruct((t, D_MODEL), F32),
                   jax.ShapeDtypeStruct((SUBLANES, 128), F32), vec_shape, vec_shape],
        scratch_shapes=[pltpu.VMEM((SEQ_TILE, D_FF), F32)],
        compiler_params=_params(),
    )(n2, h1, target, w_up_st, w_down, g2, g3)


def _bwd_mix(dh1, pa, pb, z, h, xc, r, ig, w_oa, w_ob, w_out, ln_g, ln_b, w_s, bias_full, conv_w, wr, wi, lam, jobs=()):
    t = dh1.shape[0]
    n_tiles = t // SEQ_TILE
    per_tile = SEQ_TILE // SUBLANES

    def merge_part(dh1_ref, pa_ref, pb_ref, m_ref, woa_ref, wob_ref, wout_ref, dz_ref, dya_ref, dyb_ref, mg_ref,
                   dpa_ref, dpb_ref, dh1b_ref):
        dh1b = dh1_ref[...].astype(BF16)
        dh1b_ref[...] = dh1b
        dm = _dot_nt(dh1b, wout_ref[...])
        pa = pa_ref[...]
        pb = pb_ref[...]
        sa = jax.nn.sigmoid(m_ref[:, :D_MODEL])
        sb = jax.nn.sigmoid(m_ref[:, D_MODEL:])
        mg_ref[...] = (sa * pa + sb * pb).astype(BF16)
        dz_ref[:, :D_MODEL] = (dm * pa * sa * (1.0 - sa)).astype(BF16)
        dz_ref[:, D_MODEL:] = (dm * pb * sb * (1.0 - sb)).astype(BF16)
        dpa = (dm * sa).astype(BF16)
        dpb = (dm * sb).astype(BF16)
        dpa_ref[...] = dpa
        dpb_ref[...] = dpb
        dya_ref[...] = _dot_nt(dpa, woa_ref[...])
        dyb_ref[...] = _dot_nt(dpb, wob_ref[...])

    def sgu_part(dyb_ref, ub_ref, vb_ref, lg_ref, lb_ref, ws_ref, bias_ref, dz_ref, dlg_ref, dlb_ref, dws_ref, dbs_ref,
                 dvn_ref, dsp_acc):
        i = pl.program_id(0)

        @pl.when(i == 0)
        def _():
            dlg_ref[...] = jnp.zeros_like(dlg_ref)
            dlb_ref[...] = jnp.zeros_like(dlb_ref)
            dws_ref[...] = jnp.zeros_like(dws_ref)
            dsp_acc[...] = jnp.zeros_like(dsp_acc)

        u, du, dvg, rstd, vhat, vn = _sgu_forward_parts(ub_ref[...], vb_ref[...], lg_ref, lb_ref)
        dyb = dyb_ref[...]
        mask = _causal_mask()
        wm = [jnp.where(mask, ws_ref[g], 0.0).astype(BF16) for g in range(GROUPS)]
        for c in range(SEQ_TILE // CHUNK):
            rows = slice(c * CHUNK, (c + 1) * CHUNK)
            for g in range(GROUPS):
                cols = slice(g * GROUP_DIM, (g + 1) * GROUP_DIM)
                vn_blk = vn[rows, cols]
                sp = _dot(wm[g], vn_blk) + bias_ref[:, cols]
                dyb_blk = dyb[rows, cols]
                dz_ref[rows, cols] = (dyb_blk * sp * du[rows, cols]).astype(BF16)
                dsp = dyb_blk * u[rows, cols]
                dsp_acc[:, cols] = dsp_acc[:, cols] + dsp
                dspb = dsp.astype(BF16)
                dvn_ref[rows, cols] = _dot_tn(wm[g], dspb)
                wcols = slice(g * CHUNK, (g + 1) * CHUNK)
                dws_ref[:, wcols] = dws_ref[:, wcols] + jnp.where(mask, _dot_nt(dspb, vn_blk), 0.0)
        dvn = dvn_ref[...]
        dlg_ref[...] = dlg_ref[...] + _col_sum(dvn * vhat)
        dlb_ref[...] = dlb_ref[...] + _col_sum(dvn)
        dvhat = dvn * lg_ref[...]
        dvgel = rstd * (dvhat - jnp.mean(dvhat, axis=-1, keepdims=True)
                        - vhat * jnp.mean(dvhat * vhat, axis=-1, keepdims=True))
        dz_ref[:, D_MODEL:] = (dvgel * dvg).astype(BF16)

        @pl.when(i == n_tiles - 1)
        def _():
            lane = lax.broadcasted_iota(jnp.int32, (CHUNK, 128), 1)
            out = jnp.zeros((CHUNK, 128), F32)
            for g in range(GROUPS):
                s = jnp.sum(dsp_acc[:, g * GROUP_DIM:(g + 1) * GROUP_DIM], axis=1, keepdims=True)
                out = out + jnp.where(lane == g, s, 0.0)
            dbs_ref[...] = out

    def lru_part(dya_ref, xa_ref, ga_ref, h_ref, h_prev_ref, xc_ref, r_ref, ig_ref, cw_ref, wr_ref, wi_ref, lam_ref,
                 dz_ref, dcw_ref, dcb_ref, dwr_ref, dbr_ref, dwi_ref, dbi_ref, dlam_ref, lam_carry, dxc_head):
        i = pl.program_id(0)

        @pl.when(i == 0)
        def _():
            for ref in (dcw_ref, dcb_ref, dwr_ref, dbr_ref, dwi_ref, dbi_ref, dlam_ref, lam_carry, dxc_head):
                ref[...] = jnp.zeros_like(ref)

        first_tile = i == n_tiles - 1
        h_tail = jnp.where(first_tile, 0.0, h_prev_ref[...])
        xc, r, ig = xc_ref[...], r_ref[...], ig_ref[...]
        xcb = xc.astype(BF16)
        sp, a, mult = _decay(r, lam_ref)
        h = h_ref[...]
        h_prev = _shift_down(h, h_tail, 1)
        dya = dya_ref[...]
        gg, dgg = _gelu_and_grad(ga_ref[...])
        dz_ref[:, D_MODEL:] = (dya * h * dgg).astype(BF16)
        ones = jnp.ones((SUBLANES, D_MODEL), F32)
        lam_t, lam_first = _scan_backward(_shift_up(a, ones, 1), dya * gg, lam_carry[...])
        lam_carry[...] = a[0:1] * lam_first
        dmult = lam_t * xc * ig
        dla = lam_t * h_prev * a - dmult * (a * a) / mult
        dr = dla * ((-LRU_C) * sp)
        dlam_ref[...] = dlam_ref[...] + _col_sum(dla * r) * (LRU_C * jax.nn.sigmoid(-lam_ref[...]))
        dpr = dr * r * (1.0 - r)
        dpi = lam_t * xc * mult * ig * (1.0 - ig)
        dbr_ref[...] = dbr_ref[...] + _col_sum(dpr)
        dbi_ref[...] = dbi_ref[...] + _col_sum(dpi)
        dprb = dpr.astype(BF16)
        dpib = dpi.astype(BF16)
        dxc_gate = []
        for hd in range(HEADS):
            cols = slice(hd * HEAD_DIM, (hd + 1) * HEAD_DIM)
            dxc_gate.append(_dot_nt(dprb[:, cols], wr_ref[hd]) + _dot_nt(dpib[:, cols], wi_ref[hd]))
            dwr_ref[hd] = dwr_ref[hd] + _dot_tn(xcb[:, cols], dprb[:, cols])
            dwi_ref[hd] = dwi_ref[hd] + _dot_tn(xcb[:, cols], dpib[:, cols])
        dxc = lam_t * ig * mult + jnp.concatenate(dxc_gate, axis=1)
        dcb_ref[...] = dcb_ref[...] + _col_sum(dxc)
        cw = cw_ref[...]
        head = dxc_head[...]
        xa = xa_ref[...]
        dxa = cw[0:1] * dxc
        dcw_ref[0:1, :] = dcw_ref[0:1, :] + _col_sum(dxc * xa)
        for k in range(1, CONV_WIDTH):
            dxc_k = _shift_up(dxc, head, k)
            dxa = dxa + cw[k:k + 1] * dxc_k
            dcw_ref[k:k + 1, :] = dcw_ref[k:k + 1, :] + _col_sum(dxc_k * xa)
        dxc_head[...] = dxc[0:SUBLANES]
        dz_ref[:, :D_MODEL] = dxa.astype(BF16)

    def body(dh1_ref, pa_ref, pb_ref, z_ref, h_ref, h_prev_ref, xc_ref, r_ref, ig_ref, woa_ref, wob_ref, wout_ref,
             lg_ref, lb_ref, ws_ref, bias_ref, cw_ref, wr_ref, wi_ref, lam_ref, dz_ref, mg_ref, dpa_ref, dpb_ref,
             dh1b_ref, dlg_ref, dlb_ref, dws_ref, dbs_ref, dcw_ref, dcb_ref, dwr_ref, dbr_ref, dwi_ref, dbi_ref,
             dlam_ref, dya_ref, dyb_ref, dvn_ref, dsp_acc, lam_carry, dxc_head):
        def cols(ref, first, count):
            return ref.at[:, pl.ds(first * D_MODEL, count * D_MODEL)]

        merge_part(dh1_ref, pa_ref, pb_ref, cols(z_ref, 4, 2), woa_ref, wob_ref, wout_ref, cols(dz_ref, 4, 2), dya_ref,
                   dyb_ref, mg_ref, dpa_ref, dpb_ref, dh1b_ref)
        sgu_part(dyb_ref, cols(z_ref, 2, 1), cols(z_ref, 3, 1), lg_ref, lb_ref, ws_ref, bias_ref, cols(dz_ref, 2, 2),
                 dlg_ref, dlb_ref, dws_ref, dbs_ref, dvn_ref, dsp_acc)
        lru_part(dya_ref, cols(z_ref, 0, 1), cols(z_ref, 1, 1), h_ref, h_prev_ref, xc_ref, r_ref, ig_ref, cw_ref, wr_ref,
                 wi_ref, lam_ref, cols(dz_ref, 0, 2), dcw_ref, dcb_ref, dwr_ref, dbr_ref, dwi_ref, dbi_ref, dlam_ref,
                 lam_carry, dxc_head)

    rev = lambda i: n_tiles - 1 - i
    tile = pl.BlockSpec((SEQ_TILE, D_MODEL), lambda i: (rev(i), 0))
    row = pl.BlockSpec((SEQ_TILE, D_IN), lambda i: (rev(i), 0))
    prev8 = pl.BlockSpec((SUBLANES, D_MODEL), lambda i: (jnp.maximum(rev(i) * per_tile - 1, 0), 0))
    vec = _const((1, D_MODEL))
    sq = _resident((D_MODEL, D_MODEL))
    gate_w = _resident((HEADS, HEAD_DIM, HEAD_DIM))
    gate_acc = _const((HEADS, HEAD_DIM, HEAD_DIM))
    vec_shape = jax.ShapeDtypeStruct((1, D_MODEL), F32)
    gate_shape = jax.ShapeDtypeStruct((HEADS, HEAD_DIM, HEAD_DIM), F32)
    act_bf = jax.ShapeDtypeStruct((t, D_MODEL), BF16)
    return _fused_call(
        body, jobs, name="bwd_mix", grid=(n_tiles,),
        in_specs=[tile, tile, tile, row, tile, prev8, tile, tile, tile, sq, sq, sq, vec, vec,
                  _const((GROUPS, CHUNK, CHUNK)), _const((CHUNK, D_MODEL)), _const((CONV_WIDTH, D_MODEL)), gate_w, gate_w,
                  vec],
        out_specs=[row, tile, tile, tile, tile, vec, vec, _const((CHUNK, GROUPS * CHUNK)), _const((CHUNK, 128)),
                   _const((SUBLANES, D_MODEL)), vec, gate_acc, vec, gate_acc, vec, vec],
        out_shape=[jax.ShapeDtypeStruct((t, D_IN), BF16), act_bf, act_bf, act_bf, act_bf, vec_shape, vec_shape,
                   jax.ShapeDtypeStruct((CHUNK, GROUPS * CHUNK), F32), jax.ShapeDtypeStruct((CHUNK, 128), F32),
                   jax.ShapeDtypeStruct((SUBLANES, D_MODEL), F32), vec_shape, gate_shape, vec_shape, gate_shape,
                   vec_shape, vec_shape],
        scratch_shapes=[pltpu.VMEM((SEQ_TILE, D_MODEL), F32), pltpu.VMEM((SEQ_TILE, D_MODEL), F32),
                        pltpu.VMEM((SEQ_TILE, D_MODEL), F32), pltpu.VMEM((CHUNK, D_MODEL), F32),
                        pltpu.VMEM((1, D_MODEL), F32), pltpu.VMEM((SUBLANES, D_MODEL), F32)],
        compiler_params=_params(),
    )(dh1, pa, pb, z, h, h, xc, r, ig, w_oa, w_ob, w_out, ln_g, ln_b, w_s, bias_full, conv_w, wr, wi, lam)


def _bwd_in(dz, x, dh1, w_in_st, g1, jobs=()):
    t = x.shape[0]

    def body(dz_ref, x_ref, dh1_ref, w_ref, g_ref, dx_ref, dg1_ref):
        @pl.when(pl.program_id(0) == 0)
        def _():
            dg1_ref[...] = jnp.zeros_like(dg1_ref)

        dn1 = jnp.zeros((MM_TILE, D_MODEL), F32)
        for k in range(N_CHIPS):
            dn1 = dn1 + _dot_nt(dz_ref[:, k * IN_SHARD:(k + 1) * IN_SHARD], w_ref[k])
        xhat, r1 = _rms(x_ref[...])
        dg1_ref[...] = dg1_ref[...] + _col_sum(dn1 * xhat)
        dx_ref[...] = dh1_ref[...] + _rms_bwd(dn1 * g_ref[...], xhat, r1)

    tile = pl.BlockSpec((MM_TILE, D_MODEL), lambda i: (i, 0))
    return _fused_call(
        body, jobs, name="bwd_in", grid=(t // MM_TILE,),
        in_specs=[pl.BlockSpec((MM_TILE, D_IN), lambda i: (i, 0)), tile, tile,
                  _resident((N_CHIPS, D_MODEL, IN_SHARD)), _const((1, D_MODEL))],
        out_specs=[tile, _const((1, D_MODEL))],
        out_shape=[jax.ShapeDtypeStruct((t, D_MODEL), F32), jax.ShapeDtypeStruct((1, D_MODEL), F32)],
        compiler_params=_params(),
    )(dz, x, dh1, w_in_st, g1)


def _weight_grad(name, a, b, n_blocks, a_varies, b_varies, width, jobs=()):
    t = a.shape[0]
    rows = min(DW_TILE, t)
    n_t = t // rows

    def body(a_ref, b_ref, o_ref, acc_ref):
        s = pl.program_id(1)
        part = _dot_tn(a_ref[...], b_ref[...])

        @pl.when(s == 0)
        def _():
            acc_ref[...] = part

        @pl.when(s > 0)
        def _():
            acc_ref[...] = acc_ref[...] + part

        @pl.when(s == n_t - 1)
        def _():
            o_ref[...] = acc_ref[...].astype(BF16)

    return _fused_call(
        body, jobs, name=name, grid=(n_blocks, n_t),
        in_specs=[pl.BlockSpec((rows, D_MODEL), (lambda j, s: (s, j)) if a_varies else (lambda j, s: (s, 0))),
                  pl.BlockSpec((rows, width), (lambda j, s: (s, j)) if b_varies else (lambda j, s: (s, 0)))],
        out_specs=pl.BlockSpec((None, D_MODEL, width), lambda j, s: (j, 0, 0)),
        out_shape=jax.ShapeDtypeStruct((n_blocks, D_MODEL, width), BF16),
        scratch_shapes=[pltpu.VMEM((D_MODEL, width), F32)],
        compiler_params=_params(2),
    )(a, b)


def _weight_grads_square(name, pairs, jobs=()):
    n = len(pairs)
    t = pairs[0][0].shape[0]
    rows = min(2 * MM_TILE, t)
    n_t = t // rows

    def body(*refs):
        ins, outs, accs = refs[:2 * n], refs[2 * n:3 * n], refs[3 * n:]
        s = pl.program_id(0)
        for k in range(n):
            part = _dot_tn(ins[2 * k][...], ins[2 * k + 1][...])

            @pl.when(s == 0)
            def _(k=k, part=part):
                accs[k][...] = part

            @pl.when(s > 0)
            def _(k=k, part=part):
                accs[k][...] = accs[k][...] + part

            @pl.when(s == n_t - 1)
            def _(k=k):
                outs[k][...] = accs[k][...].astype(BF16)

    tile = pl.BlockSpec((rows, D_MODEL), lambda s: (s, 0))
    return _fused_call(
        body, jobs, name=name, grid=(n_t,), in_specs=[tile] * (2 * n), out_specs=[_const((D_MODEL, D_MODEL))] * n,
        out_shape=[jax.ShapeDtypeStruct((D_MODEL, D_MODEL), BF16)] * n,
        scratch_shapes=[pltpu.VMEM((D_MODEL, D_MODEL), F32)] * n,
        compiler_params=_params(),
    )(*[x for pair in pairs for x in pair])


def _place():
    x, y, c = lax.axis_index("x"), lax.axis_index("y"), lax.axis_index("c")
    other_chips = [(1 - x, y), (x, 1 - y), (1 - x, 1 - y)]
    return x, y, c, other_chips


def _chip_index(px, py):
    return 2 * px + py


ANY = pl.BlockSpec(memory_space=pl.ANY)
SIBLING = ((0, 0, 1),)
NEIGHBOURS = ((1, 0, 0), (0, 1, 0))
OTHER_CHIPS = NEIGHBOURS + ((1, 1, 0),)


def _near_far(x, y, c):
    return (x ^ (1 - c), y ^ c), (x ^ c, y ^ (1 - c))


def _gather_near_job(shards):
    n = len(shards)
    halves = [s.shape[0] // 2 for s in shards]

    def copies(ins, outs, send, recv, local):
        x, y, c, _ = _place()
        near, _ = _near_far(x, y, c)

        def block(w, chip, pc):
            return outs[w].at[_chip_index(*chip), pl.ds(pc * halves[w], halves[w]), :]

        def copy(w, k, chip, pc, to, src=None):
            return pltpu.make_async_remote_copy(
                src_ref=block(w, chip, pc) if src is None else src, dst_ref=block(w, chip, pc),
                send_sem=send.at[2 * w + k], recv_sem=recv.at[2 * w + k], device_id=to, device_id_type=MESH)

        sends, arrivals, own = [], [], []
        for w in range(n):
            src = ins[w].at[pl.ds(c * halves[w], halves[w]), :]
            own.append(pltpu.make_async_copy(src, block(w, (x, y), c), local.at[w]))
            sends += [copy(w, 0, (x, y), c, (*near, c), src), copy(w, 1, (x, y), c, (x, y, 1 - c), src)]
            arrivals += [copy(w, 0, near, c, (x, y, c)), copy(w, 1, (x, y), 1 - c, (x, y, c))]
        return sends, arrivals, own

    return _Job(shards, [jax.ShapeDtypeStruct((N_CHIPS,) + s.shape, s.dtype) for s in shards], 2 * n, copies,
                NEIGHBOURS + SIBLING, n_local=n)


def _gather_far_job(stacked):
    n = len(stacked)
    halves = [s.shape[1] // 2 for s in stacked]

    def copies(ins, outs, send, recv, local):
        del ins, local
        x, y, c, _ = _place()
        near, far = _near_far(x, y, c)

        def copy(w, k, chip):
            blk = outs[w].at[_chip_index(*chip), pl.ds(c * halves[w], halves[w]), :]
            return pltpu.make_async_remote_copy(
                src_ref=blk, dst_ref=blk, send_sem=send.at[2 * w + k], recv_sem=recv.at[2 * w + k],
                device_id=(*far, c), device_id_type=MESH)

        sends = [copy(w, k, chip) for w in range(n) for k, chip in enumerate(((x, y), near))]
        arrivals = [copy(w, k, chip) for w in range(n) for k, chip in enumerate((far, (1 - x, 1 - y)))]
        return sends, arrivals, []

    return _Job(stacked, [jax.ShapeDtypeStruct(s.shape, s.dtype) for s in stacked], 2 * n, copies, NEIGHBOURS,
                aliases={w: w for w in range(n)})


def _gather_pass_job(stacked):
    n = len(stacked)
    halves = [s.shape[1] // 2 for s in stacked]

    def copies(ins, outs, send, recv, local):
        del ins, local
        x, y, c, chips = _place()

        def copy(w, j, chip, pc, to):
            blk = outs[w].at[_chip_index(*chip), pl.ds(pc * halves[w], halves[w]), :]
            return pltpu.make_async_remote_copy(
                src_ref=blk, dst_ref=blk, send_sem=send.at[3 * w + j], recv_sem=recv.at[3 * w + j], device_id=to,
                device_id_type=MESH)

        sends = [copy(w, j, chip, c, (x, y, 1 - c)) for w in range(n) for j, chip in enumerate(chips)]
        arrivals = [copy(w, j, chip, 1 - c, (x, y, c)) for w in range(n) for j, chip in enumerate(chips)]
        return sends, arrivals, []

    return _Job(stacked, [jax.ShapeDtypeStruct(s.shape, s.dtype) for s in stacked], 3 * n, copies, SIBLING,
                aliases={w: w for w in range(n)})


def _gather_small_job(block):
    def copies(ins, outs, send, recv, local):
        x, y, c, chips = _place()

        def copy(j, chip_from, to):
            return pltpu.make_async_remote_copy(
                src_ref=ins[0], dst_ref=outs[0].at[_chip_index(*chip_from)], send_sem=send.at[j],
                recv_sem=recv.at[j], device_id=to, device_id_type=MESH)

        own = [pltpu.make_async_copy(ins[0], outs[0].at[_chip_index(x, y)], local.at[0])]
        sends = [copy(j, (x, y), (*chip, c)) for j, chip in enumerate(chips)]
        arrivals = [copy(j, chip, (x, y, c)) for j, chip in enumerate(chips)]
        return sends, arrivals, own

    return _Job([block], [jax.ShapeDtypeStruct((N_CHIPS,) + block.shape, block.dtype)], 3, copies, OTHER_CHIPS,
                n_local=1)


def _pair_send_job(grads):
    n = len(grads)
    halves = [g.shape[1] // 2 for g in grads]

    def copies(ins, outs, send, recv, local):
        del local
        x, y, c, _ = _place()
        sends = [pltpu.make_async_remote_copy(
            src_ref=ins[w].at[:, pl.ds((1 - c) * halves[w], halves[w]), :], dst_ref=outs[w], send_sem=send.at[w],
            recv_sem=recv.at[w], device_id=(x, y, 1 - c), device_id_type=MESH) for w in range(n)]
        return sends, sends, []

    return _Job(grads, [jax.ShapeDtypeStruct((N_CHIPS, h, g.shape[2]), g.dtype) for g, h in zip(grads, halves)], n,
                copies, SIBLING)


def _row_block(rows, limit=256):
    return min(rows, limit)


def _pair_add(name, core, mine, theirs):
    _, _, h, cols = mine.shape
    rb = _row_block(h, 512)

    def body(core_ref, a_ref, b_ref, o_ref):
        del core_ref
        o_ref[...] = (a_ref[...].astype(F32) + b_ref[...].astype(F32)).astype(BF16)

    return pl.pallas_call(
        body, name=name,
        grid_spec=pltpu.PrefetchScalarGridSpec(
            num_scalar_prefetch=1, grid=(N_CHIPS, h // rb),
            in_specs=[pl.BlockSpec((None, None, rb, cols), lambda k, r, core_ref: (k, core_ref[0], r, 0)),
                      pl.BlockSpec((None, rb, cols), lambda k, r, core_ref: (k, r, 0))],
            out_specs=pl.BlockSpec((None, rb, cols), lambda k, r, core_ref: (k, r, 0))),
        out_shape=jax.ShapeDtypeStruct(theirs.shape, BF16),
        compiler_params=_params(2),
    )(core, mine, theirs)


def _sequencer_call(name, collective_id, job):
    steps, peers = job.phases, job.peers
    ins = [jax.new_ref(a, memory_space=pltpu.MemorySpace.HBM) for a in job.inputs]
    outs = [ins[{o: i for i, o in job.aliases.items()}[k]] if k in job.aliases.values()
            else jax.empty_ref(shape, memory_space=pltpu.MemorySpace.HBM) for k, shape in enumerate(job.out_shape)]
    sems = [pltpu.SemaphoreType.DMA((n,)) for step in steps for n in (step.n_sem, step.n_sem, max(step.n_local, 1))]

    @pl.kernel(mesh=plsc.ScalarSubcoreMesh(axis_name="sequencer", num_cores=1), name=name, scratch_types=tuple(sems),
               compiler_params=pltpu.CompilerParams(collective_id=collective_id))
    def launch(*sem_refs):
        x, y, c, _ = _place()
        barrier = pltpu.get_barrier_semaphore()
        for dx, dy, dc in peers:
            pl.semaphore_signal(barrier, inc=1, device_id=(x ^ dx, y ^ dy, c ^ dc), device_id_type=MESH)
        pl.semaphore_wait(barrier, len(peers))
        for k, step in enumerate(steps):
            sends, arrivals, own = step.copies(ins if k == 0 else outs, outs, *sem_refs[3 * k:3 * k + 3])
            for cp in own + sends:
                cp.start()
            for cp in arrivals:
                cp.wait_recv()
            for cp in sends:
                cp.wait_send()
            for cp in own:
                cp.wait()

    launch()
    return [ref[...] for ref in outs]


def _chip_exchange_job(sums):
    n = len(sums)

    def copies(ins, outs, send, recv, local):
        del local
        _, _, c, chips = _place()
        sends = [pltpu.make_async_remote_copy(
            src_ref=ins[w].at[_chip_index(*chip)], dst_ref=outs[w].at[j], send_sem=send.at[3 * w + j],
            recv_sem=recv.at[3 * w + j], device_id=(*chip, c), device_id_type=MESH)
            for w in range(n) for j, chip in enumerate(chips)]
        return sends, sends, []

    return _Job(sums, [jax.ShapeDtypeStruct((N_CHIPS - 1,) + s.shape[1:], s.dtype) for s in sums], 3 * n, copies,
                OTHER_CHIPS)


def _chip_sum(name, place, mine, theirs):
    _, h, cols = mine.shape
    rb = _row_block(h, 512)

    def body(place_ref, p_ref, q_ref, o_ref):
        del place_ref
        acc = p_ref[...].astype(F32)
        for j in range(N_CHIPS - 1):
            acc = acc + q_ref[j].astype(F32)
        o_ref[...] = acc

    return pl.pallas_call(
        body, name=name,
        grid_spec=pltpu.PrefetchScalarGridSpec(
            num_scalar_prefetch=1, grid=(h // rb,),
            in_specs=[pl.BlockSpec((None, rb, cols), lambda r, place_ref: (place_ref[0], r, 0)),
                      pl.BlockSpec((N_CHIPS - 1, rb, cols), lambda r, place_ref: (0, r, 0))],
            out_specs=pl.BlockSpec((None, rb, cols), lambda r, place_ref: (place_ref[1], r, 0))),
        out_shape=jax.ShapeDtypeStruct((2, h, cols), F32),
        compiler_params=_params(),
    )(place, mine, theirs)


def _share_job(bufs):
    n = len(bufs)

    def copies(ins, outs, send, recv, local):
        del ins, local
        x, y, c, _ = _place()

        def copy(w, half):
            return pltpu.make_async_remote_copy(
                src_ref=outs[w].at[half], dst_ref=outs[w].at[half], send_sem=send.at[w], recv_sem=recv.at[w],
                device_id=(x, y, 1 - c), device_id_type=MESH)

        return [copy(w, c) for w in range(n)], [copy(w, 1 - c) for w in range(n)], []

    return _Job(bufs, [jax.ShapeDtypeStruct(b.shape, b.dtype) for b in bufs], n, copies, SIBLING,
                aliases={w: w for w in range(n)})


SMALL_ROWS = 24
ROW_G1, ROW_CW, ROW_CB, ROW_BR, ROW_BI, ROW_LAM, ROW_LG, ROW_LB, ROW_G2, ROW_G3, ROW_LOSS, ROW_BS = (
    0, 1, 5, 6, 7, 8, 9, 10, 11, 12, 13, 16)
N_DEV = 8


def _pack_small(dcw, dcb, dbr, dbi, dlam, dlg, dlb, dg2, dg3, loss, dbs):
    def body(dcw_ref, dcb_ref, dbr_ref, dbi_ref, dlam_ref, dlg_ref, dlb_ref, dg2_ref, dg3_ref, loss_ref, dbs_ref, out):
        out[...] = jnp.zeros((SMALL_ROWS, D_MODEL), F32)
        for row, ref in ((ROW_CB, dcb_ref), (ROW_BR, dbr_ref), (ROW_BI, dbi_ref), (ROW_LAM, dlam_ref),
                         (ROW_LG, dlg_ref), (ROW_LB, dlb_ref), (ROW_G2, dg2_ref), (ROW_G3, dg3_ref)):
            out[row:row + 1, :] = ref[...]
        out[ROW_CW:ROW_CW + CONV_WIDTH, :] = dcw_ref[0:CONV_WIDTH, :]
        out[ROW_LOSS:ROW_LOSS + 1, 0:128] = loss_ref[0:1, :]
        out[ROW_BS:ROW_BS + GROUPS, 0:128] = jnp.transpose(dbs_ref[...])[0:GROUPS, :]

    vm = pl.BlockSpec(memory_space=pltpu.VMEM)
    return pl.pallas_call(
        body, name="pack_small", in_specs=[vm] * 11, out_specs=vm,
        out_shape=jax.ShapeDtypeStruct((SMALL_ROWS, D_MODEL), F32),
    )(dcw, dcb, dbr, dbi, dlam, dlg, dlb, dg2, dg3, loss, dbs)


def _gather_all_job(blocks):
    n = len(blocks)
    flips = [(dx, dy, dc) for dx in (0, 1) for dy in (0, 1) for dc in (0, 1)][1:]

    def copies(ins, outs, send, recv, local):
        x, y, c, _ = _place()
        me = 4 * x + 2 * y + c
        sends, arrivals, own = [], [], []
        for w in range(n):
            own.append(pltpu.make_async_copy(ins[w], outs[w].at[me], local.at[w]))
            for k, (dx, dy, dc) in enumerate(flips):
                peer = (x ^ dx, y ^ dy, c ^ dc)
                sem = dict(send_sem=send.at[7 * w + k], recv_sem=recv.at[7 * w + k])
                sends.append(pltpu.make_async_remote_copy(
                    src_ref=ins[w], dst_ref=outs[w].at[me], device_id=peer, device_id_type=MESH, **sem))
                arrivals.append(pltpu.make_async_remote_copy(
                    src_ref=ins[w], dst_ref=outs[w].at[4 * peer[0] + 2 * peer[1] + peer[2]], device_id=peer,
                    device_id_type=MESH, **sem))
        return sends, arrivals, own

    return _Job(blocks, [jax.ShapeDtypeStruct((N_DEV,) + b.shape, b.dtype) for b in blocks], 7 * n, copies,
                OTHER_CHIPS + SIBLING + tuple((dx, dy, 1) for dx, dy, _ in OTHER_CHIPS), n_local=n)


def _sum_small(vec_all, ws_all, dg1_all):
    def body(vec_ref, ws_ref, dg1_ref, vec_out, ws_out):
        vec, ws, dg1 = vec_ref[0], ws_ref[0], dg1_ref[0]
        for d in range(1, N_DEV):
            vec, ws, dg1 = vec + vec_ref[d], ws + ws_ref[d], dg1 + dg1_ref[d]
        vec_out[...] = vec
        vec_out[ROW_G1:ROW_G1 + 1, :] = dg1
        ws_out[...] = ws

    vm = pl.BlockSpec(memory_space=pltpu.VMEM)
    return pl.pallas_call(
        body, name="sum_small", in_specs=[vm] * 3, out_specs=[vm, vm],
        out_shape=[jax.ShapeDtypeStruct(vec_all.shape[1:], F32), jax.ShapeDtypeStruct(ws_all.shape[1:], F32)],
    )(vec_all, ws_all, dg1_all)


def _adamw_math(w, g, m, v):
    m = ADAM_B1 * m + (1.0 - ADAM_B1) * g
    v = ADAM_B2 * v + (1.0 - ADAM_B2) * (g * g)
    m_hat = m / (1.0 - ADAM_B1 ** ADAM_STEP)
    v_hat = v / (1.0 - ADAM_B2 ** ADAM_STEP)
    delta = (-ADAM_LR) * (m_hat / (jnp.sqrt(v_hat) + ADAM_EPS) + ADAM_WD * w)
    return delta, m, v


def _adamw(name, g, w, m, v, jobs=()):
    rows, cols = w.shape
    rb = _row_block(rows)

    def body(g_ref, w_ref, m_ref, v_ref, d_ref, nm_ref, nv_ref):
        d_ref[...], nm_ref[...], nv_ref[...] = _adamw_math(w_ref[...], g_ref[...], m_ref[...], v_ref[...])

    blk = pl.BlockSpec((rb, cols), lambda r: (r, 0))
    return _fused_call(
        body, jobs, name=name, grid=(rows // rb,), in_specs=[blk] * 4, out_specs=[blk] * 3,
        out_shape=[jax.ShapeDtypeStruct(w.shape, F32)] * 3, compiler_params=_params(),
    )(g, w, m, v)


def _adamw_small(grads, ws, ms, vs):
    n = len(grads)

    def body(*refs):
        g_refs, w_refs, m_refs, v_refs = refs[:n], refs[n:2 * n], refs[2 * n:3 * n], refs[3 * n:4 * n]
        outs = refs[4 * n:]
        for p in range(n):
            d, nm, nv = _adamw_math(w_refs[p][...], g_refs[p][...], m_refs[p][...], v_refs[p][...])
            outs[p][...] = d
            outs[n + p][...] = nm
            outs[2 * n + p][...] = nv

    vm = pl.BlockSpec(memory_space=pltpu.VMEM)
    shapes = [jax.ShapeDtypeStruct(w.shape, F32) for w in ws]
    out = pl.pallas_call(
        body, name="adamw_small", in_specs=[vm] * (4 * n), out_specs=[vm] * (3 * n), out_shape=shapes * 3,
    )(*grads, *ws, *ms, *vs)
    return out[:n], out[n:2 * n], out[2 * n:]


def _unstack_heads(w_st):
    per = HEAD_DIM // N_CHIPS
    return w_st.reshape(N_CHIPS, HEADS, per, HEAD_DIM).transpose(1, 0, 2, 3).reshape(HEADS, HEAD_DIM, HEAD_DIM)


def _stack_heads(w):
    per = HEAD_DIM // N_CHIPS
    return w.reshape(HEADS, N_CHIPS, per, HEAD_DIM).transpose(1, 0, 2, 3).reshape(N_CHIPS, HEADS * per, HEAD_DIM)


def kernel(x, norm_mix_g, w_in, conv_w, conv_b, w_rgate, b_rgate, w_igate, b_igate, lru_lambda, w_out_a, sgu_ln_g, sgu_ln_b, sgu_w_s, sgu_b_s, w_out_b, w_out, norm_mlp_g, w_up, w_down, norm_final_g, loss_target, m_norm_mix_g, m_w_in, m_conv_w, m_conv_b, m_w_rgate, m_b_rgate, m_w_igate, m_b_igate, m_lru_lambda, m_w_out_a, m_sgu_ln_g, m_sgu_ln_b, m_sgu_w_s, m_sgu_b_s, m_w_out_b, m_w_out, m_norm_mlp_g, m_w_up, m_w_down, m_norm_final_g, v_norm_mix_g, v_w_in, v_conv_w, v_conv_b, v_w_rgate, v_b_rgate, v_w_igate, v_b_igate, v_lru_lambda, v_w_out_a, v_sgu_ln_g, v_sgu_ln_b, v_sgu_w_s, v_sgu_b_s, v_w_out_b, v_w_out, v_norm_mlp_g, v_w_up, v_w_down, v_norm_final_g):
    chip = _chip_index(lax.axis_index("x"), lax.axis_index("y"))
    core = lax.axis_index("c")
    quarter_h = HEAD_DIM // N_CHIPS
    quarter_d = D_MODEL // N_CHIPS

    as_2d = lambda a: a.reshape(-1, a.shape[-1])
    big_w = [as_2d(w) for w in (w_in, w_rgate, w_igate, w_out_a, w_out_b, w_out, w_up, w_down)]
    big_m = [as_2d(w) for w in (m_w_in, m_w_rgate, m_w_igate, m_w_out_a, m_w_out_b, m_w_out, m_w_up, m_w_down)]
    big_v = [as_2d(w) for w in (v_w_in, v_w_rgate, v_w_igate, v_w_out_a, v_w_out_b, v_w_out, v_w_up, v_w_down)]

    packed = jnp.concatenate([conv_w[0], b_rgate[0], b_igate[0]], axis=1)
    packed = jnp.concatenate([packed, jnp.zeros_like(packed)], axis=0)
    s_in, s_r, s_i, s_oa, s_ob, s_out, s_up, s_down = [w.astype(BF16) for w in big_w]
    xs, target = x[0], loss_target[0]
    g3 = norm_final_g.reshape(1, D_MODEL)
    bias_s = jnp.broadcast_to(jnp.transpose(sgu_b_s[0])[:, :, None], (CHUNK, GROUPS, GROUP_DIM)).reshape(CHUNK, D_MODEL)
    core_arr = core.reshape(1).astype(jnp.int32)
    place = jnp.stack([chip, core]).astype(jnp.int32)
    quarter = lambda g: g.reshape(N_CHIPS, D_MODEL // N_CHIPS, D_MODEL)

    def pair_add(nm, g, from_sibling):
        return _pair_add("pair_add_" + nm, core_arr, g.reshape(N_CHIPS, 2, g.shape[1] // 2, g.shape[2]), from_sibling)

    def chip_sum(nm, pair, from_chips):
        return _chip_sum("chip_sum_" + nm, place, pair, from_chips)

    order = jnp.stack([chip, chip ^ 2, chip ^ 1, chip ^ 3]).astype(jnp.int32)
    (z, n1, (w_in_st, wr_st, wi_st)), ((packed_all,), late) = _fwd_in(
        xs, norm_mix_g, [s_in, s_r, s_i], order,
        jobs=[_gather_small_job(packed), _gather_near_job([s_oa, s_ob, s_out])])
    pick = lambda lo, hi: packed_all[:, :HEADS, lo:hi].transpose(1, 0, 2).reshape(HEADS, -1)
    conv_w_full = pick(0, quarter_d)
    br_full = pick(quarter_d, quarter_d + quarter_h).reshape(1, D_MODEL)
    bi_full = pick(quarter_d + quarter_h, quarter_d + 2 * quarter_h).reshape(1, D_MODEL)
    wr, wi = _unstack_heads(wr_st), _unstack_heads(wi_st)
    lru = (conv_w_full, conv_b, wr, br_full, wi, bi_full, lru_lambda)
    sgu = (sgu_ln_g, sgu_ln_b, sgu_w_s[0], bias_s)

    after = lambda arrays, result: lax.optimization_barrier((arrays, result))[0]
    w_up_st, w_dn = _sequencer_call(
        "gather_mlp", 8, _gather_near_job(after([s_up, s_down], n1)).then(_gather_far_job).then(_gather_pass_job))
    w_dn = w_dn.reshape(D_FF, D_MODEL)
    late_step = (3 * (xs.shape[0] // SEQ_TILE) // 4,)
    (ya, *saved), (late,) = _fwd_lru(z, *lru, jobs=[_gather_far_job(late).then(_gather_pass_job, at=late_step)])
    w_oa, w_ob, w_o = [w.reshape(D_MODEL, D_MODEL) for w in late]
    (yb, pa, pb, h1, n2), _ = _fwd_sgu_merge(ya, z, xs, *sgu, w_oa, w_ob, w_o, norm_mlp_g)
    (act, dup, dh2b, dh1, loss_part, dg3, dg2), _ = _mlp(n2, h1, target, w_up_st, w_dn, norm_mlp_g, g3)

    d_down, _ = _weight_grad("dw_down", act, dh2b, N_CHIPS, True, False, D_MODEL)
    r_down, = _sequencer_call("send_w_down", 10, _pair_send_job([d_down]))
    d_up, _ = _weight_grad("dw_up", n2, dup, N_CHIPS, False, True, D_MODEL)
    r_up, = _sequencer_call("send_w_up", 11, _pair_send_job([d_up]))
    p_down, p_up = pair_add("w_down", d_down, r_down), pair_add("w_up", d_up, r_up)
    (dz, merged, dpa, dpb, dh1b, dlg, dlb, dws, dbs, dcw, dcb, dwr, dbr, dwi, dbi, dlam), ((q_up, q_down),) = _bwd_mix(
        dh1, pa, pb, z, *saved, w_oa, w_ob, w_o, *sgu, conv_w_full, wr, wi, lru_lambda,
        jobs=[_chip_exchange_job([p_up, p_down])])
    half_up, half_down = chip_sum("w_up", p_up, q_up), chip_sum("w_down", p_down, q_down)
    names = ("w_in", "w_rgate", "w_igate", "w_out_a", "w_out_b", "w_out", "w_up", "w_down")
    (d_out, d_oa, d_ob), ((full_up, full_down),) = _weight_grads_square(
        "dw_projections", [(merged, dh1b), (ya, dpa), (yb, dpb)], jobs=[_share_job([half_up, half_down])])
    mids = [quarter(d_oa), quarter(d_ob), quarter(d_out)]
    r_mids = _sequencer_call("send_mids", 1, _pair_send_job(mids))
    gates = [_stack_heads(dwr).astype(BF16), _stack_heads(dwi).astype(BF16)]
    small = _pack_small(dcw, dcb, dbr, dbi, dlam, dlg, dlb, dg2, dg3, loss_part, dbs)
    p_mids = [pair_add(nm, g, r) for nm, g, r in zip(names[3:6], mids, r_mids)]
    q_mids = _sequencer_call("exchange_mids", 2, _chip_exchange_job(p_mids))
    d_in, (r_gates, (vec_all, ws_all)) = _weight_grad(
        "dw_in", n1, dz, N_CHIPS, False, True, IN_SHARD,
        jobs=[_pair_send_job(gates), _gather_all_job([small, dws])])
    r_in, = _sequencer_call("send_w_in", 3, _pair_send_job([d_in]))
    adam_args = {nm: (w, m, v) for nm, w, m, v in zip(names, big_w, big_m, big_v)}

    def adamw(nm, g):
        w, m, v = adam_args[nm]
        g = g.reshape(w.shape)
        return g, _adamw("adamw_" + nm, g, w, m, v)[0]

    p_gates = [pair_add(nm, g, r) for nm, g, r in zip(names[1:3], gates, r_gates)]
    half_mids = [chip_sum(nm, p, q) for nm, p, q in zip(names[3:6], p_mids, q_mids)]
    full_mids = _sequencer_call("share_mids", 12, _share_job(half_mids))
    p_first = [pair_add("w_in", d_in, r_in)] + p_gates
    q_first = _sequencer_call("exchange_w_in", 4, _chip_exchange_job(p_first))
    (grad_x, dg1), _ = _bwd_in(dz, xs, dh1, w_in_st, norm_mix_g)
    dg1_all, = _sequencer_call("gather_dg1", 6, _gather_all_job([dg1]))
    done = {nm: adamw(nm, f) for nm, f in zip(("w_up", "w_down") + names[3:6], [full_up, full_down] + full_mids)}
    q_first = after(q_first, [out[0] for _, out in done.values()])
    half_first = [chip_sum(nm, p, q) for nm, p, q in zip(names[:3], p_first, q_first)]
    full_first = _sequencer_call("share_last", 5, _share_job(half_first))
    done.update({nm: adamw(nm, f) for nm, f in zip(names[:3], full_first)})
    full, big_out = [done[nm][0] for nm in names], [done[nm][1] for nm in names]

    vec, ws_sum = _sum_small(vec_all, ws_all, dg1_all)
    row = lambda r: vec[r:r + 1]
    shard = lambda a, width: lax.dynamic_slice_in_dim(a, chip * width, width, axis=1)
    g_small = dict(
        norm_mix_g=row(ROW_G1), conv_w=shard(vec[ROW_CW:ROW_CW + CONV_WIDTH], quarter_d), conv_b=row(ROW_CB),
        b_rgate=shard(row(ROW_BR).reshape(HEADS, HEAD_DIM), quarter_h),
        b_igate=shard(row(ROW_BI).reshape(HEADS, HEAD_DIM), quarter_h), lru_lambda=row(ROW_LAM),
        sgu_ln_g=row(ROW_LG), sgu_ln_b=row(ROW_LB),
        sgu_w_s=ws_sum.reshape(CHUNK, GROUPS, CHUNK).transpose(1, 0, 2).reshape(GROUPS * CHUNK, CHUNK),
        sgu_b_s=vec[ROW_BS:ROW_BS + GROUPS, 0:CHUNK], norm_mlp_g=row(ROW_G2), norm_final_g=row(ROW_G3))
    loss = vec[ROW_LOSS, 0]
    small_names = list(g_small)
    given = dict(
        norm_mix_g=(norm_mix_g, m_norm_mix_g, v_norm_mix_g), conv_w=(conv_w, m_conv_w, v_conv_w),
        conv_b=(conv_b, m_conv_b, v_conv_b), b_rgate=(b_rgate, m_b_rgate, v_b_rgate),
        b_igate=(b_igate, m_b_igate, v_b_igate), lru_lambda=(lru_lambda, m_lru_lambda, v_lru_lambda),
        sgu_ln_g=(sgu_ln_g, m_sgu_ln_g, v_sgu_ln_g), sgu_ln_b=(sgu_ln_b, m_sgu_ln_b, v_sgu_ln_b),
        sgu_w_s=(sgu_w_s, m_sgu_w_s, v_sgu_w_s), sgu_b_s=(sgu_b_s, m_sgu_b_s, v_sgu_b_s),
        norm_mlp_g=(norm_mlp_g, m_norm_mlp_g, v_norm_mlp_g), norm_final_g=(norm_final_g, m_norm_final_g, v_norm_final_g))
    g2d = [g_small[nm] for nm in small_names]
    to2d = lambda a, g: a.reshape(g.shape)
    d_s, m_s, v_s = _adamw_small(
        g2d, *[[to2d(given[nm][q], g) for nm, g in zip(small_names, g2d)] for q in range(3)])

    shapes = dict(
        norm_mix_g=norm_mix_g, w_in=w_in, conv_w=conv_w, conv_b=conv_b, w_rgate=w_rgate, b_rgate=b_rgate,
        w_igate=w_igate, b_igate=b_igate, lru_lambda=lru_lambda, w_out_a=w_out_a, sgu_ln_g=sgu_ln_g,
        sgu_ln_b=sgu_ln_b, sgu_w_s=sgu_w_s, sgu_b_s=sgu_b_s, w_out_b=w_out_b, w_out=w_out, norm_mlp_g=norm_mlp_g,
        w_up=w_up, w_down=w_down, norm_final_g=norm_final_g)
    grads, deltas, new_m, new_v = {}, {}, {}, {}
    for nm, g, (d, nmom, nvar) in zip(names, full, big_out):
        grads[nm], deltas[nm], new_m[nm], new_v[nm] = g, d, nmom, nvar
    for p, nm in enumerate(small_names):
        grads[nm], deltas[nm], new_m[nm], new_v[nm] = g2d[p], d_s[p], m_s[p], v_s[p]
    order = list(shapes)
    out = [loss, grad_x[None]]
    for group in (grads, deltas, new_m, new_v):
        out += [group[nm].reshape(shapes[nm].shape) for nm in order]
    return tuple(out)
```

```python
import functools

import jax
import jax.numpy as jnp
from jax import lax
from jax.experimental import pallas as pl
from jax.experimental.pallas import tpu as pltpu
from jax.experimental.pallas import tpu_sc as plsc

F32 = jnp.float32
BF16 = jnp.bfloat16
MESH = pl.DeviceIdType.MESH

D_MODEL = 1024
D_IN = 6 * D_MODEL
D_FF = 4 * D_MODEL
N_CHIPS = 4
IN_SHARD = D_IN // N_CHIPS
HEADS = 4
HEAD_DIM = D_MODEL // HEADS
GROUPS = 4
GROUP_DIM = D_MODEL // GROUPS
CHUNK = 128
CONV_WIDTH = 4
LRU_C = 8.0
NORM_EPS = 1e-6
LN_EPS = 1e-5

ADAM_LR = 0.001
ADAM_B1 = 0.9
ADAM_B2 = 0.999
ADAM_EPS = 1e-08
ADAM_WD = 0.01
ADAM_STEP = 10

SUBLANES = 8
MM_TILE = 512
IN_TILE = 1024
SEQ_TILE = 256
DW_TILE = 2048
VMEM_LIMIT_BYTES = 56 * 1024 * 1024

GELU_K0 = 0.7978845608028654
GELU_K1 = 0.044715


def _params(n_grid_axes=1):
    return pltpu.CompilerParams(
        dimension_semantics=("arbitrary",) * n_grid_axes, vmem_limit_bytes=VMEM_LIMIT_BYTES)


def _resident(shape):
    nd = len(shape)
    return pl.BlockSpec(shape, lambda *_: (0,) * nd, pipeline_mode=pl.Buffered(1))


def _const(shape):
    nd = len(shape)
    return pl.BlockSpec(shape, lambda *_: (0,) * nd)


def _dot(a, b):
    return jnp.dot(a, b, preferred_element_type=F32)


def _dot_nt(a, b):
    return lax.dot_general(a, b, (((1,), (1,)), ((), ())), preferred_element_type=F32)


def _dot_tn(a, b):
    return lax.dot_general(a, b, (((0,), (0,)), ((), ())), preferred_element_type=F32)


def _gelu(x):
    t = jnp.tanh(x * (GELU_K0 + (GELU_K0 * GELU_K1) * (x * x)))
    return x * (0.5 + 0.5 * t)


def _gelu_and_grad(x):
    x2 = x * x
    t = jnp.tanh(x * (GELU_K0 + (GELU_K0 * GELU_K1) * x2))
    s = 0.5 + 0.5 * t
    dg = s + (x * (1.0 - t * t)) * (0.5 * GELU_K0 + (1.5 * GELU_K0 * GELU_K1) * x2)
    return x * s, dg


def _gate(x):
    return 0.5 + 0.5 * jnp.tanh(0.5 * x)


def _rms(x):
    r = lax.rsqrt(jnp.mean(x * x, axis=-1, keepdims=True) + NORM_EPS)
    return x * r, r


def _rms_bwd(dn, xhat, r):
    return r * (dn - xhat * jnp.mean(dn * xhat, axis=-1, keepdims=True))


def _col_sum(v):
    return jnp.sum(v, axis=0, keepdims=True)


def _shift_down(x, tail8, k):
    xs = pltpu.roll(x, k, 0)
    ts = pltpu.roll(tail8, k, 0)
    ridx = lax.broadcasted_iota(jnp.int32, tail8.shape, 0)
    head = jnp.where(ridx < k, ts, xs[0:SUBLANES])
    return jnp.concatenate([head, xs[SUBLANES:]], axis=0)


def _shift_up(x, head8, k):
    n = x.shape[0]
    xs = pltpu.roll(x, n - k, 0)
    hs = pltpu.roll(head8, SUBLANES - k, 0)
    ridx = lax.broadcasted_iota(jnp.int32, head8.shape, 0)
    last = jnp.where(ridx >= SUBLANES - k, hs, xs[n - SUBLANES:n])
    return jnp.concatenate([xs[:n - SUBLANES], last], axis=0)


def _scan_forward(a, b, carry):
    n, cols = a.shape
    groups = n // SUBLANES
    a = a.reshape(groups, SUBLANES, cols)
    b = b.reshape(groups, SUBLANES, cols)
    sub = lax.broadcasted_iota(jnp.int32, a.shape, 1)
    for s in (1, 2, 4):
        a_s = pltpu.roll(a, s, 1)
        b_s = pltpu.roll(b, s, 1)
        m = sub >= s
        b = jnp.where(m, a * b_s + b, b)
        a = jnp.where(m, a * a_s, a)
    out = []
    for g in range(groups):
        h = a[g] * carry + b[g]
        out.append(h)
        carry = h[SUBLANES - 1:SUBLANES]
    return jnp.concatenate(out, axis=0), carry


def _scan_backward(a, b, carry):
    n, cols = a.shape
    groups = n // SUBLANES
    a = a.reshape(groups, SUBLANES, cols)
    b = b.reshape(groups, SUBLANES, cols)
    sub = lax.broadcasted_iota(jnp.int32, a.shape, 1)
    for s in (1, 2, 4):
        a_s = pltpu.roll(a, SUBLANES - s, 1)
        b_s = pltpu.roll(b, SUBLANES - s, 1)
        m = sub < SUBLANES - s
        b = jnp.where(m, a * b_s + b, b)
        a = jnp.where(m, a * a_s, a)
    out = [None] * groups
    for g in reversed(range(groups)):
        h = a[g] * carry + b[g]
        out[g] = h
        carry = h[0:1]
    return jnp.concatenate(out, axis=0), carry


def _softplus_neg(lam):
    e = jnp.exp(-jnp.abs(lam))
    u = 1.0 + e
    log1p_e = jnp.where(u == 1.0, e, jnp.log(u) * (e / jnp.where(u == 1.0, 1.0, u - 1.0)))
    return jnp.maximum(-lam, 0.0) + log1p_e


def _lru_gates(xa, tail8, cw_ref, cb_ref, wr_ref, br_ref, wi_ref, bi_ref, lam_ref):
    cw = cw_ref[...]
    xc = cb_ref[...] + cw[0:1] * xa
    for k in range(1, CONV_WIDTH):
        xc = xc + cw[k:k + 1] * _shift_down(xa, tail8, k)
    xcb = xc.astype(BF16)
    pre_r, pre_i = [], []
    for h in range(HEADS):
        cols = slice(h * HEAD_DIM, (h + 1) * HEAD_DIM)
        pre_r.append(_dot(xcb[:, cols], wr_ref[h]))
        pre_i.append(_dot(xcb[:, cols], wi_ref[h]))
    r = jax.nn.sigmoid(jnp.concatenate(pre_r, axis=1) + br_ref[...])
    ig = jax.nn.sigmoid(jnp.concatenate(pre_i, axis=1) + bi_ref[...])
    _, a, mult, _ = _decay(r, lam_ref)
    return xc, r, ig, a, mult


def _decay(r, lam_ref):
    sp = _softplus_neg(lam_ref[...])
    log_a = ((-LRU_C) * sp) * r
    a = jnp.exp(log_a)
    th = jnp.tanh(log_a)
    q = (-2.0 * th) / (1.0 - th)
    inv = lax.rsqrt(q)
    return sp, a, jnp.where(q > 0.0, q * inv, 0.0), inv


class _Phase:
    def __init__(self, copies, n_sem, n_local, start=None, finish=None):
        self.copies, self.n_sem, self.n_local, self.start, self.finish = copies, n_sem, n_local, start, finish


class _Job:
    def __init__(self, inputs, out_shape, n_sem, copies, peers, aliases=None, n_local=0):
        self.inputs, self.out_shape = list(inputs), list(out_shape)
        self.aliases = dict(aliases or {})
        self.phases = [_Phase(copies, n_sem, n_local)]
        self.peers = tuple(peers)

    def then(self, make, at=None):
        nxt = make(self.out_shape)
        self.phases[-1].finish = at
        nxt.phases[0].start = at
        self.phases += nxt.phases
        self.peers = tuple(sorted(set(self.peers + nxt.peers)))
        return self


def _fused_call(body, jobs, *, name, grid, in_specs, out_specs, out_shape, scratch_shapes=(),
                input_output_aliases=None, compiler_params=None, n_prefetch=0, jobs_start_after=None):
    single = not isinstance(out_shape, (list, tuple))
    out_specs = [out_specs] if single else list(out_specs)
    out_shape = [out_shape] if single else list(out_shape)
    n_scr = len(scratch_shapes)
    in_specs, scratch_shapes = list(in_specs), list(scratch_shapes)
    n_in, n_out = len(in_specs), len(out_shape)
    aliases = dict(input_output_aliases or {})
    in_at, out_at, phases = [], [], []
    for q, job in enumerate(jobs):
        in_at.append(len(in_specs))
        out_at.append(len(out_shape))
        for i, o in job.aliases.items():
            aliases[n_prefetch + len(in_specs) + i] = len(out_shape) + o
        in_specs += [ANY] * len(job.inputs)
        out_specs += [ANY] * len(job.out_shape)
        out_shape += job.out_shape
        for k, phase in enumerate(job.phases):
            phases.append((q, k, phase, len(scratch_shapes)))
            scratch_shapes += [pltpu.SemaphoreType.DMA((phase.n_sem,)), pltpu.SemaphoreType.DMA((phase.n_sem,)),
                               pltpu.SemaphoreType.DMA((max(phase.n_local, 1),))]
    n_in_all, n_out_all = len(in_specs), len(out_shape)
    first_step, last_step = (0,) * len(grid), tuple(g - 1 for g in grid)

    def full_body(*refs):
        prefetch, refs = refs[:n_prefetch], refs[n_prefetch:]
        ins, outs, scr = refs[:n_in_all], refs[n_in_all:n_in_all + n_out_all], refs[n_in_all + n_out_all:]
        ids = [pl.program_id(a) for a in range(len(grid))]
        at_step = lambda step: functools.reduce(jnp.logical_and, [i == k for i, k in zip(ids, step)])

        def copies(q, k, phase, sem_at):
            job = jobs[q]
            mine = outs[out_at[q]:out_at[q] + len(job.out_shape)]
            return phase.copies(ins[in_at[q]:in_at[q] + len(job.inputs)] if k == 0 else mine, mine,
                                *scr[sem_at:sem_at + 3])

        def start(*phase):
            def go():
                sends, _, local = copies(*phase)
                for cp in local + sends:
                    cp.start()
            return go

        def finish(*phase):
            def go():
                sends, arrivals, local = copies(*phase)
                for cp in arrivals:
                    cp.wait_recv()
                for cp in sends:
                    cp.wait_send()
                for cp in local:
                    cp.wait()
            return go

        for phase in phases:
            if phase[2].start is None and jobs_start_after is None:
                pl.when(at_step(first_step))(start(*phase))
        body(*prefetch, *ins[:n_in], *outs[:n_out], *scr[:n_scr])
        for phase in phases:
            pl.when(at_step(phase[2].finish or last_step))(finish(*phase))
            nxt = phase[2].start or jobs_start_after
            if nxt is not None:
                pl.when(at_step(nxt))(start(*phase))

    if n_prefetch:
        layout = dict(grid_spec=pltpu.PrefetchScalarGridSpec(
            num_scalar_prefetch=n_prefetch, grid=grid, in_specs=in_specs, out_specs=out_specs,
            scratch_shapes=scratch_shapes))
    else:
        layout = dict(grid=grid, in_specs=in_specs, out_specs=out_specs, scratch_shapes=scratch_shapes)
    call = pl.pallas_call(
        full_body, name=name, out_shape=out_shape, input_output_aliases=aliases, compiler_params=compiler_params,
        **layout)

    def run(*args):
        res = call(*args, *[a for job in jobs for a in job.inputs])
        mine = res[0] if single else list(res[:n_out])
        return mine, [list(res[at:at + len(job.out_shape)]) for at, job in zip(out_at, jobs)]

    return run


def _fwd_in(x, g1, shards, order, jobs=()):
    t = x.shape[0]
    rows_per_step = min(IN_TILE, t)
    n_tiles = t // rows_per_step
    n = len(shards)
    halves = [s.shape[0] // 2 for s in shards]

    def body(order_ref, x_ref, g_ref, *refs):
        del order_ref
        ins, (z_ref, n_ref), outs = refs[:n], refs[n:n + 2], refs[n + 2:2 * n + 2]
        wbuf, nbuf, send, recv, local = refs[2 * n + 2:]
        s, i = pl.program_id(0), pl.program_id(1)
        x_, y_, c, chips = _place()
        near, far = _near_far(x_, y_, c)
        k_me = _chip_index(x_, y_)

        def block(w, chip, pc):
            return outs[w].at[_chip_index(*chip), pl.ds(pc * halves[w], halves[w]), :]

        def over_ici(w, j, landing):
            return pltpu.make_async_remote_copy(
                src_ref=ins[w].at[pl.ds(c * halves[w], halves[w]), :],
                dst_ref=block(w, chips[j] if landing else (x_, y_), c), send_sem=send.at[6 * w + j],
                recv_sem=recv.at[6 * w + j], device_id=(*chips[j], c), device_id_type=MESH)

        def onward(w, landing):
            blk = block(w, chips[2] if landing else near, c)
            return pltpu.make_async_remote_copy(
                src_ref=blk, dst_ref=blk, send_sem=send.at[6 * w + 2], recv_sem=recv.at[6 * w + 2],
                device_id=(*far, c), device_id_type=MESH)

        def to_sibling(w, j, landing):
            blk = block(w, chips[j], 1 - c if landing else c)
            return pltpu.make_async_remote_copy(
                src_ref=blk, dst_ref=blk, send_sem=send.at[6 * w + 3 + j], recv_sem=recv.at[6 * w + 3 + j],
                device_id=(x_, y_, 1 - c), device_id_type=MESH)

        own = [pltpu.make_async_copy(wbuf, outs[0].at[k_me], local.at[0])]
        own += [pltpu.make_async_copy(ins[w], outs[w].at[k_me], local.at[w]) for w in range(1, n)]

        @pl.when((s == 0) & (i == 0))
        def _():
            for j in range(2):
                for w in range(n):
                    over_ici(w, j, False).start()
            load = pltpu.make_async_copy(ins[0], wbuf, local.at[n])
            load.start()
            load.wait()
            for cp in own:
                cp.start()

        for j in range(N_CHIPS - 1):
            @pl.when((s == j + 1) & (i == 0))
            def _(j=j):
                if j == 0:
                    for k in range(2):
                        for w in range(n):
                            over_ici(w, k, True).wait_recv()
                    for w in range(n):
                        onward(w, False).start()
                    for k in range(2):
                        for w in range(n):
                            to_sibling(w, k, False).start()
                    own[0].wait()
                if j == 2:
                    for w in range(n):
                        onward(w, True).wait_recv()
                    for w in range(n):
                        to_sibling(w, j, False).start()
                for w in range(n):
                    to_sibling(w, j, True).wait_recv()
                load = pltpu.make_async_copy(outs[0].at[_chip_index(*chips[j])], wbuf, local.at[n])
                load.start()
                load.wait()

        rows = pl.ds(pl.multiple_of(i * rows_per_step, rows_per_step), rows_per_step)

        @pl.when(s == 0)
        def _():
            xhat, _ = _rms(x_ref[...])
            nrm = (xhat * g_ref[...]).astype(BF16)
            nbuf[rows, :] = nrm
            n_ref[...] = nrm

        z_ref[...] = _dot(nbuf[rows, :], wbuf[...])

        @pl.when((s == N_CHIPS - 1) & (i == n_tiles - 1))
        def _():
            for j in range(N_CHIPS - 1):
                for w in range(n):
                    (over_ici(w, j, False) if j < 2 else onward(w, False)).wait_send()
                    to_sibling(w, j, False).wait_send()
            for cp in own[1:]:
                cp.wait()

    once = lambda s, i, order: (jnp.where(s == 0, i, n_tiles - 1), 0)
    (z, n1, *stacked), job_outs = _fused_call(
        body, jobs, name="fwd_in", grid=(N_CHIPS, n_tiles), n_prefetch=1,
        in_specs=[pl.BlockSpec((rows_per_step, D_MODEL), once), _const((1, D_MODEL))] + [ANY] * n,
        out_specs=[pl.BlockSpec((rows_per_step, IN_SHARD), lambda s, i, order: (i, order[s])),
                   pl.BlockSpec((rows_per_step, D_MODEL), once)] + [ANY] * n,
        out_shape=[jax.ShapeDtypeStruct((t, D_IN), F32), jax.ShapeDtypeStruct((t, D_MODEL), BF16)]
        + [jax.ShapeDtypeStruct((N_CHIPS,) + s.shape, s.dtype) for s in shards],
        scratch_shapes=[pltpu.VMEM(shards[0].shape, BF16), pltpu.VMEM((t, D_MODEL), BF16),
                        pltpu.SemaphoreType.DMA((6 * n,)),
                        pltpu.SemaphoreType.DMA((6 * n,)), pltpu.SemaphoreType.DMA((n + 1,))],
        compiler_params=_params(2), jobs_start_after=(1, 0),
    )(order, x, g1, *shards)
    return (z, n1, stacked), job_outs


def _fwd_lru(z, conv_w, conv_b, wr, br, wi, bi, lam, jobs=()):
    t = z.shape[0]

    def body(xa_ref, ga_ref, cw_ref, cb_ref, wr_ref, br_ref, wi_ref, bi_ref, lam_ref, ya_ref, h_ref, xc_ref, r_ref,
             ig_ref, tail_ref, carry_ref):
        @pl.when(pl.program_id(0) == 0)
        def _():
            tail_ref[...] = jnp.zeros_like(tail_ref)
            carry_ref[...] = jnp.zeros_like(carry_ref)

        xa = xa_ref[...]
        xc, r, ig, a, mult = _lru_gates(xa, tail_ref[...], cw_ref, cb_ref, wr_ref, br_ref, wi_ref, bi_ref, lam_ref)
        tail_ref[...] = xa[SEQ_TILE - SUBLANES:]
        xc_ref[...], r_ref[...], ig_ref[...] = xc, r, ig
        h, carry = _scan_forward(a, xc * ig * mult, carry_ref[...])
        carry_ref[...] = carry
        h_ref[...] = h
        ya_ref[...] = (h * _gelu(ga_ref[...])).astype(BF16)

    tile = lambda j: pl.BlockSpec((SEQ_TILE, D_MODEL), lambda i: (i, j))
    return _fused_call(
        body, jobs, name="fwd_lru", grid=(t // SEQ_TILE,),
        in_specs=[tile(0), tile(1), _const((CONV_WIDTH, D_MODEL)), _const((1, D_MODEL)),
                  _resident((HEADS, HEAD_DIM, HEAD_DIM)), _const((1, D_MODEL)),
                  _resident((HEADS, HEAD_DIM, HEAD_DIM)), _const((1, D_MODEL)), _const((1, D_MODEL))],
        out_specs=[tile(0)] * 5,
        out_shape=[jax.ShapeDtypeStruct((t, D_MODEL), BF16)] + [jax.ShapeDtypeStruct((t, D_MODEL), F32)] * 4,
        scratch_shapes=[pltpu.VMEM((SUBLANES, D_MODEL), F32), pltpu.VMEM((1, D_MODEL), F32)],
        compiler_params=_params(),
    )(z, z, conv_w, conv_b, wr, br, wi, bi, lam)


def _sgu_forward_parts(ub, vb, lg_ref, lb_ref):
    u, du = _gelu_and_grad(ub)
    vg, dvg = _gelu_and_grad(vb)
    mu = jnp.mean(vg, axis=-1, keepdims=True)
    d = vg - mu
    rstd = lax.rsqrt(jnp.mean(d * d, axis=-1, keepdims=True) + LN_EPS)
    vhat = d * rstd
    vn = (vhat * lg_ref[...] + lb_ref[...]).astype(BF16)
    return u, du, dvg, rstd, vhat, vn


def _causal_mask():
    rows = lax.broadcasted_iota(jnp.int32, (CHUNK, CHUNK), 0)
    cols = lax.broadcasted_iota(jnp.int32, (CHUNK, CHUNK), 1)
    return rows >= cols


def _fwd_sgu_merge(ya, z, x, ln_g, ln_b, w_s, bias_full, w_oa, w_ob, w_out, g2, jobs=()):
    t = x.shape[0]

    def body(ya_ref, ub_ref, vb_ref, m_ref, x_ref, lg_ref, lb_ref, ws_ref, bias_ref, woa_ref, wob_ref, wout_ref, g_ref,
             yb_ref, pa_ref, pb_ref, h1_ref, n2_ref):
        u, _, _, _, _, vn = _sgu_forward_parts(ub_ref[...], vb_ref[...], lg_ref, lb_ref)
        mask = _causal_mask()
        wm = [jnp.where(mask, ws_ref[g], 0.0).astype(BF16) for g in range(GROUPS)]
        for c in range(SEQ_TILE // CHUNK):
            rows = slice(c * CHUNK, (c + 1) * CHUNK)
            for g in range(GROUPS):
                cols = slice(g * GROUP_DIM, (g + 1) * GROUP_DIM)
                sp = _dot(wm[g], vn[rows, cols]) + bias_ref[:, cols]
                yb_ref[rows, cols] = (u[rows, cols] * sp).astype(BF16)
        pa = _dot(ya_ref[...], woa_ref[...])
        pb = _dot(yb_ref[...], wob_ref[...])
        pa_ref[...] = pa
        pb_ref[...] = pb
        merged = _gate(m_ref[:, :D_MODEL]) * pa + _gate(m_ref[:, D_MODEL:]) * pb
        h1 = x_ref[...] + _dot(merged.astype(BF16), wout_ref[...])
        h1_ref[...] = h1
        xhat, _ = _rms(h1)
        n2_ref[...] = (xhat * g_ref[...]).astype(BF16)

    tile = lambda j: pl.BlockSpec((SEQ_TILE, D_MODEL), lambda i: (i, j))
    sq = _resident((D_MODEL, D_MODEL))
    vec = _const((1, D_MODEL))
    bf, f32 = jax.ShapeDtypeStruct((t, D_MODEL), BF16), jax.ShapeDtypeStruct((t, D_MODEL), F32)
    return _fused_call(
        body, jobs, name="fwd_sgu_merge", grid=(t // SEQ_TILE,),
        in_specs=[tile(0), tile(2), tile(3), pl.BlockSpec((SEQ_TILE, 2 * D_MODEL), lambda i: (i, 2)), tile(0), vec, vec,
                  _const((GROUPS, CHUNK, CHUNK)), _const((CHUNK, D_MODEL)), sq, sq, sq, vec],
        out_specs=[tile(0)] * 5,
        out_shape=[bf, f32, f32, f32, bf],
        compiler_params=_params(),
    )(ya, z, z, z, x, ln_g, ln_b, w_s, bias_full, w_oa, w_ob, w_out, g2)


def _mlp(n2, h1, target, w_up_st, w_down, g2, g3, jobs=()):
    t = n2.shape[0]

    def body(n2_ref, h1_ref, tgt_ref, wup_ref, wdown_ref, g2_ref, g3_ref, act_ref, dup_ref, dh2b_ref, dh1_ref,
             loss_ref, dg3_ref, dg2_ref, relu_ref):
        @pl.when(pl.program_id(0) == 0)
        def _():
            for ref in (loss_ref, dg3_ref, dg2_ref):
                ref[...] = jnp.zeros_like(ref)

        n2 = n2_ref[...]
        h1 = h1_ref[...]
        h2 = h1
        for k in range(N_CHIPS):
            cols = slice(k * D_MODEL, (k + 1) * D_MODEL)
            r = jnp.maximum(_dot(n2, wup_ref[k]), 0.0)
            relu_ref[:, cols] = r
            act = (r * r).astype(BF16)
            act_ref[:, cols] = act
            h2 = h2 + _dot(act, wdown_ref[cols, :])
        xhat, r3 = _rms(h2)
        diff = xhat * g3_ref[...] - tgt_ref[...]
        sq = jnp.sum(diff * diff, axis=1, keepdims=True)
        loss_ref[...] = loss_ref[...] + (0.5 / D_MODEL) * jnp.sum(sq, axis=0, keepdims=True)
        dy = diff * (1.0 / D_MODEL)
        dg3_ref[...] = dg3_ref[...] + _col_sum(dy * xhat)
        dh2 = _rms_bwd(dy * g3_ref[...], xhat, r3)
        dh2b = dh2.astype(BF16)
        dh2b_ref[...] = dh2b
        dn2 = jnp.zeros((SEQ_TILE, D_MODEL), F32)
        for k in range(N_CHIPS):
            cols = slice(k * D_MODEL, (k + 1) * D_MODEL)
            dup = (_dot_nt(dh2b, wdown_ref[cols, :]) * (2.0 * relu_ref[:, cols])).astype(BF16)
            dup_ref[:, cols] = dup
            dn2 = dn2 + _dot_nt(dup, wup_ref[k])
        xhat, r2 = _rms(h1)
        dg2_ref[...] = dg2_ref[...] + _col_sum(dn2 * xhat)
        dh1_ref[...] = dh2 + _rms_bwd(dn2 * g2_ref[...], xhat, r2)

    tile = pl.BlockSpec((SEQ_TILE, D_MODEL), lambda i: (i, 0))
    wide = pl.BlockSpec((SEQ_TILE, D_FF), lambda i: (i, 0))
    vec = _const((1, D_MODEL))
    vec_shape = jax.ShapeDtypeStruct((1, D_MODEL), F32)
    return _fused_call(
        body, jobs, name="mlp", grid=(t // SEQ_TILE,),
        in_specs=[tile, tile, tile, _resident((N_CHIPS, D_MODEL, D_MODEL)), _resident((D_FF, D_MODEL)), vec, vec],
        out_specs=[wide, wide, tile, tile, _const((SUBLANES, 128)), vec, vec],
        out_shape=[jax.ShapeDtypeStruct((t, D_FF), BF16), jax.ShapeDtypeStruct((t, D_FF), BF16),
                   jax.ShapeDtypeStruct((t, D_MODEL), BF16), jax.ShapeDtypeStruct((t, D_MODEL), F32),
                   jax.ShapeDtypeStruct((SUBLANES, 128), F32), vec_shape, vec_shape],
        scratch_shapes=[pltpu.VMEM((SEQ_TILE, D_FF), F32)],
        compiler_params=_params(),
    )(n2, h1, target, w_up_st, w_down, g2, g3)


def _bwd_mix(dh1, pa, pb, z, h, xc, r, ig, w_oa, w_ob, w_out, ln_g, ln_b, w_s, bias_full, conv_w, wr, wi, lam, jobs=()):
    t = dh1.shape[0]
    n_tiles = t // SEQ_TILE
    per_tile = SEQ_TILE // SUBLANES

    def merge_part(dh1_ref, pa_ref, pb_ref, m_ref, woa_ref, wob_ref, wout_ref, dz_ref, dya_ref, dyb_ref, mg_ref,
                   dpa_ref, dpb_ref, dh1b_ref):
        dh1b = dh1_ref[...].astype(BF16)
        dh1b_ref[...] = dh1b
        dm = _dot_nt(dh1b, wout_ref[...])
        pa = pa_ref[...]
        pb = pb_ref[...]
        sa = _gate(m_ref[:, :D_MODEL])
        sb = _gate(m_ref[:, D_MODEL:])
        mg_ref[...] = (sa * pa + sb * pb).astype(BF16)
        dpa = dm * sa
        dpb = dm * sb
        dz_ref[:, :D_MODEL] = ((dpa * pa) * (1.0 - sa)).astype(BF16)
        dz_ref[:, D_MODEL:] = ((dpb * pb) * (1.0 - sb)).astype(BF16)
        dpa = dpa.astype(BF16)
        dpb = dpb.astype(BF16)
        dpa_ref[...] = dpa
        dpb_ref[...] = dpb
        dya_ref[...] = _dot_nt(dpa, woa_ref[...])
        dyb_ref[...] = _dot_nt(dpb, wob_ref[...])

    def sgu_part(dyb_ref, ub_ref, vb_ref, lg_ref, lb_ref, ws_ref, bias_ref, dz_ref, dlg_ref, dlb_ref, dws_ref, dbs_ref,
                 dvn_ref, dsp_acc):
        i = pl.program_id(0)

        @pl.when(i == 0)
        def _():
            dlg_ref[...] = jnp.zeros_like(dlg_ref)
            dlb_ref[...] = jnp.zeros_like(dlb_ref)
            dws_ref[...] = jnp.zeros_like(dws_ref)
            dsp_acc[...] = jnp.zeros_like(dsp_acc)

        u, du, dvg, rstd, vhat, vn = _sgu_forward_parts(ub_ref[...], vb_ref[...], lg_ref, lb_ref)
        dyb = dyb_ref[...]
        mask = _causal_mask()
        wm = [jnp.where(mask, ws_ref[g], 0.0).astype(BF16) for g in range(GROUPS)]
        for c in range(SEQ_TILE // CHUNK):
            rows = slice(c * CHUNK, (c + 1) * CHUNK)
            for g in range(GROUPS):
                cols = slice(g * GROUP_DIM, (g + 1) * GROUP_DIM)
                vn_blk = vn[rows, cols]
                sp = _dot(wm[g], vn_blk) + bias_ref[:, cols]
                dyb_blk = dyb[rows, cols]
                dz_ref[rows, cols] = (dyb_blk * sp * du[rows, cols]).astype(BF16)
                dsp = dyb_blk * u[rows, cols]
                dsp_acc[:, cols] = dsp_acc[:, cols] + dsp
                dspb = dsp.astype(BF16)
                dvn_ref[rows, cols] = _dot_tn(wm[g], dspb)
                wcols = slice(g * CHUNK, (g + 1) * CHUNK)
                dws_ref[:, wcols] = dws_ref[:, wcols] + jnp.where(mask, _dot_nt(dspb, vn_blk), 0.0)
        dvn = dvn_ref[...]
        dlg_ref[...] = dlg_ref[...] + _col_sum(dvn * vhat)
        dlb_ref[...] = dlb_ref[...] + _col_sum(dvn)
        dvhat = dvn * lg_ref[...]
        dvgel = rstd * (dvhat - jnp.mean(dvhat, axis=-1, keepdims=True)
                        - vhat * jnp.mean(dvhat * vhat, axis=-1, keepdims=True))
        dz_ref[:, D_MODEL:] = (dvgel * dvg).astype(BF16)

        @pl.when(i == n_tiles - 1)
        def _():
            lane = lax.broadcasted_iota(jnp.int32, (CHUNK, 128), 1)
            out = jnp.zeros((CHUNK, 128), F32)
            for g in range(GROUPS):
                s = jnp.sum(dsp_acc[:, g * GROUP_DIM:(g + 1) * GROUP_DIM], axis=1, keepdims=True)
                out = out + jnp.where(lane == g, s, 0.0)
            dbs_ref[...] = out

    def lru_part(dya_ref, xa_ref, ga_ref, h_ref, h_prev_ref, xc_ref, r_ref, ig_ref, cw_ref, wr_ref, wi_ref, lam_ref,
                 dz_ref, dcw_ref, dcb_ref, dwr_ref, dbr_ref, dwi_ref, dbi_ref, dlam_ref, lam_carry, dxc_head):
        i = pl.program_id(0)

        @pl.when(i == 0)
        def _():
            for ref in (dcw_ref, dcb_ref, dwr_ref, dbr_ref, dwi_ref, dbi_ref, dlam_ref, lam_carry, dxc_head):
                ref[...] = jnp.zeros_like(ref)

        first_tile = i == n_tiles - 1
        h_tail = jnp.where(first_tile, 0.0, h_prev_ref[...])
        xc, r, ig = xc_ref[...], r_ref[...], ig_ref[...]
        xcb = xc.astype(BF16)
        sp, a, mult, inv_mult = _decay(r, lam_ref)
        h = h_ref[...]
        h_prev = _shift_down(h, h_tail, 1)
        dya = dya_ref[...]
        gg, dgg = _gelu_and_grad(ga_ref[...])
        dz_ref[:, D_MODEL:] = (dya * h * dgg).astype(BF16)
        ones = jnp.ones((SUBLANES, D_MODEL), F32)
        lam_t, lam_first = _scan_backward(_shift_up(a, ones, 1), dya * gg, lam_carry[...])
        lam_carry[...] = a[0:1] * lam_first
        lam_ig = lam_t * ig
        dxc_direct = lam_ig * mult
        dmult = lam_ig * xc
        dla = a * (lam_t * h_prev - (dmult * a) * inv_mult)
        dla_r = dla * r
        dlam_ref[...] = dlam_ref[...] + _col_sum(dla_r) * (LRU_C * jax.nn.sigmoid(-lam_ref[...]))
        dpr = (dla_r * ((-LRU_C) * sp)) * (1.0 - r)
        dpi = (dxc_direct * xc) * (1.0 - ig)
        dbr_ref[...] = dbr_ref[...] + _col_sum(dpr)
        dbi_ref[...] = dbi_ref[...] + _col_sum(dpi)
        dprb = dpr.astype(BF16)
        dpib = dpi.astype(BF16)
        dxc_gate = []
        for hd in range(HEADS):
            cols = slice(hd * HEAD_DIM, (hd + 1) * HEAD_DIM)
            dxc_gate.append(_dot_nt(dprb[:, cols], wr_ref[hd]) + _dot_nt(dpib[:, cols], wi_ref[hd]))
            dwr_ref[hd] = dwr_ref[hd] + _dot_tn(xcb[:, cols], dprb[:, cols])
            dwi_ref[hd] = dwi_ref[hd] + _dot_tn(xcb[:, cols], dpib[:, cols])
        dxc = dxc_direct + jnp.concatenate(dxc_gate, axis=1)
        dcb_ref[...] = dcb_ref[...] + _col_sum(dxc)
        cw = cw_ref[...]
        head = dxc_head[...]
        xa = xa_ref[...]
        dxa = cw[0:1] * dxc
        dcw_ref[0:1, :] = dcw_ref[0:1, :] + _col_sum(dxc * xa)
        for k in range(1, CONV_WIDTH):
            dxc_k = _shift_up(dxc, head, k)
            dxa = dxa + cw[k:k + 1] * dxc_k
            dcw_ref[k:k + 1, :] = dcw_ref[k:k + 1, :] + _col_sum(dxc_k * xa)
        dxc_head[...] = dxc[0:SUBLANES]
        dz_ref[:, :D_MODEL] = dxa.astype(BF16)

    def body(dh1_ref, pa_ref, pb_ref, z_ref, h_ref, h_prev_ref, xc_ref, r_ref, ig_ref, woa_ref, wob_ref, wout_ref,
             lg_ref, lb_ref, ws_ref, bias_ref, cw_ref, wr_ref, wi_ref, lam_ref, dz_ref, mg_ref, dpa_ref, dpb_ref,
             dh1b_ref, dlg_ref, dlb_ref, dws_ref, dbs_ref, dcw_ref, dcb_ref, dwr_ref, dbr_ref, dwi_ref, dbi_ref,
             dlam_ref, dya_ref, dyb_ref, dvn_ref, dsp_acc, lam_carry, dxc_head):
        def cols(ref, first, count):
            return ref.at[:, pl.ds(first * D_MODEL, count * D_MODEL)]

        merge_part(dh1_ref, pa_ref, pb_ref, cols(z_ref, 4, 2), woa_ref, wob_ref, wout_ref, cols(dz_ref, 4, 2), dya_ref,
                   dyb_ref, mg_ref, dpa_ref, dpb_ref, dh1b_ref)
        sgu_part(dyb_ref, cols(z_ref, 2, 1), cols(z_ref, 3, 1), lg_ref, lb_ref, ws_ref, bias_ref, cols(dz_ref, 2, 2),
                 dlg_ref, dlb_ref, dws_ref, dbs_ref, dvn_ref, dsp_acc)
        lru_part(dya_ref, cols(z_ref, 0, 1), cols(z_ref, 1, 1), h_ref, h_prev_ref, xc_ref, r_ref, ig_ref, cw_ref, wr_ref,
                 wi_ref, lam_ref, cols(dz_ref, 0, 2), dcw_ref, dcb_ref, dwr_ref, dbr_ref, dwi_ref, dbi_ref, dlam_ref,
                 lam_carry, dxc_head)

    rev = lambda i: n_tiles - 1 - i
    tile = pl.BlockSpec((SEQ_TILE, D_MODEL), lambda i: (rev(i), 0))
    row = pl.BlockSpec((SEQ_TILE, D_IN), lambda i: (rev(i), 0))
    prev8 = pl.BlockSpec((SUBLANES, D_MODEL), lambda i: (jnp.maximum(rev(i) * per_tile - 1, 0), 0))
    vec = _const((1, D_MODEL))
    sq = _resident((D_MODEL, D_MODEL))
    gate_w = _resident((HEADS, HEAD_DIM, HEAD_DIM))
    gate_acc = _const((HEADS, HEAD_DIM, HEAD_DIM))
    vec_shape = jax.ShapeDtypeStruct((1, D_MODEL), F32)
    gate_shape = jax.ShapeDtypeStruct((HEADS, HEAD_DIM, HEAD_DIM), F32)
    act_bf = jax.ShapeDtypeStruct((t, D_MODEL), BF16)
    return _fused_call(
        body, jobs, name="bwd_mix", grid=(n_tiles,),
        in_specs=[tile, tile, tile, row, tile, prev8, tile, tile, tile, sq, sq, sq, vec, vec,
                  _const((GROUPS, CHUNK, CHUNK)), _const((CHUNK, D_MODEL)), _const((CONV_WIDTH, D_MODEL)), gate_w, gate_w,
                  vec],
        out_specs=[row, tile, tile, tile, tile, vec, vec, _const((CHUNK, GROUPS * CHUNK)), _const((CHUNK, 128)),
                   _const((SUBLANES, D_MODEL)), vec, gate_acc, vec, gate_acc, vec, vec],
        out_shape=[jax.ShapeDtypeStruct((t, D_IN), BF16), act_bf, act_bf, act_bf, act_bf, vec_shape, vec_shape,
                   jax.ShapeDtypeStruct((CHUNK, GROUPS * CHUNK), F32), jax.ShapeDtypeStruct((CHUNK, 128), F32),
                   jax.ShapeDtypeStruct((SUBLANES, D_MODEL), F32), vec_shape, gate_shape, vec_shape, gate_shape,
                   vec_shape, vec_shape],
        scratch_shapes=[pltpu.VMEM((SEQ_TILE, D_MODEL), F32), pltpu.VMEM((SEQ_TILE, D_MODEL), F32),
                        pltpu.VMEM((SEQ_TILE, D_MODEL), F32), pltpu.VMEM((CHUNK, D_MODEL), F32),
                        pltpu.VMEM((1, D_MODEL), F32), pltpu.VMEM((SUBLANES, D_MODEL), F32)],
        compiler_params=_params(),
    )(dh1, pa, pb, z, h, h, xc, r, ig, w_oa, w_ob, w_out, ln_g, ln_b, w_s, bias_full, conv_w, wr, wi, lam)


def _bwd_in(dz, x, dh1, w_in_st, g1, jobs=()):
    t = x.shape[0]

    def body(dz_ref, x_ref, dh1_ref, w_ref, g_ref, dx_ref, dg1_ref):
        @pl.when(pl.program_id(0) == 0)
        def _():
            dg1_ref[...] = jnp.zeros_like(dg1_ref)

        dn1 = jnp.zeros((MM_TILE, D_MODEL), F32)
        for k in range(N_CHIPS):
            dn1 = dn1 + _dot_nt(dz_ref[:, k * IN_SHARD:(k + 1) * IN_SHARD], w_ref[k])
        xhat, r1 = _rms(x_ref[...])
        dg1_ref[...] = dg1_ref[...] + _col_sum(dn1 * xhat)
        dx_ref[...] = dh1_ref[...] + _rms_bwd(dn1 * g_ref[...], xhat, r1)

    tile = pl.BlockSpec((MM_TILE, D_MODEL), lambda i: (i, 0))
    return _fused_call(
        body, jobs, name="bwd_in", grid=(t // MM_TILE,),
        in_specs=[pl.BlockSpec((MM_TILE, D_IN), lambda i: (i, 0)), tile, tile,
                  _resident((N_CHIPS, D_MODEL, IN_SHARD)), _const((1, D_MODEL))],
        out_specs=[tile, _const((1, D_MODEL))],
        out_shape=[jax.ShapeDtypeStruct((t, D_MODEL), F32), jax.ShapeDtypeStruct((1, D_MODEL), F32)],
        compiler_params=_params(),
    )(dz, x, dh1, w_in_st, g1)


def _weight_grad(name, a, b, n_blocks, a_varies, b_varies, width, jobs=()):
    t = a.shape[0]
    rows = min(DW_TILE, t)
    n_t = t // rows

    def body(a_ref, b_ref, o_ref, acc_ref):
        s = pl.program_id(1)
        part = _dot_tn(a_ref[...], b_ref[...])

        @pl.when(s == 0)
        def _():
            acc_ref[...] = part

        @pl.when(s > 0)
        def _():
            acc_ref[...] = acc_ref[...] + part

        @pl.when(s == n_t - 1)
        def _():
            o_ref[...] = acc_ref[...].astype(BF16)

    return _fused_call(
        body, jobs, name=name, grid=(n_blocks, n_t),
        in_specs=[pl.BlockSpec((rows, D_MODEL), (lambda j, s: (s, j)) if a_varies else (lambda j, s: (s, 0))),
                  pl.BlockSpec((rows, width), (lambda j, s: (s, j)) if b_varies else (lambda j, s: (s, 0)))],
        out_specs=pl.BlockSpec((None, D_MODEL, width), lambda j, s: (j, 0, 0)),
        out_shape=jax.ShapeDtypeStruct((n_blocks, D_MODEL, width), BF16),
        scratch_shapes=[pltpu.VMEM((D_MODEL, width), F32)],
        compiler_params=_params(2),
    )(a, b)


def _weight_grads_square(name, pairs, jobs=()):
    n = len(pairs)
    t = pairs[0][0].shape[0]
    rows = min(2 * MM_TILE, t)
    n_t = t // rows

    def body(*refs):
        ins, outs, accs = refs[:2 * n], refs[2 * n:3 * n], refs[3 * n:]
        s = pl.program_id(0)
        for k in range(n):
            part = _dot_tn(ins[2 * k][...], ins[2 * k + 1][...])

            @pl.when(s == 0)
            def _(k=k, part=part):
                accs[k][...] = part

            @pl.when(s > 0)
            def _(k=k, part=part):
                accs[k][...] = accs[k][...] + part

            @pl.when(s == n_t - 1)
            def _(k=k):
                outs[k][...] = accs[k][...].astype(BF16)

    tile = pl.BlockSpec((rows, D_MODEL), lambda s: (s, 0))
    return _fused_call(
        body, jobs, name=name, grid=(n_t,), in_specs=[tile] * (2 * n), out_specs=[_const((D_MODEL, D_MODEL))] * n,
        out_shape=[jax.ShapeDtypeStruct((D_MODEL, D_MODEL), BF16)] * n,
        scratch_shapes=[pltpu.VMEM((D_MODEL, D_MODEL), F32)] * n,
        compiler_params=_params(),
    )(*[x for pair in pairs for x in pair])


def _place():
    x, y, c = lax.axis_index("x"), lax.axis_index("y"), lax.axis_index("c")
    other_chips = [(1 - x, y), (x, 1 - y), (1 - x, 1 - y)]
    return x, y, c, other_chips


def _chip_index(px, py):
    return 2 * px + py


ANY = pl.BlockSpec(memory_space=pl.ANY)
SIBLING = ((0, 0, 1),)
NEIGHBOURS = ((1, 0, 0), (0, 1, 0))
OTHER_CHIPS = NEIGHBOURS + ((1, 1, 0),)


def _near_far(x, y, c):
    return (x ^ (1 - c), y ^ c), (x ^ c, y ^ (1 - c))


def _gather_near_job(shards):
    n = len(shards)
    halves = [s.shape[0] // 2 for s in shards]

    def copies(ins, outs, send, recv, local):
        x, y, c, _ = _place()
        near, _ = _near_far(x, y, c)

        def block(w, chip, pc):
            return outs[w].at[_chip_index(*chip), pl.ds(pc * halves[w], halves[w]), :]

        def copy(w, k, chip, pc, to, src=None):
            return pltpu.make_async_remote_copy(
                src_ref=block(w, chip, pc) if src is None else src, dst_ref=block(w, chip, pc),
                send_sem=send.at[2 * w + k], recv_sem=recv.at[2 * w + k], device_id=to, device_id_type=MESH)

        sends, arrivals, own = [], [], []
        for w in range(n):
            src = ins[w].at[pl.ds(c * halves[w], halves[w]), :]
            own.append(pltpu.make_async_copy(src, block(w, (x, y), c), local.at[w]))
            sends += [copy(w, 0, (x, y), c, (*near, c), src), copy(w, 1, (x, y), c, (x, y, 1 - c), src)]
            arrivals += [copy(w, 0, near, c, (x, y, c)), copy(w, 1, (x, y), 1 - c, (x, y, c))]
        return sends, arrivals, own

    return _Job(shards, [jax.ShapeDtypeStruct((N_CHIPS,) + s.shape, s.dtype) for s in shards], 2 * n, copies,
                NEIGHBOURS + SIBLING, n_local=n)


def _gather_far_job(stacked):
    n = len(stacked)
    halves = [s.shape[1] // 2 for s in stacked]

    def copies(ins, outs, send, recv, local):
        del ins, local
        x, y, c, _ = _place()
        near, far = _near_far(x, y, c)

        def copy(w, k, chip):
            blk = outs[w].at[_chip_index(*chip), pl.ds(c * halves[w], halves[w]), :]
            return pltpu.make_async_remote_copy(
                src_ref=blk, dst_ref=blk, send_sem=send.at[2 * w + k], recv_sem=recv.at[2 * w + k],
                device_id=(*far, c), device_id_type=MESH)

        sends = [copy(w, k, chip) for w in range(n) for k, chip in enumerate(((x, y), near))]
        arrivals = [copy(w, k, chip) for w in range(n) for k, chip in enumerate((far, (1 - x, 1 - y)))]
        return sends, arrivals, []

    return _Job(stacked, [jax.ShapeDtypeStruct(s.shape, s.dtype) for s in stacked], 2 * n, copies, NEIGHBOURS,
                aliases={w: w for w in range(n)})


def _gather_pass_job(stacked):
    n = len(stacked)
    halves = [s.shape[1] // 2 for s in stacked]

    def copies(ins, outs, send, recv, local):
        del ins, local
        x, y, c, chips = _place()

        def copy(w, j, chip, pc, to):
            blk = outs[w].at[_chip_index(*chip), pl.ds(pc * halves[w], halves[w]), :]
            return pltpu.make_async_remote_copy(
                src_ref=blk, dst_ref=blk, send_sem=send.at[3 * w + j], recv_sem=recv.at[3 * w + j], device_id=to,
                device_id_type=MESH)

        sends = [copy(w, j, chip, c, (x, y, 1 - c)) for w in range(n) for j, chip in enumerate(chips)]
        arrivals = [copy(w, j, chip, 1 - c, (x, y, c)) for w in range(n) for j, chip in enumerate(chips)]
        return sends, arrivals, []

    return _Job(stacked, [jax.ShapeDtypeStruct(s.shape, s.dtype) for s in stacked], 3 * n, copies, SIBLING,
                aliases={w: w for w in range(n)})


def _gather_small_job(block):
    def copies(ins, outs, send, recv, local):
        x, y, c, chips = _place()

        def copy(j, chip_from, to):
            return pltpu.make_async_remote_copy(
                src_ref=ins[0], dst_ref=outs[0].at[_chip_index(*chip_from)], send_sem=send.at[j],
                recv_sem=recv.at[j], device_id=to, device_id_type=MESH)

        own = [pltpu.make_async_copy(ins[0], outs[0].at[_chip_index(x, y)], local.at[0])]
        sends = [copy(j, (x, y), (*chip, c)) for j, chip in enumerate(chips)]
        arrivals = [copy(j, chip, (x, y, c)) for j, chip in enumerate(chips)]
        return sends, arrivals, own

    return _Job([block], [jax.ShapeDtypeStruct((N_CHIPS,) + block.shape, block.dtype)], 3, copies, OTHER_CHIPS,
                n_local=1)


def _pair_send_job(grads):
    n = len(grads)
    halves = [g.shape[1] // 2 for g in grads]

    def copies(ins, outs, send, recv, local):
        del local
        x, y, c, _ = _place()
        sends = [pltpu.make_async_remote_copy(
            src_ref=ins[w].at[:, pl.ds((1 - c) * halves[w], halves[w]), :], dst_ref=outs[w], send_sem=send.at[w],
            recv_sem=recv.at[w], device_id=(x, y, 1 - c), device_id_type=MESH) for w in range(n)]
        return sends, sends, []

    return _Job(grads, [jax.ShapeDtypeStruct((N_CHIPS, h, g.shape[2]), g.dtype) for g, h in zip(grads, halves)], n,
                copies, SIBLING)


def _row_block(rows, limit=256):
    return min(rows, limit)


def _pair_add(name, core, mine, theirs):
    _, _, h, cols = mine.shape
    rb = _row_block(h, 512)

    def body(core_ref, a_ref, b_ref, o_ref):
        del core_ref
        o_ref[...] = (a_ref[...].astype(F32) + b_ref[...].astype(F32)).astype(BF16)

    return pl.pallas_call(
        body, name=name,
        grid_spec=pltpu.PrefetchScalarGridSpec(
            num_scalar_prefetch=1, grid=(N_CHIPS, h // rb),
            in_specs=[pl.BlockSpec((None, None, rb, cols), lambda k, r, core_ref: (k, core_ref[0], r, 0)),
                      pl.BlockSpec((None, rb, cols), lambda k, r, core_ref: (k, r, 0))],
            out_specs=pl.BlockSpec((None, rb, cols), lambda k, r, core_ref: (k, r, 0))),
        out_shape=jax.ShapeDtypeStruct(theirs.shape, BF16),
        compiler_params=_params(2),
    )(core, mine, theirs)


def _sequencer_call(name, collective_id, job):
    steps, peers = job.phases, job.peers
    ins = [jax.new_ref(a, memory_space=pltpu.MemorySpace.HBM) for a in job.inputs]
    outs = [ins[{o: i for i, o in job.aliases.items()}[k]] if k in job.aliases.values()
            else jax.empty_ref(shape, memory_space=pltpu.MemorySpace.HBM) for k, shape in enumerate(job.out_shape)]
    sems = [pltpu.SemaphoreType.DMA((n,)) for step in steps for n in (step.n_sem, step.n_sem, max(step.n_local, 1))]

    @pl.kernel(mesh=plsc.ScalarSubcoreMesh(axis_name="sequencer", num_cores=1), name=name, scratch_types=tuple(sems),
               compiler_params=pltpu.CompilerParams(collective_id=collective_id))
    def launch(*sem_refs):
        x, y, c, _ = _place()
        barrier = pltpu.get_barrier_semaphore()
        for dx, dy, dc in peers:
            pl.semaphore_signal(barrier, inc=1, device_id=(x ^ dx, y ^ dy, c ^ dc), device_id_type=MESH)
        pl.semaphore_wait(barrier, len(peers))
        for k, step in enumerate(steps):
            sends, arrivals, own = step.copies(ins if k == 0 else outs, outs, *sem_refs[3 * k:3 * k + 3])
            for cp in own + sends:
                cp.start()
            for cp in arrivals:
                cp.wait_recv()
            for cp in sends:
                cp.wait_send()
            for cp in own:
                cp.wait()

    launch()
    return [ref[...] for ref in outs]


def _chip_exchange_job(sums):
    n = len(sums)

    def copies(ins, outs, send, recv, local):
        del local
        _, _, c, chips = _place()
        sends = [pltpu.make_async_remote_copy(
            src_ref=ins[w].at[_chip_index(*chip)], dst_ref=outs[w].at[j], send_sem=send.at[3 * w + j],
            recv_sem=recv.at[3 * w + j], device_id=(*chip, c), device_id_type=MESH)
            for w in range(n) for j, chip in enumerate(chips)]
        return sends, sends, []

    return _Job(sums, [jax.ShapeDtypeStruct((N_CHIPS - 1,) + s.shape[1:], s.dtype) for s in sums], 3 * n, copies,
                OTHER_CHIPS)


def _chip_sum(name, place, mine, theirs):
    _, h, cols = mine.shape
    rb = _row_block(h, 512)

    def body(place_ref, p_ref, q_ref, o_ref):
        del place_ref
        acc = p_ref[...].astype(F32)
        for j in range(N_CHIPS - 1):
            acc = acc + q_ref[j].astype(F32)
        o_ref[...] = acc

    return pl.pallas_call(
        body, name=name,
        grid_spec=pltpu.PrefetchScalarGridSpec(
            num_scalar_prefetch=1, grid=(h // rb,),
            in_specs=[pl.BlockSpec((None, rb, cols), lambda r, place_ref: (place_ref[0], r, 0)),
                      pl.BlockSpec((N_CHIPS - 1, rb, cols), lambda r, place_ref: (0, r, 0))],
            out_specs=pl.BlockSpec((None, rb, cols), lambda r, place_ref: (place_ref[1], r, 0))),
        out_shape=jax.ShapeDtypeStruct((2, h, cols), F32),
        compiler_params=_params(),
    )(place, mine, theirs)


def _share_job(bufs):
    n = len(bufs)

    def copies(ins, outs, send, recv, local):
        del ins, local
        x, y, c, _ = _place()

        def copy(w, half):
            return pltpu.make_async_remote_copy(
                src_ref=outs[w].at[half], dst_ref=outs[w].at[half], send_sem=send.at[w], recv_sem=recv.at[w],
                device_id=(x, y, 1 - c), device_id_type=MESH)

        return [copy(w, c) for w in range(n)], [copy(w, 1 - c) for w in range(n)], []

    return _Job(bufs, [jax.ShapeDtypeStruct(b.shape, b.dtype) for b in bufs], n, copies, SIBLING,
                aliases={w: w for w in range(n)})


SMALL_ROWS = 24
ROW_G1, ROW_CW, ROW_CB, ROW_BR, ROW_BI, ROW_LAM, ROW_LG, ROW_LB, ROW_G2, ROW_G3, ROW_LOSS, ROW_BS = (
    0, 1, 5, 6, 7, 8, 9, 10, 11, 12, 13, 16)
N_DEV = 8


def _pack_small(dcw, dcb, dbr, dbi, dlam, dlg, dlb, dg2, dg3, loss, dbs):
    def body(dcw_ref, dcb_ref, dbr_ref, dbi_ref, dlam_ref, dlg_ref, dlb_ref, dg2_ref, dg3_ref, loss_ref, dbs_ref, out):
        out[...] = jnp.zeros((SMALL_ROWS, D_MODEL), F32)
        for row, ref in ((ROW_CB, dcb_ref), (ROW_BR, dbr_ref), (ROW_BI, dbi_ref), (ROW_LAM, dlam_ref),
                         (ROW_LG, dlg_ref), (ROW_LB, dlb_ref), (ROW_G2, dg2_ref), (ROW_G3, dg3_ref)):
            out[row:row + 1, :] = ref[...]
        out[ROW_CW:ROW_CW + CONV_WIDTH, :] = dcw_ref[0:CONV_WIDTH, :]
        out[ROW_LOSS:ROW_LOSS + 1, 0:128] = loss_ref[0:1, :]
        out[ROW_BS:ROW_BS + GROUPS, 0:128] = jnp.transpose(dbs_ref[...])[0:GROUPS, :]

    vm = pl.BlockSpec(memory_space=pltpu.VMEM)
    return pl.pallas_call(
        body, name="pack_small", in_specs=[vm] * 11, out_specs=vm,
        out_shape=jax.ShapeDtypeStruct((SMALL_ROWS, D_MODEL), F32),
    )(dcw, dcb, dbr, dbi, dlam, dlg, dlb, dg2, dg3, loss, dbs)


def _gather_all_job(blocks):
    n = len(blocks)
    flips = [(dx, dy, dc) for dx in (0, 1) for dy in (0, 1) for dc in (0, 1)][1:]

    def copies(ins, outs, send, recv, local):
        x, y, c, _ = _place()
        me = 4 * x + 2 * y + c
        sends, arrivals, own = [], [], []
        for w in range(n):
            own.append(pltpu.make_async_copy(ins[w], outs[w].at[me], local.at[w]))
            for k, (dx, dy, dc) in enumerate(flips):
                peer = (x ^ dx, y ^ dy, c ^ dc)
                sem = dict(send_sem=send.at[7 * w + k], recv_sem=recv.at[7 * w + k])
                sends.append(pltpu.make_async_remote_copy(
                    src_ref=ins[w], dst_ref=outs[w].at[me], device_id=peer, device_id_type=MESH, **sem))
                arrivals.append(pltpu.make_async_remote_copy(
                    src_ref=ins[w], dst_ref=outs[w].at[4 * peer[0] + 2 * peer[1] + peer[2]], device_id=peer,
                    device_id_type=MESH, **sem))
        return sends, arrivals, own

    return _Job(blocks, [jax.ShapeDtypeStruct((N_DEV,) + b.shape, b.dtype) for b in blocks], 7 * n, copies,
                OTHER_CHIPS + SIBLING + tuple((dx, dy, 1) for dx, dy, _ in OTHER_CHIPS), n_local=n)


def _sum_small(vec_all, ws_all, dg1_all):
    def body(vec_ref, ws_ref, dg1_ref, vec_out, ws_out):
        vec, ws, dg1 = vec_ref[0], ws_ref[0], dg1_ref[0]
        for d in range(1, N_DEV):
            vec, ws, dg1 = vec + vec_ref[d], ws + ws_ref[d], dg1 + dg1_ref[d]
        vec_out[...] = vec
        vec_out[ROW_G1:ROW_G1 + 1, :] = dg1
        ws_out[...] = ws

    vm = pl.BlockSpec(memory_space=pltpu.VMEM)
    return pl.pallas_call(
        body, name="sum_small", in_specs=[vm] * 3, out_specs=[vm, vm],
        out_shape=[jax.ShapeDtypeStruct(vec_all.shape[1:], F32), jax.ShapeDtypeStruct(ws_all.shape[1:], F32)],
    )(vec_all, ws_all, dg1_all)


def _adamw_math(w, g, m, v):
    m = ADAM_B1 * m + (1.0 - ADAM_B1) * g
    v = ADAM_B2 * v + (1.0 - ADAM_B2) * (g * g)
    m_hat = m / (1.0 - ADAM_B1 ** ADAM_STEP)
    v_hat = v / (1.0 - ADAM_B2 ** ADAM_STEP)
    delta = (-ADAM_LR) * (m_hat / (jnp.sqrt(v_hat) + ADAM_EPS) + ADAM_WD * w)
    return delta, m, v


def _adamw(name, g, w, m, v, jobs=()):
    rows, cols = w.shape
    rb = _row_block(rows)

    def body(g_ref, w_ref, m_ref, v_ref, d_ref, nm_ref, nv_ref):
        d_ref[...], nm_ref[...], nv_ref[...] = _adamw_math(w_ref[...], g_ref[...], m_ref[...], v_ref[...])

    blk = pl.BlockSpec((rb, cols), lambda r: (r, 0))
    return _fused_call(
        body, jobs, name=name, grid=(rows // rb,), in_specs=[blk] * 4, out_specs=[blk] * 3,
        out_shape=[jax.ShapeDtypeStruct(w.shape, F32)] * 3, compiler_params=_params(),
    )(g, w, m, v)


def _adamw_small(grads, ws, ms, vs):
    n = len(grads)

    def body(*refs):
        g_refs, w_refs, m_refs, v_refs = refs[:n], refs[n:2 * n], refs[2 * n:3 * n], refs[3 * n:4 * n]
        outs = refs[4 * n:]
        for p in range(n):
            d, nm, nv = _adamw_math(w_refs[p][...], g_refs[p][...], m_refs[p][...], v_refs[p][...])
            outs[p][...] = d
            outs[n + p][...] = nm
            outs[2 * n + p][...] = nv

    vm = pl.BlockSpec(memory_space=pltpu.VMEM)
    shapes = [jax.ShapeDtypeStruct(w.shape, F32) for w in ws]
    out = pl.pallas_call(
        body, name="adamw_small", in_specs=[vm] * (4 * n), out_specs=[vm] * (3 * n), out_shape=shapes * 3,
    )(*grads, *ws, *ms, *vs)
    return out[:n], out[n:2 * n], out[2 * n:]


def _unstack_heads(w_st):
    per = HEAD_DIM // N_CHIPS
    return w_st.reshape(N_CHIPS, HEADS, per, HEAD_DIM).transpose(1, 0, 2, 3).reshape(HEADS, HEAD_DIM, HEAD_DIM)


def _stack_heads(w):
    per = HEAD_DIM // N_CHIPS
    return w.reshape(HEADS, N_CHIPS, per, HEAD_DIM).transpose(1, 0, 2, 3).reshape(N_CHIPS, HEADS * per, HEAD_DIM)


def kernel(x, norm_mix_g, w_in, conv_w, conv_b, w_rgate, b_rgate, w_igate, b_igate, lru_lambda, w_out_a, sgu_ln_g, sgu_ln_b, sgu_w_s, sgu_b_s, w_out_b, w_out, norm_mlp_g, w_up, w_down, norm_final_g, loss_target, m_norm_mix_g, m_w_in, m_conv_w, m_conv_b, m_w_rgate, m_b_rgate, m_w_igate, m_b_igate, m_lru_lambda, m_w_out_a, m_sgu_ln_g, m_sgu_ln_b, m_sgu_w_s, m_sgu_b_s, m_w_out_b, m_w_out, m_norm_mlp_g, m_w_up, m_w_down, m_norm_final_g, v_norm_mix_g, v_w_in, v_conv_w, v_conv_b, v_w_rgate, v_b_rgate, v_w_igate, v_b_igate, v_lru_lambda, v_w_out_a, v_sgu_ln_g, v_sgu_ln_b, v_sgu_w_s, v_sgu_b_s, v_w_out_b, v_w_out, v_norm_mlp_g, v_w_up, v_w_down, v_norm_final_g):
    chip = _chip_index(lax.axis_index("x"), lax.axis_index("y"))
    core = lax.axis_index("c")
    quarter_h = HEAD_DIM // N_CHIPS
    quarter_d = D_MODEL // N_CHIPS

    as_2d = lambda a: a.reshape(-1, a.shape[-1])
    big_w = [as_2d(w) for w in (w_in, w_rgate, w_igate, w_out_a, w_out_b, w_out, w_up, w_down)]
    big_m = [as_2d(w) for w in (m_w_in, m_w_rgate, m_w_igate, m_w_out_a, m_w_out_b, m_w_out, m_w_up, m_w_down)]
    big_v = [as_2d(w) for w in (v_w_in, v_w_rgate, v_w_igate, v_w_out_a, v_w_out_b, v_w_out, v_w_up, v_w_down)]

    packed = jnp.concatenate([conv_w[0], b_rgate[0], b_igate[0]], axis=1)
    packed = jnp.concatenate([packed, jnp.zeros_like(packed)], axis=0)
    s_in, s_r, s_i, s_oa, s_ob, s_out, s_up, s_down = [w.astype(BF16) for w in big_w]
    xs, target = x[0], loss_target[0]
    g3 = norm_final_g.reshape(1, D_MODEL)
    bias_s = jnp.broadcast_to(jnp.transpose(sgu_b_s[0])[:, :, None], (CHUNK, GROUPS, GROUP_DIM)).reshape(CHUNK, D_MODEL)
    core_arr = core.reshape(1).astype(jnp.int32)
    place = jnp.stack([chip, core]).astype(jnp.int32)
    quarter = lambda g: g.reshape(N_CHIPS, D_MODEL // N_CHIPS, D_MODEL)

    def pair_add(nm, g, from_sibling):
        return _pair_add("pair_add_" + nm, core_arr, g.reshape(N_CHIPS, 2, g.shape[1] // 2, g.shape[2]), from_sibling)

    def chip_sum(nm, pair, from_chips):
        return _chip_sum("chip_sum_" + nm, place, pair, from_chips)

    order = jnp.stack([chip, chip ^ 2, chip ^ 1, chip ^ 3]).astype(jnp.int32)
    (z, n1, (w_in_st, wr_st, wi_st)), ((packed_all,), late) = _fwd_in(
        xs, norm_mix_g, [s_in, s_r, s_i], order,
        jobs=[_gather_small_job(packed), _gather_near_job([s_oa, s_ob, s_out])])
    pick = lambda lo, hi: packed_all[:, :HEADS, lo:hi].transpose(1, 0, 2).reshape(HEADS, -1)
    conv_w_full = pick(0, quarter_d)
    br_full = pick(quarter_d, quarter_d + quarter_h).reshape(1, D_MODEL)
    bi_full = pick(quarter_d + quarter_h, quarter_d + 2 * quarter_h).reshape(1, D_MODEL)
    wr, wi = _unstack_heads(wr_st), _unstack_heads(wi_st)
    lru = (conv_w_full, conv_b, wr, br_full, wi, bi_full, lru_lambda)
    sgu = (sgu_ln_g, sgu_ln_b, sgu_w_s[0], bias_s)

    after = lambda arrays, result: lax.optimization_barrier((arrays, result))[0]
    w_up_st, w_dn = _sequencer_call(
        "gather_mlp", 8, _gather_near_job(after([s_up, s_down], n1)).then(_gather_far_job).then(_gather_pass_job))
    w_dn = w_dn.reshape(D_FF, D_MODEL)
    late_step = (3 * (xs.shape[0] // SEQ_TILE) // 4,)
    (ya, *saved), (late,) = _fwd_lru(z, *lru, jobs=[_gather_far_job(late).then(_gather_pass_job, at=late_step)])
    w_oa, w_ob, w_o = [w.reshape(D_MODEL, D_MODEL) for w in late]
    (yb, pa, pb, h1, n2), _ = _fwd_sgu_merge(ya, z, xs, *sgu, w_oa, w_ob, w_o, norm_mlp_g)
    (act, dup, dh2b, dh1, loss_part, dg3, dg2), _ = _mlp(n2, h1, target, w_up_st, w_dn, norm_mlp_g, g3)

    d_down, _ = _weight_grad("dw_down", act, dh2b, N_CHIPS, True, False, D_MODEL)
    r_down, = _sequencer_call("send_w_down", 10, _pair_send_job([d_down]))
    d_up, _ = _weight_grad("dw_up", n2, dup, N_CHIPS, False, True, D_MODEL)
    r_up, = _sequencer_call("send_w_up", 11, _pair_send_job([d_up]))
    p_down, p_up = pair_add("w_down", d_down, r_down), pair_add("w_up", d_up, r_up)
    (dz, merged, dpa, dpb, dh1b, dlg, dlb, dws, dbs, dcw, dcb, dwr, dbr, dwi, dbi, dlam), ((q_up, q_down),) = _bwd_mix(
        dh1, pa, pb, z, *saved, w_oa, w_ob, w_o, *sgu, conv_w_full, wr, wi, lru_lambda,
        jobs=[_chip_exchange_job([p_up, p_down])])
    half_up, half_down = chip_sum("w_up", p_up, q_up), chip_sum("w_down", p_down, q_down)
    names = ("w_in", "w_rgate", "w_igate", "w_out_a", "w_out_b", "w_out", "w_up", "w_down")
    (d_out, d_oa, d_ob), ((full_up, full_down),) = _weight_grads_square(
        "dw_projections", [(merged, dh1b), (ya, dpa), (yb, dpb)], jobs=[_share_job([half_up, half_down])])
    mids = [quarter(d_oa), quarter(d_ob), quarter(d_out)]
    r_mids = _sequencer_call("send_mids", 1, _pair_send_job(mids))
    gates = [_stack_heads(dwr).astype(BF16), _stack_heads(dwi).astype(BF16)]
    small = _pack_small(dcw, dcb, dbr, dbi, dlam, dlg, dlb, dg2, dg3, loss_part, dbs)
    p_mids = [pair_add(nm, g, r) for nm, g, r in zip(names[3:6], mids, r_mids)]
    q_mids = _sequencer_call("exchange_mids", 2, _chip_exchange_job(p_mids))
    d_in, (r_gates, (vec_all, ws_all)) = _weight_grad(
        "dw_in", n1, dz, N_CHIPS, False, True, IN_SHARD,
        jobs=[_pair_send_job(gates), _gather_all_job([small, dws])])
    r_in, = _sequencer_call("send_w_in", 3, _pair_send_job([d_in]))
    adam_args = {nm: (w, m, v) for nm, w, m, v in zip(names, big_w, big_m, big_v)}

    def adamw(nm, g):
        w, m, v = adam_args[nm]
        g = g.reshape(w.shape)
        return g, _adamw("adamw_" + nm, g, w, m, v)[0]

    p_gates = [pair_add(nm, g, r) for nm, g, r in zip(names[1:3], gates, r_gates)]
    half_mids = [chip_sum(nm, p, q) for nm, p, q in zip(names[3:6], p_mids, q_mids)]
    full_mids = _sequencer_call("share_mids", 12, _share_job(half_mids))
    p_first = [pair_add("w_in", d_in, r_in)] + p_gates
    q_first = _sequencer_call("exchange_w_in", 4, _chip_exchange_job(p_first))
    (grad_x, dg1), _ = _bwd_in(dz, xs, dh1, w_in_st, norm_mix_g)
    dg1_all, = _sequencer_call("gather_dg1", 6, _gather_all_job([dg1]))
    done = {nm: adamw(nm, f) for nm, f in zip(("w_up", "w_down") + names[3:6], [full_up, full_down] + full_mids)}
    q_first = after(q_first, [out[0] for _, out in done.values()])
    half_first = [chip_sum(nm, p, q) for nm, p, q in zip(names[:3], p_first, q_first)]
    full_first = _sequencer_call("share_last", 5, _share_job(half_first))
    done.update({nm: adamw(nm, f) for nm, f in zip(names[:3], full_first)})
    full, big_out = [done[nm][0] for nm in names], [done[nm][1] for nm in names]

    vec, ws_sum = _sum_small(vec_all, ws_all, dg1_all)
    row = lambda r: vec[r:r + 1]
    shard = lambda a, width: lax.dynamic_slice_in_dim(a, chip * width, width, axis=1)
    g_small = dict(
        norm_mix_g=row(ROW_G1), conv_w=shard(vec[ROW_CW:ROW_CW + CONV_WIDTH], quarter_d), conv_b=row(ROW_CB),
        b_rgate=shard(row(ROW_BR).reshape(HEADS, HEAD_DIM), quarter_h),
        b_igate=shard(row(ROW_BI).reshape(HEADS, HEAD_DIM), quarter_h), lru_lambda=row(ROW_LAM),
        sgu_ln_g=row(ROW_LG), sgu_ln_b=row(ROW_LB),
        sgu_w_s=ws_sum.reshape(CHUNK, GROUPS, CHUNK).transpose(1, 0, 2).reshape(GROUPS * CHUNK, CHUNK),
        sgu_b_s=vec[ROW_BS:ROW_BS + GROUPS, 0:CHUNK], norm_mlp_g=row(ROW_G2), norm_final_g=row(ROW_G3))
    loss = vec[ROW_LOSS, 0]
    small_names = list(g_small)
    given = dict(
        norm_mix_g=(norm_mix_g, m_norm_mix_g, v_norm_mix_g), conv_w=(conv_w, m_conv_w, v_conv_w),
        conv_b=(conv_b, m_conv_b, v_conv_b), b_rgate=(b_rgate, m_b_rgate, v_b_rgate),
        b_igate=(b_igate, m_b_igate, v_b_igate), lru_lambda=(lru_lambda, m_lru_lambda, v_lru_lambda),
        sgu_ln_g=(sgu_ln_g, m_sgu_ln_g, v_sgu_ln_g), sgu_ln_b=(sgu_ln_b, m_sgu_ln_b, v_sgu_ln_b),
        sgu_w_s=(sgu_w_s, m_sgu_w_s, v_sgu_w_s), sgu_b_s=(sgu_b_s, m_sgu_b_s, v_sgu_b_s),
        norm_mlp_g=(norm_mlp_g, m_norm_mlp_g, v_norm_mlp_g), norm_final_g=(norm_final_g, m_norm_final_g, v_norm_final_g))
    g2d = [g_small[nm] for nm in small_names]
    to2d = lambda a, g: a.reshape(g.shape)
    d_s, m_s, v_s = _adamw_small(
        g2d, *[[to2d(given[nm][q], g) for nm, g in zip(small_names, g2d)] for q in range(3)])

    shapes = dict(
        norm_mix_g=norm_mix_g, w_in=w_in, conv_w=conv_w, conv_b=conv_b, w_rgate=w_rgate, b_rgate=b_rgate,
        w_igate=w_igate, b_igate=b_igate, lru_lambda=lru_lambda, w_out_a=w_out_a, sgu_ln_g=sgu_ln_g,
        sgu_ln_b=sgu_ln_b, sgu_w_s=sgu_w_s, sgu_b_s=sgu_b_s, w_out_b=w_out_b, w_out=w_out, norm_mlp_g=norm_mlp_g,
        w_up=w_up, w_down=w_down, norm_final_g=norm_final_g)
    grads, deltas, new_m, new_v = {}, {}, {}, {}
    for nm, g, (d, nmom, nvar) in zip(names, full, big_out):
        grads[nm], deltas[nm], new_m[nm], new_v[nm] = g, d, nmom, nvar
    for p, nm in enumerate(small_names):
        grads[nm], deltas[nm], new_m[nm], new_v[nm] = g2d[p], d_s[p], m_s[p], v_s[p]
    order = list(shapes)
    out = [loss, grad_x[None]]
    for group in (grads, deltas, new_m, new_v):
        out += [group[nm].reshape(shapes[nm].shape) for nm in order]
    return tuple(out)
```

```python
import functools

import jax
import jax.numpy as jnp
from jax import lax
from jax.experimental import pallas as pl
from jax.experimental.pallas import tpu as pltpu
from jax.experimental.pallas import tpu_sc as plsc

F32 = jnp.float32
BF16 = jnp.bfloat16
MESH = pl.DeviceIdType.MESH

D_MODEL = 1024
D_IN = 6 * D_MODEL
D_FF = 4 * D_MODEL
N_CHIPS = 4
IN_SHARD = D_IN // N_CHIPS
HEADS = 4
HEAD_DIM = D_MODEL // HEADS
GROUPS = 4
GROUP_DIM = D_MODEL // GROUPS
CHUNK = 128
CONV_WIDTH = 4
LRU_C = 8.0
NORM_EPS = 1e-6
LN_EPS = 1e-5

ADAM_LR = 0.001
ADAM_B1 = 0.9
ADAM_B2 = 0.999
ADAM_EPS = 1e-08
ADAM_WD = 0.01
ADAM_STEP = 10

SUBLANES = 8
MM_TILE = 512
IN_TILE = 1024
SEQ_TILE = 256
DW_TILE = 2048
VMEM_LIMIT_BYTES = 56 * 1024 * 1024

GELU_K0 = 0.7978845608028654
GELU_K1 = 0.044715


def _params(n_grid_axes=1):
    return pltpu.CompilerParams(
        dimension_semantics=("arbitrary",) * n_grid_axes, vmem_limit_bytes=VMEM_LIMIT_BYTES)


def _resident(shape):
    nd = len(shape)
    return pl.BlockSpec(shape, lambda *_: (0,) * nd, pipeline_mode=pl.Buffered(1))


def _const(shape):
    nd = len(shape)
    return pl.BlockSpec(shape, lambda *_: (0,) * nd)


def _dot(a, b):
    return jnp.dot(a, b, preferred_element_type=F32)


def _dot_nt(a, b):
    return lax.dot_general(a, b, (((1,), (1,)), ((), ())), preferred_element_type=F32)


def _dot_tn(a, b):
    return lax.dot_general(a, b, (((0,), (0,)), ((), ())), preferred_element_type=F32)


def _gelu(x):
    t = jnp.tanh(x * (GELU_K0 + (GELU_K0 * GELU_K1) * (x * x)))
    return x * (0.5 + 0.5 * t)


def _gelu_and_grad(x):
    x2 = x * x
    t = jnp.tanh(x * (GELU_K0 + (GELU_K0 * GELU_K1) * x2))
    s = 0.5 + 0.5 * t
    dg = s + (x * (1.0 - t * t)) * (0.5 * GELU_K0 + (1.5 * GELU_K0 * GELU_K1) * x2)
    return x * s, dg


def _gate(x):
    return 0.5 + 0.5 * jnp.tanh(0.5 * x)


def _rms(x):
    r = lax.rsqrt(jnp.mean(x * x, axis=-1, keepdims=True) + NORM_EPS)
    return x * r, r


def _rms_bwd(dn, xhat, r):
    return r * (dn - xhat * jnp.mean(dn * xhat, axis=-1, keepdims=True))


def _col_sum(v):
    return jnp.sum(v, axis=0, keepdims=True)


def _shift_down(x, tail8, k):
    xs = pltpu.roll(x, k, 0)
    ts = pltpu.roll(tail8, k, 0)
    ridx = lax.broadcasted_iota(jnp.int32, tail8.shape, 0)
    head = jnp.where(ridx < k, ts, xs[0:SUBLANES])
    return jnp.concatenate([head, xs[SUBLANES:]], axis=0)


def _shift_up(x, head8, k):
    n = x.shape[0]
    xs = pltpu.roll(x, n - k, 0)
    hs = pltpu.roll(head8, SUBLANES - k, 0)
    ridx = lax.broadcasted_iota(jnp.int32, head8.shape, 0)
    last = jnp.where(ridx >= SUBLANES - k, hs, xs[n - SUBLANES:n])
    return jnp.concatenate([xs[:n - SUBLANES], last], axis=0)


def _scan_forward(a, b, carry):
    n, cols = a.shape
    groups = n // SUBLANES
    a = a.reshape(groups, SUBLANES, cols)
    b = b.reshape(groups, SUBLANES, cols)
    sub = lax.broadcasted_iota(jnp.int32, a.shape, 1)
    for s in (1, 2, 4):
        a_s = pltpu.roll(a, s, 1)
        b_s = pltpu.roll(b, s, 1)
        m = sub >= s
        b = jnp.where(m, a * b_s + b, b)
        a = jnp.where(m, a * a_s, a)
    out = []
    for g in range(groups):
        h = a[g] * carry + b[g]
        out.append(h)
        carry = h[SUBLANES - 1:SUBLANES]
    return jnp.concatenate(out, axis=0), carry


def _scan_backward(a, b, carry):
    n, cols = a.shape
    groups = n // SUBLANES
    a = a.reshape(groups, SUBLANES, cols)
    b = b.reshape(groups, SUBLANES, cols)
    sub = lax.broadcasted_iota(jnp.int32, a.shape, 1)
    for s in (1, 2, 4):
        a_s = pltpu.roll(a, SUBLANES - s, 1)
        b_s = pltpu.roll(b, SUBLANES - s, 1)
        m = sub < SUBLANES - s
        b = jnp.where(m, a * b_s + b, b)
        a = jnp.where(m, a * a_s, a)
    out = [None] * groups
    for g in reversed(range(groups)):
        h = a[g] * carry + b[g]
        out[g] = h
        carry = h[0:1]
    return jnp.concatenate(out, axis=0), carry


def _softplus_neg(lam):
    e = jnp.exp(-jnp.abs(lam))
    u = 1.0 + e
    log1p_e = jnp.where(u == 1.0, e, jnp.log(u) * (e / jnp.where(u == 1.0, 1.0, u - 1.0)))
    return jnp.maximum(-lam, 0.0) + log1p_e


def _lru_gates(xa, tail8, cw_ref, cb_ref, wr_ref, br_ref, wi_ref, bi_ref, lam_ref):
    cw = cw_ref[...]
    xc = cb_ref[...] + cw[0:1] * xa
    for k in range(1, CONV_WIDTH):
        xc = xc + cw[k:k + 1] * _shift_down(xa, tail8, k)
    xcb = xc.astype(BF16)
    pre_r, pre_i = [], []
    for h in range(HEADS):
        cols = slice(h * HEAD_DIM, (h + 1) * HEAD_DIM)
        pre_r.append(_dot(xcb[:, cols], wr_ref[h]))
        pre_i.append(_dot(xcb[:, cols], wi_ref[h]))
    r = _gate(jnp.concatenate(pre_r, axis=1) + br_ref[...])
    ig = _gate(jnp.concatenate(pre_i, axis=1) + bi_ref[...])
    _, a, mult, _ = _decay(r, lam_ref)
    return xc, r, ig, a, mult


def _decay(r, lam_ref):
    sp = _softplus_neg(lam_ref[...])
    log_a = ((-LRU_C) * sp) * r
    a = jnp.exp(log_a)
    th = jnp.tanh(log_a)
    q = (-2.0 * th) / (1.0 - th)
    inv = lax.rsqrt(q)
    return sp, a, jnp.where(q > 0.0, q * inv, 0.0), inv


class _Phase:
    def __init__(self, copies, n_sem, n_local, start=None, finish=None):
        self.copies, self.n_sem, self.n_local, self.start, self.finish = copies, n_sem, n_local, start, finish


class _Job:
    def __init__(self, inputs, out_shape, n_sem, copies, peers, aliases=None, n_local=0):
        self.inputs, self.out_shape = list(inputs), list(out_shape)
        self.aliases = dict(aliases or {})
        self.phases = [_Phase(copies, n_sem, n_local)]
        self.peers = tuple(peers)

    def then(self, make, at=None):
        nxt = make(self.out_shape)
        self.phases[-1].finish = at
        nxt.phases[0].start = at
        self.phases += nxt.phases
        self.peers = tuple(sorted(set(self.peers + nxt.peers)))
        return self


def _fused_call(body, jobs, *, name, grid, in_specs, out_specs, out_shape, scratch_shapes=(),
                input_output_aliases=None, compiler_params=None, n_prefetch=0, jobs_start_after=None):
    single = not isinstance(out_shape, (list, tuple))
    out_specs = [out_specs] if single else list(out_specs)
    out_shape = [out_shape] if single else list(out_shape)
    n_scr = len(scratch_shapes)
    in_specs, scratch_shapes = list(in_specs), list(scratch_shapes)
    n_in, n_out = len(in_specs), len(out_shape)
    aliases = dict(input_output_aliases or {})
    in_at, out_at, phases = [], [], []
    for q, job in enumerate(jobs):
        in_at.append(len(in_specs))
        out_at.append(len(out_shape))
        for i, o in job.aliases.items():
            aliases[n_prefetch + len(in_specs) + i] = len(out_shape) + o
        in_specs += [ANY] * len(job.inputs)
        out_specs += [ANY] * len(job.out_shape)
        out_shape += job.out_shape
        for k, phase in enumerate(job.phases):
            phases.append((q, k, phase, len(scratch_shapes)))
            scratch_shapes += [pltpu.SemaphoreType.DMA((phase.n_sem,)), pltpu.SemaphoreType.DMA((phase.n_sem,)),
                               pltpu.SemaphoreType.DMA((max(phase.n_local, 1),))]
    n_in_all, n_out_all = len(in_specs), len(out_shape)
    first_step, last_step = (0,) * len(grid), tuple(g - 1 for g in grid)

    def full_body(*refs):
        prefetch, refs = refs[:n_prefetch], refs[n_prefetch:]
        ins, outs, scr = refs[:n_in_all], refs[n_in_all:n_in_all + n_out_all], refs[n_in_all + n_out_all:]
        ids = [pl.program_id(a) for a in range(len(grid))]
        at_step = lambda step: functools.reduce(jnp.logical_and, [i == k for i, k in zip(ids, step)])

        def copies(q, k, phase, sem_at):
            job = jobs[q]
            mine = outs[out_at[q]:out_at[q] + len(job.out_shape)]
            return phase.copies(ins[in_at[q]:in_at[q] + len(job.inputs)] if k == 0 else mine, mine,
                                *scr[sem_at:sem_at + 3])

        def start(*phase):
            def go():
                sends, _, local = copies(*phase)
                for cp in local + sends:
                    cp.start()
            return go

        def finish(*phase):
            def go():
                sends, arrivals, local = copies(*phase)
                for cp in arrivals:
                    cp.wait_recv()
                for cp in sends:
                    cp.wait_send()
                for cp in local:
                    cp.wait()
            return go

        for phase in phases:
            if phase[2].start is None and jobs_start_after is None:
                pl.when(at_step(first_step))(start(*phase))
        body(*prefetch, *ins[:n_in], *outs[:n_out], *scr[:n_scr])
        for phase in phases:
            pl.when(at_step(phase[2].finish or last_step))(finish(*phase))
            nxt = phase[2].start or jobs_start_after
            if nxt is not None:
                pl.when(at_step(nxt))(start(*phase))

    if n_prefetch:
        layout = dict(grid_spec=pltpu.PrefetchScalarGridSpec(
            num_scalar_prefetch=n_prefetch, grid=grid, in_specs=in_specs, out_specs=out_specs,
            scratch_shapes=scratch_shapes))
    else:
        layout = dict(grid=grid, in_specs=in_specs, out_specs=out_specs, scratch_shapes=scratch_shapes)
    call = pl.pallas_call(
        full_body, name=name, out_shape=out_shape, input_output_aliases=aliases, compiler_params=compiler_params,
        **layout)

    def run(*args):
        res = call(*args, *[a for job in jobs for a in job.inputs])
        mine = res[0] if single else list(res[:n_out])
        return mine, [list(res[at:at + len(job.out_shape)]) for at, job in zip(out_at, jobs)]

    return run


def _fwd_in(x, g1, shards, order, jobs=()):
    t = x.shape[0]
    rows_per_step = min(IN_TILE, t)
    n_tiles = t // rows_per_step
    n = len(shards)
    halves = [s.shape[0] // 2 for s in shards]

    def body(order_ref, x_ref, g_ref, *refs):
        del order_ref
        ins, (z_ref, n_ref), outs = refs[:n], refs[n:n + 2], refs[n + 2:2 * n + 2]
        wbuf, nbuf, send, recv, local = refs[2 * n + 2:]
        s, i = pl.program_id(0), pl.program_id(1)
        x_, y_, c, chips = _place()
        near, far = _near_far(x_, y_, c)
        k_me = _chip_index(x_, y_)

        def block(w, chip, pc):
            return outs[w].at[_chip_index(*chip), pl.ds(pc * halves[w], halves[w]), :]

        def over_ici(w, j, landing):
            return pltpu.make_async_remote_copy(
                src_ref=ins[w].at[pl.ds(c * halves[w], halves[w]), :],
                dst_ref=block(w, chips[j] if landing else (x_, y_), c), send_sem=send.at[6 * w + j],
                recv_sem=recv.at[6 * w + j], device_id=(*chips[j], c), device_id_type=MESH)

        def onward(w, landing):
            blk = block(w, chips[2] if landing else near, c)
            return pltpu.make_async_remote_copy(
                src_ref=blk, dst_ref=blk, send_sem=send.at[6 * w + 2], recv_sem=recv.at[6 * w + 2],
                device_id=(*far, c), device_id_type=MESH)

        def to_sibling(w, j, landing):
            blk = block(w, chips[j], 1 - c if landing else c)
            return pltpu.make_async_remote_copy(
                src_ref=blk, dst_ref=blk, send_sem=send.at[6 * w + 3 + j], recv_sem=recv.at[6 * w + 3 + j],
                device_id=(x_, y_, 1 - c), device_id_type=MESH)

        own = [pltpu.make_async_copy(wbuf, outs[0].at[k_me], local.at[0])]
        own += [pltpu.make_async_copy(ins[w], outs[w].at[k_me], local.at[w]) for w in range(1, n)]

        @pl.when((s == 0) & (i == 0))
        def _():
            for j in range(2):
                for w in range(n):
                    over_ici(w, j, False).start()
            load = pltpu.make_async_copy(ins[0], wbuf, local.at[n])
            load.start()
            load.wait()
            for cp in own:
                cp.start()

        for j in range(N_CHIPS - 1):
            @pl.when((s == j + 1) & (i == 0))
            def _(j=j):
                if j == 0:
                    for k in range(2):
                        for w in range(n):
                            over_ici(w, k, True).wait_recv()
                    for w in range(n):
                        onward(w, False).start()
                    for k in range(2):
                        for w in range(n):
                            to_sibling(w, k, False).start()
                    own[0].wait()
                if j == 2:
                    for w in range(n):
                        onward(w, True).wait_recv()
                    for w in range(n):
                        to_sibling(w, j, False).start()
                for w in range(n):
                    to_sibling(w, j, True).wait_recv()
                load = pltpu.make_async_copy(outs[0].at[_chip_index(*chips[j])], wbuf, local.at[n])
                load.start()
                load.wait()

        rows = pl.ds(pl.multiple_of(i * rows_per_step, rows_per_step), rows_per_step)

        @pl.when(s == 0)
        def _():
            xhat, _ = _rms(x_ref[...])
            nrm = (xhat * g_ref[...]).astype(BF16)
            nbuf[rows, :] = nrm
            n_ref[...] = nrm

        z_ref[...] = _dot(nbuf[rows, :], wbuf[...])

        @pl.when((s == N_CHIPS - 1) & (i == n_tiles - 1))
        def _():
            for j in range(N_CHIPS - 1):
                for w in range(n):
                    (over_ici(w, j, False) if j < 2 else onward(w, False)).wait_send()
                    to_sibling(w, j, False).wait_send()
            for cp in own[1:]:
                cp.wait()

    once = lambda s, i, order: (jnp.where(s == 0, i, n_tiles - 1), 0)
    (z, n1, *stacked), job_outs = _fused_call(
        body, jobs, name="fwd_in", grid=(N_CHIPS, n_tiles), n_prefetch=1,
        in_specs=[pl.BlockSpec((rows_per_step, D_MODEL), once), _const((1, D_MODEL))] + [ANY] * n,
        out_specs=[pl.BlockSpec((rows_per_step, IN_SHARD), lambda s, i, order: (i, order[s])),
                   pl.BlockSpec((rows_per_step, D_MODEL), once)] + [ANY] * n,
        out_shape=[jax.ShapeDtypeStruct((t, D_IN), F32), jax.ShapeDtypeStruct((t, D_MODEL), BF16)]
        + [jax.ShapeDtypeStruct((N_CHIPS,) + s.shape, s.dtype) for s in shards],
        scratch_shapes=[pltpu.VMEM(shards[0].shape, BF16), pltpu.VMEM((t, D_MODEL), BF16),
                        pltpu.SemaphoreType.DMA((6 * n,)),
                        pltpu.SemaphoreType.DMA((6 * n,)), pltpu.SemaphoreType.DMA((n + 1,))],
        compiler_params=_params(2), jobs_start_after=(1, 0),
    )(order, x, g1, *shards)
    return (z, n1, stacked), job_outs


def _fwd_lru(z, conv_w, conv_b, wr, br, wi, bi, lam, jobs=()):
    t = z.shape[0]

    def body(xa_ref, ga_ref, cw_ref, cb_ref, wr_ref, br_ref, wi_ref, bi_ref, lam_ref, ya_ref, h_ref, xc_ref, r_ref,
             ig_ref, tail_ref, carry_ref):
        @pl.when(pl.program_id(0) == 0)
        def _():
            tail_ref[...] = jnp.zeros_like(tail_ref)
            carry_ref[...] = jnp.zeros_like(carry_ref)

        xa = xa_ref[...]
        xc, r, ig, a, mult = _lru_gates(xa, tail_ref[...], cw_ref, cb_ref, wr_ref, br_ref, wi_ref, bi_ref, lam_ref)
        tail_ref[...] = xa[SEQ_TILE - SUBLANES:]
        xc_ref[...], r_ref[...], ig_ref[...] = xc, r, ig
        h, carry = _scan_forward(a, xc * ig * mult, carry_ref[...])
        carry_ref[...] = carry
        h_ref[...] = h
        ya_ref[...] = (h * _gelu(ga_ref[...])).astype(BF16)

    tile = lambda j: pl.BlockSpec((SEQ_TILE, D_MODEL), lambda i: (i, j))
    return _fused_call(
        body, jobs, name="fwd_lru", grid=(t // SEQ_TILE,),
        in_specs=[tile(0), tile(1), _const((CONV_WIDTH, D_MODEL)), _const((1, D_MODEL)),
                  _resident((HEADS, HEAD_DIM, HEAD_DIM)), _const((1, D_MODEL)),
                  _resident((HEADS, HEAD_DIM, HEAD_DIM)), _const((1, D_MODEL)), _const((1, D_MODEL))],
        out_specs=[tile(0)] * 5,
        out_shape=[jax.ShapeDtypeStruct((t, D_MODEL), BF16)] + [jax.ShapeDtypeStruct((t, D_MODEL), F32)] * 4,
        scratch_shapes=[pltpu.VMEM((SUBLANES, D_MODEL), F32), pltpu.VMEM((1, D_MODEL), F32)],
        compiler_params=_params(),
    )(z, z, conv_w, conv_b, wr, br, wi, bi, lam)


def _sgu_forward_parts(ub, vb, lg_ref, lb_ref):
    u, du = _gelu_and_grad(ub)
    vg, dvg = _gelu_and_grad(vb)
    mu = jnp.mean(vg, axis=-1, keepdims=True)
    d = vg - mu
    rstd = lax.rsqrt(jnp.mean(d * d, axis=-1, keepdims=True) + LN_EPS)
    vhat = d * rstd
    vn = (vhat * lg_ref[...] + lb_ref[...]).astype(BF16)
    return u, du, dvg, rstd, vhat, vn


def _causal_mask():
    rows = lax.broadcasted_iota(jnp.int32, (CHUNK, CHUNK), 0)
    cols = lax.broadcasted_iota(jnp.int32, (CHUNK, CHUNK), 1)
    return rows >= cols


def _fwd_sgu_merge(ya, z, x, ln_g, ln_b, w_s, bias_full, w_oa, w_ob, w_out, g2, jobs=()):
    t = x.shape[0]

    def body(ya_ref, ub_ref, vb_ref, m_ref, x_ref, lg_ref, lb_ref, ws_ref, bias_ref, woa_ref, wob_ref, wout_ref, g_ref,
             yb_ref, h1_ref, n2_ref):
        u, _, _, _, _, vn = _sgu_forward_parts(ub_ref[...], vb_ref[...], lg_ref, lb_ref)
        mask = _causal_mask()
        wm = [jnp.where(mask, ws_ref[g], 0.0).astype(BF16) for g in range(GROUPS)]
        for c in range(SEQ_TILE // CHUNK):
            rows = slice(c * CHUNK, (c + 1) * CHUNK)
            for g in range(GROUPS):
                cols = slice(g * GROUP_DIM, (g + 1) * GROUP_DIM)
                sp = _dot(wm[g], vn[rows, cols]) + bias_ref[:, cols]
                yb_ref[rows, cols] = (u[rows, cols] * sp).astype(BF16)
        pa = _dot(ya_ref[...], woa_ref[...])
        pb = _dot(yb_ref[...], wob_ref[...])
        merged = _gate(m_ref[:, :D_MODEL]) * pa + _gate(m_ref[:, D_MODEL:]) * pb
        h1 = x_ref[...] + _dot(merged.astype(BF16), wout_ref[...])
        h1_ref[...] = h1
        xhat, _ = _rms(h1)
        n2_ref[...] = (xhat * g_ref[...]).astype(BF16)

    tile = lambda j: pl.BlockSpec((SEQ_TILE, D_MODEL), lambda i: (i, j))
    sq = _resident((D_MODEL, D_MODEL))
    vec = _const((1, D_MODEL))
    bf, f32 = jax.ShapeDtypeStruct((t, D_MODEL), BF16), jax.ShapeDtypeStruct((t, D_MODEL), F32)
    return _fused_call(
        body, jobs, name="fwd_sgu_merge", grid=(t // SEQ_TILE,),
        in_specs=[tile(0), tile(2), tile(3), pl.BlockSpec((SEQ_TILE, 2 * D_MODEL), lambda i: (i, 2)), tile(0), vec, vec,
                  _const((GROUPS, CHUNK, CHUNK)), _const((CHUNK, D_MODEL)), sq, sq, sq, vec],
        out_specs=[tile(0)] * 3,
        out_shape=[bf, f32, bf],
        compiler_params=_params(),
    )(ya, z, z, z, x, ln_g, ln_b, w_s, bias_full, w_oa, w_ob, w_out, g2)


def _mlp(n2, h1, target, w_up_st, w_down, g2, g3, jobs=()):
    t = n2.shape[0]

    def body(n2_ref, h1_ref, tgt_ref, wup_ref, wdown_ref, g2_ref, g3_ref, act_ref, dup_ref, dh2b_ref, dh1_ref,
             loss_ref, dg3_ref, dg2_ref, relu_ref):
        @pl.when(pl.program_id(0) == 0)
        def _():
            for ref in (loss_ref, dg3_ref, dg2_ref):
                ref[...] = jnp.zeros_like(ref)

        n2 = n2_ref[...]
        h1 = h1_ref[...]
        h2 = h1
        for k in range(N_CHIPS):
            cols = slice(k * D_MODEL, (k + 1) * D_MODEL)
            r = jnp.maximum(_dot(n2, wup_ref[k]), 0.0)
            relu_ref[:, cols] = r
            act = (r * r).astype(BF16)
            act_ref[:, cols] = act
            h2 = h2 + _dot(act, wdown_ref[cols, :])
        xhat, r3 = _rms(h2)
        diff = xhat * g3_ref[...] - tgt_ref[...]
        sq = jnp.sum(diff * diff, axis=1, keepdims=True)
        loss_ref[...] = loss_ref[...] + (0.5 / D_MODEL) * jnp.sum(sq, axis=0, keepdims=True)
        dy = diff * (1.0 / D_MODEL)
        dg3_ref[...] = dg3_ref[...] + _col_sum(dy * xhat)
        dh2 = _rms_bwd(dy * g3_ref[...], xhat, r3)
        dh2b = dh2.astype(BF16)
        dh2b_ref[...] = dh2b
        dn2 = jnp.zeros((SEQ_TILE, D_MODEL), F32)
        for k in range(N_CHIPS):
            cols = slice(k * D_MODEL, (k + 1) * D_MODEL)
            dup = (_dot_nt(dh2b, wdown_ref[cols, :]) * (2.0 * relu_ref[:, cols])).astype(BF16)
            dup_ref[:, cols] = dup
            dn2 = dn2 + _dot_nt(dup, wup_ref[k])
        xhat, r2 = _rms(h1)
        dg2_ref[...] = dg2_ref[...] + _col_sum(dn2 * xhat)
        dh1_ref[...] = dh2 + _rms_bwd(dn2 * g2_ref[...], xhat, r2)

    tile = pl.BlockSpec((SEQ_TILE, D_MODEL), lambda i: (i, 0))
    wide = pl.BlockSpec((SEQ_TILE, D_FF), lambda i: (i, 0))
    vec = _const((1, D_MODEL))
    vec_shape = jax.ShapeDtypeStruct((1, D_MODEL), F32)
    return _fused_call(
        body, jobs, name="mlp", grid=(t // SEQ_TILE,),
        in_specs=[tile, tile, tile, _resident((N_CHIPS, D_MODEL, D_MODEL)), _resident((D_FF, D_MODEL)), vec, vec],
        out_specs=[wide, wide, tile, tile, _const((SUBLANES, 128)), vec, vec],
        out_shape=[jax.ShapeDtypeStruct((t, D_FF), BF16), jax.ShapeDtypeStruct((t, D_FF), BF16),
                   jax.ShapeDtypeStruct((t, D_MODEL), BF16), jax.ShapeDtypeStruct((t, D_MODEL), F32),
                   jax.ShapeDtypeStruct((SUBLANES, 128), F32), vec_shape, vec_shape],
        scratch_shapes=[pltpu.VMEM((SEQ_TILE, D_FF), F32)],
        compiler_params=_params(),
    )(n2, h1, target, w_up_st, w_down, g2, g3)


def _bwd_mix(dh1, ya, yb, z, h, xc, r, ig, w_oa, w_ob, w_out, ln_g, ln_b, w_s, bias_full, conv_w, wr, wi, lam, jobs=()):
    t = dh1.shape[0]
    n_tiles = t // SEQ_TILE
    per_tile = SEQ_TILE // SUBLANES

    def merge_part(dh1_ref, ya_ref, yb_ref, m_ref, woa_ref, wob_ref, wout_ref, dz_ref, dya_ref, dyb_ref, mg_ref,
                   dpa_ref, dpb_ref, dh1b_ref):
        dh1b = dh1_ref[...].astype(BF16)
        dh1b_ref[...] = dh1b
        dm = _dot_nt(dh1b, wout_ref[...])
        pa = _dot(ya_ref[...], woa_ref[...])
        pb = _dot(yb_ref[...], wob_ref[...])
        sa = _gate(m_ref[:, :D_MODEL])
        sb = _gate(m_ref[:, D_MODEL:])
        mg_ref[...] = (sa * pa + sb * pb).astype(BF16)
        dpa = dm * sa
        dpb = dm * sb
        dz_ref[:, :D_MODEL] = ((dpa * pa) * (1.0 - sa)).astype(BF16)
        dz_ref[:, D_MODEL:] = ((dpb * pb) * (1.0 - sb)).astype(BF16)
        dpa = dpa.astype(BF16)
        dpb = dpb.astype(BF16)
        dpa_ref[...] = dpa
        dpb_ref[...] = dpb
        dya_ref[...] = _dot_nt(dpa, woa_ref[...])
        dyb_ref[...] = _dot_nt(dpb, wob_ref[...])

    def sgu_part(dyb_ref, ub_ref, vb_ref, lg_ref, lb_ref, ws_ref, bias_ref, dz_ref, dlg_ref, dlb_ref, dws_ref, dbs_ref,
                 dvn_ref, dsp_acc):
        i = pl.program_id(0)

        @pl.when(i == 0)
        def _():
            dlg_ref[...] = jnp.zeros_like(dlg_ref)
            dlb_ref[...] = jnp.zeros_like(dlb_ref)
            dws_ref[...] = jnp.zeros_like(dws_ref)
            dsp_acc[...] = jnp.zeros_like(dsp_acc)

        u, du, dvg, rstd, vhat, vn = _sgu_forward_parts(ub_ref[...], vb_ref[...], lg_ref, lb_ref)
        dyb = dyb_ref[...]
        mask = _causal_mask()
        wm = [jnp.where(mask, ws_ref[g], 0.0).astype(BF16) for g in range(GROUPS)]
        for c in range(SEQ_TILE // CHUNK):
            rows = slice(c * CHUNK, (c + 1) * CHUNK)
            for g in range(GROUPS):
                cols = slice(g * GROUP_DIM, (g + 1) * GROUP_DIM)
                vn_blk = vn[rows, cols]
                sp = _dot(wm[g], vn_blk) + bias_ref[:, cols]
                dyb_blk = dyb[rows, cols]
                dz_ref[rows, cols] = (dyb_blk * sp * du[rows, cols]).astype(BF16)
                dsp = dyb_blk * u[rows, cols]
                dsp_acc[:, cols] = dsp_acc[:, cols] + dsp
                dspb = dsp.astype(BF16)
                dvn_ref[rows, cols] = _dot_tn(wm[g], dspb)
                wcols = slice(g * CHUNK, (g + 1) * CHUNK)
                dws_ref[:, wcols] = dws_ref[:, wcols] + jnp.where(mask, _dot_nt(dspb, vn_blk), 0.0)
        dvn = dvn_ref[...]
        dlg_ref[...] = dlg_ref[...] + _col_sum(dvn * vhat)
        dlb_ref[...] = dlb_ref[...] + _col_sum(dvn)
        dvhat = dvn * lg_ref[...]
        dvgel = rstd * (dvhat - jnp.mean(dvhat, axis=-1, keepdims=True)
                        - vhat * jnp.mean(dvhat * vhat, axis=-1, keepdims=True))
        dz_ref[:, D_MODEL:] = (dvgel * dvg).astype(BF16)

        @pl.when(i == n_tiles - 1)
        def _():
            lane = lax.broadcasted_iota(jnp.int32, (CHUNK, 128), 1)
            out = jnp.zeros((CHUNK, 128), F32)
            for g in range(GROUPS):
                s = jnp.sum(dsp_acc[:, g * GROUP_DIM:(g + 1) * GROUP_DIM], axis=1, keepdims=True)
                out = out + jnp.where(lane == g, s, 0.0)
            dbs_ref[...] = out

    def lru_part(dya_ref, xa_ref, ga_ref, h_ref, h_prev_ref, xc_ref, r_ref, ig_ref, cw_ref, wr_ref, wi_ref, lam_ref,
                 dz_ref, dcw_ref, dcb_ref, dwr_ref, dbr_ref, dwi_ref, dbi_ref, dlam_ref, lam_carry, dxc_head):
        i = pl.program_id(0)

        @pl.when(i == 0)
        def _():
            for ref in (dcw_ref, dcb_ref, dwr_ref, dbr_ref, dwi_ref, dbi_ref, dlam_ref, lam_carry, dxc_head):
                ref[...] = jnp.zeros_like(ref)

        first_tile = i == n_tiles - 1
        h_tail = jnp.where(first_tile, 0.0, h_prev_ref[...])
        xc, r, ig = xc_ref[...], r_ref[...], ig_ref[...]
        xcb = xc.astype(BF16)
        sp, a, mult, inv_mult = _decay(r, lam_ref)
        h = h_ref[...]
        h_prev = _shift_down(h, h_tail, 1)
        dya = dya_ref[...]
        gg, dgg = _gelu_and_grad(ga_ref[...])
        dz_ref[:, D_MODEL:] = (dya * h * dgg).astype(BF16)
        ones = jnp.ones((SUBLANES, D_MODEL), F32)
        lam_t, lam_first = _scan_backward(_shift_up(a, ones, 1), dya * gg, lam_carry[...])
        lam_carry[...] = a[0:1] * lam_first
        lam_ig = lam_t * ig
        dxc_direct = lam_ig * mult
        dmult = lam_ig * xc
        dla = a * (lam_t * h_prev - (dmult * a) * inv_mult)
        dla_r = dla * r
        dlam_ref[...] = dlam_ref[...] + _col_sum(dla_r) * (LRU_C * jax.nn.sigmoid(-lam_ref[...]))
        dpr = (dla_r * ((-LRU_C) * sp)) * (1.0 - r)
        dpi = (dxc_direct * xc) * (1.0 - ig)
        dbr_ref[...] = dbr_ref[...] + _col_sum(dpr)
        dbi_ref[...] = dbi_ref[...] + _col_sum(dpi)
        dprb = dpr.astype(BF16)
        dpib = dpi.astype(BF16)
        dxc_gate = []
        for hd in range(HEADS):
            cols = slice(hd * HEAD_DIM, (hd + 1) * HEAD_DIM)
            dxc_gate.append(_dot_nt(dprb[:, cols], wr_ref[hd]) + _dot_nt(dpib[:, cols], wi_ref[hd]))
            dwr_ref[hd] = dwr_ref[hd] + _dot_tn(xcb[:, cols], dprb[:, cols])
            dwi_ref[hd] = dwi_ref[hd] + _dot_tn(xcb[:, cols], dpib[:, cols])
        dxc = dxc_direct + jnp.concatenate(dxc_gate, axis=1)
        dcb_ref[...] = dcb_ref[...] + _col_sum(dxc)
        cw = cw_ref[...]
        head = dxc_head[...]
        xa = xa_ref[...]
        dxa = cw[0:1] * dxc
        dcw_ref[0:1, :] = dcw_ref[0:1, :] + _col_sum(dxc * xa)
        for k in range(1, CONV_WIDTH):
            dxc_k = _shift_up(dxc, head, k)
            dxa = dxa + cw[k:k + 1] * dxc_k
            dcw_ref[k:k + 1, :] = dcw_ref[k:k + 1, :] + _col_sum(dxc_k * xa)
        dxc_head[...] = dxc[0:SUBLANES]
        dz_ref[:, :D_MODEL] = dxa.astype(BF16)

    def body(dh1_ref, ya_ref, yb_ref, z_ref, h_ref, h_prev_ref, xc_ref, r_ref, ig_ref, woa_ref, wob_ref, wout_ref,
             lg_ref, lb_ref, ws_ref, bias_ref, cw_ref, wr_ref, wi_ref, lam_ref, dz_ref, mg_ref, dpa_ref, dpb_ref,
             dh1b_ref, dlg_ref, dlb_ref, dws_ref, dbs_ref, dcw_ref, dcb_ref, dwr_ref, dbr_ref, dwi_ref, dbi_ref,
             dlam_ref, dya_ref, dyb_ref, dvn_ref, dsp_acc, lam_carry, dxc_head):
        def cols(ref, first, count):
            return ref.at[:, pl.ds(first * D_MODEL, count * D_MODEL)]

        merge_part(dh1_ref, ya_ref, yb_ref, cols(z_ref, 4, 2), woa_ref, wob_ref, wout_ref, cols(dz_ref, 4, 2), dya_ref,
                   dyb_ref, mg_ref, dpa_ref, dpb_ref, dh1b_ref)
        sgu_part(dyb_ref, cols(z_ref, 2, 1), cols(z_ref, 3, 1), lg_ref, lb_ref, ws_ref, bias_ref, cols(dz_ref, 2, 2),
                 dlg_ref, dlb_ref, dws_ref, dbs_ref, dvn_ref, dsp_acc)
        lru_part(dya_ref, cols(z_ref, 0, 1), cols(z_ref, 1, 1), h_ref, h_prev_ref, xc_ref, r_ref, ig_ref, cw_ref, wr_ref,
                 wi_ref, lam_ref, cols(dz_ref, 0, 2), dcw_ref, dcb_ref, dwr_ref, dbr_ref, dwi_ref, dbi_ref, dlam_ref,
                 lam_carry, dxc_head)

    rev = lambda i: n_tiles - 1 - i
    tile = pl.BlockSpec((SEQ_TILE, D_MODEL), lambda i: (rev(i), 0))
    row = pl.BlockSpec((SEQ_TILE, D_IN), lambda i: (rev(i), 0))
    prev8 = pl.BlockSpec((SUBLANES, D_MODEL), lambda i: (jnp.maximum(rev(i) * per_tile - 1, 0), 0))
    vec = _const((1, D_MODEL))
    sq = _resident((D_MODEL, D_MODEL))
    gate_w = _resident((HEADS, HEAD_DIM, HEAD_DIM))
    gate_acc = _const((HEADS, HEAD_DIM, HEAD_DIM))
    vec_shape = jax.ShapeDtypeStruct((1, D_MODEL), F32)
    gate_shape = jax.ShapeDtypeStruct((HEADS, HEAD_DIM, HEAD_DIM), F32)
    act_bf = jax.ShapeDtypeStruct((t, D_MODEL), BF16)
    return _fused_call(
        body, jobs, name="bwd_mix", grid=(n_tiles,),
        in_specs=[tile, tile, tile, row, tile, prev8, tile, tile, tile, sq, sq, sq, vec, vec,
                  _const((GROUPS, CHUNK, CHUNK)), _const((CHUNK, D_MODEL)), _const((CONV_WIDTH, D_MODEL)), gate_w, gate_w,
                  vec],
        out_specs=[row, tile, tile, tile, tile, vec, vec, _const((CHUNK, GROUPS * CHUNK)), _const((CHUNK, 128)),
                   _const((SUBLANES, D_MODEL)), vec, gate_acc, vec, gate_acc, vec, vec],
        out_shape=[jax.ShapeDtypeStruct((t, D_IN), BF16), act_bf, act_bf, act_bf, act_bf, vec_shape, vec_shape,
                   jax.ShapeDtypeStruct((CHUNK, GROUPS * CHUNK), F32), jax.ShapeDtypeStruct((CHUNK, 128), F32),
                   jax.ShapeDtypeStruct((SUBLANES, D_MODEL), F32), vec_shape, gate_shape, vec_shape, gate_shape,
                   vec_shape, vec_shape],
        scratch_shapes=[pltpu.VMEM((SEQ_TILE, D_MODEL), F32), pltpu.VMEM((SEQ_TILE, D_MODEL), F32),
                        pltpu.VMEM((SEQ_TILE, D_MODEL), F32), pltpu.VMEM((CHUNK, D_MODEL), F32),
                        pltpu.VMEM((1, D_MODEL), F32), pltpu.VMEM((SUBLANES, D_MODEL), F32)],
        compiler_params=_params(),
    )(dh1, ya, yb, z, h, h, xc, r, ig, w_oa, w_ob, w_out, ln_g, ln_b, w_s, bias_full, conv_w, wr, wi, lam)


def _bwd_in(dz, x, dh1, w_in_st, g1, jobs=()):
    t = x.shape[0]

    def body(dz_ref, x_ref, dh1_ref, w_ref, g_ref, dx_ref, dg1_ref):
        @pl.when(pl.program_id(0) == 0)
        def _():
            dg1_ref[...] = jnp.zeros_like(dg1_ref)

        dn1 = jnp.zeros((MM_TILE, D_MODEL), F32)
        for k in range(N_CHIPS):
            dn1 = dn1 + _dot_nt(dz_ref[:, k * IN_SHARD:(k + 1) * IN_SHARD], w_ref[k])
        xhat, r1 = _rms(x_ref[...])
        dg1_ref[...] = dg1_ref[...] + _col_sum(dn1 * xhat)
        dx_ref[...] = dh1_ref[...] + _rms_bwd(dn1 * g_ref[...], xhat, r1)

    tile = pl.BlockSpec((MM_TILE, D_MODEL), lambda i: (i, 0))
    return _fused_call(
        body, jobs, name="bwd_in", grid=(t // MM_TILE,),
        in_specs=[pl.BlockSpec((MM_TILE, D_IN), lambda i: (i, 0)), tile, tile,
                  _resident((N_CHIPS, D_MODEL, IN_SHARD)), _const((1, D_MODEL))],
        out_specs=[tile, _const((1, D_MODEL))],
        out_shape=[jax.ShapeDtypeStruct((t, D_MODEL), F32), jax.ShapeDtypeStruct((1, D_MODEL), F32)],
        compiler_params=_params(),
    )(dz, x, dh1, w_in_st, g1)


def _weight_grad(name, a, b, n_blocks, a_varies, b_varies, width, jobs=()):
    t = a.shape[0]
    rows = min(DW_TILE, t)
    n_t = t // rows

    def body(a_ref, b_ref, o_ref, acc_ref):
        s = pl.program_id(1)
        part = _dot_tn(a_ref[...], b_ref[...])

        @pl.when(s == 0)
        def _():
            acc_ref[...] = part

        @pl.when(s > 0)
        def _():
            acc_ref[...] = acc_ref[...] + part

        @pl.when(s == n_t - 1)
        def _():
            o_ref[...] = acc_ref[...].astype(BF16)

    return _fused_call(
        body, jobs, name=name, grid=(n_blocks, n_t),
        in_specs=[pl.BlockSpec((rows, D_MODEL), (lambda j, s: (s, j)) if a_varies else (lambda j, s: (s, 0))),
                  pl.BlockSpec((rows, width), (lambda j, s: (s, j)) if b_varies else (lambda j, s: (s, 0)))],
        out_specs=pl.BlockSpec((None, D_MODEL, width), lambda j, s: (j, 0, 0)),
        out_shape=jax.ShapeDtypeStruct((n_blocks, D_MODEL, width), BF16),
        scratch_shapes=[pltpu.VMEM((D_MODEL, width), F32)],
        compiler_params=_params(2),
    )(a, b)


def _weight_grads_square(name, pairs, jobs=()):
    n = len(pairs)
    t = pairs[0][0].shape[0]
    rows = min(2 * MM_TILE, t)
    n_t = t // rows

    def body(*refs):
        ins, outs, accs = refs[:2 * n], refs[2 * n:3 * n], refs[3 * n:]
        s = pl.program_id(0)
        for k in range(n):
            part = _dot_tn(ins[2 * k][...], ins[2 * k + 1][...])

            @pl.when(s == 0)
            def _(k=k, part=part):
                accs[k][...] = part

            @pl.when(s > 0)
            def _(k=k, part=part):
                accs[k][...] = accs[k][...] + part

            @pl.when(s == n_t - 1)
            def _(k=k):
                outs[k][...] = accs[k][...].astype(BF16)

    tile = pl.BlockSpec((rows, D_MODEL), lambda s: (s, 0))
    return _fused_call(
        body, jobs, name=name, grid=(n_t,), in_specs=[tile] * (2 * n), out_specs=[_const((D_MODEL, D_MODEL))] * n,
        out_shape=[jax.ShapeDtypeStruct((D_MODEL, D_MODEL), BF16)] * n,
        scratch_shapes=[pltpu.VMEM((D_MODEL, D_MODEL), F32)] * n,
        compiler_params=_params(),
    )(*[x for pair in pairs for x in pair])


def _place():
    x, y, c = lax.axis_index("x"), lax.axis_index("y"), lax.axis_index("c")
    other_chips = [(1 - x, y), (x, 1 - y), (1 - x, 1 - y)]
    return x, y, c, other_chips


def _chip_index(px, py):
    return 2 * px + py


ANY = pl.BlockSpec(memory_space=pl.ANY)
SIBLING = ((0, 0, 1),)
NEIGHBOURS = ((1, 0, 0), (0, 1, 0))
OTHER_CHIPS = NEIGHBOURS + ((1, 1, 0),)


def _near_far(x, y, c):
    return (x ^ (1 - c), y ^ c), (x ^ c, y ^ (1 - c))


def _gather_near_job(shards):
    n = len(shards)
    halves = [s.shape[0] // 2 for s in shards]

    def copies(ins, outs, send, recv, local):
        x, y, c, _ = _place()
        near, _ = _near_far(x, y, c)

        def block(w, chip, pc):
            return outs[w].at[_chip_index(*chip), pl.ds(pc * halves[w], halves[w]), :]

        def copy(w, k, chip, pc, to, src=None):
            return pltpu.make_async_remote_copy(
                src_ref=block(w, chip, pc) if src is None else src, dst_ref=block(w, chip, pc),
                send_sem=send.at[2 * w + k], recv_sem=recv.at[2 * w + k], device_id=to, device_id_type=MESH)

        sends, arrivals, own = [], [], []
        for w in range(n):
            src = ins[w].at[pl.ds(c * halves[w], halves[w]), :]
            own.append(pltpu.make_async_copy(src, block(w, (x, y), c), local.at[w]))
            sends += [copy(w, 0, (x, y), c, (*near, c), src), copy(w, 1, (x, y), c, (x, y, 1 - c), src)]
            arrivals += [copy(w, 0, near, c, (x, y, c)), copy(w, 1, (x, y), 1 - c, (x, y, c))]
        return sends, arrivals, own

    return _Job(shards, [jax.ShapeDtypeStruct((N_CHIPS,) + s.shape, s.dtype) for s in shards], 2 * n, copies,
                NEIGHBOURS + SIBLING, n_local=n)


def _gather_far_job(stacked):
    n = len(stacked)
    halves = [s.shape[1] // 2 for s in stacked]

    def copies(ins, outs, send, recv, local):
        del ins, local
        x, y, c, _ = _place()
        near, far = _near_far(x, y, c)

        def copy(w, k, chip):
            blk = outs[w].at[_chip_index(*chip), pl.ds(c * halves[w], halves[w]), :]
            return pltpu.make_async_remote_copy(
                src_ref=blk, dst_ref=blk, send_sem=send.at[2 * w + k], recv_sem=recv.at[2 * w + k],
                device_id=(*far, c), device_id_type=MESH)

        sends = [copy(w, k, chip) for w in range(n) for k, chip in enumerate(((x, y), near))]
        arrivals = [copy(w, k, chip) for w in range(n) for k, chip in enumerate((far, (1 - x, 1 - y)))]
        return sends, arrivals, []

    return _Job(stacked, [jax.ShapeDtypeStruct(s.shape, s.dtype) for s in stacked], 2 * n, copies, NEIGHBOURS,
                aliases={w: w for w in range(n)})


def _gather_pass_job(stacked):
    n = len(stacked)
    halves = [s.shape[1] // 2 for s in stacked]

    def copies(ins, outs, send, recv, local):
        del ins, local
        x, y, c, chips = _place()

        def copy(w, j, chip, pc, to):
            blk = outs[w].at[_chip_index(*chip), pl.ds(pc * halves[w], halves[w]), :]
            return pltpu.make_async_remote_copy(
                src_ref=blk, dst_ref=blk, send_sem=send.at[3 * w + j], recv_sem=recv.at[3 * w + j], device_id=to,
                device_id_type=MESH)

        sends = [copy(w, j, chip, c, (x, y, 1 - c)) for w in range(n) for j, chip in enumerate(chips)]
        arrivals = [copy(w, j, chip, 1 - c, (x, y, c)) for w in range(n) for j, chip in enumerate(chips)]
        return sends, arrivals, []

    return _Job(stacked, [jax.ShapeDtypeStruct(s.shape, s.dtype) for s in stacked], 3 * n, copies, SIBLING,
                aliases={w: w for w in range(n)})


def _gather_small_job(block):
    def copies(ins, outs, send, recv, local):
        x, y, c, chips = _place()

        def copy(j, chip_from, to):
            return pltpu.make_async_remote_copy(
                src_ref=ins[0], dst_ref=outs[0].at[_chip_index(*chip_from)], send_sem=send.at[j],
                recv_sem=recv.at[j], device_id=to, device_id_type=MESH)

        own = [pltpu.make_async_copy(ins[0], outs[0].at[_chip_index(x, y)], local.at[0])]
        sends = [copy(j, (x, y), (*chip, c)) for j, chip in enumerate(chips)]
        arrivals = [copy(j, chip, (x, y, c)) for j, chip in enumerate(chips)]
        return sends, arrivals, own

    return _Job([block], [jax.ShapeDtypeStruct((N_CHIPS,) + block.shape, block.dtype)], 3, copies, OTHER_CHIPS,
                n_local=1)


def _pair_send_job(grads):
    n = len(grads)
    halves = [g.shape[1] // 2 for g in grads]

    def copies(ins, outs, send, recv, local):
        del local
        x, y, c, _ = _place()
        sends = [pltpu.make_async_remote_copy(
            src_ref=ins[w].at[:, pl.ds((1 - c) * halves[w], halves[w]), :], dst_ref=outs[w], send_sem=send.at[w],
            recv_sem=recv.at[w], device_id=(x, y, 1 - c), device_id_type=MESH) for w in range(n)]
        return sends, sends, []

    return _Job(grads, [jax.ShapeDtypeStruct((N_CHIPS, h, g.shape[2]), g.dtype) for g, h in zip(grads, halves)], n,
                copies, SIBLING)


def _row_block(rows, limit=256):
    return min(rows, limit)


def _pair_add(name, core, mine, theirs):
    _, _, h, cols = mine.shape
    rb = _row_block(h, 512)

    def body(core_ref, a_ref, b_ref, o_ref):
        del core_ref
        o_ref[...] = (a_ref[...].astype(F32) + b_ref[...].astype(F32)).astype(BF16)

    return pl.pallas_call(
        body, name=name,
        grid_spec=pltpu.PrefetchScalarGridSpec(
            num_scalar_prefetch=1, grid=(N_CHIPS, h // rb),
            in_specs=[pl.BlockSpec((None, None, rb, cols), lambda k, r, core_ref: (k, core_ref[0], r, 0)),
                      pl.BlockSpec((None, rb, cols), lambda k, r, core_ref: (k, r, 0))],
            out_specs=pl.BlockSpec((None, rb, cols), lambda k, r, core_ref: (k, r, 0))),
        out_shape=jax.ShapeDtypeStruct(theirs.shape, BF16),
        compiler_params=_params(2),
    )(core, mine, theirs)


def _sequencer_call(name, collective_id, job):
    steps, peers = job.phases, job.peers
    ins = [jax.new_ref(a, memory_space=pltpu.MemorySpace.HBM) for a in job.inputs]
    outs = [ins[{o: i for i, o in job.aliases.items()}[k]] if k in job.aliases.values()
            else jax.empty_ref(shape, memory_space=pltpu.MemorySpace.HBM) for k, shape in enumerate(job.out_shape)]
    sems = [pltpu.SemaphoreType.DMA((n,)) for step in steps for n in (step.n_sem, step.n_sem, max(step.n_local, 1))]

    @pl.kernel(mesh=plsc.ScalarSubcoreMesh(axis_name="sequencer", num_cores=1), name=name, scratch_types=tuple(sems),
               compiler_params=pltpu.CompilerParams(collective_id=collective_id))
    def launch(*sem_refs):
        x, y, c, _ = _place()
        barrier = pltpu.get_barrier_semaphore()
        for dx, dy, dc in peers:
            pl.semaphore_signal(barrier, inc=1, device_id=(x ^ dx, y ^ dy, c ^ dc), device_id_type=MESH)
        pl.semaphore_wait(barrier, len(peers))
        for k, step in enumerate(steps):
            sends, arrivals, own = step.copies(ins if k == 0 else outs, outs, *sem_refs[3 * k:3 * k + 3])
            for cp in own + sends:
                cp.start()
            for cp in arrivals:
                cp.wait_recv()
            for cp in sends:
                cp.wait_send()
            for cp in own:
                cp.wait()

    launch()
    return [ref[...] for ref in outs]


def _chip_exchange_job(sums):
    n = len(sums)

    def copies(ins, outs, send, recv, local):
        del local
        _, _, c, chips = _place()
        sends = [pltpu.make_async_remote_copy(
            src_ref=ins[w].at[_chip_index(*chip)], dst_ref=outs[w].at[j], send_sem=send.at[3 * w + j],
            recv_sem=recv.at[3 * w + j], device_id=(*chip, c), device_id_type=MESH)
            for w in range(n) for j, chip in enumerate(chips)]
        return sends, sends, []

    return _Job(sums, [jax.ShapeDtypeStruct((N_CHIPS - 1,) + s.shape[1:], s.dtype) for s in sums], 3 * n, copies,
                OTHER_CHIPS)


def _chip_sum(name, place, mine, theirs):
    _, h, cols = mine.shape
    rb = _row_block(h, 512)

    def body(place_ref, p_ref, q_ref, o_ref):
        del place_ref
        acc = p_ref[...].astype(F32)
        for j in range(N_CHIPS - 1):
            acc = acc + q_ref[j].astype(F32)
        o_ref[...] = acc

    return pl.pallas_call(
        body, name=name,
        grid_spec=pltpu.PrefetchScalarGridSpec(
            num_scalar_prefetch=1, grid=(h // rb,),
            in_specs=[pl.BlockSpec((None, rb, cols), lambda r, place_ref: (place_ref[0], r, 0)),
                      pl.BlockSpec((N_CHIPS - 1, rb, cols), lambda r, place_ref: (0, r, 0))],
            out_specs=pl.BlockSpec((None, rb, cols), lambda r, place_ref: (place_ref[1], r, 0))),
        out_shape=jax.ShapeDtypeStruct((2, h, cols), F32),
        compiler_params=_params(),
    )(place, mine, theirs)


def _share_job(bufs):
    n = len(bufs)

    def copies(ins, outs, send, recv, local):
        del ins, local
        x, y, c, _ = _place()

        def copy(w, half):
            return pltpu.make_async_remote_copy(
                src_ref=outs[w].at[half], dst_ref=outs[w].at[half], send_sem=send.at[w], recv_sem=recv.at[w],
                device_id=(x, y, 1 - c), device_id_type=MESH)

        return [copy(w, c) for w in range(n)], [copy(w, 1 - c) for w in range(n)], []

    return _Job(bufs, [jax.ShapeDtypeStruct(b.shape, b.dtype) for b in bufs], n, copies, SIBLING,
                aliases={w: w for w in range(n)})


SMALL_ROWS = 24
ROW_G1, ROW_CW, ROW_CB, ROW_BR, ROW_BI, ROW_LAM, ROW_LG, ROW_LB, ROW_G2, ROW_G3, ROW_LOSS, ROW_BS = (
    0, 1, 5, 6, 7, 8, 9, 10, 11, 12, 13, 16)
N_DEV = 8


def _pack_small(dcw, dcb, dbr, dbi, dlam, dlg, dlb, dg2, dg3, loss, dbs):
    def body(dcw_ref, dcb_ref, dbr_ref, dbi_ref, dlam_ref, dlg_ref, dlb_ref, dg2_ref, dg3_ref, loss_ref, dbs_ref, out):
        out[...] = jnp.zeros((SMALL_ROWS, D_MODEL), F32)
        for row, ref in ((ROW_CB, dcb_ref), (ROW_BR, dbr_ref), (ROW_BI, dbi_ref), (ROW_LAM, dlam_ref),
                         (ROW_LG, dlg_ref), (ROW_LB, dlb_ref), (ROW_G2, dg2_ref), (ROW_G3, dg3_ref)):
            out[row:row + 1, :] = ref[...]
        out[ROW_CW:ROW_CW + CONV_WIDTH, :] = dcw_ref[0:CONV_WIDTH, :]
        out[ROW_LOSS:ROW_LOSS + 1, 0:128] = loss_ref[0:1, :]
        out[ROW_BS:ROW_BS + GROUPS, 0:128] = jnp.transpose(dbs_ref[...])[0:GROUPS, :]

    vm = pl.BlockSpec(memory_space=pltpu.VMEM)
    return pl.pallas_call(
        body, name="pack_small", in_specs=[vm] * 11, out_specs=vm,
        out_shape=jax.ShapeDtypeStruct((SMALL_ROWS, D_MODEL), F32),
    )(dcw, dcb, dbr, dbi, dlam, dlg, dlb, dg2, dg3, loss, dbs)


def _gather_all_job(blocks):
    n = len(blocks)
    flips = [(dx, dy, dc) for dx in (0, 1) for dy in (0, 1) for dc in (0, 1)][1:]

    def copies(ins, outs, send, recv, local):
        x, y, c, _ = _place()
        me = 4 * x + 2 * y + c
        sends, arrivals, own = [], [], []
        for w in range(n):
            own.append(pltpu.make_async_copy(ins[w], outs[w].at[me], local.at[w]))
            for k, (dx, dy, dc) in enumerate(flips):
                peer = (x ^ dx, y ^ dy, c ^ dc)
                sem = dict(send_sem=send.at[7 * w + k], recv_sem=recv.at[7 * w + k])
                sends.append(pltpu.make_async_remote_copy(
                    src_ref=ins[w], dst_ref=outs[w].at[me], device_id=peer, device_id_type=MESH, **sem))
                arrivals.append(pltpu.make_async_remote_copy(
                    src_ref=ins[w], dst_ref=outs[w].at[4 * peer[0] + 2 * peer[1] + peer[2]], device_id=peer,
                    device_id_type=MESH, **sem))
        return sends, arrivals, own

    return _Job(blocks, [jax.ShapeDtypeStruct((N_DEV,) + b.shape, b.dtype) for b in blocks], 7 * n, copies,
                OTHER_CHIPS + SIBLING + tuple((dx, dy, 1) for dx, dy, _ in OTHER_CHIPS), n_local=n)


def _sum_small(vec_all, ws_all, dg1_all):
    def body(vec_ref, ws_ref, dg1_ref, vec_out, ws_out):
        vec, ws, dg1 = vec_ref[0], ws_ref[0], dg1_ref[0]
        for d in range(1, N_DEV):
            vec, ws, dg1 = vec + vec_ref[d], ws + ws_ref[d], dg1 + dg1_ref[d]
        vec_out[...] = vec
        vec_out[ROW_G1:ROW_G1 + 1, :] = dg1
        ws_out[...] = ws

    vm = pl.BlockSpec(memory_space=pltpu.VMEM)
    return pl.pallas_call(
        body, name="sum_small", in_specs=[vm] * 3, out_specs=[vm, vm],
        out_shape=[jax.ShapeDtypeStruct(vec_all.shape[1:], F32), jax.ShapeDtypeStruct(ws_all.shape[1:], F32)],
    )(vec_all, ws_all, dg1_all)


def _adamw_math(w, g, m, v):
    m = ADAM_B1 * m + (1.0 - ADAM_B1) * g
    v = ADAM_B2 * v + (1.0 - ADAM_B2) * (g * g)
    m_hat = m / (1.0 - ADAM_B1 ** ADAM_STEP)
    v_hat = v / (1.0 - ADAM_B2 ** ADAM_STEP)
    delta = (-ADAM_LR) * (m_hat / (jnp.sqrt(v_hat) + ADAM_EPS) + ADAM_WD * w)
    return delta, m, v


def _adamw(name, g, w, m, v, jobs=()):
    rows, cols = w.shape
    rb = _row_block(rows)

    def body(g_ref, w_ref, m_ref, v_ref, d_ref, nm_ref, nv_ref):
        d_ref[...], nm_ref[...], nv_ref[...] = _adamw_math(w_ref[...], g_ref[...], m_ref[...], v_ref[...])

    blk = pl.BlockSpec((rb, cols), lambda r: (r, 0))
    return _fused_call(
        body, jobs, name=name, grid=(rows // rb,), in_specs=[blk] * 4, out_specs=[blk] * 3,
        out_shape=[jax.ShapeDtypeStruct(w.shape, F32)] * 3, compiler_params=_params(),
    )(g, w, m, v)


def _adamw_small(grads, ws, ms, vs):
    n = len(grads)

    def body(*refs):
        g_refs, w_refs, m_refs, v_refs = refs[:n], refs[n:2 * n], refs[2 * n:3 * n], refs[3 * n:4 * n]
        outs = refs[4 * n:]
        for p in range(n):
            d, nm, nv = _adamw_math(w_refs[p][...], g_refs[p][...], m_refs[p][...], v_refs[p][...])
            outs[p][...] = d
            outs[n + p][...] = nm
            outs[2 * n + p][...] = nv

    vm = pl.BlockSpec(memory_space=pltpu.VMEM)
    shapes = [jax.ShapeDtypeStruct(w.shape, F32) for w in ws]
    out = pl.pallas_call(
        body, name="adamw_small", in_specs=[vm] * (4 * n), out_specs=[vm] * (3 * n), out_shape=shapes * 3,
    )(*grads, *ws, *ms, *vs)
    return out[:n], out[n:2 * n], out[2 * n:]


def _unstack_heads(w_st):
    per = HEAD_DIM // N_CHIPS
    return w_st.reshape(N_CHIPS, HEADS, per, HEAD_DIM).transpose(1, 0, 2, 3).reshape(HEADS, HEAD_DIM, HEAD_DIM)


def _stack_heads(w):
    per = HEAD_DIM // N_CHIPS
    return w.reshape(HEADS, N_CHIPS, per, HEAD_DIM).transpose(1, 0, 2, 3).reshape(N_CHIPS, HEADS * per, HEAD_DIM)


def kernel(x, norm_mix_g, w_in, conv_w, conv_b, w_rgate, b_rgate, w_igate, b_igate, lru_lambda, w_out_a, sgu_ln_g, sgu_ln_b, sgu_w_s, sgu_b_s, w_out_b, w_out, norm_mlp_g, w_up, w_down, norm_final_g, loss_target, m_norm_mix_g, m_w_in, m_conv_w, m_conv_b, m_w_rgate, m_b_rgate, m_w_igate, m_b_igate, m_lru_lambda, m_w_out_a, m_sgu_ln_g, m_sgu_ln_b, m_sgu_w_s, m_sgu_b_s, m_w_out_b, m_w_out, m_norm_mlp_g, m_w_up, m_w_down, m_norm_final_g, v_norm_mix_g, v_w_in, v_conv_w, v_conv_b, v_w_rgate, v_b_rgate, v_w_igate, v_b_igate, v_lru_lambda, v_w_out_a, v_sgu_ln_g, v_sgu_ln_b, v_sgu_w_s, v_sgu_b_s, v_w_out_b, v_w_out, v_norm_mlp_g, v_w_up, v_w_down, v_norm_final_g):
    chip = _chip_index(lax.axis_index("x"), lax.axis_index("y"))
    core = lax.axis_index("c")
    quarter_h = HEAD_DIM // N_CHIPS
    quarter_d = D_MODEL // N_CHIPS

    as_2d = lambda a: a.reshape(-1, a.shape[-1])
    big_w = [as_2d(w) for w in (w_in, w_rgate, w_igate, w_out_a, w_out_b, w_out, w_up, w_down)]
    big_m = [as_2d(w) for w in (m_w_in, m_w_rgate, m_w_igate, m_w_out_a, m_w_out_b, m_w_out, m_w_up, m_w_down)]
    big_v = [as_2d(w) for w in (v_w_in, v_w_rgate, v_w_igate, v_w_out_a, v_w_out_b, v_w_out, v_w_up, v_w_down)]

    packed = jnp.concatenate([conv_w[0], b_rgate[0], b_igate[0]], axis=1)
    packed = jnp.concatenate([packed, jnp.zeros_like(packed)], axis=0)
    s_in, s_r, s_i, s_oa, s_ob, s_out, s_up, s_down = [w.astype(BF16) for w in big_w]
    xs, target = x[0], loss_target[0]
    g3 = norm_final_g.reshape(1, D_MODEL)
    bias_s = jnp.broadcast_to(jnp.transpose(sgu_b_s[0])[:, :, None], (CHUNK, GROUPS, GROUP_DIM)).reshape(CHUNK, D_MODEL)
    core_arr = core.reshape(1).astype(jnp.int32)
    place = jnp.stack([chip, core]).astype(jnp.int32)
    quarter = lambda g: g.reshape(N_CHIPS, D_MODEL // N_CHIPS, D_MODEL)

    def pair_add(nm, g, from_sibling):
        return _pair_add("pair_add_" + nm, core_arr, g.reshape(N_CHIPS, 2, g.shape[1] // 2, g.shape[2]), from_sibling)

    def chip_sum(nm, pair, from_chips):
        return _chip_sum("chip_sum_" + nm, place, pair, from_chips)

    order = jnp.stack([chip, chip ^ 2, chip ^ 1, chip ^ 3]).astype(jnp.int32)
    (z, n1, (w_in_st, wr_st, wi_st)), ((packed_all,), late) = _fwd_in(
        xs, norm_mix_g, [s_in, s_r, s_i], order,
        jobs=[_gather_small_job(packed), _gather_near_job([s_oa, s_ob, s_out])])
    pick = lambda lo, hi: packed_all[:, :HEADS, lo:hi].transpose(1, 0, 2).reshape(HEADS, -1)
    conv_w_full = pick(0, quarter_d)
    br_full = pick(quarter_d, quarter_d + quarter_h).reshape(1, D_MODEL)
    bi_full = pick(quarter_d + quarter_h, quarter_d + 2 * quarter_h).reshape(1, D_MODEL)
    wr, wi = _unstack_heads(wr_st), _unstack_heads(wi_st)
    lru = (conv_w_full, conv_b, wr, br_full, wi, bi_full, lru_lambda)
    sgu = (sgu_ln_g, sgu_ln_b, sgu_w_s[0], bias_s)

    after = lambda arrays, result: lax.optimization_barrier((arrays, result))[0]
    w_up_st, w_dn = _sequencer_call(
        "gather_mlp", 8, _gather_near_job(after([s_up, s_down], n1)).then(_gather_far_job).then(_gather_pass_job))
    w_dn = w_dn.reshape(D_FF, D_MODEL)
    late_step = (3 * (xs.shape[0] // SEQ_TILE) // 4,)
    (ya, *saved), (late,) = _fwd_lru(z, *lru, jobs=[_gather_far_job(late).then(_gather_pass_job, at=late_step)])
    w_oa, w_ob, w_o = [w.reshape(D_MODEL, D_MODEL) for w in late]
    (yb, h1, n2), _ = _fwd_sgu_merge(ya, z, xs, *sgu, w_oa, w_ob, w_o, norm_mlp_g)
    (act, dup, dh2b, dh1, loss_part, dg3, dg2), _ = _mlp(n2, h1, target, w_up_st, w_dn, norm_mlp_g, g3)

    d_down, _ = _weight_grad("dw_down", act, dh2b, N_CHIPS, True, False, D_MODEL)
    r_down, = _sequencer_call("send_w_down", 10, _pair_send_job([d_down]))
    d_up, _ = _weight_grad("dw_up", n2, dup, N_CHIPS, False, True, D_MODEL)
    r_up, = _sequencer_call("send_w_up", 11, _pair_send_job([d_up]))
    p_down, p_up = pair_add("w_down", d_down, r_down), pair_add("w_up", d_up, r_up)
    (dz, merged, dpa, dpb, dh1b, dlg, dlb, dws, dbs, dcw, dcb, dwr, dbr, dwi, dbi, dlam), ((q_up, q_down),) = _bwd_mix(
        dh1, ya, yb, z, *saved, w_oa, w_ob, w_o, *sgu, conv_w_full, wr, wi, lru_lambda,
        jobs=[_chip_exchange_job([p_up, p_down])])
    half_up, half_down = chip_sum("w_up", p_up, q_up), chip_sum("w_down", p_down, q_down)
    names = ("w_in", "w_rgate", "w_igate", "w_out_a", "w_out_b", "w_out", "w_up", "w_down")
    (d_out, d_oa, d_ob), ((full_up, full_down),) = _weight_grads_square(
        "dw_projections", [(merged, dh1b), (ya, dpa), (yb, dpb)], jobs=[_share_job([half_up, half_down])])
    mids = [quarter(d_oa), quarter(d_ob), quarter(d_out)]
    r_mids = _sequencer_call("send_mids", 1, _pair_send_job(mids))
    gates = [_stack_heads(dwr).astype(BF16), _stack_heads(dwi).astype(BF16)]
    small = _pack_small(dcw, dcb, dbr, dbi, dlam, dlg, dlb, dg2, dg3, loss_part, dbs)
    p_mids = [pair_add(nm, g, r) for nm, g, r in zip(names[3:6], mids, r_mids)]
    q_mids = _sequencer_call("exchange_mids", 2, _chip_exchange_job(p_mids))
    d_in, (r_gates, (vec_all, ws_all)) = _weight_grad(
        "dw_in", n1, dz, N_CHIPS, False, True, IN_SHARD,
        jobs=[_pair_send_job(gates), _gather_all_job([small, dws])])
    r_in, = _sequencer_call("send_w_in", 3, _pair_send_job([d_in]))
    adam_args = {nm: (w, m, v) for nm, w, m, v in zip(names, big_w, big_m, big_v)}

    def adamw(nm, g):
        w, m, v = adam_args[nm]
        g = g.reshape(w.shape)
        return g, _adamw("adamw_" + nm, g, w, m, v)[0]

    p_gates = [pair_add(nm, g, r) for nm, g, r in zip(names[1:3], gates, r_gates)]
    half_mids = [chip_sum(nm, p, q) for nm, p, q in zip(names[3:6], p_mids, q_mids)]
    full_mids = _sequencer_call("share_mids", 12, _share_job(half_mids))
    p_first = [pair_add("w_in", d_in, r_in)] + p_gates
    q_first = _sequencer_call("exchange_w_in", 4, _chip_exchange_job(p_first))
    (grad_x, dg1), _ = _bwd_in(dz, xs, dh1, w_in_st, norm_mix_g)
    dg1_all, = _sequencer_call("gather_dg1", 6, _gather_all_job([dg1]))
    done = {nm: adamw(nm, f) for nm, f in zip(("w_up", "w_down") + names[3:6], [full_up, full_down] + full_mids)}
    q_first = after(q_first, [out[0] for _, out in done.values()])
    half_first = [chip_sum(nm, p, q) for nm, p, q in zip(names[:3], p_first, q_first)]
    full_first = _sequencer_call("share_last", 5, _share_job(half_first))
    done.update({nm: adamw(nm, f) for nm, f in zip(names[:3], full_first)})
    full, big_out = [done[nm][0] for nm in names], [done[nm][1] for nm in names]

    vec, ws_sum = _sum_small(vec_all, ws_all, dg1_all)
    row = lambda r: vec[r:r + 1]
    shard = lambda a, width: lax.dynamic_slice_in_dim(a, chip * width, width, axis=1)
    g_small = dict(
        norm_mix_g=row(ROW_G1), conv_w=shard(vec[ROW_CW:ROW_CW + CONV_WIDTH], quarter_d), conv_b=row(ROW_CB),
        b_rgate=shard(row(ROW_BR).reshape(HEADS, HEAD_DIM), quarter_h),
        b_igate=shard(row(ROW_BI).reshape(HEADS, HEAD_DIM), quarter_h), lru_lambda=row(ROW_LAM),
        sgu_ln_g=row(ROW_LG), sgu_ln_b=row(ROW_LB),
        sgu_w_s=ws_sum.reshape(CHUNK, GROUPS, CHUNK).transpose(1, 0, 2).reshape(GROUPS * CHUNK, CHUNK),
        sgu_b_s=vec[ROW_BS:ROW_BS + GROUPS, 0:CHUNK], norm_mlp_g=row(ROW_G2), norm_final_g=row(ROW_G3))
    loss = vec[ROW_LOSS, 0]
    small_names = list(g_small)
    given = dict(
        norm_mix_g=(norm_mix_g, m_norm_mix_g, v_norm_mix_g), conv_w=(conv_w, m_conv_w, v_conv_w),
        conv_b=(conv_b, m_conv_b, v_conv_b), b_rgate=(b_rgate, m_b_rgate, v_b_rgate),
        b_igate=(b_igate, m_b_igate, v_b_igate), lru_lambda=(lru_lambda, m_lru_lambda, v_lru_lambda),
        sgu_ln_g=(sgu_ln_g, m_sgu_ln_g, v_sgu_ln_g), sgu_ln_b=(sgu_ln_b, m_sgu_ln_b, v_sgu_ln_b),
        sgu_w_s=(sgu_w_s, m_sgu_w_s, v_sgu_w_s), sgu_b_s=(sgu_b_s, m_sgu_b_s, v_sgu_b_s),
        norm_mlp_g=(norm_mlp_g, m_norm_mlp_g, v_norm_mlp_g), norm_final_g=(norm_final_g, m_norm_final_g, v_norm_final_g))
    g2d = [g_small[nm] for nm in small_names]
    to2d = lambda a, g: a.reshape(g.shape)
    d_s, m_s, v_s = _adamw_small(
        g2d, *[[to2d(given[nm][q], g) for nm, g in zip(small_names, g2d)] for q in range(3)])

    shapes = dict(
        norm_mix_g=norm_mix_g, w_in=w_in, conv_w=conv_w, conv_b=conv_b, w_rgate=w_rgate, b_rgate=b_rgate,
        w_igate=w_igate, b_igate=b_igate, lru_lambda=lru_lambda, w_out_a=w_out_a, sgu_ln_g=sgu_ln_g,
        sgu_ln_b=sgu_ln_b, sgu_w_s=sgu_w_s, sgu_b_s=sgu_b_s, w_out_b=w_out_b, w_out=w_out, norm_mlp_g=norm_mlp_g,
        w_up=w_up, w_down=w_down, norm_final_g=norm_final_g)
    grads, deltas, new_m, new_v = {}, {}, {}, {}
    for nm, g, (d, nmom, nvar) in zip(names, full, big_out):
        grads[nm], deltas[nm], new_m[nm], new_v[nm] = g, d, nmom, nvar
    for p, nm in enumerate(small_names):
        grads[nm], deltas[nm], new_m[nm], new_v[nm] = g2d[p], d_s[p], m_s[p], v_s[p]
    order = list(shapes)
    out = [loss, grad_x[None]]
    for group in (grads, deltas, new_m, new_v):
        out += [group[nm].reshape(shapes[nm].shape) for nm in order]
    return tuple(out)
```

```python
import functools

import jax
import jax.numpy as jnp
from jax import lax
from jax.experimental import pallas as pl
from jax.experimental.pallas import tpu as pltpu
from jax.experimental.pallas import tpu_sc as plsc

F32 = jnp.float32
BF16 = jnp.bfloat16
MESH = pl.DeviceIdType.MESH

D_MODEL = 1024
D_IN = 6 * D_MODEL
D_FF = 4 * D_MODEL
N_CHIPS = 4
IN_SHARD = D_IN // N_CHIPS
HEADS = 4
HEAD_DIM = D_MODEL // HEADS
GROUPS = 4
GROUP_DIM = D_MODEL // GROUPS
CHUNK = 128
CONV_WIDTH = 4
LRU_C = 8.0
NORM_EPS = 1e-6
LN_EPS = 1e-5

ADAM_LR = 0.001
ADAM_B1 = 0.9
ADAM_B2 = 0.999
ADAM_EPS = 1e-08
ADAM_WD = 0.01
ADAM_STEP = 10

SUBLANES = 8
MM_TILE = 512
IN_TILE = 1024
SEQ_TILE = 256
DW_TILE = 2048
VMEM_LIMIT_BYTES = 56 * 1024 * 1024

GELU_K0 = 0.7978845608028654
GELU_K1 = 0.044715


def _params(n_grid_axes=1):
    return pltpu.CompilerParams(
        dimension_semantics=("arbitrary",) * n_grid_axes, vmem_limit_bytes=VMEM_LIMIT_BYTES)


def _resident(shape):
    nd = len(shape)
    return pl.BlockSpec(shape, lambda *_: (0,) * nd, pipeline_mode=pl.Buffered(1))


def _const(shape):
    nd = len(shape)
    return pl.BlockSpec(shape, lambda *_: (0,) * nd)


def _dot(a, b):
    return jnp.dot(a, b, preferred_element_type=F32)


def _dot_nt(a, b):
    return lax.dot_general(a, b, (((1,), (1,)), ((), ())), preferred_element_type=F32)


def _dot_tn(a, b):
    return lax.dot_general(a, b, (((0,), (0,)), ((), ())), preferred_element_type=F32)


def _gelu(x):
    t = jnp.tanh(x * (GELU_K0 + (GELU_K0 * GELU_K1) * (x * x)))
    return x * (0.5 + 0.5 * t)


def _gelu_and_grad(x):
    x2 = x * x
    t = jnp.tanh(x * (GELU_K0 + (GELU_K0 * GELU_K1) * x2))
    s = 0.5 + 0.5 * t
    dg = s + (x * (1.0 - t * t)) * (0.5 * GELU_K0 + (1.5 * GELU_K0 * GELU_K1) * x2)
    return x * s, dg


def _gate(x):
    return 0.5 + 0.5 * jnp.tanh(0.5 * x.astype(F32))


def _rms(x):
    r = lax.rsqrt(jnp.mean(x * x, axis=-1, keepdims=True) + NORM_EPS)
    return x * r, r


def _rms_bwd(dn, xhat, r):
    return r * (dn - xhat * jnp.mean(dn * xhat, axis=-1, keepdims=True))


def _col_sum(v):
    return jnp.sum(v, axis=0, keepdims=True)


def _shift_down(x, tail8, k):
    xs = pltpu.roll(x, k, 0)
    ts = pltpu.roll(tail8, k, 0)
    ridx = lax.broadcasted_iota(jnp.int32, tail8.shape, 0)
    head = jnp.where(ridx < k, ts, xs[0:SUBLANES])
    return jnp.concatenate([head, xs[SUBLANES:]], axis=0)


def _shift_up(x, head8, k):
    n = x.shape[0]
    xs = pltpu.roll(x, n - k, 0)
    hs = pltpu.roll(head8, SUBLANES - k, 0)
    ridx = lax.broadcasted_iota(jnp.int32, head8.shape, 0)
    last = jnp.where(ridx >= SUBLANES - k, hs, xs[n - SUBLANES:n])
    return jnp.concatenate([xs[:n - SUBLANES], last], axis=0)


def _scan_forward(a, b, carry):
    n, cols = a.shape
    groups = n // SUBLANES
    a = a.reshape(groups, SUBLANES, cols)
    b = b.reshape(groups, SUBLANES, cols)
    sub = lax.broadcasted_iota(jnp.int32, a.shape, 1)
    for s in (1, 2, 4):
        a_s = pltpu.roll(a, s, 1)
        b_s = pltpu.roll(b, s, 1)
        m = sub >= s
        b = jnp.where(m, a * b_s + b, b)
        a = jnp.where(m, a * a_s, a)
    out = []
    for g in range(groups):
        h = a[g] * carry + b[g]
        out.append(h)
        carry = h[SUBLANES - 1:SUBLANES]
    return jnp.concatenate(out, axis=0), carry


def _scan_backward(a, b, carry):
    n, cols = a.shape
    groups = n // SUBLANES
    a = a.reshape(groups, SUBLANES, cols)
    b = b.reshape(groups, SUBLANES, cols)
    sub = lax.broadcasted_iota(jnp.int32, a.shape, 1)
    for s in (1, 2, 4):
        a_s = pltpu.roll(a, SUBLANES - s, 1)
        b_s = pltpu.roll(b, SUBLANES - s, 1)
        m = sub < SUBLANES - s
        b = jnp.where(m, a * b_s + b, b)
        a = jnp.where(m, a * a_s, a)
    out = [None] * groups
    for g in reversed(range(groups)):
        h = a[g] * carry + b[g]
        out[g] = h
        carry = h[0:1]
    return jnp.concatenate(out, axis=0), carry


def _softplus_neg(lam):
    e = jnp.exp(-jnp.abs(lam))
    u = 1.0 + e
    log1p_e = jnp.where(u == 1.0, e, jnp.log(u) * (e / jnp.where(u == 1.0, 1.0, u - 1.0)))
    return jnp.maximum(-lam, 0.0) + log1p_e


def _lru_gates(xa, tail8, cw_ref, cb_ref, wr_ref, br_ref, wi_ref, bi_ref, lam_ref):
    cw = cw_ref[...]
    xc = cb_ref[...] + cw[0:1] * xa
    for k in range(1, CONV_WIDTH):
        xc = xc + cw[k:k + 1] * _shift_down(xa, tail8, k)
    xcb = xc.astype(BF16)
    pre_r, pre_i = [], []
    for h in range(HEADS):
        cols = slice(h * HEAD_DIM, (h + 1) * HEAD_DIM)
        pre_r.append(_dot(xcb[:, cols], wr_ref[h]))
        pre_i.append(_dot(xcb[:, cols], wi_ref[h]))
    r = jax.nn.sigmoid(jnp.concatenate(pre_r, axis=1) + br_ref[...])
    ig = jax.nn.sigmoid(jnp.concatenate(pre_i, axis=1) + bi_ref[...])
    _, a, mult, _ = _decay(r, lam_ref)
    return xc, r, ig, a, mult


def _decay(r, lam_ref):
    sp = _softplus_neg(lam_ref[...])
    log_a = ((-LRU_C) * sp) * r
    a = jnp.exp(log_a)
    th = jnp.tanh(log_a)
    q = (-2.0 * th) / (1.0 - th)
    inv = lax.rsqrt(q)
    return sp, a, jnp.where(q > 0.0, q * inv, 0.0), inv


class _Phase:
    def __init__(self, copies, n_sem, n_local, start=None, finish=None):
        self.copies, self.n_sem, self.n_local, self.start, self.finish = copies, n_sem, n_local, start, finish


class _Job:
    def __init__(self, inputs, out_shape, n_sem, copies, peers, aliases=None, n_local=0):
        self.inputs, self.out_shape = list(inputs), list(out_shape)
        self.aliases = dict(aliases or {})
        self.phases = [_Phase(copies, n_sem, n_local)]
        self.peers = tuple(peers)

    def then(self, make, at=None):
        nxt = make(self.out_shape)
        self.phases[-1].finish = at
        nxt.phases[0].start = at
        self.phases += nxt.phases
        self.peers = tuple(sorted(set(self.peers + nxt.peers)))
        return self


def _fused_call(body, jobs, *, name, grid, in_specs, out_specs, out_shape, scratch_shapes=(),
                input_output_aliases=None, compiler_params=None, n_prefetch=0, jobs_start_after=None):
    single = not isinstance(out_shape, (list, tuple))
    out_specs = [out_specs] if single else list(out_specs)
    out_shape = [out_shape] if single else list(out_shape)
    n_scr = len(scratch_shapes)
    in_specs, scratch_shapes = list(in_specs), list(scratch_shapes)
    n_in, n_out = len(in_specs), len(out_shape)
    aliases = dict(input_output_aliases or {})
    in_at, out_at, phases = [], [], []
    for q, job in enumerate(jobs):
        in_at.append(len(in_specs))
        out_at.append(len(out_shape))
        for i, o in job.aliases.items():
            aliases[n_prefetch + len(in_specs) + i] = len(out_shape) + o
        in_specs += [ANY] * len(job.inputs)
        out_specs += [ANY] * len(job.out_shape)
        out_shape += job.out_shape
        for k, phase in enumerate(job.phases):
            phases.append((q, k, phase, len(scratch_shapes)))
            scratch_shapes += [pltpu.SemaphoreType.DMA((phase.n_sem,)), pltpu.SemaphoreType.DMA((phase.n_sem,)),
                               pltpu.SemaphoreType.DMA((max(phase.n_local, 1),))]
    n_in_all, n_out_all = len(in_specs), len(out_shape)
    first_step, last_step = (0,) * len(grid), tuple(g - 1 for g in grid)

    def full_body(*refs):
        prefetch, refs = refs[:n_prefetch], refs[n_prefetch:]
        ins, outs, scr = refs[:n_in_all], refs[n_in_all:n_in_all + n_out_all], refs[n_in_all + n_out_all:]
        ids = [pl.program_id(a) for a in range(len(grid))]
        at_step = lambda step: functools.reduce(jnp.logical_and, [i == k for i, k in zip(ids, step)])

        def copies(q, k, phase, sem_at):
            job = jobs[q]
            mine = outs[out_at[q]:out_at[q] + len(job.out_shape)]
            return phase.copies(ins[in_at[q]:in_at[q] + len(job.inputs)] if k == 0 else mine, mine,
                                *scr[sem_at:sem_at + 3])

        def start(*phase):
            def go():
                sends, _, local = copies(*phase)
                for cp in local + sends:
                    cp.start()
            return go

        def finish(*phase):
            def go():
                sends, arrivals, local = copies(*phase)
                for cp in arrivals:
                    cp.wait_recv()
                for cp in sends:
                    cp.wait_send()
                for cp in local:
                    cp.wait()
            return go

        for phase in phases:
            if phase[2].start is None and jobs_start_after is None:
                pl.when(at_step(first_step))(start(*phase))
        body(*prefetch, *ins[:n_in], *outs[:n_out], *scr[:n_scr])
        for phase in phases:
            pl.when(at_step(phase[2].finish or last_step))(finish(*phase))
            nxt = phase[2].start or jobs_start_after
            if nxt is not None:
                pl.when(at_step(nxt))(start(*phase))

    if n_prefetch:
        layout = dict(grid_spec=pltpu.PrefetchScalarGridSpec(
            num_scalar_prefetch=n_prefetch, grid=grid, in_specs=in_specs, out_specs=out_specs,
            scratch_shapes=scratch_shapes))
    else:
        layout = dict(grid=grid, in_specs=in_specs, out_specs=out_specs, scratch_shapes=scratch_shapes)
    call = pl.pallas_call(
        full_body, name=name, out_shape=out_shape, input_output_aliases=aliases, compiler_params=compiler_params,
        **layout)

    def run(*args):
        res = call(*args, *[a for job in jobs for a in job.inputs])
        mine = res[0] if single else list(res[:n_out])
        return mine, [list(res[at:at + len(job.out_shape)]) for at, job in zip(out_at, jobs)]

    return run


def _fwd_in(x, g1, shards, order, jobs=()):
    t = x.shape[0]
    rows_per_step = min(IN_TILE, t)
    n_tiles = t // rows_per_step
    n = len(shards)
    halves = [s.shape[0] // 2 for s in shards]

    def body(order_ref, x_ref, g_ref, *refs):
        del order_ref
        ins, (z_ref, n_ref), outs = refs[:n], refs[n:n + 2], refs[n + 2:2 * n + 2]
        wbuf, nbuf, send, recv, local = refs[2 * n + 2:]
        s, i = pl.program_id(0), pl.program_id(1)
        x_, y_, c, chips = _place()
        near, far = _near_far(x_, y_, c)
        k_me = _chip_index(x_, y_)

        def block(w, chip, pc):
            return outs[w].at[_chip_index(*chip), pl.ds(pc * halves[w], halves[w]), :]

        def over_ici(w, j, landing):
            return pltpu.make_async_remote_copy(
                src_ref=ins[w].at[pl.ds(c * halves[w], halves[w]), :],
                dst_ref=block(w, chips[j] if landing else (x_, y_), c), send_sem=send.at[6 * w + j],
                recv_sem=recv.at[6 * w + j], device_id=(*chips[j], c), device_id_type=MESH)

        def onward(w, landing):
            blk = block(w, chips[2] if landing else near, c)
            return pltpu.make_async_remote_copy(
                src_ref=blk, dst_ref=blk, send_sem=send.at[6 * w + 2], recv_sem=recv.at[6 * w + 2],
                device_id=(*far, c), device_id_type=MESH)

        def to_sibling(w, j, landing):
            blk = block(w, chips[j], 1 - c if landing else c)
            return pltpu.make_async_remote_copy(
                src_ref=blk, dst_ref=blk, send_sem=send.at[6 * w + 3 + j], recv_sem=recv.at[6 * w + 3 + j],
                device_id=(x_, y_, 1 - c), device_id_type=MESH)

        own = [pltpu.make_async_copy(wbuf, outs[0].at[k_me], local.at[0])]
        own += [pltpu.make_async_copy(ins[w], outs[w].at[k_me], local.at[w]) for w in range(1, n)]

        @pl.when((s == 0) & (i == 0))
        def _():
            for j in range(2):
                for w in range(n):
                    over_ici(w, j, False).start()
            load = pltpu.make_async_copy(ins[0], wbuf, local.at[n])
            load.start()
            load.wait()
            for cp in own:
                cp.start()

        for j in range(N_CHIPS - 1):
            @pl.when((s == j + 1) & (i == 0))
            def _(j=j):
                if j == 0:
                    for k in range(2):
                        for w in range(n):
                            over_ici(w, k, True).wait_recv()
                    for w in range(n):
                        onward(w, False).start()
                    for k in range(2):
                        for w in range(n):
                            to_sibling(w, k, False).start()
                    own[0].wait()
                if j == 2:
                    for w in range(n):
                        onward(w, True).wait_recv()
                    for w in range(n):
                        to_sibling(w, j, False).start()
                for w in range(n):
                    to_sibling(w, j, True).wait_recv()
                load = pltpu.make_async_copy(outs[0].at[_chip_index(*chips[j])], wbuf, local.at[n])
                load.start()
                load.wait()

        rows = pl.ds(pl.multiple_of(i * rows_per_step, rows_per_step), rows_per_step)

        @pl.when(s == 0)
        def _():
            xhat, _ = _rms(x_ref[...])
            nrm = (xhat * g_ref[...]).astype(BF16)
            nbuf[rows, :] = nrm
            n_ref[...] = nrm

        z_ref[...] = _dot(nbuf[rows, :], wbuf[...]).astype(BF16)

        @pl.when((s == N_CHIPS - 1) & (i == n_tiles - 1))
        def _():
            for j in range(N_CHIPS - 1):
                for w in range(n):
                    (over_ici(w, j, False) if j < 2 else onward(w, False)).wait_send()
                    to_sibling(w, j, False).wait_send()
            for cp in own[1:]:
                cp.wait()

    once = lambda s, i, order: (jnp.where(s == 0, i, n_tiles - 1), 0)
    (z, n1, *stacked), job_outs = _fused_call(
        body, jobs, name="fwd_in", grid=(N_CHIPS, n_tiles), n_prefetch=1,
        in_specs=[pl.BlockSpec((rows_per_step, D_MODEL), once), _const((1, D_MODEL))] + [ANY] * n,
        out_specs=[pl.BlockSpec((rows_per_step, IN_SHARD), lambda s, i, order: (i, order[s])),
                   pl.BlockSpec((rows_per_step, D_MODEL), once)] + [ANY] * n,
        out_shape=[jax.ShapeDtypeStruct((t, D_IN), BF16), jax.ShapeDtypeStruct((t, D_MODEL), BF16)]
        + [jax.ShapeDtypeStruct((N_CHIPS,) + s.shape, s.dtype) for s in shards],
        scratch_shapes=[pltpu.VMEM(shards[0].shape, BF16), pltpu.VMEM((t, D_MODEL), BF16),
                        pltpu.SemaphoreType.DMA((6 * n,)),
                        pltpu.SemaphoreType.DMA((6 * n,)), pltpu.SemaphoreType.DMA((n + 1,))],
        compiler_params=_params(2), jobs_start_after=(1, 0),
    )(order, x, g1, *shards)
    return (z, n1, stacked), job_outs


def _fwd_lru(z, conv_w, conv_b, wr, br, wi, bi, lam, jobs=()):
    t = z.shape[0]

    def body(xa_ref, ga_ref, cw_ref, cb_ref, wr_ref, br_ref, wi_ref, bi_ref, lam_ref, ya_ref, h_ref, xc_ref, r_ref,
             ig_ref, tail_ref, carry_ref):
        @pl.when(pl.program_id(0) == 0)
        def _():
            tail_ref[...] = jnp.zeros_like(tail_ref)
            carry_ref[...] = jnp.zeros_like(carry_ref)

        xa = xa_ref[...].astype(F32)
        xc, r, ig, a, mult = _lru_gates(xa, tail_ref[...], cw_ref, cb_ref, wr_ref, br_ref, wi_ref, bi_ref, lam_ref)
        tail_ref[...] = xa[SEQ_TILE - SUBLANES:]
        xc_ref[...], r_ref[...], ig_ref[...] = xc, r, ig
        h, carry = _scan_forward(a, xc * ig * mult, carry_ref[...])
        carry_ref[...] = carry
        h_ref[...] = h
        ya_ref[...] = (h * _gelu(ga_ref[...].astype(F32))).astype(BF16)

    tile = lambda j: pl.BlockSpec((SEQ_TILE, D_MODEL), lambda i: (i, j))
    return _fused_call(
        body, jobs, name="fwd_lru", grid=(t // SEQ_TILE,),
        in_specs=[tile(0), tile(1), _const((CONV_WIDTH, D_MODEL)), _const((1, D_MODEL)),
                  _resident((HEADS, HEAD_DIM, HEAD_DIM)), _const((1, D_MODEL)),
                  _resident((HEADS, HEAD_DIM, HEAD_DIM)), _const((1, D_MODEL)), _const((1, D_MODEL))],
        out_specs=[tile(0)] * 5,
        out_shape=[jax.ShapeDtypeStruct((t, D_MODEL), BF16)] + [jax.ShapeDtypeStruct((t, D_MODEL), F32)] * 4,
        scratch_shapes=[pltpu.VMEM((SUBLANES, D_MODEL), F32), pltpu.VMEM((1, D_MODEL), F32)],
        compiler_params=_params(),
    )(z, z, conv_w, conv_b, wr, br, wi, bi, lam)


def _sgu_forward_parts(ub, vb, lg_ref, lb_ref):
    u, du = _gelu_and_grad(ub.astype(F32))
    vg, dvg = _gelu_and_grad(vb.astype(F32))
    mu = jnp.mean(vg, axis=-1, keepdims=True)
    d = vg - mu
    rstd = lax.rsqrt(jnp.mean(d * d, axis=-1, keepdims=True) + LN_EPS)
    vhat = d * rstd
    vn = (vhat * lg_ref[...] + lb_ref[...]).astype(BF16)
    return u, du, dvg, rstd, vhat, vn


def _causal_mask():
    rows = lax.broadcasted_iota(jnp.int32, (CHUNK, CHUNK), 0)
    cols = lax.broadcasted_iota(jnp.int32, (CHUNK, CHUNK), 1)
    return rows >= cols


def _fwd_sgu_merge(ya, z, x, ln_g, ln_b, w_s, bias_full, w_oa, w_ob, w_out, g2, jobs=()):
    t = x.shape[0]

    def body(ya_ref, ub_ref, vb_ref, m_ref, x_ref, lg_ref, lb_ref, ws_ref, bias_ref, woa_ref, wob_ref, wout_ref, g_ref,
             yb_ref, pa_ref, pb_ref, h1_ref, n2_ref):
        u, _, _, _, _, vn = _sgu_forward_parts(ub_ref[...], vb_ref[...], lg_ref, lb_ref)
        mask = _causal_mask()
        wm = [jnp.where(mask, ws_ref[g], 0.0).astype(BF16) for g in range(GROUPS)]
        for c in range(SEQ_TILE // CHUNK):
            rows = slice(c * CHUNK, (c + 1) * CHUNK)
            for g in range(GROUPS):
                cols = slice(g * GROUP_DIM, (g + 1) * GROUP_DIM)
                sp = _dot(wm[g], vn[rows, cols]) + bias_ref[:, cols]
                yb_ref[rows, cols] = (u[rows, cols] * sp).astype(BF16)
        pa = _dot(ya_ref[...], woa_ref[...])
        pb = _dot(yb_ref[...], wob_ref[...])
        pa_ref[...] = pa
        pb_ref[...] = pb
        merged = _gate(m_ref[:, :D_MODEL]) * pa + _gate(m_ref[:, D_MODEL:]) * pb
        h1 = x_ref[...] + _dot(merged.astype(BF16), wout_ref[...])
        h1_ref[...] = h1
        xhat, _ = _rms(h1)
        n2_ref[...] = (xhat * g_ref[...]).astype(BF16)

    tile = lambda j: pl.BlockSpec((SEQ_TILE, D_MODEL), lambda i: (i, j))
    sq = _resident((D_MODEL, D_MODEL))
    vec = _const((1, D_MODEL))
    bf, f32 = jax.ShapeDtypeStruct((t, D_MODEL), BF16), jax.ShapeDtypeStruct((t, D_MODEL), F32)
    return _fused_call(
        body, jobs, name="fwd_sgu_merge", grid=(t // SEQ_TILE,),
        in_specs=[tile(0), tile(2), tile(3), pl.BlockSpec((SEQ_TILE, 2 * D_MODEL), lambda i: (i, 2)), tile(0), vec, vec,
                  _const((GROUPS, CHUNK, CHUNK)), _const((CHUNK, D_MODEL)), sq, sq, sq, vec],
        out_specs=[tile(0)] * 5,
        out_shape=[bf, f32, f32, f32, bf],
        compiler_params=_params(),
    )(ya, z, z, z, x, ln_g, ln_b, w_s, bias_full, w_oa, w_ob, w_out, g2)


def _mlp(n2, h1, target, w_up_st, w_down, g2, g3, jobs=()):
    t = n2.shape[0]

    def body(n2_ref, h1_ref, tgt_ref, wup_ref, wdown_ref, g2_ref, g3_ref, act_ref, dup_ref, dh2b_ref, dh1_ref,
             loss_ref, dg3_ref, dg2_ref, relu_ref):
        @pl.when(pl.program_id(0) == 0)
        def _():
            for ref in (loss_ref, dg3_ref, dg2_ref):
                ref[...] = jnp.zeros_like(ref)

        n2 = n2_ref[...]
        h1 = h1_ref[...]
        h2 = h1
        for k in range(N_CHIPS):
            cols = slice(k * D_MODEL, (k + 1) * D_MODEL)
            r = jnp.maximum(_dot(n2, wup_ref[k]), 0.0)
            relu_ref[:, cols] = r
            act = (r * r).astype(BF16)
            act_ref[:, cols] = act
            h2 = h2 + _dot(act, wdown_ref[cols, :])
        xhat, r3 = _rms(h2)
        diff = xhat * g3_ref[...] - tgt_ref[...]
        sq = jnp.sum(diff * diff, axis=1, keepdims=True)
        loss_ref[...] = loss_ref[...] + (0.5 / D_MODEL) * jnp.sum(sq, axis=0, keepdims=True)
        dy = diff * (1.0 / D_MODEL)
        dg3_ref[...] = dg3_ref[...] + _col_sum(dy * xhat)
        dh2 = _rms_bwd(dy * g3_ref[...], xhat, r3)
        dh2b = dh2.astype(BF16)
        dh2b_ref[...] = dh2b
        dn2 = jnp.zeros((SEQ_TILE, D_MODEL), F32)
        for k in range(N_CHIPS):
            cols = slice(k * D_MODEL, (k + 1) * D_MODEL)
            dup = (_dot_nt(dh2b, wdown_ref[cols, :]) * (2.0 * relu_ref[:, cols])).astype(BF16)
            dup_ref[:, cols] = dup
            dn2 = dn2 + _dot_nt(dup, wup_ref[k])
        xhat, r2 = _rms(h1)
        dg2_ref[...] = dg2_ref[...] + _col_sum(dn2 * xhat)
        dh1_ref[...] = dh2 + _rms_bwd(dn2 * g2_ref[...], xhat, r2)

    tile = pl.BlockSpec((SEQ_TILE, D_MODEL), lambda i: (i, 0))
    wide = pl.BlockSpec((SEQ_TILE, D_FF), lambda i: (i, 0))
    vec = _const((1, D_MODEL))
    vec_shape = jax.ShapeDtypeStruct((1, D_MODEL), F32)
    return _fused_call(
        body, jobs, name="mlp", grid=(t // SEQ_TILE,),
        in_specs=[tile, tile, tile, _resident((N_CHIPS, D_MODEL, D_MODEL)), _resident((D_FF, D_MODEL)), vec, vec],
        out_specs=[wide, wide, tile, tile, _const((SUBLANES, 128)), vec, vec],
        out_shape=[jax.ShapeDtypeStruct((t, D_FF), BF16), jax.ShapeDtypeStruct((t, D_FF), BF16),
                   jax.ShapeDtypeStruct((t, D_MODEL), BF16), jax.ShapeDtypeStruct((t, D_MODEL), F32),
                   jax.ShapeDtypeStruct((SUBLANES, 128), F32), vec_shape, vec_shape],
        scratch_shapes=[pltpu.VMEM((SEQ_TILE, D_FF), F32)],
        compiler_params=_params(),
    )(n2, h1, target, w_up_st, w_down, g2, g3)


def _bwd_mix(dh1, pa, pb, z, h, xc, r, ig, w_oa, w_ob, w_out, ln_g, ln_b, w_s, bias_full, conv_w, wr, wi, lam, jobs=()):
    t = dh1.shape[0]
    n_tiles = t // SEQ_TILE
    per_tile = SEQ_TILE // SUBLANES

    def merge_part(dh1_ref, pa_ref, pb_ref, m_ref, woa_ref, wob_ref, wout_ref, dz_ref, dya_ref, dyb_ref, mg_ref,
                   dpa_ref, dpb_ref, dh1b_ref):
        dh1b = dh1_ref[...].astype(BF16)
        dh1b_ref[...] = dh1b
        dm = _dot_nt(dh1b, wout_ref[...])
        pa = pa_ref[...]
        pb = pb_ref[...]
        sa = _gate(m_ref[:, :D_MODEL])
        sb = _gate(m_ref[:, D_MODEL:])
        mg_ref[...] = (sa * pa + sb * pb).astype(BF16)
        dpa = dm * sa
        dpb = dm * sb
        dz_ref[:, :D_MODEL] = ((dpa * pa) * (1.0 - sa)).astype(BF16)
        dz_ref[:, D_MODEL:] = ((dpb * pb) * (1.0 - sb)).astype(BF16)
        dpa = dpa.astype(BF16)
        dpb = dpb.astype(BF16)
        dpa_ref[...] = dpa
        dpb_ref[...] = dpb
        dya_ref[...] = _dot_nt(dpa, woa_ref[...])
        dyb_ref[...] = _dot_nt(dpb, wob_ref[...])

    def sgu_part(dyb_ref, ub_ref, vb_ref, lg_ref, lb_ref, ws_ref, bias_ref, dz_ref, dlg_ref, dlb_ref, dws_ref, dbs_ref,
                 dvn_ref, dsp_acc):
        i = pl.program_id(0)

        @pl.when(i == 0)
        def _():
            dlg_ref[...] = jnp.zeros_like(dlg_ref)
            dlb_ref[...] = jnp.zeros_like(dlb_ref)
            dws_ref[...] = jnp.zeros_like(dws_ref)
            dsp_acc[...] = jnp.zeros_like(dsp_acc)

        u, du, dvg, rstd, vhat, vn = _sgu_forward_parts(ub_ref[...], vb_ref[...], lg_ref, lb_ref)
        dyb = dyb_ref[...]
        mask = _causal_mask()
        wm = [jnp.where(mask, ws_ref[g], 0.0).astype(BF16) for g in range(GROUPS)]
        for c in range(SEQ_TILE // CHUNK):
            rows = slice(c * CHUNK, (c + 1) * CHUNK)
            for g in range(GROUPS):
                cols = slice(g * GROUP_DIM, (g + 1) * GROUP_DIM)
                vn_blk = vn[rows, cols]
                sp = _dot(wm[g], vn_blk) + bias_ref[:, cols]
                dyb_blk = dyb[rows, cols]
                dz_ref[rows, cols] = (dyb_blk * sp * du[rows, cols]).astype(BF16)
                dsp = dyb_blk * u[rows, cols]
                dsp_acc[:, cols] = dsp_acc[:, cols] + dsp
                dspb = dsp.astype(BF16)
                dvn_ref[rows, cols] = _dot_tn(wm[g], dspb)
                wcols = slice(g * CHUNK, (g + 1) * CHUNK)
                dws_ref[:, wcols] = dws_ref[:, wcols] + jnp.where(mask, _dot_nt(dspb, vn_blk), 0.0)
        dvn = dvn_ref[...]
        dlg_ref[...] = dlg_ref[...] + _col_sum(dvn * vhat)
        dlb_ref[...] = dlb_ref[...] + _col_sum(dvn)
        dvhat = dvn * lg_ref[...]
        dvgel = rstd * (dvhat - jnp.mean(dvhat, axis=-1, keepdims=True)
                        - vhat * jnp.mean(dvhat * vhat, axis=-1, keepdims=True))
        dz_ref[:, D_MODEL:] = (dvgel * dvg).astype(BF16)

        @pl.when(i == n_tiles - 1)
        def _():
            lane = lax.broadcasted_iota(jnp.int32, (CHUNK, 128), 1)
            out = jnp.zeros((CHUNK, 128), F32)
            for g in range(GROUPS):
                s = jnp.sum(dsp_acc[:, g * GROUP_DIM:(g + 1) * GROUP_DIM], axis=1, keepdims=True)
                out = out + jnp.where(lane == g, s, 0.0)
            dbs_ref[...] = out

    def lru_part(dya_ref, xa_ref, ga_ref, h_ref, h_prev_ref, xc_ref, r_ref, ig_ref, cw_ref, wr_ref, wi_ref, lam_ref,
                 dz_ref, dcw_ref, dcb_ref, dwr_ref, dbr_ref, dwi_ref, dbi_ref, dlam_ref, lam_carry, dxc_head):
        i = pl.program_id(0)

        @pl.when(i == 0)
        def _():
            for ref in (dcw_ref, dcb_ref, dwr_ref, dbr_ref, dwi_ref, dbi_ref, dlam_ref, lam_carry, dxc_head):
                ref[...] = jnp.zeros_like(ref)

        first_tile = i == n_tiles - 1
        h_tail = jnp.where(first_tile, 0.0, h_prev_ref[...])
        xc, r, ig = xc_ref[...], r_ref[...], ig_ref[...]
        xcb = xc.astype(BF16)
        sp, a, mult, inv_mult = _decay(r, lam_ref)
        h = h_ref[...]
        h_prev = _shift_down(h, h_tail, 1)
        dya = dya_ref[...]
        gg, dgg = _gelu_and_grad(ga_ref[...].astype(F32))
        dz_ref[:, D_MODEL:] = (dya * h * dgg).astype(BF16)
        ones = jnp.ones((SUBLANES, D_MODEL), F32)
        lam_t, lam_first = _scan_backward(_shift_up(a, ones, 1), dya * gg, lam_carry[...])
        lam_carry[...] = a[0:1] * lam_first
        lam_ig = lam_t * ig
        dxc_direct = lam_ig * mult
        dmult = lam_ig * xc
        dla = a * (lam_t * h_prev - (dmult * a) * inv_mult)
        dla_r = dla * r
        dlam_ref[...] = dlam_ref[...] + _col_sum(dla_r) * (LRU_C * jax.nn.sigmoid(-lam_ref[...]))
        dpr = (dla_r * ((-LRU_C) * sp)) * (1.0 - r)
        dpi = (dxc_direct * xc) * (1.0 - ig)
        dbr_ref[...] = dbr_ref[...] + _col_sum(dpr)
        dbi_ref[...] = dbi_ref[...] + _col_sum(dpi)
        dprb = dpr.astype(BF16)
        dpib = dpi.astype(BF16)
        dxc_gate = []
        for hd in range(HEADS):
            cols = slice(hd * HEAD_DIM, (hd + 1) * HEAD_DIM)
            dxc_gate.append(_dot_nt(dprb[:, cols], wr_ref[hd]) + _dot_nt(dpib[:, cols], wi_ref[hd]))
            dwr_ref[hd] = dwr_ref[hd] + _dot_tn(xcb[:, cols], dprb[:, cols])
            dwi_ref[hd] = dwi_ref[hd] + _dot_tn(xcb[:, cols], dpib[:, cols])
        dxc = dxc_direct + jnp.concatenate(dxc_gate, axis=1)
        dcb_ref[...] = dcb_ref[...] + _col_sum(dxc)
        cw = cw_ref[...]
        head = dxc_head[...]
        xa = xa_ref[...].astype(F32)
        dxa = cw[0:1] * dxc
        dcw_ref[0:1, :] = dcw_ref[0:1, :] + _col_sum(dxc * xa)
        for k in range(1, CONV_WIDTH):
            dxc_k = _shift_up(dxc, head, k)
            dxa = dxa + cw[k:k + 1] * dxc_k
            dcw_ref[k:k + 1, :] = dcw_ref[k:k + 1, :] + _col_sum(dxc_k * xa)
        dxc_head[...] = dxc[0:SUBLANES]
        dz_ref[:, :D_MODEL] = dxa.astype(BF16)

    def body(dh1_ref, pa_ref, pb_ref, z_ref, h_ref, h_prev_ref, xc_ref, r_ref, ig_ref, woa_ref, wob_ref, wout_ref,
             lg_ref, lb_ref, ws_ref, bias_ref, cw_ref, wr_ref, wi_ref, lam_ref, dz_ref, mg_ref, dpa_ref, dpb_ref,
             dh1b_ref, dlg_ref, dlb_ref, dws_ref, dbs_ref, dcw_ref, dcb_ref, dwr_ref, dbr_ref, dwi_ref, dbi_ref,
             dlam_ref, dya_ref, dyb_ref, dvn_ref, dsp_acc, lam_carry, dxc_head):
        def cols(ref, first, count):
            return ref.at[:, pl.ds(first * D_MODEL, count * D_MODEL)]

        merge_part(dh1_ref, pa_ref, pb_ref, cols(z_ref, 4, 2), woa_ref, wob_ref, wout_ref, cols(dz_ref, 4, 2), dya_ref,
                   dyb_ref, mg_ref, dpa_ref, dpb_ref, dh1b_ref)
        sgu_part(dyb_ref, cols(z_ref, 2, 1), cols(z_ref, 3, 1), lg_ref, lb_ref, ws_ref, bias_ref, cols(dz_ref, 2, 2),
                 dlg_ref, dlb_ref, dws_ref, dbs_ref, dvn_ref, dsp_acc)
        lru_part(dya_ref, cols(z_ref, 0, 1), cols(z_ref, 1, 1), h_ref, h_prev_ref, xc_ref, r_ref, ig_ref, cw_ref, wr_ref,
                 wi_ref, lam_ref, cols(dz_ref, 0, 2), dcw_ref, dcb_ref, dwr_ref, dbr_ref, dwi_ref, dbi_ref, dlam_ref,
                 lam_carry, dxc_head)

    rev = lambda i: n_tiles - 1 - i
    tile = pl.BlockSpec((SEQ_TILE, D_MODEL), lambda i: (rev(i), 0))
    row = pl.BlockSpec((SEQ_TILE, D_IN), lambda i: (rev(i), 0))
    prev8 = pl.BlockSpec((SUBLANES, D_MODEL), lambda i: (jnp.maximum(rev(i) * per_tile - 1, 0), 0))
    vec = _const((1, D_MODEL))
    sq = _resident((D_MODEL, D_MODEL))
    gate_w = _resident((HEADS, HEAD_DIM, HEAD_DIM))
    gate_acc = _const((HEADS, HEAD_DIM, HEAD_DIM))
    vec_shape = jax.ShapeDtypeStruct((1, D_MODEL), F32)
    gate_shape = jax.ShapeDtypeStruct((HEADS, HEAD_DIM, HEAD_DIM), F32)
    act_bf = jax.ShapeDtypeStruct((t, D_MODEL), BF16)
    return _fused_call(
        body, jobs, name="bwd_mix", grid=(n_tiles,),
        in_specs=[tile, tile, tile, row, tile, prev8, tile, tile, tile, sq, sq, sq, vec, vec,
                  _const((GROUPS, CHUNK, CHUNK)), _const((CHUNK, D_MODEL)), _const((CONV_WIDTH, D_MODEL)), gate_w, gate_w,
                  vec],
        out_specs=[row, tile, tile, tile, tile, vec, vec, _const((CHUNK, GROUPS * CHUNK)), _const((CHUNK, 128)),
                   _const((SUBLANES, D_MODEL)), vec, gate_acc, vec, gate_acc, vec, vec],
        out_shape=[jax.ShapeDtypeStruct((t, D_IN), BF16), act_bf, act_bf, act_bf, act_bf, vec_shape, vec_shape,
                   jax.ShapeDtypeStruct((CHUNK, GROUPS * CHUNK), F32), jax.ShapeDtypeStruct((CHUNK, 128), F32),
                   jax.ShapeDtypeStruct((SUBLANES, D_MODEL), F32), vec_shape, gate_shape, vec_shape, gate_shape,
                   vec_shape, vec_shape],
        scratch_shapes=[pltpu.VMEM((SEQ_TILE, D_MODEL), F32), pltpu.VMEM((SEQ_TILE, D_MODEL), F32),
                        pltpu.VMEM((SEQ_TILE, D_MODEL), F32), pltpu.VMEM((CHUNK, D_MODEL), F32),
                        pltpu.VMEM((1, D_MODEL), F32), pltpu.VMEM((SUBLANES, D_MODEL), F32)],
        compiler_params=_params(),
    )(dh1, pa, pb, z, h, h, xc, r, ig, w_oa, w_ob, w_out, ln_g, ln_b, w_s, bias_full, conv_w, wr, wi, lam)


def _bwd_in(dz, x, dh1, w_in_st, g1, jobs=()):
    t = x.shape[0]

    def body(dz_ref, x_ref, dh1_ref, w_ref, g_ref, dx_ref, dg1_ref):
        @pl.when(pl.program_id(0) == 0)
        def _():
            dg1_ref[...] = jnp.zeros_like(dg1_ref)

        dn1 = jnp.zeros((MM_TILE, D_MODEL), F32)
        for k in range(N_CHIPS):
            dn1 = dn1 + _dot_nt(dz_ref[:, k * IN_SHARD:(k + 1) * IN_SHARD], w_ref[k])
        xhat, r1 = _rms(x_ref[...])
        dg1_ref[...] = dg1_ref[...] + _col_sum(dn1 * xhat)
        dx_ref[...] = dh1_ref[...] + _rms_bwd(dn1 * g_ref[...], xhat, r1)

    tile = pl.BlockSpec((MM_TILE, D_MODEL), lambda i: (i, 0))
    return _fused_call(
        body, jobs, name="bwd_in", grid=(t // MM_TILE,),
        in_specs=[pl.BlockSpec((MM_TILE, D_IN), lambda i: (i, 0)), tile, tile,
                  _resident((N_CHIPS, D_MODEL, IN_SHARD)), _const((1, D_MODEL))],
        out_specs=[tile, _const((1, D_MODEL))],
        out_shape=[jax.ShapeDtypeStruct((t, D_MODEL), F32), jax.ShapeDtypeStruct((1, D_MODEL), F32)],
        compiler_params=_params(),
    )(dz, x, dh1, w_in_st, g1)


def _weight_grad(name, a, b, n_blocks, a_varies, b_varies, width, jobs=()):
    t = a.shape[0]
    rows = min(DW_TILE, t)
    n_t = t // rows

    def body(a_ref, b_ref, o_ref, acc_ref):
        s = pl.program_id(1)
        part = _dot_tn(a_ref[...], b_ref[...])

        @pl.when(s == 0)
        def _():
            acc_ref[...] = part

        @pl.when(s > 0)
        def _():
            acc_ref[...] = acc_ref[...] + part

        @pl.when(s == n_t - 1)
        def _():
            o_ref[...] = acc_ref[...].astype(BF16)

    return _fused_call(
        body, jobs, name=name, grid=(n_blocks, n_t),
        in_specs=[pl.BlockSpec((rows, D_MODEL), (lambda j, s: (s, j)) if a_varies else (lambda j, s: (s, 0))),
                  pl.BlockSpec((rows, width), (lambda j, s: (s, j)) if b_varies else (lambda j, s: (s, 0)))],
        out_specs=pl.BlockSpec((None, D_MODEL, width), lambda j, s: (j, 0, 0)),
        out_shape=jax.ShapeDtypeStruct((n_blocks, D_MODEL, width), BF16),
        scratch_shapes=[pltpu.VMEM((D_MODEL, width), F32)],
        compiler_params=_params(2),
    )(a, b)


def _weight_grads_square(name, pairs, jobs=()):
    n = len(pairs)
    t = pairs[0][0].shape[0]
    rows = min(2 * MM_TILE, t)
    n_t = t // rows

    def body(*refs):
        ins, outs, accs = refs[:2 * n], refs[2 * n:3 * n], refs[3 * n:]
        s = pl.program_id(0)
        for k in range(n):
            part = _dot_tn(ins[2 * k][...], ins[2 * k + 1][...])

            @pl.when(s == 0)
            def _(k=k, part=part):
                accs[k][...] = part

            @pl.when(s > 0)
            def _(k=k, part=part):
                accs[k][...] = accs[k][...] + part

            @pl.when(s == n_t - 1)
            def _(k=k):
                outs[k][...] = accs[k][...].astype(BF16)

    tile = pl.BlockSpec((rows, D_MODEL), lambda s: (s, 0))
    return _fused_call(
        body, jobs, name=name, grid=(n_t,), in_specs=[tile] * (2 * n), out_specs=[_const((D_MODEL, D_MODEL))] * n,
        out_shape=[jax.ShapeDtypeStruct((D_MODEL, D_MODEL), BF16)] * n,
        scratch_shapes=[pltpu.VMEM((D_MODEL, D_MODEL), F32)] * n,
        compiler_params=_params(),
    )(*[x for pair in pairs for x in pair])


def _place():
    x, y, c = lax.axis_index("x"), lax.axis_index("y"), lax.axis_index("c")
    other_chips = [(1 - x, y), (x, 1 - y), (1 - x, 1 - y)]
    return x, y, c, other_chips


def _chip_index(px, py):
    return 2 * px + py


ANY = pl.BlockSpec(memory_space=pl.ANY)
SIBLING = ((0, 0, 1),)
NEIGHBOURS = ((1, 0, 0), (0, 1, 0))
OTHER_CHIPS = NEIGHBOURS + ((1, 1, 0),)


def _near_far(x, y, c):
    return (x ^ (1 - c), y ^ c), (x ^ c, y ^ (1 - c))


def _gather_near_job(shards):
    n = len(shards)
    halves = [s.shape[0] // 2 for s in shards]

    def copies(ins, outs, send, recv, local):
        x, y, c, _ = _place()
        near, _ = _near_far(x, y, c)

        def block(w, chip, pc):
            return outs[w].at[_chip_index(*chip), pl.ds(pc * halves[w], halves[w]), :]

        def copy(w, k, chip, pc, to, src=None):
            return pltpu.make_async_remote_copy(
                src_ref=block(w, chip, pc) if src is None else src, dst_ref=block(w, chip, pc),
                send_sem=send.at[2 * w + k], recv_sem=recv.at[2 * w + k], device_id=to, device_id_type=MESH)

        sends, arrivals, own = [], [], []
        for w in range(n):
            src = ins[w].at[pl.ds(c * halves[w], halves[w]), :]
            own.append(pltpu.make_async_copy(src, block(w, (x, y), c), local.at[w]))
            sends += [copy(w, 0, (x, y), c, (*near, c), src), copy(w, 1, (x, y), c, (x, y, 1 - c), src)]
            arrivals += [copy(w, 0, near, c, (x, y, c)), copy(w, 1, (x, y), 1 - c, (x, y, c))]
        return sends, arrivals, own

    return _Job(shards, [jax.ShapeDtypeStruct((N_CHIPS,) + s.shape, s.dtype) for s in shards], 2 * n, copies,
                NEIGHBOURS + SIBLING, n_local=n)


def _gather_far_job(stacked):
    n = len(stacked)
    halves = [s.shape[1] // 2 for s in stacked]

    def copies(ins, outs, send, recv, local):
        del ins, local
        x, y, c, _ = _place()
        near, far = _near_far(x, y, c)

        def copy(w, k, chip):
            blk = outs[w].at[_chip_index(*chip), pl.ds(c * halves[w], halves[w]), :]
            return pltpu.make_async_remote_copy(
                src_ref=blk, dst_ref=blk, send_sem=send.at[2 * w + k], recv_sem=recv.at[2 * w + k],
                device_id=(*far, c), device_id_type=MESH)

        sends = [copy(w, k, chip) for w in range(n) for k, chip in enumerate(((x, y), near))]
        arrivals = [copy(w, k, chip) for w in range(n) for k, chip in enumerate((far, (1 - x, 1 - y)))]
        return sends, arrivals, []

    return _Job(stacked, [jax.ShapeDtypeStruct(s.shape, s.dtype) for s in stacked], 2 * n, copies, NEIGHBOURS,
                aliases={w: w for w in range(n)})


def _gather_pass_job(stacked):
    n = len(stacked)
    halves = [s.shape[1] // 2 for s in stacked]

    def copies(ins, outs, send, recv, local):
        del ins, local
        x, y, c, chips = _place()

        def copy(w, j, chip, pc, to):
            blk = outs[w].at[_chip_index(*chip), pl.ds(pc * halves[w], halves[w]), :]
            return pltpu.make_async_remote_copy(
                src_ref=blk, dst_ref=blk, send_sem=send.at[3 * w + j], recv_sem=recv.at[3 * w + j], device_id=to,
                device_id_type=MESH)

        sends = [copy(w, j, chip, c, (x, y, 1 - c)) for w in range(n) for j, chip in enumerate(chips)]
        arrivals = [copy(w, j, chip, 1 - c, (x, y, c)) for w in range(n) for j, chip in enumerate(chips)]
        return sends, arrivals, []

    return _Job(stacked, [jax.ShapeDtypeStruct(s.shape, s.dtype) for s in stacked], 3 * n, copies, SIBLING,
                aliases={w: w for w in range(n)})


def _gather_small_job(block):
    def copies(ins, outs, send, recv, local):
        x, y, c, chips = _place()

        def copy(j, chip_from, to):
            return pltpu.make_async_remote_copy(
                src_ref=ins[0], dst_ref=outs[0].at[_chip_index(*chip_from)], send_sem=send.at[j],
                recv_sem=recv.at[j], device_id=to, device_id_type=MESH)

        own = [pltpu.make_async_copy(ins[0], outs[0].at[_chip_index(x, y)], local.at[0])]
        sends = [copy(j, (x, y), (*chip, c)) for j, chip in enumerate(chips)]
        arrivals = [copy(j, chip, (x, y, c)) for j, chip in enumerate(chips)]
        return sends, arrivals, own

    return _Job([block], [jax.ShapeDtypeStruct((N_CHIPS,) + block.shape, block.dtype)], 3, copies, OTHER_CHIPS,
                n_local=1)


def _pair_send_job(grads):
    n = len(grads)
    halves = [g.shape[1] // 2 for g in grads]

    def copies(ins, outs, send, recv, local):
        del local
        x, y, c, _ = _place()
        sends = [pltpu.make_async_remote_copy(
            src_ref=ins[w].at[:, pl.ds((1 - c) * halves[w], halves[w]), :], dst_ref=outs[w], send_sem=send.at[w],
            recv_sem=recv.at[w], device_id=(x, y, 1 - c), device_id_type=MESH) for w in range(n)]
        return sends, sends, []

    return _Job(grads, [jax.ShapeDtypeStruct((N_CHIPS, h, g.shape[2]), g.dtype) for g, h in zip(grads, halves)], n,
                copies, SIBLING)


def _row_block(rows, limit=256):
    return min(rows, limit)


def _pair_add(name, core, mine, theirs):
    _, _, h, cols = mine.shape
    rb = _row_block(h, 512)

    def body(core_ref, a_ref, b_ref, o_ref):
        del core_ref
        o_ref[...] = (a_ref[...].astype(F32) + b_ref[...].astype(F32)).astype(BF16)

    return pl.pallas_call(
        body, name=name,
        grid_spec=pltpu.PrefetchScalarGridSpec(
            num_scalar_prefetch=1, grid=(N_CHIPS, h // rb),
            in_specs=[pl.BlockSpec((None, None, rb, cols), lambda k, r, core_ref: (k, core_ref[0], r, 0)),
                      pl.BlockSpec((None, rb, cols), lambda k, r, core_ref: (k, r, 0))],
            out_specs=pl.BlockSpec((None, rb, cols), lambda k, r, core_ref: (k, r, 0))),
        out_shape=jax.ShapeDtypeStruct(theirs.shape, BF16),
        compiler_params=_params(2),
    )(core, mine, theirs)


def _sequencer_call(name, collective_id, job):
    steps, peers = job.phases, job.peers
    ins = [jax.new_ref(a, memory_space=pltpu.MemorySpace.HBM) for a in job.inputs]
    outs = [ins[{o: i for i, o in job.aliases.items()}[k]] if k in job.aliases.values()
            else jax.empty_ref(shape, memory_space=pltpu.MemorySpace.HBM) for k, shape in enumerate(job.out_shape)]
    sems = [pltpu.SemaphoreType.DMA((n,)) for step in steps for n in (step.n_sem, step.n_sem, max(step.n_local, 1))]

    @pl.kernel(mesh=plsc.ScalarSubcoreMesh(axis_name="sequencer", num_cores=1), name=name, scratch_types=tuple(sems),
               compiler_params=pltpu.CompilerParams(collective_id=collective_id))
    def launch(*sem_refs):
        x, y, c, _ = _place()
        barrier = pltpu.get_barrier_semaphore()
        for dx, dy, dc in peers:
            pl.semaphore_signal(barrier, inc=1, device_id=(x ^ dx, y ^ dy, c ^ dc), device_id_type=MESH)
        pl.semaphore_wait(barrier, len(peers))
        for k, step in enumerate(steps):
            sends, arrivals, own = step.copies(ins if k == 0 else outs, outs, *sem_refs[3 * k:3 * k + 3])
            for cp in own + sends:
                cp.start()
            for cp in arrivals:
                cp.wait_recv()
            for cp in sends:
                cp.wait_send()
            for cp in own:
                cp.wait()

    launch()
    return [ref[...] for ref in outs]


def _chip_exchange_job(sums):
    n = len(sums)

    def copies(ins, outs, send, recv, local):
        del local
        _, _, c, chips = _place()
        sends = [pltpu.make_async_remote_copy(
            src_ref=ins[w].at[_chip_index(*chip)], dst_ref=outs[w].at[j], send_sem=send.at[3 * w + j],
            recv_sem=recv.at[3 * w + j], device_id=(*chip, c), device_id_type=MESH)
            for w in range(n) for j, chip in enumerate(chips)]
        return sends, sends, []

    return _Job(sums, [jax.ShapeDtypeStruct((N_CHIPS - 1,) + s.shape[1:], s.dtype) for s in sums], 3 * n, copies,
                OTHER_CHIPS)


def _chip_sum(name, place, mine, theirs):
    _, h, cols = mine.shape
    rb = _row_block(h, 512)

    def body(place_ref, p_ref, q_ref, o_ref):
        del place_ref
        acc = p_ref[...].astype(F32)
        for j in range(N_CHIPS - 1):
            acc = acc + q_ref[j].astype(F32)
        o_ref[...] = acc

    return pl.pallas_call(
        body, name=name,
        grid_spec=pltpu.PrefetchScalarGridSpec(
            num_scalar_prefetch=1, grid=(h // rb,),
            in_specs=[pl.BlockSpec((None, rb, cols), lambda r, place_ref: (place_ref[0], r, 0)),
                      pl.BlockSpec((N_CHIPS - 1, rb, cols), lambda r, place_ref: (0, r, 0))],
            out_specs=pl.BlockSpec((None, rb, cols), lambda r, place_ref: (place_ref[1], r, 0))),
        out_shape=jax.ShapeDtypeStruct((2, h, cols), F32),
        compiler_params=_params(),
    )(place, mine, theirs)


def _share_job(bufs):
    n = len(bufs)

    def copies(ins, outs, send, recv, local):
        del ins, local
        x, y, c, _ = _place()

        def copy(w, half):
            return pltpu.make_async_remote_copy(
                src_ref=outs[w].at[half], dst_ref=outs[w].at[half], send_sem=send.at[w], recv_sem=recv.at[w],
                device_id=(x, y, 1 - c), device_id_type=MESH)

        return [copy(w, c) for w in range(n)], [copy(w, 1 - c) for w in range(n)], []

    return _Job(bufs, [jax.ShapeDtypeStruct(b.shape, b.dtype) for b in bufs], n, copies, SIBLING,
                aliases={w: w for w in range(n)})


SMALL_ROWS = 24
ROW_G1, ROW_CW, ROW_CB, ROW_BR, ROW_BI, ROW_LAM, ROW_LG, ROW_LB, ROW_G2, ROW_G3, ROW_LOSS, ROW_BS = (
    0, 1, 5, 6, 7, 8, 9, 10, 11, 12, 13, 16)
N_DEV = 8


def _pack_small(dcw, dcb, dbr, dbi, dlam, dlg, dlb, dg2, dg3, loss, dbs):
    def body(dcw_ref, dcb_ref, dbr_ref, dbi_ref, dlam_ref, dlg_ref, dlb_ref, dg2_ref, dg3_ref, loss_ref, dbs_ref, out):
        out[...] = jnp.zeros((SMALL_ROWS, D_MODEL), F32)
        for row, ref in ((ROW_CB, dcb_ref), (ROW_BR, dbr_ref), (ROW_BI, dbi_ref), (ROW_LAM, dlam_ref),
                         (ROW_LG, dlg_ref), (ROW_LB, dlb_ref), (ROW_G2, dg2_ref), (ROW_G3, dg3_ref)):
            out[row:row + 1, :] = ref[...]
        out[ROW_CW:ROW_CW + CONV_WIDTH, :] = dcw_ref[0:CONV_WIDTH, :]
        out[ROW_LOSS:ROW_LOSS + 1, 0:128] = loss_ref[0:1, :]
        out[ROW_BS:ROW_BS + GROUPS, 0:128] = jnp.transpose(dbs_ref[...])[0:GROUPS, :]

    vm = pl.BlockSpec(memory_space=pltpu.VMEM)
    return pl.pallas_call(
        body, name="pack_small", in_specs=[vm] * 11, out_specs=vm,
        out_shape=jax.ShapeDtypeStruct((SMALL_ROWS, D_MODEL), F32),
    )(dcw, dcb, dbr, dbi, dlam, dlg, dlb, dg2, dg3, loss, dbs)


def _gather_all_job(blocks):
    n = len(blocks)
    flips = [(dx, dy, dc) for dx in (0, 1) for dy in (0, 1) for dc in (0, 1)][1:]

    def copies(ins, outs, send, recv, local):
        x, y, c, _ = _place()
        me = 4 * x + 2 * y + c
        sends, arrivals, own = [], [], []
        for w in range(n):
            own.append(pltpu.make_async_copy(ins[w], outs[w].at[me], local.at[w]))
            for k, (dx, dy, dc) in enumerate(flips):
                peer = (x ^ dx, y ^ dy, c ^ dc)
                sem = dict(send_sem=send.at[7 * w + k], recv_sem=recv.at[7 * w + k])
                sends.append(pltpu.make_async_remote_copy(
                    src_ref=ins[w], dst_ref=outs[w].at[me], device_id=peer, device_id_type=MESH, **sem))
                arrivals.append(pltpu.make_async_remote_copy(
                    src_ref=ins[w], dst_ref=outs[w].at[4 * peer[0] + 2 * peer[1] + peer[2]], device_id=peer,
                    device_id_type=MESH, **sem))
        return sends, arrivals, own

    return _Job(blocks, [jax.ShapeDtypeStruct((N_DEV,) + b.shape, b.dtype) for b in blocks], 7 * n, copies,
                OTHER_CHIPS + SIBLING + tuple((dx, dy, 1) for dx, dy, _ in OTHER_CHIPS), n_local=n)


def _sum_small(vec_all, ws_all, dg1_all):
    def body(vec_ref, ws_ref, dg1_ref, vec_out, ws_out):
        vec, ws, dg1 = vec_ref[0], ws_ref[0], dg1_ref[0]
        for d in range(1, N_DEV):
            vec, ws, dg1 = vec + vec_ref[d], ws + ws_ref[d], dg1 + dg1_ref[d]
        vec_out[...] = vec
        vec_out[ROW_G1:ROW_G1 + 1, :] = dg1
        ws_out[...] = ws

    vm = pl.BlockSpec(memory_space=pltpu.VMEM)
    return pl.pallas_call(
        body, name="sum_small", in_specs=[vm] * 3, out_specs=[vm, vm],
        out_shape=[jax.ShapeDtypeStruct(vec_all.shape[1:], F32), jax.ShapeDtypeStruct(ws_all.shape[1:], F32)],
    )(vec_all, ws_all, dg1_all)


def _adamw_math(w, g, m, v):
    m = ADAM_B1 * m + (1.0 - ADAM_B1) * g
    v = ADAM_B2 * v + (1.0 - ADAM_B2) * (g * g)
    m_hat = m / (1.0 - ADAM_B1 ** ADAM_STEP)
    v_hat = v / (1.0 - ADAM_B2 ** ADAM_STEP)
    delta = (-ADAM_LR) * (m_hat / (jnp.sqrt(v_hat) + ADAM_EPS) + ADAM_WD * w)
    return delta, m, v


def _adamw(name, g, w, m, v, jobs=()):
    rows, cols = w.shape
    rb = _row_block(rows)

    def body(g_ref, w_ref, m_ref, v_ref, d_ref, nm_ref, nv_ref):
        d_ref[...], nm_ref[...], nv_ref[...] = _adamw_math(w_ref[...], g_ref[...], m_ref[...], v_ref[...])

    blk = pl.BlockSpec((rb, cols), lambda r: (r, 0))
    return _fused_call(
        body, jobs, name=name, grid=(rows // rb,), in_specs=[blk] * 4, out_specs=[blk] * 3,
        out_shape=[jax.ShapeDtypeStruct(w.shape, F32)] * 3, compiler_params=_params(),
    )(g, w, m, v)


def _adamw_small(grads, ws, ms, vs):
    n = len(grads)

    def body(*refs):
        g_refs, w_refs, m_refs, v_refs = refs[:n], refs[n:2 * n], refs[2 * n:3 * n], refs[3 * n:4 * n]
        outs = refs[4 * n:]
        for p in range(n):
            d, nm, nv = _adamw_math(w_refs[p][...], g_refs[p][...], m_refs[p][...], v_refs[p][...])
            outs[p][...] = d
            outs[n + p][...] = nm
            outs[2 * n + p][...] = nv

    vm = pl.BlockSpec(memory_space=pltpu.VMEM)
    shapes = [jax.ShapeDtypeStruct(w.shape, F32) for w in ws]
    out = pl.pallas_call(
        body, name="adamw_small", in_specs=[vm] * (4 * n), out_specs=[vm] * (3 * n), out_shape=shapes * 3,
    )(*grads, *ws, *ms, *vs)
    return out[:n], out[n:2 * n], out[2 * n:]


def _unstack_heads(w_st):
    per = HEAD_DIM // N_CHIPS
    return w_st.reshape(N_CHIPS, HEADS, per, HEAD_DIM).transpose(1, 0, 2, 3).reshape(HEADS, HEAD_DIM, HEAD_DIM)


def _stack_heads(w):
    per = HEAD_DIM // N_CHIPS
    return w.reshape(HEADS, N_CHIPS, per, HEAD_DIM).transpose(1, 0, 2, 3).reshape(N_CHIPS, HEADS * per, HEAD_DIM)


def kernel(x, norm_mix_g, w_in, conv_w, conv_b, w_rgate, b_rgate, w_igate, b_igate, lru_lambda, w_out_a, sgu_ln_g, sgu_ln_b, sgu_w_s, sgu_b_s, w_out_b, w_out, norm_mlp_g, w_up, w_down, norm_final_g, loss_target, m_norm_mix_g, m_w_in, m_conv_w, m_conv_b, m_w_rgate, m_b_rgate, m_w_igate, m_b_igate, m_lru_lambda, m_w_out_a, m_sgu_ln_g, m_sgu_ln_b, m_sgu_w_s, m_sgu_b_s, m_w_out_b, m_w_out, m_norm_mlp_g, m_w_up, m_w_down, m_norm_final_g, v_norm_mix_g, v_w_in, v_conv_w, v_conv_b, v_w_rgate, v_b_rgate, v_w_igate, v_b_igate, v_lru_lambda, v_w_out_a, v_sgu_ln_g, v_sgu_ln_b, v_sgu_w_s, v_sgu_b_s, v_w_out_b, v_w_out, v_norm_mlp_g, v_w_up, v_w_down, v_norm_final_g):
    chip = _chip_index(lax.axis_index("x"), lax.axis_index("y"))
    core = lax.axis_index("c")
    quarter_h = HEAD_DIM // N_CHIPS
    quarter_d = D_MODEL // N_CHIPS

    as_2d = lambda a: a.reshape(-1, a.shape[-1])
    big_w = [as_2d(w) for w in (w_in, w_rgate, w_igate, w_out_a, w_out_b, w_out, w_up, w_down)]
    big_m = [as_2d(w) for w in (m_w_in, m_w_rgate, m_w_igate, m_w_out_a, m_w_out_b, m_w_out, m_w_up, m_w_down)]
    big_v = [as_2d(w) for w in (v_w_in, v_w_rgate, v_w_igate, v_w_out_a, v_w_out_b, v_w_out, v_w_up, v_w_down)]

    packed = jnp.concatenate([conv_w[0], b_rgate[0], b_igate[0]], axis=1)
    packed = jnp.concatenate([packed, jnp.zeros_like(packed)], axis=0)
    s_in, s_r, s_i, s_oa, s_ob, s_out, s_up, s_down = [w.astype(BF16) for w in big_w]
    xs, target = x[0], loss_target[0]
    g3 = norm_final_g.reshape(1, D_MODEL)
    bias_s = jnp.broadcast_to(jnp.transpose(sgu_b_s[0])[:, :, None], (CHUNK, GROUPS, GROUP_DIM)).reshape(CHUNK, D_MODEL)
    core_arr = core.reshape(1).astype(jnp.int32)
    place = jnp.stack([chip, core]).astype(jnp.int32)
    quarter = lambda g: g.reshape(N_CHIPS, D_MODEL // N_CHIPS, D_MODEL)

    def pair_add(nm, g, from_sibling):
        return _pair_add("pair_add_" + nm, core_arr, g.reshape(N_CHIPS, 2, g.shape[1] // 2, g.shape[2]), from_sibling)

    def chip_sum(nm, pair, from_chips):
        return _chip_sum("chip_sum_" + nm, place, pair, from_chips)

    order = jnp.stack([chip, chip ^ 2, chip ^ 1, chip ^ 3]).astype(jnp.int32)
    (z, n1, (w_in_st, wr_st, wi_st)), ((packed_all,), late) = _fwd_in(
        xs, norm_mix_g, [s_in, s_r, s_i], order,
        jobs=[_gather_small_job(packed), _gather_near_job([s_oa, s_ob, s_out])])
    pick = lambda lo, hi: packed_all[:, :HEADS, lo:hi].transpose(1, 0, 2).reshape(HEADS, -1)
    conv_w_full = pick(0, quarter_d)
    br_full = pick(quarter_d, quarter_d + quarter_h).reshape(1, D_MODEL)
    bi_full = pick(quarter_d + quarter_h, quarter_d + 2 * quarter_h).reshape(1, D_MODEL)
    wr, wi = _unstack_heads(wr_st), _unstack_heads(wi_st)
    lru = (conv_w_full, conv_b, wr, br_full, wi, bi_full, lru_lambda)
    sgu = (sgu_ln_g, sgu_ln_b, sgu_w_s[0], bias_s)

    after = lambda arrays, result: lax.optimization_barrier((arrays, result))[0]
    w_up_st, w_dn = _sequencer_call(
        "gather_mlp", 8, _gather_near_job(after([s_up, s_down], n1)).then(_gather_far_job).then(_gather_pass_job))
    w_dn = w_dn.reshape(D_FF, D_MODEL)
    late_step = (3 * (xs.shape[0] // SEQ_TILE) // 4,)
    (ya, *saved), (late,) = _fwd_lru(z, *lru, jobs=[_gather_far_job(late).then(_gather_pass_job, at=late_step)])
    w_oa, w_ob, w_o = [w.reshape(D_MODEL, D_MODEL) for w in late]
    (yb, pa, pb, h1, n2), _ = _fwd_sgu_merge(ya, z, xs, *sgu, w_oa, w_ob, w_o, norm_mlp_g)
    (act, dup, dh2b, dh1, loss_part, dg3, dg2), _ = _mlp(n2, h1, target, w_up_st, w_dn, norm_mlp_g, g3)

    d_down, _ = _weight_grad("dw_down", act, dh2b, N_CHIPS, True, False, D_MODEL)
    r_down, = _sequencer_call("send_w_down", 10, _pair_send_job([d_down]))
    d_up, _ = _weight_grad("dw_up", n2, dup, N_CHIPS, False, True, D_MODEL)
    r_up, = _sequencer_call("send_w_up", 11, _pair_send_job([d_up]))
    p_down, p_up = pair_add("w_down", d_down, r_down), pair_add("w_up", d_up, r_up)
    (dz, merged, dpa, dpb, dh1b, dlg, dlb, dws, dbs, dcw, dcb, dwr, dbr, dwi, dbi, dlam), ((q_up, q_down),) = _bwd_mix(
        dh1, pa, pb, z, *saved, w_oa, w_ob, w_o, *sgu, conv_w_full, wr, wi, lru_lambda,
        jobs=[_chip_exchange_job([p_up, p_down])])
    half_up, half_down = chip_sum("w_up", p_up, q_up), chip_sum("w_down", p_down, q_down)
    names = ("w_in", "w_rgate", "w_igate", "w_out_a", "w_out_b", "w_out", "w_up", "w_down")
    (d_out, d_oa, d_ob), ((full_up, full_down),) = _weight_grads_square(
        "dw_projections", [(merged, dh1b), (ya, dpa), (yb, dpb)], jobs=[_share_job([half_up, half_down])])
    mids = [quarter(d_oa), quarter(d_ob), quarter(d_out)]
    r_mids = _sequencer_call("send_mids", 1, _pair_send_job(mids))
    gates = [_stack_heads(dwr).astype(BF16), _stack_heads(dwi).astype(BF16)]
    small = _pack_small(dcw, dcb, dbr, dbi, dlam, dlg, dlb, dg2, dg3, loss_part, dbs)
    p_mids = [pair_add(nm, g, r) for nm, g, r in zip(names[3:6], mids, r_mids)]
    q_mids = _sequencer_call("exchange_mids", 2, _chip_exchange_job(p_mids))
    d_in, (r_gates, (vec_all, ws_all)) = _weight_grad(
        "dw_in", n1, dz, N_CHIPS, False, True, IN_SHARD,
        jobs=[_pair_send_job(gates), _gather_all_job([small, dws])])
    r_in, = _sequencer_call("send_w_in", 3, _pair_send_job([d_in]))
    adam_args = {nm: (w, m, v) for nm, w, m, v in zip(names, big_w, big_m, big_v)}

    def adamw(nm, g):
        w, m, v = adam_args[nm]
        g = g.reshape(w.shape)
        return g, _adamw("adamw_" + nm, g, w, m, v)[0]

    p_gates = [pair_add(nm, g, r) for nm, g, r in zip(names[1:3], gates, r_gates)]
    half_mids = [chip_sum(nm, p, q) for nm, p, q in zip(names[3:6], p_mids, q_mids)]
    full_mids = _sequencer_call("share_mids", 12, _share_job(half_mids))
    p_first = [pair_add("w_in", d_in, r_in)] + p_gates
    q_first = _sequencer_call("exchange_w_in", 4, _chip_exchange_job(p_first))
    (grad_x, dg1), _ = _bwd_in(dz, xs, dh1, w_in_st, norm_mix_g)
    dg1_all, = _sequencer_call("gather_dg1", 6, _gather_all_job([dg1]))
    done = {nm: adamw(nm, f) for nm, f in zip(("w_up", "w_down") + names[3:6], [full_up, full_down] + full_mids)}
    q_first = after(q_first, [out[0] for _, out in done.values()])
    half_first = [chip_sum(nm, p, q) for nm, p, q in zip(names[:3], p_first, q_first)]
    full_first = _sequencer_call("share_last", 5, _share_job(half_first))
    done.update({nm: adamw(nm, f) for nm, f in zip(names[:3], full_first)})
    full, big_out = [done[nm][0] for nm in names], [done[nm][1] for nm in names]

    vec, ws_sum = _sum_small(vec_all, ws_all, dg1_all)
    row = lambda r: vec[r:r + 1]
    shard = lambda a, width: lax.dynamic_slice_in_dim(a, chip * width, width, axis=1)
    g_small = dict(
        norm_mix_g=row(ROW_G1), conv_w=shard(vec[ROW_CW:ROW_CW + CONV_WIDTH], quarter_d), conv_b=row(ROW_CB),
        b_rgate=shard(row(ROW_BR).reshape(HEADS, HEAD_DIM), quarter_h),
        b_igate=shard(row(ROW_BI).reshape(HEADS, HEAD_DIM), quarter_h), lru_lambda=row(ROW_LAM),
        sgu_ln_g=row(ROW_LG), sgu_ln_b=row(ROW_LB),
        sgu_w_s=ws_sum.reshape(CHUNK, GROUPS, CHUNK).transpose(1, 0, 2).reshape(GROUPS * CHUNK, CHUNK),
        sgu_b_s=vec[ROW_BS:ROW_BS + GROUPS, 0:CHUNK], norm_mlp_g=row(ROW_G2), norm_final_g=row(ROW_G3))
    loss = vec[ROW_LOSS, 0]
    small_names = list(g_small)
    given = dict(
        norm_mix_g=(norm_mix_g, m_norm_mix_g, v_norm_mix_g), conv_w=(conv_w, m_conv_w, v_conv_w),
        conv_b=(conv_b, m_conv_b, v_conv_b), b_rgate=(b_rgate, m_b_rgate, v_b_rgate),
        b_igate=(b_igate, m_b_igate, v_b_igate), lru_lambda=(lru_lambda, m_lru_lambda, v_lru_lambda),
        sgu_ln_g=(sgu_ln_g, m_sgu_ln_g, v_sgu_ln_g), sgu_ln_b=(sgu_ln_b, m_sgu_ln_b, v_sgu_ln_b),
        sgu_w_s=(sgu_w_s, m_sgu_w_s, v_sgu_w_s), sgu_b_s=(sgu_b_s, m_sgu_b_s, v_sgu_b_s),
        norm_mlp_g=(norm_mlp_g, m_norm_mlp_g, v_norm_mlp_g), norm_final_g=(norm_final_g, m_norm_final_g, v_norm_final_g))
    g2d = [g_small[nm] for nm in small_names]
    to2d = lambda a, g: a.reshape(g.shape)
    d_s, m_s, v_s = _adamw_small(
        g2d, *[[to2d(given[nm][q], g) for nm, g in zip(small_names, g2d)] for q in range(3)])

    shapes = dict(
        norm_mix_g=norm_mix_g, w_in=w_in, conv_w=conv_w, conv_b=conv_b, w_rgate=w_rgate, b_rgate=b_rgate,
        w_igate=w_igate, b_igate=b_igate, lru_lambda=lru_lambda, w_out_a=w_out_a, sgu_ln_g=sgu_ln_g,
        sgu_ln_b=sgu_ln_b, sgu_w_s=sgu_w_s, sgu_b_s=sgu_b_s, w_out_b=w_out_b, w_out=w_out, norm_mlp_g=norm_mlp_g,
        w_up=w_up, w_down=w_down, norm_final_g=norm_final_g)
    grads, deltas, new_m, new_v = {}, {}, {}, {}
    for nm, g, (d, nmom, nvar) in zip(names, full, big_out):
        grads[nm], deltas[nm], new_m[nm], new_v[nm] = g, d, nmom, nvar
    for p, nm in enumerate(small_names):
        grads[nm], deltas[nm], new_m[nm], new_v[nm] = g2d[p], d_s[p], m_s[p], v_s[p]
    order = list(shapes)
    out = [loss, grad_x[None]]
    for group in (grads, deltas, new_m, new_v):
        out += [group[nm].reshape(shapes[nm].shape) for nm in order]
    return tuple(out)
```

```python
import functools

import jax
import jax.numpy as jnp
from jax import lax
from jax.experimental import pallas as pl
from jax.experimental.pallas import tpu as pltpu
from jax.experimental.pallas import tpu_sc as plsc

F32 = jnp.float32
BF16 = jnp.bfloat16
MESH = pl.DeviceIdType.MESH

D_MODEL = 1024
D_IN = 6 * D_MODEL
D_FF = 4 * D_MODEL
N_CHIPS = 4
IN_SHARD = D_IN // N_CHIPS
HEADS = 4
HEAD_DIM = D_MODEL // HEADS
GROUPS = 4
GROUP_DIM = D_MODEL // GROUPS
CHUNK = 128
CONV_WIDTH = 4
LRU_C = 8.0
NORM_EPS = 1e-6
LN_EPS = 1e-5

ADAM_LR = 0.001
ADAM_B1 = 0.9
ADAM_B2 = 0.999
ADAM_EPS = 1e-08
ADAM_WD = 0.01
ADAM_STEP = 10

SUBLANES = 8
MM_TILE = 512
IN_TILE = 1024
SEQ_TILE = 256
DW_TILE = 2048
VMEM_LIMIT_BYTES = 56 * 1024 * 1024

GELU_K0 = 0.7978845608028654
GELU_K1 = 0.044715


def _params(n_grid_axes=1):
    return pltpu.CompilerParams(
        dimension_semantics=("arbitrary",) * n_grid_axes, vmem_limit_bytes=VMEM_LIMIT_BYTES)


def _resident(shape):
    nd = len(shape)
    return pl.BlockSpec(shape, lambda *_: (0,) * nd, pipeline_mode=pl.Buffered(1))


def _const(shape):
    nd = len(shape)
    return pl.BlockSpec(shape, lambda *_: (0,) * nd)


def _dot(a, b):
    return jnp.dot(a, b, preferred_element_type=F32)


def _dot_nt(a, b):
    return lax.dot_general(a, b, (((1,), (1,)), ((), ())), preferred_element_type=F32)


def _dot_tn(a, b):
    return lax.dot_general(a, b, (((0,), (0,)), ((), ())), preferred_element_type=F32)


def _gelu(x):
    t = jnp.tanh(x * (GELU_K0 + (GELU_K0 * GELU_K1) * (x * x)))
    return x * (0.5 + 0.5 * t)


def _gelu_and_grad(x):
    x2 = x * x
    t = jnp.tanh(x * (GELU_K0 + (GELU_K0 * GELU_K1) * x2))
    s = 0.5 + 0.5 * t
    dg = s + (x * (1.0 - t * t)) * (0.5 * GELU_K0 + (1.5 * GELU_K0 * GELU_K1) * x2)
    return x * s, dg


def _gate(x):
    return 0.5 + 0.5 * jnp.tanh(0.5 * x.astype(F32))


def _rms(x):
    r = lax.rsqrt(jnp.mean(x * x, axis=-1, keepdims=True) + NORM_EPS)
    return x * r, r


def _rms_bwd(dn, xhat, r):
    return r * (dn - xhat * jnp.mean(dn * xhat, axis=-1, keepdims=True))


def _col_sum(v):
    return jnp.sum(v, axis=0, keepdims=True)


def _shift_down(x, tail8, k):
    xs = pltpu.roll(x, k, 0)
    ts = pltpu.roll(tail8, k, 0)
    ridx = lax.broadcasted_iota(jnp.int32, tail8.shape, 0)
    head = jnp.where(ridx < k, ts, xs[0:SUBLANES])
    return jnp.concatenate([head, xs[SUBLANES:]], axis=0)


def _shift_up(x, head8, k):
    n = x.shape[0]
    xs = pltpu.roll(x, n - k, 0)
    hs = pltpu.roll(head8, SUBLANES - k, 0)
    ridx = lax.broadcasted_iota(jnp.int32, head8.shape, 0)
    last = jnp.where(ridx >= SUBLANES - k, hs, xs[n - SUBLANES:n])
    return jnp.concatenate([xs[:n - SUBLANES], last], axis=0)


def _scan_forward(a, b, carry):
    n, cols = a.shape
    groups = n // SUBLANES
    a = a.reshape(groups, SUBLANES, cols)
    b = b.reshape(groups, SUBLANES, cols)
    sub = lax.broadcasted_iota(jnp.int32, a.shape, 1)
    for s in (1, 2, 4):
        a_s = pltpu.roll(a, s, 1)
        b_s = pltpu.roll(b, s, 1)
        m = sub >= s
        b = jnp.where(m, a * b_s + b, b)
        a = jnp.where(m, a * a_s, a)
    out = []
    for g in range(groups):
        h = a[g] * carry + b[g]
        out.append(h)
        carry = h[SUBLANES - 1:SUBLANES]
    return jnp.concatenate(out, axis=0), carry


def _scan_backward(a, b, carry):
    n, cols = a.shape
    groups = n // SUBLANES
    a = a.reshape(groups, SUBLANES, cols)
    b = b.reshape(groups, SUBLANES, cols)
    sub = lax.broadcasted_iota(jnp.int32, a.shape, 1)
    for s in (1, 2, 4):
        a_s = pltpu.roll(a, SUBLANES - s, 1)
        b_s = pltpu.roll(b, SUBLANES - s, 1)
        m = sub < SUBLANES - s
        b = jnp.where(m, a * b_s + b, b)
        a = jnp.where(m, a * a_s, a)
    out = [None] * groups
    for g in reversed(range(groups)):
        h = a[g] * carry + b[g]
        out[g] = h
        carry = h[0:1]
    return jnp.concatenate(out, axis=0), carry


def _softplus_neg(lam):
    e = jnp.exp(-jnp.abs(lam))
    u = 1.0 + e
    log1p_e = jnp.where(u == 1.0, e, jnp.log(u) * (e / jnp.where(u == 1.0, 1.0, u - 1.0)))
    return jnp.maximum(-lam, 0.0) + log1p_e


def _lru_gates(xa, tail8, cw_ref, cb_ref, wr_ref, br_ref, wi_ref, bi_ref, lam_ref):
    cw = cw_ref[...]
    xc = cb_ref[...] + cw[0:1] * xa
    for k in range(1, CONV_WIDTH):
        xc = xc + cw[k:k + 1] * _shift_down(xa, tail8, k)
    xcb = xc.astype(BF16)
    pre_r, pre_i = [], []
    for h in range(HEADS):
        cols = slice(h * HEAD_DIM, (h + 1) * HEAD_DIM)
        pre_r.append(_dot(xcb[:, cols], wr_ref[h]))
        pre_i.append(_dot(xcb[:, cols], wi_ref[h]))
    r = jax.nn.sigmoid(jnp.concatenate(pre_r, axis=1) + br_ref[...])
    ig = jax.nn.sigmoid(jnp.concatenate(pre_i, axis=1) + bi_ref[...])
    _, a, mult, _ = _decay(r, lam_ref)
    return xc, r, ig, a, mult


def _decay(r, lam_ref):
    sp = _softplus_neg(lam_ref[...])
    log_a = ((-LRU_C) * sp) * r
    a = jnp.exp(log_a)
    th = jnp.tanh(log_a)
    q = (-2.0 * th) / (1.0 - th)
    inv = lax.rsqrt(q)
    return sp, a, jnp.where(q > 0.0, q * inv, 0.0), inv


class _Phase:
    def __init__(self, copies, n_sem, n_local, start=None, finish=None):
        self.copies, self.n_sem, self.n_local, self.start, self.finish = copies, n_sem, n_local, start, finish


class _Job:
    def __init__(self, inputs, out_shape, n_sem, copies, peers, aliases=None, n_local=0):
        self.inputs, self.out_shape = list(inputs), list(out_shape)
        self.aliases = dict(aliases or {})
        self.phases = [_Phase(copies, n_sem, n_local)]
        self.peers = tuple(peers)

    def then(self, make, at=None):
        nxt = make(self.out_shape)
        self.phases[-1].finish = at
        nxt.phases[0].start = at
        self.phases += nxt.phases
        self.peers = tuple(sorted(set(self.peers + nxt.peers)))
        return self


def _fused_call(body, jobs, *, name, grid, in_specs, out_specs, out_shape, scratch_shapes=(),
                input_output_aliases=None, compiler_params=None, n_prefetch=0, jobs_start_after=None):
    single = not isinstance(out_shape, (list, tuple))
    out_specs = [out_specs] if single else list(out_specs)
    out_shape = [out_shape] if single else list(out_shape)
    n_scr = len(scratch_shapes)
    in_specs, scratch_shapes = list(in_specs), list(scratch_shapes)
    n_in, n_out = len(in_specs), len(out_shape)
    aliases = dict(input_output_aliases or {})
    in_at, out_at, phases = [], [], []
    for q, job in enumerate(jobs):
        in_at.append(len(in_specs))
        out_at.append(len(out_shape))
        for i, o in job.aliases.items():
            aliases[n_prefetch + len(in_specs) + i] = len(out_shape) + o
        in_specs += [ANY] * len(job.inputs)
        out_specs += [ANY] * len(job.out_shape)
        out_shape += job.out_shape
        for k, phase in enumerate(job.phases):
            phases.append((q, k, phase, len(scratch_shapes)))
            scratch_shapes += [pltpu.SemaphoreType.DMA((phase.n_sem,)), pltpu.SemaphoreType.DMA((phase.n_sem,)),
                               pltpu.SemaphoreType.DMA((max(phase.n_local, 1),))]
    n_in_all, n_out_all = len(in_specs), len(out_shape)
    first_step, last_step = (0,) * len(grid), tuple(g - 1 for g in grid)

    def full_body(*refs):
        prefetch, refs = refs[:n_prefetch], refs[n_prefetch:]
        ins, outs, scr = refs[:n_in_all], refs[n_in_all:n_in_all + n_out_all], refs[n_in_all + n_out_all:]
        ids = [pl.program_id(a) for a in range(len(grid))]
        at_step = lambda step: functools.reduce(jnp.logical_and, [i == k for i, k in zip(ids, step)])

        def copies(q, k, phase, sem_at):
            job = jobs[q]
            mine = outs[out_at[q]:out_at[q] + len(job.out_shape)]
            return phase.copies(ins[in_at[q]:in_at[q] + len(job.inputs)] if k == 0 else mine, mine,
                                *scr[sem_at:sem_at + 3])

        def start(*phase):
            def go():
                sends, _, local = copies(*phase)
                for cp in local + sends:
                    cp.start()
            return go

        def finish(*phase):
            def go():
                sends, arrivals, local = copies(*phase)
                for cp in arrivals:
                    cp.wait_recv()
                for cp in sends:
                    cp.wait_send()
                for cp in local:
                    cp.wait()
            return go

        for phase in phases:
            if phase[2].start is None and jobs_start_after is None:
                pl.when(at_step(first_step))(start(*phase))
        body(*prefetch, *ins[:n_in], *outs[:n_out], *scr[:n_scr])
        for phase in phases:
            pl.when(at_step(phase[2].finish or last_step))(finish(*phase))
            nxt = phase[2].start or jobs_start_after
            if nxt is not None:
                pl.when(at_step(nxt))(start(*phase))

    if n_prefetch:
        layout = dict(grid_spec=pltpu.PrefetchScalarGridSpec(
            num_scalar_prefetch=n_prefetch, grid=grid, in_specs=in_specs, out_specs=out_specs,
            scratch_shapes=scratch_shapes))
    else:
        layout = dict(grid=grid, in_specs=in_specs, out_specs=out_specs, scratch_shapes=scratch_shapes)
    call = pl.pallas_call(
        full_body, name=name, out_shape=out_shape, input_output_aliases=aliases, compiler_params=compiler_params,
        **layout)

    def run(*args):
        res = call(*args, *[a for job in jobs for a in job.inputs])
        mine = res[0] if single else list(res[:n_out])
        return mine, [list(res[at:at + len(job.out_shape)]) for at, job in zip(out_at, jobs)]

    return run


def _fwd_in(x, g1, shards, order, jobs=()):
    t = x.shape[0]
    rows_per_step = min(IN_TILE, t)
    n_tiles = t // rows_per_step
    n = len(shards)
    halves = [s.shape[0] // 2 for s in shards]

    def body(order_ref, x_ref, g_ref, *refs):
        del order_ref
        ins, (z_ref, n_ref), outs = refs[:n], refs[n:n + 2], refs[n + 2:2 * n + 2]
        wbuf, nbuf, send, recv, local = refs[2 * n + 2:]
        s, i = pl.program_id(0), pl.program_id(1)
        x_, y_, c, chips = _place()
        near, far = _near_far(x_, y_, c)
        k_me = _chip_index(x_, y_)

        def block(w, chip, pc):
            return outs[w].at[_chip_index(*chip), pl.ds(pc * halves[w], halves[w]), :]

        def over_ici(w, j, landing):
            return pltpu.make_async_remote_copy(
                src_ref=ins[w].at[pl.ds(c * halves[w], halves[w]), :],
                dst_ref=block(w, chips[j] if landing else (x_, y_), c), send_sem=send.at[6 * w + j],
                recv_sem=recv.at[6 * w + j], device_id=(*chips[j], c), device_id_type=MESH)

        def onward(w, landing):
            blk = block(w, chips[2] if landing else near, c)
            return pltpu.make_async_remote_copy(
                src_ref=blk, dst_ref=blk, send_sem=send.at[6 * w + 2], recv_sem=recv.at[6 * w + 2],
                device_id=(*far, c), device_id_type=MESH)

        def to_sibling(w, j, landing):
            blk = block(w, chips[j], 1 - c if landing else c)
            return pltpu.make_async_remote_copy(
                src_ref=blk, dst_ref=blk, send_sem=send.at[6 * w + 3 + j], recv_sem=recv.at[6 * w + 3 + j],
                device_id=(x_, y_, 1 - c), device_id_type=MESH)

        own = [pltpu.make_async_copy(wbuf, outs[0].at[k_me], local.at[0])]
        own += [pltpu.make_async_copy(ins[w], outs[w].at[k_me], local.at[w]) for w in range(1, n)]

        @pl.when((s == 0) & (i == 0))
        def _():
            for j in range(2):
                for w in range(n):
                    over_ici(w, j, False).start()
            load = pltpu.make_async_copy(ins[0], wbuf, local.at[n])
            load.start()
            load.wait()
            for cp in own:
                cp.start()

        for j in range(N_CHIPS - 1):
            @pl.when((s == j + 1) & (i == 0))
            def _(j=j):
                if j == 0:
                    for k in range(2):
                        for w in range(n):
                            over_ici(w, k, True).wait_recv()
                    for w in range(n):
                        onward(w, False).start()
                    for k in range(2):
                        for w in range(n):
                            to_sibling(w, k, False).start()
                    own[0].wait()
                if j == 2:
                    for w in range(n):
                        onward(w, True).wait_recv()
                    for w in range(n):
                        to_sibling(w, j, False).start()
                for w in range(n):
                    to_sibling(w, j, True).wait_recv()
                load = pltpu.make_async_copy(outs[0].at[_chip_index(*chips[j])], wbuf, local.at[n])
                load.start()
                load.wait()

        rows = pl.ds(pl.multiple_of(i * rows_per_step, rows_per_step), rows_per_step)

        @pl.when(s == 0)
        def _():
            xhat, _ = _rms(x_ref[...])
            nrm = (xhat * g_ref[...]).astype(BF16)
            nbuf[rows, :] = nrm
            n_ref[...] = nrm

        z_ref[...] = _dot(nbuf[rows, :], wbuf[...]).astype(BF16)

        @pl.when((s == N_CHIPS - 1) & (i == n_tiles - 1))
        def _():
            for j in range(N_CHIPS - 1):
                for w in range(n):
                    (over_ici(w, j, False) if j < 2 else onward(w, False)).wait_send()
                    to_sibling(w, j, False).wait_send()
            for cp in own[1:]:
                cp.wait()

    once = lambda s, i, order: (jnp.where(s == 0, i, n_tiles - 1), 0)
    (z, n1, *stacked), job_outs = _fused_call(
        body, jobs, name="fwd_in", grid=(N_CHIPS, n_tiles), n_prefetch=1,
        in_specs=[pl.BlockSpec((rows_per_step, D_MODEL), once), _const((1, D_MODEL))] + [ANY] * n,
        out_specs=[pl.BlockSpec((rows_per_step, IN_SHARD), lambda s, i, order: (i, order[s])),
                   pl.BlockSpec((rows_per_step, D_MODEL), once)] + [ANY] * n,
        out_shape=[jax.ShapeDtypeStruct((t, D_IN), BF16), jax.ShapeDtypeStruct((t, D_MODEL), BF16)]
        + [jax.ShapeDtypeStruct((N_CHIPS,) + s.shape, s.dtype) for s in shards],
        scratch_shapes=[pltpu.VMEM(shards[0].shape, BF16), pltpu.VMEM((t, D_MODEL), BF16),
                        pltpu.SemaphoreType.DMA((6 * n,)),
                        pltpu.SemaphoreType.DMA((6 * n,)), pltpu.SemaphoreType.DMA((n + 1,))],
        compiler_params=_params(2), jobs_start_after=(1, 0),
    )(order, x, g1, *shards)
    return (z, n1, stacked), job_outs


def _fwd_lru(z, conv_w, conv_b, wr, br, wi, bi, lam, jobs=()):
    t = z.shape[0]

    def body(xa_ref, ga_ref, cw_ref, cb_ref, wr_ref, br_ref, wi_ref, bi_ref, lam_ref, ya_ref, h_ref, xc_ref, r_ref,
             ig_ref, tail_ref, carry_ref):
        @pl.when(pl.program_id(0) == 0)
        def _():
            tail_ref[...] = jnp.zeros_like(tail_ref)
            carry_ref[...] = jnp.zeros_like(carry_ref)

        xa = xa_ref[...].astype(F32)
        xc, r, ig, a, mult = _lru_gates(xa, tail_ref[...], cw_ref, cb_ref, wr_ref, br_ref, wi_ref, bi_ref, lam_ref)
        tail_ref[...] = xa[SEQ_TILE - SUBLANES:]
        xc_ref[...], r_ref[...], ig_ref[...] = xc, r, ig
        h, carry = _scan_forward(a, xc * ig * mult, carry_ref[...])
        carry_ref[...] = carry
        h_ref[...] = h
        ya_ref[...] = (h * _gelu(ga_ref[...].astype(F32))).astype(BF16)

    tile = lambda j: pl.BlockSpec((SEQ_TILE, D_MODEL), lambda i: (i, j))
    return _fused_call(
        body, jobs, name="fwd_lru", grid=(t // SEQ_TILE,),
        in_specs=[tile(0), tile(1), _const((CONV_WIDTH, D_MODEL)), _const((1, D_MODEL)),
                  _resident((HEADS, HEAD_DIM, HEAD_DIM)), _const((1, D_MODEL)),
                  _resident((HEADS, HEAD_DIM, HEAD_DIM)), _const((1, D_MODEL)), _const((1, D_MODEL))],
        out_specs=[tile(0)] * 5,
        out_shape=[jax.ShapeDtypeStruct((t, D_MODEL), BF16)] + [jax.ShapeDtypeStruct((t, D_MODEL), F32)] * 4,
        scratch_shapes=[pltpu.VMEM((SUBLANES, D_MODEL), F32), pltpu.VMEM((1, D_MODEL), F32)],
        compiler_params=_params(),
    )(z, z, conv_w, conv_b, wr, br, wi, bi, lam)


def _sgu_forward_parts(ub, vb, lg_ref, lb_ref):
    u, du = _gelu_and_grad(ub.astype(F32))
    vg, dvg = _gelu_and_grad(vb.astype(F32))
    mu = jnp.mean(vg, axis=-1, keepdims=True)
    d = vg - mu
    rstd = lax.rsqrt(jnp.mean(d * d, axis=-1, keepdims=True) + LN_EPS)
    vhat = d * rstd
    vn = (vhat * lg_ref[...] + lb_ref[...]).astype(BF16)
    return u, du, dvg, rstd, vhat, vn


def _causal_mask():
    rows = lax.broadcasted_iota(jnp.int32, (CHUNK, CHUNK), 0)
    cols = lax.broadcasted_iota(jnp.int32, (CHUNK, CHUNK), 1)
    return rows >= cols


def _fwd_sgu_merge(ya, z, x, ln_g, ln_b, w_s, bias_full, w_oa, w_ob, w_out, g2, jobs=()):
    t = x.shape[0]

    def body(ya_ref, ub_ref, vb_ref, m_ref, x_ref, lg_ref, lb_ref, ws_ref, bias_ref, woa_ref, wob_ref, wout_ref, g_ref,
             yb_ref, pa_ref, pb_ref, h1_ref, n2_ref):
        u, _, _, _, _, vn = _sgu_forward_parts(ub_ref[...], vb_ref[...], lg_ref, lb_ref)
        mask = _causal_mask()
        wm = [jnp.where(mask, ws_ref[g], 0.0).astype(BF16) for g in range(GROUPS)]
        for c in range(SEQ_TILE // CHUNK):
            rows = slice(c * CHUNK, (c + 1) * CHUNK)
            for g in range(GROUPS):
                cols = slice(g * GROUP_DIM, (g + 1) * GROUP_DIM)
                sp = _dot(wm[g], vn[rows, cols]) + bias_ref[:, cols]
                yb_ref[rows, cols] = (u[rows, cols] * sp).astype(BF16)
        pa = _dot(ya_ref[...], woa_ref[...])
        pb = _dot(yb_ref[...], wob_ref[...])
        pa_ref[...] = pa
        pb_ref[...] = pb
        merged = _gate(m_ref[:, :D_MODEL]) * pa + _gate(m_ref[:, D_MODEL:]) * pb
        h1 = x_ref[...] + _dot(merged.astype(BF16), wout_ref[...])
        h1_ref[...] = h1
        xhat, _ = _rms(h1)
        n2_ref[...] = (xhat * g_ref[...]).astype(BF16)

    tile = lambda j: pl.BlockSpec((SEQ_TILE, D_MODEL), lambda i: (i, j))
    sq = _resident((D_MODEL, D_MODEL))
    vec = _const((1, D_MODEL))
    bf, f32 = jax.ShapeDtypeStruct((t, D_MODEL), BF16), jax.ShapeDtypeStruct((t, D_MODEL), F32)
    return _fused_call(
        body, jobs, name="fwd_sgu_merge", grid=(t // SEQ_TILE,),
        in_specs=[tile(0), tile(2), tile(3), pl.BlockSpec((SEQ_TILE, 2 * D_MODEL), lambda i: (i, 2)), tile(0), vec, vec,
                  _const((GROUPS, CHUNK, CHUNK)), _const((CHUNK, D_MODEL)), sq, sq, sq, vec],
        out_specs=[tile(0)] * 5,
        out_shape=[bf, f32, f32, f32, bf],
        compiler_params=_params(),
    )(ya, z, z, z, x, ln_g, ln_b, w_s, bias_full, w_oa, w_ob, w_out, g2)


def _mlp(n2, h1, target, w_up_st, w_down, g2, g3, jobs=()):
    t = n2.shape[0]

    def body(n2_ref, h1_ref, tgt_ref, wup_ref, wdown_ref, g2_ref, g3_ref, act_ref, dup_ref, dh2b_ref, dh1_ref,
             loss_ref, dg3_ref, dg2_ref, relu_ref):
        @pl.when(pl.program_id(0) == 0)
        def _():
            for ref in (loss_ref, dg3_ref, dg2_ref):
                ref[...] = jnp.zeros_like(ref)

        n2 = n2_ref[...]
        h1 = h1_ref[...]
        h2 = h1
        for k in range(N_CHIPS):
            cols = slice(k * D_MODEL, (k + 1) * D_MODEL)
            r = jnp.maximum(_dot(n2, wup_ref[k]), 0.0)
            relu_ref[:, cols] = r
            act = (r * r).astype(BF16)
            act_ref[:, cols] = act
            h2 = h2 + _dot(act, wdown_ref[cols, :])
        xhat, r3 = _rms(h2)
        diff = xhat * g3_ref[...] - tgt_ref[...]
        sq = jnp.sum(diff * diff, axis=1, keepdims=True)
        loss_ref[...] = loss_ref[...] + (0.5 / D_MODEL) * jnp.sum(sq, axis=0, keepdims=True)
        dy = diff * (1.0 / D_MODEL)
        dg3_ref[...] = dg3_ref[...] + _col_sum(dy * xhat)
        dh2 = _rms_bwd(dy * g3_ref[...], xhat, r3)
        dh2b = dh2.astype(BF16)
        dh2b_ref[...] = dh2b
        dn2 = jnp.zeros((SEQ_TILE, D_MODEL), F32)
        for k in range(N_CHIPS):
            cols = slice(k * D_MODEL, (k + 1) * D_MODEL)
            dup = (_dot_nt(dh2b, wdown_ref[cols, :]) * (2.0 * relu_ref[:, cols])).astype(BF16)
            dup_ref[:, cols] = dup
            dn2 = dn2 + _dot_nt(dup, wup_ref[k])
        xhat, r2 = _rms(h1)
        dg2_ref[...] = dg2_ref[...] + _col_sum(dn2 * xhat)
        dh1_ref[...] = dh2 + _rms_bwd(dn2 * g2_ref[...], xhat, r2)

    tile = pl.BlockSpec((SEQ_TILE, D_MODEL), lambda i: (i, 0))
    wide = pl.BlockSpec((SEQ_TILE, D_FF), lambda i: (i, 0))
    vec = _const((1, D_MODEL))
    vec_shape = jax.ShapeDtypeStruct((1, D_MODEL), F32)
    return _fused_call(
        body, jobs, name="mlp", grid=(t // SEQ_TILE,),
        in_specs=[tile, tile, tile, _resident((N_CHIPS, D_MODEL, D_MODEL)), _resident((D_FF, D_MODEL)), vec, vec],
        out_specs=[wide, wide, tile, tile, _const((SUBLANES, 128)), vec, vec],
        out_shape=[jax.ShapeDtypeStruct((t, D_FF), BF16), jax.ShapeDtypeStruct((t, D_FF), BF16),
                   jax.ShapeDtypeStruct((t, D_MODEL), BF16), jax.ShapeDtypeStruct((t, D_MODEL), F32),
                   jax.ShapeDtypeStruct((SUBLANES, 128), F32), vec_shape, vec_shape],
        scratch_shapes=[pltpu.VMEM((SEQ_TILE, D_FF), F32)],
        compiler_params=_params(),
    )(n2, h1, target, w_up_st, w_down, g2, g3)


def _bwd_mix(dh1, pa, pb, z, h, xc, r, ig, w_oa, w_ob, w_out, ln_g, ln_b, w_s, bias_full, conv_w, wr, wi, lam, jobs=()):
    t = dh1.shape[0]
    n_tiles = t // SEQ_TILE
    per_tile = SEQ_TILE // SUBLANES

    def merge_part(dh1_ref, pa_ref, pb_ref, m_ref, woa_ref, wob_ref, wout_ref, dz_ref, dya_ref, dyb_ref, mg_ref,
                   dpa_ref, dpb_ref, dh1b_ref):
        dh1b = dh1_ref[...].astype(BF16)
        dh1b_ref[...] = dh1b
        dm = _dot_nt(dh1b, wout_ref[...])
        pa = pa_ref[...]
        pb = pb_ref[...]
        sa = _gate(m_ref[:, :D_MODEL])
        sb = _gate(m_ref[:, D_MODEL:])
        mg_ref[...] = (sa * pa + sb * pb).astype(BF16)
        dpa = dm * sa
        dpb = dm * sb
        dz_ref[:, :D_MODEL] = ((dpa * pa) * (1.0 - sa)).astype(BF16)
        dz_ref[:, D_MODEL:] = ((dpb * pb) * (1.0 - sb)).astype(BF16)
        dpa = dpa.astype(BF16)
        dpb = dpb.astype(BF16)
        dpa_ref[...] = dpa
        dpb_ref[...] = dpb
        dya_ref[...] = _dot_nt(dpa, woa_ref[...])
        dyb_ref[...] = _dot_nt(dpb, wob_ref[...])

    def sgu_part(dyb_ref, ub_ref, vb_ref, lg_ref, lb_ref, ws_ref, bias_ref, dz_ref, dlg_ref, dlb_ref, dws_ref, dbs_ref,
                 dvn_ref, dsp_acc):
        i = pl.program_id(0)

        @pl.when(i == 0)
        def _():
            dlg_ref[...] = jnp.zeros_like(dlg_ref)
            dlb_ref[...] = jnp.zeros_like(dlb_ref)
            dws_ref[...] = jnp.zeros_like(dws_ref)
            dsp_acc[...] = jnp.zeros_like(dsp_acc)

        u, du, dvg, rstd, vhat, vn = _sgu_forward_parts(ub_ref[...], vb_ref[...], lg_ref, lb_ref)
        dyb = dyb_ref[...]
        mask = _causal_mask()
        wm = [jnp.where(mask, ws_ref[g], 0.0).astype(BF16) for g in range(GROUPS)]
        for c in range(SEQ_TILE // CHUNK):
            rows = slice(c * CHUNK, (c + 1) * CHUNK)
            for g in range(GROUPS):
                cols = slice(g * GROUP_DIM, (g + 1) * GROUP_DIM)
                vn_blk = vn[rows, cols]
                sp = _dot(wm[g], vn_blk) + bias_ref[:, cols]
                dyb_blk = dyb[rows, cols]
                dz_ref[rows, cols] = (dyb_blk * sp * du[rows, cols]).astype(BF16)
                dsp = dyb_blk * u[rows, cols]
                dsp_acc[:, cols] = dsp_acc[:, cols] + dsp
                dspb = dsp.astype(BF16)
                dvn_ref[rows, cols] = _dot_tn(wm[g], dspb)
                wcols = slice(g * CHUNK, (g + 1) * CHUNK)
                dws_ref[:, wcols] = dws_ref[:, wcols] + jnp.where(mask, _dot_nt(dspb, vn_blk), 0.0)
        dvn = dvn_ref[...]
        dlg_ref[...] = dlg_ref[...] + _col_sum(dvn * vhat)
        dlb_ref[...] = dlb_ref[...] + _col_sum(dvn)
        dvhat = dvn * lg_ref[...]
        dvgel = rstd * (dvhat - jnp.mean(dvhat, axis=-1, keepdims=True)
                        - vhat * jnp.mean(dvhat * vhat, axis=-1, keepdims=True))
        dz_ref[:, D_MODEL:] = (dvgel * dvg).astype(BF16)

        @pl.when(i == n_tiles - 1)
        def _():
            lane = lax.broadcasted_iota(jnp.int32, (CHUNK, 128), 1)
            out = jnp.zeros((CHUNK, 128), F32)
            for g in range(GROUPS):
                s = jnp.sum(dsp_acc[:, g * GROUP_DIM:(g + 1) * GROUP_DIM], axis=1, keepdims=True)
                out = out + jnp.where(lane == g, s, 0.0)
            dbs_ref[...] = out

    def lru_part(dya_ref, xa_ref, ga_ref, h_ref, h_prev_ref, xc_ref, r_ref, ig_ref, cw_ref, wr_ref, wi_ref, lam_ref,
                 dz_ref, dcw_ref, dcb_ref, dwr_ref, dbr_ref, dwi_ref, dbi_ref, dlam_ref, lam_carry, dxc_head):
        i = pl.program_id(0)

        @pl.when(i == 0)
        def _():
            for ref in (dcw_ref, dcb_ref, dwr_ref, dbr_ref, dwi_ref, dbi_ref, dlam_ref, lam_carry, dxc_head):
                ref[...] = jnp.zeros_like(ref)

        first_tile = i == n_tiles - 1
        h_tail = jnp.where(first_tile, 0.0, h_prev_ref[...])
        xc, r, ig = xc_ref[...], r_ref[...], ig_ref[...]
        xcb = xc.astype(BF16)
        sp, a, mult, inv_mult = _decay(r, lam_ref)
        h = h_ref[...]
        h_prev = _shift_down(h, h_tail, 1)
        dya = dya_ref[...]
        gg, dgg = _gelu_and_grad(ga_ref[...].astype(F32))
        dz_ref[:, D_MODEL:] = (dya * h * dgg).astype(BF16)
        ones = jnp.ones((SUBLANES, D_MODEL), F32)
        lam_t, lam_first = _scan_backward(_shift_up(a, ones, 1), dya * gg, lam_carry[...])
        lam_carry[...] = a[0:1] * lam_first
        lam_ig = lam_t * ig
        dxc_direct = lam_ig * mult
        dmult = lam_ig * xc
        dla = a * (lam_t * h_prev - (dmult * a) * inv_mult)
        dla_r = dla * r
        dlam_ref[...] = dlam_ref[...] + _col_sum(dla_r) * (LRU_C * jax.nn.sigmoid(-lam_ref[...]))
        dpr = (dla_r * ((-LRU_C) * sp)) * (1.0 - r)
        dpi = (dxc_direct * xc) * (1.0 - ig)
        dbr_ref[...] = dbr_ref[...] + _col_sum(dpr)
        dbi_ref[...] = dbi_ref[...] + _col_sum(dpi)
        dprb = dpr.astype(BF16)
        dpib = dpi.astype(BF16)
        dxc_gate = []
        for hd in range(HEADS):
            cols = slice(hd * HEAD_DIM, (hd + 1) * HEAD_DIM)
            dxc_gate.append(_dot_nt(dprb[:, cols], wr_ref[hd]) + _dot_nt(dpib[:, cols], wi_ref[hd]))
            dwr_ref[hd] = dwr_ref[hd] + _dot_tn(xcb[:, cols], dprb[:, cols])
            dwi_ref[hd] = dwi_ref[hd] + _dot_tn(xcb[:, cols], dpib[:, cols])
        dxc = dxc_direct + jnp.concatenate(dxc_gate, axis=1)
        dcb_ref[...] = dcb_ref[...] + _col_sum(dxc)
        cw = cw_ref[...]
        head = dxc_head[...]
        xa = xa_ref[...].astype(F32)
        dxa = cw[0:1] * dxc
        dcw_ref[0:1, :] = dcw_ref[0:1, :] + _col_sum(dxc * xa)
        for k in range(1, CONV_WIDTH):
            dxc_k = _shift_up(dxc, head, k)
            dxa = dxa + cw[k:k + 1] * dxc_k
            dcw_ref[k:k + 1, :] = dcw_ref[k:k + 1, :] + _col_sum(dxc_k * xa)
        dxc_head[...] = dxc[0:SUBLANES]
        dz_ref[:, :D_MODEL] = dxa.astype(BF16)

    def body(dh1_ref, pa_ref, pb_ref, z_ref, h_ref, h_prev_ref, xc_ref, r_ref, ig_ref, woa_ref, wob_ref, wout_ref,
             lg_ref, lb_ref, ws_ref, bias_ref, cw_ref, wr_ref, wi_ref, lam_ref, dz_ref, mg_ref, dpa_ref, dpb_ref,
             dh1b_ref, dlg_ref, dlb_ref, dws_ref, dbs_ref, dcw_ref, dcb_ref, dwr_ref, dbr_ref, dwi_ref, dbi_ref,
             dlam_ref, dya_ref, dyb_ref, dvn_ref, dsp_acc, lam_carry, dxc_head):
        def cols(ref, first, count):
            return ref.at[:, pl.ds(first * D_MODEL, count * D_MODEL)]

        merge_part(dh1_ref, pa_ref, pb_ref, cols(z_ref, 4, 2), woa_ref, wob_ref, wout_ref, cols(dz_ref, 4, 2), dya_ref,
                   dyb_ref, mg_ref, dpa_ref, dpb_ref, dh1b_ref)
        sgu_part(dyb_ref, cols(z_ref, 2, 1), cols(z_ref, 3, 1), lg_ref, lb_ref, ws_ref, bias_ref, cols(dz_ref, 2, 2),
                 dlg_ref, dlb_ref, dws_ref, dbs_ref, dvn_ref, dsp_acc)
        lru_part(dya_ref, cols(z_ref, 0, 1), cols(z_ref, 1, 1), h_ref, h_prev_ref, xc_ref, r_ref, ig_ref, cw_ref, wr_ref,
                 wi_ref, lam_ref, cols(dz_ref, 0, 2), dcw_ref, dcb_ref, dwr_ref, dbr_ref, dwi_ref, dbi_ref, dlam_ref,
                 lam_carry, dxc_head)

    rev = lambda i: n_tiles - 1 - i
    tile = pl.BlockSpec((SEQ_TILE, D_MODEL), lambda i: (rev(i), 0))
    row = pl.BlockSpec((SEQ_TILE, D_IN), lambda i: (rev(i), 0))
    prev8 = pl.BlockSpec((SUBLANES, D_MODEL), lambda i: (jnp.maximum(rev(i) * per_tile - 1, 0), 0))
    vec = _const((1, D_MODEL))
    sq = _resident((D_MODEL, D_MODEL))
    gate_w = _resident((HEADS, HEAD_DIM, HEAD_DIM))
    gate_acc = _const((HEADS, HEAD_DIM, HEAD_DIM))
    vec_shape = jax.ShapeDtypeStruct((1, D_MODEL), F32)
    gate_shape = jax.ShapeDtypeStruct((HEADS, HEAD_DIM, HEAD_DIM), F32)
    act_bf = jax.ShapeDtypeStruct((t, D_MODEL), BF16)
    return _fused_call(
        body, jobs, name="bwd_mix", grid=(n_tiles,),
        in_specs=[tile, tile, tile, row, tile, prev8, tile, tile, tile, sq, sq, sq, vec, vec,
                  _const((GROUPS, CHUNK, CHUNK)), _const((CHUNK, D_MODEL)), _const((CONV_WIDTH, D_MODEL)), gate_w, gate_w,
                  vec],
        out_specs=[row, tile, tile, tile, tile, vec, vec, _const((CHUNK, GROUPS * CHUNK)), _const((CHUNK, 128)),
                   _const((SUBLANES, D_MODEL)), vec, gate_acc, vec, gate_acc, vec, vec],
        out_shape=[jax.ShapeDtypeStruct((t, D_IN), BF16), act_bf, act_bf, act_bf, act_bf, vec_shape, vec_shape,
                   jax.ShapeDtypeStruct((CHUNK, GROUPS * CHUNK), F32), jax.ShapeDtypeStruct((CHUNK, 128), F32),
                   jax.ShapeDtypeStruct((SUBLANES, D_MODEL), F32), vec_shape, gate_shape, vec_shape, gate_shape,
                   vec_shape, vec_shape],
        scratch_shapes=[pltpu.VMEM((SEQ_TILE, D_MODEL), F32), pltpu.VMEM((SEQ_TILE, D_MODEL), F32),
                        pltpu.VMEM((SEQ_TILE, D_MODEL), F32), pltpu.VMEM((CHUNK, D_MODEL), F32),
                        pltpu.VMEM((1, D_MODEL), F32), pltpu.VMEM((SUBLANES, D_MODEL), F32)],
        compiler_params=_params(),
    )(dh1, pa, pb, z, h, h, xc, r, ig, w_oa, w_ob, w_out, ln_g, ln_b, w_s, bias_full, conv_w, wr, wi, lam)


def _bwd_in(dz, x, dh1, w_in_st, g1, jobs=()):
    t = x.shape[0]

    def body(dz_ref, x_ref, dh1_ref, w_ref, g_ref, dx_ref, dg1_ref):
        @pl.when(pl.program_id(0) == 0)
        def _():
            dg1_ref[...] = jnp.zeros_like(dg1_ref)

        dn1 = jnp.zeros((MM_TILE, D_MODEL), F32)
        for k in range(N_CHIPS):
            dn1 = dn1 + _dot_nt(dz_ref[:, k * IN_SHARD:(k + 1) * IN_SHARD], w_ref[k])
        xhat, r1 = _rms(x_ref[...])
        dg1_ref[...] = dg1_ref[...] + _col_sum(dn1 * xhat)
        dx_ref[...] = dh1_ref[...] + _rms_bwd(dn1 * g_ref[...], xhat, r1)

    tile = pl.BlockSpec((MM_TILE, D_MODEL), lambda i: (i, 0))
    return _fused_call(
        body, jobs, name="bwd_in", grid=(t // MM_TILE,),
        in_specs=[pl.BlockSpec((MM_TILE, D_IN), lambda i: (i, 0)), tile, tile,
                  _resident((N_CHIPS, D_MODEL, IN_SHARD)), _const((1, D_MODEL))],
        out_specs=[tile, _const((1, D_MODEL))],
        out_shape=[jax.ShapeDtypeStruct((t, D_MODEL), F32), jax.ShapeDtypeStruct((1, D_MODEL), F32)],
        compiler_params=_params(),
    )(dz, x, dh1, w_in_st, g1)


def _weight_grad(name, a, b, n_blocks, a_varies, b_varies, width, jobs=()):
    t = a.shape[0]
    rows = min(DW_TILE, t)
    n_t = t // rows

    def body(a_ref, b_ref, o_ref, acc_ref):
        s = pl.program_id(1)
        part = _dot_tn(a_ref[...], b_ref[...])

        @pl.when(s == 0)
        def _():
            acc_ref[...] = part

        @pl.when(s > 0)
        def _():
            acc_ref[...] = acc_ref[...] + part

        @pl.when(s == n_t - 1)
        def _():
            o_ref[...] = acc_ref[...].astype(BF16)

    return _fused_call(
        body, jobs, name=name, grid=(n_blocks, n_t),
        in_specs=[pl.BlockSpec((rows, D_MODEL), (lambda j, s: (s, j)) if a_varies else (lambda j, s: (s, 0))),
                  pl.BlockSpec((rows, width), (lambda j, s: (s, j)) if b_varies else (lambda j, s: (s, 0)))],
        out_specs=pl.BlockSpec((None, D_MODEL, width), lambda j, s: (j, 0, 0)),
        out_shape=jax.ShapeDtypeStruct((n_blocks, D_MODEL, width), BF16),
        scratch_shapes=[pltpu.VMEM((D_MODEL, width), F32)],
        compiler_params=_params(2),
    )(a, b)


def _weight_grads_square(name, pairs, jobs=()):
    n = len(pairs)
    t = pairs[0][0].shape[0]
    rows = min(2 * MM_TILE, t)
    n_t = t // rows

    def body(*refs):
        ins, outs, accs = refs[:2 * n], refs[2 * n:3 * n], refs[3 * n:]
        s = pl.program_id(0)
        for k in range(n):
            part = _dot_tn(ins[2 * k][...], ins[2 * k + 1][...])

            @pl.when(s == 0)
            def _(k=k, part=part):
                accs[k][...] = part

            @pl.when(s > 0)
            def _(k=k, part=part):
                accs[k][...] = accs[k][...] + part

            @pl.when(s == n_t - 1)
            def _(k=k):
                outs[k][...] = accs[k][...].astype(BF16)

    tile = pl.BlockSpec((rows, D_MODEL), lambda s: (s, 0))
    return _fused_call(
        body, jobs, name=name, grid=(n_t,), in_specs=[tile] * (2 * n), out_specs=[_const((D_MODEL, D_MODEL))] * n,
        out_shape=[jax.ShapeDtypeStruct((D_MODEL, D_MODEL), BF16)] * n,
        scratch_shapes=[pltpu.VMEM((D_MODEL, D_MODEL), F32)] * n,
        compiler_params=_params(),
    )(*[x for pair in pairs for x in pair])


def _place():
    x, y, c = lax.axis_index("x"), lax.axis_index("y"), lax.axis_index("c")
    other_chips = [(1 - x, y), (x, 1 - y), (1 - x, 1 - y)]
    return x, y, c, other_chips


def _chip_index(px, py):
    return 2 * px + py


ANY = pl.BlockSpec(memory_space=pl.ANY)
SIBLING = ((0, 0, 1),)
NEIGHBOURS = ((1, 0, 0), (0, 1, 0))
OTHER_CHIPS = NEIGHBOURS + ((1, 1, 0),)


def _near_far(x, y, c):
    return (x ^ (1 - c), y ^ c), (x ^ c, y ^ (1 - c))


def _gather_near_job(shards):
    n = len(shards)
    halves = [s.shape[0] // 2 for s in shards]

    def copies(ins, outs, send, recv, local):
        x, y, c, _ = _place()
        near, _ = _near_far(x, y, c)

        def block(w, chip, pc):
            return outs[w].at[_chip_index(*chip), pl.ds(pc * halves[w], halves[w]), :]

        def copy(w, k, chip, pc, to, src=None):
            return pltpu.make_async_remote_copy(
                src_ref=block(w, chip, pc) if src is None else src, dst_ref=block(w, chip, pc),
                send_sem=send.at[2 * w + k], recv_sem=recv.at[2 * w + k], device_id=to, device_id_type=MESH)

        sends, arrivals, own = [], [], []
        for w in range(n):
            src = ins[w].at[pl.ds(c * halves[w], halves[w]), :]
            own.append(pltpu.make_async_copy(src, block(w, (x, y), c), local.at[w]))
            sends += [copy(w, 0, (x, y), c, (*near, c), src), copy(w, 1, (x, y), c, (x, y, 1 - c), src)]
            arrivals += [copy(w, 0, near, c, (x, y, c)), copy(w, 1, (x, y), 1 - c, (x, y, c))]
        return sends, arrivals, own

    return _Job(shards, [jax.ShapeDtypeStruct((N_CHIPS,) + s.shape, s.dtype) for s in shards], 2 * n, copies,
                NEIGHBOURS + SIBLING, n_local=n)


def _gather_far_job(stacked):
    n = len(stacked)
    halves = [s.shape[1] // 2 for s in stacked]

    def copies(ins, outs, send, recv, local):
        del ins, local
        x, y, c, _ = _place()
        near, far = _near_far(x, y, c)

        def copy(w, k, chip):
            blk = outs[w].at[_chip_index(*chip), pl.ds(c * halves[w], halves[w]), :]
            return pltpu.make_async_remote_copy(
                src_ref=blk, dst_ref=blk, send_sem=send.at[2 * w + k], recv_sem=recv.at[2 * w + k],
                device_id=(*far, c), device_id_type=MESH)

        sends = [copy(w, k, chip) for w in range(n) for k, chip in enumerate(((x, y), near))]
        arrivals = [copy(w, k, chip) for w in range(n) for k, chip in enumerate((far, (1 - x, 1 - y)))]
        return sends, arrivals, []

    return _Job(stacked, [jax.ShapeDtypeStruct(s.shape, s.dtype) for s in stacked], 2 * n, copies, NEIGHBOURS,
                aliases={w: w for w in range(n)})


def _gather_pass_job(stacked):
    n = len(stacked)
    halves = [s.shape[1] // 2 for s in stacked]

    def copies(ins, outs, send, recv, local):
        del ins, local
        x, y, c, chips = _place()

        def copy(w, j, chip, pc, to):
            blk = outs[w].at[_chip_index(*chip), pl.ds(pc * halves[w], halves[w]), :]
            return pltpu.make_async_remote_copy(
                src_ref=blk, dst_ref=blk, send_sem=send.at[3 * w + j], recv_sem=recv.at[3 * w + j], device_id=to,
                device_id_type=MESH)

        sends = [copy(w, j, chip, c, (x, y, 1 - c)) for w in range(n) for j, chip in enumerate(chips)]
        arrivals = [copy(w, j, chip, 1 - c, (x, y, c)) for w in range(n) for j, chip in enumerate(chips)]
        return sends, arrivals, []

    return _Job(stacked, [jax.ShapeDtypeStruct(s.shape, s.dtype) for s in stacked], 3 * n, copies, SIBLING,
                aliases={w: w for w in range(n)})


def _gather_small_job(block):
    def copies(ins, outs, send, recv, local):
        x, y, c, chips = _place()

        def copy(j, chip_from, to):
            return pltpu.make_async_remote_copy(
                src_ref=ins[0], dst_ref=outs[0].at[_chip_index(*chip_from)], send_sem=send.at[j],
                recv_sem=recv.at[j], device_id=to, device_id_type=MESH)

        own = [pltpu.make_async_copy(ins[0], outs[0].at[_chip_index(x, y)], local.at[0])]
        sends = [copy(j, (x, y), (*chip, c)) for j, chip in enumerate(chips)]
        arrivals = [copy(j, chip, (x, y, c)) for j, chip in enumerate(chips)]
        return sends, arrivals, own

    return _Job([block], [jax.ShapeDtypeStruct((N_CHIPS,) + block.shape, block.dtype)], 3, copies, OTHER_CHIPS,
                n_local=1)


def _pair_send_job(grads):
    n = len(grads)
    halves = [g.shape[1] // 2 for g in grads]

    def copies(ins, outs, send, recv, local):
        del local
        x, y, c, _ = _place()
        sends = [pltpu.make_async_remote_copy(
            src_ref=ins[w].at[:, pl.ds((1 - c) * halves[w], halves[w]), :], dst_ref=outs[w], send_sem=send.at[w],
            recv_sem=recv.at[w], device_id=(x, y, 1 - c), device_id_type=MESH) for w in range(n)]
        return sends, sends, []

    return _Job(grads, [jax.ShapeDtypeStruct((N_CHIPS, h, g.shape[2]), g.dtype) for g, h in zip(grads, halves)], n,
                copies, SIBLING)


ROW_STEPS = 4


def _pair_add(name, core, mine, theirs):
    n = len(mine)

    def body(core_ref, *refs):
        del core_ref
        for a_ref, b_ref, o_ref in zip(refs[:n], refs[n:2 * n], refs[2 * n:]):
            o_ref[...] = (a_ref[...].astype(F32) + b_ref[...].astype(F32)).astype(BF16)

    half = lambda a: pl.BlockSpec((None, None) + a.shape[2:], lambda k, core_ref: (k, core_ref[0], 0, 0))
    block = lambda b: pl.BlockSpec((None,) + b.shape[1:], lambda k, core_ref: (k, 0, 0))
    return pl.pallas_call(
        body, name=name,
        grid_spec=pltpu.PrefetchScalarGridSpec(
            num_scalar_prefetch=1, grid=(N_CHIPS,),
            in_specs=[half(a) for a in mine] + [block(b) for b in theirs], out_specs=[block(b) for b in theirs]),
        out_shape=[jax.ShapeDtypeStruct(b.shape, BF16) for b in theirs],
        compiler_params=_params(),
    )(core, *mine, *theirs)


def _sequencer_call(name, collective_id, job):
    steps, peers = job.phases, job.peers
    ins = [jax.new_ref(a, memory_space=pltpu.MemorySpace.HBM) for a in job.inputs]
    outs = [ins[{o: i for i, o in job.aliases.items()}[k]] if k in job.aliases.values()
            else jax.empty_ref(shape, memory_space=pltpu.MemorySpace.HBM) for k, shape in enumerate(job.out_shape)]
    sems = [pltpu.SemaphoreType.DMA((n,)) for step in steps for n in (step.n_sem, step.n_sem, max(step.n_local, 1))]

    @pl.kernel(mesh=plsc.ScalarSubcoreMesh(axis_name="sequencer", num_cores=1), name=name, scratch_types=tuple(sems),
               compiler_params=pltpu.CompilerParams(collective_id=collective_id))
    def launch(*sem_refs):
        x, y, c, _ = _place()
        barrier = pltpu.get_barrier_semaphore()
        for dx, dy, dc in peers:
            pl.semaphore_signal(barrier, inc=1, device_id=(x ^ dx, y ^ dy, c ^ dc), device_id_type=MESH)
        pl.semaphore_wait(barrier, len(peers))
        for k, step in enumerate(steps):
            sends, arrivals, own = step.copies(ins if k == 0 else outs, outs, *sem_refs[3 * k:3 * k + 3])
            for cp in own + sends:
                cp.start()
            for cp in arrivals:
                cp.wait_recv()
            for cp in sends:
                cp.wait_send()
            for cp in own:
                cp.wait()

    launch()
    return [ref[...] for ref in outs]


def _chip_exchange_job(sums):
    n = len(sums)

    def copies(ins, outs, send, recv, local):
        del local
        _, _, c, chips = _place()
        sends = [pltpu.make_async_remote_copy(
            src_ref=ins[w].at[_chip_index(*chip)], dst_ref=outs[w].at[j], send_sem=send.at[3 * w + j],
            recv_sem=recv.at[3 * w + j], device_id=(*chip, c), device_id_type=MESH)
            for w in range(n) for j, chip in enumerate(chips)]
        return sends, sends, []

    return _Job(sums, [jax.ShapeDtypeStruct((N_CHIPS - 1,) + s.shape[1:], s.dtype) for s in sums], 3 * n, copies,
                OTHER_CHIPS)


def _chip_sum(name, place, mine, theirs):
    n = len(mine)

    def body(place_ref, *refs):
        del place_ref
        for p_ref, q_ref, o_ref in zip(refs[:n], refs[n:2 * n], refs[2 * n:]):
            acc = p_ref[...].astype(F32)
            for j in range(N_CHIPS - 1):
                acc = acc + q_ref[j].astype(F32)
            o_ref[...] = acc

    def block(p, lead, pick):
        return pl.BlockSpec((lead, p.shape[1] // ROW_STEPS, p.shape[2]), lambda r, place_ref: (pick(place_ref), r, 0))

    return pl.pallas_call(
        body, name=name,
        grid_spec=pltpu.PrefetchScalarGridSpec(
            num_scalar_prefetch=1, grid=(ROW_STEPS,),
            in_specs=[block(p, None, lambda place_ref: place_ref[0]) for p in mine]
            + [block(p, N_CHIPS - 1, lambda place_ref: 0) for p in mine],
            out_specs=[block(p, None, lambda place_ref: place_ref[1]) for p in mine]),
        out_shape=[jax.ShapeDtypeStruct((2,) + p.shape[1:], F32) for p in mine],
        compiler_params=_params(),
    )(place, *mine, *theirs)


def _share_job(bufs):
    n = len(bufs)

    def copies(ins, outs, send, recv, local):
        del ins, local
        x, y, c, _ = _place()

        def copy(w, half):
            return pltpu.make_async_remote_copy(
                src_ref=outs[w].at[half], dst_ref=outs[w].at[half], send_sem=send.at[w], recv_sem=recv.at[w],
                device_id=(x, y, 1 - c), device_id_type=MESH)

        return [copy(w, c) for w in range(n)], [copy(w, 1 - c) for w in range(n)], []

    return _Job(bufs, [jax.ShapeDtypeStruct(b.shape, b.dtype) for b in bufs], n, copies, SIBLING,
                aliases={w: w for w in range(n)})


SMALL_ROWS = 24
ROW_G1, ROW_CW, ROW_CB, ROW_BR, ROW_BI, ROW_LAM, ROW_LG, ROW_LB, ROW_G2, ROW_G3, ROW_LOSS, ROW_BS = (
    0, 1, 5, 6, 7, 8, 9, 10, 11, 12, 13, 16)
N_DEV = 8


def _pack_small(dcw, dcb, dbr, dbi, dlam, dlg, dlb, dg2, dg3, loss, dbs):
    def body(dcw_ref, dcb_ref, dbr_ref, dbi_ref, dlam_ref, dlg_ref, dlb_ref, dg2_ref, dg3_ref, loss_ref, dbs_ref, out):
        out[...] = jnp.zeros((SMALL_ROWS, D_MODEL), F32)
        for row, ref in ((ROW_CB, dcb_ref), (ROW_BR, dbr_ref), (ROW_BI, dbi_ref), (ROW_LAM, dlam_ref),
                         (ROW_LG, dlg_ref), (ROW_LB, dlb_ref), (ROW_G2, dg2_ref), (ROW_G3, dg3_ref)):
            out[row:row + 1, :] = ref[...]
        out[ROW_CW:ROW_CW + CONV_WIDTH, :] = dcw_ref[0:CONV_WIDTH, :]
        out[ROW_LOSS:ROW_LOSS + 1, 0:128] = loss_ref[0:1, :]
        out[ROW_BS:ROW_BS + GROUPS, 0:128] = jnp.transpose(dbs_ref[...])[0:GROUPS, :]

    vm = pl.BlockSpec(memory_space=pltpu.VMEM)
    return pl.pallas_call(
        body, name="pack_small", in_specs=[vm] * 11, out_specs=vm,
        out_shape=jax.ShapeDtypeStruct((SMALL_ROWS, D_MODEL), F32),
    )(dcw, dcb, dbr, dbi, dlam, dlg, dlb, dg2, dg3, loss, dbs)


def _gather_all_job(blocks):
    n = len(blocks)
    flips = [(dx, dy, dc) for dx in (0, 1) for dy in (0, 1) for dc in (0, 1)][1:]

    def copies(ins, outs, send, recv, local):
        x, y, c, _ = _place()
        me = 4 * x + 2 * y + c
        sends, arrivals, own = [], [], []
        for w in range(n):
            own.append(pltpu.make_async_copy(ins[w], outs[w].at[me], local.at[w]))
            for k, (dx, dy, dc) in enumerate(flips):
                peer = (x ^ dx, y ^ dy, c ^ dc)
                sem = dict(send_sem=send.at[7 * w + k], recv_sem=recv.at[7 * w + k])
                sends.append(pltpu.make_async_remote_copy(
                    src_ref=ins[w], dst_ref=outs[w].at[me], device_id=peer, device_id_type=MESH, **sem))
                arrivals.append(pltpu.make_async_remote_copy(
                    src_ref=ins[w], dst_ref=outs[w].at[4 * peer[0] + 2 * peer[1] + peer[2]], device_id=peer,
                    device_id_type=MESH, **sem))
        return sends, arrivals, own

    return _Job(blocks, [jax.ShapeDtypeStruct((N_DEV,) + b.shape, b.dtype) for b in blocks], 7 * n, copies,
                OTHER_CHIPS + SIBLING + tuple((dx, dy, 1) for dx, dy, _ in OTHER_CHIPS), n_local=n)


def _sum_small(vec_all, ws_all, dg1_all):
    def body(vec_ref, ws_ref, dg1_ref, vec_out, ws_out):
        vec, ws, dg1 = vec_ref[0], ws_ref[0], dg1_ref[0]
        for d in range(1, N_DEV):
            vec, ws, dg1 = vec + vec_ref[d], ws + ws_ref[d], dg1 + dg1_ref[d]
        vec_out[...] = vec
        vec_out[ROW_G1:ROW_G1 + 1, :] = dg1
        ws_out[...] = ws

    vm = pl.BlockSpec(memory_space=pltpu.VMEM)
    return pl.pallas_call(
        body, name="sum_small", in_specs=[vm] * 3, out_specs=[vm, vm],
        out_shape=[jax.ShapeDtypeStruct(vec_all.shape[1:], F32), jax.ShapeDtypeStruct(ws_all.shape[1:], F32)],
    )(vec_all, ws_all, dg1_all)


def _adamw_math(w, g, m, v):
    m = ADAM_B1 * m + (1.0 - ADAM_B1) * g
    v = ADAM_B2 * v + (1.0 - ADAM_B2) * (g * g)
    m_hat = m / (1.0 - ADAM_B1 ** ADAM_STEP)
    v_hat = v / (1.0 - ADAM_B2 ** ADAM_STEP)
    delta = (-ADAM_LR) * (m_hat / (jnp.sqrt(v_hat) + ADAM_EPS) + ADAM_WD * w)
    return delta, m, v


def _adamw(name, gs, ws, ms, vs):
    n = len(ws)

    def body(*refs):
        ins, outs = refs[:4 * n], refs[4 * n:]
        for p in range(n):
            g_ref, w_ref, m_ref, v_ref = ins[p::n]
            outs[3 * p][...], outs[3 * p + 1][...], outs[3 * p + 2][...] = _adamw_math(
                w_ref[...], g_ref[...], m_ref[...], v_ref[...])

    blocks = [pl.BlockSpec((w.shape[0] // ROW_STEPS, w.shape[1]), lambda r: (r, 0)) for w in ws]
    out = pl.pallas_call(
        body, name=name, grid=(ROW_STEPS,), in_specs=blocks * 4, out_specs=[b for b in blocks for _ in range(3)],
        out_shape=[jax.ShapeDtypeStruct(w.shape, F32) for w in ws for _ in range(3)], compiler_params=_params(),
    )(*gs, *ws, *ms, *vs)
    return [tuple(out[3 * p:3 * p + 3]) for p in range(n)]


def _adamw_small(grads, ws, ms, vs):
    n = len(grads)

    def body(*refs):
        g_refs, w_refs, m_refs, v_refs = refs[:n], refs[n:2 * n], refs[2 * n:3 * n], refs[3 * n:4 * n]
        outs = refs[4 * n:]
        for p in range(n):
            d, nm, nv = _adamw_math(w_refs[p][...], g_refs[p][...], m_refs[p][...], v_refs[p][...])
            outs[p][...] = d
            outs[n + p][...] = nm
            outs[2 * n + p][...] = nv

    vm = pl.BlockSpec(memory_space=pltpu.VMEM)
    shapes = [jax.ShapeDtypeStruct(w.shape, F32) for w in ws]
    out = pl.pallas_call(
        body, name="adamw_small", in_specs=[vm] * (4 * n), out_specs=[vm] * (3 * n), out_shape=shapes * 3,
    )(*grads, *ws, *ms, *vs)
    return out[:n], out[n:2 * n], out[2 * n:]


def _unstack_heads(w_st):
    per = HEAD_DIM // N_CHIPS
    return w_st.reshape(N_CHIPS, HEADS, per, HEAD_DIM).transpose(1, 0, 2, 3).reshape(HEADS, HEAD_DIM, HEAD_DIM)


def _stack_heads(w):
    per = HEAD_DIM // N_CHIPS
    return w.reshape(HEADS, N_CHIPS, per, HEAD_DIM).transpose(1, 0, 2, 3).reshape(N_CHIPS, HEADS * per, HEAD_DIM)


def kernel(x, norm_mix_g, w_in, conv_w, conv_b, w_rgate, b_rgate, w_igate, b_igate, lru_lambda, w_out_a, sgu_ln_g, sgu_ln_b, sgu_w_s, sgu_b_s, w_out_b, w_out, norm_mlp_g, w_up, w_down, norm_final_g, loss_target, m_norm_mix_g, m_w_in, m_conv_w, m_conv_b, m_w_rgate, m_b_rgate, m_w_igate, m_b_igate, m_lru_lambda, m_w_out_a, m_sgu_ln_g, m_sgu_ln_b, m_sgu_w_s, m_sgu_b_s, m_w_out_b, m_w_out, m_norm_mlp_g, m_w_up, m_w_down, m_norm_final_g, v_norm_mix_g, v_w_in, v_conv_w, v_conv_b, v_w_rgate, v_b_rgate, v_w_igate, v_b_igate, v_lru_lambda, v_w_out_a, v_sgu_ln_g, v_sgu_ln_b, v_sgu_w_s, v_sgu_b_s, v_w_out_b, v_w_out, v_norm_mlp_g, v_w_up, v_w_down, v_norm_final_g):
    chip = _chip_index(lax.axis_index("x"), lax.axis_index("y"))
    core = lax.axis_index("c")
    quarter_h = HEAD_DIM // N_CHIPS
    quarter_d = D_MODEL // N_CHIPS

    as_2d = lambda a: a.reshape(-1, a.shape[-1])
    big_w = [as_2d(w) for w in (w_in, w_rgate, w_igate, w_out_a, w_out_b, w_out, w_up, w_down)]
    big_m = [as_2d(w) for w in (m_w_in, m_w_rgate, m_w_igate, m_w_out_a, m_w_out_b, m_w_out, m_w_up, m_w_down)]
    big_v = [as_2d(w) for w in (v_w_in, v_w_rgate, v_w_igate, v_w_out_a, v_w_out_b, v_w_out, v_w_up, v_w_down)]

    packed = jnp.concatenate([conv_w[0], b_rgate[0], b_igate[0]], axis=1)
    packed = jnp.concatenate([packed, jnp.zeros_like(packed)], axis=0)
    s_in, s_r, s_i, s_oa, s_ob, s_out, s_up, s_down = [w.astype(BF16) for w in big_w]
    xs, target = x[0], loss_target[0]
    g3 = norm_final_g.reshape(1, D_MODEL)
    bias_s = jnp.broadcast_to(jnp.transpose(sgu_b_s[0])[:, :, None], (CHUNK, GROUPS, GROUP_DIM)).reshape(CHUNK, D_MODEL)
    core_arr = core.reshape(1).astype(jnp.int32)
    place = jnp.stack([chip, core]).astype(jnp.int32)
    quarter = lambda g: g.reshape(N_CHIPS, D_MODEL // N_CHIPS, D_MODEL)

    def pair_add(nm, grads, from_sibling):
        halves = [g.reshape(N_CHIPS, 2, g.shape[1] // 2, g.shape[2]) for g in grads]
        return list(_pair_add("pair_add_" + nm, core_arr, halves, from_sibling))

    def chip_sum(nm, pairs, from_chips):
        return list(_chip_sum("chip_sum_" + nm, place, pairs, from_chips))

    order = jnp.stack([chip, chip ^ 2, chip ^ 1, chip ^ 3]).astype(jnp.int32)
    (z, n1, (w_in_st, wr_st, wi_st)), ((packed_all,), late) = _fwd_in(
        xs, norm_mix_g, [s_in, s_r, s_i], order,
        jobs=[_gather_small_job(packed), _gather_near_job([s_oa, s_ob, s_out])])
    pick = lambda lo, hi: packed_all[:, :HEADS, lo:hi].transpose(1, 0, 2).reshape(HEADS, -1)
    conv_w_full = pick(0, quarter_d)
    br_full = pick(quarter_d, quarter_d + quarter_h).reshape(1, D_MODEL)
    bi_full = pick(quarter_d + quarter_h, quarter_d + 2 * quarter_h).reshape(1, D_MODEL)
    wr, wi = _unstack_heads(wr_st), _unstack_heads(wi_st)
    lru = (conv_w_full, conv_b, wr, br_full, wi, bi_full, lru_lambda)
    sgu = (sgu_ln_g, sgu_ln_b, sgu_w_s[0], bias_s)

    after = lambda arrays, result: lax.optimization_barrier((arrays, result))[0]
    w_up_st, w_dn = _sequencer_call(
        "gather_mlp", 8, _gather_near_job(after([s_up, s_down], n1)).then(_gather_far_job).then(_gather_pass_job))
    w_dn = w_dn.reshape(D_FF, D_MODEL)
    late_step = (3 * (xs.shape[0] // SEQ_TILE) // 4,)
    (ya, *saved), (late,) = _fwd_lru(z, *lru, jobs=[_gather_far_job(late).then(_gather_pass_job, at=late_step)])
    w_oa, w_ob, w_o = [w.reshape(D_MODEL, D_MODEL) for w in late]
    (yb, pa, pb, h1, n2), _ = _fwd_sgu_merge(ya, z, xs, *sgu, w_oa, w_ob, w_o, norm_mlp_g)
    (act, dup, dh2b, dh1, loss_part, dg3, dg2), _ = _mlp(n2, h1, target, w_up_st, w_dn, norm_mlp_g, g3)

    d_down, _ = _weight_grad("dw_down", act, dh2b, N_CHIPS, True, False, D_MODEL)
    r_down, = _sequencer_call("send_w_down", 10, _pair_send_job([d_down]))
    d_up, _ = _weight_grad("dw_up", n2, dup, N_CHIPS, False, True, D_MODEL)
    r_up, = _sequencer_call("send_w_up", 11, _pair_send_job([d_up]))
    (p_down,), (p_up,) = pair_add("w_down", [d_down], [r_down]), pair_add("w_up", [d_up], [r_up])
    (dz, merged, dpa, dpb, dh1b, dlg, dlb, dws, dbs, dcw, dcb, dwr, dbr, dwi, dbi, dlam), ((q_up, q_down),) = _bwd_mix(
        dh1, pa, pb, z, *saved, w_oa, w_ob, w_o, *sgu, conv_w_full, wr, wi, lru_lambda,
        jobs=[_chip_exchange_job([p_up, p_down])])
    half_up, half_down = chip_sum("mlp", [p_up, p_down], [q_up, q_down])
    names = ("w_in", "w_rgate", "w_igate", "w_out_a", "w_out_b", "w_out", "w_up", "w_down")
    (d_out, d_oa, d_ob), ((full_up, full_down),) = _weight_grads_square(
        "dw_projections", [(merged, dh1b), (ya, dpa), (yb, dpb)], jobs=[_share_job([half_up, half_down])])
    mids = [quarter(d_oa), quarter(d_ob), quarter(d_out)]
    r_mids = _sequencer_call("send_mids", 1, _pair_send_job(mids))
    gates = [_stack_heads(dwr).astype(BF16), _stack_heads(dwi).astype(BF16)]
    small = _pack_small(dcw, dcb, dbr, dbi, dlam, dlg, dlb, dg2, dg3, loss_part, dbs)
    p_mids = pair_add("projections", mids, r_mids)
    q_mids = _sequencer_call("exchange_mids", 2, _chip_exchange_job(p_mids))
    d_in, (r_gates, (vec_all, ws_all)) = _weight_grad(
        "dw_in", n1, dz, N_CHIPS, False, True, IN_SHARD,
        jobs=[_pair_send_job(gates), _gather_all_job([small, dws])])
    r_in, = _sequencer_call("send_w_in", 3, _pair_send_job([d_in]))
    adam_args = {nm: (w, m, v) for nm, w, m, v in zip(names, big_w, big_m, big_v)}

    def adamw(group, nms, grads):
        given = [adam_args[nm] for nm in nms]
        grads = [g.reshape(w.shape) for g, (w, _, _) in zip(grads, given)]
        outs = _adamw("adamw_" + group, grads, *[[a[q] for a in given] for q in range(3)])
        return {nm: (g, out) for nm, g, out in zip(nms, grads, outs)}

    p_gates = pair_add("gates", gates, r_gates)
    half_mids = chip_sum("projections", p_mids, q_mids)
    full_mids = _sequencer_call("share_mids", 12, _share_job(half_mids))
    p_first = pair_add("w_in", [d_in], [r_in]) + p_gates
    q_first = _sequencer_call("exchange_w_in", 4, _chip_exchange_job(p_first))
    (grad_x, dg1), _ = _bwd_in(dz, xs, dh1, w_in_st, norm_mix_g)
    dg1_all, = _sequencer_call("gather_dg1", 6, _gather_all_job([dg1]))
    done = adamw("mlp", ("w_up", "w_down"), [full_up, full_down])
    q_first = after(q_first, [out[0] for _, out in done.values()])
    half_first = chip_sum("first", p_first, q_first)
    full_first = _sequencer_call("share_last", 5, _share_job(half_first))
    done.update(adamw("projections", names[3:6], after(full_mids, half_first)))
    done.update(adamw("first", names[:3], full_first))
    full, big_out = [done[nm][0] for nm in names], [done[nm][1] for nm in names]

    vec, ws_sum = _sum_small(vec_all, ws_all, dg1_all)
    row = lambda r: vec[r:r + 1]
    shard = lambda a, width: lax.dynamic_slice_in_dim(a, chip * width, width, axis=1)
    g_small = dict(
        norm_mix_g=row(ROW_G1), conv_w=shard(vec[ROW_CW:ROW_CW + CONV_WIDTH], quarter_d), conv_b=row(ROW_CB),
        b_rgate=shard(row(ROW_BR).reshape(HEADS, HEAD_DIM), quarter_h),
        b_igate=shard(row(ROW_BI).reshape(HEADS, HEAD_DIM), quarter_h), lru_lambda=row(ROW_LAM),
        sgu_ln_g=row(ROW_LG), sgu_ln_b=row(ROW_LB),
        sgu_w_s=ws_sum.reshape(CHUNK, GROUPS, CHUNK).transpose(1, 0, 2).reshape(GROUPS * CHUNK, CHUNK),
        sgu_b_s=vec[ROW_BS:ROW_BS + GROUPS, 0:CHUNK], norm_mlp_g=row(ROW_G2), norm_final_g=row(ROW_G3))
    loss = vec[ROW_LOSS, 0]
    small_names = list(g_small)
    given = dict(
        norm_mix_g=(norm_mix_g, m_norm_mix_g, v_norm_mix_g), conv_w=(conv_w, m_conv_w, v_conv_w),
        conv_b=(conv_b, m_conv_b, v_conv_b), b_rgate=(b_rgate, m_b_rgate, v_b_rgate),
        b_igate=(b_igate, m_b_igate, v_b_igate), lru_lambda=(lru_lambda, m_lru_lambda, v_lru_lambda),
        sgu_ln_g=(sgu_ln_g, m_sgu_ln_g, v_sgu_ln_g), sgu_ln_b=(sgu_ln_b, m_sgu_ln_b, v_sgu_ln_b),
        sgu_w_s=(sgu_w_s, m_sgu_w_s, v_sgu_w_s), sgu_b_s=(sgu_b_s, m_sgu_b_s, v_sgu_b_s),
        norm_mlp_g=(norm_mlp_g, m_norm_mlp_g, v_norm_mlp_g), norm_final_g=(norm_final_g, m_norm_final_g, v_norm_final_g))
    g2d = [g_small[nm] for nm in small_names]
    to2d = lambda a, g: a.reshape(g.shape)
    d_s, m_s, v_s = _adamw_small(
        g2d, *[[to2d(given[nm][q], g) for nm, g in zip(small_names, g2d)] for q in range(3)])

    shapes = dict(
        norm_mix_g=norm_mix_g, w_in=w_in, conv_w=conv_w, conv_b=conv_b, w_rgate=w_rgate, b_rgate=b_rgate,
        w_igate=w_igate, b_igate=b_igate, lru_lambda=lru_lambda, w_out_a=w_out_a, sgu_ln_g=sgu_ln_g,
        sgu_ln_b=sgu_ln_b, sgu_w_s=sgu_w_s, sgu_b_s=sgu_b_s, w_out_b=w_out_b, w_out=w_out, norm_mlp_g=norm_mlp_g,
        w_up=w_up, w_down=w_down, norm_final_g=norm_final_g)
    grads, deltas, new_m, new_v = {}, {}, {}, {}
    for nm, g, (d, nmom, nvar) in zip(names, full, big_out):
        grads[nm], deltas[nm], new_m[nm], new_v[nm] = g, d, nmom, nvar
    for p, nm in enumerate(small_names):
        grads[nm], deltas[nm], new_m[nm], new_v[nm] = g2d[p], d_s[p], m_s[p], v_s[p]
    order = list(shapes)
    out = [loss, grad_x[None]]
    for group in (grads, deltas, new_m, new_v):
        out += [group[nm].reshape(shapes[nm].shape) for nm in order]
    return tuple(out)
```

```python
import functools

import jax
import jax.numpy as jnp
from jax import lax
from jax.experimental import pallas as pl
from jax.experimental.pallas import tpu as pltpu
from jax.experimental.pallas import tpu_sc as plsc

F32 = jnp.float32
BF16 = jnp.bfloat16
MESH = pl.DeviceIdType.MESH

D_MODEL = 1024
D_IN = 6 * D_MODEL
D_FF = 4 * D_MODEL
N_CHIPS = 4
IN_SHARD = D_IN // N_CHIPS
HEADS = 4
HEAD_DIM = D_MODEL // HEADS
GROUPS = 4
GROUP_DIM = D_MODEL // GROUPS
CHUNK = 128
CONV_WIDTH = 4
LRU_C = 8.0
NORM_EPS = 1e-6
LN_EPS = 1e-5

ADAM_LR = 0.001
ADAM_B1 = 0.9
ADAM_B2 = 0.999
ADAM_EPS = 1e-08
ADAM_WD = 0.01
ADAM_STEP = 10

SUBLANES = 8
MM_TILE = 512
IN_TILE = 1024
SEQ_TILE = 256
DW_TILE = 2048
VMEM_LIMIT_BYTES = 56 * 1024 * 1024

GELU_K0 = 0.7978845608028654
GELU_K1 = 0.044715


def _params(n_grid_axes=1):
    return pltpu.CompilerParams(
        dimension_semantics=("arbitrary",) * n_grid_axes, vmem_limit_bytes=VMEM_LIMIT_BYTES)


def _resident(shape):
    nd = len(shape)
    return pl.BlockSpec(shape, lambda *_: (0,) * nd, pipeline_mode=pl.Buffered(1))


def _const(shape):
    nd = len(shape)
    return pl.BlockSpec(shape, lambda *_: (0,) * nd)


def _dot(a, b):
    return jnp.dot(a, b, preferred_element_type=F32)


def _dot_nt(a, b):
    return lax.dot_general(a, b, (((1,), (1,)), ((), ())), preferred_element_type=F32)


def _dot_tn(a, b):
    return lax.dot_general(a, b, (((0,), (0,)), ((), ())), preferred_element_type=F32)


def _gelu(x):
    t = jnp.tanh(x * (GELU_K0 + (GELU_K0 * GELU_K1) * (x * x)))
    return x * (0.5 + 0.5 * t)


def _gelu_and_grad(x):
    x2 = x * x
    t = jnp.tanh(x * (GELU_K0 + (GELU_K0 * GELU_K1) * x2))
    s = 0.5 + 0.5 * t
    dg = s + (x * (1.0 - t * t)) * (0.5 * GELU_K0 + (1.5 * GELU_K0 * GELU_K1) * x2)
    return x * s, dg


def _gate(x):
    return 0.5 + 0.5 * jnp.tanh(0.5 * x.astype(F32))


def _rms(x):
    r = lax.rsqrt(jnp.mean(x * x, axis=-1, keepdims=True) + NORM_EPS)
    return x * r, r


def _rms_bwd(dn, xhat, r):
    return r * (dn - xhat * jnp.mean(dn * xhat, axis=-1, keepdims=True))


def _col_sum(v):
    return jnp.sum(v, axis=0, keepdims=True)


def _shift_down(x, tail8, k):
    xs = pltpu.roll(x, k, 0)
    ts = pltpu.roll(tail8, k, 0)
    ridx = lax.broadcasted_iota(jnp.int32, tail8.shape, 0)
    head = jnp.where(ridx < k, ts, xs[0:SUBLANES])
    return jnp.concatenate([head, xs[SUBLANES:]], axis=0)


def _shift_up(x, head8, k):
    n = x.shape[0]
    xs = pltpu.roll(x, n - k, 0)
    hs = pltpu.roll(head8, SUBLANES - k, 0)
    ridx = lax.broadcasted_iota(jnp.int32, head8.shape, 0)
    last = jnp.where(ridx >= SUBLANES - k, hs, xs[n - SUBLANES:n])
    return jnp.concatenate([xs[:n - SUBLANES], last], axis=0)


def _scan_forward(a, b, carry):
    n, cols = a.shape
    groups = n // SUBLANES
    a = a.reshape(groups, SUBLANES, cols)
    b = b.reshape(groups, SUBLANES, cols)
    sub = lax.broadcasted_iota(jnp.int32, a.shape, 1)
    for s in (1, 2, 4):
        a_s = pltpu.roll(a, s, 1)
        b_s = pltpu.roll(b, s, 1)
        m = sub >= s
        b = jnp.where(m, a * b_s + b, b)
        a = jnp.where(m, a * a_s, a)
    out = []
    for g in range(groups):
        h = a[g] * carry + b[g]
        out.append(h)
        carry = h[SUBLANES - 1:SUBLANES]
    return jnp.concatenate(out, axis=0), carry


def _scan_backward(a, b, carry):
    n, cols = a.shape
    groups = n // SUBLANES
    a = a.reshape(groups, SUBLANES, cols)
    b = b.reshape(groups, SUBLANES, cols)
    sub = lax.broadcasted_iota(jnp.int32, a.shape, 1)
    for s in (1, 2, 4):
        a_s = pltpu.roll(a, SUBLANES - s, 1)
        b_s = pltpu.roll(b, SUBLANES - s, 1)
        m = sub < SUBLANES - s
        b = jnp.where(m, a * b_s + b, b)
        a = jnp.where(m, a * a_s, a)
    out = [None] * groups
    for g in reversed(range(groups)):
        h = a[g] * carry + b[g]
        out[g] = h
        carry = h[0:1]
    return jnp.concatenate(out, axis=0), carry


def _softplus_neg(lam):
    e = jnp.exp(-jnp.abs(lam))
    u = 1.0 + e
    log1p_e = jnp.where(u == 1.0, e, jnp.log(u) * (e / jnp.where(u == 1.0, 1.0, u - 1.0)))
    return jnp.maximum(-lam, 0.0) + log1p_e


def _lru_gates(xa, tail8, cw_ref, cb_ref, wr_ref, br_ref, wi_ref, bi_ref, lam_ref):
    cw = cw_ref[...]
    xc = cb_ref[...] + cw[0:1] * xa
    for k in range(1, CONV_WIDTH):
        xc = xc + cw[k:k + 1] * _shift_down(xa, tail8, k)
    xcb = xc.astype(BF16)
    pre_r, pre_i = [], []
    for h in range(HEADS):
        cols = slice(h * HEAD_DIM, (h + 1) * HEAD_DIM)
        pre_r.append(_dot(xcb[:, cols], wr_ref[h]))
        pre_i.append(_dot(xcb[:, cols], wi_ref[h]))
    r = jax.nn.sigmoid(jnp.concatenate(pre_r, axis=1) + br_ref[...])
    ig = jax.nn.sigmoid(jnp.concatenate(pre_i, axis=1) + bi_ref[...])
    _, a, mult, _ = _decay(r, lam_ref)
    return xc, r, ig, a, mult


def _decay(r, lam_ref):
    sp = _softplus_neg(lam_ref[...])
    log_a = ((-LRU_C) * sp) * r
    a = jnp.exp(log_a)
    th = jnp.tanh(log_a)
    q = (-2.0 * th) / (1.0 - th)
    inv = lax.rsqrt(q)
    return sp, a, jnp.where(q > 0.0, q * inv, 0.0), inv


class _Phase:
    def __init__(self, copies, n_sem, n_local, start=None, finish=None):
        self.copies, self.n_sem, self.n_local, self.start, self.finish = copies, n_sem, n_local, start, finish


class _Job:
    def __init__(self, inputs, out_shape, n_sem, copies, peers, aliases=None, n_local=0):
        self.inputs, self.out_shape = list(inputs), list(out_shape)
        self.aliases = dict(aliases or {})
        self.phases = [_Phase(copies, n_sem, n_local)]
        self.peers = tuple(peers)

    def then(self, make, at=None):
        nxt = make(self.out_shape)
        self.phases[-1].finish = at
        nxt.phases[0].start = at
        self.phases += nxt.phases
        self.peers = tuple(sorted(set(self.peers + nxt.peers)))
        return self


def _fused_call(body, jobs, *, name, grid, in_specs, out_specs, out_shape, scratch_shapes=(),
                input_output_aliases=None, compiler_params=None, n_prefetch=0, jobs_start_after=None):
    single = not isinstance(out_shape, (list, tuple))
    out_specs = [out_specs] if single else list(out_specs)
    out_shape = [out_shape] if single else list(out_shape)
    n_scr = len(scratch_shapes)
    in_specs, scratch_shapes = list(in_specs), list(scratch_shapes)
    n_in, n_out = len(in_specs), len(out_shape)
    aliases = dict(input_output_aliases or {})
    in_at, out_at, phases = [], [], []
    for q, job in enumerate(jobs):
        in_at.append(len(in_specs))
        out_at.append(len(out_shape))
        for i, o in job.aliases.items():
            aliases[n_prefetch + len(in_specs) + i] = len(out_shape) + o
        in_specs += [ANY] * len(job.inputs)
        out_specs += [ANY] * len(job.out_shape)
        out_shape += job.out_shape
        for k, phase in enumerate(job.phases):
            phases.append((q, k, phase, len(scratch_shapes)))
            scratch_shapes += [pltpu.SemaphoreType.DMA((phase.n_sem,)), pltpu.SemaphoreType.DMA((phase.n_sem,)),
                               pltpu.SemaphoreType.DMA((max(phase.n_local, 1),))]
    n_in_all, n_out_all = len(in_specs), len(out_shape)
    first_step, last_step = (0,) * len(grid), tuple(g - 1 for g in grid)

    def full_body(*refs):
        prefetch, refs = refs[:n_prefetch], refs[n_prefetch:]
        ins, outs, scr = refs[:n_in_all], refs[n_in_all:n_in_all + n_out_all], refs[n_in_all + n_out_all:]
        ids = [pl.program_id(a) for a in range(len(grid))]
        at_step = lambda step: functools.reduce(jnp.logical_and, [i == k for i, k in zip(ids, step)])

        def copies(q, k, phase, sem_at):
            job = jobs[q]
            mine = outs[out_at[q]:out_at[q] + len(job.out_shape)]
            return phase.copies(ins[in_at[q]:in_at[q] + len(job.inputs)] if k == 0 else mine, mine,
                                *scr[sem_at:sem_at + 3])

        def start(*phase):
            def go():
                sends, _, local = copies(*phase)
                for cp in local + sends:
                    cp.start()
            return go

        def finish(*phase):
            def go():
                sends, arrivals, local = copies(*phase)
                for cp in arrivals:
                    cp.wait_recv()
                for cp in sends:
                    cp.wait_send()
                for cp in local:
                    cp.wait()
            return go

        for phase in phases:
            if phase[2].start is None and jobs_start_after is None:
                pl.when(at_step(first_step))(start(*phase))
        body(*prefetch, *ins[:n_in], *outs[:n_out], *scr[:n_scr])
        for phase in phases:
            pl.when(at_step(phase[2].finish or last_step))(finish(*phase))
            nxt = phase[2].start or jobs_start_after
            if nxt is not None:
                pl.when(at_step(nxt))(start(*phase))

    if n_prefetch:
        layout = dict(grid_spec=pltpu.PrefetchScalarGridSpec(
            num_scalar_prefetch=n_prefetch, grid=grid, in_specs=in_specs, out_specs=out_specs,
            scratch_shapes=scratch_shapes))
    else:
        layout = dict(grid=grid, in_specs=in_specs, out_specs=out_specs, scratch_shapes=scratch_shapes)
    call = pl.pallas_call(
        full_body, name=name, out_shape=out_shape, input_output_aliases=aliases, compiler_params=compiler_params,
        **layout)

    def run(*args):
        res = call(*args, *[a for job in jobs for a in job.inputs])
        mine = res[0] if single else list(res[:n_out])
        return mine, [list(res[at:at + len(job.out_shape)]) for at, job in zip(out_at, jobs)]

    return run


def _fwd_in(x, g1, shards, order, jobs=()):
    t = x.shape[0]
    rows_per_step = min(IN_TILE, t)
    n_tiles = t // rows_per_step
    n = len(shards)
    halves = [s.shape[0] // 2 for s in shards]

    def body(order_ref, x_ref, g_ref, *refs):
        del order_ref
        ins, (z_ref, n_ref), outs = refs[:n], refs[n:n + 2], refs[n + 2:2 * n + 2]
        wbuf, nbuf, send, recv, local = refs[2 * n + 2:]
        s, i = pl.program_id(0), pl.program_id(1)
        x_, y_, c, chips = _place()
        near, far = _near_far(x_, y_, c)
        k_me = _chip_index(x_, y_)

        def block(w, chip, pc):
            return outs[w].at[_chip_index(*chip), pl.ds(pc * halves[w], halves[w]), :]

        def over_ici(w, j, landing):
            return pltpu.make_async_remote_copy(
                src_ref=ins[w].at[pl.ds(c * halves[w], halves[w]), :],
                dst_ref=block(w, chips[j] if landing else (x_, y_), c), send_sem=send.at[6 * w + j],
                recv_sem=recv.at[6 * w + j], device_id=(*chips[j], c), device_id_type=MESH)

        def onward(w, landing):
            blk = block(w, chips[2] if landing else near, c)
            return pltpu.make_async_remote_copy(
                src_ref=blk, dst_ref=blk, send_sem=send.at[6 * w + 2], recv_sem=recv.at[6 * w + 2],
                device_id=(*far, c), device_id_type=MESH)

        def to_sibling(w, j, landing):
            blk = block(w, chips[j], 1 - c if landing else c)
            return pltpu.make_async_remote_copy(
                src_ref=blk, dst_ref=blk, send_sem=send.at[6 * w + 3 + j], recv_sem=recv.at[6 * w + 3 + j],
                device_id=(x_, y_, 1 - c), device_id_type=MESH)

        own = [pltpu.make_async_copy(wbuf, outs[0].at[k_me], local.at[0])]
        own += [pltpu.make_async_copy(ins[w], outs[w].at[k_me], local.at[w]) for w in range(1, n)]

        @pl.when((s == 0) & (i == 0))
        def _():
            for j in range(2):
                for w in range(n):
                    over_ici(w, j, False).start()
            load = pltpu.make_async_copy(ins[0], wbuf, local.at[n])
            load.start()
            load.wait()
            for cp in own:
                cp.start()

        for j in range(N_CHIPS - 1):
            @pl.when((s == j + 1) & (i == 0))
            def _(j=j):
                if j == 0:
                    for k in range(2):
                        for w in range(n):
                            over_ici(w, k, True).wait_recv()
                    for w in range(n):
                        onward(w, False).start()
                    for k in range(2):
                        for w in range(n):
                            to_sibling(w, k, False).start()
                    own[0].wait()
                if j == 2:
                    for w in range(n):
                        onward(w, True).wait_recv()
                    for w in range(n):
                        to_sibling(w, j, False).start()
                for w in range(n):
                    to_sibling(w, j, True).wait_recv()
                load = pltpu.make_async_copy(outs[0].at[_chip_index(*chips[j])], wbuf, local.at[n])
                load.start()
                load.wait()

        rows = pl.ds(pl.multiple_of(i * rows_per_step, rows_per_step), rows_per_step)

        @pl.when(s == 0)
        def _():
            xhat, _ = _rms(x_ref[...])
            nrm = (xhat * g_ref[...]).astype(BF16)
            nbuf[rows, :] = nrm
            n_ref[...] = nrm

        z_ref[...] = _dot(nbuf[rows, :], wbuf[...]).astype(BF16)

        @pl.when((s == N_CHIPS - 1) & (i == n_tiles - 1))
        def _():
            for j in range(N_CHIPS - 1):
                for w in range(n):
                    (over_ici(w, j, False) if j < 2 else onward(w, False)).wait_send()
                    to_sibling(w, j, False).wait_send()
            for cp in own[1:]:
                cp.wait()

    once = lambda s, i, order: (jnp.where(s == 0, i, n_tiles - 1), 0)
    (z, n1, *stacked), job_outs = _fused_call(
        body, jobs, name="fwd_in", grid=(N_CHIPS, n_tiles), n_prefetch=1,
        in_specs=[pl.BlockSpec((rows_per_step, D_MODEL), once), _const((1, D_MODEL))] + [ANY] * n,
        out_specs=[pl.BlockSpec((rows_per_step, IN_SHARD), lambda s, i, order: (i, order[s])),
                   pl.BlockSpec((rows_per_step, D_MODEL), once)] + [ANY] * n,
        out_shape=[jax.ShapeDtypeStruct((t, D_IN), BF16), jax.ShapeDtypeStruct((t, D_MODEL), BF16)]
        + [jax.ShapeDtypeStruct((N_CHIPS,) + s.shape, s.dtype) for s in shards],
        scratch_shapes=[pltpu.VMEM(shards[0].shape, BF16), pltpu.VMEM((t, D_MODEL), BF16),
                        pltpu.SemaphoreType.DMA((6 * n,)),
                        pltpu.SemaphoreType.DMA((6 * n,)), pltpu.SemaphoreType.DMA((n + 1,))],
        compiler_params=_params(2), jobs_start_after=(1, 0),
    )(order, x, g1, *shards)
    return (z, n1, stacked), job_outs


def _fwd_lru(z, conv_w, conv_b, wr, br, wi, bi, lam, jobs=()):
    t = z.shape[0]

    def body(xa_ref, ga_ref, cw_ref, cb_ref, wr_ref, br_ref, wi_ref, bi_ref, lam_ref, ya_ref, h_ref, xc_ref, r_ref,
             ig_ref, tail_ref, carry_ref):
        @pl.when(pl.program_id(0) == 0)
        def _():
            tail_ref[...] = jnp.zeros_like(tail_ref)
            carry_ref[...] = jnp.zeros_like(carry_ref)

        xa = xa_ref[...].astype(F32)
        xc, r, ig, a, mult = _lru_gates(xa, tail_ref[...], cw_ref, cb_ref, wr_ref, br_ref, wi_ref, bi_ref, lam_ref)
        tail_ref[...] = xa[SEQ_TILE - SUBLANES:]
        xc_ref[...], r_ref[...], ig_ref[...] = xc, r, ig
        h, carry = _scan_forward(a, xc * ig * mult, carry_ref[...])
        carry_ref[...] = carry
        h_ref[...] = h
        ya_ref[...] = (h * _gelu(ga_ref[...].astype(F32))).astype(BF16)

    tile = lambda j: pl.BlockSpec((SEQ_TILE, D_MODEL), lambda i: (i, j))
    return _fused_call(
        body, jobs, name="fwd_lru", grid=(t // SEQ_TILE,),
        in_specs=[tile(0), tile(1), _const((CONV_WIDTH, D_MODEL)), _const((1, D_MODEL)),
                  _resident((HEADS, HEAD_DIM, HEAD_DIM)), _const((1, D_MODEL)),
                  _resident((HEADS, HEAD_DIM, HEAD_DIM)), _const((1, D_MODEL)), _const((1, D_MODEL))],
        out_specs=[tile(0)] * 5,
        out_shape=[jax.ShapeDtypeStruct((t, D_MODEL), BF16)] + [jax.ShapeDtypeStruct((t, D_MODEL), F32)] * 4,
        scratch_shapes=[pltpu.VMEM((SUBLANES, D_MODEL), F32), pltpu.VMEM((1, D_MODEL), F32)],
        compiler_params=_params(),
    )(z, z, conv_w, conv_b, wr, br, wi, bi, lam)


def _sgu_forward_parts(ub, vb, lg_ref, lb_ref):
    u, du = _gelu_and_grad(ub.astype(F32))
    vg, dvg = _gelu_and_grad(vb.astype(F32))
    mu = jnp.mean(vg, axis=-1, keepdims=True)
    d = vg - mu
    rstd = lax.rsqrt(jnp.mean(d * d, axis=-1, keepdims=True) + LN_EPS)
    vhat = d * rstd
    vn = (vhat * lg_ref[...] + lb_ref[...]).astype(BF16)
    return u, du, dvg, rstd, vhat, vn


def _causal_mask():
    rows = lax.broadcasted_iota(jnp.int32, (CHUNK, CHUNK), 0)
    cols = lax.broadcasted_iota(jnp.int32, (CHUNK, CHUNK), 1)
    return rows >= cols


def _fwd_sgu_merge(ya, z, x, ln_g, ln_b, w_s, bias_full, w_oa, w_ob, w_out, g2, jobs=()):
    t = x.shape[0]

    def body(ya_ref, ub_ref, vb_ref, m_ref, x_ref, lg_ref, lb_ref, ws_ref, bias_ref, woa_ref, wob_ref, wout_ref, g_ref,
             yb_ref, pa_ref, pb_ref, h1_ref, n2_ref):
        u, _, _, _, _, vn = _sgu_forward_parts(ub_ref[...], vb_ref[...], lg_ref, lb_ref)
        mask = _causal_mask()
        wm = [jnp.where(mask, ws_ref[g], 0.0).astype(BF16) for g in range(GROUPS)]
        for c in range(SEQ_TILE // CHUNK):
            rows = slice(c * CHUNK, (c + 1) * CHUNK)
            for g in range(GROUPS):
                cols = slice(g * GROUP_DIM, (g + 1) * GROUP_DIM)
                sp = _dot(wm[g], vn[rows, cols]) + bias_ref[:, cols]
                yb_ref[rows, cols] = (u[rows, cols] * sp).astype(BF16)
        pa = _dot(ya_ref[...], woa_ref[...])
        pb = _dot(yb_ref[...], wob_ref[...])
        pa_ref[...] = pa
        pb_ref[...] = pb
        merged = _gate(m_ref[:, :D_MODEL]) * pa + _gate(m_ref[:, D_MODEL:]) * pb
        h1 = x_ref[...] + _dot(merged.astype(BF16), wout_ref[...])
        h1_ref[...] = h1
        xhat, _ = _rms(h1)
        n2_ref[...] = (xhat * g_ref[...]).astype(BF16)

    tile = lambda j: pl.BlockSpec((SEQ_TILE, D_MODEL), lambda i: (i, j))
    sq = _resident((D_MODEL, D_MODEL))
    vec = _const((1, D_MODEL))
    bf, f32 = jax.ShapeDtypeStruct((t, D_MODEL), BF16), jax.ShapeDtypeStruct((t, D_MODEL), F32)
    return _fused_call(
        body, jobs, name="fwd_sgu_merge", grid=(t // SEQ_TILE,),
        in_specs=[tile(0), tile(2), tile(3), pl.BlockSpec((SEQ_TILE, 2 * D_MODEL), lambda i: (i, 2)), tile(0), vec, vec,
                  _const((GROUPS, CHUNK, CHUNK)), _const((CHUNK, D_MODEL)), sq, sq, sq, vec],
        out_specs=[tile(0)] * 5,
        out_shape=[bf, f32, f32, f32, bf],
        compiler_params=_params(),
    )(ya, z, z, z, x, ln_g, ln_b, w_s, bias_full, w_oa, w_ob, w_out, g2)


def _mlp(n2, h1, target, w_up_st, w_down, g2, g3, jobs=()):
    t = n2.shape[0]

    def body(n2_ref, h1_ref, tgt_ref, wup_ref, wdown_ref, g2_ref, g3_ref, act_ref, dup_ref, dh2b_ref, dh1_ref,
             loss_ref, dg3_ref, dg2_ref, relu_ref):
        @pl.when(pl.program_id(0) == 0)
        def _():
            for ref in (loss_ref, dg3_ref, dg2_ref):
                ref[...] = jnp.zeros_like(ref)

        n2 = n2_ref[...]
        h1 = h1_ref[...]
        h2 = h1
        for k in range(N_CHIPS):
            cols = slice(k * D_MODEL, (k + 1) * D_MODEL)
            r = jnp.maximum(_dot(n2, wup_ref[k]), 0.0)
            relu_ref[:, cols] = r
            act = (r * r).astype(BF16)
            act_ref[:, cols] = act
            h2 = h2 + _dot(act, wdown_ref[cols, :])
        xhat, r3 = _rms(h2)
        diff = xhat * g3_ref[...] - tgt_ref[...]
        sq = jnp.sum(diff * diff, axis=1, keepdims=True)
        loss_ref[...] = loss_ref[...] + (0.5 / D_MODEL) * jnp.sum(sq, axis=0, keepdims=True)
        dy = diff * (1.0 / D_MODEL)
        dg3_ref[...] = dg3_ref[...] + _col_sum(dy * xhat)
        dh2 = _rms_bwd(dy * g3_ref[...], xhat, r3)
        dh2b = dh2.astype(BF16)
        dh2b_ref[...] = dh2b
        dn2 = jnp.zeros((SEQ_TILE, D_MODEL), F32)
        for k in range(N_CHIPS):
            cols = slice(k * D_MODEL, (k + 1) * D_MODEL)
            dup = (_dot_nt(dh2b, wdown_ref[cols, :]) * (2.0 * relu_ref[:, cols])).astype(BF16)
            dup_ref[:, cols] = dup
            dn2 = dn2 + _dot_nt(dup, wup_ref[k])
        xhat, r2 = _rms(h1)
        dg2_ref[...] = dg2_ref[...] + _col_sum(dn2 * xhat)
        dh1_ref[...] = dh2 + _rms_bwd(dn2 * g2_ref[...], xhat, r2)

    tile = pl.BlockSpec((SEQ_TILE, D_MODEL), lambda i: (i, 0))
    wide = pl.BlockSpec((SEQ_TILE, D_FF), lambda i: (i, 0))
    vec = _const((1, D_MODEL))
    vec_shape = jax.ShapeDtypeStruct((1, D_MODEL), F32)
    return _fused_call(
        body, jobs, name="mlp", grid=(t // SEQ_TILE,),
        in_specs=[tile, tile, tile, _resident((N_CHIPS, D_MODEL, D_MODEL)), _resident((D_FF, D_MODEL)), vec, vec],
        out_specs=[wide, wide, tile, tile, _const((SUBLANES, 128)), vec, vec],
        out_shape=[jax.ShapeDtypeStruct((t, D_FF), BF16), jax.ShapeDtypeStruct((t, D_FF), BF16),
                   jax.ShapeDtypeStruct((t, D_MODEL), BF16), jax.ShapeDtypeStruct((t, D_MODEL), F32),
                   jax.ShapeDtypeStruct((SUBLANES, 128), F32), vec_shape, vec_shape],
        scratch_shapes=[pltpu.VMEM((SEQ_TILE, D_FF), F32)],
        compiler_params=_params(),
    )(n2, h1, target, w_up_st, w_down, g2, g3)


def _bwd_mix(dh1, pa, pb, z, h, xc, r, ig, w_oa, w_ob, w_out, ln_g, ln_b, w_s, bias_full, conv_w, wr, wi, lam, jobs=()):
    t = dh1.shape[0]
    n_tiles = t // SEQ_TILE
    per_tile = SEQ_TILE // SUBLANES

    def merge_part(dh1_ref, pa_ref, pb_ref, m_ref, woa_ref, wob_ref, wout_ref, dz_ref, dya_ref, dyb_ref, mg_ref,
                   dpa_ref, dpb_ref, dh1b_ref):
        dh1b = dh1_ref[...].astype(BF16)
        dh1b_ref[...] = dh1b
        dm = _dot_nt(dh1b, wout_ref[...])
        pa = pa_ref[...]
        pb = pb_ref[...]
        sa = _gate(m_ref[:, :D_MODEL])
        sb = _gate(m_ref[:, D_MODEL:])
        mg_ref[...] = (sa * pa + sb * pb).astype(BF16)
        dpa = dm * sa
        dpb = dm * sb
        dz_ref[:, :D_MODEL] = ((dpa * pa) * (1.0 - sa)).astype(BF16)
        dz_ref[:, D_MODEL:] = ((dpb * pb) * (1.0 - sb)).astype(BF16)
        dpa = dpa.astype(BF16)
        dpb = dpb.astype(BF16)
        dpa_ref[...] = dpa
        dpb_ref[...] = dpb
        dya_ref[...] = _dot_nt(dpa, woa_ref[...])
        dyb_ref[...] = _dot_nt(dpb, wob_ref[...])

    def sgu_part(dyb_ref, ub_ref, vb_ref, lg_ref, lb_ref, ws_ref, bias_ref, dz_ref, dlg_ref, dlb_ref, dws_ref, dbs_ref,
                 dvn_ref, dsp_acc):
        i = pl.program_id(0)

        @pl.when(i == 0)
        def _():
            dlg_ref[...] = jnp.zeros_like(dlg_ref)
            dlb_ref[...] = jnp.zeros_like(dlb_ref)
            dws_ref[...] = jnp.zeros_like(dws_ref)
            dsp_acc[...] = jnp.zeros_like(dsp_acc)

        u, du, dvg, rstd, vhat, vn = _sgu_forward_parts(ub_ref[...], vb_ref[...], lg_ref, lb_ref)
        dyb = dyb_ref[...]
        mask = _causal_mask()
        wm = [jnp.where(mask, ws_ref[g], 0.0).astype(BF16) for g in range(GROUPS)]
        for c in range(SEQ_TILE // CHUNK):
            rows = slice(c * CHUNK, (c + 1) * CHUNK)
            for g in range(GROUPS):
                cols = slice(g * GROUP_DIM, (g + 1) * GROUP_DIM)
                vn_blk = vn[rows, cols]
                sp = _dot(wm[g], vn_blk) + bias_ref[:, cols]
                dyb_blk = dyb[rows, cols]
                dz_ref[rows, cols] = (dyb_blk * sp * du[rows, cols]).astype(BF16)
                dsp = dyb_blk * u[rows, cols]
                dsp_acc[:, cols] = dsp_acc[:, cols] + dsp
                dspb = dsp.astype(BF16)
                dvn_ref[rows, cols] = _dot_tn(wm[g], dspb)
                wcols = slice(g * CHUNK, (g + 1) * CHUNK)
                dws_ref[:, wcols] = dws_ref[:, wcols] + jnp.where(mask, _dot_nt(dspb, vn_blk), 0.0)
        dvn = dvn_ref[...]
        dlg_ref[...] = dlg_ref[...] + _col_sum(dvn * vhat)
        dlb_ref[...] = dlb_ref[...] + _col_sum(dvn)
        dvhat = dvn * lg_ref[...]
        dvgel = rstd * (dvhat - jnp.mean(dvhat, axis=-1, keepdims=True)
                        - vhat * jnp.mean(dvhat * vhat, axis=-1, keepdims=True))
        dz_ref[:, D_MODEL:] = (dvgel * dvg).astype(BF16)

        @pl.when(i == n_tiles - 1)
        def _():
            lane = lax.broadcasted_iota(jnp.int32, (CHUNK, 128), 1)
            out = jnp.zeros((CHUNK, 128), F32)
            for g in range(GROUPS):
                s = jnp.sum(dsp_acc[:, g * GROUP_DIM:(g + 1) * GROUP_DIM], axis=1, keepdims=True)
                out = out + jnp.where(lane == g, s, 0.0)
            dbs_ref[...] = out

    def lru_part(dya_ref, xa_ref, ga_ref, h_ref, h_prev_ref, xc_ref, r_ref, ig_ref, cw_ref, wr_ref, wi_ref, lam_ref,
                 dz_ref, dcw_ref, dcb_ref, dwr_ref, dbr_ref, dwi_ref, dbi_ref, dlam_ref, lam_carry, dxc_head):
        i = pl.program_id(0)

        @pl.when(i == 0)
        def _():
            for ref in (dcw_ref, dcb_ref, dwr_ref, dbr_ref, dwi_ref, dbi_ref, dlam_ref, lam_carry, dxc_head):
                ref[...] = jnp.zeros_like(ref)

        first_tile = i == n_tiles - 1
        h_tail = jnp.where(first_tile, 0.0, h_prev_ref[...])
        xc, r, ig = xc_ref[...], r_ref[...], ig_ref[...]
        xcb = xc.astype(BF16)
        sp, a, mult, inv_mult = _decay(r, lam_ref)
        h = h_ref[...]
        h_prev = _shift_down(h, h_tail, 1)
        dya = dya_ref[...]
        gg, dgg = _gelu_and_grad(ga_ref[...].astype(F32))
        dz_ref[:, D_MODEL:] = (dya * h * dgg).astype(BF16)
        ones = jnp.ones((SUBLANES, D_MODEL), F32)
        lam_t, lam_first = _scan_backward(_shift_up(a, ones, 1), dya * gg, lam_carry[...])
        lam_carry[...] = a[0:1] * lam_first
        lam_ig = lam_t * ig
        dxc_direct = lam_ig * mult
        dmult = lam_ig * xc
        dla = a * (lam_t * h_prev - (dmult * a) * inv_mult)
        dla_r = dla * r
        dlam_ref[...] = dlam_ref[...] + _col_sum(dla_r) * (LRU_C * jax.nn.sigmoid(-lam_ref[...]))
        dpr = (dla_r * ((-LRU_C) * sp)) * (1.0 - r)
        dpi = (dxc_direct * xc) * (1.0 - ig)
        dbr_ref[...] = dbr_ref[...] + _col_sum(dpr)
        dbi_ref[...] = dbi_ref[...] + _col_sum(dpi)
        dprb = dpr.astype(BF16)
        dpib = dpi.astype(BF16)
        dxc_gate = []
        for hd in range(HEADS):
            cols = slice(hd * HEAD_DIM, (hd + 1) * HEAD_DIM)
            dxc_gate.append(_dot_nt(dprb[:, cols], wr_ref[hd]) + _dot_nt(dpib[:, cols], wi_ref[hd]))
            dwr_ref[hd] = dwr_ref[hd] + _dot_tn(xcb[:, cols], dprb[:, cols])
            dwi_ref[hd] = dwi_ref[hd] + _dot_tn(xcb[:, cols], dpib[:, cols])
        dxc = dxc_direct + jnp.concatenate(dxc_gate, axis=1)
        dcb_ref[...] = dcb_ref[...] + _col_sum(dxc)
        cw = cw_ref[...]
        head = dxc_head[...]
        xa = xa_ref[...].astype(F32)
        dxa = cw[0:1] * dxc
        dcw_ref[0:1, :] = dcw_ref[0:1, :] + _col_sum(dxc * xa)
        for k in range(1, CONV_WIDTH):
            dxc_k = _shift_up(dxc, head, k)
            dxa = dxa + cw[k:k + 1] * dxc_k
            dcw_ref[k:k + 1, :] = dcw_ref[k:k + 1, :] + _col_sum(dxc_k * xa)
        dxc_head[...] = dxc[0:SUBLANES]
        dz_ref[:, :D_MODEL] = dxa.astype(BF16)

    def body(dh1_ref, pa_ref, pb_ref, z_ref, h_ref, h_prev_ref, xc_ref, r_ref, ig_ref, woa_ref, wob_ref, wout_ref,
             lg_ref, lb_ref, ws_ref, bias_ref, cw_ref, wr_ref, wi_ref, lam_ref, dz_ref, mg_ref, dpa_ref, dpb_ref,
             dh1b_ref, dlg_ref, dlb_ref, dws_ref, dbs_ref, dcw_ref, dcb_ref, dwr_ref, dbr_ref, dwi_ref, dbi_ref,
             dlam_ref, dya_ref, dyb_ref, dvn_ref, dsp_acc, lam_carry, dxc_head):
        def cols(ref, first, count):
            return ref.at[:, pl.ds(first * D_MODEL, count * D_MODEL)]

        merge_part(dh1_ref, pa_ref, pb_ref, cols(z_ref, 4, 2), woa_ref, wob_ref, wout_ref, cols(dz_ref, 4, 2), dya_ref,
                   dyb_ref, mg_ref, dpa_ref, dpb_ref, dh1b_ref)
        sgu_part(dyb_ref, cols(z_ref, 2, 1), cols(z_ref, 3, 1), lg_ref, lb_ref, ws_ref, bias_ref, cols(dz_ref, 2, 2),
                 dlg_ref, dlb_ref, dws_ref, dbs_ref, dvn_ref, dsp_acc)
        lru_part(dya_ref, cols(z_ref, 0, 1), cols(z_ref, 1, 1), h_ref, h_prev_ref, xc_ref, r_ref, ig_ref, cw_ref, wr_ref,
                 wi_ref, lam_ref, cols(dz_ref, 0, 2), dcw_ref, dcb_ref, dwr_ref, dbr_ref, dwi_ref, dbi_ref, dlam_ref,
                 lam_carry, dxc_head)

    rev = lambda i: n_tiles - 1 - i
    tile = pl.BlockSpec((SEQ_TILE, D_MODEL), lambda i: (rev(i), 0))
    row = pl.BlockSpec((SEQ_TILE, D_IN), lambda i: (rev(i), 0))
    prev8 = pl.BlockSpec((SUBLANES, D_MODEL), lambda i: (jnp.maximum(rev(i) * per_tile - 1, 0), 0))
    vec = _const((1, D_MODEL))
    sq = _resident((D_MODEL, D_MODEL))
    gate_w = _resident((HEADS, HEAD_DIM, HEAD_DIM))
    gate_acc = _const((HEADS, HEAD_DIM, HEAD_DIM))
    vec_shape = jax.ShapeDtypeStruct((1, D_MODEL), F32)
    gate_shape = jax.ShapeDtypeStruct((HEADS, HEAD_DIM, HEAD_DIM), F32)
    act_bf = jax.ShapeDtypeStruct((t, D_MODEL), BF16)
    return _fused_call(
        body, jobs, name="bwd_mix", grid=(n_tiles,),
        in_specs=[tile, tile, tile, row, tile, prev8, tile, tile, tile, sq, sq, sq, vec, vec,
                  _const((GROUPS, CHUNK, CHUNK)), _const((CHUNK, D_MODEL)), _const((CONV_WIDTH, D_MODEL)), gate_w, gate_w,
                  vec],
        out_specs=[row, tile, tile, tile, tile, vec, vec, _const((CHUNK, GROUPS * CHUNK)), _const((CHUNK, 128)),
                   _const((SUBLANES, D_MODEL)), vec, gate_acc, vec, gate_acc, vec, vec],
        out_shape=[jax.ShapeDtypeStruct((t, D_IN), BF16), act_bf, act_bf, act_bf, act_bf, vec_shape, vec_shape,
                   jax.ShapeDtypeStruct((CHUNK, GROUPS * CHUNK), F32), jax.ShapeDtypeStruct((CHUNK, 128), F32),
                   jax.ShapeDtypeStruct((SUBLANES, D_MODEL), F32), vec_shape, gate_shape, vec_shape, gate_shape,
                   vec_shape, vec_shape],
        scratch_shapes=[pltpu.VMEM((SEQ_TILE, D_MODEL), F32), pltpu.VMEM((SEQ_TILE, D_MODEL), F32),
                        pltpu.VMEM((SEQ_TILE, D_MODEL), F32), pltpu.VMEM((CHUNK, D_MODEL), F32),
                        pltpu.VMEM((1, D_MODEL), F32), pltpu.VMEM((SUBLANES, D_MODEL), F32)],
        compiler_params=_params(),
    )(dh1, pa, pb, z, h, h, xc, r, ig, w_oa, w_ob, w_out, ln_g, ln_b, w_s, bias_full, conv_w, wr, wi, lam)


def _bwd_in(dz, x, dh1, w_in_st, g1, jobs=()):
    t = x.shape[0]

    def body(dz_ref, x_ref, dh1_ref, w_ref, g_ref, dx_ref, dg1_ref):
        @pl.when(pl.program_id(0) == 0)
        def _():
            dg1_ref[...] = jnp.zeros_like(dg1_ref)

        dn1 = jnp.zeros((MM_TILE, D_MODEL), F32)
        for k in range(N_CHIPS):
            dn1 = dn1 + _dot_nt(dz_ref[:, k * IN_SHARD:(k + 1) * IN_SHARD], w_ref[k])
        xhat, r1 = _rms(x_ref[...])
        dg1_ref[...] = dg1_ref[...] + _col_sum(dn1 * xhat)
        dx_ref[...] = dh1_ref[...] + _rms_bwd(dn1 * g_ref[...], xhat, r1)

    tile = pl.BlockSpec((MM_TILE, D_MODEL), lambda i: (i, 0))
    return _fused_call(
        body, jobs, name="bwd_in", grid=(t // MM_TILE,),
        in_specs=[pl.BlockSpec((MM_TILE, D_IN), lambda i: (i, 0)), tile, tile,
                  _resident((N_CHIPS, D_MODEL, IN_SHARD)), _const((1, D_MODEL))],
        out_specs=[tile, _const((1, D_MODEL))],
        out_shape=[jax.ShapeDtypeStruct((t, D_MODEL), F32), jax.ShapeDtypeStruct((1, D_MODEL), F32)],
        compiler_params=_params(),
    )(dz, x, dh1, w_in_st, g1)


def _weight_grad(name, a, b, n_blocks, a_varies, b_varies, width, jobs=()):
    t = a.shape[0]
    rows = min(DW_TILE, t)
    n_t = t // rows

    def body(a_ref, b_ref, o_ref, acc_ref):
        s = pl.program_id(1)
        part = _dot_tn(a_ref[...], b_ref[...])

        @pl.when(s == 0)
        def _():
            acc_ref[...] = part

        @pl.when(s > 0)
        def _():
            acc_ref[...] = acc_ref[...] + part

        @pl.when(s == n_t - 1)
        def _():
            o_ref[...] = acc_ref[...].astype(BF16)

    return _fused_call(
        body, jobs, name=name, grid=(n_blocks, n_t),
        in_specs=[pl.BlockSpec((rows, D_MODEL), (lambda j, s: (s, j)) if a_varies else (lambda j, s: (s, 0))),
                  pl.BlockSpec((rows, width), (lambda j, s: (s, j)) if b_varies else (lambda j, s: (s, 0)))],
        out_specs=pl.BlockSpec((None, D_MODEL, width), lambda j, s: (j, 0, 0)),
        out_shape=jax.ShapeDtypeStruct((n_blocks, D_MODEL, width), BF16),
        scratch_shapes=[pltpu.VMEM((D_MODEL, width), F32)],
        compiler_params=_params(2),
    )(a, b)


def _weight_grads_square(name, pairs, jobs=()):
    n = len(pairs)
    t = pairs[0][0].shape[0]
    rows = min(2 * MM_TILE, t)
    n_t = t // rows

    def body(*refs):
        ins, outs, accs = refs[:2 * n], refs[2 * n:3 * n], refs[3 * n:]
        s = pl.program_id(0)
        for k in range(n):
            part = _dot_tn(ins[2 * k][...], ins[2 * k + 1][...])

            @pl.when(s == 0)
            def _(k=k, part=part):
                accs[k][...] = part

            @pl.when(s > 0)
            def _(k=k, part=part):
                accs[k][...] = accs[k][...] + part

            @pl.when(s == n_t - 1)
            def _(k=k):
                outs[k][...] = accs[k][...].astype(BF16)

    tile = pl.BlockSpec((rows, D_MODEL), lambda s: (s, 0))
    return _fused_call(
        body, jobs, name=name, grid=(n_t,), in_specs=[tile] * (2 * n), out_specs=[_const((D_MODEL, D_MODEL))] * n,
        out_shape=[jax.ShapeDtypeStruct((D_MODEL, D_MODEL), BF16)] * n,
        scratch_shapes=[pltpu.VMEM((D_MODEL, D_MODEL), F32)] * n,
        compiler_params=_params(),
    )(*[x for pair in pairs for x in pair])


def _place():
    x, y, c = lax.axis_index("x"), lax.axis_index("y"), lax.axis_index("c")
    other_chips = [(1 - x, y), (x, 1 - y), (1 - x, 1 - y)]
    return x, y, c, other_chips


def _chip_index(px, py):
    return 2 * px + py


ANY = pl.BlockSpec(memory_space=pl.ANY)
SIBLING = ((0, 0, 1),)
NEIGHBOURS = ((1, 0, 0), (0, 1, 0))
OTHER_CHIPS = NEIGHBOURS + ((1, 1, 0),)


def _near_far(x, y, c):
    return (x ^ (1 - c), y ^ c), (x ^ c, y ^ (1 - c))


def _gather_near_job(shards):
    n = len(shards)
    halves = [s.shape[0] // 2 for s in shards]

    def copies(ins, outs, send, recv, local):
        x, y, c, _ = _place()
        near, _ = _near_far(x, y, c)

        def block(w, chip, pc):
            return outs[w].at[_chip_index(*chip), pl.ds(pc * halves[w], halves[w]), :]

        def copy(w, k, chip, pc, to, src=None):
            return pltpu.make_async_remote_copy(
                src_ref=block(w, chip, pc) if src is None else src, dst_ref=block(w, chip, pc),
                send_sem=send.at[2 * w + k], recv_sem=recv.at[2 * w + k], device_id=to, device_id_type=MESH)

        sends, arrivals, own = [], [], []
        for w in range(n):
            src = ins[w].at[pl.ds(c * halves[w], halves[w]), :]
            own.append(pltpu.make_async_copy(src, block(w, (x, y), c), local.at[w]))
            sends += [copy(w, 0, (x, y), c, (*near, c), src), copy(w, 1, (x, y), c, (x, y, 1 - c), src)]
            arrivals += [copy(w, 0, near, c, (x, y, c)), copy(w, 1, (x, y), 1 - c, (x, y, c))]
        return sends, arrivals, own

    return _Job(shards, [jax.ShapeDtypeStruct((N_CHIPS,) + s.shape, s.dtype) for s in shards], 2 * n, copies,
                NEIGHBOURS + SIBLING, n_local=n)


def _gather_far_job(stacked):
    n = len(stacked)
    halves = [s.shape[1] // 2 for s in stacked]

    def copies(ins, outs, send, recv, local):
        del ins, local
        x, y, c, _ = _place()
        near, far = _near_far(x, y, c)

        def copy(w, k, chip):
            blk = outs[w].at[_chip_index(*chip), pl.ds(c * halves[w], halves[w]), :]
            return pltpu.make_async_remote_copy(
                src_ref=blk, dst_ref=blk, send_sem=send.at[2 * w + k], recv_sem=recv.at[2 * w + k],
                device_id=(*far, c), device_id_type=MESH)

        sends = [copy(w, k, chip) for w in range(n) for k, chip in enumerate(((x, y), near))]
        arrivals = [copy(w, k, chip) for w in range(n) for k, chip in enumerate((far, (1 - x, 1 - y)))]
        return sends, arrivals, []

    return _Job(stacked, [jax.ShapeDtypeStruct(s.shape, s.dtype) for s in stacked], 2 * n, copies, NEIGHBOURS,
                aliases={w: w for w in range(n)})


def _gather_pass_job(stacked):
    n = len(stacked)
    halves = [s.shape[1] // 2 for s in stacked]

    def copies(ins, outs, send, recv, local):
        del ins, local
        x, y, c, chips = _place()

        def copy(w, j, chip, pc, to):
            blk = outs[w].at[_chip_index(*chip), pl.ds(pc * halves[w], halves[w]), :]
            return pltpu.make_async_remote_copy(
                src_ref=blk, dst_ref=blk, send_sem=send.at[3 * w + j], recv_sem=recv.at[3 * w + j], device_id=to,
                device_id_type=MESH)

        sends = [copy(w, j, chip, c, (x, y, 1 - c)) for w in range(n) for j, chip in enumerate(chips)]
        arrivals = [copy(w, j, chip, 1 - c, (x, y, c)) for w in range(n) for j, chip in enumerate(chips)]
        return sends, arrivals, []

    return _Job(stacked, [jax.ShapeDtypeStruct(s.shape, s.dtype) for s in stacked], 3 * n, copies, SIBLING,
                aliases={w: w for w in range(n)})


def _gather_small_job(block):
    def copies(ins, outs, send, recv, local):
        x, y, c, chips = _place()

        def copy(j, chip_from, to):
            return pltpu.make_async_remote_copy(
                src_ref=ins[0], dst_ref=outs[0].at[_chip_index(*chip_from)], send_sem=send.at[j],
                recv_sem=recv.at[j], device_id=to, device_id_type=MESH)

        own = [pltpu.make_async_copy(ins[0], outs[0].at[_chip_index(x, y)], local.at[0])]
        sends = [copy(j, (x, y), (*chip, c)) for j, chip in enumerate(chips)]
        arrivals = [copy(j, chip, (x, y, c)) for j, chip in enumerate(chips)]
        return sends, arrivals, own

    return _Job([block], [jax.ShapeDtypeStruct((N_CHIPS,) + block.shape, block.dtype)], 3, copies, OTHER_CHIPS,
                n_local=1)


def _pair_send_job(grads):
    n = len(grads)
    halves = [g.shape[1] // 2 for g in grads]

    def copies(ins, outs, send, recv, local):
        del local
        x, y, c, _ = _place()
        sends = [pltpu.make_async_remote_copy(
            src_ref=ins[w].at[:, pl.ds((1 - c) * halves[w], halves[w]), :], dst_ref=outs[w], send_sem=send.at[w],
            recv_sem=recv.at[w], device_id=(x, y, 1 - c), device_id_type=MESH) for w in range(n)]
        return sends, sends, []

    return _Job(grads, [jax.ShapeDtypeStruct((N_CHIPS, h, g.shape[2]), g.dtype) for g, h in zip(grads, halves)], n,
                copies, SIBLING)


ROW_STEPS = 4


def _pair_add(name, core, mine, theirs):
    n = len(mine)

    def body(core_ref, *refs):
        del core_ref
        for a_ref, b_ref, o_ref in zip(refs[:n], refs[n:2 * n], refs[2 * n:]):
            o_ref[...] = (a_ref[...].astype(F32) + b_ref[...].astype(F32)).astype(BF16)

    half = lambda a: pl.BlockSpec((None, None) + a.shape[2:], lambda k, core_ref: (k, core_ref[0], 0, 0))
    block = lambda b: pl.BlockSpec((None,) + b.shape[1:], lambda k, core_ref: (k, 0, 0))
    return pl.pallas_call(
        body, name=name,
        grid_spec=pltpu.PrefetchScalarGridSpec(
            num_scalar_prefetch=1, grid=(N_CHIPS,),
            in_specs=[half(a) for a in mine] + [block(b) for b in theirs], out_specs=[block(b) for b in theirs]),
        out_shape=[jax.ShapeDtypeStruct(b.shape, BF16) for b in theirs],
        compiler_params=_params(),
    )(core, *mine, *theirs)


def _sequencer_call(name, collective_id, job):
    steps, peers = job.phases, job.peers
    ins = [jax.new_ref(a, memory_space=pltpu.MemorySpace.HBM) for a in job.inputs]
    outs = [ins[{o: i for i, o in job.aliases.items()}[k]] if k in job.aliases.values()
            else jax.empty_ref(shape, memory_space=pltpu.MemorySpace.HBM) for k, shape in enumerate(job.out_shape)]
    sems = [pltpu.SemaphoreType.DMA((n,)) for step in steps for n in (step.n_sem, step.n_sem, max(step.n_local, 1))]

    @pl.kernel(mesh=plsc.ScalarSubcoreMesh(axis_name="sequencer", num_cores=1), name=name, scratch_types=tuple(sems),
               compiler_params=pltpu.CompilerParams(collective_id=collective_id))
    def launch(*sem_refs):
        x, y, c, _ = _place()
        barrier = pltpu.get_barrier_semaphore()
        for dx, dy, dc in peers:
            pl.semaphore_signal(barrier, inc=1, device_id=(x ^ dx, y ^ dy, c ^ dc), device_id_type=MESH)
        pl.semaphore_wait(barrier, len(peers))
        for k, step in enumerate(steps):
            sends, arrivals, own = step.copies(ins if k == 0 else outs, outs, *sem_refs[3 * k:3 * k + 3])
            for cp in own + sends:
                cp.start()
            for cp in arrivals:
                cp.wait_recv()
            for cp in sends:
                cp.wait_send()
            for cp in own:
                cp.wait()

    launch()
    return [ref[...] for ref in outs]


def _chip_exchange_job(sums):
    n = len(sums)

    def copies(ins, outs, send, recv, local):
        del local
        _, _, c, chips = _place()
        sends = [pltpu.make_async_remote_copy(
            src_ref=ins[w].at[_chip_index(*chip)], dst_ref=outs[w].at[j], send_sem=send.at[3 * w + j],
            recv_sem=recv.at[3 * w + j], device_id=(*chip, c), device_id_type=MESH)
            for w in range(n) for j, chip in enumerate(chips)]
        return sends, sends, []

    return _Job(sums, [jax.ShapeDtypeStruct((N_CHIPS - 1,) + s.shape[1:], s.dtype) for s in sums], 3 * n, copies,
                OTHER_CHIPS)


def _chip_sum(name, place, mine, theirs):
    n = len(mine)

    def body(place_ref, *refs):
        del place_ref
        for p_ref, q_ref, o_ref in zip(refs[:n], refs[n:2 * n], refs[2 * n:]):
            acc = p_ref[...].astype(F32)
            for j in range(N_CHIPS - 1):
                acc = acc + q_ref[j].astype(F32)
            o_ref[...] = acc

    def block(p, lead, pick):
        return pl.BlockSpec((lead, p.shape[1] // ROW_STEPS, p.shape[2]), lambda r, place_ref: (pick(place_ref), r, 0))

    return pl.pallas_call(
        body, name=name,
        grid_spec=pltpu.PrefetchScalarGridSpec(
            num_scalar_prefetch=1, grid=(ROW_STEPS,),
            in_specs=[block(p, None, lambda place_ref: place_ref[0]) for p in mine]
            + [block(p, N_CHIPS - 1, lambda place_ref: 0) for p in mine],
            out_specs=[block(p, None, lambda place_ref: place_ref[1]) for p in mine]),
        out_shape=[jax.ShapeDtypeStruct((2,) + p.shape[1:], F32) for p in mine],
        compiler_params=_params(),
    )(place, *mine, *theirs)


def _share_job(bufs):
    n = len(bufs)

    def copies(ins, outs, send, recv, local):
        del ins, local
        x, y, c, _ = _place()

        def copy(w, half):
            return pltpu.make_async_remote_copy(
                src_ref=outs[w].at[half], dst_ref=outs[w].at[half], send_sem=send.at[w], recv_sem=recv.at[w],
                device_id=(x, y, 1 - c), device_id_type=MESH)

        return [copy(w, c) for w in range(n)], [copy(w, 1 - c) for w in range(n)], []

    return _Job(bufs, [jax.ShapeDtypeStruct(b.shape, b.dtype) for b in bufs], n, copies, SIBLING,
                aliases={w: w for w in range(n)})


SMALL_ROWS = 24
ROW_G1, ROW_CW, ROW_CB, ROW_BR, ROW_BI, ROW_LAM, ROW_LG, ROW_LB, ROW_G2, ROW_G3, ROW_LOSS, ROW_BS = (
    0, 1, 5, 6, 7, 8, 9, 10, 11, 12, 13, 16)
N_DEV = 8


def _pack_small(dcw, dcb, dbr, dbi, dlam, dlg, dlb, dg2, dg3, loss, dbs):
    def body(dcw_ref, dcb_ref, dbr_ref, dbi_ref, dlam_ref, dlg_ref, dlb_ref, dg2_ref, dg3_ref, loss_ref, dbs_ref, out):
        out[...] = jnp.zeros((SMALL_ROWS, D_MODEL), F32)
        for row, ref in ((ROW_CB, dcb_ref), (ROW_BR, dbr_ref), (ROW_BI, dbi_ref), (ROW_LAM, dlam_ref),
                         (ROW_LG, dlg_ref), (ROW_LB, dlb_ref), (ROW_G2, dg2_ref), (ROW_G3, dg3_ref)):
            out[row:row + 1, :] = ref[...]
        out[ROW_CW:ROW_CW + CONV_WIDTH, :] = dcw_ref[0:CONV_WIDTH, :]
        out[ROW_LOSS:ROW_LOSS + 1, 0:128] = loss_ref[0:1, :]
        out[ROW_BS:ROW_BS + GROUPS, 0:128] = jnp.transpose(dbs_ref[...])[0:GROUPS, :]

    vm = pl.BlockSpec(memory_space=pltpu.VMEM)
    return pl.pallas_call(
        body, name="pack_small", in_specs=[vm] * 11, out_specs=vm,
        out_shape=jax.ShapeDtypeStruct((SMALL_ROWS, D_MODEL), F32),
    )(dcw, dcb, dbr, dbi, dlam, dlg, dlb, dg2, dg3, loss, dbs)


def _gather_all_job(blocks):
    n = len(blocks)
    flips = [(dx, dy, dc) for dx in (0, 1) for dy in (0, 1) for dc in (0, 1)][1:]

    def copies(ins, outs, send, recv, local):
        x, y, c, _ = _place()
        me = 4 * x + 2 * y + c
        sends, arrivals, own = [], [], []
        for w in range(n):
            own.append(pltpu.make_async_copy(ins[w], outs[w].at[me], local.at[w]))
            for k, (dx, dy, dc) in enumerate(flips):
                peer = (x ^ dx, y ^ dy, c ^ dc)
                sem = dict(send_sem=send.at[7 * w + k], recv_sem=recv.at[7 * w + k])
                sends.append(pltpu.make_async_remote_copy(
                    src_ref=ins[w], dst_ref=outs[w].at[me], device_id=peer, device_id_type=MESH, **sem))
                arrivals.append(pltpu.make_async_remote_copy(
                    src_ref=ins[w], dst_ref=outs[w].at[4 * peer[0] + 2 * peer[1] + peer[2]], device_id=peer,
                    device_id_type=MESH, **sem))
        return sends, arrivals, own

    return _Job(blocks, [jax.ShapeDtypeStruct((N_DEV,) + b.shape, b.dtype) for b in blocks], 7 * n, copies,
                OTHER_CHIPS + SIBLING + tuple((dx, dy, 1) for dx, dy, _ in OTHER_CHIPS), n_local=n)


def _sum_small(vec_all, ws_all, dg1_all):
    def body(vec_ref, ws_ref, dg1_ref, vec_out, ws_out):
        vec, ws, dg1 = vec_ref[0], ws_ref[0], dg1_ref[0]
        for d in range(1, N_DEV):
            vec, ws, dg1 = vec + vec_ref[d], ws + ws_ref[d], dg1 + dg1_ref[d]
        vec_out[...] = vec
        vec_out[ROW_G1:ROW_G1 + 1, :] = dg1
        ws_out[...] = ws

    vm = pl.BlockSpec(memory_space=pltpu.VMEM)
    return pl.pallas_call(
        body, name="sum_small", in_specs=[vm] * 3, out_specs=[vm, vm],
        out_shape=[jax.ShapeDtypeStruct(vec_all.shape[1:], F32), jax.ShapeDtypeStruct(ws_all.shape[1:], F32)],
    )(vec_all, ws_all, dg1_all)


def _adamw_math(w, g, m, v):
    m = ADAM_B1 * m + (1.0 - ADAM_B1) * g
    v = ADAM_B2 * v + (1.0 - ADAM_B2) * (g * g)
    m_hat = m / (1.0 - ADAM_B1 ** ADAM_STEP)
    v_hat = v / (1.0 - ADAM_B2 ** ADAM_STEP)
    delta = (-ADAM_LR) * (m_hat / (jnp.sqrt(v_hat) + ADAM_EPS) + ADAM_WD * w)
    return delta, m, v


def _adamw(name, gs, ws, ms, vs):
    n = len(ws)

    def body(*refs):
        ins, outs = refs[:4 * n], refs[4 * n:]
        for p in range(n):
            g_ref, w_ref, m_ref, v_ref = ins[p::n]
            outs[3 * p][...], outs[3 * p + 1][...], outs[3 * p + 2][...] = _adamw_math(
                w_ref[...], g_ref[...], m_ref[...], v_ref[...])

    blocks = [pl.BlockSpec((w.shape[0] // ROW_STEPS, w.shape[1]), lambda r: (r, 0)) for w in ws]
    out = pl.pallas_call(
        body, name=name, grid=(ROW_STEPS,), in_specs=blocks * 4, out_specs=[b for b in blocks for _ in range(3)],
        out_shape=[jax.ShapeDtypeStruct(w.shape, F32) for w in ws for _ in range(3)], compiler_params=_params(),
    )(*gs, *ws, *ms, *vs)
    return [tuple(out[3 * p:3 * p + 3]) for p in range(n)]


def _adamw_small(grads, ws, ms, vs):
    n = len(grads)

    def body(*refs):
        g_refs, w_refs, m_refs, v_refs = refs[:n], refs[n:2 * n], refs[2 * n:3 * n], refs[3 * n:4 * n]
        outs = refs[4 * n:]
        for p in range(n):
            d, nm, nv = _adamw_math(w_refs[p][...], g_refs[p][...], m_refs[p][...], v_refs[p][...])
            outs[p][...] = d
            outs[n + p][...] = nm
            outs[2 * n + p][...] = nv

    vm = pl.BlockSpec(memory_space=pltpu.VMEM)
    shapes = [jax.ShapeDtypeStruct(w.shape, F32) for w in ws]
    out = pl.pallas_call(
        body, name="adamw_small", in_specs=[vm] * (4 * n), out_specs=[vm] * (3 * n), out_shape=shapes * 3,
    )(*grads, *ws, *ms, *vs)
    return out[:n], out[n:2 * n], out[2 * n:]


def _unstack_heads(w_st):
    per = HEAD_DIM // N_CHIPS
    return w_st.reshape(N_CHIPS, HEADS, per, HEAD_DIM).transpose(1, 0, 2, 3).reshape(HEADS, HEAD_DIM, HEAD_DIM)


def _stack_heads(w):
    per = HEAD_DIM // N_CHIPS
    return w.reshape(HEADS, N_CHIPS, per, HEAD_DIM).transpose(1, 0, 2, 3).reshape(N_CHIPS, HEADS * per, HEAD_DIM)


def kernel(x, norm_mix_g, w_in, conv_w, conv_b, w_rgate, b_rgate, w_igate, b_igate, lru_lambda, w_out_a, sgu_ln_g, sgu_ln_b, sgu_w_s, sgu_b_s, w_out_b, w_out, norm_mlp_g, w_up, w_down, norm_final_g, loss_target, m_norm_mix_g, m_w_in, m_conv_w, m_conv_b, m_w_rgate, m_b_rgate, m_w_igate, m_b_igate, m_lru_lambda, m_w_out_a, m_sgu_ln_g, m_sgu_ln_b, m_sgu_w_s, m_sgu_b_s, m_w_out_b, m_w_out, m_norm_mlp_g, m_w_up, m_w_down, m_norm_final_g, v_norm_mix_g, v_w_in, v_conv_w, v_conv_b, v_w_rgate, v_b_rgate, v_w_igate, v_b_igate, v_lru_lambda, v_w_out_a, v_sgu_ln_g, v_sgu_ln_b, v_sgu_w_s, v_sgu_b_s, v_w_out_b, v_w_out, v_norm_mlp_g, v_w_up, v_w_down, v_norm_final_g):
    chip = _chip_index(lax.axis_index("x"), lax.axis_index("y"))
    core = lax.axis_index("c")
    quarter_h = HEAD_DIM // N_CHIPS
    quarter_d = D_MODEL // N_CHIPS

    as_2d = lambda a: a.reshape(-1, a.shape[-1])
    big_w = [as_2d(w) for w in (w_in, w_rgate, w_igate, w_out_a, w_out_b, w_out, w_up, w_down)]
    big_m = [as_2d(w) for w in (m_w_in, m_w_rgate, m_w_igate, m_w_out_a, m_w_out_b, m_w_out, m_w_up, m_w_down)]
    big_v = [as_2d(w) for w in (v_w_in, v_w_rgate, v_w_igate, v_w_out_a, v_w_out_b, v_w_out, v_w_up, v_w_down)]

    packed = jnp.concatenate([conv_w[0], b_rgate[0], b_igate[0]], axis=1)
    packed = jnp.concatenate([packed, jnp.zeros_like(packed)], axis=0)
    s_in, s_r, s_i, s_oa, s_ob, s_out, s_up, s_down = [w.astype(BF16) for w in big_w]
    xs, target = x[0], loss_target[0]
    g3 = norm_final_g.reshape(1, D_MODEL)
    bias_s = jnp.broadcast_to(jnp.transpose(sgu_b_s[0])[:, :, None], (CHUNK, GROUPS, GROUP_DIM)).reshape(CHUNK, D_MODEL)
    core_arr = core.reshape(1).astype(jnp.int32)
    place = jnp.stack([chip, core]).astype(jnp.int32)
    quarter = lambda g: g.reshape(N_CHIPS, D_MODEL // N_CHIPS, D_MODEL)

    def pair_add(nm, grads, from_sibling):
        halves = [g.reshape(N_CHIPS, 2, g.shape[1] // 2, g.shape[2]) for g in grads]
        return list(_pair_add("pair_add_" + nm, core_arr, halves, from_sibling))

    def chip_sum(nm, pairs, from_chips):
        return list(_chip_sum("chip_sum_" + nm, place, pairs, from_chips))

    order = jnp.stack([chip, chip ^ 2, chip ^ 1, chip ^ 3]).astype(jnp.int32)
    (z, n1, (w_in_st, wr_st, wi_st)), ((packed_all,), late) = _fwd_in(
        xs, norm_mix_g, [s_in, s_r, s_i], order,
        jobs=[_gather_small_job(packed), _gather_near_job([s_oa, s_ob, s_out])])
    pick = lambda lo, hi: packed_all[:, :HEADS, lo:hi].transpose(1, 0, 2).reshape(HEADS, -1)
    conv_w_full = pick(0, quarter_d)
    br_full = pick(quarter_d, quarter_d + quarter_h).reshape(1, D_MODEL)
    bi_full = pick(quarter_d + quarter_h, quarter_d + 2 * quarter_h).reshape(1, D_MODEL)
    wr, wi = _unstack_heads(wr_st), _unstack_heads(wi_st)
    lru = (conv_w_full, conv_b, wr, br_full, wi, bi_full, lru_lambda)
    sgu = (sgu_ln_g, sgu_ln_b, sgu_w_s[0], bias_s)

    after = lambda arrays, result: lax.optimization_barrier((arrays, result))[0]
    w_up_st, w_dn = _sequencer_call(
        "gather_mlp", 8, _gather_near_job(after([s_up, s_down], n1)).then(_gather_far_job).then(_gather_pass_job))
    w_dn = w_dn.reshape(D_FF, D_MODEL)
    late_step = (3 * (xs.shape[0] // SEQ_TILE) // 4,)
    (ya, *saved), (late,) = _fwd_lru(z, *lru, jobs=[_gather_far_job(late).then(_gather_pass_job, at=late_step)])
    w_oa, w_ob, w_o = [w.reshape(D_MODEL, D_MODEL) for w in late]
    (yb, pa, pb, h1, n2), _ = _fwd_sgu_merge(ya, z, xs, *sgu, w_oa, w_ob, w_o, norm_mlp_g)
    (act, dup, dh2b, dh1, loss_part, dg3, dg2), _ = _mlp(n2, h1, target, w_up_st, w_dn, norm_mlp_g, g3)

    d_down, _ = _weight_grad("dw_down", act, dh2b, N_CHIPS, True, False, D_MODEL)
    r_down, = _sequencer_call("send_w_down", 10, _pair_send_job([d_down]))
    d_up, _ = _weight_grad("dw_up", n2, dup, N_CHIPS, False, True, D_MODEL)
    r_up, = _sequencer_call("send_w_up", 11, _pair_send_job([d_up]))
    (p_down,), (p_up,) = pair_add("w_down", [d_down], [r_down]), pair_add("w_up", [d_up], [r_up])
    (dz, merged, dpa, dpb, dh1b, dlg, dlb, dws, dbs, dcw, dcb, dwr, dbr, dwi, dbi, dlam), ((q_up, q_down),) = _bwd_mix(
        dh1, pa, pb, z, *saved, w_oa, w_ob, w_o, *sgu, conv_w_full, wr, wi, lru_lambda,
        jobs=[_chip_exchange_job([p_up, p_down])])
    half_up, half_down = chip_sum("mlp", [p_up, p_down], [q_up, q_down])
    names = ("w_in", "w_rgate", "w_igate", "w_out_a", "w_out_b", "w_out", "w_up", "w_down")
    (d_out, d_oa, d_ob), ((full_up, full_down),) = _weight_grads_square(
        "dw_projections", [(merged, dh1b), (ya, dpa), (yb, dpb)], jobs=[_share_job([half_up, half_down])])
    mids = [quarter(d_oa), quarter(d_ob), quarter(d_out)]
    r_mids = _sequencer_call("send_mids", 1, _pair_send_job(mids))
    gates = [_stack_heads(dwr).astype(BF16), _stack_heads(dwi).astype(BF16)]
    small = _pack_small(dcw, dcb, dbr, dbi, dlam, dlg, dlb, dg2, dg3, loss_part, dbs)
    p_mids = pair_add("projections", mids, r_mids)
    q_mids = _sequencer_call("exchange_mids", 2, _chip_exchange_job(p_mids))
    d_in, (r_gates, (vec_all, ws_all)) = _weight_grad(
        "dw_in", n1, after(dz, p_mids), N_CHIPS, False, True, IN_SHARD,
        jobs=[_pair_send_job(gates), _gather_all_job([small, dws])])
    r_in, = _sequencer_call("send_w_in", 3, _pair_send_job([d_in]))
    adam_args = {nm: (w, m, v) for nm, w, m, v in zip(names, big_w, big_m, big_v)}

    def adamw(group, nms, grads):
        given = [adam_args[nm] for nm in nms]
        grads = [g.reshape(w.shape) for g, (w, _, _) in zip(grads, given)]
        outs = _adamw("adamw_" + group, grads, *[[a[q] for a in given] for q in range(3)])
        return {nm: (g, out) for nm, g, out in zip(nms, grads, outs)}

    p_gates = pair_add("gates", gates, r_gates)
    half_mids = chip_sum("projections", p_mids, q_mids)
    full_mids = _sequencer_call("share_mids", 12, _share_job(half_mids))
    p_first = pair_add("w_in", [d_in], [r_in]) + p_gates
    q_first = _sequencer_call("exchange_w_in", 4, _chip_exchange_job(p_first))
    (grad_x, dg1), _ = _bwd_in(dz, xs, dh1, w_in_st, norm_mix_g)
    dg1_all, = _sequencer_call("gather_dg1", 6, _gather_all_job([dg1]))
    done = adamw("mlp", ("w_up", "w_down"), [full_up, full_down])
    q_first = after(q_first, [out[0] for _, out in done.values()])
    half_first = chip_sum("first", p_first, q_first)
    full_first = _sequencer_call("share_last", 5, _share_job(half_first))
    done.update(adamw("projections", names[3:6], after(full_mids, half_first)))
    done.update(adamw("first", names[:3], full_first))
    full, big_out = [done[nm][0] for nm in names], [done[nm][1] for nm in names]

    vec, ws_sum = _sum_small(vec_all, ws_all, dg1_all)
    row = lambda r: vec[r:r + 1]
    shard = lambda a, width: lax.dynamic_slice_in_dim(a, chip * width, width, axis=1)
    g_small = dict(
        norm_mix_g=row(ROW_G1), conv_w=shard(vec[ROW_CW:ROW_CW + CONV_WIDTH], quarter_d), conv_b=row(ROW_CB),
        b_rgate=shard(row(ROW_BR).reshape(HEADS, HEAD_DIM), quarter_h),
        b_igate=shard(row(ROW_BI).reshape(HEADS, HEAD_DIM), quarter_h), lru_lambda=row(ROW_LAM),
        sgu_ln_g=row(ROW_LG), sgu_ln_b=row(ROW_LB),
        sgu_w_s=ws_sum.reshape(CHUNK, GROUPS, CHUNK).transpose(1, 0, 2).reshape(GROUPS * CHUNK, CHUNK),
        sgu_b_s=vec[ROW_BS:ROW_BS + GROUPS, 0:CHUNK], norm_mlp_g=row(ROW_G2), norm_final_g=row(ROW_G3))
    loss = vec[ROW_LOSS, 0]
    small_names = list(g_small)
    given = dict(
        norm_mix_g=(norm_mix_g, m_norm_mix_g, v_norm_mix_g), conv_w=(conv_w, m_conv_w, v_conv_w),
        conv_b=(conv_b, m_conv_b, v_conv_b), b_rgate=(b_rgate, m_b_rgate, v_b_rgate),
        b_igate=(b_igate, m_b_igate, v_b_igate), lru_lambda=(lru_lambda, m_lru_lambda, v_lru_lambda),
        sgu_ln_g=(sgu_ln_g, m_sgu_ln_g, v_sgu_ln_g), sgu_ln_b=(sgu_ln_b, m_sgu_ln_b, v_sgu_ln_b),
        sgu_w_s=(sgu_w_s, m_sgu_w_s, v_sgu_w_s), sgu_b_s=(sgu_b_s, m_sgu_b_s, v_sgu_b_s),
        norm_mlp_g=(norm_mlp_g, m_norm_mlp_g, v_norm_mlp_g), norm_final_g=(norm_final_g, m_norm_final_g, v_norm_final_g))
    g2d = [g_small[nm] for nm in small_names]
    to2d = lambda a, g: a.reshape(g.shape)
    d_s, m_s, v_s = _adamw_small(
        g2d, *[[to2d(given[nm][q], g) for nm, g in zip(small_names, g2d)] for q in range(3)])

    shapes = dict(
        norm_mix_g=norm_mix_g, w_in=w_in, conv_w=conv_w, conv_b=conv_b, w_rgate=w_rgate, b_rgate=b_rgate,
        w_igate=w_igate, b_igate=b_igate, lru_lambda=lru_lambda, w_out_a=w_out_a, sgu_ln_g=sgu_ln_g,
        sgu_ln_b=sgu_ln_b, sgu_w_s=sgu_w_s, sgu_b_s=sgu_b_s, w_out_b=w_out_b, w_out=w_out, norm_mlp_g=norm_mlp_g,
        w_up=w_up, w_down=w_down, norm_final_g=norm_final_g)
    grads, deltas, new_m, new_v = {}, {}, {}, {}
    for nm, g, (d, nmom, nvar) in zip(names, full, big_out):
        grads[nm], deltas[nm], new_m[nm], new_v[nm] = g, d, nmom, nvar
    for p, nm in enumerate(small_names):
        grads[nm], deltas[nm], new_m[nm], new_v[nm] = g2d[p], d_s[p], m_s[p], v_s[p]
    order = list(shapes)
    out = [loss, grad_x[None]]
    for group in (grads, deltas, new_m, new_v):
        out += [group[nm].reshape(shapes[nm].shape) for nm in order]
    return tuple(out)
```

```python
import functools

import jax
import jax.numpy as jnp
from jax import lax
from jax.experimental import pallas as pl
from jax.experimental.pallas import tpu as pltpu
from jax.experimental.pallas import tpu_sc as plsc

F32 = jnp.float32
BF16 = jnp.bfloat16
MESH = pl.DeviceIdType.MESH

D_MODEL = 1024
D_IN = 6 * D_MODEL
D_FF = 4 * D_MODEL
N_CHIPS = 4
IN_SHARD = D_IN // N_CHIPS
HEADS = 4
HEAD_DIM = D_MODEL // HEADS
GROUPS = 4
GROUP_DIM = D_MODEL // GROUPS
CHUNK = 128
CONV_WIDTH = 4
LRU_C = 8.0
NORM_EPS = 1e-6
LN_EPS = 1e-5

ADAM_LR = 0.001
ADAM_B1 = 0.9
ADAM_B2 = 0.999
ADAM_EPS = 1e-08
ADAM_WD = 0.01
ADAM_STEP = 10

SUBLANES = 8
MM_TILE = 512
IN_TILE = 1024
SEQ_TILE = 256
DW_TILE = 2048
VMEM_LIMIT_BYTES = 56 * 1024 * 1024

GELU_K0 = 0.7978845608028654
GELU_K1 = 0.044715


def _params(n_grid_axes=1):
    return pltpu.CompilerParams(
        dimension_semantics=("arbitrary",) * n_grid_axes, vmem_limit_bytes=VMEM_LIMIT_BYTES)


def _resident(shape):
    nd = len(shape)
    return pl.BlockSpec(shape, lambda *_: (0,) * nd, pipeline_mode=pl.Buffered(1))


def _const(shape):
    nd = len(shape)
    return pl.BlockSpec(shape, lambda *_: (0,) * nd)


def _dot(a, b):
    return jnp.dot(a, b, preferred_element_type=F32)


def _dot_nt(a, b):
    return lax.dot_general(a, b, (((1,), (1,)), ((), ())), preferred_element_type=F32)


def _dot_tn(a, b):
    return lax.dot_general(a, b, (((0,), (0,)), ((), ())), preferred_element_type=F32)


def _gelu(x):
    t = jnp.tanh(x * (GELU_K0 + (GELU_K0 * GELU_K1) * (x * x)))
    return x * (0.5 + 0.5 * t)


def _gelu_and_grad(x):
    x2 = x * x
    t = jnp.tanh(x * (GELU_K0 + (GELU_K0 * GELU_K1) * x2))
    s = 0.5 + 0.5 * t
    dg = s + (x * (1.0 - t * t)) * (0.5 * GELU_K0 + (1.5 * GELU_K0 * GELU_K1) * x2)
    return x * s, dg


def _gate(x):
    return 0.5 + 0.5 * jnp.tanh(0.5 * x.astype(F32))


def _rms(x):
    r = lax.rsqrt(jnp.mean(x * x, axis=-1, keepdims=True) + NORM_EPS)
    return x * r, r


def _rms_bwd(dn, xhat, r):
    return r * (dn - xhat * jnp.mean(dn * xhat, axis=-1, keepdims=True))


def _col_sum(v):
    return jnp.sum(v, axis=0, keepdims=True)


def _shift_down(x, tail8, k):
    xs = pltpu.roll(x, k, 0)
    ts = pltpu.roll(tail8, k, 0)
    ridx = lax.broadcasted_iota(jnp.int32, tail8.shape, 0)
    head = jnp.where(ridx < k, ts, xs[0:SUBLANES])
    return jnp.concatenate([head, xs[SUBLANES:]], axis=0)


def _shift_up(x, head8, k):
    n = x.shape[0]
    xs = pltpu.roll(x, n - k, 0)
    hs = pltpu.roll(head8, SUBLANES - k, 0)
    ridx = lax.broadcasted_iota(jnp.int32, head8.shape, 0)
    last = jnp.where(ridx >= SUBLANES - k, hs, xs[n - SUBLANES:n])
    return jnp.concatenate([xs[:n - SUBLANES], last], axis=0)


def _scan_forward(a, b, carry):
    n, cols = a.shape
    groups = n // SUBLANES
    a = a.reshape(groups, SUBLANES, cols)
    b = b.reshape(groups, SUBLANES, cols)
    sub = lax.broadcasted_iota(jnp.int32, a.shape, 1)
    for s in (1, 2, 4):
        a_s = pltpu.roll(a, s, 1)
        b_s = pltpu.roll(b, s, 1)
        m = sub >= s
        b = jnp.where(m, a * b_s + b, b)
        a = jnp.where(m, a * a_s, a)
    out = []
    for g in range(groups):
        h = a[g] * carry + b[g]
        out.append(h)
        carry = h[SUBLANES - 1:SUBLANES]
    return jnp.concatenate(out, axis=0), carry


def _scan_backward(a, b, carry):
    n, cols = a.shape
    groups = n // SUBLANES
    a = a.reshape(groups, SUBLANES, cols)
    b = b.reshape(groups, SUBLANES, cols)
    sub = lax.broadcasted_iota(jnp.int32, a.shape, 1)
    for s in (1, 2, 4):
        a_s = pltpu.roll(a, SUBLANES - s, 1)
        b_s = pltpu.roll(b, SUBLANES - s, 1)
        m = sub < SUBLANES - s
        b = jnp.where(m, a * b_s + b, b)
        a = jnp.where(m, a * a_s, a)
    out = [None] * groups
    for g in reversed(range(groups)):
        h = a[g] * carry + b[g]
        out[g] = h
        carry = h[0:1]
    return jnp.concatenate(out, axis=0), carry


def _softplus_neg(lam):
    e = jnp.exp(-jnp.abs(lam))
    u = 1.0 + e
    log1p_e = jnp.where(u == 1.0, e, jnp.log(u) * (e / jnp.where(u == 1.0, 1.0, u - 1.0)))
    return jnp.maximum(-lam, 0.0) + log1p_e


def _lru_gates(xa, tail8, cw_ref, cb_ref, wr_ref, br_ref, wi_ref, bi_ref, lam_ref):
    cw = cw_ref[...]
    xc = cb_ref[...] + cw[0:1] * xa
    for k in range(1, CONV_WIDTH):
        xc = xc + cw[k:k + 1] * _shift_down(xa, tail8, k)
    xcb = xc.astype(BF16)
    pre_r, pre_i = [], []
    for h in range(HEADS):
        cols = slice(h * HEAD_DIM, (h + 1) * HEAD_DIM)
        pre_r.append(_dot(xcb[:, cols], wr_ref[h]))
        pre_i.append(_dot(xcb[:, cols], wi_ref[h]))
    r = jax.nn.sigmoid(jnp.concatenate(pre_r, axis=1) + br_ref[...])
    ig = jax.nn.sigmoid(jnp.concatenate(pre_i, axis=1) + bi_ref[...])
    _, a, mult, _ = _decay(r, lam_ref)
    return xc, r, ig, a, mult


def _decay(r, lam_ref):
    sp = _softplus_neg(lam_ref[...])
    log_a = ((-LRU_C) * sp) * r
    a = jnp.exp(log_a)
    th = jnp.tanh(log_a)
    q = (-2.0 * th) / (1.0 - th)
    inv = lax.rsqrt(q)
    return sp, a, jnp.where(q > 0.0, q * inv, 0.0), inv


class _Phase:
    def __init__(self, copies, n_sem, n_local, start=None, finish=None):
        self.copies, self.n_sem, self.n_local, self.start, self.finish = copies, n_sem, n_local, start, finish


class _Job:
    def __init__(self, inputs, out_shape, n_sem, copies, peers, aliases=None, n_local=0):
        self.inputs, self.out_shape = list(inputs), list(out_shape)
        self.aliases = dict(aliases or {})
        self.phases = [_Phase(copies, n_sem, n_local)]
        self.peers = tuple(peers)

    def then(self, make, at=None):
        nxt = make(self.out_shape)
        self.phases[-1].finish = at
        nxt.phases[0].start = at
        self.phases += nxt.phases
        self.peers = tuple(sorted(set(self.peers + nxt.peers)))
        return self


def _fused_call(body, jobs, *, name, grid, in_specs, out_specs, out_shape, scratch_shapes=(),
                input_output_aliases=None, compiler_params=None, n_prefetch=0, jobs_start_after=None):
    single = not isinstance(out_shape, (list, tuple))
    out_specs = [out_specs] if single else list(out_specs)
    out_shape = [out_shape] if single else list(out_shape)
    n_scr = len(scratch_shapes)
    in_specs, scratch_shapes = list(in_specs), list(scratch_shapes)
    n_in, n_out = len(in_specs), len(out_shape)
    aliases = dict(input_output_aliases or {})
    in_at, out_at, phases = [], [], []
    for q, job in enumerate(jobs):
        in_at.append(len(in_specs))
        out_at.append(len(out_shape))
        for i, o in job.aliases.items():
            aliases[n_prefetch + len(in_specs) + i] = len(out_shape) + o
        in_specs += [ANY] * len(job.inputs)
        out_specs += [ANY] * len(job.out_shape)
        out_shape += job.out_shape
        for k, phase in enumerate(job.phases):
            phases.append((q, k, phase, len(scratch_shapes)))
            scratch_shapes += [pltpu.SemaphoreType.DMA((phase.n_sem,)), pltpu.SemaphoreType.DMA((phase.n_sem,)),
                               pltpu.SemaphoreType.DMA((max(phase.n_local, 1),))]
    n_in_all, n_out_all = len(in_specs), len(out_shape)
    first_step, last_step = (0,) * len(grid), tuple(g - 1 for g in grid)

    def full_body(*refs):
        prefetch, refs = refs[:n_prefetch], refs[n_prefetch:]
        ins, outs, scr = refs[:n_in_all], refs[n_in_all:n_in_all + n_out_all], refs[n_in_all + n_out_all:]
        ids = [pl.program_id(a) for a in range(len(grid))]
        at_step = lambda step: functools.reduce(jnp.logical_and, [i == k for i, k in zip(ids, step)])

        def copies(q, k, phase, sem_at):
            job = jobs[q]
            mine = outs[out_at[q]:out_at[q] + len(job.out_shape)]
            return phase.copies(ins[in_at[q]:in_at[q] + len(job.inputs)] if k == 0 else mine, mine,
                                *scr[sem_at:sem_at + 3])

        def start(*phase):
            def go():
                sends, _, local = copies(*phase)
                for cp in local + sends:
                    cp.start()
            return go

        def finish(*phase):
            def go():
                sends, arrivals, local = copies(*phase)
                for cp in arrivals:
                    cp.wait_recv()
                for cp in sends:
                    cp.wait_send()
                for cp in local:
                    cp.wait()
            return go

        for phase in phases:
            if phase[2].start is None and jobs_start_after is None:
                pl.when(at_step(first_step))(start(*phase))
        body(*prefetch, *ins[:n_in], *outs[:n_out], *scr[:n_scr])
        for phase in phases:
            pl.when(at_step(phase[2].finish or last_step))(finish(*phase))
            nxt = phase[2].start or jobs_start_after
            if nxt is not None:
                pl.when(at_step(nxt))(start(*phase))

    if n_prefetch:
        layout = dict(grid_spec=pltpu.PrefetchScalarGridSpec(
            num_scalar_prefetch=n_prefetch, grid=grid, in_specs=in_specs, out_specs=out_specs,
            scratch_shapes=scratch_shapes))
    else:
        layout = dict(grid=grid, in_specs=in_specs, out_specs=out_specs, scratch_shapes=scratch_shapes)
    call = pl.pallas_call(
        full_body, name=name, out_shape=out_shape, input_output_aliases=aliases, compiler_params=compiler_params,
        **layout)

    def run(*args):
        res = call(*args, *[a for job in jobs for a in job.inputs])
        mine = res[0] if single else list(res[:n_out])
        return mine, [list(res[at:at + len(job.out_shape)]) for at, job in zip(out_at, jobs)]

    return run


def _fwd_in(x, g1, shards, order, jobs=()):
    t = x.shape[0]
    rows_per_step = min(IN_TILE, t)
    n_tiles = t // rows_per_step
    n = len(shards)
    halves = [s.shape[0] // 2 for s in shards]

    def body(order_ref, x_ref, g_ref, *refs):
        del order_ref
        ins, (z_ref, n_ref), outs = refs[:n], refs[n:n + 2], refs[n + 2:2 * n + 2]
        wbuf, nbuf, send, recv, local = refs[2 * n + 2:]
        s, i = pl.program_id(0), pl.program_id(1)
        x_, y_, c, chips = _place()
        near, far = _near_far(x_, y_, c)
        k_me = _chip_index(x_, y_)

        def block(w, chip, pc):
            return outs[w].at[_chip_index(*chip), pl.ds(pc * halves[w], halves[w]), :]

        def over_ici(w, j, landing):
            return pltpu.make_async_remote_copy(
                src_ref=ins[w].at[pl.ds(c * halves[w], halves[w]), :],
                dst_ref=block(w, chips[j] if landing else (x_, y_), c), send_sem=send.at[6 * w + j],
                recv_sem=recv.at[6 * w + j], device_id=(*chips[j], c), device_id_type=MESH)

        def onward(w, landing):
            blk = block(w, chips[2] if landing else near, c)
            return pltpu.make_async_remote_copy(
                src_ref=blk, dst_ref=blk, send_sem=send.at[6 * w + 2], recv_sem=recv.at[6 * w + 2],
                device_id=(*far, c), device_id_type=MESH)

        def to_sibling(w, j, landing):
            blk = block(w, chips[j], 1 - c if landing else c)
            return pltpu.make_async_remote_copy(
                src_ref=blk, dst_ref=blk, send_sem=send.at[6 * w + 3 + j], recv_sem=recv.at[6 * w + 3 + j],
                device_id=(x_, y_, 1 - c), device_id_type=MESH)

        own = [pltpu.make_async_copy(wbuf, outs[0].at[k_me], local.at[0])]
        own += [pltpu.make_async_copy(ins[w], outs[w].at[k_me], local.at[w]) for w in range(1, n)]

        @pl.when((s == 0) & (i == 0))
        def _():
            for j in range(2):
                for w in range(n):
                    over_ici(w, j, False).start()
            load = pltpu.make_async_copy(ins[0], wbuf, local.at[n])
            load.start()
            load.wait()
            for cp in own:
                cp.start()

        for j in range(N_CHIPS - 1):
            @pl.when((s == j + 1) & (i == 0))
            def _(j=j):
                if j == 0:
                    for k in range(2):
                        for w in range(n):
                            over_ici(w, k, True).wait_recv()
                    for w in range(n):
                        onward(w, False).start()
                    for k in range(2):
                        for w in range(n):
                            to_sibling(w, k, False).start()
                    own[0].wait()
                if j == 2:
                    for w in range(n):
                        onward(w, True).wait_recv()
                    for w in range(n):
                        to_sibling(w, j, False).start()
                for w in range(n):
                    to_sibling(w, j, True).wait_recv()
                load = pltpu.make_async_copy(outs[0].at[_chip_index(*chips[j])], wbuf, local.at[n])
                load.start()
                load.wait()

        rows = pl.ds(pl.multiple_of(i * rows_per_step, rows_per_step), rows_per_step)

        @pl.when(s == 0)
        def _():
            xhat, _ = _rms(x_ref[...])
            nrm = (xhat * g_ref[...]).astype(BF16)
            nbuf[rows, :] = nrm
            n_ref[...] = nrm

        z_ref[...] = _dot(nbuf[rows, :], wbuf[...]).astype(BF16)

        @pl.when((s == N_CHIPS - 1) & (i == n_tiles - 1))
        def _():
            for j in range(N_CHIPS - 1):
                for w in range(n):
                    (over_ici(w, j, False) if j < 2 else onward(w, False)).wait_send()
                    to_sibling(w, j, False).wait_send()
            for cp in own[1:]:
                cp.wait()

    once = lambda s, i, order: (jnp.where(s == 0, i, n_tiles - 1), 0)
    (z, n1, *stacked), job_outs = _fused_call(
        body, jobs, name="fwd_in", grid=(N_CHIPS, n_tiles), n_prefetch=1,
        in_specs=[pl.BlockSpec((rows_per_step, D_MODEL), once), _const((1, D_MODEL))] + [ANY] * n,
        out_specs=[pl.BlockSpec((rows_per_step, IN_SHARD), lambda s, i, order: (i, order[s])),
                   pl.BlockSpec((rows_per_step, D_MODEL), once)] + [ANY] * n,
        out_shape=[jax.ShapeDtypeStruct((t, D_IN), BF16), jax.ShapeDtypeStruct((t, D_MODEL), BF16)]
        + [jax.ShapeDtypeStruct((N_CHIPS,) + s.shape, s.dtype) for s in shards],
        scratch_shapes=[pltpu.VMEM(shards[0].shape, BF16), pltpu.VMEM((t, D_MODEL), BF16),
                        pltpu.SemaphoreType.DMA((6 * n,)),
                        pltpu.SemaphoreType.DMA((6 * n,)), pltpu.SemaphoreType.DMA((n + 1,))],
        compiler_params=_params(2), jobs_start_after=(1, 0),
    )(order, x, g1, *shards)
    return (z, n1, stacked), job_outs


def _fwd_lru(z, conv_w, conv_b, wr, br, wi, bi, lam, jobs=()):
    t = z.shape[0]

    def body(xa_ref, ga_ref, cw_ref, cb_ref, wr_ref, br_ref, wi_ref, bi_ref, lam_ref, ya_ref, h_ref, xc_ref, r_ref,
             ig_ref, tail_ref, carry_ref):
        @pl.when(pl.program_id(0) == 0)
        def _():
            tail_ref[...] = jnp.zeros_like(tail_ref)
            carry_ref[...] = jnp.zeros_like(carry_ref)

        xa = xa_ref[...].astype(F32)
        xc, r, ig, a, mult = _lru_gates(xa, tail_ref[...], cw_ref, cb_ref, wr_ref, br_ref, wi_ref, bi_ref, lam_ref)
        tail_ref[...] = xa[SEQ_TILE - SUBLANES:]
        xc_ref[...], r_ref[...], ig_ref[...] = xc, r, ig
        h, carry = _scan_forward(a, xc * ig * mult, carry_ref[...])
        carry_ref[...] = carry
        h_ref[...] = h
        ya_ref[...] = (h * _gelu(ga_ref[...].astype(F32))).astype(BF16)

    tile = lambda j: pl.BlockSpec((SEQ_TILE, D_MODEL), lambda i: (i, j))
    return _fused_call(
        body, jobs, name="fwd_lru", grid=(t // SEQ_TILE,),
        in_specs=[tile(0), tile(1), _const((CONV_WIDTH, D_MODEL)), _const((1, D_MODEL)),
                  _resident((HEADS, HEAD_DIM, HEAD_DIM)), _const((1, D_MODEL)),
                  _resident((HEADS, HEAD_DIM, HEAD_DIM)), _const((1, D_MODEL)), _const((1, D_MODEL))],
        out_specs=[tile(0)] * 5,
        out_shape=[jax.ShapeDtypeStruct((t, D_MODEL), BF16)] + [jax.ShapeDtypeStruct((t, D_MODEL), F32)] * 4,
        scratch_shapes=[pltpu.VMEM((SUBLANES, D_MODEL), F32), pltpu.VMEM((1, D_MODEL), F32)],
        compiler_params=_params(),
    )(z, z, conv_w, conv_b, wr, br, wi, bi, lam)


def _sgu_forward_parts(ub, vb, lg_ref, lb_ref):
    u, du = _gelu_and_grad(ub.astype(F32))
    vg, dvg = _gelu_and_grad(vb.astype(F32))
    mu = jnp.mean(vg, axis=-1, keepdims=True)
    d = vg - mu
    rstd = lax.rsqrt(jnp.mean(d * d, axis=-1, keepdims=True) + LN_EPS)
    vhat = d * rstd
    vn = (vhat * lg_ref[...] + lb_ref[...]).astype(BF16)
    return u, du, dvg, rstd, vhat, vn


def _causal_mask():
    rows = lax.broadcasted_iota(jnp.int32, (CHUNK, CHUNK), 0)
    cols = lax.broadcasted_iota(jnp.int32, (CHUNK, CHUNK), 1)
    return rows >= cols


def _fwd_sgu_merge(ya, z, x, ln_g, ln_b, w_s, bias_full, w_oa, w_ob, w_out, g2, jobs=()):
    t = x.shape[0]

    def body(ya_ref, ub_ref, vb_ref, m_ref, x_ref, lg_ref, lb_ref, ws_ref, bias_ref, woa_ref, wob_ref, wout_ref, g_ref,
             yb_ref, pa_ref, pb_ref, h1_ref, n2_ref):
        u, _, _, _, _, vn = _sgu_forward_parts(ub_ref[...], vb_ref[...], lg_ref, lb_ref)
        mask = _causal_mask()
        wm = [jnp.where(mask, ws_ref[g], 0.0).astype(BF16) for g in range(GROUPS)]
        for c in range(SEQ_TILE // CHUNK):
            rows = slice(c * CHUNK, (c + 1) * CHUNK)
            for g in range(GROUPS):
                cols = slice(g * GROUP_DIM, (g + 1) * GROUP_DIM)
                sp = _dot(wm[g], vn[rows, cols]) + bias_ref[:, cols]
                yb_ref[rows, cols] = (u[rows, cols] * sp).astype(BF16)
        pa = _dot(ya_ref[...], woa_ref[...])
        pb = _dot(yb_ref[...], wob_ref[...])
        pa_ref[...] = pa
        pb_ref[...] = pb
        merged = _gate(m_ref[:, :D_MODEL]) * pa + _gate(m_ref[:, D_MODEL:]) * pb
        h1 = x_ref[...] + _dot(merged.astype(BF16), wout_ref[...])
        h1_ref[...] = h1
        xhat, _ = _rms(h1)
        n2_ref[...] = (xhat * g_ref[...]).astype(BF16)

    tile = lambda j: pl.BlockSpec((SEQ_TILE, D_MODEL), lambda i: (i, j))
    sq = _resident((D_MODEL, D_MODEL))
    vec = _const((1, D_MODEL))
    bf, f32 = jax.ShapeDtypeStruct((t, D_MODEL), BF16), jax.ShapeDtypeStruct((t, D_MODEL), F32)
    return _fused_call(
        body, jobs, name="fwd_sgu_merge", grid=(t // SEQ_TILE,),
        in_specs=[tile(0), tile(2), tile(3), pl.BlockSpec((SEQ_TILE, 2 * D_MODEL), lambda i: (i, 2)), tile(0), vec, vec,
                  _const((GROUPS, CHUNK, CHUNK)), _const((CHUNK, D_MODEL)), sq, sq, sq, vec],
        out_specs=[tile(0)] * 5,
        out_shape=[bf, f32, f32, f32, bf],
        compiler_params=_params(),
    )(ya, z, z, z, x, ln_g, ln_b, w_s, bias_full, w_oa, w_ob, w_out, g2)


def _mlp(n2, h1, target, w_up_st, w_down, g2, g3, jobs=()):
    t = n2.shape[0]

    def body(n2_ref, h1_ref, tgt_ref, wup_ref, wdown_ref, g2_ref, g3_ref, act_ref, dup_ref, dh2b_ref, dh1_ref,
             loss_ref, dg3_ref, dg2_ref, relu_ref):
        @pl.when(pl.program_id(0) == 0)
        def _():
            for ref in (loss_ref, dg3_ref, dg2_ref):
                ref[...] = jnp.zeros_like(ref)

        n2 = n2_ref[...]
        h1 = h1_ref[...]
        h2 = h1
        for k in range(N_CHIPS):
            cols = slice(k * D_MODEL, (k + 1) * D_MODEL)
            r = jnp.maximum(_dot(n2, wup_ref[k]), 0.0)
            relu_ref[:, cols] = r
            act = (r * r).astype(BF16)
            act_ref[:, cols] = act
            h2 = h2 + _dot(act, wdown_ref[cols, :])
        xhat, r3 = _rms(h2)
        diff = xhat * g3_ref[...] - tgt_ref[...]
        sq = jnp.sum(diff * diff, axis=1, keepdims=True)
        loss_ref[...] = loss_ref[...] + (0.5 / D_MODEL) * jnp.sum(sq, axis=0, keepdims=True)
        dy = diff * (1.0 / D_MODEL)
        dg3_ref[...] = dg3_ref[...] + _col_sum(dy * xhat)
        dh2 = _rms_bwd(dy * g3_ref[...], xhat, r3)
        dh2b = dh2.astype(BF16)
        dh2b_ref[...] = dh2b
        dn2 = jnp.zeros((SEQ_TILE, D_MODEL), F32)
        for k in range(N_CHIPS):
            cols = slice(k * D_MODEL, (k + 1) * D_MODEL)
            dup = (_dot_nt(dh2b, wdown_ref[cols, :]) * (2.0 * relu_ref[:, cols])).astype(BF16)
            dup_ref[:, cols] = dup
            dn2 = dn2 + _dot_nt(dup, wup_ref[k])
        xhat, r2 = _rms(h1)
        dg2_ref[...] = dg2_ref[...] + _col_sum(dn2 * xhat)
        dh1_ref[...] = dh2 + _rms_bwd(dn2 * g2_ref[...], xhat, r2)

    tile = pl.BlockSpec((SEQ_TILE, D_MODEL), lambda i: (i, 0))
    wide = pl.BlockSpec((SEQ_TILE, D_FF), lambda i: (i, 0))
    vec = _const((1, D_MODEL))
    vec_shape = jax.ShapeDtypeStruct((1, D_MODEL), F32)
    return _fused_call(
        body, jobs, name="mlp", grid=(t // SEQ_TILE,),
        in_specs=[tile, tile, tile, _resident((N_CHIPS, D_MODEL, D_MODEL)), _resident((D_FF, D_MODEL)), vec, vec],
        out_specs=[wide, wide, tile, tile, _const((SUBLANES, 128)), vec, vec],
        out_shape=[jax.ShapeDtypeStruct((t, D_FF), BF16), jax.ShapeDtypeStruct((t, D_FF), BF16),
                   jax.ShapeDtypeStruct((t, D_MODEL), BF16), jax.ShapeDtypeStruct((t, D_MODEL), F32),
                   jax.ShapeDtypeStruct((SUBLANES, 128), F32), vec_shape, vec_shape],
        scratch_shapes=[pltpu.VMEM((SEQ_TILE, D_FF), F32)],
        compiler_params=_params(),
    )(n2, h1, target, w_up_st, w_down, g2, g3)


def _bwd_mix(dh1, pa, pb, z, h, xc, r, ig, w_oa, w_ob, w_out, ln_g, ln_b, w_s, bias_full, conv_w, wr, wi, lam, jobs=()):
    t = dh1.shape[0]
    n_tiles = t // SEQ_TILE
    per_tile = SEQ_TILE // SUBLANES

    def merge_part(dh1_ref, pa_ref, pb_ref, m_ref, woa_ref, wob_ref, wout_ref, dz_ref, dya_ref, dyb_ref, mg_ref,
                   dpa_ref, dpb_ref, dh1b_ref):
        dh1b = dh1_ref[...].astype(BF16)
        dh1b_ref[...] = dh1b
        dm = _dot_nt(dh1b, wout_ref[...])
        pa = pa_ref[...]
        pb = pb_ref[...]
        sa = _gate(m_ref[:, :D_MODEL])
        sb = _gate(m_ref[:, D_MODEL:])
        mg_ref[...] = (sa * pa + sb * pb).astype(BF16)
        dpa = dm * sa
        dpb = dm * sb
        dz_ref[:, :D_MODEL] = ((dpa * pa) * (1.0 - sa)).astype(BF16)
        dz_ref[:, D_MODEL:] = ((dpb * pb) * (1.0 - sb)).astype(BF16)
        dpa = dpa.astype(BF16)
        dpb = dpb.astype(BF16)
        dpa_ref[...] = dpa
        dpb_ref[...] = dpb
        dya_ref[...] = _dot_nt(dpa, woa_ref[...])
        dyb_ref[...] = _dot_nt(dpb, wob_ref[...])

    def sgu_part(dyb_ref, ub_ref, vb_ref, lg_ref, lb_ref, ws_ref, bias_ref, dz_ref, dlg_ref, dlb_ref, dws_ref, dbs_ref,
                 dvn_ref, dsp_acc):
        i = pl.program_id(0)

        @pl.when(i == 0)
        def _():
            dlg_ref[...] = jnp.zeros_like(dlg_ref)
            dlb_ref[...] = jnp.zeros_like(dlb_ref)
            dws_ref[...] = jnp.zeros_like(dws_ref)
            dsp_acc[...] = jnp.zeros_like(dsp_acc)

        u, du, dvg, rstd, vhat, vn = _sgu_forward_parts(ub_ref[...], vb_ref[...], lg_ref, lb_ref)
        dyb = dyb_ref[...]
        mask = _causal_mask()
        wm = [jnp.where(mask, ws_ref[g], 0.0).astype(BF16) for g in range(GROUPS)]
        for c in range(SEQ_TILE // CHUNK):
            rows = slice(c * CHUNK, (c + 1) * CHUNK)
            for g in range(GROUPS):
                cols = slice(g * GROUP_DIM, (g + 1) * GROUP_DIM)
                vn_blk = vn[rows, cols]
                sp = _dot(wm[g], vn_blk) + bias_ref[:, cols]
                dyb_blk = dyb[rows, cols]
                dz_ref[rows, cols] = (dyb_blk * sp * du[rows, cols]).astype(BF16)
                dsp = dyb_blk * u[rows, cols]
                dsp_acc[:, cols] = dsp_acc[:, cols] + dsp
                dspb = dsp.astype(BF16)
                dvn_ref[rows, cols] = _dot_tn(wm[g], dspb)
                wcols = slice(g * CHUNK, (g + 1) * CHUNK)
                dws_ref[:, wcols] = dws_ref[:, wcols] + jnp.where(mask, _dot_nt(dspb, vn_blk), 0.0)
        dvn = dvn_ref[...]
        dlg_ref[...] = dlg_ref[...] + _col_sum(dvn * vhat)
        dlb_ref[...] = dlb_ref[...] + _col_sum(dvn)
        dvhat = dvn * lg_ref[...]
        dvgel = rstd * (dvhat - jnp.mean(dvhat, axis=-1, keepdims=True)
                        - vhat * jnp.mean(dvhat * vhat, axis=-1, keepdims=True))
        dz_ref[:, D_MODEL:] = (dvgel * dvg).astype(BF16)

        @pl.when(i == n_tiles - 1)
        def _():
            lane = lax.broadcasted_iota(jnp.int32, (CHUNK, 128), 1)
            out = jnp.zeros((CHUNK, 128), F32)
            for g in range(GROUPS):
                s = jnp.sum(dsp_acc[:, g * GROUP_DIM:(g + 1) * GROUP_DIM], axis=1, keepdims=True)
                out = out + jnp.where(lane == g, s, 0.0)
            dbs_ref[...] = out

    def lru_part(dya_ref, xa_ref, ga_ref, h_ref, h_prev_ref, xc_ref, r_ref, ig_ref, cw_ref, wr_ref, wi_ref, lam_ref,
                 dz_ref, dcw_ref, dcb_ref, dwr_ref, dbr_ref, dwi_ref, dbi_ref, dlam_ref, lam_carry, dxc_head):
        i = pl.program_id(0)

        @pl.when(i == 0)
        def _():
            for ref in (dcw_ref, dcb_ref, dwr_ref, dbr_ref, dwi_ref, dbi_ref, dlam_ref, lam_carry, dxc_head):
                ref[...] = jnp.zeros_like(ref)

        first_tile = i == n_tiles - 1
        h_tail = jnp.where(first_tile, 0.0, h_prev_ref[...])
        xc, r, ig = xc_ref[...], r_ref[...], ig_ref[...]
        xcb = xc.astype(BF16)
        sp, a, mult, inv_mult = _decay(r, lam_ref)
        h = h_ref[...]
        h_prev = _shift_down(h, h_tail, 1)
        dya = dya_ref[...]
        gg, dgg = _gelu_and_grad(ga_ref[...].astype(F32))
        dz_ref[:, D_MODEL:] = (dya * h * dgg).astype(BF16)
        ones = jnp.ones((SUBLANES, D_MODEL), F32)
        lam_t, lam_first = _scan_backward(_shift_up(a, ones, 1), dya * gg, lam_carry[...])
        lam_carry[...] = a[0:1] * lam_first
        lam_ig = lam_t * ig
        dxc_direct = lam_ig * mult
        dmult = lam_ig * xc
        dla = a * (lam_t * h_prev - (dmult * a) * inv_mult)
        dla_r = dla * r
        dlam_ref[...] = dlam_ref[...] + _col_sum(dla_r) * (LRU_C * jax.nn.sigmoid(-lam_ref[...]))
        dpr = (dla_r * ((-LRU_C) * sp)) * (1.0 - r)
        dpi = (dxc_direct * xc) * (1.0 - ig)
        dbr_ref[...] = dbr_ref[...] + _col_sum(dpr)
        dbi_ref[...] = dbi_ref[...] + _col_sum(dpi)
        dprb = dpr.astype(BF16)
        dpib = dpi.astype(BF16)
        dxc_gate = []
        for hd in range(HEADS):
            cols = slice(hd * HEAD_DIM, (hd + 1) * HEAD_DIM)
            dxc_gate.append(_dot_nt(dprb[:, cols], wr_ref[hd]) + _dot_nt(dpib[:, cols], wi_ref[hd]))
            dwr_ref[hd] = dwr_ref[hd] + _dot_tn(xcb[:, cols], dprb[:, cols])
            dwi_ref[hd] = dwi_ref[hd] + _dot_tn(xcb[:, cols], dpib[:, cols])
        dxc = dxc_direct + jnp.concatenate(dxc_gate, axis=1)
        dcb_ref[...] = dcb_ref[...] + _col_sum(dxc)
        cw = cw_ref[...]
        head = dxc_head[...]
        xa = xa_ref[...].astype(F32)
        dxa = cw[0:1] * dxc
        dcw_ref[0:1, :] = dcw_ref[0:1, :] + _col_sum(dxc * xa)
        for k in range(1, CONV_WIDTH):
            dxc_k = _shift_up(dxc, head, k)
            dxa = dxa + cw[k:k + 1] * dxc_k
            dcw_ref[k:k + 1, :] = dcw_ref[k:k + 1, :] + _col_sum(dxc_k * xa)
        dxc_head[...] = dxc[0:SUBLANES]
        dz_ref[:, :D_MODEL] = dxa.astype(BF16)

    def body(dh1_ref, pa_ref, pb_ref, z_ref, h_ref, h_prev_ref, xc_ref, r_ref, ig_ref, woa_ref, wob_ref, wout_ref,
             lg_ref, lb_ref, ws_ref, bias_ref, cw_ref, wr_ref, wi_ref, lam_ref, dz_ref, mg_ref, dpa_ref, dpb_ref,
             dh1b_ref, dlg_ref, dlb_ref, dws_ref, dbs_ref, dcw_ref, dcb_ref, dwr_ref, dbr_ref, dwi_ref, dbi_ref,
             dlam_ref, dya_ref, dyb_ref, dvn_ref, dsp_acc, lam_carry, dxc_head):
        def cols(ref, first, count):
            return ref.at[:, pl.ds(first * D_MODEL, count * D_MODEL)]

        merge_part(dh1_ref, pa_ref, pb_ref, cols(z_ref, 4, 2), woa_ref, wob_ref, wout_ref, cols(dz_ref, 4, 2), dya_ref,
                   dyb_ref, mg_ref, dpa_ref, dpb_ref, dh1b_ref)
        sgu_part(dyb_ref, cols(z_ref, 2, 1), cols(z_ref, 3, 1), lg_ref, lb_ref, ws_ref, bias_ref, cols(dz_ref, 2, 2),
                 dlg_ref, dlb_ref, dws_ref, dbs_ref, dvn_ref, dsp_acc)
        lru_part(dya_ref, cols(z_ref, 0, 1), cols(z_ref, 1, 1), h_ref, h_prev_ref, xc_ref, r_ref, ig_ref, cw_ref, wr_ref,
                 wi_ref, lam_ref, cols(dz_ref, 0, 2), dcw_ref, dcb_ref, dwr_ref, dbr_ref, dwi_ref, dbi_ref, dlam_ref,
                 lam_carry, dxc_head)

    rev = lambda i: n_tiles - 1 - i
    tile = pl.BlockSpec((SEQ_TILE, D_MODEL), lambda i: (rev(i), 0))
    row = pl.BlockSpec((SEQ_TILE, D_IN), lambda i: (rev(i), 0))
    prev8 = pl.BlockSpec((SUBLANES, D_MODEL), lambda i: (jnp.maximum(rev(i) * per_tile - 1, 0), 0))
    vec = _const((1, D_MODEL))
    sq = _resident((D_MODEL, D_MODEL))
    gate_w = _resident((HEADS, HEAD_DIM, HEAD_DIM))
    gate_acc = _const((HEADS, HEAD_DIM, HEAD_DIM))
    vec_shape = jax.ShapeDtypeStruct((1, D_MODEL), F32)
    gate_shape = jax.ShapeDtypeStruct((HEADS, HEAD_DIM, HEAD_DIM), F32)
    act_bf = jax.ShapeDtypeStruct((t, D_MODEL), BF16)
    return _fused_call(
        body, jobs, name="bwd_mix", grid=(n_tiles,),
        in_specs=[tile, tile, tile, row, tile, prev8, tile, tile, tile, sq, sq, sq, vec, vec,
                  _const((GROUPS, CHUNK, CHUNK)), _const((CHUNK, D_MODEL)), _const((CONV_WIDTH, D_MODEL)), gate_w, gate_w,
                  vec],
        out_specs=[row, tile, tile, tile, tile, vec, vec, _const((CHUNK, GROUPS * CHUNK)), _const((CHUNK, 128)),
                   _const((SUBLANES, D_MODEL)), vec, gate_acc, vec, gate_acc, vec, vec],
        out_shape=[jax.ShapeDtypeStruct((t, D_IN), BF16), act_bf, act_bf, act_bf, act_bf, vec_shape, vec_shape,
                   jax.ShapeDtypeStruct((CHUNK, GROUPS * CHUNK), F32), jax.ShapeDtypeStruct((CHUNK, 128), F32),
                   jax.ShapeDtypeStruct((SUBLANES, D_MODEL), F32), vec_shape, gate_shape, vec_shape, gate_shape,
                   vec_shape, vec_shape],
        scratch_shapes=[pltpu.VMEM((SEQ_TILE, D_MODEL), F32), pltpu.VMEM((SEQ_TILE, D_MODEL), F32),
                        pltpu.VMEM((SEQ_TILE, D_MODEL), F32), pltpu.VMEM((CHUNK, D_MODEL), F32),
                        pltpu.VMEM((1, D_MODEL), F32), pltpu.VMEM((SUBLANES, D_MODEL), F32)],
        compiler_params=_params(),
    )(dh1, pa, pb, z, h, h, xc, r, ig, w_oa, w_ob, w_out, ln_g, ln_b, w_s, bias_full, conv_w, wr, wi, lam)


def _bwd_in(dz, x, dh1, w_in_st, g1, jobs=()):
    t = x.shape[0]

    def body(dz_ref, x_ref, dh1_ref, w_ref, g_ref, dx_ref, dg1_ref):
        @pl.when(pl.program_id(0) == 0)
        def _():
            dg1_ref[...] = jnp.zeros_like(dg1_ref)

        dn1 = jnp.zeros((MM_TILE, D_MODEL), F32)
        for k in range(N_CHIPS):
            dn1 = dn1 + _dot_nt(dz_ref[:, k * IN_SHARD:(k + 1) * IN_SHARD], w_ref[k])
        xhat, r1 = _rms(x_ref[...])
        dg1_ref[...] = dg1_ref[...] + _col_sum(dn1 * xhat)
        dx_ref[...] = dh1_ref[...] + _rms_bwd(dn1 * g_ref[...], xhat, r1)

    tile = pl.BlockSpec((MM_TILE, D_MODEL), lambda i: (i, 0))
    return _fused_call(
        body, jobs, name="bwd_in", grid=(t // MM_TILE,),
        in_specs=[pl.BlockSpec((MM_TILE, D_IN), lambda i: (i, 0)), tile, tile,
                  _resident((N_CHIPS, D_MODEL, IN_SHARD)), _const((1, D_MODEL))],
        out_specs=[tile, _const((1, D_MODEL))],
        out_shape=[jax.ShapeDtypeStruct((t, D_MODEL), F32), jax.ShapeDtypeStruct((1, D_MODEL), F32)],
        compiler_params=_params(),
    )(dz, x, dh1, w_in_st, g1)


def _weight_grad(name, a, b, n_blocks, a_varies, b_varies, width, jobs=()):
    t = a.shape[0]
    rows = min(DW_TILE, t)
    n_t = t // rows

    def body(a_ref, b_ref, o_ref, acc_ref):
        s = pl.program_id(1)
        part = _dot_tn(a_ref[...], b_ref[...])

        @pl.when(s == 0)
        def _():
            acc_ref[...] = part

        @pl.when(s > 0)
        def _():
            acc_ref[...] = acc_ref[...] + part

        @pl.when(s == n_t - 1)
        def _():
            o_ref[...] = acc_ref[...].astype(BF16)

    return _fused_call(
        body, jobs, name=name, grid=(n_blocks, n_t),
        in_specs=[pl.BlockSpec((rows, D_MODEL), (lambda j, s: (s, j)) if a_varies else (lambda j, s: (s, 0))),
                  pl.BlockSpec((rows, width), (lambda j, s: (s, j)) if b_varies else (lambda j, s: (s, 0)))],
        out_specs=pl.BlockSpec((None, D_MODEL, width), lambda j, s: (j, 0, 0)),
        out_shape=jax.ShapeDtypeStruct((n_blocks, D_MODEL, width), BF16),
        scratch_shapes=[pltpu.VMEM((D_MODEL, width), F32)],
        compiler_params=_params(2),
    )(a, b)


def _weight_grads_square(name, pairs, jobs=()):
    n = len(pairs)
    t = pairs[0][0].shape[0]
    rows = min(2 * MM_TILE, t)
    n_t = t // rows

    def body(*refs):
        ins, outs, accs = refs[:2 * n], refs[2 * n:3 * n], refs[3 * n:]
        s = pl.program_id(0)
        for k in range(n):
            part = _dot_tn(ins[2 * k][...], ins[2 * k + 1][...])

            @pl.when(s == 0)
            def _(k=k, part=part):
                accs[k][...] = part

            @pl.when(s > 0)
            def _(k=k, part=part):
                accs[k][...] = accs[k][...] + part

            @pl.when(s == n_t - 1)
            def _(k=k):
                outs[k][...] = accs[k][...].astype(BF16)

    tile = pl.BlockSpec((rows, D_MODEL), lambda s: (s, 0))
    return _fused_call(
        body, jobs, name=name, grid=(n_t,), in_specs=[tile] * (2 * n), out_specs=[_const((D_MODEL, D_MODEL))] * n,
        out_shape=[jax.ShapeDtypeStruct((D_MODEL, D_MODEL), BF16)] * n,
        scratch_shapes=[pltpu.VMEM((D_MODEL, D_MODEL), F32)] * n,
        compiler_params=_params(),
    )(*[x for pair in pairs for x in pair])


def _place():
    x, y, c = lax.axis_index("x"), lax.axis_index("y"), lax.axis_index("c")
    other_chips = [(1 - x, y), (x, 1 - y), (1 - x, 1 - y)]
    return x, y, c, other_chips


def _chip_index(px, py):
    return 2 * px + py


ANY = pl.BlockSpec(memory_space=pl.ANY)
SIBLING = ((0, 0, 1),)
NEIGHBOURS = ((1, 0, 0), (0, 1, 0))
OTHER_CHIPS = NEIGHBOURS + ((1, 1, 0),)


def _near_far(x, y, c):
    return (x ^ (1 - c), y ^ c), (x ^ c, y ^ (1 - c))


def _gather_near_job(shards):
    n = len(shards)
    halves = [s.shape[0] // 2 for s in shards]

    def copies(ins, outs, send, recv, local):
        x, y, c, _ = _place()
        near, _ = _near_far(x, y, c)

        def block(w, chip, pc):
            return outs[w].at[_chip_index(*chip), pl.ds(pc * halves[w], halves[w]), :]

        def copy(w, k, chip, pc, to, src=None):
            return pltpu.make_async_remote_copy(
                src_ref=block(w, chip, pc) if src is None else src, dst_ref=block(w, chip, pc),
                send_sem=send.at[2 * w + k], recv_sem=recv.at[2 * w + k], device_id=to, device_id_type=MESH)

        sends, arrivals, own = [], [], []
        for w in range(n):
            src = ins[w].at[pl.ds(c * halves[w], halves[w]), :]
            own.append(pltpu.make_async_copy(src, block(w, (x, y), c), local.at[w]))
            sends += [copy(w, 0, (x, y), c, (*near, c), src), copy(w, 1, (x, y), c, (x, y, 1 - c), src)]
            arrivals += [copy(w, 0, near, c, (x, y, c)), copy(w, 1, (x, y), 1 - c, (x, y, c))]
        return sends, arrivals, own

    return _Job(shards, [jax.ShapeDtypeStruct((N_CHIPS,) + s.shape, s.dtype) for s in shards], 2 * n, copies,
                NEIGHBOURS + SIBLING, n_local=n)


def _gather_far_job(stacked):
    n = len(stacked)
    halves = [s.shape[1] // 2 for s in stacked]

    def copies(ins, outs, send, recv, local):
        del ins, local
        x, y, c, _ = _place()
        near, far = _near_far(x, y, c)

        def copy(w, k, chip):
            blk = outs[w].at[_chip_index(*chip), pl.ds(c * halves[w], halves[w]), :]
            return pltpu.make_async_remote_copy(
                src_ref=blk, dst_ref=blk, send_sem=send.at[2 * w + k], recv_sem=recv.at[2 * w + k],
                device_id=(*far, c), device_id_type=MESH)

        sends = [copy(w, k, chip) for w in range(n) for k, chip in enumerate(((x, y), near))]
        arrivals = [copy(w, k, chip) for w in range(n) for k, chip in enumerate((far, (1 - x, 1 - y)))]
        return sends, arrivals, []

    return _Job(stacked, [jax.ShapeDtypeStruct(s.shape, s.dtype) for s in stacked], 2 * n, copies, NEIGHBOURS,
                aliases={w: w for w in range(n)})


def _gather_pass_job(stacked):
    n = len(stacked)
    halves = [s.shape[1] // 2 for s in stacked]

    def copies(ins, outs, send, recv, local):
        del ins, local
        x, y, c, chips = _place()

        def copy(w, j, chip, pc, to):
            blk = outs[w].at[_chip_index(*chip), pl.ds(pc * halves[w], halves[w]), :]
            return pltpu.make_async_remote_copy(
                src_ref=blk, dst_ref=blk, send_sem=send.at[3 * w + j], recv_sem=recv.at[3 * w + j], device_id=to,
                device_id_type=MESH)

        sends = [copy(w, j, chip, c, (x, y, 1 - c)) for w in range(n) for j, chip in enumerate(chips)]
        arrivals = [copy(w, j, chip, 1 - c, (x, y, c)) for w in range(n) for j, chip in enumerate(chips)]
        return sends, arrivals, []

    return _Job(stacked, [jax.ShapeDtypeStruct(s.shape, s.dtype) for s in stacked], 3 * n, copies, SIBLING,
                aliases={w: w for w in range(n)})


def _gather_small_job(block):
    def copies(ins, outs, send, recv, local):
        x, y, c, chips = _place()

        def copy(j, chip_from, to):
            return pltpu.make_async_remote_copy(
                src_ref=ins[0], dst_ref=outs[0].at[_chip_index(*chip_from)], send_sem=send.at[j],
                recv_sem=recv.at[j], device_id=to, device_id_type=MESH)

        own = [pltpu.make_async_copy(ins[0], outs[0].at[_chip_index(x, y)], local.at[0])]
        sends = [copy(j, (x, y), (*chip, c)) for j, chip in enumerate(chips)]
        arrivals = [copy(j, chip, (x, y, c)) for j, chip in enumerate(chips)]
        return sends, arrivals, own

    return _Job([block], [jax.ShapeDtypeStruct((N_CHIPS,) + block.shape, block.dtype)], 3, copies, OTHER_CHIPS,
                n_local=1)


def _pair_send_job(grads):
    n = len(grads)
    halves = [g.shape[1] // 2 for g in grads]

    def copies(ins, outs, send, recv, local):
        del local
        x, y, c, _ = _place()
        sends = [pltpu.make_async_remote_copy(
            src_ref=ins[w].at[:, pl.ds((1 - c) * halves[w], halves[w]), :], dst_ref=outs[w], send_sem=send.at[w],
            recv_sem=recv.at[w], device_id=(x, y, 1 - c), device_id_type=MESH) for w in range(n)]
        return sends, sends, []

    return _Job(grads, [jax.ShapeDtypeStruct((N_CHIPS, h, g.shape[2]), g.dtype) for g, h in zip(grads, halves)], n,
                copies, SIBLING)


ROW_STEPS = 4


def _pair_add(name, core, mine, theirs):
    n = len(mine)

    def body(core_ref, *refs):
        del core_ref
        for a_ref, b_ref, o_ref in zip(refs[:n], refs[n:2 * n], refs[2 * n:]):
            o_ref[...] = (a_ref[...].astype(F32) + b_ref[...].astype(F32)).astype(BF16)

    half = lambda a: pl.BlockSpec((None, None) + a.shape[2:], lambda k, core_ref: (k, core_ref[0], 0, 0))
    block = lambda b: pl.BlockSpec((None,) + b.shape[1:], lambda k, core_ref: (k, 0, 0))
    return pl.pallas_call(
        body, name=name,
        grid_spec=pltpu.PrefetchScalarGridSpec(
            num_scalar_prefetch=1, grid=(N_CHIPS,),
            in_specs=[half(a) for a in mine] + [block(b) for b in theirs], out_specs=[block(b) for b in theirs]),
        out_shape=[jax.ShapeDtypeStruct(b.shape, BF16) for b in theirs],
        compiler_params=_params(),
    )(core, *mine, *theirs)


def _sequencer_call(name, collective_id, job):
    steps, peers = job.phases, job.peers
    ins = [jax.new_ref(a, memory_space=pltpu.MemorySpace.HBM) for a in job.inputs]
    outs = [ins[{o: i for i, o in job.aliases.items()}[k]] if k in job.aliases.values()
            else jax.empty_ref(shape, memory_space=pltpu.MemorySpace.HBM) for k, shape in enumerate(job.out_shape)]
    sems = [pltpu.SemaphoreType.DMA((n,)) for step in steps for n in (step.n_sem, step.n_sem, max(step.n_local, 1))]

    @pl.kernel(mesh=plsc.ScalarSubcoreMesh(axis_name="sequencer", num_cores=1), name=name, scratch_types=tuple(sems),
               compiler_params=pltpu.CompilerParams(collective_id=collective_id))
    def launch(*sem_refs):
        x, y, c, _ = _place()
        barrier = pltpu.get_barrier_semaphore()
        for dx, dy, dc in peers:
            pl.semaphore_signal(barrier, inc=1, device_id=(x ^ dx, y ^ dy, c ^ dc), device_id_type=MESH)
        pl.semaphore_wait(barrier, len(peers))
        for k, step in enumerate(steps):
            sends, arrivals, own = step.copies(ins if k == 0 else outs, outs, *sem_refs[3 * k:3 * k + 3])
            for cp in own + sends:
                cp.start()
            for cp in arrivals:
                cp.wait_recv()
            for cp in sends:
                cp.wait_send()
            for cp in own:
                cp.wait()

    launch()
    return [ref[...] for ref in outs]


def _chip_exchange_job(sums):
    n = len(sums)

    def copies(ins, outs, send, recv, local):
        del local
        _, _, c, chips = _place()
        sends = [pltpu.make_async_remote_copy(
            src_ref=ins[w].at[_chip_index(*chip)], dst_ref=outs[w].at[j], send_sem=send.at[3 * w + j],
            recv_sem=recv.at[3 * w + j], device_id=(*chip, c), device_id_type=MESH)
            for w in range(n) for j, chip in enumerate(chips)]
        return sends, sends, []

    return _Job(sums, [jax.ShapeDtypeStruct((N_CHIPS - 1,) + s.shape[1:], s.dtype) for s in sums], 3 * n, copies,
                OTHER_CHIPS)


def _chip_sum(name, place, mine, theirs):
    n = len(mine)

    def body(place_ref, *refs):
        del place_ref
        for p_ref, q_ref, o_ref in zip(refs[:n], refs[n:2 * n], refs[2 * n:]):
            acc = p_ref[...].astype(F32)
            for j in range(N_CHIPS - 1):
                acc = acc + q_ref[j].astype(F32)
            o_ref[...] = acc

    def block(p, lead, pick):
        return pl.BlockSpec((lead, p.shape[1] // ROW_STEPS, p.shape[2]), lambda r, place_ref: (pick(place_ref), r, 0))

    return pl.pallas_call(
        body, name=name,
        grid_spec=pltpu.PrefetchScalarGridSpec(
            num_scalar_prefetch=1, grid=(ROW_STEPS,),
            in_specs=[block(p, None, lambda place_ref: place_ref[0]) for p in mine]
            + [block(p, N_CHIPS - 1, lambda place_ref: 0) for p in mine],
            out_specs=[block(p, None, lambda place_ref: place_ref[1]) for p in mine]),
        out_shape=[jax.ShapeDtypeStruct((2,) + p.shape[1:], F32) for p in mine],
        compiler_params=_params(),
    )(place, *mine, *theirs)


def _share_job(bufs):
    n = len(bufs)

    def copies(ins, outs, send, recv, local):
        del ins, local
        x, y, c, _ = _place()

        def copy(w, half):
            return pltpu.make_async_remote_copy(
                src_ref=outs[w].at[half], dst_ref=outs[w].at[half], send_sem=send.at[w], recv_sem=recv.at[w],
                device_id=(x, y, 1 - c), device_id_type=MESH)

        return [copy(w, c) for w in range(n)], [copy(w, 1 - c) for w in range(n)], []

    return _Job(bufs, [jax.ShapeDtypeStruct(b.shape, b.dtype) for b in bufs], n, copies, SIBLING,
                aliases={w: w for w in range(n)})


SMALL_ROWS = 24
ROW_G1, ROW_CW, ROW_CB, ROW_BR, ROW_BI, ROW_LAM, ROW_LG, ROW_LB, ROW_G2, ROW_G3, ROW_LOSS, ROW_BS = (
    0, 1, 5, 6, 7, 8, 9, 10, 11, 12, 13, 16)
N_DEV = 8


def _pack_small(dcw, dcb, dbr, dbi, dlam, dlg, dlb, dg2, dg3, loss, dbs):
    def body(dcw_ref, dcb_ref, dbr_ref, dbi_ref, dlam_ref, dlg_ref, dlb_ref, dg2_ref, dg3_ref, loss_ref, dbs_ref, out):
        out[...] = jnp.zeros((SMALL_ROWS, D_MODEL), F32)
        for row, ref in ((ROW_CB, dcb_ref), (ROW_BR, dbr_ref), (ROW_BI, dbi_ref), (ROW_LAM, dlam_ref),
                         (ROW_LG, dlg_ref), (ROW_LB, dlb_ref), (ROW_G2, dg2_ref), (ROW_G3, dg3_ref)):
            out[row:row + 1, :] = ref[...]
        out[ROW_CW:ROW_CW + CONV_WIDTH, :] = dcw_ref[0:CONV_WIDTH, :]
        out[ROW_LOSS:ROW_LOSS + 1, 0:128] = loss_ref[0:1, :]
        out[ROW_BS:ROW_BS + GROUPS, 0:128] = jnp.transpose(dbs_ref[...])[0:GROUPS, :]

    vm = pl.BlockSpec(memory_space=pltpu.VMEM)
    return pl.pallas_call(
        body, name="pack_small", in_specs=[vm] * 11, out_specs=vm,
        out_shape=jax.ShapeDtypeStruct((SMALL_ROWS, D_MODEL), F32),
    )(dcw, dcb, dbr, dbi, dlam, dlg, dlb, dg2, dg3, loss, dbs)


def _gather_all_job(blocks):
    n = len(blocks)
    flips = [(dx, dy, dc) for dx in (0, 1) for dy in (0, 1) for dc in (0, 1)][1:]

    def copies(ins, outs, send, recv, local):
        x, y, c, _ = _place()
        me = 4 * x + 2 * y + c
        sends, arrivals, own = [], [], []
        for w in range(n):
            own.append(pltpu.make_async_copy(ins[w], outs[w].at[me], local.at[w]))
            for k, (dx, dy, dc) in enumerate(flips):
                peer = (x ^ dx, y ^ dy, c ^ dc)
                sem = dict(send_sem=send.at[7 * w + k], recv_sem=recv.at[7 * w + k])
                sends.append(pltpu.make_async_remote_copy(
                    src_ref=ins[w], dst_ref=outs[w].at[me], device_id=peer, device_id_type=MESH, **sem))
                arrivals.append(pltpu.make_async_remote_copy(
                    src_ref=ins[w], dst_ref=outs[w].at[4 * peer[0] + 2 * peer[1] + peer[2]], device_id=peer,
                    device_id_type=MESH, **sem))
        return sends, arrivals, own

    return _Job(blocks, [jax.ShapeDtypeStruct((N_DEV,) + b.shape, b.dtype) for b in blocks], 7 * n, copies,
                OTHER_CHIPS + SIBLING + tuple((dx, dy, 1) for dx, dy, _ in OTHER_CHIPS), n_local=n)


def _sum_small(vec_all, ws_all, dg1_all):
    def body(vec_ref, ws_ref, dg1_ref, vec_out, ws_out):
        vec, ws, dg1 = vec_ref[0], ws_ref[0], dg1_ref[0]
        for d in range(1, N_DEV):
            vec, ws, dg1 = vec + vec_ref[d], ws + ws_ref[d], dg1 + dg1_ref[d]
        vec_out[...] = vec
        vec_out[ROW_G1:ROW_G1 + 1, :] = dg1
        ws_out[...] = ws

    vm = pl.BlockSpec(memory_space=pltpu.VMEM)
    return pl.pallas_call(
        body, name="sum_small", in_specs=[vm] * 3, out_specs=[vm, vm],
        out_shape=[jax.ShapeDtypeStruct(vec_all.shape[1:], F32), jax.ShapeDtypeStruct(ws_all.shape[1:], F32)],
    )(vec_all, ws_all, dg1_all)


def _adamw_math(w, g, m, v):
    m = ADAM_B1 * m + (1.0 - ADAM_B1) * g
    v = ADAM_B2 * v + (1.0 - ADAM_B2) * (g * g)
    m_hat = m / (1.0 - ADAM_B1 ** ADAM_STEP)
    v_hat = v / (1.0 - ADAM_B2 ** ADAM_STEP)
    delta = (-ADAM_LR) * (m_hat / (jnp.sqrt(v_hat) + ADAM_EPS) + ADAM_WD * w)
    return delta, m, v


def _adamw(name, gs, ws, ms, vs):
    n = len(ws)

    def body(*refs):
        ins, outs = refs[:4 * n], refs[4 * n:]
        for p in range(n):
            g_ref, w_ref, m_ref, v_ref = ins[p::n]
            outs[3 * p][...], outs[3 * p + 1][...], outs[3 * p + 2][...] = _adamw_math(
                w_ref[...], g_ref[...], m_ref[...], v_ref[...])

    blocks = [pl.BlockSpec((w.shape[0] // ROW_STEPS, w.shape[1]), lambda r: (r, 0)) for w in ws]
    out = pl.pallas_call(
        body, name=name, grid=(ROW_STEPS,), in_specs=blocks * 4, out_specs=[b for b in blocks for _ in range(3)],
        out_shape=[jax.ShapeDtypeStruct(w.shape, F32) for w in ws for _ in range(3)], compiler_params=_params(),
    )(*gs, *ws, *ms, *vs)
    return [tuple(out[3 * p:3 * p + 3]) for p in range(n)]


def _adamw_small(grads, ws, ms, vs):
    n = len(grads)

    def body(*refs):
        g_refs, w_refs, m_refs, v_refs = refs[:n], refs[n:2 * n], refs[2 * n:3 * n], refs[3 * n:4 * n]
        outs = refs[4 * n:]
        for p in range(n):
            d, nm, nv = _adamw_math(w_refs[p][...], g_refs[p][...], m_refs[p][...], v_refs[p][...])
            outs[p][...] = d
            outs[n + p][...] = nm
            outs[2 * n + p][...] = nv

    vm = pl.BlockSpec(memory_space=pltpu.VMEM)
    shapes = [jax.ShapeDtypeStruct(w.shape, F32) for w in ws]
    out = pl.pallas_call(
        body, name="adamw_small", in_specs=[vm] * (4 * n), out_specs=[vm] * (3 * n), out_shape=shapes * 3,
    )(*grads, *ws, *ms, *vs)
    return out[:n], out[n:2 * n], out[2 * n:]


def _unstack_heads(w_st):
    per = HEAD_DIM // N_CHIPS
    return w_st.reshape(N_CHIPS, HEADS, per, HEAD_DIM).transpose(1, 0, 2, 3).reshape(HEADS, HEAD_DIM, HEAD_DIM)


def _stack_heads(w):
    per = HEAD_DIM // N_CHIPS
    return w.reshape(HEADS, N_CHIPS, per, HEAD_DIM).transpose(1, 0, 2, 3).reshape(N_CHIPS, HEADS * per, HEAD_DIM)


def kernel(x, norm_mix_g, w_in, conv_w, conv_b, w_rgate, b_rgate, w_igate, b_igate, lru_lambda, w_out_a, sgu_ln_g, sgu_ln_b, sgu_w_s, sgu_b_s, w_out_b, w_out, norm_mlp_g, w_up, w_down, norm_final_g, loss_target, m_norm_mix_g, m_w_in, m_conv_w, m_conv_b, m_w_rgate, m_b_rgate, m_w_igate, m_b_igate, m_lru_lambda, m_w_out_a, m_sgu_ln_g, m_sgu_ln_b, m_sgu_w_s, m_sgu_b_s, m_w_out_b, m_w_out, m_norm_mlp_g, m_w_up, m_w_down, m_norm_final_g, v_norm_mix_g, v_w_in, v_conv_w, v_conv_b, v_w_rgate, v_b_rgate, v_w_igate, v_b_igate, v_lru_lambda, v_w_out_a, v_sgu_ln_g, v_sgu_ln_b, v_sgu_w_s, v_sgu_b_s, v_w_out_b, v_w_out, v_norm_mlp_g, v_w_up, v_w_down, v_norm_final_g):
    chip = _chip_index(lax.axis_index("x"), lax.axis_index("y"))
    core = lax.axis_index("c")
    quarter_h = HEAD_DIM // N_CHIPS
    quarter_d = D_MODEL // N_CHIPS

    as_2d = lambda a: a.reshape(-1, a.shape[-1])
    big_w = [as_2d(w) for w in (w_in, w_rgate, w_igate, w_out_a, w_out_b, w_out, w_up, w_down)]
    big_m = [as_2d(w) for w in (m_w_in, m_w_rgate, m_w_igate, m_w_out_a, m_w_out_b, m_w_out, m_w_up, m_w_down)]
    big_v = [as_2d(w) for w in (v_w_in, v_w_rgate, v_w_igate, v_w_out_a, v_w_out_b, v_w_out, v_w_up, v_w_down)]

    packed = jnp.concatenate([conv_w[0], b_rgate[0], b_igate[0]], axis=1)
    packed = jnp.concatenate([packed, jnp.zeros_like(packed)], axis=0)
    s_in, s_r, s_i, s_oa, s_ob, s_out, s_up, s_down = [w.astype(BF16) for w in big_w]
    xs, target = x[0], loss_target[0]
    g3 = norm_final_g.reshape(1, D_MODEL)
    bias_s = jnp.broadcast_to(jnp.transpose(sgu_b_s[0])[:, :, None], (CHUNK, GROUPS, GROUP_DIM)).reshape(CHUNK, D_MODEL)
    core_arr = core.reshape(1).astype(jnp.int32)
    place = jnp.stack([chip, core]).astype(jnp.int32)
    quarter = lambda g: g.reshape(N_CHIPS, D_MODEL // N_CHIPS, D_MODEL)

    def pair_add(nm, grads, from_sibling):
        halves = [g.reshape(N_CHIPS, 2, g.shape[1] // 2, g.shape[2]) for g in grads]
        return list(_pair_add("pair_add_" + nm, core_arr, halves, from_sibling))

    def chip_sum(nm, pairs, from_chips):
        return list(_chip_sum("chip_sum_" + nm, place, pairs, from_chips))

    order = jnp.stack([chip, chip ^ 2, chip ^ 1, chip ^ 3]).astype(jnp.int32)
    (z, n1, (w_in_st, wr_st, wi_st)), ((packed_all,), late) = _fwd_in(
        xs, norm_mix_g, [s_in, s_r, s_i], order,
        jobs=[_gather_small_job(packed), _gather_near_job([s_oa, s_ob, s_out])])
    pick = lambda lo, hi: packed_all[:, :HEADS, lo:hi].transpose(1, 0, 2).reshape(HEADS, -1)
    conv_w_full = pick(0, quarter_d)
    br_full = pick(quarter_d, quarter_d + quarter_h).reshape(1, D_MODEL)
    bi_full = pick(quarter_d + quarter_h, quarter_d + 2 * quarter_h).reshape(1, D_MODEL)
    wr, wi = _unstack_heads(wr_st), _unstack_heads(wi_st)
    lru = (conv_w_full, conv_b, wr, br_full, wi, bi_full, lru_lambda)
    sgu = (sgu_ln_g, sgu_ln_b, sgu_w_s[0], bias_s)

    after = lambda arrays, result: lax.optimization_barrier((arrays, result))[0]
    w_up_st, w_dn = _sequencer_call(
        "gather_mlp", 8, _gather_near_job(after([s_up, s_down], n1)).then(_gather_far_job).then(_gather_pass_job))
    w_dn = w_dn.reshape(D_FF, D_MODEL)
    late_step = (3 * (xs.shape[0] // SEQ_TILE) // 4,)
    (ya, *saved), (late,) = _fwd_lru(z, *lru, jobs=[_gather_far_job(late).then(_gather_pass_job, at=late_step)])
    w_oa, w_ob, w_o = [w.reshape(D_MODEL, D_MODEL) for w in late]
    (yb, pa, pb, h1, n2), _ = _fwd_sgu_merge(ya, z, xs, *sgu, w_oa, w_ob, w_o, norm_mlp_g)
    (act, dup, dh2b, dh1, loss_part, dg3, dg2), _ = _mlp(n2, h1, target, w_up_st, w_dn, norm_mlp_g, g3)

    d_down, _ = _weight_grad("dw_down", act, dh2b, N_CHIPS, True, False, D_MODEL)
    r_down, = _sequencer_call("send_w_down", 10, _pair_send_job([d_down]))
    d_up, _ = _weight_grad("dw_up", n2, dup, N_CHIPS, False, True, D_MODEL)
    r_up, = _sequencer_call("send_w_up", 11, _pair_send_job([d_up]))
    (p_down,), (p_up,) = pair_add("w_down", [d_down], [r_down]), pair_add("w_up", [d_up], [r_up])
    (dz, merged, dpa, dpb, dh1b, dlg, dlb, dws, dbs, dcw, dcb, dwr, dbr, dwi, dbi, dlam), ((q_up, q_down),) = _bwd_mix(
        dh1, pa, pb, z, *saved, w_oa, w_ob, w_o, *sgu, conv_w_full, wr, wi, lru_lambda,
        jobs=[_chip_exchange_job([p_up, p_down])])
    half_up, half_down = chip_sum("mlp", [p_up, p_down], [q_up, q_down])
    names = ("w_in", "w_rgate", "w_igate", "w_out_a", "w_out_b", "w_out", "w_up", "w_down")
    (d_out, d_oa, d_ob), ((full_up, full_down),) = _weight_grads_square(
        "dw_projections", [(merged, dh1b), (ya, dpa), (yb, dpb)], jobs=[_share_job([half_up, half_down])])
    mids = [quarter(d_oa), quarter(d_ob), quarter(d_out)]
    r_mids = _sequencer_call("send_mids", 1, _pair_send_job(mids))
    gates = [_stack_heads(dwr).astype(BF16), _stack_heads(dwi).astype(BF16)]
    small = _pack_small(dcw, dcb, dbr, dbi, dlam, dlg, dlb, dg2, dg3, loss_part, dbs)
    p_mids = pair_add("projections", mids, r_mids)
    q_mids = _sequencer_call("exchange_mids", 2, _chip_exchange_job(p_mids))
    d_in, (r_gates, (vec_all, ws_all)) = _weight_grad(
        "dw_in", n1, after(dz, p_mids), N_CHIPS, False, True, IN_SHARD,
        jobs=[_pair_send_job(gates), _gather_all_job([small, dws])])
    r_in, = _sequencer_call("send_w_in", 3, _pair_send_job(after([d_in], q_mids)))
    adam_args = {nm: (w, m, v) for nm, w, m, v in zip(names, big_w, big_m, big_v)}

    def adamw(group, nms, grads):
        given = [adam_args[nm] for nm in nms]
        grads = [g.reshape(w.shape) for g, (w, _, _) in zip(grads, given)]
        outs = _adamw("adamw_" + group, grads, *[[a[q] for a in given] for q in range(3)])
        return {nm: (g, out) for nm, g, out in zip(nms, grads, outs)}

    p_gates = pair_add("gates", gates, r_gates)
    half_mids = chip_sum("projections", p_mids, q_mids)
    full_mids = _sequencer_call("share_mids", 12, _share_job(half_mids))
    p_first = pair_add("w_in", [d_in], [r_in]) + p_gates
    q_first = _sequencer_call("exchange_w_in", 4, _chip_exchange_job(p_first))
    (grad_x, dg1), _ = _bwd_in(dz, xs, dh1, w_in_st, norm_mix_g)
    dg1_all, = _sequencer_call("gather_dg1", 6, _gather_all_job([dg1]))
    done = adamw("mlp", ("w_up", "w_down"), [full_up, full_down])
    q_first = after(q_first, [out[0] for _, out in done.values()])
    half_first = chip_sum("first", p_first, q_first)
    full_first = _sequencer_call("share_last", 5, _share_job(half_first))
    done.update(adamw("projections", names[3:6], after(full_mids, half_first)))
    done.update(adamw("first", names[:3], full_first))
    full, big_out = [done[nm][0] for nm in names], [done[nm][1] for nm in names]

    vec, ws_sum = _sum_small(vec_all, ws_all, dg1_all)
    row = lambda r: vec[r:r + 1]
    shard = lambda a, width: lax.dynamic_slice_in_dim(a, chip * width, width, axis=1)
    g_small = dict(
        norm_mix_g=row(ROW_G1), conv_w=shard(vec[ROW_CW:ROW_CW + CONV_WIDTH], quarter_d), conv_b=row(ROW_CB),
        b_rgate=shard(row(ROW_BR).reshape(HEADS, HEAD_DIM), quarter_h),
        b_igate=shard(row(ROW_BI).reshape(HEADS, HEAD_DIM), quarter_h), lru_lambda=row(ROW_LAM),
        sgu_ln_g=row(ROW_LG), sgu_ln_b=row(ROW_LB),
        sgu_w_s=ws_sum.reshape(CHUNK, GROUPS, CHUNK).transpose(1, 0, 2).reshape(GROUPS * CHUNK, CHUNK),
        sgu_b_s=vec[ROW_BS:ROW_BS + GROUPS, 0:CHUNK], norm_mlp_g=row(ROW_G2), norm_final_g=row(ROW_G3))
    loss = vec[ROW_LOSS, 0]
    small_names = list(g_small)
    given = dict(
        norm_mix_g=(norm_mix_g, m_norm_mix_g, v_norm_mix_g), conv_w=(conv_w, m_conv_w, v_conv_w),
        conv_b=(conv_b, m_conv_b, v_conv_b), b_rgate=(b_rgate, m_b_rgate, v_b_rgate),
        b_igate=(b_igate, m_b_igate, v_b_igate), lru_lambda=(lru_lambda, m_lru_lambda, v_lru_lambda),
        sgu_ln_g=(sgu_ln_g, m_sgu_ln_g, v_sgu_ln_g), sgu_ln_b=(sgu_ln_b, m_sgu_ln_b, v_sgu_ln_b),
        sgu_w_s=(sgu_w_s, m_sgu_w_s, v_sgu_w_s), sgu_b_s=(sgu_b_s, m_sgu_b_s, v_sgu_b_s),
        norm_mlp_g=(norm_mlp_g, m_norm_mlp_g, v_norm_mlp_g), norm_final_g=(norm_final_g, m_norm_final_g, v_norm_final_g))
    g2d = [g_small[nm] for nm in small_names]
    to2d = lambda a, g: a.reshape(g.shape)
    d_s, m_s, v_s = _adamw_small(
        g2d, *[[to2d(given[nm][q], g) for nm, g in zip(small_names, g2d)] for q in range(3)])

    shapes = dict(
        norm_mix_g=norm_mix_g, w_in=w_in, conv_w=conv_w, conv_b=conv_b, w_rgate=w_rgate, b_rgate=b_rgate,
        w_igate=w_igate, b_igate=b_igate, lru_lambda=lru_lambda, w_out_a=w_out_a, sgu_ln_g=sgu_ln_g,
        sgu_ln_b=sgu_ln_b, sgu_w_s=sgu_w_s, sgu_b_s=sgu_b_s, w_out_b=w_out_b, w_out=w_out, norm_mlp_g=norm_mlp_g,
        w_up=w_up, w_down=w_down, norm_final_g=norm_final_g)
    grads, deltas, new_m, new_v = {}, {}, {}, {}
    for nm, g, (d, nmom, nvar) in zip(names, full, big_out):
        grads[nm], deltas[nm], new_m[nm], new_v[nm] = g, d, nmom, nvar
    for p, nm in enumerate(small_names):
        grads[nm], deltas[nm], new_m[nm], new_v[nm] = g2d[p], d_s[p], m_s[p], v_s[p]
    order = list(shapes)
    out = [loss, grad_x[None]]
    for group in (grads, deltas, new_m, new_v):
        out += [group[nm].reshape(shapes[nm].shape) for nm in order]
    return tuple(out)
```

```python
import functools

import jax
import jax.numpy as jnp
from jax import lax
from jax.experimental import pallas as pl
from jax.experimental.pallas import tpu as pltpu
from jax.experimental.pallas import tpu_sc as plsc

F32 = jnp.float32
BF16 = jnp.bfloat16
MESH = pl.DeviceIdType.MESH

D_MODEL = 1024
D_IN = 6 * D_MODEL
D_FF = 4 * D_MODEL
N_CHIPS = 4
IN_SHARD = D_IN // N_CHIPS
HEADS = 4
HEAD_DIM = D_MODEL // HEADS
GROUPS = 4
GROUP_DIM = D_MODEL // GROUPS
CHUNK = 128
CONV_WIDTH = 4
LRU_C = 8.0
NORM_EPS = 1e-6
LN_EPS = 1e-5

ADAM_LR = 0.001
ADAM_B1 = 0.9
ADAM_B2 = 0.999
ADAM_EPS = 1e-08
ADAM_WD = 0.01
ADAM_STEP = 10

SUBLANES = 8
MM_TILE = 512
IN_TILE = 1024
SEQ_TILE = 256
DW_TILE = 2048
VMEM_LIMIT_BYTES = 56 * 1024 * 1024

GELU_K0 = 0.7978845608028654
GELU_K1 = 0.044715


def _params(n_grid_axes=1):
    return pltpu.CompilerParams(
        dimension_semantics=("arbitrary",) * n_grid_axes, vmem_limit_bytes=VMEM_LIMIT_BYTES)


def _resident(shape):
    nd = len(shape)
    return pl.BlockSpec(shape, lambda *_: (0,) * nd, pipeline_mode=pl.Buffered(1))


def _const(shape):
    nd = len(shape)
    return pl.BlockSpec(shape, lambda *_: (0,) * nd)


def _dot(a, b):
    return jnp.dot(a, b, preferred_element_type=F32)


def _dot_nt(a, b):
    return lax.dot_general(a, b, (((1,), (1,)), ((), ())), preferred_element_type=F32)


def _dot_tn(a, b):
    return lax.dot_general(a, b, (((0,), (0,)), ((), ())), preferred_element_type=F32)


def _gelu(x):
    t = jnp.tanh(x * (GELU_K0 + (GELU_K0 * GELU_K1) * (x * x)))
    return x * (0.5 + 0.5 * t)


def _gelu_and_grad(x):
    x2 = x * x
    t = jnp.tanh(x * (GELU_K0 + (GELU_K0 * GELU_K1) * x2))
    s = 0.5 + 0.5 * t
    dg = s + (x * (1.0 - t * t)) * (0.5 * GELU_K0 + (1.5 * GELU_K0 * GELU_K1) * x2)
    return x * s, dg


def _gate(x):
    return 0.5 + 0.5 * jnp.tanh(0.5 * x.astype(F32))


def _rms(x):
    r = lax.rsqrt(jnp.mean(x * x, axis=-1, keepdims=True) + NORM_EPS)
    return x * r, r


def _rms_bwd(dn, xhat, r):
    return r * (dn - xhat * jnp.mean(dn * xhat, axis=-1, keepdims=True))


def _col_sum(v):
    return jnp.sum(v, axis=0, keepdims=True)


def _shift_down(x, tail8, k):
    xs = pltpu.roll(x, k, 0)
    ts = pltpu.roll(tail8, k, 0)
    ridx = lax.broadcasted_iota(jnp.int32, tail8.shape, 0)
    head = jnp.where(ridx < k, ts, xs[0:SUBLANES])
    return jnp.concatenate([head, xs[SUBLANES:]], axis=0)


def _shift_up(x, head8, k):
    n = x.shape[0]
    xs = pltpu.roll(x, n - k, 0)
    hs = pltpu.roll(head8, SUBLANES - k, 0)
    ridx = lax.broadcasted_iota(jnp.int32, head8.shape, 0)
    last = jnp.where(ridx >= SUBLANES - k, hs, xs[n - SUBLANES:n])
    return jnp.concatenate([xs[:n - SUBLANES], last], axis=0)


def _scan_forward(a, b, carry):
    n, cols = a.shape
    groups = n // SUBLANES
    a = a.reshape(groups, SUBLANES, cols)
    b = b.reshape(groups, SUBLANES, cols)
    sub = lax.broadcasted_iota(jnp.int32, a.shape, 1)
    for s in (1, 2, 4):
        a_s = pltpu.roll(a, s, 1)
        b_s = pltpu.roll(b, s, 1)
        m = sub >= s
        b = jnp.where(m, a * b_s + b, b)
        a = jnp.where(m, a * a_s, a)
    out = []
    for g in range(groups):
        h = a[g] * carry + b[g]
        out.append(h)
        carry = h[SUBLANES - 1:SUBLANES]
    return jnp.concatenate(out, axis=0), carry


def _scan_backward(a, b, carry):
    n, cols = a.shape
    groups = n // SUBLANES
    a = a.reshape(groups, SUBLANES, cols)
    b = b.reshape(groups, SUBLANES, cols)
    sub = lax.broadcasted_iota(jnp.int32, a.shape, 1)
    for s in (1, 2, 4):
        a_s = pltpu.roll(a, SUBLANES - s, 1)
        b_s = pltpu.roll(b, SUBLANES - s, 1)
        m = sub < SUBLANES - s
        b = jnp.where(m, a * b_s + b, b)
        a = jnp.where(m, a * a_s, a)
    out = [None] * groups
    for g in reversed(range(groups)):
        h = a[g] * carry + b[g]
        out[g] = h
        carry = h[0:1]
    return jnp.concatenate(out, axis=0), carry


def _softplus_neg(lam):
    e = jnp.exp(-jnp.abs(lam))
    u = 1.0 + e
    log1p_e = jnp.where(u == 1.0, e, jnp.log(u) * (e / jnp.where(u == 1.0, 1.0, u - 1.0)))
    return jnp.maximum(-lam, 0.0) + log1p_e


def _lru_gates(xa, tail8, cw_ref, cb_ref, wr_ref, br_ref, wi_ref, bi_ref, lam_ref):
    cw = cw_ref[...]
    xc = cb_ref[...] + cw[0:1] * xa
    for k in range(1, CONV_WIDTH):
        xc = xc + cw[k:k + 1] * _shift_down(xa, tail8, k)
    xcb = xc.astype(BF16)
    pre_r, pre_i = [], []
    for h in range(HEADS):
        cols = slice(h * HEAD_DIM, (h + 1) * HEAD_DIM)
        pre_r.append(_dot(xcb[:, cols], wr_ref[h]))
        pre_i.append(_dot(xcb[:, cols], wi_ref[h]))
    r = jax.nn.sigmoid(jnp.concatenate(pre_r, axis=1) + br_ref[...])
    ig = jax.nn.sigmoid(jnp.concatenate(pre_i, axis=1) + bi_ref[...])
    _, a, mult, _ = _decay(r, lam_ref)
    return xc, r, ig, a, mult


def _decay(r, lam_ref):
    sp = _softplus_neg(lam_ref[...])
    log_a = ((-LRU_C) * sp) * r
    a = jnp.exp(log_a)
    th = jnp.tanh(log_a)
    q = (-2.0 * th) / (1.0 - th)
    inv = lax.rsqrt(q)
    return sp, a, jnp.where(q > 0.0, q * inv, 0.0), inv


class _Phase:
    def __init__(self, copies, n_sem, n_local, start=None, finish=None):
        self.copies, self.n_sem, self.n_local, self.start, self.finish = copies, n_sem, n_local, start, finish


class _Job:
    def __init__(self, inputs, out_shape, n_sem, copies, peers, aliases=None, n_local=0):
        self.inputs, self.out_shape = list(inputs), list(out_shape)
        self.aliases = dict(aliases or {})
        self.phases = [_Phase(copies, n_sem, n_local)]
        self.peers = tuple(peers)

    def then(self, make, at=None):
        nxt = make(self.out_shape)
        self.phases[-1].finish = at
        nxt.phases[0].start = at
        self.phases += nxt.phases
        self.peers = tuple(sorted(set(self.peers + nxt.peers)))
        return self


def _fused_call(body, jobs, *, name, grid, in_specs, out_specs, out_shape, scratch_shapes=(),
                input_output_aliases=None, compiler_params=None, n_prefetch=0, jobs_start_after=None):
    single = not isinstance(out_shape, (list, tuple))
    out_specs = [out_specs] if single else list(out_specs)
    out_shape = [out_shape] if single else list(out_shape)
    n_scr = len(scratch_shapes)
    in_specs, scratch_shapes = list(in_specs), list(scratch_shapes)
    n_in, n_out = len(in_specs), len(out_shape)
    aliases = dict(input_output_aliases or {})
    in_at, out_at, phases = [], [], []
    for q, job in enumerate(jobs):
        in_at.append(len(in_specs))
        out_at.append(len(out_shape))
        for i, o in job.aliases.items():
            aliases[n_prefetch + len(in_specs) + i] = len(out_shape) + o
        in_specs += [ANY] * len(job.inputs)
        out_specs += [ANY] * len(job.out_shape)
        out_shape += job.out_shape
        for k, phase in enumerate(job.phases):
            phases.append((q, k, phase, len(scratch_shapes)))
            scratch_shapes += [pltpu.SemaphoreType.DMA((phase.n_sem,)), pltpu.SemaphoreType.DMA((phase.n_sem,)),
                               pltpu.SemaphoreType.DMA((max(phase.n_local, 1),))]
    n_in_all, n_out_all = len(in_specs), len(out_shape)
    first_step, last_step = (0,) * len(grid), tuple(g - 1 for g in grid)

    def full_body(*refs):
        prefetch, refs = refs[:n_prefetch], refs[n_prefetch:]
        ins, outs, scr = refs[:n_in_all], refs[n_in_all:n_in_all + n_out_all], refs[n_in_all + n_out_all:]
        ids = [pl.program_id(a) for a in range(len(grid))]
        at_step = lambda step: functools.reduce(jnp.logical_and, [i == k for i, k in zip(ids, step)])

        def copies(q, k, phase, sem_at):
            job = jobs[q]
            mine = outs[out_at[q]:out_at[q] + len(job.out_shape)]
            return phase.copies(ins[in_at[q]:in_at[q] + len(job.inputs)] if k == 0 else mine, mine,
                                *scr[sem_at:sem_at + 3])

        def start(*phase):
            def go():
                sends, _, local = copies(*phase)
                for cp in local + sends:
                    cp.start()
            return go

        def finish(*phase):
            def go():
                sends, arrivals, local = copies(*phase)
                for cp in arrivals:
                    cp.wait_recv()
                for cp in sends:
                    cp.wait_send()
                for cp in local:
                    cp.wait()
            return go

        for phase in phases:
            if phase[2].start is None and jobs_start_after is None:
                pl.when(at_step(first_step))(start(*phase))
        body(*prefetch, *ins[:n_in], *outs[:n_out], *scr[:n_scr])
        for phase in phases:
            pl.when(at_step(phase[2].finish or last_step))(finish(*phase))
            nxt = phase[2].start or jobs_start_after
            if nxt is not None:
                pl.when(at_step(nxt))(start(*phase))

    if n_prefetch:
        layout = dict(grid_spec=pltpu.PrefetchScalarGridSpec(
            num_scalar_prefetch=n_prefetch, grid=grid, in_specs=in_specs, out_specs=out_specs,
            scratch_shapes=scratch_shapes))
    else:
        layout = dict(grid=grid, in_specs=in_specs, out_specs=out_specs, scratch_shapes=scratch_shapes)
    call = pl.pallas_call(
        full_body, name=name, out_shape=out_shape, input_output_aliases=aliases, compiler_params=compiler_params,
        **layout)

    def run(*args):
        res = call(*args, *[a for job in jobs for a in job.inputs])
        mine = res[0] if single else list(res[:n_out])
        return mine, [list(res[at:at + len(job.out_shape)]) for at, job in zip(out_at, jobs)]

    return run


def _fwd_in(x, g1, shards, order, jobs=()):
    t = x.shape[0]
    rows_per_step = min(IN_TILE, t)
    n_tiles = t // rows_per_step
    n = len(shards)
    halves = [s.shape[0] // 2 for s in shards]

    def body(order_ref, x_ref, g_ref, *refs):
        del order_ref
        ins, (z_ref, n_ref), outs = refs[:n], refs[n:n + 2], refs[n + 2:2 * n + 2]
        wbuf, nbuf, send, recv, local = refs[2 * n + 2:]
        s, i = pl.program_id(0), pl.program_id(1)
        x_, y_, c, chips = _place()
        near, far = _near_far(x_, y_, c)
        k_me = _chip_index(x_, y_)

        def block(w, chip, pc):
            return outs[w].at[_chip_index(*chip), pl.ds(pc * halves[w], halves[w]), :]

        def over_ici(w, j, landing):
            return pltpu.make_async_remote_copy(
                src_ref=ins[w].at[pl.ds(c * halves[w], halves[w]), :],
                dst_ref=block(w, chips[j] if landing else (x_, y_), c), send_sem=send.at[6 * w + j],
                recv_sem=recv.at[6 * w + j], device_id=(*chips[j], c), device_id_type=MESH)

        def onward(w, landing):
            blk = block(w, chips[2] if landing else near, c)
            return pltpu.make_async_remote_copy(
                src_ref=blk, dst_ref=blk, send_sem=send.at[6 * w + 2], recv_sem=recv.at[6 * w + 2],
                device_id=(*far, c), device_id_type=MESH)

        def to_sibling(w, j, landing):
            blk = block(w, chips[j], 1 - c if landing else c)
            return pltpu.make_async_remote_copy(
                src_ref=blk, dst_ref=blk, send_sem=send.at[6 * w + 3 + j], recv_sem=recv.at[6 * w + 3 + j],
                device_id=(x_, y_, 1 - c), device_id_type=MESH)

        own = [pltpu.make_async_copy(wbuf, outs[0].at[k_me], local.at[0])]
        own += [pltpu.make_async_copy(ins[w], outs[w].at[k_me], local.at[w]) for w in range(1, n)]

        @pl.when((s == 0) & (i == 0))
        def _():
            for j in range(2):
                for w in range(n):
                    over_ici(w, j, False).start()
            load = pltpu.make_async_copy(ins[0], wbuf, local.at[n])
            load.start()
            load.wait()
            for cp in own:
                cp.start()

        for j in range(N_CHIPS - 1):
            @pl.when((s == j + 1) & (i == 0))
            def _(j=j):
                if j == 0:
                    for k in range(2):
                        for w in range(n):
                            over_ici(w, k, True).wait_recv()
                    for w in range(n):
                        onward(w, False).start()
                    for k in range(2):
                        for w in range(n):
                            to_sibling(w, k, False).start()
                    own[0].wait()
                if j == 2:
                    for w in range(n):
                        onward(w, True).wait_recv()
                    for w in range(n):
                        to_sibling(w, j, False).start()
                for w in range(n):
                    to_sibling(w, j, True).wait_recv()
                load = pltpu.make_async_copy(outs[0].at[_chip_index(*chips[j])], wbuf, local.at[n])
                load.start()
                load.wait()

        rows = pl.ds(pl.multiple_of(i * rows_per_step, rows_per_step), rows_per_step)

        @pl.when(s == 0)
        def _():
            xhat, _ = _rms(x_ref[...])
            nrm = (xhat * g_ref[...]).astype(BF16)
            nbuf[rows, :] = nrm
            n_ref[...] = nrm

        z_ref[...] = _dot(nbuf[rows, :], wbuf[...]).astype(BF16)

        @pl.when((s == N_CHIPS - 1) & (i == n_tiles - 1))
        def _():
            for j in range(N_CHIPS - 1):
                for w in range(n):
                    (over_ici(w, j, False) if j < 2 else onward(w, False)).wait_send()
                    to_sibling(w, j, False).wait_send()
            for cp in own[1:]:
                cp.wait()

    once = lambda s, i, order: (jnp.where(s == 0, i, n_tiles - 1), 0)
    (z, n1, *stacked), job_outs = _fused_call(
        body, jobs, name="fwd_in", grid=(N_CHIPS, n_tiles), n_prefetch=1,
        in_specs=[pl.BlockSpec((rows_per_step, D_MODEL), once), _const((1, D_MODEL))] + [ANY] * n,
        out_specs=[pl.BlockSpec((rows_per_step, IN_SHARD), lambda s, i, order: (i, order[s])),
                   pl.BlockSpec((rows_per_step, D_MODEL), once)] + [ANY] * n,
        out_shape=[jax.ShapeDtypeStruct((t, D_IN), BF16), jax.ShapeDtypeStruct((t, D_MODEL), BF16)]
        + [jax.ShapeDtypeStruct((N_CHIPS,) + s.shape, s.dtype) for s in shards],
        scratch_shapes=[pltpu.VMEM(shards[0].shape, BF16), pltpu.VMEM((t, D_MODEL), BF16),
                        pltpu.SemaphoreType.DMA((6 * n,)),
                        pltpu.SemaphoreType.DMA((6 * n,)), pltpu.SemaphoreType.DMA((n + 1,))],
        compiler_params=_params(2), jobs_start_after=(1, 0),
    )(order, x, g1, *shards)
    return (z, n1, stacked), job_outs


def _fwd_lru(z, conv_w, conv_b, wr, br, wi, bi, lam, jobs=()):
    t = z.shape[0]

    def body(xa_ref, ga_ref, cw_ref, cb_ref, wr_ref, br_ref, wi_ref, bi_ref, lam_ref, ya_ref, h_ref, xc_ref, r_ref,
             ig_ref, tail_ref, carry_ref):
        @pl.when(pl.program_id(0) == 0)
        def _():
            tail_ref[...] = jnp.zeros_like(tail_ref)
            carry_ref[...] = jnp.zeros_like(carry_ref)

        xa = xa_ref[...].astype(F32)
        xc, r, ig, a, mult = _lru_gates(xa, tail_ref[...], cw_ref, cb_ref, wr_ref, br_ref, wi_ref, bi_ref, lam_ref)
        tail_ref[...] = xa[SEQ_TILE - SUBLANES:]
        xc_ref[...], r_ref[...], ig_ref[...] = xc, r, ig
        h, carry = _scan_forward(a, xc * ig * mult, carry_ref[...])
        carry_ref[...] = carry
        h_ref[...] = h
        ya_ref[...] = (h * _gelu(ga_ref[...].astype(F32))).astype(BF16)

    tile = lambda j: pl.BlockSpec((SEQ_TILE, D_MODEL), lambda i: (i, j))
    return _fused_call(
        body, jobs, name="fwd_lru", grid=(t // SEQ_TILE,),
        in_specs=[tile(0), tile(1), _const((CONV_WIDTH, D_MODEL)), _const((1, D_MODEL)),
                  _resident((HEADS, HEAD_DIM, HEAD_DIM)), _const((1, D_MODEL)),
                  _resident((HEADS, HEAD_DIM, HEAD_DIM)), _const((1, D_MODEL)), _const((1, D_MODEL))],
        out_specs=[tile(0)] * 5,
        out_shape=[jax.ShapeDtypeStruct((t, D_MODEL), BF16)] + [jax.ShapeDtypeStruct((t, D_MODEL), F32)] * 4,
        scratch_shapes=[pltpu.VMEM((SUBLANES, D_MODEL), F32), pltpu.VMEM((1, D_MODEL), F32)],
        compiler_params=_params(),
    )(z, z, conv_w, conv_b, wr, br, wi, bi, lam)


def _sgu_forward_parts(ub, vb, lg_ref, lb_ref):
    u, du = _gelu_and_grad(ub.astype(F32))
    vg, dvg = _gelu_and_grad(vb.astype(F32))
    mu = jnp.mean(vg, axis=-1, keepdims=True)
    d = vg - mu
    rstd = lax.rsqrt(jnp.mean(d * d, axis=-1, keepdims=True) + LN_EPS)
    vhat = d * rstd
    vn = (vhat * lg_ref[...] + lb_ref[...]).astype(BF16)
    return u, du, dvg, rstd, vhat, vn


def _causal_mask():
    rows = lax.broadcasted_iota(jnp.int32, (CHUNK, CHUNK), 0)
    cols = lax.broadcasted_iota(jnp.int32, (CHUNK, CHUNK), 1)
    return rows >= cols


def _fwd_sgu_merge(ya, z, x, ln_g, ln_b, w_s, bias_full, w_oa, w_ob, w_out, g2, jobs=()):
    t = x.shape[0]

    def body(ya_ref, ub_ref, vb_ref, m_ref, x_ref, lg_ref, lb_ref, ws_ref, bias_ref, woa_ref, wob_ref, wout_ref, g_ref,
             yb_ref, pa_ref, pb_ref, h1_ref, n2_ref):
        u, _, _, _, _, vn = _sgu_forward_parts(ub_ref[...], vb_ref[...], lg_ref, lb_ref)
        mask = _causal_mask()
        wm = [jnp.where(mask, ws_ref[g], 0.0).astype(BF16) for g in range(GROUPS)]
        for c in range(SEQ_TILE // CHUNK):
            rows = slice(c * CHUNK, (c + 1) * CHUNK)
            for g in range(GROUPS):
                cols = slice(g * GROUP_DIM, (g + 1) * GROUP_DIM)
                sp = _dot(wm[g], vn[rows, cols]) + bias_ref[:, cols]
                yb_ref[rows, cols] = (u[rows, cols] * sp).astype(BF16)
        pa = _dot(ya_ref[...], woa_ref[...])
        pb = _dot(yb_ref[...], wob_ref[...])
        pa_ref[...] = pa
        pb_ref[...] = pb
        merged = _gate(m_ref[:, :D_MODEL]) * pa + _gate(m_ref[:, D_MODEL:]) * pb
        h1 = x_ref[...] + _dot(merged.astype(BF16), wout_ref[...])
        h1_ref[...] = h1
        xhat, _ = _rms(h1)
        n2_ref[...] = (xhat * g_ref[...]).astype(BF16)

    tile = lambda j: pl.BlockSpec((SEQ_TILE, D_MODEL), lambda i: (i, j))
    sq = _resident((D_MODEL, D_MODEL))
    vec = _const((1, D_MODEL))
    bf, f32 = jax.ShapeDtypeStruct((t, D_MODEL), BF16), jax.ShapeDtypeStruct((t, D_MODEL), F32)
    return _fused_call(
        body, jobs, name="fwd_sgu_merge", grid=(t // SEQ_TILE,),
        in_specs=[tile(0), tile(2), tile(3), pl.BlockSpec((SEQ_TILE, 2 * D_MODEL), lambda i: (i, 2)), tile(0), vec, vec,
                  _const((GROUPS, CHUNK, CHUNK)), _const((CHUNK, D_MODEL)), sq, sq, sq, vec],
        out_specs=[tile(0)] * 5,
        out_shape=[bf, f32, f32, f32, bf],
        compiler_params=_params(),
    )(ya, z, z, z, x, ln_g, ln_b, w_s, bias_full, w_oa, w_ob, w_out, g2)


def _mlp(n2, h1, target, w_up_st, w_down, g2, g3, jobs=()):
    t = n2.shape[0]

    def body(n2_ref, h1_ref, tgt_ref, wup_ref, wdown_ref, g2_ref, g3_ref, act_ref, dup_ref, dh2b_ref, dh1_ref,
             loss_ref, dg3_ref, dg2_ref, relu_ref):
        @pl.when(pl.program_id(0) == 0)
        def _():
            for ref in (loss_ref, dg3_ref, dg2_ref):
                ref[...] = jnp.zeros_like(ref)

        n2 = n2_ref[...]
        h1 = h1_ref[...]
        h2 = h1
        for k in range(N_CHIPS):
            cols = slice(k * D_MODEL, (k + 1) * D_MODEL)
            r = jnp.maximum(_dot(n2, wup_ref[k]), 0.0)
            relu_ref[:, cols] = r
            act = (r * r).astype(BF16)
            act_ref[:, cols] = act
            h2 = h2 + _dot(act, wdown_ref[cols, :])
        xhat, r3 = _rms(h2)
        diff = xhat * g3_ref[...] - tgt_ref[...]
        sq = jnp.sum(diff * diff, axis=1, keepdims=True)
        loss_ref[...] = loss_ref[...] + (0.5 / D_MODEL) * jnp.sum(sq, axis=0, keepdims=True)
        dy = diff * (1.0 / D_MODEL)
        dg3_ref[...] = dg3_ref[...] + _col_sum(dy * xhat)
        dh2 = _rms_bwd(dy * g3_ref[...], xhat, r3)
        dh2b = dh2.astype(BF16)
        dh2b_ref[...] = dh2b
        dn2 = jnp.zeros((SEQ_TILE, D_MODEL), F32)
        for k in range(N_CHIPS):
            cols = slice(k * D_MODEL, (k + 1) * D_MODEL)
            dup = (_dot_nt(dh2b, wdown_ref[cols, :]) * (2.0 * relu_ref[:, cols])).astype(BF16)
            dup_ref[:, cols] = dup
            dn2 = dn2 + _dot_nt(dup, wup_ref[k])
        xhat, r2 = _rms(h1)
        dg2_ref[...] = dg2_ref[...] + _col_sum(dn2 * xhat)
        dh1_ref[...] = dh2 + _rms_bwd(dn2 * g2_ref[...], xhat, r2)

    tile = pl.BlockSpec((SEQ_TILE, D_MODEL), lambda i: (i, 0))
    wide = pl.BlockSpec((SEQ_TILE, D_FF), lambda i: (i, 0))
    vec = _const((1, D_MODEL))
    vec_shape = jax.ShapeDtypeStruct((1, D_MODEL), F32)
    return _fused_call(
        body, jobs, name="mlp", grid=(t // SEQ_TILE,),
        in_specs=[tile, tile, tile, _resident((N_CHIPS, D_MODEL, D_MODEL)), _resident((D_FF, D_MODEL)), vec, vec],
        out_specs=[wide, wide, tile, tile, _const((SUBLANES, 128)), vec, vec],
        out_shape=[jax.ShapeDtypeStruct((t, D_FF), BF16), jax.ShapeDtypeStruct((t, D_FF), BF16),
                   jax.ShapeDtypeStruct((t, D_MODEL), BF16), jax.ShapeDtypeStruct((t, D_MODEL), F32),
                   jax.ShapeDtypeStruct((SUBLANES, 128), F32), vec_shape, vec_shape],
        scratch_shapes=[pltpu.VMEM((SEQ_TILE, D_FF), F32)],
        compiler_params=_params(),
    )(n2, h1, target, w_up_st, w_down, g2, g3)


def _bwd_mix(dh1, pa, pb, z, h, xc, r, ig, w_oa, w_ob, w_out, ln_g, ln_b, w_s, bias_full, conv_w, wr, wi, lam, jobs=()):
    t = dh1.shape[0]
    n_tiles = t // SEQ_TILE
    per_tile = SEQ_TILE // SUBLANES

    def merge_part(dh1_ref, pa_ref, pb_ref, m_ref, woa_ref, wob_ref, wout_ref, dz_ref, dya_ref, dyb_ref, mg_ref,
                   dpa_ref, dpb_ref, dh1b_ref):
        dh1b = dh1_ref[...].astype(BF16)
        dh1b_ref[...] = dh1b
        dm = _dot_nt(dh1b, wout_ref[...])
        pa = pa_ref[...]
        pb = pb_ref[...]
        sa = _gate(m_ref[:, :D_MODEL])
        sb = _gate(m_ref[:, D_MODEL:])
        mg_ref[...] = (sa * pa + sb * pb).astype(BF16)
        dpa = dm * sa
        dpb = dm * sb
        dz_ref[:, :D_MODEL] = ((dpa * pa) * (1.0 - sa)).astype(BF16)
        dz_ref[:, D_MODEL:] = ((dpb * pb) * (1.0 - sb)).astype(BF16)
        dpa = dpa.astype(BF16)
        dpb = dpb.astype(BF16)
        dpa_ref[...] = dpa
        dpb_ref[...] = dpb
        dya_ref[...] = _dot_nt(dpa, woa_ref[...])
        dyb_ref[...] = _dot_nt(dpb, wob_ref[...])

    def sgu_part(dyb_ref, ub_ref, vb_ref, lg_ref, lb_ref, ws_ref, bias_ref, dz_ref, dlg_ref, dlb_ref, dws_ref, dbs_ref,
                 dvn_ref, dsp_acc):
        i = pl.program_id(0)

        @pl.when(i == 0)
        def _():
            dlg_ref[...] = jnp.zeros_like(dlg_ref)
            dlb_ref[...] = jnp.zeros_like(dlb_ref)
            dws_ref[...] = jnp.zeros_like(dws_ref)
            dsp_acc[...] = jnp.zeros_like(dsp_acc)

        u, du, dvg, rstd, vhat, vn = _sgu_forward_parts(ub_ref[...], vb_ref[...], lg_ref, lb_ref)
        dyb = dyb_ref[...]
        mask = _causal_mask()
        wm = [jnp.where(mask, ws_ref[g], 0.0).astype(BF16) for g in range(GROUPS)]
        for c in range(SEQ_TILE // CHUNK):
            rows = slice(c * CHUNK, (c + 1) * CHUNK)
            for g in range(GROUPS):
                cols = slice(g * GROUP_DIM, (g + 1) * GROUP_DIM)
                vn_blk = vn[rows, cols]
                sp = _dot(wm[g], vn_blk) + bias_ref[:, cols]
                dyb_blk = dyb[rows, cols]
                dz_ref[rows, cols] = (dyb_blk * sp * du[rows, cols]).astype(BF16)
                dsp = dyb_blk * u[rows, cols]
                dsp_acc[:, cols] = dsp_acc[:, cols] + dsp
                dspb = dsp.astype(BF16)
                dvn_ref[rows, cols] = _dot_tn(wm[g], dspb)
                wcols = slice(g * CHUNK, (g + 1) * CHUNK)
                dws_ref[:, wcols] = dws_ref[:, wcols] + jnp.where(mask, _dot_nt(dspb, vn_blk), 0.0)
        dvn = dvn_ref[...]
        dlg_ref[...] = dlg_ref[...] + _col_sum(dvn * vhat)
        dlb_ref[...] = dlb_ref[...] + _col_sum(dvn)
        dvhat = dvn * lg_ref[...]
        dvgel = rstd * (dvhat - jnp.mean(dvhat, axis=-1, keepdims=True)
                        - vhat * jnp.mean(dvhat * vhat, axis=-1, keepdims=True))
        dz_ref[:, D_MODEL:] = (dvgel * dvg).astype(BF16)

        @pl.when(i == n_tiles - 1)
        def _():
            lane = lax.broadcasted_iota(jnp.int32, (CHUNK, 128), 1)
            out = jnp.zeros((CHUNK, 128), F32)
            for g in range(GROUPS):
                s = jnp.sum(dsp_acc[:, g * GROUP_DIM:(g + 1) * GROUP_DIM], axis=1, keepdims=True)
                out = out + jnp.where(lane == g, s, 0.0)
            dbs_ref[...] = out

    def lru_part(dya_ref, xa_ref, ga_ref, h_ref, h_prev_ref, xc_ref, r_ref, ig_ref, cw_ref, wr_ref, wi_ref, lam_ref,
                 dz_ref, dcw_ref, dcb_ref, dwr_ref, dbr_ref, dwi_ref, dbi_ref, dlam_ref, lam_carry, dxc_head):
        i = pl.program_id(0)

        @pl.when(i == 0)
        def _():
            for ref in (dcw_ref, dcb_ref, dwr_ref, dbr_ref, dwi_ref, dbi_ref, dlam_ref, lam_carry, dxc_head):
                ref[...] = jnp.zeros_like(ref)

        first_tile = i == n_tiles - 1
        h_tail = jnp.where(first_tile, 0.0, h_prev_ref[...])
        xc, r, ig = xc_ref[...], r_ref[...], ig_ref[...]
        xcb = xc.astype(BF16)
        sp, a, mult, inv_mult = _decay(r, lam_ref)
        h = h_ref[...]
        h_prev = _shift_down(h, h_tail, 1)
        dya = dya_ref[...]
        gg, dgg = _gelu_and_grad(ga_ref[...].astype(F32))
        dz_ref[:, D_MODEL:] = (dya * h * dgg).astype(BF16)
        ones = jnp.ones((SUBLANES, D_MODEL), F32)
        lam_t, lam_first = _scan_backward(_shift_up(a, ones, 1), dya * gg, lam_carry[...])
        lam_carry[...] = a[0:1] * lam_first
        lam_ig = lam_t * ig
        dxc_direct = lam_ig * mult
        dmult = lam_ig * xc
        dla = a * (lam_t * h_prev - (dmult * a) * inv_mult)
        dla_r = dla * r
        dlam_ref[...] = dlam_ref[...] + _col_sum(dla_r) * (LRU_C * jax.nn.sigmoid(-lam_ref[...]))
        dpr = (dla_r * ((-LRU_C) * sp)) * (1.0 - r)
        dpi = (dxc_direct * xc) * (1.0 - ig)
        dbr_ref[...] = dbr_ref[...] + _col_sum(dpr)
        dbi_ref[...] = dbi_ref[...] + _col_sum(dpi)
        dprb = dpr.astype(BF16)
        dpib = dpi.astype(BF16)
        dxc_gate = []
        for hd in range(HEADS):
            cols = slice(hd * HEAD_DIM, (hd + 1) * HEAD_DIM)
            dxc_gate.append(_dot_nt(dprb[:, cols], wr_ref[hd]) + _dot_nt(dpib[:, cols], wi_ref[hd]))
            dwr_ref[hd] = dwr_ref[hd] + _dot_tn(xcb[:, cols], dprb[:, cols])
            dwi_ref[hd] = dwi_ref[hd] + _dot_tn(xcb[:, cols], dpib[:, cols])
        dxc = dxc_direct + jnp.concatenate(dxc_gate, axis=1)
        dcb_ref[...] = dcb_ref[...] + _col_sum(dxc)
        cw = cw_ref[...]
        head = dxc_head[...]
        xa = xa_ref[...].astype(F32)
        dxa = cw[0:1] * dxc
        dcw_ref[0:1, :] = dcw_ref[0:1, :] + _col_sum(dxc * xa)
        for k in range(1, CONV_WIDTH):
            dxc_k = _shift_up(dxc, head, k)
            dxa = dxa + cw[k:k + 1] * dxc_k
            dcw_ref[k:k + 1, :] = dcw_ref[k:k + 1, :] + _col_sum(dxc_k * xa)
        dxc_head[...] = dxc[0:SUBLANES]
        dz_ref[:, :D_MODEL] = dxa.astype(BF16)

    def body(dh1_ref, pa_ref, pb_ref, z_ref, h_ref, h_prev_ref, xc_ref, r_ref, ig_ref, woa_ref, wob_ref, wout_ref,
             lg_ref, lb_ref, ws_ref, bias_ref, cw_ref, wr_ref, wi_ref, lam_ref, dz_ref, mg_ref, dpa_ref, dpb_ref,
             dh1b_ref, dlg_ref, dlb_ref, dws_ref, dbs_ref, dcw_ref, dcb_ref, dwr_ref, dbr_ref, dwi_ref, dbi_ref,
             dlam_ref, dya_ref, dyb_ref, dvn_ref, dsp_acc, lam_carry, dxc_head):
        def cols(ref, first, count):
            return ref.at[:, pl.ds(first * D_MODEL, count * D_MODEL)]

        merge_part(dh1_ref, pa_ref, pb_ref, cols(z_ref, 4, 2), woa_ref, wob_ref, wout_ref, cols(dz_ref, 4, 2), dya_ref,
                   dyb_ref, mg_ref, dpa_ref, dpb_ref, dh1b_ref)
        sgu_part(dyb_ref, cols(z_ref, 2, 1), cols(z_ref, 3, 1), lg_ref, lb_ref, ws_ref, bias_ref, cols(dz_ref, 2, 2),
                 dlg_ref, dlb_ref, dws_ref, dbs_ref, dvn_ref, dsp_acc)
        lru_part(dya_ref, cols(z_ref, 0, 1), cols(z_ref, 1, 1), h_ref, h_prev_ref, xc_ref, r_ref, ig_ref, cw_ref, wr_ref,
                 wi_ref, lam_ref, cols(dz_ref, 0, 2), dcw_ref, dcb_ref, dwr_ref, dbr_ref, dwi_ref, dbi_ref, dlam_ref,
                 lam_carry, dxc_head)

    rev = lambda i: n_tiles - 1 - i
    tile = pl.BlockSpec((SEQ_TILE, D_MODEL), lambda i: (rev(i), 0))
    row = pl.BlockSpec((SEQ_TILE, D_IN), lambda i: (rev(i), 0))
    prev8 = pl.BlockSpec((SUBLANES, D_MODEL), lambda i: (jnp.maximum(rev(i) * per_tile - 1, 0), 0))
    vec = _const((1, D_MODEL))
    sq = _resident((D_MODEL, D_MODEL))
    gate_w = _resident((HEADS, HEAD_DIM, HEAD_DIM))
    gate_acc = _const((HEADS, HEAD_DIM, HEAD_DIM))
    vec_shape = jax.ShapeDtypeStruct((1, D_MODEL), F32)
    gate_shape = jax.ShapeDtypeStruct((HEADS, HEAD_DIM, HEAD_DIM), F32)
    act_bf = jax.ShapeDtypeStruct((t, D_MODEL), BF16)
    return _fused_call(
        body, jobs, name="bwd_mix", grid=(n_tiles,),
        in_specs=[tile, tile, tile, row, tile, prev8, tile, tile, tile, sq, sq, sq, vec, vec,
                  _const((GROUPS, CHUNK, CHUNK)), _const((CHUNK, D_MODEL)), _const((CONV_WIDTH, D_MODEL)), gate_w, gate_w,
                  vec],
        out_specs=[row, tile, tile, tile, tile, vec, vec, _const((CHUNK, GROUPS * CHUNK)), _const((CHUNK, 128)),
                   _const((SUBLANES, D_MODEL)), vec, gate_acc, vec, gate_acc, vec, vec],
        out_shape=[jax.ShapeDtypeStruct((t, D_IN), BF16), act_bf, act_bf, act_bf, act_bf, vec_shape, vec_shape,
                   jax.ShapeDtypeStruct((CHUNK, GROUPS * CHUNK), F32), jax.ShapeDtypeStruct((CHUNK, 128), F32),
                   jax.ShapeDtypeStruct((SUBLANES, D_MODEL), F32), vec_shape, gate_shape, vec_shape, gate_shape,
                   vec_shape, vec_shape],
        scratch_shapes=[pltpu.VMEM((SEQ_TILE, D_MODEL), F32), pltpu.VMEM((SEQ_TILE, D_MODEL), F32),
                        pltpu.VMEM((SEQ_TILE, D_MODEL), F32), pltpu.VMEM((CHUNK, D_MODEL), F32),
                        pltpu.VMEM((1, D_MODEL), F32), pltpu.VMEM((SUBLANES, D_MODEL), F32)],
        compiler_params=_params(),
    )(dh1, pa, pb, z, h, h, xc, r, ig, w_oa, w_ob, w_out, ln_g, ln_b, w_s, bias_full, conv_w, wr, wi, lam)


def _bwd_in(dz, x, dh1, w_in_st, g1, jobs=()):
    t = x.shape[0]

    def body(dz_ref, x_ref, dh1_ref, w_ref, g_ref, dx_ref, dg1_ref):
        @pl.when(pl.program_id(0) == 0)
        def _():
            dg1_ref[...] = jnp.zeros_like(dg1_ref)

        dn1 = jnp.zeros((MM_TILE, D_MODEL), F32)
        for k in range(N_CHIPS):
            dn1 = dn1 + _dot_nt(dz_ref[:, k * IN_SHARD:(k + 1) * IN_SHARD], w_ref[k])
        xhat, r1 = _rms(x_ref[...])
        dg1_ref[...] = dg1_ref[...] + _col_sum(dn1 * xhat)
        dx_ref[...] = dh1_ref[...] + _rms_bwd(dn1 * g_ref[...], xhat, r1)

    tile = pl.BlockSpec((MM_TILE, D_MODEL), lambda i: (i, 0))
    return _fused_call(
        body, jobs, name="bwd_in", grid=(t // MM_TILE,),
        in_specs=[pl.BlockSpec((MM_TILE, D_IN), lambda i: (i, 0)), tile, tile,
                  _resident((N_CHIPS, D_MODEL, IN_SHARD)), _const((1, D_MODEL))],
        out_specs=[tile, _const((1, D_MODEL))],
        out_shape=[jax.ShapeDtypeStruct((t, D_MODEL), F32), jax.ShapeDtypeStruct((1, D_MODEL), F32)],
        compiler_params=_params(),
    )(dz, x, dh1, w_in_st, g1)


def _weight_grad(name, a, b, n_blocks, a_varies, b_varies, width, jobs=()):
    t = a.shape[0]
    rows = min(DW_TILE, t)
    n_t = t // rows

    def body(a_ref, b_ref, o_ref, acc_ref):
        s = pl.program_id(1)
        part = _dot_tn(a_ref[...], b_ref[...])

        @pl.when(s == 0)
        def _():
            acc_ref[...] = part

        @pl.when(s > 0)
        def _():
            acc_ref[...] = acc_ref[...] + part

        @pl.when(s == n_t - 1)
        def _():
            o_ref[...] = acc_ref[...].astype(BF16)

    return _fused_call(
        body, jobs, name=name, grid=(n_blocks, n_t),
        in_specs=[pl.BlockSpec((rows, D_MODEL), (lambda j, s: (s, j)) if a_varies else (lambda j, s: (s, 0))),
                  pl.BlockSpec((rows, width), (lambda j, s: (s, j)) if b_varies else (lambda j, s: (s, 0)))],
        out_specs=pl.BlockSpec((None, D_MODEL, width), lambda j, s: (j, 0, 0)),
        out_shape=jax.ShapeDtypeStruct((n_blocks, D_MODEL, width), BF16),
        scratch_shapes=[pltpu.VMEM((D_MODEL, width), F32)],
        compiler_params=_params(2),
    )(a, b)


def _weight_grads_square(name, pairs, jobs=()):
    n = len(pairs)
    t = pairs[0][0].shape[0]
    rows = min(2 * MM_TILE, t)
    n_t = t // rows

    def body(*refs):
        ins, outs, accs = refs[:2 * n], refs[2 * n:3 * n], refs[3 * n:]
        s = pl.program_id(0)
        for k in range(n):
            part = _dot_tn(ins[2 * k][...], ins[2 * k + 1][...])

            @pl.when(s == 0)
            def _(k=k, part=part):
                accs[k][...] = part

            @pl.when(s > 0)
            def _(k=k, part=part):
                accs[k][...] = accs[k][...] + part

            @pl.when(s == n_t - 1)
            def _(k=k):
                outs[k][...] = accs[k][...].astype(BF16)

    tile = pl.BlockSpec((rows, D_MODEL), lambda s: (s, 0))
    return _fused_call(
        body, jobs, name=name, grid=(n_t,), in_specs=[tile] * (2 * n), out_specs=[_const((D_MODEL, D_MODEL))] * n,
        out_shape=[jax.ShapeDtypeStruct((D_MODEL, D_MODEL), BF16)] * n,
        scratch_shapes=[pltpu.VMEM((D_MODEL, D_MODEL), F32)] * n,
        compiler_params=_params(),
    )(*[x for pair in pairs for x in pair])


def _place():
    x, y, c = lax.axis_index("x"), lax.axis_index("y"), lax.axis_index("c")
    other_chips = [(1 - x, y), (x, 1 - y), (1 - x, 1 - y)]
    return x, y, c, other_chips


def _chip_index(px, py):
    return 2 * px + py


ANY = pl.BlockSpec(memory_space=pl.ANY)
SIBLING = ((0, 0, 1),)
NEIGHBOURS = ((1, 0, 0), (0, 1, 0))
OTHER_CHIPS = NEIGHBOURS + ((1, 1, 0),)


def _near_far(x, y, c):
    return (x ^ (1 - c), y ^ c), (x ^ c, y ^ (1 - c))


def _gather_near_job(shards):
    n = len(shards)
    halves = [s.shape[0] // 2 for s in shards]

    def copies(ins, outs, send, recv, local):
        x, y, c, _ = _place()
        near, _ = _near_far(x, y, c)

        def block(w, chip, pc):
            return outs[w].at[_chip_index(*chip), pl.ds(pc * halves[w], halves[w]), :]

        def copy(w, k, chip, pc, to, src=None):
            return pltpu.make_async_remote_copy(
                src_ref=block(w, chip, pc) if src is None else src, dst_ref=block(w, chip, pc),
                send_sem=send.at[2 * w + k], recv_sem=recv.at[2 * w + k], device_id=to, device_id_type=MESH)

        sends, arrivals, own = [], [], []
        for w in range(n):
            src = ins[w].at[pl.ds(c * halves[w], halves[w]), :]
            own.append(pltpu.make_async_copy(src, block(w, (x, y), c), local.at[w]))
            sends += [copy(w, 0, (x, y), c, (*near, c), src), copy(w, 1, (x, y), c, (x, y, 1 - c), src)]
            arrivals += [copy(w, 0, near, c, (x, y, c)), copy(w, 1, (x, y), 1 - c, (x, y, c))]
        return sends, arrivals, own

    return _Job(shards, [jax.ShapeDtypeStruct((N_CHIPS,) + s.shape, s.dtype) for s in shards], 2 * n, copies,
                NEIGHBOURS + SIBLING, n_local=n)


def _gather_far_job(stacked):
    n = len(stacked)
    halves = [s.shape[1] // 2 for s in stacked]

    def copies(ins, outs, send, recv, local):
        del ins, local
        x, y, c, _ = _place()
        near, far = _near_far(x, y, c)

        def copy(w, k, chip):
            blk = outs[w].at[_chip_index(*chip), pl.ds(c * halves[w], halves[w]), :]
            return pltpu.make_async_remote_copy(
                src_ref=blk, dst_ref=blk, send_sem=send.at[2 * w + k], recv_sem=recv.at[2 * w + k],
                device_id=(*far, c), device_id_type=MESH)

        sends = [copy(w, k, chip) for w in range(n) for k, chip in enumerate(((x, y), near))]
        arrivals = [copy(w, k, chip) for w in range(n) for k, chip in enumerate((far, (1 - x, 1 - y)))]
        return sends, arrivals, []

    return _Job(stacked, [jax.ShapeDtypeStruct(s.shape, s.dtype) for s in stacked], 2 * n, copies, NEIGHBOURS,
                aliases={w: w for w in range(n)})


def _gather_pass_job(stacked):
    n = len(stacked)
    halves = [s.shape[1] // 2 for s in stacked]

    def copies(ins, outs, send, recv, local):
        del ins, local
        x, y, c, chips = _place()

        def copy(w, j, chip, pc, to):
            blk = outs[w].at[_chip_index(*chip), pl.ds(pc * halves[w], halves[w]), :]
            return pltpu.make_async_remote_copy(
                src_ref=blk, dst_ref=blk, send_sem=send.at[3 * w + j], recv_sem=recv.at[3 * w + j], device_id=to,
                device_id_type=MESH)

        sends = [copy(w, j, chip, c, (x, y, 1 - c)) for w in range(n) for j, chip in enumerate(chips)]
        arrivals = [copy(w, j, chip, 1 - c, (x, y, c)) for w in range(n) for j, chip in enumerate(chips)]
        return sends, arrivals, []

    return _Job(stacked, [jax.ShapeDtypeStruct(s.shape, s.dtype) for s in stacked], 3 * n, copies, SIBLING,
                aliases={w: w for w in range(n)})


def _gather_small_job(block):
    def copies(ins, outs, send, recv, local):
        x, y, c, chips = _place()

        def copy(j, chip_from, to):
            return pltpu.make_async_remote_copy(
                src_ref=ins[0], dst_ref=outs[0].at[_chip_index(*chip_from)], send_sem=send.at[j],
                recv_sem=recv.at[j], device_id=to, device_id_type=MESH)

        own = [pltpu.make_async_copy(ins[0], outs[0].at[_chip_index(x, y)], local.at[0])]
        sends = [copy(j, (x, y), (*chip, c)) for j, chip in enumerate(chips)]
        arrivals = [copy(j, chip, (x, y, c)) for j, chip in enumerate(chips)]
        return sends, arrivals, own

    return _Job([block], [jax.ShapeDtypeStruct((N_CHIPS,) + block.shape, block.dtype)], 3, copies, OTHER_CHIPS,
                n_local=1)


def _pair_send_job(grads):
    n = len(grads)
    halves = [g.shape[1] // 2 for g in grads]

    def copies(ins, outs, send, recv, local):
        del local
        x, y, c, _ = _place()
        sends = [pltpu.make_async_remote_copy(
            src_ref=ins[w].at[:, pl.ds((1 - c) * halves[w], halves[w]), :], dst_ref=outs[w], send_sem=send.at[w],
            recv_sem=recv.at[w], device_id=(x, y, 1 - c), device_id_type=MESH) for w in range(n)]
        return sends, sends, []

    return _Job(grads, [jax.ShapeDtypeStruct((N_CHIPS, h, g.shape[2]), g.dtype) for g, h in zip(grads, halves)], n,
                copies, SIBLING)


ROW_STEPS = 4


def _pair_add(name, core, mine, theirs):
    n = len(mine)

    def body(core_ref, *refs):
        del core_ref
        for a_ref, b_ref, o_ref in zip(refs[:n], refs[n:2 * n], refs[2 * n:]):
            o_ref[...] = (a_ref[...].astype(F32) + b_ref[...].astype(F32)).astype(BF16)

    half = lambda a: pl.BlockSpec((None, None) + a.shape[2:], lambda k, core_ref: (k, core_ref[0], 0, 0))
    block = lambda b: pl.BlockSpec((None,) + b.shape[1:], lambda k, core_ref: (k, 0, 0))
    return pl.pallas_call(
        body, name=name,
        grid_spec=pltpu.PrefetchScalarGridSpec(
            num_scalar_prefetch=1, grid=(N_CHIPS,),
            in_specs=[half(a) for a in mine] + [block(b) for b in theirs], out_specs=[block(b) for b in theirs]),
        out_shape=[jax.ShapeDtypeStruct(b.shape, BF16) for b in theirs],
        compiler_params=_params(),
    )(core, *mine, *theirs)


def _sequencer_call(name, collective_id, job):
    steps, peers = job.phases, job.peers
    ins = [jax.new_ref(a, memory_space=pltpu.MemorySpace.HBM) for a in job.inputs]
    outs = [ins[{o: i for i, o in job.aliases.items()}[k]] if k in job.aliases.values()
            else jax.empty_ref(shape, memory_space=pltpu.MemorySpace.HBM) for k, shape in enumerate(job.out_shape)]
    sems = [pltpu.SemaphoreType.DMA((n,)) for step in steps for n in (step.n_sem, step.n_sem, max(step.n_local, 1))]

    @pl.kernel(mesh=plsc.ScalarSubcoreMesh(axis_name="sequencer", num_cores=1), name=name, scratch_types=tuple(sems),
               compiler_params=pltpu.CompilerParams(collective_id=collective_id))
    def launch(*sem_refs):
        x, y, c, _ = _place()
        barrier = pltpu.get_barrier_semaphore()
        for dx, dy, dc in peers:
            pl.semaphore_signal(barrier, inc=1, device_id=(x ^ dx, y ^ dy, c ^ dc), device_id_type=MESH)
        pl.semaphore_wait(barrier, len(peers))
        for k, step in enumerate(steps):
            sends, arrivals, own = step.copies(ins if k == 0 else outs, outs, *sem_refs[3 * k:3 * k + 3])
            for cp in own + sends:
                cp.start()
            for cp in arrivals:
                cp.wait_recv()
            for cp in sends:
                cp.wait_send()
            for cp in own:
                cp.wait()

    launch()
    return [ref[...] for ref in outs]


def _chip_exchange_job(sums):
    n = len(sums)

    def copies(ins, outs, send, recv, local):
        del local
        _, _, c, chips = _place()
        sends = [pltpu.make_async_remote_copy(
            src_ref=ins[w].at[_chip_index(*chip)], dst_ref=outs[w].at[j], send_sem=send.at[3 * w + j],
            recv_sem=recv.at[3 * w + j], device_id=(*chip, c), device_id_type=MESH)
            for w in range(n) for j, chip in enumerate(chips)]
        return sends, sends, []

    return _Job(sums, [jax.ShapeDtypeStruct((N_CHIPS - 1,) + s.shape[1:], s.dtype) for s in sums], 3 * n, copies,
                OTHER_CHIPS)


def _chip_sum(name, place, mine, theirs):
    n = len(mine)

    def body(place_ref, *refs):
        del place_ref
        for p_ref, q_ref, o_ref in zip(refs[:n], refs[n:2 * n], refs[2 * n:]):
            acc = p_ref[...].astype(F32)
            for j in range(N_CHIPS - 1):
                acc = acc + q_ref[j].astype(F32)
            o_ref[...] = acc

    def block(p, lead, pick):
        return pl.BlockSpec((lead, p.shape[1] // ROW_STEPS, p.shape[2]), lambda r, place_ref: (pick(place_ref), r, 0))

    return pl.pallas_call(
        body, name=name,
        grid_spec=pltpu.PrefetchScalarGridSpec(
            num_scalar_prefetch=1, grid=(ROW_STEPS,),
            in_specs=[block(p, None, lambda place_ref: place_ref[0]) for p in mine]
            + [block(p, N_CHIPS - 1, lambda place_ref: 0) for p in mine],
            out_specs=[block(p, None, lambda place_ref: place_ref[1]) for p in mine]),
        out_shape=[jax.ShapeDtypeStruct((2,) + p.shape[1:], F32) for p in mine],
        compiler_params=_params(),
    )(place, *mine, *theirs)


def _share_job(bufs):
    n = len(bufs)

    def copies(ins, outs, send, recv, local):
        del ins, local
        x, y, c, _ = _place()

        def copy(w, half):
            return pltpu.make_async_remote_copy(
                src_ref=outs[w].at[half], dst_ref=outs[w].at[half], send_sem=send.at[w], recv_sem=recv.at[w],
                device_id=(x, y, 1 - c), device_id_type=MESH)

        return [copy(w, c) for w in range(n)], [copy(w, 1 - c) for w in range(n)], []

    return _Job(bufs, [jax.ShapeDtypeStruct(b.shape, b.dtype) for b in bufs], n, copies, SIBLING,
                aliases={w: w for w in range(n)})


SMALL_ROWS = 24
ROW_G1, ROW_CW, ROW_CB, ROW_BR, ROW_BI, ROW_LAM, ROW_LG, ROW_LB, ROW_G2, ROW_G3, ROW_LOSS, ROW_BS = (
    0, 1, 5, 6, 7, 8, 9, 10, 11, 12, 13, 16)
N_DEV = 8


def _pack_small(dcw, dcb, dbr, dbi, dlam, dlg, dlb, dg2, dg3, loss, dbs):
    def body(dcw_ref, dcb_ref, dbr_ref, dbi_ref, dlam_ref, dlg_ref, dlb_ref, dg2_ref, dg3_ref, loss_ref, dbs_ref, out):
        out[...] = jnp.zeros((SMALL_ROWS, D_MODEL), F32)
        for row, ref in ((ROW_CB, dcb_ref), (ROW_BR, dbr_ref), (ROW_BI, dbi_ref), (ROW_LAM, dlam_ref),
                         (ROW_LG, dlg_ref), (ROW_LB, dlb_ref), (ROW_G2, dg2_ref), (ROW_G3, dg3_ref)):
            out[row:row + 1, :] = ref[...]
        out[ROW_CW:ROW_CW + CONV_WIDTH, :] = dcw_ref[0:CONV_WIDTH, :]
        out[ROW_LOSS:ROW_LOSS + 1, 0:128] = loss_ref[0:1, :]
        out[ROW_BS:ROW_BS + GROUPS, 0:128] = jnp.transpose(dbs_ref[...])[0:GROUPS, :]

    vm = pl.BlockSpec(memory_space=pltpu.VMEM)
    return pl.pallas_call(
        body, name="pack_small", in_specs=[vm] * 11, out_specs=vm,
        out_shape=jax.ShapeDtypeStruct((SMALL_ROWS, D_MODEL), F32),
    )(dcw, dcb, dbr, dbi, dlam, dlg, dlb, dg2, dg3, loss, dbs)


def _gather_all_job(blocks):
    n = len(blocks)
    flips = [(dx, dy, dc) for dx in (0, 1) for dy in (0, 1) for dc in (0, 1)][1:]

    def copies(ins, outs, send, recv, local):
        x, y, c, _ = _place()
        me = 4 * x + 2 * y + c
        sends, arrivals, own = [], [], []
        for w in range(n):
            own.append(pltpu.make_async_copy(ins[w], outs[w].at[me], local.at[w]))
            for k, (dx, dy, dc) in enumerate(flips):
                peer = (x ^ dx, y ^ dy, c ^ dc)
                sem = dict(send_sem=send.at[7 * w + k], recv_sem=recv.at[7 * w + k])
                sends.append(pltpu.make_async_remote_copy(
                    src_ref=ins[w], dst_ref=outs[w].at[me], device_id=peer, device_id_type=MESH, **sem))
                arrivals.append(pltpu.make_async_remote_copy(
                    src_ref=ins[w], dst_ref=outs[w].at[4 * peer[0] + 2 * peer[1] + peer[2]], device_id=peer,
                    device_id_type=MESH, **sem))
        return sends, arrivals, own

    return _Job(blocks, [jax.ShapeDtypeStruct((N_DEV,) + b.shape, b.dtype) for b in blocks], 7 * n, copies,
                OTHER_CHIPS + SIBLING + tuple((dx, dy, 1) for dx, dy, _ in OTHER_CHIPS), n_local=n)


def _sum_small(vec_all, ws_all, dg1_all):
    def body(vec_ref, ws_ref, dg1_ref, vec_out, ws_out):
        vec, ws, dg1 = vec_ref[0], ws_ref[0], dg1_ref[0]
        for d in range(1, N_DEV):
            vec, ws, dg1 = vec + vec_ref[d], ws + ws_ref[d], dg1 + dg1_ref[d]
        vec_out[...] = vec
        vec_out[ROW_G1:ROW_G1 + 1, :] = dg1
        ws_out[...] = ws

    vm = pl.BlockSpec(memory_space=pltpu.VMEM)
    return pl.pallas_call(
        body, name="sum_small", in_specs=[vm] * 3, out_specs=[vm, vm],
        out_shape=[jax.ShapeDtypeStruct(vec_all.shape[1:], F32), jax.ShapeDtypeStruct(ws_all.shape[1:], F32)],
    )(vec_all, ws_all, dg1_all)


def _adamw_math(w, g, m, v):
    m = ADAM_B1 * m + (1.0 - ADAM_B1) * g
    v = ADAM_B2 * v + (1.0 - ADAM_B2) * (g * g)
    m_hat = m / (1.0 - ADAM_B1 ** ADAM_STEP)
    v_hat = v / (1.0 - ADAM_B2 ** ADAM_STEP)
    delta = (-ADAM_LR) * (m_hat / (jnp.sqrt(v_hat) + ADAM_EPS) + ADAM_WD * w)
    return delta, m, v


def _adamw(name, gs, ws, ms, vs):
    n = len(ws)

    def body(*refs):
        ins, outs = refs[:4 * n], refs[4 * n:]
        for p in range(n):
            g_ref, w_ref, m_ref, v_ref = ins[p::n]
            g = g_ref[...]
            outs[4 * p][...] = g
            outs[4 * p + 1][...], outs[4 * p + 2][...], outs[4 * p + 3][...] = _adamw_math(
                w_ref[...], g, m_ref[...], v_ref[...])

    blocks = [pl.BlockSpec((w.shape[0] // ROW_STEPS, w.shape[1]), lambda r: (r, 0)) for w in ws]
    out = pl.pallas_call(
        body, name=name, grid=(ROW_STEPS,), in_specs=blocks * 4, out_specs=[b for b in blocks for _ in range(4)],
        out_shape=[jax.ShapeDtypeStruct(w.shape, F32) for w in ws for _ in range(4)], compiler_params=_params(),
    )(*gs, *ws, *ms, *vs)
    return [tuple(out[4 * p:4 * p + 4]) for p in range(n)]


def _adamw_small(grads, ws, ms, vs):
    n = len(grads)

    def body(*refs):
        g_refs, w_refs, m_refs, v_refs = refs[:n], refs[n:2 * n], refs[2 * n:3 * n], refs[3 * n:4 * n]
        outs = refs[4 * n:]
        for p in range(n):
            d, nm, nv = _adamw_math(w_refs[p][...], g_refs[p][...], m_refs[p][...], v_refs[p][...])
            outs[p][...] = d
            outs[n + p][...] = nm
            outs[2 * n + p][...] = nv

    vm = pl.BlockSpec(memory_space=pltpu.VMEM)
    shapes = [jax.ShapeDtypeStruct(w.shape, F32) for w in ws]
    out = pl.pallas_call(
        body, name="adamw_small", in_specs=[vm] * (4 * n), out_specs=[vm] * (3 * n), out_shape=shapes * 3,
    )(*grads, *ws, *ms, *vs)
    return out[:n], out[n:2 * n], out[2 * n:]


def _unstack_heads(w_st):
    per = HEAD_DIM // N_CHIPS
    return w_st.reshape(N_CHIPS, HEADS, per, HEAD_DIM).transpose(1, 0, 2, 3).reshape(HEADS, HEAD_DIM, HEAD_DIM)


def _stack_heads(w):
    per = HEAD_DIM // N_CHIPS
    return w.reshape(HEADS, N_CHIPS, per, HEAD_DIM).transpose(1, 0, 2, 3).reshape(N_CHIPS, HEADS * per, HEAD_DIM)


def kernel(x, norm_mix_g, w_in, conv_w, conv_b, w_rgate, b_rgate, w_igate, b_igate, lru_lambda, w_out_a, sgu_ln_g, sgu_ln_b, sgu_w_s, sgu_b_s, w_out_b, w_out, norm_mlp_g, w_up, w_down, norm_final_g, loss_target, m_norm_mix_g, m_w_in, m_conv_w, m_conv_b, m_w_rgate, m_b_rgate, m_w_igate, m_b_igate, m_lru_lambda, m_w_out_a, m_sgu_ln_g, m_sgu_ln_b, m_sgu_w_s, m_sgu_b_s, m_w_out_b, m_w_out, m_norm_mlp_g, m_w_up, m_w_down, m_norm_final_g, v_norm_mix_g, v_w_in, v_conv_w, v_conv_b, v_w_rgate, v_b_rgate, v_w_igate, v_b_igate, v_lru_lambda, v_w_out_a, v_sgu_ln_g, v_sgu_ln_b, v_sgu_w_s, v_sgu_b_s, v_w_out_b, v_w_out, v_norm_mlp_g, v_w_up, v_w_down, v_norm_final_g):
    chip = _chip_index(lax.axis_index("x"), lax.axis_index("y"))
    core = lax.axis_index("c")
    quarter_h = HEAD_DIM // N_CHIPS
    quarter_d = D_MODEL // N_CHIPS

    as_2d = lambda a: a.reshape(-1, a.shape[-1])
    big_w = [as_2d(w) for w in (w_in, w_rgate, w_igate, w_out_a, w_out_b, w_out, w_up, w_down)]
    big_m = [as_2d(w) for w in (m_w_in, m_w_rgate, m_w_igate, m_w_out_a, m_w_out_b, m_w_out, m_w_up, m_w_down)]
    big_v = [as_2d(w) for w in (v_w_in, v_w_rgate, v_w_igate, v_w_out_a, v_w_out_b, v_w_out, v_w_up, v_w_down)]

    packed = jnp.concatenate([conv_w[0], b_rgate[0], b_igate[0]], axis=1)
    packed = jnp.concatenate([packed, jnp.zeros_like(packed)], axis=0)
    s_in, s_r, s_i, s_oa, s_ob, s_out, s_up, s_down = [w.astype(BF16) for w in big_w]
    xs, target = x[0], loss_target[0]
    g3 = norm_final_g.reshape(1, D_MODEL)
    bias_s = jnp.broadcast_to(jnp.transpose(sgu_b_s[0])[:, :, None], (CHUNK, GROUPS, GROUP_DIM)).reshape(CHUNK, D_MODEL)
    core_arr = core.reshape(1).astype(jnp.int32)
    place = jnp.stack([chip, core]).astype(jnp.int32)
    quarter = lambda g: g.reshape(N_CHIPS, D_MODEL // N_CHIPS, D_MODEL)

    def pair_add(nm, grads, from_sibling):
        halves = [g.reshape(N_CHIPS, 2, g.shape[1] // 2, g.shape[2]) for g in grads]
        return list(_pair_add("pair_add_" + nm, core_arr, halves, from_sibling))

    def chip_sum(nm, pairs, from_chips):
        return list(_chip_sum("chip_sum_" + nm, place, pairs, from_chips))

    order = jnp.stack([chip, chip ^ 2, chip ^ 1, chip ^ 3]).astype(jnp.int32)
    (z, n1, (w_in_st, wr_st, wi_st)), ((packed_all,), late) = _fwd_in(
        xs, norm_mix_g, [s_in, s_r, s_i], order,
        jobs=[_gather_small_job(packed), _gather_near_job([s_oa, s_ob, s_out])])
    pick = lambda lo, hi: packed_all[:, :HEADS, lo:hi].transpose(1, 0, 2).reshape(HEADS, -1)
    conv_w_full = pick(0, quarter_d)
    br_full = pick(quarter_d, quarter_d + quarter_h).reshape(1, D_MODEL)
    bi_full = pick(quarter_d + quarter_h, quarter_d + 2 * quarter_h).reshape(1, D_MODEL)
    wr, wi = _unstack_heads(wr_st), _unstack_heads(wi_st)
    lru = (conv_w_full, conv_b, wr, br_full, wi, bi_full, lru_lambda)
    sgu = (sgu_ln_g, sgu_ln_b, sgu_w_s[0], bias_s)

    after = lambda arrays, result: lax.optimization_barrier((arrays, result))[0]
    w_up_st, w_dn = _sequencer_call(
        "gather_mlp", 8, _gather_near_job(after([s_up, s_down], n1)).then(_gather_far_job).then(_gather_pass_job))
    w_dn = w_dn.reshape(D_FF, D_MODEL)
    late_step = (3 * (xs.shape[0] // SEQ_TILE) // 4,)
    (ya, *saved), (late,) = _fwd_lru(z, *lru, jobs=[_gather_far_job(late).then(_gather_pass_job, at=late_step)])
    w_oa, w_ob, w_o = [w.reshape(D_MODEL, D_MODEL) for w in late]
    (yb, pa, pb, h1, n2), _ = _fwd_sgu_merge(ya, z, xs, *sgu, w_oa, w_ob, w_o, norm_mlp_g)
    (act, dup, dh2b, dh1, loss_part, dg3, dg2), _ = _mlp(n2, h1, target, w_up_st, w_dn, norm_mlp_g, g3)

    d_down, _ = _weight_grad("dw_down", act, dh2b, N_CHIPS, True, False, D_MODEL)
    r_down, = _sequencer_call("send_w_down", 10, _pair_send_job([d_down]))
    d_up, _ = _weight_grad("dw_up", n2, dup, N_CHIPS, False, True, D_MODEL)
    r_up, = _sequencer_call("send_w_up", 11, _pair_send_job([d_up]))
    (p_down,), (p_up,) = pair_add("w_down", [d_down], [r_down]), pair_add("w_up", [d_up], [r_up])
    (dz, merged, dpa, dpb, dh1b, dlg, dlb, dws, dbs, dcw, dcb, dwr, dbr, dwi, dbi, dlam), ((q_up, q_down),) = _bwd_mix(
        dh1, pa, pb, z, *saved, w_oa, w_ob, w_o, *sgu, conv_w_full, wr, wi, lru_lambda,
        jobs=[_chip_exchange_job([p_up, p_down])])
    names = ("w_in", "w_rgate", "w_igate", "w_out_a", "w_out_b", "w_out", "w_up", "w_down")
    (d_out, d_oa, d_ob), _ = _weight_grads_square("dw_projections", [(merged, dh1b), (ya, dpa), (yb, dpb)])
    mids = [quarter(d_oa), quarter(d_ob), quarter(d_out)]
    r_mids = _sequencer_call("send_mids", 1, _pair_send_job(mids))
    half_up, half_down = chip_sum("mlp", [p_up, p_down], after([q_up, q_down], mids))
    gates = [_stack_heads(dwr).astype(BF16), _stack_heads(dwi).astype(BF16)]
    small = _pack_small(dcw, dcb, dbr, dbi, dlam, dlg, dlb, dg2, dg3, loss_part, dbs)
    p_mids = pair_add("projections", mids, r_mids)
    q_mids = _sequencer_call("exchange_mids", 2, _chip_exchange_job(p_mids))
    d_in, (r_gates, (vec_all, ws_all), (full_up, full_down)) = _weight_grad(
        "dw_in", n1, after(dz, p_mids), N_CHIPS, False, True, IN_SHARD,
        jobs=[_pair_send_job(gates), _gather_all_job([small, dws]), _share_job([half_up, half_down])])
    r_in, = _sequencer_call("send_w_in", 3, _pair_send_job(after([d_in], q_mids)))
    adam_args = {nm: (w, m, v) for nm, w, m, v in zip(names, big_w, big_m, big_v)}

    def adamw(group, nms, grads):
        given = [adam_args[nm] for nm in nms]
        grads = [g.reshape(w.shape) for g, (w, _, _) in zip(grads, given)]
        outs = _adamw("adamw_" + group, grads, *[[a[q] for a in given] for q in range(3)])
        return {nm: (out[0], out[1:]) for nm, out in zip(nms, outs)}

    p_gates = pair_add("gates", gates, r_gates)
    half_mids = chip_sum("projections", p_mids, q_mids)
    full_mids = _sequencer_call("share_mids", 12, _share_job(half_mids))
    p_first = pair_add("w_in", after([d_in], half_mids), [r_in]) + p_gates
    q_first = _sequencer_call("exchange_w_in", 4, _chip_exchange_job(p_first))
    (grad_x, dg1), _ = _bwd_in(dz, xs, dh1, w_in_st, norm_mix_g)
    dg1_all, = _sequencer_call("gather_dg1", 6, _gather_all_job([dg1]))
    done = adamw("mlp", ("w_up", "w_down"), [full_up, full_down])
    q_first = after(q_first, [out[0] for _, out in done.values()])
    half_first = chip_sum("first", p_first, q_first)
    full_first = _sequencer_call("share_last", 5, _share_job(half_first))
    done.update(adamw("projections", names[3:6], after(full_mids, half_first)))
    done.update(adamw("first", names[:3], full_first))
    full, big_out = [done[nm][0] for nm in names], [done[nm][1] for nm in names]

    vec, ws_sum = _sum_small(vec_all, ws_all, dg1_all)
    row = lambda r: vec[r:r + 1]
    shard = lambda a, width: lax.dynamic_slice_in_dim(a, chip * width, width, axis=1)
    g_small = dict(
        norm_mix_g=row(ROW_G1), conv_w=shard(vec[ROW_CW:ROW_CW + CONV_WIDTH], quarter_d), conv_b=row(ROW_CB),
        b_rgate=shard(row(ROW_BR).reshape(HEADS, HEAD_DIM), quarter_h),
        b_igate=shard(row(ROW_BI).reshape(HEADS, HEAD_DIM), quarter_h), lru_lambda=row(ROW_LAM),
        sgu_ln_g=row(ROW_LG), sgu_ln_b=row(ROW_LB),
        sgu_w_s=ws_sum.reshape(CHUNK, GROUPS, CHUNK).transpose(1, 0, 2).reshape(GROUPS * CHUNK, CHUNK),
        sgu_b_s=vec[ROW_BS:ROW_BS + GROUPS, 0:CHUNK], norm_mlp_g=row(ROW_G2), norm_final_g=row(ROW_G3))
    loss = vec[ROW_LOSS, 0]
    small_names = list(g_small)
    given = dict(
        norm_mix_g=(norm_mix_g, m_norm_mix_g, v_norm_mix_g), conv_w=(conv_w, m_conv_w, v_conv_w),
        conv_b=(conv_b, m_conv_b, v_conv_b), b_rgate=(b_rgate, m_b_rgate, v_b_rgate),
        b_igate=(b_igate, m_b_igate, v_b_igate), lru_lambda=(lru_lambda, m_lru_lambda, v_lru_lambda),
        sgu_ln_g=(sgu_ln_g, m_sgu_ln_g, v_sgu_ln_g), sgu_ln_b=(sgu_ln_b, m_sgu_ln_b, v_sgu_ln_b),
        sgu_w_s=(sgu_w_s, m_sgu_w_s, v_sgu_w_s), sgu_b_s=(sgu_b_s, m_sgu_b_s, v_sgu_b_s),
        norm_mlp_g=(norm_mlp_g, m_norm_mlp_g, v_norm_mlp_g), norm_final_g=(norm_final_g, m_norm_final_g, v_norm_final_g))
    g2d = [g_small[nm] for nm in small_names]
    to2d = lambda a, g: a.reshape(g.shape)
    d_s, m_s, v_s = _adamw_small(
        g2d, *[[to2d(given[nm][q], g) for nm, g in zip(small_names, g2d)] for q in range(3)])

    shapes = dict(
        norm_mix_g=norm_mix_g, w_in=w_in, conv_w=conv_w, conv_b=conv_b, w_rgate=w_rgate, b_rgate=b_rgate,
        w_igate=w_igate, b_igate=b_igate, lru_lambda=lru_lambda, w_out_a=w_out_a, sgu_ln_g=sgu_ln_g,
        sgu_ln_b=sgu_ln_b, sgu_w_s=sgu_w_s, sgu_b_s=sgu_b_s, w_out_b=w_out_b, w_out=w_out, norm_mlp_g=norm_mlp_g,
        w_up=w_up, w_down=w_down, norm_final_g=norm_final_g)
    grads, deltas, new_m, new_v = {}, {}, {}, {}
    for nm, g, (d, nmom, nvar) in zip(names, full, big_out):
        grads[nm], deltas[nm], new_m[nm], new_v[nm] = g, d, nmom, nvar
    for p, nm in enumerate(small_names):
        grads[nm], deltas[nm], new_m[nm], new_v[nm] = g2d[p], d_s[p], m_s[p], v_s[p]
    order = list(shapes)
    out = [loss, grad_x[None]]
    for group in (grads, deltas, new_m, new_v):
        out += [group[nm].reshape(shapes[nm].shape) for nm in order]
    return tuple(out)
```

```python
import functools

import jax
import jax.numpy as jnp
from jax import lax
from jax.experimental import pallas as pl
from jax.experimental.pallas import tpu as pltpu
from jax.experimental.pallas import tpu_sc as plsc

F32 = jnp.float32
BF16 = jnp.bfloat16
MESH = pl.DeviceIdType.MESH

D_MODEL = 1024
D_IN = 6 * D_MODEL
D_FF = 4 * D_MODEL
N_CHIPS = 4
IN_SHARD = D_IN // N_CHIPS
HEADS = 4
HEAD_DIM = D_MODEL // HEADS
GROUPS = 4
GROUP_DIM = D_MODEL // GROUPS
CHUNK = 128
CONV_WIDTH = 4
LRU_C = 8.0
NORM_EPS = 1e-6
LN_EPS = 1e-5

ADAM_LR = 0.001
ADAM_B1 = 0.9
ADAM_B2 = 0.999
ADAM_EPS = 1e-08
ADAM_WD = 0.01
ADAM_STEP = 10

SUBLANES = 8
MM_TILE = 512
IN_TILE = 1024
SEQ_TILE = 256
DW_TILE = 2048
VMEM_LIMIT_BYTES = 56 * 1024 * 1024

GELU_K0 = 0.7978845608028654
GELU_K1 = 0.044715


def _params(n_grid_axes=1):
    return pltpu.CompilerParams(
        dimension_semantics=("arbitrary",) * n_grid_axes, vmem_limit_bytes=VMEM_LIMIT_BYTES)


def _resident(shape):
    nd = len(shape)
    return pl.BlockSpec(shape, lambda *_: (0,) * nd, pipeline_mode=pl.Buffered(1))


def _const(shape):
    nd = len(shape)
    return pl.BlockSpec(shape, lambda *_: (0,) * nd)


def _dot(a, b):
    return jnp.dot(a, b, preferred_element_type=F32)


def _dot_nt(a, b):
    return lax.dot_general(a, b, (((1,), (1,)), ((), ())), preferred_element_type=F32)


def _dot_tn(a, b):
    return lax.dot_general(a, b, (((0,), (0,)), ((), ())), preferred_element_type=F32)


def _gelu(x):
    t = jnp.tanh(x * (GELU_K0 + (GELU_K0 * GELU_K1) * (x * x)))
    return x * (0.5 + 0.5 * t)


def _gelu_and_grad(x):
    x2 = x * x
    t = jnp.tanh(x * (GELU_K0 + (GELU_K0 * GELU_K1) * x2))
    s = 0.5 + 0.5 * t
    dg = s + (x * (1.0 - t * t)) * (0.5 * GELU_K0 + (1.5 * GELU_K0 * GELU_K1) * x2)
    return x * s, dg


def _gate(x):
    return 0.5 + 0.5 * jnp.tanh(0.5 * x.astype(F32))


def _rms(x):
    r = lax.rsqrt(jnp.mean(x * x, axis=-1, keepdims=True) + NORM_EPS)
    return x * r, r


def _rms_bwd(dn, xhat, r):
    return r * (dn - xhat * jnp.mean(dn * xhat, axis=-1, keepdims=True))


def _col_sum(v):
    return jnp.sum(v, axis=0, keepdims=True)


def _shift_down(x, tail8, k):
    xs = pltpu.roll(x, k, 0)
    ts = pltpu.roll(tail8, k, 0)
    ridx = lax.broadcasted_iota(jnp.int32, tail8.shape, 0)
    head = jnp.where(ridx < k, ts, xs[0:SUBLANES])
    return jnp.concatenate([head, xs[SUBLANES:]], axis=0)


def _shift_up(x, head8, k):
    n = x.shape[0]
    xs = pltpu.roll(x, n - k, 0)
    hs = pltpu.roll(head8, SUBLANES - k, 0)
    ridx = lax.broadcasted_iota(jnp.int32, head8.shape, 0)
    last = jnp.where(ridx >= SUBLANES - k, hs, xs[n - SUBLANES:n])
    return jnp.concatenate([xs[:n - SUBLANES], last], axis=0)


def _scan_forward(a, b, carry):
    n, cols = a.shape
    groups = n // SUBLANES
    a = a.reshape(groups, SUBLANES, cols)
    b = b.reshape(groups, SUBLANES, cols)
    sub = lax.broadcasted_iota(jnp.int32, a.shape, 1)
    for s in (1, 2, 4):
        a_s = pltpu.roll(a, s, 1)
        b_s = pltpu.roll(b, s, 1)
        m = sub >= s
        b = jnp.where(m, a * b_s + b, b)
        a = jnp.where(m, a * a_s, a)
    out = []
    for g in range(groups):
        h = a[g] * carry + b[g]
        out.append(h)
        carry = h[SUBLANES - 1:SUBLANES]
    return jnp.concatenate(out, axis=0), carry


def _scan_backward(a, b, carry):
    n, cols = a.shape
    groups = n // SUBLANES
    a = a.reshape(groups, SUBLANES, cols)
    b = b.reshape(groups, SUBLANES, cols)
    sub = lax.broadcasted_iota(jnp.int32, a.shape, 1)
    for s in (1, 2, 4):
        a_s = pltpu.roll(a, SUBLANES - s, 1)
        b_s = pltpu.roll(b, SUBLANES - s, 1)
        m = sub < SUBLANES - s
        b = jnp.where(m, a * b_s + b, b)
        a = jnp.where(m, a * a_s, a)
    out = [None] * groups
    for g in reversed(range(groups)):
        h = a[g] * carry + b[g]
        out[g] = h
        carry = h[0:1]
    return jnp.concatenate(out, axis=0), carry


def _softplus_neg(lam):
    e = jnp.exp(-jnp.abs(lam))
    u = 1.0 + e
    log1p_e = jnp.where(u == 1.0, e, jnp.log(u) * (e / jnp.where(u == 1.0, 1.0, u - 1.0)))
    return jnp.maximum(-lam, 0.0) + log1p_e


def _lru_gates(xa, tail8, cw_ref, cb_ref, wr_ref, br_ref, wi_ref, bi_ref, lam_ref):
    cw = cw_ref[...]
    xc = cb_ref[...] + cw[0:1] * xa
    for k in range(1, CONV_WIDTH):
        xc = xc + cw[k:k + 1] * _shift_down(xa, tail8, k)
    xcb = xc.astype(BF16)
    pre_r, pre_i = [], []
    for h in range(HEADS):
        cols = slice(h * HEAD_DIM, (h + 1) * HEAD_DIM)
        pre_r.append(_dot(xcb[:, cols], wr_ref[h]))
        pre_i.append(_dot(xcb[:, cols], wi_ref[h]))
    r = jax.nn.sigmoid(jnp.concatenate(pre_r, axis=1) + br_ref[...])
    ig = jax.nn.sigmoid(jnp.concatenate(pre_i, axis=1) + bi_ref[...])
    _, a, mult, _ = _decay(r, lam_ref)
    return xc, r, ig, a, mult


def _decay(r, lam_ref):
    sp = _softplus_neg(lam_ref[...])
    log_a = ((-LRU_C) * sp) * r
    a = jnp.exp(log_a)
    th = jnp.tanh(log_a)
    q = (-2.0 * th) / (1.0 - th)
    inv = lax.rsqrt(q)
    return sp, a, jnp.where(q > 0.0, q * inv, 0.0), inv


class _Phase:
    def __init__(self, copies, n_sem, n_local, start=None, finish=None):
        self.copies, self.n_sem, self.n_local, self.start, self.finish = copies, n_sem, n_local, start, finish


class _Job:
    def __init__(self, inputs, out_shape, n_sem, copies, peers, aliases=None, n_local=0):
        self.inputs, self.out_shape = list(inputs), list(out_shape)
        self.aliases = dict(aliases or {})
        self.phases = [_Phase(copies, n_sem, n_local)]
        self.peers = tuple(peers)

    def then(self, make, at=None):
        nxt = make(self.out_shape)
        self.phases[-1].finish = at
        nxt.phases[0].start = at
        self.phases += nxt.phases
        self.peers = tuple(sorted(set(self.peers + nxt.peers)))
        return self


def _fused_call(body, jobs, *, name, grid, in_specs, out_specs, out_shape, scratch_shapes=(),
                input_output_aliases=None, compiler_params=None, n_prefetch=0, jobs_start_after=None):
    single = not isinstance(out_shape, (list, tuple))
    out_specs = [out_specs] if single else list(out_specs)
    out_shape = [out_shape] if single else list(out_shape)
    n_scr = len(scratch_shapes)
    in_specs, scratch_shapes = list(in_specs), list(scratch_shapes)
    n_in, n_out = len(in_specs), len(out_shape)
    aliases = dict(input_output_aliases or {})
    in_at, out_at, phases = [], [], []
    for q, job in enumerate(jobs):
        in_at.append(len(in_specs))
        out_at.append(len(out_shape))
        for i, o in job.aliases.items():
            aliases[n_prefetch + len(in_specs) + i] = len(out_shape) + o
        in_specs += [ANY] * len(job.inputs)
        out_specs += [ANY] * len(job.out_shape)
        out_shape += job.out_shape
        for k, phase in enumerate(job.phases):
            phases.append((q, k, phase, len(scratch_shapes)))
            scratch_shapes += [pltpu.SemaphoreType.DMA((phase.n_sem,)), pltpu.SemaphoreType.DMA((phase.n_sem,)),
                               pltpu.SemaphoreType.DMA((max(phase.n_local, 1),))]
    n_in_all, n_out_all = len(in_specs), len(out_shape)
    first_step, last_step = (0,) * len(grid), tuple(g - 1 for g in grid)

    def full_body(*refs):
        prefetch, refs = refs[:n_prefetch], refs[n_prefetch:]
        ins, outs, scr = refs[:n_in_all], refs[n_in_all:n_in_all + n_out_all], refs[n_in_all + n_out_all:]
        ids = [pl.program_id(a) for a in range(len(grid))]
        at_step = lambda step: functools.reduce(jnp.logical_and, [i == k for i, k in zip(ids, step)])

        def copies(q, k, phase, sem_at):
            job = jobs[q]
            mine = outs[out_at[q]:out_at[q] + len(job.out_shape)]
            return phase.copies(ins[in_at[q]:in_at[q] + len(job.inputs)] if k == 0 else mine, mine,
                                *scr[sem_at:sem_at + 3])

        def start(*phase):
            def go():
                sends, _, local = copies(*phase)
                for cp in local + sends:
                    cp.start()
            return go

        def finish(*phase):
            def go():
                sends, arrivals, local = copies(*phase)
                for cp in arrivals:
                    cp.wait_recv()
                for cp in sends:
                    cp.wait_send()
                for cp in local:
                    cp.wait()
            return go

        for phase in phases:
            if phase[2].start is None and jobs_start_after is None:
                pl.when(at_step(first_step))(start(*phase))
        body(*prefetch, *ins[:n_in], *outs[:n_out], *scr[:n_scr])
        for phase in phases:
            pl.when(at_step(phase[2].finish or last_step))(finish(*phase))
            nxt = phase[2].start or jobs_start_after
            if nxt is not None:
                pl.when(at_step(nxt))(start(*phase))

    if n_prefetch:
        layout = dict(grid_spec=pltpu.PrefetchScalarGridSpec(
            num_scalar_prefetch=n_prefetch, grid=grid, in_specs=in_specs, out_specs=out_specs,
            scratch_shapes=scratch_shapes))
    else:
        layout = dict(grid=grid, in_specs=in_specs, out_specs=out_specs, scratch_shapes=scratch_shapes)
    call = pl.pallas_call(
        full_body, name=name, out_shape=out_shape, input_output_aliases=aliases, compiler_params=compiler_params,
        **layout)

    def run(*args):
        res = call(*args, *[a for job in jobs for a in job.inputs])
        mine = res[0] if single else list(res[:n_out])
        return mine, [list(res[at:at + len(job.out_shape)]) for at, job in zip(out_at, jobs)]

    return run


def _fwd_in(x, g1, shards, order, jobs=()):
    t = x.shape[0]
    rows_per_step = min(IN_TILE, t)
    n_tiles = t // rows_per_step
    n = len(shards)
    halves = [s.shape[0] // 2 for s in shards]

    def body(order_ref, x_ref, g_ref, *refs):
        del order_ref
        ins, (z_ref, n_ref), outs = refs[:n], refs[n:n + 2], refs[n + 2:2 * n + 2]
        wbuf, nbuf, send, recv, local = refs[2 * n + 2:]
        s, i = pl.program_id(0), pl.program_id(1)
        x_, y_, c, chips = _place()
        near, far = _near_far(x_, y_, c)
        k_me = _chip_index(x_, y_)

        def block(w, chip, pc):
            return outs[w].at[_chip_index(*chip), pl.ds(pc * halves[w], halves[w]), :]

        def over_ici(w, j, landing):
            return pltpu.make_async_remote_copy(
                src_ref=ins[w].at[pl.ds(c * halves[w], halves[w]), :],
                dst_ref=block(w, chips[j] if landing else (x_, y_), c), send_sem=send.at[6 * w + j],
                recv_sem=recv.at[6 * w + j], device_id=(*chips[j], c), device_id_type=MESH)

        def onward(w, landing):
            blk = block(w, chips[2] if landing else near, c)
            return pltpu.make_async_remote_copy(
                src_ref=blk, dst_ref=blk, send_sem=send.at[6 * w + 2], recv_sem=recv.at[6 * w + 2],
                device_id=(*far, c), device_id_type=MESH)

        def to_sibling(w, j, landing):
            blk = block(w, chips[j], 1 - c if landing else c)
            return pltpu.make_async_remote_copy(
                src_ref=blk, dst_ref=blk, send_sem=send.at[6 * w + 3 + j], recv_sem=recv.at[6 * w + 3 + j],
                device_id=(x_, y_, 1 - c), device_id_type=MESH)

        own = [pltpu.make_async_copy(wbuf, outs[0].at[k_me], local.at[0])]
        own += [pltpu.make_async_copy(ins[w], outs[w].at[k_me], local.at[w]) for w in range(1, n)]

        @pl.when((s == 0) & (i == 0))
        def _():
            for j in range(2):
                for w in range(n):
                    over_ici(w, j, False).start()
            load = pltpu.make_async_copy(ins[0], wbuf, local.at[n])
            load.start()
            load.wait()
            for cp in own:
                cp.start()

        for j in range(N_CHIPS - 1):
            @pl.when((s == j + 1) & (i == 0))
            def _(j=j):
                if j == 0:
                    for k in range(2):
                        for w in range(n):
                            over_ici(w, k, True).wait_recv()
                    for w in range(n):
                        onward(w, False).start()
                    for k in range(2):
                        for w in range(n):
                            to_sibling(w, k, False).start()
                    own[0].wait()
                if j == 2:
                    for w in range(n):
                        onward(w, True).wait_recv()
                    for w in range(n):
                        to_sibling(w, j, False).start()
                for w in range(n):
                    to_sibling(w, j, True).wait_recv()
                load = pltpu.make_async_copy(outs[0].at[_chip_index(*chips[j])], wbuf, local.at[n])
                load.start()
                load.wait()

        rows = pl.ds(pl.multiple_of(i * rows_per_step, rows_per_step), rows_per_step)

        @pl.when(s == 0)
        def _():
            xhat, _ = _rms(x_ref[...])
            nrm = (xhat * g_ref[...]).astype(BF16)
            nbuf[rows, :] = nrm
            n_ref[...] = nrm

        z_ref[...] = _dot(nbuf[rows, :], wbuf[...]).astype(BF16)

        @pl.when((s == N_CHIPS - 1) & (i == n_tiles - 1))
        def _():
            for j in range(N_CHIPS - 1):
                for w in range(n):
                    (over_ici(w, j, False) if j < 2 else onward(w, False)).wait_send()
                    to_sibling(w, j, False).wait_send()
            for cp in own[1:]:
                cp.wait()

    once = lambda s, i, order: (jnp.where(s == 0, i, n_tiles - 1), 0)
    (z, n1, *stacked), job_outs = _fused_call(
        body, jobs, name="fwd_in", grid=(N_CHIPS, n_tiles), n_prefetch=1,
        in_specs=[pl.BlockSpec((rows_per_step, D_MODEL), once), _const((1, D_MODEL))] + [ANY] * n,
        out_specs=[pl.BlockSpec((rows_per_step, IN_SHARD), lambda s, i, order: (i, order[s])),
                   pl.BlockSpec((rows_per_step, D_MODEL), once)] + [ANY] * n,
        out_shape=[jax.ShapeDtypeStruct((t, D_IN), BF16), jax.ShapeDtypeStruct((t, D_MODEL), BF16)]
        + [jax.ShapeDtypeStruct((N_CHIPS,) + s.shape, s.dtype) for s in shards],
        scratch_shapes=[pltpu.VMEM(shards[0].shape, BF16), pltpu.VMEM((t, D_MODEL), BF16),
                        pltpu.SemaphoreType.DMA((6 * n,)),
                        pltpu.SemaphoreType.DMA((6 * n,)), pltpu.SemaphoreType.DMA((n + 1,))],
        compiler_params=_params(2), jobs_start_after=(1, 0),
    )(order, x, g1, *shards)
    return (z, n1, stacked), job_outs


def _fwd_lru(z, conv_w, conv_b, wr, br, wi, bi, lam, jobs=()):
    t = z.shape[0]

    def body(xa_ref, ga_ref, cw_ref, cb_ref, wr_ref, br_ref, wi_ref, bi_ref, lam_ref, ya_ref, h_ref, xc_ref, r_ref,
             ig_ref, tail_ref, carry_ref):
        @pl.when(pl.program_id(0) == 0)
        def _():
            tail_ref[...] = jnp.zeros_like(tail_ref)
            carry_ref[...] = jnp.zeros_like(carry_ref)

        xa = xa_ref[...].astype(F32)
        xc, r, ig, a, mult = _lru_gates(xa, tail_ref[...], cw_ref, cb_ref, wr_ref, br_ref, wi_ref, bi_ref, lam_ref)
        tail_ref[...] = xa[SEQ_TILE - SUBLANES:]
        xc_ref[...], r_ref[...], ig_ref[...] = xc, r, ig
        h, carry = _scan_forward(a, xc * ig * mult, carry_ref[...])
        carry_ref[...] = carry
        h_ref[...] = h
        ya_ref[...] = (h * _gelu(ga_ref[...].astype(F32))).astype(BF16)

    tile = lambda j: pl.BlockSpec((SEQ_TILE, D_MODEL), lambda i: (i, j))
    return _fused_call(
        body, jobs, name="fwd_lru", grid=(t // SEQ_TILE,),
        in_specs=[tile(0), tile(1), _const((CONV_WIDTH, D_MODEL)), _const((1, D_MODEL)),
                  _resident((HEADS, HEAD_DIM, HEAD_DIM)), _const((1, D_MODEL)),
                  _resident((HEADS, HEAD_DIM, HEAD_DIM)), _const((1, D_MODEL)), _const((1, D_MODEL))],
        out_specs=[tile(0)] * 5,
        out_shape=[jax.ShapeDtypeStruct((t, D_MODEL), BF16)] + [jax.ShapeDtypeStruct((t, D_MODEL), F32)] * 4,
        scratch_shapes=[pltpu.VMEM((SUBLANES, D_MODEL), F32), pltpu.VMEM((1, D_MODEL), F32)],
        compiler_params=_params(),
    )(z, z, conv_w, conv_b, wr, br, wi, bi, lam)


def _sgu_forward_parts(ub, vb, lg_ref, lb_ref):
    u, du = _gelu_and_grad(ub.astype(F32))
    vg, dvg = _gelu_and_grad(vb.astype(F32))
    mu = jnp.mean(vg, axis=-1, keepdims=True)
    d = vg - mu
    rstd = lax.rsqrt(jnp.mean(d * d, axis=-1, keepdims=True) + LN_EPS)
    vhat = d * rstd
    vn = (vhat * lg_ref[...] + lb_ref[...]).astype(BF16)
    return u, du, dvg, rstd, vhat, vn


def _causal_mask():
    rows = lax.broadcasted_iota(jnp.int32, (CHUNK, CHUNK), 0)
    cols = lax.broadcasted_iota(jnp.int32, (CHUNK, CHUNK), 1)
    return rows >= cols


def _fwd_sgu_merge(ya, z, x, ln_g, ln_b, w_s, bias_full, w_oa, w_ob, w_out, g2, jobs=()):
    t = x.shape[0]

    def body(ya_ref, ub_ref, vb_ref, m_ref, x_ref, lg_ref, lb_ref, ws_ref, bias_ref, woa_ref, wob_ref, wout_ref, g_ref,
             yb_ref, pa_ref, pb_ref, h1_ref, n2_ref):
        u, _, _, _, _, vn = _sgu_forward_parts(ub_ref[...], vb_ref[...], lg_ref, lb_ref)
        mask = _causal_mask()
        wm = [jnp.where(mask, ws_ref[g], 0.0).astype(BF16) for g in range(GROUPS)]
        for c in range(SEQ_TILE // CHUNK):
            rows = slice(c * CHUNK, (c + 1) * CHUNK)
            for g in range(GROUPS):
                cols = slice(g * GROUP_DIM, (g + 1) * GROUP_DIM)
                sp = _dot(wm[g], vn[rows, cols]) + bias_ref[:, cols]
                yb_ref[rows, cols] = (u[rows, cols] * sp).astype(BF16)
        pa = _dot(ya_ref[...], woa_ref[...])
        pb = _dot(yb_ref[...], wob_ref[...])
        pa_ref[...] = pa
        pb_ref[...] = pb
        merged = _gate(m_ref[:, :D_MODEL]) * pa + _gate(m_ref[:, D_MODEL:]) * pb
        h1 = x_ref[...] + _dot(merged.astype(BF16), wout_ref[...])
        h1_ref[...] = h1
        xhat, _ = _rms(h1)
        n2_ref[...] = (xhat * g_ref[...]).astype(BF16)

    tile = lambda j: pl.BlockSpec((SEQ_TILE, D_MODEL), lambda i: (i, j))
    sq = _resident((D_MODEL, D_MODEL))
    vec = _const((1, D_MODEL))
    bf, f32 = jax.ShapeDtypeStruct((t, D_MODEL), BF16), jax.ShapeDtypeStruct((t, D_MODEL), F32)
    return _fused_call(
        body, jobs, name="fwd_sgu_merge", grid=(t // SEQ_TILE,),
        in_specs=[tile(0), tile(2), tile(3), pl.BlockSpec((SEQ_TILE, 2 * D_MODEL), lambda i: (i, 2)), tile(0), vec, vec,
                  _const((GROUPS, CHUNK, CHUNK)), _const((CHUNK, D_MODEL)), sq, sq, sq, vec],
        out_specs=[tile(0)] * 5,
        out_shape=[bf, f32, f32, f32, bf],
        compiler_params=_params(),
    )(ya, z, z, z, x, ln_g, ln_b, w_s, bias_full, w_oa, w_ob, w_out, g2)


def _mlp(n2, h1, target, w_up_st, w_down, g2, g3, jobs=()):
    t = n2.shape[0]

    def body(n2_ref, h1_ref, tgt_ref, wup_ref, wdown_ref, g2_ref, g3_ref, act_ref, dup_ref, dh2b_ref, dh1_ref,
             loss_ref, dg3_ref, dg2_ref, relu_ref):
        @pl.when(pl.program_id(0) == 0)
        def _():
            for ref in (loss_ref, dg3_ref, dg2_ref):
                ref[...] = jnp.zeros_like(ref)

        n2 = n2_ref[...]
        h1 = h1_ref[...]
        h2 = h1
        for k in range(N_CHIPS):
            cols = slice(k * D_MODEL, (k + 1) * D_MODEL)
            r = jnp.maximum(_dot(n2, wup_ref[k]), 0.0)
            relu_ref[:, cols] = r
            act = (r * r).astype(BF16)
            act_ref[:, cols] = act
            h2 = h2 + _dot(act, wdown_ref[cols, :])
        xhat, r3 = _rms(h2)
        diff = xhat * g3_ref[...] - tgt_ref[...]
        sq = jnp.sum(diff * diff, axis=1, keepdims=True)
        loss_ref[...] = loss_ref[...] + (0.5 / D_MODEL) * jnp.sum(sq, axis=0, keepdims=True)
        dy = diff * (1.0 / D_MODEL)
        dg3_ref[...] = dg3_ref[...] + _col_sum(dy * xhat)
        dh2 = _rms_bwd(dy * g3_ref[...], xhat, r3)
        dh2b = dh2.astype(BF16)
        dh2b_ref[...] = dh2b
        dn2 = jnp.zeros((SEQ_TILE, D_MODEL), F32)
        for k in range(N_CHIPS):
            cols = slice(k * D_MODEL, (k + 1) * D_MODEL)
            dup = (_dot_nt(dh2b, wdown_ref[cols, :]) * (2.0 * relu_ref[:, cols])).astype(BF16)
            dup_ref[:, cols] = dup
            dn2 = dn2 + _dot_nt(dup, wup_ref[k])
        xhat, r2 = _rms(h1)
        dg2_ref[...] = dg2_ref[...] + _col_sum(dn2 * xhat)
        dh1_ref[...] = dh2 + _rms_bwd(dn2 * g2_ref[...], xhat, r2)

    tile = pl.BlockSpec((SEQ_TILE, D_MODEL), lambda i: (i, 0))
    wide = pl.BlockSpec((SEQ_TILE, D_FF), lambda i: (i, 0))
    vec = _const((1, D_MODEL))
    vec_shape = jax.ShapeDtypeStruct((1, D_MODEL), F32)
    return _fused_call(
        body, jobs, name="mlp", grid=(t // SEQ_TILE,),
        in_specs=[tile, tile, tile, _resident((N_CHIPS, D_MODEL, D_MODEL)), _resident((D_FF, D_MODEL)), vec, vec],
        out_specs=[wide, wide, tile, tile, _const((SUBLANES, 128)), vec, vec],
        out_shape=[jax.ShapeDtypeStruct((t, D_FF), BF16), jax.ShapeDtypeStruct((t, D_FF), BF16),
                   jax.ShapeDtypeStruct((t, D_MODEL), BF16), jax.ShapeDtypeStruct((t, D_MODEL), F32),
                   jax.ShapeDtypeStruct((SUBLANES, 128), F32), vec_shape, vec_shape],
        scratch_shapes=[pltpu.VMEM((SEQ_TILE, D_FF), F32)],
        compiler_params=_params(),
    )(n2, h1, target, w_up_st, w_down, g2, g3)


def _bwd_mix(dh1, pa, pb, z, h, xc, r, ig, w_oa, w_ob, w_out, ln_g, ln_b, w_s, bias_full, conv_w, wr, wi, lam, jobs=()):
    t = dh1.shape[0]
    n_tiles = t // SEQ_TILE
    per_tile = SEQ_TILE // SUBLANES

    def merge_part(dh1_ref, pa_ref, pb_ref, m_ref, woa_ref, wob_ref, wout_ref, dz_ref, dya_ref, dyb_ref, mg_ref,
                   dpa_ref, dpb_ref, dh1b_ref):
        dh1b = dh1_ref[...].astype(BF16)
        dh1b_ref[...] = dh1b
        dm = _dot_nt(dh1b, wout_ref[...])
        pa = pa_ref[...]
        pb = pb_ref[...]
        sa = _gate(m_ref[:, :D_MODEL])
        sb = _gate(m_ref[:, D_MODEL:])
        mg_ref[...] = (sa * pa + sb * pb).astype(BF16)
        dpa = dm * sa
        dpb = dm * sb
        dz_ref[:, :D_MODEL] = ((dpa * pa) * (1.0 - sa)).astype(BF16)
        dz_ref[:, D_MODEL:] = ((dpb * pb) * (1.0 - sb)).astype(BF16)
        dpa = dpa.astype(BF16)
        dpb = dpb.astype(BF16)
        dpa_ref[...] = dpa
        dpb_ref[...] = dpb
        dya_ref[...] = _dot_nt(dpa, woa_ref[...])
        dyb_ref[...] = _dot_nt(dpb, wob_ref[...])

    def sgu_part(dyb_ref, ub_ref, vb_ref, lg_ref, lb_ref, ws_ref, bias_ref, dz_ref, dlg_ref, dlb_ref, dws_ref, dbs_ref,
                 dvn_ref, dsp_acc):
        i = pl.program_id(0)

        @pl.when(i == 0)
        def _():
            dlg_ref[...] = jnp.zeros_like(dlg_ref)
            dlb_ref[...] = jnp.zeros_like(dlb_ref)
            dws_ref[...] = jnp.zeros_like(dws_ref)
            dsp_acc[...] = jnp.zeros_like(dsp_acc)

        u, du, dvg, rstd, vhat, vn = _sgu_forward_parts(ub_ref[...], vb_ref[...], lg_ref, lb_ref)
        dyb = dyb_ref[...]
        mask = _causal_mask()
        wm = [jnp.where(mask, ws_ref[g], 0.0).astype(BF16) for g in range(GROUPS)]
        for c in range(SEQ_TILE // CHUNK):
            rows = slice(c * CHUNK, (c + 1) * CHUNK)
            for g in range(GROUPS):
                cols = slice(g * GROUP_DIM, (g + 1) * GROUP_DIM)
                vn_blk = vn[rows, cols]
                sp = _dot(wm[g], vn_blk) + bias_ref[:, cols]
                dyb_blk = dyb[rows, cols]
                dz_ref[rows, cols] = (dyb_blk * sp * du[rows, cols]).astype(BF16)
                dsp = dyb_blk * u[rows, cols]
                dsp_acc[:, cols] = dsp_acc[:, cols] + dsp
                dspb = dsp.astype(BF16)
                dvn_ref[rows, cols] = _dot_tn(wm[g], dspb)
                wcols = slice(g * CHUNK, (g + 1) * CHUNK)
                dws_ref[:, wcols] = dws_ref[:, wcols] + jnp.where(mask, _dot_nt(dspb, vn_blk), 0.0)
        dvn = dvn_ref[...]
        dlg_ref[...] = dlg_ref[...] + _col_sum(dvn * vhat)
        dlb_ref[...] = dlb_ref[...] + _col_sum(dvn)
        dvhat = dvn * lg_ref[...]
        dvgel = rstd * (dvhat - jnp.mean(dvhat, axis=-1, keepdims=True)
                        - vhat * jnp.mean(dvhat * vhat, axis=-1, keepdims=True))
        dz_ref[:, D_MODEL:] = (dvgel * dvg).astype(BF16)

        @pl.when(i == n_tiles - 1)
        def _():
            lane = lax.broadcasted_iota(jnp.int32, (CHUNK, 128), 1)
            out = jnp.zeros((CHUNK, 128), F32)
            for g in range(GROUPS):
                s = jnp.sum(dsp_acc[:, g * GROUP_DIM:(g + 1) * GROUP_DIM], axis=1, keepdims=True)
                out = out + jnp.where(lane == g, s, 0.0)
            dbs_ref[...] = out

    def lru_part(dya_ref, xa_ref, ga_ref, h_ref, h_prev_ref, xc_ref, r_ref, ig_ref, cw_ref, wr_ref, wi_ref, lam_ref,
                 dz_ref, dcw_ref, dcb_ref, dwr_ref, dbr_ref, dwi_ref, dbi_ref, dlam_ref, lam_carry, dxc_head):
        i = pl.program_id(0)

        @pl.when(i == 0)
        def _():
            for ref in (dcw_ref, dcb_ref, dwr_ref, dbr_ref, dwi_ref, dbi_ref, dlam_ref, lam_carry, dxc_head):
                ref[...] = jnp.zeros_like(ref)

        first_tile = i == n_tiles - 1
        h_tail = jnp.where(first_tile, 0.0, h_prev_ref[...])
        xc, r, ig = xc_ref[...], r_ref[...], ig_ref[...]
        xcb = xc.astype(BF16)
        sp, a, mult, inv_mult = _decay(r, lam_ref)
        h = h_ref[...]
        h_prev = _shift_down(h, h_tail, 1)
        dya = dya_ref[...]
        gg, dgg = _gelu_and_grad(ga_ref[...].astype(F32))
        dz_ref[:, D_MODEL:] = (dya * h * dgg).astype(BF16)
        ones = jnp.ones((SUBLANES, D_MODEL), F32)
        lam_t, lam_first = _scan_backward(_shift_up(a, ones, 1), dya * gg, lam_carry[...])
        lam_carry[...] = a[0:1] * lam_first
        lam_ig = lam_t * ig
        dxc_direct = lam_ig * mult
        dmult = lam_ig * xc
        dla = a * (lam_t * h_prev - (dmult * a) * inv_mult)
        dla_r = dla * r
        dlam_ref[...] = dlam_ref[...] + _col_sum(dla_r) * (LRU_C * jax.nn.sigmoid(-lam_ref[...]))
        dpr = (dla_r * ((-LRU_C) * sp)) * (1.0 - r)
        dpi = (dxc_direct * xc) * (1.0 - ig)
        dbr_ref[...] = dbr_ref[...] + _col_sum(dpr)
        dbi_ref[...] = dbi_ref[...] + _col_sum(dpi)
        dprb = dpr.astype(BF16)
        dpib = dpi.astype(BF16)
        dxc_gate = []
        for hd in range(HEADS):
            cols = slice(hd * HEAD_DIM, (hd + 1) * HEAD_DIM)
            dxc_gate.append(_dot_nt(dprb[:, cols], wr_ref[hd]) + _dot_nt(dpib[:, cols], wi_ref[hd]))
            dwr_ref[hd] = dwr_ref[hd] + _dot_tn(xcb[:, cols], dprb[:, cols])
            dwi_ref[hd] = dwi_ref[hd] + _dot_tn(xcb[:, cols], dpib[:, cols])
        dxc = dxc_direct + jnp.concatenate(dxc_gate, axis=1)
        dcb_ref[...] = dcb_ref[...] + _col_sum(dxc)
        cw = cw_ref[...]
        head = dxc_head[...]
        xa = xa_ref[...].astype(F32)
        dxa = cw[0:1] * dxc
        dcw_ref[0:1, :] = dcw_ref[0:1, :] + _col_sum(dxc * xa)
        for k in range(1, CONV_WIDTH):
            dxc_k = _shift_up(dxc, head, k)
            dxa = dxa + cw[k:k + 1] * dxc_k
            dcw_ref[k:k + 1, :] = dcw_ref[k:k + 1, :] + _col_sum(dxc_k * xa)
        dxc_head[...] = dxc[0:SUBLANES]
        dz_ref[:, :D_MODEL] = dxa.astype(BF16)

    def body(dh1_ref, pa_ref, pb_ref, z_ref, h_ref, h_prev_ref, xc_ref, r_ref, ig_ref, woa_ref, wob_ref, wout_ref,
             lg_ref, lb_ref, ws_ref, bias_ref, cw_ref, wr_ref, wi_ref, lam_ref, dz_ref, mg_ref, dpa_ref, dpb_ref,
             dh1b_ref, dlg_ref, dlb_ref, dws_ref, dbs_ref, dcw_ref, dcb_ref, dwr_ref, dbr_ref, dwi_ref, dbi_ref,
             dlam_ref, dya_ref, dyb_ref, dvn_ref, dsp_acc, lam_carry, dxc_head):
        def cols(ref, first, count):
            return ref.at[:, pl.ds(first * D_MODEL, count * D_MODEL)]

        merge_part(dh1_ref, pa_ref, pb_ref, cols(z_ref, 4, 2), woa_ref, wob_ref, wout_ref, cols(dz_ref, 4, 2), dya_ref,
                   dyb_ref, mg_ref, dpa_ref, dpb_ref, dh1b_ref)
        sgu_part(dyb_ref, cols(z_ref, 2, 1), cols(z_ref, 3, 1), lg_ref, lb_ref, ws_ref, bias_ref, cols(dz_ref, 2, 2),
                 dlg_ref, dlb_ref, dws_ref, dbs_ref, dvn_ref, dsp_acc)
        lru_part(dya_ref, cols(z_ref, 0, 1), cols(z_ref, 1, 1), h_ref, h_prev_ref, xc_ref, r_ref, ig_ref, cw_ref, wr_ref,
                 wi_ref, lam_ref, cols(dz_ref, 0, 2), dcw_ref, dcb_ref, dwr_ref, dbr_ref, dwi_ref, dbi_ref, dlam_ref,
                 lam_carry, dxc_head)

    rev = lambda i: n_tiles - 1 - i
    tile = pl.BlockSpec((SEQ_TILE, D_MODEL), lambda i: (rev(i), 0))
    row = pl.BlockSpec((SEQ_TILE, D_IN), lambda i: (rev(i), 0))
    prev8 = pl.BlockSpec((SUBLANES, D_MODEL), lambda i: (jnp.maximum(rev(i) * per_tile - 1, 0), 0))
    vec = _const((1, D_MODEL))
    sq = _resident((D_MODEL, D_MODEL))
    gate_w = _resident((HEADS, HEAD_DIM, HEAD_DIM))
    gate_acc = _const((HEADS, HEAD_DIM, HEAD_DIM))
    vec_shape = jax.ShapeDtypeStruct((1, D_MODEL), F32)
    gate_shape = jax.ShapeDtypeStruct((HEADS, HEAD_DIM, HEAD_DIM), F32)
    act_bf = jax.ShapeDtypeStruct((t, D_MODEL), BF16)
    return _fused_call(
        body, jobs, name="bwd_mix", grid=(n_tiles,),
        in_specs=[tile, tile, tile, row, tile, prev8, tile, tile, tile, sq, sq, sq, vec, vec,
                  _const((GROUPS, CHUNK, CHUNK)), _const((CHUNK, D_MODEL)), _const((CONV_WIDTH, D_MODEL)), gate_w, gate_w,
                  vec],
        out_specs=[row, tile, tile, tile, tile, vec, vec, _const((CHUNK, GROUPS * CHUNK)), _const((CHUNK, 128)),
                   _const((SUBLANES, D_MODEL)), vec, gate_acc, vec, gate_acc, vec, vec],
        out_shape=[jax.ShapeDtypeStruct((t, D_IN), BF16), act_bf, act_bf, act_bf, act_bf, vec_shape, vec_shape,
                   jax.ShapeDtypeStruct((CHUNK, GROUPS * CHUNK), F32), jax.ShapeDtypeStruct((CHUNK, 128), F32),
                   jax.ShapeDtypeStruct((SUBLANES, D_MODEL), F32), vec_shape, gate_shape, vec_shape, gate_shape,
                   vec_shape, vec_shape],
        scratch_shapes=[pltpu.VMEM((SEQ_TILE, D_MODEL), F32), pltpu.VMEM((SEQ_TILE, D_MODEL), F32),
                        pltpu.VMEM((SEQ_TILE, D_MODEL), F32), pltpu.VMEM((CHUNK, D_MODEL), F32),
                        pltpu.VMEM((1, D_MODEL), F32), pltpu.VMEM((SUBLANES, D_MODEL), F32)],
        compiler_params=_params(),
    )(dh1, pa, pb, z, h, h, xc, r, ig, w_oa, w_ob, w_out, ln_g, ln_b, w_s, bias_full, conv_w, wr, wi, lam)


def _bwd_in(dz, x, dh1, w_in_st, g1, jobs=()):
    t = x.shape[0]

    def body(dz_ref, x_ref, dh1_ref, w_ref, g_ref, dx_ref, dg1_ref):
        @pl.when(pl.program_id(0) == 0)
        def _():
            dg1_ref[...] = jnp.zeros_like(dg1_ref)

        dn1 = jnp.zeros((MM_TILE, D_MODEL), F32)
        for k in range(N_CHIPS):
            dn1 = dn1 + _dot_nt(dz_ref[:, k * IN_SHARD:(k + 1) * IN_SHARD], w_ref[k])
        xhat, r1 = _rms(x_ref[...])
        dg1_ref[...] = dg1_ref[...] + _col_sum(dn1 * xhat)
        dx_ref[...] = dh1_ref[...] + _rms_bwd(dn1 * g_ref[...], xhat, r1)

    tile = pl.BlockSpec((MM_TILE, D_MODEL), lambda i: (i, 0))
    return _fused_call(
        body, jobs, name="bwd_in", grid=(t // MM_TILE,),
        in_specs=[pl.BlockSpec((MM_TILE, D_IN), lambda i: (i, 0)), tile, tile,
                  _resident((N_CHIPS, D_MODEL, IN_SHARD)), _const((1, D_MODEL))],
        out_specs=[tile, _const((1, D_MODEL))],
        out_shape=[jax.ShapeDtypeStruct((t, D_MODEL), F32), jax.ShapeDtypeStruct((1, D_MODEL), F32)],
        compiler_params=_params(),
    )(dz, x, dh1, w_in_st, g1)


def _weight_grad(name, a, b, n_blocks, a_varies, b_varies, width, jobs=()):
    t = a.shape[0]
    rows = min(DW_TILE, t)
    n_t = t // rows

    def body(a_ref, b_ref, o_ref, acc_ref):
        s = pl.program_id(1)
        part = _dot_tn(a_ref[...], b_ref[...])

        @pl.when(s == 0)
        def _():
            acc_ref[...] = part

        @pl.when(s > 0)
        def _():
            acc_ref[...] = acc_ref[...] + part

        @pl.when(s == n_t - 1)
        def _():
            o_ref[...] = acc_ref[...].astype(BF16)

    return _fused_call(
        body, jobs, name=name, grid=(n_blocks, n_t),
        in_specs=[pl.BlockSpec((rows, D_MODEL), (lambda j, s: (s, j)) if a_varies else (lambda j, s: (s, 0))),
                  pl.BlockSpec((rows, width), (lambda j, s: (s, j)) if b_varies else (lambda j, s: (s, 0)))],
        out_specs=pl.BlockSpec((None, D_MODEL, width), lambda j, s: (j, 0, 0)),
        out_shape=jax.ShapeDtypeStruct((n_blocks, D_MODEL, width), BF16),
        scratch_shapes=[pltpu.VMEM((D_MODEL, width), F32)],
        compiler_params=_params(2),
    )(a, b)


def _weight_grads_square(name, pairs, jobs=()):
    n = len(pairs)
    t = pairs[0][0].shape[0]
    rows = min(2 * MM_TILE, t)
    n_t = t // rows

    def body(*refs):
        ins, outs, accs = refs[:2 * n], refs[2 * n:3 * n], refs[3 * n:]
        s = pl.program_id(0)
        for k in range(n):
            part = _dot_tn(ins[2 * k][...], ins[2 * k + 1][...])

            @pl.when(s == 0)
            def _(k=k, part=part):
                accs[k][...] = part

            @pl.when(s > 0)
            def _(k=k, part=part):
                accs[k][...] = accs[k][...] + part

            @pl.when(s == n_t - 1)
            def _(k=k):
                outs[k][...] = accs[k][...].astype(BF16)

    tile = pl.BlockSpec((rows, D_MODEL), lambda s: (s, 0))
    return _fused_call(
        body, jobs, name=name, grid=(n_t,), in_specs=[tile] * (2 * n), out_specs=[_const((D_MODEL, D_MODEL))] * n,
        out_shape=[jax.ShapeDtypeStruct((D_MODEL, D_MODEL), BF16)] * n,
        scratch_shapes=[pltpu.VMEM((D_MODEL, D_MODEL), F32)] * n,
        compiler_params=_params(),
    )(*[x for pair in pairs for x in pair])


def _place():
    x, y, c = lax.axis_index("x"), lax.axis_index("y"), lax.axis_index("c")
    other_chips = [(1 - x, y), (x, 1 - y), (1 - x, 1 - y)]
    return x, y, c, other_chips


def _chip_index(px, py):
    return 2 * px + py


ANY = pl.BlockSpec(memory_space=pl.ANY)
SIBLING = ((0, 0, 1),)
NEIGHBOURS = ((1, 0, 0), (0, 1, 0))
OTHER_CHIPS = NEIGHBOURS + ((1, 1, 0),)


def _near_far(x, y, c):
    return (x ^ (1 - c), y ^ c), (x ^ c, y ^ (1 - c))


def _gather_near_job(shards):
    n = len(shards)
    halves = [s.shape[0] // 2 for s in shards]

    def copies(ins, outs, send, recv, local):
        x, y, c, _ = _place()
        near, _ = _near_far(x, y, c)

        def block(w, chip, pc):
            return outs[w].at[_chip_index(*chip), pl.ds(pc * halves[w], halves[w]), :]

        def copy(w, k, chip, pc, to, src=None):
            return pltpu.make_async_remote_copy(
                src_ref=block(w, chip, pc) if src is None else src, dst_ref=block(w, chip, pc),
                send_sem=send.at[2 * w + k], recv_sem=recv.at[2 * w + k], device_id=to, device_id_type=MESH)

        sends, arrivals, own = [], [], []
        for w in range(n):
            src = ins[w].at[pl.ds(c * halves[w], halves[w]), :]
            own.append(pltpu.make_async_copy(src, block(w, (x, y), c), local.at[w]))
            sends += [copy(w, 0, (x, y), c, (*near, c), src), copy(w, 1, (x, y), c, (x, y, 1 - c), src)]
            arrivals += [copy(w, 0, near, c, (x, y, c)), copy(w, 1, (x, y), 1 - c, (x, y, c))]
        return sends, arrivals, own

    return _Job(shards, [jax.ShapeDtypeStruct((N_CHIPS,) + s.shape, s.dtype) for s in shards], 2 * n, copies,
                NEIGHBOURS + SIBLING, n_local=n)


def _gather_far_job(stacked):
    n = len(stacked)
    halves = [s.shape[1] // 2 for s in stacked]

    def copies(ins, outs, send, recv, local):
        del ins, local
        x, y, c, _ = _place()
        near, far = _near_far(x, y, c)

        def copy(w, k, chip):
            blk = outs[w].at[_chip_index(*chip), pl.ds(c * halves[w], halves[w]), :]
            return pltpu.make_async_remote_copy(
                src_ref=blk, dst_ref=blk, send_sem=send.at[2 * w + k], recv_sem=recv.at[2 * w + k],
                device_id=(*far, c), device_id_type=MESH)

        sends = [copy(w, k, chip) for w in range(n) for k, chip in enumerate(((x, y), near))]
        arrivals = [copy(w, k, chip) for w in range(n) for k, chip in enumerate((far, (1 - x, 1 - y)))]
        return sends, arrivals, []

    return _Job(stacked, [jax.ShapeDtypeStruct(s.shape, s.dtype) for s in stacked], 2 * n, copies, NEIGHBOURS,
                aliases={w: w for w in range(n)})


def _gather_pass_job(stacked):
    n = len(stacked)
    halves = [s.shape[1] // 2 for s in stacked]

    def copies(ins, outs, send, recv, local):
        del ins, local
        x, y, c, chips = _place()

        def copy(w, j, chip, pc, to):
            blk = outs[w].at[_chip_index(*chip), pl.ds(pc * halves[w], halves[w]), :]
            return pltpu.make_async_remote_copy(
                src_ref=blk, dst_ref=blk, send_sem=send.at[3 * w + j], recv_sem=recv.at[3 * w + j], device_id=to,
                device_id_type=MESH)

        sends = [copy(w, j, chip, c, (x, y, 1 - c)) for w in range(n) for j, chip in enumerate(chips)]
        arrivals = [copy(w, j, chip, 1 - c, (x, y, c)) for w in range(n) for j, chip in enumerate(chips)]
        return sends, arrivals, []

    return _Job(stacked, [jax.ShapeDtypeStruct(s.shape, s.dtype) for s in stacked], 3 * n, copies, SIBLING,
                aliases={w: w for w in range(n)})


def _gather_small_job(block):
    def copies(ins, outs, send, recv, local):
        x, y, c, chips = _place()

        def copy(j, chip_from, to):
            return pltpu.make_async_remote_copy(
                src_ref=ins[0], dst_ref=outs[0].at[_chip_index(*chip_from)], send_sem=send.at[j],
                recv_sem=recv.at[j], device_id=to, device_id_type=MESH)

        own = [pltpu.make_async_copy(ins[0], outs[0].at[_chip_index(x, y)], local.at[0])]
        sends = [copy(j, (x, y), (*chip, c)) for j, chip in enumerate(chips)]
        arrivals = [copy(j, chip, (x, y, c)) for j, chip in enumerate(chips)]
        return sends, arrivals, own

    return _Job([block], [jax.ShapeDtypeStruct((N_CHIPS,) + block.shape, block.dtype)], 3, copies, OTHER_CHIPS,
                n_local=1)


def _pair_send_job(grads):
    n = len(grads)
    halves = [g.shape[1] // 2 for g in grads]

    def copies(ins, outs, send, recv, local):
        del local
        x, y, c, _ = _place()
        sends = [pltpu.make_async_remote_copy(
            src_ref=ins[w].at[:, pl.ds((1 - c) * halves[w], halves[w]), :], dst_ref=outs[w], send_sem=send.at[w],
            recv_sem=recv.at[w], device_id=(x, y, 1 - c), device_id_type=MESH) for w in range(n)]
        return sends, sends, []

    return _Job(grads, [jax.ShapeDtypeStruct((N_CHIPS, h, g.shape[2]), g.dtype) for g, h in zip(grads, halves)], n,
                copies, SIBLING)


ROW_STEPS = 4


def _pair_add(name, core, mine, theirs):
    n = len(mine)

    def body(core_ref, *refs):
        del core_ref
        for a_ref, b_ref, o_ref in zip(refs[:n], refs[n:2 * n], refs[2 * n:]):
            o_ref[...] = (a_ref[...].astype(F32) + b_ref[...].astype(F32)).astype(BF16)

    half = lambda a: pl.BlockSpec((None, None) + a.shape[2:], lambda k, core_ref: (k, core_ref[0], 0, 0))
    block = lambda b: pl.BlockSpec((None,) + b.shape[1:], lambda k, core_ref: (k, 0, 0))
    return pl.pallas_call(
        body, name=name,
        grid_spec=pltpu.PrefetchScalarGridSpec(
            num_scalar_prefetch=1, grid=(N_CHIPS,),
            in_specs=[half(a) for a in mine] + [block(b) for b in theirs], out_specs=[block(b) for b in theirs]),
        out_shape=[jax.ShapeDtypeStruct(b.shape, BF16) for b in theirs],
        compiler_params=_params(),
    )(core, *mine, *theirs)


def _sequencer_call(name, collective_id, job):
    steps, peers = job.phases, job.peers
    ins = [jax.new_ref(a, memory_space=pltpu.MemorySpace.HBM) for a in job.inputs]
    outs = [ins[{o: i for i, o in job.aliases.items()}[k]] if k in job.aliases.values()
            else jax.empty_ref(shape, memory_space=pltpu.MemorySpace.HBM) for k, shape in enumerate(job.out_shape)]
    sems = [pltpu.SemaphoreType.DMA((n,)) for step in steps for n in (step.n_sem, step.n_sem, max(step.n_local, 1))]

    @pl.kernel(mesh=plsc.ScalarSubcoreMesh(axis_name="sequencer", num_cores=1), name=name, scratch_types=tuple(sems),
               compiler_params=pltpu.CompilerParams(collective_id=collective_id))
    def launch(*sem_refs):
        x, y, c, _ = _place()
        barrier = pltpu.get_barrier_semaphore()
        for dx, dy, dc in peers:
            pl.semaphore_signal(barrier, inc=1, device_id=(x ^ dx, y ^ dy, c ^ dc), device_id_type=MESH)
        pl.semaphore_wait(barrier, len(peers))
        for k, step in enumerate(steps):
            sends, arrivals, own = step.copies(ins if k == 0 else outs, outs, *sem_refs[3 * k:3 * k + 3])
            for cp in own + sends:
                cp.start()
            for cp in arrivals:
                cp.wait_recv()
            for cp in sends:
                cp.wait_send()
            for cp in own:
                cp.wait()

    launch()
    return [ref[...] for ref in outs]


def _chip_exchange_job(sums):
    n = len(sums)

    def copies(ins, outs, send, recv, local):
        del local
        _, _, c, chips = _place()
        sends = [pltpu.make_async_remote_copy(
            src_ref=ins[w].at[_chip_index(*chip)], dst_ref=outs[w].at[j], send_sem=send.at[3 * w + j],
            recv_sem=recv.at[3 * w + j], device_id=(*chip, c), device_id_type=MESH)
            for w in range(n) for j, chip in enumerate(chips)]
        return sends, sends, []

    return _Job(sums, [jax.ShapeDtypeStruct((N_CHIPS - 1,) + s.shape[1:], s.dtype) for s in sums], 3 * n, copies,
                OTHER_CHIPS)


def _chip_sum(name, place, mine, theirs):
    n = len(mine)

    def body(place_ref, *refs):
        del place_ref
        for p_ref, q_ref, o_ref in zip(refs[:n], refs[n:2 * n], refs[2 * n:]):
            acc = p_ref[...].astype(F32)
            for j in range(N_CHIPS - 1):
                acc = acc + q_ref[j].astype(F32)
            o_ref[...] = acc

    def block(p, lead, pick):
        return pl.BlockSpec((lead, p.shape[1] // ROW_STEPS, p.shape[2]), lambda r, place_ref: (pick(place_ref), r, 0))

    return pl.pallas_call(
        body, name=name,
        grid_spec=pltpu.PrefetchScalarGridSpec(
            num_scalar_prefetch=1, grid=(ROW_STEPS,),
            in_specs=[block(p, None, lambda place_ref: place_ref[0]) for p in mine]
            + [block(p, N_CHIPS - 1, lambda place_ref: 0) for p in mine],
            out_specs=[block(p, None, lambda place_ref: place_ref[1]) for p in mine]),
        out_shape=[jax.ShapeDtypeStruct((2,) + p.shape[1:], F32) for p in mine],
        compiler_params=_params(),
    )(place, *mine, *theirs)


def _share_job(bufs):
    n = len(bufs)

    def copies(ins, outs, send, recv, local):
        del ins, local
        x, y, c, _ = _place()

        def copy(w, half):
            return pltpu.make_async_remote_copy(
                src_ref=outs[w].at[half], dst_ref=outs[w].at[half], send_sem=send.at[w], recv_sem=recv.at[w],
                device_id=(x, y, 1 - c), device_id_type=MESH)

        return [copy(w, c) for w in range(n)], [copy(w, 1 - c) for w in range(n)], []

    return _Job(bufs, [jax.ShapeDtypeStruct(b.shape, b.dtype) for b in bufs], n, copies, SIBLING,
                aliases={w: w for w in range(n)})


SMALL_ROWS = 24
ROW_G1, ROW_CW, ROW_CB, ROW_BR, ROW_BI, ROW_LAM, ROW_LG, ROW_LB, ROW_G2, ROW_G3, ROW_LOSS, ROW_BS = (
    0, 1, 5, 6, 7, 8, 9, 10, 11, 12, 13, 16)
N_DEV = 8


def _pack_small(dcw, dcb, dbr, dbi, dlam, dlg, dlb, dg2, dg3, loss, dbs):
    def body(dcw_ref, dcb_ref, dbr_ref, dbi_ref, dlam_ref, dlg_ref, dlb_ref, dg2_ref, dg3_ref, loss_ref, dbs_ref, out):
        out[...] = jnp.zeros((SMALL_ROWS, D_MODEL), F32)
        for row, ref in ((ROW_CB, dcb_ref), (ROW_BR, dbr_ref), (ROW_BI, dbi_ref), (ROW_LAM, dlam_ref),
                         (ROW_LG, dlg_ref), (ROW_LB, dlb_ref), (ROW_G2, dg2_ref), (ROW_G3, dg3_ref)):
            out[row:row + 1, :] = ref[...]
        out[ROW_CW:ROW_CW + CONV_WIDTH, :] = dcw_ref[0:CONV_WIDTH, :]
        out[ROW_LOSS:ROW_LOSS + 1, 0:128] = loss_ref[0:1, :]
        out[ROW_BS:ROW_BS + GROUPS, 0:128] = jnp.transpose(dbs_ref[...])[0:GROUPS, :]

    vm = pl.BlockSpec(memory_space=pltpu.VMEM)
    return pl.pallas_call(
        body, name="pack_small", in_specs=[vm] * 11, out_specs=vm,
        out_shape=jax.ShapeDtypeStruct((SMALL_ROWS, D_MODEL), F32),
    )(dcw, dcb, dbr, dbi, dlam, dlg, dlb, dg2, dg3, loss, dbs)


def _gather_all_job(blocks):
    n = len(blocks)
    flips = [(dx, dy, dc) for dx in (0, 1) for dy in (0, 1) for dc in (0, 1)][1:]

    def copies(ins, outs, send, recv, local):
        x, y, c, _ = _place()
        me = 4 * x + 2 * y + c
        sends, arrivals, own = [], [], []
        for w in range(n):
            own.append(pltpu.make_async_copy(ins[w], outs[w].at[me], local.at[w]))
            for k, (dx, dy, dc) in enumerate(flips):
                peer = (x ^ dx, y ^ dy, c ^ dc)
                sem = dict(send_sem=send.at[7 * w + k], recv_sem=recv.at[7 * w + k])
                sends.append(pltpu.make_async_remote_copy(
                    src_ref=ins[w], dst_ref=outs[w].at[me], device_id=peer, device_id_type=MESH, **sem))
                arrivals.append(pltpu.make_async_remote_copy(
                    src_ref=ins[w], dst_ref=outs[w].at[4 * peer[0] + 2 * peer[1] + peer[2]], device_id=peer,
                    device_id_type=MESH, **sem))
        return sends, arrivals, own

    return _Job(blocks, [jax.ShapeDtypeStruct((N_DEV,) + b.shape, b.dtype) for b in blocks], 7 * n, copies,
                OTHER_CHIPS + SIBLING + tuple((dx, dy, 1) for dx, dy, _ in OTHER_CHIPS), n_local=n)


def _sum_small(vec_all, ws_all, dg1_all):
    def body(vec_ref, ws_ref, dg1_ref, vec_out, ws_out):
        vec, ws, dg1 = vec_ref[0], ws_ref[0], dg1_ref[0]
        for d in range(1, N_DEV):
            vec, ws, dg1 = vec + vec_ref[d], ws + ws_ref[d], dg1 + dg1_ref[d]
        vec_out[...] = vec
        vec_out[ROW_G1:ROW_G1 + 1, :] = dg1
        ws_out[...] = ws

    vm = pl.BlockSpec(memory_space=pltpu.VMEM)
    return pl.pallas_call(
        body, name="sum_small", in_specs=[vm] * 3, out_specs=[vm, vm],
        out_shape=[jax.ShapeDtypeStruct(vec_all.shape[1:], F32), jax.ShapeDtypeStruct(ws_all.shape[1:], F32)],
    )(vec_all, ws_all, dg1_all)


def _adamw_math(w, g, m, v):
    m = ADAM_B1 * m + (1.0 - ADAM_B1) * g
    v = ADAM_B2 * v + (1.0 - ADAM_B2) * (g * g)
    m_hat = m / (1.0 - ADAM_B1 ** ADAM_STEP)
    v_hat = v / (1.0 - ADAM_B2 ** ADAM_STEP)
    delta = (-ADAM_LR) * (m_hat / (jnp.sqrt(v_hat) + ADAM_EPS) + ADAM_WD * w)
    return delta, m, v


def _adamw(name, gs, ws, ms, vs):
    n = len(ws)

    def body(*refs):
        ins, outs = refs[:4 * n], refs[4 * n:]
        for p in range(n):
            g_ref, w_ref, m_ref, v_ref = ins[p::n]
            g = g_ref[...]
            outs[4 * p][...] = g
            outs[4 * p + 1][...], outs[4 * p + 2][...], outs[4 * p + 3][...] = _adamw_math(
                w_ref[...], g, m_ref[...], v_ref[...])

    blocks = [pl.BlockSpec((w.shape[0] // ROW_STEPS, w.shape[1]), lambda r: (r, 0)) for w in ws]
    out = pl.pallas_call(
        body, name=name, grid=(ROW_STEPS,), in_specs=blocks * 4, out_specs=[b for b in blocks for _ in range(4)],
        out_shape=[jax.ShapeDtypeStruct(w.shape, F32) for w in ws for _ in range(4)], compiler_params=_params(),
    )(*gs, *ws, *ms, *vs)
    return [tuple(out[4 * p:4 * p + 4]) for p in range(n)]


def _adamw_small(grads, ws, ms, vs):
    n = len(grads)

    def body(*refs):
        g_refs, w_refs, m_refs, v_refs = refs[:n], refs[n:2 * n], refs[2 * n:3 * n], refs[3 * n:4 * n]
        outs = refs[4 * n:]
        for p in range(n):
            d, nm, nv = _adamw_math(w_refs[p][...], g_refs[p][...], m_refs[p][...], v_refs[p][...])
            outs[p][...] = d
            outs[n + p][...] = nm
            outs[2 * n + p][...] = nv

    vm = pl.BlockSpec(memory_space=pltpu.VMEM)
    shapes = [jax.ShapeDtypeStruct(w.shape, F32) for w in ws]
    out = pl.pallas_call(
        body, name="adamw_small", in_specs=[vm] * (4 * n), out_specs=[vm] * (3 * n), out_shape=shapes * 3,
    )(*grads, *ws, *ms, *vs)
    return out[:n], out[n:2 * n], out[2 * n:]


def _unstack_heads(w_st):
    per = HEAD_DIM // N_CHIPS
    return w_st.reshape(N_CHIPS, HEADS, per, HEAD_DIM).transpose(1, 0, 2, 3).reshape(HEADS, HEAD_DIM, HEAD_DIM)


def _stack_heads(w):
    per = HEAD_DIM // N_CHIPS
    return w.reshape(HEADS, N_CHIPS, per, HEAD_DIM).transpose(1, 0, 2, 3).reshape(N_CHIPS, HEADS * per, HEAD_DIM)


def kernel(x, norm_mix_g, w_in, conv_w, conv_b, w_rgate, b_rgate, w_igate, b_igate, lru_lambda, w_out_a, sgu_ln_g, sgu_ln_b, sgu_w_s, sgu_b_s, w_out_b, w_out, norm_mlp_g, w_up, w_down, norm_final_g, loss_target, m_norm_mix_g, m_w_in, m_conv_w, m_conv_b, m_w_rgate, m_b_rgate, m_w_igate, m_b_igate, m_lru_lambda, m_w_out_a, m_sgu_ln_g, m_sgu_ln_b, m_sgu_w_s, m_sgu_b_s, m_w_out_b, m_w_out, m_norm_mlp_g, m_w_up, m_w_down, m_norm_final_g, v_norm_mix_g, v_w_in, v_conv_w, v_conv_b, v_w_rgate, v_b_rgate, v_w_igate, v_b_igate, v_lru_lambda, v_w_out_a, v_sgu_ln_g, v_sgu_ln_b, v_sgu_w_s, v_sgu_b_s, v_w_out_b, v_w_out, v_norm_mlp_g, v_w_up, v_w_down, v_norm_final_g):
    chip = _chip_index(lax.axis_index("x"), lax.axis_index("y"))
    core = lax.axis_index("c")
    quarter_h = HEAD_DIM // N_CHIPS
    quarter_d = D_MODEL // N_CHIPS

    as_2d = lambda a: a.reshape(-1, a.shape[-1])
    big_w = [as_2d(w) for w in (w_in, w_rgate, w_igate, w_out_a, w_out_b, w_out, w_up, w_down)]
    big_m = [as_2d(w) for w in (m_w_in, m_w_rgate, m_w_igate, m_w_out_a, m_w_out_b, m_w_out, m_w_up, m_w_down)]
    big_v = [as_2d(w) for w in (v_w_in, v_w_rgate, v_w_igate, v_w_out_a, v_w_out_b, v_w_out, v_w_up, v_w_down)]

    packed = jnp.concatenate([conv_w[0], b_rgate[0], b_igate[0]], axis=1)
    packed = jnp.concatenate([packed, jnp.zeros_like(packed)], axis=0)
    s_in, s_r, s_i, s_oa, s_ob, s_out, s_up, s_down = [w.astype(BF16) for w in big_w]
    xs, target = x[0], loss_target[0]
    g3 = norm_final_g.reshape(1, D_MODEL)
    bias_s = jnp.broadcast_to(jnp.transpose(sgu_b_s[0])[:, :, None], (CHUNK, GROUPS, GROUP_DIM)).reshape(CHUNK, D_MODEL)
    core_arr = core.reshape(1).astype(jnp.int32)
    place = jnp.stack([chip, core]).astype(jnp.int32)
    quarter = lambda g: g.reshape(N_CHIPS, D_MODEL // N_CHIPS, D_MODEL)

    def pair_add(nm, grads, from_sibling):
        halves = [g.reshape(N_CHIPS, 2, g.shape[1] // 2, g.shape[2]) for g in grads]
        return list(_pair_add("pair_add_" + nm, core_arr, halves, from_sibling))

    def chip_sum(nm, pairs, from_chips):
        return list(_chip_sum("chip_sum_" + nm, place, pairs, from_chips))

    order = jnp.stack([chip, chip ^ 2, chip ^ 1, chip ^ 3]).astype(jnp.int32)
    (z, n1, (w_in_st, wr_st, wi_st)), ((packed_all,), late) = _fwd_in(
        xs, norm_mix_g, [s_in, s_r, s_i], order,
        jobs=[_gather_small_job(packed), _gather_near_job([s_oa, s_ob, s_out])])
    pick = lambda lo, hi: packed_all[:, :HEADS, lo:hi].transpose(1, 0, 2).reshape(HEADS, -1)
    conv_w_full = pick(0, quarter_d)
    br_full = pick(quarter_d, quarter_d + quarter_h).reshape(1, D_MODEL)
    bi_full = pick(quarter_d + quarter_h, quarter_d + 2 * quarter_h).reshape(1, D_MODEL)
    wr, wi = _unstack_heads(wr_st), _unstack_heads(wi_st)
    lru = (conv_w_full, conv_b, wr, br_full, wi, bi_full, lru_lambda)
    sgu = (sgu_ln_g, sgu_ln_b, sgu_w_s[0], bias_s)

    after = lambda arrays, result: lax.optimization_barrier((arrays, result))[0]
    w_up_st, w_dn = _sequencer_call(
        "gather_mlp", 8, _gather_near_job(after([s_up, s_down], n1)).then(_gather_far_job).then(_gather_pass_job))
    w_dn = w_dn.reshape(D_FF, D_MODEL)
    late_step = (3 * (xs.shape[0] // SEQ_TILE) // 4,)
    (ya, *saved), (late,) = _fwd_lru(z, *lru, jobs=[_gather_far_job(late).then(_gather_pass_job, at=late_step)])
    w_oa, w_ob, w_o = [w.reshape(D_MODEL, D_MODEL) for w in late]
    (yb, pa, pb, h1, n2), _ = _fwd_sgu_merge(ya, z, xs, *sgu, w_oa, w_ob, w_o, norm_mlp_g)
    (act, dup, dh2b, dh1, loss_part, dg3, dg2), _ = _mlp(n2, h1, target, w_up_st, w_dn, norm_mlp_g, g3)

    d_down, _ = _weight_grad("dw_down", act, dh2b, N_CHIPS, True, False, D_MODEL)
    r_down, = _sequencer_call("send_w_down", 10, _pair_send_job([d_down]))
    d_up, _ = _weight_grad("dw_up", n2, dup, N_CHIPS, False, True, D_MODEL)
    r_up, = _sequencer_call("send_w_up", 11, _pair_send_job([d_up]))
    (p_down,), (p_up,) = pair_add("w_down", [d_down], [r_down]), pair_add("w_up", [d_up], [r_up])
    (dz, merged, dpa, dpb, dh1b, dlg, dlb, dws, dbs, dcw, dcb, dwr, dbr, dwi, dbi, dlam), ((q_up, q_down),) = _bwd_mix(
        dh1, pa, pb, z, *saved, w_oa, w_ob, w_o, *sgu, conv_w_full, wr, wi, lru_lambda,
        jobs=[_chip_exchange_job([p_up, p_down])])
    names = ("w_in", "w_rgate", "w_igate", "w_out_a", "w_out_b", "w_out", "w_up", "w_down")
    (d_out, d_oa, d_ob), _ = _weight_grads_square("dw_projections", [(merged, dh1b), (ya, dpa), (yb, dpb)])
    mids = [quarter(d_oa), quarter(d_ob), quarter(d_out)]
    r_mids = _sequencer_call("send_mids", 1, _pair_send_job(mids))
    half_up, half_down = chip_sum("mlp", [p_up, p_down], after([q_up, q_down], mids))
    gates = [_stack_heads(dwr).astype(BF16), _stack_heads(dwi).astype(BF16)]
    small = _pack_small(dcw, dcb, dbr, dbi, dlam, dlg, dlb, dg2, dg3, loss_part, dbs)
    p_mids = pair_add("projections", after(mids, [half_up, half_down]), r_mids)
    q_mids = _sequencer_call("exchange_mids", 2, _chip_exchange_job(p_mids))
    d_in, (r_gates, (vec_all, ws_all), (full_up, full_down)) = _weight_grad(
        "dw_in", n1, after(dz, p_mids), N_CHIPS, False, True, IN_SHARD,
        jobs=[_pair_send_job(gates), _gather_all_job([small, dws]), _share_job([half_up, half_down])])
    r_in, = _sequencer_call("send_w_in", 3, _pair_send_job(after([d_in], q_mids)))
    adam_args = {nm: (w, m, v) for nm, w, m, v in zip(names, big_w, big_m, big_v)}

    def adamw(group, nms, grads):
        given = [adam_args[nm] for nm in nms]
        grads = [g.reshape(w.shape) for g, (w, _, _) in zip(grads, given)]
        outs = _adamw("adamw_" + group, grads, *[[a[q] for a in given] for q in range(3)])
        return {nm: (out[0], out[1:]) for nm, out in zip(nms, outs)}

    p_gates = pair_add("gates", gates, r_gates)
    half_mids = chip_sum("projections", p_mids, q_mids)
    full_mids = _sequencer_call("share_mids", 12, _share_job(half_mids))
    p_first = pair_add("w_in", after([d_in], half_mids), [r_in]) + p_gates
    q_first = _sequencer_call("exchange_w_in", 4, _chip_exchange_job(p_first))
    (grad_x, dg1), _ = _bwd_in(dz, xs, dh1, w_in_st, norm_mix_g)
    dg1_all, = _sequencer_call("gather_dg1", 6, _gather_all_job([dg1]))
    done = adamw("mlp", ("w_up", "w_down"), [full_up, full_down])
    q_first = after(q_first, [out[0] for _, out in done.values()])
    half_first = chip_sum("first", p_first, q_first)
    full_first = _sequencer_call("share_last", 5, _share_job(half_first))
    done.update(adamw("projections", names[3:6], after(full_mids, half_first)))
    done.update(adamw("first", names[:3], full_first))
    full, big_out = [done[nm][0] for nm in names], [done[nm][1] for nm in names]

    vec, ws_sum = _sum_small(vec_all, ws_all, dg1_all)
    row = lambda r: vec[r:r + 1]
    shard = lambda a, width: lax.dynamic_slice_in_dim(a, chip * width, width, axis=1)
    g_small = dict(
        norm_mix_g=row(ROW_G1), conv_w=shard(vec[ROW_CW:ROW_CW + CONV_WIDTH], quarter_d), conv_b=row(ROW_CB),
        b_rgate=shard(row(ROW_BR).reshape(HEADS, HEAD_DIM), quarter_h),
        b_igate=shard(row(ROW_BI).reshape(HEADS, HEAD_DIM), quarter_h), lru_lambda=row(ROW_LAM),
        sgu_ln_g=row(ROW_LG), sgu_ln_b=row(ROW_LB),
        sgu_w_s=ws_sum.reshape(CHUNK, GROUPS, CHUNK).transpose(1, 0, 2).reshape(GROUPS * CHUNK, CHUNK),
        sgu_b_s=vec[ROW_BS:ROW_BS + GROUPS, 0:CHUNK], norm_mlp_g=row(ROW_G2), norm_final_g=row(ROW_G3))
    loss = vec[ROW_LOSS, 0]
    small_names = list(g_small)
    given = dict(
        norm_mix_g=(norm_mix_g, m_norm_mix_g, v_norm_mix_g), conv_w=(conv_w, m_conv_w, v_conv_w),
        conv_b=(conv_b, m_conv_b, v_conv_b), b_rgate=(b_rgate, m_b_rgate, v_b_rgate),
        b_igate=(b_igate, m_b_igate, v_b_igate), lru_lambda=(lru_lambda, m_lru_lambda, v_lru_lambda),
        sgu_ln_g=(sgu_ln_g, m_sgu_ln_g, v_sgu_ln_g), sgu_ln_b=(sgu_ln_b, m_sgu_ln_b, v_sgu_ln_b),
        sgu_w_s=(sgu_w_s, m_sgu_w_s, v_sgu_w_s), sgu_b_s=(sgu_b_s, m_sgu_b_s, v_sgu_b_s),
        norm_mlp_g=(norm_mlp_g, m_norm_mlp_g, v_norm_mlp_g), norm_final_g=(norm_final_g, m_norm_final_g, v_norm_final_g))
    g2d = [g_small[nm] for nm in small_names]
    to2d = lambda a, g: a.reshape(g.shape)
    d_s, m_s, v_s = _adamw_small(
        g2d, *[[to2d(given[nm][q], g) for nm, g in zip(small_names, g2d)] for q in range(3)])

    shapes = dict(
        norm_mix_g=norm_mix_g, w_in=w_in, conv_w=conv_w, conv_b=conv_b, w_rgate=w_rgate, b_rgate=b_rgate,
        w_igate=w_igate, b_igate=b_igate, lru_lambda=lru_lambda, w_out_a=w_out_a, sgu_ln_g=sgu_ln_g,
        sgu_ln_b=sgu_ln_b, sgu_w_s=sgu_w_s, sgu_b_s=sgu_b_s, w_out_b=w_out_b, w_out=w_out, norm_mlp_g=norm_mlp_g,
        w_up=w_up, w_down=w_down, norm_final_g=norm_final_g)
    grads, deltas, new_m, new_v = {}, {}, {}, {}
    for nm, g, (d, nmom, nvar) in zip(names, full, big_out):
        grads[nm], deltas[nm], new_m[nm], new_v[nm] = g, d, nmom, nvar
    for p, nm in enumerate(small_names):
        grads[nm], deltas[nm], new_m[nm], new_v[nm] = g2d[p], d_s[p], m_s[p], v_s[p]
    order = list(shapes)
    out = [loss, grad_x[None]]
    for group in (grads, deltas, new_m, new_v):
        out += [group[nm].reshape(shapes[nm].shape) for nm in order]
    return tuple(out)
```

```python
import functools

import jax
import jax.numpy as jnp
from jax import lax
from jax.experimental import pallas as pl
from jax.experimental.pallas import tpu as pltpu
from jax.experimental.pallas import tpu_sc as plsc

F32 = jnp.float32
BF16 = jnp.bfloat16
MESH = pl.DeviceIdType.MESH

D_MODEL = 1024
D_IN = 6 * D_MODEL
D_FF = 4 * D_MODEL
N_CHIPS = 4
IN_SHARD = D_IN // N_CHIPS
HEADS = 4
HEAD_DIM = D_MODEL // HEADS
GROUPS = 4
GROUP_DIM = D_MODEL // GROUPS
CHUNK = 128
CONV_WIDTH = 4
LRU_C = 8.0
NORM_EPS = 1e-6
LN_EPS = 1e-5

ADAM_LR = 0.001
ADAM_B1 = 0.9
ADAM_B2 = 0.999
ADAM_EPS = 1e-08
ADAM_WD = 0.01
ADAM_STEP = 10

SUBLANES = 8
MM_TILE = 512
IN_TILE = 1024
SEQ_TILE = 256
DW_TILE = 2048
VMEM_LIMIT_BYTES = 56 * 1024 * 1024

GELU_K0 = 0.7978845608028654
GELU_K1 = 0.044715


def _params(n_grid_axes=1):
    return pltpu.CompilerParams(
        dimension_semantics=("arbitrary",) * n_grid_axes, vmem_limit_bytes=VMEM_LIMIT_BYTES)


def _resident(shape):
    nd = len(shape)
    return pl.BlockSpec(shape, lambda *_: (0,) * nd, pipeline_mode=pl.Buffered(1))


def _const(shape):
    nd = len(shape)
    return pl.BlockSpec(shape, lambda *_: (0,) * nd)


def _dot(a, b):
    return jnp.dot(a, b, preferred_element_type=F32)


def _dot_nt(a, b):
    return lax.dot_general(a, b, (((1,), (1,)), ((), ())), preferred_element_type=F32)


def _dot_tn(a, b):
    return lax.dot_general(a, b, (((0,), (0,)), ((), ())), preferred_element_type=F32)


def _gelu(x):
    t = jnp.tanh(x * (GELU_K0 + (GELU_K0 * GELU_K1) * (x * x)))
    return x * (0.5 + 0.5 * t)


def _gelu_and_grad(x):
    x2 = x * x
    t = jnp.tanh(x * (GELU_K0 + (GELU_K0 * GELU_K1) * x2))
    s = 0.5 + 0.5 * t
    dg = s + (x * (1.0 - t * t)) * (0.5 * GELU_K0 + (1.5 * GELU_K0 * GELU_K1) * x2)
    return x * s, dg


def _gate(x):
    return 0.5 + 0.5 * jnp.tanh(0.5 * x.astype(F32))


def _rms(x):
    r = lax.rsqrt(jnp.mean(x * x, axis=-1, keepdims=True) + NORM_EPS)
    return x * r, r


def _rms_bwd(dn, xhat, r):
    return r * (dn - xhat * jnp.mean(dn * xhat, axis=-1, keepdims=True))


def _col_sum(v):
    return jnp.sum(v, axis=0, keepdims=True)


def _shift_down(x, tail8, k):
    xs = pltpu.roll(x, k, 0)
    ts = pltpu.roll(tail8, k, 0)
    ridx = lax.broadcasted_iota(jnp.int32, tail8.shape, 0)
    head = jnp.where(ridx < k, ts, xs[0:SUBLANES])
    return jnp.concatenate([head, xs[SUBLANES:]], axis=0)


def _shift_up(x, head8, k):
    n = x.shape[0]
    xs = pltpu.roll(x, n - k, 0)
    hs = pltpu.roll(head8, SUBLANES - k, 0)
    ridx = lax.broadcasted_iota(jnp.int32, head8.shape, 0)
    last = jnp.where(ridx >= SUBLANES - k, hs, xs[n - SUBLANES:n])
    return jnp.concatenate([xs[:n - SUBLANES], last], axis=0)


def _scan_forward(a, b, carry):
    n, cols = a.shape
    groups = n // SUBLANES
    a = a.reshape(groups, SUBLANES, cols)
    b = b.reshape(groups, SUBLANES, cols)
    sub = lax.broadcasted_iota(jnp.int32, a.shape, 1)
    for s in (1, 2, 4):
        a_s = pltpu.roll(a, s, 1)
        b_s = pltpu.roll(b, s, 1)
        m = sub >= s
        b = jnp.where(m, a * b_s + b, b)
        a = jnp.where(m, a * a_s, a)
    out = []
    for g in range(groups):
        h = a[g] * carry + b[g]
        out.append(h)
        carry = h[SUBLANES - 1:SUBLANES]
    return jnp.concatenate(out, axis=0), carry


def _scan_backward(a, b, carry):
    n, cols = a.shape
    groups = n // SUBLANES
    a = a.reshape(groups, SUBLANES, cols)
    b = b.reshape(groups, SUBLANES, cols)
    sub = lax.broadcasted_iota(jnp.int32, a.shape, 1)
    for s in (1, 2, 4):
        a_s = pltpu.roll(a, SUBLANES - s, 1)
        b_s = pltpu.roll(b, SUBLANES - s, 1)
        m = sub < SUBLANES - s
        b = jnp.where(m, a * b_s + b, b)
        a = jnp.where(m, a * a_s, a)
    out = [None] * groups
    for g in reversed(range(groups)):
        h = a[g] * carry + b[g]
        out[g] = h
        carry = h[0:1]
    return jnp.concatenate(out, axis=0), carry


def _softplus_neg(lam):
    e = jnp.exp(-jnp.abs(lam))
    u = 1.0 + e
    log1p_e = jnp.where(u == 1.0, e, jnp.log(u) * (e / jnp.where(u == 1.0, 1.0, u - 1.0)))
    return jnp.maximum(-lam, 0.0) + log1p_e


def _lru_gates(xa, tail8, cw_ref, cb_ref, wr_ref, br_ref, wi_ref, bi_ref, lam_ref):
    cw = cw_ref[...]
    xc = cb_ref[...] + cw[0:1] * xa
    for k in range(1, CONV_WIDTH):
        xc = xc + cw[k:k + 1] * _shift_down(xa, tail8, k)
    xcb = xc.astype(BF16)
    pre_r, pre_i = [], []
    for h in range(HEADS):
        cols = slice(h * HEAD_DIM, (h + 1) * HEAD_DIM)
        pre_r.append(_dot(xcb[:, cols], wr_ref[h]))
        pre_i.append(_dot(xcb[:, cols], wi_ref[h]))
    r = jax.nn.sigmoid(jnp.concatenate(pre_r, axis=1) + br_ref[...])
    ig = jax.nn.sigmoid(jnp.concatenate(pre_i, axis=1) + bi_ref[...])
    _, a, mult, _ = _decay(r, lam_ref)
    return xc, r, ig, a, mult


def _decay(r, lam_ref):
    sp = _softplus_neg(lam_ref[...])
    log_a = ((-LRU_C) * sp) * r
    a = jnp.exp(log_a)
    th = jnp.tanh(log_a)
    q = (-2.0 * th) / (1.0 - th)
    inv = lax.rsqrt(q)
    return sp, a, jnp.where(q > 0.0, q * inv, 0.0), inv


class _Phase:
    def __init__(self, copies, n_sem, n_local, start=None, finish=None):
        self.copies, self.n_sem, self.n_local, self.start, self.finish = copies, n_sem, n_local, start, finish


class _Job:
    def __init__(self, inputs, out_shape, n_sem, copies, peers, aliases=None, n_local=0):
        self.inputs, self.out_shape = list(inputs), list(out_shape)
        self.aliases = dict(aliases or {})
        self.phases = [_Phase(copies, n_sem, n_local)]
        self.peers = tuple(peers)

    def then(self, make, at=None):
        nxt = make(self.out_shape)
        self.phases[-1].finish = at
        nxt.phases[0].start = at
        self.phases += nxt.phases
        self.peers = tuple(sorted(set(self.peers + nxt.peers)))
        return self


def _fused_call(body, jobs, *, name, grid, in_specs, out_specs, out_shape, scratch_shapes=(),
                input_output_aliases=None, compiler_params=None, n_prefetch=0, jobs_start_after=None):
    single = not isinstance(out_shape, (list, tuple))
    out_specs = [out_specs] if single else list(out_specs)
    out_shape = [out_shape] if single else list(out_shape)
    n_scr = len(scratch_shapes)
    in_specs, scratch_shapes = list(in_specs), list(scratch_shapes)
    n_in, n_out = len(in_specs), len(out_shape)
    aliases = dict(input_output_aliases or {})
    in_at, out_at, phases = [], [], []
    for q, job in enumerate(jobs):
        in_at.append(len(in_specs))
        out_at.append(len(out_shape))
        for i, o in job.aliases.items():
            aliases[n_prefetch + len(in_specs) + i] = len(out_shape) + o
        in_specs += [ANY] * len(job.inputs)
        out_specs += [ANY] * len(job.out_shape)
        out_shape += job.out_shape
        for k, phase in enumerate(job.phases):
            phases.append((q, k, phase, len(scratch_shapes)))
            scratch_shapes += [pltpu.SemaphoreType.DMA((phase.n_sem,)), pltpu.SemaphoreType.DMA((phase.n_sem,)),
                               pltpu.SemaphoreType.DMA((max(phase.n_local, 1),))]
    n_in_all, n_out_all = len(in_specs), len(out_shape)
    first_step, last_step = (0,) * len(grid), tuple(g - 1 for g in grid)

    def full_body(*refs):
        prefetch, refs = refs[:n_prefetch], refs[n_prefetch:]
        ins, outs, scr = refs[:n_in_all], refs[n_in_all:n_in_all + n_out_all], refs[n_in_all + n_out_all:]
        ids = [pl.program_id(a) for a in range(len(grid))]
        at_step = lambda step: functools.reduce(jnp.logical_and, [i == k for i, k in zip(ids, step)])

        def copies(q, k, phase, sem_at):
            job = jobs[q]
            mine = outs[out_at[q]:out_at[q] + len(job.out_shape)]
            return phase.copies(ins[in_at[q]:in_at[q] + len(job.inputs)] if k == 0 else mine, mine,
                                *scr[sem_at:sem_at + 3])

        def start(*phase):
            def go():
                sends, _, local = copies(*phase)
                for cp in local + sends:
                    cp.start()
            return go

        def finish(*phase):
            def go():
                sends, arrivals, local = copies(*phase)
                for cp in arrivals:
                    cp.wait_recv()
                for cp in sends:
                    cp.wait_send()
                for cp in local:
                    cp.wait()
            return go

        for phase in phases:
            if phase[2].start is None and jobs_start_after is None:
                pl.when(at_step(first_step))(start(*phase))
        body(*prefetch, *ins[:n_in], *outs[:n_out], *scr[:n_scr])
        for phase in phases:
            pl.when(at_step(phase[2].finish or last_step))(finish(*phase))
            nxt = phase[2].start or jobs_start_after
            if nxt is not None:
                pl.when(at_step(nxt))(start(*phase))

    if n_prefetch:
        layout = dict(grid_spec=pltpu.PrefetchScalarGridSpec(
            num_scalar_prefetch=n_prefetch, grid=grid, in_specs=in_specs, out_specs=out_specs,
            scratch_shapes=scratch_shapes))
    else:
        layout = dict(grid=grid, in_specs=in_specs, out_specs=out_specs, scratch_shapes=scratch_shapes)
    call = pl.pallas_call(
        full_body, name=name, out_shape=out_shape, input_output_aliases=aliases, compiler_params=compiler_params,
        **layout)

    def run(*args):
        res = call(*args, *[a for job in jobs for a in job.inputs])
        mine = res[0] if single else list(res[:n_out])
        return mine, [list(res[at:at + len(job.out_shape)]) for at, job in zip(out_at, jobs)]

    return run


def _fwd_in(x, g1, shards, order, jobs=()):
    t = x.shape[0]
    rows_per_step = min(IN_TILE, t)
    n_tiles = t // rows_per_step
    n = len(shards)
    halves = [s.shape[0] // 2 for s in shards]

    def body(order_ref, x_ref, g_ref, *refs):
        del order_ref
        ins, (z_ref, n_ref), outs = refs[:n], refs[n:n + 2], refs[n + 2:2 * n + 2]
        wbuf, nbuf, send, recv, local = refs[2 * n + 2:]
        s, i = pl.program_id(0), pl.program_id(1)
        x_, y_, c, chips = _place()
        near, far = _near_far(x_, y_, c)
        k_me = _chip_index(x_, y_)

        def block(w, chip, pc):
            return outs[w].at[_chip_index(*chip), pl.ds(pc * halves[w], halves[w]), :]

        def over_ici(w, j, landing):
            return pltpu.make_async_remote_copy(
                src_ref=ins[w].at[pl.ds(c * halves[w], halves[w]), :],
                dst_ref=block(w, chips[j] if landing else (x_, y_), c), send_sem=send.at[6 * w + j],
                recv_sem=recv.at[6 * w + j], device_id=(*chips[j], c), device_id_type=MESH)

        def onward(w, landing):
            blk = block(w, chips[2] if landing else near, c)
            return pltpu.make_async_remote_copy(
                src_ref=blk, dst_ref=blk, send_sem=send.at[6 * w + 2], recv_sem=recv.at[6 * w + 2],
                device_id=(*far, c), device_id_type=MESH)

        def to_sibling(w, j, landing):
            blk = block(w, chips[j], 1 - c if landing else c)
            return pltpu.make_async_remote_copy(
                src_ref=blk, dst_ref=blk, send_sem=send.at[6 * w + 3 + j], recv_sem=recv.at[6 * w + 3 + j],
                device_id=(x_, y_, 1 - c), device_id_type=MESH)

        own = [pltpu.make_async_copy(wbuf, outs[0].at[k_me], local.at[0])]
        own += [pltpu.make_async_copy(ins[w], outs[w].at[k_me], local.at[w]) for w in range(1, n)]

        @pl.when((s == 0) & (i == 0))
        def _():
            for j in range(2):
                for w in range(n):
                    over_ici(w, j, False).start()
            load = pltpu.make_async_copy(ins[0], wbuf, local.at[n])
            load.start()
            load.wait()
            for cp in own:
                cp.start()

        for j in range(N_CHIPS - 1):
            @pl.when((s == j + 1) & (i == 0))
            def _(j=j):
                if j == 0:
                    for k in range(2):
                        for w in range(n):
                            over_ici(w, k, True).wait_recv()
                    for w in range(n):
                        onward(w, False).start()
                    for k in range(2):
                        for w in range(n):
                            to_sibling(w, k, False).start()
                    own[0].wait()
                if j == 2:
                    for w in range(n):
                        onward(w, True).wait_recv()
                    for w in range(n):
                        to_sibling(w, j, False).start()
                for w in range(n):
                    to_sibling(w, j, True).wait_recv()
                load = pltpu.make_async_copy(outs[0].at[_chip_index(*chips[j])], wbuf, local.at[n])
                load.start()
                load.wait()

        rows = pl.ds(pl.multiple_of(i * rows_per_step, rows_per_step), rows_per_step)

        @pl.when(s == 0)
        def _():
            xhat, _ = _rms(x_ref[...])
            nrm = (xhat * g_ref[...]).astype(BF16)
            nbuf[rows, :] = nrm
            n_ref[...] = nrm

        z_ref[...] = _dot(nbuf[rows, :], wbuf[...]).astype(BF16)

        @pl.when((s == N_CHIPS - 1) & (i == n_tiles - 1))
        def _():
            for j in range(N_CHIPS - 1):
                for w in range(n):
                    (over_ici(w, j, False) if j < 2 else onward(w, False)).wait_send()
                    to_sibling(w, j, False).wait_send()
            for cp in own[1:]:
                cp.wait()

    once = lambda s, i, order: (jnp.where(s == 0, i, n_tiles - 1), 0)
    (z, n1, *stacked), job_outs = _fused_call(
        body, jobs, name="fwd_in", grid=(N_CHIPS, n_tiles), n_prefetch=1,
        in_specs=[pl.BlockSpec((rows_per_step, D_MODEL), once), _const((1, D_MODEL))] + [ANY] * n,
        out_specs=[pl.BlockSpec((rows_per_step, IN_SHARD), lambda s, i, order: (i, order[s])),
                   pl.BlockSpec((rows_per_step, D_MODEL), once)] + [ANY] * n,
        out_shape=[jax.ShapeDtypeStruct((t, D_IN), BF16), jax.ShapeDtypeStruct((t, D_MODEL), BF16)]
        + [jax.ShapeDtypeStruct((N_CHIPS,) + s.shape, s.dtype) for s in shards],
        scratch_shapes=[pltpu.VMEM(shards[0].shape, BF16), pltpu.VMEM((t, D_MODEL), BF16),
                        pltpu.SemaphoreType.DMA((6 * n,)),
                        pltpu.SemaphoreType.DMA((6 * n,)), pltpu.SemaphoreType.DMA((n + 1,))],
        compiler_params=_params(2), jobs_start_after=(1, 0),
    )(order, x, g1, *shards)
    return (z, n1, stacked), job_outs


def _fwd_lru(z, conv_w, conv_b, wr, br, wi, bi, lam, jobs=()):
    t = z.shape[0]

    def body(xa_ref, ga_ref, cw_ref, cb_ref, wr_ref, br_ref, wi_ref, bi_ref, lam_ref, ya_ref, h_ref, xc_ref, r_ref,
             ig_ref, tail_ref, carry_ref):
        @pl.when(pl.program_id(0) == 0)
        def _():
            tail_ref[...] = jnp.zeros_like(tail_ref)
            carry_ref[...] = jnp.zeros_like(carry_ref)

        xa = xa_ref[...].astype(F32)
        xc, r, ig, a, mult = _lru_gates(xa, tail_ref[...], cw_ref, cb_ref, wr_ref, br_ref, wi_ref, bi_ref, lam_ref)
        tail_ref[...] = xa[SEQ_TILE - SUBLANES:]
        xc_ref[...], r_ref[...], ig_ref[...] = xc, r, ig
        h, carry = _scan_forward(a, xc * ig * mult, carry_ref[...])
        carry_ref[...] = carry
        h_ref[...] = h
        ya_ref[...] = (h * _gelu(ga_ref[...].astype(F32))).astype(BF16)

    tile = lambda j: pl.BlockSpec((SEQ_TILE, D_MODEL), lambda i: (i, j))
    return _fused_call(
        body, jobs, name="fwd_lru", grid=(t // SEQ_TILE,),
        in_specs=[tile(0), tile(1), _const((CONV_WIDTH, D_MODEL)), _const((1, D_MODEL)),
                  _resident((HEADS, HEAD_DIM, HEAD_DIM)), _const((1, D_MODEL)),
                  _resident((HEADS, HEAD_DIM, HEAD_DIM)), _const((1, D_MODEL)), _const((1, D_MODEL))],
        out_specs=[tile(0)] * 5,
        out_shape=[jax.ShapeDtypeStruct((t, D_MODEL), BF16)] + [jax.ShapeDtypeStruct((t, D_MODEL), F32)] * 4,
        scratch_shapes=[pltpu.VMEM((SUBLANES, D_MODEL), F32), pltpu.VMEM((1, D_MODEL), F32)],
        compiler_params=_params(),
    )(z, z, conv_w, conv_b, wr, br, wi, bi, lam)


def _sgu_forward_parts(ub, vb, lg_ref, lb_ref):
    u, du = _gelu_and_grad(ub.astype(F32))
    vg, dvg = _gelu_and_grad(vb.astype(F32))
    mu = jnp.mean(vg, axis=-1, keepdims=True)
    d = vg - mu
    rstd = lax.rsqrt(jnp.mean(d * d, axis=-1, keepdims=True) + LN_EPS)
    vhat = d * rstd
    vn = (vhat * lg_ref[...] + lb_ref[...]).astype(BF16)
    return u, du, dvg, rstd, vhat, vn


def _causal_mask():
    rows = lax.broadcasted_iota(jnp.int32, (CHUNK, CHUNK), 0)
    cols = lax.broadcasted_iota(jnp.int32, (CHUNK, CHUNK), 1)
    return rows >= cols


def _fwd_sgu_merge(ya, z, x, ln_g, ln_b, w_s, bias_full, w_oa, w_ob, w_out, g2, jobs=()):
    t = x.shape[0]

    def body(ya_ref, ub_ref, vb_ref, m_ref, x_ref, lg_ref, lb_ref, ws_ref, bias_ref, woa_ref, wob_ref, wout_ref, g_ref,
             yb_ref, pa_ref, pb_ref, h1_ref, n2_ref):
        u, _, _, _, _, vn = _sgu_forward_parts(ub_ref[...], vb_ref[...], lg_ref, lb_ref)
        mask = _causal_mask()
        wm = [jnp.where(mask, ws_ref[g], 0.0).astype(BF16) for g in range(GROUPS)]
        for c in range(SEQ_TILE // CHUNK):
            rows = slice(c * CHUNK, (c + 1) * CHUNK)
            for g in range(GROUPS):
                cols = slice(g * GROUP_DIM, (g + 1) * GROUP_DIM)
                sp = _dot(wm[g], vn[rows, cols]) + bias_ref[:, cols]
                yb_ref[rows, cols] = (u[rows, cols] * sp).astype(BF16)
        pa = _dot(ya_ref[...], woa_ref[...])
        pb = _dot(yb_ref[...], wob_ref[...])
        pa_ref[...] = pa
        pb_ref[...] = pb
        merged = _gate(m_ref[:, :D_MODEL]) * pa + _gate(m_ref[:, D_MODEL:]) * pb
        h1 = x_ref[...] + _dot(merged.astype(BF16), wout_ref[...])
        h1_ref[...] = h1
        xhat, _ = _rms(h1)
        n2_ref[...] = (xhat * g_ref[...]).astype(BF16)

    tile = lambda j: pl.BlockSpec((SEQ_TILE, D_MODEL), lambda i: (i, j))
    sq = _resident((D_MODEL, D_MODEL))
    vec = _const((1, D_MODEL))
    bf, f32 = jax.ShapeDtypeStruct((t, D_MODEL), BF16), jax.ShapeDtypeStruct((t, D_MODEL), F32)
    return _fused_call(
        body, jobs, name="fwd_sgu_merge", grid=(t // SEQ_TILE,),
        in_specs=[tile(0), tile(2), tile(3), pl.BlockSpec((SEQ_TILE, 2 * D_MODEL), lambda i: (i, 2)), tile(0), vec, vec,
                  _const((GROUPS, CHUNK, CHUNK)), _const((CHUNK, D_MODEL)), sq, sq, sq, vec],
        out_specs=[tile(0)] * 5,
        out_shape=[bf, f32, f32, f32, bf],
        compiler_params=_params(),
    )(ya, z, z, z, x, ln_g, ln_b, w_s, bias_full, w_oa, w_ob, w_out, g2)


def _mlp(n2, h1, target, w_up_st, w_down, g2, g3, jobs=()):
    t = n2.shape[0]

    def body(n2_ref, h1_ref, tgt_ref, wup_ref, wdown_ref, g2_ref, g3_ref, act_ref, dup_ref, dh2b_ref, dh1_ref,
             loss_ref, dg3_ref, dg2_ref, relu_ref):
        @pl.when(pl.program_id(0) == 0)
        def _():
            for ref in (loss_ref, dg3_ref, dg2_ref):
                ref[...] = jnp.zeros_like(ref)

        n2 = n2_ref[...]
        h1 = h1_ref[...]
        h2 = h1
        for k in range(N_CHIPS):
            cols = slice(k * D_MODEL, (k + 1) * D_MODEL)
            r = jnp.maximum(_dot(n2, wup_ref[k]), 0.0)
            relu_ref[:, cols] = r
            act = (r * r).astype(BF16)
            act_ref[:, cols] = act
            h2 = h2 + _dot(act, wdown_ref[cols, :])
        xhat, r3 = _rms(h2)
        diff = xhat * g3_ref[...] - tgt_ref[...]
        sq = jnp.sum(diff * diff, axis=1, keepdims=True)
        loss_ref[...] = loss_ref[...] + (0.5 / D_MODEL) * jnp.sum(sq, axis=0, keepdims=True)
        dy = diff * (1.0 / D_MODEL)
        dg3_ref[...] = dg3_ref[...] + _col_sum(dy * xhat)
        dh2 = _rms_bwd(dy * g3_ref[...], xhat, r3)
        dh2b = dh2.astype(BF16)
        dh2b_ref[...] = dh2b
        dn2 = jnp.zeros((SEQ_TILE, D_MODEL), F32)
        for k in range(N_CHIPS):
            cols = slice(k * D_MODEL, (k + 1) * D_MODEL)
            dup = (_dot_nt(dh2b, wdown_ref[cols, :]) * (2.0 * relu_ref[:, cols])).astype(BF16)
            dup_ref[:, cols] = dup
            dn2 = dn2 + _dot_nt(dup, wup_ref[k])
        xhat, r2 = _rms(h1)
        dg2_ref[...] = dg2_ref[...] + _col_sum(dn2 * xhat)
        dh1_ref[...] = dh2 + _rms_bwd(dn2 * g2_ref[...], xhat, r2)

    tile = pl.BlockSpec((SEQ_TILE, D_MODEL), lambda i: (i, 0))
    wide = pl.BlockSpec((SEQ_TILE, D_FF), lambda i: (i, 0))
    vec = _const((1, D_MODEL))
    vec_shape = jax.ShapeDtypeStruct((1, D_MODEL), F32)
    return _fused_call(
        body, jobs, name="mlp", grid=(t // SEQ_TILE,),
        in_specs=[tile, tile, tile, _resident((N_CHIPS, D_MODEL, D_MODEL)), _resident((D_FF, D_MODEL)), vec, vec],
        out_specs=[wide, wide, tile, tile, _const((SUBLANES, 128)), vec, vec],
        out_shape=[jax.ShapeDtypeStruct((t, D_FF), BF16), jax.ShapeDtypeStruct((t, D_FF), BF16),
                   jax.ShapeDtypeStruct((t, D_MODEL), BF16), jax.ShapeDtypeStruct((t, D_MODEL), F32),
                   jax.ShapeDtypeStruct((SUBLANES, 128), F32), vec_shape, vec_shape],
        scratch_shapes=[pltpu.VMEM((SEQ_TILE, D_FF), F32)],
        compiler_params=_params(),
    )(n2, h1, target, w_up_st, w_down, g2, g3)


def _bwd_mix(dh1, pa, pb, z, h, xc, r, ig, w_oa, w_ob, w_out, ln_g, ln_b, w_s, bias_full, conv_w, wr, wi, lam, jobs=()):
    t = dh1.shape[0]
    n_tiles = t // SEQ_TILE
    per_tile = SEQ_TILE // SUBLANES

    def merge_part(dh1_ref, pa_ref, pb_ref, m_ref, woa_ref, wob_ref, wout_ref, dz_ref, dya_ref, dyb_ref, mg_ref,
                   dpa_ref, dpb_ref, dh1b_ref):
        dh1b = dh1_ref[...].astype(BF16)
        dh1b_ref[...] = dh1b
        dm = _dot_nt(dh1b, wout_ref[...])
        pa = pa_ref[...]
        pb = pb_ref[...]
        sa = _gate(m_ref[:, :D_MODEL])
        sb = _gate(m_ref[:, D_MODEL:])
        mg_ref[...] = (sa * pa + sb * pb).astype(BF16)
        dpa = dm * sa
        dpb = dm * sb
        dz_ref[:, :D_MODEL] = ((dpa * pa) * (1.0 - sa)).astype(BF16)
        dz_ref[:, D_MODEL:] = ((dpb * pb) * (1.0 - sb)).astype(BF16)
        dpa = dpa.astype(BF16)
        dpb = dpb.astype(BF16)
        dpa_ref[...] = dpa
        dpb_ref[...] = dpb
        dya_ref[...] = _dot_nt(dpa, woa_ref[...])
        dyb_ref[...] = _dot_nt(dpb, wob_ref[...])

    def sgu_part(dyb_ref, ub_ref, vb_ref, lg_ref, lb_ref, ws_ref, bias_ref, dz_ref, dlg_ref, dlb_ref, dws_ref, dbs_ref,
                 dvn_ref, dsp_acc):
        i = pl.program_id(0)

        @pl.when(i == 0)
        def _():
            dlg_ref[...] = jnp.zeros_like(dlg_ref)
            dlb_ref[...] = jnp.zeros_like(dlb_ref)
            dws_ref[...] = jnp.zeros_like(dws_ref)
            dsp_acc[...] = jnp.zeros_like(dsp_acc)

        u, du, dvg, rstd, vhat, vn = _sgu_forward_parts(ub_ref[...], vb_ref[...], lg_ref, lb_ref)
        dyb = dyb_ref[...]
        mask = _causal_mask()
        wm = [jnp.where(mask, ws_ref[g], 0.0).astype(BF16) for g in range(GROUPS)]
        for c in range(SEQ_TILE // CHUNK):
            rows = slice(c * CHUNK, (c + 1) * CHUNK)
            for g in range(GROUPS):
                cols = slice(g * GROUP_DIM, (g + 1) * GROUP_DIM)
                vn_blk = vn[rows, cols]
                sp = _dot(wm[g], vn_blk) + bias_ref[:, cols]
                dyb_blk = dyb[rows, cols]
                dz_ref[rows, cols] = (dyb_blk * sp * du[rows, cols]).astype(BF16)
                dsp = dyb_blk * u[rows, cols]
                dsp_acc[:, cols] = dsp_acc[:, cols] + dsp
                dspb = dsp.astype(BF16)
                dvn_ref[rows, cols] = _dot_tn(wm[g], dspb)
                wcols = slice(g * CHUNK, (g + 1) * CHUNK)
                dws_ref[:, wcols] = dws_ref[:, wcols] + jnp.where(mask, _dot_nt(dspb, vn_blk), 0.0)
        dvn = dvn_ref[...]
        dlg_ref[...] = dlg_ref[...] + _col_sum(dvn * vhat)
        dlb_ref[...] = dlb_ref[...] + _col_sum(dvn)
        dvhat = dvn * lg_ref[...]
        dvgel = rstd * (dvhat - jnp.mean(dvhat, axis=-1, keepdims=True)
                        - vhat * jnp.mean(dvhat * vhat, axis=-1, keepdims=True))
        dz_ref[:, D_MODEL:] = (dvgel * dvg).astype(BF16)

        @pl.when(i == n_tiles - 1)
        def _():
            lane = lax.broadcasted_iota(jnp.int32, (CHUNK, 128), 1)
            out = jnp.zeros((CHUNK, 128), F32)
            for g in range(GROUPS):
                s = jnp.sum(dsp_acc[:, g * GROUP_DIM:(g + 1) * GROUP_DIM], axis=1, keepdims=True)
                out = out + jnp.where(lane == g, s, 0.0)
            dbs_ref[...] = out

    def lru_part(dya_ref, xa_ref, ga_ref, h_ref, h_prev_ref, xc_ref, r_ref, ig_ref, cw_ref, wr_ref, wi_ref, lam_ref,
                 dz_ref, dcw_ref, dcb_ref, dwr_ref, dbr_ref, dwi_ref, dbi_ref, dlam_ref, lam_carry, dxc_head):
        i = pl.program_id(0)

        @pl.when(i == 0)
        def _():
            for ref in (dcw_ref, dcb_ref, dwr_ref, dbr_ref, dwi_ref, dbi_ref, dlam_ref, lam_carry, dxc_head):
                ref[...] = jnp.zeros_like(ref)

        first_tile = i == n_tiles - 1
        h_tail = jnp.where(first_tile, 0.0, h_prev_ref[...])
        xc, r, ig = xc_ref[...], r_ref[...], ig_ref[...]
        xcb = xc.astype(BF16)
        sp, a, mult, inv_mult = _decay(r, lam_ref)
        h = h_ref[...]
        h_prev = _shift_down(h, h_tail, 1)
        dya = dya_ref[...]
        gg, dgg = _gelu_and_grad(ga_ref[...].astype(F32))
        dz_ref[:, D_MODEL:] = (dya * h * dgg).astype(BF16)
        ones = jnp.ones((SUBLANES, D_MODEL), F32)
        lam_t, lam_first = _scan_backward(_shift_up(a, ones, 1), dya * gg, lam_carry[...])
        lam_carry[...] = a[0:1] * lam_first
        lam_ig = lam_t * ig
        dxc_direct = lam_ig * mult
        dmult = lam_ig * xc
        dla = a * (lam_t * h_prev - (dmult * a) * inv_mult)
        dla_r = dla * r
        dlam_ref[...] = dlam_ref[...] + _col_sum(dla_r) * (LRU_C * jax.nn.sigmoid(-lam_ref[...]))
        dpr = (dla_r * ((-LRU_C) * sp)) * (1.0 - r)
        dpi = (dxc_direct * xc) * (1.0 - ig)
        dbr_ref[...] = dbr_ref[...] + _col_sum(dpr)
        dbi_ref[...] = dbi_ref[...] + _col_sum(dpi)
        dprb = dpr.astype(BF16)
        dpib = dpi.astype(BF16)
        dxc_gate = []
        for hd in range(HEADS):
            cols = slice(hd * HEAD_DIM, (hd + 1) * HEAD_DIM)
            dxc_gate.append(_dot_nt(dprb[:, cols], wr_ref[hd]) + _dot_nt(dpib[:, cols], wi_ref[hd]))
            dwr_ref[hd] = dwr_ref[hd] + _dot_tn(xcb[:, cols], dprb[:, cols])
            dwi_ref[hd] = dwi_ref[hd] + _dot_tn(xcb[:, cols], dpib[:, cols])
        dxc = dxc_direct + jnp.concatenate(dxc_gate, axis=1)
        dcb_ref[...] = dcb_ref[...] + _col_sum(dxc)
        cw = cw_ref[...]
        head = dxc_head[...]
        xa = xa_ref[...].astype(F32)
        dxa = cw[0:1] * dxc
        dcw_ref[0:1, :] = dcw_ref[0:1, :] + _col_sum(dxc * xa)
        for k in range(1, CONV_WIDTH):
            dxc_k = _shift_up(dxc, head, k)
            dxa = dxa + cw[k:k + 1] * dxc_k
            dcw_ref[k:k + 1, :] = dcw_ref[k:k + 1, :] + _col_sum(dxc_k * xa)
        dxc_head[...] = dxc[0:SUBLANES]
        dz_ref[:, :D_MODEL] = dxa.astype(BF16)

    def body(dh1_ref, pa_ref, pb_ref, z_ref, h_ref, h_prev_ref, xc_ref, r_ref, ig_ref, woa_ref, wob_ref, wout_ref,
             lg_ref, lb_ref, ws_ref, bias_ref, cw_ref, wr_ref, wi_ref, lam_ref, dz_ref, mg_ref, dpa_ref, dpb_ref,
             dh1b_ref, dlg_ref, dlb_ref, dws_ref, dbs_ref, dcw_ref, dcb_ref, dwr_ref, dbr_ref, dwi_ref, dbi_ref,
             dlam_ref, dya_ref, dyb_ref, dvn_ref, dsp_acc, lam_carry, dxc_head):
        def cols(ref, first, count):
            return ref.at[:, pl.ds(first * D_MODEL, count * D_MODEL)]

        merge_part(dh1_ref, pa_ref, pb_ref, cols(z_ref, 4, 2), woa_ref, wob_ref, wout_ref, cols(dz_ref, 4, 2), dya_ref,
                   dyb_ref, mg_ref, dpa_ref, dpb_ref, dh1b_ref)
        sgu_part(dyb_ref, cols(z_ref, 2, 1), cols(z_ref, 3, 1), lg_ref, lb_ref, ws_ref, bias_ref, cols(dz_ref, 2, 2),
                 dlg_ref, dlb_ref, dws_ref, dbs_ref, dvn_ref, dsp_acc)
        lru_part(dya_ref, cols(z_ref, 0, 1), cols(z_ref, 1, 1), h_ref, h_prev_ref, xc_ref, r_ref, ig_ref, cw_ref, wr_ref,
                 wi_ref, lam_ref, cols(dz_ref, 0, 2), dcw_ref, dcb_ref, dwr_ref, dbr_ref, dwi_ref, dbi_ref, dlam_ref,
                 lam_carry, dxc_head)

    rev = lambda i: n_tiles - 1 - i
    tile = pl.BlockSpec((SEQ_TILE, D_MODEL), lambda i: (rev(i), 0))
    row = pl.BlockSpec((SEQ_TILE, D_IN), lambda i: (rev(i), 0))
    prev8 = pl.BlockSpec((SUBLANES, D_MODEL), lambda i: (jnp.maximum(rev(i) * per_tile - 1, 0), 0))
    vec = _const((1, D_MODEL))
    sq = _resident((D_MODEL, D_MODEL))
    gate_w = _resident((HEADS, HEAD_DIM, HEAD_DIM))
    gate_acc = _const((HEADS, HEAD_DIM, HEAD_DIM))
    vec_shape = jax.ShapeDtypeStruct((1, D_MODEL), F32)
    gate_shape = jax.ShapeDtypeStruct((HEADS, HEAD_DIM, HEAD_DIM), F32)
    act_bf = jax.ShapeDtypeStruct((t, D_MODEL), BF16)
    return _fused_call(
        body, jobs, name="bwd_mix", grid=(n_tiles,),
        in_specs=[tile, tile, tile, row, tile, prev8, tile, tile, tile, sq, sq, sq, vec, vec,
                  _const((GROUPS, CHUNK, CHUNK)), _const((CHUNK, D_MODEL)), _const((CONV_WIDTH, D_MODEL)), gate_w, gate_w,
                  vec],
        out_specs=[row, tile, tile, tile, tile, vec, vec, _const((CHUNK, GROUPS * CHUNK)), _const((CHUNK, 128)),
                   _const((SUBLANES, D_MODEL)), vec, gate_acc, vec, gate_acc, vec, vec],
        out_shape=[jax.ShapeDtypeStruct((t, D_IN), BF16), act_bf, act_bf, act_bf, act_bf, vec_shape, vec_shape,
                   jax.ShapeDtypeStruct((CHUNK, GROUPS * CHUNK), F32), jax.ShapeDtypeStruct((CHUNK, 128), F32),
                   jax.ShapeDtypeStruct((SUBLANES, D_MODEL), F32), vec_shape, gate_shape, vec_shape, gate_shape,
                   vec_shape, vec_shape],
        scratch_shapes=[pltpu.VMEM((SEQ_TILE, D_MODEL), F32), pltpu.VMEM((SEQ_TILE, D_MODEL), F32),
                        pltpu.VMEM((SEQ_TILE, D_MODEL), F32), pltpu.VMEM((CHUNK, D_MODEL), F32),
                        pltpu.VMEM((1, D_MODEL), F32), pltpu.VMEM((SUBLANES, D_MODEL), F32)],
        compiler_params=_params(),
    )(dh1, pa, pb, z, h, h, xc, r, ig, w_oa, w_ob, w_out, ln_g, ln_b, w_s, bias_full, conv_w, wr, wi, lam)


def _bwd_in(dz, x, dh1, w_in_st, g1, jobs=()):
    t = x.shape[0]

    def body(dz_ref, x_ref, dh1_ref, w_ref, g_ref, dx_ref, dg1_ref):
        @pl.when(pl.program_id(0) == 0)
        def _():
            dg1_ref[...] = jnp.zeros_like(dg1_ref)

        dn1 = jnp.zeros((MM_TILE, D_MODEL), F32)
        for k in range(N_CHIPS):
            dn1 = dn1 + _dot_nt(dz_ref[:, k * IN_SHARD:(k + 1) * IN_SHARD], w_ref[k])
        xhat, r1 = _rms(x_ref[...])
        dg1_ref[...] = dg1_ref[...] + _col_sum(dn1 * xhat)
        dx_ref[...] = dh1_ref[...] + _rms_bwd(dn1 * g_ref[...], xhat, r1)

    tile = pl.BlockSpec((MM_TILE, D_MODEL), lambda i: (i, 0))
    return _fused_call(
        body, jobs, name="bwd_in", grid=(t // MM_TILE,),
        in_specs=[pl.BlockSpec((MM_TILE, D_IN), lambda i: (i, 0)), tile, tile,
                  _resident((N_CHIPS, D_MODEL, IN_SHARD)), _const((1, D_MODEL))],
        out_specs=[tile, _const((1, D_MODEL))],
        out_shape=[jax.ShapeDtypeStruct((t, D_MODEL), F32), jax.ShapeDtypeStruct((1, D_MODEL), F32)],
        compiler_params=_params(),
    )(dz, x, dh1, w_in_st, g1)


def _weight_grad(name, a, b, n_blocks, a_varies, b_varies, width, jobs=()):
    t = a.shape[0]
    rows = min(DW_TILE, t)
    n_t = t // rows

    def body(a_ref, b_ref, o_ref, acc_ref):
        s = pl.program_id(1)
        part = _dot_tn(a_ref[...], b_ref[...])

        @pl.when(s == 0)
        def _():
            acc_ref[...] = part

        @pl.when(s > 0)
        def _():
            acc_ref[...] = acc_ref[...] + part

        @pl.when(s == n_t - 1)
        def _():
            o_ref[...] = acc_ref[...].astype(BF16)

    return _fused_call(
        body, jobs, name=name, grid=(n_blocks, n_t),
        in_specs=[pl.BlockSpec((rows, D_MODEL), (lambda j, s: (s, j)) if a_varies else (lambda j, s: (s, 0))),
                  pl.BlockSpec((rows, width), (lambda j, s: (s, j)) if b_varies else (lambda j, s: (s, 0)))],
        out_specs=pl.BlockSpec((None, D_MODEL, width), lambda j, s: (j, 0, 0)),
        out_shape=jax.ShapeDtypeStruct((n_blocks, D_MODEL, width), BF16),
        scratch_shapes=[pltpu.VMEM((D_MODEL, width), F32)],
        compiler_params=_params(2),
    )(a, b)


def _weight_grads_square(name, pairs, jobs=()):
    n = len(pairs)
    t = pairs[0][0].shape[0]
    rows = min(2 * MM_TILE, t)
    n_t = t // rows

    def body(*refs):
        ins, outs, accs = refs[:2 * n], refs[2 * n:3 * n], refs[3 * n:]
        s = pl.program_id(0)
        for k in range(n):
            part = _dot_tn(ins[2 * k][...], ins[2 * k + 1][...])

            @pl.when(s == 0)
            def _(k=k, part=part):
                accs[k][...] = part

            @pl.when(s > 0)
            def _(k=k, part=part):
                accs[k][...] = accs[k][...] + part

            @pl.when(s == n_t - 1)
            def _(k=k):
                outs[k][...] = accs[k][...].astype(BF16)

    tile = pl.BlockSpec((rows, D_MODEL), lambda s: (s, 0))
    return _fused_call(
        body, jobs, name=name, grid=(n_t,), in_specs=[tile] * (2 * n), out_specs=[_const((D_MODEL, D_MODEL))] * n,
        out_shape=[jax.ShapeDtypeStruct((D_MODEL, D_MODEL), BF16)] * n,
        scratch_shapes=[pltpu.VMEM((D_MODEL, D_MODEL), F32)] * n,
        compiler_params=_params(),
    )(*[x for pair in pairs for x in pair])


def _place():
    x, y, c = lax.axis_index("x"), lax.axis_index("y"), lax.axis_index("c")
    other_chips = [(1 - x, y), (x, 1 - y), (1 - x, 1 - y)]
    return x, y, c, other_chips


def _chip_index(px, py):
    return 2 * px + py


ANY = pl.BlockSpec(memory_space=pl.ANY)
SIBLING = ((0, 0, 1),)
NEIGHBOURS = ((1, 0, 0), (0, 1, 0))
OTHER_CHIPS = NEIGHBOURS + ((1, 1, 0),)


def _near_far(x, y, c):
    return (x ^ (1 - c), y ^ c), (x ^ c, y ^ (1 - c))


def _gather_near_job(shards):
    n = len(shards)
    halves = [s.shape[0] // 2 for s in shards]

    def copies(ins, outs, send, recv, local):
        x, y, c, _ = _place()
        near, _ = _near_far(x, y, c)

        def block(w, chip, pc):
            return outs[w].at[_chip_index(*chip), pl.ds(pc * halves[w], halves[w]), :]

        def copy(w, k, chip, pc, to, src=None):
            return pltpu.make_async_remote_copy(
                src_ref=block(w, chip, pc) if src is None else src, dst_ref=block(w, chip, pc),
                send_sem=send.at[2 * w + k], recv_sem=recv.at[2 * w + k], device_id=to, device_id_type=MESH)

        sends, arrivals, own = [], [], []
        for w in range(n):
            src = ins[w].at[pl.ds(c * halves[w], halves[w]), :]
            own.append(pltpu.make_async_copy(src, block(w, (x, y), c), local.at[w]))
            sends += [copy(w, 0, (x, y), c, (*near, c), src), copy(w, 1, (x, y), c, (x, y, 1 - c), src)]
            arrivals += [copy(w, 0, near, c, (x, y, c)), copy(w, 1, (x, y), 1 - c, (x, y, c))]
        return sends, arrivals, own

    return _Job(shards, [jax.ShapeDtypeStruct((N_CHIPS,) + s.shape, s.dtype) for s in shards], 2 * n, copies,
                NEIGHBOURS + SIBLING, n_local=n)


def _gather_direct_job(shards):
    n = len(shards)
    halves = [s.shape[0] // 2 for s in shards]

    def copies(ins, outs, send, recv, local):
        x, y, c, chips = _place()

        def block(w, chip, pc):
            return outs[w].at[_chip_index(*chip), pl.ds(pc * halves[w], halves[w]), :]

        def copy(w, k, chip, pc, to):
            return pltpu.make_async_remote_copy(
                src_ref=ins[w].at[pl.ds(pc * halves[w], halves[w]), :], dst_ref=block(w, chip, pc),
                send_sem=send.at[4 * w + k], recv_sem=recv.at[4 * w + k], device_id=to, device_id_type=MESH)

        sends, arrivals, own = [], [], []
        for w in range(n):
            own.append(pltpu.make_async_copy(ins[w].at[pl.ds(c * halves[w], halves[w]), :], block(w, (x, y), c),
                                             local.at[w]))
            sends += [copy(w, k, (x, y), c, (*chip, c)) for k, chip in enumerate(chips)]
            sends.append(copy(w, 3, (x, y), c, (x, y, 1 - c)))
            arrivals += [copy(w, k, chip, c, (x, y, c)) for k, chip in enumerate(chips)]
            arrivals.append(copy(w, 3, (x, y), 1 - c, (x, y, c)))
        return sends, arrivals, own

    return _Job(shards, [jax.ShapeDtypeStruct((N_CHIPS,) + s.shape, s.dtype) for s in shards], 4 * n, copies,
                OTHER_CHIPS + SIBLING, n_local=n)


def _gather_far_job(stacked):
    n = len(stacked)
    halves = [s.shape[1] // 2 for s in stacked]

    def copies(ins, outs, send, recv, local):
        del ins, local
        x, y, c, _ = _place()
        near, far = _near_far(x, y, c)

        def copy(w, k, chip):
            blk = outs[w].at[_chip_index(*chip), pl.ds(c * halves[w], halves[w]), :]
            return pltpu.make_async_remote_copy(
                src_ref=blk, dst_ref=blk, send_sem=send.at[2 * w + k], recv_sem=recv.at[2 * w + k],
                device_id=(*far, c), device_id_type=MESH)

        sends = [copy(w, k, chip) for w in range(n) for k, chip in enumerate(((x, y), near))]
        arrivals = [copy(w, k, chip) for w in range(n) for k, chip in enumerate((far, (1 - x, 1 - y)))]
        return sends, arrivals, []

    return _Job(stacked, [jax.ShapeDtypeStruct(s.shape, s.dtype) for s in stacked], 2 * n, copies, NEIGHBOURS,
                aliases={w: w for w in range(n)})


def _gather_pass_job(stacked):
    n = len(stacked)
    halves = [s.shape[1] // 2 for s in stacked]

    def copies(ins, outs, send, recv, local):
        del ins, local
        x, y, c, chips = _place()

        def copy(w, j, chip, pc, to):
            blk = outs[w].at[_chip_index(*chip), pl.ds(pc * halves[w], halves[w]), :]
            return pltpu.make_async_remote_copy(
                src_ref=blk, dst_ref=blk, send_sem=send.at[3 * w + j], recv_sem=recv.at[3 * w + j], device_id=to,
                device_id_type=MESH)

        sends = [copy(w, j, chip, c, (x, y, 1 - c)) for w in range(n) for j, chip in enumerate(chips)]
        arrivals = [copy(w, j, chip, 1 - c, (x, y, c)) for w in range(n) for j, chip in enumerate(chips)]
        return sends, arrivals, []

    return _Job(stacked, [jax.ShapeDtypeStruct(s.shape, s.dtype) for s in stacked], 3 * n, copies, SIBLING,
                aliases={w: w for w in range(n)})


def _gather_small_job(block):
    def copies(ins, outs, send, recv, local):
        x, y, c, chips = _place()

        def copy(j, chip_from, to):
            return pltpu.make_async_remote_copy(
                src_ref=ins[0], dst_ref=outs[0].at[_chip_index(*chip_from)], send_sem=send.at[j],
                recv_sem=recv.at[j], device_id=to, device_id_type=MESH)

        own = [pltpu.make_async_copy(ins[0], outs[0].at[_chip_index(x, y)], local.at[0])]
        sends = [copy(j, (x, y), (*chip, c)) for j, chip in enumerate(chips)]
        arrivals = [copy(j, chip, (x, y, c)) for j, chip in enumerate(chips)]
        return sends, arrivals, own

    return _Job([block], [jax.ShapeDtypeStruct((N_CHIPS,) + block.shape, block.dtype)], 3, copies, OTHER_CHIPS,
                n_local=1)


def _pair_send_job(grads):
    n = len(grads)
    halves = [g.shape[1] // 2 for g in grads]

    def copies(ins, outs, send, recv, local):
        del local
        x, y, c, _ = _place()
        sends = [pltpu.make_async_remote_copy(
            src_ref=ins[w].at[:, pl.ds((1 - c) * halves[w], halves[w]), :], dst_ref=outs[w], send_sem=send.at[w],
            recv_sem=recv.at[w], device_id=(x, y, 1 - c), device_id_type=MESH) for w in range(n)]
        return sends, sends, []

    return _Job(grads, [jax.ShapeDtypeStruct((N_CHIPS, h, g.shape[2]), g.dtype) for g, h in zip(grads, halves)], n,
                copies, SIBLING)


ROW_STEPS = 4


def _pair_add(name, core, mine, theirs):
    n = len(mine)

    def body(core_ref, *refs):
        del core_ref
        for a_ref, b_ref, o_ref in zip(refs[:n], refs[n:2 * n], refs[2 * n:]):
            o_ref[...] = (a_ref[...].astype(F32) + b_ref[...].astype(F32)).astype(BF16)

    half = lambda a: pl.BlockSpec((None, None) + a.shape[2:], lambda k, core_ref: (k, core_ref[0], 0, 0))
    block = lambda b: pl.BlockSpec((None,) + b.shape[1:], lambda k, core_ref: (k, 0, 0))
    return pl.pallas_call(
        body, name=name,
        grid_spec=pltpu.PrefetchScalarGridSpec(
            num_scalar_prefetch=1, grid=(N_CHIPS,),
            in_specs=[half(a) for a in mine] + [block(b) for b in theirs], out_specs=[block(b) for b in theirs]),
        out_shape=[jax.ShapeDtypeStruct(b.shape, BF16) for b in theirs],
        compiler_params=_params(),
    )(core, *mine, *theirs)


def _sequencer_call(name, collective_id, job):
    steps, peers = job.phases, job.peers
    ins = [jax.new_ref(a, memory_space=pltpu.MemorySpace.HBM) for a in job.inputs]
    outs = [ins[{o: i for i, o in job.aliases.items()}[k]] if k in job.aliases.values()
            else jax.empty_ref(shape, memory_space=pltpu.MemorySpace.HBM) for k, shape in enumerate(job.out_shape)]
    sems = [pltpu.SemaphoreType.DMA((n,)) for step in steps for n in (step.n_sem, step.n_sem, max(step.n_local, 1))]

    @pl.kernel(mesh=plsc.ScalarSubcoreMesh(axis_name="sequencer", num_cores=1), name=name, scratch_types=tuple(sems),
               compiler_params=pltpu.CompilerParams(collective_id=collective_id))
    def launch(*sem_refs):
        x, y, c, _ = _place()
        barrier = pltpu.get_barrier_semaphore()
        for dx, dy, dc in peers:
            pl.semaphore_signal(barrier, inc=1, device_id=(x ^ dx, y ^ dy, c ^ dc), device_id_type=MESH)
        pl.semaphore_wait(barrier, len(peers))
        for k, step in enumerate(steps):
            sends, arrivals, own = step.copies(ins if k == 0 else outs, outs, *sem_refs[3 * k:3 * k + 3])
            for cp in own + sends:
                cp.start()
            for cp in arrivals:
                cp.wait_recv()
            for cp in sends:
                cp.wait_send()
            for cp in own:
                cp.wait()

    launch()
    return [ref[...] for ref in outs]


def _chip_exchange_job(sums):
    n = len(sums)

    def copies(ins, outs, send, recv, local):
        del local
        _, _, c, chips = _place()
        sends = [pltpu.make_async_remote_copy(
            src_ref=ins[w].at[_chip_index(*chip)], dst_ref=outs[w].at[j], send_sem=send.at[3 * w + j],
            recv_sem=recv.at[3 * w + j], device_id=(*chip, c), device_id_type=MESH)
            for w in range(n) for j, chip in enumerate(chips)]
        return sends, sends, []

    return _Job(sums, [jax.ShapeDtypeStruct((N_CHIPS - 1,) + s.shape[1:], s.dtype) for s in sums], 3 * n, copies,
                OTHER_CHIPS)


def _chip_sum(name, place, mine, theirs):
    n = len(mine)

    def body(place_ref, *refs):
        del place_ref
        for p_ref, q_ref, o_ref in zip(refs[:n], refs[n:2 * n], refs[2 * n:]):
            acc = p_ref[...].astype(F32)
            for j in range(N_CHIPS - 1):
                acc = acc + q_ref[j].astype(F32)
            o_ref[...] = acc

    def block(p, lead, pick):
        return pl.BlockSpec((lead, p.shape[1] // ROW_STEPS, p.shape[2]), lambda r, place_ref: (pick(place_ref), r, 0))

    return pl.pallas_call(
        body, name=name,
        grid_spec=pltpu.PrefetchScalarGridSpec(
            num_scalar_prefetch=1, grid=(ROW_STEPS,),
            in_specs=[block(p, None, lambda place_ref: place_ref[0]) for p in mine]
            + [block(p, N_CHIPS - 1, lambda place_ref: 0) for p in mine],
            out_specs=[block(p, None, lambda place_ref: place_ref[1]) for p in mine]),
        out_shape=[jax.ShapeDtypeStruct((2,) + p.shape[1:], F32) for p in mine],
        compiler_params=_params(),
    )(place, *mine, *theirs)


def _share_job(bufs):
    n = len(bufs)

    def copies(ins, outs, send, recv, local):
        del ins, local
        x, y, c, _ = _place()

        def copy(w, half):
            return pltpu.make_async_remote_copy(
                src_ref=outs[w].at[half], dst_ref=outs[w].at[half], send_sem=send.at[w], recv_sem=recv.at[w],
                device_id=(x, y, 1 - c), device_id_type=MESH)

        return [copy(w, c) for w in range(n)], [copy(w, 1 - c) for w in range(n)], []

    return _Job(bufs, [jax.ShapeDtypeStruct(b.shape, b.dtype) for b in bufs], n, copies, SIBLING,
                aliases={w: w for w in range(n)})


SMALL_ROWS = 24
ROW_G1, ROW_CW, ROW_CB, ROW_BR, ROW_BI, ROW_LAM, ROW_LG, ROW_LB, ROW_G2, ROW_G3, ROW_LOSS, ROW_BS = (
    0, 1, 5, 6, 7, 8, 9, 10, 11, 12, 13, 16)
N_DEV = 8


def _pack_small(dcw, dcb, dbr, dbi, dlam, dlg, dlb, dg2, dg3, loss, dbs):
    def body(dcw_ref, dcb_ref, dbr_ref, dbi_ref, dlam_ref, dlg_ref, dlb_ref, dg2_ref, dg3_ref, loss_ref, dbs_ref, out):
        out[...] = jnp.zeros((SMALL_ROWS, D_MODEL), F32)
        for row, ref in ((ROW_CB, dcb_ref), (ROW_BR, dbr_ref), (ROW_BI, dbi_ref), (ROW_LAM, dlam_ref),
                         (ROW_LG, dlg_ref), (ROW_LB, dlb_ref), (ROW_G2, dg2_ref), (ROW_G3, dg3_ref)):
            out[row:row + 1, :] = ref[...]
        out[ROW_CW:ROW_CW + CONV_WIDTH, :] = dcw_ref[0:CONV_WIDTH, :]
        out[ROW_LOSS:ROW_LOSS + 1, 0:128] = loss_ref[0:1, :]
        out[ROW_BS:ROW_BS + GROUPS, 0:128] = jnp.transpose(dbs_ref[...])[0:GROUPS, :]

    vm = pl.BlockSpec(memory_space=pltpu.VMEM)
    return pl.pallas_call(
        body, name="pack_small", in_specs=[vm] * 11, out_specs=vm,
        out_shape=jax.ShapeDtypeStruct((SMALL_ROWS, D_MODEL), F32),
    )(dcw, dcb, dbr, dbi, dlam, dlg, dlb, dg2, dg3, loss, dbs)


def _gather_all_job(blocks):
    n = len(blocks)
    flips = [(dx, dy, dc) for dx in (0, 1) for dy in (0, 1) for dc in (0, 1)][1:]

    def copies(ins, outs, send, recv, local):
        x, y, c, _ = _place()
        me = 4 * x + 2 * y + c
        sends, arrivals, own = [], [], []
        for w in range(n):
            own.append(pltpu.make_async_copy(ins[w], outs[w].at[me], local.at[w]))
            for k, (dx, dy, dc) in enumerate(flips):
                peer = (x ^ dx, y ^ dy, c ^ dc)
                sem = dict(send_sem=send.at[7 * w + k], recv_sem=recv.at[7 * w + k])
                sends.append(pltpu.make_async_remote_copy(
                    src_ref=ins[w], dst_ref=outs[w].at[me], device_id=peer, device_id_type=MESH, **sem))
                arrivals.append(pltpu.make_async_remote_copy(
                    src_ref=ins[w], dst_ref=outs[w].at[4 * peer[0] + 2 * peer[1] + peer[2]], device_id=peer,
                    device_id_type=MESH, **sem))
        return sends, arrivals, own

    return _Job(blocks, [jax.ShapeDtypeStruct((N_DEV,) + b.shape, b.dtype) for b in blocks], 7 * n, copies,
                OTHER_CHIPS + SIBLING + tuple((dx, dy, 1) for dx, dy, _ in OTHER_CHIPS), n_local=n)


def _sum_small(vec_all, ws_all, dg1_all):
    def body(vec_ref, ws_ref, dg1_ref, vec_out, ws_out):
        vec, ws, dg1 = vec_ref[0], ws_ref[0], dg1_ref[0]
        for d in range(1, N_DEV):
            vec, ws, dg1 = vec + vec_ref[d], ws + ws_ref[d], dg1 + dg1_ref[d]
        vec_out[...] = vec
        vec_out[ROW_G1:ROW_G1 + 1, :] = dg1
        ws_out[...] = ws

    vm = pl.BlockSpec(memory_space=pltpu.VMEM)
    return pl.pallas_call(
        body, name="sum_small", in_specs=[vm] * 3, out_specs=[vm, vm],
        out_shape=[jax.ShapeDtypeStruct(vec_all.shape[1:], F32), jax.ShapeDtypeStruct(ws_all.shape[1:], F32)],
    )(vec_all, ws_all, dg1_all)


def _adamw_math(w, g, m, v):
    m = ADAM_B1 * m + (1.0 - ADAM_B1) * g
    v = ADAM_B2 * v + (1.0 - ADAM_B2) * (g * g)
    m_hat = m / (1.0 - ADAM_B1 ** ADAM_STEP)
    v_hat = v / (1.0 - ADAM_B2 ** ADAM_STEP)
    delta = (-ADAM_LR) * (m_hat / (jnp.sqrt(v_hat) + ADAM_EPS) + ADAM_WD * w)
    return delta, m, v


def _adamw(name, gs, ws, ms, vs):
    n = len(ws)

    def body(*refs):
        ins, outs = refs[:4 * n], refs[4 * n:]
        for p in range(n):
            g_ref, w_ref, m_ref, v_ref = ins[p::n]
            g = g_ref[...]
            outs[4 * p][...] = g
            outs[4 * p + 1][...], outs[4 * p + 2][...], outs[4 * p + 3][...] = _adamw_math(
                w_ref[...], g, m_ref[...], v_ref[...])

    blocks = [pl.BlockSpec((w.shape[0] // ROW_STEPS, w.shape[1]), lambda r: (r, 0)) for w in ws]
    out = pl.pallas_call(
        body, name=name, grid=(ROW_STEPS,), in_specs=blocks * 4, out_specs=[b for b in blocks for _ in range(4)],
        out_shape=[jax.ShapeDtypeStruct(w.shape, F32) for w in ws for _ in range(4)], compiler_params=_params(),
    )(*gs, *ws, *ms, *vs)
    return [tuple(out[4 * p:4 * p + 4]) for p in range(n)]


def _adamw_small(grads, ws, ms, vs):
    n = len(grads)

    def body(*refs):
        g_refs, w_refs, m_refs, v_refs = refs[:n], refs[n:2 * n], refs[2 * n:3 * n], refs[3 * n:4 * n]
        outs = refs[4 * n:]
        for p in range(n):
            d, nm, nv = _adamw_math(w_refs[p][...], g_refs[p][...], m_refs[p][...], v_refs[p][...])
            outs[p][...] = d
            outs[n + p][...] = nm
            outs[2 * n + p][...] = nv

    vm = pl.BlockSpec(memory_space=pltpu.VMEM)
    shapes = [jax.ShapeDtypeStruct(w.shape, F32) for w in ws]
    out = pl.pallas_call(
        body, name="adamw_small", in_specs=[vm] * (4 * n), out_specs=[vm] * (3 * n), out_shape=shapes * 3,
    )(*grads, *ws, *ms, *vs)
    return out[:n], out[n:2 * n], out[2 * n:]


def _unstack_heads(w_st):
    per = HEAD_DIM // N_CHIPS
    return w_st.reshape(N_CHIPS, HEADS, per, HEAD_DIM).transpose(1, 0, 2, 3).reshape(HEADS, HEAD_DIM, HEAD_DIM)


def _stack_heads(w):
    per = HEAD_DIM // N_CHIPS
    return w.reshape(HEADS, N_CHIPS, per, HEAD_DIM).transpose(1, 0, 2, 3).reshape(N_CHIPS, HEADS * per, HEAD_DIM)


def kernel(x, norm_mix_g, w_in, conv_w, conv_b, w_rgate, b_rgate, w_igate, b_igate, lru_lambda, w_out_a, sgu_ln_g, sgu_ln_b, sgu_w_s, sgu_b_s, w_out_b, w_out, norm_mlp_g, w_up, w_down, norm_final_g, loss_target, m_norm_mix_g, m_w_in, m_conv_w, m_conv_b, m_w_rgate, m_b_rgate, m_w_igate, m_b_igate, m_lru_lambda, m_w_out_a, m_sgu_ln_g, m_sgu_ln_b, m_sgu_w_s, m_sgu_b_s, m_w_out_b, m_w_out, m_norm_mlp_g, m_w_up, m_w_down, m_norm_final_g, v_norm_mix_g, v_w_in, v_conv_w, v_conv_b, v_w_rgate, v_b_rgate, v_w_igate, v_b_igate, v_lru_lambda, v_w_out_a, v_sgu_ln_g, v_sgu_ln_b, v_sgu_w_s, v_sgu_b_s, v_w_out_b, v_w_out, v_norm_mlp_g, v_w_up, v_w_down, v_norm_final_g):
    chip = _chip_index(lax.axis_index("x"), lax.axis_index("y"))
    core = lax.axis_index("c")
    quarter_h = HEAD_DIM // N_CHIPS
    quarter_d = D_MODEL // N_CHIPS

    as_2d = lambda a: a.reshape(-1, a.shape[-1])
    big_w = [as_2d(w) for w in (w_in, w_rgate, w_igate, w_out_a, w_out_b, w_out, w_up, w_down)]
    big_m = [as_2d(w) for w in (m_w_in, m_w_rgate, m_w_igate, m_w_out_a, m_w_out_b, m_w_out, m_w_up, m_w_down)]
    big_v = [as_2d(w) for w in (v_w_in, v_w_rgate, v_w_igate, v_w_out_a, v_w_out_b, v_w_out, v_w_up, v_w_down)]

    packed = jnp.concatenate([conv_w[0], b_rgate[0], b_igate[0]], axis=1)
    packed = jnp.concatenate([packed, jnp.zeros_like(packed)], axis=0)
    s_in, s_r, s_i, s_oa, s_ob, s_out, s_up, s_down = [w.astype(BF16) for w in big_w]
    xs, target = x[0], loss_target[0]
    g3 = norm_final_g.reshape(1, D_MODEL)
    bias_s = jnp.broadcast_to(jnp.transpose(sgu_b_s[0])[:, :, None], (CHUNK, GROUPS, GROUP_DIM)).reshape(CHUNK, D_MODEL)
    core_arr = core.reshape(1).astype(jnp.int32)
    place = jnp.stack([chip, core]).astype(jnp.int32)
    quarter = lambda g: g.reshape(N_CHIPS, D_MODEL // N_CHIPS, D_MODEL)

    def pair_add(nm, grads, from_sibling):
        halves = [g.reshape(N_CHIPS, 2, g.shape[1] // 2, g.shape[2]) for g in grads]
        return list(_pair_add("pair_add_" + nm, core_arr, halves, from_sibling))

    def chip_sum(nm, pairs, from_chips):
        return list(_chip_sum("chip_sum_" + nm, place, pairs, from_chips))

    order = jnp.stack([chip, chip ^ 2, chip ^ 1, chip ^ 3]).astype(jnp.int32)
    (z, n1, (w_in_st, wr_st, wi_st)), ((packed_all,), late) = _fwd_in(
        xs, norm_mix_g, [s_in, s_r, s_i], order,
        jobs=[_gather_small_job(packed), _gather_near_job([s_oa, s_ob, s_out])])
    pick = lambda lo, hi: packed_all[:, :HEADS, lo:hi].transpose(1, 0, 2).reshape(HEADS, -1)
    conv_w_full = pick(0, quarter_d)
    br_full = pick(quarter_d, quarter_d + quarter_h).reshape(1, D_MODEL)
    bi_full = pick(quarter_d + quarter_h, quarter_d + 2 * quarter_h).reshape(1, D_MODEL)
    wr, wi = _unstack_heads(wr_st), _unstack_heads(wi_st)
    lru = (conv_w_full, conv_b, wr, br_full, wi, bi_full, lru_lambda)
    sgu = (sgu_ln_g, sgu_ln_b, sgu_w_s[0], bias_s)

    after = lambda arrays, result: lax.optimization_barrier((arrays, result))[0]
    w_up_st, w_dn = _sequencer_call(
        "gather_mlp", 8, _gather_direct_job(after([s_up, s_down], n1)).then(_gather_pass_job))
    w_dn = w_dn.reshape(D_FF, D_MODEL)
    late_step = (3 * (xs.shape[0] // SEQ_TILE) // 4,)
    (ya, *saved), (late,) = _fwd_lru(z, *lru, jobs=[_gather_far_job(late).then(_gather_pass_job, at=late_step)])
    w_oa, w_ob, w_o = [w.reshape(D_MODEL, D_MODEL) for w in late]
    (yb, pa, pb, h1, n2), _ = _fwd_sgu_merge(ya, z, xs, *sgu, w_oa, w_ob, w_o, norm_mlp_g)
    (act, dup, dh2b, dh1, loss_part, dg3, dg2), _ = _mlp(n2, h1, target, w_up_st, w_dn, norm_mlp_g, g3)

    d_down, _ = _weight_grad("dw_down", act, dh2b, N_CHIPS, True, False, D_MODEL)
    r_down, = _sequencer_call("send_w_down", 10, _pair_send_job([d_down]))
    d_up, _ = _weight_grad("dw_up", n2, dup, N_CHIPS, False, True, D_MODEL)
    r_up, = _sequencer_call("send_w_up", 11, _pair_send_job([d_up]))
    (p_down,), (p_up,) = pair_add("w_down", [d_down], [r_down]), pair_add("w_up", [d_up], [r_up])
    (dz, merged, dpa, dpb, dh1b, dlg, dlb, dws, dbs, dcw, dcb, dwr, dbr, dwi, dbi, dlam), ((q_up, q_down),) = _bwd_mix(
        dh1, pa, pb, z, *saved, w_oa, w_ob, w_o, *sgu, conv_w_full, wr, wi, lru_lambda,
        jobs=[_chip_exchange_job([p_up, p_down])])
    names = ("w_in", "w_rgate", "w_igate", "w_out_a", "w_out_b", "w_out", "w_up", "w_down")
    (d_out, d_oa, d_ob), _ = _weight_grads_square("dw_projections", [(merged, dh1b), (ya, dpa), (yb, dpb)])
    mids = [quarter(d_oa), quarter(d_ob), quarter(d_out)]
    r_mids = _sequencer_call("send_mids", 1, _pair_send_job(mids))
    half_up, half_down = chip_sum("mlp", [p_up, p_down], after([q_up, q_down], mids))
    gates = [_stack_heads(dwr).astype(BF16), _stack_heads(dwi).astype(BF16)]
    small = _pack_small(dcw, dcb, dbr, dbi, dlam, dlg, dlb, dg2, dg3, loss_part, dbs)
    p_mids = pair_add("projections", after(mids, [half_up, half_down]), r_mids)
    q_mids = _sequencer_call("exchange_mids", 2, _chip_exchange_job(p_mids))
    d_in, (r_gates, (vec_all, ws_all), (full_up, full_down)) = _weight_grad(
        "dw_in", n1, after(dz, p_mids), N_CHIPS, False, True, IN_SHARD,
        jobs=[_pair_send_job(gates), _gather_all_job([small, dws]), _share_job([half_up, half_down])])
    r_in, = _sequencer_call("send_w_in", 3, _pair_send_job(after([d_in], q_mids)))
    adam_args = {nm: (w, m, v) for nm, w, m, v in zip(names, big_w, big_m, big_v)}

    def adamw(group, nms, grads):
        given = [adam_args[nm] for nm in nms]
        grads = [g.reshape(w.shape) for g, (w, _, _) in zip(grads, given)]
        outs = _adamw("adamw_" + group, grads, *[[a[q] for a in given] for q in range(3)])
        return {nm: (out[0], out[1:]) for nm, out in zip(nms, outs)}

    p_gates = pair_add("gates", gates, r_gates)
    half_mids = chip_sum("projections", p_mids, q_mids)
    full_mids = _sequencer_call("share_mids", 12, _share_job(half_mids))
    p_first = pair_add("w_in", after([d_in], half_mids), [r_in]) + p_gates
    q_first = _sequencer_call("exchange_w_in", 4, _chip_exchange_job(p_first))
    (grad_x, dg1), _ = _bwd_in(dz, xs, dh1, w_in_st, norm_mix_g)
    dg1_all, = _sequencer_call("gather_dg1", 6, _gather_all_job([dg1]))
    done = adamw("mlp", ("w_up", "w_down"), [full_up, full_down])
    q_first = after(q_first, [out[0] for _, out in done.values()])
    half_first = chip_sum("first", p_first, q_first)
    full_first = _sequencer_call("share_last", 5, _share_job(half_first))
    done.update(adamw("projections", names[3:6], after(full_mids, half_first)))
    done.update(adamw("first", names[:3], full_first))
    full, big_out = [done[nm][0] for nm in names], [done[nm][1] for nm in names]

    vec, ws_sum = _sum_small(vec_all, ws_all, dg1_all)
    row = lambda r: vec[r:r + 1]
    shard = lambda a, width: lax.dynamic_slice_in_dim(a, chip * width, width, axis=1)
    g_small = dict(
        norm_mix_g=row(ROW_G1), conv_w=shard(vec[ROW_CW:ROW_CW + CONV_WIDTH], quarter_d), conv_b=row(ROW_CB),
        b_rgate=shard(row(ROW_BR).reshape(HEADS, HEAD_DIM), quarter_h),
        b_igate=shard(row(ROW_BI).reshape(HEADS, HEAD_DIM), quarter_h), lru_lambda=row(ROW_LAM),
        sgu_ln_g=row(ROW_LG), sgu_ln_b=row(ROW_LB),
        sgu_w_s=ws_sum.reshape(CHUNK, GROUPS, CHUNK).transpose(1, 0, 2).reshape(GROUPS * CHUNK, CHUNK),
        sgu_b_s=vec[ROW_BS:ROW_BS + GROUPS, 0:CHUNK], norm_mlp_g=row(ROW_G2), norm_final_g=row(ROW_G3))
    loss = vec[ROW_LOSS, 0]
    small_names = list(g_small)
    given = dict(
        norm_mix_g=(norm_mix_g, m_norm_mix_g, v_norm_mix_g), conv_w=(conv_w, m_conv_w, v_conv_w),
        conv_b=(conv_b, m_conv_b, v_conv_b), b_rgate=(b_rgate, m_b_rgate, v_b_rgate),
        b_igate=(b_igate, m_b_igate, v_b_igate), lru_lambda=(lru_lambda, m_lru_lambda, v_lru_lambda),
        sgu_ln_g=(sgu_ln_g, m_sgu_ln_g, v_sgu_ln_g), sgu_ln_b=(sgu_ln_b, m_sgu_ln_b, v_sgu_ln_b),
        sgu_w_s=(sgu_w_s, m_sgu_w_s, v_sgu_w_s), sgu_b_s=(sgu_b_s, m_sgu_b_s, v_sgu_b_s),
        norm_mlp_g=(norm_mlp_g, m_norm_mlp_g, v_norm_mlp_g), norm_final_g=(norm_final_g, m_norm_final_g, v_norm_final_g))
    g2d = [g_small[nm] for nm in small_names]
    to2d = lambda a, g: a.reshape(g.shape)
    d_s, m_s, v_s = _adamw_small(
        g2d, *[[to2d(given[nm][q], g) for nm, g in zip(small_names, g2d)] for q in range(3)])

    shapes = dict(
        norm_mix_g=norm_mix_g, w_in=w_in, conv_w=conv_w, conv_b=conv_b, w_rgate=w_rgate, b_rgate=b_rgate,
        w_igate=w_igate, b_igate=b_igate, lru_lambda=lru_lambda, w_out_a=w_out_a, sgu_ln_g=sgu_ln_g,
        sgu_ln_b=sgu_ln_b, sgu_w_s=sgu_w_s, sgu_b_s=sgu_b_s, w_out_b=w_out_b, w_out=w_out, norm_mlp_g=norm_mlp_g,
        w_up=w_up, w_down=w_down, norm_final_g=norm_final_g)
    grads, deltas, new_m, new_v = {}, {}, {}, {}
    for nm, g, (d, nmom, nvar) in zip(names, full, big_out):
        grads[nm], deltas[nm], new_m[nm], new_v[nm] = g, d, nmom, nvar
    for p, nm in enumerate(small_names):
        grads[nm], deltas[nm], new_m[nm], new_v[nm] = g2d[p], d_s[p], m_s[p], v_s[p]
    order = list(shapes)
    out = [loss, grad_x[None]]
    for group in (grads, deltas, new_m, new_v):
        out += [group[nm].reshape(shapes[nm].shape) for nm in order]
    return tuple(out)
```

```python
import functools

import jax
import jax.numpy as jnp
from jax import lax
from jax.experimental import pallas as pl
from jax.experimental.pallas import tpu as pltpu
from jax.experimental.pallas import tpu_sc as plsc

F32 = jnp.float32
BF16 = jnp.bfloat16
MESH = pl.DeviceIdType.MESH

D_MODEL = 1024
D_IN = 6 * D_MODEL
D_FF = 4 * D_MODEL
N_CHIPS = 4
IN_SHARD = D_IN // N_CHIPS
HEADS = 4
HEAD_DIM = D_MODEL // HEADS
GROUPS = 4
GROUP_DIM = D_MODEL // GROUPS
CHUNK = 128
CONV_WIDTH = 4
LRU_C = 8.0
NORM_EPS = 1e-6
LN_EPS = 1e-5

ADAM_LR = 0.001
ADAM_B1 = 0.9
ADAM_B2 = 0.999
ADAM_EPS = 1e-08
ADAM_WD = 0.01
ADAM_STEP = 10

SUBLANES = 8
LANES = 128
MM_TILE = 512
IN_TILE = 1024
SEQ_TILE = 256
DW_TILE = 2048
VMEM_LIMIT_BYTES = 56 * 1024 * 1024

GELU_K0 = 0.7978845608028654
GELU_K1 = 0.044715


def _params(n_grid_axes=1):
    return pltpu.CompilerParams(
        dimension_semantics=("arbitrary",) * n_grid_axes, vmem_limit_bytes=VMEM_LIMIT_BYTES)


def _resident(shape):
    nd = len(shape)
    return pl.BlockSpec(shape, lambda *_: (0,) * nd, pipeline_mode=pl.Buffered(1))


def _const(shape):
    nd = len(shape)
    return pl.BlockSpec(shape, lambda *_: (0,) * nd)


def _dot(a, b):
    return jnp.dot(a, b, preferred_element_type=F32)


def _dot_nt(a, b):
    return lax.dot_general(a, b, (((1,), (1,)), ((), ())), preferred_element_type=F32)


def _dot_tn(a, b):
    return lax.dot_general(a, b, (((0,), (0,)), ((), ())), preferred_element_type=F32)


def _gelu(x):
    t = jnp.tanh(x * (GELU_K0 + (GELU_K0 * GELU_K1) * (x * x)))
    return x * (0.5 + 0.5 * t)


def _gelu_and_grad(x):
    x2 = x * x
    t = jnp.tanh(x * (GELU_K0 + (GELU_K0 * GELU_K1) * x2))
    s = 0.5 + 0.5 * t
    dg = s + (x * (1.0 - t * t)) * (0.5 * GELU_K0 + (1.5 * GELU_K0 * GELU_K1) * x2)
    return x * s, dg


def _gate(x):
    return 0.5 + 0.5 * jnp.tanh(0.5 * x.astype(F32))


def _rms(x):
    r = lax.rsqrt(jnp.mean(x * x, axis=-1, keepdims=True) + NORM_EPS)
    return x * r, r


def _rms_bwd(dn, xhat, r):
    return r * (dn - xhat * jnp.mean(dn * xhat, axis=-1, keepdims=True))


def _col_sum(v):
    return jnp.sum(v, axis=0, keepdims=True)


def _shift_down(x, tail8, k):
    xs = pltpu.roll(x, k, 0)
    ts = pltpu.roll(tail8, k, 0)
    ridx = lax.broadcasted_iota(jnp.int32, tail8.shape, 0)
    head = jnp.where(ridx < k, ts, xs[0:SUBLANES])
    return jnp.concatenate([head, xs[SUBLANES:]], axis=0)


def _shift_up(x, head8, k):
    n = x.shape[0]
    xs = pltpu.roll(x, n - k, 0)
    hs = pltpu.roll(head8, SUBLANES - k, 0)
    ridx = lax.broadcasted_iota(jnp.int32, head8.shape, 0)
    last = jnp.where(ridx >= SUBLANES - k, hs, xs[n - SUBLANES:n])
    return jnp.concatenate([xs[:n - SUBLANES], last], axis=0)


def _scan_scratch(rows):
    return [pltpu.VMEM((D_MODEL // LANES, rows, LANES), F32)] * 3


def _scan(a, b, carry, scratch, reverse):
    a_scr, b_scr, h_scr = scratch
    n, cols = a.shape
    length = n // SUBLANES
    steps = range(length - 1, -1, -1) if reverse else range(length)
    stripes = range(SUBLANES - 1, -1, -1) if reverse else range(SUBLANES)
    out, leaving = [], []
    for blk in range(cols // LANES):
        lanes = slice(blk * LANES, (blk + 1) * LANES)
        a_v, b_v, h_v = a_scr.at[blk], b_scr.at[blk], h_scr.at[blk]
        a_v[...] = a[:, lanes]
        b_v[...] = b[:, lanes]
        h = jnp.zeros((SUBLANES, LANES), F32)
        prod = jnp.ones((SUBLANES, LANES), F32)
        for i in steps:
            rows = pl.ds(i, SUBLANES, stride=length)
            a_i = a_v[rows, :]
            h = a_i * h + b_v[rows, :]
            prod = a_i * prod
            h_v[rows, :] = h
            a_v[rows, :] = prod
        enter = carry[:, lanes]
        entering = [None] * SUBLANES
        for j in stripes:
            entering[j] = enter
            enter = prod[j:j + 1] * enter + h[j:j + 1]
        entering = jnp.concatenate(entering, axis=0)
        for i in steps:
            rows = pl.ds(i, SUBLANES, stride=length)
            h_v[rows, :] = h_v[rows, :] + a_v[rows, :] * entering
        out.append(h_v[...])
        leaving.append(enter)
    return jnp.concatenate(out, axis=1), jnp.concatenate(leaving, axis=1)


def _softplus_neg(lam):
    e = jnp.exp(-jnp.abs(lam))
    u = 1.0 + e
    log1p_e = jnp.where(u == 1.0, e, jnp.log(u) * (e / jnp.where(u == 1.0, 1.0, u - 1.0)))
    return jnp.maximum(-lam, 0.0) + log1p_e


def _lru_gates(xa, tail8, cw_ref, cb_ref, wr_ref, br_ref, wi_ref, bi_ref, lam_ref):
    cw = cw_ref[...]
    xc = cb_ref[...] + cw[0:1] * xa
    for k in range(1, CONV_WIDTH):
        xc = xc + cw[k:k + 1] * _shift_down(xa, tail8, k)
    xcb = xc.astype(BF16)
    pre_r, pre_i = [], []
    for h in range(HEADS):
        cols = slice(h * HEAD_DIM, (h + 1) * HEAD_DIM)
        pre_r.append(_dot(xcb[:, cols], wr_ref[h]))
        pre_i.append(_dot(xcb[:, cols], wi_ref[h]))
    r = jax.nn.sigmoid(jnp.concatenate(pre_r, axis=1) + br_ref[...])
    ig = jax.nn.sigmoid(jnp.concatenate(pre_i, axis=1) + bi_ref[...])
    _, a, mult, _ = _decay(r, lam_ref)
    return xc, r, ig, a, mult


def _decay(r, lam_ref):
    sp = _softplus_neg(lam_ref[...])
    log_a = ((-LRU_C) * sp) * r
    a = jnp.exp(log_a)
    th = jnp.tanh(log_a)
    q = (-2.0 * th) / (1.0 - th)
    inv = lax.rsqrt(q)
    return sp, a, jnp.where(q > 0.0, q * inv, 0.0), inv


class _Phase:
    def __init__(self, copies, n_sem, n_local, start=None, finish=None):
        self.copies, self.n_sem, self.n_local, self.start, self.finish = copies, n_sem, n_local, start, finish


class _Job:
    def __init__(self, inputs, out_shape, n_sem, copies, peers, aliases=None, n_local=0):
        self.inputs, self.out_shape = list(inputs), list(out_shape)
        self.aliases = dict(aliases or {})
        self.phases = [_Phase(copies, n_sem, n_local)]
        self.peers = tuple(peers)

    def then(self, make, at=None):
        nxt = make(self.out_shape)
        self.phases[-1].finish = at
        nxt.phases[0].start = at
        self.phases += nxt.phases
        self.peers = tuple(sorted(set(self.peers + nxt.peers)))
        return self


def _fused_call(body, jobs, *, name, grid, in_specs, out_specs, out_shape, scratch_shapes=(),
                input_output_aliases=None, compiler_params=None, n_prefetch=0, jobs_start_after=None):
    single = not isinstance(out_shape, (list, tuple))
    out_specs = [out_specs] if single else list(out_specs)
    out_shape = [out_shape] if single else list(out_shape)
    n_scr = len(scratch_shapes)
    in_specs, scratch_shapes = list(in_specs), list(scratch_shapes)
    n_in, n_out = len(in_specs), len(out_shape)
    aliases = dict(input_output_aliases or {})
    in_at, out_at, phases = [], [], []
    for q, job in enumerate(jobs):
        in_at.append(len(in_specs))
        out_at.append(len(out_shape))
        for i, o in job.aliases.items():
            aliases[n_prefetch + len(in_specs) + i] = len(out_shape) + o
        in_specs += [ANY] * len(job.inputs)
        out_specs += [ANY] * len(job.out_shape)
        out_shape += job.out_shape
        for k, phase in enumerate(job.phases):
            phases.append((q, k, phase, len(scratch_shapes)))
            scratch_shapes += [pltpu.SemaphoreType.DMA((phase.n_sem,)), pltpu.SemaphoreType.DMA((phase.n_sem,)),
                               pltpu.SemaphoreType.DMA((max(phase.n_local, 1),))]
    n_in_all, n_out_all = len(in_specs), len(out_shape)
    first_step, last_step = (0,) * len(grid), tuple(g - 1 for g in grid)

    def full_body(*refs):
        prefetch, refs = refs[:n_prefetch], refs[n_prefetch:]
        ins, outs, scr = refs[:n_in_all], refs[n_in_all:n_in_all + n_out_all], refs[n_in_all + n_out_all:]
        ids = [pl.program_id(a) for a in range(len(grid))]
        at_step = lambda step: functools.reduce(jnp.logical_and, [i == k for i, k in zip(ids, step)])

        def copies(q, k, phase, sem_at):
            job = jobs[q]
            mine = outs[out_at[q]:out_at[q] + len(job.out_shape)]
            return phase.copies(ins[in_at[q]:in_at[q] + len(job.inputs)] if k == 0 else mine, mine,
                                *scr[sem_at:sem_at + 3])

        def start(*phase):
            def go():
                sends, _, local = copies(*phase)
                for cp in local + sends:
                    cp.start()
            return go

        def finish(*phase):
            def go():
                sends, arrivals, local = copies(*phase)
                for cp in arrivals:
                    cp.wait_recv()
                for cp in sends:
                    cp.wait_send()
                for cp in local:
                    cp.wait()
            return go

        for phase in phases:
            if phase[2].start is None and jobs_start_after is None:
                pl.when(at_step(first_step))(start(*phase))
        body(*prefetch, *ins[:n_in], *outs[:n_out], *scr[:n_scr])
        for phase in phases:
            pl.when(at_step(phase[2].finish or last_step))(finish(*phase))
            nxt = phase[2].start or jobs_start_after
            if nxt is not None:
                pl.when(at_step(nxt))(start(*phase))

    if n_prefetch:
        layout = dict(grid_spec=pltpu.PrefetchScalarGridSpec(
            num_scalar_prefetch=n_prefetch, grid=grid, in_specs=in_specs, out_specs=out_specs,
            scratch_shapes=scratch_shapes))
    else:
        layout = dict(grid=grid, in_specs=in_specs, out_specs=out_specs, scratch_shapes=scratch_shapes)
    call = pl.pallas_call(
        full_body, name=name, out_shape=out_shape, input_output_aliases=aliases, compiler_params=compiler_params,
        **layout)

    def run(*args):
        res = call(*args, *[a for job in jobs for a in job.inputs])
        mine = res[0] if single else list(res[:n_out])
        return mine, [list(res[at:at + len(job.out_shape)]) for at, job in zip(out_at, jobs)]

    return run


def _fwd_in(x, g1, shards, order, jobs=()):
    t = x.shape[0]
    rows_per_step = min(IN_TILE, t)
    n_tiles = t // rows_per_step
    n = len(shards)
    halves = [s.shape[0] // 2 for s in shards]

    def body(order_ref, x_ref, g_ref, *refs):
        del order_ref
        ins, (z_ref, n_ref), outs = refs[:n], refs[n:n + 2], refs[n + 2:2 * n + 2]
        wbuf, nbuf, send, recv, local = refs[2 * n + 2:]
        s, i = pl.program_id(0), pl.program_id(1)
        x_, y_, c, chips = _place()
        near, far = _near_far(x_, y_, c)
        k_me = _chip_index(x_, y_)

        def block(w, chip, pc):
            return outs[w].at[_chip_index(*chip), pl.ds(pc * halves[w], halves[w]), :]

        def over_ici(w, j, landing):
            return pltpu.make_async_remote_copy(
                src_ref=ins[w].at[pl.ds(c * halves[w], halves[w]), :],
                dst_ref=block(w, chips[j] if landing else (x_, y_), c), send_sem=send.at[6 * w + j],
                recv_sem=recv.at[6 * w + j], device_id=(*chips[j], c), device_id_type=MESH)

        def onward(w, landing):
            blk = block(w, chips[2] if landing else near, c)
            return pltpu.make_async_remote_copy(
                src_ref=blk, dst_ref=blk, send_sem=send.at[6 * w + 2], recv_sem=recv.at[6 * w + 2],
                device_id=(*far, c), device_id_type=MESH)

        def to_sibling(w, j, landing):
            blk = block(w, chips[j], 1 - c if landing else c)
            return pltpu.make_async_remote_copy(
                src_ref=blk, dst_ref=blk, send_sem=send.at[6 * w + 3 + j], recv_sem=recv.at[6 * w + 3 + j],
                device_id=(x_, y_, 1 - c), device_id_type=MESH)

        own = [pltpu.make_async_copy(wbuf, outs[0].at[k_me], local.at[0])]
        own += [pltpu.make_async_copy(ins[w], outs[w].at[k_me], local.at[w]) for w in range(1, n)]

        @pl.when((s == 0) & (i == 0))
        def _():
            for j in range(2):
                for w in range(n):
                    over_ici(w, j, False).start()
            load = pltpu.make_async_copy(ins[0], wbuf, local.at[n])
            load.start()
            load.wait()
            for cp in own:
                cp.start()

        for j in range(N_CHIPS - 1):
            @pl.when((s == j + 1) & (i == 0))
            def _(j=j):
                if j == 0:
                    for k in range(2):
                        for w in range(n):
                            over_ici(w, k, True).wait_recv()
                    for w in range(n):
                        onward(w, False).start()
                    for k in range(2):
                        for w in range(n):
                            to_sibling(w, k, False).start()
                    own[0].wait()
                if j == 2:
                    for w in range(n):
                        onward(w, True).wait_recv()
                    for w in range(n):
                        to_sibling(w, j, False).start()
                for w in range(n):
                    to_sibling(w, j, True).wait_recv()
                load = pltpu.make_async_copy(outs[0].at[_chip_index(*chips[j])], wbuf, local.at[n])
                load.start()
                load.wait()

        rows = pl.ds(pl.multiple_of(i * rows_per_step, rows_per_step), rows_per_step)

        @pl.when(s == 0)
        def _():
            xhat, _ = _rms(x_ref[...])
            nrm = (xhat * g_ref[...]).astype(BF16)
            nbuf[rows, :] = nrm
            n_ref[...] = nrm

        z_ref[...] = _dot(nbuf[rows, :], wbuf[...]).astype(BF16)

        @pl.when((s == N_CHIPS - 1) & (i == n_tiles - 1))
        def _():
            for j in range(N_CHIPS - 1):
                for w in range(n):
                    (over_ici(w, j, False) if j < 2 else onward(w, False)).wait_send()
                    to_sibling(w, j, False).wait_send()
            for cp in own[1:]:
                cp.wait()

    once = lambda s, i, order: (jnp.where(s == 0, i, n_tiles - 1), 0)
    (z, n1, *stacked), job_outs = _fused_call(
        body, jobs, name="fwd_in", grid=(N_CHIPS, n_tiles), n_prefetch=1,
        in_specs=[pl.BlockSpec((rows_per_step, D_MODEL), once), _const((1, D_MODEL))] + [ANY] * n,
        out_specs=[pl.BlockSpec((rows_per_step, IN_SHARD), lambda s, i, order: (i, order[s])),
                   pl.BlockSpec((rows_per_step, D_MODEL), once)] + [ANY] * n,
        out_shape=[jax.ShapeDtypeStruct((t, D_IN), BF16), jax.ShapeDtypeStruct((t, D_MODEL), BF16)]
        + [jax.ShapeDtypeStruct((N_CHIPS,) + s.shape, s.dtype) for s in shards],
        scratch_shapes=[pltpu.VMEM(shards[0].shape, BF16), pltpu.VMEM((t, D_MODEL), BF16),
                        pltpu.SemaphoreType.DMA((6 * n,)),
                        pltpu.SemaphoreType.DMA((6 * n,)), pltpu.SemaphoreType.DMA((n + 1,))],
        compiler_params=_params(2), jobs_start_after=(1, 0),
    )(order, x, g1, *shards)
    return (z, n1, stacked), job_outs


def _fwd_lru(z, conv_w, conv_b, wr, br, wi, bi, lam, jobs=()):
    t = z.shape[0]

    def body(xa_ref, ga_ref, cw_ref, cb_ref, wr_ref, br_ref, wi_ref, bi_ref, lam_ref, ya_ref, h_ref, xc_ref, r_ref,
             ig_ref, tail_ref, carry_ref, *scan_scratch):
        @pl.when(pl.program_id(0) == 0)
        def _():
            tail_ref[...] = jnp.zeros_like(tail_ref)
            carry_ref[...] = jnp.zeros_like(carry_ref)

        xa = xa_ref[...].astype(F32)
        xc, r, ig, a, mult = _lru_gates(xa, tail_ref[...], cw_ref, cb_ref, wr_ref, br_ref, wi_ref, bi_ref, lam_ref)
        tail_ref[...] = xa[SEQ_TILE - SUBLANES:]
        xc_ref[...], r_ref[...], ig_ref[...] = xc, r, ig
        h, carry = _scan(a, xc * ig * mult, carry_ref[...], scan_scratch, reverse=False)
        carry_ref[...] = carry
        h_ref[...] = h
        ya_ref[...] = (h * _gelu(ga_ref[...].astype(F32))).astype(BF16)

    tile = lambda j: pl.BlockSpec((SEQ_TILE, D_MODEL), lambda i: (i, j))
    return _fused_call(
        body, jobs, name="fwd_lru", grid=(t // SEQ_TILE,),
        in_specs=[tile(0), tile(1), _const((CONV_WIDTH, D_MODEL)), _const((1, D_MODEL)),
                  _resident((HEADS, HEAD_DIM, HEAD_DIM)), _const((1, D_MODEL)),
                  _resident((HEADS, HEAD_DIM, HEAD_DIM)), _const((1, D_MODEL)), _const((1, D_MODEL))],
        out_specs=[tile(0)] * 5,
        out_shape=[jax.ShapeDtypeStruct((t, D_MODEL), BF16)] + [jax.ShapeDtypeStruct((t, D_MODEL), F32)] * 4,
        scratch_shapes=[pltpu.VMEM((SUBLANES, D_MODEL), F32), pltpu.VMEM((1, D_MODEL), F32)] + _scan_scratch(SEQ_TILE),
        compiler_params=_params(),
    )(z, z, conv_w, conv_b, wr, br, wi, bi, lam)


def _sgu_forward_parts(ub, vb, lg_ref, lb_ref):
    u, du = _gelu_and_grad(ub.astype(F32))
    vg, dvg = _gelu_and_grad(vb.astype(F32))
    mu = jnp.mean(vg, axis=-1, keepdims=True)
    d = vg - mu
    rstd = lax.rsqrt(jnp.mean(d * d, axis=-1, keepdims=True) + LN_EPS)
    vhat = d * rstd
    vn = (vhat * lg_ref[...] + lb_ref[...]).astype(BF16)
    return u, du, dvg, rstd, vhat, vn


def _causal_mask():
    rows = lax.broadcasted_iota(jnp.int32, (CHUNK, CHUNK), 0)
    cols = lax.broadcasted_iota(jnp.int32, (CHUNK, CHUNK), 1)
    return rows >= cols


def _fwd_sgu_merge(ya, z, x, ln_g, ln_b, w_s, bias_full, w_oa, w_ob, w_out, g2, jobs=()):
    t = x.shape[0]

    def body(ya_ref, ub_ref, vb_ref, m_ref, x_ref, lg_ref, lb_ref, ws_ref, bias_ref, woa_ref, wob_ref, wout_ref, g_ref,
             yb_ref, pa_ref, pb_ref, h1_ref, n2_ref):
        u, _, _, _, _, vn = _sgu_forward_parts(ub_ref[...], vb_ref[...], lg_ref, lb_ref)
        mask = _causal_mask()
        wm = [jnp.where(mask, ws_ref[g], 0.0).astype(BF16) for g in range(GROUPS)]
        for c in range(SEQ_TILE // CHUNK):
            rows = slice(c * CHUNK, (c + 1) * CHUNK)
            for g in range(GROUPS):
                cols = slice(g * GROUP_DIM, (g + 1) * GROUP_DIM)
                sp = _dot(wm[g], vn[rows, cols]) + bias_ref[:, cols]
                yb_ref[rows, cols] = (u[rows, cols] * sp).astype(BF16)
        pa = _dot(ya_ref[...], woa_ref[...])
        pb = _dot(yb_ref[...], wob_ref[...])
        pa_ref[...] = pa
        pb_ref[...] = pb
        merged = _gate(m_ref[:, :D_MODEL]) * pa + _gate(m_ref[:, D_MODEL:]) * pb
        h1 = x_ref[...] + _dot(merged.astype(BF16), wout_ref[...])
        h1_ref[...] = h1
        xhat, _ = _rms(h1)
        n2_ref[...] = (xhat * g_ref[...]).astype(BF16)

    tile = lambda j: pl.BlockSpec((SEQ_TILE, D_MODEL), lambda i: (i, j))
    sq = _resident((D_MODEL, D_MODEL))
    vec = _const((1, D_MODEL))
    bf, f32 = jax.ShapeDtypeStruct((t, D_MODEL), BF16), jax.ShapeDtypeStruct((t, D_MODEL), F32)
    return _fused_call(
        body, jobs, name="fwd_sgu_merge", grid=(t // SEQ_TILE,),
        in_specs=[tile(0), tile(2), tile(3), pl.BlockSpec((SEQ_TILE, 2 * D_MODEL), lambda i: (i, 2)), tile(0), vec, vec,
                  _const((GROUPS, CHUNK, CHUNK)), _const((CHUNK, D_MODEL)), sq, sq, sq, vec],
        out_specs=[tile(0)] * 5,
        out_shape=[bf, f32, f32, f32, bf],
        compiler_params=_params(),
    )(ya, z, z, z, x, ln_g, ln_b, w_s, bias_full, w_oa, w_ob, w_out, g2)


def _mlp(n2, h1, target, w_up_st, w_down, g2, g3, jobs=()):
    t = n2.shape[0]

    def body(n2_ref, h1_ref, tgt_ref, wup_ref, wdown_ref, g2_ref, g3_ref, act_ref, dup_ref, dh2b_ref, dh1_ref,
             loss_ref, dg3_ref, dg2_ref, relu_ref):
        @pl.when(pl.program_id(0) == 0)
        def _():
            for ref in (loss_ref, dg3_ref, dg2_ref):
                ref[...] = jnp.zeros_like(ref)

        n2 = n2_ref[...]
        h1 = h1_ref[...]
        h2 = h1
        for k in range(N_CHIPS):
            cols = slice(k * D_MODEL, (k + 1) * D_MODEL)
            r = jnp.maximum(_dot(n2, wup_ref[k]), 0.0)
            relu_ref[:, cols] = r
            act = (r * r).astype(BF16)
            act_ref[:, cols] = act
            h2 = h2 + _dot(act, wdown_ref[cols, :])
        xhat, r3 = _rms(h2)
        diff = xhat * g3_ref[...] - tgt_ref[...]
        sq = jnp.sum(diff * diff, axis=1, keepdims=True)
        loss_ref[...] = loss_ref[...] + (0.5 / D_MODEL) * jnp.sum(sq, axis=0, keepdims=True)
        dy = diff * (1.0 / D_MODEL)
        dg3_ref[...] = dg3_ref[...] + _col_sum(dy * xhat)
        dh2 = _rms_bwd(dy * g3_ref[...], xhat, r3)
        dh2b = dh2.astype(BF16)
        dh2b_ref[...] = dh2b
        dn2 = jnp.zeros((SEQ_TILE, D_MODEL), F32)
        for k in range(N_CHIPS):
            cols = slice(k * D_MODEL, (k + 1) * D_MODEL)
            dup = (_dot_nt(dh2b, wdown_ref[cols, :]) * (2.0 * relu_ref[:, cols])).astype(BF16)
            dup_ref[:, cols] = dup
            dn2 = dn2 + _dot_nt(dup, wup_ref[k])
        xhat, r2 = _rms(h1)
        dg2_ref[...] = dg2_ref[...] + _col_sum(dn2 * xhat)
        dh1_ref[...] = dh2 + _rms_bwd(dn2 * g2_ref[...], xhat, r2)

    tile = pl.BlockSpec((SEQ_TILE, D_MODEL), lambda i: (i, 0))
    wide = pl.BlockSpec((SEQ_TILE, D_FF), lambda i: (i, 0))
    vec = _const((1, D_MODEL))
    vec_shape = jax.ShapeDtypeStruct((1, D_MODEL), F32)
    return _fused_call(
        body, jobs, name="mlp", grid=(t // SEQ_TILE,),
        in_specs=[tile, tile, tile, _resident((N_CHIPS, D_MODEL, D_MODEL)), _resident((D_FF, D_MODEL)), vec, vec],
        out_specs=[wide, wide, tile, tile, _const((SUBLANES, 128)), vec, vec],
        out_shape=[jax.ShapeDtypeStruct((t, D_FF), BF16), jax.ShapeDtypeStruct((t, D_FF), BF16),
                   jax.ShapeDtypeStruct((t, D_MODEL), BF16), jax.ShapeDtypeStruct((t, D_MODEL), F32),
                   jax.ShapeDtypeStruct((SUBLANES, 128), F32), vec_shape, vec_shape],
        scratch_shapes=[pltpu.VMEM((SEQ_TILE, D_FF), F32)],
        compiler_params=_params(),
    )(n2, h1, target, w_up_st, w_down, g2, g3)


def _bwd_mix(dh1, pa, pb, z, h, xc, r, ig, w_oa, w_ob, w_out, ln_g, ln_b, w_s, bias_full, conv_w, wr, wi, lam, jobs=()):
    t = dh1.shape[0]
    n_tiles = t // SEQ_TILE
    per_tile = SEQ_TILE // SUBLANES

    def merge_part(dh1_ref, pa_ref, pb_ref, m_ref, woa_ref, wob_ref, wout_ref, dz_ref, dya_ref, dyb_ref, mg_ref,
                   dpa_ref, dpb_ref, dh1b_ref):
        dh1b = dh1_ref[...].astype(BF16)
        dh1b_ref[...] = dh1b
        dm = _dot_nt(dh1b, wout_ref[...])
        pa = pa_ref[...]
        pb = pb_ref[...]
        sa = _gate(m_ref[:, :D_MODEL])
        sb = _gate(m_ref[:, D_MODEL:])
        mg_ref[...] = (sa * pa + sb * pb).astype(BF16)
        dpa = dm * sa
        dpb = dm * sb
        dz_ref[:, :D_MODEL] = ((dpa * pa) * (1.0 - sa)).astype(BF16)
        dz_ref[:, D_MODEL:] = ((dpb * pb) * (1.0 - sb)).astype(BF16)
        dpa = dpa.astype(BF16)
        dpb = dpb.astype(BF16)
        dpa_ref[...] = dpa
        dpb_ref[...] = dpb
        dya_ref[...] = _dot_nt(dpa, woa_ref[...])
        dyb_ref[...] = _dot_nt(dpb, wob_ref[...])

    def sgu_part(dyb_ref, ub_ref, vb_ref, lg_ref, lb_ref, ws_ref, bias_ref, dz_ref, dlg_ref, dlb_ref, dws_ref, dbs_ref,
                 dvn_ref, dsp_acc):
        i = pl.program_id(0)

        @pl.when(i == 0)
        def _():
            dlg_ref[...] = jnp.zeros_like(dlg_ref)
            dlb_ref[...] = jnp.zeros_like(dlb_ref)
            dws_ref[...] = jnp.zeros_like(dws_ref)
            dsp_acc[...] = jnp.zeros_like(dsp_acc)

        u, du, dvg, rstd, vhat, vn = _sgu_forward_parts(ub_ref[...], vb_ref[...], lg_ref, lb_ref)
        dyb = dyb_ref[...]
        mask = _causal_mask()
        wm = [jnp.where(mask, ws_ref[g], 0.0).astype(BF16) for g in range(GROUPS)]
        for c in range(SEQ_TILE // CHUNK):
            rows = slice(c * CHUNK, (c + 1) * CHUNK)
            for g in range(GROUPS):
                cols = slice(g * GROUP_DIM, (g + 1) * GROUP_DIM)
                vn_blk = vn[rows, cols]
                sp = _dot(wm[g], vn_blk) + bias_ref[:, cols]
                dyb_blk = dyb[rows, cols]
                dz_ref[rows, cols] = (dyb_blk * sp * du[rows, cols]).astype(BF16)
                dsp = dyb_blk * u[rows, cols]
                dsp_acc[:, cols] = dsp_acc[:, cols] + dsp
                dspb = dsp.astype(BF16)
                dvn_ref[rows, cols] = _dot_tn(wm[g], dspb)
                wcols = slice(g * CHUNK, (g + 1) * CHUNK)
                dws_ref[:, wcols] = dws_ref[:, wcols] + jnp.where(mask, _dot_nt(dspb, vn_blk), 0.0)
        dvn = dvn_ref[...]
        dlg_ref[...] = dlg_ref[...] + _col_sum(dvn * vhat)
        dlb_ref[...] = dlb_ref[...] + _col_sum(dvn)
        dvhat = dvn * lg_ref[...]
        dvgel = rstd * (dvhat - jnp.mean(dvhat, axis=-1, keepdims=True)
                        - vhat * jnp.mean(dvhat * vhat, axis=-1, keepdims=True))
        dz_ref[:, D_MODEL:] = (dvgel * dvg).astype(BF16)

        @pl.when(i == n_tiles - 1)
        def _():
            lane = lax.broadcasted_iota(jnp.int32, (CHUNK, 128), 1)
            out = jnp.zeros((CHUNK, 128), F32)
            for g in range(GROUPS):
                s = jnp.sum(dsp_acc[:, g * GROUP_DIM:(g + 1) * GROUP_DIM], axis=1, keepdims=True)
                out = out + jnp.where(lane == g, s, 0.0)
            dbs_ref[...] = out

    def lru_part(dya_ref, xa_ref, ga_ref, h_ref, h_prev_ref, xc_ref, r_ref, ig_ref, cw_ref, wr_ref, wi_ref, lam_ref,
                 dz_ref, dcw_ref, dcb_ref, dwr_ref, dbr_ref, dwi_ref, dbi_ref, dlam_ref, lam_carry, dxc_head,
                 scan_scratch):
        i = pl.program_id(0)

        @pl.when(i == 0)
        def _():
            for ref in (dcw_ref, dcb_ref, dwr_ref, dbr_ref, dwi_ref, dbi_ref, dlam_ref, lam_carry, dxc_head):
                ref[...] = jnp.zeros_like(ref)

        first_tile = i == n_tiles - 1
        h_tail = jnp.where(first_tile, 0.0, h_prev_ref[...])
        xc, r, ig = xc_ref[...], r_ref[...], ig_ref[...]
        xcb = xc.astype(BF16)
        sp, a, mult, inv_mult = _decay(r, lam_ref)
        h = h_ref[...]
        h_prev = _shift_down(h, h_tail, 1)
        dya = dya_ref[...]
        gg, dgg = _gelu_and_grad(ga_ref[...].astype(F32))
        dz_ref[:, D_MODEL:] = (dya * h * dgg).astype(BF16)
        ones = jnp.ones((SUBLANES, D_MODEL), F32)
        lam_t, lam_first = _scan(_shift_up(a, ones, 1), dya * gg, lam_carry[...], scan_scratch, reverse=True)
        lam_carry[...] = a[0:1] * lam_first
        lam_ig = lam_t * ig
        dxc_direct = lam_ig * mult
        dmult = lam_ig * xc
        dla = a * (lam_t * h_prev - (dmult * a) * inv_mult)
        dla_r = dla * r
        dlam_ref[...] = dlam_ref[...] + _col_sum(dla_r) * (LRU_C * jax.nn.sigmoid(-lam_ref[...]))
        dpr = (dla_r * ((-LRU_C) * sp)) * (1.0 - r)
        dpi = (dxc_direct * xc) * (1.0 - ig)
        dbr_ref[...] = dbr_ref[...] + _col_sum(dpr)
        dbi_ref[...] = dbi_ref[...] + _col_sum(dpi)
        dprb = dpr.astype(BF16)
        dpib = dpi.astype(BF16)
        dxc_gate = []
        for hd in range(HEADS):
            cols = slice(hd * HEAD_DIM, (hd + 1) * HEAD_DIM)
            dxc_gate.append(_dot_nt(dprb[:, cols], wr_ref[hd]) + _dot_nt(dpib[:, cols], wi_ref[hd]))
            dwr_ref[hd] = dwr_ref[hd] + _dot_tn(xcb[:, cols], dprb[:, cols])
            dwi_ref[hd] = dwi_ref[hd] + _dot_tn(xcb[:, cols], dpib[:, cols])
        dxc = dxc_direct + jnp.concatenate(dxc_gate, axis=1)
        dcb_ref[...] = dcb_ref[...] + _col_sum(dxc)
        cw = cw_ref[...]
        head = dxc_head[...]
        xa = xa_ref[...].astype(F32)
        dxa = cw[0:1] * dxc
        dcw_ref[0:1, :] = dcw_ref[0:1, :] + _col_sum(dxc * xa)
        for k in range(1, CONV_WIDTH):
            dxc_k = _shift_up(dxc, head, k)
            dxa = dxa + cw[k:k + 1] * dxc_k
            dcw_ref[k:k + 1, :] = dcw_ref[k:k + 1, :] + _col_sum(dxc_k * xa)
        dxc_head[...] = dxc[0:SUBLANES]
        dz_ref[:, :D_MODEL] = dxa.astype(BF16)

    def body(dh1_ref, pa_ref, pb_ref, z_ref, h_ref, h_prev_ref, xc_ref, r_ref, ig_ref, woa_ref, wob_ref, wout_ref,
             lg_ref, lb_ref, ws_ref, bias_ref, cw_ref, wr_ref, wi_ref, lam_ref, dz_ref, mg_ref, dpa_ref, dpb_ref,
             dh1b_ref, dlg_ref, dlb_ref, dws_ref, dbs_ref, dcw_ref, dcb_ref, dwr_ref, dbr_ref, dwi_ref, dbi_ref,
             dlam_ref, dya_ref, dyb_ref, dvn_ref, dsp_acc, lam_carry, dxc_head, *scan_scratch):
        def cols(ref, first, count):
            return ref.at[:, pl.ds(first * D_MODEL, count * D_MODEL)]

        merge_part(dh1_ref, pa_ref, pb_ref, cols(z_ref, 4, 2), woa_ref, wob_ref, wout_ref, cols(dz_ref, 4, 2), dya_ref,
                   dyb_ref, mg_ref, dpa_ref, dpb_ref, dh1b_ref)
        sgu_part(dyb_ref, cols(z_ref, 2, 1), cols(z_ref, 3, 1), lg_ref, lb_ref, ws_ref, bias_ref, cols(dz_ref, 2, 2),
                 dlg_ref, dlb_ref, dws_ref, dbs_ref, dvn_ref, dsp_acc)
        lru_part(dya_ref, cols(z_ref, 0, 1), cols(z_ref, 1, 1), h_ref, h_prev_ref, xc_ref, r_ref, ig_ref, cw_ref, wr_ref,
                 wi_ref, lam_ref, cols(dz_ref, 0, 2), dcw_ref, dcb_ref, dwr_ref, dbr_ref, dwi_ref, dbi_ref, dlam_ref,
                 lam_carry, dxc_head, scan_scratch)

    rev = lambda i: n_tiles - 1 - i
    tile = pl.BlockSpec((SEQ_TILE, D_MODEL), lambda i: (rev(i), 0))
    row = pl.BlockSpec((SEQ_TILE, D_IN), lambda i: (rev(i), 0))
    prev8 = pl.BlockSpec((SUBLANES, D_MODEL), lambda i: (jnp.maximum(rev(i) * per_tile - 1, 0), 0))
    vec = _const((1, D_MODEL))
    sq = _resident((D_MODEL, D_MODEL))
    gate_w = _resident((HEADS, HEAD_DIM, HEAD_DIM))
    gate_acc = _const((HEADS, HEAD_DIM, HEAD_DIM))
    vec_shape = jax.ShapeDtypeStruct((1, D_MODEL), F32)
    gate_shape = jax.ShapeDtypeStruct((HEADS, HEAD_DIM, HEAD_DIM), F32)
    act_bf = jax.ShapeDtypeStruct((t, D_MODEL), BF16)
    return _fused_call(
        body, jobs, name="bwd_mix", grid=(n_tiles,),
        in_specs=[tile, tile, tile, row, tile, prev8, tile, tile, tile, sq, sq, sq, vec, vec,
                  _const((GROUPS, CHUNK, CHUNK)), _const((CHUNK, D_MODEL)), _const((CONV_WIDTH, D_MODEL)), gate_w, gate_w,
                  vec],
        out_specs=[row, tile, tile, tile, tile, vec, vec, _const((CHUNK, GROUPS * CHUNK)), _const((CHUNK, 128)),
                   _const((SUBLANES, D_MODEL)), vec, gate_acc, vec, gate_acc, vec, vec],
        out_shape=[jax.ShapeDtypeStruct((t, D_IN), BF16), act_bf, act_bf, act_bf, act_bf, vec_shape, vec_shape,
                   jax.ShapeDtypeStruct((CHUNK, GROUPS * CHUNK), F32), jax.ShapeDtypeStruct((CHUNK, 128), F32),
                   jax.ShapeDtypeStruct((SUBLANES, D_MODEL), F32), vec_shape, gate_shape, vec_shape, gate_shape,
                   vec_shape, vec_shape],
        scratch_shapes=[pltpu.VMEM((SEQ_TILE, D_MODEL), F32), pltpu.VMEM((SEQ_TILE, D_MODEL), F32),
                        pltpu.VMEM((SEQ_TILE, D_MODEL), F32), pltpu.VMEM((CHUNK, D_MODEL), F32),
                        pltpu.VMEM((1, D_MODEL), F32), pltpu.VMEM((SUBLANES, D_MODEL), F32)] + _scan_scratch(SEQ_TILE),
        compiler_params=_params(),
    )(dh1, pa, pb, z, h, h, xc, r, ig, w_oa, w_ob, w_out, ln_g, ln_b, w_s, bias_full, conv_w, wr, wi, lam)


def _bwd_in(dz, x, dh1, w_in_st, g1, jobs=()):
    t = x.shape[0]

    def body(dz_ref, x_ref, dh1_ref, w_ref, g_ref, dx_ref, dg1_ref):
        @pl.when(pl.program_id(0) == 0)
        def _():
            dg1_ref[...] = jnp.zeros_like(dg1_ref)

        dn1 = jnp.zeros((MM_TILE, D_MODEL), F32)
        for k in range(N_CHIPS):
            dn1 = dn1 + _dot_nt(dz_ref[:, k * IN_SHARD:(k + 1) * IN_SHARD], w_ref[k])
        xhat, r1 = _rms(x_ref[...])
        dg1_ref[...] = dg1_ref[...] + _col_sum(dn1 * xhat)
        dx_ref[...] = dh1_ref[...] + _rms_bwd(dn1 * g_ref[...], xhat, r1)

    tile = pl.BlockSpec((MM_TILE, D_MODEL), lambda i: (i, 0))
    return _fused_call(
        body, jobs, name="bwd_in", grid=(t // MM_TILE,),
        in_specs=[pl.BlockSpec((MM_TILE, D_IN), lambda i: (i, 0)), tile, tile,
                  _resident((N_CHIPS, D_MODEL, IN_SHARD)), _const((1, D_MODEL))],
        out_specs=[tile, _const((1, D_MODEL))],
        out_shape=[jax.ShapeDtypeStruct((t, D_MODEL), F32), jax.ShapeDtypeStruct((1, D_MODEL), F32)],
        compiler_params=_params(),
    )(dz, x, dh1, w_in_st, g1)


def _weight_grad(name, a, b, n_blocks, a_varies, b_varies, width, jobs=()):
    t = a.shape[0]
    rows = min(DW_TILE, t)
    n_t = t // rows

    def body(a_ref, b_ref, o_ref, acc_ref):
        s = pl.program_id(1)
        part = _dot_tn(a_ref[...], b_ref[...])

        @pl.when(s == 0)
        def _():
            acc_ref[...] = part

        @pl.when(s > 0)
        def _():
            acc_ref[...] = acc_ref[...] + part

        @pl.when(s == n_t - 1)
        def _():
            o_ref[...] = acc_ref[...].astype(BF16)

    return _fused_call(
        body, jobs, name=name, grid=(n_blocks, n_t),
        in_specs=[pl.BlockSpec((rows, D_MODEL), (lambda j, s: (s, j)) if a_varies else (lambda j, s: (s, 0))),
                  pl.BlockSpec((rows, width), (lambda j, s: (s, j)) if b_varies else (lambda j, s: (s, 0)))],
        out_specs=pl.BlockSpec((None, D_MODEL, width), lambda j, s: (j, 0, 0)),
        out_shape=jax.ShapeDtypeStruct((n_blocks, D_MODEL, width), BF16),
        scratch_shapes=[pltpu.VMEM((D_MODEL, width), F32)],
        compiler_params=_params(2),
    )(a, b)


def _weight_grads_square(name, pairs, jobs=()):
    n = len(pairs)
    t = pairs[0][0].shape[0]
    rows = min(2 * MM_TILE, t)
    n_t = t // rows

    def body(*refs):
        ins, outs, accs = refs[:2 * n], refs[2 * n:3 * n], refs[3 * n:]
        s = pl.program_id(0)
        for k in range(n):
            part = _dot_tn(ins[2 * k][...], ins[2 * k + 1][...])

            @pl.when(s == 0)
            def _(k=k, part=part):
                accs[k][...] = part

            @pl.when(s > 0)
            def _(k=k, part=part):
                accs[k][...] = accs[k][...] + part

            @pl.when(s == n_t - 1)
            def _(k=k):
                outs[k][...] = accs[k][...].astype(BF16)

    tile = pl.BlockSpec((rows, D_MODEL), lambda s: (s, 0))
    return _fused_call(
        body, jobs, name=name, grid=(n_t,), in_specs=[tile] * (2 * n), out_specs=[_const((D_MODEL, D_MODEL))] * n,
        out_shape=[jax.ShapeDtypeStruct((D_MODEL, D_MODEL), BF16)] * n,
        scratch_shapes=[pltpu.VMEM((D_MODEL, D_MODEL), F32)] * n,
        compiler_params=_params(),
    )(*[x for pair in pairs for x in pair])


def _place():
    x, y, c = lax.axis_index("x"), lax.axis_index("y"), lax.axis_index("c")
    other_chips = [(1 - x, y), (x, 1 - y), (1 - x, 1 - y)]
    return x, y, c, other_chips


def _chip_index(px, py):
    return 2 * px + py


ANY = pl.BlockSpec(memory_space=pl.ANY)
SIBLING = ((0, 0, 1),)
NEIGHBOURS = ((1, 0, 0), (0, 1, 0))
OTHER_CHIPS = NEIGHBOURS + ((1, 1, 0),)


def _near_far(x, y, c):
    return (x ^ (1 - c), y ^ c), (x ^ c, y ^ (1 - c))


def _gather_near_job(shards):
    n = len(shards)
    halves = [s.shape[0] // 2 for s in shards]

    def copies(ins, outs, send, recv, local):
        x, y, c, _ = _place()
        near, _ = _near_far(x, y, c)

        def block(w, chip, pc):
            return outs[w].at[_chip_index(*chip), pl.ds(pc * halves[w], halves[w]), :]

        def copy(w, k, chip, pc, to, src=None):
            return pltpu.make_async_remote_copy(
                src_ref=block(w, chip, pc) if src is None else src, dst_ref=block(w, chip, pc),
                send_sem=send.at[2 * w + k], recv_sem=recv.at[2 * w + k], device_id=to, device_id_type=MESH)

        sends, arrivals, own = [], [], []
        for w in range(n):
            src = ins[w].at[pl.ds(c * halves[w], halves[w]), :]
            own.append(pltpu.make_async_copy(src, block(w, (x, y), c), local.at[w]))
            sends += [copy(w, 0, (x, y), c, (*near, c), src), copy(w, 1, (x, y), c, (x, y, 1 - c), src)]
            arrivals += [copy(w, 0, near, c, (x, y, c)), copy(w, 1, (x, y), 1 - c, (x, y, c))]
        return sends, arrivals, own

    return _Job(shards, [jax.ShapeDtypeStruct((N_CHIPS,) + s.shape, s.dtype) for s in shards], 2 * n, copies,
                NEIGHBOURS + SIBLING, n_local=n)


def _gather_far_job(stacked):
    n = len(stacked)
    halves = [s.shape[1] // 2 for s in stacked]

    def copies(ins, outs, send, recv, local):
        del ins, local
        x, y, c, _ = _place()
        near, far = _near_far(x, y, c)

        def copy(w, k, chip):
            blk = outs[w].at[_chip_index(*chip), pl.ds(c * halves[w], halves[w]), :]
            return pltpu.make_async_remote_copy(
                src_ref=blk, dst_ref=blk, send_sem=send.at[2 * w + k], recv_sem=recv.at[2 * w + k],
                device_id=(*far, c), device_id_type=MESH)

        sends = [copy(w, k, chip) for w in range(n) for k, chip in enumerate(((x, y), near))]
        arrivals = [copy(w, k, chip) for w in range(n) for k, chip in enumerate((far, (1 - x, 1 - y)))]
        return sends, arrivals, []

    return _Job(stacked, [jax.ShapeDtypeStruct(s.shape, s.dtype) for s in stacked], 2 * n, copies, NEIGHBOURS,
                aliases={w: w for w in range(n)})


def _gather_pass_job(stacked):
    n = len(stacked)
    halves = [s.shape[1] // 2 for s in stacked]

    def copies(ins, outs, send, recv, local):
        del ins, local
        x, y, c, chips = _place()

        def copy(w, j, chip, pc, to):
            blk = outs[w].at[_chip_index(*chip), pl.ds(pc * halves[w], halves[w]), :]
            return pltpu.make_async_remote_copy(
                src_ref=blk, dst_ref=blk, send_sem=send.at[3 * w + j], recv_sem=recv.at[3 * w + j], device_id=to,
                device_id_type=MESH)

        sends = [copy(w, j, chip, c, (x, y, 1 - c)) for w in range(n) for j, chip in enumerate(chips)]
        arrivals = [copy(w, j, chip, 1 - c, (x, y, c)) for w in range(n) for j, chip in enumerate(chips)]
        return sends, arrivals, []

    return _Job(stacked, [jax.ShapeDtypeStruct(s.shape, s.dtype) for s in stacked], 3 * n, copies, SIBLING,
                aliases={w: w for w in range(n)})


def _gather_small_job(block):
    def copies(ins, outs, send, recv, local):
        x, y, c, chips = _place()

        def copy(j, chip_from, to):
            return pltpu.make_async_remote_copy(
                src_ref=ins[0], dst_ref=outs[0].at[_chip_index(*chip_from)], send_sem=send.at[j],
                recv_sem=recv.at[j], device_id=to, device_id_type=MESH)

        own = [pltpu.make_async_copy(ins[0], outs[0].at[_chip_index(x, y)], local.at[0])]
        sends = [copy(j, (x, y), (*chip, c)) for j, chip in enumerate(chips)]
        arrivals = [copy(j, chip, (x, y, c)) for j, chip in enumerate(chips)]
        return sends, arrivals, own

    return _Job([block], [jax.ShapeDtypeStruct((N_CHIPS,) + block.shape, block.dtype)], 3, copies, OTHER_CHIPS,
                n_local=1)


def _pair_send_job(grads):
    n = len(grads)
    halves = [g.shape[1] // 2 for g in grads]

    def copies(ins, outs, send, recv, local):
        del local
        x, y, c, _ = _place()
        sends = [pltpu.make_async_remote_copy(
            src_ref=ins[w].at[:, pl.ds((1 - c) * halves[w], halves[w]), :], dst_ref=outs[w], send_sem=send.at[w],
            recv_sem=recv.at[w], device_id=(x, y, 1 - c), device_id_type=MESH) for w in range(n)]
        return sends, sends, []

    return _Job(grads, [jax.ShapeDtypeStruct((N_CHIPS, h, g.shape[2]), g.dtype) for g, h in zip(grads, halves)], n,
                copies, SIBLING)


ROW_STEPS = 4


def _pair_add(name, core, mine, theirs):
    n = len(mine)

    def body(core_ref, *refs):
        del core_ref
        for a_ref, b_ref, o_ref in zip(refs[:n], refs[n:2 * n], refs[2 * n:]):
            o_ref[...] = (a_ref[...].astype(F32) + b_ref[...].astype(F32)).astype(BF16)

    half = lambda a: pl.BlockSpec((None, None) + a.shape[2:], lambda k, core_ref: (k, core_ref[0], 0, 0))
    block = lambda b: pl.BlockSpec((None,) + b.shape[1:], lambda k, core_ref: (k, 0, 0))
    return pl.pallas_call(
        body, name=name,
        grid_spec=pltpu.PrefetchScalarGridSpec(
            num_scalar_prefetch=1, grid=(N_CHIPS,),
            in_specs=[half(a) for a in mine] + [block(b) for b in theirs], out_specs=[block(b) for b in theirs]),
        out_shape=[jax.ShapeDtypeStruct(b.shape, BF16) for b in theirs],
        compiler_params=_params(),
    )(core, *mine, *theirs)


def _sequencer_call(name, collective_id, job):
    steps, peers = job.phases, job.peers
    ins = [jax.new_ref(a, memory_space=pltpu.MemorySpace.HBM) for a in job.inputs]
    outs = [ins[{o: i for i, o in job.aliases.items()}[k]] if k in job.aliases.values()
            else jax.empty_ref(shape, memory_space=pltpu.MemorySpace.HBM) for k, shape in enumerate(job.out_shape)]
    sems = [pltpu.SemaphoreType.DMA((n,)) for step in steps for n in (step.n_sem, step.n_sem, max(step.n_local, 1))]

    @pl.kernel(mesh=plsc.ScalarSubcoreMesh(axis_name="sequencer", num_cores=1), name=name, scratch_types=tuple(sems),
               compiler_params=pltpu.CompilerParams(collective_id=collective_id))
    def launch(*sem_refs):
        x, y, c, _ = _place()
        barrier = pltpu.get_barrier_semaphore()
        for dx, dy, dc in peers:
            pl.semaphore_signal(barrier, inc=1, device_id=(x ^ dx, y ^ dy, c ^ dc), device_id_type=MESH)
        pl.semaphore_wait(barrier, len(peers))
        for k, step in enumerate(steps):
            sends, arrivals, own = step.copies(ins if k == 0 else outs, outs, *sem_refs[3 * k:3 * k + 3])
            for cp in own + sends:
                cp.start()
            for cp in arrivals:
                cp.wait_recv()
            for cp in sends:
                cp.wait_send()
            for cp in own:
                cp.wait()

    launch()
    return [ref[...] for ref in outs]


def _chip_exchange_job(sums):
    n = len(sums)

    def copies(ins, outs, send, recv, local):
        del local
        _, _, c, chips = _place()
        sends = [pltpu.make_async_remote_copy(
            src_ref=ins[w].at[_chip_index(*chip)], dst_ref=outs[w].at[j], send_sem=send.at[3 * w + j],
            recv_sem=recv.at[3 * w + j], device_id=(*chip, c), device_id_type=MESH)
            for w in range(n) for j, chip in enumerate(chips)]
        return sends, sends, []

    return _Job(sums, [jax.ShapeDtypeStruct((N_CHIPS - 1,) + s.shape[1:], s.dtype) for s in sums], 3 * n, copies,
                OTHER_CHIPS)


def _chip_sum(name, place, mine, theirs):
    n = len(mine)

    def body(place_ref, *refs):
        del place_ref
        for p_ref, q_ref, o_ref in zip(refs[:n], refs[n:2 * n], refs[2 * n:]):
            acc = p_ref[...].astype(F32)
            for j in range(N_CHIPS - 1):
                acc = acc + q_ref[j].astype(F32)
            o_ref[...] = acc

    def block(p, lead, pick):
        return pl.BlockSpec((lead, p.shape[1] // ROW_STEPS, p.shape[2]), lambda r, place_ref: (pick(place_ref), r, 0))

    return pl.pallas_call(
        body, name=name,
        grid_spec=pltpu.PrefetchScalarGridSpec(
            num_scalar_prefetch=1, grid=(ROW_STEPS,),
            in_specs=[block(p, None, lambda place_ref: place_ref[0]) for p in mine]
            + [block(p, N_CHIPS - 1, lambda place_ref: 0) for p in mine],
            out_specs=[block(p, None, lambda place_ref: place_ref[1]) for p in mine]),
        out_shape=[jax.ShapeDtypeStruct((2,) + p.shape[1:], F32) for p in mine],
        compiler_params=_params(),
    )(place, *mine, *theirs)


def _share_job(bufs):
    n = len(bufs)

    def copies(ins, outs, send, recv, local):
        del ins, local
        x, y, c, _ = _place()

        def copy(w, half):
            return pltpu.make_async_remote_copy(
                src_ref=outs[w].at[half], dst_ref=outs[w].at[half], send_sem=send.at[w], recv_sem=recv.at[w],
                device_id=(x, y, 1 - c), device_id_type=MESH)

        return [copy(w, c) for w in range(n)], [copy(w, 1 - c) for w in range(n)], []

    return _Job(bufs, [jax.ShapeDtypeStruct(b.shape, b.dtype) for b in bufs], n, copies, SIBLING,
                aliases={w: w for w in range(n)})


SMALL_ROWS = 24
ROW_G1, ROW_CW, ROW_CB, ROW_BR, ROW_BI, ROW_LAM, ROW_LG, ROW_LB, ROW_G2, ROW_G3, ROW_LOSS, ROW_BS = (
    0, 1, 5, 6, 7, 8, 9, 10, 11, 12, 13, 16)
N_DEV = 8


def _pack_small(dcw, dcb, dbr, dbi, dlam, dlg, dlb, dg2, dg3, loss, dbs):
    def body(dcw_ref, dcb_ref, dbr_ref, dbi_ref, dlam_ref, dlg_ref, dlb_ref, dg2_ref, dg3_ref, loss_ref, dbs_ref, out):
        out[...] = jnp.zeros((SMALL_ROWS, D_MODEL), F32)
        for row, ref in ((ROW_CB, dcb_ref), (ROW_BR, dbr_ref), (ROW_BI, dbi_ref), (ROW_LAM, dlam_ref),
                         (ROW_LG, dlg_ref), (ROW_LB, dlb_ref), (ROW_G2, dg2_ref), (ROW_G3, dg3_ref)):
            out[row:row + 1, :] = ref[...]
        out[ROW_CW:ROW_CW + CONV_WIDTH, :] = dcw_ref[0:CONV_WIDTH, :]
        out[ROW_LOSS:ROW_LOSS + 1, 0:128] = loss_ref[0:1, :]
        out[ROW_BS:ROW_BS + GROUPS, 0:128] = jnp.transpose(dbs_ref[...])[0:GROUPS, :]

    vm = pl.BlockSpec(memory_space=pltpu.VMEM)
    return pl.pallas_call(
        body, name="pack_small", in_specs=[vm] * 11, out_specs=vm,
        out_shape=jax.ShapeDtypeStruct((SMALL_ROWS, D_MODEL), F32),
    )(dcw, dcb, dbr, dbi, dlam, dlg, dlb, dg2, dg3, loss, dbs)


def _gather_all_job(blocks):
    n = len(blocks)
    flips = [(dx, dy, dc) for dx in (0, 1) for dy in (0, 1) for dc in (0, 1)][1:]

    def copies(ins, outs, send, recv, local):
        x, y, c, _ = _place()
        me = 4 * x + 2 * y + c
        sends, arrivals, own = [], [], []
        for w in range(n):
            own.append(pltpu.make_async_copy(ins[w], outs[w].at[me], local.at[w]))
            for k, (dx, dy, dc) in enumerate(flips):
                peer = (x ^ dx, y ^ dy, c ^ dc)
                sem = dict(send_sem=send.at[7 * w + k], recv_sem=recv.at[7 * w + k])
                sends.append(pltpu.make_async_remote_copy(
                    src_ref=ins[w], dst_ref=outs[w].at[me], device_id=peer, device_id_type=MESH, **sem))
                arrivals.append(pltpu.make_async_remote_copy(
                    src_ref=ins[w], dst_ref=outs[w].at[4 * peer[0] + 2 * peer[1] + peer[2]], device_id=peer,
                    device_id_type=MESH, **sem))
        return sends, arrivals, own

    return _Job(blocks, [jax.ShapeDtypeStruct((N_DEV,) + b.shape, b.dtype) for b in blocks], 7 * n, copies,
                OTHER_CHIPS + SIBLING + tuple((dx, dy, 1) for dx, dy, _ in OTHER_CHIPS), n_local=n)


def _sum_small(vec_all, ws_all, dg1_all):
    def body(vec_ref, ws_ref, dg1_ref, vec_out, ws_out):
        vec, ws, dg1 = vec_ref[0], ws_ref[0], dg1_ref[0]
        for d in range(1, N_DEV):
            vec, ws, dg1 = vec + vec_ref[d], ws + ws_ref[d], dg1 + dg1_ref[d]
        vec_out[...] = vec
        vec_out[ROW_G1:ROW_G1 + 1, :] = dg1
        ws_out[...] = ws

    vm = pl.BlockSpec(memory_space=pltpu.VMEM)
    return pl.pallas_call(
        body, name="sum_small", in_specs=[vm] * 3, out_specs=[vm, vm],
        out_shape=[jax.ShapeDtypeStruct(vec_all.shape[1:], F32), jax.ShapeDtypeStruct(ws_all.shape[1:], F32)],
    )(vec_all, ws_all, dg1_all)


def _adamw_math(w, g, m, v):
    m = ADAM_B1 * m + (1.0 - ADAM_B1) * g
    v = ADAM_B2 * v + (1.0 - ADAM_B2) * (g * g)
    m_hat = m / (1.0 - ADAM_B1 ** ADAM_STEP)
    v_hat = v / (1.0 - ADAM_B2 ** ADAM_STEP)
    delta = (-ADAM_LR) * (m_hat / (jnp.sqrt(v_hat) + ADAM_EPS) + ADAM_WD * w)
    return delta, m, v


def _adamw(name, gs, ws, ms, vs):
    n = len(ws)

    def body(*refs):
        ins, outs = refs[:4 * n], refs[4 * n:]
        for p in range(n):
            g_ref, w_ref, m_ref, v_ref = ins[p::n]
            g = g_ref[...]
            outs[4 * p][...] = g
            outs[4 * p + 1][...], outs[4 * p + 2][...], outs[4 * p + 3][...] = _adamw_math(
                w_ref[...], g, m_ref[...], v_ref[...])

    blocks = [pl.BlockSpec((w.shape[0] // ROW_STEPS, w.shape[1]), lambda r: (r, 0)) for w in ws]
    out = pl.pallas_call(
        body, name=name, grid=(ROW_STEPS,), in_specs=blocks * 4, out_specs=[b for b in blocks for _ in range(4)],
        out_shape=[jax.ShapeDtypeStruct(w.shape, F32) for w in ws for _ in range(4)], compiler_params=_params(),
    )(*gs, *ws, *ms, *vs)
    return [tuple(out[4 * p:4 * p + 4]) for p in range(n)]


def _adamw_small(grads, ws, ms, vs):
    n = len(grads)

    def body(*refs):
        g_refs, w_refs, m_refs, v_refs = refs[:n], refs[n:2 * n], refs[2 * n:3 * n], refs[3 * n:4 * n]
        outs = refs[4 * n:]
        for p in range(n):
            d, nm, nv = _adamw_math(w_refs[p][...], g_refs[p][...], m_refs[p][...], v_refs[p][...])
            outs[p][...] = d
            outs[n + p][...] = nm
            outs[2 * n + p][...] = nv

    vm = pl.BlockSpec(memory_space=pltpu.VMEM)
    shapes = [jax.ShapeDtypeStruct(w.shape, F32) for w in ws]
    out = pl.pallas_call(
        body, name="adamw_small", in_specs=[vm] * (4 * n), out_specs=[vm] * (3 * n), out_shape=shapes * 3,
    )(*grads, *ws, *ms, *vs)
    return out[:n], out[n:2 * n], out[2 * n:]


def _unstack_heads(w_st):
    per = HEAD_DIM // N_CHIPS
    return w_st.reshape(N_CHIPS, HEADS, per, HEAD_DIM).transpose(1, 0, 2, 3).reshape(HEADS, HEAD_DIM, HEAD_DIM)


def _stack_heads(w):
    per = HEAD_DIM // N_CHIPS
    return w.reshape(HEADS, N_CHIPS, per, HEAD_DIM).transpose(1, 0, 2, 3).reshape(N_CHIPS, HEADS * per, HEAD_DIM)


def kernel(x, norm_mix_g, w_in, conv_w, conv_b, w_rgate, b_rgate, w_igate, b_igate, lru_lambda, w_out_a, sgu_ln_g, sgu_ln_b, sgu_w_s, sgu_b_s, w_out_b, w_out, norm_mlp_g, w_up, w_down, norm_final_g, loss_target, m_norm_mix_g, m_w_in, m_conv_w, m_conv_b, m_w_rgate, m_b_rgate, m_w_igate, m_b_igate, m_lru_lambda, m_w_out_a, m_sgu_ln_g, m_sgu_ln_b, m_sgu_w_s, m_sgu_b_s, m_w_out_b, m_w_out, m_norm_mlp_g, m_w_up, m_w_down, m_norm_final_g, v_norm_mix_g, v_w_in, v_conv_w, v_conv_b, v_w_rgate, v_b_rgate, v_w_igate, v_b_igate, v_lru_lambda, v_w_out_a, v_sgu_ln_g, v_sgu_ln_b, v_sgu_w_s, v_sgu_b_s, v_w_out_b, v_w_out, v_norm_mlp_g, v_w_up, v_w_down, v_norm_final_g):
    chip = _chip_index(lax.axis_index("x"), lax.axis_index("y"))
    core = lax.axis_index("c")
    quarter_h = HEAD_DIM // N_CHIPS
    quarter_d = D_MODEL // N_CHIPS

    as_2d = lambda a: a.reshape(-1, a.shape[-1])
    big_w = [as_2d(w) for w in (w_in, w_rgate, w_igate, w_out_a, w_out_b, w_out, w_up, w_down)]
    big_m = [as_2d(w) for w in (m_w_in, m_w_rgate, m_w_igate, m_w_out_a, m_w_out_b, m_w_out, m_w_up, m_w_down)]
    big_v = [as_2d(w) for w in (v_w_in, v_w_rgate, v_w_igate, v_w_out_a, v_w_out_b, v_w_out, v_w_up, v_w_down)]

    packed = jnp.concatenate([conv_w[0], b_rgate[0], b_igate[0]], axis=1)
    packed = jnp.concatenate([packed, jnp.zeros_like(packed)], axis=0)
    s_in, s_r, s_i, s_oa, s_ob, s_out, s_up, s_down = [w.astype(BF16) for w in big_w]
    xs, target = x[0], loss_target[0]
    g3 = norm_final_g.reshape(1, D_MODEL)
    bias_s = jnp.broadcast_to(jnp.transpose(sgu_b_s[0])[:, :, None], (CHUNK, GROUPS, GROUP_DIM)).reshape(CHUNK, D_MODEL)
    core_arr = core.reshape(1).astype(jnp.int32)
    place = jnp.stack([chip, core]).astype(jnp.int32)
    quarter = lambda g: g.reshape(N_CHIPS, D_MODEL // N_CHIPS, D_MODEL)

    def pair_add(nm, grads, from_sibling):
        halves = [g.reshape(N_CHIPS, 2, g.shape[1] // 2, g.shape[2]) for g in grads]
        return list(_pair_add("pair_add_" + nm, core_arr, halves, from_sibling))

    def chip_sum(nm, pairs, from_chips):
        return list(_chip_sum("chip_sum_" + nm, place, pairs, from_chips))

    order = jnp.stack([chip, chip ^ 2, chip ^ 1, chip ^ 3]).astype(jnp.int32)
    (z, n1, (w_in_st, wr_st, wi_st)), ((packed_all,), late) = _fwd_in(
        xs, norm_mix_g, [s_in, s_r, s_i], order,
        jobs=[_gather_small_job(packed), _gather_near_job([s_oa, s_ob, s_out])])
    pick = lambda lo, hi: packed_all[:, :HEADS, lo:hi].transpose(1, 0, 2).reshape(HEADS, -1)
    conv_w_full = pick(0, quarter_d)
    br_full = pick(quarter_d, quarter_d + quarter_h).reshape(1, D_MODEL)
    bi_full = pick(quarter_d + quarter_h, quarter_d + 2 * quarter_h).reshape(1, D_MODEL)
    wr, wi = _unstack_heads(wr_st), _unstack_heads(wi_st)
    lru = (conv_w_full, conv_b, wr, br_full, wi, bi_full, lru_lambda)
    sgu = (sgu_ln_g, sgu_ln_b, sgu_w_s[0], bias_s)

    after = lambda arrays, result: lax.optimization_barrier((arrays, result))[0]
    w_up_st, w_dn = _sequencer_call(
        "gather_mlp", 8, _gather_near_job(after([s_up, s_down], n1)).then(_gather_far_job).then(_gather_pass_job))
    w_dn = w_dn.reshape(D_FF, D_MODEL)
    late_step = (3 * (xs.shape[0] // SEQ_TILE) // 4,)
    (ya, *saved), (late,) = _fwd_lru(z, *lru, jobs=[_gather_far_job(late).then(_gather_pass_job, at=late_step)])
    w_oa, w_ob, w_o = [w.reshape(D_MODEL, D_MODEL) for w in late]
    (yb, pa, pb, h1, n2), _ = _fwd_sgu_merge(ya, z, xs, *sgu, w_oa, w_ob, w_o, norm_mlp_g)
    (act, dup, dh2b, dh1, loss_part, dg3, dg2), _ = _mlp(n2, h1, target, w_up_st, w_dn, norm_mlp_g, g3)

    d_down, _ = _weight_grad("dw_down", act, dh2b, N_CHIPS, True, False, D_MODEL)
    r_down, = _sequencer_call("send_w_down", 10, _pair_send_job([d_down]))
    d_up, _ = _weight_grad("dw_up", n2, dup, N_CHIPS, False, True, D_MODEL)
    r_up, = _sequencer_call("send_w_up", 11, _pair_send_job([d_up]))
    (p_down,), (p_up,) = pair_add("w_down", [d_down], [r_down]), pair_add("w_up", [d_up], [r_up])
    (dz, merged, dpa, dpb, dh1b, dlg, dlb, dws, dbs, dcw, dcb, dwr, dbr, dwi, dbi, dlam), ((q_up, q_down),) = _bwd_mix(
        dh1, pa, pb, z, *saved, w_oa, w_ob, w_o, *sgu, conv_w_full, wr, wi, lru_lambda,
        jobs=[_chip_exchange_job([p_up, p_down])])
    names = ("w_in", "w_rgate", "w_igate", "w_out_a", "w_out_b", "w_out", "w_up", "w_down")
    (d_out, d_oa, d_ob), _ = _weight_grads_square("dw_projections", [(merged, dh1b), (ya, dpa), (yb, dpb)])
    mids = [quarter(d_oa), quarter(d_ob), quarter(d_out)]
    r_mids = _sequencer_call("send_mids", 1, _pair_send_job(mids))
    half_up, half_down = chip_sum("mlp", [p_up, p_down], after([q_up, q_down], mids))
    gates = [_stack_heads(dwr).astype(BF16), _stack_heads(dwi).astype(BF16)]
    small = _pack_small(dcw, dcb, dbr, dbi, dlam, dlg, dlb, dg2, dg3, loss_part, dbs)
    p_mids = pair_add("projections", after(mids, [half_up, half_down]), r_mids)
    q_mids = _sequencer_call("exchange_mids", 2, _chip_exchange_job(p_mids))
    d_in, (r_gates, (vec_all, ws_all), (full_up, full_down)) = _weight_grad(
        "dw_in", n1, after(dz, p_mids), N_CHIPS, False, True, IN_SHARD,
        jobs=[_pair_send_job(gates), _gather_all_job([small, dws]), _share_job([half_up, half_down])])
    r_in, = _sequencer_call("send_w_in", 3, _pair_send_job(after([d_in], q_mids)))
    adam_args = {nm: (w, m, v) for nm, w, m, v in zip(names, big_w, big_m, big_v)}

    def adamw(group, nms, grads):
        given = [adam_args[nm] for nm in nms]
        grads = [g.reshape(w.shape) for g, (w, _, _) in zip(grads, given)]
        outs = _adamw("adamw_" + group, grads, *[[a[q] for a in given] for q in range(3)])
        return {nm: (out[0], out[1:]) for nm, out in zip(nms, outs)}

    p_gates = pair_add("gates", gates, r_gates)
    half_mids = chip_sum("projections", p_mids, q_mids)
    full_mids = _sequencer_call("share_mids", 12, _share_job(half_mids))
    p_first = pair_add("w_in", after([d_in], half_mids), [r_in]) + p_gates
    q_first = _sequencer_call("exchange_w_in", 4, _chip_exchange_job(p_first))
    (grad_x, dg1), _ = _bwd_in(dz, xs, dh1, w_in_st, norm_mix_g)
    dg1_all, = _sequencer_call("gather_dg1", 6, _gather_all_job([dg1]))
    done = adamw("mlp", ("w_up", "w_down"), [full_up, full_down])
    q_first = after(q_first, [out[0] for _, out in done.values()])
    half_first = chip_sum("first", p_first, q_first)
    full_first = _sequencer_call("share_last", 5, _share_job(half_first))
    done.update(adamw("projections", names[3:6], after(full_mids, half_first)))
    done.update(adamw("first", names[:3], full_first))
    full, big_out = [done[nm][0] for nm in names], [done[nm][1] for nm in names]

    vec, ws_sum = _sum_small(vec_all, ws_all, dg1_all)
    row = lambda r: vec[r:r + 1]
    shard = lambda a, width: lax.dynamic_slice_in_dim(a, chip * width, width, axis=1)
    g_small = dict(
        norm_mix_g=row(ROW_G1), conv_w=shard(vec[ROW_CW:ROW_CW + CONV_WIDTH], quarter_d), conv_b=row(ROW_CB),
        b_rgate=shard(row(ROW_BR).reshape(HEADS, HEAD_DIM), quarter_h),
        b_igate=shard(row(ROW_BI).reshape(HEADS, HEAD_DIM), quarter_h), lru_lambda=row(ROW_LAM),
        sgu_ln_g=row(ROW_LG), sgu_ln_b=row(ROW_LB),
        sgu_w_s=ws_sum.reshape(CHUNK, GROUPS, CHUNK).transpose(1, 0, 2).reshape(GROUPS * CHUNK, CHUNK),
        sgu_b_s=vec[ROW_BS:ROW_BS + GROUPS, 0:CHUNK], norm_mlp_g=row(ROW_G2), norm_final_g=row(ROW_G3))
    loss = vec[ROW_LOSS, 0]
    small_names = list(g_small)
    given = dict(
        norm_mix_g=(norm_mix_g, m_norm_mix_g, v_norm_mix_g), conv_w=(conv_w, m_conv_w, v_conv_w),
        conv_b=(conv_b, m_conv_b, v_conv_b), b_rgate=(b_rgate, m_b_rgate, v_b_rgate),
        b_igate=(b_igate, m_b_igate, v_b_igate), lru_lambda=(lru_lambda, m_lru_lambda, v_lru_lambda),
        sgu_ln_g=(sgu_ln_g, m_sgu_ln_g, v_sgu_ln_g), sgu_ln_b=(sgu_ln_b, m_sgu_ln_b, v_sgu_ln_b),
        sgu_w_s=(sgu_w_s, m_sgu_w_s, v_sgu_w_s), sgu_b_s=(sgu_b_s, m_sgu_b_s, v_sgu_b_s),
        norm_mlp_g=(norm_mlp_g, m_norm_mlp_g, v_norm_mlp_g), norm_final_g=(norm_final_g, m_norm_final_g, v_norm_final_g))
    g2d = [g_small[nm] for nm in small_names]
    to2d = lambda a, g: a.reshape(g.shape)
    d_s, m_s, v_s = _adamw_small(
        g2d, *[[to2d(given[nm][q], g) for nm, g in zip(small_names, g2d)] for q in range(3)])

    shapes = dict(
        norm_mix_g=norm_mix_g, w_in=w_in, conv_w=conv_w, conv_b=conv_b, w_rgate=w_rgate, b_rgate=b_rgate,
        w_igate=w_igate, b_igate=b_igate, lru_lambda=lru_lambda, w_out_a=w_out_a, sgu_ln_g=sgu_ln_g,
        sgu_ln_b=sgu_ln_b, sgu_w_s=sgu_w_s, sgu_b_s=sgu_b_s, w_out_b=w_out_b, w_out=w_out, norm_mlp_g=norm_mlp_g,
        w_up=w_up, w_down=w_down, norm_final_g=norm_final_g)
    grads, deltas, new_m, new_v = {}, {}, {}, {}
    for nm, g, (d, nmom, nvar) in zip(names, full, big_out):
        grads[nm], deltas[nm], new_m[nm], new_v[nm] = g, d, nmom, nvar
    for p, nm in enumerate(small_names):
        grads[nm], deltas[nm], new_m[nm], new_v[nm] = g2d[p], d_s[p], m_s[p], v_s[p]
    order = list(shapes)
    out = [loss, grad_x[None]]
    for group in (grads, deltas, new_m, new_v):
        out += [group[nm].reshape(shapes[nm].shape) for nm in order]
    return tuple(out)
```

```python
import functools

import jax
import jax.numpy as jnp
from jax import lax
from jax.experimental import pallas as pl
from jax.experimental.pallas import tpu as pltpu
from jax.experimental.pallas import tpu_sc as plsc

F32 = jnp.float32
BF16 = jnp.bfloat16
MESH = pl.DeviceIdType.MESH

D_MODEL = 1024
D_IN = 6 * D_MODEL
D_FF = 4 * D_MODEL
N_CHIPS = 4
IN_SHARD = D_IN // N_CHIPS
HEADS = 4
HEAD_DIM = D_MODEL // HEADS
GROUPS = 4
GROUP_DIM = D_MODEL // GROUPS
CHUNK = 128
CONV_WIDTH = 4
LRU_C = 8.0
NORM_EPS = 1e-6
LN_EPS = 1e-5

ADAM_LR = 0.001
ADAM_B1 = 0.9
ADAM_B2 = 0.999
ADAM_EPS = 1e-08
ADAM_WD = 0.01
ADAM_STEP = 10

SUBLANES = 8
MM_TILE = 512
IN_TILE = 1024
SEQ_TILE = 256
DW_TILE = 2048
VMEM_LIMIT_BYTES = 56 * 1024 * 1024

GELU_K0 = 0.7978845608028654
GELU_K1 = 0.044715


def _params(n_grid_axes=1):
    return pltpu.CompilerParams(
        dimension_semantics=("arbitrary",) * n_grid_axes, vmem_limit_bytes=VMEM_LIMIT_BYTES)


def _resident(shape):
    nd = len(shape)
    return pl.BlockSpec(shape, lambda *_: (0,) * nd, pipeline_mode=pl.Buffered(1))


def _const(shape):
    nd = len(shape)
    return pl.BlockSpec(shape, lambda *_: (0,) * nd)


def _dot(a, b):
    return jnp.dot(a, b, preferred_element_type=F32)


def _dot_nt(a, b):
    return lax.dot_general(a, b, (((1,), (1,)), ((), ())), preferred_element_type=F32)


def _dot_tn(a, b):
    return lax.dot_general(a, b, (((0,), (0,)), ((), ())), preferred_element_type=F32)


def _gelu(x):
    t = jnp.tanh(x * (GELU_K0 + (GELU_K0 * GELU_K1) * (x * x)))
    return x * (0.5 + 0.5 * t)


def _gelu_and_grad(x):
    x2 = x * x
    t = jnp.tanh(x * (GELU_K0 + (GELU_K0 * GELU_K1) * x2))
    s = 0.5 + 0.5 * t
    dg = s + (x * (1.0 - t * t)) * (0.5 * GELU_K0 + (1.5 * GELU_K0 * GELU_K1) * x2)
    return x * s, dg


def _gate(x):
    return 0.5 + 0.5 * jnp.tanh(0.5 * x.astype(F32))


def _rms(x):
    r = lax.rsqrt(jnp.mean(x * x, axis=-1, keepdims=True) + NORM_EPS)
    return x * r, r


def _rms_bwd(dn, xhat, r):
    return r * (dn - xhat * jnp.mean(dn * xhat, axis=-1, keepdims=True))


def _col_sum(v):
    return jnp.sum(v, axis=0, keepdims=True)


def _shift_down(x, tail8, k):
    xs = pltpu.roll(x, k, 0)
    ts = pltpu.roll(tail8, k, 0)
    ridx = lax.broadcasted_iota(jnp.int32, tail8.shape, 0)
    head = jnp.where(ridx < k, ts, xs[0:SUBLANES])
    return jnp.concatenate([head, xs[SUBLANES:]], axis=0)


def _shift_up(x, head8, k):
    n = x.shape[0]
    xs = pltpu.roll(x, n - k, 0)
    hs = pltpu.roll(head8, SUBLANES - k, 0)
    ridx = lax.broadcasted_iota(jnp.int32, head8.shape, 0)
    last = jnp.where(ridx >= SUBLANES - k, hs, xs[n - SUBLANES:n])
    return jnp.concatenate([xs[:n - SUBLANES], last], axis=0)


def _scan_forward(a, b, carry):
    n, cols = a.shape
    groups = n // SUBLANES
    a = a.reshape(groups, SUBLANES, cols)
    b = b.reshape(groups, SUBLANES, cols)
    sub = lax.broadcasted_iota(jnp.int32, a.shape, 1)
    for s in (1, 2, 4):
        a_s = pltpu.roll(a, s, 1)
        b_s = pltpu.roll(b, s, 1)
        m = sub >= s
        b = jnp.where(m, a * b_s + b, b)
        a = jnp.where(m, a * a_s, a)
    out = []
    for g in range(groups):
        h = a[g] * carry + b[g]
        out.append(h)
        carry = h[SUBLANES - 1:SUBLANES]
    return jnp.concatenate(out, axis=0), carry


def _scan_backward(a, b, carry):
    n, cols = a.shape
    groups = n // SUBLANES
    a = a.reshape(groups, SUBLANES, cols)
    b = b.reshape(groups, SUBLANES, cols)
    sub = lax.broadcasted_iota(jnp.int32, a.shape, 1)
    for s in (1, 2, 4):
        a_s = pltpu.roll(a, SUBLANES - s, 1)
        b_s = pltpu.roll(b, SUBLANES - s, 1)
        m = sub < SUBLANES - s
        b = jnp.where(m, a * b_s + b, b)
        a = jnp.where(m, a * a_s, a)
    out = [None] * groups
    for g in reversed(range(groups)):
        h = a[g] * carry + b[g]
        out[g] = h
        carry = h[0:1]
    return jnp.concatenate(out, axis=0), carry


def _softplus_neg(lam):
    e = jnp.exp(-jnp.abs(lam))
    u = 1.0 + e
    log1p_e = jnp.where(u == 1.0, e, jnp.log(u) * (e / jnp.where(u == 1.0, 1.0, u - 1.0)))
    return jnp.maximum(-lam, 0.0) + log1p_e


def _lru_gates(xa, tail8, cw_ref, cb_ref, wr_ref, br_ref, wi_ref, bi_ref, lam_ref):
    cw = cw_ref[...]
    xc = cb_ref[...] + cw[0:1] * xa
    for k in range(1, CONV_WIDTH):
        xc = xc + cw[k:k + 1] * _shift_down(xa, tail8, k)
    xcb = xc.astype(BF16)
    pre_r, pre_i = [], []
    for h in range(HEADS):
        cols = slice(h * HEAD_DIM, (h + 1) * HEAD_DIM)
        pre_r.append(_dot(xcb[:, cols], wr_ref[h]))
        pre_i.append(_dot(xcb[:, cols], wi_ref[h]))
    r = jax.nn.sigmoid(jnp.concatenate(pre_r, axis=1) + br_ref[...])
    ig = jax.nn.sigmoid(jnp.concatenate(pre_i, axis=1) + bi_ref[...])
    _, a, mult, _ = _decay(r, lam_ref)
    return xc, r, ig, a, mult


def _decay(r, lam_ref):
    sp = _softplus_neg(lam_ref[...])
    log_a = ((-LRU_C) * sp) * r
    a = jnp.exp(log_a)
    th = jnp.tanh(log_a)
    q = (-2.0 * th) / (1.0 - th)
    inv = lax.rsqrt(q)
    return sp, a, jnp.where(q > 0.0, q * inv, 0.0), inv


class _Phase:
    def __init__(self, copies, n_sem, n_local, start=None, finish=None):
        self.copies, self.n_sem, self.n_local, self.start, self.finish = copies, n_sem, n_local, start, finish


class _Job:
    def __init__(self, inputs, out_shape, n_sem, copies, peers, aliases=None, n_local=0):
        self.inputs, self.out_shape = list(inputs), list(out_shape)
        self.aliases = dict(aliases or {})
        self.phases = [_Phase(copies, n_sem, n_local)]
        self.peers = tuple(peers)

    def then(self, make, at=None):
        nxt = make(self.out_shape)
        self.phases[-1].finish = at
        nxt.phases[0].start = at
        self.phases += nxt.phases
        self.peers = tuple(sorted(set(self.peers + nxt.peers)))
        return self


def _fused_call(body, jobs, *, name, grid, in_specs, out_specs, out_shape, scratch_shapes=(),
                input_output_aliases=None, compiler_params=None, n_prefetch=0, jobs_start_after=None):
    single = not isinstance(out_shape, (list, tuple))
    out_specs = [out_specs] if single else list(out_specs)
    out_shape = [out_shape] if single else list(out_shape)
    n_scr = len(scratch_shapes)
    in_specs, scratch_shapes = list(in_specs), list(scratch_shapes)
    n_in, n_out = len(in_specs), len(out_shape)
    aliases = dict(input_output_aliases or {})
    in_at, out_at, phases = [], [], []
    for q, job in enumerate(jobs):
        in_at.append(len(in_specs))
        out_at.append(len(out_shape))
        for i, o in job.aliases.items():
            aliases[n_prefetch + len(in_specs) + i] = len(out_shape) + o
        in_specs += [ANY] * len(job.inputs)
        out_specs += [ANY] * len(job.out_shape)
        out_shape += job.out_shape
        for k, phase in enumerate(job.phases):
            phases.append((q, k, phase, len(scratch_shapes)))
            scratch_shapes += [pltpu.SemaphoreType.DMA((phase.n_sem,)), pltpu.SemaphoreType.DMA((phase.n_sem,)),
                               pltpu.SemaphoreType.DMA((max(phase.n_local, 1),))]
    n_in_all, n_out_all = len(in_specs), len(out_shape)
    first_step, last_step = (0,) * len(grid), tuple(g - 1 for g in grid)

    def full_body(*refs):
        prefetch, refs = refs[:n_prefetch], refs[n_prefetch:]
        ins, outs, scr = refs[:n_in_all], refs[n_in_all:n_in_all + n_out_all], refs[n_in_all + n_out_all:]
        ids = [pl.program_id(a) for a in range(len(grid))]
        at_step = lambda step: functools.reduce(jnp.logical_and, [i == k for i, k in zip(ids, step)])

        def copies(q, k, phase, sem_at):
            job = jobs[q]
            mine = outs[out_at[q]:out_at[q] + len(job.out_shape)]
            return phase.copies(ins[in_at[q]:in_at[q] + len(job.inputs)] if k == 0 else mine, mine,
                                *scr[sem_at:sem_at + 3])

        def start(*phase):
            def go():
                sends, _, local = copies(*phase)
                for cp in local + sends:
                    cp.start()
            return go

        def finish(*phase):
            def go():
                sends, arrivals, local = copies(*phase)
                for cp in arrivals:
                    cp.wait_recv()
                for cp in sends:
                    cp.wait_send()
                for cp in local:
                    cp.wait()
            return go

        for phase in phases:
            if phase[2].start is None and jobs_start_after is None:
                pl.when(at_step(first_step))(start(*phase))
        body(*prefetch, *ins[:n_in], *outs[:n_out], *scr[:n_scr])
        for phase in phases:
            pl.when(at_step(phase[2].finish or last_step))(finish(*phase))
            nxt = phase[2].start or jobs_start_after
            if nxt is not None:
                pl.when(at_step(nxt))(start(*phase))

    if n_prefetch:
        layout = dict(grid_spec=pltpu.PrefetchScalarGridSpec(
            num_scalar_prefetch=n_prefetch, grid=grid, in_specs=in_specs, out_specs=out_specs,
            scratch_shapes=scratch_shapes))
    else:
        layout = dict(grid=grid, in_specs=in_specs, out_specs=out_specs, scratch_shapes=scratch_shapes)
    call = pl.pallas_call(
        full_body, name=name, out_shape=out_shape, input_output_aliases=aliases, compiler_params=compiler_params,
        **layout)

    def run(*args):
        res = call(*args, *[a for job in jobs for a in job.inputs])
        mine = res[0] if single else list(res[:n_out])
        return mine, [list(res[at:at + len(job.out_shape)]) for at, job in zip(out_at, jobs)]

    return run


def _fwd_in(x, g1, shards, order, jobs=()):
    t = x.shape[0]
    rows_per_step = min(IN_TILE, t)
    n_tiles = t // rows_per_step
    n = len(shards)
    halves = [s.shape[0] // 2 for s in shards]

    def body(order_ref, x_ref, g_ref, *refs):
        del order_ref
        ins, (z_ref, n_ref), outs = refs[:n], refs[n:n + 2], refs[n + 2:2 * n + 2]
        wbuf, nbuf, send, recv, local = refs[2 * n + 2:]
        s, i = pl.program_id(0), pl.program_id(1)
        x_, y_, c, chips = _place()
        near, far = _near_far(x_, y_, c)
        k_me = _chip_index(x_, y_)

        def block(w, chip, pc):
            return outs[w].at[_chip_index(*chip), pl.ds(pc * halves[w], halves[w]), :]

        def over_ici(w, j, landing):
            return pltpu.make_async_remote_copy(
                src_ref=ins[w].at[pl.ds(c * halves[w], halves[w]), :],
                dst_ref=block(w, chips[j] if landing else (x_, y_), c), send_sem=send.at[6 * w + j],
                recv_sem=recv.at[6 * w + j], device_id=(*chips[j], c), device_id_type=MESH)

        def onward(w, landing):
            blk = block(w, chips[2] if landing else near, c)
            return pltpu.make_async_remote_copy(
                src_ref=blk, dst_ref=blk, send_sem=send.at[6 * w + 2], recv_sem=recv.at[6 * w + 2],
                device_id=(*far, c), device_id_type=MESH)

        def to_sibling(w, j, landing):
            blk = block(w, chips[j], 1 - c if landing else c)
            return pltpu.make_async_remote_copy(
                src_ref=blk, dst_ref=blk, send_sem=send.at[6 * w + 3 + j], recv_sem=recv.at[6 * w + 3 + j],
                device_id=(x_, y_, 1 - c), device_id_type=MESH)

        own = [pltpu.make_async_copy(wbuf, outs[0].at[k_me], local.at[0])]
        own += [pltpu.make_async_copy(ins[w], outs[w].at[k_me], local.at[w]) for w in range(1, n)]

        @pl.when((s == 0) & (i == 0))
        def _():
            for j in range(2):
                for w in range(n):
                    over_ici(w, j, False).start()
            load = pltpu.make_async_copy(ins[0], wbuf, local.at[n])
            load.start()
            load.wait()
            for cp in own:
                cp.start()

        for j in range(N_CHIPS - 1):
            @pl.when((s == j + 1) & (i == 0))
            def _(j=j):
                if j == 0:
                    for k in range(2):
                        for w in range(n):
                            over_ici(w, k, True).wait_recv()
                    for w in range(n):
                        onward(w, False).start()
                    for k in range(2):
                        for w in range(n):
                            to_sibling(w, k, False).start()
                    own[0].wait()
                if j == 2:
                    for w in range(n):
                        onward(w, True).wait_recv()
                    for w in range(n):
                        to_sibling(w, j, False).start()
                for w in range(n):
                    to_sibling(w, j, True).wait_recv()
                load = pltpu.make_async_copy(outs[0].at[_chip_index(*chips[j])], wbuf, local.at[n])
                load.start()
                load.wait()

        rows = pl.ds(pl.multiple_of(i * rows_per_step, rows_per_step), rows_per_step)

        @pl.when(s == 0)
        def _():
            xhat, _ = _rms(x_ref[...])
            nrm = (xhat * g_ref[...]).astype(BF16)
            nbuf[rows, :] = nrm
            n_ref[...] = nrm

        z_ref[...] = _dot(nbuf[rows, :], wbuf[...]).astype(BF16)

        @pl.when((s == N_CHIPS - 1) & (i == n_tiles - 1))
        def _():
            for j in range(N_CHIPS - 1):
                for w in range(n):
                    (over_ici(w, j, False) if j < 2 else onward(w, False)).wait_send()
                    to_sibling(w, j, False).wait_send()
            for cp in own[1:]:
                cp.wait()

    once = lambda s, i, order: (jnp.where(s == 0, i, n_tiles - 1), 0)
    (z, n1, *stacked), job_outs = _fused_call(
        body, jobs, name="fwd_in", grid=(N_CHIPS, n_tiles), n_prefetch=1,
        in_specs=[pl.BlockSpec((rows_per_step, D_MODEL), once), _const((1, D_MODEL))] + [ANY] * n,
        out_specs=[pl.BlockSpec((rows_per_step, IN_SHARD), lambda s, i, order: (i, order[s])),
                   pl.BlockSpec((rows_per_step, D_MODEL), once)] + [ANY] * n,
        out_shape=[jax.ShapeDtypeStruct((t, D_IN), BF16), jax.ShapeDtypeStruct((t, D_MODEL), BF16)]
        + [jax.ShapeDtypeStruct((N_CHIPS,) + s.shape, s.dtype) for s in shards],
        scratch_shapes=[pltpu.VMEM(shards[0].shape, BF16), pltpu.VMEM((t, D_MODEL), BF16),
                        pltpu.SemaphoreType.DMA((6 * n,)),
                        pltpu.SemaphoreType.DMA((6 * n,)), pltpu.SemaphoreType.DMA((n + 1,))],
        compiler_params=_params(2), jobs_start_after=(1, 0),
    )(order, x, g1, *shards)
    return (z, n1, stacked), job_outs


def _fwd_lru(z, conv_w, conv_b, wr, br, wi, bi, lam, jobs=()):
    t = z.shape[0]

    def body(xa_ref, ga_ref, cw_ref, cb_ref, wr_ref, br_ref, wi_ref, bi_ref, lam_ref, ya_ref, h_ref, xc_ref, r_ref,
             ig_ref, tail_ref, carry_ref):
        @pl.when(pl.program_id(0) == 0)
        def _():
            tail_ref[...] = jnp.zeros_like(tail_ref)
            carry_ref[...] = jnp.zeros_like(carry_ref)

        xa = xa_ref[...].astype(F32)
        xc, r, ig, a, mult = _lru_gates(xa, tail_ref[...], cw_ref, cb_ref, wr_ref, br_ref, wi_ref, bi_ref, lam_ref)
        tail_ref[...] = xa[SEQ_TILE - SUBLANES:]
        xc_ref[...], r_ref[...], ig_ref[...] = xc, r, ig
        h, carry = _scan_forward(a, xc * ig * mult, carry_ref[...])
        carry_ref[...] = carry
        h_ref[...] = h
        ya_ref[...] = (h * _gelu(ga_ref[...].astype(F32))).astype(BF16)

    tile = lambda j: pl.BlockSpec((SEQ_TILE, D_MODEL), lambda i: (i, j))
    return _fused_call(
        body, jobs, name="fwd_lru", grid=(t // SEQ_TILE,),
        in_specs=[tile(0), tile(1), _const((CONV_WIDTH, D_MODEL)), _const((1, D_MODEL)),
                  _resident((HEADS, HEAD_DIM, HEAD_DIM)), _const((1, D_MODEL)),
                  _resident((HEADS, HEAD_DIM, HEAD_DIM)), _const((1, D_MODEL)), _const((1, D_MODEL))],
        out_specs=[tile(0)] * 5,
        out_shape=[jax.ShapeDtypeStruct((t, D_MODEL), BF16)] + [jax.ShapeDtypeStruct((t, D_MODEL), F32)] * 4,
        scratch_shapes=[pltpu.VMEM((SUBLANES, D_MODEL), F32), pltpu.VMEM((1, D_MODEL), F32)],
        compiler_params=_params(),
    )(z, z, conv_w, conv_b, wr, br, wi, bi, lam)


def _sgu_forward_parts(ub, vb, lg_ref, lb_ref):
    u, du = _gelu_and_grad(ub.astype(F32))
    vg, dvg = _gelu_and_grad(vb.astype(F32))
    mu = jnp.mean(vg, axis=-1, keepdims=True)
    d = vg - mu
    rstd = lax.rsqrt(jnp.mean(d * d, axis=-1, keepdims=True) + LN_EPS)
    vhat = d * rstd
    vn = (vhat * lg_ref[...] + lb_ref[...]).astype(BF16)
    return u, du, dvg, rstd, vhat, vn


def _causal_mask():
    rows = lax.broadcasted_iota(jnp.int32, (CHUNK, CHUNK), 0)
    cols = lax.broadcasted_iota(jnp.int32, (CHUNK, CHUNK), 1)
    return rows >= cols


def _fwd_sgu_merge(ya, z, x, ln_g, ln_b, w_s, bias_full, w_oa, w_ob, w_out, g2, jobs=()):
    t = x.shape[0]

    def body(ya_ref, ub_ref, vb_ref, m_ref, x_ref, lg_ref, lb_ref, ws_ref, bias_ref, woa_ref, wob_ref, wout_ref, g_ref,
             yb_ref, pa_ref, pb_ref, h1_ref, n2_ref):
        u, _, _, _, _, vn = _sgu_forward_parts(ub_ref[...], vb_ref[...], lg_ref, lb_ref)
        mask = _causal_mask()
        wm = [jnp.where(mask, ws_ref[g], 0.0).astype(BF16) for g in range(GROUPS)]
        for c in range(SEQ_TILE // CHUNK):
            rows = slice(c * CHUNK, (c + 1) * CHUNK)
            for g in range(GROUPS):
                cols = slice(g * GROUP_DIM, (g + 1) * GROUP_DIM)
                sp = _dot(wm[g], vn[rows, cols]) + bias_ref[:, cols]
                yb_ref[rows, cols] = (u[rows, cols] * sp).astype(BF16)
        pa = _dot(ya_ref[...], woa_ref[...])
        pb = _dot(yb_ref[...], wob_ref[...])
        pa_ref[...] = pa
        pb_ref[...] = pb
        merged = _gate(m_ref[:, :D_MODEL]) * pa + _gate(m_ref[:, D_MODEL:]) * pb
        h1 = x_ref[...] + _dot(merged.astype(BF16), wout_ref[...])
        h1_ref[...] = h1
        xhat, _ = _rms(h1)
        n2_ref[...] = (xhat * g_ref[...]).astype(BF16)

    tile = lambda j: pl.BlockSpec((SEQ_TILE, D_MODEL), lambda i: (i, j))
    sq = _resident((D_MODEL, D_MODEL))
    vec = _const((1, D_MODEL))
    bf, f32 = jax.ShapeDtypeStruct((t, D_MODEL), BF16), jax.ShapeDtypeStruct((t, D_MODEL), F32)
    return _fused_call(
        body, jobs, name="fwd_sgu_merge", grid=(t // SEQ_TILE,),
        in_specs=[tile(0), tile(2), tile(3), pl.BlockSpec((SEQ_TILE, 2 * D_MODEL), lambda i: (i, 2)), tile(0), vec, vec,
                  _const((GROUPS, CHUNK, CHUNK)), _const((CHUNK, D_MODEL)), sq, sq, sq, vec],
        out_specs=[tile(0)] * 5,
        out_shape=[bf, f32, f32, f32, bf],
        compiler_params=_params(),
    )(ya, z, z, z, x, ln_g, ln_b, w_s, bias_full, w_oa, w_ob, w_out, g2)


def _mlp(n2, h1, target, w_up_st, w_down, g2, g3, jobs=()):
    t = n2.shape[0]

    def body(n2_ref, h1_ref, tgt_ref, wup_ref, wdown_ref, g2_ref, g3_ref, act_ref, dup_ref, dh2b_ref, dh1_ref,
             loss_ref, dg3_ref, dg2_ref, relu_ref):
        @pl.when(pl.program_id(0) == 0)
        def _():
            for ref in (loss_ref, dg3_ref, dg2_ref):
                ref[...] = jnp.zeros_like(ref)

        n2 = n2_ref[...]
        h1 = h1_ref[...]
        h2 = h1
        for k in range(N_CHIPS):
            cols = slice(k * D_MODEL, (k + 1) * D_MODEL)
            r = jnp.maximum(_dot(n2, wup_ref[k]), 0.0)
            relu_ref[:, cols] = r
            act = (r * r).astype(BF16)
            act_ref[:, cols] = act
            h2 = h2 + _dot(act, wdown_ref[cols, :])
        xhat, r3 = _rms(h2)
        diff = xhat * g3_ref[...] - tgt_ref[...]
        sq = jnp.sum(diff * diff, axis=1, keepdims=True)
        loss_ref[...] = loss_ref[...] + (0.5 / D_MODEL) * jnp.sum(sq, axis=0, keepdims=True)
        dy = diff * (1.0 / D_MODEL)
        dg3_ref[...] = dg3_ref[...] + _col_sum(dy * xhat)
        dh2 = _rms_bwd(dy * g3_ref[...], xhat, r3)
        dh2b = dh2.astype(BF16)
        dh2b_ref[...] = dh2b
        dn2 = jnp.zeros((SEQ_TILE, D_MODEL), F32)
        for k in range(N_CHIPS):
            cols = slice(k * D_MODEL, (k + 1) * D_MODEL)
            dup = (_dot_nt(dh2b, wdown_ref[cols, :]) * (2.0 * relu_ref[:, cols])).astype(BF16)
            dup_ref[:, cols] = dup
            dn2 = dn2 + _dot_nt(dup, wup_ref[k])
        xhat, r2 = _rms(h1)
        dg2_ref[...] = dg2_ref[...] + _col_sum(dn2 * xhat)
        dh1_ref[...] = dh2 + _rms_bwd(dn2 * g2_ref[...], xhat, r2)

    tile = pl.BlockSpec((SEQ_TILE, D_MODEL), lambda i: (i, 0))
    wide = pl.BlockSpec((SEQ_TILE, D_FF), lambda i: (i, 0))
    vec = _const((1, D_MODEL))
    vec_shape = jax.ShapeDtypeStruct((1, D_MODEL), F32)
    return _fused_call(
        body, jobs, name="mlp", grid=(t // SEQ_TILE,),
        in_specs=[tile, tile, tile, _resident((N_CHIPS, D_MODEL, D_MODEL)), _resident((D_FF, D_MODEL)), vec, vec],
        out_specs=[wide, wide, tile, tile, _const((SUBLANES, 128)), vec, vec],
        out_shape=[jax.ShapeDtypeStruct((t, D_FF), BF16), jax.ShapeDtypeStruct((t, D_FF), BF16),
                   jax.ShapeDtypeStruct((t, D_MODEL), BF16), jax.ShapeDtypeStruct((t, D_MODEL), F32),
                   jax.ShapeDtypeStruct((SUBLANES, 128), F32), vec_shape, vec_shape],
        scratch_shapes=[pltpu.VMEM((SEQ_TILE, D_FF), F32)],
        compiler_params=_params(),
    )(n2, h1, target, w_up_st, w_down, g2, g3)


def _bwd_mix(dh1, pa, pb, z, h, xc, r, ig, w_oa, w_ob, w_out, ln_g, ln_b, w_s, bias_full, conv_w, wr, wi, lam, jobs=()):
    t = dh1.shape[0]
    n_tiles = t // SEQ_TILE
    per_tile = SEQ_TILE // SUBLANES

    def merge_part(dh1_ref, pa_ref, pb_ref, m_ref, woa_ref, wob_ref, wout_ref, dz_ref, dya_ref, dyb_ref, mg_ref,
                   dpa_ref, dpb_ref, dh1b_ref):
        dh1b = dh1_ref[...].astype(BF16)
        dh1b_ref[...] = dh1b
        dm = _dot_nt(dh1b, wout_ref[...])
        pa = pa_ref[...]
        pb = pb_ref[...]
        sa = _gate(m_ref[:, :D_MODEL])
        sb = _gate(m_ref[:, D_MODEL:])
        mg_ref[...] = (sa * pa + sb * pb).astype(BF16)
        dpa = dm * sa
        dpb = dm * sb
        dz_ref[:, :D_MODEL] = ((dpa * pa) * (1.0 - sa)).astype(BF16)
        dz_ref[:, D_MODEL:] = ((dpb * pb) * (1.0 - sb)).astype(BF16)
        dpa = dpa.astype(BF16)
        dpb = dpb.astype(BF16)
        dpa_ref[...] = dpa
        dpb_ref[...] = dpb
        dya_ref[...] = _dot_nt(dpa, woa_ref[...])
        dyb_ref[...] = _dot_nt(dpb, wob_ref[...])

    def sgu_part(dyb_ref, ub_ref, vb_ref, lg_ref, lb_ref, ws_ref, bias_ref, dz_ref, dlg_ref, dlb_ref, dws_ref, dbs_ref,
                 dvn_ref, dsp_acc):
        i = pl.program_id(0)

        @pl.when(i == 0)
        def _():
            dlg_ref[...] = jnp.zeros_like(dlg_ref)
            dlb_ref[...] = jnp.zeros_like(dlb_ref)
            dws_ref[...] = jnp.zeros_like(dws_ref)
            dsp_acc[...] = jnp.zeros_like(dsp_acc)

        u, du, dvg, rstd, vhat, vn = _sgu_forward_parts(ub_ref[...], vb_ref[...], lg_ref, lb_ref)
        dyb = dyb_ref[...]
        mask = _causal_mask()
        wm = [jnp.where(mask, ws_ref[g], 0.0).astype(BF16) for g in range(GROUPS)]
        for c in range(SEQ_TILE // CHUNK):
            rows = slice(c * CHUNK, (c + 1) * CHUNK)
            for g in range(GROUPS):
                cols = slice(g * GROUP_DIM, (g + 1) * GROUP_DIM)
                vn_blk = vn[rows, cols]
                sp = _dot(wm[g], vn_blk) + bias_ref[:, cols]
                dyb_blk = dyb[rows, cols]
                dz_ref[rows, cols] = (dyb_blk * sp * du[rows, cols]).astype(BF16)
                dsp = dyb_blk * u[rows, cols]
                dsp_acc[:, cols] = dsp_acc[:, cols] + dsp
                dspb = dsp.astype(BF16)
                dvn_ref[rows, cols] = _dot_tn(wm[g], dspb)
                wcols = slice(g * CHUNK, (g + 1) * CHUNK)
                dws_ref[:, wcols] = dws_ref[:, wcols] + jnp.where(mask, _dot_nt(dspb, vn_blk), 0.0)
        dvn = dvn_ref[...]
        dlg_ref[...] = dlg_ref[...] + _col_sum(dvn * vhat)
        dlb_ref[...] = dlb_ref[...] + _col_sum(dvn)
        dvhat = dvn * lg_ref[...]
        dvgel = rstd * (dvhat - jnp.mean(dvhat, axis=-1, keepdims=True)
                        - vhat * jnp.mean(dvhat * vhat, axis=-1, keepdims=True))
        dz_ref[:, D_MODEL:] = (dvgel * dvg).astype(BF16)

        @pl.when(i == n_tiles - 1)
        def _():
            lane = lax.broadcasted_iota(jnp.int32, (CHUNK, 128), 1)
            out = jnp.zeros((CHUNK, 128), F32)
            for g in range(GROUPS):
                s = jnp.sum(dsp_acc[:, g * GROUP_DIM:(g + 1) * GROUP_DIM], axis=1, keepdims=True)
                out = out + jnp.where(lane == g, s, 0.0)
            dbs_ref[...] = out

    def lru_part(dya_ref, xa_ref, ga_ref, h_ref, h_prev_ref, xc_ref, r_ref, ig_ref, cw_ref, wr_ref, wi_ref, lam_ref,
                 dz_ref, dcw_ref, dcb_ref, dwr_ref, dbr_ref, dwi_ref, dbi_ref, dlam_ref, lam_carry, dxc_head):
        i = pl.program_id(0)

        @pl.when(i == 0)
        def _():
            for ref in (dcw_ref, dcb_ref, dwr_ref, dbr_ref, dwi_ref, dbi_ref, dlam_ref, lam_carry, dxc_head):
                ref[...] = jnp.zeros_like(ref)

        first_tile = i == n_tiles - 1
        h_tail = jnp.where(first_tile, 0.0, h_prev_ref[...])
        xc, r, ig = xc_ref[...], r_ref[...], ig_ref[...]
        xcb = xc.astype(BF16)
        sp, a, mult, inv_mult = _decay(r, lam_ref)
        h = h_ref[...]
        h_prev = _shift_down(h, h_tail, 1)
        dya = dya_ref[...]
        gg, dgg = _gelu_and_grad(ga_ref[...].astype(F32))
        dz_ref[:, D_MODEL:] = (dya * h * dgg).astype(BF16)
        ones = jnp.ones((SUBLANES, D_MODEL), F32)
        lam_t, lam_first = _scan_backward(_shift_up(a, ones, 1), dya * gg, lam_carry[...])
        lam_carry[...] = a[0:1] * lam_first
        lam_ig = lam_t * ig
        dxc_direct = lam_ig * mult
        dmult = lam_ig * xc
        dla = a * (lam_t * h_prev - (dmult * a) * inv_mult)
        dla_r = dla * r
        dlam_ref[...] = dlam_ref[...] + _col_sum(dla_r) * (LRU_C * jax.nn.sigmoid(-lam_ref[...]))
        dpr = (dla_r * ((-LRU_C) * sp)) * (1.0 - r)
        dpi = (dxc_direct * xc) * (1.0 - ig)
        dbr_ref[...] = dbr_ref[...] + _col_sum(dpr)
        dbi_ref[...] = dbi_ref[...] + _col_sum(dpi)
        dprb = dpr.astype(BF16)
        dpib = dpi.astype(BF16)
        dxc_gate = []
        for hd in range(HEADS):
            cols = slice(hd * HEAD_DIM, (hd + 1) * HEAD_DIM)
            dxc_gate.append(_dot_nt(dprb[:, cols], wr_ref[hd]) + _dot_nt(dpib[:, cols], wi_ref[hd]))
            dwr_ref[hd] = dwr_ref[hd] + _dot_tn(xcb[:, cols], dprb[:, cols])
            dwi_ref[hd] = dwi_ref[hd] + _dot_tn(xcb[:, cols], dpib[:, cols])
        dxc = dxc_direct + jnp.concatenate(dxc_gate, axis=1)
        dcb_ref[...] = dcb_ref[...] + _col_sum(dxc)
        cw = cw_ref[...]
        head = dxc_head[...]
        xa = xa_ref[...].astype(F32)
        dxa = cw[0:1] * dxc
        dcw_ref[0:1, :] = dcw_ref[0:1, :] + _col_sum(dxc * xa)
        for k in range(1, CONV_WIDTH):
            dxc_k = _shift_up(dxc, head, k)
            dxa = dxa + cw[k:k + 1] * dxc_k
            dcw_ref[k:k + 1, :] = dcw_ref[k:k + 1, :] + _col_sum(dxc_k * xa)
        dxc_head[...] = dxc[0:SUBLANES]
        dz_ref[:, :D_MODEL] = dxa.astype(BF16)

    def body(dh1_ref, pa_ref, pb_ref, z_ref, h_ref, h_prev_ref, xc_ref, r_ref, ig_ref, woa_ref, wob_ref, wout_ref,
             lg_ref, lb_ref, ws_ref, bias_ref, cw_ref, wr_ref, wi_ref, lam_ref, dz_ref, mg_ref, dpa_ref, dpb_ref,
             dh1b_ref, dlg_ref, dlb_ref, dws_ref, dbs_ref, dcw_ref, dcb_ref, dwr_ref, dbr_ref, dwi_ref, dbi_ref,
             dlam_ref, dya_ref, dyb_ref, dvn_ref, dsp_acc, lam_carry, dxc_head):
        def cols(ref, first, count):
            return ref.at[:, pl.ds(first * D_MODEL, count * D_MODEL)]

        merge_part(dh1_ref, pa_ref, pb_ref, cols(z_ref, 4, 2), woa_ref, wob_ref, wout_ref, cols(dz_ref, 4, 2), dya_ref,
                   dyb_ref, mg_ref, dpa_ref, dpb_ref, dh1b_ref)
        sgu_part(dyb_ref, cols(z_ref, 2, 1), cols(z_ref, 3, 1), lg_ref, lb_ref, ws_ref, bias_ref, cols(dz_ref, 2, 2),
                 dlg_ref, dlb_ref, dws_ref, dbs_ref, dvn_ref, dsp_acc)
        lru_part(dya_ref, cols(z_ref, 0, 1), cols(z_ref, 1, 1), h_ref, h_prev_ref, xc_ref, r_ref, ig_ref, cw_ref, wr_ref,
                 wi_ref, lam_ref, cols(dz_ref, 0, 2), dcw_ref, dcb_ref, dwr_ref, dbr_ref, dwi_ref, dbi_ref, dlam_ref,
                 lam_carry, dxc_head)

    rev = lambda i: n_tiles - 1 - i
    tile = pl.BlockSpec((SEQ_TILE, D_MODEL), lambda i: (rev(i), 0))
    row = pl.BlockSpec((SEQ_TILE, D_IN), lambda i: (rev(i), 0))
    prev8 = pl.BlockSpec((SUBLANES, D_MODEL), lambda i: (jnp.maximum(rev(i) * per_tile - 1, 0), 0))
    vec = _const((1, D_MODEL))
    sq = _resident((D_MODEL, D_MODEL))
    gate_w = _resident((HEADS, HEAD_DIM, HEAD_DIM))
    gate_acc = _const((HEADS, HEAD_DIM, HEAD_DIM))
    vec_shape = jax.ShapeDtypeStruct((1, D_MODEL), F32)
    gate_shape = jax.ShapeDtypeStruct((HEADS, HEAD_DIM, HEAD_DIM), F32)
    act_bf = jax.ShapeDtypeStruct((t, D_MODEL), BF16)
    return _fused_call(
        body, jobs, name="bwd_mix", grid=(n_tiles,),
        in_specs=[tile, tile, tile, row, tile, prev8, tile, tile, tile, sq, sq, sq, vec, vec,
                  _const((GROUPS, CHUNK, CHUNK)), _const((CHUNK, D_MODEL)), _const((CONV_WIDTH, D_MODEL)), gate_w, gate_w,
                  vec],
        out_specs=[row, tile, tile, tile, tile, vec, vec, _const((CHUNK, GROUPS * CHUNK)), _const((CHUNK, 128)),
                   _const((SUBLANES, D_MODEL)), vec, gate_acc, vec, gate_acc, vec, vec],
        out_shape=[jax.ShapeDtypeStruct((t, D_IN), BF16), act_bf, act_bf, act_bf, act_bf, vec_shape, vec_shape,
                   jax.ShapeDtypeStruct((CHUNK, GROUPS * CHUNK), F32), jax.ShapeDtypeStruct((CHUNK, 128), F32),
                   jax.ShapeDtypeStruct((SUBLANES, D_MODEL), F32), vec_shape, gate_shape, vec_shape, gate_shape,
                   vec_shape, vec_shape],
        scratch_shapes=[pltpu.VMEM((SEQ_TILE, D_MODEL), F32), pltpu.VMEM((SEQ_TILE, D_MODEL), F32),
                        pltpu.VMEM((SEQ_TILE, D_MODEL), F32), pltpu.VMEM((CHUNK, D_MODEL), F32),
                        pltpu.VMEM((1, D_MODEL), F32), pltpu.VMEM((SUBLANES, D_MODEL), F32)],
        compiler_params=_params(),
    )(dh1, pa, pb, z, h, h, xc, r, ig, w_oa, w_ob, w_out, ln_g, ln_b, w_s, bias_full, conv_w, wr, wi, lam)


def _bwd_in(dz, x, dh1, w_in_st, g1, jobs=()):
    t = x.shape[0]

    def body(dz_ref, x_ref, dh1_ref, w_ref, g_ref, dx_ref, dg1_ref):
        @pl.when(pl.program_id(0) == 0)
        def _():
            dg1_ref[...] = jnp.zeros_like(dg1_ref)

        dn1 = jnp.zeros((MM_TILE, D_MODEL), F32)
        for k in range(N_CHIPS):
            dn1 = dn1 + _dot_nt(dz_ref[:, k * IN_SHARD:(k + 1) * IN_SHARD], w_ref[k])
        xhat, r1 = _rms(x_ref[...])
        dg1_ref[...] = dg1_ref[...] + _col_sum(dn1 * xhat)
        dx_ref[...] = dh1_ref[...] + _rms_bwd(dn1 * g_ref[...], xhat, r1)

    tile = pl.BlockSpec((MM_TILE, D_MODEL), lambda i: (i, 0))
    return _fused_call(
        body, jobs, name="bwd_in", grid=(t // MM_TILE,),
        in_specs=[pl.BlockSpec((MM_TILE, D_IN), lambda i: (i, 0)), tile, tile,
                  _resident((N_CHIPS, D_MODEL, IN_SHARD)), _const((1, D_MODEL))],
        out_specs=[tile, _const((1, D_MODEL))],
        out_shape=[jax.ShapeDtypeStruct((t, D_MODEL), F32), jax.ShapeDtypeStruct((1, D_MODEL), F32)],
        compiler_params=_params(),
    )(dz, x, dh1, w_in_st, g1)


def _weight_grad(name, a, b, n_blocks, a_varies, b_varies, width, jobs=()):
    t = a.shape[0]
    rows = min(DW_TILE, t)
    n_t = t // rows

    def body(a_ref, b_ref, o_ref, acc_ref):
        s = pl.program_id(1)
        part = _dot_tn(a_ref[...], b_ref[...])

        @pl.when(s == 0)
        def _():
            acc_ref[...] = part

        @pl.when(s > 0)
        def _():
            acc_ref[...] = acc_ref[...] + part

        @pl.when(s == n_t - 1)
        def _():
            o_ref[...] = acc_ref[...].astype(BF16)

    return _fused_call(
        body, jobs, name=name, grid=(n_blocks, n_t),
        in_specs=[pl.BlockSpec((rows, D_MODEL), (lambda j, s: (s, j)) if a_varies else (lambda j, s: (s, 0))),
                  pl.BlockSpec((rows, width), (lambda j, s: (s, j)) if b_varies else (lambda j, s: (s, 0)))],
        out_specs=pl.BlockSpec((None, D_MODEL, width), lambda j, s: (j, 0, 0)),
        out_shape=jax.ShapeDtypeStruct((n_blocks, D_MODEL, width), BF16),
        scratch_shapes=[pltpu.VMEM((D_MODEL, width), F32)],
        compiler_params=_params(2),
    )(a, b)


def _weight_grads_square(name, pairs, jobs=()):
    n = len(pairs)
    t = pairs[0][0].shape[0]
    rows = min(2 * MM_TILE, t)
    n_t = t // rows

    def body(*refs):
        ins, outs, accs = refs[:2 * n], refs[2 * n:3 * n], refs[3 * n:]
        s = pl.program_id(0)
        for k in range(n):
            part = _dot_tn(ins[2 * k][...], ins[2 * k + 1][...])

            @pl.when(s == 0)
            def _(k=k, part=part):
                accs[k][...] = part

            @pl.when(s > 0)
            def _(k=k, part=part):
                accs[k][...] = accs[k][...] + part

            @pl.when(s == n_t - 1)
            def _(k=k):
                outs[k][...] = accs[k][...].astype(BF16)

    tile = pl.BlockSpec((rows, D_MODEL), lambda s: (s, 0))
    return _fused_call(
        body, jobs, name=name, grid=(n_t,), in_specs=[tile] * (2 * n), out_specs=[_const((D_MODEL, D_MODEL))] * n,
        out_shape=[jax.ShapeDtypeStruct((D_MODEL, D_MODEL), BF16)] * n,
        scratch_shapes=[pltpu.VMEM((D_MODEL, D_MODEL), F32)] * n,
        compiler_params=_params(),
    )(*[x for pair in pairs for x in pair])


def _place():
    x, y, c = lax.axis_index("x"), lax.axis_index("y"), lax.axis_index("c")
    other_chips = [(1 - x, y), (x, 1 - y), (1 - x, 1 - y)]
    return x, y, c, other_chips


def _chip_index(px, py):
    return 2 * px + py


ANY = pl.BlockSpec(memory_space=pl.ANY)
SIBLING = ((0, 0, 1),)
NEIGHBOURS = ((1, 0, 0), (0, 1, 0))
OTHER_CHIPS = NEIGHBOURS + ((1, 1, 0),)


def _near_far(x, y, c):
    return (x ^ (1 - c), y ^ c), (x ^ c, y ^ (1 - c))


def _gather_near_job(shards):
    n = len(shards)
    halves = [s.shape[0] // 2 for s in shards]

    def copies(ins, outs, send, recv, local):
        x, y, c, _ = _place()
        near, _ = _near_far(x, y, c)

        def block(w, chip, pc):
            return outs[w].at[_chip_index(*chip), pl.ds(pc * halves[w], halves[w]), :]

        def copy(w, k, chip, pc, to, src=None):
            return pltpu.make_async_remote_copy(
                src_ref=block(w, chip, pc) if src is None else src, dst_ref=block(w, chip, pc),
                send_sem=send.at[2 * w + k], recv_sem=recv.at[2 * w + k], device_id=to, device_id_type=MESH)

        sends, arrivals, own = [], [], []
        for w in range(n):
            src = ins[w].at[pl.ds(c * halves[w], halves[w]), :]
            own.append(pltpu.make_async_copy(src, block(w, (x, y), c), local.at[w]))
            sends += [copy(w, 0, (x, y), c, (*near, c), src), copy(w, 1, (x, y), c, (x, y, 1 - c), src)]
            arrivals += [copy(w, 0, near, c, (x, y, c)), copy(w, 1, (x, y), 1 - c, (x, y, c))]
        return sends, arrivals, own

    return _Job(shards, [jax.ShapeDtypeStruct((N_CHIPS,) + s.shape, s.dtype) for s in shards], 2 * n, copies,
                NEIGHBOURS + SIBLING, n_local=n)


def _gather_far_job(stacked):
    n = len(stacked)
    halves = [s.shape[1] // 2 for s in stacked]

    def copies(ins, outs, send, recv, local):
        del ins, local
        x, y, c, _ = _place()
        near, far = _near_far(x, y, c)

        def copy(w, k, chip):
            blk = outs[w].at[_chip_index(*chip), pl.ds(c * halves[w], halves[w]), :]
            return pltpu.make_async_remote_copy(
                src_ref=blk, dst_ref=blk, send_sem=send.at[2 * w + k], recv_sem=recv.at[2 * w + k],
                device_id=(*far, c), device_id_type=MESH)

        sends = [copy(w, k, chip) for w in range(n) for k, chip in enumerate(((x, y), near))]
        arrivals = [copy(w, k, chip) for w in range(n) for k, chip in enumerate((far, (1 - x, 1 - y)))]
        return sends, arrivals, []

    return _Job(stacked, [jax.ShapeDtypeStruct(s.shape, s.dtype) for s in stacked], 2 * n, copies, NEIGHBOURS,
                aliases={w: w for w in range(n)})


def _gather_pass_job(stacked):
    n = len(stacked)
    halves = [s.shape[1] // 2 for s in stacked]

    def copies(ins, outs, send, recv, local):
        del ins, local
        x, y, c, chips = _place()

        def copy(w, j, chip, pc, to):
            blk = outs[w].at[_chip_index(*chip), pl.ds(pc * halves[w], halves[w]), :]
            return pltpu.make_async_remote_copy(
                src_ref=blk, dst_ref=blk, send_sem=send.at[3 * w + j], recv_sem=recv.at[3 * w + j], device_id=to,
                device_id_type=MESH)

        sends = [copy(w, j, chip, c, (x, y, 1 - c)) for w in range(n) for j, chip in enumerate(chips)]
        arrivals = [copy(w, j, chip, 1 - c, (x, y, c)) for w in range(n) for j, chip in enumerate(chips)]
        return sends, arrivals, []

    return _Job(stacked, [jax.ShapeDtypeStruct(s.shape, s.dtype) for s in stacked], 3 * n, copies, SIBLING,
                aliases={w: w for w in range(n)})


def _gather_small_job(block):
    def copies(ins, outs, send, recv, local):
        x, y, c, chips = _place()

        def copy(j, chip_from, to):
            return pltpu.make_async_remote_copy(
                src_ref=ins[0], dst_ref=outs[0].at[_chip_index(*chip_from)], send_sem=send.at[j],
                recv_sem=recv.at[j], device_id=to, device_id_type=MESH)

        own = [pltpu.make_async_copy(ins[0], outs[0].at[_chip_index(x, y)], local.at[0])]
        sends = [copy(j, (x, y), (*chip, c)) for j, chip in enumerate(chips)]
        arrivals = [copy(j, chip, (x, y, c)) for j, chip in enumerate(chips)]
        return sends, arrivals, own

    return _Job([block], [jax.ShapeDtypeStruct((N_CHIPS,) + block.shape, block.dtype)], 3, copies, OTHER_CHIPS,
                n_local=1)


def _pair_send_job(grads):
    n = len(grads)
    halves = [g.shape[1] // 2 for g in grads]

    def copies(ins, outs, send, recv, local):
        del local
        x, y, c, _ = _place()
        sends = [pltpu.make_async_remote_copy(
            src_ref=ins[w].at[:, pl.ds((1 - c) * halves[w], halves[w]), :], dst_ref=outs[w], send_sem=send.at[w],
            recv_sem=recv.at[w], device_id=(x, y, 1 - c), device_id_type=MESH) for w in range(n)]
        return sends, sends, []

    return _Job(grads, [jax.ShapeDtypeStruct((N_CHIPS, h, g.shape[2]), g.dtype) for g, h in zip(grads, halves)], n,
                copies, SIBLING)


ROW_STEPS = 4
PAIR_STEPS = 2


def _pair_add(name, core, mine, theirs):
    n = len(mine)

    def body(core_ref, *refs):
        del core_ref
        for a_ref, b_ref, o_ref in zip(refs[:n], refs[n:2 * n], refs[2 * n:]):
            o_ref[...] = (a_ref[...].astype(F32) + b_ref[...].astype(F32)).astype(BF16)

    def half(a):
        return pl.BlockSpec((None, None, a.shape[2] // PAIR_STEPS, a.shape[3]),
                            lambda k, r, core_ref: (k, core_ref[0], r, 0))

    def block(b):
        return pl.BlockSpec((None, b.shape[1] // PAIR_STEPS, b.shape[2]), lambda k, r, core_ref: (k, r, 0))

    return pl.pallas_call(
        body, name=name,
        grid_spec=pltpu.PrefetchScalarGridSpec(
            num_scalar_prefetch=1, grid=(N_CHIPS, PAIR_STEPS),
            in_specs=[half(a) for a in mine] + [block(b) for b in theirs], out_specs=[block(b) for b in theirs]),
        out_shape=[jax.ShapeDtypeStruct(b.shape, BF16) for b in theirs],
        compiler_params=_params(2),
    )(core, *mine, *theirs)


def _sequencer_call(name, collective_id, job):
    steps, peers = job.phases, job.peers
    ins = [jax.new_ref(a, memory_space=pltpu.MemorySpace.HBM) for a in job.inputs]
    outs = [ins[{o: i for i, o in job.aliases.items()}[k]] if k in job.aliases.values()
            else jax.empty_ref(shape, memory_space=pltpu.MemorySpace.HBM) for k, shape in enumerate(job.out_shape)]
    sems = [pltpu.SemaphoreType.DMA((n,)) for step in steps for n in (step.n_sem, step.n_sem, max(step.n_local, 1))]

    @pl.kernel(mesh=plsc.ScalarSubcoreMesh(axis_name="sequencer", num_cores=1), name=name, scratch_types=tuple(sems),
               compiler_params=pltpu.CompilerParams(collective_id=collective_id))
    def launch(*sem_refs):
        x, y, c, _ = _place()
        barrier = pltpu.get_barrier_semaphore()
        for dx, dy, dc in peers:
            pl.semaphore_signal(barrier, inc=1, device_id=(x ^ dx, y ^ dy, c ^ dc), device_id_type=MESH)
        pl.semaphore_wait(barrier, len(peers))
        for k, step in enumerate(steps):
            sends, arrivals, own = step.copies(ins if k == 0 else outs, outs, *sem_refs[3 * k:3 * k + 3])
            for cp in own + sends:
                cp.start()
            for cp in arrivals:
                cp.wait_recv()
            for cp in sends:
                cp.wait_send()
            for cp in own:
                cp.wait()

    launch()
    return [ref[...] for ref in outs]


def _chip_exchange_job(sums):
    n = len(sums)

    def copies(ins, outs, send, recv, local):
        del local
        _, _, c, chips = _place()
        sends = [pltpu.make_async_remote_copy(
            src_ref=ins[w].at[_chip_index(*chip)], dst_ref=outs[w].at[j], send_sem=send.at[3 * w + j],
            recv_sem=recv.at[3 * w + j], device_id=(*chip, c), device_id_type=MESH)
            for w in range(n) for j, chip in enumerate(chips)]
        return sends, sends, []

    return _Job(sums, [jax.ShapeDtypeStruct((N_CHIPS - 1,) + s.shape[1:], s.dtype) for s in sums], 3 * n, copies,
                OTHER_CHIPS)


def _chip_sum(name, place, mine, theirs):
    n = len(mine)

    def body(place_ref, *refs):
        del place_ref
        for p_ref, q_ref, o_ref in zip(refs[:n], refs[n:2 * n], refs[2 * n:]):
            acc = p_ref[...].astype(F32)
            for j in range(N_CHIPS - 1):
                acc = acc + q_ref[j].astype(F32)
            o_ref[...] = acc

    def block(p, lead, pick):
        return pl.BlockSpec((lead, p.shape[1] // ROW_STEPS, p.shape[2]), lambda r, place_ref: (pick(place_ref), r, 0))

    return pl.pallas_call(
        body, name=name,
        grid_spec=pltpu.PrefetchScalarGridSpec(
            num_scalar_prefetch=1, grid=(ROW_STEPS,),
            in_specs=[block(p, None, lambda place_ref: place_ref[0]) for p in mine]
            + [block(p, N_CHIPS - 1, lambda place_ref: 0) for p in mine],
            out_specs=[block(p, None, lambda place_ref: place_ref[1]) for p in mine]),
        out_shape=[jax.ShapeDtypeStruct((2,) + p.shape[1:], F32) for p in mine],
        compiler_params=_params(),
    )(place, *mine, *theirs)


def _share_job(bufs):
    n = len(bufs)

    def copies(ins, outs, send, recv, local):
        del ins, local
        x, y, c, _ = _place()

        def copy(w, half):
            return pltpu.make_async_remote_copy(
                src_ref=outs[w].at[half], dst_ref=outs[w].at[half], send_sem=send.at[w], recv_sem=recv.at[w],
                device_id=(x, y, 1 - c), device_id_type=MESH)

        return [copy(w, c) for w in range(n)], [copy(w, 1 - c) for w in range(n)], []

    return _Job(bufs, [jax.ShapeDtypeStruct(b.shape, b.dtype) for b in bufs], n, copies, SIBLING,
                aliases={w: w for w in range(n)})


SMALL_ROWS = 24
ROW_G1, ROW_CW, ROW_CB, ROW_BR, ROW_BI, ROW_LAM, ROW_LG, ROW_LB, ROW_G2, ROW_G3, ROW_LOSS, ROW_BS = (
    0, 1, 5, 6, 7, 8, 9, 10, 11, 12, 13, 16)
N_DEV = 8


def _pack_small(dcw, dcb, dbr, dbi, dlam, dlg, dlb, dg2, dg3, loss, dbs):
    def body(dcw_ref, dcb_ref, dbr_ref, dbi_ref, dlam_ref, dlg_ref, dlb_ref, dg2_ref, dg3_ref, loss_ref, dbs_ref, out):
        out[...] = jnp.zeros((SMALL_ROWS, D_MODEL), F32)
        for row, ref in ((ROW_CB, dcb_ref), (ROW_BR, dbr_ref), (ROW_BI, dbi_ref), (ROW_LAM, dlam_ref),
                         (ROW_LG, dlg_ref), (ROW_LB, dlb_ref), (ROW_G2, dg2_ref), (ROW_G3, dg3_ref)):
            out[row:row + 1, :] = ref[...]
        out[ROW_CW:ROW_CW + CONV_WIDTH, :] = dcw_ref[0:CONV_WIDTH, :]
        out[ROW_LOSS:ROW_LOSS + 1, 0:128] = loss_ref[0:1, :]
        out[ROW_BS:ROW_BS + GROUPS, 0:128] = jnp.transpose(dbs_ref[...])[0:GROUPS, :]

    vm = pl.BlockSpec(memory_space=pltpu.VMEM)
    return pl.pallas_call(
        body, name="pack_small", in_specs=[vm] * 11, out_specs=vm,
        out_shape=jax.ShapeDtypeStruct((SMALL_ROWS, D_MODEL), F32),
    )(dcw, dcb, dbr, dbi, dlam, dlg, dlb, dg2, dg3, loss, dbs)


def _gather_all_job(blocks):
    n = len(blocks)
    flips = [(dx, dy, dc) for dx in (0, 1) for dy in (0, 1) for dc in (0, 1)][1:]

    def copies(ins, outs, send, recv, local):
        x, y, c, _ = _place()
        me = 4 * x + 2 * y + c
        sends, arrivals, own = [], [], []
        for w in range(n):
            own.append(pltpu.make_async_copy(ins[w], outs[w].at[me], local.at[w]))
            for k, (dx, dy, dc) in enumerate(flips):
                peer = (x ^ dx, y ^ dy, c ^ dc)
                sem = dict(send_sem=send.at[7 * w + k], recv_sem=recv.at[7 * w + k])
                sends.append(pltpu.make_async_remote_copy(
                    src_ref=ins[w], dst_ref=outs[w].at[me], device_id=peer, device_id_type=MESH, **sem))
                arrivals.append(pltpu.make_async_remote_copy(
                    src_ref=ins[w], dst_ref=outs[w].at[4 * peer[0] + 2 * peer[1] + peer[2]], device_id=peer,
                    device_id_type=MESH, **sem))
        return sends, arrivals, own

    return _Job(blocks, [jax.ShapeDtypeStruct((N_DEV,) + b.shape, b.dtype) for b in blocks], 7 * n, copies,
                OTHER_CHIPS + SIBLING + tuple((dx, dy, 1) for dx, dy, _ in OTHER_CHIPS), n_local=n)


def _sum_small(vec_all, ws_all, dg1_all):
    def body(vec_ref, ws_ref, dg1_ref, vec_out, ws_out):
        vec, ws, dg1 = vec_ref[0], ws_ref[0], dg1_ref[0]
        for d in range(1, N_DEV):
            vec, ws, dg1 = vec + vec_ref[d], ws + ws_ref[d], dg1 + dg1_ref[d]
        vec_out[...] = vec
        vec_out[ROW_G1:ROW_G1 + 1, :] = dg1
        ws_out[...] = ws

    vm = pl.BlockSpec(memory_space=pltpu.VMEM)
    return pl.pallas_call(
        body, name="sum_small", in_specs=[vm] * 3, out_specs=[vm, vm],
        out_shape=[jax.ShapeDtypeStruct(vec_all.shape[1:], F32), jax.ShapeDtypeStruct(ws_all.shape[1:], F32)],
    )(vec_all, ws_all, dg1_all)


def _adamw_math(w, g, m, v):
    m = ADAM_B1 * m + (1.0 - ADAM_B1) * g
    v = ADAM_B2 * v + (1.0 - ADAM_B2) * (g * g)
    m_hat = m / (1.0 - ADAM_B1 ** ADAM_STEP)
    v_hat = v / (1.0 - ADAM_B2 ** ADAM_STEP)
    delta = (-ADAM_LR) * (m_hat / (jnp.sqrt(v_hat) + ADAM_EPS) + ADAM_WD * w)
    return delta, m, v


def _adamw(name, gs, ws, ms, vs):
    n = len(ws)

    def body(*refs):
        ins, outs = refs[:4 * n], refs[4 * n:]
        for p in range(n):
            g_ref, w_ref, m_ref, v_ref = ins[p::n]
            g = g_ref[...]
            outs[4 * p][...] = g
            outs[4 * p + 1][...], outs[4 * p + 2][...], outs[4 * p + 3][...] = _adamw_math(
                w_ref[...], g, m_ref[...], v_ref[...])

    blocks = [pl.BlockSpec((w.shape[0] // ROW_STEPS, w.shape[1]), lambda r: (r, 0)) for w in ws]
    out = pl.pallas_call(
        body, name=name, grid=(ROW_STEPS,), in_specs=blocks * 4, out_specs=[b for b in blocks for _ in range(4)],
        out_shape=[jax.ShapeDtypeStruct(w.shape, F32) for w in ws for _ in range(4)], compiler_params=_params(),
    )(*gs, *ws, *ms, *vs)
    return [tuple(out[4 * p:4 * p + 4]) for p in range(n)]


def _adamw_small(grads, ws, ms, vs):
    n = len(grads)

    def body(*refs):
        g_refs, w_refs, m_refs, v_refs = refs[:n], refs[n:2 * n], refs[2 * n:3 * n], refs[3 * n:4 * n]
        outs = refs[4 * n:]
        for p in range(n):
            d, nm, nv = _adamw_math(w_refs[p][...], g_refs[p][...], m_refs[p][...], v_refs[p][...])
            outs[p][...] = d
            outs[n + p][...] = nm
            outs[2 * n + p][...] = nv

    vm = pl.BlockSpec(memory_space=pltpu.VMEM)
    shapes = [jax.ShapeDtypeStruct(w.shape, F32) for w in ws]
    out = pl.pallas_call(
        body, name="adamw_small", in_specs=[vm] * (4 * n), out_specs=[vm] * (3 * n), out_shape=shapes * 3,
    )(*grads, *ws, *ms, *vs)
    return out[:n], out[n:2 * n], out[2 * n:]


def _unstack_heads(w_st):
    per = HEAD_DIM // N_CHIPS
    return w_st.reshape(N_CHIPS, HEADS, per, HEAD_DIM).transpose(1, 0, 2, 3).reshape(HEADS, HEAD_DIM, HEAD_DIM)


def _stack_heads(w):
    per = HEAD_DIM // N_CHIPS
    return w.reshape(HEADS, N_CHIPS, per, HEAD_DIM).transpose(1, 0, 2, 3).reshape(N_CHIPS, HEADS * per, HEAD_DIM)


def kernel(x, norm_mix_g, w_in, conv_w, conv_b, w_rgate, b_rgate, w_igate, b_igate, lru_lambda, w_out_a, sgu_ln_g, sgu_ln_b, sgu_w_s, sgu_b_s, w_out_b, w_out, norm_mlp_g, w_up, w_down, norm_final_g, loss_target, m_norm_mix_g, m_w_in, m_conv_w, m_conv_b, m_w_rgate, m_b_rgate, m_w_igate, m_b_igate, m_lru_lambda, m_w_out_a, m_sgu_ln_g, m_sgu_ln_b, m_sgu_w_s, m_sgu_b_s, m_w_out_b, m_w_out, m_norm_mlp_g, m_w_up, m_w_down, m_norm_final_g, v_norm_mix_g, v_w_in, v_conv_w, v_conv_b, v_w_rgate, v_b_rgate, v_w_igate, v_b_igate, v_lru_lambda, v_w_out_a, v_sgu_ln_g, v_sgu_ln_b, v_sgu_w_s, v_sgu_b_s, v_w_out_b, v_w_out, v_norm_mlp_g, v_w_up, v_w_down, v_norm_final_g):
    chip = _chip_index(lax.axis_index("x"), lax.axis_index("y"))
    core = lax.axis_index("c")
    quarter_h = HEAD_DIM // N_CHIPS
    quarter_d = D_MODEL // N_CHIPS

    as_2d = lambda a: a.reshape(-1, a.shape[-1])
    big_w = [as_2d(w) for w in (w_in, w_rgate, w_igate, w_out_a, w_out_b, w_out, w_up, w_down)]
    big_m = [as_2d(w) for w in (m_w_in, m_w_rgate, m_w_igate, m_w_out_a, m_w_out_b, m_w_out, m_w_up, m_w_down)]
    big_v = [as_2d(w) for w in (v_w_in, v_w_rgate, v_w_igate, v_w_out_a, v_w_out_b, v_w_out, v_w_up, v_w_down)]

    packed = jnp.concatenate([conv_w[0], b_rgate[0], b_igate[0]], axis=1)
    packed = jnp.concatenate([packed, jnp.zeros_like(packed)], axis=0)
    s_in, s_r, s_i, s_oa, s_ob, s_out, s_up, s_down = [w.astype(BF16) for w in big_w]
    xs, target = x[0], loss_target[0]
    g3 = norm_final_g.reshape(1, D_MODEL)
    bias_s = jnp.broadcast_to(jnp.transpose(sgu_b_s[0])[:, :, None], (CHUNK, GROUPS, GROUP_DIM)).reshape(CHUNK, D_MODEL)
    core_arr = core.reshape(1).astype(jnp.int32)
    place = jnp.stack([chip, core]).astype(jnp.int32)
    quarter = lambda g: g.reshape(N_CHIPS, D_MODEL // N_CHIPS, D_MODEL)

    def pair_add(nm, grads, from_sibling):
        halves = [g.reshape(N_CHIPS, 2, g.shape[1] // 2, g.shape[2]) for g in grads]
        return list(_pair_add("pair_add_" + nm, core_arr, halves, from_sibling))

    def chip_sum(nm, pairs, from_chips):
        return list(_chip_sum("chip_sum_" + nm, place, pairs, from_chips))

    order = jnp.stack([chip, chip ^ 2, chip ^ 1, chip ^ 3]).astype(jnp.int32)
    (z, n1, (w_in_st, wr_st, wi_st)), ((packed_all,), late) = _fwd_in(
        xs, norm_mix_g, [s_in, s_r, s_i], order,
        jobs=[_gather_small_job(packed), _gather_near_job([s_oa, s_ob, s_out])])
    pick = lambda lo, hi: packed_all[:, :HEADS, lo:hi].transpose(1, 0, 2).reshape(HEADS, -1)
    conv_w_full = pick(0, quarter_d)
    br_full = pick(quarter_d, quarter_d + quarter_h).reshape(1, D_MODEL)
    bi_full = pick(quarter_d + quarter_h, quarter_d + 2 * quarter_h).reshape(1, D_MODEL)
    wr, wi = _unstack_heads(wr_st), _unstack_heads(wi_st)
    lru = (conv_w_full, conv_b, wr, br_full, wi, bi_full, lru_lambda)
    sgu = (sgu_ln_g, sgu_ln_b, sgu_w_s[0], bias_s)

    after = lambda arrays, result: lax.optimization_barrier((arrays, result))[0]
    w_up_st, w_dn = _sequencer_call(
        "gather_mlp", 8, _gather_near_job(after([s_up, s_down], n1)).then(_gather_far_job).then(_gather_pass_job))
    w_dn = w_dn.reshape(D_FF, D_MODEL)
    late_step = (3 * (xs.shape[0] // SEQ_TILE) // 4,)
    (ya, *saved), (late,) = _fwd_lru(z, *lru, jobs=[_gather_far_job(late).then(_gather_pass_job, at=late_step)])
    w_oa, w_ob, w_o = [w.reshape(D_MODEL, D_MODEL) for w in late]
    (yb, pa, pb, h1, n2), _ = _fwd_sgu_merge(ya, z, xs, *sgu, w_oa, w_ob, w_o, norm_mlp_g)
    (act, dup, dh2b, dh1, loss_part, dg3, dg2), _ = _mlp(n2, h1, target, w_up_st, w_dn, norm_mlp_g, g3)

    d_down, _ = _weight_grad("dw_down", act, dh2b, N_CHIPS, True, False, D_MODEL)
    r_down, = _sequencer_call("send_w_down", 10, _pair_send_job([d_down]))
    d_up, _ = _weight_grad("dw_up", n2, dup, N_CHIPS, False, True, D_MODEL)
    r_up, = _sequencer_call("send_w_up", 11, _pair_send_job([d_up]))
    (p_down,), (p_up,) = pair_add("w_down", [d_down], [r_down]), pair_add("w_up", [d_up], [r_up])
    (dz, merged, dpa, dpb, dh1b, dlg, dlb, dws, dbs, dcw, dcb, dwr, dbr, dwi, dbi, dlam), ((q_up, q_down),) = _bwd_mix(
        dh1, pa, pb, z, *saved, w_oa, w_ob, w_o, *sgu, conv_w_full, wr, wi, lru_lambda,
        jobs=[_chip_exchange_job([p_up, p_down])])
    names = ("w_in", "w_rgate", "w_igate", "w_out_a", "w_out_b", "w_out", "w_up", "w_down")
    (d_out, d_oa, d_ob), _ = _weight_grads_square("dw_projections", [(merged, dh1b), (ya, dpa), (yb, dpb)])
    mids = [quarter(d_oa), quarter(d_ob), quarter(d_out)]
    r_mids = _sequencer_call("send_mids", 1, _pair_send_job(mids))
    half_up, half_down = chip_sum("mlp", [p_up, p_down], after([q_up, q_down], mids))
    gates = [_stack_heads(dwr).astype(BF16), _stack_heads(dwi).astype(BF16)]
    small = _pack_small(dcw, dcb, dbr, dbi, dlam, dlg, dlb, dg2, dg3, loss_part, dbs)
    p_mids = pair_add("projections", after(mids, [half_up, half_down]), r_mids)
    q_mids = _sequencer_call("exchange_mids", 2, _chip_exchange_job(p_mids))
    d_in, (r_gates, (vec_all, ws_all), (full_up, full_down)) = _weight_grad(
        "dw_in", n1, after(dz, p_mids), N_CHIPS, False, True, IN_SHARD,
        jobs=[_pair_send_job(gates), _gather_all_job([small, dws]), _share_job([half_up, half_down])])
    r_in, = _sequencer_call("send_w_in", 3, _pair_send_job(after([d_in], q_mids)))
    adam_args = {nm: (w, m, v) for nm, w, m, v in zip(names, big_w, big_m, big_v)}

    def adamw(group, nms, grads):
        given = [adam_args[nm] for nm in nms]
        grads = [g.reshape(w.shape) for g, (w, _, _) in zip(grads, given)]
        outs = _adamw("adamw_" + group, grads, *[[a[q] for a in given] for q in range(3)])
        return {nm: (out[0], out[1:]) for nm, out in zip(nms, outs)}

    p_gates = pair_add("gates", gates, r_gates)
    half_mids = chip_sum("projections", p_mids, q_mids)
    full_mids = _sequencer_call("share_mids", 12, _share_job(half_mids))
    p_first = pair_add("w_in", after([d_in], half_mids), [r_in]) + p_gates
    q_first = _sequencer_call("exchange_w_in", 4, _chip_exchange_job(p_first))
    (grad_x, dg1), _ = _bwd_in(dz, xs, dh1, w_in_st, norm_mix_g)
    dg1_all, = _sequencer_call("gather_dg1", 6, _gather_all_job([dg1]))
    done = adamw("mlp", ("w_up", "w_down"), [full_up, full_down])
    q_first = after(q_first, [out[0] for _, out in done.values()])
    half_first = chip_sum("first", p_first, q_first)
    full_first = _sequencer_call("share_last", 5, _share_job(half_first))
    done.update(adamw("projections", names[3:6], after(full_mids, half_first)))
    done.update(adamw("first", names[:3], full_first))
    full, big_out = [done[nm][0] for nm in names], [done[nm][1] for nm in names]

    vec, ws_sum = _sum_small(vec_all, ws_all, dg1_all)
    row = lambda r: vec[r:r + 1]
    shard = lambda a, width: lax.dynamic_slice_in_dim(a, chip * width, width, axis=1)
    g_small = dict(
        norm_mix_g=row(ROW_G1), conv_w=shard(vec[ROW_CW:ROW_CW + CONV_WIDTH], quarter_d), conv_b=row(ROW_CB),
        b_rgate=shard(row(ROW_BR).reshape(HEADS, HEAD_DIM), quarter_h),
        b_igate=shard(row(ROW_BI).reshape(HEADS, HEAD_DIM), quarter_h), lru_lambda=row(ROW_LAM),
        sgu_ln_g=row(ROW_LG), sgu_ln_b=row(ROW_LB),
        sgu_w_s=ws_sum.reshape(CHUNK, GROUPS, CHUNK).transpose(1, 0, 2).reshape(GROUPS * CHUNK, CHUNK),
        sgu_b_s=vec[ROW_BS:ROW_BS + GROUPS, 0:CHUNK], norm_mlp_g=row(ROW_G2), norm_final_g=row(ROW_G3))
    loss = vec[ROW_LOSS, 0]
    small_names = list(g_small)
    given = dict(
        norm_mix_g=(norm_mix_g, m_norm_mix_g, v_norm_mix_g), conv_w=(conv_w, m_conv_w, v_conv_w),
        conv_b=(conv_b, m_conv_b, v_conv_b), b_rgate=(b_rgate, m_b_rgate, v_b_rgate),
        b_igate=(b_igate, m_b_igate, v_b_igate), lru_lambda=(lru_lambda, m_lru_lambda, v_lru_lambda),
        sgu_ln_g=(sgu_ln_g, m_sgu_ln_g, v_sgu_ln_g), sgu_ln_b=(sgu_ln_b, m_sgu_ln_b, v_sgu_ln_b),
        sgu_w_s=(sgu_w_s, m_sgu_w_s, v_sgu_w_s), sgu_b_s=(sgu_b_s, m_sgu_b_s, v_sgu_b_s),
        norm_mlp_g=(norm_mlp_g, m_norm_mlp_g, v_norm_mlp_g), norm_final_g=(norm_final_g, m_norm_final_g, v_norm_final_g))
    g2d = [g_small[nm] for nm in small_names]
    to2d = lambda a, g: a.reshape(g.shape)
    d_s, m_s, v_s = _adamw_small(
        g2d, *[[to2d(given[nm][q], g) for nm, g in zip(small_names, g2d)] for q in range(3)])

    shapes = dict(
        norm_mix_g=norm_mix_g, w_in=w_in, conv_w=conv_w, conv_b=conv_b, w_rgate=w_rgate, b_rgate=b_rgate,
        w_igate=w_igate, b_igate=b_igate, lru_lambda=lru_lambda, w_out_a=w_out_a, sgu_ln_g=sgu_ln_g,
        sgu_ln_b=sgu_ln_b, sgu_w_s=sgu_w_s, sgu_b_s=sgu_b_s, w_out_b=w_out_b, w_out=w_out, norm_mlp_g=norm_mlp_g,
        w_up=w_up, w_down=w_down, norm_final_g=norm_final_g)
    grads, deltas, new_m, new_v = {}, {}, {}, {}
    for nm, g, (d, nmom, nvar) in zip(names, full, big_out):
        grads[nm], deltas[nm], new_m[nm], new_v[nm] = g, d, nmom, nvar
    for p, nm in enumerate(small_names):
        grads[nm], deltas[nm], new_m[nm], new_v[nm] = g2d[p], d_s[p], m_s[p], v_s[p]
    order = list(shapes)
    out = [loss, grad_x[None]]
    for group in (grads, deltas, new_m, new_v):
        out += [group[nm].reshape(shapes[nm].shape) for nm in order]
    return tuple(out)
```

```python
import functools

import jax
import jax.numpy as jnp
from jax import lax
from jax.experimental import pallas as pl
from jax.experimental.pallas import tpu as pltpu
from jax.experimental.pallas import tpu_sc as plsc

F32 = jnp.float32
BF16 = jnp.bfloat16
MESH = pl.DeviceIdType.MESH

D_MODEL = 1024
D_IN = 6 * D_MODEL
D_FF = 4 * D_MODEL
N_CHIPS = 4
IN_SHARD = D_IN // N_CHIPS
HEADS = 4
HEAD_DIM = D_MODEL // HEADS
GROUPS = 4
GROUP_DIM = D_MODEL // GROUPS
CHUNK = 128
CONV_WIDTH = 4
LRU_C = 8.0
NORM_EPS = 1e-6
LN_EPS = 1e-5

ADAM_LR = 0.001
ADAM_B1 = 0.9
ADAM_B2 = 0.999
ADAM_EPS = 1e-08
ADAM_WD = 0.01
ADAM_STEP = 10

SUBLANES = 8
MM_TILE = 512
IN_TILE = 1024
SEQ_TILE = 256
DW_TILE = 2048
VMEM_LIMIT_BYTES = 56 * 1024 * 1024

GELU_K0 = 0.7978845608028654
GELU_K1 = 0.044715


def _params(n_grid_axes=1):
    return pltpu.CompilerParams(
        dimension_semantics=("arbitrary",) * n_grid_axes, vmem_limit_bytes=VMEM_LIMIT_BYTES)


def _resident(shape):
    nd = len(shape)
    return pl.BlockSpec(shape, lambda *_: (0,) * nd, pipeline_mode=pl.Buffered(1))


def _const(shape):
    nd = len(shape)
    return pl.BlockSpec(shape, lambda *_: (0,) * nd)


def _dot(a, b):
    return jnp.dot(a, b, preferred_element_type=F32)


def _dot_nt(a, b):
    return lax.dot_general(a, b, (((1,), (1,)), ((), ())), preferred_element_type=F32)


def _dot_tn(a, b):
    return lax.dot_general(a, b, (((0,), (0,)), ((), ())), preferred_element_type=F32)


def _gelu(x):
    t = jnp.tanh(x * (GELU_K0 + (GELU_K0 * GELU_K1) * (x * x)))
    return x * (0.5 + 0.5 * t)


def _gelu_and_grad(x):
    x2 = x * x
    t = jnp.tanh(x * (GELU_K0 + (GELU_K0 * GELU_K1) * x2))
    s = 0.5 + 0.5 * t
    dg = s + (x * (1.0 - t * t)) * (0.5 * GELU_K0 + (1.5 * GELU_K0 * GELU_K1) * x2)
    return x * s, dg


def _gate(x):
    return 0.5 + 0.5 * jnp.tanh(0.5 * x.astype(F32))


def _rms(x):
    r = lax.rsqrt(jnp.mean(x * x, axis=-1, keepdims=True) + NORM_EPS)
    return x * r, r


def _rms_bwd(dn, xhat, r):
    return r * (dn - xhat * jnp.mean(dn * xhat, axis=-1, keepdims=True))


def _col_sum(v):
    return jnp.sum(v, axis=0, keepdims=True)


def _shift_down(x, tail8, k):
    xs = pltpu.roll(x, k, 0)
    ts = pltpu.roll(tail8, k, 0)
    ridx = lax.broadcasted_iota(jnp.int32, tail8.shape, 0)
    head = jnp.where(ridx < k, ts, xs[0:SUBLANES])
    return jnp.concatenate([head, xs[SUBLANES:]], axis=0)


def _shift_up(x, head8, k):
    n = x.shape[0]
    xs = pltpu.roll(x, n - k, 0)
    hs = pltpu.roll(head8, SUBLANES - k, 0)
    ridx = lax.broadcasted_iota(jnp.int32, head8.shape, 0)
    last = jnp.where(ridx >= SUBLANES - k, hs, xs[n - SUBLANES:n])
    return jnp.concatenate([xs[:n - SUBLANES], last], axis=0)


def _scan_forward(a, b, carry):
    n, cols = a.shape
    groups = n // SUBLANES
    a = a.reshape(groups, SUBLANES, cols)
    b = b.reshape(groups, SUBLANES, cols)
    sub = lax.broadcasted_iota(jnp.int32, a.shape, 1)
    for s in (1, 2, 4):
        a_s = pltpu.roll(a, s, 1)
        b_s = pltpu.roll(b, s, 1)
        m = sub >= s
        b = jnp.where(m, a * b_s + b, b)
        a = jnp.where(m, a * a_s, a)
    out = []
    for g in range(groups):
        h = a[g] * carry + b[g]
        out.append(h)
        carry = h[SUBLANES - 1:SUBLANES]
    return jnp.concatenate(out, axis=0), carry


def _scan_backward(a, b, carry):
    n, cols = a.shape
    groups = n // SUBLANES
    a = a.reshape(groups, SUBLANES, cols)
    b = b.reshape(groups, SUBLANES, cols)
    sub = lax.broadcasted_iota(jnp.int32, a.shape, 1)
    for s in (1, 2, 4):
        a_s = pltpu.roll(a, SUBLANES - s, 1)
        b_s = pltpu.roll(b, SUBLANES - s, 1)
        m = sub < SUBLANES - s
        b = jnp.where(m, a * b_s + b, b)
        a = jnp.where(m, a * a_s, a)
    out = [None] * groups
    for g in reversed(range(groups)):
        h = a[g] * carry + b[g]
        out[g] = h
        carry = h[0:1]
    return jnp.concatenate(out, axis=0), carry


def _softplus_neg(lam):
    e = jnp.exp(-jnp.abs(lam))
    u = 1.0 + e
    log1p_e = jnp.where(u == 1.0, e, jnp.log(u) * (e / jnp.where(u == 1.0, 1.0, u - 1.0)))
    return jnp.maximum(-lam, 0.0) + log1p_e


def _lru_gates(xa, tail8, cw_ref, cb_ref, wr_ref, br_ref, wi_ref, bi_ref, lam_ref):
    cw = cw_ref[...]
    xc = cb_ref[...] + cw[0:1] * xa
    for k in range(1, CONV_WIDTH):
        xc = xc + cw[k:k + 1] * _shift_down(xa, tail8, k)
    xcb = xc.astype(BF16)
    pre_r, pre_i = [], []
    for h in range(HEADS):
        cols = slice(h * HEAD_DIM, (h + 1) * HEAD_DIM)
        pre_r.append(_dot(xcb[:, cols], wr_ref[h]))
        pre_i.append(_dot(xcb[:, cols], wi_ref[h]))
    r = jax.nn.sigmoid(jnp.concatenate(pre_r, axis=1) + br_ref[...])
    ig = jax.nn.sigmoid(jnp.concatenate(pre_i, axis=1) + bi_ref[...])
    _, a, mult, _ = _decay(r, lam_ref)
    return xc, r, ig, a, mult


def _decay(r, lam_ref):
    sp = _softplus_neg(lam_ref[...])
    log_a = ((-LRU_C) * sp) * r
    a = jnp.exp(log_a)
    th = jnp.tanh(log_a)
    q = (-2.0 * th) / (1.0 - th)
    inv = lax.rsqrt(q)
    return sp, a, jnp.where(q > 0.0, q * inv, 0.0), inv


class _Phase:
    def __init__(self, copies, n_sem, n_local, start=None, finish=None):
        self.copies, self.n_sem, self.n_local, self.start, self.finish = copies, n_sem, n_local, start, finish


class _Job:
    def __init__(self, inputs, out_shape, n_sem, copies, peers, aliases=None, n_local=0):
        self.inputs, self.out_shape = list(inputs), list(out_shape)
        self.aliases = dict(aliases or {})
        self.phases = [_Phase(copies, n_sem, n_local)]
        self.peers = tuple(peers)

    def then(self, make, at=None):
        nxt = make(self.out_shape)
        self.phases[-1].finish = at
        nxt.phases[0].start = at
        self.phases += nxt.phases
        self.peers = tuple(sorted(set(self.peers + nxt.peers)))
        return self


def _fused_call(body, jobs, *, name, grid, in_specs, out_specs, out_shape, scratch_shapes=(),
                input_output_aliases=None, compiler_params=None, n_prefetch=0, jobs_start_after=None):
    single = not isinstance(out_shape, (list, tuple))
    out_specs = [out_specs] if single else list(out_specs)
    out_shape = [out_shape] if single else list(out_shape)
    n_scr = len(scratch_shapes)
    in_specs, scratch_shapes = list(in_specs), list(scratch_shapes)
    n_in, n_out = len(in_specs), len(out_shape)
    aliases = dict(input_output_aliases or {})
    in_at, out_at, phases = [], [], []
    for q, job in enumerate(jobs):
        in_at.append(len(in_specs))
        out_at.append(len(out_shape))
        for i, o in job.aliases.items():
            aliases[n_prefetch + len(in_specs) + i] = len(out_shape) + o
        in_specs += [ANY] * len(job.inputs)
        out_specs += [ANY] * len(job.out_shape)
        out_shape += job.out_shape
        for k, phase in enumerate(job.phases):
            phases.append((q, k, phase, len(scratch_shapes)))
            scratch_shapes += [pltpu.SemaphoreType.DMA((phase.n_sem,)), pltpu.SemaphoreType.DMA((phase.n_sem,)),
                               pltpu.SemaphoreType.DMA((max(phase.n_local, 1),))]
    n_in_all, n_out_all = len(in_specs), len(out_shape)
    first_step, last_step = (0,) * len(grid), tuple(g - 1 for g in grid)

    def full_body(*refs):
        prefetch, refs = refs[:n_prefetch], refs[n_prefetch:]
        ins, outs, scr = refs[:n_in_all], refs[n_in_all:n_in_all + n_out_all], refs[n_in_all + n_out_all:]
        ids = [pl.program_id(a) for a in range(len(grid))]
        at_step = lambda step: functools.reduce(jnp.logical_and, [i == k for i, k in zip(ids, step)])

        def copies(q, k, phase, sem_at):
            job = jobs[q]
            mine = outs[out_at[q]:out_at[q] + len(job.out_shape)]
            return phase.copies(ins[in_at[q]:in_at[q] + len(job.inputs)] if k == 0 else mine, mine,
                                *scr[sem_at:sem_at + 3])

        def start(*phase):
            def go():
                sends, _, local = copies(*phase)
                for cp in local + sends:
                    cp.start()
            return go

        def finish(*phase):
            def go():
                sends, arrivals, local = copies(*phase)
                for cp in arrivals:
                    cp.wait_recv()
                for cp in sends:
                    cp.wait_send()
                for cp in local:
                    cp.wait()
            return go

        for phase in phases:
            if phase[2].start is None and jobs_start_after is None:
                pl.when(at_step(first_step))(start(*phase))
        body(*prefetch, *ins[:n_in], *outs[:n_out], *scr[:n_scr])
        for phase in phases:
            pl.when(at_step(phase[2].finish or last_step))(finish(*phase))
            nxt = phase[2].start or jobs_start_after
            if nxt is not None:
                pl.when(at_step(nxt))(start(*phase))

    if n_prefetch:
        layout = dict(grid_spec=pltpu.PrefetchScalarGridSpec(
            num_scalar_prefetch=n_prefetch, grid=grid, in_specs=in_specs, out_specs=out_specs,
            scratch_shapes=scratch_shapes))
    else:
        layout = dict(grid=grid, in_specs=in_specs, out_specs=out_specs, scratch_shapes=scratch_shapes)
    call = pl.pallas_call(
        full_body, name=name, out_shape=out_shape, input_output_aliases=aliases, compiler_params=compiler_params,
        **layout)

    def run(*args):
        res = call(*args, *[a for job in jobs for a in job.inputs])
        mine = res[0] if single else list(res[:n_out])
        return mine, [list(res[at:at + len(job.out_shape)]) for at, job in zip(out_at, jobs)]

    return run


def _fwd_in(x, g1, shards, order, jobs=()):
    t = x.shape[0]
    rows_per_step = min(IN_TILE, t)
    n_tiles = t // rows_per_step
    n = len(shards)
    halves = [s.shape[0] // 2 for s in shards]

    def body(order_ref, x_ref, g_ref, *refs):
        del order_ref
        ins, (z_ref, n_ref), outs = refs[:n], refs[n:n + 2], refs[n + 2:2 * n + 2]
        wbuf, nbuf, send, recv, local = refs[2 * n + 2:]
        s, i = pl.program_id(0), pl.program_id(1)
        x_, y_, c, chips = _place()
        near, far = _near_far(x_, y_, c)
        k_me = _chip_index(x_, y_)

        def block(w, chip, pc):
            return outs[w].at[_chip_index(*chip), pl.ds(pc * halves[w], halves[w]), :]

        def over_ici(w, j, landing):
            return pltpu.make_async_remote_copy(
                src_ref=ins[w].at[pl.ds(c * halves[w], halves[w]), :],
                dst_ref=block(w, chips[j] if landing else (x_, y_), c), send_sem=send.at[6 * w + j],
                recv_sem=recv.at[6 * w + j], device_id=(*chips[j], c), device_id_type=MESH)

        def onward(w, landing):
            blk = block(w, chips[2] if landing else near, c)
            return pltpu.make_async_remote_copy(
                src_ref=blk, dst_ref=blk, send_sem=send.at[6 * w + 2], recv_sem=recv.at[6 * w + 2],
                device_id=(*far, c), device_id_type=MESH)

        def to_sibling(w, j, landing):
            blk = block(w, chips[j], 1 - c if landing else c)
            return pltpu.make_async_remote_copy(
                src_ref=blk, dst_ref=blk, send_sem=send.at[6 * w + 3 + j], recv_sem=recv.at[6 * w + 3 + j],
                device_id=(x_, y_, 1 - c), device_id_type=MESH)

        own = [pltpu.make_async_copy(wbuf, outs[0].at[k_me], local.at[0])]
        own += [pltpu.make_async_copy(ins[w], outs[w].at[k_me], local.at[w]) for w in range(1, n)]

        @pl.when((s == 0) & (i == 0))
        def _():
            for j in range(2):
                for w in range(n):
                    over_ici(w, j, False).start()
            load = pltpu.make_async_copy(ins[0], wbuf, local.at[n])
            load.start()
            load.wait()
            for cp in own:
                cp.start()

        for j in range(N_CHIPS - 1):
            @pl.when((s == j + 1) & (i == 0))
            def _(j=j):
                if j == 0:
                    for k in range(2):
                        for w in range(n):
                            over_ici(w, k, True).wait_recv()
                    for w in range(n):
                        onward(w, False).start()
                    for k in range(2):
                        for w in range(n):
                            to_sibling(w, k, False).start()
                    own[0].wait()
                if j == 2:
                    for w in range(n):
                        onward(w, True).wait_recv()
                    for w in range(n):
                        to_sibling(w, j, False).start()
                for w in range(n):
                    to_sibling(w, j, True).wait_recv()
                load = pltpu.make_async_copy(outs[0].at[_chip_index(*chips[j])], wbuf, local.at[n])
                load.start()
                load.wait()

        rows = pl.ds(pl.multiple_of(i * rows_per_step, rows_per_step), rows_per_step)

        @pl.when(s == 0)
        def _():
            xhat, _ = _rms(x_ref[...])
            nrm = (xhat * g_ref[...]).astype(BF16)
            nbuf[rows, :] = nrm
            n_ref[...] = nrm

        z_ref[...] = _dot(nbuf[rows, :], wbuf[...]).astype(BF16)

        @pl.when((s == N_CHIPS - 1) & (i == n_tiles - 1))
        def _():
            for j in range(N_CHIPS - 1):
                for w in range(n):
                    (over_ici(w, j, False) if j < 2 else onward(w, False)).wait_send()
                    to_sibling(w, j, False).wait_send()
            for cp in own[1:]:
                cp.wait()

    once = lambda s, i, order: (jnp.where(s == 0, i, n_tiles - 1), 0)
    (z, n1, *stacked), job_outs = _fused_call(
        body, jobs, name="fwd_in", grid=(N_CHIPS, n_tiles), n_prefetch=1,
        in_specs=[pl.BlockSpec((rows_per_step, D_MODEL), once), _const((1, D_MODEL))] + [ANY] * n,
        out_specs=[pl.BlockSpec((rows_per_step, IN_SHARD), lambda s, i, order: (i, order[s])),
                   pl.BlockSpec((rows_per_step, D_MODEL), once)] + [ANY] * n,
        out_shape=[jax.ShapeDtypeStruct((t, D_IN), BF16), jax.ShapeDtypeStruct((t, D_MODEL), BF16)]
        + [jax.ShapeDtypeStruct((N_CHIPS,) + s.shape, s.dtype) for s in shards],
        scratch_shapes=[pltpu.VMEM(shards[0].shape, BF16), pltpu.VMEM((t, D_MODEL), BF16),
                        pltpu.SemaphoreType.DMA((6 * n,)),
                        pltpu.SemaphoreType.DMA((6 * n,)), pltpu.SemaphoreType.DMA((n + 1,))],
        compiler_params=_params(2), jobs_start_after=(1, 0),
    )(order, x, g1, *shards)
    return (z, n1, stacked), job_outs


def _fwd_lru(z, conv_w, conv_b, wr, br, wi, bi, lam, jobs=()):
    t = z.shape[0]

    def body(xa_ref, ga_ref, cw_ref, cb_ref, wr_ref, br_ref, wi_ref, bi_ref, lam_ref, ya_ref, h_ref, xc_ref, r_ref,
             ig_ref, tail_ref, carry_ref):
        @pl.when(pl.program_id(0) == 0)
        def _():
            tail_ref[...] = jnp.zeros_like(tail_ref)
            carry_ref[...] = jnp.zeros_like(carry_ref)

        xa = xa_ref[...].astype(F32)
        xc, r, ig, a, mult = _lru_gates(xa, tail_ref[...], cw_ref, cb_ref, wr_ref, br_ref, wi_ref, bi_ref, lam_ref)
        tail_ref[...] = xa[SEQ_TILE - SUBLANES:]
        xc_ref[...], r_ref[...], ig_ref[...] = xc, r, ig
        h, carry = _scan_forward(a, xc * ig * mult, carry_ref[...])
        carry_ref[...] = carry
        h_ref[...] = h
        ya_ref[...] = (h * _gelu(ga_ref[...].astype(F32))).astype(BF16)

    tile = lambda j: pl.BlockSpec((SEQ_TILE, D_MODEL), lambda i: (i, j))
    return _fused_call(
        body, jobs, name="fwd_lru", grid=(t // SEQ_TILE,),
        in_specs=[tile(0), tile(1), _const((CONV_WIDTH, D_MODEL)), _const((1, D_MODEL)),
                  _resident((HEADS, HEAD_DIM, HEAD_DIM)), _const((1, D_MODEL)),
                  _resident((HEADS, HEAD_DIM, HEAD_DIM)), _const((1, D_MODEL)), _const((1, D_MODEL))],
        out_specs=[tile(0)] * 5,
        out_shape=[jax.ShapeDtypeStruct((t, D_MODEL), BF16)] + [jax.ShapeDtypeStruct((t, D_MODEL), F32)] * 4,
        scratch_shapes=[pltpu.VMEM((SUBLANES, D_MODEL), F32), pltpu.VMEM((1, D_MODEL), F32)],
        compiler_params=_params(),
    )(z, z, conv_w, conv_b, wr, br, wi, bi, lam)


def _sgu_forward_parts(ub, vb, lg_ref, lb_ref):
    u, du = _gelu_and_grad(ub.astype(F32))
    vg, dvg = _gelu_and_grad(vb.astype(F32))
    mu = jnp.mean(vg, axis=-1, keepdims=True)
    d = vg - mu
    rstd = lax.rsqrt(jnp.mean(d * d, axis=-1, keepdims=True) + LN_EPS)
    vhat = d * rstd
    vn = (vhat * lg_ref[...] + lb_ref[...]).astype(BF16)
    return u, du, dvg, rstd, vhat, vn


def _causal_mask():
    rows = lax.broadcasted_iota(jnp.int32, (CHUNK, CHUNK), 0)
    cols = lax.broadcasted_iota(jnp.int32, (CHUNK, CHUNK), 1)
    return rows >= cols


def _fwd_sgu_merge(ya, z, x, ln_g, ln_b, w_s, bias_full, w_oa, w_ob, w_out, g2, jobs=()):
    t = x.shape[0]

    def body(ya_ref, ub_ref, vb_ref, m_ref, x_ref, lg_ref, lb_ref, ws_ref, bias_ref, woa_ref, wob_ref, wout_ref, g_ref,
             yb_ref, pa_ref, pb_ref, h1_ref, n2_ref):
        u, _, _, _, _, vn = _sgu_forward_parts(ub_ref[...], vb_ref[...], lg_ref, lb_ref)
        mask = _causal_mask()
        wm = [jnp.where(mask, ws_ref[g], 0.0).astype(BF16) for g in range(GROUPS)]
        for c in range(SEQ_TILE // CHUNK):
            rows = slice(c * CHUNK, (c + 1) * CHUNK)
            for g in range(GROUPS):
                cols = slice(g * GROUP_DIM, (g + 1) * GROUP_DIM)
                sp = _dot(wm[g], vn[rows, cols]) + bias_ref[:, cols]
                yb_ref[rows, cols] = (u[rows, cols] * sp).astype(BF16)
        pa = _dot(ya_ref[...], woa_ref[...])
        pb = _dot(yb_ref[...], wob_ref[...])
        pa_ref[...] = pa
        pb_ref[...] = pb
        merged = _gate(m_ref[:, :D_MODEL]) * pa + _gate(m_ref[:, D_MODEL:]) * pb
        h1 = x_ref[...] + _dot(merged.astype(BF16), wout_ref[...])
        h1_ref[...] = h1
        xhat, _ = _rms(h1)
        n2_ref[...] = (xhat * g_ref[...]).astype(BF16)

    tile = lambda j: pl.BlockSpec((SEQ_TILE, D_MODEL), lambda i: (i, j))
    sq = _resident((D_MODEL, D_MODEL))
    vec = _const((1, D_MODEL))
    bf, f32 = jax.ShapeDtypeStruct((t, D_MODEL), BF16), jax.ShapeDtypeStruct((t, D_MODEL), F32)
    return _fused_call(
        body, jobs, name="fwd_sgu_merge", grid=(t // SEQ_TILE,),
        in_specs=[tile(0), tile(2), tile(3), pl.BlockSpec((SEQ_TILE, 2 * D_MODEL), lambda i: (i, 2)), tile(0), vec, vec,
                  _const((GROUPS, CHUNK, CHUNK)), _const((CHUNK, D_MODEL)), sq, sq, sq, vec],
        out_specs=[tile(0)] * 5,
        out_shape=[bf, f32, f32, f32, bf],
        compiler_params=_params(),
    )(ya, z, z, z, x, ln_g, ln_b, w_s, bias_full, w_oa, w_ob, w_out, g2)


def _mlp(n2, h1, target, w_up_st, w_down, g2, g3, jobs=()):
    t = n2.shape[0]

    def body(n2_ref, h1_ref, tgt_ref, wup_ref, wdown_ref, g2_ref, g3_ref, act_ref, dup_ref, dh2b_ref, dh1_ref,
             loss_ref, dg3_ref, dg2_ref, relu_ref):
        @pl.when(pl.program_id(0) == 0)
        def _():
            for ref in (loss_ref, dg3_ref, dg2_ref):
                ref[...] = jnp.zeros_like(ref)

        n2 = n2_ref[...]
        h1 = h1_ref[...]
        h2 = h1
        for k in range(N_CHIPS):
            cols = slice(k * D_MODEL, (k + 1) * D_MODEL)
            r = jnp.maximum(_dot(n2, wup_ref[k]), 0.0)
            relu_ref[:, cols] = r
            act = (r * r).astype(BF16)
            act_ref[:, cols] = act
            h2 = h2 + _dot(act, wdown_ref[cols, :])
        xhat, r3 = _rms(h2)
        diff = xhat * g3_ref[...] - tgt_ref[...]
        sq = jnp.sum(diff * diff, axis=1, keepdims=True)
        loss_ref[...] = loss_ref[...] + (0.5 / D_MODEL) * jnp.sum(sq, axis=0, keepdims=True)
        dy = diff * (1.0 / D_MODEL)
        dg3_ref[...] = dg3_ref[...] + _col_sum(dy * xhat)
        dh2 = _rms_bwd(dy * g3_ref[...], xhat, r3)
        dh2b = dh2.astype(BF16)
        dh2b_ref[...] = dh2b
        dn2 = jnp.zeros((SEQ_TILE, D_MODEL), F32)
        for k in range(N_CHIPS):
            cols = slice(k * D_MODEL, (k + 1) * D_MODEL)
            dup = (_dot_nt(dh2b, wdown_ref[cols, :]) * (2.0 * relu_ref[:, cols])).astype(BF16)
            dup_ref[:, cols] = dup
            dn2 = dn2 + _dot_nt(dup, wup_ref[k])
        xhat, r2 = _rms(h1)
        dg2_ref[...] = dg2_ref[...] + _col_sum(dn2 * xhat)
        dh1_ref[...] = dh2 + _rms_bwd(dn2 * g2_ref[...], xhat, r2)

    tile = pl.BlockSpec((SEQ_TILE, D_MODEL), lambda i: (i, 0))
    wide = pl.BlockSpec((SEQ_TILE, D_FF), lambda i: (i, 0))
    vec = _const((1, D_MODEL))
    vec_shape = jax.ShapeDtypeStruct((1, D_MODEL), F32)
    return _fused_call(
        body, jobs, name="mlp", grid=(t // SEQ_TILE,),
        in_specs=[tile, tile, tile, _resident((N_CHIPS, D_MODEL, D_MODEL)), _resident((D_FF, D_MODEL)), vec, vec],
        out_specs=[wide, wide, tile, tile, _const((SUBLANES, 128)), vec, vec],
        out_shape=[jax.ShapeDtypeStruct((t, D_FF), BF16), jax.ShapeDtypeStruct((t, D_FF), BF16),
                   jax.ShapeDtypeStruct((t, D_MODEL), BF16), jax.ShapeDtypeStruct((t, D_MODEL), F32),
                   jax.ShapeDtypeStruct((SUBLANES, 128), F32), vec_shape, vec_shape],
        scratch_shapes=[pltpu.VMEM((SEQ_TILE, D_FF), F32)],
        compiler_params=_params(),
    )(n2, h1, target, w_up_st, w_down, g2, g3)


def _bwd_mix(dh1, pa, pb, z, h, xc, r, ig, w_oa, w_ob, w_out, ln_g, ln_b, w_s, bias_full, conv_w, wr, wi, lam, jobs=()):
    t = dh1.shape[0]
    n_tiles = t // SEQ_TILE
    per_tile = SEQ_TILE // SUBLANES

    def merge_part(dh1_ref, pa_ref, pb_ref, m_ref, woa_ref, wob_ref, wout_ref, dz_ref, dya_ref, dyb_ref, mg_ref,
                   dpa_ref, dpb_ref, dh1b_ref):
        dh1b = dh1_ref[...].astype(BF16)
        dh1b_ref[...] = dh1b
        dm = _dot_nt(dh1b, wout_ref[...])
        pa = pa_ref[...]
        pb = pb_ref[...]
        sa = _gate(m_ref[:, :D_MODEL])
        sb = _gate(m_ref[:, D_MODEL:])
        mg_ref[...] = (sa * pa + sb * pb).astype(BF16)
        dpa = dm * sa
        dpb = dm * sb
        dz_ref[:, :D_MODEL] = ((dpa * pa) * (1.0 - sa)).astype(BF16)
        dz_ref[:, D_MODEL:] = ((dpb * pb) * (1.0 - sb)).astype(BF16)
        dpa = dpa.astype(BF16)
        dpb = dpb.astype(BF16)
        dpa_ref[...] = dpa
        dpb_ref[...] = dpb
        dya_ref[...] = _dot_nt(dpa, woa_ref[...])
        dyb_ref[...] = _dot_nt(dpb, wob_ref[...])

    def sgu_part(dyb_ref, ub_ref, vb_ref, lg_ref, lb_ref, ws_ref, bias_ref, dz_ref, dlg_ref, dlb_ref, dws_ref, dbs_ref,
                 dvn_ref, dsp_acc):
        i = pl.program_id(0)

        @pl.when(i == 0)
        def _():
            dlg_ref[...] = jnp.zeros_like(dlg_ref)
            dlb_ref[...] = jnp.zeros_like(dlb_ref)
            dws_ref[...] = jnp.zeros_like(dws_ref)
            dsp_acc[...] = jnp.zeros_like(dsp_acc)

        u, du, dvg, rstd, vhat, vn = _sgu_forward_parts(ub_ref[...], vb_ref[...], lg_ref, lb_ref)
        dyb = dyb_ref[...]
        mask = _causal_mask()
        wm = [jnp.where(mask, ws_ref[g], 0.0).astype(BF16) for g in range(GROUPS)]
        for c in range(SEQ_TILE // CHUNK):
            rows = slice(c * CHUNK, (c + 1) * CHUNK)
            for g in range(GROUPS):
                cols = slice(g * GROUP_DIM, (g + 1) * GROUP_DIM)
                vn_blk = vn[rows, cols]
                sp = _dot(wm[g], vn_blk) + bias_ref[:, cols]
                dyb_blk = dyb[rows, cols]
                dz_ref[rows, cols] = (dyb_blk * sp * du[rows, cols]).astype(BF16)
                dsp = dyb_blk * u[rows, cols]
                dsp_acc[:, cols] = dsp_acc[:, cols] + dsp
                dspb = dsp.astype(BF16)
                dvn_ref[rows, cols] = _dot_tn(wm[g], dspb)
                wcols = slice(g * CHUNK, (g + 1) * CHUNK)
                dws_ref[:, wcols] = dws_ref[:, wcols] + jnp.where(mask, _dot_nt(dspb, vn_blk), 0.0)
        dvn = dvn_ref[...]
        dlg_ref[...] = dlg_ref[...] + _col_sum(dvn * vhat)
        dlb_ref[...] = dlb_ref[...] + _col_sum(dvn)
        dvhat = dvn * lg_ref[...]
        dvgel = rstd * (dvhat - jnp.mean(dvhat, axis=-1, keepdims=True)
                        - vhat * jnp.mean(dvhat * vhat, axis=-1, keepdims=True))
        dz_ref[:, D_MODEL:] = (dvgel * dvg).astype(BF16)

        @pl.when(i == n_tiles - 1)
        def _():
            lane = lax.broadcasted_iota(jnp.int32, (CHUNK, 128), 1)
            out = jnp.zeros((CHUNK, 128), F32)
            for g in range(GROUPS):
                s = jnp.sum(dsp_acc[:, g * GROUP_DIM:(g + 1) * GROUP_DIM], axis=1, keepdims=True)
                out = out + jnp.where(lane == g, s, 0.0)
            dbs_ref[...] = out

    def lru_part(dya_ref, xa_ref, ga_ref, h_ref, h_prev_ref, xc_ref, r_ref, ig_ref, cw_ref, wr_ref, wi_ref, lam_ref,
                 dz_ref, dcw_ref, dcb_ref, dwr_ref, dbr_ref, dwi_ref, dbi_ref, dlam_ref, lam_carry, dxc_head):
        i = pl.program_id(0)

        @pl.when(i == 0)
        def _():
            for ref in (dcw_ref, dcb_ref, dwr_ref, dbr_ref, dwi_ref, dbi_ref, dlam_ref, lam_carry, dxc_head):
                ref[...] = jnp.zeros_like(ref)

        first_tile = i == n_tiles - 1
        h_tail = jnp.where(first_tile, 0.0, h_prev_ref[...])
        xc, r, ig = xc_ref[...], r_ref[...], ig_ref[...]
        xcb = xc.astype(BF16)
        sp, a, mult, inv_mult = _decay(r, lam_ref)
        h = h_ref[...]
        h_prev = _shift_down(h, h_tail, 1)
        dya = dya_ref[...]
        gg, dgg = _gelu_and_grad(ga_ref[...].astype(F32))
        dz_ref[:, D_MODEL:] = (dya * h * dgg).astype(BF16)
        ones = jnp.ones((SUBLANES, D_MODEL), F32)
        lam_t, lam_first = _scan_backward(_shift_up(a, ones, 1), dya * gg, lam_carry[...])
        lam_carry[...] = a[0:1] * lam_first
        lam_ig = lam_t * ig
        dxc_direct = lam_ig * mult
        dmult = lam_ig * xc
        dla = a * (lam_t * h_prev - (dmult * a) * inv_mult)
        dla_r = dla * r
        dlam_ref[...] = dlam_ref[...] + _col_sum(dla_r) * (LRU_C * jax.nn.sigmoid(-lam_ref[...]))
        dpr = (dla_r * ((-LRU_C) * sp)) * (1.0 - r)
        dpi = (dxc_direct * xc) * (1.0 - ig)
        dbr_ref[...] = dbr_ref[...] + _col_sum(dpr)
        dbi_ref[...] = dbi_ref[...] + _col_sum(dpi)
        dprb = dpr.astype(BF16)
        dpib = dpi.astype(BF16)
        dxc_gate = []
        for hd in range(HEADS):
            cols = slice(hd * HEAD_DIM, (hd + 1) * HEAD_DIM)
            dxc_gate.append(_dot_nt(dprb[:, cols], wr_ref[hd]) + _dot_nt(dpib[:, cols], wi_ref[hd]))
            dwr_ref[hd] = dwr_ref[hd] + _dot_tn(xcb[:, cols], dprb[:, cols])
            dwi_ref[hd] = dwi_ref[hd] + _dot_tn(xcb[:, cols], dpib[:, cols])
        dxc = dxc_direct + jnp.concatenate(dxc_gate, axis=1)
        dcb_ref[...] = dcb_ref[...] + _col_sum(dxc)
        cw = cw_ref[...]
        head = dxc_head[...]
        xa = xa_ref[...].astype(F32)
        dxa = cw[0:1] * dxc
        dcw_ref[0:1, :] = dcw_ref[0:1, :] + _col_sum(dxc * xa)
        for k in range(1, CONV_WIDTH):
            dxc_k = _shift_up(dxc, head, k)
            dxa = dxa + cw[k:k + 1] * dxc_k
            dcw_ref[k:k + 1, :] = dcw_ref[k:k + 1, :] + _col_sum(dxc_k * xa)
        dxc_head[...] = dxc[0:SUBLANES]
        dz_ref[:, :D_MODEL] = dxa.astype(BF16)

    def body(dh1_ref, pa_ref, pb_ref, z_ref, h_ref, h_prev_ref, xc_ref, r_ref, ig_ref, woa_ref, wob_ref, wout_ref,
             lg_ref, lb_ref, ws_ref, bias_ref, cw_ref, wr_ref, wi_ref, lam_ref, dz_ref, mg_ref, dpa_ref, dpb_ref,
             dh1b_ref, dlg_ref, dlb_ref, dws_ref, dbs_ref, dcw_ref, dcb_ref, dwr_ref, dbr_ref, dwi_ref, dbi_ref,
             dlam_ref, dya_ref, dyb_ref, dvn_ref, dsp_acc, lam_carry, dxc_head):
        def cols(ref, first, count):
            return ref.at[:, pl.ds(first * D_MODEL, count * D_MODEL)]

        merge_part(dh1_ref, pa_ref, pb_ref, cols(z_ref, 4, 2), woa_ref, wob_ref, wout_ref, cols(dz_ref, 4, 2), dya_ref,
                   dyb_ref, mg_ref, dpa_ref, dpb_ref, dh1b_ref)
        sgu_part(dyb_ref, cols(z_ref, 2, 1), cols(z_ref, 3, 1), lg_ref, lb_ref, ws_ref, bias_ref, cols(dz_ref, 2, 2),
                 dlg_ref, dlb_ref, dws_ref, dbs_ref, dvn_ref, dsp_acc)
        lru_part(dya_ref, cols(z_ref, 0, 1), cols(z_ref, 1, 1), h_ref, h_prev_ref, xc_ref, r_ref, ig_ref, cw_ref, wr_ref,
                 wi_ref, lam_ref, cols(dz_ref, 0, 2), dcw_ref, dcb_ref, dwr_ref, dbr_ref, dwi_ref, dbi_ref, dlam_ref,
                 lam_carry, dxc_head)

    rev = lambda i: n_tiles - 1 - i
    tile = pl.BlockSpec((SEQ_TILE, D_MODEL), lambda i: (rev(i), 0))
    row = pl.BlockSpec((SEQ_TILE, D_IN), lambda i: (rev(i), 0))
    prev8 = pl.BlockSpec((SUBLANES, D_MODEL), lambda i: (jnp.maximum(rev(i) * per_tile - 1, 0), 0))
    vec = _const((1, D_MODEL))
    sq = _resident((D_MODEL, D_MODEL))
    gate_w = _resident((HEADS, HEAD_DIM, HEAD_DIM))
    gate_acc = _const((HEADS, HEAD_DIM, HEAD_DIM))
    vec_shape = jax.ShapeDtypeStruct((1, D_MODEL), F32)
    gate_shape = jax.ShapeDtypeStruct((HEADS, HEAD_DIM, HEAD_DIM), F32)
    act_bf = jax.ShapeDtypeStruct((t, D_MODEL), BF16)
    return _fused_call(
        body, jobs, name="bwd_mix", grid=(n_tiles,),
        in_specs=[tile, tile, tile, row, tile, prev8, tile, tile, tile, sq, sq, sq, vec, vec,
                  _const((GROUPS, CHUNK, CHUNK)), _const((CHUNK, D_MODEL)), _const((CONV_WIDTH, D_MODEL)), gate_w, gate_w,
                  vec],
        out_specs=[row, tile, tile, tile, tile, vec, vec, _const((CHUNK, GROUPS * CHUNK)), _const((CHUNK, 128)),
                   _const((SUBLANES, D_MODEL)), vec, gate_acc, vec, gate_acc, vec, vec],
        out_shape=[jax.ShapeDtypeStruct((t, D_IN), BF16), act_bf, act_bf, act_bf, act_bf, vec_shape, vec_shape,
                   jax.ShapeDtypeStruct((CHUNK, GROUPS * CHUNK), F32), jax.ShapeDtypeStruct((CHUNK, 128), F32),
                   jax.ShapeDtypeStruct((SUBLANES, D_MODEL), F32), vec_shape, gate_shape, vec_shape, gate_shape,
                   vec_shape, vec_shape],
        scratch_shapes=[pltpu.VMEM((SEQ_TILE, D_MODEL), F32), pltpu.VMEM((SEQ_TILE, D_MODEL), F32),
                        pltpu.VMEM((SEQ_TILE, D_MODEL), F32), pltpu.VMEM((CHUNK, D_MODEL), F32),
                        pltpu.VMEM((1, D_MODEL), F32), pltpu.VMEM((SUBLANES, D_MODEL), F32)],
        compiler_params=_params(),
    )(dh1, pa, pb, z, h, h, xc, r, ig, w_oa, w_ob, w_out, ln_g, ln_b, w_s, bias_full, conv_w, wr, wi, lam)


def _bwd_in(dz, x, dh1, w_in_st, g1, jobs=()):
    t = x.shape[0]

    def body(dz_ref, x_ref, dh1_ref, w_ref, g_ref, dx_ref, dg1_ref):
        @pl.when(pl.program_id(0) == 0)
        def _():
            dg1_ref[...] = jnp.zeros_like(dg1_ref)

        dn1 = jnp.zeros((MM_TILE, D_MODEL), F32)
        for k in range(N_CHIPS):
            dn1 = dn1 + _dot_nt(dz_ref[:, k * IN_SHARD:(k + 1) * IN_SHARD], w_ref[k])
        xhat, r1 = _rms(x_ref[...])
        dg1_ref[...] = dg1_ref[...] + _col_sum(dn1 * xhat)
        dx_ref[...] = dh1_ref[...] + _rms_bwd(dn1 * g_ref[...], xhat, r1)

    tile = pl.BlockSpec((MM_TILE, D_MODEL), lambda i: (i, 0))
    return _fused_call(
        body, jobs, name="bwd_in", grid=(t // MM_TILE,),
        in_specs=[pl.BlockSpec((MM_TILE, D_IN), lambda i: (i, 0)), tile, tile,
                  _resident((N_CHIPS, D_MODEL, IN_SHARD)), _const((1, D_MODEL))],
        out_specs=[tile, _const((1, D_MODEL))],
        out_shape=[jax.ShapeDtypeStruct((t, D_MODEL), F32), jax.ShapeDtypeStruct((1, D_MODEL), F32)],
        compiler_params=_params(),
    )(dz, x, dh1, w_in_st, g1)


def _weight_grad(name, a, b, n_blocks, a_varies, b_varies, width, jobs=()):
    t = a.shape[0]
    rows = min(DW_TILE, t)
    n_t = t // rows

    def body(a_ref, b_ref, o_ref, acc_ref):
        s = pl.program_id(1)
        part = _dot_tn(a_ref[...], b_ref[...])

        @pl.when(s == 0)
        def _():
            acc_ref[...] = part

        @pl.when(s > 0)
        def _():
            acc_ref[...] = acc_ref[...] + part

        @pl.when(s == n_t - 1)
        def _():
            o_ref[...] = acc_ref[...].astype(BF16)

    return _fused_call(
        body, jobs, name=name, grid=(n_blocks, n_t),
        in_specs=[pl.BlockSpec((rows, D_MODEL), (lambda j, s: (s, j)) if a_varies else (lambda j, s: (s, 0))),
                  pl.BlockSpec((rows, width), (lambda j, s: (s, j)) if b_varies else (lambda j, s: (s, 0)))],
        out_specs=pl.BlockSpec((None, D_MODEL, width), lambda j, s: (j, 0, 0)),
        out_shape=jax.ShapeDtypeStruct((n_blocks, D_MODEL, width), BF16),
        scratch_shapes=[pltpu.VMEM((D_MODEL, width), F32)],
        compiler_params=_params(2),
    )(a, b)


def _weight_grads_square(name, pairs, jobs=()):
    n = len(pairs)
    t = pairs[0][0].shape[0]
    rows = min(2 * MM_TILE, t)
    n_t = t // rows

    def body(*refs):
        ins, outs, accs = refs[:2 * n], refs[2 * n:3 * n], refs[3 * n:]
        s = pl.program_id(0)
        for k in range(n):
            part = _dot_tn(ins[2 * k][...], ins[2 * k + 1][...])

            @pl.when(s == 0)
            def _(k=k, part=part):
                accs[k][...] = part

            @pl.when(s > 0)
            def _(k=k, part=part):
                accs[k][...] = accs[k][...] + part

            @pl.when(s == n_t - 1)
            def _(k=k):
                outs[k][...] = accs[k][...].astype(BF16)

    tile = pl.BlockSpec((rows, D_MODEL), lambda s: (s, 0))
    return _fused_call(
        body, jobs, name=name, grid=(n_t,), in_specs=[tile] * (2 * n), out_specs=[_const((D_MODEL, D_MODEL))] * n,
        out_shape=[jax.ShapeDtypeStruct((D_MODEL, D_MODEL), BF16)] * n,
        scratch_shapes=[pltpu.VMEM((D_MODEL, D_MODEL), F32)] * n,
        compiler_params=_params(),
    )(*[x for pair in pairs for x in pair])


def _place():
    x, y, c = lax.axis_index("x"), lax.axis_index("y"), lax.axis_index("c")
    other_chips = [(1 - x, y), (x, 1 - y), (1 - x, 1 - y)]
    return x, y, c, other_chips


def _chip_index(px, py):
    return 2 * px + py


ANY = pl.BlockSpec(memory_space=pl.ANY)
SIBLING = ((0, 0, 1),)
NEIGHBOURS = ((1, 0, 0), (0, 1, 0))
OTHER_CHIPS = NEIGHBOURS + ((1, 1, 0),)


def _near_far(x, y, c):
    return (x ^ (1 - c), y ^ c), (x ^ c, y ^ (1 - c))


def _gather_near_job(shards):
    n = len(shards)
    halves = [s.shape[0] // 2 for s in shards]

    def copies(ins, outs, send, recv, local):
        x, y, c, _ = _place()
        near, _ = _near_far(x, y, c)

        def block(w, chip, pc):
            return outs[w].at[_chip_index(*chip), pl.ds(pc * halves[w], halves[w]), :]

        def copy(w, k, chip, pc, to, src=None):
            return pltpu.make_async_remote_copy(
                src_ref=block(w, chip, pc) if src is None else src, dst_ref=block(w, chip, pc),
                send_sem=send.at[2 * w + k], recv_sem=recv.at[2 * w + k], device_id=to, device_id_type=MESH)

        sends, arrivals, own = [], [], []
        for w in range(n):
            src = ins[w].at[pl.ds(c * halves[w], halves[w]), :]
            own.append(pltpu.make_async_copy(src, block(w, (x, y), c), local.at[w]))
            sends += [copy(w, 0, (x, y), c, (*near, c), src), copy(w, 1, (x, y), c, (x, y, 1 - c), src)]
            arrivals += [copy(w, 0, near, c, (x, y, c)), copy(w, 1, (x, y), 1 - c, (x, y, c))]
        return sends, arrivals, own

    return _Job(shards, [jax.ShapeDtypeStruct((N_CHIPS,) + s.shape, s.dtype) for s in shards], 2 * n, copies,
                NEIGHBOURS + SIBLING, n_local=n)


def _gather_far_job(stacked):
    n = len(stacked)
    halves = [s.shape[1] // 2 for s in stacked]

    def copies(ins, outs, send, recv, local):
        del ins, local
        x, y, c, _ = _place()
        near, far = _near_far(x, y, c)

        def copy(w, k, chip):
            blk = outs[w].at[_chip_index(*chip), pl.ds(c * halves[w], halves[w]), :]
            return pltpu.make_async_remote_copy(
                src_ref=blk, dst_ref=blk, send_sem=send.at[2 * w + k], recv_sem=recv.at[2 * w + k],
                device_id=(*far, c), device_id_type=MESH)

        sends = [copy(w, k, chip) for w in range(n) for k, chip in enumerate(((x, y), near))]
        arrivals = [copy(w, k, chip) for w in range(n) for k, chip in enumerate((far, (1 - x, 1 - y)))]
        return sends, arrivals, []

    return _Job(stacked, [jax.ShapeDtypeStruct(s.shape, s.dtype) for s in stacked], 2 * n, copies, NEIGHBOURS,
                aliases={w: w for w in range(n)})


def _gather_pass_job(stacked):
    n = len(stacked)
    halves = [s.shape[1] // 2 for s in stacked]

    def copies(ins, outs, send, recv, local):
        del ins, local
        x, y, c, chips = _place()

        def copy(w, j, chip, pc, to):
            blk = outs[w].at[_chip_index(*chip), pl.ds(pc * halves[w], halves[w]), :]
            return pltpu.make_async_remote_copy(
                src_ref=blk, dst_ref=blk, send_sem=send.at[3 * w + j], recv_sem=recv.at[3 * w + j], device_id=to,
                device_id_type=MESH)

        sends = [copy(w, j, chip, c, (x, y, 1 - c)) for w in range(n) for j, chip in enumerate(chips)]
        arrivals = [copy(w, j, chip, 1 - c, (x, y, c)) for w in range(n) for j, chip in enumerate(chips)]
        return sends, arrivals, []

    return _Job(stacked, [jax.ShapeDtypeStruct(s.shape, s.dtype) for s in stacked], 3 * n, copies, SIBLING,
                aliases={w: w for w in range(n)})


def _gather_small_job(block):
    def copies(ins, outs, send, recv, local):
        x, y, c, chips = _place()

        def copy(j, chip_from, to):
            return pltpu.make_async_remote_copy(
                src_ref=ins[0], dst_ref=outs[0].at[_chip_index(*chip_from)], send_sem=send.at[j],
                recv_sem=recv.at[j], device_id=to, device_id_type=MESH)

        own = [pltpu.make_async_copy(ins[0], outs[0].at[_chip_index(x, y)], local.at[0])]
        sends = [copy(j, (x, y), (*chip, c)) for j, chip in enumerate(chips)]
        arrivals = [copy(j, chip, (x, y, c)) for j, chip in enumerate(chips)]
        return sends, arrivals, own

    return _Job([block], [jax.ShapeDtypeStruct((N_CHIPS,) + block.shape, block.dtype)], 3, copies, OTHER_CHIPS,
                n_local=1)


def _pair_send_job(grads):
    n = len(grads)
    halves = [g.shape[1] // 2 for g in grads]

    def copies(ins, outs, send, recv, local):
        del local
        x, y, c, _ = _place()
        sends = [pltpu.make_async_remote_copy(
            src_ref=ins[w].at[:, pl.ds((1 - c) * halves[w], halves[w]), :], dst_ref=outs[w], send_sem=send.at[w],
            recv_sem=recv.at[w], device_id=(x, y, 1 - c), device_id_type=MESH) for w in range(n)]
        return sends, sends, []

    return _Job(grads, [jax.ShapeDtypeStruct((N_CHIPS, h, g.shape[2]), g.dtype) for g, h in zip(grads, halves)], n,
                copies, SIBLING)


ROW_STEPS = 4


def _pair_add(name, core, mine, theirs):
    n = len(mine)

    def body(core_ref, *refs):
        del core_ref
        for a_ref, b_ref, o_ref in zip(refs[:n], refs[n:2 * n], refs[2 * n:]):
            o_ref[...] = (a_ref[...].astype(F32) + b_ref[...].astype(F32)).astype(BF16)

    half = lambda a: pl.BlockSpec((2, None) + a.shape[2:], lambda k, core_ref: (k, core_ref[0], 0, 0))
    block = lambda b: pl.BlockSpec((2,) + b.shape[1:], lambda k, core_ref: (k, 0, 0))
    return pl.pallas_call(
        body, name=name,
        grid_spec=pltpu.PrefetchScalarGridSpec(
            num_scalar_prefetch=1, grid=(N_CHIPS // 2,),
            in_specs=[half(a) for a in mine] + [block(b) for b in theirs], out_specs=[block(b) for b in theirs]),
        out_shape=[jax.ShapeDtypeStruct(b.shape, BF16) for b in theirs],
        compiler_params=_params(),
    )(core, *mine, *theirs)


def _sequencer_call(name, collective_id, job):
    steps, peers = job.phases, job.peers
    ins = [jax.new_ref(a, memory_space=pltpu.MemorySpace.HBM) for a in job.inputs]
    outs = [ins[{o: i for i, o in job.aliases.items()}[k]] if k in job.aliases.values()
            else jax.empty_ref(shape, memory_space=pltpu.MemorySpace.HBM) for k, shape in enumerate(job.out_shape)]
    sems = [pltpu.SemaphoreType.DMA((n,)) for step in steps for n in (step.n_sem, step.n_sem, max(step.n_local, 1))]

    @pl.kernel(mesh=plsc.ScalarSubcoreMesh(axis_name="sequencer", num_cores=1), name=name, scratch_types=tuple(sems),
               compiler_params=pltpu.CompilerParams(collective_id=collective_id))
    def launch(*sem_refs):
        x, y, c, _ = _place()
        barrier = pltpu.get_barrier_semaphore()
        for dx, dy, dc in peers:
            pl.semaphore_signal(barrier, inc=1, device_id=(x ^ dx, y ^ dy, c ^ dc), device_id_type=MESH)
        pl.semaphore_wait(barrier, len(peers))
        for k, step in enumerate(steps):
            sends, arrivals, own = step.copies(ins if k == 0 else outs, outs, *sem_refs[3 * k:3 * k + 3])
            for cp in own + sends:
                cp.start()
            for cp in arrivals:
                cp.wait_recv()
            for cp in sends:
                cp.wait_send()
            for cp in own:
                cp.wait()

    launch()
    return [ref[...] for ref in outs]


def _chip_exchange_job(sums):
    n = len(sums)

    def copies(ins, outs, send, recv, local):
        del local
        _, _, c, chips = _place()
        sends = [pltpu.make_async_remote_copy(
            src_ref=ins[w].at[_chip_index(*chip)], dst_ref=outs[w].at[j], send_sem=send.at[3 * w + j],
            recv_sem=recv.at[3 * w + j], device_id=(*chip, c), device_id_type=MESH)
            for w in range(n) for j, chip in enumerate(chips)]
        return sends, sends, []

    return _Job(sums, [jax.ShapeDtypeStruct((N_CHIPS - 1,) + s.shape[1:], s.dtype) for s in sums], 3 * n, copies,
                OTHER_CHIPS)


def _chip_sum(name, place, mine, theirs):
    n = len(mine)

    def body(place_ref, *refs):
        del place_ref
        for p_ref, q_ref, o_ref in zip(refs[:n], refs[n:2 * n], refs[2 * n:]):
            acc = p_ref[...].astype(F32)
            for j in range(N_CHIPS - 1):
                acc = acc + q_ref[j].astype(F32)
            o_ref[...] = acc

    def block(p, lead, pick):
        return pl.BlockSpec((lead, p.shape[1] // ROW_STEPS, p.shape[2]), lambda r, place_ref: (pick(place_ref), r, 0))

    return pl.pallas_call(
        body, name=name,
        grid_spec=pltpu.PrefetchScalarGridSpec(
            num_scalar_prefetch=1, grid=(ROW_STEPS,),
            in_specs=[block(p, None, lambda place_ref: place_ref[0]) for p in mine]
            + [block(p, N_CHIPS - 1, lambda place_ref: 0) for p in mine],
            out_specs=[block(p, None, lambda place_ref: place_ref[1]) for p in mine]),
        out_shape=[jax.ShapeDtypeStruct((2,) + p.shape[1:], F32) for p in mine],
        compiler_params=_params(),
    )(place, *mine, *theirs)


def _share_job(bufs):
    n = len(bufs)

    def copies(ins, outs, send, recv, local):
        del ins, local
        x, y, c, _ = _place()

        def copy(w, half):
            return pltpu.make_async_remote_copy(
                src_ref=outs[w].at[half], dst_ref=outs[w].at[half], send_sem=send.at[w], recv_sem=recv.at[w],
                device_id=(x, y, 1 - c), device_id_type=MESH)

        return [copy(w, c) for w in range(n)], [copy(w, 1 - c) for w in range(n)], []

    return _Job(bufs, [jax.ShapeDtypeStruct(b.shape, b.dtype) for b in bufs], n, copies, SIBLING,
                aliases={w: w for w in range(n)})


SMALL_ROWS = 24
ROW_G1, ROW_CW, ROW_CB, ROW_BR, ROW_BI, ROW_LAM, ROW_LG, ROW_LB, ROW_G2, ROW_G3, ROW_LOSS, ROW_BS = (
    0, 1, 5, 6, 7, 8, 9, 10, 11, 12, 13, 16)
N_DEV = 8


def _pack_small(dcw, dcb, dbr, dbi, dlam, dlg, dlb, dg2, dg3, loss, dbs):
    def body(dcw_ref, dcb_ref, dbr_ref, dbi_ref, dlam_ref, dlg_ref, dlb_ref, dg2_ref, dg3_ref, loss_ref, dbs_ref, out):
        out[...] = jnp.zeros((SMALL_ROWS, D_MODEL), F32)
        for row, ref in ((ROW_CB, dcb_ref), (ROW_BR, dbr_ref), (ROW_BI, dbi_ref), (ROW_LAM, dlam_ref),
                         (ROW_LG, dlg_ref), (ROW_LB, dlb_ref), (ROW_G2, dg2_ref), (ROW_G3, dg3_ref)):
            out[row:row + 1, :] = ref[...]
        out[ROW_CW:ROW_CW + CONV_WIDTH, :] = dcw_ref[0:CONV_WIDTH, :]
        out[ROW_LOSS:ROW_LOSS + 1, 0:128] = loss_ref[0:1, :]
        out[ROW_BS:ROW_BS + GROUPS, 0:128] = jnp.transpose(dbs_ref[...])[0:GROUPS, :]

    vm = pl.BlockSpec(memory_space=pltpu.VMEM)
    return pl.pallas_call(
        body, name="pack_small", in_specs=[vm] * 11, out_specs=vm,
        out_shape=jax.ShapeDtypeStruct((SMALL_ROWS, D_MODEL), F32),
    )(dcw, dcb, dbr, dbi, dlam, dlg, dlb, dg2, dg3, loss, dbs)


def _gather_all_job(blocks):
    n = len(blocks)
    flips = [(dx, dy, dc) for dx in (0, 1) for dy in (0, 1) for dc in (0, 1)][1:]

    def copies(ins, outs, send, recv, local):
        x, y, c, _ = _place()
        me = 4 * x + 2 * y + c
        sends, arrivals, own = [], [], []
        for w in range(n):
            own.append(pltpu.make_async_copy(ins[w], outs[w].at[me], local.at[w]))
            for k, (dx, dy, dc) in enumerate(flips):
                peer = (x ^ dx, y ^ dy, c ^ dc)
                sem = dict(send_sem=send.at[7 * w + k], recv_sem=recv.at[7 * w + k])
                sends.append(pltpu.make_async_remote_copy(
                    src_ref=ins[w], dst_ref=outs[w].at[me], device_id=peer, device_id_type=MESH, **sem))
                arrivals.append(pltpu.make_async_remote_copy(
                    src_ref=ins[w], dst_ref=outs[w].at[4 * peer[0] + 2 * peer[1] + peer[2]], device_id=peer,
                    device_id_type=MESH, **sem))
        return sends, arrivals, own

    return _Job(blocks, [jax.ShapeDtypeStruct((N_DEV,) + b.shape, b.dtype) for b in blocks], 7 * n, copies,
                OTHER_CHIPS + SIBLING + tuple((dx, dy, 1) for dx, dy, _ in OTHER_CHIPS), n_local=n)


def _sum_small(vec_all, ws_all, dg1_all):
    def body(vec_ref, ws_ref, dg1_ref, vec_out, ws_out):
        vec, ws, dg1 = vec_ref[0], ws_ref[0], dg1_ref[0]
        for d in range(1, N_DEV):
            vec, ws, dg1 = vec + vec_ref[d], ws + ws_ref[d], dg1 + dg1_ref[d]
        vec_out[...] = vec
        vec_out[ROW_G1:ROW_G1 + 1, :] = dg1
        ws_out[...] = ws

    vm = pl.BlockSpec(memory_space=pltpu.VMEM)
    return pl.pallas_call(
        body, name="sum_small", in_specs=[vm] * 3, out_specs=[vm, vm],
        out_shape=[jax.ShapeDtypeStruct(vec_all.shape[1:], F32), jax.ShapeDtypeStruct(ws_all.shape[1:], F32)],
    )(vec_all, ws_all, dg1_all)


def _adamw_math(w, g, m, v):
    m = ADAM_B1 * m + (1.0 - ADAM_B1) * g
    v = ADAM_B2 * v + (1.0 - ADAM_B2) * (g * g)
    m_hat = m / (1.0 - ADAM_B1 ** ADAM_STEP)
    v_hat = v / (1.0 - ADAM_B2 ** ADAM_STEP)
    delta = (-ADAM_LR) * (m_hat / (jnp.sqrt(v_hat) + ADAM_EPS) + ADAM_WD * w)
    return delta, m, v


def _adamw(name, gs, ws, ms, vs):
    n = len(ws)

    def body(*refs):
        ins, outs = refs[:4 * n], refs[4 * n:]
        for p in range(n):
            g_ref, w_ref, m_ref, v_ref = ins[p::n]
            g = g_ref[...]
            outs[4 * p][...] = g
            outs[4 * p + 1][...], outs[4 * p + 2][...], outs[4 * p + 3][...] = _adamw_math(
                w_ref[...], g, m_ref[...], v_ref[...])

    blocks = [pl.BlockSpec((w.shape[0] // ROW_STEPS, w.shape[1]), lambda r: (r, 0)) for w in ws]
    out = pl.pallas_call(
        body, name=name, grid=(ROW_STEPS,), in_specs=blocks * 4, out_specs=[b for b in blocks for _ in range(4)],
        out_shape=[jax.ShapeDtypeStruct(w.shape, F32) for w in ws for _ in range(4)], compiler_params=_params(),
    )(*gs, *ws, *ms, *vs)
    return [tuple(out[4 * p:4 * p + 4]) for p in range(n)]


def _adamw_small(grads, ws, ms, vs):
    n = len(grads)

    def body(*refs):
        g_refs, w_refs, m_refs, v_refs = refs[:n], refs[n:2 * n], refs[2 * n:3 * n], refs[3 * n:4 * n]
        outs = refs[4 * n:]
        for p in range(n):
            d, nm, nv = _adamw_math(w_refs[p][...], g_refs[p][...], m_refs[p][...], v_refs[p][...])
            outs[p][...] = d
            outs[n + p][...] = nm
            outs[2 * n + p][...] = nv

    vm = pl.BlockSpec(memory_space=pltpu.VMEM)
    shapes = [jax.ShapeDtypeStruct(w.shape, F32) for w in ws]
    out = pl.pallas_call(
        body, name="adamw_small", in_specs=[vm] * (4 * n), out_specs=[vm] * (3 * n), out_shape=shapes * 3,
    )(*grads, *ws, *ms, *vs)
    return out[:n], out[n:2 * n], out[2 * n:]


def _unstack_heads(w_st):
    per = HEAD_DIM // N_CHIPS
    return w_st.reshape(N_CHIPS, HEADS, per, HEAD_DIM).transpose(1, 0, 2, 3).reshape(HEADS, HEAD_DIM, HEAD_DIM)


def _stack_heads(w):
    per = HEAD_DIM // N_CHIPS
    return w.reshape(HEADS, N_CHIPS, per, HEAD_DIM).transpose(1, 0, 2, 3).reshape(N_CHIPS, HEADS * per, HEAD_DIM)


def kernel(x, norm_mix_g, w_in, conv_w, conv_b, w_rgate, b_rgate, w_igate, b_igate, lru_lambda, w_out_a, sgu_ln_g, sgu_ln_b, sgu_w_s, sgu_b_s, w_out_b, w_out, norm_mlp_g, w_up, w_down, norm_final_g, loss_target, m_norm_mix_g, m_w_in, m_conv_w, m_conv_b, m_w_rgate, m_b_rgate, m_w_igate, m_b_igate, m_lru_lambda, m_w_out_a, m_sgu_ln_g, m_sgu_ln_b, m_sgu_w_s, m_sgu_b_s, m_w_out_b, m_w_out, m_norm_mlp_g, m_w_up, m_w_down, m_norm_final_g, v_norm_mix_g, v_w_in, v_conv_w, v_conv_b, v_w_rgate, v_b_rgate, v_w_igate, v_b_igate, v_lru_lambda, v_w_out_a, v_sgu_ln_g, v_sgu_ln_b, v_sgu_w_s, v_sgu_b_s, v_w_out_b, v_w_out, v_norm_mlp_g, v_w_up, v_w_down, v_norm_final_g):
    chip = _chip_index(lax.axis_index("x"), lax.axis_index("y"))
    core = lax.axis_index("c")
    quarter_h = HEAD_DIM // N_CHIPS
    quarter_d = D_MODEL // N_CHIPS

    as_2d = lambda a: a.reshape(-1, a.shape[-1])
    big_w = [as_2d(w) for w in (w_in, w_rgate, w_igate, w_out_a, w_out_b, w_out, w_up, w_down)]
    big_m = [as_2d(w) for w in (m_w_in, m_w_rgate, m_w_igate, m_w_out_a, m_w_out_b, m_w_out, m_w_up, m_w_down)]
    big_v = [as_2d(w) for w in (v_w_in, v_w_rgate, v_w_igate, v_w_out_a, v_w_out_b, v_w_out, v_w_up, v_w_down)]

    packed = jnp.concatenate([conv_w[0], b_rgate[0], b_igate[0]], axis=1)
    packed = jnp.concatenate([packed, jnp.zeros_like(packed)], axis=0)
    s_in, s_r, s_i, s_oa, s_ob, s_out, s_up, s_down = [w.astype(BF16) for w in big_w]
    xs, target = x[0], loss_target[0]
    g3 = norm_final_g.reshape(1, D_MODEL)
    bias_s = jnp.broadcast_to(jnp.transpose(sgu_b_s[0])[:, :, None], (CHUNK, GROUPS, GROUP_DIM)).reshape(CHUNK, D_MODEL)
    core_arr = core.reshape(1).astype(jnp.int32)
    place = jnp.stack([chip, core]).astype(jnp.int32)
    quarter = lambda g: g.reshape(N_CHIPS, D_MODEL // N_CHIPS, D_MODEL)

    def pair_add(nm, grads, from_sibling):
        halves = [g.reshape(N_CHIPS, 2, g.shape[1] // 2, g.shape[2]) for g in grads]
        return list(_pair_add("pair_add_" + nm, core_arr, halves, from_sibling))

    def chip_sum(nm, pairs, from_chips):
        return list(_chip_sum("chip_sum_" + nm, place, pairs, from_chips))

    order = jnp.stack([chip, chip ^ 2, chip ^ 1, chip ^ 3]).astype(jnp.int32)
    (z, n1, (w_in_st, wr_st, wi_st)), ((packed_all,), late) = _fwd_in(
        xs, norm_mix_g, [s_in, s_r, s_i], order,
        jobs=[_gather_small_job(packed), _gather_near_job([s_oa, s_ob, s_out])])
    pick = lambda lo, hi: packed_all[:, :HEADS, lo:hi].transpose(1, 0, 2).reshape(HEADS, -1)
    conv_w_full = pick(0, quarter_d)
    br_full = pick(quarter_d, quarter_d + quarter_h).reshape(1, D_MODEL)
    bi_full = pick(quarter_d + quarter_h, quarter_d + 2 * quarter_h).reshape(1, D_MODEL)
    wr, wi = _unstack_heads(wr_st), _unstack_heads(wi_st)
    lru = (conv_w_full, conv_b, wr, br_full, wi, bi_full, lru_lambda)
    sgu = (sgu_ln_g, sgu_ln_b, sgu_w_s[0], bias_s)

    after = lambda arrays, result: lax.optimization_barrier((arrays, result))[0]
    w_up_st, w_dn = _sequencer_call(
        "gather_mlp", 8, _gather_near_job(after([s_up, s_down], n1)).then(_gather_far_job).then(_gather_pass_job))
    w_dn = w_dn.reshape(D_FF, D_MODEL)
    late_step = (3 * (xs.shape[0] // SEQ_TILE) // 4,)
    (ya, *saved), (late,) = _fwd_lru(z, *lru, jobs=[_gather_far_job(late).then(_gather_pass_job, at=late_step)])
    w_oa, w_ob, w_o = [w.reshape(D_MODEL, D_MODEL) for w in late]
    (yb, pa, pb, h1, n2), _ = _fwd_sgu_merge(ya, z, xs, *sgu, w_oa, w_ob, w_o, norm_mlp_g)
    (act, dup, dh2b, dh1, loss_part, dg3, dg2), _ = _mlp(n2, h1, target, w_up_st, w_dn, norm_mlp_g, g3)

    d_down, _ = _weight_grad("dw_down", act, dh2b, N_CHIPS, True, False, D_MODEL)
    r_down, = _sequencer_call("send_w_down", 10, _pair_send_job([d_down]))
    d_up, _ = _weight_grad("dw_up", n2, dup, N_CHIPS, False, True, D_MODEL)
    r_up, = _sequencer_call("send_w_up", 11, _pair_send_job([d_up]))
    (p_down,), (p_up,) = pair_add("w_down", [d_down], [r_down]), pair_add("w_up", [d_up], [r_up])
    (dz, merged, dpa, dpb, dh1b, dlg, dlb, dws, dbs, dcw, dcb, dwr, dbr, dwi, dbi, dlam), ((q_up, q_down),) = _bwd_mix(
        dh1, pa, pb, z, *saved, w_oa, w_ob, w_o, *sgu, conv_w_full, wr, wi, lru_lambda,
        jobs=[_chip_exchange_job([p_up, p_down])])
    names = ("w_in", "w_rgate", "w_igate", "w_out_a", "w_out_b", "w_out", "w_up", "w_down")
    (d_out, d_oa, d_ob), _ = _weight_grads_square("dw_projections", [(merged, dh1b), (ya, dpa), (yb, dpb)])
    mids = [quarter(d_oa), quarter(d_ob), quarter(d_out)]
    r_mids = _sequencer_call("send_mids", 1, _pair_send_job(mids))
    half_up, half_down = chip_sum("mlp", [p_up, p_down], after([q_up, q_down], mids))
    gates = [_stack_heads(dwr).astype(BF16), _stack_heads(dwi).astype(BF16)]
    small = _pack_small(dcw, dcb, dbr, dbi, dlam, dlg, dlb, dg2, dg3, loss_part, dbs)
    p_mids = pair_add("projections", after(mids, [half_up, half_down]), r_mids)
    q_mids = _sequencer_call("exchange_mids", 2, _chip_exchange_job(p_mids))
    d_in, (r_gates, (vec_all, ws_all), (full_up, full_down)) = _weight_grad(
        "dw_in", n1, after(dz, p_mids), N_CHIPS, False, True, IN_SHARD,
        jobs=[_pair_send_job(gates), _gather_all_job([small, dws]), _share_job([half_up, half_down])])
    r_in, = _sequencer_call("send_w_in", 3, _pair_send_job(after([d_in], q_mids)))
    adam_args = {nm: (w, m, v) for nm, w, m, v in zip(names, big_w, big_m, big_v)}

    def adamw(group, nms, grads):
        given = [adam_args[nm] for nm in nms]
        grads = [g.reshape(w.shape) for g, (w, _, _) in zip(grads, given)]
        outs = _adamw("adamw_" + group, grads, *[[a[q] for a in given] for q in range(3)])
        return {nm: (out[0], out[1:]) for nm, out in zip(nms, outs)}

    p_gates = pair_add("gates", gates, r_gates)
    half_mids = chip_sum("projections", p_mids, q_mids)
    full_mids = _sequencer_call("share_mids", 12, _share_job(half_mids))
    p_first = pair_add("w_in", after([d_in], half_mids), [r_in]) + p_gates
    q_first = _sequencer_call("exchange_w_in", 4, _chip_exchange_job(p_first))
    (grad_x, dg1), _ = _bwd_in(dz, xs, dh1, w_in_st, norm_mix_g)
    dg1_all, = _sequencer_call("gather_dg1", 6, _gather_all_job([dg1]))
    done = adamw("mlp", ("w_up", "w_down"), [full_up, full_down])
    q_first = after(q_first, [out[0] for _, out in done.values()])
    half_first = chip_sum("first", p_first, q_first)
    full_first = _sequencer_call("share_last", 5, _share_job(half_first))
    done.update(adamw("projections", names[3:6], after(full_mids, half_first)))
    done.update(adamw("first", names[:3], full_first))
    full, big_out = [done[nm][0] for nm in names], [done[nm][1] for nm in names]

    vec, ws_sum = _sum_small(vec_all, ws_all, dg1_all)
    row = lambda r: vec[r:r + 1]
    shard = lambda a, width: lax.dynamic_slice_in_dim(a, chip * width, width, axis=1)
    g_small = dict(
        norm_mix_g=row(ROW_G1), conv_w=shard(vec[ROW_CW:ROW_CW + CONV_WIDTH], quarter_d), conv_b=row(ROW_CB),
        b_rgate=shard(row(ROW_BR).reshape(HEADS, HEAD_DIM), quarter_h),
        b_igate=shard(row(ROW_BI).reshape(HEADS, HEAD_DIM), quarter_h), lru_lambda=row(ROW_LAM),
        sgu_ln_g=row(ROW_LG), sgu_ln_b=row(ROW_LB),
        sgu_w_s=ws_sum.reshape(CHUNK, GROUPS, CHUNK).transpose(1, 0, 2).reshape(GROUPS * CHUNK, CHUNK),
        sgu_b_s=vec[ROW_BS:ROW_BS + GROUPS, 0:CHUNK], norm_mlp_g=row(ROW_G2), norm_final_g=row(ROW_G3))
    loss = vec[ROW_LOSS, 0]
    small_names = list(g_small)
    given = dict(
        norm_mix_g=(norm_mix_g, m_norm_mix_g, v_norm_mix_g), conv_w=(conv_w, m_conv_w, v_conv_w),
        conv_b=(conv_b, m_conv_b, v_conv_b), b_rgate=(b_rgate, m_b_rgate, v_b_rgate),
        b_igate=(b_igate, m_b_igate, v_b_igate), lru_lambda=(lru_lambda, m_lru_lambda, v_lru_lambda),
        sgu_ln_g=(sgu_ln_g, m_sgu_ln_g, v_sgu_ln_g), sgu_ln_b=(sgu_ln_b, m_sgu_ln_b, v_sgu_ln_b),
        sgu_w_s=(sgu_w_s, m_sgu_w_s, v_sgu_w_s), sgu_b_s=(sgu_b_s, m_sgu_b_s, v_sgu_b_s),
        norm_mlp_g=(norm_mlp_g, m_norm_mlp_g, v_norm_mlp_g), norm_final_g=(norm_final_g, m_norm_final_g, v_norm_final_g))
    g2d = [g_small[nm] for nm in small_names]
    to2d = lambda a, g: a.reshape(g.shape)
    d_s, m_s, v_s = _adamw_small(
        g2d, *[[to2d(given[nm][q], g) for nm, g in zip(small_names, g2d)] for q in range(3)])

    shapes = dict(
        norm_mix_g=norm_mix_g, w_in=w_in, conv_w=conv_w, conv_b=conv_b, w_rgate=w_rgate, b_rgate=b_rgate,
        w_igate=w_igate, b_igate=b_igate, lru_lambda=lru_lambda, w_out_a=w_out_a, sgu_ln_g=sgu_ln_g,
        sgu_ln_b=sgu_ln_b, sgu_w_s=sgu_w_s, sgu_b_s=sgu_b_s, w_out_b=w_out_b, w_out=w_out, norm_mlp_g=norm_mlp_g,
        w_up=w_up, w_down=w_down, norm_final_g=norm_final_g)
    grads, deltas, new_m, new_v = {}, {}, {}, {}
    for nm, g, (d, nmom, nvar) in zip(names, full, big_out):
        grads[nm], deltas[nm], new_m[nm], new_v[nm] = g, d, nmom, nvar
    for p, nm in enumerate(small_names):
        grads[nm], deltas[nm], new_m[nm], new_v[nm] = g2d[p], d_s[p], m_s[p], v_s[p]
    order = list(shapes)
    out = [loss, grad_x[None]]
    for group in (grads, deltas, new_m, new_v):
        out += [group[nm].reshape(shapes[nm].shape) for nm in order]
    return tuple(out)
```

```python
import functools

import jax
import jax.numpy as jnp
from jax import lax
from jax.experimental import pallas as pl
from jax.experimental.pallas import tpu as pltpu
from jax.experimental.pallas import tpu_sc as plsc

F32 = jnp.float32
BF16 = jnp.bfloat16
MESH = pl.DeviceIdType.MESH

D_MODEL = 1024
D_IN = 6 * D_MODEL
D_FF = 4 * D_MODEL
N_CHIPS = 4
IN_SHARD = D_IN // N_CHIPS
HEADS = 4
HEAD_DIM = D_MODEL // HEADS
GROUPS = 4
GROUP_DIM = D_MODEL // GROUPS
CHUNK = 128
CONV_WIDTH = 4
LRU_C = 8.0
NORM_EPS = 1e-6
LN_EPS = 1e-5

ADAM_LR = 0.001
ADAM_B1 = 0.9
ADAM_B2 = 0.999
ADAM_EPS = 1e-08
ADAM_WD = 0.01
ADAM_STEP = 10

SUBLANES = 8
MM_TILE = 512
IN_TILE = 1024
SEQ_TILE = 256
DW_TILE = 2048
VMEM_LIMIT_BYTES = 56 * 1024 * 1024

GELU_K0 = 0.7978845608028654
GELU_K1 = 0.044715


def _params(n_grid_axes=1):
    return pltpu.CompilerParams(
        dimension_semantics=("arbitrary",) * n_grid_axes, vmem_limit_bytes=VMEM_LIMIT_BYTES)


def _resident(shape):
    nd = len(shape)
    return pl.BlockSpec(shape, lambda *_: (0,) * nd, pipeline_mode=pl.Buffered(1))


def _const(shape):
    nd = len(shape)
    return pl.BlockSpec(shape, lambda *_: (0,) * nd)


def _dot(a, b):
    return jnp.dot(a, b, preferred_element_type=F32)


def _dot_nt(a, b):
    return lax.dot_general(a, b, (((1,), (1,)), ((), ())), preferred_element_type=F32)


def _dot_tn(a, b):
    return lax.dot_general(a, b, (((0,), (0,)), ((), ())), preferred_element_type=F32)


def _gelu(x):
    t = jnp.tanh(x * (GELU_K0 + (GELU_K0 * GELU_K1) * (x * x)))
    return x * (0.5 + 0.5 * t)


def _gelu_and_grad(x):
    x2 = x * x
    t = jnp.tanh(x * (GELU_K0 + (GELU_K0 * GELU_K1) * x2))
    s = 0.5 + 0.5 * t
    dg = s + (x * (1.0 - t * t)) * (0.5 * GELU_K0 + (1.5 * GELU_K0 * GELU_K1) * x2)
    return x * s, dg


def _gate(x):
    return 0.5 + 0.5 * jnp.tanh(0.5 * x.astype(F32))


def _rms(x):
    r = lax.rsqrt(jnp.mean(x * x, axis=-1, keepdims=True) + NORM_EPS)
    return x * r, r


def _rms_bwd(dn, xhat, r):
    return r * (dn - xhat * jnp.mean(dn * xhat, axis=-1, keepdims=True))


def _col_sum(v):
    return jnp.sum(v, axis=0, keepdims=True)


def _shift_down(x, tail8, k):
    xs = pltpu.roll(x, k, 0)
    ts = pltpu.roll(tail8, k, 0)
    ridx = lax.broadcasted_iota(jnp.int32, tail8.shape, 0)
    head = jnp.where(ridx < k, ts, xs[0:SUBLANES])
    return jnp.concatenate([head, xs[SUBLANES:]], axis=0)


def _shift_up(x, head8, k):
    n = x.shape[0]
    xs = pltpu.roll(x, n - k, 0)
    hs = pltpu.roll(head8, SUBLANES - k, 0)
    ridx = lax.broadcasted_iota(jnp.int32, head8.shape, 0)
    last = jnp.where(ridx >= SUBLANES - k, hs, xs[n - SUBLANES:n])
    return jnp.concatenate([xs[:n - SUBLANES], last], axis=0)


def _scan_forward(a, b, carry):
    n, cols = a.shape
    groups = n // SUBLANES
    a = a.reshape(groups, SUBLANES, cols)
    b = b.reshape(groups, SUBLANES, cols)
    sub = lax.broadcasted_iota(jnp.int32, a.shape, 1)
    for s in (1, 2, 4):
        a_s = pltpu.roll(a, s, 1)
        b_s = pltpu.roll(b, s, 1)
        m = sub >= s
        b = jnp.where(m, a * b_s + b, b)
        a = jnp.where(m, a * a_s, a)
    out = []
    for g in range(groups):
        h = a[g] * carry + b[g]
        out.append(h)
        carry = h[SUBLANES - 1:SUBLANES]
    return jnp.concatenate(out, axis=0), carry


def _scan_backward(a, b, carry):
    n, cols = a.shape
    groups = n // SUBLANES
    a = a.reshape(groups, SUBLANES, cols)
    b = b.reshape(groups, SUBLANES, cols)
    sub = lax.broadcasted_iota(jnp.int32, a.shape, 1)
    for s in (1, 2, 4):
        a_s = pltpu.roll(a, SUBLANES - s, 1)
        b_s = pltpu.roll(b, SUBLANES - s, 1)
        m = sub < SUBLANES - s
        b = jnp.where(m, a * b_s + b, b)
        a = jnp.where(m, a * a_s, a)
    out = [None] * groups
    for g in reversed(range(groups)):
        h = a[g] * carry + b[g]
        out[g] = h
        carry = h[0:1]
    return jnp.concatenate(out, axis=0), carry


def _softplus_neg(lam):
    e = jnp.exp(-jnp.abs(lam))
    u = 1.0 + e
    log1p_e = jnp.where(u == 1.0, e, jnp.log(u) * (e / jnp.where(u == 1.0, 1.0, u - 1.0)))
    return jnp.maximum(-lam, 0.0) + log1p_e


def _lru_gates(xa, tail8, cw_ref, cb_ref, wr_ref, br_ref, wi_ref, bi_ref, lam_ref):
    cw = cw_ref[...]
    xc = cb_ref[...] + cw[0:1] * xa
    for k in range(1, CONV_WIDTH):
        xc = xc + cw[k:k + 1] * _shift_down(xa, tail8, k)
    xcb = xc.astype(BF16)
    pre_r, pre_i = [], []
    for h in range(HEADS):
        cols = slice(h * HEAD_DIM, (h + 1) * HEAD_DIM)
        pre_r.append(_dot(xcb[:, cols], wr_ref[h]))
        pre_i.append(_dot(xcb[:, cols], wi_ref[h]))
    r = jax.nn.sigmoid(jnp.concatenate(pre_r, axis=1) + br_ref[...])
    ig = jax.nn.sigmoid(jnp.concatenate(pre_i, axis=1) + bi_ref[...])
    _, a, mult, _ = _decay(r, lam_ref)
    return xc, r, ig, a, mult


def _decay(r, lam_ref):
    sp = _softplus_neg(lam_ref[...])
    log_a = ((-LRU_C) * sp) * r
    a = jnp.exp(log_a)
    th = jnp.tanh(log_a)
    q = (-2.0 * th) / (1.0 - th)
    inv = lax.rsqrt(q)
    return sp, a, jnp.where(q > 0.0, q * inv, 0.0), inv


class _Phase:
    def __init__(self, copies, n_sem, n_local, start=None, finish=None):
        self.copies, self.n_sem, self.n_local, self.start, self.finish = copies, n_sem, n_local, start, finish


class _Job:
    def __init__(self, inputs, out_shape, n_sem, copies, peers, aliases=None, n_local=0):
        self.inputs, self.out_shape = list(inputs), list(out_shape)
        self.aliases = dict(aliases or {})
        self.phases = [_Phase(copies, n_sem, n_local)]
        self.peers = tuple(peers)

    def then(self, make, at=None):
        nxt = make(self.out_shape)
        self.phases[-1].finish = at
        nxt.phases[0].start = at
        self.phases += nxt.phases
        self.peers = tuple(sorted(set(self.peers + nxt.peers)))
        return self


def _fused_call(body, jobs, *, name, grid, in_specs, out_specs, out_shape, scratch_shapes=(),
                input_output_aliases=None, compiler_params=None, n_prefetch=0, jobs_start_after=None):
    single = not isinstance(out_shape, (list, tuple))
    out_specs = [out_specs] if single else list(out_specs)
    out_shape = [out_shape] if single else list(out_shape)
    n_scr = len(scratch_shapes)
    in_specs, scratch_shapes = list(in_specs), list(scratch_shapes)
    n_in, n_out = len(in_specs), len(out_shape)
    aliases = dict(input_output_aliases or {})
    in_at, out_at, phases = [], [], []
    for q, job in enumerate(jobs):
        in_at.append(len(in_specs))
        out_at.append(len(out_shape))
        for i, o in job.aliases.items():
            aliases[n_prefetch + len(in_specs) + i] = len(out_shape) + o
        in_specs += [ANY] * len(job.inputs)
        out_specs += [ANY] * len(job.out_shape)
        out_shape += job.out_shape
        for k, phase in enumerate(job.phases):
            phases.append((q, k, phase, len(scratch_shapes)))
            scratch_shapes += [pltpu.SemaphoreType.DMA((phase.n_sem,)), pltpu.SemaphoreType.DMA((phase.n_sem,)),
                               pltpu.SemaphoreType.DMA((max(phase.n_local, 1),))]
    n_in_all, n_out_all = len(in_specs), len(out_shape)
    first_step, last_step = (0,) * len(grid), tuple(g - 1 for g in grid)

    def full_body(*refs):
        prefetch, refs = refs[:n_prefetch], refs[n_prefetch:]
        ins, outs, scr = refs[:n_in_all], refs[n_in_all:n_in_all + n_out_all], refs[n_in_all + n_out_all:]
        ids = [pl.program_id(a) for a in range(len(grid))]
        at_step = lambda step: functools.reduce(jnp.logical_and, [i == k for i, k in zip(ids, step)])

        def copies(q, k, phase, sem_at):
            job = jobs[q]
            mine = outs[out_at[q]:out_at[q] + len(job.out_shape)]
            return phase.copies(ins[in_at[q]:in_at[q] + len(job.inputs)] if k == 0 else mine, mine,
                                *scr[sem_at:sem_at + 3])

        def start(*phase):
            def go():
                sends, _, local = copies(*phase)
                for cp in local + sends:
                    cp.start()
            return go

        def finish(*phase):
            def go():
                sends, arrivals, local = copies(*phase)
                for cp in arrivals:
                    cp.wait_recv()
                for cp in sends:
                    cp.wait_send()
                for cp in local:
                    cp.wait()
            return go

        for phase in phases:
            if phase[2].start is None and jobs_start_after is None:
                pl.when(at_step(first_step))(start(*phase))
        body(*prefetch, *ins[:n_in], *outs[:n_out], *scr[:n_scr])
        for phase in phases:
            pl.when(at_step(phase[2].finish or last_step))(finish(*phase))
            nxt = phase[2].start or jobs_start_after
            if nxt is not None:
                pl.when(at_step(nxt))(start(*phase))

    if n_prefetch:
        layout = dict(grid_spec=pltpu.PrefetchScalarGridSpec(
            num_scalar_prefetch=n_prefetch, grid=grid, in_specs=in_specs, out_specs=out_specs,
            scratch_shapes=scratch_shapes))
    else:
        layout = dict(grid=grid, in_specs=in_specs, out_specs=out_specs, scratch_shapes=scratch_shapes)
    call = pl.pallas_call(
        full_body, name=name, out_shape=out_shape, input_output_aliases=aliases, compiler_params=compiler_params,
        **layout)

    def run(*args):
        res = call(*args, *[a for job in jobs for a in job.inputs])
        mine = res[0] if single else list(res[:n_out])
        return mine, [list(res[at:at + len(job.out_shape)]) for at, job in zip(out_at, jobs)]

    return run


def _fwd_in(x, g1, shards, order, jobs=()):
    t = x.shape[0]
    rows_per_step = min(IN_TILE, t)
    n_tiles = t // rows_per_step
    n = len(shards)
    halves = [s.shape[0] // 2 for s in shards]

    def body(order_ref, x_ref, g_ref, *refs):
        del order_ref
        ins, (z_ref, n_ref), outs = refs[:n], refs[n:n + 2], refs[n + 2:2 * n + 2]
        wbuf, nbuf, send, recv, local = refs[2 * n + 2:]
        s, i = pl.program_id(0), pl.program_id(1)
        x_, y_, c, chips = _place()
        near, far = _near_far(x_, y_, c)
        k_me = _chip_index(x_, y_)

        def block(w, chip, pc):
            return outs[w].at[_chip_index(*chip), pl.ds(pc * halves[w], halves[w]), :]

        def over_ici(w, j, landing):
            return pltpu.make_async_remote_copy(
                src_ref=ins[w].at[pl.ds(c * halves[w], halves[w]), :],
                dst_ref=block(w, chips[j] if landing else (x_, y_), c), send_sem=send.at[6 * w + j],
                recv_sem=recv.at[6 * w + j], device_id=(*chips[j], c), device_id_type=MESH)

        def onward(w, landing):
            blk = block(w, chips[2] if landing else near, c)
            return pltpu.make_async_remote_copy(
                src_ref=blk, dst_ref=blk, send_sem=send.at[6 * w + 2], recv_sem=recv.at[6 * w + 2],
                device_id=(*far, c), device_id_type=MESH)

        def to_sibling(w, j, landing):
            blk = block(w, chips[j], 1 - c if landing else c)
            return pltpu.make_async_remote_copy(
                src_ref=blk, dst_ref=blk, send_sem=send.at[6 * w + 3 + j], recv_sem=recv.at[6 * w + 3 + j],
                device_id=(x_, y_, 1 - c), device_id_type=MESH)

        own = [pltpu.make_async_copy(wbuf, outs[0].at[k_me], local.at[0])]
        own += [pltpu.make_async_copy(ins[w], outs[w].at[k_me], local.at[w]) for w in range(1, n)]

        @pl.when((s == 0) & (i == 0))
        def _():
            for j in range(2):
                for w in range(n):
                    over_ici(w, j, False).start()
            load = pltpu.make_async_copy(ins[0], wbuf, local.at[n])
            load.start()
            load.wait()
            for cp in own:
                cp.start()

        for j in range(N_CHIPS - 1):
            @pl.when((s == j + 1) & (i == 0))
            def _(j=j):
                if j == 0:
                    for k in range(2):
                        for w in range(n):
                            over_ici(w, k, True).wait_recv()
                    for w in range(n):
                        onward(w, False).start()
                    for k in range(2):
                        for w in range(n):
                            to_sibling(w, k, False).start()
                    own[0].wait()
                if j == 2:
                    for w in range(n):
                        onward(w, True).wait_recv()
                    for w in range(n):
                        to_sibling(w, j, False).start()
                for w in range(n):
                    to_sibling(w, j, True).wait_recv()
                load = pltpu.make_async_copy(outs[0].at[_chip_index(*chips[j])], wbuf, local.at[n])
                load.start()
                load.wait()

        rows = pl.ds(pl.multiple_of(i * rows_per_step, rows_per_step), rows_per_step)

        @pl.when(s == 0)
        def _():
            xhat, _ = _rms(x_ref[...])
            nrm = (xhat * g_ref[...]).astype(BF16)
            nbuf[rows, :] = nrm
            n_ref[...] = nrm

        z_ref[...] = _dot(nbuf[rows, :], wbuf[...]).astype(BF16)

        @pl.when((s == N_CHIPS - 1) & (i == n_tiles - 1))
        def _():
            for j in range(N_CHIPS - 1):
                for w in range(n):
                    (over_ici(w, j, False) if j < 2 else onward(w, False)).wait_send()
                    to_sibling(w, j, False).wait_send()
            for cp in own[1:]:
                cp.wait()

    once = lambda s, i, order: (jnp.where(s == 0, i, n_tiles - 1), 0)
    (z, n1, *stacked), job_outs = _fused_call(
        body, jobs, name="fwd_in", grid=(N_CHIPS, n_tiles), n_prefetch=1,
        in_specs=[pl.BlockSpec((rows_per_step, D_MODEL), once), _const((1, D_MODEL))] + [ANY] * n,
        out_specs=[pl.BlockSpec((rows_per_step, IN_SHARD), lambda s, i, order: (i, order[s])),
                   pl.BlockSpec((rows_per_step, D_MODEL), once)] + [ANY] * n,
        out_shape=[jax.ShapeDtypeStruct((t, D_IN), BF16), jax.ShapeDtypeStruct((t, D_MODEL), BF16)]
        + [jax.ShapeDtypeStruct((N_CHIPS,) + s.shape, s.dtype) for s in shards],
        scratch_shapes=[pltpu.VMEM(shards[0].shape, BF16), pltpu.VMEM((t, D_MODEL), BF16),
                        pltpu.SemaphoreType.DMA((6 * n,)),
                        pltpu.SemaphoreType.DMA((6 * n,)), pltpu.SemaphoreType.DMA((n + 1,))],
        compiler_params=_params(2), jobs_start_after=(1, 0),
    )(order, x, g1, *shards)
    return (z, n1, stacked), job_outs


def _fwd_lru(z, conv_w, conv_b, wr, br, wi, bi, lam, jobs=()):
    t = z.shape[0]

    def body(xa_ref, ga_ref, cw_ref, cb_ref, wr_ref, br_ref, wi_ref, bi_ref, lam_ref, ya_ref, h_ref, xc_ref, r_ref,
             ig_ref, tail_ref, carry_ref):
        @pl.when(pl.program_id(0) == 0)
        def _():
            tail_ref[...] = jnp.zeros_like(tail_ref)
            carry_ref[...] = jnp.zeros_like(carry_ref)

        xa = xa_ref[...].astype(F32)
        xc, r, ig, a, mult = _lru_gates(xa, tail_ref[...], cw_ref, cb_ref, wr_ref, br_ref, wi_ref, bi_ref, lam_ref)
        tail_ref[...] = xa[SEQ_TILE - SUBLANES:]
        xc_ref[...], r_ref[...], ig_ref[...] = xc, r, ig
        h, carry = _scan_forward(a, xc * ig * mult, carry_ref[...])
        carry_ref[...] = carry
        h_ref[...] = h
        ya_ref[...] = (h * _gelu(ga_ref[...].astype(F32))).astype(BF16)

    tile = lambda j: pl.BlockSpec((SEQ_TILE, D_MODEL), lambda i: (i, j))
    return _fused_call(
        body, jobs, name="fwd_lru", grid=(t // SEQ_TILE,),
        in_specs=[tile(0), tile(1), _const((CONV_WIDTH, D_MODEL)), _const((1, D_MODEL)),
                  _resident((HEADS, HEAD_DIM, HEAD_DIM)), _const((1, D_MODEL)),
                  _resident((HEADS, HEAD_DIM, HEAD_DIM)), _const((1, D_MODEL)), _const((1, D_MODEL))],
        out_specs=[tile(0)] * 5,
        out_shape=[jax.ShapeDtypeStruct((t, D_MODEL), BF16)] + [jax.ShapeDtypeStruct((t, D_MODEL), F32)] * 4,
        scratch_shapes=[pltpu.VMEM((SUBLANES, D_MODEL), F32), pltpu.VMEM((1, D_MODEL), F32)],
        compiler_params=_params(),
    )(z, z, conv_w, conv_b, wr, br, wi, bi, lam)


def _sgu_forward_parts(ub, vb, lg_ref, lb_ref):
    u, du = _gelu_and_grad(ub.astype(F32))
    vg, dvg = _gelu_and_grad(vb.astype(F32))
    mu = jnp.mean(vg, axis=-1, keepdims=True)
    d = vg - mu
    rstd = lax.rsqrt(jnp.mean(d * d, axis=-1, keepdims=True) + LN_EPS)
    vhat = d * rstd
    vn = (vhat * lg_ref[...] + lb_ref[...]).astype(BF16)
    return u, du, dvg, rstd, vhat, vn


def _causal_mask():
    rows = lax.broadcasted_iota(jnp.int32, (CHUNK, CHUNK), 0)
    cols = lax.broadcasted_iota(jnp.int32, (CHUNK, CHUNK), 1)
    return rows >= cols


def _fwd_sgu_merge(ya, z, x, ln_g, ln_b, w_s, bias_full, w_oa, w_ob, w_out, g2, jobs=()):
    t = x.shape[0]

    def body(ya_ref, ub_ref, vb_ref, m_ref, x_ref, lg_ref, lb_ref, ws_ref, bias_ref, woa_ref, wob_ref, wout_ref, g_ref,
             yb_ref, pa_ref, pb_ref, h1_ref, n2_ref):
        u, _, _, _, _, vn = _sgu_forward_parts(ub_ref[...], vb_ref[...], lg_ref, lb_ref)
        mask = _causal_mask()
        wm = [jnp.where(mask, ws_ref[g], 0.0).astype(BF16) for g in range(GROUPS)]
        for c in range(SEQ_TILE // CHUNK):
            rows = slice(c * CHUNK, (c + 1) * CHUNK)
            for g in range(GROUPS):
                cols = slice(g * GROUP_DIM, (g + 1) * GROUP_DIM)
                sp = _dot(wm[g], vn[rows, cols]) + bias_ref[:, cols]
                yb_ref[rows, cols] = (u[rows, cols] * sp).astype(BF16)
        pa = _dot(ya_ref[...], woa_ref[...])
        pb = _dot(yb_ref[...], wob_ref[...])
        pa_ref[...] = pa
        pb_ref[...] = pb
        merged = _gate(m_ref[:, :D_MODEL]) * pa + _gate(m_ref[:, D_MODEL:]) * pb
        h1 = x_ref[...] + _dot(merged.astype(BF16), wout_ref[...])
        h1_ref[...] = h1
        xhat, _ = _rms(h1)
        n2_ref[...] = (xhat * g_ref[...]).astype(BF16)

    tile = lambda j: pl.BlockSpec((SEQ_TILE, D_MODEL), lambda i: (i, j))
    sq = _resident((D_MODEL, D_MODEL))
    vec = _const((1, D_MODEL))
    bf, f32 = jax.ShapeDtypeStruct((t, D_MODEL), BF16), jax.ShapeDtypeStruct((t, D_MODEL), F32)
    return _fused_call(
        body, jobs, name="fwd_sgu_merge", grid=(t // SEQ_TILE,),
        in_specs=[tile(0), tile(2), tile(3), pl.BlockSpec((SEQ_TILE, 2 * D_MODEL), lambda i: (i, 2)), tile(0), vec, vec,
                  _const((GROUPS, CHUNK, CHUNK)), _const((CHUNK, D_MODEL)), sq, sq, sq, vec],
        out_specs=[tile(0)] * 5,
        out_shape=[bf, f32, f32, f32, bf],
        compiler_params=_params(),
    )(ya, z, z, z, x, ln_g, ln_b, w_s, bias_full, w_oa, w_ob, w_out, g2)


def _mlp(n2, h1, target, w_up_st, w_down, g2, g3, jobs=()):
    t = n2.shape[0]

    def body(n2_ref, h1_ref, tgt_ref, wup_ref, wdown_ref, g2_ref, g3_ref, act_ref, dup_ref, dh2b_ref, dh1_ref,
             loss_ref, dg3_ref, dg2_ref, relu_ref):
        @pl.when(pl.program_id(0) == 0)
        def _():
            for ref in (loss_ref, dg3_ref, dg2_ref):
                ref[...] = jnp.zeros_like(ref)

        n2 = n2_ref[...]
        h1 = h1_ref[...]
        h2 = h1
        for k in range(N_CHIPS):
            cols = slice(k * D_MODEL, (k + 1) * D_MODEL)
            r = jnp.maximum(_dot(n2, wup_ref[k]), 0.0)
            relu_ref[:, cols] = r
            act = (r * r).astype(BF16)
            act_ref[:, cols] = act
            h2 = h2 + _dot(act, wdown_ref[cols, :])
        xhat, r3 = _rms(h2)
        diff = xhat * g3_ref[...] - tgt_ref[...]
        sq = jnp.sum(diff * diff, axis=1, keepdims=True)
        loss_ref[...] = loss_ref[...] + (0.5 / D_MODEL) * jnp.sum(sq, axis=0, keepdims=True)
        dy = diff * (1.0 / D_MODEL)
        dg3_ref[...] = dg3_ref[...] + _col_sum(dy * xhat)
        dh2 = _rms_bwd(dy * g3_ref[...], xhat, r3)
        dh2b = dh2.astype(BF16)
        dh2b_ref[...] = dh2b
        dn2 = jnp.zeros((SEQ_TILE, D_MODEL), F32)
        for k in range(N_CHIPS):
            cols = slice(k * D_MODEL, (k + 1) * D_MODEL)
            dup = (_dot_nt(dh2b, wdown_ref[cols, :]) * (2.0 * relu_ref[:, cols])).astype(BF16)
            dup_ref[:, cols] = dup
            dn2 = dn2 + _dot_nt(dup, wup_ref[k])
        xhat, r2 = _rms(h1)
        dg2_ref[...] = dg2_ref[...] + _col_sum(dn2 * xhat)
        dh1_ref[...] = dh2 + _rms_bwd(dn2 * g2_ref[...], xhat, r2)

    tile = pl.BlockSpec((SEQ_TILE, D_MODEL), lambda i: (i, 0))
    wide = pl.BlockSpec((SEQ_TILE, D_FF), lambda i: (i, 0))
    vec = _const((1, D_MODEL))
    vec_shape = jax.ShapeDtypeStruct((1, D_MODEL), F32)
    return _fused_call(
        body, jobs, name="mlp", grid=(t // SEQ_TILE,),
        in_specs=[tile, tile, tile, _resident((N_CHIPS, D_MODEL, D_MODEL)), _resident((D_FF, D_MODEL)), vec, vec],
        out_specs=[wide, wide, tile, tile, _const((SUBLANES, 128)), vec, vec],
        out_shape=[jax.ShapeDtypeStruct((t, D_FF), BF16), jax.ShapeDtypeStruct((t, D_FF), BF16),
                   jax.ShapeDtypeStruct((t, D_MODEL), BF16), jax.ShapeDtypeStruct((t, D_MODEL), F32),
                   jax.ShapeDtypeStruct((SUBLANES, 128), F32), vec_shape, vec_shape],
        scratch_shapes=[pltpu.VMEM((SEQ_TILE, D_FF), F32)],
        compiler_params=_params(),
    )(n2, h1, target, w_up_st, w_down, g2, g3)


def _bwd_mix(dh1, pa, pb, z, h, xc, r, ig, w_oa, w_ob, w_out, ln_g, ln_b, w_s, bias_full, conv_w, wr, wi, lam, jobs=()):
    t = dh1.shape[0]
    n_tiles = t // SEQ_TILE
    per_tile = SEQ_TILE // SUBLANES

    def merge_part(dh1_ref, pa_ref, pb_ref, m_ref, woa_ref, wob_ref, wout_ref, dz_ref, dya_ref, dyb_ref, mg_ref,
                   dpa_ref, dpb_ref, dh1b_ref):
        dh1b = dh1_ref[...].astype(BF16)
        dh1b_ref[...] = dh1b
        dm = _dot_nt(dh1b, wout_ref[...])
        pa = pa_ref[...]
        pb = pb_ref[...]
        sa = _gate(m_ref[:, :D_MODEL])
        sb = _gate(m_ref[:, D_MODEL:])
        mg_ref[...] = (sa * pa + sb * pb).astype(BF16)
        dpa = dm * sa
        dpb = dm * sb
        dz_ref[:, :D_MODEL] = ((dpa * pa) * (1.0 - sa)).astype(BF16)
        dz_ref[:, D_MODEL:] = ((dpb * pb) * (1.0 - sb)).astype(BF16)
        dpa = dpa.astype(BF16)
        dpb = dpb.astype(BF16)
        dpa_ref[...] = dpa
        dpb_ref[...] = dpb
        dya_ref[...] = _dot_nt(dpa, woa_ref[...])
        dyb_ref[...] = _dot_nt(dpb, wob_ref[...])

    def sgu_part(dyb_ref, ub_ref, vb_ref, lg_ref, lb_ref, ws_ref, bias_ref, dz_ref, dlg_ref, dlb_ref, dws_ref, dbs_ref,
                 dvn_ref, dsp_acc):
        i = pl.program_id(0)

        @pl.when(i == 0)
        def _():
            dlg_ref[...] = jnp.zeros_like(dlg_ref)
            dlb_ref[...] = jnp.zeros_like(dlb_ref)
            dws_ref[...] = jnp.zeros_like(dws_ref)
            dsp_acc[...] = jnp.zeros_like(dsp_acc)

        u, du, dvg, rstd, vhat, vn = _sgu_forward_parts(ub_ref[...], vb_ref[...], lg_ref, lb_ref)
        dyb = dyb_ref[...]
        mask = _causal_mask()
        wm = [jnp.where(mask, ws_ref[g], 0.0).astype(BF16) for g in range(GROUPS)]
        for c in range(SEQ_TILE // CHUNK):
            rows = slice(c * CHUNK, (c + 1) * CHUNK)
            for g in range(GROUPS):
                cols = slice(g * GROUP_DIM, (g + 1) * GROUP_DIM)
                vn_blk = vn[rows, cols]
                sp = _dot(wm[g], vn_blk) + bias_ref[:, cols]
                dyb_blk = dyb[rows, cols]
                dz_ref[rows, cols] = (dyb_blk * sp * du[rows, cols]).astype(BF16)
                dsp = dyb_blk * u[rows, cols]
                dsp_acc[:, cols] = dsp_acc[:, cols] + dsp
                dspb = dsp.astype(BF16)
                dvn_ref[rows, cols] = _dot_tn(wm[g], dspb)
                wcols = slice(g * CHUNK, (g + 1) * CHUNK)
                dws_ref[:, wcols] = dws_ref[:, wcols] + jnp.where(mask, _dot_nt(dspb, vn_blk), 0.0)
        dvn = dvn_ref[...]
        dlg_ref[...] = dlg_ref[...] + _col_sum(dvn * vhat)
        dlb_ref[...] = dlb_ref[...] + _col_sum(dvn)
        dvhat = dvn * lg_ref[...]
        dvgel = rstd * (dvhat - jnp.mean(dvhat, axis=-1, keepdims=True)
                        - vhat * jnp.mean(dvhat * vhat, axis=-1, keepdims=True))
        dz_ref[:, D_MODEL:] = (dvgel * dvg).astype(BF16)

        @pl.when(i == n_tiles - 1)
        def _():
            lane = lax.broadcasted_iota(jnp.int32, (CHUNK, 128), 1)
            out = jnp.zeros((CHUNK, 128), F32)
            for g in range(GROUPS):
                s = jnp.sum(dsp_acc[:, g * GROUP_DIM:(g + 1) * GROUP_DIM], axis=1, keepdims=True)
                out = out + jnp.where(lane == g, s, 0.0)
            dbs_ref[...] = out

    def lru_part(dya_ref, xa_ref, ga_ref, h_ref, h_prev_ref, xc_ref, r_ref, ig_ref, cw_ref, wr_ref, wi_ref, lam_ref,
                 dz_ref, dcw_ref, dcb_ref, dwr_ref, dbr_ref, dwi_ref, dbi_ref, dlam_ref, lam_carry, dxc_head):
        i = pl.program_id(0)

        @pl.when(i == 0)
        def _():
            for ref in (dcw_ref, dcb_ref, dwr_ref, dbr_ref, dwi_ref, dbi_ref, dlam_ref, lam_carry, dxc_head):
                ref[...] = jnp.zeros_like(ref)

        first_tile = i == n_tiles - 1
        h_tail = jnp.where(first_tile, 0.0, h_prev_ref[...])
        xc, r, ig = xc_ref[...], r_ref[...], ig_ref[...]
        xcb = xc.astype(BF16)
        sp, a, mult, inv_mult = _decay(r, lam_ref)
        h = h_ref[...]
        h_prev = _shift_down(h, h_tail, 1)
        dya = dya_ref[...]
        gg, dgg = _gelu_and_grad(ga_ref[...].astype(F32))
        dz_ref[:, D_MODEL:] = (dya * h * dgg).astype(BF16)
        ones = jnp.ones((SUBLANES, D_MODEL), F32)
        lam_t, lam_first = _scan_backward(_shift_up(a, ones, 1), dya * gg, lam_carry[...])
        lam_carry[...] = a[0:1] * lam_first
        lam_ig = lam_t * ig
        dxc_direct = lam_ig * mult
        dmult = lam_ig * xc
        dla = a * (lam_t * h_prev - (dmult * a) * inv_mult)
        dla_r = dla * r
        dlam_ref[...] = dlam_ref[...] + _col_sum(dla_r) * (LRU_C * jax.nn.sigmoid(-lam_ref[...]))
        dpr = (dla_r * ((-LRU_C) * sp)) * (1.0 - r)
        dpi = (dxc_direct * xc) * (1.0 - ig)
        dbr_ref[...] = dbr_ref[...] + _col_sum(dpr)
        dbi_ref[...] = dbi_ref[...] + _col_sum(dpi)
        dprb = dpr.astype(BF16)
        dpib = dpi.astype(BF16)
        dxc_gate = []
        for hd in range(HEADS):
            cols = slice(hd * HEAD_DIM, (hd + 1) * HEAD_DIM)
            dxc_gate.append(_dot_nt(dprb[:, cols], wr_ref[hd]) + _dot_nt(dpib[:, cols], wi_ref[hd]))
            dwr_ref[hd] = dwr_ref[hd] + _dot_tn(xcb[:, cols], dprb[:, cols])
            dwi_ref[hd] = dwi_ref[hd] + _dot_tn(xcb[:, cols], dpib[:, cols])
        dxc = dxc_direct + jnp.concatenate(dxc_gate, axis=1)
        dcb_ref[...] = dcb_ref[...] + _col_sum(dxc)
        cw = cw_ref[...]
        head = dxc_head[...]
        xa = xa_ref[...].astype(F32)
        dxa = cw[0:1] * dxc
        dcw_ref[0:1, :] = dcw_ref[0:1, :] + _col_sum(dxc * xa)
        for k in range(1, CONV_WIDTH):
            dxc_k = _shift_up(dxc, head, k)
            dxa = dxa + cw[k:k + 1] * dxc_k
            dcw_ref[k:k + 1, :] = dcw_ref[k:k + 1, :] + _col_sum(dxc_k * xa)
        dxc_head[...] = dxc[0:SUBLANES]
        dz_ref[:, :D_MODEL] = dxa.astype(BF16)

    def body(dh1_ref, pa_ref, pb_ref, z_ref, h_ref, h_prev_ref, xc_ref, r_ref, ig_ref, woa_ref, wob_ref, wout_ref,
             lg_ref, lb_ref, ws_ref, bias_ref, cw_ref, wr_ref, wi_ref, lam_ref, dz_ref, mg_ref, dpa_ref, dpb_ref,
             dh1b_ref, dlg_ref, dlb_ref, dws_ref, dbs_ref, dcw_ref, dcb_ref, dwr_ref, dbr_ref, dwi_ref, dbi_ref,
             dlam_ref, dya_ref, dyb_ref, dvn_ref, dsp_acc, lam_carry, dxc_head):
        def cols(ref, first, count):
            return ref.at[:, pl.ds(first * D_MODEL, count * D_MODEL)]

        merge_part(dh1_ref, pa_ref, pb_ref, cols(z_ref, 4, 2), woa_ref, wob_ref, wout_ref, cols(dz_ref, 4, 2), dya_ref,
                   dyb_ref, mg_ref, dpa_ref, dpb_ref, dh1b_ref)
        sgu_part(dyb_ref, cols(z_ref, 2, 1), cols(z_ref, 3, 1), lg_ref, lb_ref, ws_ref, bias_ref, cols(dz_ref, 2, 2),
                 dlg_ref, dlb_ref, dws_ref, dbs_ref, dvn_ref, dsp_acc)
        lru_part(dya_ref, cols(z_ref, 0, 1), cols(z_ref, 1, 1), h_ref, h_prev_ref, xc_ref, r_ref, ig_ref, cw_ref, wr_ref,
                 wi_ref, lam_ref, cols(dz_ref, 0, 2), dcw_ref, dcb_ref, dwr_ref, dbr_ref, dwi_ref, dbi_ref, dlam_ref,
                 lam_carry, dxc_head)

    rev = lambda i: n_tiles - 1 - i
    tile = pl.BlockSpec((SEQ_TILE, D_MODEL), lambda i: (rev(i), 0))
    row = pl.BlockSpec((SEQ_TILE, D_IN), lambda i: (rev(i), 0))
    prev8 = pl.BlockSpec((SUBLANES, D_MODEL), lambda i: (jnp.maximum(rev(i) * per_tile - 1, 0), 0))
    vec = _const((1, D_MODEL))
    sq = _resident((D_MODEL, D_MODEL))
    gate_w = _resident((HEADS, HEAD_DIM, HEAD_DIM))
    gate_acc = _const((HEADS, HEAD_DIM, HEAD_DIM))
    vec_shape = jax.ShapeDtypeStruct((1, D_MODEL), F32)
    gate_shape = jax.ShapeDtypeStruct((HEADS, HEAD_DIM, HEAD_DIM), F32)
    act_bf = jax.ShapeDtypeStruct((t, D_MODEL), BF16)
    return _fused_call(
        body, jobs, name="bwd_mix", grid=(n_tiles,),
        in_specs=[tile, tile, tile, row, tile, prev8, tile, tile, tile, sq, sq, sq, vec, vec,
                  _const((GROUPS, CHUNK, CHUNK)), _const((CHUNK, D_MODEL)), _const((CONV_WIDTH, D_MODEL)), gate_w, gate_w,
                  vec],
        out_specs=[row, tile, tile, tile, tile, vec, vec, _const((CHUNK, GROUPS * CHUNK)), _const((CHUNK, 128)),
                   _const((SUBLANES, D_MODEL)), vec, gate_acc, vec, gate_acc, vec, vec],
        out_shape=[jax.ShapeDtypeStruct((t, D_IN), BF16), act_bf, act_bf, act_bf, act_bf, vec_shape, vec_shape,
                   jax.ShapeDtypeStruct((CHUNK, GROUPS * CHUNK), F32), jax.ShapeDtypeStruct((CHUNK, 128), F32),
                   jax.ShapeDtypeStruct((SUBLANES, D_MODEL), F32), vec_shape, gate_shape, vec_shape, gate_shape,
                   vec_shape, vec_shape],
        scratch_shapes=[pltpu.VMEM((SEQ_TILE, D_MODEL), F32), pltpu.VMEM((SEQ_TILE, D_MODEL), F32),
                        pltpu.VMEM((SEQ_TILE, D_MODEL), F32), pltpu.VMEM((CHUNK, D_MODEL), F32),
                        pltpu.VMEM((1, D_MODEL), F32), pltpu.VMEM((SUBLANES, D_MODEL), F32)],
        compiler_params=_params(),
    )(dh1, pa, pb, z, h, h, xc, r, ig, w_oa, w_ob, w_out, ln_g, ln_b, w_s, bias_full, conv_w, wr, wi, lam)


def _bwd_in(dz, x, dh1, w_in_st, g1, jobs=()):
    t = x.shape[0]

    def body(dz_ref, x_ref, dh1_ref, w_ref, g_ref, dx_ref, dg1_ref):
        @pl.when(pl.program_id(0) == 0)
        def _():
            dg1_ref[...] = jnp.zeros_like(dg1_ref)

        dn1 = jnp.zeros((MM_TILE, D_MODEL), F32)
        for k in range(N_CHIPS):
            dn1 = dn1 + _dot_nt(dz_ref[:, k * IN_SHARD:(k + 1) * IN_SHARD], w_ref[k])
        xhat, r1 = _rms(x_ref[...])
        dg1_ref[...] = dg1_ref[...] + _col_sum(dn1 * xhat)
        dx_ref[...] = dh1_ref[...] + _rms_bwd(dn1 * g_ref[...], xhat, r1)

    tile = pl.BlockSpec((MM_TILE, D_MODEL), lambda i: (i, 0))
    return _fused_call(
        body, jobs, name="bwd_in", grid=(t // MM_TILE,),
        in_specs=[pl.BlockSpec((MM_TILE, D_IN), lambda i: (i, 0)), tile, tile,
                  _resident((N_CHIPS, D_MODEL, IN_SHARD)), _const((1, D_MODEL))],
        out_specs=[tile, _const((1, D_MODEL))],
        out_shape=[jax.ShapeDtypeStruct((t, D_MODEL), F32), jax.ShapeDtypeStruct((1, D_MODEL), F32)],
        compiler_params=_params(),
    )(dz, x, dh1, w_in_st, g1)


def _weight_grad(name, a, b, n_blocks, a_varies, b_varies, width, jobs=()):
    t = a.shape[0]
    rows = min(DW_TILE, t)
    n_t = t // rows

    def body(a_ref, b_ref, o_ref, acc_ref):
        s = pl.program_id(1)
        part = _dot_tn(a_ref[...], b_ref[...])

        @pl.when(s == 0)
        def _():
            acc_ref[...] = part

        @pl.when(s > 0)
        def _():
            acc_ref[...] = acc_ref[...] + part

        @pl.when(s == n_t - 1)
        def _():
            o_ref[...] = acc_ref[...].astype(BF16)

    return _fused_call(
        body, jobs, name=name, grid=(n_blocks, n_t),
        in_specs=[pl.BlockSpec((rows, D_MODEL), (lambda j, s: (s, j)) if a_varies else (lambda j, s: (s, 0))),
                  pl.BlockSpec((rows, width), (lambda j, s: (s, j)) if b_varies else (lambda j, s: (s, 0)))],
        out_specs=pl.BlockSpec((None, D_MODEL, width), lambda j, s: (j, 0, 0)),
        out_shape=jax.ShapeDtypeStruct((n_blocks, D_MODEL, width), BF16),
        scratch_shapes=[pltpu.VMEM((D_MODEL, width), F32)],
        compiler_params=_params(2),
    )(a, b)


def _weight_grads_square(name, pairs, jobs=()):
    n = len(pairs)
    t = pairs[0][0].shape[0]
    rows = min(2 * MM_TILE, t)
    n_t = t // rows

    def body(*refs):
        ins, outs, accs = refs[:2 * n], refs[2 * n:3 * n], refs[3 * n:]
        s = pl.program_id(0)
        for k in range(n):
            part = _dot_tn(ins[2 * k][...], ins[2 * k + 1][...])

            @pl.when(s == 0)
            def _(k=k, part=part):
                accs[k][...] = part

            @pl.when(s > 0)
            def _(k=k, part=part):
                accs[k][...] = accs[k][...] + part

            @pl.when(s == n_t - 1)
            def _(k=k):
                outs[k][...] = accs[k][...].astype(BF16)

    tile = pl.BlockSpec((rows, D_MODEL), lambda s: (s, 0))
    return _fused_call(
        body, jobs, name=name, grid=(n_t,), in_specs=[tile] * (2 * n), out_specs=[_const((D_MODEL, D_MODEL))] * n,
        out_shape=[jax.ShapeDtypeStruct((D_MODEL, D_MODEL), BF16)] * n,
        scratch_shapes=[pltpu.VMEM((D_MODEL, D_MODEL), F32)] * n,
        compiler_params=_params(),
    )(*[x for pair in pairs for x in pair])


def _place():
    x, y, c = lax.axis_index("x"), lax.axis_index("y"), lax.axis_index("c")
    other_chips = [(1 - x, y), (x, 1 - y), (1 - x, 1 - y)]
    return x, y, c, other_chips


def _chip_index(px, py):
    return 2 * px + py


ANY = pl.BlockSpec(memory_space=pl.ANY)
SIBLING = ((0, 0, 1),)
NEIGHBOURS = ((1, 0, 0), (0, 1, 0))
OTHER_CHIPS = NEIGHBOURS + ((1, 1, 0),)


def _near_far(x, y, c):
    return (x ^ (1 - c), y ^ c), (x ^ c, y ^ (1 - c))


def _gather_near_job(shards):
    n = len(shards)
    halves = [s.shape[0] // 2 for s in shards]

    def copies(ins, outs, send, recv, local):
        x, y, c, _ = _place()
        near, _ = _near_far(x, y, c)

        def block(w, chip, pc):
            return outs[w].at[_chip_index(*chip), pl.ds(pc * halves[w], halves[w]), :]

        def copy(w, k, chip, pc, to, src=None):
            return pltpu.make_async_remote_copy(
                src_ref=block(w, chip, pc) if src is None else src, dst_ref=block(w, chip, pc),
                send_sem=send.at[2 * w + k], recv_sem=recv.at[2 * w + k], device_id=to, device_id_type=MESH)

        sends, arrivals, own = [], [], []
        for w in range(n):
            src = ins[w].at[pl.ds(c * halves[w], halves[w]), :]
            own.append(pltpu.make_async_copy(src, block(w, (x, y), c), local.at[w]))
            sends += [copy(w, 0, (x, y), c, (*near, c), src), copy(w, 1, (x, y), c, (x, y, 1 - c), src)]
            arrivals += [copy(w, 0, near, c, (x, y, c)), copy(w, 1, (x, y), 1 - c, (x, y, c))]
        return sends, arrivals, own

    return _Job(shards, [jax.ShapeDtypeStruct((N_CHIPS,) + s.shape, s.dtype) for s in shards], 2 * n, copies,
                NEIGHBOURS + SIBLING, n_local=n)


def _gather_far_job(stacked):
    n = len(stacked)
    halves = [s.shape[1] // 2 for s in stacked]

    def copies(ins, outs, send, recv, local):
        del ins, local
        x, y, c, _ = _place()
        near, far = _near_far(x, y, c)

        def copy(w, k, chip):
            blk = outs[w].at[_chip_index(*chip), pl.ds(c * halves[w], halves[w]), :]
            return pltpu.make_async_remote_copy(
                src_ref=blk, dst_ref=blk, send_sem=send.at[2 * w + k], recv_sem=recv.at[2 * w + k],
                device_id=(*far, c), device_id_type=MESH)

        sends = [copy(w, k, chip) for w in range(n) for k, chip in enumerate(((x, y), near))]
        arrivals = [copy(w, k, chip) for w in range(n) for k, chip in enumerate((far, (1 - x, 1 - y)))]
        return sends, arrivals, []

    return _Job(stacked, [jax.ShapeDtypeStruct(s.shape, s.dtype) for s in stacked], 2 * n, copies, NEIGHBOURS,
                aliases={w: w for w in range(n)})


def _gather_pass_job(stacked):
    n = len(stacked)
    halves = [s.shape[1] // 2 for s in stacked]

    def copies(ins, outs, send, recv, local):
        del ins, local
        x, y, c, chips = _place()

        def copy(w, j, chip, pc, to):
            blk = outs[w].at[_chip_index(*chip), pl.ds(pc * halves[w], halves[w]), :]
            return pltpu.make_async_remote_copy(
                src_ref=blk, dst_ref=blk, send_sem=send.at[3 * w + j], recv_sem=recv.at[3 * w + j], device_id=to,
                device_id_type=MESH)

        sends = [copy(w, j, chip, c, (x, y, 1 - c)) for w in range(n) for j, chip in enumerate(chips)]
        arrivals = [copy(w, j, chip, 1 - c, (x, y, c)) for w in range(n) for j, chip in enumerate(chips)]
        return sends, arrivals, []

    return _Job(stacked, [jax.ShapeDtypeStruct(s.shape, s.dtype) for s in stacked], 3 * n, copies, SIBLING,
                aliases={w: w for w in range(n)})


def _gather_small_job(block):
    def copies(ins, outs, send, recv, local):
        x, y, c, chips = _place()

        def copy(j, chip_from, to):
            return pltpu.make_async_remote_copy(
                src_ref=ins[0], dst_ref=outs[0].at[_chip_index(*chip_from)], send_sem=send.at[j],
                recv_sem=recv.at[j], device_id=to, device_id_type=MESH)

        own = [pltpu.make_async_copy(ins[0], outs[0].at[_chip_index(x, y)], local.at[0])]
        sends = [copy(j, (x, y), (*chip, c)) for j, chip in enumerate(chips)]
        arrivals = [copy(j, chip, (x, y, c)) for j, chip in enumerate(chips)]
        return sends, arrivals, own

    return _Job([block], [jax.ShapeDtypeStruct((N_CHIPS,) + block.shape, block.dtype)], 3, copies, OTHER_CHIPS,
                n_local=1)


def _pair_send_job(grads):
    n = len(grads)
    halves = [g.shape[1] // 2 for g in grads]

    def copies(ins, outs, send, recv, local):
        del local
        x, y, c, _ = _place()
        sends = [pltpu.make_async_remote_copy(
            src_ref=ins[w].at[:, pl.ds((1 - c) * halves[w], halves[w]), :], dst_ref=outs[w], send_sem=send.at[w],
            recv_sem=recv.at[w], device_id=(x, y, 1 - c), device_id_type=MESH) for w in range(n)]
        return sends, sends, []

    return _Job(grads, [jax.ShapeDtypeStruct((N_CHIPS, h, g.shape[2]), g.dtype) for g, h in zip(grads, halves)], n,
                copies, SIBLING)


ROW_STEPS = 4
SUM_STEPS = 2


def _pair_add(name, core, mine, theirs):
    n = len(mine)

    def body(core_ref, *refs):
        del core_ref
        for a_ref, b_ref, o_ref in zip(refs[:n], refs[n:2 * n], refs[2 * n:]):
            o_ref[...] = (a_ref[...].astype(F32) + b_ref[...].astype(F32)).astype(BF16)

    half = lambda a: pl.BlockSpec((2, None) + a.shape[2:], lambda k, core_ref: (k, core_ref[0], 0, 0))
    block = lambda b: pl.BlockSpec((2,) + b.shape[1:], lambda k, core_ref: (k, 0, 0))
    return pl.pallas_call(
        body, name=name,
        grid_spec=pltpu.PrefetchScalarGridSpec(
            num_scalar_prefetch=1, grid=(N_CHIPS // 2,),
            in_specs=[half(a) for a in mine] + [block(b) for b in theirs], out_specs=[block(b) for b in theirs]),
        out_shape=[jax.ShapeDtypeStruct(b.shape, BF16) for b in theirs],
        compiler_params=_params(),
    )(core, *mine, *theirs)


def _sequencer_call(name, collective_id, job):
    steps, peers = job.phases, job.peers
    ins = [jax.new_ref(a, memory_space=pltpu.MemorySpace.HBM) for a in job.inputs]
    outs = [ins[{o: i for i, o in job.aliases.items()}[k]] if k in job.aliases.values()
            else jax.empty_ref(shape, memory_space=pltpu.MemorySpace.HBM) for k, shape in enumerate(job.out_shape)]
    sems = [pltpu.SemaphoreType.DMA((n,)) for step in steps for n in (step.n_sem, step.n_sem, max(step.n_local, 1))]

    @pl.kernel(mesh=plsc.ScalarSubcoreMesh(axis_name="sequencer", num_cores=1), name=name, scratch_types=tuple(sems),
               compiler_params=pltpu.CompilerParams(collective_id=collective_id))
    def launch(*sem_refs):
        x, y, c, _ = _place()
        barrier = pltpu.get_barrier_semaphore()
        for dx, dy, dc in peers:
            pl.semaphore_signal(barrier, inc=1, device_id=(x ^ dx, y ^ dy, c ^ dc), device_id_type=MESH)
        pl.semaphore_wait(barrier, len(peers))
        for k, step in enumerate(steps):
            sends, arrivals, own = step.copies(ins if k == 0 else outs, outs, *sem_refs[3 * k:3 * k + 3])
            for cp in own + sends:
                cp.start()
            for cp in arrivals:
                cp.wait_recv()
            for cp in sends:
                cp.wait_send()
            for cp in own:
                cp.wait()

    launch()
    return [ref[...] for ref in outs]


def _chip_exchange_job(sums):
    n = len(sums)

    def copies(ins, outs, send, recv, local):
        del local
        _, _, c, chips = _place()
        sends = [pltpu.make_async_remote_copy(
            src_ref=ins[w].at[_chip_index(*chip)], dst_ref=outs[w].at[j], send_sem=send.at[3 * w + j],
            recv_sem=recv.at[3 * w + j], device_id=(*chip, c), device_id_type=MESH)
            for w in range(n) for j, chip in enumerate(chips)]
        return sends, sends, []

    return _Job(sums, [jax.ShapeDtypeStruct((N_CHIPS - 1,) + s.shape[1:], s.dtype) for s in sums], 3 * n, copies,
                OTHER_CHIPS)


def _chip_sum(name, place, mine, theirs):
    n = len(mine)

    def body(place_ref, *refs):
        del place_ref
        for p_ref, q_ref, o_ref in zip(refs[:n], refs[n:2 * n], refs[2 * n:]):
            acc = p_ref[...].astype(F32)
            for j in range(N_CHIPS - 1):
                acc = acc + q_ref[j].astype(F32)
            o_ref[...] = acc

    def block(p, lead, pick):
        return pl.BlockSpec((lead, p.shape[1] // SUM_STEPS, p.shape[2]), lambda r, place_ref: (pick(place_ref), r, 0))

    return pl.pallas_call(
        body, name=name,
        grid_spec=pltpu.PrefetchScalarGridSpec(
            num_scalar_prefetch=1, grid=(SUM_STEPS,),
            in_specs=[block(p, None, lambda place_ref: place_ref[0]) for p in mine]
            + [block(p, N_CHIPS - 1, lambda place_ref: 0) for p in mine],
            out_specs=[block(p, None, lambda place_ref: place_ref[1]) for p in mine]),
        out_shape=[jax.ShapeDtypeStruct((2,) + p.shape[1:], F32) for p in mine],
        compiler_params=_params(),
    )(place, *mine, *theirs)


def _share_job(bufs):
    n = len(bufs)

    def copies(ins, outs, send, recv, local):
        del ins, local
        x, y, c, _ = _place()

        def copy(w, half):
            return pltpu.make_async_remote_copy(
                src_ref=outs[w].at[half], dst_ref=outs[w].at[half], send_sem=send.at[w], recv_sem=recv.at[w],
                device_id=(x, y, 1 - c), device_id_type=MESH)

        return [copy(w, c) for w in range(n)], [copy(w, 1 - c) for w in range(n)], []

    return _Job(bufs, [jax.ShapeDtypeStruct(b.shape, b.dtype) for b in bufs], n, copies, SIBLING,
                aliases={w: w for w in range(n)})


SMALL_ROWS = 24
ROW_G1, ROW_CW, ROW_CB, ROW_BR, ROW_BI, ROW_LAM, ROW_LG, ROW_LB, ROW_G2, ROW_G3, ROW_LOSS, ROW_BS = (
    0, 1, 5, 6, 7, 8, 9, 10, 11, 12, 13, 16)
N_DEV = 8


def _pack_small(dcw, dcb, dbr, dbi, dlam, dlg, dlb, dg2, dg3, loss, dbs):
    def body(dcw_ref, dcb_ref, dbr_ref, dbi_ref, dlam_ref, dlg_ref, dlb_ref, dg2_ref, dg3_ref, loss_ref, dbs_ref, out):
        out[...] = jnp.zeros((SMALL_ROWS, D_MODEL), F32)
        for row, ref in ((ROW_CB, dcb_ref), (ROW_BR, dbr_ref), (ROW_BI, dbi_ref), (ROW_LAM, dlam_ref),
                         (ROW_LG, dlg_ref), (ROW_LB, dlb_ref), (ROW_G2, dg2_ref), (ROW_G3, dg3_ref)):
            out[row:row + 1, :] = ref[...]
        out[ROW_CW:ROW_CW + CONV_WIDTH, :] = dcw_ref[0:CONV_WIDTH, :]
        out[ROW_LOSS:ROW_LOSS + 1, 0:128] = loss_ref[0:1, :]
        out[ROW_BS:ROW_BS + GROUPS, 0:128] = jnp.transpose(dbs_ref[...])[0:GROUPS, :]

    vm = pl.BlockSpec(memory_space=pltpu.VMEM)
    return pl.pallas_call(
        body, name="pack_small", in_specs=[vm] * 11, out_specs=vm,
        out_shape=jax.ShapeDtypeStruct((SMALL_ROWS, D_MODEL), F32),
    )(dcw, dcb, dbr, dbi, dlam, dlg, dlb, dg2, dg3, loss, dbs)


def _gather_all_job(blocks):
    n = len(blocks)
    flips = [(dx, dy, dc) for dx in (0, 1) for dy in (0, 1) for dc in (0, 1)][1:]

    def copies(ins, outs, send, recv, local):
        x, y, c, _ = _place()
        me = 4 * x + 2 * y + c
        sends, arrivals, own = [], [], []
        for w in range(n):
            own.append(pltpu.make_async_copy(ins[w], outs[w].at[me], local.at[w]))
            for k, (dx, dy, dc) in enumerate(flips):
                peer = (x ^ dx, y ^ dy, c ^ dc)
                sem = dict(send_sem=send.at[7 * w + k], recv_sem=recv.at[7 * w + k])
                sends.append(pltpu.make_async_remote_copy(
                    src_ref=ins[w], dst_ref=outs[w].at[me], device_id=peer, device_id_type=MESH, **sem))
                arrivals.append(pltpu.make_async_remote_copy(
                    src_ref=ins[w], dst_ref=outs[w].at[4 * peer[0] + 2 * peer[1] + peer[2]], device_id=peer,
                    device_id_type=MESH, **sem))
        return sends, arrivals, own

    return _Job(blocks, [jax.ShapeDtypeStruct((N_DEV,) + b.shape, b.dtype) for b in blocks], 7 * n, copies,
                OTHER_CHIPS + SIBLING + tuple((dx, dy, 1) for dx, dy, _ in OTHER_CHIPS), n_local=n)


def _sum_small(vec_all, ws_all, dg1_all):
    def body(vec_ref, ws_ref, dg1_ref, vec_out, ws_out):
        vec, ws, dg1 = vec_ref[0], ws_ref[0], dg1_ref[0]
        for d in range(1, N_DEV):
            vec, ws, dg1 = vec + vec_ref[d], ws + ws_ref[d], dg1 + dg1_ref[d]
        vec_out[...] = vec
        vec_out[ROW_G1:ROW_G1 + 1, :] = dg1
        ws_out[...] = ws

    vm = pl.BlockSpec(memory_space=pltpu.VMEM)
    return pl.pallas_call(
        body, name="sum_small", in_specs=[vm] * 3, out_specs=[vm, vm],
        out_shape=[jax.ShapeDtypeStruct(vec_all.shape[1:], F32), jax.ShapeDtypeStruct(ws_all.shape[1:], F32)],
    )(vec_all, ws_all, dg1_all)


def _adamw_math(w, g, m, v):
    m = ADAM_B1 * m + (1.0 - ADAM_B1) * g
    v = ADAM_B2 * v + (1.0 - ADAM_B2) * (g * g)
    m_hat = m / (1.0 - ADAM_B1 ** ADAM_STEP)
    v_hat = v / (1.0 - ADAM_B2 ** ADAM_STEP)
    delta = (-ADAM_LR) * (m_hat / (jnp.sqrt(v_hat) + ADAM_EPS) + ADAM_WD * w)
    return delta, m, v


def _adamw(name, gs, ws, ms, vs):
    n = len(ws)

    def body(*refs):
        ins, outs = refs[:4 * n], refs[4 * n:]
        for p in range(n):
            g_ref, w_ref, m_ref, v_ref = ins[p::n]
            g = g_ref[...]
            outs[4 * p][...] = g
            outs[4 * p + 1][...], outs[4 * p + 2][...], outs[4 * p + 3][...] = _adamw_math(
                w_ref[...], g, m_ref[...], v_ref[...])

    blocks = [pl.BlockSpec((w.shape[0] // ROW_STEPS, w.shape[1]), lambda r: (r, 0)) for w in ws]
    out = pl.pallas_call(
        body, name=name, grid=(ROW_STEPS,), in_specs=blocks * 4, out_specs=[b for b in blocks for _ in range(4)],
        out_shape=[jax.ShapeDtypeStruct(w.shape, F32) for w in ws for _ in range(4)], compiler_params=_params(),
    )(*gs, *ws, *ms, *vs)
    return [tuple(out[4 * p:4 * p + 4]) for p in range(n)]


def _adamw_small(grads, ws, ms, vs):
    n = len(grads)

    def body(*refs):
        g_refs, w_refs, m_refs, v_refs = refs[:n], refs[n:2 * n], refs[2 * n:3 * n], refs[3 * n:4 * n]
        outs = refs[4 * n:]
        for p in range(n):
            d, nm, nv = _adamw_math(w_refs[p][...], g_refs[p][...], m_refs[p][...], v_refs[p][...])
            outs[p][...] = d
            outs[n + p][...] = nm
            outs[2 * n + p][...] = nv

    vm = pl.BlockSpec(memory_space=pltpu.VMEM)
    shapes = [jax.ShapeDtypeStruct(w.shape, F32) for w in ws]
    out = pl.pallas_call(
        body, name="adamw_small", in_specs=[vm] * (4 * n), out_specs=[vm] * (3 * n), out_shape=shapes * 3,
    )(*grads, *ws, *ms, *vs)
    return out[:n], out[n:2 * n], out[2 * n:]


def _unstack_heads(w_st):
    per = HEAD_DIM // N_CHIPS
    return w_st.reshape(N_CHIPS, HEADS, per, HEAD_DIM).transpose(1, 0, 2, 3).reshape(HEADS, HEAD_DIM, HEAD_DIM)


def _stack_heads(w):
    per = HEAD_DIM // N_CHIPS
    return w.reshape(HEADS, N_CHIPS, per, HEAD_DIM).transpose(1, 0, 2, 3).reshape(N_CHIPS, HEADS * per, HEAD_DIM)


def kernel(x, norm_mix_g, w_in, conv_w, conv_b, w_rgate, b_rgate, w_igate, b_igate, lru_lambda, w_out_a, sgu_ln_g, sgu_ln_b, sgu_w_s, sgu_b_s, w_out_b, w_out, norm_mlp_g, w_up, w_down, norm_final_g, loss_target, m_norm_mix_g, m_w_in, m_conv_w, m_conv_b, m_w_rgate, m_b_rgate, m_w_igate, m_b_igate, m_lru_lambda, m_w_out_a, m_sgu_ln_g, m_sgu_ln_b, m_sgu_w_s, m_sgu_b_s, m_w_out_b, m_w_out, m_norm_mlp_g, m_w_up, m_w_down, m_norm_final_g, v_norm_mix_g, v_w_in, v_conv_w, v_conv_b, v_w_rgate, v_b_rgate, v_w_igate, v_b_igate, v_lru_lambda, v_w_out_a, v_sgu_ln_g, v_sgu_ln_b, v_sgu_w_s, v_sgu_b_s, v_w_out_b, v_w_out, v_norm_mlp_g, v_w_up, v_w_down, v_norm_final_g):
    chip = _chip_index(lax.axis_index("x"), lax.axis_index("y"))
    core = lax.axis_index("c")
    quarter_h = HEAD_DIM // N_CHIPS
    quarter_d = D_MODEL // N_CHIPS

    as_2d = lambda a: a.reshape(-1, a.shape[-1])
    big_w = [as_2d(w) for w in (w_in, w_rgate, w_igate, w_out_a, w_out_b, w_out, w_up, w_down)]
    big_m = [as_2d(w) for w in (m_w_in, m_w_rgate, m_w_igate, m_w_out_a, m_w_out_b, m_w_out, m_w_up, m_w_down)]
    big_v = [as_2d(w) for w in (v_w_in, v_w_rgate, v_w_igate, v_w_out_a, v_w_out_b, v_w_out, v_w_up, v_w_down)]

    packed = jnp.concatenate([conv_w[0], b_rgate[0], b_igate[0]], axis=1)
    packed = jnp.concatenate([packed, jnp.zeros_like(packed)], axis=0)
    s_in, s_r, s_i, s_oa, s_ob, s_out, s_up, s_down = [w.astype(BF16) for w in big_w]
    xs, target = x[0], loss_target[0]
    g3 = norm_final_g.reshape(1, D_MODEL)
    bias_s = jnp.broadcast_to(jnp.transpose(sgu_b_s[0])[:, :, None], (CHUNK, GROUPS, GROUP_DIM)).reshape(CHUNK, D_MODEL)
    core_arr = core.reshape(1).astype(jnp.int32)
    place = jnp.stack([chip, core]).astype(jnp.int32)
    quarter = lambda g: g.reshape(N_CHIPS, D_MODEL // N_CHIPS, D_MODEL)

    def pair_add(nm, grads, from_sibling):
        halves = [g.reshape(N_CHIPS, 2, g.shape[1] // 2, g.shape[2]) for g in grads]
        return list(_pair_add("pair_add_" + nm, core_arr, halves, from_sibling))

    def chip_sum(nm, pairs, from_chips):
        return list(_chip_sum("chip_sum_" + nm, place, pairs, from_chips))

    order = jnp.stack([chip, chip ^ 2, chip ^ 1, chip ^ 3]).astype(jnp.int32)
    (z, n1, (w_in_st, wr_st, wi_st)), ((packed_all,), late) = _fwd_in(
        xs, norm_mix_g, [s_in, s_r, s_i], order,
        jobs=[_gather_small_job(packed), _gather_near_job([s_oa, s_ob, s_out])])
    pick = lambda lo, hi: packed_all[:, :HEADS, lo:hi].transpose(1, 0, 2).reshape(HEADS, -1)
    conv_w_full = pick(0, quarter_d)
    br_full = pick(quarter_d, quarter_d + quarter_h).reshape(1, D_MODEL)
    bi_full = pick(quarter_d + quarter_h, quarter_d + 2 * quarter_h).reshape(1, D_MODEL)
    wr, wi = _unstack_heads(wr_st), _unstack_heads(wi_st)
    lru = (conv_w_full, conv_b, wr, br_full, wi, bi_full, lru_lambda)
    sgu = (sgu_ln_g, sgu_ln_b, sgu_w_s[0], bias_s)

    after = lambda arrays, result: lax.optimization_barrier((arrays, result))[0]
    w_up_st, w_dn = _sequencer_call(
        "gather_mlp", 8, _gather_near_job(after([s_up, s_down], n1)).then(_gather_far_job).then(_gather_pass_job))
    w_dn = w_dn.reshape(D_FF, D_MODEL)
    late_step = (3 * (xs.shape[0] // SEQ_TILE) // 4,)
    (ya, *saved), (late,) = _fwd_lru(z, *lru, jobs=[_gather_far_job(late).then(_gather_pass_job, at=late_step)])
    w_oa, w_ob, w_o = [w.reshape(D_MODEL, D_MODEL) for w in late]
    (yb, pa, pb, h1, n2), _ = _fwd_sgu_merge(ya, z, xs, *sgu, w_oa, w_ob, w_o, norm_mlp_g)
    (act, dup, dh2b, dh1, loss_part, dg3, dg2), _ = _mlp(n2, h1, target, w_up_st, w_dn, norm_mlp_g, g3)

    d_down, _ = _weight_grad("dw_down", act, dh2b, N_CHIPS, True, False, D_MODEL)
    r_down, = _sequencer_call("send_w_down", 10, _pair_send_job([d_down]))
    d_up, _ = _weight_grad("dw_up", n2, dup, N_CHIPS, False, True, D_MODEL)
    r_up, = _sequencer_call("send_w_up", 11, _pair_send_job([d_up]))
    (p_down,), (p_up,) = pair_add("w_down", [d_down], [r_down]), pair_add("w_up", [d_up], [r_up])
    (dz, merged, dpa, dpb, dh1b, dlg, dlb, dws, dbs, dcw, dcb, dwr, dbr, dwi, dbi, dlam), ((q_up, q_down),) = _bwd_mix(
        dh1, pa, pb, z, *saved, w_oa, w_ob, w_o, *sgu, conv_w_full, wr, wi, lru_lambda,
        jobs=[_chip_exchange_job([p_up, p_down])])
    names = ("w_in", "w_rgate", "w_igate", "w_out_a", "w_out_b", "w_out", "w_up", "w_down")
    (d_out, d_oa, d_ob), _ = _weight_grads_square("dw_projections", [(merged, dh1b), (ya, dpa), (yb, dpb)])
    mids = [quarter(d_oa), quarter(d_ob), quarter(d_out)]
    r_mids = _sequencer_call("send_mids", 1, _pair_send_job(mids))
    half_up, half_down = chip_sum("mlp", [p_up, p_down], after([q_up, q_down], mids))
    gates = [_stack_heads(dwr).astype(BF16), _stack_heads(dwi).astype(BF16)]
    small = _pack_small(dcw, dcb, dbr, dbi, dlam, dlg, dlb, dg2, dg3, loss_part, dbs)
    p_mids = pair_add("projections", after(mids, [half_up, half_down]), r_mids)
    q_mids = _sequencer_call("exchange_mids", 2, _chip_exchange_job(p_mids))
    d_in, (r_gates, (vec_all, ws_all), (full_up, full_down)) = _weight_grad(
        "dw_in", n1, after(dz, p_mids), N_CHIPS, False, True, IN_SHARD,
        jobs=[_pair_send_job(gates), _gather_all_job([small, dws]), _share_job([half_up, half_down])])
    r_in, = _sequencer_call("send_w_in", 3, _pair_send_job(after([d_in], q_mids)))
    adam_args = {nm: (w, m, v) for nm, w, m, v in zip(names, big_w, big_m, big_v)}

    def adamw(group, nms, grads):
        given = [adam_args[nm] for nm in nms]
        grads = [g.reshape(w.shape) for g, (w, _, _) in zip(grads, given)]
        outs = _adamw("adamw_" + group, grads, *[[a[q] for a in given] for q in range(3)])
        return {nm: (out[0], out[1:]) for nm, out in zip(nms, outs)}

    p_gates = pair_add("gates", gates, r_gates)
    half_mids = chip_sum("projections", p_mids, q_mids)
    full_mids = _sequencer_call("share_mids", 12, _share_job(half_mids))
    p_first = pair_add("w_in", after([d_in], half_mids), [r_in]) + p_gates
    q_first = _sequencer_call("exchange_w_in", 4, _chip_exchange_job(p_first))
    (grad_x, dg1), _ = _bwd_in(dz, xs, dh1, w_in_st, norm_mix_g)
    dg1_all, = _sequencer_call("gather_dg1", 6, _gather_all_job([dg1]))
    done = adamw("mlp", ("w_up", "w_down"), [full_up, full_down])
    q_first = after(q_first, [out[0] for _, out in done.values()])
    half_first = chip_sum("first", p_first, q_first)
    full_first = _sequencer_call("share_last", 5, _share_job(half_first))
    done.update(adamw("projections", names[3:6], after(full_mids, half_first)))
    done.update(adamw("first", names[:3], full_first))
    full, big_out = [done[nm][0] for nm in names], [done[nm][1] for nm in names]

    vec, ws_sum = _sum_small(vec_all, ws_all, dg1_all)
    row = lambda r: vec[r:r + 1]
    shard = lambda a, width: lax.dynamic_slice_in_dim(a, chip * width, width, axis=1)
    g_small = dict(
        norm_mix_g=row(ROW_G1), conv_w=shard(vec[ROW_CW:ROW_CW + CONV_WIDTH], quarter_d), conv_b=row(ROW_CB),
        b_rgate=shard(row(ROW_BR).reshape(HEADS, HEAD_DIM), quarter_h),
        b_igate=shard(row(ROW_BI).reshape(HEADS, HEAD_DIM), quarter_h), lru_lambda=row(ROW_LAM),
        sgu_ln_g=row(ROW_LG), sgu_ln_b=row(ROW_LB),
        sgu_w_s=ws_sum.reshape(CHUNK, GROUPS, CHUNK).transpose(1, 0, 2).reshape(GROUPS * CHUNK, CHUNK),
        sgu_b_s=vec[ROW_BS:ROW_BS + GROUPS, 0:CHUNK], norm_mlp_g=row(ROW_G2), norm_final_g=row(ROW_G3))
    loss = vec[ROW_LOSS, 0]
    small_names = list(g_small)
    given = dict(
        norm_mix_g=(norm_mix_g, m_norm_mix_g, v_norm_mix_g), conv_w=(conv_w, m_conv_w, v_conv_w),
        conv_b=(conv_b, m_conv_b, v_conv_b), b_rgate=(b_rgate, m_b_rgate, v_b_rgate),
        b_igate=(b_igate, m_b_igate, v_b_igate), lru_lambda=(lru_lambda, m_lru_lambda, v_lru_lambda),
        sgu_ln_g=(sgu_ln_g, m_sgu_ln_g, v_sgu_ln_g), sgu_ln_b=(sgu_ln_b, m_sgu_ln_b, v_sgu_ln_b),
        sgu_w_s=(sgu_w_s, m_sgu_w_s, v_sgu_w_s), sgu_b_s=(sgu_b_s, m_sgu_b_s, v_sgu_b_s),
        norm_mlp_g=(norm_mlp_g, m_norm_mlp_g, v_norm_mlp_g), norm_final_g=(norm_final_g, m_norm_final_g, v_norm_final_g))
    g2d = [g_small[nm] for nm in small_names]
    to2d = lambda a, g: a.reshape(g.shape)
    d_s, m_s, v_s = _adamw_small(
        g2d, *[[to2d(given[nm][q], g) for nm, g in zip(small_names, g2d)] for q in range(3)])

    shapes = dict(
        norm_mix_g=norm_mix_g, w_in=w_in, conv_w=conv_w, conv_b=conv_b, w_rgate=w_rgate, b_rgate=b_rgate,
        w_igate=w_igate, b_igate=b_igate, lru_lambda=lru_lambda, w_out_a=w_out_a, sgu_ln_g=sgu_ln_g,
        sgu_ln_b=sgu_ln_b, sgu_w_s=sgu_w_s, sgu_b_s=sgu_b_s, w_out_b=w_out_b, w_out=w_out, norm_mlp_g=norm_mlp_g,
        w_up=w_up, w_down=w_down, norm_final_g=norm_final_g)
    grads, deltas, new_m, new_v = {}, {}, {}, {}
    for nm, g, (d, nmom, nvar) in zip(names, full, big_out):
        grads[nm], deltas[nm], new_m[nm], new_v[nm] = g, d, nmom, nvar
    for p, nm in enumerate(small_names):
        grads[nm], deltas[nm], new_m[nm], new_v[nm] = g2d[p], d_s[p], m_s[p], v_s[p]
    order = list(shapes)
    out = [loss, grad_x[None]]
    for group in (grads, deltas, new_m, new_v):
        out += [group[nm].reshape(shapes[nm].shape) for nm in order]
    return tuple(out)
```

```python
import functools

import jax
import jax.numpy as jnp
from jax import lax
from jax.experimental import pallas as pl
from jax.experimental.pallas import tpu as pltpu
from jax.experimental.pallas import tpu_sc as plsc

F32 = jnp.float32
BF16 = jnp.bfloat16
MESH = pl.DeviceIdType.MESH

D_MODEL = 1024
D_IN = 6 * D_MODEL
D_FF = 4 * D_MODEL
N_CHIPS = 4
IN_SHARD = D_IN // N_CHIPS
HEADS = 4
HEAD_DIM = D_MODEL // HEADS
GROUPS = 4
GROUP_DIM = D_MODEL // GROUPS
CHUNK = 128
CONV_WIDTH = 4
LRU_C = 8.0
NORM_EPS = 1e-6
LN_EPS = 1e-5

ADAM_LR = 0.001
ADAM_B1 = 0.9
ADAM_B2 = 0.999
ADAM_EPS = 1e-08
ADAM_WD = 0.01
ADAM_STEP = 10

SUBLANES = 8
MM_TILE = 512
IN_TILE = 1024
SEQ_TILE = 256
DW_TILE = 2048
VMEM_LIMIT_BYTES = 56 * 1024 * 1024

GELU_K0 = 0.7978845608028654
GELU_K1 = 0.044715


def _params(n_grid_axes=1):
    return pltpu.CompilerParams(
        dimension_semantics=("arbitrary",) * n_grid_axes, vmem_limit_bytes=VMEM_LIMIT_BYTES)


def _resident(shape):
    nd = len(shape)
    return pl.BlockSpec(shape, lambda *_: (0,) * nd, pipeline_mode=pl.Buffered(1))


def _const(shape):
    nd = len(shape)
    return pl.BlockSpec(shape, lambda *_: (0,) * nd)


def _dot(a, b):
    return jnp.dot(a, b, preferred_element_type=F32)


def _dot_nt(a, b):
    return lax.dot_general(a, b, (((1,), (1,)), ((), ())), preferred_element_type=F32)


def _dot_tn(a, b):
    return lax.dot_general(a, b, (((0,), (0,)), ((), ())), preferred_element_type=F32)


def _gelu(x):
    t = jnp.tanh(x * (GELU_K0 + (GELU_K0 * GELU_K1) * (x * x)))
    return x * (0.5 + 0.5 * t)


def _gelu_and_grad(x):
    x2 = x * x
    t = jnp.tanh(x * (GELU_K0 + (GELU_K0 * GELU_K1) * x2))
    s = 0.5 + 0.5 * t
    dg = s + (x * (1.0 - t * t)) * (0.5 * GELU_K0 + (1.5 * GELU_K0 * GELU_K1) * x2)
    return x * s, dg


def _gate(x):
    return 0.5 + 0.5 * jnp.tanh(0.5 * x.astype(F32))


def _rms(x):
    r = lax.rsqrt(jnp.mean(x * x, axis=-1, keepdims=True) + NORM_EPS)
    return x * r, r


def _rms_bwd(dn, xhat, r):
    return r * (dn - xhat * jnp.mean(dn * xhat, axis=-1, keepdims=True))


def _col_sum(v):
    return jnp.sum(v, axis=0, keepdims=True)


def _shift_down(x, tail8, k):
    xs = pltpu.roll(x, k, 0)
    ts = pltpu.roll(tail8, k, 0)
    ridx = lax.broadcasted_iota(jnp.int32, tail8.shape, 0)
    head = jnp.where(ridx < k, ts, xs[0:SUBLANES])
    return jnp.concatenate([head, xs[SUBLANES:]], axis=0)


def _shift_up(x, head8, k):
    n = x.shape[0]
    xs = pltpu.roll(x, n - k, 0)
    hs = pltpu.roll(head8, SUBLANES - k, 0)
    ridx = lax.broadcasted_iota(jnp.int32, head8.shape, 0)
    last = jnp.where(ridx >= SUBLANES - k, hs, xs[n - SUBLANES:n])
    return jnp.concatenate([xs[:n - SUBLANES], last], axis=0)


def _scan_forward(a, b, carry):
    n, cols = a.shape
    groups = n // SUBLANES
    a = a.reshape(groups, SUBLANES, cols)
    b = b.reshape(groups, SUBLANES, cols)
    sub = lax.broadcasted_iota(jnp.int32, a.shape, 1)
    for s in (1, 2, 4):
        a_s = pltpu.roll(a, s, 1)
        b_s = pltpu.roll(b, s, 1)
        m = sub >= s
        b = jnp.where(m, a * b_s + b, b)
        a = jnp.where(m, a * a_s, a)
    out = []
    for g in range(groups):
        h = a[g] * carry + b[g]
        out.append(h)
        carry = h[SUBLANES - 1:SUBLANES]
    return jnp.concatenate(out, axis=0), carry


def _scan_backward(a, b, carry):
    n, cols = a.shape
    groups = n // SUBLANES
    a = a.reshape(groups, SUBLANES, cols)
    b = b.reshape(groups, SUBLANES, cols)
    sub = lax.broadcasted_iota(jnp.int32, a.shape, 1)
    for s in (1, 2, 4):
        a_s = pltpu.roll(a, SUBLANES - s, 1)
        b_s = pltpu.roll(b, SUBLANES - s, 1)
        m = sub < SUBLANES - s
        b = jnp.where(m, a * b_s + b, b)
        a = jnp.where(m, a * a_s, a)
    out = [None] * groups
    for g in reversed(range(groups)):
        h = a[g] * carry + b[g]
        out[g] = h
        carry = h[0:1]
    return jnp.concatenate(out, axis=0), carry


def _softplus_neg(lam):
    e = jnp.exp(-jnp.abs(lam))
    u = 1.0 + e
    log1p_e = jnp.where(u == 1.0, e, jnp.log(u) * (e / jnp.where(u == 1.0, 1.0, u - 1.0)))
    return jnp.maximum(-lam, 0.0) + log1p_e


def _lru_gates(xa, tail8, cw_ref, cb_ref, wr_ref, br_ref, wi_ref, bi_ref, lam_ref):
    cw = cw_ref[...]
    xc = cb_ref[...] + cw[0:1] * xa
    for k in range(1, CONV_WIDTH):
        xc = xc + cw[k:k + 1] * _shift_down(xa, tail8, k)
    xcb = xc.astype(BF16)
    pre_r, pre_i = [], []
    for h in range(HEADS):
        cols = slice(h * HEAD_DIM, (h + 1) * HEAD_DIM)
        pre_r.append(_dot(xcb[:, cols], wr_ref[h]))
        pre_i.append(_dot(xcb[:, cols], wi_ref[h]))
    r = jax.nn.sigmoid(jnp.concatenate(pre_r, axis=1) + br_ref[...])
    ig = jax.nn.sigmoid(jnp.concatenate(pre_i, axis=1) + bi_ref[...])
    _, a, mult, _ = _decay(r, lam_ref)
    return xc, r, ig, a, mult


def _decay(r, lam_ref):
    sp = _softplus_neg(lam_ref[...])
    log_a = ((-LRU_C) * sp) * r
    a = jnp.exp(log_a)
    th = jnp.tanh(log_a)
    q = (-2.0 * th) / (1.0 - th)
    inv = lax.rsqrt(q)
    return sp, a, jnp.where(q > 0.0, q * inv, 0.0), inv


class _Phase:
    def __init__(self, copies, n_sem, n_local, start=None, finish=None):
        self.copies, self.n_sem, self.n_local, self.start, self.finish = copies, n_sem, n_local, start, finish


class _Job:
    def __init__(self, inputs, out_shape, n_sem, copies, peers, aliases=None, n_local=0):
        self.inputs, self.out_shape = list(inputs), list(out_shape)
        self.aliases = dict(aliases or {})
        self.phases = [_Phase(copies, n_sem, n_local)]
        self.peers = tuple(peers)

    def then(self, make, at=None):
        nxt = make(self.out_shape)
        self.phases[-1].finish = at
        nxt.phases[0].start = at
        self.phases += nxt.phases
        self.peers = tuple(sorted(set(self.peers + nxt.peers)))
        return self


def _fused_call(body, jobs, *, name, grid, in_specs, out_specs, out_shape, scratch_shapes=(),
                input_output_aliases=None, compiler_params=None, n_prefetch=0, jobs_start_after=None):
    single = not isinstance(out_shape, (list, tuple))
    out_specs = [out_specs] if single else list(out_specs)
    out_shape = [out_shape] if single else list(out_shape)
    n_scr = len(scratch_shapes)
    in_specs, scratch_shapes = list(in_specs), list(scratch_shapes)
    n_in, n_out = len(in_specs), len(out_shape)
    aliases = dict(input_output_aliases or {})
    in_at, out_at, phases = [], [], []
    for q, job in enumerate(jobs):
        in_at.append(len(in_specs))
        out_at.append(len(out_shape))
        for i, o in job.aliases.items():
            aliases[n_prefetch + len(in_specs) + i] = len(out_shape) + o
        in_specs += [ANY] * len(job.inputs)
        out_specs += [ANY] * len(job.out_shape)
        out_shape += job.out_shape
        for k, phase in enumerate(job.phases):
            phases.append((q, k, phase, len(scratch_shapes)))
            scratch_shapes += [pltpu.SemaphoreType.DMA((phase.n_sem,)), pltpu.SemaphoreType.DMA((phase.n_sem,)),
                               pltpu.SemaphoreType.DMA((max(phase.n_local, 1),))]
    n_in_all, n_out_all = len(in_specs), len(out_shape)
    first_step, last_step = (0,) * len(grid), tuple(g - 1 for g in grid)

    def full_body(*refs):
        prefetch, refs = refs[:n_prefetch], refs[n_prefetch:]
        ins, outs, scr = refs[:n_in_all], refs[n_in_all:n_in_all + n_out_all], refs[n_in_all + n_out_all:]
        ids = [pl.program_id(a) for a in range(len(grid))]
        at_step = lambda step: functools.reduce(jnp.logical_and, [i == k for i, k in zip(ids, step)])

        def copies(q, k, phase, sem_at):
            job = jobs[q]
            mine = outs[out_at[q]:out_at[q] + len(job.out_shape)]
            return phase.copies(ins[in_at[q]:in_at[q] + len(job.inputs)] if k == 0 else mine, mine,
                                *scr[sem_at:sem_at + 3])

        def start(*phase):
            def go():
                sends, _, local = copies(*phase)
                for cp in local + sends:
                    cp.start()
            return go

        def finish(*phase):
            def go():
                sends, arrivals, local = copies(*phase)
                for cp in arrivals:
                    cp.wait_recv()
                for cp in sends:
                    cp.wait_send()
                for cp in local:
                    cp.wait()
            return go

        for phase in phases:
            if phase[2].start is None and jobs_start_after is None:
                pl.when(at_step(first_step))(start(*phase))
        body(*prefetch, *ins[:n_in], *outs[:n_out], *scr[:n_scr])
        for phase in phases:
            pl.when(at_step(phase[2].finish or last_step))(finish(*phase))
            nxt = phase[2].start or jobs_start_after
            if nxt is not None:
                pl.when(at_step(nxt))(start(*phase))

    if n_prefetch:
        layout = dict(grid_spec=pltpu.PrefetchScalarGridSpec(
            num_scalar_prefetch=n_prefetch, grid=grid, in_specs=in_specs, out_specs=out_specs,
            scratch_shapes=scratch_shapes))
    else:
        layout = dict(grid=grid, in_specs=in_specs, out_specs=out_specs, scratch_shapes=scratch_shapes)
    call = pl.pallas_call(
        full_body, name=name, out_shape=out_shape, input_output_aliases=aliases, compiler_params=compiler_params,
        **layout)

    def run(*args):
        res = call(*args, *[a for job in jobs for a in job.inputs])
        mine = res[0] if single else list(res[:n_out])
        return mine, [list(res[at:at + len(job.out_shape)]) for at, job in zip(out_at, jobs)]

    return run


def _fwd_in(x, g1, shards, order, jobs=()):
    t = x.shape[0]
    rows_per_step = min(IN_TILE, t)
    n_tiles = t // rows_per_step
    n = len(shards)
    halves = [s.shape[0] // 2 for s in shards]

    def body(order_ref, x_ref, g_ref, *refs):
        del order_ref
        ins, (z_ref, n_ref), outs = refs[:n], refs[n:n + 2], refs[n + 2:2 * n + 2]
        wbuf, nbuf, send, recv, local = refs[2 * n + 2:]
        s, i = pl.program_id(0), pl.program_id(1)
        x_, y_, c, chips = _place()
        near, far = _near_far(x_, y_, c)
        k_me = _chip_index(x_, y_)

        def block(w, chip, pc):
            return outs[w].at[_chip_index(*chip), pl.ds(pc * halves[w], halves[w]), :]

        def over_ici(w, j, landing):
            return pltpu.make_async_remote_copy(
                src_ref=ins[w].at[pl.ds(c * halves[w], halves[w]), :],
                dst_ref=block(w, chips[j] if landing else (x_, y_), c), send_sem=send.at[6 * w + j],
                recv_sem=recv.at[6 * w + j], device_id=(*chips[j], c), device_id_type=MESH)

        def onward(w, landing):
            blk = block(w, chips[2] if landing else near, c)
            return pltpu.make_async_remote_copy(
                src_ref=blk, dst_ref=blk, send_sem=send.at[6 * w + 2], recv_sem=recv.at[6 * w + 2],
                device_id=(*far, c), device_id_type=MESH)

        def to_sibling(w, j, landing):
            blk = block(w, chips[j], 1 - c if landing else c)
            return pltpu.make_async_remote_copy(
                src_ref=blk, dst_ref=blk, send_sem=send.at[6 * w + 3 + j], recv_sem=recv.at[6 * w + 3 + j],
                device_id=(x_, y_, 1 - c), device_id_type=MESH)

        own = [pltpu.make_async_copy(wbuf, outs[0].at[k_me], local.at[0])]
        own += [pltpu.make_async_copy(ins[w], outs[w].at[k_me], local.at[w]) for w in range(1, n)]

        @pl.when((s == 0) & (i == 0))
        def _():
            for j in range(2):
                for w in range(n):
                    over_ici(w, j, False).start()
            load = pltpu.make_async_copy(ins[0], wbuf, local.at[n])
            load.start()
            load.wait()
            for cp in own:
                cp.start()

        for j in range(N_CHIPS - 1):
            @pl.when((s == j + 1) & (i == 0))
            def _(j=j):
                if j == 0:
                    for k in range(2):
                        for w in range(n):
                            over_ici(w, k, True).wait_recv()
                    for w in range(n):
                        onward(w, False).start()
                    for k in range(2):
                        for w in range(n):
                            to_sibling(w, k, False).start()
                    own[0].wait()
                if j == 2:
                    for w in range(n):
                        onward(w, True).wait_recv()
                    for w in range(n):
                        to_sibling(w, j, False).start()
                for w in range(n):
                    to_sibling(w, j, True).wait_recv()
                load = pltpu.make_async_copy(outs[0].at[_chip_index(*chips[j])], wbuf, local.at[n])
                load.start()
                load.wait()

        rows = pl.ds(pl.multiple_of(i * rows_per_step, rows_per_step), rows_per_step)

        @pl.when(s == 0)
        def _():
            xhat, _ = _rms(x_ref[...])
            nrm = (xhat * g_ref[...]).astype(BF16)
            nbuf[rows, :] = nrm
            n_ref[...] = nrm

        z_ref[...] = _dot(nbuf[rows, :], wbuf[...]).astype(BF16)

        @pl.when((s == N_CHIPS - 1) & (i == n_tiles - 1))
        def _():
            for j in range(N_CHIPS - 1):
                for w in range(n):
                    (over_ici(w, j, False) if j < 2 else onward(w, False)).wait_send()
                    to_sibling(w, j, False).wait_send()
            for cp in own[1:]:
                cp.wait()

    once = lambda s, i, order: (jnp.where(s == 0, i, n_tiles - 1), 0)
    (z, n1, *stacked), job_outs = _fused_call(
        body, jobs, name="fwd_in", grid=(N_CHIPS, n_tiles), n_prefetch=1,
        in_specs=[pl.BlockSpec((rows_per_step, D_MODEL), once), _const((1, D_MODEL))] + [ANY] * n,
        out_specs=[pl.BlockSpec((rows_per_step, IN_SHARD), lambda s, i, order: (i, order[s])),
                   pl.BlockSpec((rows_per_step, D_MODEL), once)] + [ANY] * n,
        out_shape=[jax.ShapeDtypeStruct((t, D_IN), BF16), jax.ShapeDtypeStruct((t, D_MODEL), BF16)]
        + [jax.ShapeDtypeStruct((N_CHIPS,) + s.shape, s.dtype) for s in shards],
        scratch_shapes=[pltpu.VMEM(shards[0].shape, BF16), pltpu.VMEM((t, D_MODEL), BF16),
                        pltpu.SemaphoreType.DMA((6 * n,)),
                        pltpu.SemaphoreType.DMA((6 * n,)), pltpu.SemaphoreType.DMA((n + 1,))],
        compiler_params=_params(2), jobs_start_after=(1, 0),
    )(order, x, g1, *shards)
    return (z, n1, stacked), job_outs


def _fwd_lru(z, conv_w, conv_b, wr, br, wi, bi, lam, jobs=()):
    t = z.shape[0]

    def body(xa_ref, ga_ref, cw_ref, cb_ref, wr_ref, br_ref, wi_ref, bi_ref, lam_ref, ya_ref, h_ref, xc_ref, r_ref,
             ig_ref, tail_ref, carry_ref):
        @pl.when(pl.program_id(0) == 0)
        def _():
            tail_ref[...] = jnp.zeros_like(tail_ref)
            carry_ref[...] = jnp.zeros_like(carry_ref)

        xa = xa_ref[...].astype(F32)
        xc, r, ig, a, mult = _lru_gates(xa, tail_ref[...], cw_ref, cb_ref, wr_ref, br_ref, wi_ref, bi_ref, lam_ref)
        tail_ref[...] = xa[SEQ_TILE - SUBLANES:]
        xc_ref[...], r_ref[...], ig_ref[...] = xc, r, ig
        h, carry = _scan_forward(a, xc * ig * mult, carry_ref[...])
        carry_ref[...] = carry
        h_ref[...] = h
        ya_ref[...] = (h * _gelu(ga_ref[...].astype(F32))).astype(BF16)

    tile = lambda j: pl.BlockSpec((SEQ_TILE, D_MODEL), lambda i: (i, j))
    return _fused_call(
        body, jobs, name="fwd_lru", grid=(t // SEQ_TILE,),
        in_specs=[tile(0), tile(1), _const((CONV_WIDTH, D_MODEL)), _const((1, D_MODEL)),
                  _resident((HEADS, HEAD_DIM, HEAD_DIM)), _const((1, D_MODEL)),
                  _resident((HEADS, HEAD_DIM, HEAD_DIM)), _const((1, D_MODEL)), _const((1, D_MODEL))],
        out_specs=[tile(0)] * 5,
        out_shape=[jax.ShapeDtypeStruct((t, D_MODEL), BF16)] + [jax.ShapeDtypeStruct((t, D_MODEL), F32)] * 4,
        scratch_shapes=[pltpu.VMEM((SUBLANES, D_MODEL), F32), pltpu.VMEM((1, D_MODEL), F32)],
        compiler_params=_params(),
    )(z, z, conv_w, conv_b, wr, br, wi, bi, lam)


def _sgu_forward_parts(ub, vb, lg_ref, lb_ref):
    u, du = _gelu_and_grad(ub.astype(F32))
    vg, dvg = _gelu_and_grad(vb.astype(F32))
    mu = jnp.mean(vg, axis=-1, keepdims=True)
    d = vg - mu
    rstd = lax.rsqrt(jnp.mean(d * d, axis=-1, keepdims=True) + LN_EPS)
    vhat = d * rstd
    vn = (vhat * lg_ref[...] + lb_ref[...]).astype(BF16)
    return u, du, dvg, rstd, vhat, vn


def _causal_mask():
    rows = lax.broadcasted_iota(jnp.int32, (CHUNK, CHUNK), 0)
    cols = lax.broadcasted_iota(jnp.int32, (CHUNK, CHUNK), 1)
    return rows >= cols


def _fwd_sgu_merge(ya, z, x, ln_g, ln_b, w_s, bias_full, w_oa, w_ob, w_out, g2, jobs=()):
    t = x.shape[0]

    def body(ya_ref, ub_ref, vb_ref, m_ref, x_ref, lg_ref, lb_ref, ws_ref, bias_ref, woa_ref, wob_ref, wout_ref, g_ref,
             yb_ref, pa_ref, pb_ref, h1_ref, n2_ref):
        u, _, _, _, _, vn = _sgu_forward_parts(ub_ref[...], vb_ref[...], lg_ref, lb_ref)
        mask = _causal_mask()
        wm = [jnp.where(mask, ws_ref[g], 0.0).astype(BF16) for g in range(GROUPS)]
        for c in range(SEQ_TILE // CHUNK):
            rows = slice(c * CHUNK, (c + 1) * CHUNK)
            for g in range(GROUPS):
                cols = slice(g * GROUP_DIM, (g + 1) * GROUP_DIM)
                sp = _dot(wm[g], vn[rows, cols]) + bias_ref[:, cols]
                yb_ref[rows, cols] = (u[rows, cols] * sp).astype(BF16)
        pa = _dot(ya_ref[...], woa_ref[...])
        pb = _dot(yb_ref[...], wob_ref[...])
        pa_ref[...] = pa
        pb_ref[...] = pb
        merged = _gate(m_ref[:, :D_MODEL]) * pa + _gate(m_ref[:, D_MODEL:]) * pb
        h1 = x_ref[...] + _dot(merged.astype(BF16), wout_ref[...])
        h1_ref[...] = h1
        xhat, _ = _rms(h1)
        n2_ref[...] = (xhat * g_ref[...]).astype(BF16)

    tile = lambda j: pl.BlockSpec((SEQ_TILE, D_MODEL), lambda i: (i, j))
    sq = _resident((D_MODEL, D_MODEL))
    vec = _const((1, D_MODEL))
    bf, f32 = jax.ShapeDtypeStruct((t, D_MODEL), BF16), jax.ShapeDtypeStruct((t, D_MODEL), F32)
    return _fused_call(
        body, jobs, name="fwd_sgu_merge", grid=(t // SEQ_TILE,),
        in_specs=[tile(0), tile(2), tile(3), pl.BlockSpec((SEQ_TILE, 2 * D_MODEL), lambda i: (i, 2)), tile(0), vec, vec,
                  _const((GROUPS, CHUNK, CHUNK)), _const((CHUNK, D_MODEL)), sq, sq, sq, vec],
        out_specs=[tile(0)] * 5,
        out_shape=[bf, f32, f32, f32, bf],
        compiler_params=_params(),
    )(ya, z, z, z, x, ln_g, ln_b, w_s, bias_full, w_oa, w_ob, w_out, g2)


def _mlp(n2, h1, target, w_up_st, w_down, g2, g3, jobs=()):
    t = n2.shape[0]

    def body(n2_ref, h1_ref, tgt_ref, wup_ref, wdown_ref, g2_ref, g3_ref, act_ref, dup_ref, dh2b_ref, dh1_ref,
             loss_ref, dg3_ref, dg2_ref, relu_ref):
        @pl.when(pl.program_id(0) == 0)
        def _():
            for ref in (loss_ref, dg3_ref, dg2_ref):
                ref[...] = jnp.zeros_like(ref)

        n2 = n2_ref[...]
        h1 = h1_ref[...]
        h2 = h1
        for k in range(N_CHIPS):
            cols = slice(k * D_MODEL, (k + 1) * D_MODEL)
            r = jnp.maximum(_dot(n2, wup_ref[k]), 0.0)
            relu_ref[:, cols] = r
            act = (r * r).astype(BF16)
            act_ref[:, cols] = act
            h2 = h2 + _dot(act, wdown_ref[cols, :])
        xhat, r3 = _rms(h2)
        diff = xhat * g3_ref[...] - tgt_ref[...]
        sq = jnp.sum(diff * diff, axis=1, keepdims=True)
        loss_ref[...] = loss_ref[...] + (0.5 / D_MODEL) * jnp.sum(sq, axis=0, keepdims=True)
        dy = diff * (1.0 / D_MODEL)
        dg3_ref[...] = dg3_ref[...] + _col_sum(dy * xhat)
        dh2 = _rms_bwd(dy * g3_ref[...], xhat, r3)
        dh2b = dh2.astype(BF16)
        dh2b_ref[...] = dh2b
        dn2 = jnp.zeros((SEQ_TILE, D_MODEL), F32)
        for k in range(N_CHIPS):
            cols = slice(k * D_MODEL, (k + 1) * D_MODEL)
            dup = (_dot_nt(dh2b, wdown_ref[cols, :]) * (2.0 * relu_ref[:, cols])).astype(BF16)
            dup_ref[:, cols] = dup
            dn2 = dn2 + _dot_nt(dup, wup_ref[k])
        xhat, r2 = _rms(h1)
        dg2_ref[...] = dg2_ref[...] + _col_sum(dn2 * xhat)
        dh1_ref[...] = dh2 + _rms_bwd(dn2 * g2_ref[...], xhat, r2)

    tile = pl.BlockSpec((SEQ_TILE, D_MODEL), lambda i: (i, 0))
    wide = pl.BlockSpec((SEQ_TILE, D_FF), lambda i: (i, 0))
    vec = _const((1, D_MODEL))
    vec_shape = jax.ShapeDtypeStruct((1, D_MODEL), F32)
    return _fused_call(
        body, jobs, name="mlp", grid=(t // SEQ_TILE,),
        in_specs=[tile, tile, tile, _resident((N_CHIPS, D_MODEL, D_MODEL)), _resident((D_FF, D_MODEL)), vec, vec],
        out_specs=[wide, wide, tile, tile, _const((SUBLANES, 128)), vec, vec],
        out_shape=[jax.ShapeDtypeStruct((t, D_FF), BF16), jax.ShapeDtypeStruct((t, D_FF), BF16),
                   jax.ShapeDtypeStruct((t, D_MODEL), BF16), jax.ShapeDtypeStruct((t, D_MODEL), F32),
                   jax.ShapeDtypeStruct((SUBLANES, 128), F32), vec_shape, vec_shape],
        scratch_shapes=[pltpu.VMEM((SEQ_TILE, D_FF), F32)],
        compiler_params=_params(),
    )(n2, h1, target, w_up_st, w_down, g2, g3)


def _bwd_mix(dh1, pa, pb, z, h, xc, r, ig, w_oa, w_ob, w_out, ln_g, ln_b, w_s, bias_full, conv_w, wr, wi, lam, jobs=()):
    t = dh1.shape[0]
    n_tiles = t // SEQ_TILE
    per_tile = SEQ_TILE // SUBLANES

    def merge_part(dh1_ref, pa_ref, pb_ref, m_ref, woa_ref, wob_ref, wout_ref, dz_ref, dya_ref, dyb_ref, mg_ref,
                   dpa_ref, dpb_ref, dh1b_ref):
        dh1b = dh1_ref[...].astype(BF16)
        dh1b_ref[...] = dh1b
        dm = _dot_nt(dh1b, wout_ref[...])
        pa = pa_ref[...]
        pb = pb_ref[...]
        sa = _gate(m_ref[:, :D_MODEL])
        sb = _gate(m_ref[:, D_MODEL:])
        mg_ref[...] = (sa * pa + sb * pb).astype(BF16)
        dpa = dm * sa
        dpb = dm * sb
        dz_ref[:, :D_MODEL] = ((dpa * pa) * (1.0 - sa)).astype(BF16)
        dz_ref[:, D_MODEL:] = ((dpb * pb) * (1.0 - sb)).astype(BF16)
        dpa = dpa.astype(BF16)
        dpb = dpb.astype(BF16)
        dpa_ref[...] = dpa
        dpb_ref[...] = dpb
        dya_ref[...] = _dot_nt(dpa, woa_ref[...])
        dyb_ref[...] = _dot_nt(dpb, wob_ref[...])

    def sgu_part(dyb_ref, ub_ref, vb_ref, lg_ref, lb_ref, ws_ref, bias_ref, dz_ref, dlg_ref, dlb_ref, dws_ref, dbs_ref,
                 dvn_ref, dsp_acc):
        i = pl.program_id(0)

        @pl.when(i == 0)
        def _():
            dlg_ref[...] = jnp.zeros_like(dlg_ref)
            dlb_ref[...] = jnp.zeros_like(dlb_ref)
            dws_ref[...] = jnp.zeros_like(dws_ref)
            dsp_acc[...] = jnp.zeros_like(dsp_acc)

        u, du, dvg, rstd, vhat, vn = _sgu_forward_parts(ub_ref[...], vb_ref[...], lg_ref, lb_ref)
        dyb = dyb_ref[...]
        mask = _causal_mask()
        wm = [jnp.where(mask, ws_ref[g], 0.0).astype(BF16) for g in range(GROUPS)]
        for c in range(SEQ_TILE // CHUNK):
            rows = slice(c * CHUNK, (c + 1) * CHUNK)
            for g in range(GROUPS):
                cols = slice(g * GROUP_DIM, (g + 1) * GROUP_DIM)
                vn_blk = vn[rows, cols]
                sp = _dot(wm[g], vn_blk) + bias_ref[:, cols]
                dyb_blk = dyb[rows, cols]
                dz_ref[rows, cols] = (dyb_blk * sp * du[rows, cols]).astype(BF16)
                dsp = dyb_blk * u[rows, cols]
                dsp_acc[:, cols] = dsp_acc[:, cols] + dsp
                dspb = dsp.astype(BF16)
                dvn_ref[rows, cols] = _dot_tn(wm[g], dspb)
                wcols = slice(g * CHUNK, (g + 1) * CHUNK)
                dws_ref[:, wcols] = dws_ref[:, wcols] + jnp.where(mask, _dot_nt(dspb, vn_blk), 0.0)
        dvn = dvn_ref[...]
        dlg_ref[...] = dlg_ref[...] + _col_sum(dvn * vhat)
        dlb_ref[...] = dlb_ref[...] + _col_sum(dvn)
        dvhat = dvn * lg_ref[...]
        dvgel = rstd * (dvhat - jnp.mean(dvhat, axis=-1, keepdims=True)
                        - vhat * jnp.mean(dvhat * vhat, axis=-1, keepdims=True))
        dz_ref[:, D_MODEL:] = (dvgel * dvg).astype(BF16)

        @pl.when(i == n_tiles - 1)
        def _():
            lane = lax.broadcasted_iota(jnp.int32, (CHUNK, 128), 1)
            out = jnp.zeros((CHUNK, 128), F32)
            for g in range(GROUPS):
                s = jnp.sum(dsp_acc[:, g * GROUP_DIM:(g + 1) * GROUP_DIM], axis=1, keepdims=True)
                out = out + jnp.where(lane == g, s, 0.0)
            dbs_ref[...] = out

    def lru_part(dya_ref, xa_ref, ga_ref, h_ref, h_prev_ref, xc_ref, r_ref, ig_ref, cw_ref, wr_ref, wi_ref, lam_ref,
                 dz_ref, dcw_ref, dcb_ref, dwr_ref, dbr_ref, dwi_ref, dbi_ref, dlam_ref, lam_carry, dxc_head):
        i = pl.program_id(0)

        @pl.when(i == 0)
        def _():
            for ref in (dcw_ref, dcb_ref, dwr_ref, dbr_ref, dwi_ref, dbi_ref, dlam_ref, lam_carry, dxc_head):
                ref[...] = jnp.zeros_like(ref)

        first_tile = i == n_tiles - 1
        for hd in range(HEADS):
            cols = slice(hd * HEAD_DIM, (hd + 1) * HEAD_DIM)
            ga_cols = slice(D_MODEL + hd * HEAD_DIM, D_MODEL + (hd + 1) * HEAD_DIM)
            lam_cols = lam_ref.at[:, cols]
            h_tail = jnp.where(first_tile, 0.0, h_prev_ref[:, cols])
            xc, r, ig = xc_ref[:, cols], r_ref[:, cols], ig_ref[:, cols]
            xcb = xc.astype(BF16)
            sp, a, mult, inv_mult = _decay(r, lam_cols)
            h = h_ref[:, cols]
            h_prev = _shift_down(h, h_tail, 1)
            dya = dya_ref[:, cols]
            gg, dgg = _gelu_and_grad(ga_ref[:, cols].astype(F32))
            dz_ref[:, ga_cols] = (dya * h * dgg).astype(BF16)
            ones = jnp.ones((SUBLANES, HEAD_DIM), F32)
            lam_t, lam_first = _scan_backward(_shift_up(a, ones, 1), dya * gg, lam_carry[:, cols])
            lam_carry[:, cols] = a[0:1] * lam_first
            lam_ig = lam_t * ig
            dxc_direct = lam_ig * mult
            dmult = lam_ig * xc
            dla = a * (lam_t * h_prev - (dmult * a) * inv_mult)
            dla_r = dla * r
            dlam_ref[:, cols] = dlam_ref[:, cols] + _col_sum(dla_r) * (LRU_C * jax.nn.sigmoid(-lam_cols[...]))
            dpr = (dla_r * ((-LRU_C) * sp)) * (1.0 - r)
            dpi = (dxc_direct * xc) * (1.0 - ig)
            dbr_ref[:, cols] = dbr_ref[:, cols] + _col_sum(dpr)
            dbi_ref[:, cols] = dbi_ref[:, cols] + _col_sum(dpi)
            dprb = dpr.astype(BF16)
            dpib = dpi.astype(BF16)
            dxc = dxc_direct + (_dot_nt(dprb, wr_ref[hd]) + _dot_nt(dpib, wi_ref[hd]))
            dwr_ref[hd] = dwr_ref[hd] + _dot_tn(xcb, dprb)
            dwi_ref[hd] = dwi_ref[hd] + _dot_tn(xcb, dpib)
            dcb_ref[:, cols] = dcb_ref[:, cols] + _col_sum(dxc)
            cw = cw_ref[:, cols]
            head = dxc_head[:, cols]
            xa = xa_ref[:, cols].astype(F32)
            dxa = cw[0:1] * dxc
            dcw_ref[0:1, cols] = dcw_ref[0:1, cols] + _col_sum(dxc * xa)
            for k in range(1, CONV_WIDTH):
                dxc_k = _shift_up(dxc, head, k)
                dxa = dxa + cw[k:k + 1] * dxc_k
                dcw_ref[k:k + 1, cols] = dcw_ref[k:k + 1, cols] + _col_sum(dxc_k * xa)
            dxc_head[:, cols] = dxc[0:SUBLANES]
            dz_ref[:, cols] = dxa.astype(BF16)

    def body(dh1_ref, pa_ref, pb_ref, z_ref, h_ref, h_prev_ref, xc_ref, r_ref, ig_ref, woa_ref, wob_ref, wout_ref,
             lg_ref, lb_ref, ws_ref, bias_ref, cw_ref, wr_ref, wi_ref, lam_ref, dz_ref, mg_ref, dpa_ref, dpb_ref,
             dh1b_ref, dlg_ref, dlb_ref, dws_ref, dbs_ref, dcw_ref, dcb_ref, dwr_ref, dbr_ref, dwi_ref, dbi_ref,
             dlam_ref, dya_ref, dyb_ref, dvn_ref, dsp_acc, lam_carry, dxc_head):
        def cols(ref, first, count):
            return ref.at[:, pl.ds(first * D_MODEL, count * D_MODEL)]

        merge_part(dh1_ref, pa_ref, pb_ref, cols(z_ref, 4, 2), woa_ref, wob_ref, wout_ref, cols(dz_ref, 4, 2), dya_ref,
                   dyb_ref, mg_ref, dpa_ref, dpb_ref, dh1b_ref)
        sgu_part(dyb_ref, cols(z_ref, 2, 1), cols(z_ref, 3, 1), lg_ref, lb_ref, ws_ref, bias_ref, cols(dz_ref, 2, 2),
                 dlg_ref, dlb_ref, dws_ref, dbs_ref, dvn_ref, dsp_acc)
        lru_part(dya_ref, cols(z_ref, 0, 1), cols(z_ref, 1, 1), h_ref, h_prev_ref, xc_ref, r_ref, ig_ref, cw_ref, wr_ref,
                 wi_ref, lam_ref, cols(dz_ref, 0, 2), dcw_ref, dcb_ref, dwr_ref, dbr_ref, dwi_ref, dbi_ref, dlam_ref,
                 lam_carry, dxc_head)

    rev = lambda i: n_tiles - 1 - i
    tile = pl.BlockSpec((SEQ_TILE, D_MODEL), lambda i: (rev(i), 0))
    row = pl.BlockSpec((SEQ_TILE, D_IN), lambda i: (rev(i), 0))
    prev8 = pl.BlockSpec((SUBLANES, D_MODEL), lambda i: (jnp.maximum(rev(i) * per_tile - 1, 0), 0))
    vec = _const((1, D_MODEL))
    sq = _resident((D_MODEL, D_MODEL))
    gate_w = _resident((HEADS, HEAD_DIM, HEAD_DIM))
    gate_acc = _const((HEADS, HEAD_DIM, HEAD_DIM))
    vec_shape = jax.ShapeDtypeStruct((1, D_MODEL), F32)
    gate_shape = jax.ShapeDtypeStruct((HEADS, HEAD_DIM, HEAD_DIM), F32)
    act_bf = jax.ShapeDtypeStruct((t, D_MODEL), BF16)
    return _fused_call(
        body, jobs, name="bwd_mix", grid=(n_tiles,),
        in_specs=[tile, tile, tile, row, tile, prev8, tile, tile, tile, sq, sq, sq, vec, vec,
                  _const((GROUPS, CHUNK, CHUNK)), _const((CHUNK, D_MODEL)), _const((CONV_WIDTH, D_MODEL)), gate_w, gate_w,
                  vec],
        out_specs=[row, tile, tile, tile, tile, vec, vec, _const((CHUNK, GROUPS * CHUNK)), _const((CHUNK, 128)),
                   _const((SUBLANES, D_MODEL)), vec, gate_acc, vec, gate_acc, vec, vec],
        out_shape=[jax.ShapeDtypeStruct((t, D_IN), BF16), act_bf, act_bf, act_bf, act_bf, vec_shape, vec_shape,
                   jax.ShapeDtypeStruct((CHUNK, GROUPS * CHUNK), F32), jax.ShapeDtypeStruct((CHUNK, 128), F32),
                   jax.ShapeDtypeStruct((SUBLANES, D_MODEL), F32), vec_shape, gate_shape, vec_shape, gate_shape,
                   vec_shape, vec_shape],
        scratch_shapes=[pltpu.VMEM((SEQ_TILE, D_MODEL), F32), pltpu.VMEM((SEQ_TILE, D_MODEL), F32),
                        pltpu.VMEM((SEQ_TILE, D_MODEL), F32), pltpu.VMEM((CHUNK, D_MODEL), F32),
                        pltpu.VMEM((1, D_MODEL), F32), pltpu.VMEM((SUBLANES, D_MODEL), F32)],
        compiler_params=_params(),
    )(dh1, pa, pb, z, h, h, xc, r, ig, w_oa, w_ob, w_out, ln_g, ln_b, w_s, bias_full, conv_w, wr, wi, lam)


def _bwd_in(dz, x, dh1, w_in_st, g1, jobs=()):
    t = x.shape[0]

    def body(dz_ref, x_ref, dh1_ref, w_ref, g_ref, dx_ref, dg1_ref):
        @pl.when(pl.program_id(0) == 0)
        def _():
            dg1_ref[...] = jnp.zeros_like(dg1_ref)

        dn1 = jnp.zeros((MM_TILE, D_MODEL), F32)
        for k in range(N_CHIPS):
            dn1 = dn1 + _dot_nt(dz_ref[:, k * IN_SHARD:(k + 1) * IN_SHARD], w_ref[k])
        xhat, r1 = _rms(x_ref[...])
        dg1_ref[...] = dg1_ref[...] + _col_sum(dn1 * xhat)
        dx_ref[...] = dh1_ref[...] + _rms_bwd(dn1 * g_ref[...], xhat, r1)

    tile = pl.BlockSpec((MM_TILE, D_MODEL), lambda i: (i, 0))
    return _fused_call(
        body, jobs, name="bwd_in", grid=(t // MM_TILE,),
        in_specs=[pl.BlockSpec((MM_TILE, D_IN), lambda i: (i, 0)), tile, tile,
                  _resident((N_CHIPS, D_MODEL, IN_SHARD)), _const((1, D_MODEL))],
        out_specs=[tile, _const((1, D_MODEL))],
        out_shape=[jax.ShapeDtypeStruct((t, D_MODEL), F32), jax.ShapeDtypeStruct((1, D_MODEL), F32)],
        compiler_params=_params(),
    )(dz, x, dh1, w_in_st, g1)


def _weight_grad(name, a, b, n_blocks, a_varies, b_varies, width, jobs=()):
    t = a.shape[0]
    rows = min(DW_TILE, t)
    n_t = t // rows

    def body(a_ref, b_ref, o_ref, acc_ref):
        s = pl.program_id(1)
        part = _dot_tn(a_ref[...], b_ref[...])

        @pl.when(s == 0)
        def _():
            acc_ref[...] = part

        @pl.when(s > 0)
        def _():
            acc_ref[...] = acc_ref[...] + part

        @pl.when(s == n_t - 1)
        def _():
            o_ref[...] = acc_ref[...].astype(BF16)

    return _fused_call(
        body, jobs, name=name, grid=(n_blocks, n_t),
        in_specs=[pl.BlockSpec((rows, D_MODEL), (lambda j, s: (s, j)) if a_varies else (lambda j, s: (s, 0))),
                  pl.BlockSpec((rows, width), (lambda j, s: (s, j)) if b_varies else (lambda j, s: (s, 0)))],
        out_specs=pl.BlockSpec((None, D_MODEL, width), lambda j, s: (j, 0, 0)),
        out_shape=jax.ShapeDtypeStruct((n_blocks, D_MODEL, width), BF16),
        scratch_shapes=[pltpu.VMEM((D_MODEL, width), F32)],
        compiler_params=_params(2),
    )(a, b)


def _weight_grads_square(name, pairs, jobs=()):
    n = len(pairs)
    t = pairs[0][0].shape[0]
    rows = min(2 * MM_TILE, t)
    n_t = t // rows

    def body(*refs):
        ins, outs, accs = refs[:2 * n], refs[2 * n:3 * n], refs[3 * n:]
        s = pl.program_id(0)
        for k in range(n):
            part = _dot_tn(ins[2 * k][...], ins[2 * k + 1][...])

            @pl.when(s == 0)
            def _(k=k, part=part):
                accs[k][...] = part

            @pl.when(s > 0)
            def _(k=k, part=part):
                accs[k][...] = accs[k][...] + part

            @pl.when(s == n_t - 1)
            def _(k=k):
                outs[k][...] = accs[k][...].astype(BF16)

    tile = pl.BlockSpec((rows, D_MODEL), lambda s: (s, 0))
    return _fused_call(
        body, jobs, name=name, grid=(n_t,), in_specs=[tile] * (2 * n), out_specs=[_const((D_MODEL, D_MODEL))] * n,
        out_shape=[jax.ShapeDtypeStruct((D_MODEL, D_MODEL), BF16)] * n,
        scratch_shapes=[pltpu.VMEM((D_MODEL, D_MODEL), F32)] * n,
        compiler_params=_params(),
    )(*[x for pair in pairs for x in pair])


def _place():
    x, y, c = lax.axis_index("x"), lax.axis_index("y"), lax.axis_index("c")
    other_chips = [(1 - x, y), (x, 1 - y), (1 - x, 1 - y)]
    return x, y, c, other_chips


def _chip_index(px, py):
    return 2 * px + py


ANY = pl.BlockSpec(memory_space=pl.ANY)
SIBLING = ((0, 0, 1),)
NEIGHBOURS = ((1, 0, 0), (0, 1, 0))
OTHER_CHIPS = NEIGHBOURS + ((1, 1, 0),)


def _near_far(x, y, c):
    return (x ^ (1 - c), y ^ c), (x ^ c, y ^ (1 - c))


def _gather_near_job(shards):
    n = len(shards)
    halves = [s.shape[0] // 2 for s in shards]

    def copies(ins, outs, send, recv, local):
        x, y, c, _ = _place()
        near, _ = _near_far(x, y, c)

        def block(w, chip, pc):
            return outs[w].at[_chip_index(*chip), pl.ds(pc * halves[w], halves[w]), :]

        def copy(w, k, chip, pc, to, src=None):
            return pltpu.make_async_remote_copy(
                src_ref=block(w, chip, pc) if src is None else src, dst_ref=block(w, chip, pc),
                send_sem=send.at[2 * w + k], recv_sem=recv.at[2 * w + k], device_id=to, device_id_type=MESH)

        sends, arrivals, own = [], [], []
        for w in range(n):
            src = ins[w].at[pl.ds(c * halves[w], halves[w]), :]
            own.append(pltpu.make_async_copy(src, block(w, (x, y), c), local.at[w]))
            sends += [copy(w, 0, (x, y), c, (*near, c), src), copy(w, 1, (x, y), c, (x, y, 1 - c), src)]
            arrivals += [copy(w, 0, near, c, (x, y, c)), copy(w, 1, (x, y), 1 - c, (x, y, c))]
        return sends, arrivals, own

    return _Job(shards, [jax.ShapeDtypeStruct((N_CHIPS,) + s.shape, s.dtype) for s in shards], 2 * n, copies,
                NEIGHBOURS + SIBLING, n_local=n)


def _gather_far_job(stacked):
    n = len(stacked)
    halves = [s.shape[1] // 2 for s in stacked]

    def copies(ins, outs, send, recv, local):
        del ins, local
        x, y, c, _ = _place()
        near, far = _near_far(x, y, c)

        def copy(w, k, chip):
            blk = outs[w].at[_chip_index(*chip), pl.ds(c * halves[w], halves[w]), :]
            return pltpu.make_async_remote_copy(
                src_ref=blk, dst_ref=blk, send_sem=send.at[2 * w + k], recv_sem=recv.at[2 * w + k],
                device_id=(*far, c), device_id_type=MESH)

        sends = [copy(w, k, chip) for w in range(n) for k, chip in enumerate(((x, y), near))]
        arrivals = [copy(w, k, chip) for w in range(n) for k, chip in enumerate((far, (1 - x, 1 - y)))]
        return sends, arrivals, []

    return _Job(stacked, [jax.ShapeDtypeStruct(s.shape, s.dtype) for s in stacked], 2 * n, copies, NEIGHBOURS,
                aliases={w: w for w in range(n)})


def _gather_pass_job(stacked):
    n = len(stacked)
    halves = [s.shape[1] // 2 for s in stacked]

    def copies(ins, outs, send, recv, local):
        del ins, local
        x, y, c, chips = _place()

        def copy(w, j, chip, pc, to):
            blk = outs[w].at[_chip_index(*chip), pl.ds(pc * halves[w], halves[w]), :]
            return pltpu.make_async_remote_copy(
                src_ref=blk, dst_ref=blk, send_sem=send.at[3 * w + j], recv_sem=recv.at[3 * w + j], device_id=to,
                device_id_type=MESH)

        sends = [copy(w, j, chip, c, (x, y, 1 - c)) for w in range(n) for j, chip in enumerate(chips)]
        arrivals = [copy(w, j, chip, 1 - c, (x, y, c)) for w in range(n) for j, chip in enumerate(chips)]
        return sends, arrivals, []

    return _Job(stacked, [jax.ShapeDtypeStruct(s.shape, s.dtype) for s in stacked], 3 * n, copies, SIBLING,
                aliases={w: w for w in range(n)})


def _gather_small_job(block):
    def copies(ins, outs, send, recv, local):
        x, y, c, chips = _place()

        def copy(j, chip_from, to):
            return pltpu.make_async_remote_copy(
                src_ref=ins[0], dst_ref=outs[0].at[_chip_index(*chip_from)], send_sem=send.at[j],
                recv_sem=recv.at[j], device_id=to, device_id_type=MESH)

        own = [pltpu.make_async_copy(ins[0], outs[0].at[_chip_index(x, y)], local.at[0])]
        sends = [copy(j, (x, y), (*chip, c)) for j, chip in enumerate(chips)]
        arrivals = [copy(j, chip, (x, y, c)) for j, chip in enumerate(chips)]
        return sends, arrivals, own

    return _Job([block], [jax.ShapeDtypeStruct((N_CHIPS,) + block.shape, block.dtype)], 3, copies, OTHER_CHIPS,
                n_local=1)


def _pair_send_job(grads):
    n = len(grads)
    halves = [g.shape[1] // 2 for g in grads]

    def copies(ins, outs, send, recv, local):
        del local
        x, y, c, _ = _place()
        sends = [pltpu.make_async_remote_copy(
            src_ref=ins[w].at[:, pl.ds((1 - c) * halves[w], halves[w]), :], dst_ref=outs[w], send_sem=send.at[w],
            recv_sem=recv.at[w], device_id=(x, y, 1 - c), device_id_type=MESH) for w in range(n)]
        return sends, sends, []

    return _Job(grads, [jax.ShapeDtypeStruct((N_CHIPS, h, g.shape[2]), g.dtype) for g, h in zip(grads, halves)], n,
                copies, SIBLING)


ROW_STEPS = 4
SUM_STEPS = 2


def _pair_add(name, core, mine, theirs):
    n = len(mine)

    def body(core_ref, *refs):
        del core_ref
        for a_ref, b_ref, o_ref in zip(refs[:n], refs[n:2 * n], refs[2 * n:]):
            o_ref[...] = (a_ref[...].astype(F32) + b_ref[...].astype(F32)).astype(BF16)

    half = lambda a: pl.BlockSpec((2, None) + a.shape[2:], lambda k, core_ref: (k, core_ref[0], 0, 0))
    block = lambda b: pl.BlockSpec((2,) + b.shape[1:], lambda k, core_ref: (k, 0, 0))
    return pl.pallas_call(
        body, name=name,
        grid_spec=pltpu.PrefetchScalarGridSpec(
            num_scalar_prefetch=1, grid=(N_CHIPS // 2,),
            in_specs=[half(a) for a in mine] + [block(b) for b in theirs], out_specs=[block(b) for b in theirs]),
        out_shape=[jax.ShapeDtypeStruct(b.shape, BF16) for b in theirs],
        compiler_params=_params(),
    )(core, *mine, *theirs)


def _sequencer_call(name, collective_id, job):
    steps, peers = job.phases, job.peers
    ins = [jax.new_ref(a, memory_space=pltpu.MemorySpace.HBM) for a in job.inputs]
    outs = [ins[{o: i for i, o in job.aliases.items()}[k]] if k in job.aliases.values()
            else jax.empty_ref(shape, memory_space=pltpu.MemorySpace.HBM) for k, shape in enumerate(job.out_shape)]
    sems = [pltpu.SemaphoreType.DMA((n,)) for step in steps for n in (step.n_sem, step.n_sem, max(step.n_local, 1))]

    @pl.kernel(mesh=plsc.ScalarSubcoreMesh(axis_name="sequencer", num_cores=1), name=name, scratch_types=tuple(sems),
               compiler_params=pltpu.CompilerParams(collective_id=collective_id))
    def launch(*sem_refs):
        x, y, c, _ = _place()
        barrier = pltpu.get_barrier_semaphore()
        for dx, dy, dc in peers:
            pl.semaphore_signal(barrier, inc=1, device_id=(x ^ dx, y ^ dy, c ^ dc), device_id_type=MESH)
        pl.semaphore_wait(barrier, len(peers))
        for k, step in enumerate(steps):
            sends, arrivals, own = step.copies(ins if k == 0 else outs, outs, *sem_refs[3 * k:3 * k + 3])
            for cp in own + sends:
                cp.start()
            for cp in arrivals:
                cp.wait_recv()
            for cp in sends:
                cp.wait_send()
            for cp in own:
                cp.wait()

    launch()
    return [ref[...] for ref in outs]


def _chip_exchange_job(sums):
    n = len(sums)

    def copies(ins, outs, send, recv, local):
        del local
        _, _, c, chips = _place()
        sends = [pltpu.make_async_remote_copy(
            src_ref=ins[w].at[_chip_index(*chip)], dst_ref=outs[w].at[j], send_sem=send.at[3 * w + j],
            recv_sem=recv.at[3 * w + j], device_id=(*chip, c), device_id_type=MESH)
            for w in range(n) for j, chip in enumerate(chips)]
        return sends, sends, []

    return _Job(sums, [jax.ShapeDtypeStruct((N_CHIPS - 1,) + s.shape[1:], s.dtype) for s in sums], 3 * n, copies,
                OTHER_CHIPS)


def _chip_sum(name, place, mine, theirs):
    n = len(mine)

    def body(place_ref, *refs):
        del place_ref
        for p_ref, q_ref, o_ref in zip(refs[:n], refs[n:2 * n], refs[2 * n:]):
            acc = p_ref[...].astype(F32)
            for j in range(N_CHIPS - 1):
                acc = acc + q_ref[j].astype(F32)
            o_ref[...] = acc

    def block(p, lead, pick):
        return pl.BlockSpec((lead, p.shape[1] // SUM_STEPS, p.shape[2]), lambda r, place_ref: (pick(place_ref), r, 0))

    return pl.pallas_call(
        body, name=name,
        grid_spec=pltpu.PrefetchScalarGridSpec(
            num_scalar_prefetch=1, grid=(SUM_STEPS,),
            in_specs=[block(p, None, lambda place_ref: place_ref[0]) for p in mine]
            + [block(p, N_CHIPS - 1, lambda place_ref: 0) for p in mine],
            out_specs=[block(p, None, lambda place_ref: place_ref[1]) for p in mine]),
        out_shape=[jax.ShapeDtypeStruct((2,) + p.shape[1:], F32) for p in mine],
        compiler_params=_params(),
    )(place, *mine, *theirs)


def _share_job(bufs):
    n = len(bufs)

    def copies(ins, outs, send, recv, local):
        del ins, local
        x, y, c, _ = _place()

        def copy(w, half):
            return pltpu.make_async_remote_copy(
                src_ref=outs[w].at[half], dst_ref=outs[w].at[half], send_sem=send.at[w], recv_sem=recv.at[w],
                device_id=(x, y, 1 - c), device_id_type=MESH)

        return [copy(w, c) for w in range(n)], [copy(w, 1 - c) for w in range(n)], []

    return _Job(bufs, [jax.ShapeDtypeStruct(b.shape, b.dtype) for b in bufs], n, copies, SIBLING,
                aliases={w: w for w in range(n)})


SMALL_ROWS = 24
ROW_G1, ROW_CW, ROW_CB, ROW_BR, ROW_BI, ROW_LAM, ROW_LG, ROW_LB, ROW_G2, ROW_G3, ROW_LOSS, ROW_BS = (
    0, 1, 5, 6, 7, 8, 9, 10, 11, 12, 13, 16)
N_DEV = 8


def _pack_small(dcw, dcb, dbr, dbi, dlam, dlg, dlb, dg2, dg3, loss, dbs):
    def body(dcw_ref, dcb_ref, dbr_ref, dbi_ref, dlam_ref, dlg_ref, dlb_ref, dg2_ref, dg3_ref, loss_ref, dbs_ref, out):
        out[...] = jnp.zeros((SMALL_ROWS, D_MODEL), F32)
        for row, ref in ((ROW_CB, dcb_ref), (ROW_BR, dbr_ref), (ROW_BI, dbi_ref), (ROW_LAM, dlam_ref),
                         (ROW_LG, dlg_ref), (ROW_LB, dlb_ref), (ROW_G2, dg2_ref), (ROW_G3, dg3_ref)):
            out[row:row + 1, :] = ref[...]
        out[ROW_CW:ROW_CW + CONV_WIDTH, :] = dcw_ref[0:CONV_WIDTH, :]
        out[ROW_LOSS:ROW_LOSS + 1, 0:128] = loss_ref[0:1, :]
        out[ROW_BS:ROW_BS + GROUPS, 0:128] = jnp.transpose(dbs_ref[...])[0:GROUPS, :]

    vm = pl.BlockSpec(memory_space=pltpu.VMEM)
    return pl.pallas_call(
        body, name="pack_small", in_specs=[vm] * 11, out_specs=vm,
        out_shape=jax.ShapeDtypeStruct((SMALL_ROWS, D_MODEL), F32),
    )(dcw, dcb, dbr, dbi, dlam, dlg, dlb, dg2, dg3, loss, dbs)


def _gather_all_job(blocks):
    n = len(blocks)
    flips = [(dx, dy, dc) for dx in (0, 1) for dy in (0, 1) for dc in (0, 1)][1:]

    def copies(ins, outs, send, recv, local):
        x, y, c, _ = _place()
        me = 4 * x + 2 * y + c
        sends, arrivals, own = [], [], []
        for w in range(n):
            own.append(pltpu.make_async_copy(ins[w], outs[w].at[me], local.at[w]))
            for k, (dx, dy, dc) in enumerate(flips):
                peer = (x ^ dx, y ^ dy, c ^ dc)
                sem = dict(send_sem=send.at[7 * w + k], recv_sem=recv.at[7 * w + k])
                sends.append(pltpu.make_async_remote_copy(
                    src_ref=ins[w], dst_ref=outs[w].at[me], device_id=peer, device_id_type=MESH, **sem))
                arrivals.append(pltpu.make_async_remote_copy(
                    src_ref=ins[w], dst_ref=outs[w].at[4 * peer[0] + 2 * peer[1] + peer[2]], device_id=peer,
                    device_id_type=MESH, **sem))
        return sends, arrivals, own

    return _Job(blocks, [jax.ShapeDtypeStruct((N_DEV,) + b.shape, b.dtype) for b in blocks], 7 * n, copies,
                OTHER_CHIPS + SIBLING + tuple((dx, dy, 1) for dx, dy, _ in OTHER_CHIPS), n_local=n)


def _sum_small(vec_all, ws_all, dg1_all):
    def body(vec_ref, ws_ref, dg1_ref, vec_out, ws_out):
        vec, ws, dg1 = vec_ref[0], ws_ref[0], dg1_ref[0]
        for d in range(1, N_DEV):
            vec, ws, dg1 = vec + vec_ref[d], ws + ws_ref[d], dg1 + dg1_ref[d]
        vec_out[...] = vec
        vec_out[ROW_G1:ROW_G1 + 1, :] = dg1
        ws_out[...] = ws

    vm = pl.BlockSpec(memory_space=pltpu.VMEM)
    return pl.pallas_call(
        body, name="sum_small", in_specs=[vm] * 3, out_specs=[vm, vm],
        out_shape=[jax.ShapeDtypeStruct(vec_all.shape[1:], F32), jax.ShapeDtypeStruct(ws_all.shape[1:], F32)],
    )(vec_all, ws_all, dg1_all)


def _adamw_math(w, g, m, v):
    m = ADAM_B1 * m + (1.0 - ADAM_B1) * g
    v = ADAM_B2 * v + (1.0 - ADAM_B2) * (g * g)
    m_hat = m / (1.0 - ADAM_B1 ** ADAM_STEP)
    v_hat = v / (1.0 - ADAM_B2 ** ADAM_STEP)
    delta = (-ADAM_LR) * (m_hat / (jnp.sqrt(v_hat) + ADAM_EPS) + ADAM_WD * w)
    return delta, m, v


def _adamw(name, gs, ws, ms, vs):
    n = len(ws)

    def body(*refs):
        ins, outs = refs[:4 * n], refs[4 * n:]
        for p in range(n):
            g_ref, w_ref, m_ref, v_ref = ins[p::n]
            g = g_ref[...]
            outs[4 * p][...] = g
            outs[4 * p + 1][...], outs[4 * p + 2][...], outs[4 * p + 3][...] = _adamw_math(
                w_ref[...], g, m_ref[...], v_ref[...])

    blocks = [pl.BlockSpec((w.shape[0] // ROW_STEPS, w.shape[1]), lambda r: (r, 0)) for w in ws]
    out = pl.pallas_call(
        body, name=name, grid=(ROW_STEPS,), in_specs=blocks * 4, out_specs=[b for b in blocks for _ in range(4)],
        out_shape=[jax.ShapeDtypeStruct(w.shape, F32) for w in ws for _ in range(4)], compiler_params=_params(),
    )(*gs, *ws, *ms, *vs)
    return [tuple(out[4 * p:4 * p + 4]) for p in range(n)]


def _adamw_small(grads, ws, ms, vs):
    n = len(grads)

    def body(*refs):
        g_refs, w_refs, m_refs, v_refs = refs[:n], refs[n:2 * n], refs[2 * n:3 * n], refs[3 * n:4 * n]
        outs = refs[4 * n:]
        for p in range(n):
            d, nm, nv = _adamw_math(w_refs[p][...], g_refs[p][...], m_refs[p][...], v_refs[p][...])
            outs[p][...] = d
            outs[n + p][...] = nm
            outs[2 * n + p][...] = nv

    vm = pl.BlockSpec(memory_space=pltpu.VMEM)
    shapes = [jax.ShapeDtypeStruct(w.shape, F32) for w in ws]
    out = pl.pallas_call(
        body, name="adamw_small", in_specs=[vm] * (4 * n), out_specs=[vm] * (3 * n), out_shape=shapes * 3,
    )(*grads, *ws, *ms, *vs)
    return out[:n], out[n:2 * n], out[2 * n:]


def _unstack_heads(w_st):
    per = HEAD_DIM // N_CHIPS
    return w_st.reshape(N_CHIPS, HEADS, per, HEAD_DIM).transpose(1, 0, 2, 3).reshape(HEADS, HEAD_DIM, HEAD_DIM)


def _stack_heads(w):
    per = HEAD_DIM // N_CHIPS
    return w.reshape(HEADS, N_CHIPS, per, HEAD_DIM).transpose(1, 0, 2, 3).reshape(N_CHIPS, HEADS * per, HEAD_DIM)


def kernel(x, norm_mix_g, w_in, conv_w, conv_b, w_rgate, b_rgate, w_igate, b_igate, lru_lambda, w_out_a, sgu_ln_g, sgu_ln_b, sgu_w_s, sgu_b_s, w_out_b, w_out, norm_mlp_g, w_up, w_down, norm_final_g, loss_target, m_norm_mix_g, m_w_in, m_conv_w, m_conv_b, m_w_rgate, m_b_rgate, m_w_igate, m_b_igate, m_lru_lambda, m_w_out_a, m_sgu_ln_g, m_sgu_ln_b, m_sgu_w_s, m_sgu_b_s, m_w_out_b, m_w_out, m_norm_mlp_g, m_w_up, m_w_down, m_norm_final_g, v_norm_mix_g, v_w_in, v_conv_w, v_conv_b, v_w_rgate, v_b_rgate, v_w_igate, v_b_igate, v_lru_lambda, v_w_out_a, v_sgu_ln_g, v_sgu_ln_b, v_sgu_w_s, v_sgu_b_s, v_w_out_b, v_w_out, v_norm_mlp_g, v_w_up, v_w_down, v_norm_final_g):
    chip = _chip_index(lax.axis_index("x"), lax.axis_index("y"))
    core = lax.axis_index("c")
    quarter_h = HEAD_DIM // N_CHIPS
    quarter_d = D_MODEL // N_CHIPS

    as_2d = lambda a: a.reshape(-1, a.shape[-1])
    big_w = [as_2d(w) for w in (w_in, w_rgate, w_igate, w_out_a, w_out_b, w_out, w_up, w_down)]
    big_m = [as_2d(w) for w in (m_w_in, m_w_rgate, m_w_igate, m_w_out_a, m_w_out_b, m_w_out, m_w_up, m_w_down)]
    big_v = [as_2d(w) for w in (v_w_in, v_w_rgate, v_w_igate, v_w_out_a, v_w_out_b, v_w_out, v_w_up, v_w_down)]

    packed = jnp.concatenate([conv_w[0], b_rgate[0], b_igate[0]], axis=1)
    packed = jnp.concatenate([packed, jnp.zeros_like(packed)], axis=0)
    s_in, s_r, s_i, s_oa, s_ob, s_out, s_up, s_down = [w.astype(BF16) for w in big_w]
    xs, target = x[0], loss_target[0]
    g3 = norm_final_g.reshape(1, D_MODEL)
    bias_s = jnp.broadcast_to(jnp.transpose(sgu_b_s[0])[:, :, None], (CHUNK, GROUPS, GROUP_DIM)).reshape(CHUNK, D_MODEL)
    core_arr = core.reshape(1).astype(jnp.int32)
    place = jnp.stack([chip, core]).astype(jnp.int32)
    quarter = lambda g: g.reshape(N_CHIPS, D_MODEL // N_CHIPS, D_MODEL)

    def pair_add(nm, grads, from_sibling):
        halves = [g.reshape(N_CHIPS, 2, g.shape[1] // 2, g.shape[2]) for g in grads]
        return list(_pair_add("pair_add_" + nm, core_arr, halves, from_sibling))

    def chip_sum(nm, pairs, from_chips):
        return list(_chip_sum("chip_sum_" + nm, place, pairs, from_chips))

    order = jnp.stack([chip, chip ^ 2, chip ^ 1, chip ^ 3]).astype(jnp.int32)
    (z, n1, (w_in_st, wr_st, wi_st)), ((packed_all,), late) = _fwd_in(
        xs, norm_mix_g, [s_in, s_r, s_i], order,
        jobs=[_gather_small_job(packed), _gather_near_job([s_oa, s_ob, s_out])])
    pick = lambda lo, hi: packed_all[:, :HEADS, lo:hi].transpose(1, 0, 2).reshape(HEADS, -1)
    conv_w_full = pick(0, quarter_d)
    br_full = pick(quarter_d, quarter_d + quarter_h).reshape(1, D_MODEL)
    bi_full = pick(quarter_d + quarter_h, quarter_d + 2 * quarter_h).reshape(1, D_MODEL)
    wr, wi = _unstack_heads(wr_st), _unstack_heads(wi_st)
    lru = (conv_w_full, conv_b, wr, br_full, wi, bi_full, lru_lambda)
    sgu = (sgu_ln_g, sgu_ln_b, sgu_w_s[0], bias_s)

    after = lambda arrays, result: lax.optimization_barrier((arrays, result))[0]
    w_up_st, w_dn = _sequencer_call(
        "gather_mlp", 8, _gather_near_job(after([s_up, s_down], n1)).then(_gather_far_job).then(_gather_pass_job))
    w_dn = w_dn.reshape(D_FF, D_MODEL)
    late_step = (3 * (xs.shape[0] // SEQ_TILE) // 4,)
    (ya, *saved), (late,) = _fwd_lru(z, *lru, jobs=[_gather_far_job(late).then(_gather_pass_job, at=late_step)])
    w_oa, w_ob, w_o = [w.reshape(D_MODEL, D_MODEL) for w in late]
    (yb, pa, pb, h1, n2), _ = _fwd_sgu_merge(ya, z, xs, *sgu, w_oa, w_ob, w_o, norm_mlp_g)
    (act, dup, dh2b, dh1, loss_part, dg3, dg2), _ = _mlp(n2, h1, target, w_up_st, w_dn, norm_mlp_g, g3)

    d_down, _ = _weight_grad("dw_down", act, dh2b, N_CHIPS, True, False, D_MODEL)
    r_down, = _sequencer_call("send_w_down", 10, _pair_send_job([d_down]))
    d_up, _ = _weight_grad("dw_up", n2, dup, N_CHIPS, False, True, D_MODEL)
    r_up, = _sequencer_call("send_w_up", 11, _pair_send_job([d_up]))
    (p_down,), (p_up,) = pair_add("w_down", [d_down], [r_down]), pair_add("w_up", [d_up], [r_up])
    (dz, merged, dpa, dpb, dh1b, dlg, dlb, dws, dbs, dcw, dcb, dwr, dbr, dwi, dbi, dlam), ((q_up, q_down),) = _bwd_mix(
        dh1, pa, pb, z, *saved, w_oa, w_ob, w_o, *sgu, conv_w_full, wr, wi, lru_lambda,
        jobs=[_chip_exchange_job([p_up, p_down])])
    names = ("w_in", "w_rgate", "w_igate", "w_out_a", "w_out_b", "w_out", "w_up", "w_down")
    (d_out, d_oa, d_ob), _ = _weight_grads_square("dw_projections", [(merged, dh1b), (ya, dpa), (yb, dpb)])
    mids = [quarter(d_oa), quarter(d_ob), quarter(d_out)]
    r_mids = _sequencer_call("send_mids", 1, _pair_send_job(mids))
    half_up, half_down = chip_sum("mlp", [p_up, p_down], after([q_up, q_down], mids))
    gates = [_stack_heads(dwr).astype(BF16), _stack_heads(dwi).astype(BF16)]
    small = _pack_small(dcw, dcb, dbr, dbi, dlam, dlg, dlb, dg2, dg3, loss_part, dbs)
    p_mids = pair_add("projections", after(mids, [half_up, half_down]), r_mids)
    q_mids = _sequencer_call("exchange_mids", 2, _chip_exchange_job(p_mids))
    d_in, (r_gates, (vec_all, ws_all), (full_up, full_down)) = _weight_grad(
        "dw_in", n1, after(dz, p_mids), N_CHIPS, False, True, IN_SHARD,
        jobs=[_pair_send_job(gates), _gather_all_job([small, dws]), _share_job([half_up, half_down])])
    r_in, = _sequencer_call("send_w_in", 3, _pair_send_job(after([d_in], q_mids)))
    adam_args = {nm: (w, m, v) for nm, w, m, v in zip(names, big_w, big_m, big_v)}

    def adamw(group, nms, grads):
        given = [adam_args[nm] for nm in nms]
        grads = [g.reshape(w.shape) for g, (w, _, _) in zip(grads, given)]
        outs = _adamw("adamw_" + group, grads, *[[a[q] for a in given] for q in range(3)])
        return {nm: (out[0], out[1:]) for nm, out in zip(nms, outs)}

    p_gates = pair_add("gates", gates, r_gates)
    half_mids = chip_sum("projections", p_mids, q_mids)
    full_mids = _sequencer_call("share_mids", 12, _share_job(half_mids))
    p_first = pair_add("w_in", after([d_in], half_mids), [r_in]) + p_gates
    q_first = _sequencer_call("exchange_w_in", 4, _chip_exchange_job(p_first))
    (grad_x, dg1), _ = _bwd_in(dz, xs, dh1, w_in_st, norm_mix_g)
    dg1_all, = _sequencer_call("gather_dg1", 6, _gather_all_job([dg1]))
    done = adamw("mlp", ("w_up", "w_down"), [full_up, full_down])
    q_first = after(q_first, [out[0] for _, out in done.values()])
    half_first = chip_sum("first", p_first, q_first)
    full_first = _sequencer_call("share_last", 5, _share_job(half_first))
    done.update(adamw("projections", names[3:6], after(full_mids, half_first)))
    done.update(adamw("first", names[:3], full_first))
    full, big_out = [done[nm][0] for nm in names], [done[nm][1] for nm in names]

    vec, ws_sum = _sum_small(vec_all, ws_all, dg1_all)
    row = lambda r: vec[r:r + 1]
    shard = lambda a, width: lax.dynamic_slice_in_dim(a, chip * width, width, axis=1)
    g_small = dict(
        norm_mix_g=row(ROW_G1), conv_w=shard(vec[ROW_CW:ROW_CW + CONV_WIDTH], quarter_d), conv_b=row(ROW_CB),
        b_rgate=shard(row(ROW_BR).reshape(HEADS, HEAD_DIM), quarter_h),
        b_igate=shard(row(ROW_BI).reshape(HEADS, HEAD_DIM), quarter_h), lru_lambda=row(ROW_LAM),
        sgu_ln_g=row(ROW_LG), sgu_ln_b=row(ROW_LB),
        sgu_w_s=ws_sum.reshape(CHUNK, GROUPS, CHUNK).transpose(1, 0, 2).reshape(GROUPS * CHUNK, CHUNK),
        sgu_b_s=vec[ROW_BS:ROW_BS + GROUPS, 0:CHUNK], norm_mlp_g=row(ROW_G2), norm_final_g=row(ROW_G3))
    loss = vec[ROW_LOSS, 0]
    small_names = list(g_small)
    given = dict(
        norm_mix_g=(norm_mix_g, m_norm_mix_g, v_norm_mix_g), conv_w=(conv_w, m_conv_w, v_conv_w),
        conv_b=(conv_b, m_conv_b, v_conv_b), b_rgate=(b_rgate, m_b_rgate, v_b_rgate),
        b_igate=(b_igate, m_b_igate, v_b_igate), lru_lambda=(lru_lambda, m_lru_lambda, v_lru_lambda),
        sgu_ln_g=(sgu_ln_g, m_sgu_ln_g, v_sgu_ln_g), sgu_ln_b=(sgu_ln_b, m_sgu_ln_b, v_sgu_ln_b),
        sgu_w_s=(sgu_w_s, m_sgu_w_s, v_sgu_w_s), sgu_b_s=(sgu_b_s, m_sgu_b_s, v_sgu_b_s),
        norm_mlp_g=(norm_mlp_g, m_norm_mlp_g, v_norm_mlp_g), norm_final_g=(norm_final_g, m_norm_final_g, v_norm_final_g))
    g2d = [g_small[nm] for nm in small_names]
    to2d = lambda a, g: a.reshape(g.shape)
    d_s, m_s, v_s = _adamw_small(
        g2d, *[[to2d(given[nm][q], g) for nm, g in zip(small_names, g2d)] for q in range(3)])

    shapes = dict(
        norm_mix_g=norm_mix_g, w_in=w_in, conv_w=conv_w, conv_b=conv_b, w_rgate=w_rgate, b_rgate=b_rgate,
        w_igate=w_igate, b_igate=b_igate, lru_lambda=lru_lambda, w_out_a=w_out_a, sgu_ln_g=sgu_ln_g,
        sgu_ln_b=sgu_ln_b, sgu_w_s=sgu_w_s, sgu_b_s=sgu_b_s, w_out_b=w_out_b, w_out=w_out, norm_mlp_g=norm_mlp_g,
        w_up=w_up, w_down=w_down, norm_final_g=norm_final_g)
    grads, deltas, new_m, new_v = {}, {}, {}, {}
    for nm, g, (d, nmom, nvar) in zip(names, full, big_out):
        grads[nm], deltas[nm], new_m[nm], new_v[nm] = g, d, nmom, nvar
    for p, nm in enumerate(small_names):
        grads[nm], deltas[nm], new_m[nm], new_v[nm] = g2d[p], d_s[p], m_s[p], v_s[p]
    order = list(shapes)
    out = [loss, grad_x[None]]
    for group in (grads, deltas, new_m, new_v):
        out += [group[nm].reshape(shapes[nm].shape) for nm in order]
    return tuple(out)
```
